```python
import jax, jax.numpy as jnp
from jax import lax
import numpy as np

D_MODEL = 1024
BATCH = 8
SEQ = 8192
DEPTH = 1

N_Q_HEADS = 8
N_KV_HEADS = 2
HEAD_DIM = 64
ATTN_WIDTH = N_Q_HEADS * HEAD_DIM
KV_WIDTH = N_KV_HEADS * HEAD_DIM
WINDOW = 128
BLOCK = 128
GMLP_GROUPS = 8
GMLP_GROUP_DIM = 64
GMLP_WIDTH = GMLP_GROUPS * GMLP_GROUP_DIM
CHUNK = 128
MIX_WIDTH = ATTN_WIDTH + GMLP_WIDTH
IN_WIDTH = ATTN_WIDTH + 2 * KV_WIDTH + 2 * GMLP_WIDTH
D_FF = 2816
FFN_RES = 0.5
EPS = 1e-6

kernel_name = "hybrid_swa_sink_gmlp_macaron"


def rmsnorm(x, g):
    xf = x.astype(jnp.float32)
    y = xf * lax.rsqrt(jnp.mean(xf * xf, axis=-1, keepdims=True) + EPS) * g.astype(jnp.float32)
    return y.astype(x.dtype)


def layernorm(x, g, b):
    xf = x.astype(jnp.float32)
    mu = jnp.mean(xf, axis=-1, keepdims=True)
    var = jnp.mean(jnp.square(xf - mu), axis=-1, keepdims=True)
    y = (xf - mu) * lax.rsqrt(var + EPS) * g.astype(jnp.float32) + b.astype(jnp.float32)
    return y.astype(x.dtype)


def swiglu(h, w_gate, w_up, w_down):
    return (jax.nn.silu(h @ w_gate) * (h @ w_up)) @ w_down


def sliding_window_sink_attention(q, k, v, sinks):
    b, s = q.shape[0], q.shape[1]
    nb = s // BLOCK
    rep = N_Q_HEADS // N_KV_HEADS
    qb = q.reshape(b, nb, BLOCK, N_KV_HEADS, rep, HEAD_DIM)

    def band(t):
        tb = t.reshape(b, nb, BLOCK, N_KV_HEADS, HEAD_DIM)
        prev = jnp.pad(tb, ((0, 0), (1, 0), (0, 0), (0, 0), (0, 0)))[:, :-1]
        return jnp.concatenate([prev, tb], axis=2)

    kw, vw = band(k), band(v)
    scores = jnp.einsum('bnqgrd,bnkgd->bngrqk', qb, kw,
                        preferred_element_type=jnp.float32) * (HEAD_DIM ** -0.5)
    qpos = jnp.arange(BLOCK)[:, None]
    kpos = jnp.arange(2 * BLOCK)[None, :] - BLOCK
    rel = qpos - kpos
    in_window = (rel >= 0) & (rel < WINDOW)
    real_key = (kpos >= 0)[None] | (jnp.arange(nb) > 0)[:, None, None]
    mask = (in_window[None] & real_key)[None, :, None, None]
    scores = jnp.where(mask, scores, -jnp.inf)
    sink = sinks.astype(jnp.float32).reshape(N_KV_HEADS, rep)[None, None, :, :, None, None]
    m = jnp.maximum(jnp.max(scores, axis=-1, keepdims=True), sink)
    p = jnp.exp(scores - m)
    p = p / (jnp.sum(p, axis=-1, keepdims=True) + jnp.exp(sink - m))
    out = jnp.einsum('bngrqk,bnkgd->bnqgrd', p.astype(v.dtype), vw)
    return out.reshape(b, s, ATTN_WIDTH)


def chunked_spatial_gating(z, ln_g, ln_b, w_s, b_s):
    u, vv = jnp.split(z, 2, axis=-1)
    vv = layernorm(vv, ln_g, ln_b)
    b, s = vv.shape[0], vv.shape[1]
    nc = s // CHUNK
    vc = vv.reshape(b, nc, CHUNK, GMLP_GROUPS, GMLP_GROUP_DIM)
    causal = jnp.tril(jnp.ones((CHUNK, CHUNK), dtype=bool))
    w = jnp.where(causal[None], w_s, 0).astype(vc.dtype)
    mixed = jnp.einsum('gts,bnsge->bntge', w, vc) + b_s.T.astype(vc.dtype)[None, None, :, :, None]
    return u * mixed.reshape(b, s, GMLP_WIDTH)


def _fwd_setup_inputs(seed: int = 0) -> dict:
    key = jax.random.key(seed)
    ks = jax.random.split(key, 24)
    f32 = jnp.float32
    L = DEPTH

    def nrm(k, shape, scale):
        return jax.random.normal(k, shape, f32) * scale

    def gain(k, n):
        return 1.0 + 0.02 * jax.random.normal(k, (L, n), f32)

    return {
        "x": jax.random.normal(ks[0], (BATCH, SEQ, D_MODEL), f32),
        "ffn1_norm_g": gain(ks[1], D_MODEL),
        "ffn1_w_gate": nrm(ks[2], (L, D_MODEL, D_FF), D_MODEL ** -0.5),
        "ffn1_w_up": nrm(ks[3], (L, D_MODEL, D_FF), D_MODEL ** -0.5),
        "ffn1_w_down": nrm(ks[4], (L, D_FF, D_MODEL), D_FF ** -0.5),
        "mix_norm_g": gain(ks[5], D_MODEL),
        "w_in": nrm(ks[6], (L, D_MODEL, IN_WIDTH), D_MODEL ** -0.5),
        "b_in": nrm(ks[7], (L, IN_WIDTH), 0.02),
        "attn_sinks": nrm(ks[8], (L, N_Q_HEADS), 1.0),
        "gmlp_ln_g": gain(ks[9], GMLP_WIDTH),
        "gmlp_ln_b": nrm(ks[10], (L, GMLP_WIDTH), 0.02),
        "gmlp_w_s": nrm(ks[11], (L, GMLP_GROUPS, CHUNK, CHUNK), CHUNK ** -0.5),
        "gmlp_b_s": 1.0 + nrm(ks[12], (L, GMLP_GROUPS, CHUNK), 0.02),
        "attn_out_norm_g": gain(ks[13], ATTN_WIDTH),
        "gmlp_out_norm_g": gain(ks[14], GMLP_WIDTH),
        "w_out": nrm(ks[15], (L, MIX_WIDTH, D_MODEL), MIX_WIDTH ** -0.5),
        "b_out": nrm(ks[16], (L, D_MODEL), 0.02),
        "ffn2_norm_g": gain(ks[17], D_MODEL),
        "ffn2_w_gate": nrm(ks[18], (L, D_MODEL, D_FF), D_MODEL ** -0.5),
        "ffn2_w_up": nrm(ks[19], (L, D_MODEL, D_FF), D_MODEL ** -0.5),
        "ffn2_w_down": nrm(ks[20], (L, D_FF, D_MODEL), D_FF ** -0.5),
        "final_norm_g": 1.0 + 0.02 * jax.random.normal(ks[21], (D_MODEL,), f32),
    }


def _fwd_reference(x, ffn1_norm_g, ffn1_w_gate, ffn1_w_up, ffn1_w_down, mix_norm_g, w_in, b_in,
              attn_sinks, gmlp_ln_g, gmlp_ln_b, gmlp_w_s, gmlp_b_s, attn_out_norm_g,
              gmlp_out_norm_g, w_out, b_out, ffn2_norm_g, ffn2_w_gate, ffn2_w_up, ffn2_w_down,
              final_norm_g):
    b, s, _ = x.shape
    for l in range(DEPTH):
        x = x + FFN_RES * swiglu(rmsnorm(x, ffn1_norm_g[l]), ffn1_w_gate[l], ffn1_w_up[l], ffn1_w_down[l])
        h = rmsnorm(x, mix_norm_g[l])
        proj = h @ w_in[l] + b_in[l]
        q, k, v, zg = jnp.split(proj, [ATTN_WIDTH, ATTN_WIDTH + KV_WIDTH,
                                       ATTN_WIDTH + 2 * KV_WIDTH], axis=-1)
        q = q.reshape(b, s, N_Q_HEADS, HEAD_DIM)
        k = k.reshape(b, s, N_KV_HEADS, HEAD_DIM)
        v = v.reshape(b, s, N_KV_HEADS, HEAD_DIM)
        y_attn = sliding_window_sink_attention(q, k, v, attn_sinks[l])
        y_gmlp = chunked_spatial_gating(jax.nn.gelu(zg), gmlp_ln_g[l], gmlp_ln_b[l],
                                        gmlp_w_s[l], gmlp_b_s[l])
        y = jnp.concatenate([rmsnorm(y_attn, attn_out_norm_g[l]),
                             rmsnorm(y_gmlp, gmlp_out_norm_g[l])], axis=-1)
        x = x + (y @ w_out[l] + b_out[l])
        x = x + FFN_RES * swiglu(rmsnorm(x, ffn2_norm_g[l]), ffn2_w_gate[l], ffn2_w_up[l], ffn2_w_down[l])
    return rmsnorm(x, final_norm_g)


import jax as _jax
import jax.numpy as _jnp

TWIN_FORMAT = 'train_step'
FWD_PARAMS = ['x', 'ffn1_norm_g', 'ffn1_w_gate', 'ffn1_w_up', 'ffn1_w_down', 'mix_norm_g', 'w_in', 'b_in', 'attn_sinks', 'gmlp_ln_g', 'gmlp_ln_b', 'gmlp_w_s', 'gmlp_b_s', 'attn_out_norm_g', 'gmlp_out_norm_g', 'w_out', 'b_out', 'ffn2_norm_g', 'ffn2_w_gate', 'ffn2_w_up', 'ffn2_w_down', 'final_norm_g']
TWIN_WEIGHTS = ['ffn1_norm_g', 'ffn1_w_gate', 'ffn1_w_up', 'ffn1_w_down', 'mix_norm_g', 'w_in', 'b_in', 'attn_sinks', 'gmlp_ln_g', 'gmlp_ln_b', 'gmlp_w_s', 'gmlp_b_s', 'attn_out_norm_g', 'gmlp_out_norm_g', 'w_out', 'b_out', 'ffn2_norm_g', 'ffn2_w_gate', 'ffn2_w_up', 'ffn2_w_down', 'final_norm_g']
TWIN_DIFF_INPUT = 'x'
TWIN_INPUTS = ['x', 'ffn1_norm_g', 'ffn1_w_gate', 'ffn1_w_up', 'ffn1_w_down', 'mix_norm_g', 'w_in', 'b_in', 'attn_sinks', 'gmlp_ln_g', 'gmlp_ln_b', 'gmlp_w_s', 'gmlp_b_s', 'attn_out_norm_g', 'gmlp_out_norm_g', 'w_out', 'b_out', 'ffn2_norm_g', 'ffn2_w_gate', 'ffn2_w_up', 'ffn2_w_down', 'final_norm_g', 'loss_target', 'm_ffn1_norm_g', 'm_ffn1_w_gate', 'm_ffn1_w_up', 'm_ffn1_w_down', 'm_mix_norm_g', 'm_w_in', 'm_b_in', 'm_attn_sinks', 'm_gmlp_ln_g', 'm_gmlp_ln_b', 'm_gmlp_w_s', 'm_gmlp_b_s', 'm_attn_out_norm_g', 'm_gmlp_out_norm_g', 'm_w_out', 'm_b_out', 'm_ffn2_norm_g', 'm_ffn2_w_gate', 'm_ffn2_w_up', 'm_ffn2_w_down', 'm_final_norm_g', 'v_ffn1_norm_g', 'v_ffn1_w_gate', 'v_ffn1_w_up', 'v_ffn1_w_down', 'v_mix_norm_g', 'v_w_in', 'v_b_in', 'v_attn_sinks', 'v_gmlp_ln_g', 'v_gmlp_ln_b', 'v_gmlp_w_s', 'v_gmlp_b_s', 'v_attn_out_norm_g', 'v_gmlp_out_norm_g', 'v_w_out', 'v_b_out', 'v_ffn2_norm_g', 'v_ffn2_w_gate', 'v_ffn2_w_up', 'v_ffn2_w_down', 'v_final_norm_g']
TWIN_OUTPUTS = ['loss', 'grad_x', 'grad_ffn1_norm_g', 'grad_ffn1_w_gate', 'grad_ffn1_w_up', 'grad_ffn1_w_down', 'grad_mix_norm_g', 'grad_w_in', 'grad_b_in', 'grad_attn_sinks', 'grad_gmlp_ln_g', 'grad_gmlp_ln_b', 'grad_gmlp_w_s', 'grad_gmlp_b_s', 'grad_attn_out_norm_g', 'grad_gmlp_out_norm_g', 'grad_w_out', 'grad_b_out', 'grad_ffn2_norm_g', 'grad_ffn2_w_gate', 'grad_ffn2_w_up', 'grad_ffn2_w_down', 'grad_final_norm_g', 'delta_ffn1_norm_g', 'delta_ffn1_w_gate', 'delta_ffn1_w_up', 'delta_ffn1_w_down', 'delta_mix_norm_g', 'delta_w_in', 'delta_b_in', 'delta_attn_sinks', 'delta_gmlp_ln_g', 'delta_gmlp_ln_b', 'delta_gmlp_w_s', 'delta_gmlp_b_s', 'delta_attn_out_norm_g', 'delta_gmlp_out_norm_g', 'delta_w_out', 'delta_b_out', 'delta_ffn2_norm_g', 'delta_ffn2_w_gate', 'delta_ffn2_w_up', 'delta_ffn2_w_down', 'delta_final_norm_g', 'new_m_ffn1_norm_g', 'new_m_ffn1_w_gate', 'new_m_ffn1_w_up', 'new_m_ffn1_w_down', 'new_m_mix_norm_g', 'new_m_w_in', 'new_m_b_in', 'new_m_attn_sinks', 'new_m_gmlp_ln_g', 'new_m_gmlp_ln_b', 'new_m_gmlp_w_s', 'new_m_gmlp_b_s', 'new_m_attn_out_norm_g', 'new_m_gmlp_out_norm_g', 'new_m_w_out', 'new_m_b_out', 'new_m_ffn2_norm_g', 'new_m_ffn2_w_gate', 'new_m_ffn2_w_up', 'new_m_ffn2_w_down', 'new_m_final_norm_g', 'new_v_ffn1_norm_g', 'new_v_ffn1_w_gate', 'new_v_ffn1_w_up', 'new_v_ffn1_w_down', 'new_v_mix_norm_g', 'new_v_w_in', 'new_v_b_in', 'new_v_attn_sinks', 'new_v_gmlp_ln_g', 'new_v_gmlp_ln_b', 'new_v_gmlp_w_s', 'new_v_gmlp_b_s', 'new_v_attn_out_norm_g', 'new_v_gmlp_out_norm_g', 'new_v_w_out', 'new_v_b_out', 'new_v_ffn2_norm_g', 'new_v_ffn2_w_gate', 'new_v_ffn2_w_up', 'new_v_ffn2_w_down', 'new_v_final_norm_g']
TWIN_LEAF_KINDS = {'loss': 'loss', 'grad_x': 'grad_x', 'grad_ffn1_norm_g': 'grad_w', 'grad_ffn1_w_gate': 'grad_w', 'grad_ffn1_w_up': 'grad_w', 'grad_ffn1_w_down': 'grad_w', 'grad_mix_norm_g': 'grad_w', 'grad_w_in': 'grad_w', 'grad_b_in': 'grad_w', 'grad_attn_sinks': 'grad_w', 'grad_gmlp_ln_g': 'grad_w', 'grad_gmlp_ln_b': 'grad_w', 'grad_gmlp_w_s': 'grad_w', 'grad_gmlp_b_s': 'grad_w', 'grad_attn_out_norm_g': 'grad_w', 'grad_gmlp_out_norm_g': 'grad_w', 'grad_w_out': 'grad_w', 'grad_b_out': 'grad_w', 'grad_ffn2_norm_g': 'grad_w', 'grad_ffn2_w_gate': 'grad_w', 'grad_ffn2_w_up': 'grad_w', 'grad_ffn2_w_down': 'grad_w', 'grad_final_norm_g': 'grad_w', 'delta_ffn1_norm_g': 'delta_w', 'delta_ffn1_w_gate': 'delta_w', 'delta_ffn1_w_up': 'delta_w', 'delta_ffn1_w_down': 'delta_w', 'delta_mix_norm_g': 'delta_w', 'delta_w_in': 'delta_w', 'delta_b_in': 'delta_w', 'delta_attn_sinks': 'delta_w', 'delta_gmlp_ln_g': 'delta_w', 'delta_gmlp_ln_b': 'delta_w', 'delta_gmlp_w_s': 'delta_w', 'delta_gmlp_b_s': 'delta_w', 'delta_attn_out_norm_g': 'delta_w', 'delta_gmlp_out_norm_g': 'delta_w', 'delta_w_out': 'delta_w', 'delta_b_out': 'delta_w', 'delta_ffn2_norm_g': 'delta_w', 'delta_ffn2_w_gate': 'delta_w', 'delta_ffn2_w_up': 'delta_w', 'delta_ffn2_w_down': 'delta_w', 'delta_final_norm_g': 'delta_w', 'new_m_ffn1_norm_g': 'new_m', 'new_m_ffn1_w_gate': 'new_m', 'new_m_ffn1_w_up': 'new_m', 'new_m_ffn1_w_down': 'new_m', 'new_m_mix_norm_g': 'new_m', 'new_m_w_in': 'new_m', 'new_m_b_in': 'new_m', 'new_m_attn_sinks': 'new_m', 'new_m_gmlp_ln_g': 'new_m', 'new_m_gmlp_ln_b': 'new_m', 'new_m_gmlp_w_s': 'new_m', 'new_m_gmlp_b_s': 'new_m', 'new_m_attn_out_norm_g': 'new_m', 'new_m_gmlp_out_norm_g': 'new_m', 'new_m_w_out': 'new_m', 'new_m_b_out': 'new_m', 'new_m_ffn2_norm_g': 'new_m', 'new_m_ffn2_w_gate': 'new_m', 'new_m_ffn2_w_up': 'new_m', 'new_m_ffn2_w_down': 'new_m', 'new_m_final_norm_g': 'new_m', 'new_v_ffn1_norm_g': 'new_v', 'new_v_ffn1_w_gate': 'new_v', 'new_v_ffn1_w_up': 'new_v', 'new_v_ffn1_w_down': 'new_v', 'new_v_mix_norm_g': 'new_v', 'new_v_w_in': 'new_v', 'new_v_b_in': 'new_v', 'new_v_attn_sinks': 'new_v', 'new_v_gmlp_ln_g': 'new_v', 'new_v_gmlp_ln_b': 'new_v', 'new_v_gmlp_w_s': 'new_v', 'new_v_gmlp_b_s': 'new_v', 'new_v_attn_out_norm_g': 'new_v', 'new_v_gmlp_out_norm_g': 'new_v', 'new_v_w_out': 'new_v', 'new_v_b_out': 'new_v', 'new_v_ffn2_norm_g': 'new_v', 'new_v_ffn2_w_gate': 'new_v', 'new_v_ffn2_w_up': 'new_v', 'new_v_ffn2_w_down': 'new_v', 'new_v_final_norm_g': 'new_v'}


def _forward(args):
    return _fwd_reference(*[args[k] for k in FWD_PARAMS])


def _output_shape():
    def fwd():
        inp = _fwd_setup_inputs(0)
        return _fwd_reference(*[inp[k] for k in FWD_PARAMS])
    out = _jax.eval_shape(fwd)
    return out.shape, out.dtype

N_MICROBATCH = 1
ADAM_LR = 0.001
ADAM_B1 = 0.9
ADAM_B2 = 0.999
ADAM_EPS = 1e-08
ADAM_WD = 0.01
ADAM_STEP = 10
PER_EXAMPLE_BATCH_AXIS = {'x': 0, 'loss_target': 0}
SHARED_INPUTS = []
_WEIGHT_DTYPES = {'ffn1_norm_g': _jnp.float32, 'ffn1_w_gate': _jnp.float32, 'ffn1_w_up': _jnp.float32, 'ffn1_w_down': _jnp.float32, 'mix_norm_g': _jnp.float32, 'w_in': _jnp.float32, 'b_in': _jnp.float32, 'attn_sinks': _jnp.float32, 'gmlp_ln_g': _jnp.float32, 'gmlp_ln_b': _jnp.float32, 'gmlp_w_s': _jnp.float32, 'gmlp_b_s': _jnp.float32, 'attn_out_norm_g': _jnp.float32, 'gmlp_out_norm_g': _jnp.float32, 'w_out': _jnp.float32, 'b_out': _jnp.float32, 'ffn2_norm_g': _jnp.float32, 'ffn2_w_gate': _jnp.float32, 'ffn2_w_up': _jnp.float32, 'ffn2_w_down': _jnp.float32, 'final_norm_g': _jnp.float32}
MOMENT_SCALE = {'ffn1_norm_g': 1.309756e-01, 'ffn1_w_gate': 5.679686e-02, 'ffn1_w_up': 5.513385e-02, 'ffn1_w_down': 9.133940e-02, 'mix_norm_g': 2.710758e-01, 'w_in': 1.979554e-01, 'b_in': 1.132481e+00, 'attn_sinks': 3.854464e-02, 'gmlp_ln_g': 1.006293e-01, 'gmlp_ln_b': 1.004053e-01, 'gmlp_w_s': 6.967430e-02, 'gmlp_b_s': 1.046232e-01, 'attn_out_norm_g': 2.038664e-01, 'gmlp_out_norm_g': 2.036813e-01, 'w_out': 1.860077e-01, 'b_out': 2.768564e-01, 'ffn2_norm_g': 8.506473e-02, 'ffn2_w_gate': 3.172214e-02, 'ffn2_w_up': 3.075688e-02, 'ffn2_w_down': 5.125875e-02, 'final_norm_g': 6.414976e+01}


def _to_microbatches(a, axis):
    t = _jnp.moveaxis(a, axis, 0)
    t = t.reshape((N_MICROBATCH, t.shape[0] // N_MICROBATCH) + t.shape[1:])
    return _jnp.moveaxis(t, 1, axis + 1)


def setup_inputs(seed: int = 0) -> dict:
    inp = _fwd_setup_inputs(seed)
    key = _jax.random.fold_in(_jax.random.key(seed), 7919)
    shape, _ = _output_shape()
    out = dict(inp)
    out["loss_target"] = _jax.random.normal(_jax.random.fold_in(key, 0), shape, _jnp.float32)
    for i, name in enumerate(TWIN_WEIGHTS):
        w = inp[name].astype(_jnp.float32)
        if MOMENT_SCALE is None:
            s = _jnp.sqrt(_jnp.mean(_jnp.square(w)) + 1e-30)
        else:
            s = MOMENT_SCALE[name]
        km, kv = _jax.random.split(_jax.random.fold_in(key, i + 1))
        out[name] = w
        out["m_" + name] = s * _jax.random.normal(km, w.shape, _jnp.float32)
        out["v_" + name] = (s * s) * _jax.random.uniform(kv, w.shape, _jnp.float32, 0.5, 1.5)
    if N_MICROBATCH > 1:
        for name, axis in PER_EXAMPLE_BATCH_AXIS.items():
            out[name] = _to_microbatches(out[name], axis)
    return {'x': out['x'], 'ffn1_norm_g': out['ffn1_norm_g'], 'ffn1_w_gate': out['ffn1_w_gate'], 'ffn1_w_up': out['ffn1_w_up'], 'ffn1_w_down': out['ffn1_w_down'], 'mix_norm_g': out['mix_norm_g'], 'w_in': out['w_in'], 'b_in': out['b_in'], 'attn_sinks': out['attn_sinks'], 'gmlp_ln_g': out['gmlp_ln_g'], 'gmlp_ln_b': out['gmlp_ln_b'], 'gmlp_w_s': out['gmlp_w_s'], 'gmlp_b_s': out['gmlp_b_s'], 'attn_out_norm_g': out['attn_out_norm_g'], 'gmlp_out_norm_g': out['gmlp_out_norm_g'], 'w_out': out['w_out'], 'b_out': out['b_out'], 'ffn2_norm_g': out['ffn2_norm_g'], 'ffn2_w_gate': out['ffn2_w_gate'], 'ffn2_w_up': out['ffn2_w_up'], 'ffn2_w_down': out['ffn2_w_down'], 'final_norm_g': out['final_norm_g'], 'loss_target': out['loss_target'], 'm_ffn1_norm_g': out['m_ffn1_norm_g'], 'm_ffn1_w_gate': out['m_ffn1_w_gate'], 'm_ffn1_w_up': out['m_ffn1_w_up'], 'm_ffn1_w_down': out['m_ffn1_w_down'], 'm_mix_norm_g': out['m_mix_norm_g'], 'm_w_in': out['m_w_in'], 'm_b_in': out['m_b_in'], 'm_attn_sinks': out['m_attn_sinks'], 'm_gmlp_ln_g': out['m_gmlp_ln_g'], 'm_gmlp_ln_b': out['m_gmlp_ln_b'], 'm_gmlp_w_s': out['m_gmlp_w_s'], 'm_gmlp_b_s': out['m_gmlp_b_s'], 'm_attn_out_norm_g': out['m_attn_out_norm_g'], 'm_gmlp_out_norm_g': out['m_gmlp_out_norm_g'], 'm_w_out': out['m_w_out'], 'm_b_out': out['m_b_out'], 'm_ffn2_norm_g': out['m_ffn2_norm_g'], 'm_ffn2_w_gate': out['m_ffn2_w_gate'], 'm_ffn2_w_up': out['m_ffn2_w_up'], 'm_ffn2_w_down': out['m_ffn2_w_down'], 'm_final_norm_g': out['m_final_norm_g'], 'v_ffn1_norm_g': out['v_ffn1_norm_g'], 'v_ffn1_w_gate': out['v_ffn1_w_gate'], 'v_ffn1_w_up': out['v_ffn1_w_up'], 'v_ffn1_w_down': out['v_ffn1_w_down'], 'v_mix_norm_g': out['v_mix_norm_g'], 'v_w_in': out['v_w_in'], 'v_b_in': out['v_b_in'], 'v_attn_sinks': out['v_attn_sinks'], 'v_gmlp_ln_g': out['v_gmlp_ln_g'], 'v_gmlp_ln_b': out['v_gmlp_ln_b'], 'v_gmlp_w_s': out['v_gmlp_w_s'], 'v_gmlp_b_s': out['v_gmlp_b_s'], 'v_attn_out_norm_g': out['v_attn_out_norm_g'], 'v_gmlp_out_norm_g': out['v_gmlp_out_norm_g'], 'v_w_out': out['v_w_out'], 'v_b_out': out['v_b_out'], 'v_ffn2_norm_g': out['v_ffn2_norm_g'], 'v_ffn2_w_gate': out['v_ffn2_w_gate'], 'v_ffn2_w_up': out['v_ffn2_w_up'], 'v_ffn2_w_down': out['v_ffn2_w_down'], 'v_final_norm_g': out['v_final_norm_g']}


def _loss(weights, diff, rest, loss_target):
    with _jax.named_scope("forward"):
        args = {**rest, TWIN_DIFF_INPUT: diff, **{k: w.astype(_WEIGHT_DTYPES[k]) for k, w in weights.items()}}
        y = _forward(args)
    with _jax.named_scope("loss_head"):
        err = _jnp.square(y.astype(_jnp.float32) - loss_target)
        return 0.5 * _jnp.sum(_jnp.mean(err, axis=-1)) if err.ndim else 0.5 * err


def _adamw(w, g, m, v):
    m = ADAM_B1 * m + (1.0 - ADAM_B1) * g
    v = ADAM_B2 * v + (1.0 - ADAM_B2) * _jnp.square(g)
    m_hat = m / (1.0 - ADAM_B1 ** ADAM_STEP)
    v_hat = v / (1.0 - ADAM_B2 ** ADAM_STEP)
    delta = -ADAM_LR * (m_hat / (_jnp.sqrt(v_hat) + ADAM_EPS) + ADAM_WD * w)
    return delta, m, v


def reference(x, ffn1_norm_g, ffn1_w_gate, ffn1_w_up, ffn1_w_down, mix_norm_g, w_in, b_in, attn_sinks, gmlp_ln_g, gmlp_ln_b, gmlp_w_s, gmlp_b_s, attn_out_norm_g, gmlp_out_norm_g, w_out, b_out, ffn2_norm_g, ffn2_w_gate, ffn2_w_up, ffn2_w_down, final_norm_g, loss_target, m_ffn1_norm_g, m_ffn1_w_gate, m_ffn1_w_up, m_ffn1_w_down, m_mix_norm_g, m_w_in, m_b_in, m_attn_sinks, m_gmlp_ln_g, m_gmlp_ln_b, m_gmlp_w_s, m_gmlp_b_s, m_attn_out_norm_g, m_gmlp_out_norm_g, m_w_out, m_b_out, m_ffn2_norm_g, m_ffn2_w_gate, m_ffn2_w_up, m_ffn2_w_down, m_final_norm_g, v_ffn1_norm_g, v_ffn1_w_gate, v_ffn1_w_up, v_ffn1_w_down, v_mix_norm_g, v_w_in, v_b_in, v_attn_sinks, v_gmlp_ln_g, v_gmlp_ln_b, v_gmlp_w_s, v_gmlp_b_s, v_attn_out_norm_g, v_gmlp_out_norm_g, v_w_out, v_b_out, v_ffn2_norm_g, v_ffn2_w_gate, v_ffn2_w_up, v_ffn2_w_down, v_final_norm_g):
    given = dict(x=x, ffn1_norm_g=ffn1_norm_g, ffn1_w_gate=ffn1_w_gate, ffn1_w_up=ffn1_w_up, ffn1_w_down=ffn1_w_down, mix_norm_g=mix_norm_g, w_in=w_in, b_in=b_in, attn_sinks=attn_sinks, gmlp_ln_g=gmlp_ln_g, gmlp_ln_b=gmlp_ln_b, gmlp_w_s=gmlp_w_s, gmlp_b_s=gmlp_b_s, attn_out_norm_g=attn_out_norm_g, gmlp_out_norm_g=gmlp_out_norm_g, w_out=w_out, b_out=b_out, ffn2_norm_g=ffn2_norm_g, ffn2_w_gate=ffn2_w_gate, ffn2_w_up=ffn2_w_up, ffn2_w_down=ffn2_w_down, final_norm_g=final_norm_g, loss_target=loss_target, m_ffn1_norm_g=m_ffn1_norm_g, m_ffn1_w_gate=m_ffn1_w_gate, m_ffn1_w_up=m_ffn1_w_up, m_ffn1_w_down=m_ffn1_w_down, m_mix_norm_g=m_mix_norm_g, m_w_in=m_w_in, m_b_in=m_b_in, m_attn_sinks=m_attn_sinks, m_gmlp_ln_g=m_gmlp_ln_g, m_gmlp_ln_b=m_gmlp_ln_b, m_gmlp_w_s=m_gmlp_w_s, m_gmlp_b_s=m_gmlp_b_s, m_attn_out_norm_g=m_attn_out_norm_g, m_gmlp_out_norm_g=m_gmlp_out_norm_g, m_w_out=m_w_out, m_b_out=m_b_out, m_ffn2_norm_g=m_ffn2_norm_g, m_ffn2_w_gate=m_ffn2_w_gate, m_ffn2_w_up=m_ffn2_w_up, m_ffn2_w_down=m_ffn2_w_down, m_final_norm_g=m_final_norm_g, v_ffn1_norm_g=v_ffn1_norm_g, v_ffn1_w_gate=v_ffn1_w_gate, v_ffn1_w_up=v_ffn1_w_up, v_ffn1_w_down=v_ffn1_w_down, v_mix_norm_g=v_mix_norm_g, v_w_in=v_w_in, v_b_in=v_b_in, v_attn_sinks=v_attn_sinks, v_gmlp_ln_g=v_gmlp_ln_g, v_gmlp_ln_b=v_gmlp_ln_b, v_gmlp_w_s=v_gmlp_w_s, v_gmlp_b_s=v_gmlp_b_s, v_attn_out_norm_g=v_attn_out_norm_g, v_gmlp_out_norm_g=v_gmlp_out_norm_g, v_w_out=v_w_out, v_b_out=v_b_out, v_ffn2_norm_g=v_ffn2_norm_g, v_ffn2_w_gate=v_ffn2_w_gate, v_ffn2_w_up=v_ffn2_w_up, v_ffn2_w_down=v_ffn2_w_down, v_final_norm_g=v_final_norm_g)
    weights = {n: given[n] for n in TWIN_WEIGHTS}
    shared = {n: given[n] for n in SHARED_INPUTS}
    per_example = {n: given[n] for n in ['x']}
    grad_fn = _jax.value_and_grad(_loss, argnums=(0, 1))

    def one_microbatch(ex, loss_target):
        ex = dict(ex)
        diff = ex.pop(TWIN_DIFF_INPUT)
        return grad_fn(weights, diff, {**shared, **ex}, loss_target)

    if N_MICROBATCH == 1:
        loss, (grad_w, grad_x) = one_microbatch(per_example, given["loss_target"])
    else:
        def body(carry, xs):
            loss_sum, grad_sum = carry
            l_k, (gw_k, gx_k) = one_microbatch(xs[0], xs[1])
            with _jax.named_scope("update"):
                return (loss_sum + l_k, _jax.tree.map(_jnp.add, grad_sum, gw_k)), gx_k

        init = (_jnp.zeros((), _jnp.float32), _jax.tree.map(_jnp.zeros_like, weights))
        (loss, grad_w), grad_x = _jax.lax.scan(body, init, (per_example, given["loss_target"]))
    with _jax.named_scope("update"):
        delta_w, new_m, new_v = {}, {}, {}
        for n in TWIN_WEIGHTS:
            delta_w[n], new_m[n], new_v[n] = _adamw(weights[n], grad_w[n], given["m_" + n], given["v_" + n])
    return (loss, grad_x, *[grad_w[n] for n in TWIN_WEIGHTS], *[delta_w[n] for n in TWIN_WEIGHTS],
            *[new_m[n] for n in TWIN_WEIGHTS], *[new_v[n] for n in TWIN_WEIGHTS])
```

```python
import functools

import jax
import jax.numpy as jnp
from jax import lax
from jax.experimental import pallas as pl
from jax.experimental.pallas import tpu as pltpu

F32 = jnp.float32
BF16 = jnp.bfloat16

N_DEV = 8
N_Q_HEADS = 8
N_KV_HEADS = 2
HEAD_DIM = 64
ATTN_WIDTH = N_Q_HEADS * HEAD_DIM
KV_WIDTH = N_KV_HEADS * HEAD_DIM
QKV_WIDTH = ATTN_WIDTH + 2 * KV_WIDTH
BLOCK = 128
GMLP_GROUPS = 8
GMLP_GROUP_DIM = 64
GMLP_WIDTH = GMLP_GROUPS * GMLP_GROUP_DIM
EPS = 1e-6
FFN_RES = 0.5
ATTN_SCALE = HEAD_DIM ** -0.5
MASK_VALUE = -1e30

ADAM_LR = 0.001
ADAM_B1 = 0.9
ADAM_B2 = 0.999
ADAM_EPS = 1e-08
ADAM_WD = 0.01
ADAM_STEP = 10

V7X_VMEM_BYTES = 64 * 1024 * 1024
VMEM_LIMIT = 56 * 1024 * 1024
LANES = 128

NT = (((1,), (1,)), ((), ()))
TN = (((0,), (0,)), ((), ()))


def _dot(a, b):
    return jnp.dot(a, b, preferred_element_type=F32)


def _dg(a, b, dims):
    return lax.dot_general(a, b, dims, preferred_element_type=F32)


def _params(semantics=None):
    return pltpu.CompilerParams(dimension_semantics=semantics, vmem_limit_bytes=VMEM_LIMIT)


def _resident(shape):
    zeros = (0,) * len(shape)
    return pl.BlockSpec(shape, lambda *_: zeros, pipeline_mode=pl.Buffered(1))


def _token_tile(t):
    return min(t, 512)


def _f_chunk(f):
    for c in (256, 128):
        if f % c == 0:
            return c
    return f


def ffn_fwd(x, gain, wg_t, wu_t, wd, name):
    t, d = x.shape
    f = wd.shape[0]
    tm = _token_tile(t)
    fc = _f_chunk(f)

    def body(x_ref, g_ref, wg_ref, wu_ref, wd_ref, xo_ref, a_ref, b_ref):
        xv = x_ref[...]
        r = lax.rsqrt(jnp.mean(xv * xv, axis=-1, keepdims=True) + EPS)
        h = (xv * r * g_ref[...]).astype(BF16)
        acc = jnp.zeros((tm, d), F32)
        for c in range(f // fc):
            sl = slice(c * fc, (c + 1) * fc)
            a = _dg(h, wg_ref[sl, :], NT)
            b = _dg(h, wu_ref[sl, :], NT)
            a_ref[:, sl] = a.astype(BF16)
            b_ref[:, sl] = b.astype(BF16)
            s = (a * jax.nn.sigmoid(a) * b).astype(BF16)
            acc = acc + _dot(s, wd_ref[sl, :])
        xo_ref[...] = xv + FFN_RES * acc

    return pl.pallas_call(
        body, name=name, grid=(t // tm,),
        in_specs=[pl.BlockSpec((tm, d), lambda i: (i, 0)), _resident((1, d)),
                  _resident((f, d)), _resident((f, d)), _resident((f, d))],
        out_specs=[pl.BlockSpec((tm, d), lambda i: (i, 0)), pl.BlockSpec((tm, f), lambda i: (i, 0)),
                   pl.BlockSpec((tm, f), lambda i: (i, 0))],
        out_shape=[jax.ShapeDtypeStruct((t, d), F32), jax.ShapeDtypeStruct((t, f), BF16),
                   jax.ShapeDtypeStruct((t, f), BF16)],
        compiler_params=_params(("arbitrary",)),
    )(x, gain, wg_t, wu_t, wd)


def _rmsnorm_bwd_rows(xv, r, gain, dh):
    n = xv * r
    dgain = jnp.sum(dh * n, axis=0, keepdims=True)
    dn = dh * gain
    dx = r * (dn - n * jnp.mean(dn * n, axis=-1, keepdims=True))
    return dx, dgain


def ffn_bwd(x, dxo, gain, a, b, wg_t, wu_t, wd, name):
    t, d = x.shape
    f = wd.shape[0]
    tm = min(t, 256)
    fc = _f_chunk(f)

    def body(x_ref, dxo_ref, g_ref, a_ref, b_ref, wg_ref, wu_ref, wd_ref,
             dx_ref, da_ref, db_ref, s_ref, h_ref, dg_ref):
        i = pl.program_id(0)
        xv = x_ref[...]
        dxo = dxo_ref[...]
        gain_v = g_ref[...]
        r = lax.rsqrt(jnp.mean(xv * xv, axis=-1, keepdims=True) + EPS)
        h_ref[...] = (xv * r * gain_v).astype(BF16)
        df = (FFN_RES * dxo).astype(BF16)
        dh = jnp.zeros((tm, d), F32)
        for c in range(f // fc):
            sl = slice(c * fc, (c + 1) * fc)
            av = a_ref[:, sl].astype(F32)
            bv = b_ref[:, sl].astype(F32)
            ds = _dg(df, wd_ref[sl, :], NT)
            sig = jax.nn.sigmoid(av)
            sil = av * sig
            s_ref[:, sl] = (sil * bv).astype(BF16)
            da = (ds * bv * (sig * (1.0 + av * (1.0 - sig)))).astype(BF16)
            db = (ds * sil).astype(BF16)
            da_ref[:, sl] = da
            db_ref[:, sl] = db
            dh = dh + _dot(da, wg_ref[sl, :]) + _dot(db, wu_ref[sl, :])
        dx, dgain = _rmsnorm_bwd_rows(xv, r, gain_v, dh)
        dx_ref[...] = dxo + dx

        @pl.when(i == 0)
        def _():
            dg_ref[...] = jnp.zeros_like(dg_ref)

        dg_ref[...] += dgain

    tok = lambda w: pl.BlockSpec((tm, w), lambda i: (i, 0))
    return pl.pallas_call(
        body, name=name, grid=(t // tm,),
        in_specs=[tok(d), tok(d), _resident((1, d)), tok(f), tok(f),
                  _resident((f, d)), _resident((f, d)), _resident((f, d))],
        out_specs=[tok(d), tok(f), tok(f), tok(f), tok(d), pl.BlockSpec((1, d), lambda i: (0, 0))],
        out_shape=[jax.ShapeDtypeStruct((t, d), F32), jax.ShapeDtypeStruct((t, f), BF16),
                   jax.ShapeDtypeStruct((t, f), BF16), jax.ShapeDtypeStruct((t, f), BF16),
                   jax.ShapeDtypeStruct((t, d), BF16), jax.ShapeDtypeStruct((1, d), F32)],
        compiler_params=_params(("arbitrary",)),
    )(x, dxo, gain, a, b, wg_t, wu_t, wd)


def _row_tile(m, cap):
    if m <= cap:
        return m
    best = None
    for bm in range(LANES, cap + 1, LANES):
        if m % bm == 0:
            best = bm
    return best if best is not None else m


def grad_tn(a, b, scale, name):
    t, m = a.shape
    n = b.shape[1]
    bm = _row_tile(m, 1408)
    bk = min(t, 1024)
    nk = t // bk

    def body(a_ref, b_ref, o_ref, acc_ref):
        k = pl.program_id(1)

        @pl.when(k == 0)
        def _():
            acc_ref[...] = jnp.zeros_like(acc_ref)

        acc_ref[...] += _dg(a_ref[...].astype(BF16), b_ref[...].astype(BF16), TN)

        @pl.when(k == nk - 1)
        def _():
            o_ref[...] = (scale * acc_ref[...]).astype(BF16)

    return pl.pallas_call(
        body, name=name, grid=(m // bm, nk),
        in_specs=[pl.BlockSpec((bk, bm), lambda i, k: (k, i)), pl.BlockSpec((bk, n), lambda i, k: (k, 0))],
        out_specs=pl.BlockSpec((bm, n), lambda i, k: (i, 0)),
        out_shape=jax.ShapeDtypeStruct((m, n), BF16),
        scratch_shapes=[pltpu.VMEM((bm, n), F32)],
        compiler_params=_params(("arbitrary", "arbitrary")),
    )(a, b)


def grad_nn(a_list, b, name):
    t, n = b.shape
    ms = [a.shape[0] for a in a_list]
    m = sum(ms)
    bk = min(t, 1024)
    nk = t // bk
    na = len(a_list)

    def body(*refs):
        a_refs, b_ref, o_ref, acc_ref = refs[:na], refs[na], refs[na + 1], refs[na + 2]
        k = pl.program_id(0)

        @pl.when(k == 0)
        def _():
            acc_ref[...] = jnp.zeros_like(acc_ref)

        bv = b_ref[...].astype(BF16)
        off = 0
        for a_ref, mi in zip(a_refs, ms):
            acc_ref[off:off + mi, :] += _dot(a_ref[...].astype(BF16), bv)
            off += mi

        @pl.when(k == nk - 1)
        def _():
            o_ref[...] = acc_ref[...].astype(BF16)

    return pl.pallas_call(
        body, name=name, grid=(nk,),
        in_specs=[pl.BlockSpec((mi, bk), lambda k: (0, k)) for mi in ms] + [pl.BlockSpec((bk, n), lambda k: (k, 0))],
        out_specs=pl.BlockSpec((m, n), lambda k: (0, 0)),
        out_shape=jax.ShapeDtypeStruct((m, n), BF16),
        scratch_shapes=[pltpu.VMEM((m, n), F32)],
        compiler_params=_params(("arbitrary",)),
    )(*a_list, b)


def mix_in_fwd(x, gain, win_t, b_in_col, name):
    t, d = x.shape
    w = win_t.shape[0]
    tm = _token_tile(t)

    def body(x_ref, g_ref, w_ref, bias_ref, qkv_ref, zg_ref):
        xv = x_ref[...]
        r = lax.rsqrt(jnp.mean(xv * xv, axis=-1, keepdims=True) + EPS)
        h = (xv * r * g_ref[...]).astype(BF16)
        qkv_ref[...] = (_dg(w_ref[:QKV_WIDTH, :], h, NT) + bias_ref[:QKV_WIDTH, :]).astype(BF16)
        zg_ref[...] = (_dg(w_ref[QKV_WIDTH:, :], h, NT) + bias_ref[QKV_WIDTH:, :]).astype(BF16)

    return pl.pallas_call(
        body, name=name, grid=(t // tm,),
        in_specs=[pl.BlockSpec((tm, d), lambda i: (i, 0)), _resident((1, d)), _resident((w, d)), _resident((w, 1))],
        out_specs=[pl.BlockSpec((QKV_WIDTH, tm), lambda i: (0, i)), pl.BlockSpec((w - QKV_WIDTH, tm), lambda i: (0, i))],
        out_shape=[jax.ShapeDtypeStruct((QKV_WIDTH, t), BF16), jax.ShapeDtypeStruct((w - QKV_WIDTH, t), BF16)],
        compiler_params=_params(("arbitrary",)),
    )(x, gain, win_t, b_in_col)


def _attn_specs(nb):
    last = nb - 1
    cur = lambda n: jnp.minimum(n, last)
    prev = lambda n: jnp.maximum(jnp.minimum(n, last) - 1, 0)
    k_row = ATTN_WIDTH // KV_WIDTH
    return [
        pl.BlockSpec((ATTN_WIDTH, BLOCK), lambda n: (0, cur(n))),
        pl.BlockSpec((KV_WIDTH, BLOCK), lambda n: (k_row, prev(n))),
        pl.BlockSpec((KV_WIDTH, BLOCK), lambda n: (k_row, cur(n))),
        pl.BlockSpec((KV_WIDTH, BLOCK), lambda n: (k_row + 1, prev(n))),
        pl.BlockSpec((KV_WIDTH, BLOCK), lambda n: (k_row + 1, cur(n))),
    ]


def _attn_probs(q, kp, kc, sink, mask_prev, mask_cur):
    sp = jnp.where(mask_prev, _dg(kp, q, TN) * ATTN_SCALE, MASK_VALUE)
    sc = jnp.where(mask_cur, _dg(kc, q, TN) * ATTN_SCALE, MASK_VALUE)
    m = jnp.maximum(jnp.maximum(jnp.max(sp, axis=0, keepdims=True), jnp.max(sc, axis=0, keepdims=True)), sink)
    pp = jnp.exp(sp - m)
    pc = jnp.exp(sc - m)
    es = jnp.exp(sink - m)
    inv = 1.0 / (jnp.sum(pp, axis=0, keepdims=True) + jnp.sum(pc, axis=0, keepdims=True) + es)
    return pp * inv, pc * inv, es * inv


def _attn_masks(n):
    key = lax.broadcasted_iota(jnp.int32, (BLOCK, BLOCK), 0)
    qry = lax.broadcasted_iota(jnp.int32, (BLOCK, BLOCK), 1)
    return (key > qry) & (n > 0), key <= qry


def attn_fwd(qkv_t, sinks, name):
    t = qkv_t.shape[1]
    nb = t // BLOCK
    rep = N_Q_HEADS // N_KV_HEADS

    def body(sink_ref, q_ref, kp_ref, kc_ref, vp_ref, vc_ref, y_ref):
        n = pl.program_id(0)
        mask_prev, mask_cur = _attn_masks(n)
        for g in range(N_KV_HEADS):
            gs = slice(g * HEAD_DIM, (g + 1) * HEAD_DIM)
            kp, kc, vp, vc = kp_ref[gs, :], kc_ref[gs, :], vp_ref[gs, :], vc_ref[gs, :]
            for rr in range(rep):
                hd = g * rep + rr
                hs = slice(hd * HEAD_DIM, (hd + 1) * HEAD_DIM)
                pp, pc, _ = _attn_probs(q_ref[hs, :], kp, kc, sink_ref[0, hd], mask_prev, mask_cur)
                y_ref[hs, :] = (_dot(vp, pp.astype(BF16)) + _dot(vc, pc.astype(BF16))).astype(BF16)

    return pl.pallas_call(
        body, name=name, grid=(nb,),
        in_specs=[pl.BlockSpec(memory_space=pltpu.SMEM)] + _attn_specs(nb),
        out_specs=pl.BlockSpec((ATTN_WIDTH, BLOCK), lambda n: (0, n)),
        out_shape=jax.ShapeDtypeStruct((ATTN_WIDTH, t), BF16),
        compiler_params=_params(("arbitrary",)),
    )(sinks, qkv_t, qkv_t, qkv_t, qkv_t, qkv_t)


def attn_bwd(qkv_t, dy_t, sinks, name):
    t = qkv_t.shape[1]
    nb = t // BLOCK
    rep = N_Q_HEADS // N_KV_HEADS

    def body(sink_ref, q_ref, kp_ref, kc_ref, vp_ref, vc_ref, do_ref, dq_ref, dk_ref, dv_ref, dsink_ref,
             ck_ref, cv_ref, sacc_ref):
        n = pl.program_id(0)

        @pl.when(n == 0)
        def _():
            sacc_ref[...] = jnp.zeros_like(sacc_ref)

        @pl.when(n < nb)
        def _():
            mask_prev, mask_cur = _attn_masks(n)
            for g in range(N_KV_HEADS):
                gs = slice(g * HEAD_DIM, (g + 1) * HEAD_DIM)
                kp, kc, vp, vc = kp_ref[gs, :], kc_ref[gs, :], vp_ref[gs, :], vc_ref[gs, :]
                dkp = jnp.zeros((HEAD_DIM, BLOCK), F32)
                dkc = jnp.zeros((HEAD_DIM, BLOCK), F32)
                dvp = jnp.zeros((HEAD_DIM, BLOCK), F32)
                dvc = jnp.zeros((HEAD_DIM, BLOCK), F32)
                for rr in range(rep):
                    hd = g * rep + rr
                    hs = slice(hd * HEAD_DIM, (hd + 1) * HEAD_DIM)
                    q = q_ref[hs, :]
                    do = do_ref[hs, :]
                    pp, pc, ps = _attn_probs(q, kp, kc, sink_ref[0, hd], mask_prev, mask_cur)
                    dpp = _dg(vp, do, TN)
                    dpc = _dg(vc, do, TN)
                    delta = jnp.sum(pp * dpp, axis=0, keepdims=True) + jnp.sum(pc * dpc, axis=0, keepdims=True)
                    dsp = (pp * (dpp - delta)).astype(BF16)
                    dsc = (pc * (dpc - delta)).astype(BF16)
                    sacc_ref[hd:hd + 1, :] += -(ps * delta)
                    dq_ref[hs, :] = (ATTN_SCALE * (_dot(kp, dsp) + _dot(kc, dsc))).astype(BF16)
                    dkp = dkp + _dg(q, dsp, NT)
                    dkc = dkc + _dg(q, dsc, NT)
                    dvp = dvp + _dg(do, pp.astype(BF16), NT)
                    dvc = dvc + _dg(do, pc.astype(BF16), NT)

                @pl.when(n > 0)
                def _():
                    dk_ref[gs, :] = (ck_ref[gs, :] + ATTN_SCALE * dkp).astype(BF16)
                    dv_ref[gs, :] = (cv_ref[gs, :] + dvp).astype(BF16)

                ck_ref[gs, :] = ATTN_SCALE * dkc
                cv_ref[gs, :] = dvc

        @pl.when(n == nb)
        def _():
            dk_ref[...] = ck_ref[...].astype(BF16)
            dv_ref[...] = cv_ref[...].astype(BF16)
            dsink_ref[...] = jnp.sum(sacc_ref[...], axis=1, keepdims=True)

    last = nb - 1
    done = lambda n: jnp.maximum(n - 1, 0)
    outs = pl.pallas_call(
        body, name=name, grid=(nb + 1,),
        in_specs=[pl.BlockSpec(memory_space=pltpu.SMEM)] + _attn_specs(nb)
        + [pl.BlockSpec((ATTN_WIDTH, BLOCK), lambda n: (0, jnp.minimum(n, last)))],
        out_specs=[pl.BlockSpec((ATTN_WIDTH, BLOCK), lambda n: (0, jnp.minimum(n, last))),
                   pl.BlockSpec((KV_WIDTH, BLOCK), lambda n: (0, done(n))),
                   pl.BlockSpec((KV_WIDTH, BLOCK), lambda n: (0, done(n))),
                   pl.BlockSpec((N_Q_HEADS, 1), lambda n: (0, 0))],
        out_shape=[jax.ShapeDtypeStruct((ATTN_WIDTH, t), BF16), jax.ShapeDtypeStruct((KV_WIDTH, t), BF16),
                   jax.ShapeDtypeStruct((KV_WIDTH, t), BF16), jax.ShapeDtypeStruct((N_Q_HEADS, 1), F32)],
        scratch_shapes=[pltpu.VMEM((KV_WIDTH, BLOCK), F32), pltpu.VMEM((KV_WIDTH, BLOCK), F32),
                        pltpu.VMEM((N_Q_HEADS, BLOCK), F32)],
        compiler_params=_params(("arbitrary",)),
    )(sinks, qkv_t, qkv_t, qkv_t, qkv_t, qkv_t, dy_t)
    return outs


GELU_C = 0.7978845608028654
GELU_A = 0.044715


def _gelu(x):
    return 0.5 * x * (1.0 + jnp.tanh(GELU_C * (x + GELU_A * x * x * x)))


def _gelu_grad(x):
    th = jnp.tanh(GELU_C * (x + GELU_A * x * x * x))
    return 0.5 * (1.0 + th) + 0.5 * x * (1.0 - th * th) * GELU_C * (1.0 + 3.0 * GELU_A * x * x)


def _gmlp_parts(zg, lng, lnb):
    z = _gelu(zg)
    u = z[:GMLP_WIDTH, :]
    vv = z[GMLP_WIDTH:, :]
    mu = jnp.mean(vv, axis=0, keepdims=True)
    xc = vv - mu
    rstd = lax.rsqrt(jnp.mean(xc * xc, axis=0, keepdims=True) + EPS)
    xhat = xc * rstd
    return u, xhat, rstd, xhat * lng + lnb


def _causal(w):
    row = lax.broadcasted_iota(jnp.int32, (BLOCK, BLOCK), 0)
    col = lax.broadcasted_iota(jnp.int32, (BLOCK, BLOCK), 1)
    return jnp.where(col <= row, w, 0.0)


def gmlp_fwd(zg_t, lng, lnb, w_s, b_s, name):
    t = zg_t.shape[1]
    nb = t // BLOCK

    def body(zg_ref, lng_ref, lnb_ref, w_ref, bs_ref, y_ref):
        u, _, _, vn = _gmlp_parts(zg_ref[...].astype(F32), lng_ref[...], lnb_ref[...])
        for g in range(GMLP_GROUPS):
            gs = slice(g * GMLP_GROUP_DIM, (g + 1) * GMLP_GROUP_DIM)
            wc = _causal(w_ref[g]).astype(BF16)
            mixed = _dg(vn[gs, :].astype(BF16), wc, NT) + bs_ref[g:g + 1, :]
            y_ref[gs, :] = (u[gs, :] * mixed).astype(BF16)

    return pl.pallas_call(
        body, name=name, grid=(nb,),
        in_specs=[pl.BlockSpec((2 * GMLP_WIDTH, BLOCK), lambda n: (0, n)), _resident((GMLP_WIDTH, 1)),
                  _resident((GMLP_WIDTH, 1)), _resident((GMLP_GROUPS, BLOCK, BLOCK)), _resident((GMLP_GROUPS, BLOCK))],
        out_specs=pl.BlockSpec((GMLP_WIDTH, BLOCK), lambda n: (0, n)),
        out_shape=jax.ShapeDtypeStruct((GMLP_WIDTH, t), BF16),
        compiler_params=_params(("arbitrary",)),
    )(zg_t, lng, lnb, w_s, b_s)


def gmlp_bwd(zg_t, dy_t, lng, lnb, w_s, b_s, name):
    t = zg_t.shape[1]
    nb = t // BLOCK

    def body(zg_ref, dy_ref, lng_ref, lnb_ref, w_ref, bs_ref, dzg_ref, dw_ref, dbs_ref, dlng_ref, dlnb_ref,
             gacc_ref, bacc_ref):
        n = pl.program_id(0)

        @pl.when(n == 0)
        def _():
            dw_ref[...] = jnp.zeros_like(dw_ref)
            dbs_ref[...] = jnp.zeros_like(dbs_ref)
            gacc_ref[...] = jnp.zeros_like(gacc_ref)
            bacc_ref[...] = jnp.zeros_like(bacc_ref)

        zg = zg_ref[...].astype(F32)
        lng_v = lng_ref[...]
        u, xhat, rstd, vn = _gmlp_parts(zg, lng_v, lnb_ref[...])
        dy = dy_ref[...].astype(F32)
        dvn_parts = []
        for g in range(GMLP_GROUPS):
            gs = slice(g * GMLP_GROUP_DIM, (g + 1) * GMLP_GROUP_DIM)
            wc = _causal(w_ref[g]).astype(BF16)
            vn_g = vn[gs, :].astype(BF16)
            mixed = _dg(vn_g, wc, NT) + bs_ref[g:g + 1, :]
            dy_g = dy[gs, :]
            dmixed = dy_g * u[gs, :]
            dmixed_b = dmixed.astype(BF16)
            dzg_ref[gs, :] = (dy_g * mixed * _gelu_grad(zg[gs, :])).astype(BF16)
            dw_ref[g] += _causal(_dg(dmixed_b, vn_g, TN))
            dbs_ref[g:g + 1, :] += jnp.sum(dmixed, axis=0, keepdims=True)
            dvn_parts.append(_dot(dmixed_b, wc))
        dvn = jnp.concatenate(dvn_parts, axis=0)
        gacc_ref[...] += dvn * xhat
        bacc_ref[...] += dvn
        dxhat = dvn * lng_v
        m1 = jnp.mean(dxhat, axis=0, keepdims=True)
        m2 = jnp.mean(dxhat * xhat, axis=0, keepdims=True)
        dvv = rstd * (dxhat - m1 - xhat * m2)
        dzg_ref[GMLP_WIDTH:, :] = (dvv * _gelu_grad(zg[GMLP_WIDTH:, :])).astype(BF16)

        @pl.when(n == nb - 1)
        def _():
            dlng_ref[...] = jnp.sum(gacc_ref[...], axis=1, keepdims=True)
            dlnb_ref[...] = jnp.sum(bacc_ref[...], axis=1, keepdims=True)

    const2 = lambda n: (0, 0)
    return pl.pallas_call(
        body, name=name, grid=(nb,),
        in_specs=[pl.BlockSpec((2 * GMLP_WIDTH, BLOCK), lambda n: (0, n)), pl.BlockSpec((GMLP_WIDTH, BLOCK), lambda n: (0, n)),
                  _resident((GMLP_WIDTH, 1)), _resident((GMLP_WIDTH, 1)),
                  _resident((GMLP_GROUPS, BLOCK, BLOCK)), _resident((GMLP_GROUPS, BLOCK))],
        out_specs=[pl.BlockSpec((2 * GMLP_WIDTH, BLOCK), lambda n: (0, n)),
                   pl.BlockSpec((GMLP_GROUPS, BLOCK, BLOCK), lambda n: (0, 0, 0)),
                   pl.BlockSpec((GMLP_GROUPS, BLOCK), const2),
                   pl.BlockSpec((GMLP_WIDTH, 1), const2), pl.BlockSpec((GMLP_WIDTH, 1), const2)],
        out_shape=[jax.ShapeDtypeStruct((2 * GMLP_WIDTH, t), BF16),
                   jax.ShapeDtypeStruct((GMLP_GROUPS, BLOCK, BLOCK), F32),
                   jax.ShapeDtypeStruct((GMLP_GROUPS, BLOCK), F32),
                   jax.ShapeDtypeStruct((GMLP_WIDTH, 1), F32), jax.ShapeDtypeStruct((GMLP_WIDTH, 1), F32)],
        scratch_shapes=[pltpu.VMEM((GMLP_WIDTH, BLOCK), F32), pltpu.VMEM((GMLP_WIDTH, BLOCK), F32)],
        compiler_params=_params(("arbitrary",)),
    )(zg_t, dy_t, lng, lnb, w_s, b_s)


def _rmsnorm_cols(y, gain_col):
    r = lax.rsqrt(jnp.mean(y * y, axis=0, keepdims=True) + EPS)
    n = y * r
    return n, r, n * gain_col


def mix_out_fwd(x, ya_t, yg_t, gao, ggo, wout, b_out, name):
    t, d = x.shape
    tm = _token_tile(t)

    def body(x_ref, ya_ref, yg_ref, gao_ref, ggo_ref, w_ref, b_ref, o_ref):
        _, _, ya = _rmsnorm_cols(ya_ref[...].astype(F32), gao_ref[...])
        _, _, yg = _rmsnorm_cols(yg_ref[...].astype(F32), ggo_ref[...])
        o_ref[...] = (x_ref[...] + _dg(ya.astype(BF16), w_ref[:ATTN_WIDTH, :], TN)
                      + _dg(yg.astype(BF16), w_ref[ATTN_WIDTH:, :], TN) + b_ref[...])

    return pl.pallas_call(
        body, name=name, grid=(t // tm,),
        in_specs=[pl.BlockSpec((tm, d), lambda i: (i, 0)), pl.BlockSpec((ATTN_WIDTH, tm), lambda i: (0, i)),
                  pl.BlockSpec((GMLP_WIDTH, tm), lambda i: (0, i)), _resident((ATTN_WIDTH, 1)), _resident((GMLP_WIDTH, 1)),
                  _resident((ATTN_WIDTH + GMLP_WIDTH, d)), _resident((1, d))],
        out_specs=pl.BlockSpec((tm, d), lambda i: (i, 0)),
        out_shape=jax.ShapeDtypeStruct((t, d), F32),
        compiler_params=_params(("arbitrary",)),
    )(x, ya_t, yg_t, gao, ggo, wout, b_out)


def _lane_fold(v):
    out = v[:, :LANES]
    for j in range(1, v.shape[1] // LANES):
        out = out + v[:, j * LANES:(j + 1) * LANES]
    return out


def mix_out_bwd(dx, ya_t, yg_t, gao, ggo, wout, name):
    t, d = dx.shape
    tm = _token_tile(t)
    nt = t // tm

    def body(dx_ref, ya_ref, yg_ref, gao_ref, ggo_ref, w_ref, dya_ref, dyg_ref, y_ref, dgao_ref, dggo_ref, db_ref,
             acc_ref):
        i = pl.program_id(0)

        @pl.when(i == 0)
        def _():
            acc_ref[...] = jnp.zeros_like(acc_ref)
            db_ref[...] = jnp.zeros_like(db_ref)

        dxv = dx_ref[...]
        db_ref[...] += jnp.sum(dxv, axis=0, keepdims=True)
        dxb = dxv.astype(BF16)
        for part, (src_ref, gain_ref, dst_ref) in enumerate(((ya_ref, gao_ref, dya_ref), (yg_ref, ggo_ref, dyg_ref))):
            rows = slice(part * ATTN_WIDTH, (part + 1) * ATTN_WIDTH)
            gain = gain_ref[...]
            nrm, r, yn = _rmsnorm_cols(src_ref[...].astype(F32), gain)
            y_ref[rows, :] = yn.astype(BF16)
            dy = _dg(w_ref[rows, :], dxb, NT)
            acc_ref[rows, :] += _lane_fold(dy * nrm)
            dn = dy * gain
            dst_ref[...] = (r * (dn - nrm * jnp.mean(dn * nrm, axis=0, keepdims=True))).astype(BF16)

        @pl.when(i == nt - 1)
        def _():
            dgao_ref[...] = jnp.sum(acc_ref[:ATTN_WIDTH, :], axis=1, keepdims=True)
            dggo_ref[...] = jnp.sum(acc_ref[ATTN_WIDTH:, :], axis=1, keepdims=True)

    const2 = lambda i: (0, 0)
    feat = lambda w: pl.BlockSpec((w, tm), lambda i: (0, i))
    return pl.pallas_call(
        body, name=name, grid=(nt,),
        in_specs=[pl.BlockSpec((tm, d), lambda i: (i, 0)), feat(ATTN_WIDTH), feat(GMLP_WIDTH),
                  _resident((ATTN_WIDTH, 1)), _resident((GMLP_WIDTH, 1)), _resident((ATTN_WIDTH + GMLP_WIDTH, d))],
        out_specs=[feat(ATTN_WIDTH), feat(GMLP_WIDTH), feat(ATTN_WIDTH + GMLP_WIDTH),
                   pl.BlockSpec((ATTN_WIDTH, 1), const2), pl.BlockSpec((GMLP_WIDTH, 1), const2),
                   pl.BlockSpec((1, d), const2)],
        out_shape=[jax.ShapeDtypeStruct((ATTN_WIDTH, t), BF16), jax.ShapeDtypeStruct((GMLP_WIDTH, t), BF16),
                   jax.ShapeDtypeStruct((ATTN_WIDTH + GMLP_WIDTH, t), BF16),
                   jax.ShapeDtypeStruct((ATTN_WIDTH, 1), F32), jax.ShapeDtypeStruct((GMLP_WIDTH, 1), F32),
                   jax.ShapeDtypeStruct((1, d), F32)],
        scratch_shapes=[pltpu.VMEM((ATTN_WIDTH + GMLP_WIDTH, LANES), F32)],
        compiler_params=_params(("arbitrary",)),
    )(dx, ya_t, yg_t, gao, ggo, wout)


def mix_in_bwd(x, dxo, gain, dq_t, dk_t, dv_t, dzg_t, win_t, name):
    t, d = x.shape
    w = win_t.shape[0]
    tm = _token_tile(t)
    nt = t // tm
    parts = ((0, ATTN_WIDTH), (ATTN_WIDTH, KV_WIDTH), (ATTN_WIDTH + KV_WIDTH, KV_WIDTH), (QKV_WIDTH, w - QKV_WIDTH))

    def body(x_ref, dxo_ref, g_ref, dq_ref, dk_ref, dv_ref, dzg_ref, w_ref, dx_ref, h_ref, dg_ref, db_ref, acc_ref):
        i = pl.program_id(0)

        @pl.when(i == 0)
        def _():
            acc_ref[...] = jnp.zeros_like(acc_ref)
            dg_ref[...] = jnp.zeros_like(dg_ref)

        xv = x_ref[...]
        gain_v = g_ref[...]
        r = lax.rsqrt(jnp.mean(xv * xv, axis=-1, keepdims=True) + EPS)
        h_ref[...] = (xv * r * gain_v).astype(BF16)
        dh = jnp.zeros((tm, d), F32)
        for (off, size), src_ref in zip(parts, (dq_ref, dk_ref, dv_ref, dzg_ref)):
            dp = src_ref[...]
            acc_ref[off:off + size, :] += _lane_fold(dp.astype(F32))
            dh = dh + _dg(dp, w_ref[off:off + size, :], TN)
        dx, dgain = _rmsnorm_bwd_rows(xv, r, gain_v, dh)
        dx_ref[...] = dxo_ref[...] + dx
        dg_ref[...] += dgain

        @pl.when(i == nt - 1)
        def _():
            db_ref[...] = jnp.sum(acc_ref[...], axis=1, keepdims=True)

    const2 = lambda i: (0, 0)
    tok = lambda: pl.BlockSpec((tm, d), lambda i: (i, 0))
    feat = lambda rows: pl.BlockSpec((rows, tm), lambda i: (0, i))
    return pl.pallas_call(
        body, name=name, grid=(nt,),
        in_specs=[tok(), tok(), _resident((1, d)), feat(ATTN_WIDTH), feat(KV_WIDTH), feat(KV_WIDTH),
                  feat(w - QKV_WIDTH), _resident((w, d))],
        out_specs=[tok(), tok(), pl.BlockSpec((1, d), const2), pl.BlockSpec((w, 1), const2)],
        out_shape=[jax.ShapeDtypeStruct((t, d), F32), jax.ShapeDtypeStruct((t, d), BF16),
                   jax.ShapeDtypeStruct((1, d), F32), jax.ShapeDtypeStruct((w, 1), F32)],
        scratch_shapes=[pltpu.VMEM((w, LANES), F32)],
        compiler_params=_params(("arbitrary",)),
    )(x, dxo, gain, dq_t, dk_t, dv_t, dzg_t, win_t)


def final_bwd(x, target, gain, name):
    t, d = x.shape
    tm = _token_tile(t)

    def body(x_ref, t_ref, g_ref, dx_ref, dg_ref, loss_ref):
        i = pl.program_id(0)

        @pl.when(i == 0)
        def _():
            dg_ref[...] = jnp.zeros_like(dg_ref)
            loss_ref[...] = jnp.zeros_like(loss_ref)

        xv = x_ref[...]
        gain_v = g_ref[...]
        r = lax.rsqrt(jnp.mean(xv * xv, axis=-1, keepdims=True) + EPS)
        err = xv * r * gain_v - t_ref[...]
        row_loss = jnp.mean(err * err, axis=-1, keepdims=True)
        loss_ref[...] += 0.5 * jnp.sum(row_loss, axis=0, keepdims=True)
        dx, dgain = _rmsnorm_bwd_rows(xv, r, gain_v, err * (1.0 / d))
        dx_ref[...] = dx
        dg_ref[...] += dgain

    const2 = lambda i: (0, 0)
    return pl.pallas_call(
        body, name=name, grid=(t // tm,),
        in_specs=[pl.BlockSpec((tm, d), lambda i: (i, 0)), pl.BlockSpec((tm, d), lambda i: (i, 0)), _resident((1, d))],
        out_specs=[pl.BlockSpec((tm, d), lambda i: (i, 0)), pl.BlockSpec((1, d), const2), pl.BlockSpec((1, LANES), const2)],
        out_shape=[jax.ShapeDtypeStruct((t, d), F32), jax.ShapeDtypeStruct((1, d), F32),
                   jax.ShapeDtypeStruct((1, LANES), F32)],
        compiler_params=_params(("arbitrary",)),
    )(x, target, gain)


def _adamw_math(w, g, m, v):
    m = ADAM_B1 * m + (1.0 - ADAM_B1) * g
    v = ADAM_B2 * v + (1.0 - ADAM_B2) * (g * g)
    m_hat = m / (1.0 - ADAM_B1 ** ADAM_STEP)
    v_hat = v / (1.0 - ADAM_B2 ** ADAM_STEP)
    delta = -ADAM_LR * (m_hat / (jnp.sqrt(v_hat) + ADAM_EPS) + ADAM_WD * w)
    return delta, m, v


def adamw(w, g, m, v, name):
    rows, cols = w.shape
    tr = rows
    if rows * cols > 256 * 1024:
        for cand in (256, 128, 64, 32, 16, 8):
            if rows % cand == 0:
                tr = cand
                break

    def body(w_ref, g_ref, m_ref, v_ref, d_ref, mo_ref, vo_ref):
        delta, mn, vn = _adamw_math(w_ref[...], g_ref[...], m_ref[...], v_ref[...])
        d_ref[...] = delta
        mo_ref[...] = mn
        vo_ref[...] = vn

    spec = pl.BlockSpec((tr, cols), lambda i: (i, 0))
    shape = jax.ShapeDtypeStruct((rows, cols), F32)
    return pl.pallas_call(
        body, name=name, grid=(rows // tr,),
        in_specs=[spec] * 4, out_specs=[spec] * 3, out_shape=[shape] * 3,
        compiler_params=_params(("arbitrary",)),
    )(w, g, m, v)


def sum_partials(landing, name):
    nd, rows, cols = landing.shape
    tr = rows
    for cand in (128, 112, 88, 64, 32, 16):
        if rows % cand == 0:
            tr = cand
            break

    def body(l_ref, o_ref):
        acc = l_ref[0].astype(F32)
        for j in range(1, nd):
            acc = acc + l_ref[j].astype(F32)
        o_ref[...] = acc

    return pl.pallas_call(
        body, name=name, grid=(rows // tr,),
        in_specs=[pl.BlockSpec((nd, tr, cols), lambda i: (0, i, 0))],
        out_specs=pl.BlockSpec((tr, cols), lambda i: (i, 0)),
        out_shape=jax.ShapeDtypeStruct((rows, cols), F32),
        compiler_params=_params(("arbitrary",)),
    )(landing)


def _mesh_place():
    x, y, c = lax.axis_index("x"), lax.axis_index("y"), lax.axis_index("c")
    return x, y, c, 4 * x + 2 * y + c


def _peer(x, y, c, k):
    return (x ^ ((k >> 2) & 1), y ^ ((k >> 1) & 1), c ^ (k & 1))


def allgather_rows(shards, name):
    na = len(shards)

    def body(*refs):
        in_refs, out_refs = refs[:na], refs[na:2 * na]
        send_sems, recv_sems, local_sems = refs[2 * na:]
        x, y, c, me = _mesh_place()
        local = [pltpu.make_async_copy(in_refs[a], out_refs[a].at[me], local_sems.at[a]) for a in range(na)]
        for cp in local:
            cp.start()
        sends = []
        for k in range(1, N_DEV):
            for a in range(na):
                cp = pltpu.make_async_remote_copy(
                    src_ref=in_refs[a], dst_ref=out_refs[a].at[me], send_sem=send_sems.at[k - 1, a],
                    recv_sem=recv_sems.at[k - 1, a], device_id=_peer(x, y, c, k), device_id_type=pl.DeviceIdType.MESH)
                cp.start()
                sends.append(cp)
        for k in range(1, N_DEV):
            for a in range(na):
                pltpu.make_async_remote_copy(
                    src_ref=in_refs[a], dst_ref=out_refs[a].at[me ^ k], send_sem=send_sems.at[k - 1, a],
                    recv_sem=recv_sems.at[k - 1, a], device_id=_peer(x, y, c, k),
                    device_id_type=pl.DeviceIdType.MESH).wait_recv()
        for cp in sends:
            cp.wait_send()
        for cp in local:
            cp.wait()

    any_spec = pl.BlockSpec(memory_space=pl.ANY)
    return pl.pallas_call(
        body, name=name,
        in_specs=[any_spec] * na, out_specs=[any_spec] * na,
        out_shape=[jax.ShapeDtypeStruct((N_DEV,) + s.shape, s.dtype) for s in shards],
        scratch_shapes=[pltpu.SemaphoreType.DMA((N_DEV - 1, na)), pltpu.SemaphoreType.DMA((N_DEV - 1, na)),
                        pltpu.SemaphoreType.DMA((na,))],
    )(*shards)


def exchange_blocks(grads, name):
    na = len(grads)

    def body(*refs):
        in_refs, out_refs = refs[:na], refs[na:2 * na]
        send_sems, recv_sems, local_sems = refs[2 * na:]
        x, y, c, me = _mesh_place()
        local = [pltpu.make_async_copy(in_refs[a].at[me], out_refs[a].at[me], local_sems.at[a]) for a in range(na)]
        for cp in local:
            cp.start()
        sends = []
        for k in range(1, N_DEV):
            for a in range(na):
                cp = pltpu.make_async_remote_copy(
                    src_ref=in_refs[a].at[me ^ k], dst_ref=out_refs[a].at[me], send_sem=send_sems.at[k - 1, a],
                    recv_sem=recv_sems.at[k - 1, a], device_id=_peer(x, y, c, k), device_id_type=pl.DeviceIdType.MESH)
                cp.start()
                sends.append(cp)
        for k in range(1, N_DEV):
            for a in range(na):
                pltpu.make_async_remote_copy(
                    src_ref=in_refs[a].at[me ^ k], dst_ref=out_refs[a].at[me ^ k], send_sem=send_sems.at[k - 1, a],
                    recv_sem=recv_sems.at[k - 1, a], device_id=_peer(x, y, c, k),
                    device_id_type=pl.DeviceIdType.MESH).wait_recv()
        for cp in sends:
            cp.wait_send()
        for cp in local:
            cp.wait()

    any_spec = pl.BlockSpec(memory_space=pl.ANY)
    return pl.pallas_call(
        body, name=name,
        in_specs=[any_spec] * na, out_specs=[any_spec] * na,
        out_shape=[jax.ShapeDtypeStruct(g.shape, g.dtype) for g in grads],
        scratch_shapes=[pltpu.SemaphoreType.DMA((N_DEV - 1, na)), pltpu.SemaphoreType.DMA((N_DEV - 1, na)),
                        pltpu.SemaphoreType.DMA((na,))],
    )(*grads)


def allreduce_small(pack, name):
    rows, cols = pack.shape

    def body(p_ref, o_ref, land_ref, send_sems, recv_sems):
        x, y, c, me = _mesh_place()
        sends = []
        for k in range(1, N_DEV):
            cp = pltpu.make_async_remote_copy(
                src_ref=p_ref, dst_ref=land_ref.at[me], send_sem=send_sems.at[k - 1], recv_sem=recv_sems.at[k - 1],
                device_id=_peer(x, y, c, k), device_id_type=pl.DeviceIdType.MESH)
            cp.start()
            sends.append(cp)
        land_ref[me] = p_ref[...]
        for k in range(1, N_DEV):
            pltpu.make_async_remote_copy(
                src_ref=p_ref, dst_ref=land_ref.at[me ^ k], send_sem=send_sems.at[k - 1], recv_sem=recv_sems.at[k - 1],
                device_id=_peer(x, y, c, k), device_id_type=pl.DeviceIdType.MESH).wait_recv()
        for cp in sends:
            cp.wait_send()
        acc = land_ref[0]
        for j in range(1, N_DEV):
            acc = acc + land_ref[j]
        o_ref[...] = acc

    vmem = pl.BlockSpec(memory_space=pltpu.VMEM)
    return pl.pallas_call(
        body, name=name,
        in_specs=[vmem], out_specs=vmem,
        out_shape=jax.ShapeDtypeStruct((rows, cols), F32),
        scratch_shapes=[pltpu.VMEM((N_DEV, rows, cols), F32), pltpu.SemaphoreType.DMA((N_DEV - 1,)),
                        pltpu.SemaphoreType.DMA((N_DEV - 1,))],
        compiler_params=pltpu.CompilerParams(vmem_limit_bytes=VMEM_LIMIT),
    )(pack)


def _pack_rows(size):
    rows = -(-size // LANES)
    return -(-rows // 8) * 8


def pack_small(parts):
    out = []
    for p in parts:
        flat = p.reshape(-1).astype(F32)
        rows = _pack_rows(flat.shape[0])
        out.append(jnp.pad(flat, (0, rows * LANES - flat.shape[0])).reshape(rows, LANES))
    return jnp.concatenate(out, axis=0)


def unpack_small(pack, shapes):
    out, off = [], 0
    for shp in shapes:
        size = 1
        for s in shp:
            size *= s
        rows = _pack_rows(size)
        out.append(pack[off:off + rows].reshape(-1)[:size].reshape(shp))
        off += rows
    return out


def local_step(x, target, small, big):
    col = lambda v: v.reshape(-1, 1)
    x1, a1, b1 = ffn_fwd(x, small["ffn1_norm_g"], big["wg1"], big["wu1"], big["wd1"], "ffn1_fwd")
    qkv_t, zg_t = mix_in_fwd(x1, small["mix_norm_g"], big["win"], col(small["b_in"]), "mix_in_fwd")
    ya_t = attn_fwd(qkv_t, small["attn_sinks"], "attn_fwd")
    lng, lnb = col(small["gmlp_ln_g"]), col(small["gmlp_ln_b"])
    w_s, b_s = small["gmlp_w_s"][0], small["gmlp_b_s"][0]
    yg_t = gmlp_fwd(zg_t, lng, lnb, w_s, b_s, "gmlp_fwd")
    gao, ggo = col(small["attn_out_norm_g"]), col(small["gmlp_out_norm_g"])
    x2 = mix_out_fwd(x1, ya_t, yg_t, gao, ggo, big["wout"], small["b_out"], "mix_out_fwd")
    x3, a2, b2 = ffn_fwd(x2, small["ffn2_norm_g"], big["wg2"], big["wu2"], big["wd2"], "ffn2_fwd")

    gs, gb = {}, {}
    dx3, gs["final_norm_g"], loss = final_bwd(x3, target, small["final_norm_g"].reshape(1, -1), "final_bwd")

    dx2, da, db, s, h, gs["ffn2_norm_g"] = ffn_bwd(x2, dx3, small["ffn2_norm_g"], a2, b2,
                                                    big["wg2"], big["wu2"], big["wd2"], "ffn2_bwd")
    gb["wg2"] = grad_tn(da, h, 1.0, "ffn2_dwg")
    gb["wu2"] = grad_tn(db, h, 1.0, "ffn2_dwu")
    gb["wd2"] = grad_tn(s, dx3, FFN_RES, "ffn2_dwd")

    dya_t, dyg_t, y_t, dgao, dggo, gs["b_out"] = mix_out_bwd(dx2, ya_t, yg_t, gao, ggo, big["wout"], "mix_out_bwd")
    gs["attn_out_norm_g"], gs["gmlp_out_norm_g"] = dgao.reshape(1, -1), dggo.reshape(1, -1)
    gb["wout"] = grad_nn([y_t], dx2, "dwout")
    dzg_t, dws, dbs, dlng, dlnb = gmlp_bwd(zg_t, dyg_t, lng, lnb, w_s, b_s, "gmlp_bwd")
    gs["gmlp_w_s"], gs["gmlp_b_s"] = dws[None], dbs[None]
    gs["gmlp_ln_g"], gs["gmlp_ln_b"] = dlng.reshape(1, -1), dlnb.reshape(1, -1)
    dq_t, dk_t, dv_t, dsinks = attn_bwd(qkv_t, dya_t, small["attn_sinks"], "attn_bwd")
    gs["attn_sinks"] = dsinks.reshape(1, -1)
    dx1, h2, gs["mix_norm_g"], dbin = mix_in_bwd(x1, dx2, small["mix_norm_g"], dq_t, dk_t, dv_t, dzg_t,
                                                 big["win"], "mix_in_bwd")
    gs["b_in"] = dbin.reshape(1, -1)
    gb["win"] = grad_nn([dq_t, dk_t, dv_t, dzg_t], h2, "dwin")

    dx0, da, db, s, h, gs["ffn1_norm_g"] = ffn_bwd(x, dx1, small["ffn1_norm_g"], a1, b1,
                                                   big["wg1"], big["wu1"], big["wd1"], "ffn1_bwd")
    gb["wg1"] = grad_tn(da, h, 1.0, "ffn1_dwg")
    gb["wu1"] = grad_tn(db, h, 1.0, "ffn1_dwu")
    gb["wd1"] = grad_tn(s, dx1, FFN_RES, "ffn1_dwd")
    return dx0, gb, gs, loss


SMALL_NAMES = ["ffn1_norm_g", "mix_norm_g", "b_in", "attn_sinks", "gmlp_ln_g", "gmlp_ln_b", "gmlp_w_s", "gmlp_b_s",
               "attn_out_norm_g", "gmlp_out_norm_g", "b_out", "ffn2_norm_g", "final_norm_g"]
BIG_NAMES = {"wg1": ("ffn1_w_gate", True), "wu1": ("ffn1_w_up", True), "wd1": ("ffn1_w_down", False),
             "win": ("w_in", True), "wout": ("w_out", False),
             "wg2": ("ffn2_w_gate", True), "wu2": ("ffn2_w_up", True), "wd2": ("ffn2_w_down", False)}
WEIGHT_ORDER = ["ffn1_norm_g", "ffn1_w_gate", "ffn1_w_up", "ffn1_w_down", "mix_norm_g", "w_in", "b_in", "attn_sinks",
                "gmlp_ln_g", "gmlp_ln_b", "gmlp_w_s", "gmlp_b_s", "attn_out_norm_g", "gmlp_out_norm_g", "w_out",
                "b_out", "ffn2_norm_g", "ffn2_w_gate", "ffn2_w_up", "ffn2_w_down", "final_norm_g"]


def kernel(x, ffn1_norm_g, ffn1_w_gate, ffn1_w_up, ffn1_w_down, mix_norm_g, w_in, b_in, attn_sinks, gmlp_ln_g, gmlp_ln_b, gmlp_w_s, gmlp_b_s, attn_out_norm_g, gmlp_out_norm_g, w_out, b_out, ffn2_norm_g, ffn2_w_gate, ffn2_w_up, ffn2_w_down, final_norm_g, loss_target, m_ffn1_norm_g, m_ffn1_w_gate, m_ffn1_w_up, m_ffn1_w_down, m_mix_norm_g, m_w_in, m_b_in, m_attn_sinks, m_gmlp_ln_g, m_gmlp_ln_b, m_gmlp_w_s, m_gmlp_b_s, m_attn_out_norm_g, m_gmlp_out_norm_g, m_w_out, m_b_out, m_ffn2_norm_g, m_ffn2_w_gate, m_ffn2_w_up, m_ffn2_w_down, m_final_norm_g, v_ffn1_norm_g, v_ffn1_w_gate, v_ffn1_w_up, v_ffn1_w_down, v_mix_norm_g, v_w_in, v_b_in, v_attn_sinks, v_gmlp_ln_g, v_gmlp_ln_b, v_gmlp_w_s, v_gmlp_b_s, v_attn_out_norm_g, v_gmlp_out_norm_g, v_w_out, v_b_out, v_ffn2_norm_g, v_ffn2_w_gate, v_ffn2_w_up, v_ffn2_w_down, v_final_norm_g):
    args = dict(locals())
    w = {n: args[n] for n in WEIGHT_ORDER}
    m = {n: args["m_" + n] for n in WEIGHT_ORDER}
    v = {n: args["v_" + n] for n in WEIGHT_ORDER}

    shards = {}
    for key, (pname, transposed) in BIG_NAMES.items():
        mat = w[pname][0]
        shards[key] = (mat.T if transposed else mat).astype(BF16)
    keys = list(BIG_NAMES)
    gathered = allgather_rows([shards[k] for k in keys], "allgather_weights")
    big = {k: g.reshape(-1, g.shape[-1]) for k, g in zip(keys, gathered)}

    small = {n: w[n] for n in SMALL_NAMES}
    grad_x, gb, gs, loss = local_step(x[0], loss_target[0], small, big)

    landing = exchange_blocks([gb[k].reshape(N_DEV, -1, gb[k].shape[-1]) for k in keys], "exchange_grads")
    grads, deltas, new_m, new_v = {}, {}, {}, {}
    for key, land in zip(keys, landing):
        pname, transposed = BIG_NAMES[key]
        g = sum_partials(land, "sum_" + key)
        g = g.T if transposed else g
        d_, m_, v_ = adamw(w[pname][0], g, m[pname][0], v[pname][0], "adamw_" + key)
        grads[pname], deltas[pname], new_m[pname], new_v[pname] = g[None], d_[None], m_[None], v_[None]

    shapes = [w[n].shape for n in SMALL_NAMES]
    total = allreduce_small(pack_small([gs[n] for n in SMALL_NAMES] + [loss[:, :1]]), "allreduce_small")
    small_g = unpack_small(total, shapes + [(1, 1)])
    loss_total = small_g[-1].reshape(())
    gpack = pack_small(small_g[:-1])
    d_, m_, v_ = adamw(pack_small([w[n] for n in SMALL_NAMES]), gpack, pack_small([m[n] for n in SMALL_NAMES]),
                       pack_small([v[n] for n in SMALL_NAMES]), "adamw_small")
    for n, g_, dd, mm, vv in zip(SMALL_NAMES, small_g[:-1], unpack_small(d_, shapes), unpack_small(m_, shapes),
                                 unpack_small(v_, shapes)):
        grads[n], deltas[n], new_m[n], new_v[n] = g_, dd, mm, vv

    return (loss_total, grad_x[None], *[grads[n] for n in WEIGHT_ORDER], *[deltas[n] for n in WEIGHT_ORDER],
            *[new_m[n] for n in WEIGHT_ORDER], *[new_v[n] for n in WEIGHT_ORDER])
```

```python
import functools

import jax
import jax.numpy as jnp
from jax import lax
from jax.experimental import pallas as pl
from jax.experimental.pallas import tpu as pltpu

F32 = jnp.float32
BF16 = jnp.bfloat16

N_DEV = 8
N_Q_HEADS = 8
N_KV_HEADS = 2
HEAD_DIM = 64
ATTN_WIDTH = N_Q_HEADS * HEAD_DIM
KV_WIDTH = N_KV_HEADS * HEAD_DIM
QKV_WIDTH = ATTN_WIDTH + 2 * KV_WIDTH
BLOCK = 128
GMLP_GROUPS = 8
GMLP_GROUP_DIM = 64
GMLP_WIDTH = GMLP_GROUPS * GMLP_GROUP_DIM
EPS = 1e-6
FFN_RES = 0.5
ATTN_SCALE = HEAD_DIM ** -0.5
MASK_VALUE = -1e30

ADAM_LR = 0.001
ADAM_B1 = 0.9
ADAM_B2 = 0.999
ADAM_EPS = 1e-08
ADAM_WD = 0.01
ADAM_STEP = 10

V7X_VMEM_BYTES = 64 * 1024 * 1024
VMEM_LIMIT = 56 * 1024 * 1024
LANES = 128

NT = (((1,), (1,)), ((), ()))
TN = (((0,), (0,)), ((), ()))


def _dot(a, b):
    return jnp.dot(a, b, preferred_element_type=F32)


def _dg(a, b, dims):
    return lax.dot_general(a, b, dims, preferred_element_type=F32)


def _params(semantics=None):
    return pltpu.CompilerParams(dimension_semantics=semantics, vmem_limit_bytes=VMEM_LIMIT)


def _resident(shape):
    zeros = (0,) * len(shape)
    return pl.BlockSpec(shape, lambda *_: zeros, pipeline_mode=pl.Buffered(1))


def _drop_deps(body, n_in, n_deps):
    return lambda *refs: body(*refs[:n_in], *refs[n_in + n_deps:])


ANY_SPEC = pl.BlockSpec(memory_space=pl.ANY)


def _token_tile(t):
    return min(t, 512)


def _f_chunk(f):
    for c in (256, 128):
        if f % c == 0:
            return c
    return f


def ffn_fwd(x, gain, wg_t, wu_t, wd, name, deps=()):
    t, d = x.shape
    f = wd.shape[0]
    tm = _token_tile(t)
    fc = _f_chunk(f)

    def body(x_ref, g_ref, wg_ref, wu_ref, wd_ref, xo_ref, a_ref, b_ref):
        xv = x_ref[...]
        r = lax.rsqrt(jnp.mean(xv * xv, axis=-1, keepdims=True) + EPS)
        h = (xv * r * g_ref[...]).astype(BF16)
        acc = jnp.zeros((tm, d), F32)
        for c in range(f // fc):
            sl = slice(c * fc, (c + 1) * fc)
            a = _dg(h, wg_ref[sl, :], NT)
            b = _dg(h, wu_ref[sl, :], NT)
            a_ref[:, sl] = a.astype(BF16)
            b_ref[:, sl] = b.astype(BF16)
            s = (a * jax.nn.sigmoid(a) * b).astype(BF16)
            acc = acc + _dot(s, wd_ref[sl, :])
        xo_ref[...] = xv + FFN_RES * acc

    return pl.pallas_call(
        _drop_deps(body, 5, len(deps)), name=name, grid=(t // tm,),
        in_specs=[pl.BlockSpec((tm, d), lambda i: (i, 0)), _resident((1, d)),
                  _resident((f, d)), _resident((f, d)), _resident((f, d))] + [ANY_SPEC] * len(deps),
        out_specs=[pl.BlockSpec((tm, d), lambda i: (i, 0)), pl.BlockSpec((tm, f), lambda i: (i, 0)),
                   pl.BlockSpec((tm, f), lambda i: (i, 0))],
        out_shape=[jax.ShapeDtypeStruct((t, d), F32), jax.ShapeDtypeStruct((t, f), BF16),
                   jax.ShapeDtypeStruct((t, f), BF16)],
        compiler_params=_params(("arbitrary",)),
    )(x, gain, wg_t, wu_t, wd, *deps)


def _rmsnorm_bwd_rows(xv, r, gain, dh):
    n = xv * r
    dgain = jnp.sum(dh * n, axis=0, keepdims=True)
    dn = dh * gain
    dx = r * (dn - n * jnp.mean(dn * n, axis=-1, keepdims=True))
    return dx, dgain


def ffn_bwd(x, dxo, gain, a, b, wg_t, wu_t, wd, name, deps=()):
    t, d = x.shape
    f = wd.shape[0]
    tm = min(t, 256)
    fc = _f_chunk(f)

    def body(x_ref, dxo_ref, g_ref, a_ref, b_ref, wg_ref, wu_ref, wd_ref,
             dx_ref, da_ref, db_ref, s_ref, h_ref, dg_ref):
        i = pl.program_id(0)
        xv = x_ref[...]
        dxo = dxo_ref[...]
        gain_v = g_ref[...]
        r = lax.rsqrt(jnp.mean(xv * xv, axis=-1, keepdims=True) + EPS)
        h_ref[...] = (xv * r * gain_v).astype(BF16)
        df = (FFN_RES * dxo).astype(BF16)
        dh = jnp.zeros((tm, d), F32)
        for c in range(f // fc):
            sl = slice(c * fc, (c + 1) * fc)
            av = a_ref[:, sl].astype(F32)
            bv = b_ref[:, sl].astype(F32)
            ds = _dg(df, wd_ref[sl, :], NT)
            sig = jax.nn.sigmoid(av)
            sil = av * sig
            s_ref[:, sl] = (sil * bv).astype(BF16)
            da = (ds * bv * (sig * (1.0 + av * (1.0 - sig)))).astype(BF16)
            db = (ds * sil).astype(BF16)
            da_ref[:, sl] = da
            db_ref[:, sl] = db
            dh = dh + _dot(da, wg_ref[sl, :]) + _dot(db, wu_ref[sl, :])
        dx, dgain = _rmsnorm_bwd_rows(xv, r, gain_v, dh)
        dx_ref[...] = dxo + dx

        @pl.when(i == 0)
        def _():
            dg_ref[...] = jnp.zeros_like(dg_ref)

        dg_ref[...] += dgain

    tok = lambda w: pl.BlockSpec((tm, w), lambda i: (i, 0))
    return pl.pallas_call(
        _drop_deps(body, 8, len(deps)), name=name, grid=(t // tm,),
        in_specs=[tok(d), tok(d), _resident((1, d)), tok(f), tok(f),
                  _resident((f, d)), _resident((f, d)), _resident((f, d))] + [ANY_SPEC] * len(deps),
        out_specs=[tok(d), tok(f), tok(f), tok(f), tok(d), pl.BlockSpec((1, d), lambda i: (0, 0))],
        out_shape=[jax.ShapeDtypeStruct((t, d), F32), jax.ShapeDtypeStruct((t, f), BF16),
                   jax.ShapeDtypeStruct((t, f), BF16), jax.ShapeDtypeStruct((t, f), BF16),
                   jax.ShapeDtypeStruct((t, d), BF16), jax.ShapeDtypeStruct((1, d), F32)],
        compiler_params=_params(("arbitrary",)),
    )(x, dxo, gain, a, b, wg_t, wu_t, wd, *deps)


def _row_tile(m, cap):
    if m <= cap:
        return m
    best = None
    for bm in range(LANES, cap + 1, LANES):
        if m % bm == 0:
            best = bm
    return best if best is not None else m


def grad_tn(a, b, scale, name):
    t, m = a.shape
    n = b.shape[1]
    bm = _row_tile(m, 1408)
    bk = min(t, 1024)
    nk = t // bk

    def body(a_ref, b_ref, o_ref, acc_ref):
        k = pl.program_id(1)

        @pl.when(k == 0)
        def _():
            acc_ref[...] = jnp.zeros_like(acc_ref)

        acc_ref[...] += _dg(a_ref[...].astype(BF16), b_ref[...].astype(BF16), TN)

        @pl.when(k == nk - 1)
        def _():
            o_ref[...] = (scale * acc_ref[...]).astype(BF16)

    return pl.pallas_call(
        body, name=name, grid=(m // bm, nk),
        in_specs=[pl.BlockSpec((bk, bm), lambda i, k: (k, i)), pl.BlockSpec((bk, n), lambda i, k: (k, 0))],
        out_specs=pl.BlockSpec((bm, n), lambda i, k: (i, 0)),
        out_shape=jax.ShapeDtypeStruct((m, n), BF16),
        scratch_shapes=[pltpu.VMEM((bm, n), F32)],
        compiler_params=_params(("arbitrary", "arbitrary")),
    )(a, b)


def grad_nn(a_list, b, name):
    t, n = b.shape
    ms = [a.shape[0] for a in a_list]
    m = sum(ms)
    bk = min(t, 1024)
    nk = t // bk
    na = len(a_list)

    def body(*refs):
        a_refs, b_ref, o_ref, acc_ref = refs[:na], refs[na], refs[na + 1], refs[na + 2]
        k = pl.program_id(0)

        @pl.when(k == 0)
        def _():
            acc_ref[...] = jnp.zeros_like(acc_ref)

        bv = b_ref[...].astype(BF16)
        off = 0
        for a_ref, mi in zip(a_refs, ms):
            acc_ref[off:off + mi, :] += _dot(a_ref[...].astype(BF16), bv)
            off += mi

        @pl.when(k == nk - 1)
        def _():
            o_ref[...] = acc_ref[...].astype(BF16)

    return pl.pallas_call(
        body, name=name, grid=(nk,),
        in_specs=[pl.BlockSpec((mi, bk), lambda k: (0, k)) for mi in ms] + [pl.BlockSpec((bk, n), lambda k: (k, 0))],
        out_specs=pl.BlockSpec((m, n), lambda k: (0, 0)),
        out_shape=jax.ShapeDtypeStruct((m, n), BF16),
        scratch_shapes=[pltpu.VMEM((m, n), F32)],
        compiler_params=_params(("arbitrary",)),
    )(*a_list, b)


def mix_in_fwd(x, gain, win_t, b_in_col, name):
    t, d = x.shape
    w = win_t.shape[0]
    tm = _token_tile(t)

    def body(x_ref, g_ref, w_ref, bias_ref, qkv_ref, zg_ref):
        xv = x_ref[...]
        r = lax.rsqrt(jnp.mean(xv * xv, axis=-1, keepdims=True) + EPS)
        h = (xv * r * g_ref[...]).astype(BF16)
        qkv_ref[...] = (_dg(w_ref[:QKV_WIDTH, :], h, NT) + bias_ref[:QKV_WIDTH, :]).astype(BF16)
        zg_ref[...] = (_dg(w_ref[QKV_WIDTH:, :], h, NT) + bias_ref[QKV_WIDTH:, :]).astype(BF16)

    return pl.pallas_call(
        body, name=name, grid=(t // tm,),
        in_specs=[pl.BlockSpec((tm, d), lambda i: (i, 0)), _resident((1, d)), _resident((w, d)), _resident((w, 1))],
        out_specs=[pl.BlockSpec((QKV_WIDTH, tm), lambda i: (0, i)), pl.BlockSpec((w - QKV_WIDTH, tm), lambda i: (0, i))],
        out_shape=[jax.ShapeDtypeStruct((QKV_WIDTH, t), BF16), jax.ShapeDtypeStruct((w - QKV_WIDTH, t), BF16)],
        compiler_params=_params(("arbitrary",)),
    )(x, gain, win_t, b_in_col)


def _attn_specs(nb):
    last = nb - 1
    cur = lambda n: jnp.minimum(n, last)
    prev = lambda n: jnp.maximum(jnp.minimum(n, last) - 1, 0)
    k_row = ATTN_WIDTH // KV_WIDTH
    return [
        pl.BlockSpec((ATTN_WIDTH, BLOCK), lambda n: (0, cur(n))),
        pl.BlockSpec((KV_WIDTH, BLOCK), lambda n: (k_row, prev(n))),
        pl.BlockSpec((KV_WIDTH, BLOCK), lambda n: (k_row, cur(n))),
        pl.BlockSpec((KV_WIDTH, BLOCK), lambda n: (k_row + 1, prev(n))),
        pl.BlockSpec((KV_WIDTH, BLOCK), lambda n: (k_row + 1, cur(n))),
    ]


def _attn_probs(q, kp, kc, sink, mask_prev, mask_cur):
    sp = jnp.where(mask_prev, _dg(kp, q, TN) * ATTN_SCALE, MASK_VALUE)
    sc = jnp.where(mask_cur, _dg(kc, q, TN) * ATTN_SCALE, MASK_VALUE)
    m = jnp.maximum(jnp.maximum(jnp.max(sp, axis=0, keepdims=True), jnp.max(sc, axis=0, keepdims=True)), sink)
    pp = jnp.exp(sp - m)
    pc = jnp.exp(sc - m)
    es = jnp.exp(sink - m)
    inv = 1.0 / (jnp.sum(pp, axis=0, keepdims=True) + jnp.sum(pc, axis=0, keepdims=True) + es)
    return pp * inv, pc * inv, es * inv


def _attn_masks(n):
    key = lax.broadcasted_iota(jnp.int32, (BLOCK, BLOCK), 0)
    qry = lax.broadcasted_iota(jnp.int32, (BLOCK, BLOCK), 1)
    return (key > qry) & (n > 0), key <= qry


def attn_fwd(qkv_t, sinks, name):
    t = qkv_t.shape[1]
    nb = t // BLOCK
    rep = N_Q_HEADS // N_KV_HEADS

    def body(sink_ref, q_ref, kp_ref, kc_ref, vp_ref, vc_ref, y_ref):
        n = pl.program_id(0)
        mask_prev, mask_cur = _attn_masks(n)
        for g in range(N_KV_HEADS):
            gs = slice(g * HEAD_DIM, (g + 1) * HEAD_DIM)
            kp, kc, vp, vc = kp_ref[gs, :], kc_ref[gs, :], vp_ref[gs, :], vc_ref[gs, :]
            for rr in range(rep):
                hd = g * rep + rr
                hs = slice(hd * HEAD_DIM, (hd + 1) * HEAD_DIM)
                pp, pc, _ = _attn_probs(q_ref[hs, :], kp, kc, sink_ref[0, hd], mask_prev, mask_cur)
                y_ref[hs, :] = (_dot(vp, pp.astype(BF16)) + _dot(vc, pc.astype(BF16))).astype(BF16)

    return pl.pallas_call(
        body, name=name, grid=(nb,),
        in_specs=[pl.BlockSpec(memory_space=pltpu.SMEM)] + _attn_specs(nb),
        out_specs=pl.BlockSpec((ATTN_WIDTH, BLOCK), lambda n: (0, n)),
        out_shape=jax.ShapeDtypeStruct((ATTN_WIDTH, t), BF16),
        compiler_params=_params(("arbitrary",)),
    )(sinks, qkv_t, qkv_t, qkv_t, qkv_t, qkv_t)


def attn_bwd(qkv_t, dy_t, sinks, name):
    t = qkv_t.shape[1]
    nb = t // BLOCK
    rep = N_Q_HEADS // N_KV_HEADS

    def body(sink_ref, q_ref, kp_ref, kc_ref, vp_ref, vc_ref, do_ref, dq_ref, dk_ref, dv_ref, dsink_ref,
             ck_ref, cv_ref, sacc_ref):
        n = pl.program_id(0)

        @pl.when(n == 0)
        def _():
            sacc_ref[...] = jnp.zeros_like(sacc_ref)

        @pl.when(n < nb)
        def _():
            mask_prev, mask_cur = _attn_masks(n)
            for g in range(N_KV_HEADS):
                gs = slice(g * HEAD_DIM, (g + 1) * HEAD_DIM)
                kp, kc, vp, vc = kp_ref[gs, :], kc_ref[gs, :], vp_ref[gs, :], vc_ref[gs, :]
                dkp = jnp.zeros((HEAD_DIM, BLOCK), F32)
                dkc = jnp.zeros((HEAD_DIM, BLOCK), F32)
                dvp = jnp.zeros((HEAD_DIM, BLOCK), F32)
                dvc = jnp.zeros((HEAD_DIM, BLOCK), F32)
                for rr in range(rep):
                    hd = g * rep + rr
                    hs = slice(hd * HEAD_DIM, (hd + 1) * HEAD_DIM)
                    q = q_ref[hs, :]
                    do = do_ref[hs, :]
                    pp, pc, ps = _attn_probs(q, kp, kc, sink_ref[0, hd], mask_prev, mask_cur)
                    dpp = _dg(vp, do, TN)
                    dpc = _dg(vc, do, TN)
                    delta = jnp.sum(pp * dpp, axis=0, keepdims=True) + jnp.sum(pc * dpc, axis=0, keepdims=True)
                    dsp = (pp * (dpp - delta)).astype(BF16)
                    dsc = (pc * (dpc - delta)).astype(BF16)
                    sacc_ref[hd:hd + 1, :] += -(ps * delta)
                    dq_ref[hs, :] = (ATTN_SCALE * (_dot(kp, dsp) + _dot(kc, dsc))).astype(BF16)
                    dkp = dkp + _dg(q, dsp, NT)
                    dkc = dkc + _dg(q, dsc, NT)
                    dvp = dvp + _dg(do, pp.astype(BF16), NT)
                    dvc = dvc + _dg(do, pc.astype(BF16), NT)

                @pl.when(n > 0)
                def _():
                    dk_ref[gs, :] = (ck_ref[gs, :] + ATTN_SCALE * dkp).astype(BF16)
                    dv_ref[gs, :] = (cv_ref[gs, :] + dvp).astype(BF16)

                ck_ref[gs, :] = ATTN_SCALE * dkc
                cv_ref[gs, :] = dvc

        @pl.when(n == nb)
        def _():
            dk_ref[...] = ck_ref[...].astype(BF16)
            dv_ref[...] = cv_ref[...].astype(BF16)
            dsink_ref[...] = jnp.sum(sacc_ref[...], axis=1, keepdims=True)

    last = nb - 1
    done = lambda n: jnp.maximum(n - 1, 0)
    outs = pl.pallas_call(
        body, name=name, grid=(nb + 1,),
        in_specs=[pl.BlockSpec(memory_space=pltpu.SMEM)] + _attn_specs(nb)
        + [pl.BlockSpec((ATTN_WIDTH, BLOCK), lambda n: (0, jnp.minimum(n, last)))],
        out_specs=[pl.BlockSpec((ATTN_WIDTH, BLOCK), lambda n: (0, jnp.minimum(n, last))),
                   pl.BlockSpec((KV_WIDTH, BLOCK), lambda n: (0, done(n))),
                   pl.BlockSpec((KV_WIDTH, BLOCK), lambda n: (0, done(n))),
                   pl.BlockSpec((N_Q_HEADS, 1), lambda n: (0, 0))],
        out_shape=[jax.ShapeDtypeStruct((ATTN_WIDTH, t), BF16), jax.ShapeDtypeStruct((KV_WIDTH, t), BF16),
                   jax.ShapeDtypeStruct((KV_WIDTH, t), BF16), jax.ShapeDtypeStruct((N_Q_HEADS, 1), F32)],
        scratch_shapes=[pltpu.VMEM((KV_WIDTH, BLOCK), F32), pltpu.VMEM((KV_WIDTH, BLOCK), F32),
                        pltpu.VMEM((N_Q_HEADS, BLOCK), F32)],
        compiler_params=_params(("arbitrary",)),
    )(sinks, qkv_t, qkv_t, qkv_t, qkv_t, qkv_t, dy_t)
    return outs


GELU_C = 0.7978845608028654
GELU_A = 0.044715


def _gelu(x):
    return 0.5 * x * (1.0 + jnp.tanh(GELU_C * (x + GELU_A * x * x * x)))


def _gelu_grad(x):
    th = jnp.tanh(GELU_C * (x + GELU_A * x * x * x))
    return 0.5 * (1.0 + th) + 0.5 * x * (1.0 - th * th) * GELU_C * (1.0 + 3.0 * GELU_A * x * x)


def _gmlp_parts(zg, lng, lnb):
    z = _gelu(zg)
    u = z[:GMLP_WIDTH, :]
    vv = z[GMLP_WIDTH:, :]
    mu = jnp.mean(vv, axis=0, keepdims=True)
    xc = vv - mu
    rstd = lax.rsqrt(jnp.mean(xc * xc, axis=0, keepdims=True) + EPS)
    xhat = xc * rstd
    return u, xhat, rstd, xhat * lng + lnb


def _causal(w):
    row = lax.broadcasted_iota(jnp.int32, (BLOCK, BLOCK), 0)
    col = lax.broadcasted_iota(jnp.int32, (BLOCK, BLOCK), 1)
    return jnp.where(col <= row, w, 0.0)


def gmlp_fwd(zg_t, lng, lnb, w_s, b_s, name):
    t = zg_t.shape[1]
    nb = t // BLOCK

    def body(zg_ref, lng_ref, lnb_ref, w_ref, bs_ref, y_ref):
        u, _, _, vn = _gmlp_parts(zg_ref[...].astype(F32), lng_ref[...], lnb_ref[...])
        for g in range(GMLP_GROUPS):
            gs = slice(g * GMLP_GROUP_DIM, (g + 1) * GMLP_GROUP_DIM)
            wc = _causal(w_ref[g]).astype(BF16)
            mixed = _dg(vn[gs, :].astype(BF16), wc, NT) + bs_ref[g:g + 1, :]
            y_ref[gs, :] = (u[gs, :] * mixed).astype(BF16)

    return pl.pallas_call(
        body, name=name, grid=(nb,),
        in_specs=[pl.BlockSpec((2 * GMLP_WIDTH, BLOCK), lambda n: (0, n)), _resident((GMLP_WIDTH, 1)),
                  _resident((GMLP_WIDTH, 1)), _resident((GMLP_GROUPS, BLOCK, BLOCK)), _resident((GMLP_GROUPS, BLOCK))],
        out_specs=pl.BlockSpec((GMLP_WIDTH, BLOCK), lambda n: (0, n)),
        out_shape=jax.ShapeDtypeStruct((GMLP_WIDTH, t), BF16),
        compiler_params=_params(("arbitrary",)),
    )(zg_t, lng, lnb, w_s, b_s)


def gmlp_bwd(zg_t, dy_t, lng, lnb, w_s, b_s, name):
    t = zg_t.shape[1]
    nb = t // BLOCK

    def body(zg_ref, dy_ref, lng_ref, lnb_ref, w_ref, bs_ref, dzg_ref, dw_ref, dbs_ref, dlng_ref, dlnb_ref,
             gacc_ref, bacc_ref):
        n = pl.program_id(0)

        @pl.when(n == 0)
        def _():
            dw_ref[...] = jnp.zeros_like(dw_ref)
            dbs_ref[...] = jnp.zeros_like(dbs_ref)
            gacc_ref[...] = jnp.zeros_like(gacc_ref)
            bacc_ref[...] = jnp.zeros_like(bacc_ref)

        zg = zg_ref[...].astype(F32)
        lng_v = lng_ref[...]
        u, xhat, rstd, vn = _gmlp_parts(zg, lng_v, lnb_ref[...])
        dy = dy_ref[...].astype(F32)
        dvn_parts = []
        for g in range(GMLP_GROUPS):
            gs = slice(g * GMLP_GROUP_DIM, (g + 1) * GMLP_GROUP_DIM)
            wc = _causal(w_ref[g]).astype(BF16)
            vn_g = vn[gs, :].astype(BF16)
            mixed = _dg(vn_g, wc, NT) + bs_ref[g:g + 1, :]
            dy_g = dy[gs, :]
            dmixed = dy_g * u[gs, :]
            dmixed_b = dmixed.astype(BF16)
            dzg_ref[gs, :] = (dy_g * mixed * _gelu_grad(zg[gs, :])).astype(BF16)
            dw_ref[g] += _causal(_dg(dmixed_b, vn_g, TN))
            dbs_ref[g:g + 1, :] += jnp.sum(dmixed, axis=0, keepdims=True)
            dvn_parts.append(_dot(dmixed_b, wc))
        dvn = jnp.concatenate(dvn_parts, axis=0)
        gacc_ref[...] += dvn * xhat
        bacc_ref[...] += dvn
        dxhat = dvn * lng_v
        m1 = jnp.mean(dxhat, axis=0, keepdims=True)
        m2 = jnp.mean(dxhat * xhat, axis=0, keepdims=True)
        dvv = rstd * (dxhat - m1 - xhat * m2)
        dzg_ref[GMLP_WIDTH:, :] = (dvv * _gelu_grad(zg[GMLP_WIDTH:, :])).astype(BF16)

        @pl.when(n == nb - 1)
        def _():
            dlng_ref[...] = jnp.sum(gacc_ref[...], axis=1, keepdims=True)
            dlnb_ref[...] = jnp.sum(bacc_ref[...], axis=1, keepdims=True)

    const2 = lambda n: (0, 0)
    return pl.pallas_call(
        body, name=name, grid=(nb,),
        in_specs=[pl.BlockSpec((2 * GMLP_WIDTH, BLOCK), lambda n: (0, n)), pl.BlockSpec((GMLP_WIDTH, BLOCK), lambda n: (0, n)),
                  _resident((GMLP_WIDTH, 1)), _resident((GMLP_WIDTH, 1)),
                  _resident((GMLP_GROUPS, BLOCK, BLOCK)), _resident((GMLP_GROUPS, BLOCK))],
        out_specs=[pl.BlockSpec((2 * GMLP_WIDTH, BLOCK), lambda n: (0, n)),
                   pl.BlockSpec((GMLP_GROUPS, BLOCK, BLOCK), lambda n: (0, 0, 0)),
                   pl.BlockSpec((GMLP_GROUPS, BLOCK), const2),
                   pl.BlockSpec((GMLP_WIDTH, 1), const2), pl.BlockSpec((GMLP_WIDTH, 1), const2)],
        out_shape=[jax.ShapeDtypeStruct((2 * GMLP_WIDTH, t), BF16),
                   jax.ShapeDtypeStruct((GMLP_GROUPS, BLOCK, BLOCK), F32),
                   jax.ShapeDtypeStruct((GMLP_GROUPS, BLOCK), F32),
                   jax.ShapeDtypeStruct((GMLP_WIDTH, 1), F32), jax.ShapeDtypeStruct((GMLP_WIDTH, 1), F32)],
        scratch_shapes=[pltpu.VMEM((GMLP_WIDTH, BLOCK), F32), pltpu.VMEM((GMLP_WIDTH, BLOCK), F32)],
        compiler_params=_params(("arbitrary",)),
    )(zg_t, dy_t, lng, lnb, w_s, b_s)


def _rmsnorm_cols(y, gain_col):
    r = lax.rsqrt(jnp.mean(y * y, axis=0, keepdims=True) + EPS)
    n = y * r
    return n, r, n * gain_col


def mix_out_fwd(x, ya_t, yg_t, gao, ggo, wout, b_out, name):
    t, d = x.shape
    tm = _token_tile(t)

    def body(x_ref, ya_ref, yg_ref, gao_ref, ggo_ref, w_ref, b_ref, o_ref):
        _, _, ya = _rmsnorm_cols(ya_ref[...].astype(F32), gao_ref[...])
        _, _, yg = _rmsnorm_cols(yg_ref[...].astype(F32), ggo_ref[...])
        o_ref[...] = (x_ref[...] + _dg(ya.astype(BF16), w_ref[:ATTN_WIDTH, :], TN)
                      + _dg(yg.astype(BF16), w_ref[ATTN_WIDTH:, :], TN) + b_ref[...])

    return pl.pallas_call(
        body, name=name, grid=(t // tm,),
        in_specs=[pl.BlockSpec((tm, d), lambda i: (i, 0)), pl.BlockSpec((ATTN_WIDTH, tm), lambda i: (0, i)),
                  pl.BlockSpec((GMLP_WIDTH, tm), lambda i: (0, i)), _resident((ATTN_WIDTH, 1)), _resident((GMLP_WIDTH, 1)),
                  _resident((ATTN_WIDTH + GMLP_WIDTH, d)), _resident((1, d))],
        out_specs=pl.BlockSpec((tm, d), lambda i: (i, 0)),
        out_shape=jax.ShapeDtypeStruct((t, d), F32),
        compiler_params=_params(("arbitrary",)),
    )(x, ya_t, yg_t, gao, ggo, wout, b_out)


def _lane_fold(v):
    out = v[:, :LANES]
    for j in range(1, v.shape[1] // LANES):
        out = out + v[:, j * LANES:(j + 1) * LANES]
    return out


def mix_out_bwd(dx, ya_t, yg_t, gao, ggo, wout, name, deps=()):
    t, d = dx.shape
    tm = _token_tile(t)
    nt = t // tm

    def body(dx_ref, ya_ref, yg_ref, gao_ref, ggo_ref, w_ref, dya_ref, dyg_ref, y_ref, dgao_ref, dggo_ref, db_ref,
             acc_ref):
        i = pl.program_id(0)

        @pl.when(i == 0)
        def _():
            acc_ref[...] = jnp.zeros_like(acc_ref)
            db_ref[...] = jnp.zeros_like(db_ref)

        dxv = dx_ref[...]
        db_ref[...] += jnp.sum(dxv, axis=0, keepdims=True)
        dxb = dxv.astype(BF16)
        for part, (src_ref, gain_ref, dst_ref) in enumerate(((ya_ref, gao_ref, dya_ref), (yg_ref, ggo_ref, dyg_ref))):
            rows = slice(part * ATTN_WIDTH, (part + 1) * ATTN_WIDTH)
            gain = gain_ref[...]
            nrm, r, yn = _rmsnorm_cols(src_ref[...].astype(F32), gain)
            y_ref[rows, :] = yn.astype(BF16)
            dy = _dg(w_ref[rows, :], dxb, NT)
            acc_ref[rows, :] += _lane_fold(dy * nrm)
            dn = dy * gain
            dst_ref[...] = (r * (dn - nrm * jnp.mean(dn * nrm, axis=0, keepdims=True))).astype(BF16)

        @pl.when(i == nt - 1)
        def _():
            dgao_ref[...] = jnp.sum(acc_ref[:ATTN_WIDTH, :], axis=1, keepdims=True)
            dggo_ref[...] = jnp.sum(acc_ref[ATTN_WIDTH:, :], axis=1, keepdims=True)

    const2 = lambda i: (0, 0)
    feat = lambda w: pl.BlockSpec((w, tm), lambda i: (0, i))
    return pl.pallas_call(
        _drop_deps(body, 6, len(deps)), name=name, grid=(nt,),
        in_specs=[pl.BlockSpec((tm, d), lambda i: (i, 0)), feat(ATTN_WIDTH), feat(GMLP_WIDTH),
                  _resident((ATTN_WIDTH, 1)), _resident((GMLP_WIDTH, 1)), _resident((ATTN_WIDTH + GMLP_WIDTH, d))]
        + [ANY_SPEC] * len(deps),
        out_specs=[feat(ATTN_WIDTH), feat(GMLP_WIDTH), feat(ATTN_WIDTH + GMLP_WIDTH),
                   pl.BlockSpec((ATTN_WIDTH, 1), const2), pl.BlockSpec((GMLP_WIDTH, 1), const2),
                   pl.BlockSpec((1, d), const2)],
        out_shape=[jax.ShapeDtypeStruct((ATTN_WIDTH, t), BF16), jax.ShapeDtypeStruct((GMLP_WIDTH, t), BF16),
                   jax.ShapeDtypeStruct((ATTN_WIDTH + GMLP_WIDTH, t), BF16),
                   jax.ShapeDtypeStruct((ATTN_WIDTH, 1), F32), jax.ShapeDtypeStruct((GMLP_WIDTH, 1), F32),
                   jax.ShapeDtypeStruct((1, d), F32)],
        scratch_shapes=[pltpu.VMEM((ATTN_WIDTH + GMLP_WIDTH, LANES), F32)],
        compiler_params=_params(("arbitrary",)),
    )(dx, ya_t, yg_t, gao, ggo, wout, *deps)


def mix_in_bwd(x, dxo, gain, dq_t, dk_t, dv_t, dzg_t, win_t, name):
    t, d = x.shape
    w = win_t.shape[0]
    tm = _token_tile(t)
    nt = t // tm
    parts = ((0, ATTN_WIDTH), (ATTN_WIDTH, KV_WIDTH), (ATTN_WIDTH + KV_WIDTH, KV_WIDTH), (QKV_WIDTH, w - QKV_WIDTH))

    def body(x_ref, dxo_ref, g_ref, dq_ref, dk_ref, dv_ref, dzg_ref, w_ref, dx_ref, h_ref, dg_ref, db_ref, acc_ref):
        i = pl.program_id(0)

        @pl.when(i == 0)
        def _():
            acc_ref[...] = jnp.zeros_like(acc_ref)
            dg_ref[...] = jnp.zeros_like(dg_ref)

        xv = x_ref[...]
        gain_v = g_ref[...]
        r = lax.rsqrt(jnp.mean(xv * xv, axis=-1, keepdims=True) + EPS)
        h_ref[...] = (xv * r * gain_v).astype(BF16)
        dh = jnp.zeros((tm, d), F32)
        for (off, size), src_ref in zip(parts, (dq_ref, dk_ref, dv_ref, dzg_ref)):
            dp = src_ref[...]
            acc_ref[off:off + size, :] += _lane_fold(dp.astype(F32))
            dh = dh + _dg(dp, w_ref[off:off + size, :], TN)
        dx, dgain = _rmsnorm_bwd_rows(xv, r, gain_v, dh)
        dx_ref[...] = dxo_ref[...] + dx
        dg_ref[...] += dgain

        @pl.when(i == nt - 1)
        def _():
            db_ref[...] = jnp.sum(acc_ref[...], axis=1, keepdims=True)

    const2 = lambda i: (0, 0)
    tok = lambda: pl.BlockSpec((tm, d), lambda i: (i, 0))
    feat = lambda rows: pl.BlockSpec((rows, tm), lambda i: (0, i))
    return pl.pallas_call(
        body, name=name, grid=(nt,),
        in_specs=[tok(), tok(), _resident((1, d)), feat(ATTN_WIDTH), feat(KV_WIDTH), feat(KV_WIDTH),
                  feat(w - QKV_WIDTH), _resident((w, d))],
        out_specs=[tok(), tok(), pl.BlockSpec((1, d), const2), pl.BlockSpec((w, 1), const2)],
        out_shape=[jax.ShapeDtypeStruct((t, d), F32), jax.ShapeDtypeStruct((t, d), BF16),
                   jax.ShapeDtypeStruct((1, d), F32), jax.ShapeDtypeStruct((w, 1), F32)],
        scratch_shapes=[pltpu.VMEM((w, LANES), F32)],
        compiler_params=_params(("arbitrary",)),
    )(x, dxo, gain, dq_t, dk_t, dv_t, dzg_t, win_t)


def final_bwd(x, target, gain, name):
    t, d = x.shape
    tm = _token_tile(t)

    def body(x_ref, t_ref, g_ref, dx_ref, dg_ref, loss_ref):
        i = pl.program_id(0)

        @pl.when(i == 0)
        def _():
            dg_ref[...] = jnp.zeros_like(dg_ref)
            loss_ref[...] = jnp.zeros_like(loss_ref)

        xv = x_ref[...]
        gain_v = g_ref[...]
        r = lax.rsqrt(jnp.mean(xv * xv, axis=-1, keepdims=True) + EPS)
        err = xv * r * gain_v - t_ref[...]
        row_loss = jnp.mean(err * err, axis=-1, keepdims=True)
        loss_ref[...] += 0.5 * jnp.sum(row_loss, axis=0, keepdims=True)
        dx, dgain = _rmsnorm_bwd_rows(xv, r, gain_v, err * (1.0 / d))
        dx_ref[...] = dx
        dg_ref[...] += dgain

    const2 = lambda i: (0, 0)
    return pl.pallas_call(
        body, name=name, grid=(t // tm,),
        in_specs=[pl.BlockSpec((tm, d), lambda i: (i, 0)), pl.BlockSpec((tm, d), lambda i: (i, 0)), _resident((1, d))],
        out_specs=[pl.BlockSpec((tm, d), lambda i: (i, 0)), pl.BlockSpec((1, d), const2), pl.BlockSpec((1, LANES), const2)],
        out_shape=[jax.ShapeDtypeStruct((t, d), F32), jax.ShapeDtypeStruct((1, d), F32),
                   jax.ShapeDtypeStruct((1, LANES), F32)],
        compiler_params=_params(("arbitrary",)),
    )(x, target, gain)


def _adamw_math(w, g, m, v):
    m = ADAM_B1 * m + (1.0 - ADAM_B1) * g
    v = ADAM_B2 * v + (1.0 - ADAM_B2) * (g * g)
    m_hat = m / (1.0 - ADAM_B1 ** ADAM_STEP)
    v_hat = v / (1.0 - ADAM_B2 ** ADAM_STEP)
    delta = -ADAM_LR * (m_hat / (jnp.sqrt(v_hat) + ADAM_EPS) + ADAM_WD * w)
    return delta, m, v


def adamw(w, g, m, v, name):
    rows, cols = w.shape
    tr = rows
    if rows * cols > 256 * 1024:
        for cand in (256, 128, 64, 32, 16, 8):
            if rows % cand == 0:
                tr = cand
                break

    def body(w_ref, g_ref, m_ref, v_ref, d_ref, mo_ref, vo_ref):
        delta, mn, vn = _adamw_math(w_ref[...], g_ref[...], m_ref[...], v_ref[...])
        d_ref[...] = delta
        mo_ref[...] = mn
        vo_ref[...] = vn

    spec = pl.BlockSpec((tr, cols), lambda i: (i, 0))
    shape = jax.ShapeDtypeStruct((rows, cols), F32)
    return pl.pallas_call(
        body, name=name, grid=(rows // tr,),
        in_specs=[spec] * 4, out_specs=[spec] * 3, out_shape=[shape] * 3,
        compiler_params=_params(("arbitrary",)),
    )(w, g, m, v)


def sum_partials(landing, grad3d, me, name, deps=()):
    nd, rows, cols = landing.shape
    tr = rows
    for cand in (128, 112, 88, 64, 32, 16):
        if rows % cand == 0:
            tr = cand
            break

    def body(me_ref, l_ref, own_ref, o_ref):
        acc = own_ref[...].astype(F32)
        for j in range(nd):
            acc = acc + l_ref[j].astype(F32)
        o_ref[...] = acc

    return pl.pallas_call(
        _drop_deps(body, 3, len(deps)), name=name,
        grid_spec=pltpu.PrefetchScalarGridSpec(
            num_scalar_prefetch=1, grid=(rows // tr,),
            in_specs=[pl.BlockSpec((nd, tr, cols), lambda i, me_ref: (0, i, 0)),
                      pl.BlockSpec((None, tr, cols), lambda i, me_ref: (me_ref[0], i, 0))] + [ANY_SPEC] * len(deps),
            out_specs=pl.BlockSpec((tr, cols), lambda i, me_ref: (i, 0))),
        out_shape=jax.ShapeDtypeStruct((rows, cols), F32),
        compiler_params=_params(("arbitrary",)),
    )(me, landing, grad3d, *deps)


def _mesh_place():
    x, y, c = lax.axis_index("x"), lax.axis_index("y"), lax.axis_index("c")
    return x, y, c, 4 * x + 2 * y + c


def _peer(x, y, c, k):
    return (x ^ ((k >> 2) & 1), y ^ ((k >> 1) & 1), c ^ (k & 1))


def allgather_rows(shards, n_gather, name):
    na = len(shards)

    def body(*refs):
        in_refs, out_refs = refs[:na], refs[na:2 * na]
        send_sems, recv_sems, local_sems = refs[2 * na:]
        x, y, c, me = _mesh_place()
        local = [pltpu.make_async_copy(in_refs[a], out_refs[a].at[me], local_sems.at[a]) for a in range(na)]
        for cp in local:
            cp.start()
        sends = []
        for k in range(1, N_DEV):
            for a in range(n_gather):
                cp = pltpu.make_async_remote_copy(
                    src_ref=in_refs[a], dst_ref=out_refs[a].at[me], send_sem=send_sems.at[k - 1, a],
                    recv_sem=recv_sems.at[k - 1, a], device_id=_peer(x, y, c, k), device_id_type=pl.DeviceIdType.MESH)
                cp.start()
                sends.append(cp)
        for k in range(1, N_DEV):
            for a in range(n_gather):
                pltpu.make_async_remote_copy(
                    src_ref=in_refs[a], dst_ref=out_refs[a].at[me ^ k], send_sem=send_sems.at[k - 1, a],
                    recv_sem=recv_sems.at[k - 1, a], device_id=_peer(x, y, c, k),
                    device_id_type=pl.DeviceIdType.MESH).wait_recv()
        for cp in sends:
            cp.wait_send()
        for cp in local:
            cp.wait()

    return pl.pallas_call(
        body, name=name,
        in_specs=[ANY_SPEC] * na, out_specs=[ANY_SPEC] * na,
        out_shape=[jax.ShapeDtypeStruct((N_DEV,) + s.shape, s.dtype) for s in shards],
        scratch_shapes=[pltpu.SemaphoreType.DMA((N_DEV - 1, n_gather)), pltpu.SemaphoreType.DMA((N_DEV - 1, n_gather)),
                        pltpu.SemaphoreType.DMA((na,))],
    )(*shards)


HBM_SPEC = pl.BlockSpec(memory_space=pltpu.HBM)
SEM_SPEC = pl.BlockSpec(memory_space=pltpu.SEMAPHORE)
SIDE_EFFECT = pltpu.SideEffectType.DATAFLOW_SIDE_EFFECTING


def _hbm(v):
    return pltpu.with_memory_space_constraint(v, pltpu.HBM)


def gather_slices(src, land, me, k):
    return src, land.at[me], land.at[me ^ k]


def exchange_slices(src, land, me, k):
    return src.at[me ^ k], land.at[k - 1], land.at[k - 1]


def split_start(groups, slices, deps, name):
    sizes = [len(srcs) for srcs, _ in groups]
    flat_src = [s for srcs, _ in groups for s in srcs]
    flat_land = [l for _, lands in groups for l in lands]
    na, ng, nd = len(flat_src), len(groups), len(deps)

    def body(*refs):
        src_refs, land_refs = refs[:na], refs[na:2 * na]
        sem_refs = refs[2 * na + nd:2 * na + nd + 2 * ng]
        token_ref = refs[-1]
        x, y, c, me = _mesh_place()
        a0 = 0
        for gi, size in enumerate(sizes):
            send_sems, recv_sems = sem_refs[2 * gi], sem_refs[2 * gi + 1]
            for k in range(1, N_DEV):
                for j in range(size):
                    src, dst, _ = slices(src_refs[a0 + j], land_refs[a0 + j], me, k)
                    pltpu.make_async_remote_copy(
                        src_ref=src, dst_ref=dst, send_sem=send_sems.at[(k - 1) * size + j],
                        recv_sem=recv_sems.at[(k - 1) * size + j],
                        device_id=_peer(x, y, c, k), device_id_type=pl.DeviceIdType.MESH).start()
            a0 += size
        token_ref[...] = jnp.zeros_like(token_ref)

    sems = [pltpu.SemaphoreType.DMA(((N_DEV - 1) * size,)) for size in sizes for _ in range(2)]
    thru = [pltpu.HBM(v.shape, v.dtype) for v in flat_src + flat_land]
    outs = pl.pallas_call(
        body, name=name,
        in_specs=[HBM_SPEC] * (2 * na) + [ANY_SPEC] * nd,
        out_specs=[SEM_SPEC] * (2 * ng) + [HBM_SPEC] * (2 * na) + [pl.BlockSpec(memory_space=pltpu.VMEM)],
        out_shape=sems + thru + [jax.ShapeDtypeStruct((8, LANES), F32)],
        input_output_aliases={i: 2 * ng + i for i in range(2 * na)},
        compiler_params=pltpu.CompilerParams(has_side_effects=SIDE_EFFECT),
    )(*[_hbm(v) for v in flat_src + flat_land], *deps)
    started, a0 = [], 0
    for gi, size in enumerate(sizes):
        srcs = outs[2 * ng + a0:2 * ng + a0 + size]
        lands = outs[2 * ng + na + a0:2 * ng + na + a0 + size]
        started.append((outs[2 * gi], outs[2 * gi + 1], list(srcs), list(lands)))
        a0 += size
    return started, outs[-1]


def split_wait(started, slices, after, name):
    send_sems, recv_sems, srcs, lands = started
    na = len(srcs)

    def body(*refs):
        src_refs, land_refs = refs[:na], refs[na:2 * na]
        send_ref, recv_ref = refs[2 * na], refs[2 * na + 1]
        x, y, c, me = _mesh_place()
        for k in range(1, N_DEV):
            for j in range(na):
                src, _, got = slices(src_refs[j], land_refs[j], me, k)
                cp = pltpu.make_async_remote_copy(
                    src_ref=src, dst_ref=got, send_sem=send_ref.at[(k - 1) * na + j], recv_sem=recv_ref.at[(k - 1) * na + j],
                    device_id=_peer(x, y, c, k), device_id_type=pl.DeviceIdType.MESH)
                cp.wait_send()
                cp.wait_recv()

    outs = pl.pallas_call(
        body, name=name,
        in_specs=[HBM_SPEC] * (2 * na) + [SEM_SPEC, SEM_SPEC] + [ANY_SPEC] * len(after),
        out_specs=[HBM_SPEC] * (2 * na),
        out_shape=[pltpu.HBM(v.shape, v.dtype) for v in srcs + lands],
        input_output_aliases={i: i for i in range(2 * na)},
        compiler_params=pltpu.CompilerParams(has_side_effects=SIDE_EFFECT),
    )(*srcs, *lands, send_sems, recv_sems, *after)
    return list(outs[:na]), list(outs[na:])


def allreduce_small(pack, name):
    rows, cols = pack.shape

    def body(p_ref, o_ref, land_ref, send_sems, recv_sems):
        x, y, c, me = _mesh_place()
        sends = []
        for k in range(1, N_DEV):
            cp = pltpu.make_async_remote_copy(
                src_ref=p_ref, dst_ref=land_ref.at[me], send_sem=send_sems.at[k - 1], recv_sem=recv_sems.at[k - 1],
                device_id=_peer(x, y, c, k), device_id_type=pl.DeviceIdType.MESH)
            cp.start()
            sends.append(cp)
        land_ref[me] = p_ref[...]
        for k in range(1, N_DEV):
            pltpu.make_async_remote_copy(
                src_ref=p_ref, dst_ref=land_ref.at[me ^ k], send_sem=send_sems.at[k - 1], recv_sem=recv_sems.at[k - 1],
                device_id=_peer(x, y, c, k), device_id_type=pl.DeviceIdType.MESH).wait_recv()
        for cp in sends:
            cp.wait_send()
        acc = land_ref[0]
        for j in range(1, N_DEV):
            acc = acc + land_ref[j]
        o_ref[...] = acc

    vmem = pl.BlockSpec(memory_space=pltpu.VMEM)
    return pl.pallas_call(
        body, name=name,
        in_specs=[vmem], out_specs=vmem,
        out_shape=jax.ShapeDtypeStruct((rows, cols), F32),
        scratch_shapes=[pltpu.VMEM((N_DEV, rows, cols), F32), pltpu.SemaphoreType.DMA((N_DEV - 1,)),
                        pltpu.SemaphoreType.DMA((N_DEV - 1,))],
        compiler_params=pltpu.CompilerParams(vmem_limit_bytes=VMEM_LIMIT),
    )(pack)


def _pack_rows(size):
    rows = -(-size // LANES)
    return -(-rows // 8) * 8


def pack_small(parts):
    out = []
    for p in parts:
        flat = p.reshape(-1).astype(F32)
        rows = _pack_rows(flat.shape[0])
        out.append(jnp.pad(flat, (0, rows * LANES - flat.shape[0])).reshape(rows, LANES))
    return jnp.concatenate(out, axis=0)


def unpack_small(pack, shapes):
    out, off = [], 0
    for shp in shapes:
        size = 1
        for s in shp:
            size *= s
        rows = _pack_rows(size)
        out.append(pack[off:off + rows].reshape(-1)[:size].reshape(shp))
        off += rows
    return out


def local_step(x, target, small, get_w, put_g):
    col = lambda v: v.reshape(-1, 1)
    (wg1, wu1, wd1), deps = get_w(("wg1", "wu1", "wd1"), ())
    x1, a1, b1 = ffn_fwd(x, small["ffn1_norm_g"], wg1, wu1, wd1, "ffn1_fwd", deps)
    (win, wout), _ = get_w(("win", "wout"), (x1,))
    qkv_t, zg_t = mix_in_fwd(x1, small["mix_norm_g"], win, col(small["b_in"]), "mix_in_fwd")
    ya_t = attn_fwd(qkv_t, small["attn_sinks"], "attn_fwd")
    lng, lnb = col(small["gmlp_ln_g"]), col(small["gmlp_ln_b"])
    w_s, b_s = small["gmlp_w_s"][0], small["gmlp_b_s"][0]
    yg_t = gmlp_fwd(zg_t, lng, lnb, w_s, b_s, "gmlp_fwd")
    gao, ggo = col(small["attn_out_norm_g"]), col(small["gmlp_out_norm_g"])
    x2 = mix_out_fwd(x1, ya_t, yg_t, gao, ggo, wout, small["b_out"], "mix_out_fwd")
    (wg2, wu2, wd2), _ = get_w(("wg2", "wu2", "wd2"), (x2,))
    x3, a2, b2 = ffn_fwd(x2, small["ffn2_norm_g"], wg2, wu2, wd2, "ffn2_fwd")

    gs = {}
    dx3, gs["final_norm_g"], loss = final_bwd(x3, target, small["final_norm_g"].reshape(1, -1), "final_bwd")

    dx2, da, db, s, h, gs["ffn2_norm_g"] = ffn_bwd(x2, dx3, small["ffn2_norm_g"], a2, b2, wg2, wu2, wd2, "ffn2_bwd")
    deps = put_g({"wg2": grad_tn(da, h, 1.0, "ffn2_dwg"), "wu2": grad_tn(db, h, 1.0, "ffn2_dwu"),
                  "wd2": grad_tn(s, dx3, FFN_RES, "ffn2_dwd")})

    dya_t, dyg_t, y_t, dgao, dggo, gs["b_out"] = mix_out_bwd(dx2, ya_t, yg_t, gao, ggo, wout, "mix_out_bwd", deps)
    gs["attn_out_norm_g"], gs["gmlp_out_norm_g"] = dgao.reshape(1, -1), dggo.reshape(1, -1)
    g_wout = grad_nn([y_t], dx2, "dwout")
    dzg_t, dws, dbs, dlng, dlnb = gmlp_bwd(zg_t, dyg_t, lng, lnb, w_s, b_s, "gmlp_bwd")
    gs["gmlp_w_s"], gs["gmlp_b_s"] = dws[None], dbs[None]
    gs["gmlp_ln_g"], gs["gmlp_ln_b"] = dlng.reshape(1, -1), dlnb.reshape(1, -1)
    dq_t, dk_t, dv_t, dsinks = attn_bwd(qkv_t, dya_t, small["attn_sinks"], "attn_bwd")
    gs["attn_sinks"] = dsinks.reshape(1, -1)
    dx1, h2, gs["mix_norm_g"], dbin = mix_in_bwd(x1, dx2, small["mix_norm_g"], dq_t, dk_t, dv_t, dzg_t, win,
                                                 "mix_in_bwd")
    gs["b_in"] = dbin.reshape(1, -1)
    deps = put_g({"wout": g_wout, "win": grad_nn([dq_t, dk_t, dv_t, dzg_t], h2, "dwin")})

    dx0, da, db, s, h, gs["ffn1_norm_g"] = ffn_bwd(x, dx1, small["ffn1_norm_g"], a1, b1, wg1, wu1, wd1, "ffn1_bwd",
                                                   deps)
    put_g({"wg1": grad_tn(da, h, 1.0, "ffn1_dwg"), "wu1": grad_tn(db, h, 1.0, "ffn1_dwu"),
           "wd1": grad_tn(s, dx1, FFN_RES, "ffn1_dwd")})
    return dx0, gs, loss


SMALL_NAMES = ["ffn1_norm_g", "mix_norm_g", "b_in", "attn_sinks", "gmlp_ln_g", "gmlp_ln_b", "gmlp_w_s", "gmlp_b_s",
               "attn_out_norm_g", "gmlp_out_norm_g", "b_out", "ffn2_norm_g", "final_norm_g"]
BIG_NAMES = {"wg1": ("ffn1_w_gate", True), "wu1": ("ffn1_w_up", True), "wd1": ("ffn1_w_down", False),
             "win": ("w_in", True), "wout": ("w_out", False),
             "wg2": ("ffn2_w_gate", True), "wu2": ("ffn2_w_up", True), "wd2": ("ffn2_w_down", False)}
WEIGHT_ORDER = ["ffn1_norm_g", "ffn1_w_gate", "ffn1_w_up", "ffn1_w_down", "mix_norm_g", "w_in", "b_in", "attn_sinks",
                "gmlp_ln_g", "gmlp_ln_b", "gmlp_w_s", "gmlp_b_s", "attn_out_norm_g", "gmlp_out_norm_g", "w_out",
                "b_out", "ffn2_norm_g", "ffn2_w_gate", "ffn2_w_up", "ffn2_w_down", "final_norm_g"]


def kernel(x, ffn1_norm_g, ffn1_w_gate, ffn1_w_up, ffn1_w_down, mix_norm_g, w_in, b_in, attn_sinks, gmlp_ln_g, gmlp_ln_b, gmlp_w_s, gmlp_b_s, attn_out_norm_g, gmlp_out_norm_g, w_out, b_out, ffn2_norm_g, ffn2_w_gate, ffn2_w_up, ffn2_w_down, final_norm_g, loss_target, m_ffn1_norm_g, m_ffn1_w_gate, m_ffn1_w_up, m_ffn1_w_down, m_mix_norm_g, m_w_in, m_b_in, m_attn_sinks, m_gmlp_ln_g, m_gmlp_ln_b, m_gmlp_w_s, m_gmlp_b_s, m_attn_out_norm_g, m_gmlp_out_norm_g, m_w_out, m_b_out, m_ffn2_norm_g, m_ffn2_w_gate, m_ffn2_w_up, m_ffn2_w_down, m_final_norm_g, v_ffn1_norm_g, v_ffn1_w_gate, v_ffn1_w_up, v_ffn1_w_down, v_mix_norm_g, v_w_in, v_b_in, v_attn_sinks, v_gmlp_ln_g, v_gmlp_ln_b, v_gmlp_w_s, v_gmlp_b_s, v_attn_out_norm_g, v_gmlp_out_norm_g, v_w_out, v_b_out, v_ffn2_norm_g, v_ffn2_w_gate, v_ffn2_w_up, v_ffn2_w_down, v_final_norm_g):
    args = dict(locals())
    w = {n: args[n] for n in WEIGHT_ORDER}
    m = {n: args["m_" + n] for n in WEIGHT_ORDER}
    v = {n: args["v_" + n] for n in WEIGHT_ORDER}

    shards = {}
    for key, (pname, transposed) in BIG_NAMES.items():
        mat = w[pname][0]
        shards[key] = (mat.T if transposed else mat).astype(BF16)
    d_model = x.shape[-1]
    flat = lambda g: g.reshape(-1, d_model)
    me = _mesh_place()[3].astype(jnp.int32).reshape(1)
    first = ("wg1", "wu1", "wd1")
    later = (("win", "wout"), ("wg2", "wu2", "wd2"))
    gathers, exchanges, last_token = {}, [], []

    def get_w(keys, after):
        if keys == first:
            order = list(first) + [k for grp in later for k in grp]
            full = dict(zip(order, allgather_rows([shards[k] for k in order], len(first), "allgather_ffn1")))
            started, token = split_start([([shards[k] for k in grp], [full[k] for k in grp]) for grp in later],
                                         gather_slices, (), "gather_start")
            gathers.update(zip(later, started))
            return [flat(full[k]) for k in keys], (token,)
        _, lands = split_wait(gathers[keys], gather_slices, after, "gather_wait_" + keys[0])
        return [flat(land) for land in lands], ()

    def put_g(gb):
        keys = tuple(gb)
        g3 = [gb[k].reshape(N_DEV, -1, d_model) for k in keys]
        lands = [lax.empty((N_DEV - 1,) + g.shape[1:], BF16) for g in g3]
        started, token = split_start([(g3, lands)], exchange_slices, (), "exchange_start_" + keys[0])
        exchanges.append((keys, started[0]))
        last_token[:] = [token]
        return (token,)

    small = {n: w[n] for n in SMALL_NAMES}
    grad_x, gs, loss = local_step(x[0], loss_target[0], small, get_w, put_g)

    grads, deltas, new_m, new_v = {}, {}, {}, {}
    after = tuple(last_token)
    for keys, started in exchanges:
        g3, lands = split_wait(started, exchange_slices, after, "exchange_wait_" + keys[0])
        for key, own, land in zip(keys, g3, lands):
            pname, transposed = BIG_NAMES[key]
            g = sum_partials(land, own, me, "sum_" + key)
            after = (g,)
            g = g.T if transposed else g
            d_, m_, v_ = adamw(w[pname][0], g, m[pname][0], v[pname][0], "adamw_" + key)
            grads[pname], deltas[pname], new_m[pname], new_v[pname] = g[None], d_[None], m_[None], v_[None]

    shapes = [w[n].shape for n in SMALL_NAMES]
    total = allreduce_small(pack_small([gs[n] for n in SMALL_NAMES] + [loss[:, :1]]), "allreduce_small")
    small_g = unpack_small(total, shapes + [(1, 1)])
    loss_total = small_g[-1].reshape(())
    gpack = pack_small(small_g[:-1])
    d_, m_, v_ = adamw(pack_small([w[n] for n in SMALL_NAMES]), gpack, pack_small([m[n] for n in SMALL_NAMES]),
                       pack_small([v[n] for n in SMALL_NAMES]), "adamw_small")
    for n, g_, dd, mm, vv in zip(SMALL_NAMES, small_g[:-1], unpack_small(d_, shapes), unpack_small(m_, shapes),
                                 unpack_small(v_, shapes)):
        grads[n], deltas[n], new_m[n], new_v[n] = g_, dd, mm, vv

    return (loss_total, grad_x[None], *[grads[n] for n in WEIGHT_ORDER], *[deltas[n] for n in WEIGHT_ORDER],
            *[new_m[n] for n in WEIGHT_ORDER], *[new_v[n] for n in WEIGHT_ORDER])
```

```python
import functools

import jax
import jax.numpy as jnp
from jax import lax
from jax.experimental import pallas as pl
from jax.experimental.pallas import tpu as pltpu

F32 = jnp.float32
BF16 = jnp.bfloat16

N_DEV = 8
N_Q_HEADS = 8
N_KV_HEADS = 2
HEAD_DIM = 64
ATTN_WIDTH = N_Q_HEADS * HEAD_DIM
KV_WIDTH = N_KV_HEADS * HEAD_DIM
QKV_WIDTH = ATTN_WIDTH + 2 * KV_WIDTH
BLOCK = 128
GMLP_GROUPS = 8
GMLP_GROUP_DIM = 64
GMLP_WIDTH = GMLP_GROUPS * GMLP_GROUP_DIM
EPS = 1e-6
FFN_RES = 0.5
ATTN_SCALE = HEAD_DIM ** -0.5
MASK_VALUE = -1e30

ADAM_LR = 0.001
ADAM_B1 = 0.9
ADAM_B2 = 0.999
ADAM_EPS = 1e-08
ADAM_WD = 0.01
ADAM_STEP = 10

V7X_VMEM_BYTES = 64 * 1024 * 1024
VMEM_LIMIT = 56 * 1024 * 1024
LANES = 128

NT = (((1,), (1,)), ((), ()))
TN = (((0,), (0,)), ((), ()))


def _dot(a, b):
    return jnp.dot(a, b, preferred_element_type=F32)


def _dg(a, b, dims):
    return lax.dot_general(a, b, dims, preferred_element_type=F32)


def _params(semantics=None):
    return pltpu.CompilerParams(dimension_semantics=semantics, vmem_limit_bytes=VMEM_LIMIT)


def _resident(shape):
    zeros = (0,) * len(shape)
    return pl.BlockSpec(shape, lambda *_: zeros, pipeline_mode=pl.Buffered(1))


def _drop_deps(body, n_in, n_deps):
    return lambda *refs: body(*refs[:n_in], *refs[n_in + n_deps:])


ANY_SPEC = pl.BlockSpec(memory_space=pl.ANY)


def _token_tile(t):
    return min(t, 512)


def _f_chunk(f):
    for c in (256, 128):
        if f % c == 0:
            return c
    return f


def ffn_fwd(x, gain, wg_t, wu_t, wd, name, deps=()):
    t, d = x.shape
    f = wd.shape[0]
    tm = _token_tile(t)
    fc = _f_chunk(f)

    def body(x_ref, g_ref, wg_ref, wu_ref, wd_ref, xo_ref, a_ref, b_ref):
        xv = x_ref[...]
        r = lax.rsqrt(jnp.mean(xv * xv, axis=-1, keepdims=True) + EPS)
        h = (xv * r * g_ref[...]).astype(BF16)
        acc = jnp.zeros((tm, d), F32)
        for c in range(f // fc):
            sl = slice(c * fc, (c + 1) * fc)
            a = _dg(h, wg_ref[sl, :], NT)
            b = _dg(h, wu_ref[sl, :], NT)
            a_ref[:, sl] = a.astype(BF16)
            b_ref[:, sl] = b.astype(BF16)
            s = (a * jax.nn.sigmoid(a) * b).astype(BF16)
            acc = acc + _dot(s, wd_ref[sl, :])
        xo_ref[...] = xv + FFN_RES * acc

    return pl.pallas_call(
        _drop_deps(body, 5, len(deps)), name=name, grid=(t // tm,),
        in_specs=[pl.BlockSpec((tm, d), lambda i: (i, 0)), _resident((1, d)),
                  _resident((f, d)), _resident((f, d)), _resident((f, d))] + [ANY_SPEC] * len(deps),
        out_specs=[pl.BlockSpec((tm, d), lambda i: (i, 0)), pl.BlockSpec((tm, f), lambda i: (i, 0)),
                   pl.BlockSpec((tm, f), lambda i: (i, 0))],
        out_shape=[jax.ShapeDtypeStruct((t, d), F32), jax.ShapeDtypeStruct((t, f), BF16),
                   jax.ShapeDtypeStruct((t, f), BF16)],
        compiler_params=_params(("arbitrary",)),
    )(x, gain, wg_t, wu_t, wd, *deps)


def _rmsnorm_bwd_rows(xv, r, gain, dh):
    n = xv * r
    dgain = jnp.sum(dh * n, axis=0, keepdims=True)
    dn = dh * gain
    dx = r * (dn - n * jnp.mean(dn * n, axis=-1, keepdims=True))
    return dx, dgain


def ffn_bwd(x, dxo, gain, a, b, wg_t, wu_t, wd, name, deps=()):
    t, d = x.shape
    f = wd.shape[0]
    tm = min(t, 256)
    fc = _f_chunk(f)

    def body(x_ref, dxo_ref, g_ref, a_ref, b_ref, wg_ref, wu_ref, wd_ref,
             dx_ref, da_ref, db_ref, s_ref, h_ref, dg_ref):
        i = pl.program_id(0)
        xv = x_ref[...]
        dxo = dxo_ref[...]
        gain_v = g_ref[...]
        r = lax.rsqrt(jnp.mean(xv * xv, axis=-1, keepdims=True) + EPS)
        h_ref[...] = (xv * r * gain_v).astype(BF16)
        df = (FFN_RES * dxo).astype(BF16)
        dh = jnp.zeros((tm, d), F32)
        for c in range(f // fc):
            sl = slice(c * fc, (c + 1) * fc)
            av = a_ref[:, sl].astype(F32)
            bv = b_ref[:, sl].astype(F32)
            ds = _dg(df, wd_ref[sl, :], NT)
            sig = jax.nn.sigmoid(av)
            sil = av * sig
            s_ref[:, sl] = (sil * bv).astype(BF16)
            da = (ds * bv * (sig * (1.0 + av * (1.0 - sig)))).astype(BF16)
            db = (ds * sil).astype(BF16)
            da_ref[:, sl] = da
            db_ref[:, sl] = db
            dh = dh + _dot(da, wg_ref[sl, :]) + _dot(db, wu_ref[sl, :])
        dx, dgain = _rmsnorm_bwd_rows(xv, r, gain_v, dh)
        dx_ref[...] = dxo + dx

        @pl.when(i == 0)
        def _():
            dg_ref[...] = jnp.zeros_like(dg_ref)

        dg_ref[...] += dgain

    tok = lambda w: pl.BlockSpec((tm, w), lambda i: (i, 0))
    return pl.pallas_call(
        _drop_deps(body, 8, len(deps)), name=name, grid=(t // tm,),
        in_specs=[tok(d), tok(d), _resident((1, d)), tok(f), tok(f),
                  _resident((f, d)), _resident((f, d)), _resident((f, d))] + [ANY_SPEC] * len(deps),
        out_specs=[tok(d), tok(f), tok(f), tok(f), tok(d), pl.BlockSpec((1, d), lambda i: (0, 0))],
        out_shape=[jax.ShapeDtypeStruct((t, d), F32), jax.ShapeDtypeStruct((t, f), BF16),
                   jax.ShapeDtypeStruct((t, f), BF16), jax.ShapeDtypeStruct((t, f), BF16),
                   jax.ShapeDtypeStruct((t, d), BF16), jax.ShapeDtypeStruct((1, d), F32)],
        compiler_params=_params(("arbitrary",)),
    )(x, dxo, gain, a, b, wg_t, wu_t, wd, *deps)


def _row_tile(m, cap):
    if m <= cap:
        return m
    best = None
    for bm in range(LANES, cap + 1, LANES):
        if m % bm == 0:
            best = bm
    return best if best is not None else m


def grad_tn(a, b, scale, name, deps=()):
    t, m = a.shape
    n = b.shape[1]
    bm = _row_tile(m, 1408)
    bk = min(t, 1024)
    nk = t // bk

    def body(a_ref, b_ref, o_ref, acc_ref):
        k = pl.program_id(1)

        @pl.when(k == 0)
        def _():
            acc_ref[...] = jnp.zeros_like(acc_ref)

        acc_ref[...] += _dg(a_ref[...].astype(BF16), b_ref[...].astype(BF16), TN)

        @pl.when(k == nk - 1)
        def _():
            o_ref[...] = (scale * acc_ref[...]).astype(BF16)

    return pl.pallas_call(
        _drop_deps(body, 2, len(deps)), name=name, grid=(m // bm, nk),
        in_specs=[pl.BlockSpec((bk, bm), lambda i, k: (k, i)), pl.BlockSpec((bk, n), lambda i, k: (k, 0))]
        + [ANY_SPEC] * len(deps),
        out_specs=pl.BlockSpec((bm, n), lambda i, k: (i, 0)),
        out_shape=jax.ShapeDtypeStruct((m, n), BF16),
        scratch_shapes=[pltpu.VMEM((bm, n), F32)],
        compiler_params=_params(("arbitrary", "arbitrary")),
    )(a, b, *deps)


def grad_nn(a_list, b, name):
    t, n = b.shape
    ms = [a.shape[0] for a in a_list]
    m = sum(ms)
    bk = min(t, 1024)
    nk = t // bk
    na = len(a_list)

    def body(*refs):
        a_refs, b_ref, o_ref, acc_ref = refs[:na], refs[na], refs[na + 1], refs[na + 2]
        k = pl.program_id(0)

        @pl.when(k == 0)
        def _():
            acc_ref[...] = jnp.zeros_like(acc_ref)

        bv = b_ref[...].astype(BF16)
        off = 0
        for a_ref, mi in zip(a_refs, ms):
            acc_ref[off:off + mi, :] += _dot(a_ref[...].astype(BF16), bv)
            off += mi

        @pl.when(k == nk - 1)
        def _():
            o_ref[...] = acc_ref[...].astype(BF16)

    return pl.pallas_call(
        body, name=name, grid=(nk,),
        in_specs=[pl.BlockSpec((mi, bk), lambda k: (0, k)) for mi in ms] + [pl.BlockSpec((bk, n), lambda k: (k, 0))],
        out_specs=pl.BlockSpec((m, n), lambda k: (0, 0)),
        out_shape=jax.ShapeDtypeStruct((m, n), BF16),
        scratch_shapes=[pltpu.VMEM((m, n), F32)],
        compiler_params=_params(("arbitrary",)),
    )(*a_list, b)


def mix_in_fwd(x, gain, win_t, b_in_col, name):
    t, d = x.shape
    w = win_t.shape[0]
    tm = _token_tile(t)

    def body(x_ref, g_ref, w_ref, bias_ref, qkv_ref, zg_ref):
        xv = x_ref[...]
        r = lax.rsqrt(jnp.mean(xv * xv, axis=-1, keepdims=True) + EPS)
        h = (xv * r * g_ref[...]).astype(BF16)
        qkv_ref[...] = (_dg(w_ref[:QKV_WIDTH, :], h, NT) + bias_ref[:QKV_WIDTH, :]).astype(BF16)
        zg_ref[...] = (_dg(w_ref[QKV_WIDTH:, :], h, NT) + bias_ref[QKV_WIDTH:, :]).astype(BF16)

    return pl.pallas_call(
        body, name=name, grid=(t // tm,),
        in_specs=[pl.BlockSpec((tm, d), lambda i: (i, 0)), _resident((1, d)), _resident((w, d)), _resident((w, 1))],
        out_specs=[pl.BlockSpec((QKV_WIDTH, tm), lambda i: (0, i)), pl.BlockSpec((w - QKV_WIDTH, tm), lambda i: (0, i))],
        out_shape=[jax.ShapeDtypeStruct((QKV_WIDTH, t), BF16), jax.ShapeDtypeStruct((w - QKV_WIDTH, t), BF16)],
        compiler_params=_params(("arbitrary",)),
    )(x, gain, win_t, b_in_col)


def _attn_specs(nb):
    last = nb - 1
    cur = lambda n: jnp.minimum(n, last)
    prev = lambda n: jnp.maximum(jnp.minimum(n, last) - 1, 0)
    k_row = ATTN_WIDTH // KV_WIDTH
    return [
        pl.BlockSpec((ATTN_WIDTH, BLOCK), lambda n: (0, cur(n))),
        pl.BlockSpec((KV_WIDTH, BLOCK), lambda n: (k_row, prev(n))),
        pl.BlockSpec((KV_WIDTH, BLOCK), lambda n: (k_row, cur(n))),
        pl.BlockSpec((KV_WIDTH, BLOCK), lambda n: (k_row + 1, prev(n))),
        pl.BlockSpec((KV_WIDTH, BLOCK), lambda n: (k_row + 1, cur(n))),
    ]


def _attn_probs(q, kp, kc, sink, mask_prev, mask_cur):
    sp = jnp.where(mask_prev, _dg(kp, q, TN) * ATTN_SCALE, MASK_VALUE)
    sc = jnp.where(mask_cur, _dg(kc, q, TN) * ATTN_SCALE, MASK_VALUE)
    m = jnp.maximum(jnp.maximum(jnp.max(sp, axis=0, keepdims=True), jnp.max(sc, axis=0, keepdims=True)), sink)
    pp = jnp.exp(sp - m)
    pc = jnp.exp(sc - m)
    es = jnp.exp(sink - m)
    inv = 1.0 / (jnp.sum(pp, axis=0, keepdims=True) + jnp.sum(pc, axis=0, keepdims=True) + es)
    return pp * inv, pc * inv, es * inv


def _attn_masks(n):
    key = lax.broadcasted_iota(jnp.int32, (BLOCK, BLOCK), 0)
    qry = lax.broadcasted_iota(jnp.int32, (BLOCK, BLOCK), 1)
    return (key > qry) & (n > 0), key <= qry


def attn_fwd(qkv_t, sinks, name):
    t = qkv_t.shape[1]
    nb = t // BLOCK
    rep = N_Q_HEADS // N_KV_HEADS

    def body(sink_ref, q_ref, kp_ref, kc_ref, vp_ref, vc_ref, y_ref):
        n = pl.program_id(0)
        mask_prev, mask_cur = _attn_masks(n)
        for g in range(N_KV_HEADS):
            gs = slice(g * HEAD_DIM, (g + 1) * HEAD_DIM)
            kp, kc, vp, vc = kp_ref[gs, :], kc_ref[gs, :], vp_ref[gs, :], vc_ref[gs, :]
            for rr in range(rep):
                hd = g * rep + rr
                hs = slice(hd * HEAD_DIM, (hd + 1) * HEAD_DIM)
                pp, pc, _ = _attn_probs(q_ref[hs, :], kp, kc, sink_ref[0, hd], mask_prev, mask_cur)
                y_ref[hs, :] = (_dot(vp, pp.astype(BF16)) + _dot(vc, pc.astype(BF16))).astype(BF16)

    return pl.pallas_call(
        body, name=name, grid=(nb,),
        in_specs=[pl.BlockSpec(memory_space=pltpu.SMEM)] + _attn_specs(nb),
        out_specs=pl.BlockSpec((ATTN_WIDTH, BLOCK), lambda n: (0, n)),
        out_shape=jax.ShapeDtypeStruct((ATTN_WIDTH, t), BF16),
        compiler_params=_params(("arbitrary",)),
    )(sinks, qkv_t, qkv_t, qkv_t, qkv_t, qkv_t)


def attn_bwd(qkv_t, dy_t, sinks, name):
    t = qkv_t.shape[1]
    nb = t // BLOCK
    rep = N_Q_HEADS // N_KV_HEADS

    def body(sink_ref, q_ref, kp_ref, kc_ref, vp_ref, vc_ref, do_ref, dq_ref, dk_ref, dv_ref, dsink_ref,
             ck_ref, cv_ref, sacc_ref):
        n = pl.program_id(0)

        @pl.when(n == 0)
        def _():
            sacc_ref[...] = jnp.zeros_like(sacc_ref)

        @pl.when(n < nb)
        def _():
            mask_prev, mask_cur = _attn_masks(n)
            for g in range(N_KV_HEADS):
                gs = slice(g * HEAD_DIM, (g + 1) * HEAD_DIM)
                kp, kc, vp, vc = kp_ref[gs, :], kc_ref[gs, :], vp_ref[gs, :], vc_ref[gs, :]
                dkp = jnp.zeros((HEAD_DIM, BLOCK), F32)
                dkc = jnp.zeros((HEAD_DIM, BLOCK), F32)
                dvp = jnp.zeros((HEAD_DIM, BLOCK), F32)
                dvc = jnp.zeros((HEAD_DIM, BLOCK), F32)
                for rr in range(rep):
                    hd = g * rep + rr
                    hs = slice(hd * HEAD_DIM, (hd + 1) * HEAD_DIM)
                    q = q_ref[hs, :]
                    do = do_ref[hs, :]
                    pp, pc, ps = _attn_probs(q, kp, kc, sink_ref[0, hd], mask_prev, mask_cur)
                    dpp = _dg(vp, do, TN)
                    dpc = _dg(vc, do, TN)
                    delta = jnp.sum(pp * dpp, axis=0, keepdims=True) + jnp.sum(pc * dpc, axis=0, keepdims=True)
                    dsp = (pp * (dpp - delta)).astype(BF16)
                    dsc = (pc * (dpc - delta)).astype(BF16)
                    sacc_ref[hd:hd + 1, :] += -(ps * delta)
                    dq_ref[hs, :] = (ATTN_SCALE * (_dot(kp, dsp) + _dot(kc, dsc))).astype(BF16)
                    dkp = dkp + _dg(q, dsp, NT)
                    dkc = dkc + _dg(q, dsc, NT)
                    dvp = dvp + _dg(do, pp.astype(BF16), NT)
                    dvc = dvc + _dg(do, pc.astype(BF16), NT)

                @pl.when(n > 0)
                def _():
                    dk_ref[gs, :] = (ck_ref[gs, :] + ATTN_SCALE * dkp).astype(BF16)
                    dv_ref[gs, :] = (cv_ref[gs, :] + dvp).astype(BF16)

                ck_ref[gs, :] = ATTN_SCALE * dkc
                cv_ref[gs, :] = dvc

        @pl.when(n == nb)
        def _():
            dk_ref[...] = ck_ref[...].astype(BF16)
            dv_ref[...] = cv_ref[...].astype(BF16)
            dsink_ref[...] = jnp.sum(sacc_ref[...], axis=1, keepdims=True)

    last = nb - 1
    done = lambda n: jnp.maximum(n - 1, 0)
    outs = pl.pallas_call(
        body, name=name, grid=(nb + 1,),
        in_specs=[pl.BlockSpec(memory_space=pltpu.SMEM)] + _attn_specs(nb)
        + [pl.BlockSpec((ATTN_WIDTH, BLOCK), lambda n: (0, jnp.minimum(n, last)))],
        out_specs=[pl.BlockSpec((ATTN_WIDTH, BLOCK), lambda n: (0, jnp.minimum(n, last))),
                   pl.BlockSpec((KV_WIDTH, BLOCK), lambda n: (0, done(n))),
                   pl.BlockSpec((KV_WIDTH, BLOCK), lambda n: (0, done(n))),
                   pl.BlockSpec((N_Q_HEADS, 1), lambda n: (0, 0))],
        out_shape=[jax.ShapeDtypeStruct((ATTN_WIDTH, t), BF16), jax.ShapeDtypeStruct((KV_WIDTH, t), BF16),
                   jax.ShapeDtypeStruct((KV_WIDTH, t), BF16), jax.ShapeDtypeStruct((N_Q_HEADS, 1), F32)],
        scratch_shapes=[pltpu.VMEM((KV_WIDTH, BLOCK), F32), pltpu.VMEM((KV_WIDTH, BLOCK), F32),
                        pltpu.VMEM((N_Q_HEADS, BLOCK), F32)],
        compiler_params=_params(("arbitrary",)),
    )(sinks, qkv_t, qkv_t, qkv_t, qkv_t, qkv_t, dy_t)
    return outs


GELU_C = 0.7978845608028654
GELU_A = 0.044715


def _gelu(x):
    return 0.5 * x * (1.0 + jnp.tanh(GELU_C * (x + GELU_A * x * x * x)))


def _gelu_grad(x):
    th = jnp.tanh(GELU_C * (x + GELU_A * x * x * x))
    return 0.5 * (1.0 + th) + 0.5 * x * (1.0 - th * th) * GELU_C * (1.0 + 3.0 * GELU_A * x * x)


def _gmlp_parts(zg, lng, lnb):
    z = _gelu(zg)
    u = z[:GMLP_WIDTH, :]
    vv = z[GMLP_WIDTH:, :]
    mu = jnp.mean(vv, axis=0, keepdims=True)
    xc = vv - mu
    rstd = lax.rsqrt(jnp.mean(xc * xc, axis=0, keepdims=True) + EPS)
    xhat = xc * rstd
    return u, xhat, rstd, xhat * lng + lnb


def _causal(w):
    row = lax.broadcasted_iota(jnp.int32, (BLOCK, BLOCK), 0)
    col = lax.broadcasted_iota(jnp.int32, (BLOCK, BLOCK), 1)
    return jnp.where(col <= row, w, 0.0)


def gmlp_fwd(zg_t, lng, lnb, w_s, b_s, name):
    t = zg_t.shape[1]
    nb = t // BLOCK

    def body(zg_ref, lng_ref, lnb_ref, w_ref, bs_ref, y_ref):
        u, _, _, vn = _gmlp_parts(zg_ref[...].astype(F32), lng_ref[...], lnb_ref[...])
        for g in range(GMLP_GROUPS):
            gs = slice(g * GMLP_GROUP_DIM, (g + 1) * GMLP_GROUP_DIM)
            wc = _causal(w_ref[g]).astype(BF16)
            mixed = _dg(vn[gs, :].astype(BF16), wc, NT) + bs_ref[g:g + 1, :]
            y_ref[gs, :] = (u[gs, :] * mixed).astype(BF16)

    return pl.pallas_call(
        body, name=name, grid=(nb,),
        in_specs=[pl.BlockSpec((2 * GMLP_WIDTH, BLOCK), lambda n: (0, n)), _resident((GMLP_WIDTH, 1)),
                  _resident((GMLP_WIDTH, 1)), _resident((GMLP_GROUPS, BLOCK, BLOCK)), _resident((GMLP_GROUPS, BLOCK))],
        out_specs=pl.BlockSpec((GMLP_WIDTH, BLOCK), lambda n: (0, n)),
        out_shape=jax.ShapeDtypeStruct((GMLP_WIDTH, t), BF16),
        compiler_params=_params(("arbitrary",)),
    )(zg_t, lng, lnb, w_s, b_s)


def gmlp_bwd(zg_t, dy_t, lng, lnb, w_s, b_s, name):
    t = zg_t.shape[1]
    nb = t // BLOCK

    def body(zg_ref, dy_ref, lng_ref, lnb_ref, w_ref, bs_ref, dzg_ref, dw_ref, dbs_ref, dlng_ref, dlnb_ref,
             gacc_ref, bacc_ref):
        n = pl.program_id(0)

        @pl.when(n == 0)
        def _():
            dw_ref[...] = jnp.zeros_like(dw_ref)
            dbs_ref[...] = jnp.zeros_like(dbs_ref)
            gacc_ref[...] = jnp.zeros_like(gacc_ref)
            bacc_ref[...] = jnp.zeros_like(bacc_ref)

        zg = zg_ref[...].astype(F32)
        lng_v = lng_ref[...]
        u, xhat, rstd, vn = _gmlp_parts(zg, lng_v, lnb_ref[...])
        dy = dy_ref[...].astype(F32)
        dvn_parts = []
        for g in range(GMLP_GROUPS):
            gs = slice(g * GMLP_GROUP_DIM, (g + 1) * GMLP_GROUP_DIM)
            wc = _causal(w_ref[g]).astype(BF16)
            vn_g = vn[gs, :].astype(BF16)
            mixed = _dg(vn_g, wc, NT) + bs_ref[g:g + 1, :]
            dy_g = dy[gs, :]
            dmixed = dy_g * u[gs, :]
            dmixed_b = dmixed.astype(BF16)
            dzg_ref[gs, :] = (dy_g * mixed * _gelu_grad(zg[gs, :])).astype(BF16)
            dw_ref[g] += _causal(_dg(dmixed_b, vn_g, TN))
            dbs_ref[g:g + 1, :] += jnp.sum(dmixed, axis=0, keepdims=True)
            dvn_parts.append(_dot(dmixed_b, wc))
        dvn = jnp.concatenate(dvn_parts, axis=0)
        gacc_ref[...] += dvn * xhat
        bacc_ref[...] += dvn
        dxhat = dvn * lng_v
        m1 = jnp.mean(dxhat, axis=0, keepdims=True)
        m2 = jnp.mean(dxhat * xhat, axis=0, keepdims=True)
        dvv = rstd * (dxhat - m1 - xhat * m2)
        dzg_ref[GMLP_WIDTH:, :] = (dvv * _gelu_grad(zg[GMLP_WIDTH:, :])).astype(BF16)

        @pl.when(n == nb - 1)
        def _():
            dlng_ref[...] = jnp.sum(gacc_ref[...], axis=1, keepdims=True)
            dlnb_ref[...] = jnp.sum(bacc_ref[...], axis=1, keepdims=True)

    const2 = lambda n: (0, 0)
    return pl.pallas_call(
        body, name=name, grid=(nb,),
        in_specs=[pl.BlockSpec((2 * GMLP_WIDTH, BLOCK), lambda n: (0, n)), pl.BlockSpec((GMLP_WIDTH, BLOCK), lambda n: (0, n)),
                  _resident((GMLP_WIDTH, 1)), _resident((GMLP_WIDTH, 1)),
                  _resident((GMLP_GROUPS, BLOCK, BLOCK)), _resident((GMLP_GROUPS, BLOCK))],
        out_specs=[pl.BlockSpec((2 * GMLP_WIDTH, BLOCK), lambda n: (0, n)),
                   pl.BlockSpec((GMLP_GROUPS, BLOCK, BLOCK), lambda n: (0, 0, 0)),
                   pl.BlockSpec((GMLP_GROUPS, BLOCK), const2),
                   pl.BlockSpec((GMLP_WIDTH, 1), const2), pl.BlockSpec((GMLP_WIDTH, 1), const2)],
        out_shape=[jax.ShapeDtypeStruct((2 * GMLP_WIDTH, t), BF16),
                   jax.ShapeDtypeStruct((GMLP_GROUPS, BLOCK, BLOCK), F32),
                   jax.ShapeDtypeStruct((GMLP_GROUPS, BLOCK), F32),
                   jax.ShapeDtypeStruct((GMLP_WIDTH, 1), F32), jax.ShapeDtypeStruct((GMLP_WIDTH, 1), F32)],
        scratch_shapes=[pltpu.VMEM((GMLP_WIDTH, BLOCK), F32), pltpu.VMEM((GMLP_WIDTH, BLOCK), F32)],
        compiler_params=_params(("arbitrary",)),
    )(zg_t, dy_t, lng, lnb, w_s, b_s)


def _rmsnorm_cols(y, gain_col):
    r = lax.rsqrt(jnp.mean(y * y, axis=0, keepdims=True) + EPS)
    n = y * r
    return n, r, n * gain_col


def mix_out_fwd(x, ya_t, yg_t, gao, ggo, wout, b_out, name):
    t, d = x.shape
    tm = _token_tile(t)

    def body(x_ref, ya_ref, yg_ref, gao_ref, ggo_ref, w_ref, b_ref, o_ref):
        _, _, ya = _rmsnorm_cols(ya_ref[...].astype(F32), gao_ref[...])
        _, _, yg = _rmsnorm_cols(yg_ref[...].astype(F32), ggo_ref[...])
        o_ref[...] = (x_ref[...] + _dg(ya.astype(BF16), w_ref[:ATTN_WIDTH, :], TN)
                      + _dg(yg.astype(BF16), w_ref[ATTN_WIDTH:, :], TN) + b_ref[...])

    return pl.pallas_call(
        body, name=name, grid=(t // tm,),
        in_specs=[pl.BlockSpec((tm, d), lambda i: (i, 0)), pl.BlockSpec((ATTN_WIDTH, tm), lambda i: (0, i)),
                  pl.BlockSpec((GMLP_WIDTH, tm), lambda i: (0, i)), _resident((ATTN_WIDTH, 1)), _resident((GMLP_WIDTH, 1)),
                  _resident((ATTN_WIDTH + GMLP_WIDTH, d)), _resident((1, d))],
        out_specs=pl.BlockSpec((tm, d), lambda i: (i, 0)),
        out_shape=jax.ShapeDtypeStruct((t, d), F32),
        compiler_params=_params(("arbitrary",)),
    )(x, ya_t, yg_t, gao, ggo, wout, b_out)


def _lane_fold(v):
    out = v[:, :LANES]
    for j in range(1, v.shape[1] // LANES):
        out = out + v[:, j * LANES:(j + 1) * LANES]
    return out


def mix_out_bwd(dx, ya_t, yg_t, gao, ggo, wout, name, deps=()):
    t, d = dx.shape
    tm = _token_tile(t)
    nt = t // tm

    def body(dx_ref, ya_ref, yg_ref, gao_ref, ggo_ref, w_ref, dya_ref, dyg_ref, y_ref, dgao_ref, dggo_ref, db_ref,
             acc_ref):
        i = pl.program_id(0)

        @pl.when(i == 0)
        def _():
            acc_ref[...] = jnp.zeros_like(acc_ref)
            db_ref[...] = jnp.zeros_like(db_ref)

        dxv = dx_ref[...]
        db_ref[...] += jnp.sum(dxv, axis=0, keepdims=True)
        dxb = dxv.astype(BF16)
        for part, (src_ref, gain_ref, dst_ref) in enumerate(((ya_ref, gao_ref, dya_ref), (yg_ref, ggo_ref, dyg_ref))):
            rows = slice(part * ATTN_WIDTH, (part + 1) * ATTN_WIDTH)
            gain = gain_ref[...]
            nrm, r, yn = _rmsnorm_cols(src_ref[...].astype(F32), gain)
            y_ref[rows, :] = yn.astype(BF16)
            dy = _dg(w_ref[rows, :], dxb, NT)
            acc_ref[rows, :] += _lane_fold(dy * nrm)
            dn = dy * gain
            dst_ref[...] = (r * (dn - nrm * jnp.mean(dn * nrm, axis=0, keepdims=True))).astype(BF16)

        @pl.when(i == nt - 1)
        def _():
            dgao_ref[...] = jnp.sum(acc_ref[:ATTN_WIDTH, :], axis=1, keepdims=True)
            dggo_ref[...] = jnp.sum(acc_ref[ATTN_WIDTH:, :], axis=1, keepdims=True)

    const2 = lambda i: (0, 0)
    feat = lambda w: pl.BlockSpec((w, tm), lambda i: (0, i))
    return pl.pallas_call(
        _drop_deps(body, 6, len(deps)), name=name, grid=(nt,),
        in_specs=[pl.BlockSpec((tm, d), lambda i: (i, 0)), feat(ATTN_WIDTH), feat(GMLP_WIDTH),
                  _resident((ATTN_WIDTH, 1)), _resident((GMLP_WIDTH, 1)), _resident((ATTN_WIDTH + GMLP_WIDTH, d))]
        + [ANY_SPEC] * len(deps),
        out_specs=[feat(ATTN_WIDTH), feat(GMLP_WIDTH), feat(ATTN_WIDTH + GMLP_WIDTH),
                   pl.BlockSpec((ATTN_WIDTH, 1), const2), pl.BlockSpec((GMLP_WIDTH, 1), const2),
                   pl.BlockSpec((1, d), const2)],
        out_shape=[jax.ShapeDtypeStruct((ATTN_WIDTH, t), BF16), jax.ShapeDtypeStruct((GMLP_WIDTH, t), BF16),
                   jax.ShapeDtypeStruct((ATTN_WIDTH + GMLP_WIDTH, t), BF16),
                   jax.ShapeDtypeStruct((ATTN_WIDTH, 1), F32), jax.ShapeDtypeStruct((GMLP_WIDTH, 1), F32),
                   jax.ShapeDtypeStruct((1, d), F32)],
        scratch_shapes=[pltpu.VMEM((ATTN_WIDTH + GMLP_WIDTH, LANES), F32)],
        compiler_params=_params(("arbitrary",)),
    )(dx, ya_t, yg_t, gao, ggo, wout, *deps)


def mix_in_bwd(x, dxo, gain, dq_t, dk_t, dv_t, dzg_t, win_t, name):
    t, d = x.shape
    w = win_t.shape[0]
    tm = _token_tile(t)
    nt = t // tm
    parts = ((0, ATTN_WIDTH), (ATTN_WIDTH, KV_WIDTH), (ATTN_WIDTH + KV_WIDTH, KV_WIDTH), (QKV_WIDTH, w - QKV_WIDTH))

    def body(x_ref, dxo_ref, g_ref, dq_ref, dk_ref, dv_ref, dzg_ref, w_ref, dx_ref, h_ref, dg_ref, db_ref, acc_ref):
        i = pl.program_id(0)

        @pl.when(i == 0)
        def _():
            acc_ref[...] = jnp.zeros_like(acc_ref)
            dg_ref[...] = jnp.zeros_like(dg_ref)

        xv = x_ref[...]
        gain_v = g_ref[...]
        r = lax.rsqrt(jnp.mean(xv * xv, axis=-1, keepdims=True) + EPS)
        h_ref[...] = (xv * r * gain_v).astype(BF16)
        dh = jnp.zeros((tm, d), F32)
        for (off, size), src_ref in zip(parts, (dq_ref, dk_ref, dv_ref, dzg_ref)):
            dp = src_ref[...]
            acc_ref[off:off + size, :] += _lane_fold(dp.astype(F32))
            dh = dh + _dg(dp, w_ref[off:off + size, :], TN)
        dx, dgain = _rmsnorm_bwd_rows(xv, r, gain_v, dh)
        dx_ref[...] = dxo_ref[...] + dx
        dg_ref[...] += dgain

        @pl.when(i == nt - 1)
        def _():
            db_ref[...] = jnp.sum(acc_ref[...], axis=1, keepdims=True)

    const2 = lambda i: (0, 0)
    tok = lambda: pl.BlockSpec((tm, d), lambda i: (i, 0))
    feat = lambda rows: pl.BlockSpec((rows, tm), lambda i: (0, i))
    return pl.pallas_call(
        body, name=name, grid=(nt,),
        in_specs=[tok(), tok(), _resident((1, d)), feat(ATTN_WIDTH), feat(KV_WIDTH), feat(KV_WIDTH),
                  feat(w - QKV_WIDTH), _resident((w, d))],
        out_specs=[tok(), tok(), pl.BlockSpec((1, d), const2), pl.BlockSpec((w, 1), const2)],
        out_shape=[jax.ShapeDtypeStruct((t, d), F32), jax.ShapeDtypeStruct((t, d), BF16),
                   jax.ShapeDtypeStruct((1, d), F32), jax.ShapeDtypeStruct((w, 1), F32)],
        scratch_shapes=[pltpu.VMEM((w, LANES), F32)],
        compiler_params=_params(("arbitrary",)),
    )(x, dxo, gain, dq_t, dk_t, dv_t, dzg_t, win_t)


def final_bwd(x, target, gain, name):
    t, d = x.shape
    tm = _token_tile(t)

    def body(x_ref, t_ref, g_ref, dx_ref, dg_ref, loss_ref):
        i = pl.program_id(0)

        @pl.when(i == 0)
        def _():
            dg_ref[...] = jnp.zeros_like(dg_ref)
            loss_ref[...] = jnp.zeros_like(loss_ref)

        xv = x_ref[...]
        gain_v = g_ref[...]
        r = lax.rsqrt(jnp.mean(xv * xv, axis=-1, keepdims=True) + EPS)
        err = xv * r * gain_v - t_ref[...]
        row_loss = jnp.mean(err * err, axis=-1, keepdims=True)
        loss_ref[...] += 0.5 * jnp.sum(row_loss, axis=0, keepdims=True)
        dx, dgain = _rmsnorm_bwd_rows(xv, r, gain_v, err * (1.0 / d))
        dx_ref[...] = dx
        dg_ref[...] += dgain

    const2 = lambda i: (0, 0)
    return pl.pallas_call(
        body, name=name, grid=(t // tm,),
        in_specs=[pl.BlockSpec((tm, d), lambda i: (i, 0)), pl.BlockSpec((tm, d), lambda i: (i, 0)), _resident((1, d))],
        out_specs=[pl.BlockSpec((tm, d), lambda i: (i, 0)), pl.BlockSpec((1, d), const2), pl.BlockSpec((1, LANES), const2)],
        out_shape=[jax.ShapeDtypeStruct((t, d), F32), jax.ShapeDtypeStruct((1, d), F32),
                   jax.ShapeDtypeStruct((1, LANES), F32)],
        compiler_params=_params(("arbitrary",)),
    )(x, target, gain)


def _adamw_math(w, g, m, v):
    m = ADAM_B1 * m + (1.0 - ADAM_B1) * g
    v = ADAM_B2 * v + (1.0 - ADAM_B2) * (g * g)
    m_hat = m / (1.0 - ADAM_B1 ** ADAM_STEP)
    v_hat = v / (1.0 - ADAM_B2 ** ADAM_STEP)
    delta = -ADAM_LR * (m_hat / (jnp.sqrt(v_hat) + ADAM_EPS) + ADAM_WD * w)
    return delta, m, v


def adamw(w, g, m, v, name):
    rows, cols = w.shape
    tr = rows
    if rows * cols > 256 * 1024:
        for cand in (256, 128, 64, 32, 16, 8):
            if rows % cand == 0:
                tr = cand
                break

    def body(w_ref, g_ref, m_ref, v_ref, d_ref, mo_ref, vo_ref):
        delta, mn, vn = _adamw_math(w_ref[...], g_ref[...], m_ref[...], v_ref[...])
        d_ref[...] = delta
        mo_ref[...] = mn
        vo_ref[...] = vn

    spec = pl.BlockSpec((tr, cols), lambda i: (i, 0))
    shape = jax.ShapeDtypeStruct((rows, cols), F32)
    return pl.pallas_call(
        body, name=name, grid=(rows // tr,),
        in_specs=[spec] * 4, out_specs=[spec] * 3, out_shape=[shape] * 3,
        compiler_params=_params(("arbitrary",)),
    )(w, g, m, v)


def sum_partials(landing, grad3d, me, name, deps=()):
    nd, rows, cols = landing.shape
    tr = rows
    for cand in (128, 112, 88, 64, 32, 16):
        if rows % cand == 0:
            tr = cand
            break

    def body(me_ref, l_ref, own_ref, o_ref):
        acc = own_ref[...].astype(F32)
        for j in range(nd):
            acc = acc + l_ref[j].astype(F32)
        o_ref[...] = acc

    return pl.pallas_call(
        _drop_deps(body, 3, len(deps)), name=name,
        grid_spec=pltpu.PrefetchScalarGridSpec(
            num_scalar_prefetch=1, grid=(rows // tr,),
            in_specs=[pl.BlockSpec((nd, tr, cols), lambda i, me_ref: (0, i, 0)),
                      pl.BlockSpec((None, tr, cols), lambda i, me_ref: (me_ref[0], i, 0))] + [ANY_SPEC] * len(deps),
            out_specs=pl.BlockSpec((tr, cols), lambda i, me_ref: (i, 0))),
        out_shape=jax.ShapeDtypeStruct((rows, cols), F32),
        compiler_params=_params(("arbitrary",)),
    )(me, landing, grad3d, *deps)


def _mesh_place():
    x, y, c = lax.axis_index("x"), lax.axis_index("y"), lax.axis_index("c")
    return x, y, c, 4 * x + 2 * y + c


def _peer(x, y, c, k):
    return (x ^ ((k >> 2) & 1), y ^ ((k >> 1) & 1), c ^ (k & 1))


def allgather_rows(shards, n_gather, name):
    na = len(shards)

    def body(*refs):
        in_refs, out_refs = refs[:na], refs[na:2 * na]
        send_sems, recv_sems, local_sems = refs[2 * na:]
        x, y, c, me = _mesh_place()
        local = [pltpu.make_async_copy(in_refs[a], out_refs[a].at[me], local_sems.at[a]) for a in range(na)]
        for cp in local:
            cp.start()
        def copy(sem, a, origin, src, k):
            return pltpu.make_async_remote_copy(
                src_ref=src, dst_ref=out_refs[a].at[origin], send_sem=send_sems.at[sem, a],
                recv_sem=recv_sems.at[sem, a], device_id=_peer(x, y, c, k), device_id_type=pl.DeviceIdType.MESH)

        chips = (2, 4, 6)
        sends = []
        for a in range(n_gather):
            for sem, k in enumerate((1,) + chips):
                sends.append(copy(sem, a, me, in_refs[a], k))
        for cp in sends:
            cp.start()
        for j, k in enumerate(chips):
            for a in range(n_gather):
                copy(1 + j, a, me ^ k, in_refs[a], k).wait_recv()
                fwd = copy(4 + j, a, me ^ k, out_refs[a].at[me ^ k], 1)
                fwd.start()
                sends.append(fwd)
        for a in range(n_gather):
            copy(0, a, me ^ 1, in_refs[a], 1).wait_recv()
        for j, k in enumerate(chips):
            for a in range(n_gather):
                copy(4 + j, a, me ^ 1 ^ k, in_refs[a], 1).wait_recv()
        for cp in sends:
            cp.wait_send()
        for cp in local:
            cp.wait()

    return pl.pallas_call(
        body, name=name,
        in_specs=[ANY_SPEC] * na, out_specs=[ANY_SPEC] * na,
        out_shape=[jax.ShapeDtypeStruct((N_DEV,) + s.shape, s.dtype) for s in shards],
        scratch_shapes=[pltpu.SemaphoreType.DMA((N_DEV - 1, n_gather)), pltpu.SemaphoreType.DMA((N_DEV - 1, n_gather)),
                        pltpu.SemaphoreType.DMA((na,))],
    )(*shards)


HBM_SPEC = pl.BlockSpec(memory_space=pltpu.HBM)
SEM_SPEC = pl.BlockSpec(memory_space=pltpu.SEMAPHORE)
SIDE_EFFECT = pltpu.SideEffectType.DATAFLOW_SIDE_EFFECTING


def _hbm(v):
    return pltpu.with_memory_space_constraint(v, pltpu.HBM)


def gather_slices(src, land, me, k):
    return src, land.at[me], land.at[me ^ k]


def exchange_slices(src, land, me, k):
    return src.at[me ^ k], land.at[k - 1], land.at[k - 1]


def split_start(groups, slices, deps, name):
    sizes = [len(srcs) for srcs, _ in groups]
    flat_src = [s for srcs, _ in groups for s in srcs]
    flat_land = [l for _, lands in groups for l in lands]
    na, ng, nd = len(flat_src), len(groups), len(deps)

    def body(*refs):
        src_refs, land_refs = refs[:na], refs[na:2 * na]
        sem_refs = refs[2 * na + nd:2 * na + nd + 2 * ng]
        token_ref = refs[-1]
        x, y, c, me = _mesh_place()
        a0 = 0
        for gi, size in enumerate(sizes):
            send_sems, recv_sems = sem_refs[2 * gi], sem_refs[2 * gi + 1]
            for k in range(1, N_DEV):
                for j in range(size):
                    src, dst, _ = slices(src_refs[a0 + j], land_refs[a0 + j], me, k)
                    pltpu.make_async_remote_copy(
                        src_ref=src, dst_ref=dst, send_sem=send_sems.at[(k - 1) * size + j],
                        recv_sem=recv_sems.at[(k - 1) * size + j],
                        device_id=_peer(x, y, c, k), device_id_type=pl.DeviceIdType.MESH).start()
            a0 += size
        token_ref[...] = jnp.zeros_like(token_ref)

    sems = [pltpu.SemaphoreType.DMA(((N_DEV - 1) * size,)) for size in sizes for _ in range(2)]
    thru = [pltpu.HBM(v.shape, v.dtype) for v in flat_src + flat_land]
    outs = pl.pallas_call(
        body, name=name,
        in_specs=[HBM_SPEC] * (2 * na) + [ANY_SPEC] * nd,
        out_specs=[SEM_SPEC] * (2 * ng) + [HBM_SPEC] * (2 * na) + [pl.BlockSpec(memory_space=pltpu.VMEM)],
        out_shape=sems + thru + [jax.ShapeDtypeStruct((8, LANES), F32)],
        input_output_aliases={i: 2 * ng + i for i in range(2 * na)},
        compiler_params=pltpu.CompilerParams(has_side_effects=SIDE_EFFECT),
    )(*[_hbm(v) for v in flat_src + flat_land], *deps)
    started, a0 = [], 0
    for gi, size in enumerate(sizes):
        srcs = outs[2 * ng + a0:2 * ng + a0 + size]
        lands = outs[2 * ng + na + a0:2 * ng + na + a0 + size]
        started.append((outs[2 * gi], outs[2 * gi + 1], list(srcs), list(lands)))
        a0 += size
    return started, outs[-1]


def split_wait(started, slices, after, name):
    send_sems, recv_sems, srcs, lands = started
    na = len(srcs)

    def body(*refs):
        src_refs, land_refs = refs[:na], refs[na:2 * na]
        send_ref, recv_ref = refs[2 * na], refs[2 * na + 1]
        x, y, c, me = _mesh_place()
        for k in range(1, N_DEV):
            for j in range(na):
                src, _, got = slices(src_refs[j], land_refs[j], me, k)
                cp = pltpu.make_async_remote_copy(
                    src_ref=src, dst_ref=got, send_sem=send_ref.at[(k - 1) * na + j], recv_sem=recv_ref.at[(k - 1) * na + j],
                    device_id=_peer(x, y, c, k), device_id_type=pl.DeviceIdType.MESH)
                cp.wait_send()
                cp.wait_recv()

    outs = pl.pallas_call(
        body, name=name,
        in_specs=[HBM_SPEC] * (2 * na) + [SEM_SPEC, SEM_SPEC] + [ANY_SPEC] * len(after),
        out_specs=[HBM_SPEC] * (2 * na),
        out_shape=[pltpu.HBM(v.shape, v.dtype) for v in srcs + lands],
        input_output_aliases={i: i for i in range(2 * na)},
        compiler_params=pltpu.CompilerParams(has_side_effects=SIDE_EFFECT),
    )(*srcs, *lands, send_sems, recv_sems, *after)
    return list(outs[:na]), list(outs[na:])


def place_own(pack, me, name):
    rows, cols = pack.shape

    def body(me_ref, p_ref, o_ref):
        o_ref[...] = p_ref[...]

    return pl.pallas_call(
        body, name=name,
        grid_spec=pltpu.PrefetchScalarGridSpec(
            num_scalar_prefetch=1, grid=(1,),
            in_specs=[pl.BlockSpec((rows, cols), lambda i, me_ref: (0, 0))],
            out_specs=pl.BlockSpec((None, rows, cols), lambda i, me_ref: (me_ref[0], 0, 0))),
        out_shape=jax.ShapeDtypeStruct((N_DEV, rows, cols), F32),
        compiler_params=_params(("arbitrary",)),
    )(me, pack)


def sum_slots(slots, name):
    nd, rows, cols = slots.shape

    def body(s_ref, o_ref):
        acc = s_ref[0]
        for j in range(1, nd):
            acc = acc + s_ref[j]
        o_ref[...] = acc

    return pl.pallas_call(
        body, name=name, grid=(1,),
        in_specs=[pl.BlockSpec((nd, rows, cols), lambda i: (0, 0, 0))],
        out_specs=pl.BlockSpec((rows, cols), lambda i: (0, 0)),
        out_shape=jax.ShapeDtypeStruct((rows, cols), F32),
        compiler_params=_params(("arbitrary",)),
    )(slots)


def _pack_rows(size):
    rows = -(-size // LANES)
    return -(-rows // 8) * 8


def pack_small(parts):
    out = []
    for p in parts:
        flat = p.reshape(-1).astype(F32)
        rows = _pack_rows(flat.shape[0])
        out.append(jnp.pad(flat, (0, rows * LANES - flat.shape[0])).reshape(rows, LANES))
    return jnp.concatenate(out, axis=0)


def unpack_small(pack, shapes):
    out, off = [], 0
    for shp in shapes:
        size = 1
        for s in shp:
            size *= s
        rows = _pack_rows(size)
        out.append(pack[off:off + rows].reshape(-1)[:size].reshape(shp))
        off += rows
    return out


def local_step(x, target, small, get_w, put_g, put_small):
    col = lambda v: v.reshape(-1, 1)
    (wg1, wu1, wd1), deps = get_w(("wg1", "wu1", "wd1"), ())
    x1, a1, b1 = ffn_fwd(x, small["ffn1_norm_g"], wg1, wu1, wd1, "ffn1_fwd", deps)
    (win, wout), _ = get_w(("win", "wout"), (x1,))
    qkv_t, zg_t = mix_in_fwd(x1, small["mix_norm_g"], win, col(small["b_in"]), "mix_in_fwd")
    ya_t = attn_fwd(qkv_t, small["attn_sinks"], "attn_fwd")
    lng, lnb = col(small["gmlp_ln_g"]), col(small["gmlp_ln_b"])
    w_s, b_s = small["gmlp_w_s"][0], small["gmlp_b_s"][0]
    yg_t = gmlp_fwd(zg_t, lng, lnb, w_s, b_s, "gmlp_fwd")
    gao, ggo = col(small["attn_out_norm_g"]), col(small["gmlp_out_norm_g"])
    x2 = mix_out_fwd(x1, ya_t, yg_t, gao, ggo, wout, small["b_out"], "mix_out_fwd")
    (wg2, wu2, wd2), _ = get_w(("wg2", "wu2", "wd2"), (x2,))
    x3, a2, b2 = ffn_fwd(x2, small["ffn2_norm_g"], wg2, wu2, wd2, "ffn2_fwd")

    gs = {}
    dx3, gs["final_norm_g"], loss = final_bwd(x3, target, small["final_norm_g"].reshape(1, -1), "final_bwd")

    dx2, da, db, s, h, gs["ffn2_norm_g"] = ffn_bwd(x2, dx3, small["ffn2_norm_g"], a2, b2, wg2, wu2, wd2, "ffn2_bwd")
    deps = put_g({"wg2": grad_tn(da, h, 1.0, "ffn2_dwg"), "wu2": grad_tn(db, h, 1.0, "ffn2_dwu"),
                  "wd2": grad_tn(s, dx3, FFN_RES, "ffn2_dwd")})

    dya_t, dyg_t, y_t, dgao, dggo, gs["b_out"] = mix_out_bwd(dx2, ya_t, yg_t, gao, ggo, wout, "mix_out_bwd", deps)
    gs["attn_out_norm_g"], gs["gmlp_out_norm_g"] = dgao.reshape(1, -1), dggo.reshape(1, -1)
    g_wout = grad_nn([y_t], dx2, "dwout")
    dzg_t, dws, dbs, dlng, dlnb = gmlp_bwd(zg_t, dyg_t, lng, lnb, w_s, b_s, "gmlp_bwd")
    gs["gmlp_w_s"], gs["gmlp_b_s"] = dws[None], dbs[None]
    gs["gmlp_ln_g"], gs["gmlp_ln_b"] = dlng.reshape(1, -1), dlnb.reshape(1, -1)
    dq_t, dk_t, dv_t, dsinks = attn_bwd(qkv_t, dya_t, small["attn_sinks"], "attn_bwd")
    gs["attn_sinks"] = dsinks.reshape(1, -1)
    dx1, h2, gs["mix_norm_g"], dbin = mix_in_bwd(x1, dx2, small["mix_norm_g"], dq_t, dk_t, dv_t, dzg_t, win,
                                                 "mix_in_bwd")
    gs["b_in"] = dbin.reshape(1, -1)
    deps = put_g({"wout": g_wout, "win": grad_nn([dq_t, dk_t, dv_t, dzg_t], h2, "dwin")})

    dx0, da, db, s, h, gs["ffn1_norm_g"] = ffn_bwd(x, dx1, small["ffn1_norm_g"], a1, b1, wg1, wu1, wd1, "ffn1_bwd",
                                                   deps)
    deps = put_small(gs, loss)
    deps = put_g({"wg1": grad_tn(da, h, 1.0, "ffn1_dwg", deps)})
    deps = put_g({"wu1": grad_tn(db, h, 1.0, "ffn1_dwu", deps)})
    put_g({"wd1": grad_tn(s, dx1, FFN_RES, "ffn1_dwd", deps)})
    return dx0


SMALL_NAMES = ["ffn1_norm_g", "mix_norm_g", "b_in", "attn_sinks", "gmlp_ln_g", "gmlp_ln_b", "gmlp_w_s", "gmlp_b_s",
               "attn_out_norm_g", "gmlp_out_norm_g", "b_out", "ffn2_norm_g", "final_norm_g"]
BIG_NAMES = {"wg1": ("ffn1_w_gate", True), "wu1": ("ffn1_w_up", True), "wd1": ("ffn1_w_down", False),
             "win": ("w_in", True), "wout": ("w_out", False),
             "wg2": ("ffn2_w_gate", True), "wu2": ("ffn2_w_up", True), "wd2": ("ffn2_w_down", False)}
WEIGHT_ORDER = ["ffn1_norm_g", "ffn1_w_gate", "ffn1_w_up", "ffn1_w_down", "mix_norm_g", "w_in", "b_in", "attn_sinks",
                "gmlp_ln_g", "gmlp_ln_b", "gmlp_w_s", "gmlp_b_s", "attn_out_norm_g", "gmlp_out_norm_g", "w_out",
                "b_out", "ffn2_norm_g", "ffn2_w_gate", "ffn2_w_up", "ffn2_w_down", "final_norm_g"]


def kernel(x, ffn1_norm_g, ffn1_w_gate, ffn1_w_up, ffn1_w_down, mix_norm_g, w_in, b_in, attn_sinks, gmlp_ln_g, gmlp_ln_b, gmlp_w_s, gmlp_b_s, attn_out_norm_g, gmlp_out_norm_g, w_out, b_out, ffn2_norm_g, ffn2_w_gate, ffn2_w_up, ffn2_w_down, final_norm_g, loss_target, m_ffn1_norm_g, m_ffn1_w_gate, m_ffn1_w_up, m_ffn1_w_down, m_mix_norm_g, m_w_in, m_b_in, m_attn_sinks, m_gmlp_ln_g, m_gmlp_ln_b, m_gmlp_w_s, m_gmlp_b_s, m_attn_out_norm_g, m_gmlp_out_norm_g, m_w_out, m_b_out, m_ffn2_norm_g, m_ffn2_w_gate, m_ffn2_w_up, m_ffn2_w_down, m_final_norm_g, v_ffn1_norm_g, v_ffn1_w_gate, v_ffn1_w_up, v_ffn1_w_down, v_mix_norm_g, v_w_in, v_b_in, v_attn_sinks, v_gmlp_ln_g, v_gmlp_ln_b, v_gmlp_w_s, v_gmlp_b_s, v_attn_out_norm_g, v_gmlp_out_norm_g, v_w_out, v_b_out, v_ffn2_norm_g, v_ffn2_w_gate, v_ffn2_w_up, v_ffn2_w_down, v_final_norm_g):
    args = dict(locals())
    w = {n: args[n] for n in WEIGHT_ORDER}
    m = {n: args["m_" + n] for n in WEIGHT_ORDER}
    v = {n: args["v_" + n] for n in WEIGHT_ORDER}

    shards = {}
    for key, (pname, transposed) in BIG_NAMES.items():
        mat = w[pname][0]
        shards[key] = (mat.T if transposed else mat).astype(BF16)
    d_model = x.shape[-1]
    flat = lambda g: g.reshape(-1, d_model)
    me = _mesh_place()[3].astype(jnp.int32).reshape(1)
    first = ("wg1", "wu1", "wd1")
    later = (("win", "wout"), ("wg2", "wu2", "wd2"))
    gathers, exchanges, last_token = {}, [], []

    def get_w(keys, after):
        if keys == first:
            order = list(first) + [k for grp in later for k in grp]
            full = dict(zip(order, allgather_rows([shards[k] for k in order], len(first), "allgather_ffn1")))
            started, token = split_start([([shards[k] for k in grp], [full[k] for k in grp]) for grp in later],
                                         gather_slices, (), "gather_start")
            gathers.update(zip(later, started))
            return [flat(full[k]) for k in keys], (token,)
        _, lands = split_wait(gathers[keys], gather_slices, after, "gather_wait_" + keys[0])
        return [flat(land) for land in lands], ()

    def put_g(gb):
        keys = tuple(gb)
        g3 = [gb[k].reshape(N_DEV, -1, d_model) for k in keys]
        lands = [lax.empty((N_DEV - 1,) + g.shape[1:], BF16) for g in g3]
        started, token = split_start([(g3, lands)], exchange_slices, (), "exchange_start_" + keys[0])
        exchanges.append((keys, started[0]))
        last_token[:] = [token]
        return (token,)

    small_started = []

    def put_small(gs, loss):
        pack = pack_small([gs[n] for n in SMALL_NAMES] + [loss[:, :1]])
        started, token = split_start([([pack], [place_own(pack, me, "place_small")])], gather_slices, (), "small_start")
        small_started[:] = started
        return (token,)

    small = {n: w[n] for n in SMALL_NAMES}
    grad_x = local_step(x[0], loss_target[0], small, get_w, put_g, put_small)

    grads, deltas, new_m, new_v = {}, {}, {}, {}
    after = tuple(last_token)
    for keys, started in exchanges:
        g3, lands = split_wait(started, exchange_slices, after, "exchange_wait_" + keys[0])
        for key, own, land in zip(keys, g3, lands):
            pname, transposed = BIG_NAMES[key]
            g = sum_partials(land, own, me, "sum_" + key)
            after = (g,)
            g = g.T if transposed else g
            d_, m_, v_ = adamw(w[pname][0], g, m[pname][0], v[pname][0], "adamw_" + key)
            grads[pname], deltas[pname], new_m[pname], new_v[pname] = g[None], d_[None], m_[None], v_[None]

    shapes = [w[n].shape for n in SMALL_NAMES]
    _, (slots,) = split_wait(small_started[0], gather_slices, after, "small_wait")
    total = sum_slots(slots, "sum_small")
    small_g = unpack_small(total, shapes + [(1, 1)])
    loss_total = small_g[-1].reshape(())
    gpack = pack_small(small_g[:-1])
    d_, m_, v_ = adamw(pack_small([w[n] for n in SMALL_NAMES]), gpack, pack_small([m[n] for n in SMALL_NAMES]),
                       pack_small([v[n] for n in SMALL_NAMES]), "adamw_small")
    for n, g_, dd, mm, vv in zip(SMALL_NAMES, small_g[:-1], unpack_small(d_, shapes), unpack_small(m_, shapes),
                                 unpack_small(v_, shapes)):
        grads[n], deltas[n], new_m[n], new_v[n] = g_, dd, mm, vv

    return (loss_total, grad_x[None], *[grads[n] for n in WEIGHT_ORDER], *[deltas[n] for n in WEIGHT_ORDER],
            *[new_m[n] for n in WEIGHT_ORDER], *[new_v[n] for n in WEIGHT_ORDER])
```

```python
import functools

import jax
import jax.numpy as jnp
from jax import lax
from jax.experimental import pallas as pl
from jax.experimental.pallas import tpu as pltpu

F32 = jnp.float32
BF16 = jnp.bfloat16

N_DEV = 8
N_Q_HEADS = 8
N_KV_HEADS = 2
HEAD_DIM = 64
ATTN_WIDTH = N_Q_HEADS * HEAD_DIM
KV_WIDTH = N_KV_HEADS * HEAD_DIM
QKV_WIDTH = ATTN_WIDTH + 2 * KV_WIDTH
BLOCK = 128
GMLP_GROUPS = 8
GMLP_GROUP_DIM = 64
GMLP_WIDTH = GMLP_GROUPS * GMLP_GROUP_DIM
EPS = 1e-6
FFN_RES = 0.5
ATTN_SCALE = HEAD_DIM ** -0.5
MASK_VALUE = -1e30

ADAM_LR = 0.001
ADAM_B1 = 0.9
ADAM_B2 = 0.999
ADAM_EPS = 1e-08
ADAM_WD = 0.01
ADAM_STEP = 10

V7X_VMEM_BYTES = 64 * 1024 * 1024
VMEM_LIMIT = 56 * 1024 * 1024
LANES = 128

NT = (((1,), (1,)), ((), ()))
TN = (((0,), (0,)), ((), ()))


def _dot(a, b):
    return jnp.dot(a, b, preferred_element_type=F32)


def _dg(a, b, dims):
    return lax.dot_general(a, b, dims, preferred_element_type=F32)


def _params(semantics=None):
    return pltpu.CompilerParams(dimension_semantics=semantics, vmem_limit_bytes=VMEM_LIMIT)


def _resident(shape):
    zeros = (0,) * len(shape)
    return pl.BlockSpec(shape, lambda *_: zeros, pipeline_mode=pl.Buffered(1))


def _drop_deps(body, n_in, n_deps):
    return lambda *refs: body(*refs[:n_in], *refs[n_in + n_deps:])


ANY_SPEC = pl.BlockSpec(memory_space=pl.ANY)


def _token_tile(t):
    return min(t, 512)


def _f_chunk(f):
    for c in (256, 128):
        if f % c == 0:
            return c
    return f


def ffn_fwd(x, gain, wg_t, wu_t, wd, name, deps=()):
    t, d = x.shape
    f = wd.shape[0]
    tm = _token_tile(t)
    fc = _f_chunk(f)

    def body(x_ref, g_ref, wg_ref, wu_ref, wd_ref, xo_ref, a_ref, b_ref):
        xv = x_ref[...]
        r = lax.rsqrt(jnp.mean(xv * xv, axis=-1, keepdims=True) + EPS)
        h = (xv * r * g_ref[...]).astype(BF16)
        acc = jnp.zeros((tm, d), F32)
        for c in range(f // fc):
            sl = slice(c * fc, (c + 1) * fc)
            a = _dg(h, wg_ref[sl, :], NT)
            b = _dg(h, wu_ref[sl, :], NT)
            a_ref[:, sl] = a.astype(BF16)
            b_ref[:, sl] = b.astype(BF16)
            s = (a * jax.nn.sigmoid(a) * b).astype(BF16)
            acc = acc + _dot(s, wd_ref[sl, :])
        xo_ref[...] = xv + FFN_RES * acc

    return pl.pallas_call(
        _drop_deps(body, 5, len(deps)), name=name, grid=(t // tm,),
        in_specs=[pl.BlockSpec((tm, d), lambda i: (i, 0)), _resident((1, d)),
                  _resident((f, d)), _resident((f, d)), _resident((f, d))] + [ANY_SPEC] * len(deps),
        out_specs=[pl.BlockSpec((tm, d), lambda i: (i, 0)), pl.BlockSpec((tm, f), lambda i: (i, 0)),
                   pl.BlockSpec((tm, f), lambda i: (i, 0))],
        out_shape=[jax.ShapeDtypeStruct((t, d), F32), jax.ShapeDtypeStruct((t, f), BF16),
                   jax.ShapeDtypeStruct((t, f), BF16)],
        compiler_params=_params(("arbitrary",)),
    )(x, gain, wg_t, wu_t, wd, *deps)


def _rmsnorm_bwd_rows(xv, r, gain, dh):
    n = xv * r
    dgain = jnp.sum(dh * n, axis=0, keepdims=True)
    dn = dh * gain
    dx = r * (dn - n * jnp.mean(dn * n, axis=-1, keepdims=True))
    return dx, dgain


def ffn_bwd(x, dxo, gain, a, b, wg_t, wu_t, wd, name, deps=()):
    t, d = x.shape
    f = wd.shape[0]
    tm = min(t, 256)
    fc = _f_chunk(f)

    def body(x_ref, dxo_ref, g_ref, a_ref, b_ref, wg_ref, wu_ref, wd_ref,
             dx_ref, da_ref, db_ref, s_ref, h_ref, dg_ref):
        i = pl.program_id(0)
        xv = x_ref[...]
        dxo = dxo_ref[...]
        gain_v = g_ref[...]
        r = lax.rsqrt(jnp.mean(xv * xv, axis=-1, keepdims=True) + EPS)
        h_ref[...] = (xv * r * gain_v).astype(BF16)
        df = (FFN_RES * dxo).astype(BF16)
        dh = jnp.zeros((tm, d), F32)
        for c in range(f // fc):
            sl = slice(c * fc, (c + 1) * fc)
            av = a_ref[:, sl].astype(F32)
            bv = b_ref[:, sl].astype(F32)
            ds = _dg(df, wd_ref[sl, :], NT)
            sig = jax.nn.sigmoid(av)
            sil = av * sig
            s_ref[:, sl] = (sil * bv).astype(BF16)
            da = (ds * bv * (sig * (1.0 + av * (1.0 - sig)))).astype(BF16)
            db = (ds * sil).astype(BF16)
            da_ref[:, sl] = da
            db_ref[:, sl] = db
            dh = dh + _dot(da, wg_ref[sl, :]) + _dot(db, wu_ref[sl, :])
        dx, dgain = _rmsnorm_bwd_rows(xv, r, gain_v, dh)
        dx_ref[...] = dxo + dx

        @pl.when(i == 0)
        def _():
            dg_ref[...] = jnp.zeros_like(dg_ref)

        dg_ref[...] += dgain

    tok = lambda w: pl.BlockSpec((tm, w), lambda i: (i, 0))
    return pl.pallas_call(
        _drop_deps(body, 8, len(deps)), name=name, grid=(t // tm,),
        in_specs=[tok(d), tok(d), _resident((1, d)), tok(f), tok(f),
                  _resident((f, d)), _resident((f, d)), _resident((f, d))] + [ANY_SPEC] * len(deps),
        out_specs=[tok(d), tok(f), tok(f), tok(f), tok(d), pl.BlockSpec((1, d), lambda i: (0, 0))],
        out_shape=[jax.ShapeDtypeStruct((t, d), F32), jax.ShapeDtypeStruct((t, f), BF16),
                   jax.ShapeDtypeStruct((t, f), BF16), jax.ShapeDtypeStruct((t, f), BF16),
                   jax.ShapeDtypeStruct((t, d), BF16), jax.ShapeDtypeStruct((1, d), F32)],
        compiler_params=_params(("arbitrary",)),
    )(x, dxo, gain, a, b, wg_t, wu_t, wd, *deps)


def _row_tile(m, cap):
    if m <= cap:
        return m
    best = None
    for bm in range(LANES, cap + 1, LANES):
        if m % bm == 0:
            best = bm
    return best if best is not None else m


def grad_tn(a, b, scale, name, deps=()):
    t, m = a.shape
    n = b.shape[1]
    bm = _row_tile(m, 1408)
    bk = min(t, 1024)
    nk = t // bk

    def body(a_ref, b_ref, o_ref, acc_ref):
        k = pl.program_id(1)

        @pl.when(k == 0)
        def _():
            acc_ref[...] = jnp.zeros_like(acc_ref)

        acc_ref[...] += _dg(a_ref[...].astype(BF16), b_ref[...].astype(BF16), TN)

        @pl.when(k == nk - 1)
        def _():
            o_ref[...] = (scale * acc_ref[...]).astype(BF16)

    return pl.pallas_call(
        _drop_deps(body, 2, len(deps)), name=name, grid=(m // bm, nk),
        in_specs=[pl.BlockSpec((bk, bm), lambda i, k: (k, i)), pl.BlockSpec((bk, n), lambda i, k: (k, 0))]
        + [ANY_SPEC] * len(deps),
        out_specs=pl.BlockSpec((bm, n), lambda i, k: (i, 0)),
        out_shape=jax.ShapeDtypeStruct((m, n), BF16),
        scratch_shapes=[pltpu.VMEM((bm, n), F32)],
        compiler_params=_params(("arbitrary", "arbitrary")),
    )(a, b, *deps)


def grad_nn(a_list, b, name):
    t, n = b.shape
    ms = [a.shape[0] for a in a_list]
    m = sum(ms)
    bk = min(t, 1024)
    nk = t // bk
    na = len(a_list)

    def body(*refs):
        a_refs, b_ref, o_ref, acc_ref = refs[:na], refs[na], refs[na + 1], refs[na + 2]
        k = pl.program_id(0)

        @pl.when(k == 0)
        def _():
            acc_ref[...] = jnp.zeros_like(acc_ref)

        bv = b_ref[...].astype(BF16)
        off = 0
        for a_ref, mi in zip(a_refs, ms):
            acc_ref[off:off + mi, :] += _dot(a_ref[...].astype(BF16), bv)
            off += mi

        @pl.when(k == nk - 1)
        def _():
            o_ref[...] = acc_ref[...].astype(BF16)

    return pl.pallas_call(
        body, name=name, grid=(nk,),
        in_specs=[pl.BlockSpec((mi, bk), lambda k: (0, k)) for mi in ms] + [pl.BlockSpec((bk, n), lambda k: (k, 0))],
        out_specs=pl.BlockSpec((m, n), lambda k: (0, 0)),
        out_shape=jax.ShapeDtypeStruct((m, n), BF16),
        scratch_shapes=[pltpu.VMEM((m, n), F32)],
        compiler_params=_params(("arbitrary",)),
    )(*a_list, b)


def mix_in_fwd(x, gain, win_t, b_in_col, name):
    t, d = x.shape
    w = win_t.shape[0]
    tm = _token_tile(t)

    def body(x_ref, g_ref, w_ref, bias_ref, qkv_ref, zg_ref):
        xv = x_ref[...]
        r = lax.rsqrt(jnp.mean(xv * xv, axis=-1, keepdims=True) + EPS)
        h = (xv * r * g_ref[...]).astype(BF16)
        qkv_ref[...] = (_dg(w_ref[:QKV_WIDTH, :], h, NT) + bias_ref[:QKV_WIDTH, :]).astype(BF16)
        zg_ref[...] = (_dg(w_ref[QKV_WIDTH:, :], h, NT) + bias_ref[QKV_WIDTH:, :]).astype(BF16)

    return pl.pallas_call(
        body, name=name, grid=(t // tm,),
        in_specs=[pl.BlockSpec((tm, d), lambda i: (i, 0)), _resident((1, d)), _resident((w, d)), _resident((w, 1))],
        out_specs=[pl.BlockSpec((QKV_WIDTH, tm), lambda i: (0, i)), pl.BlockSpec((w - QKV_WIDTH, tm), lambda i: (0, i))],
        out_shape=[jax.ShapeDtypeStruct((QKV_WIDTH, t), BF16), jax.ShapeDtypeStruct((w - QKV_WIDTH, t), BF16)],
        compiler_params=_params(("arbitrary",)),
    )(x, gain, win_t, b_in_col)


def _attn_specs(nb):
    last = nb - 1
    cur = lambda n: jnp.minimum(n, last)
    prev = lambda n: jnp.maximum(jnp.minimum(n, last) - 1, 0)
    k_row = ATTN_WIDTH // KV_WIDTH
    return [
        pl.BlockSpec((ATTN_WIDTH, BLOCK), lambda n: (0, cur(n))),
        pl.BlockSpec((KV_WIDTH, BLOCK), lambda n: (k_row, prev(n))),
        pl.BlockSpec((KV_WIDTH, BLOCK), lambda n: (k_row, cur(n))),
        pl.BlockSpec((KV_WIDTH, BLOCK), lambda n: (k_row + 1, prev(n))),
        pl.BlockSpec((KV_WIDTH, BLOCK), lambda n: (k_row + 1, cur(n))),
    ]


def _attn_probs(q, kp, kc, sink, mask_prev, mask_cur):
    sp = jnp.where(mask_prev, _dg(kp, q, TN) * ATTN_SCALE, MASK_VALUE)
    sc = jnp.where(mask_cur, _dg(kc, q, TN) * ATTN_SCALE, MASK_VALUE)
    m = jnp.maximum(jnp.maximum(jnp.max(sp, axis=0, keepdims=True), jnp.max(sc, axis=0, keepdims=True)), sink)
    pp = jnp.exp(sp - m)
    pc = jnp.exp(sc - m)
    es = jnp.exp(sink - m)
    inv = 1.0 / (jnp.sum(pp, axis=0, keepdims=True) + jnp.sum(pc, axis=0, keepdims=True) + es)
    return pp * inv, pc * inv, es * inv


def _attn_masks(n):
    key = lax.broadcasted_iota(jnp.int32, (BLOCK, BLOCK), 0)
    qry = lax.broadcasted_iota(jnp.int32, (BLOCK, BLOCK), 1)
    return (key > qry) & (n > 0), key <= qry


def attn_fwd(qkv_t, sinks, name):
    t = qkv_t.shape[1]
    nb = t // BLOCK
    rep = N_Q_HEADS // N_KV_HEADS

    def body(sink_ref, q_ref, kp_ref, kc_ref, vp_ref, vc_ref, y_ref):
        n = pl.program_id(0)
        mask_prev, mask_cur = _attn_masks(n)
        for g in range(N_KV_HEADS):
            gs = slice(g * HEAD_DIM, (g + 1) * HEAD_DIM)
            kp, kc, vp, vc = kp_ref[gs, :], kc_ref[gs, :], vp_ref[gs, :], vc_ref[gs, :]
            for rr in range(rep):
                hd = g * rep + rr
                hs = slice(hd * HEAD_DIM, (hd + 1) * HEAD_DIM)
                pp, pc, _ = _attn_probs(q_ref[hs, :], kp, kc, sink_ref[0, hd], mask_prev, mask_cur)
                y_ref[hs, :] = (_dot(vp, pp.astype(BF16)) + _dot(vc, pc.astype(BF16))).astype(BF16)

    return pl.pallas_call(
        body, name=name, grid=(nb,),
        in_specs=[pl.BlockSpec(memory_space=pltpu.SMEM)] + _attn_specs(nb),
        out_specs=pl.BlockSpec((ATTN_WIDTH, BLOCK), lambda n: (0, n)),
        out_shape=jax.ShapeDtypeStruct((ATTN_WIDTH, t), BF16),
        compiler_params=_params(("arbitrary",)),
    )(sinks, qkv_t, qkv_t, qkv_t, qkv_t, qkv_t)


def attn_bwd(qkv_t, dy_t, sinks, name):
    t = qkv_t.shape[1]
    nb = t // BLOCK
    rep = N_Q_HEADS // N_KV_HEADS

    def body(sink_ref, q_ref, kp_ref, kc_ref, vp_ref, vc_ref, do_ref, dq_ref, dk_ref, dv_ref, dsink_ref,
             ck_ref, cv_ref, sacc_ref):
        n = pl.program_id(0)

        @pl.when(n == 0)
        def _():
            sacc_ref[...] = jnp.zeros_like(sacc_ref)

        @pl.when(n < nb)
        def _():
            mask_prev, mask_cur = _attn_masks(n)
            for g in range(N_KV_HEADS):
                gs = slice(g * HEAD_DIM, (g + 1) * HEAD_DIM)
                kp, kc, vp, vc = kp_ref[gs, :], kc_ref[gs, :], vp_ref[gs, :], vc_ref[gs, :]
                dkp = jnp.zeros((HEAD_DIM, BLOCK), F32)
                dkc = jnp.zeros((HEAD_DIM, BLOCK), F32)
                dvp = jnp.zeros((HEAD_DIM, BLOCK), F32)
                dvc = jnp.zeros((HEAD_DIM, BLOCK), F32)
                for rr in range(rep):
                    hd = g * rep + rr
                    hs = slice(hd * HEAD_DIM, (hd + 1) * HEAD_DIM)
                    q = q_ref[hs, :]
                    do = do_ref[hs, :]
                    pp, pc, ps = _attn_probs(q, kp, kc, sink_ref[0, hd], mask_prev, mask_cur)
                    dpp = _dg(vp, do, TN)
                    dpc = _dg(vc, do, TN)
                    delta = jnp.sum(pp * dpp, axis=0, keepdims=True) + jnp.sum(pc * dpc, axis=0, keepdims=True)
                    dsp = (pp * (dpp - delta)).astype(BF16)
                    dsc = (pc * (dpc - delta)).astype(BF16)
                    sacc_ref[hd:hd + 1, :] += -(ps * delta)
                    dq_ref[hs, :] = (ATTN_SCALE * (_dot(kp, dsp) + _dot(kc, dsc))).astype(BF16)
                    dkp = dkp + _dg(q, dsp, NT)
                    dkc = dkc + _dg(q, dsc, NT)
                    dvp = dvp + _dg(do, pp.astype(BF16), NT)
                    dvc = dvc + _dg(do, pc.astype(BF16), NT)

                @pl.when(n > 0)
                def _():
                    dk_ref[gs, :] = (ck_ref[gs, :] + ATTN_SCALE * dkp).astype(BF16)
                    dv_ref[gs, :] = (cv_ref[gs, :] + dvp).astype(BF16)

                ck_ref[gs, :] = ATTN_SCALE * dkc
                cv_ref[gs, :] = dvc

        @pl.when(n == nb)
        def _():
            dk_ref[...] = ck_ref[...].astype(BF16)
            dv_ref[...] = cv_ref[...].astype(BF16)
            dsink_ref[...] = jnp.sum(sacc_ref[...], axis=1, keepdims=True)

    last = nb - 1
    done = lambda n: jnp.maximum(n - 1, 0)
    outs = pl.pallas_call(
        body, name=name, grid=(nb + 1,),
        in_specs=[pl.BlockSpec(memory_space=pltpu.SMEM)] + _attn_specs(nb)
        + [pl.BlockSpec((ATTN_WIDTH, BLOCK), lambda n: (0, jnp.minimum(n, last)))],
        out_specs=[pl.BlockSpec((ATTN_WIDTH, BLOCK), lambda n: (0, jnp.minimum(n, last))),
                   pl.BlockSpec((KV_WIDTH, BLOCK), lambda n: (0, done(n))),
                   pl.BlockSpec((KV_WIDTH, BLOCK), lambda n: (0, done(n))),
                   pl.BlockSpec((N_Q_HEADS, 1), lambda n: (0, 0))],
        out_shape=[jax.ShapeDtypeStruct((ATTN_WIDTH, t), BF16), jax.ShapeDtypeStruct((KV_WIDTH, t), BF16),
                   jax.ShapeDtypeStruct((KV_WIDTH, t), BF16), jax.ShapeDtypeStruct((N_Q_HEADS, 1), F32)],
        scratch_shapes=[pltpu.VMEM((KV_WIDTH, BLOCK), F32), pltpu.VMEM((KV_WIDTH, BLOCK), F32),
                        pltpu.VMEM((N_Q_HEADS, BLOCK), F32)],
        compiler_params=_params(("arbitrary",)),
    )(sinks, qkv_t, qkv_t, qkv_t, qkv_t, qkv_t, dy_t)
    return outs


GELU_C = 0.7978845608028654
GELU_A = 0.044715


def _gelu(x):
    return 0.5 * x * (1.0 + jnp.tanh(GELU_C * (x + GELU_A * x * x * x)))


def _gelu_grad(x):
    th = jnp.tanh(GELU_C * (x + GELU_A * x * x * x))
    return 0.5 * (1.0 + th) + 0.5 * x * (1.0 - th * th) * GELU_C * (1.0 + 3.0 * GELU_A * x * x)


def _gmlp_parts(zg, lng, lnb):
    z = _gelu(zg)
    u = z[:GMLP_WIDTH, :]
    vv = z[GMLP_WIDTH:, :]
    mu = jnp.mean(vv, axis=0, keepdims=True)
    xc = vv - mu
    rstd = lax.rsqrt(jnp.mean(xc * xc, axis=0, keepdims=True) + EPS)
    xhat = xc * rstd
    return u, xhat, rstd, xhat * lng + lnb


def _causal(w):
    row = lax.broadcasted_iota(jnp.int32, (BLOCK, BLOCK), 0)
    col = lax.broadcasted_iota(jnp.int32, (BLOCK, BLOCK), 1)
    return jnp.where(col <= row, w, 0.0)


def gmlp_fwd(zg_t, lng, lnb, w_s, b_s, name):
    t = zg_t.shape[1]
    nb = t // BLOCK

    def body(zg_ref, lng_ref, lnb_ref, w_ref, bs_ref, y_ref):
        u, _, _, vn = _gmlp_parts(zg_ref[...].astype(F32), lng_ref[...], lnb_ref[...])
        for g in range(GMLP_GROUPS):
            gs = slice(g * GMLP_GROUP_DIM, (g + 1) * GMLP_GROUP_DIM)
            wc = _causal(w_ref[g]).astype(BF16)
            mixed = _dg(vn[gs, :].astype(BF16), wc, NT) + bs_ref[g:g + 1, :]
            y_ref[gs, :] = (u[gs, :] * mixed).astype(BF16)

    return pl.pallas_call(
        body, name=name, grid=(nb,),
        in_specs=[pl.BlockSpec((2 * GMLP_WIDTH, BLOCK), lambda n: (0, n)), _resident((GMLP_WIDTH, 1)),
                  _resident((GMLP_WIDTH, 1)), _resident((GMLP_GROUPS, BLOCK, BLOCK)), _resident((GMLP_GROUPS, BLOCK))],
        out_specs=pl.BlockSpec((GMLP_WIDTH, BLOCK), lambda n: (0, n)),
        out_shape=jax.ShapeDtypeStruct((GMLP_WIDTH, t), BF16),
        compiler_params=_params(("arbitrary",)),
    )(zg_t, lng, lnb, w_s, b_s)


def gmlp_bwd(zg_t, dy_t, lng, lnb, w_s, b_s, name):
    t = zg_t.shape[1]
    nb = t // BLOCK

    def body(zg_ref, dy_ref, lng_ref, lnb_ref, w_ref, bs_ref, dzg_ref, dw_ref, dbs_ref, dlng_ref, dlnb_ref,
             gacc_ref, bacc_ref):
        n = pl.program_id(0)

        @pl.when(n == 0)
        def _():
            dw_ref[...] = jnp.zeros_like(dw_ref)
            dbs_ref[...] = jnp.zeros_like(dbs_ref)
            gacc_ref[...] = jnp.zeros_like(gacc_ref)
            bacc_ref[...] = jnp.zeros_like(bacc_ref)

        zg = zg_ref[...].astype(F32)
        lng_v = lng_ref[...]
        u, xhat, rstd, vn = _gmlp_parts(zg, lng_v, lnb_ref[...])
        dy = dy_ref[...].astype(F32)
        dvn_parts = []
        for g in range(GMLP_GROUPS):
            gs = slice(g * GMLP_GROUP_DIM, (g + 1) * GMLP_GROUP_DIM)
            wc = _causal(w_ref[g]).astype(BF16)
            vn_g = vn[gs, :].astype(BF16)
            mixed = _dg(vn_g, wc, NT) + bs_ref[g:g + 1, :]
            dy_g = dy[gs, :]
            dmixed = dy_g * u[gs, :]
            dmixed_b = dmixed.astype(BF16)
            dzg_ref[gs, :] = (dy_g * mixed * _gelu_grad(zg[gs, :])).astype(BF16)
            dw_ref[g] += _causal(_dg(dmixed_b, vn_g, TN))
            dbs_ref[g:g + 1, :] += jnp.sum(dmixed, axis=0, keepdims=True)
            dvn_parts.append(_dot(dmixed_b, wc))
        dvn = jnp.concatenate(dvn_parts, axis=0)
        gacc_ref[...] += dvn * xhat
        bacc_ref[...] += dvn
        dxhat = dvn * lng_v
        m1 = jnp.mean(dxhat, axis=0, keepdims=True)
        m2 = jnp.mean(dxhat * xhat, axis=0, keepdims=True)
        dvv = rstd * (dxhat - m1 - xhat * m2)
        dzg_ref[GMLP_WIDTH:, :] = (dvv * _gelu_grad(zg[GMLP_WIDTH:, :])).astype(BF16)

        @pl.when(n == nb - 1)
        def _():
            dlng_ref[...] = jnp.sum(gacc_ref[...], axis=1, keepdims=True)
            dlnb_ref[...] = jnp.sum(bacc_ref[...], axis=1, keepdims=True)

    const2 = lambda n: (0, 0)
    return pl.pallas_call(
        body, name=name, grid=(nb,),
        in_specs=[pl.BlockSpec((2 * GMLP_WIDTH, BLOCK), lambda n: (0, n)), pl.BlockSpec((GMLP_WIDTH, BLOCK), lambda n: (0, n)),
                  _resident((GMLP_WIDTH, 1)), _resident((GMLP_WIDTH, 1)),
                  _resident((GMLP_GROUPS, BLOCK, BLOCK)), _resident((GMLP_GROUPS, BLOCK))],
        out_specs=[pl.BlockSpec((2 * GMLP_WIDTH, BLOCK), lambda n: (0, n)),
                   pl.BlockSpec((GMLP_GROUPS, BLOCK, BLOCK), lambda n: (0, 0, 0)),
                   pl.BlockSpec((GMLP_GROUPS, BLOCK), const2),
                   pl.BlockSpec((GMLP_WIDTH, 1), const2), pl.BlockSpec((GMLP_WIDTH, 1), const2)],
        out_shape=[jax.ShapeDtypeStruct((2 * GMLP_WIDTH, t), BF16),
                   jax.ShapeDtypeStruct((GMLP_GROUPS, BLOCK, BLOCK), F32),
                   jax.ShapeDtypeStruct((GMLP_GROUPS, BLOCK), F32),
                   jax.ShapeDtypeStruct((GMLP_WIDTH, 1), F32), jax.ShapeDtypeStruct((GMLP_WIDTH, 1), F32)],
        scratch_shapes=[pltpu.VMEM((GMLP_WIDTH, BLOCK), F32), pltpu.VMEM((GMLP_WIDTH, BLOCK), F32)],
        compiler_params=_params(("arbitrary",)),
    )(zg_t, dy_t, lng, lnb, w_s, b_s)


def _rmsnorm_cols(y, gain_col):
    r = lax.rsqrt(jnp.mean(y * y, axis=0, keepdims=True) + EPS)
    n = y * r
    return n, r, n * gain_col


def mix_out_fwd(x, ya_t, yg_t, gao, ggo, wout, b_out, name):
    t, d = x.shape
    tm = _token_tile(t)

    def body(x_ref, ya_ref, yg_ref, gao_ref, ggo_ref, w_ref, b_ref, o_ref):
        _, _, ya = _rmsnorm_cols(ya_ref[...].astype(F32), gao_ref[...])
        _, _, yg = _rmsnorm_cols(yg_ref[...].astype(F32), ggo_ref[...])
        o_ref[...] = (x_ref[...] + _dg(ya.astype(BF16), w_ref[:ATTN_WIDTH, :], TN)
                      + _dg(yg.astype(BF16), w_ref[ATTN_WIDTH:, :], TN) + b_ref[...])

    return pl.pallas_call(
        body, name=name, grid=(t // tm,),
        in_specs=[pl.BlockSpec((tm, d), lambda i: (i, 0)), pl.BlockSpec((ATTN_WIDTH, tm), lambda i: (0, i)),
                  pl.BlockSpec((GMLP_WIDTH, tm), lambda i: (0, i)), _resident((ATTN_WIDTH, 1)), _resident((GMLP_WIDTH, 1)),
                  _resident((ATTN_WIDTH + GMLP_WIDTH, d)), _resident((1, d))],
        out_specs=pl.BlockSpec((tm, d), lambda i: (i, 0)),
        out_shape=jax.ShapeDtypeStruct((t, d), F32),
        compiler_params=_params(("arbitrary",)),
    )(x, ya_t, yg_t, gao, ggo, wout, b_out)


def _lane_fold(v):
    out = v[:, :LANES]
    for j in range(1, v.shape[1] // LANES):
        out = out + v[:, j * LANES:(j + 1) * LANES]
    return out


def mix_out_bwd(dx, ya_t, yg_t, gao, ggo, wout, name, deps=()):
    t, d = dx.shape
    tm = _token_tile(t)
    nt = t // tm

    def body(dx_ref, ya_ref, yg_ref, gao_ref, ggo_ref, w_ref, dya_ref, dyg_ref, y_ref, dgao_ref, dggo_ref, db_ref,
             acc_ref):
        i = pl.program_id(0)

        @pl.when(i == 0)
        def _():
            acc_ref[...] = jnp.zeros_like(acc_ref)
            db_ref[...] = jnp.zeros_like(db_ref)

        dxv = dx_ref[...]
        db_ref[...] += jnp.sum(dxv, axis=0, keepdims=True)
        dxb = dxv.astype(BF16)
        for part, (src_ref, gain_ref, dst_ref) in enumerate(((ya_ref, gao_ref, dya_ref), (yg_ref, ggo_ref, dyg_ref))):
            rows = slice(part * ATTN_WIDTH, (part + 1) * ATTN_WIDTH)
            gain = gain_ref[...]
            nrm, r, yn = _rmsnorm_cols(src_ref[...].astype(F32), gain)
            y_ref[rows, :] = yn.astype(BF16)
            dy = _dg(w_ref[rows, :], dxb, NT)
            acc_ref[rows, :] += _lane_fold(dy * nrm)
            dn = dy * gain
            dst_ref[...] = (r * (dn - nrm * jnp.mean(dn * nrm, axis=0, keepdims=True))).astype(BF16)

        @pl.when(i == nt - 1)
        def _():
            dgao_ref[...] = jnp.sum(acc_ref[:ATTN_WIDTH, :], axis=1, keepdims=True)
            dggo_ref[...] = jnp.sum(acc_ref[ATTN_WIDTH:, :], axis=1, keepdims=True)

    const2 = lambda i: (0, 0)
    feat = lambda w: pl.BlockSpec((w, tm), lambda i: (0, i))
    return pl.pallas_call(
        _drop_deps(body, 6, len(deps)), name=name, grid=(nt,),
        in_specs=[pl.BlockSpec((tm, d), lambda i: (i, 0)), feat(ATTN_WIDTH), feat(GMLP_WIDTH),
                  _resident((ATTN_WIDTH, 1)), _resident((GMLP_WIDTH, 1)), _resident((ATTN_WIDTH + GMLP_WIDTH, d))]
        + [ANY_SPEC] * len(deps),
        out_specs=[feat(ATTN_WIDTH), feat(GMLP_WIDTH), feat(ATTN_WIDTH + GMLP_WIDTH),
                   pl.BlockSpec((ATTN_WIDTH, 1), const2), pl.BlockSpec((GMLP_WIDTH, 1), const2),
                   pl.BlockSpec((1, d), const2)],
        out_shape=[jax.ShapeDtypeStruct((ATTN_WIDTH, t), BF16), jax.ShapeDtypeStruct((GMLP_WIDTH, t), BF16),
                   jax.ShapeDtypeStruct((ATTN_WIDTH + GMLP_WIDTH, t), BF16),
                   jax.ShapeDtypeStruct((ATTN_WIDTH, 1), F32), jax.ShapeDtypeStruct((GMLP_WIDTH, 1), F32),
                   jax.ShapeDtypeStruct((1, d), F32)],
        scratch_shapes=[pltpu.VMEM((ATTN_WIDTH + GMLP_WIDTH, LANES), F32)],
        compiler_params=_params(("arbitrary",)),
    )(dx, ya_t, yg_t, gao, ggo, wout, *deps)


def mix_in_bwd(x, dxo, gain, dq_t, dk_t, dv_t, dzg_t, win_t, name):
    t, d = x.shape
    w = win_t.shape[0]
    tm = _token_tile(t)
    nt = t // tm
    parts = ((0, ATTN_WIDTH), (ATTN_WIDTH, KV_WIDTH), (ATTN_WIDTH + KV_WIDTH, KV_WIDTH), (QKV_WIDTH, w - QKV_WIDTH))

    def body(x_ref, dxo_ref, g_ref, dq_ref, dk_ref, dv_ref, dzg_ref, w_ref, dx_ref, h_ref, dg_ref, db_ref, acc_ref):
        i = pl.program_id(0)

        @pl.when(i == 0)
        def _():
            acc_ref[...] = jnp.zeros_like(acc_ref)
            dg_ref[...] = jnp.zeros_like(dg_ref)

        xv = x_ref[...]
        gain_v = g_ref[...]
        r = lax.rsqrt(jnp.mean(xv * xv, axis=-1, keepdims=True) + EPS)
        h_ref[...] = (xv * r * gain_v).astype(BF16)
        dh = jnp.zeros((tm, d), F32)
        for (off, size), src_ref in zip(parts, (dq_ref, dk_ref, dv_ref, dzg_ref)):
            dp = src_ref[...]
            acc_ref[off:off + size, :] += _lane_fold(dp.astype(F32))
            dh = dh + _dg(dp, w_ref[off:off + size, :], TN)
        dx, dgain = _rmsnorm_bwd_rows(xv, r, gain_v, dh)
        dx_ref[...] = dxo_ref[...] + dx
        dg_ref[...] += dgain

        @pl.when(i == nt - 1)
        def _():
            db_ref[...] = jnp.sum(acc_ref[...], axis=1, keepdims=True)

    const2 = lambda i: (0, 0)
    tok = lambda: pl.BlockSpec((tm, d), lambda i: (i, 0))
    feat = lambda rows: pl.BlockSpec((rows, tm), lambda i: (0, i))
    return pl.pallas_call(
        body, name=name, grid=(nt,),
        in_specs=[tok(), tok(), _resident((1, d)), feat(ATTN_WIDTH), feat(KV_WIDTH), feat(KV_WIDTH),
                  feat(w - QKV_WIDTH), _resident((w, d))],
        out_specs=[tok(), tok(), pl.BlockSpec((1, d), const2), pl.BlockSpec((w, 1), const2)],
        out_shape=[jax.ShapeDtypeStruct((t, d), F32), jax.ShapeDtypeStruct((t, d), BF16),
                   jax.ShapeDtypeStruct((1, d), F32), jax.ShapeDtypeStruct((w, 1), F32)],
        scratch_shapes=[pltpu.VMEM((w, LANES), F32)],
        compiler_params=_params(("arbitrary",)),
    )(x, dxo, gain, dq_t, dk_t, dv_t, dzg_t, win_t)


def final_bwd(x, target, gain, name):
    t, d = x.shape
    tm = _token_tile(t)

    def body(x_ref, t_ref, g_ref, dx_ref, dg_ref, loss_ref):
        i = pl.program_id(0)

        @pl.when(i == 0)
        def _():
            dg_ref[...] = jnp.zeros_like(dg_ref)
            loss_ref[...] = jnp.zeros_like(loss_ref)

        xv = x_ref[...]
        gain_v = g_ref[...]
        r = lax.rsqrt(jnp.mean(xv * xv, axis=-1, keepdims=True) + EPS)
        err = xv * r * gain_v - t_ref[...]
        row_loss = jnp.mean(err * err, axis=-1, keepdims=True)
        loss_ref[...] += 0.5 * jnp.sum(row_loss, axis=0, keepdims=True)
        dx, dgain = _rmsnorm_bwd_rows(xv, r, gain_v, err * (1.0 / d))
        dx_ref[...] = dx
        dg_ref[...] += dgain

    const2 = lambda i: (0, 0)
    return pl.pallas_call(
        body, name=name, grid=(t // tm,),
        in_specs=[pl.BlockSpec((tm, d), lambda i: (i, 0)), pl.BlockSpec((tm, d), lambda i: (i, 0)), _resident((1, d))],
        out_specs=[pl.BlockSpec((tm, d), lambda i: (i, 0)), pl.BlockSpec((1, d), const2), pl.BlockSpec((1, LANES), const2)],
        out_shape=[jax.ShapeDtypeStruct((t, d), F32), jax.ShapeDtypeStruct((1, d), F32),
                   jax.ShapeDtypeStruct((1, LANES), F32)],
        compiler_params=_params(("arbitrary",)),
    )(x, target, gain)


def _adamw_math(w, g, m, v):
    m = ADAM_B1 * m + (1.0 - ADAM_B1) * g
    v = ADAM_B2 * v + (1.0 - ADAM_B2) * (g * g)
    m_hat = m / (1.0 - ADAM_B1 ** ADAM_STEP)
    v_hat = v / (1.0 - ADAM_B2 ** ADAM_STEP)
    delta = -ADAM_LR * (m_hat / (jnp.sqrt(v_hat) + ADAM_EPS) + ADAM_WD * w)
    return delta, m, v


def adamw(w, g, m, v, name):
    rows, cols = w.shape
    tr = rows
    if rows * cols > 256 * 1024:
        for cand in (256, 128, 64, 32, 16, 8):
            if rows % cand == 0:
                tr = cand
                break

    def body(w_ref, g_ref, m_ref, v_ref, d_ref, mo_ref, vo_ref):
        delta, mn, vn = _adamw_math(w_ref[...], g_ref[...], m_ref[...], v_ref[...])
        d_ref[...] = delta
        mo_ref[...] = mn
        vo_ref[...] = vn

    spec = pl.BlockSpec((tr, cols), lambda i: (i, 0))
    shape = jax.ShapeDtypeStruct((rows, cols), F32)
    return pl.pallas_call(
        body, name=name, grid=(rows // tr,),
        in_specs=[spec] * 4, out_specs=[spec] * 3, out_shape=[shape] * 3,
        compiler_params=_params(("arbitrary",)),
    )(w, g, m, v)


def sum_partials(landing, grad3d, me, name, deps=()):
    nd, rows, cols = landing.shape
    tr = rows
    for cand in (128, 112, 88, 64, 32, 16):
        if rows % cand == 0:
            tr = cand
            break

    def body(me_ref, l_ref, own_ref, o_ref):
        acc = own_ref[...].astype(F32)
        for j in range(nd):
            acc = acc + l_ref[j].astype(F32)
        o_ref[...] = acc

    return pl.pallas_call(
        _drop_deps(body, 3, len(deps)), name=name,
        grid_spec=pltpu.PrefetchScalarGridSpec(
            num_scalar_prefetch=1, grid=(rows // tr,),
            in_specs=[pl.BlockSpec((nd, tr, cols), lambda i, me_ref: (0, i, 0)),
                      pl.BlockSpec((None, tr, cols), lambda i, me_ref: (me_ref[0], i, 0))] + [ANY_SPEC] * len(deps),
            out_specs=pl.BlockSpec((tr, cols), lambda i, me_ref: (i, 0))),
        out_shape=jax.ShapeDtypeStruct((rows, cols), F32),
        compiler_params=_params(("arbitrary",)),
    )(me, landing, grad3d, *deps)


def _mesh_place():
    x, y, c = lax.axis_index("x"), lax.axis_index("y"), lax.axis_index("c")
    return x, y, c, 4 * x + 2 * y + c


def _peer(x, y, c, k):
    return (x ^ ((k >> 2) & 1), y ^ ((k >> 1) & 1), c ^ (k & 1))


def allgather_rows(shards, n_gather, name):
    na = len(shards)

    def body(*refs):
        in_refs, out_refs = refs[:na], refs[na:2 * na]
        send_sems, recv_sems, local_sems = refs[2 * na:]
        x, y, c, me = _mesh_place()
        local = [pltpu.make_async_copy(in_refs[a], out_refs[a].at[me], local_sems.at[a]) for a in range(na)]
        for cp in local:
            cp.start()
        def copy(sem, a, origin, src, k):
            return pltpu.make_async_remote_copy(
                src_ref=src, dst_ref=out_refs[a].at[origin], send_sem=send_sems.at[sem, a],
                recv_sem=recv_sems.at[sem, a], device_id=_peer(x, y, c, k), device_id_type=pl.DeviceIdType.MESH)

        ka = 2 + 2 * c
        kb = 6 - ka
        sends = []

        def send(sem, a, origin, src, k):
            cp = copy(sem, a, origin, src, k)
            cp.start()
            sends.append(cp)

        def landed(a, origin):
            return out_refs[a].at[origin]

        for a in range(n_gather):
            send(1, a, me, in_refs[a], ka)
            send(2, a, me, in_refs[a], kb)
            send(0, a, me, in_refs[a], 1)
        for a in range(n_gather):
            copy(1, a, me ^ ka, in_refs[a], ka).wait_recv()
            send(3, a, me ^ ka, landed(a, me ^ ka), kb)
            send(4, a, me ^ ka, landed(a, me ^ ka), 1)
            copy(2, a, me ^ kb, in_refs[a], kb).wait_recv()
            send(5, a, me ^ kb, landed(a, me ^ kb), 1)
        for a in range(n_gather):
            copy(3, a, me ^ 6, in_refs[a], kb).wait_recv()
            send(6, a, me ^ 6, landed(a, me ^ 6), 1)
        for a in range(n_gather):
            copy(0, a, me ^ 1, in_refs[a], 1).wait_recv()
            copy(4, a, me ^ 1 ^ kb, in_refs[a], 1).wait_recv()
            copy(5, a, me ^ 1 ^ ka, in_refs[a], 1).wait_recv()
            copy(6, a, me ^ 1 ^ 6, in_refs[a], 1).wait_recv()
        for cp in sends:
            cp.wait_send()
        for cp in local:
            cp.wait()

    return pl.pallas_call(
        body, name=name,
        in_specs=[ANY_SPEC] * na, out_specs=[ANY_SPEC] * na,
        out_shape=[jax.ShapeDtypeStruct((N_DEV,) + s.shape, s.dtype) for s in shards],
        scratch_shapes=[pltpu.SemaphoreType.DMA((N_DEV - 1, n_gather)), pltpu.SemaphoreType.DMA((N_DEV - 1, n_gather)),
                        pltpu.SemaphoreType.DMA((na,))],
    )(*shards)


HBM_SPEC = pl.BlockSpec(memory_space=pltpu.HBM)
SEM_SPEC = pl.BlockSpec(memory_space=pltpu.SEMAPHORE)
SIDE_EFFECT = pltpu.SideEffectType.DATAFLOW_SIDE_EFFECTING


def _hbm(v):
    return pltpu.with_memory_space_constraint(v, pltpu.HBM)


def gather_slices(src, land, me, k):
    return src, land.at[me], land.at[me ^ k]


def exchange_slices(src, land, me, k):
    return src.at[me ^ k], land.at[k - 1], land.at[k - 1]


def split_start(groups, slices, deps, name):
    sizes = [len(srcs) for srcs, _ in groups]
    flat_src = [s for srcs, _ in groups for s in srcs]
    flat_land = [l for _, lands in groups for l in lands]
    na, ng, nd = len(flat_src), len(groups), len(deps)

    def body(*refs):
        src_refs, land_refs = refs[:na], refs[na:2 * na]
        sem_refs = refs[2 * na + nd:2 * na + nd + 2 * ng]
        token_ref = refs[-1]
        x, y, c, me = _mesh_place()
        a0 = 0
        for gi, size in enumerate(sizes):
            send_sems, recv_sems = sem_refs[2 * gi], sem_refs[2 * gi + 1]
            for k in range(1, N_DEV):
                for j in range(size):
                    src, dst, _ = slices(src_refs[a0 + j], land_refs[a0 + j], me, k)
                    pltpu.make_async_remote_copy(
                        src_ref=src, dst_ref=dst, send_sem=send_sems.at[(k - 1) * size + j],
                        recv_sem=recv_sems.at[(k - 1) * size + j],
                        device_id=_peer(x, y, c, k), device_id_type=pl.DeviceIdType.MESH).start()
            a0 += size
        token_ref[...] = jnp.zeros_like(token_ref)

    sems = [pltpu.SemaphoreType.DMA(((N_DEV - 1) * size,)) for size in sizes for _ in range(2)]
    thru = [pltpu.HBM(v.shape, v.dtype) for v in flat_src + flat_land]
    outs = pl.pallas_call(
        body, name=name,
        in_specs=[HBM_SPEC] * (2 * na) + [ANY_SPEC] * nd,
        out_specs=[SEM_SPEC] * (2 * ng) + [HBM_SPEC] * (2 * na) + [pl.BlockSpec(memory_space=pltpu.VMEM)],
        out_shape=sems + thru + [jax.ShapeDtypeStruct((8, LANES), F32)],
        input_output_aliases={i: 2 * ng + i for i in range(2 * na)},
        compiler_params=pltpu.CompilerParams(has_side_effects=SIDE_EFFECT),
    )(*[_hbm(v) for v in flat_src + flat_land], *deps)
    started, a0 = [], 0
    for gi, size in enumerate(sizes):
        srcs = outs[2 * ng + a0:2 * ng + a0 + size]
        lands = outs[2 * ng + na + a0:2 * ng + na + a0 + size]
        started.append((outs[2 * gi], outs[2 * gi + 1], list(srcs), list(lands)))
        a0 += size
    return started, outs[-1]


def split_wait(started, slices, after, name):
    send_sems, recv_sems, srcs, lands = started
    na = len(srcs)

    def body(*refs):
        src_refs, land_refs = refs[:na], refs[na:2 * na]
        send_ref, recv_ref = refs[2 * na], refs[2 * na + 1]
        x, y, c, me = _mesh_place()
        for k in range(1, N_DEV):
            for j in range(na):
                src, _, got = slices(src_refs[j], land_refs[j], me, k)
                cp = pltpu.make_async_remote_copy(
                    src_ref=src, dst_ref=got, send_sem=send_ref.at[(k - 1) * na + j], recv_sem=recv_ref.at[(k - 1) * na + j],
                    device_id=_peer(x, y, c, k), device_id_type=pl.DeviceIdType.MESH)
                cp.wait_send()
                cp.wait_recv()

    outs = pl.pallas_call(
        body, name=name,
        in_specs=[HBM_SPEC] * (2 * na) + [SEM_SPEC, SEM_SPEC] + [ANY_SPEC] * len(after),
        out_specs=[HBM_SPEC] * (2 * na),
        out_shape=[pltpu.HBM(v.shape, v.dtype) for v in srcs + lands],
        input_output_aliases={i: i for i in range(2 * na)},
        compiler_params=pltpu.CompilerParams(has_side_effects=SIDE_EFFECT),
    )(*srcs, *lands, send_sems, recv_sems, *after)
    return list(outs[:na]), list(outs[na:])


def place_own(pack, me, name):
    rows, cols = pack.shape

    def body(me_ref, p_ref, o_ref):
        o_ref[...] = p_ref[...]

    return pl.pallas_call(
        body, name=name,
        grid_spec=pltpu.PrefetchScalarGridSpec(
            num_scalar_prefetch=1, grid=(1,),
            in_specs=[pl.BlockSpec((rows, cols), lambda i, me_ref: (0, 0))],
            out_specs=pl.BlockSpec((None, rows, cols), lambda i, me_ref: (me_ref[0], 0, 0))),
        out_shape=jax.ShapeDtypeStruct((N_DEV, rows, cols), F32),
        compiler_params=_params(("arbitrary",)),
    )(me, pack)


def sum_slots(slots, name):
    nd, rows, cols = slots.shape

    def body(s_ref, o_ref):
        acc = s_ref[0]
        for j in range(1, nd):
            acc = acc + s_ref[j]
        o_ref[...] = acc

    return pl.pallas_call(
        body, name=name, grid=(1,),
        in_specs=[pl.BlockSpec((nd, rows, cols), lambda i: (0, 0, 0))],
        out_specs=pl.BlockSpec((rows, cols), lambda i: (0, 0)),
        out_shape=jax.ShapeDtypeStruct((rows, cols), F32),
        compiler_params=_params(("arbitrary",)),
    )(slots)


def _pack_rows(size):
    rows = -(-size // LANES)
    return -(-rows // 8) * 8


def pack_small(parts):
    out = []
    for p in parts:
        flat = p.reshape(-1).astype(F32)
        rows = _pack_rows(flat.shape[0])
        out.append(jnp.pad(flat, (0, rows * LANES - flat.shape[0])).reshape(rows, LANES))
    return jnp.concatenate(out, axis=0)


def unpack_small(pack, shapes):
    out, off = [], 0
    for shp in shapes:
        size = 1
        for s in shp:
            size *= s
        rows = _pack_rows(size)
        out.append(pack[off:off + rows].reshape(-1)[:size].reshape(shp))
        off += rows
    return out


def local_step(x, target, small, get_w, put_g, put_small):
    col = lambda v: v.reshape(-1, 1)
    (wg1, wu1, wd1), deps = get_w(("wg1", "wu1", "wd1"), ())
    x1, a1, b1 = ffn_fwd(x, small["ffn1_norm_g"], wg1, wu1, wd1, "ffn1_fwd", deps)
    (win, wout), _ = get_w(("win", "wout"), (x1,))
    qkv_t, zg_t = mix_in_fwd(x1, small["mix_norm_g"], win, col(small["b_in"]), "mix_in_fwd")
    ya_t = attn_fwd(qkv_t, small["attn_sinks"], "attn_fwd")
    lng, lnb = col(small["gmlp_ln_g"]), col(small["gmlp_ln_b"])
    w_s, b_s = small["gmlp_w_s"][0], small["gmlp_b_s"][0]
    yg_t = gmlp_fwd(zg_t, lng, lnb, w_s, b_s, "gmlp_fwd")
    gao, ggo = col(small["attn_out_norm_g"]), col(small["gmlp_out_norm_g"])
    x2 = mix_out_fwd(x1, ya_t, yg_t, gao, ggo, wout, small["b_out"], "mix_out_fwd")
    (wg2, wu2, wd2), _ = get_w(("wg2", "wu2", "wd2"), (x2,))
    x3, a2, b2 = ffn_fwd(x2, small["ffn2_norm_g"], wg2, wu2, wd2, "ffn2_fwd")

    gs = {}
    dx3, gs["final_norm_g"], loss = final_bwd(x3, target, small["final_norm_g"].reshape(1, -1), "final_bwd")

    dx2, da, db, s, h, gs["ffn2_norm_g"] = ffn_bwd(x2, dx3, small["ffn2_norm_g"], a2, b2, wg2, wu2, wd2, "ffn2_bwd")
    deps = put_g({"wg2": grad_tn(da, h, 1.0, "ffn2_dwg"), "wu2": grad_tn(db, h, 1.0, "ffn2_dwu"),
                  "wd2": grad_tn(s, dx3, FFN_RES, "ffn2_dwd")})

    dya_t, dyg_t, y_t, dgao, dggo, gs["b_out"] = mix_out_bwd(dx2, ya_t, yg_t, gao, ggo, wout, "mix_out_bwd", deps)
    gs["attn_out_norm_g"], gs["gmlp_out_norm_g"] = dgao.reshape(1, -1), dggo.reshape(1, -1)
    g_wout = grad_nn([y_t], dx2, "dwout")
    dzg_t, dws, dbs, dlng, dlnb = gmlp_bwd(zg_t, dyg_t, lng, lnb, w_s, b_s, "gmlp_bwd")
    gs["gmlp_w_s"], gs["gmlp_b_s"] = dws[None], dbs[None]
    gs["gmlp_ln_g"], gs["gmlp_ln_b"] = dlng.reshape(1, -1), dlnb.reshape(1, -1)
    dq_t, dk_t, dv_t, dsinks = attn_bwd(qkv_t, dya_t, small["attn_sinks"], "attn_bwd")
    gs["attn_sinks"] = dsinks.reshape(1, -1)
    dx1, h2, gs["mix_norm_g"], dbin = mix_in_bwd(x1, dx2, small["mix_norm_g"], dq_t, dk_t, dv_t, dzg_t, win,
                                                 "mix_in_bwd")
    gs["b_in"] = dbin.reshape(1, -1)
    deps = put_g({"wout": g_wout, "win": grad_nn([dq_t, dk_t, dv_t, dzg_t], h2, "dwin")})

    dx0, da, db, s, h, gs["ffn1_norm_g"] = ffn_bwd(x, dx1, small["ffn1_norm_g"], a1, b1, wg1, wu1, wd1, "ffn1_bwd",
                                                   deps)
    deps = put_small(gs, loss)
    deps = put_g({"wg1": grad_tn(da, h, 1.0, "ffn1_dwg", deps)})
    deps = put_g({"wu1": grad_tn(db, h, 1.0, "ffn1_dwu", deps)})
    put_g({"wd1": grad_tn(s, dx1, FFN_RES, "ffn1_dwd", deps)})
    return dx0


SMALL_NAMES = ["ffn1_norm_g", "mix_norm_g", "b_in", "attn_sinks", "gmlp_ln_g", "gmlp_ln_b", "gmlp_w_s", "gmlp_b_s",
               "attn_out_norm_g", "gmlp_out_norm_g", "b_out", "ffn2_norm_g", "final_norm_g"]
BIG_NAMES = {"wg1": ("ffn1_w_gate", True), "wu1": ("ffn1_w_up", True), "wd1": ("ffn1_w_down", False),
             "win": ("w_in", True), "wout": ("w_out", False),
             "wg2": ("ffn2_w_gate", True), "wu2": ("ffn2_w_up", True), "wd2": ("ffn2_w_down", False)}
WEIGHT_ORDER = ["ffn1_norm_g", "ffn1_w_gate", "ffn1_w_up", "ffn1_w_down", "mix_norm_g", "w_in", "b_in", "attn_sinks",
                "gmlp_ln_g", "gmlp_ln_b", "gmlp_w_s", "gmlp_b_s", "attn_out_norm_g", "gmlp_out_norm_g", "w_out",
                "b_out", "ffn2_norm_g", "ffn2_w_gate", "ffn2_w_up", "ffn2_w_down", "final_norm_g"]


def kernel(x, ffn1_norm_g, ffn1_w_gate, ffn1_w_up, ffn1_w_down, mix_norm_g, w_in, b_in, attn_sinks, gmlp_ln_g, gmlp_ln_b, gmlp_w_s, gmlp_b_s, attn_out_norm_g, gmlp_out_norm_g, w_out, b_out, ffn2_norm_g, ffn2_w_gate, ffn2_w_up, ffn2_w_down, final_norm_g, loss_target, m_ffn1_norm_g, m_ffn1_w_gate, m_ffn1_w_up, m_ffn1_w_down, m_mix_norm_g, m_w_in, m_b_in, m_attn_sinks, m_gmlp_ln_g, m_gmlp_ln_b, m_gmlp_w_s, m_gmlp_b_s, m_attn_out_norm_g, m_gmlp_out_norm_g, m_w_out, m_b_out, m_ffn2_norm_g, m_ffn2_w_gate, m_ffn2_w_up, m_ffn2_w_down, m_final_norm_g, v_ffn1_norm_g, v_ffn1_w_gate, v_ffn1_w_up, v_ffn1_w_down, v_mix_norm_g, v_w_in, v_b_in, v_attn_sinks, v_gmlp_ln_g, v_gmlp_ln_b, v_gmlp_w_s, v_gmlp_b_s, v_attn_out_norm_g, v_gmlp_out_norm_g, v_w_out, v_b_out, v_ffn2_norm_g, v_ffn2_w_gate, v_ffn2_w_up, v_ffn2_w_down, v_final_norm_g):
    args = dict(locals())
    w = {n: args[n] for n in WEIGHT_ORDER}
    m = {n: args["m_" + n] for n in WEIGHT_ORDER}
    v = {n: args["v_" + n] for n in WEIGHT_ORDER}

    shards = {}
    for key, (pname, transposed) in BIG_NAMES.items():
        mat = w[pname][0]
        shards[key] = (mat.T if transposed else mat).astype(BF16)
    d_model = x.shape[-1]
    flat = lambda g: g.reshape(-1, d_model)
    me = _mesh_place()[3].astype(jnp.int32).reshape(1)
    first = ("wg1", "wu1", "wd1")
    later = (("win", "wout"), ("wg2", "wu2", "wd2"))
    gathers, exchanges, last_token = {}, [], []

    def get_w(keys, after):
        if keys == first:
            order = list(first) + [k for grp in later for k in grp]
            full = dict(zip(order, allgather_rows([shards[k] for k in order], len(first), "allgather_ffn1")))
            started, token = split_start([([shards[k] for k in grp], [full[k] for k in grp]) for grp in later],
                                         gather_slices, (), "gather_start")
            gathers.update(zip(later, started))
            return [flat(full[k]) for k in keys], (token,)
        _, lands = split_wait(gathers[keys], gather_slices, after, "gather_wait_" + keys[0])
        return [flat(land) for land in lands], ()

    def put_g(gb):
        keys = tuple(gb)
        g3 = [gb[k].reshape(N_DEV, -1, d_model) for k in keys]
        lands = [lax.empty((N_DEV - 1,) + g.shape[1:], BF16) for g in g3]
        started, token = split_start([(g3, lands)], exchange_slices, (), "exchange_start_" + keys[0])
        exchanges.append((keys, started[0]))
        last_token[:] = [token]
        return (token,)

    small_started = []

    def put_small(gs, loss):
        pack = pack_small([gs[n] for n in SMALL_NAMES] + [loss[:, :1]])
        started, token = split_start([([pack], [place_own(pack, me, "place_small")])], gather_slices, (), "small_start")
        small_started[:] = started
        return (token,)

    small = {n: w[n] for n in SMALL_NAMES}
    grad_x = local_step(x[0], loss_target[0], small, get_w, put_g, put_small)

    grads, deltas, new_m, new_v = {}, {}, {}, {}
    after = tuple(last_token)
    for keys, started in exchanges:
        g3, lands = split_wait(started, exchange_slices, after, "exchange_wait_" + keys[0])
        for key, own, land in zip(keys, g3, lands):
            pname, transposed = BIG_NAMES[key]
            g = sum_partials(land, own, me, "sum_" + key)
            after = (g,)
            g = g.T if transposed else g
            d_, m_, v_ = adamw(w[pname][0], g, m[pname][0], v[pname][0], "adamw_" + key)
            grads[pname], deltas[pname], new_m[pname], new_v[pname] = g[None], d_[None], m_[None], v_[None]

    shapes = [w[n].shape for n in SMALL_NAMES]
    _, (slots,) = split_wait(small_started[0], gather_slices, after, "small_wait")
    total = sum_slots(slots, "sum_small")
    small_g = unpack_small(total, shapes + [(1, 1)])
    loss_total = small_g[-1].reshape(())
    gpack = pack_small(small_g[:-1])
    d_, m_, v_ = adamw(pack_small([w[n] for n in SMALL_NAMES]), gpack, pack_small([m[n] for n in SMALL_NAMES]),
                       pack_small([v[n] for n in SMALL_NAMES]), "adamw_small")
    for n, g_, dd, mm, vv in zip(SMALL_NAMES, small_g[:-1], unpack_small(d_, shapes), unpack_small(m_, shapes),
                                 unpack_small(v_, shapes)):
        grads[n], deltas[n], new_m[n], new_v[n] = g_, dd, mm, vv

    return (loss_total, grad_x[None], *[grads[n] for n in WEIGHT_ORDER], *[deltas[n] for n in WEIGHT_ORDER],
            *[new_m[n] for n in WEIGHT_ORDER], *[new_v[n] for n in WEIGHT_ORDER])
```

```python
import functools

import jax
import jax.numpy as jnp
from jax import lax
from jax.experimental import pallas as pl
from jax.experimental.pallas import tpu as pltpu

F32 = jnp.float32
BF16 = jnp.bfloat16

N_DEV = 8
N_Q_HEADS = 8
N_KV_HEADS = 2
HEAD_DIM = 64
ATTN_WIDTH = N_Q_HEADS * HEAD_DIM
KV_WIDTH = N_KV_HEADS * HEAD_DIM
QKV_WIDTH = ATTN_WIDTH + 2 * KV_WIDTH
BLOCK = 128
GMLP_GROUPS = 8
GMLP_GROUP_DIM = 64
GMLP_WIDTH = GMLP_GROUPS * GMLP_GROUP_DIM
EPS = 1e-6
FFN_RES = 0.5
ATTN_SCALE = HEAD_DIM ** -0.5
MASK_VALUE = -1e30

ADAM_LR = 0.001
ADAM_B1 = 0.9
ADAM_B2 = 0.999
ADAM_EPS = 1e-08
ADAM_WD = 0.01
ADAM_STEP = 10

V7X_VMEM_BYTES = 64 * 1024 * 1024
VMEM_LIMIT = 62 * 1024 * 1024
LANES = 128
MXU_WIDTH = 256

NT = (((1,), (1,)), ((), ()))
TN = (((0,), (0,)), ((), ()))


def _dot(a, b):
    return jnp.dot(a, b, preferred_element_type=F32)


def _dg(a, b, dims):
    return lax.dot_general(a, b, dims, preferred_element_type=F32)


def _params(semantics=None):
    return pltpu.CompilerParams(dimension_semantics=semantics, vmem_limit_bytes=VMEM_LIMIT)


def _resident(shape):
    zeros = (0,) * len(shape)
    return pl.BlockSpec(shape, lambda *_: zeros, pipeline_mode=pl.Buffered(1))


def _drop_deps(body, n_in, n_deps):
    return lambda *refs: body(*refs[:n_in], *refs[n_in + n_deps:])


ANY_SPEC = pl.BlockSpec(memory_space=pl.ANY)


def _token_tile(t):
    return min(t, 512)


def _f_chunk(f):
    for c in (256, 128):
        if f % c == 0:
            return c
    return f


def ffn_fwd(x, gain, wg_t, wu_t, wd, name, deps=()):
    t, d = x.shape
    f = wd.shape[0]
    tm = _token_tile(t)
    fc = _f_chunk(f)

    def body(x_ref, g_ref, wg_ref, wu_ref, wd_ref, xo_ref, sil_ref, gp_ref, s_ref):
        xv = x_ref[...]
        r = lax.rsqrt(jnp.mean(xv * xv, axis=-1, keepdims=True) + EPS)
        h = (xv * r * g_ref[...]).astype(BF16)
        for c in range(f // fc):
            sl = slice(c * fc, (c + 1) * fc)
            a = _dg(h, wg_ref[sl, :], NT)
            b = _dg(h, wu_ref[sl, :], NT)
            sig = jax.nn.sigmoid(a)
            sil = a * sig
            sil_ref[:, sl] = sil.astype(BF16)
            gp_ref[:, sl] = (b * (sig + sil - sil * sig)).astype(BF16)
            s_ref[:, sl] = (sil * b).astype(BF16)
        xo_ref[...] = xv + FFN_RES * _dot(s_ref[...], wd_ref[...])

    wide = lambda: pl.BlockSpec((tm, f), lambda i: (i, 0))
    return pl.pallas_call(
        _drop_deps(body, 5, len(deps)), name=name, grid=(t // tm,),
        in_specs=[pl.BlockSpec((tm, d), lambda i: (i, 0)), _resident((1, d)),
                  _resident((f, d)), _resident((f, d)), _resident((f, d))] + [ANY_SPEC] * len(deps),
        out_specs=[pl.BlockSpec((tm, d), lambda i: (i, 0)), wide(), wide(), wide()],
        out_shape=[jax.ShapeDtypeStruct((t, d), F32)] + [jax.ShapeDtypeStruct((t, f), BF16)] * 3,
        compiler_params=_params(("arbitrary",)),
    )(x, gain, wg_t, wu_t, wd, *deps)


def _rmsnorm_bwd_rows(xv, r, gain, dh):
    n = xv * r
    dgain = jnp.sum(dh * n, axis=0, keepdims=True)
    dn = dh * gain
    dx = r * (dn - n * jnp.mean(dn * n, axis=-1, keepdims=True))
    return dx, dgain


def ffn_bwd(x, dxo, gain, sil, gp, wg_t, wu_t, wd, name, deps=()):
    t, d = x.shape
    f = wd.shape[0]
    tm = _token_tile(t)
    nh = 2 if f % (2 * LANES) == 0 else 1
    fh = f // nh
    chunks = [(off, min(MXU_WIDTH, fh - off)) for off in range(0, fh, MXU_WIDTH)]

    def body(x_ref, dxo_ref, g_ref, sil_ref, gp_ref, wg_ref, wu_ref, wd_ref,
             dx_ref, da_ref, db_ref, h_ref, dg_ref, dh_ref):
        i, j = pl.program_id(0), pl.program_id(1)
        gain_v = g_ref[...]

        @pl.when(j == 0)
        def _():
            xv = x_ref[...]
            r = lax.rsqrt(jnp.mean(xv * xv, axis=-1, keepdims=True) + EPS)
            h_ref[...] = (xv * r * gain_v).astype(BF16)
            dh_ref[...] = jnp.zeros_like(dh_ref)

        df = (FFN_RES * dxo_ref[...]).astype(BF16)
        for off, size in chunks:
            sl = slice(off, off + size)
            rows = pl.ds(pl.multiple_of(j * fh + off, LANES), size)
            ds = _dg(df, wd_ref[rows, :], NT)
            da_ref[:, sl] = (ds * gp_ref[:, sl].astype(F32)).astype(BF16)
            db_ref[:, sl] = (ds * sil_ref[:, sl].astype(F32)).astype(BF16)
        half_rows = pl.ds(pl.multiple_of(j * fh, LANES), fh)
        dh_ref[...] += _dot(da_ref[...], wg_ref[half_rows, :]) + _dot(db_ref[...], wu_ref[half_rows, :])

        @pl.when(j == nh - 1)
        def _():
            xv = x_ref[...]
            r = lax.rsqrt(jnp.mean(xv * xv, axis=-1, keepdims=True) + EPS)
            dx, dgain = _rmsnorm_bwd_rows(xv, r, gain_v, dh_ref[...])
            dx_ref[...] = dxo_ref[...] + dx

            @pl.when(i == 0)
            def _():
                dg_ref[...] = jnp.zeros_like(dg_ref)

            dg_ref[...] += dgain

    tok = lambda: pl.BlockSpec((tm, d), lambda i, j: (i, 0))
    half = lambda: pl.BlockSpec((tm, fh), lambda i, j: (i, j))
    return pl.pallas_call(
        _drop_deps(body, 8, len(deps)), name=name, grid=(t // tm, nh),
        in_specs=[tok(), tok(), _resident((1, d)), half(), half(),
                  _resident((f, d)), _resident((f, d)), _resident((f, d))] + [ANY_SPEC] * len(deps),
        out_specs=[tok(), half(), half(), tok(), pl.BlockSpec((1, d), lambda i, j: (0, 0))],
        out_shape=[jax.ShapeDtypeStruct((t, d), F32), jax.ShapeDtypeStruct((t, f), BF16),
                   jax.ShapeDtypeStruct((t, f), BF16), jax.ShapeDtypeStruct((t, d), BF16),
                   jax.ShapeDtypeStruct((1, d), F32)],
        scratch_shapes=[pltpu.VMEM((tm, d), F32)],
        compiler_params=_params(("arbitrary", "arbitrary")),
    )(x, dxo, gain, sil, gp, wg_t, wu_t, wd, *deps)


def _row_tile(m, cap):
    if m <= cap:
        return m
    best = None
    for bm in range(LANES, cap + 1, LANES):
        if m % bm == 0:
            best = bm
    return best if best is not None else m


def grad_tn(a, b, scale, name, deps=()):
    t, m = a.shape
    n = b.shape[1]
    bm = _row_tile(m, 1408)
    bk = min(t, 1024)
    nk = t // bk

    def body(a_ref, b_ref, o_ref, acc_ref):
        k = pl.program_id(1)

        @pl.when(k == 0)
        def _():
            acc_ref[...] = jnp.zeros_like(acc_ref)

        acc_ref[...] += _dg(a_ref[...].astype(BF16), b_ref[...].astype(BF16), TN)

        @pl.when(k == nk - 1)
        def _():
            o_ref[...] = (scale * acc_ref[...]).astype(BF16)

    return pl.pallas_call(
        _drop_deps(body, 2, len(deps)), name=name, grid=(m // bm, nk),
        in_specs=[pl.BlockSpec((bk, bm), lambda i, k: (k, i)), pl.BlockSpec((bk, n), lambda i, k: (k, 0))]
        + [ANY_SPEC] * len(deps),
        out_specs=pl.BlockSpec((bm, n), lambda i, k: (i, 0)),
        out_shape=jax.ShapeDtypeStruct((m, n), BF16),
        scratch_shapes=[pltpu.VMEM((bm, n), F32)],
        compiler_params=_params(("arbitrary", "arbitrary")),
    )(a, b, *deps)


def grad_nn(a_list, b, name):
    t, n = b.shape
    ms = [a.shape[0] for a in a_list]
    m = sum(ms)
    bk = min(t, 1024)
    nk = t // bk
    na = len(a_list)

    def body(*refs):
        a_refs, b_ref, o_ref, acc_ref = refs[:na], refs[na], refs[na + 1], refs[na + 2]
        k = pl.program_id(0)

        @pl.when(k == 0)
        def _():
            acc_ref[...] = jnp.zeros_like(acc_ref)

        bv = b_ref[...].astype(BF16)
        off = 0
        for a_ref, mi in zip(a_refs, ms):
            acc_ref[off:off + mi, :] += _dot(a_ref[...].astype(BF16), bv)
            off += mi

        @pl.when(k == nk - 1)
        def _():
            o_ref[...] = acc_ref[...].astype(BF16)

    return pl.pallas_call(
        body, name=name, grid=(nk,),
        in_specs=[pl.BlockSpec((mi, bk), lambda k: (0, k)) for mi in ms] + [pl.BlockSpec((bk, n), lambda k: (k, 0))],
        out_specs=pl.BlockSpec((m, n), lambda k: (0, 0)),
        out_shape=jax.ShapeDtypeStruct((m, n), BF16),
        scratch_shapes=[pltpu.VMEM((m, n), F32)],
        compiler_params=_params(("arbitrary",)),
    )(*a_list, b)


def mix_in_fwd(x, gain, win_t, b_in_col, name):
    t, d = x.shape
    w = win_t.shape[0]
    tm = _token_tile(t)

    def body(x_ref, g_ref, w_ref, bias_ref, qkv_ref, zg_ref):
        xv = x_ref[...]
        r = lax.rsqrt(jnp.mean(xv * xv, axis=-1, keepdims=True) + EPS)
        h = (xv * r * g_ref[...]).astype(BF16)
        qkv_ref[...] = (_dg(w_ref[:QKV_WIDTH, :], h, NT) + bias_ref[:QKV_WIDTH, :]).astype(BF16)
        zg_ref[...] = (_dg(w_ref[QKV_WIDTH:, :], h, NT) + bias_ref[QKV_WIDTH:, :]).astype(BF16)

    return pl.pallas_call(
        body, name=name, grid=(t // tm,),
        in_specs=[pl.BlockSpec((tm, d), lambda i: (i, 0)), _resident((1, d)), _resident((w, d)), _resident((w, 1))],
        out_specs=[pl.BlockSpec((QKV_WIDTH, tm), lambda i: (0, i)), pl.BlockSpec((w - QKV_WIDTH, tm), lambda i: (0, i))],
        out_shape=[jax.ShapeDtypeStruct((QKV_WIDTH, t), BF16), jax.ShapeDtypeStruct((w - QKV_WIDTH, t), BF16)],
        compiler_params=_params(("arbitrary",)),
    )(x, gain, win_t, b_in_col)


def _attn_specs(nb):
    last = nb - 1
    cur = lambda n: jnp.minimum(n, last)
    prev = lambda n: jnp.maximum(jnp.minimum(n, last) - 1, 0)
    k_row = ATTN_WIDTH // KV_WIDTH
    return [
        pl.BlockSpec((ATTN_WIDTH, BLOCK), lambda n: (0, cur(n))),
        pl.BlockSpec((KV_WIDTH, BLOCK), lambda n: (k_row, prev(n))),
        pl.BlockSpec((KV_WIDTH, BLOCK), lambda n: (k_row, cur(n))),
        pl.BlockSpec((KV_WIDTH, BLOCK), lambda n: (k_row + 1, prev(n))),
        pl.BlockSpec((KV_WIDTH, BLOCK), lambda n: (k_row + 1, cur(n))),
    ]


def _attn_probs(q, kp, kc, sink, mask_prev, mask_cur):
    sp = jnp.where(mask_prev, _dg(kp, q, TN) * ATTN_SCALE, MASK_VALUE)
    sc = jnp.where(mask_cur, _dg(kc, q, TN) * ATTN_SCALE, MASK_VALUE)
    m = jnp.maximum(jnp.maximum(jnp.max(sp, axis=0, keepdims=True), jnp.max(sc, axis=0, keepdims=True)), sink)
    pp = jnp.exp(sp - m)
    pc = jnp.exp(sc - m)
    es = jnp.exp(sink - m)
    inv = 1.0 / (jnp.sum(pp, axis=0, keepdims=True) + jnp.sum(pc, axis=0, keepdims=True) + es)
    return pp * inv, pc * inv, es * inv


def _attn_masks(n):
    key = lax.broadcasted_iota(jnp.int32, (BLOCK, BLOCK), 0)
    qry = lax.broadcasted_iota(jnp.int32, (BLOCK, BLOCK), 1)
    return (key > qry) & (n > 0), key <= qry


def attn_fwd(qkv_t, sinks, name):
    t = qkv_t.shape[1]
    nb = t // BLOCK
    rep = N_Q_HEADS // N_KV_HEADS

    def body(sink_ref, q_ref, kp_ref, kc_ref, vp_ref, vc_ref, y_ref):
        n = pl.program_id(0)
        mask_prev, mask_cur = _attn_masks(n)
        for g in range(N_KV_HEADS):
            gs = slice(g * HEAD_DIM, (g + 1) * HEAD_DIM)
            kp, kc, vp, vc = kp_ref[gs, :], kc_ref[gs, :], vp_ref[gs, :], vc_ref[gs, :]
            for rr in range(rep):
                hd = g * rep + rr
                hs = slice(hd * HEAD_DIM, (hd + 1) * HEAD_DIM)
                pp, pc, _ = _attn_probs(q_ref[hs, :], kp, kc, sink_ref[0, hd], mask_prev, mask_cur)
                y_ref[hs, :] = (_dot(vp, pp.astype(BF16)) + _dot(vc, pc.astype(BF16))).astype(BF16)

    return pl.pallas_call(
        body, name=name, grid=(nb,),
        in_specs=[pl.BlockSpec(memory_space=pltpu.SMEM)] + _attn_specs(nb),
        out_specs=pl.BlockSpec((ATTN_WIDTH, BLOCK), lambda n: (0, n)),
        out_shape=jax.ShapeDtypeStruct((ATTN_WIDTH, t), BF16),
        compiler_params=_params(("arbitrary",)),
    )(sinks, qkv_t, qkv_t, qkv_t, qkv_t, qkv_t)


def attn_bwd(qkv_t, dy_t, sinks, name):
    t = qkv_t.shape[1]
    nb = t // BLOCK
    rep = N_Q_HEADS // N_KV_HEADS

    def body(sink_ref, q_ref, kp_ref, kc_ref, vp_ref, vc_ref, do_ref, dq_ref, dk_ref, dv_ref, dsink_ref,
             ck_ref, cv_ref, sacc_ref):
        n = pl.program_id(0)

        @pl.when(n == 0)
        def _():
            sacc_ref[...] = jnp.zeros_like(sacc_ref)

        @pl.when(n < nb)
        def _():
            mask_prev, mask_cur = _attn_masks(n)
            for g in range(N_KV_HEADS):
                gs = slice(g * HEAD_DIM, (g + 1) * HEAD_DIM)
                kp, kc, vp, vc = kp_ref[gs, :], kc_ref[gs, :], vp_ref[gs, :], vc_ref[gs, :]
                dkp = jnp.zeros((HEAD_DIM, BLOCK), F32)
                dkc = jnp.zeros((HEAD_DIM, BLOCK), F32)
                dvp = jnp.zeros((HEAD_DIM, BLOCK), F32)
                dvc = jnp.zeros((HEAD_DIM, BLOCK), F32)
                for rr in range(rep):
                    hd = g * rep + rr
                    hs = slice(hd * HEAD_DIM, (hd + 1) * HEAD_DIM)
                    q = q_ref[hs, :]
                    do = do_ref[hs, :]
                    pp, pc, ps = _attn_probs(q, kp, kc, sink_ref[0, hd], mask_prev, mask_cur)
                    dpp = _dg(vp, do, TN)
                    dpc = _dg(vc, do, TN)
                    delta = jnp.sum(pp * dpp, axis=0, keepdims=True) + jnp.sum(pc * dpc, axis=0, keepdims=True)
                    dsp = (pp * (dpp - delta)).astype(BF16)
                    dsc = (pc * (dpc - delta)).astype(BF16)
                    sacc_ref[hd:hd + 1, :] += -(ps * delta)
                    dq_ref[hs, :] = (ATTN_SCALE * (_dot(kp, dsp) + _dot(kc, dsc))).astype(BF16)
                    dkp = dkp + _dg(q, dsp, NT)
                    dkc = dkc + _dg(q, dsc, NT)
                    dvp = dvp + _dg(do, pp.astype(BF16), NT)
                    dvc = dvc + _dg(do, pc.astype(BF16), NT)

                @pl.when(n > 0)
                def _():
                    dk_ref[gs, :] = (ck_ref[gs, :] + ATTN_SCALE * dkp).astype(BF16)
                    dv_ref[gs, :] = (cv_ref[gs, :] + dvp).astype(BF16)

                ck_ref[gs, :] = ATTN_SCALE * dkc
                cv_ref[gs, :] = dvc

        @pl.when(n == nb)
        def _():
            dk_ref[...] = ck_ref[...].astype(BF16)
            dv_ref[...] = cv_ref[...].astype(BF16)
            dsink_ref[...] = jnp.sum(sacc_ref[...], axis=1, keepdims=True)

    last = nb - 1
    done = lambda n: jnp.maximum(n - 1, 0)
    outs = pl.pallas_call(
        body, name=name, grid=(nb + 1,),
        in_specs=[pl.BlockSpec(memory_space=pltpu.SMEM)] + _attn_specs(nb)
        + [pl.BlockSpec((ATTN_WIDTH, BLOCK), lambda n: (0, jnp.minimum(n, last)))],
        out_specs=[pl.BlockSpec((ATTN_WIDTH, BLOCK), lambda n: (0, jnp.minimum(n, last))),
                   pl.BlockSpec((KV_WIDTH, BLOCK), lambda n: (0, done(n))),
                   pl.BlockSpec((KV_WIDTH, BLOCK), lambda n: (0, done(n))),
                   pl.BlockSpec((N_Q_HEADS, 1), lambda n: (0, 0))],
        out_shape=[jax.ShapeDtypeStruct((ATTN_WIDTH, t), BF16), jax.ShapeDtypeStruct((KV_WIDTH, t), BF16),
                   jax.ShapeDtypeStruct((KV_WIDTH, t), BF16), jax.ShapeDtypeStruct((N_Q_HEADS, 1), F32)],
        scratch_shapes=[pltpu.VMEM((KV_WIDTH, BLOCK), F32), pltpu.VMEM((KV_WIDTH, BLOCK), F32),
                        pltpu.VMEM((N_Q_HEADS, BLOCK), F32)],
        compiler_params=_params(("arbitrary",)),
    )(sinks, qkv_t, qkv_t, qkv_t, qkv_t, qkv_t, dy_t)
    return outs


GELU_C = 0.7978845608028654
GELU_A = 0.044715


def _gelu(x):
    return 0.5 * x * (1.0 + jnp.tanh(GELU_C * (x + GELU_A * x * x * x)))


def _gelu_grad(x):
    th = jnp.tanh(GELU_C * (x + GELU_A * x * x * x))
    return 0.5 * (1.0 + th) + 0.5 * x * (1.0 - th * th) * GELU_C * (1.0 + 3.0 * GELU_A * x * x)


def _gmlp_parts(zg, lng, lnb):
    z = _gelu(zg)
    u = z[:GMLP_WIDTH, :]
    vv = z[GMLP_WIDTH:, :]
    mu = jnp.mean(vv, axis=0, keepdims=True)
    xc = vv - mu
    rstd = lax.rsqrt(jnp.mean(xc * xc, axis=0, keepdims=True) + EPS)
    xhat = xc * rstd
    return u, xhat, rstd, xhat * lng + lnb


def _causal(w):
    row = lax.broadcasted_iota(jnp.int32, (BLOCK, BLOCK), 0)
    col = lax.broadcasted_iota(jnp.int32, (BLOCK, BLOCK), 1)
    return jnp.where(col <= row, w, 0.0)


def gmlp_fwd(zg_t, lng, lnb, w_s, b_s, name):
    t = zg_t.shape[1]
    nb = t // BLOCK

    def body(zg_ref, lng_ref, lnb_ref, w_ref, bs_ref, y_ref):
        u, _, _, vn = _gmlp_parts(zg_ref[...].astype(F32), lng_ref[...], lnb_ref[...])
        for g in range(GMLP_GROUPS):
            gs = slice(g * GMLP_GROUP_DIM, (g + 1) * GMLP_GROUP_DIM)
            wc = _causal(w_ref[g]).astype(BF16)
            mixed = _dg(vn[gs, :].astype(BF16), wc, NT) + bs_ref[g:g + 1, :]
            y_ref[gs, :] = (u[gs, :] * mixed).astype(BF16)

    return pl.pallas_call(
        body, name=name, grid=(nb,),
        in_specs=[pl.BlockSpec((2 * GMLP_WIDTH, BLOCK), lambda n: (0, n)), _resident((GMLP_WIDTH, 1)),
                  _resident((GMLP_WIDTH, 1)), _resident((GMLP_GROUPS, BLOCK, BLOCK)), _resident((GMLP_GROUPS, BLOCK))],
        out_specs=pl.BlockSpec((GMLP_WIDTH, BLOCK), lambda n: (0, n)),
        out_shape=jax.ShapeDtypeStruct((GMLP_WIDTH, t), BF16),
        compiler_params=_params(("arbitrary",)),
    )(zg_t, lng, lnb, w_s, b_s)


def gmlp_bwd(zg_t, dy_t, lng, lnb, w_s, b_s, name):
    t = zg_t.shape[1]
    nb = t // BLOCK

    def body(zg_ref, dy_ref, lng_ref, lnb_ref, w_ref, bs_ref, dzg_ref, dw_ref, dbs_ref, dlng_ref, dlnb_ref,
             gacc_ref, bacc_ref):
        n = pl.program_id(0)

        @pl.when(n == 0)
        def _():
            dw_ref[...] = jnp.zeros_like(dw_ref)
            dbs_ref[...] = jnp.zeros_like(dbs_ref)
            gacc_ref[...] = jnp.zeros_like(gacc_ref)
            bacc_ref[...] = jnp.zeros_like(bacc_ref)

        zg = zg_ref[...].astype(F32)
        lng_v = lng_ref[...]
        u, xhat, rstd, vn = _gmlp_parts(zg, lng_v, lnb_ref[...])
        dy = dy_ref[...].astype(F32)
        dvn_parts = []
        for g in range(GMLP_GROUPS):
            gs = slice(g * GMLP_GROUP_DIM, (g + 1) * GMLP_GROUP_DIM)
            wc = _causal(w_ref[g]).astype(BF16)
            vn_g = vn[gs, :].astype(BF16)
            mixed = _dg(vn_g, wc, NT) + bs_ref[g:g + 1, :]
            dy_g = dy[gs, :]
            dmixed = dy_g * u[gs, :]
            dmixed_b = dmixed.astype(BF16)
            dzg_ref[gs, :] = (dy_g * mixed * _gelu_grad(zg[gs, :])).astype(BF16)
            dw_ref[g] += _causal(_dg(dmixed_b, vn_g, TN))
            dbs_ref[g:g + 1, :] += jnp.sum(dmixed, axis=0, keepdims=True)
            dvn_parts.append(_dot(dmixed_b, wc))
        dvn = jnp.concatenate(dvn_parts, axis=0)
        gacc_ref[...] += dvn * xhat
        bacc_ref[...] += dvn
        dxhat = dvn * lng_v
        m1 = jnp.mean(dxhat, axis=0, keepdims=True)
        m2 = jnp.mean(dxhat * xhat, axis=0, keepdims=True)
        dvv = rstd * (dxhat - m1 - xhat * m2)
        dzg_ref[GMLP_WIDTH:, :] = (dvv * _gelu_grad(zg[GMLP_WIDTH:, :])).astype(BF16)

        @pl.when(n == nb - 1)
        def _():
            dlng_ref[...] = jnp.sum(gacc_ref[...], axis=1, keepdims=True)
            dlnb_ref[...] = jnp.sum(bacc_ref[...], axis=1, keepdims=True)

    const2 = lambda n: (0, 0)
    return pl.pallas_call(
        body, name=name, grid=(nb,),
        in_specs=[pl.BlockSpec((2 * GMLP_WIDTH, BLOCK), lambda n: (0, n)), pl.BlockSpec((GMLP_WIDTH, BLOCK), lambda n: (0, n)),
                  _resident((GMLP_WIDTH, 1)), _resident((GMLP_WIDTH, 1)),
                  _resident((GMLP_GROUPS, BLOCK, BLOCK)), _resident((GMLP_GROUPS, BLOCK))],
        out_specs=[pl.BlockSpec((2 * GMLP_WIDTH, BLOCK), lambda n: (0, n)),
                   pl.BlockSpec((GMLP_GROUPS, BLOCK, BLOCK), lambda n: (0, 0, 0)),
                   pl.BlockSpec((GMLP_GROUPS, BLOCK), const2),
                   pl.BlockSpec((GMLP_WIDTH, 1), const2), pl.BlockSpec((GMLP_WIDTH, 1), const2)],
        out_shape=[jax.ShapeDtypeStruct((2 * GMLP_WIDTH, t), BF16),
                   jax.ShapeDtypeStruct((GMLP_GROUPS, BLOCK, BLOCK), F32),
                   jax.ShapeDtypeStruct((GMLP_GROUPS, BLOCK), F32),
                   jax.ShapeDtypeStruct((GMLP_WIDTH, 1), F32), jax.ShapeDtypeStruct((GMLP_WIDTH, 1), F32)],
        scratch_shapes=[pltpu.VMEM((GMLP_WIDTH, BLOCK), F32), pltpu.VMEM((GMLP_WIDTH, BLOCK), F32)],
        compiler_params=_params(("arbitrary",)),
    )(zg_t, dy_t, lng, lnb, w_s, b_s)


def _rmsnorm_cols(y, gain_col):
    r = lax.rsqrt(jnp.mean(y * y, axis=0, keepdims=True) + EPS)
    n = y * r
    return n, r, n * gain_col


def mix_out_fwd(x, ya_t, yg_t, gao, ggo, wout, b_out, name):
    t, d = x.shape
    tm = _token_tile(t)

    def body(x_ref, ya_ref, yg_ref, gao_ref, ggo_ref, w_ref, b_ref, o_ref):
        _, _, ya = _rmsnorm_cols(ya_ref[...].astype(F32), gao_ref[...])
        _, _, yg = _rmsnorm_cols(yg_ref[...].astype(F32), ggo_ref[...])
        o_ref[...] = (x_ref[...] + _dg(ya.astype(BF16), w_ref[:ATTN_WIDTH, :], TN)
                      + _dg(yg.astype(BF16), w_ref[ATTN_WIDTH:, :], TN) + b_ref[...])

    return pl.pallas_call(
        body, name=name, grid=(t // tm,),
        in_specs=[pl.BlockSpec((tm, d), lambda i: (i, 0)), pl.BlockSpec((ATTN_WIDTH, tm), lambda i: (0, i)),
                  pl.BlockSpec((GMLP_WIDTH, tm), lambda i: (0, i)), _resident((ATTN_WIDTH, 1)), _resident((GMLP_WIDTH, 1)),
                  _resident((ATTN_WIDTH + GMLP_WIDTH, d)), _resident((1, d))],
        out_specs=pl.BlockSpec((tm, d), lambda i: (i, 0)),
        out_shape=jax.ShapeDtypeStruct((t, d), F32),
        compiler_params=_params(("arbitrary",)),
    )(x, ya_t, yg_t, gao, ggo, wout, b_out)


def _lane_fold(v):
    out = v[:, :LANES]
    for j in range(1, v.shape[1] // LANES):
        out = out + v[:, j * LANES:(j + 1) * LANES]
    return out


def mix_out_bwd(dx, ya_t, yg_t, gao, ggo, wout, name, deps=()):
    t, d = dx.shape
    tm = _token_tile(t)
    nt = t // tm

    def body(dx_ref, ya_ref, yg_ref, gao_ref, ggo_ref, w_ref, dya_ref, dyg_ref, y_ref, dgao_ref, dggo_ref, db_ref,
             acc_ref):
        i = pl.program_id(0)

        @pl.when(i == 0)
        def _():
            acc_ref[...] = jnp.zeros_like(acc_ref)
            db_ref[...] = jnp.zeros_like(db_ref)

        dxv = dx_ref[...]
        db_ref[...] += jnp.sum(dxv, axis=0, keepdims=True)
        dxb = dxv.astype(BF16)
        for part, (src_ref, gain_ref, dst_ref) in enumerate(((ya_ref, gao_ref, dya_ref), (yg_ref, ggo_ref, dyg_ref))):
            rows = slice(part * ATTN_WIDTH, (part + 1) * ATTN_WIDTH)
            gain = gain_ref[...]
            nrm, r, yn = _rmsnorm_cols(src_ref[...].astype(F32), gain)
            y_ref[rows, :] = yn.astype(BF16)
            dy = _dg(w_ref[rows, :], dxb, NT)
            acc_ref[rows, :] += _lane_fold(dy * nrm)
            dn = dy * gain
            dst_ref[...] = (r * (dn - nrm * jnp.mean(dn * nrm, axis=0, keepdims=True))).astype(BF16)

        @pl.when(i == nt - 1)
        def _():
            dgao_ref[...] = jnp.sum(acc_ref[:ATTN_WIDTH, :], axis=1, keepdims=True)
            dggo_ref[...] = jnp.sum(acc_ref[ATTN_WIDTH:, :], axis=1, keepdims=True)

    const2 = lambda i: (0, 0)
    feat = lambda w: pl.BlockSpec((w, tm), lambda i: (0, i))
    return pl.pallas_call(
        _drop_deps(body, 6, len(deps)), name=name, grid=(nt,),
        in_specs=[pl.BlockSpec((tm, d), lambda i: (i, 0)), feat(ATTN_WIDTH), feat(GMLP_WIDTH),
                  _resident((ATTN_WIDTH, 1)), _resident((GMLP_WIDTH, 1)), _resident((ATTN_WIDTH + GMLP_WIDTH, d))]
        + [ANY_SPEC] * len(deps),
        out_specs=[feat(ATTN_WIDTH), feat(GMLP_WIDTH), feat(ATTN_WIDTH + GMLP_WIDTH),
                   pl.BlockSpec((ATTN_WIDTH, 1), const2), pl.BlockSpec((GMLP_WIDTH, 1), const2),
                   pl.BlockSpec((1, d), const2)],
        out_shape=[jax.ShapeDtypeStruct((ATTN_WIDTH, t), BF16), jax.ShapeDtypeStruct((GMLP_WIDTH, t), BF16),
                   jax.ShapeDtypeStruct((ATTN_WIDTH + GMLP_WIDTH, t), BF16),
                   jax.ShapeDtypeStruct((ATTN_WIDTH, 1), F32), jax.ShapeDtypeStruct((GMLP_WIDTH, 1), F32),
                   jax.ShapeDtypeStruct((1, d), F32)],
        scratch_shapes=[pltpu.VMEM((ATTN_WIDTH + GMLP_WIDTH, LANES), F32)],
        compiler_params=_params(("arbitrary",)),
    )(dx, ya_t, yg_t, gao, ggo, wout, *deps)


def mix_in_bwd(x, dxo, gain, dq_t, dk_t, dv_t, dzg_t, win_t, name):
    t, d = x.shape
    w = win_t.shape[0]
    tm = _token_tile(t)
    nt = t // tm
    parts = ((0, ATTN_WIDTH), (ATTN_WIDTH, KV_WIDTH), (ATTN_WIDTH + KV_WIDTH, KV_WIDTH), (QKV_WIDTH, w - QKV_WIDTH))

    def body(x_ref, dxo_ref, g_ref, dq_ref, dk_ref, dv_ref, dzg_ref, w_ref, dx_ref, h_ref, dg_ref, db_ref, acc_ref):
        i = pl.program_id(0)

        @pl.when(i == 0)
        def _():
            acc_ref[...] = jnp.zeros_like(acc_ref)
            dg_ref[...] = jnp.zeros_like(dg_ref)

        xv = x_ref[...]
        gain_v = g_ref[...]
        r = lax.rsqrt(jnp.mean(xv * xv, axis=-1, keepdims=True) + EPS)
        h_ref[...] = (xv * r * gain_v).astype(BF16)
        dh = jnp.zeros((tm, d), F32)
        for (off, size), src_ref in zip(parts, (dq_ref, dk_ref, dv_ref, dzg_ref)):
            dp = src_ref[...]
            acc_ref[off:off + size, :] += _lane_fold(dp.astype(F32))
            dh = dh + _dg(dp, w_ref[off:off + size, :], TN)
        dx, dgain = _rmsnorm_bwd_rows(xv, r, gain_v, dh)
        dx_ref[...] = dxo_ref[...] + dx
        dg_ref[...] += dgain

        @pl.when(i == nt - 1)
        def _():
            db_ref[...] = jnp.sum(acc_ref[...], axis=1, keepdims=True)

    const2 = lambda i: (0, 0)
    tok = lambda: pl.BlockSpec((tm, d), lambda i: (i, 0))
    feat = lambda rows: pl.BlockSpec((rows, tm), lambda i: (0, i))
    return pl.pallas_call(
        body, name=name, grid=(nt,),
        in_specs=[tok(), tok(), _resident((1, d)), feat(ATTN_WIDTH), feat(KV_WIDTH), feat(KV_WIDTH),
                  feat(w - QKV_WIDTH), _resident((w, d))],
        out_specs=[tok(), tok(), pl.BlockSpec((1, d), const2), pl.BlockSpec((w, 1), const2)],
        out_shape=[jax.ShapeDtypeStruct((t, d), F32), jax.ShapeDtypeStruct((t, d), BF16),
                   jax.ShapeDtypeStruct((1, d), F32), jax.ShapeDtypeStruct((w, 1), F32)],
        scratch_shapes=[pltpu.VMEM((w, LANES), F32)],
        compiler_params=_params(("arbitrary",)),
    )(x, dxo, gain, dq_t, dk_t, dv_t, dzg_t, win_t)


def final_bwd(x, target, gain, name):
    t, d = x.shape
    tm = _token_tile(t)

    def body(x_ref, t_ref, g_ref, dx_ref, dg_ref, loss_ref):
        i = pl.program_id(0)

        @pl.when(i == 0)
        def _():
            dg_ref[...] = jnp.zeros_like(dg_ref)
            loss_ref[...] = jnp.zeros_like(loss_ref)

        xv = x_ref[...]
        gain_v = g_ref[...]
        r = lax.rsqrt(jnp.mean(xv * xv, axis=-1, keepdims=True) + EPS)
        err = xv * r * gain_v - t_ref[...]
        row_loss = jnp.mean(err * err, axis=-1, keepdims=True)
        loss_ref[...] += 0.5 * jnp.sum(row_loss, axis=0, keepdims=True)
        dx, dgain = _rmsnorm_bwd_rows(xv, r, gain_v, err * (1.0 / d))
        dx_ref[...] = dx
        dg_ref[...] += dgain

    const2 = lambda i: (0, 0)
    return pl.pallas_call(
        body, name=name, grid=(t // tm,),
        in_specs=[pl.BlockSpec((tm, d), lambda i: (i, 0)), pl.BlockSpec((tm, d), lambda i: (i, 0)), _resident((1, d))],
        out_specs=[pl.BlockSpec((tm, d), lambda i: (i, 0)), pl.BlockSpec((1, d), const2), pl.BlockSpec((1, LANES), const2)],
        out_shape=[jax.ShapeDtypeStruct((t, d), F32), jax.ShapeDtypeStruct((1, d), F32),
                   jax.ShapeDtypeStruct((1, LANES), F32)],
        compiler_params=_params(("arbitrary",)),
    )(x, target, gain)


def _adamw_math(w, g, m, v):
    m = ADAM_B1 * m + (1.0 - ADAM_B1) * g
    v = ADAM_B2 * v + (1.0 - ADAM_B2) * (g * g)
    m_hat = m / (1.0 - ADAM_B1 ** ADAM_STEP)
    v_hat = v / (1.0 - ADAM_B2 ** ADAM_STEP)
    delta = -ADAM_LR * (m_hat / (jnp.sqrt(v_hat) + ADAM_EPS) + ADAM_WD * w)
    return delta, m, v


def adamw(w, g, m, v, name):
    rows, cols = w.shape
    tr = rows
    if rows * cols > 256 * 1024:
        for cand in (256, 128, 64, 32, 16, 8):
            if rows % cand == 0:
                tr = cand
                break

    def body(w_ref, g_ref, m_ref, v_ref, d_ref, mo_ref, vo_ref):
        delta, mn, vn = _adamw_math(w_ref[...], g_ref[...], m_ref[...], v_ref[...])
        d_ref[...] = delta
        mo_ref[...] = mn
        vo_ref[...] = vn

    spec = pl.BlockSpec((tr, cols), lambda i: (i, 0))
    shape = jax.ShapeDtypeStruct((rows, cols), F32)
    return pl.pallas_call(
        body, name=name, grid=(rows // tr,),
        in_specs=[spec] * 4, out_specs=[spec] * 3, out_shape=[shape] * 3,
        compiler_params=_params(("arbitrary",)),
    )(w, g, m, v)


def sum_partials(landing, grad3d, me, name, deps=()):
    nd, rows, cols = landing.shape
    tr = rows
    for cand in (128, 112, 88, 64, 32, 16):
        if rows % cand == 0:
            tr = cand
            break

    def body(me_ref, l_ref, own_ref, o_ref):
        acc = own_ref[...].astype(F32)
        for j in range(nd):
            acc = acc + l_ref[j].astype(F32)
        o_ref[...] = acc

    return pl.pallas_call(
        _drop_deps(body, 3, len(deps)), name=name,
        grid_spec=pltpu.PrefetchScalarGridSpec(
            num_scalar_prefetch=1, grid=(rows // tr,),
            in_specs=[pl.BlockSpec((nd, tr, cols), lambda i, me_ref: (0, i, 0)),
                      pl.BlockSpec((None, tr, cols), lambda i, me_ref: (me_ref[0], i, 0))] + [ANY_SPEC] * len(deps),
            out_specs=pl.BlockSpec((tr, cols), lambda i, me_ref: (i, 0))),
        out_shape=jax.ShapeDtypeStruct((rows, cols), F32),
        compiler_params=_params(("arbitrary",)),
    )(me, landing, grad3d, *deps)


def _mesh_place():
    x, y, c = lax.axis_index("x"), lax.axis_index("y"), lax.axis_index("c")
    return x, y, c, 4 * x + 2 * y + c


def _peer(x, y, c, k):
    return (x ^ ((k >> 2) & 1), y ^ ((k >> 1) & 1), c ^ (k & 1))


def allgather_rows(shards, n_gather, name):
    na = len(shards)

    def body(*refs):
        in_refs, out_refs = refs[:na], refs[na:2 * na]
        send_sems, recv_sems, local_sems = refs[2 * na:]
        x, y, c, me = _mesh_place()
        local = [pltpu.make_async_copy(in_refs[a], out_refs[a].at[me], local_sems.at[a]) for a in range(na)]
        for cp in local:
            cp.start()
        def copy(sem, a, origin, src, k):
            return pltpu.make_async_remote_copy(
                src_ref=src, dst_ref=out_refs[a].at[origin], send_sem=send_sems.at[sem, a],
                recv_sem=recv_sems.at[sem, a], device_id=_peer(x, y, c, k), device_id_type=pl.DeviceIdType.MESH)

        kx, ky = 4, 2

        def recv(sem, a, origin):
            copy(sem, a, origin, in_refs[a], 1).wait_recv()

        def landed(a, origin):
            return out_refs[a].at[origin]

        @pl.when(c == 1)
        def _():
            sends = []

            def send(sem, a, origin, src, k):
                cp = copy(sem, a, origin, src, k)
                cp.start()
                sends.append(cp)

            for a in range(n_gather):
                send(1, a, me, in_refs[a], kx)
                send(2, a, me, in_refs[a], ky)
                send(0, a, me, in_refs[a], 1)
            for a in range(n_gather):
                recv(0, a, me ^ 1)
                send(3, a, me ^ 1, landed(a, me ^ 1), kx)
                send(4, a, me ^ 1, landed(a, me ^ 1), ky)
            for a in range(n_gather):
                recv(1, a, me ^ kx)
                send(5, a, me ^ kx, landed(a, me ^ kx), ky)
                send(7, a, me ^ kx, landed(a, me ^ kx), 1)
                recv(2, a, me ^ ky)
                send(8, a, me ^ ky, landed(a, me ^ ky), 1)
            for a in range(n_gather):
                recv(3, a, me ^ kx ^ 1)
                send(9, a, me ^ kx ^ 1, landed(a, me ^ kx ^ 1), 1)
                recv(4, a, me ^ ky ^ 1)
                send(6, a, me ^ ky ^ 1, landed(a, me ^ ky ^ 1), kx)
                send(10, a, me ^ ky ^ 1, landed(a, me ^ ky ^ 1), 1)
            for a in range(n_gather):
                recv(5, a, me ^ 6)
                send(11, a, me ^ 6, landed(a, me ^ 6), 1)
                recv(6, a, me ^ 7)
                send(12, a, me ^ 7, landed(a, me ^ 7), 1)
            for cp in sends:
                cp.wait_send()

        @pl.when(c == 0)
        def _():
            north = me ^ 1
            own = [copy(0, a, me, in_refs[a], 1) for a in range(n_gather)]
            for cp in own:
                cp.start()
            for a in range(n_gather):
                recv(0, a, north)
                for sem, origin in ((7, north ^ kx), (8, north ^ ky), (9, north ^ kx ^ 1), (10, north ^ ky ^ 1),
                                    (11, north ^ 6), (12, north ^ 7)):
                    recv(sem, a, origin)
            for cp in own:
                cp.wait_send()

        for cp in local:
            cp.wait()

    return pl.pallas_call(
        body, name=name,
        in_specs=[ANY_SPEC] * na, out_specs=[ANY_SPEC] * na,
        out_shape=[jax.ShapeDtypeStruct((N_DEV,) + s.shape, s.dtype) for s in shards],
        scratch_shapes=[pltpu.SemaphoreType.DMA((13, n_gather)), pltpu.SemaphoreType.DMA((13, n_gather)),
                        pltpu.SemaphoreType.DMA((na,))],
    )(*shards)


HBM_SPEC = pl.BlockSpec(memory_space=pltpu.HBM)
SEM_SPEC = pl.BlockSpec(memory_space=pltpu.SEMAPHORE)
SIDE_EFFECT = pltpu.SideEffectType.DATAFLOW_SIDE_EFFECTING


def _hbm(v):
    return pltpu.with_memory_space_constraint(v, pltpu.HBM)


def gather_slices(src, land, me, k):
    return src, land.at[me], land.at[me ^ k]


def exchange_slices(src, land, me, k):
    return src.at[me ^ k], land.at[k - 1], land.at[k - 1]


def split_start(groups, slices, deps, name):
    sizes = [len(srcs) for srcs, _ in groups]
    flat_src = [s for srcs, _ in groups for s in srcs]
    flat_land = [l for _, lands in groups for l in lands]
    na, ng, nd = len(flat_src), len(groups), len(deps)

    def body(*refs):
        src_refs, land_refs = refs[:na], refs[na:2 * na]
        sem_refs = refs[2 * na + nd:2 * na + nd + 2 * ng]
        token_ref = refs[-1]
        x, y, c, me = _mesh_place()
        a0 = 0
        for gi, size in enumerate(sizes):
            send_sems, recv_sems = sem_refs[2 * gi], sem_refs[2 * gi + 1]
            for k in range(1, N_DEV):
                for j in range(size):
                    src, dst, _ = slices(src_refs[a0 + j], land_refs[a0 + j], me, k)
                    pltpu.make_async_remote_copy(
                        src_ref=src, dst_ref=dst, send_sem=send_sems.at[(k - 1) * size + j],
                        recv_sem=recv_sems.at[(k - 1) * size + j],
                        device_id=_peer(x, y, c, k), device_id_type=pl.DeviceIdType.MESH).start()
            a0 += size
        token_ref[...] = jnp.zeros_like(token_ref)

    sems = [pltpu.SemaphoreType.DMA(((N_DEV - 1) * size,)) for size in sizes for _ in range(2)]
    thru = [pltpu.HBM(v.shape, v.dtype) for v in flat_src + flat_land]
    outs = pl.pallas_call(
        body, name=name,
        in_specs=[HBM_SPEC] * (2 * na) + [ANY_SPEC] * nd,
        out_specs=[SEM_SPEC] * (2 * ng) + [HBM_SPEC] * (2 * na) + [pl.BlockSpec(memory_space=pltpu.VMEM)],
        out_shape=sems + thru + [jax.ShapeDtypeStruct((8, LANES), F32)],
        input_output_aliases={i: 2 * ng + i for i in range(2 * na)},
        compiler_params=pltpu.CompilerParams(has_side_effects=SIDE_EFFECT),
    )(*[_hbm(v) for v in flat_src + flat_land], *deps)
    started, a0 = [], 0
    for gi, size in enumerate(sizes):
        srcs = outs[2 * ng + a0:2 * ng + a0 + size]
        lands = outs[2 * ng + na + a0:2 * ng + na + a0 + size]
        started.append((outs[2 * gi], outs[2 * gi + 1], list(srcs), list(lands)))
        a0 += size
    return started, outs[-1]


def split_wait(started, slices, after, name):
    send_sems, recv_sems, srcs, lands = started
    na = len(srcs)

    def body(*refs):
        src_refs, land_refs = refs[:na], refs[na:2 * na]
        send_ref, recv_ref = refs[2 * na], refs[2 * na + 1]
        x, y, c, me = _mesh_place()
        for k in range(1, N_DEV):
            for j in range(na):
                src, _, got = slices(src_refs[j], land_refs[j], me, k)
                cp = pltpu.make_async_remote_copy(
                    src_ref=src, dst_ref=got, send_sem=send_ref.at[(k - 1) * na + j], recv_sem=recv_ref.at[(k - 1) * na + j],
                    device_id=_peer(x, y, c, k), device_id_type=pl.DeviceIdType.MESH)
                cp.wait_send()
                cp.wait_recv()

    outs = pl.pallas_call(
        body, name=name,
        in_specs=[HBM_SPEC] * (2 * na) + [SEM_SPEC, SEM_SPEC] + [ANY_SPEC] * len(after),
        out_specs=[HBM_SPEC] * (2 * na),
        out_shape=[pltpu.HBM(v.shape, v.dtype) for v in srcs + lands],
        input_output_aliases={i: i for i in range(2 * na)},
        compiler_params=pltpu.CompilerParams(has_side_effects=SIDE_EFFECT),
    )(*srcs, *lands, send_sems, recv_sems, *after)
    return list(outs[:na]), list(outs[na:])


def place_own(pack, me, name):
    rows, cols = pack.shape

    def body(me_ref, p_ref, o_ref):
        o_ref[...] = p_ref[...]

    return pl.pallas_call(
        body, name=name,
        grid_spec=pltpu.PrefetchScalarGridSpec(
            num_scalar_prefetch=1, grid=(1,),
            in_specs=[pl.BlockSpec((rows, cols), lambda i, me_ref: (0, 0))],
            out_specs=pl.BlockSpec((None, rows, cols), lambda i, me_ref: (me_ref[0], 0, 0))),
        out_shape=jax.ShapeDtypeStruct((N_DEV, rows, cols), F32),
        compiler_params=_params(("arbitrary",)),
    )(me, pack)


def sum_slots(slots, name):
    nd, rows, cols = slots.shape

    def body(s_ref, o_ref):
        acc = s_ref[0]
        for j in range(1, nd):
            acc = acc + s_ref[j]
        o_ref[...] = acc

    return pl.pallas_call(
        body, name=name, grid=(1,),
        in_specs=[pl.BlockSpec((nd, rows, cols), lambda i: (0, 0, 0))],
        out_specs=pl.BlockSpec((rows, cols), lambda i: (0, 0)),
        out_shape=jax.ShapeDtypeStruct((rows, cols), F32),
        compiler_params=_params(("arbitrary",)),
    )(slots)


def _pack_rows(size):
    rows = -(-size // LANES)
    return -(-rows // 8) * 8


def pack_small(parts):
    out = []
    for p in parts:
        flat = p.reshape(-1).astype(F32)
        rows = _pack_rows(flat.shape[0])
        out.append(jnp.pad(flat, (0, rows * LANES - flat.shape[0])).reshape(rows, LANES))
    return jnp.concatenate(out, axis=0)


def unpack_small(pack, shapes):
    out, off = [], 0
    for shp in shapes:
        size = 1
        for s in shp:
            size *= s
        rows = _pack_rows(size)
        out.append(pack[off:off + rows].reshape(-1)[:size].reshape(shp))
        off += rows
    return out


def local_step(x, target, small, get_w, put_g, put_small):
    col = lambda v: v.reshape(-1, 1)
    (wg1, wu1, wd1), deps = get_w(("wg1", "wu1", "wd1"), ())
    x1, sil1, gp1, s1 = ffn_fwd(x, small["ffn1_norm_g"], wg1, wu1, wd1, "ffn1_fwd", deps)
    (win, wout), _ = get_w(("win", "wout"), (x1,))
    qkv_t, zg_t = mix_in_fwd(x1, small["mix_norm_g"], win, col(small["b_in"]), "mix_in_fwd")
    ya_t = attn_fwd(qkv_t, small["attn_sinks"], "attn_fwd")
    lng, lnb = col(small["gmlp_ln_g"]), col(small["gmlp_ln_b"])
    w_s, b_s = small["gmlp_w_s"][0], small["gmlp_b_s"][0]
    yg_t = gmlp_fwd(zg_t, lng, lnb, w_s, b_s, "gmlp_fwd")
    gao, ggo = col(small["attn_out_norm_g"]), col(small["gmlp_out_norm_g"])
    x2 = mix_out_fwd(x1, ya_t, yg_t, gao, ggo, wout, small["b_out"], "mix_out_fwd")
    (wg2, wu2, wd2), _ = get_w(("wg2", "wu2", "wd2"), (x2,))
    x3, sil2, gp2, s2 = ffn_fwd(x2, small["ffn2_norm_g"], wg2, wu2, wd2, "ffn2_fwd")

    gs = {}
    dx3, gs["final_norm_g"], loss = final_bwd(x3, target, small["final_norm_g"].reshape(1, -1), "final_bwd")

    dx2, da, db, h, gs["ffn2_norm_g"] = ffn_bwd(x2, dx3, small["ffn2_norm_g"], sil2, gp2, wg2, wu2, wd2, "ffn2_bwd")
    deps = put_g({"wg2": grad_tn(da, h, 1.0, "ffn2_dwg"), "wu2": grad_tn(db, h, 1.0, "ffn2_dwu"),
                  "wd2": grad_tn(s2, dx3, FFN_RES, "ffn2_dwd")})

    dya_t, dyg_t, y_t, dgao, dggo, gs["b_out"] = mix_out_bwd(dx2, ya_t, yg_t, gao, ggo, wout, "mix_out_bwd", deps)
    gs["attn_out_norm_g"], gs["gmlp_out_norm_g"] = dgao.reshape(1, -1), dggo.reshape(1, -1)
    g_wout = grad_nn([y_t], dx2, "dwout")
    dzg_t, dws, dbs, dlng, dlnb = gmlp_bwd(zg_t, dyg_t, lng, lnb, w_s, b_s, "gmlp_bwd")
    gs["gmlp_w_s"], gs["gmlp_b_s"] = dws[None], dbs[None]
    gs["gmlp_ln_g"], gs["gmlp_ln_b"] = dlng.reshape(1, -1), dlnb.reshape(1, -1)
    dq_t, dk_t, dv_t, dsinks = attn_bwd(qkv_t, dya_t, small["attn_sinks"], "attn_bwd")
    gs["attn_sinks"] = dsinks.reshape(1, -1)
    dx1, h2, gs["mix_norm_g"], dbin = mix_in_bwd(x1, dx2, small["mix_norm_g"], dq_t, dk_t, dv_t, dzg_t, win,
                                                 "mix_in_bwd")
    gs["b_in"] = dbin.reshape(1, -1)
    deps = put_g({"wout": g_wout, "win": grad_nn([dq_t, dk_t, dv_t, dzg_t], h2, "dwin")})

    dx0, da, db, h, gs["ffn1_norm_g"] = ffn_bwd(x, dx1, small["ffn1_norm_g"], sil1, gp1, wg1, wu1, wd1, "ffn1_bwd",
                                                deps)
    deps = put_small(gs, loss)
    deps = put_g({"wg1": grad_tn(da, h, 1.0, "ffn1_dwg", deps)})
    deps = put_g({"wu1": grad_tn(db, h, 1.0, "ffn1_dwu", deps)})
    put_g({"wd1": grad_tn(s1, dx1, FFN_RES, "ffn1_dwd", deps)})
    return dx0


SMALL_NAMES = ["ffn1_norm_g", "mix_norm_g", "b_in", "attn_sinks", "gmlp_ln_g", "gmlp_ln_b", "gmlp_w_s", "gmlp_b_s",
               "attn_out_norm_g", "gmlp_out_norm_g", "b_out", "ffn2_norm_g", "final_norm_g"]
BIG_NAMES = {"wg1": ("ffn1_w_gate", True), "wu1": ("ffn1_w_up", True), "wd1": ("ffn1_w_down", False),
             "win": ("w_in", True), "wout": ("w_out", False),
             "wg2": ("ffn2_w_gate", True), "wu2": ("ffn2_w_up", True), "wd2": ("ffn2_w_down", False)}
WEIGHT_ORDER = ["ffn1_norm_g", "ffn1_w_gate", "ffn1_w_up", "ffn1_w_down", "mix_norm_g", "w_in", "b_in", "attn_sinks",
                "gmlp_ln_g", "gmlp_ln_b", "gmlp_w_s", "gmlp_b_s", "attn_out_norm_g", "gmlp_out_norm_g", "w_out",
                "b_out", "ffn2_norm_g", "ffn2_w_gate", "ffn2_w_up", "ffn2_w_down", "final_norm_g"]


def kernel(x, ffn1_norm_g, ffn1_w_gate, ffn1_w_up, ffn1_w_down, mix_norm_g, w_in, b_in, attn_sinks, gmlp_ln_g, gmlp_ln_b, gmlp_w_s, gmlp_b_s, attn_out_norm_g, gmlp_out_norm_g, w_out, b_out, ffn2_norm_g, ffn2_w_gate, ffn2_w_up, ffn2_w_down, final_norm_g, loss_target, m_ffn1_norm_g, m_ffn1_w_gate, m_ffn1_w_up, m_ffn1_w_down, m_mix_norm_g, m_w_in, m_b_in, m_attn_sinks, m_gmlp_ln_g, m_gmlp_ln_b, m_gmlp_w_s, m_gmlp_b_s, m_attn_out_norm_g, m_gmlp_out_norm_g, m_w_out, m_b_out, m_ffn2_norm_g, m_ffn2_w_gate, m_ffn2_w_up, m_ffn2_w_down, m_final_norm_g, v_ffn1_norm_g, v_ffn1_w_gate, v_ffn1_w_up, v_ffn1_w_down, v_mix_norm_g, v_w_in, v_b_in, v_attn_sinks, v_gmlp_ln_g, v_gmlp_ln_b, v_gmlp_w_s, v_gmlp_b_s, v_attn_out_norm_g, v_gmlp_out_norm_g, v_w_out, v_b_out, v_ffn2_norm_g, v_ffn2_w_gate, v_ffn2_w_up, v_ffn2_w_down, v_final_norm_g):
    args = dict(locals())
    w = {n: args[n] for n in WEIGHT_ORDER}
    m = {n: args["m_" + n] for n in WEIGHT_ORDER}
    v = {n: args["v_" + n] for n in WEIGHT_ORDER}

    shards = {}
    for key, (pname, transposed) in BIG_NAMES.items():
        mat = w[pname][0]
        shards[key] = (mat.T if transposed else mat).astype(BF16)
    d_model = x.shape[-1]
    flat = lambda g: g.reshape(-1, d_model)
    me = _mesh_place()[3].astype(jnp.int32).reshape(1)
    first = ("wg1", "wu1", "wd1")
    later = (("win", "wout"), ("wg2", "wu2", "wd2"))
    gathers, exchanges, last_token = {}, [], []

    def get_w(keys, after):
        if keys == first:
            order = list(first) + [k for grp in later for k in grp]
            full = dict(zip(order, allgather_rows([shards[k] for k in order], len(first), "allgather_ffn1")))
            started, token = split_start([([shards[k] for k in grp], [full[k] for k in grp]) for grp in later],
                                         gather_slices, (), "gather_start")
            gathers.update(zip(later, started))
            return [flat(full[k]) for k in keys], (token,)
        _, lands = split_wait(gathers[keys], gather_slices, after, "gather_wait_" + keys[0])
        return [flat(land) for land in lands], ()

    def put_g(gb):
        keys = tuple(gb)
        g3 = [gb[k].reshape(N_DEV, -1, d_model) for k in keys]
        lands = [lax.empty((N_DEV - 1,) + g.shape[1:], BF16) for g in g3]
        started, token = split_start([(g3, lands)], exchange_slices, (), "exchange_start_" + keys[0])
        exchanges.append((keys, started[0]))
        last_token[:] = [token]
        return (token,)

    small_started = []

    def put_small(gs, loss):
        pack = pack_small([gs[n] for n in SMALL_NAMES] + [loss[:, :1]])
        started, token = split_start([([pack], [place_own(pack, me, "place_small")])], gather_slices, (), "small_start")
        small_started[:] = started
        return (token,)

    small = {n: w[n] for n in SMALL_NAMES}
    grad_x = local_step(x[0], loss_target[0], small, get_w, put_g, put_small)

    grads, deltas, new_m, new_v = {}, {}, {}, {}
    after = tuple(last_token)
    for keys, started in exchanges:
        g3, lands = split_wait(started, exchange_slices, after, "exchange_wait_" + keys[0])
        for key, own, land in zip(keys, g3, lands):
            pname, transposed = BIG_NAMES[key]
            g = sum_partials(land, own, me, "sum_" + key)
            after = (g,)
            g = g.T if transposed else g
            d_, m_, v_ = adamw(w[pname][0], g, m[pname][0], v[pname][0], "adamw_" + key)
            grads[pname], deltas[pname], new_m[pname], new_v[pname] = g[None], d_[None], m_[None], v_[None]

    shapes = [w[n].shape for n in SMALL_NAMES]
    _, (slots,) = split_wait(small_started[0], gather_slices, after, "small_wait")
    total = sum_slots(slots, "sum_small")
    small_g = unpack_small(total, shapes + [(1, 1)])
    loss_total = small_g[-1].reshape(())
    gpack = pack_small(small_g[:-1])
    d_, m_, v_ = adamw(pack_small([w[n] for n in SMALL_NAMES]), gpack, pack_small([m[n] for n in SMALL_NAMES]),
                       pack_small([v[n] for n in SMALL_NAMES]), "adamw_small")
    for n, g_, dd, mm, vv in zip(SMALL_NAMES, small_g[:-1], unpack_small(d_, shapes), unpack_small(m_, shapes),
                                 unpack_small(v_, shapes)):
        grads[n], deltas[n], new_m[n], new_v[n] = g_, dd, mm, vv

    return (loss_total, grad_x[None], *[grads[n] for n in WEIGHT_ORDER], *[deltas[n] for n in WEIGHT_ORDER],
            *[new_m[n] for n in WEIGHT_ORDER], *[new_v[n] for n in WEIGHT_ORDER])
```

```python
import functools

import jax
import jax.numpy as jnp
from jax import lax
from jax.experimental import pallas as pl
from jax.experimental.pallas import tpu as pltpu

F32 = jnp.float32
BF16 = jnp.bfloat16

N_DEV = 8
N_Q_HEADS = 8
N_KV_HEADS = 2
HEAD_DIM = 64
ATTN_WIDTH = N_Q_HEADS * HEAD_DIM
KV_WIDTH = N_KV_HEADS * HEAD_DIM
QKV_WIDTH = ATTN_WIDTH + 2 * KV_WIDTH
BLOCK = 128
GMLP_GROUPS = 8
GMLP_GROUP_DIM = 64
GMLP_WIDTH = GMLP_GROUPS * GMLP_GROUP_DIM
EPS = 1e-6
FFN_RES = 0.5
ATTN_SCALE = HEAD_DIM ** -0.5
MASK_VALUE = -1e30

ADAM_LR = 0.001
ADAM_B1 = 0.9
ADAM_B2 = 0.999
ADAM_EPS = 1e-08
ADAM_WD = 0.01
ADAM_STEP = 10

V7X_VMEM_BYTES = 64 * 1024 * 1024
VMEM_LIMIT = 62 * 1024 * 1024
LANES = 128
MXU_WIDTH = 256

NT = (((1,), (1,)), ((), ()))
TN = (((0,), (0,)), ((), ()))


def _dot(a, b):
    return jnp.dot(a, b, preferred_element_type=F32)


def _dg(a, b, dims):
    return lax.dot_general(a, b, dims, preferred_element_type=F32)


def _params(semantics=None):
    return pltpu.CompilerParams(dimension_semantics=semantics, vmem_limit_bytes=VMEM_LIMIT)


def _resident(shape):
    zeros = (0,) * len(shape)
    return pl.BlockSpec(shape, lambda *_: zeros, pipeline_mode=pl.Buffered(1))


def _drop_deps(body, n_in, n_deps):
    return lambda *refs: body(*refs[:n_in], *refs[n_in + n_deps:])


ANY_SPEC = pl.BlockSpec(memory_space=pl.ANY)


def _token_tile(t):
    return min(t, 512)


def _f_chunk(f):
    for c in (256, 128):
        if f % c == 0:
            return c
    return f


def ffn_fwd(x, gain, wg_t, wu_t, wd, name, deps=()):
    t, d = x.shape
    f = wd.shape[0]
    tm = _token_tile(t)
    fc = _f_chunk(f)

    def body(x_ref, g_ref, wg_ref, wu_ref, wd_ref, xo_ref, sil_ref, gp_ref, s_ref):
        xv = x_ref[...]
        r = lax.rsqrt(jnp.mean(xv * xv, axis=-1, keepdims=True) + EPS)
        h = (xv * r * g_ref[...]).astype(BF16)
        for c in range(f // fc):
            sl = slice(c * fc, (c + 1) * fc)
            a = _dg(h, wg_ref[sl, :], NT)
            b = _dg(h, wu_ref[sl, :], NT)
            sig = jax.nn.sigmoid(a)
            sil = a * sig
            sil_ref[:, sl] = sil.astype(BF16)
            gp_ref[:, sl] = (b * (sig + sil - sil * sig)).astype(BF16)
            s_ref[:, sl] = (sil * b).astype(BF16)
        xo_ref[...] = xv + FFN_RES * _dot(s_ref[...], wd_ref[...])

    wide = lambda: pl.BlockSpec((tm, f), lambda i: (i, 0))
    return pl.pallas_call(
        _drop_deps(body, 5, len(deps)), name=name, grid=(t // tm,),
        in_specs=[pl.BlockSpec((tm, d), lambda i: (i, 0)), _resident((1, d)),
                  _resident((f, d)), _resident((f, d)), _resident((f, d))] + [ANY_SPEC] * len(deps),
        out_specs=[pl.BlockSpec((tm, d), lambda i: (i, 0)), wide(), wide(), wide()],
        out_shape=[jax.ShapeDtypeStruct((t, d), F32)] + [jax.ShapeDtypeStruct((t, f), BF16)] * 3,
        compiler_params=_params(("arbitrary",)),
    )(x, gain, wg_t, wu_t, wd, *deps)


def _rmsnorm_bwd_rows(xv, r, gain, dh):
    n = xv * r
    dgain = jnp.sum(dh * n, axis=0, keepdims=True)
    dn = dh * gain
    dx = r * (dn - n * jnp.mean(dn * n, axis=-1, keepdims=True))
    return dx, dgain


def ffn_bwd(x, dxo, gain, sil, gp, wg_t, wu_t, wd, name, deps=()):
    t, d = x.shape
    f = wd.shape[0]
    tm = _token_tile(t)
    nh = 2 if f % (2 * LANES) == 0 else 1
    fh = f // nh
    chunks = [(off, min(MXU_WIDTH, fh - off)) for off in range(0, fh, MXU_WIDTH)]

    def body(x_ref, dxo_ref, g_ref, sil_ref, gp_ref, wg_ref, wu_ref, wd_ref,
             dx_ref, da_ref, db_ref, h_ref, dg_ref, dh_ref):
        i, j = pl.program_id(0), pl.program_id(1)
        gain_v = g_ref[...]

        @pl.when(j == 0)
        def _():
            xv = x_ref[...]
            r = lax.rsqrt(jnp.mean(xv * xv, axis=-1, keepdims=True) + EPS)
            h_ref[...] = (xv * r * gain_v).astype(BF16)
            dh_ref[...] = jnp.zeros_like(dh_ref)

        df = (FFN_RES * dxo_ref[...]).astype(BF16)
        for off, size in chunks:
            sl = slice(off, off + size)
            rows = pl.ds(pl.multiple_of(j * fh + off, LANES), size)
            ds = _dg(df, wd_ref[rows, :], NT)
            da_ref[:, sl] = (ds * gp_ref[:, sl].astype(F32)).astype(BF16)
            db_ref[:, sl] = (ds * sil_ref[:, sl].astype(F32)).astype(BF16)
        half_rows = pl.ds(pl.multiple_of(j * fh, LANES), fh)
        dh_ref[...] += _dot(da_ref[...], wg_ref[half_rows, :]) + _dot(db_ref[...], wu_ref[half_rows, :])

        @pl.when(j == nh - 1)
        def _():
            xv = x_ref[...]
            r = lax.rsqrt(jnp.mean(xv * xv, axis=-1, keepdims=True) + EPS)
            dx, dgain = _rmsnorm_bwd_rows(xv, r, gain_v, dh_ref[...])
            dx_ref[...] = dxo_ref[...] + dx

            @pl.when(i == 0)
            def _():
                dg_ref[...] = jnp.zeros_like(dg_ref)

            dg_ref[...] += dgain

    tok = lambda: pl.BlockSpec((tm, d), lambda i, j: (i, 0))
    half = lambda: pl.BlockSpec((tm, fh), lambda i, j: (i, j))
    return pl.pallas_call(
        _drop_deps(body, 8, len(deps)), name=name, grid=(t // tm, nh),
        in_specs=[tok(), tok(), _resident((1, d)), half(), half(),
                  _resident((f, d)), _resident((f, d)), _resident((f, d))] + [ANY_SPEC] * len(deps),
        out_specs=[tok(), half(), half(), tok(), pl.BlockSpec((1, d), lambda i, j: (0, 0))],
        out_shape=[jax.ShapeDtypeStruct((t, d), F32), jax.ShapeDtypeStruct((t, f), BF16),
                   jax.ShapeDtypeStruct((t, f), BF16), jax.ShapeDtypeStruct((t, d), BF16),
                   jax.ShapeDtypeStruct((1, d), F32)],
        scratch_shapes=[pltpu.VMEM((tm, d), F32)],
        compiler_params=_params(("arbitrary", "arbitrary")),
    )(x, dxo, gain, sil, gp, wg_t, wu_t, wd, *deps)


def _row_tile(m, cap):
    if m <= cap:
        return m
    best = None
    for bm in range(LANES, cap + 1, LANES):
        if m % bm == 0:
            best = bm
    return best if best is not None else m


def grad_tn(a, b, scale, name, deps=()):
    t, m = a.shape
    n = b.shape[1]
    bm = _row_tile(m, 1408)
    bk = min(t, 1024)
    nk = t // bk

    def body(a_ref, b_ref, o_ref, acc_ref):
        k = pl.program_id(1)

        @pl.when(k == 0)
        def _():
            acc_ref[...] = jnp.zeros_like(acc_ref)

        acc_ref[...] += _dg(a_ref[...].astype(BF16), b_ref[...].astype(BF16), TN)

        @pl.when(k == nk - 1)
        def _():
            o_ref[...] = (scale * acc_ref[...]).astype(BF16)

    return pl.pallas_call(
        _drop_deps(body, 2, len(deps)), name=name, grid=(m // bm, nk),
        in_specs=[pl.BlockSpec((bk, bm), lambda i, k: (k, i)), pl.BlockSpec((bk, n), lambda i, k: (k, 0))]
        + [ANY_SPEC] * len(deps),
        out_specs=pl.BlockSpec((bm, n), lambda i, k: (i, 0)),
        out_shape=jax.ShapeDtypeStruct((m, n), BF16),
        scratch_shapes=[pltpu.VMEM((bm, n), F32)],
        compiler_params=_params(("arbitrary", "arbitrary")),
    )(a, b, *deps)


def grad_nn(a_list, b, name):
    t, n = b.shape
    ms = [a.shape[0] for a in a_list]
    m = sum(ms)
    bk = min(t, 1024)
    nk = t // bk
    na = len(a_list)

    def body(*refs):
        a_refs, b_ref, o_ref, acc_ref = refs[:na], refs[na], refs[na + 1], refs[na + 2]
        k = pl.program_id(0)

        @pl.when(k == 0)
        def _():
            acc_ref[...] = jnp.zeros_like(acc_ref)

        bv = b_ref[...].astype(BF16)
        off = 0
        for a_ref, mi in zip(a_refs, ms):
            acc_ref[off:off + mi, :] += _dot(a_ref[...].astype(BF16), bv)
            off += mi

        @pl.when(k == nk - 1)
        def _():
            o_ref[...] = acc_ref[...].astype(BF16)

    return pl.pallas_call(
        body, name=name, grid=(nk,),
        in_specs=[pl.BlockSpec((mi, bk), lambda k: (0, k)) for mi in ms] + [pl.BlockSpec((bk, n), lambda k: (k, 0))],
        out_specs=pl.BlockSpec((m, n), lambda k: (0, 0)),
        out_shape=jax.ShapeDtypeStruct((m, n), BF16),
        scratch_shapes=[pltpu.VMEM((m, n), F32)],
        compiler_params=_params(("arbitrary",)),
    )(*a_list, b)


def mix_in_fwd(x, gain, win_t, b_in_col, name):
    t, d = x.shape
    w = win_t.shape[0]
    tm = _token_tile(t)

    def body(x_ref, g_ref, w_ref, bias_ref, qkv_ref, zg_ref):
        xv = x_ref[...]
        r = lax.rsqrt(jnp.mean(xv * xv, axis=-1, keepdims=True) + EPS)
        h = (xv * r * g_ref[...]).astype(BF16)
        qkv_ref[...] = (_dg(w_ref[:QKV_WIDTH, :], h, NT) + bias_ref[:QKV_WIDTH, :]).astype(BF16)
        zg_ref[...] = (_dg(w_ref[QKV_WIDTH:, :], h, NT) + bias_ref[QKV_WIDTH:, :]).astype(BF16)

    return pl.pallas_call(
        body, name=name, grid=(t // tm,),
        in_specs=[pl.BlockSpec((tm, d), lambda i: (i, 0)), _resident((1, d)), _resident((w, d)), _resident((w, 1))],
        out_specs=[pl.BlockSpec((QKV_WIDTH, tm), lambda i: (0, i)), pl.BlockSpec((w - QKV_WIDTH, tm), lambda i: (0, i))],
        out_shape=[jax.ShapeDtypeStruct((QKV_WIDTH, t), BF16), jax.ShapeDtypeStruct((w - QKV_WIDTH, t), BF16)],
        compiler_params=_params(("arbitrary",)),
    )(x, gain, win_t, b_in_col)


def _attn_specs(nb):
    last = nb - 1
    cur = lambda n: jnp.minimum(n, last)
    prev = lambda n: jnp.maximum(jnp.minimum(n, last) - 1, 0)
    k_row = ATTN_WIDTH // KV_WIDTH
    return [
        pl.BlockSpec((ATTN_WIDTH, BLOCK), lambda n: (0, cur(n))),
        pl.BlockSpec((KV_WIDTH, BLOCK), lambda n: (k_row, prev(n))),
        pl.BlockSpec((KV_WIDTH, BLOCK), lambda n: (k_row, cur(n))),
        pl.BlockSpec((KV_WIDTH, BLOCK), lambda n: (k_row + 1, prev(n))),
        pl.BlockSpec((KV_WIDTH, BLOCK), lambda n: (k_row + 1, cur(n))),
    ]


def _attn_probs(q, kp, kc, sink, mask_prev, mask_cur):
    sp = jnp.where(mask_prev, _dg(kp, q, TN) * ATTN_SCALE, MASK_VALUE)
    sc = jnp.where(mask_cur, _dg(kc, q, TN) * ATTN_SCALE, MASK_VALUE)
    m = jnp.maximum(jnp.maximum(jnp.max(sp, axis=0, keepdims=True), jnp.max(sc, axis=0, keepdims=True)), sink)
    pp = jnp.exp(sp - m)
    pc = jnp.exp(sc - m)
    es = jnp.exp(sink - m)
    inv = 1.0 / (jnp.sum(pp, axis=0, keepdims=True) + jnp.sum(pc, axis=0, keepdims=True) + es)
    return pp * inv, pc * inv, es * inv


def _attn_masks(n):
    key = lax.broadcasted_iota(jnp.int32, (BLOCK, BLOCK), 0)
    qry = lax.broadcasted_iota(jnp.int32, (BLOCK, BLOCK), 1)
    return (key > qry) & (n > 0), key <= qry


def attn_fwd(qkv_t, sinks, name):
    t = qkv_t.shape[1]
    nb = t // BLOCK
    rep = N_Q_HEADS // N_KV_HEADS

    def body(sink_ref, q_ref, kp_ref, kc_ref, vp_ref, vc_ref, y_ref):
        n = pl.program_id(0)
        mask_prev, mask_cur = _attn_masks(n)
        for g in range(N_KV_HEADS):
            gs = slice(g * HEAD_DIM, (g + 1) * HEAD_DIM)
            kp, kc, vp, vc = kp_ref[gs, :], kc_ref[gs, :], vp_ref[gs, :], vc_ref[gs, :]
            for rr in range(rep):
                hd = g * rep + rr
                hs = slice(hd * HEAD_DIM, (hd + 1) * HEAD_DIM)
                pp, pc, _ = _attn_probs(q_ref[hs, :], kp, kc, sink_ref[0, hd], mask_prev, mask_cur)
                y_ref[hs, :] = (_dot(vp, pp.astype(BF16)) + _dot(vc, pc.astype(BF16))).astype(BF16)

    return pl.pallas_call(
        body, name=name, grid=(nb,),
        in_specs=[pl.BlockSpec(memory_space=pltpu.SMEM)] + _attn_specs(nb),
        out_specs=pl.BlockSpec((ATTN_WIDTH, BLOCK), lambda n: (0, n)),
        out_shape=jax.ShapeDtypeStruct((ATTN_WIDTH, t), BF16),
        compiler_params=_params(("arbitrary",)),
    )(sinks, qkv_t, qkv_t, qkv_t, qkv_t, qkv_t)


def attn_bwd(qkv_t, dy_t, sinks, name):
    t = qkv_t.shape[1]
    nb = t // BLOCK
    rep = N_Q_HEADS // N_KV_HEADS

    def body(sink_ref, q_ref, kp_ref, kc_ref, vp_ref, vc_ref, do_ref, dq_ref, dk_ref, dv_ref, dsink_ref,
             ck_ref, cv_ref, sacc_ref):
        n = pl.program_id(0)

        @pl.when(n == 0)
        def _():
            sacc_ref[...] = jnp.zeros_like(sacc_ref)

        @pl.when(n < nb)
        def _():
            mask_prev, mask_cur = _attn_masks(n)
            for g in range(N_KV_HEADS):
                gs = slice(g * HEAD_DIM, (g + 1) * HEAD_DIM)
                kp, kc, vp, vc = kp_ref[gs, :], kc_ref[gs, :], vp_ref[gs, :], vc_ref[gs, :]
                dkp = jnp.zeros((HEAD_DIM, BLOCK), F32)
                dkc = jnp.zeros((HEAD_DIM, BLOCK), F32)
                dvp = jnp.zeros((HEAD_DIM, BLOCK), F32)
                dvc = jnp.zeros((HEAD_DIM, BLOCK), F32)
                for rr in range(rep):
                    hd = g * rep + rr
                    hs = slice(hd * HEAD_DIM, (hd + 1) * HEAD_DIM)
                    q = q_ref[hs, :]
                    do = do_ref[hs, :]
                    pp, pc, ps = _attn_probs(q, kp, kc, sink_ref[0, hd], mask_prev, mask_cur)
                    dpp = _dg(vp, do, TN)
                    dpc = _dg(vc, do, TN)
                    delta = jnp.sum(pp * dpp, axis=0, keepdims=True) + jnp.sum(pc * dpc, axis=0, keepdims=True)
                    dsp = (pp * (dpp - delta)).astype(BF16)
                    dsc = (pc * (dpc - delta)).astype(BF16)
                    sacc_ref[hd:hd + 1, :] += -(ps * delta)
                    dq_ref[hs, :] = (ATTN_SCALE * (_dot(kp, dsp) + _dot(kc, dsc))).astype(BF16)
                    dkp = dkp + _dg(q, dsp, NT)
                    dkc = dkc + _dg(q, dsc, NT)
                    dvp = dvp + _dg(do, pp.astype(BF16), NT)
                    dvc = dvc + _dg(do, pc.astype(BF16), NT)

                @pl.when(n > 0)
                def _():
                    dk_ref[gs, :] = (ck_ref[gs, :] + ATTN_SCALE * dkp).astype(BF16)
                    dv_ref[gs, :] = (cv_ref[gs, :] + dvp).astype(BF16)

                ck_ref[gs, :] = ATTN_SCALE * dkc
                cv_ref[gs, :] = dvc

        @pl.when(n == nb)
        def _():
            dk_ref[...] = ck_ref[...].astype(BF16)
            dv_ref[...] = cv_ref[...].astype(BF16)
            dsink_ref[...] = jnp.sum(sacc_ref[...], axis=1, keepdims=True)

    last = nb - 1
    done = lambda n: jnp.maximum(n - 1, 0)
    outs = pl.pallas_call(
        body, name=name, grid=(nb + 1,),
        in_specs=[pl.BlockSpec(memory_space=pltpu.SMEM)] + _attn_specs(nb)
        + [pl.BlockSpec((ATTN_WIDTH, BLOCK), lambda n: (0, jnp.minimum(n, last)))],
        out_specs=[pl.BlockSpec((ATTN_WIDTH, BLOCK), lambda n: (0, jnp.minimum(n, last))),
                   pl.BlockSpec((KV_WIDTH, BLOCK), lambda n: (0, done(n))),
                   pl.BlockSpec((KV_WIDTH, BLOCK), lambda n: (0, done(n))),
                   pl.BlockSpec((N_Q_HEADS, 1), lambda n: (0, 0))],
        out_shape=[jax.ShapeDtypeStruct((ATTN_WIDTH, t), BF16), jax.ShapeDtypeStruct((KV_WIDTH, t), BF16),
                   jax.ShapeDtypeStruct((KV_WIDTH, t), BF16), jax.ShapeDtypeStruct((N_Q_HEADS, 1), F32)],
        scratch_shapes=[pltpu.VMEM((KV_WIDTH, BLOCK), F32), pltpu.VMEM((KV_WIDTH, BLOCK), F32),
                        pltpu.VMEM((N_Q_HEADS, BLOCK), F32)],
        compiler_params=_params(("arbitrary",)),
    )(sinks, qkv_t, qkv_t, qkv_t, qkv_t, qkv_t, dy_t)
    return outs


GELU_C = 0.7978845608028654
GELU_A = 0.044715


def _gelu(x):
    return 0.5 * x * (1.0 + jnp.tanh(GELU_C * (x + GELU_A * x * x * x)))


def _gelu_grad(x):
    th = jnp.tanh(GELU_C * (x + GELU_A * x * x * x))
    return 0.5 * (1.0 + th) + 0.5 * x * (1.0 - th * th) * GELU_C * (1.0 + 3.0 * GELU_A * x * x)


def _gmlp_parts(zg, lng, lnb):
    z = _gelu(zg)
    u = z[:GMLP_WIDTH, :]
    vv = z[GMLP_WIDTH:, :]
    mu = jnp.mean(vv, axis=0, keepdims=True)
    xc = vv - mu
    rstd = lax.rsqrt(jnp.mean(xc * xc, axis=0, keepdims=True) + EPS)
    xhat = xc * rstd
    return u, xhat, rstd, xhat * lng + lnb


def _causal(w):
    row = lax.broadcasted_iota(jnp.int32, (BLOCK, BLOCK), 0)
    col = lax.broadcasted_iota(jnp.int32, (BLOCK, BLOCK), 1)
    return jnp.where(col <= row, w, 0.0)


def gmlp_fwd(zg_t, lng, lnb, w_s, b_s, name):
    t = zg_t.shape[1]
    nb = t // BLOCK

    def body(zg_ref, lng_ref, lnb_ref, w_ref, bs_ref, y_ref):
        u, _, _, vn = _gmlp_parts(zg_ref[...].astype(F32), lng_ref[...], lnb_ref[...])
        for g in range(GMLP_GROUPS):
            gs = slice(g * GMLP_GROUP_DIM, (g + 1) * GMLP_GROUP_DIM)
            wc = _causal(w_ref[g]).astype(BF16)
            mixed = _dg(vn[gs, :].astype(BF16), wc, NT) + bs_ref[g:g + 1, :]
            y_ref[gs, :] = (u[gs, :] * mixed).astype(BF16)

    return pl.pallas_call(
        body, name=name, grid=(nb,),
        in_specs=[pl.BlockSpec((2 * GMLP_WIDTH, BLOCK), lambda n: (0, n)), _resident((GMLP_WIDTH, 1)),
                  _resident((GMLP_WIDTH, 1)), _resident((GMLP_GROUPS, BLOCK, BLOCK)), _resident((GMLP_GROUPS, BLOCK))],
        out_specs=pl.BlockSpec((GMLP_WIDTH, BLOCK), lambda n: (0, n)),
        out_shape=jax.ShapeDtypeStruct((GMLP_WIDTH, t), BF16),
        compiler_params=_params(("arbitrary",)),
    )(zg_t, lng, lnb, w_s, b_s)


def gmlp_bwd(zg_t, dy_t, lng, lnb, w_s, b_s, name):
    t = zg_t.shape[1]
    nb = t // BLOCK

    def body(zg_ref, dy_ref, lng_ref, lnb_ref, w_ref, bs_ref, dzg_ref, dw_ref, dbs_ref, dlng_ref, dlnb_ref,
             gacc_ref, bacc_ref):
        n = pl.program_id(0)

        @pl.when(n == 0)
        def _():
            dw_ref[...] = jnp.zeros_like(dw_ref)
            dbs_ref[...] = jnp.zeros_like(dbs_ref)
            gacc_ref[...] = jnp.zeros_like(gacc_ref)
            bacc_ref[...] = jnp.zeros_like(bacc_ref)

        zg = zg_ref[...].astype(F32)
        lng_v = lng_ref[...]
        u, xhat, rstd, vn = _gmlp_parts(zg, lng_v, lnb_ref[...])
        dy = dy_ref[...].astype(F32)
        dvn_parts = []
        for g in range(GMLP_GROUPS):
            gs = slice(g * GMLP_GROUP_DIM, (g + 1) * GMLP_GROUP_DIM)
            wc = _causal(w_ref[g]).astype(BF16)
            vn_g = vn[gs, :].astype(BF16)
            mixed = _dg(vn_g, wc, NT) + bs_ref[g:g + 1, :]
            dy_g = dy[gs, :]
            dmixed = dy_g * u[gs, :]
            dmixed_b = dmixed.astype(BF16)
            dzg_ref[gs, :] = (dy_g * mixed * _gelu_grad(zg[gs, :])).astype(BF16)
            dw_ref[g] += _causal(_dg(dmixed_b, vn_g, TN))
            dbs_ref[g:g + 1, :] += jnp.sum(dmixed, axis=0, keepdims=True)
            dvn_parts.append(_dot(dmixed_b, wc))
        dvn = jnp.concatenate(dvn_parts, axis=0)
        gacc_ref[...] += dvn * xhat
        bacc_ref[...] += dvn
        dxhat = dvn * lng_v
        m1 = jnp.mean(dxhat, axis=0, keepdims=True)
        m2 = jnp.mean(dxhat * xhat, axis=0, keepdims=True)
        dvv = rstd * (dxhat - m1 - xhat * m2)
        dzg_ref[GMLP_WIDTH:, :] = (dvv * _gelu_grad(zg[GMLP_WIDTH:, :])).astype(BF16)

        @pl.when(n == nb - 1)
        def _():
            dlng_ref[...] = jnp.sum(gacc_ref[...], axis=1, keepdims=True)
            dlnb_ref[...] = jnp.sum(bacc_ref[...], axis=1, keepdims=True)

    const2 = lambda n: (0, 0)
    return pl.pallas_call(
        body, name=name, grid=(nb,),
        in_specs=[pl.BlockSpec((2 * GMLP_WIDTH, BLOCK), lambda n: (0, n)), pl.BlockSpec((GMLP_WIDTH, BLOCK), lambda n: (0, n)),
                  _resident((GMLP_WIDTH, 1)), _resident((GMLP_WIDTH, 1)),
                  _resident((GMLP_GROUPS, BLOCK, BLOCK)), _resident((GMLP_GROUPS, BLOCK))],
        out_specs=[pl.BlockSpec((2 * GMLP_WIDTH, BLOCK), lambda n: (0, n)),
                   pl.BlockSpec((GMLP_GROUPS, BLOCK, BLOCK), lambda n: (0, 0, 0)),
                   pl.BlockSpec((GMLP_GROUPS, BLOCK), const2),
                   pl.BlockSpec((GMLP_WIDTH, 1), const2), pl.BlockSpec((GMLP_WIDTH, 1), const2)],
        out_shape=[jax.ShapeDtypeStruct((2 * GMLP_WIDTH, t), BF16),
                   jax.ShapeDtypeStruct((GMLP_GROUPS, BLOCK, BLOCK), F32),
                   jax.ShapeDtypeStruct((GMLP_GROUPS, BLOCK), F32),
                   jax.ShapeDtypeStruct((GMLP_WIDTH, 1), F32), jax.ShapeDtypeStruct((GMLP_WIDTH, 1), F32)],
        scratch_shapes=[pltpu.VMEM((GMLP_WIDTH, BLOCK), F32), pltpu.VMEM((GMLP_WIDTH, BLOCK), F32)],
        compiler_params=_params(("arbitrary",)),
    )(zg_t, dy_t, lng, lnb, w_s, b_s)


def _rmsnorm_cols(y, gain_col):
    r = lax.rsqrt(jnp.mean(y * y, axis=0, keepdims=True) + EPS)
    n = y * r
    return n, r, n * gain_col


def mix_out_fwd(x, ya_t, yg_t, gao, ggo, wout, b_out, name):
    t, d = x.shape
    tm = _token_tile(t)

    def body(x_ref, ya_ref, yg_ref, gao_ref, ggo_ref, w_ref, b_ref, o_ref):
        _, _, ya = _rmsnorm_cols(ya_ref[...].astype(F32), gao_ref[...])
        _, _, yg = _rmsnorm_cols(yg_ref[...].astype(F32), ggo_ref[...])
        o_ref[...] = (x_ref[...] + _dg(ya.astype(BF16), w_ref[:ATTN_WIDTH, :], TN)
                      + _dg(yg.astype(BF16), w_ref[ATTN_WIDTH:, :], TN) + b_ref[...])

    return pl.pallas_call(
        body, name=name, grid=(t // tm,),
        in_specs=[pl.BlockSpec((tm, d), lambda i: (i, 0)), pl.BlockSpec((ATTN_WIDTH, tm), lambda i: (0, i)),
                  pl.BlockSpec((GMLP_WIDTH, tm), lambda i: (0, i)), _resident((ATTN_WIDTH, 1)), _resident((GMLP_WIDTH, 1)),
                  _resident((ATTN_WIDTH + GMLP_WIDTH, d)), _resident((1, d))],
        out_specs=pl.BlockSpec((tm, d), lambda i: (i, 0)),
        out_shape=jax.ShapeDtypeStruct((t, d), F32),
        compiler_params=_params(("arbitrary",)),
    )(x, ya_t, yg_t, gao, ggo, wout, b_out)


def _lane_fold(v):
    out = v[:, :LANES]
    for j in range(1, v.shape[1] // LANES):
        out = out + v[:, j * LANES:(j + 1) * LANES]
    return out


def mix_out_bwd(dx, ya_t, yg_t, gao, ggo, wout, name, deps=()):
    t, d = dx.shape
    tm = _token_tile(t)
    nt = t // tm

    def body(dx_ref, ya_ref, yg_ref, gao_ref, ggo_ref, w_ref, dya_ref, dyg_ref, y_ref, dgao_ref, dggo_ref, db_ref,
             acc_ref):
        i = pl.program_id(0)

        @pl.when(i == 0)
        def _():
            acc_ref[...] = jnp.zeros_like(acc_ref)
            db_ref[...] = jnp.zeros_like(db_ref)

        dxv = dx_ref[...]
        db_ref[...] += jnp.sum(dxv, axis=0, keepdims=True)
        dxb = dxv.astype(BF16)
        for part, (src_ref, gain_ref, dst_ref) in enumerate(((ya_ref, gao_ref, dya_ref), (yg_ref, ggo_ref, dyg_ref))):
            rows = slice(part * ATTN_WIDTH, (part + 1) * ATTN_WIDTH)
            gain = gain_ref[...]
            nrm, r, yn = _rmsnorm_cols(src_ref[...].astype(F32), gain)
            y_ref[rows, :] = yn.astype(BF16)
            dy = _dg(w_ref[rows, :], dxb, NT)
            acc_ref[rows, :] += _lane_fold(dy * nrm)
            dn = dy * gain
            dst_ref[...] = (r * (dn - nrm * jnp.mean(dn * nrm, axis=0, keepdims=True))).astype(BF16)

        @pl.when(i == nt - 1)
        def _():
            dgao_ref[...] = jnp.sum(acc_ref[:ATTN_WIDTH, :], axis=1, keepdims=True)
            dggo_ref[...] = jnp.sum(acc_ref[ATTN_WIDTH:, :], axis=1, keepdims=True)

    const2 = lambda i: (0, 0)
    feat = lambda w: pl.BlockSpec((w, tm), lambda i: (0, i))
    return pl.pallas_call(
        _drop_deps(body, 6, len(deps)), name=name, grid=(nt,),
        in_specs=[pl.BlockSpec((tm, d), lambda i: (i, 0)), feat(ATTN_WIDTH), feat(GMLP_WIDTH),
                  _resident((ATTN_WIDTH, 1)), _resident((GMLP_WIDTH, 1)), _resident((ATTN_WIDTH + GMLP_WIDTH, d))]
        + [ANY_SPEC] * len(deps),
        out_specs=[feat(ATTN_WIDTH), feat(GMLP_WIDTH), feat(ATTN_WIDTH + GMLP_WIDTH),
                   pl.BlockSpec((ATTN_WIDTH, 1), const2), pl.BlockSpec((GMLP_WIDTH, 1), const2),
                   pl.BlockSpec((1, d), const2)],
        out_shape=[jax.ShapeDtypeStruct((ATTN_WIDTH, t), BF16), jax.ShapeDtypeStruct((GMLP_WIDTH, t), BF16),
                   jax.ShapeDtypeStruct((ATTN_WIDTH + GMLP_WIDTH, t), BF16),
                   jax.ShapeDtypeStruct((ATTN_WIDTH, 1), F32), jax.ShapeDtypeStruct((GMLP_WIDTH, 1), F32),
                   jax.ShapeDtypeStruct((1, d), F32)],
        scratch_shapes=[pltpu.VMEM((ATTN_WIDTH + GMLP_WIDTH, LANES), F32)],
        compiler_params=_params(("arbitrary",)),
    )(dx, ya_t, yg_t, gao, ggo, wout, *deps)


def mix_in_bwd(x, dxo, gain, dq_t, dk_t, dv_t, dzg_t, win_t, name):
    t, d = x.shape
    w = win_t.shape[0]
    tm = _token_tile(t)
    nt = t // tm
    parts = ((0, ATTN_WIDTH), (ATTN_WIDTH, KV_WIDTH), (ATTN_WIDTH + KV_WIDTH, KV_WIDTH), (QKV_WIDTH, w - QKV_WIDTH))

    def body(x_ref, dxo_ref, g_ref, dq_ref, dk_ref, dv_ref, dzg_ref, w_ref, dx_ref, h_ref, dg_ref, db_ref, acc_ref):
        i = pl.program_id(0)

        @pl.when(i == 0)
        def _():
            acc_ref[...] = jnp.zeros_like(acc_ref)
            dg_ref[...] = jnp.zeros_like(dg_ref)

        xv = x_ref[...]
        gain_v = g_ref[...]
        r = lax.rsqrt(jnp.mean(xv * xv, axis=-1, keepdims=True) + EPS)
        h_ref[...] = (xv * r * gain_v).astype(BF16)
        dh = jnp.zeros((tm, d), F32)
        for (off, size), src_ref in zip(parts, (dq_ref, dk_ref, dv_ref, dzg_ref)):
            dp = src_ref[...]
            acc_ref[off:off + size, :] += _lane_fold(dp.astype(F32))
            dh = dh + _dg(dp, w_ref[off:off + size, :], TN)
        dx, dgain = _rmsnorm_bwd_rows(xv, r, gain_v, dh)
        dx_ref[...] = dxo_ref[...] + dx
        dg_ref[...] += dgain

        @pl.when(i == nt - 1)
        def _():
            db_ref[...] = jnp.sum(acc_ref[...], axis=1, keepdims=True)

    const2 = lambda i: (0, 0)
    tok = lambda: pl.BlockSpec((tm, d), lambda i: (i, 0))
    feat = lambda rows: pl.BlockSpec((rows, tm), lambda i: (0, i))
    return pl.pallas_call(
        body, name=name, grid=(nt,),
        in_specs=[tok(), tok(), _resident((1, d)), feat(ATTN_WIDTH), feat(KV_WIDTH), feat(KV_WIDTH),
                  feat(w - QKV_WIDTH), _resident((w, d))],
        out_specs=[tok(), tok(), pl.BlockSpec((1, d), const2), pl.BlockSpec((w, 1), const2)],
        out_shape=[jax.ShapeDtypeStruct((t, d), F32), jax.ShapeDtypeStruct((t, d), BF16),
                   jax.ShapeDtypeStruct((1, d), F32), jax.ShapeDtypeStruct((w, 1), F32)],
        scratch_shapes=[pltpu.VMEM((w, LANES), F32)],
        compiler_params=_params(("arbitrary",)),
    )(x, dxo, gain, dq_t, dk_t, dv_t, dzg_t, win_t)


def final_bwd(x, target, gain, name):
    t, d = x.shape
    tm = _token_tile(t)

    def body(x_ref, t_ref, g_ref, dx_ref, dg_ref, loss_ref):
        i = pl.program_id(0)

        @pl.when(i == 0)
        def _():
            dg_ref[...] = jnp.zeros_like(dg_ref)
            loss_ref[...] = jnp.zeros_like(loss_ref)

        xv = x_ref[...]
        gain_v = g_ref[...]
        r = lax.rsqrt(jnp.mean(xv * xv, axis=-1, keepdims=True) + EPS)
        err = xv * r * gain_v - t_ref[...]
        row_loss = jnp.mean(err * err, axis=-1, keepdims=True)
        loss_ref[...] += 0.5 * jnp.sum(row_loss, axis=0, keepdims=True)
        dx, dgain = _rmsnorm_bwd_rows(xv, r, gain_v, err * (1.0 / d))
        dx_ref[...] = dx
        dg_ref[...] += dgain

    const2 = lambda i: (0, 0)
    return pl.pallas_call(
        body, name=name, grid=(t // tm,),
        in_specs=[pl.BlockSpec((tm, d), lambda i: (i, 0)), pl.BlockSpec((tm, d), lambda i: (i, 0)), _resident((1, d))],
        out_specs=[pl.BlockSpec((tm, d), lambda i: (i, 0)), pl.BlockSpec((1, d), const2), pl.BlockSpec((1, LANES), const2)],
        out_shape=[jax.ShapeDtypeStruct((t, d), F32), jax.ShapeDtypeStruct((1, d), F32),
                   jax.ShapeDtypeStruct((1, LANES), F32)],
        compiler_params=_params(("arbitrary",)),
    )(x, target, gain)


def _adamw_math(w, g, m, v):
    m = ADAM_B1 * m + (1.0 - ADAM_B1) * g
    v = ADAM_B2 * v + (1.0 - ADAM_B2) * (g * g)
    m_hat = m / (1.0 - ADAM_B1 ** ADAM_STEP)
    v_hat = v / (1.0 - ADAM_B2 ** ADAM_STEP)
    delta = -ADAM_LR * (m_hat / (jnp.sqrt(v_hat) + ADAM_EPS) + ADAM_WD * w)
    return delta, m, v


def adamw(w, g, m, v, name):
    rows, cols = w.shape
    tr = rows
    if rows * cols > 256 * 1024:
        for cand in (256, 128, 64, 32, 16, 8):
            if rows % cand == 0:
                tr = cand
                break

    def body(w_ref, g_ref, m_ref, v_ref, d_ref, mo_ref, vo_ref):
        delta, mn, vn = _adamw_math(w_ref[...], g_ref[...], m_ref[...], v_ref[...])
        d_ref[...] = delta
        mo_ref[...] = mn
        vo_ref[...] = vn

    spec = pl.BlockSpec((tr, cols), lambda i: (i, 0))
    shape = jax.ShapeDtypeStruct((rows, cols), F32)
    return pl.pallas_call(
        body, name=name, grid=(rows // tr,),
        in_specs=[spec] * 4, out_specs=[spec] * 3, out_shape=[shape] * 3,
        compiler_params=_params(("arbitrary",)),
    )(w, g, m, v)


def sum_partials(landing, grad3d, me, name, deps=()):
    nd, rows, cols = landing.shape
    tr = rows
    for cand in (128, 112, 88, 64, 32, 16):
        if rows % cand == 0:
            tr = cand
            break

    def body(me_ref, l_ref, own_ref, o_ref):
        acc = own_ref[...].astype(F32)
        for j in range(nd):
            acc = acc + l_ref[j].astype(F32)
        o_ref[...] = acc

    return pl.pallas_call(
        _drop_deps(body, 3, len(deps)), name=name,
        grid_spec=pltpu.PrefetchScalarGridSpec(
            num_scalar_prefetch=1, grid=(rows // tr,),
            in_specs=[pl.BlockSpec((nd, tr, cols), lambda i, me_ref: (0, i, 0)),
                      pl.BlockSpec((None, tr, cols), lambda i, me_ref: (me_ref[0], i, 0))] + [ANY_SPEC] * len(deps),
            out_specs=pl.BlockSpec((tr, cols), lambda i, me_ref: (i, 0))),
        out_shape=jax.ShapeDtypeStruct((rows, cols), F32),
        compiler_params=_params(("arbitrary",)),
    )(me, landing, grad3d, *deps)


def _mesh_place():
    x, y, c = lax.axis_index("x"), lax.axis_index("y"), lax.axis_index("c")
    return x, y, c, 4 * x + 2 * y + c


def _peer(x, y, c, k):
    return (x ^ ((k >> 2) & 1), y ^ ((k >> 1) & 1), c ^ (k & 1))


def allgather_rows(shards, n_gather, name):
    na = len(shards)

    def body(*refs):
        in_refs, out_refs = refs[:na], refs[na:2 * na]
        send_sems, recv_sems, local_sems = refs[2 * na:]
        x, y, c, me = _mesh_place()
        local = [pltpu.make_async_copy(in_refs[a], out_refs[a].at[me], local_sems.at[a]) for a in range(na)]
        for cp in local:
            cp.start()
        def copy(sem, a, origin, src, k):
            return pltpu.make_async_remote_copy(
                src_ref=src, dst_ref=out_refs[a].at[origin], send_sem=send_sems.at[sem, a],
                recv_sem=recv_sems.at[sem, a], device_id=_peer(x, y, c, k), device_id_type=pl.DeviceIdType.MESH)

        kx, ky = 4, 2

        def recv(sem, a, origin):
            copy(sem, a, origin, in_refs[a], 1).wait_recv()

        def landed(a, origin):
            return out_refs[a].at[origin]

        @pl.when(c == 1)
        def _():
            sends = []

            def send(sem, a, origin, src, k):
                cp = copy(sem, a, origin, src, k)
                cp.start()
                sends.append(cp)

            for a in range(n_gather):
                send(1, a, me, in_refs[a], kx)
                send(2, a, me, in_refs[a], ky)
                send(0, a, me, in_refs[a], 1)
            for a in range(n_gather):
                recv(0, a, me ^ 1)
                send(3, a, me ^ 1, landed(a, me ^ 1), kx)
                send(4, a, me ^ 1, landed(a, me ^ 1), ky)
            for a in range(n_gather):
                recv(1, a, me ^ kx)
                send(5, a, me ^ kx, landed(a, me ^ kx), ky)
                send(7, a, me ^ kx, landed(a, me ^ kx), 1)
                recv(2, a, me ^ ky)
                send(8, a, me ^ ky, landed(a, me ^ ky), 1)
            for a in range(n_gather):
                recv(3, a, me ^ kx ^ 1)
                send(9, a, me ^ kx ^ 1, landed(a, me ^ kx ^ 1), 1)
                recv(4, a, me ^ ky ^ 1)
                send(6, a, me ^ ky ^ 1, landed(a, me ^ ky ^ 1), kx)
                send(10, a, me ^ ky ^ 1, landed(a, me ^ ky ^ 1), 1)
            for a in range(n_gather):
                recv(5, a, me ^ 6)
                send(11, a, me ^ 6, landed(a, me ^ 6), 1)
                recv(6, a, me ^ 7)
                send(12, a, me ^ 7, landed(a, me ^ 7), 1)
            for cp in sends:
                cp.wait_send()

        @pl.when(c == 0)
        def _():
            north = me ^ 1
            own = [copy(0, a, me, in_refs[a], 1) for a in range(n_gather)]
            for cp in own:
                cp.start()
            for a in range(n_gather):
                recv(0, a, north)
                for sem, origin in ((7, north ^ kx), (8, north ^ ky), (9, north ^ kx ^ 1), (10, north ^ ky ^ 1),
                                    (11, north ^ 6), (12, north ^ 7)):
                    recv(sem, a, origin)
            for cp in own:
                cp.wait_send()

        for cp in local:
            cp.wait()

    return pl.pallas_call(
        body, name=name,
        in_specs=[ANY_SPEC] * na, out_specs=[ANY_SPEC] * na,
        out_shape=[jax.ShapeDtypeStruct((N_DEV,) + s.shape, s.dtype) for s in shards],
        scratch_shapes=[pltpu.SemaphoreType.DMA((13, n_gather)), pltpu.SemaphoreType.DMA((13, n_gather)),
                        pltpu.SemaphoreType.DMA((na,))],
    )(*shards)


HBM_SPEC = pl.BlockSpec(memory_space=pltpu.HBM)
SEM_SPEC = pl.BlockSpec(memory_space=pltpu.SEMAPHORE)
SIDE_EFFECT = pltpu.SideEffectType.DATAFLOW_SIDE_EFFECTING


def _hbm(v):
    return pltpu.with_memory_space_constraint(v, pltpu.HBM)


def gather_slices(src, land, me, k):
    return src, land.at[me], land.at[me ^ k]


def exchange_slices(src, land, me, k):
    return src.at[me ^ k], land.at[k - 1], land.at[k - 1]


def split_start(groups, slices, deps, name):
    sizes = [len(srcs) for srcs, _ in groups]
    flat_src = [s for srcs, _ in groups for s in srcs]
    flat_land = [l for _, lands in groups for l in lands]
    na, ng, nd = len(flat_src), len(groups), len(deps)

    def body(*refs):
        src_refs, land_refs = refs[:na], refs[na:2 * na]
        sem_refs = refs[2 * na + nd:2 * na + nd + 2 * ng]
        token_ref = refs[-1]
        x, y, c, me = _mesh_place()
        a0 = 0
        for gi, size in enumerate(sizes):
            send_sems, recv_sems = sem_refs[2 * gi], sem_refs[2 * gi + 1]
            for k in range(1, N_DEV):
                for j in range(size):
                    src, dst, _ = slices(src_refs[a0 + j], land_refs[a0 + j], me, k)
                    pltpu.make_async_remote_copy(
                        src_ref=src, dst_ref=dst, send_sem=send_sems.at[(k - 1) * size + j],
                        recv_sem=recv_sems.at[(k - 1) * size + j],
                        device_id=_peer(x, y, c, k), device_id_type=pl.DeviceIdType.MESH).start()
            a0 += size
        token_ref[...] = jnp.zeros_like(token_ref)

    sems = [pltpu.SemaphoreType.DMA(((N_DEV - 1) * size,)) for size in sizes for _ in range(2)]
    thru = [pltpu.HBM(v.shape, v.dtype) for v in flat_src + flat_land]
    outs = pl.pallas_call(
        body, name=name,
        in_specs=[HBM_SPEC] * (2 * na) + [ANY_SPEC] * nd,
        out_specs=[SEM_SPEC] * (2 * ng) + [HBM_SPEC] * (2 * na) + [pl.BlockSpec(memory_space=pltpu.VMEM)],
        out_shape=sems + thru + [jax.ShapeDtypeStruct((8, LANES), F32)],
        input_output_aliases={i: 2 * ng + i for i in range(2 * na)},
        compiler_params=pltpu.CompilerParams(has_side_effects=SIDE_EFFECT),
    )(*[_hbm(v) for v in flat_src + flat_land], *deps)
    started, a0 = [], 0
    for gi, size in enumerate(sizes):
        srcs = outs[2 * ng + a0:2 * ng + a0 + size]
        lands = outs[2 * ng + na + a0:2 * ng + na + a0 + size]
        started.append((outs[2 * gi], outs[2 * gi + 1], list(srcs), list(lands)))
        a0 += size
    return started, outs[-1]


def split_wait(started, slices, after, name):
    send_sems, recv_sems, srcs, lands = started
    na = len(srcs)

    def body(*refs):
        src_refs, land_refs = refs[:na], refs[na:2 * na]
        send_ref, recv_ref = refs[2 * na], refs[2 * na + 1]
        x, y, c, me = _mesh_place()
        for k in range(1, N_DEV):
            for j in range(na):
                src, _, got = slices(src_refs[j], land_refs[j], me, k)
                cp = pltpu.make_async_remote_copy(
                    src_ref=src, dst_ref=got, send_sem=send_ref.at[(k - 1) * na + j], recv_sem=recv_ref.at[(k - 1) * na + j],
                    device_id=_peer(x, y, c, k), device_id_type=pl.DeviceIdType.MESH)
                cp.wait_send()
                cp.wait_recv()

    outs = pl.pallas_call(
        body, name=name,
        in_specs=[HBM_SPEC] * (2 * na) + [SEM_SPEC, SEM_SPEC] + [ANY_SPEC] * len(after),
        out_specs=[HBM_SPEC] * (2 * na),
        out_shape=[pltpu.HBM(v.shape, v.dtype) for v in srcs + lands],
        input_output_aliases={i: i for i in range(2 * na)},
        compiler_params=pltpu.CompilerParams(has_side_effects=SIDE_EFFECT),
    )(*srcs, *lands, send_sems, recv_sems, *after)
    return list(outs[:na]), list(outs[na:])


def place_own(pack, me, name):
    rows, cols = pack.shape

    def body(me_ref, p_ref, o_ref):
        o_ref[...] = p_ref[...]

    return pl.pallas_call(
        body, name=name,
        grid_spec=pltpu.PrefetchScalarGridSpec(
            num_scalar_prefetch=1, grid=(1,),
            in_specs=[pl.BlockSpec((rows, cols), lambda i, me_ref: (0, 0))],
            out_specs=pl.BlockSpec((None, rows, cols), lambda i, me_ref: (me_ref[0], 0, 0))),
        out_shape=jax.ShapeDtypeStruct((N_DEV, rows, cols), F32),
        compiler_params=_params(("arbitrary",)),
    )(me, pack)


def sum_slots(slots, name):
    nd, rows, cols = slots.shape

    def body(s_ref, o_ref):
        acc = s_ref[0]
        for j in range(1, nd):
            acc = acc + s_ref[j]
        o_ref[...] = acc

    return pl.pallas_call(
        body, name=name, grid=(1,),
        in_specs=[pl.BlockSpec((nd, rows, cols), lambda i: (0, 0, 0))],
        out_specs=pl.BlockSpec((rows, cols), lambda i: (0, 0)),
        out_shape=jax.ShapeDtypeStruct((rows, cols), F32),
        compiler_params=_params(("arbitrary",)),
    )(slots)


def _pack_rows(size):
    rows = -(-size // LANES)
    return -(-rows // 8) * 8


def pack_small(parts):
    out = []
    for p in parts:
        flat = p.reshape(-1).astype(F32)
        rows = _pack_rows(flat.shape[0])
        out.append(jnp.pad(flat, (0, rows * LANES - flat.shape[0])).reshape(rows, LANES))
    return jnp.concatenate(out, axis=0)


def unpack_small(pack, shapes):
    out, off = [], 0
    for shp in shapes:
        size = 1
        for s in shp:
            size *= s
        rows = _pack_rows(size)
        out.append(pack[off:off + rows].reshape(-1)[:size].reshape(shp))
        off += rows
    return out


def local_step(x, target, small, get_w, put_g, put_small):
    col = lambda v: v.reshape(-1, 1)
    (wg1, wu1, wd1), deps = get_w(("wg1", "wu1", "wd1"), ())
    x1, sil1, gp1, s1 = ffn_fwd(x, small["ffn1_norm_g"], wg1, wu1, wd1, "ffn1_fwd", deps)
    (win, wout), _ = get_w(("win", "wout"), (x1,))
    qkv_t, zg_t = mix_in_fwd(x1, small["mix_norm_g"], win, col(small["b_in"]), "mix_in_fwd")
    ya_t = attn_fwd(qkv_t, small["attn_sinks"], "attn_fwd")
    lng, lnb = col(small["gmlp_ln_g"]), col(small["gmlp_ln_b"])
    w_s, b_s = small["gmlp_w_s"][0], small["gmlp_b_s"][0]
    yg_t = gmlp_fwd(zg_t, lng, lnb, w_s, b_s, "gmlp_fwd")
    gao, ggo = col(small["attn_out_norm_g"]), col(small["gmlp_out_norm_g"])
    x2 = mix_out_fwd(x1, ya_t, yg_t, gao, ggo, wout, small["b_out"], "mix_out_fwd")
    (wg2, wu2, wd2), _ = get_w(("wg2", "wu2", "wd2"), (x2,))
    x3, sil2, gp2, s2 = ffn_fwd(x2, small["ffn2_norm_g"], wg2, wu2, wd2, "ffn2_fwd")

    gs = {}
    dx3, gs["final_norm_g"], loss = final_bwd(x3, target, small["final_norm_g"].reshape(1, -1), "final_bwd")

    dx2, da, db, h, gs["ffn2_norm_g"] = ffn_bwd(x2, dx3, small["ffn2_norm_g"], sil2, gp2, wg2, wu2, wd2, "ffn2_bwd")
    deps = put_g({"wg2": grad_tn(da, h, 1.0, "ffn2_dwg"), "wu2": grad_tn(db, h, 1.0, "ffn2_dwu"),
                  "wd2": grad_tn(s2, dx3, FFN_RES, "ffn2_dwd")})

    dya_t, dyg_t, y_t, dgao, dggo, gs["b_out"] = mix_out_bwd(dx2, ya_t, yg_t, gao, ggo, wout, "mix_out_bwd", deps)
    gs["attn_out_norm_g"], gs["gmlp_out_norm_g"] = dgao.reshape(1, -1), dggo.reshape(1, -1)
    g_wout = grad_nn([y_t], dx2, "dwout")
    dzg_t, dws, dbs, dlng, dlnb = gmlp_bwd(zg_t, dyg_t, lng, lnb, w_s, b_s, "gmlp_bwd")
    gs["gmlp_w_s"], gs["gmlp_b_s"] = dws[None], dbs[None]
    gs["gmlp_ln_g"], gs["gmlp_ln_b"] = dlng.reshape(1, -1), dlnb.reshape(1, -1)
    dq_t, dk_t, dv_t, dsinks = attn_bwd(qkv_t, dya_t, small["attn_sinks"], "attn_bwd")
    gs["attn_sinks"] = dsinks.reshape(1, -1)
    dx1, h2, gs["mix_norm_g"], dbin = mix_in_bwd(x1, dx2, small["mix_norm_g"], dq_t, dk_t, dv_t, dzg_t, win,
                                                 "mix_in_bwd")
    gs["b_in"] = dbin.reshape(1, -1)
    deps = put_g({"wout": g_wout, "win": grad_nn([dq_t, dk_t, dv_t, dzg_t], h2, "dwin")})

    dx0, da, db, h, gs["ffn1_norm_g"] = ffn_bwd(x, dx1, small["ffn1_norm_g"], sil1, gp1, wg1, wu1, wd1, "ffn1_bwd",
                                                deps)
    deps = put_small(gs, loss)
    deps = put_g({"wg1": grad_tn(da, h, 1.0, "ffn1_dwg", deps)})
    deps = put_g({"wu1": grad_tn(db, h, 1.0, "ffn1_dwu", deps)})
    put_g({"wd1": grad_tn(s1, dx1, FFN_RES, "ffn1_dwd", deps)})
    return dx0


SMALL_NAMES = ["ffn1_norm_g", "mix_norm_g", "b_in", "attn_sinks", "gmlp_ln_g", "gmlp_ln_b", "gmlp_w_s", "gmlp_b_s",
               "attn_out_norm_g", "gmlp_out_norm_g", "b_out", "ffn2_norm_g", "final_norm_g"]
BIG_NAMES = {"wg1": ("ffn1_w_gate", True), "wu1": ("ffn1_w_up", True), "wd1": ("ffn1_w_down", False),
             "win": ("w_in", True), "wout": ("w_out", False),
             "wg2": ("ffn2_w_gate", True), "wu2": ("ffn2_w_up", True), "wd2": ("ffn2_w_down", False)}
WEIGHT_ORDER = ["ffn1_norm_g", "ffn1_w_gate", "ffn1_w_up", "ffn1_w_down", "mix_norm_g", "w_in", "b_in", "attn_sinks",
                "gmlp_ln_g", "gmlp_ln_b", "gmlp_w_s", "gmlp_b_s", "attn_out_norm_g", "gmlp_out_norm_g", "w_out",
                "b_out", "ffn2_norm_g", "ffn2_w_gate", "ffn2_w_up", "ffn2_w_down", "final_norm_g"]


def kernel(x, ffn1_norm_g, ffn1_w_gate, ffn1_w_up, ffn1_w_down, mix_norm_g, w_in, b_in, attn_sinks, gmlp_ln_g, gmlp_ln_b, gmlp_w_s, gmlp_b_s, attn_out_norm_g, gmlp_out_norm_g, w_out, b_out, ffn2_norm_g, ffn2_w_gate, ffn2_w_up, ffn2_w_down, final_norm_g, loss_target, m_ffn1_norm_g, m_ffn1_w_gate, m_ffn1_w_up, m_ffn1_w_down, m_mix_norm_g, m_w_in, m_b_in, m_attn_sinks, m_gmlp_ln_g, m_gmlp_ln_b, m_gmlp_w_s, m_gmlp_b_s, m_attn_out_norm_g, m_gmlp_out_norm_g, m_w_out, m_b_out, m_ffn2_norm_g, m_ffn2_w_gate, m_ffn2_w_up, m_ffn2_w_down, m_final_norm_g, v_ffn1_norm_g, v_ffn1_w_gate, v_ffn1_w_up, v_ffn1_w_down, v_mix_norm_g, v_w_in, v_b_in, v_attn_sinks, v_gmlp_ln_g, v_gmlp_ln_b, v_gmlp_w_s, v_gmlp_b_s, v_attn_out_norm_g, v_gmlp_out_norm_g, v_w_out, v_b_out, v_ffn2_norm_g, v_ffn2_w_gate, v_ffn2_w_up, v_ffn2_w_down, v_final_norm_g):
    args = dict(locals())
    w = {n: args[n] for n in WEIGHT_ORDER}
    m = {n: args["m_" + n] for n in WEIGHT_ORDER}
    v = {n: args["v_" + n] for n in WEIGHT_ORDER}

    shards = {}
    for key, (pname, transposed) in BIG_NAMES.items():
        mat = w[pname][0]
        shards[key] = (mat.T if transposed else mat).astype(BF16)
    d_model = x.shape[-1]
    flat = lambda g: g.reshape(-1, d_model)
    me = _mesh_place()[3].astype(jnp.int32).reshape(1)
    first = ("wg1", "wu1", "wd1")
    later = (("win", "wout"), ("wg2", "wu2", "wd2"))
    gathers, exchanges, last_token = {}, [], []

    def get_w(keys, after):
        if keys == first:
            order = list(first) + [k for grp in later for k in grp]
            full = dict(zip(order, allgather_rows([shards[k] for k in order], len(first), "allgather_ffn1")))
            started, token = split_start([([shards[k] for k in grp], [full[k] for k in grp]) for grp in later],
                                         gather_slices, (), "gather_start")
            gathers.update(zip(later, started))
            return [flat(full[k]) for k in keys], (token,)
        _, lands = split_wait(gathers[keys], gather_slices, after, "gather_wait_" + keys[0])
        return [flat(land) for land in lands], ()

    def put_g(gb):
        keys = tuple(gb)
        g3 = [gb[k].reshape(N_DEV, -1, d_model) for k in keys]
        lands = [lax.empty((N_DEV - 1,) + g.shape[1:], BF16) for g in g3]
        started, token = split_start([(g3, lands)], exchange_slices, (), "exchange_start_" + keys[0])
        exchanges.append((keys, started[0]))
        last_token[:] = [token]
        return (token,)

    small_started = []

    def put_small(gs, loss):
        pack = pack_small([gs[n] for n in SMALL_NAMES] + [loss[:, :1]])
        started, token = split_start([([pack], [place_own(pack, me, "place_small")])], gather_slices, (), "small_start")
        small_started[:] = started
        return (token,)

    small = {n: w[n] for n in SMALL_NAMES}
    grad_x = local_step(x[0], loss_target[0], small, get_w, put_g, put_small)

    grads, deltas, new_m, new_v = {}, {}, {}, {}
    after = tuple(last_token)
    for keys, started in exchanges:
        g3, lands = split_wait(started, exchange_slices, after, "exchange_wait_" + keys[0])
        for key, own, land in zip(keys, g3, lands):
            pname, transposed = BIG_NAMES[key]
            g = sum_partials(land, own, me, "sum_" + key)
            after = (g,)
            to_rows = (lambda p: p[0].T) if transposed else (lambda p: p[0])
            from_rows = (lambda r: r.T[None]) if transposed else (lambda r: r[None])
            d_, m_, v_ = adamw(to_rows(w[pname]), g, to_rows(m[pname]), to_rows(v[pname]), "adamw_" + key)
            grads[pname], deltas[pname], new_m[pname], new_v[pname] = (from_rows(g), from_rows(d_), from_rows(m_),
                                                                        from_rows(v_))

    shapes = [w[n].shape for n in SMALL_NAMES]
    _, (slots,) = split_wait(small_started[0], gather_slices, after, "small_wait")
    total = sum_slots(slots, "sum_small")
    small_g = unpack_small(total, shapes + [(1, 1)])
    loss_total = small_g[-1].reshape(())
    gpack = total[:sum(_pack_rows(w[n].size) for n in SMALL_NAMES)]
    d_, m_, v_ = adamw(pack_small([w[n] for n in SMALL_NAMES]), gpack, pack_small([m[n] for n in SMALL_NAMES]),
                       pack_small([v[n] for n in SMALL_NAMES]), "adamw_small")
    for n, g_, dd, mm, vv in zip(SMALL_NAMES, small_g[:-1], unpack_small(d_, shapes), unpack_small(m_, shapes),
                                 unpack_small(v_, shapes)):
        grads[n], deltas[n], new_m[n], new_v[n] = g_, dd, mm, vv

    return (loss_total, grad_x[None], *[grads[n] for n in WEIGHT_ORDER], *[deltas[n] for n in WEIGHT_ORDER],
            *[new_m[n] for n in WEIGHT_ORDER], *[new_v[n] for n in WEIGHT_ORDER])
```

```python
import functools

import jax
import jax.numpy as jnp
from jax import lax
from jax.experimental import pallas as pl
from jax.experimental.pallas import tpu as pltpu

F32 = jnp.float32
BF16 = jnp.bfloat16

N_DEV = 8
N_Q_HEADS = 8
N_KV_HEADS = 2
HEAD_DIM = 64
ATTN_WIDTH = N_Q_HEADS * HEAD_DIM
KV_WIDTH = N_KV_HEADS * HEAD_DIM
QKV_WIDTH = ATTN_WIDTH + 2 * KV_WIDTH
BLOCK = 128
GMLP_GROUPS = 8
GMLP_GROUP_DIM = 64
GMLP_WIDTH = GMLP_GROUPS * GMLP_GROUP_DIM
EPS = 1e-6
FFN_RES = 0.5
ATTN_SCALE = HEAD_DIM ** -0.5
MASK_VALUE = -1e30

ADAM_LR = 0.001
ADAM_B1 = 0.9
ADAM_B2 = 0.999
ADAM_EPS = 1e-08
ADAM_WD = 0.01
ADAM_STEP = 10

V7X_VMEM_BYTES = 64 * 1024 * 1024
VMEM_LIMIT = 62 * 1024 * 1024
LANES = 128
MXU_WIDTH = 256

NT = (((1,), (1,)), ((), ()))
TN = (((0,), (0,)), ((), ()))


def _dot(a, b):
    return jnp.dot(a, b, preferred_element_type=F32)


def _dg(a, b, dims):
    return lax.dot_general(a, b, dims, preferred_element_type=F32)


def _params(semantics=None):
    return pltpu.CompilerParams(dimension_semantics=semantics, vmem_limit_bytes=VMEM_LIMIT)


def _resident(shape):
    zeros = (0,) * len(shape)
    return pl.BlockSpec(shape, lambda *_: zeros, pipeline_mode=pl.Buffered(1))


def _drop_deps(body, n_in, n_deps):
    return lambda *refs: body(*refs[:n_in], *refs[n_in + n_deps:])


ANY_SPEC = pl.BlockSpec(memory_space=pl.ANY)


def _token_tile(t):
    return min(t, 512)


def _f_chunk(f):
    for c in (256, 128):
        if f % c == 0:
            return c
    return f


def ffn_fwd(x, gain, wg_t, wu_t, wd, name, deps=()):
    t, d = x.shape
    f = wd.shape[0]
    tm = _token_tile(t)
    fc = _f_chunk(f)

    def body(x_ref, g_ref, wg_ref, wu_ref, wd_ref, xo_ref, sil_ref, gp_ref, s_ref):
        xv = x_ref[...]
        r = lax.rsqrt(jnp.mean(xv * xv, axis=-1, keepdims=True) + EPS)
        h = (xv * r * g_ref[...]).astype(BF16)
        for c in range(f // fc):
            sl = slice(c * fc, (c + 1) * fc)
            a = _dg(h, wg_ref[sl, :], NT)
            b = _dg(h, wu_ref[sl, :], NT)
            sig = jax.nn.sigmoid(a)
            sil = a * sig
            sil_ref[:, sl] = sil.astype(BF16)
            gp_ref[:, sl] = (b * (sig + sil - sil * sig)).astype(BF16)
            s_ref[:, sl] = (sil * b).astype(BF16)
        xo_ref[...] = xv + FFN_RES * _dot(s_ref[...], wd_ref[...])

    wide = lambda: pl.BlockSpec((tm, f), lambda i: (i, 0))
    return pl.pallas_call(
        _drop_deps(body, 5, len(deps)), name=name, grid=(t // tm,),
        in_specs=[pl.BlockSpec((tm, d), lambda i: (i, 0)), _resident((1, d)),
                  _resident((f, d)), _resident((f, d)), _resident((f, d))] + [ANY_SPEC] * len(deps),
        out_specs=[pl.BlockSpec((tm, d), lambda i: (i, 0)), wide(), wide(), wide()],
        out_shape=[jax.ShapeDtypeStruct((t, d), F32)] + [jax.ShapeDtypeStruct((t, f), BF16)] * 3,
        compiler_params=_params(("arbitrary",)),
    )(x, gain, wg_t, wu_t, wd, *deps)


def _rmsnorm_bwd_rows(xv, r, gain, dh):
    n = xv * r
    dgain = jnp.sum(dh * n, axis=0, keepdims=True)
    dn = dh * gain
    dx = r * (dn - n * jnp.mean(dn * n, axis=-1, keepdims=True))
    return dx, dgain


def ffn_bwd(x, dxo, gain, sil, gp, wg_t, wu_t, wd, name, deps=()):
    t, d = x.shape
    f = wd.shape[0]
    tm = _token_tile(t)
    nh = 2 if f % (2 * LANES) == 0 else 1
    fh = f // nh
    chunks = [(off, min(MXU_WIDTH, fh - off)) for off in range(0, fh, MXU_WIDTH)]

    def body(x_ref, dxo_ref, g_ref, sil_ref, gp_ref, wg_ref, wu_ref, wd_ref,
             dx_ref, da_ref, db_ref, h_ref, dg_ref, dh_ref):
        i, j = pl.program_id(0), pl.program_id(1)
        gain_v = g_ref[...]

        @pl.when(j == 0)
        def _():
            xv = x_ref[...]
            r = lax.rsqrt(jnp.mean(xv * xv, axis=-1, keepdims=True) + EPS)
            h_ref[...] = (xv * r * gain_v).astype(BF16)
            dh_ref[...] = jnp.zeros_like(dh_ref)

        df = (FFN_RES * dxo_ref[...]).astype(BF16)
        for off, size in chunks:
            sl = slice(off, off + size)
            rows = pl.ds(pl.multiple_of(j * fh + off, LANES), size)
            ds = _dg(df, wd_ref[rows, :], NT)
            da_ref[:, sl] = (ds * gp_ref[:, sl].astype(F32)).astype(BF16)
            db_ref[:, sl] = (ds * sil_ref[:, sl].astype(F32)).astype(BF16)
        half_rows = pl.ds(pl.multiple_of(j * fh, LANES), fh)
        dh_ref[...] += _dot(da_ref[...], wg_ref[half_rows, :]) + _dot(db_ref[...], wu_ref[half_rows, :])

        @pl.when(j == nh - 1)
        def _():
            xv = x_ref[...]
            r = lax.rsqrt(jnp.mean(xv * xv, axis=-1, keepdims=True) + EPS)
            dx, dgain = _rmsnorm_bwd_rows(xv, r, gain_v, dh_ref[...])
            dx_ref[...] = dxo_ref[...] + dx

            @pl.when(i == 0)
            def _():
                dg_ref[...] = jnp.zeros_like(dg_ref)

            dg_ref[...] += dgain

    tok = lambda: pl.BlockSpec((tm, d), lambda i, j: (i, 0))
    half = lambda: pl.BlockSpec((tm, fh), lambda i, j: (i, j))
    return pl.pallas_call(
        _drop_deps(body, 8, len(deps)), name=name, grid=(t // tm, nh),
        in_specs=[tok(), tok(), _resident((1, d)), half(), half(),
                  _resident((f, d)), _resident((f, d)), _resident((f, d))] + [ANY_SPEC] * len(deps),
        out_specs=[tok(), half(), half(), tok(), pl.BlockSpec((1, d), lambda i, j: (0, 0))],
        out_shape=[jax.ShapeDtypeStruct((t, d), F32), jax.ShapeDtypeStruct((t, f), BF16),
                   jax.ShapeDtypeStruct((t, f), BF16), jax.ShapeDtypeStruct((t, d), BF16),
                   jax.ShapeDtypeStruct((1, d), F32)],
        scratch_shapes=[pltpu.VMEM((tm, d), F32)],
        compiler_params=_params(("arbitrary", "arbitrary")),
    )(x, dxo, gain, sil, gp, wg_t, wu_t, wd, *deps)


def _row_tile(m, cap):
    if m <= cap:
        return m
    best = None
    for bm in range(LANES, cap + 1, LANES):
        if m % bm == 0:
            best = bm
    return best if best is not None else m


def grad_tn(a, b, scale, name, deps=()):
    t, m = a.shape
    n = b.shape[1]
    bm = _row_tile(m, 1408)
    bk = min(t, 1024)
    nk = t // bk

    def body(a_ref, b_ref, o_ref, acc_ref):
        k = pl.program_id(1)

        @pl.when(k == 0)
        def _():
            acc_ref[...] = jnp.zeros_like(acc_ref)

        acc_ref[...] += _dg(a_ref[...].astype(BF16), b_ref[...].astype(BF16), TN)

        @pl.when(k == nk - 1)
        def _():
            o_ref[...] = (scale * acc_ref[...]).astype(BF16)

    return pl.pallas_call(
        _drop_deps(body, 2, len(deps)), name=name, grid=(m // bm, nk),
        in_specs=[pl.BlockSpec((bk, bm), lambda i, k: (k, i)), pl.BlockSpec((bk, n), lambda i, k: (k, 0))]
        + [ANY_SPEC] * len(deps),
        out_specs=pl.BlockSpec((bm, n), lambda i, k: (i, 0)),
        out_shape=jax.ShapeDtypeStruct((m, n), BF16),
        scratch_shapes=[pltpu.VMEM((bm, n), F32)],
        compiler_params=_params(("arbitrary", "arbitrary")),
    )(a, b, *deps)


def grad_nn(a_list, b, name):
    t, n = b.shape
    ms = [a.shape[0] for a in a_list]
    m = sum(ms)
    bk = min(t, 1024)
    nk = t // bk
    na = len(a_list)

    def body(*refs):
        a_refs, b_ref, o_ref, acc_ref = refs[:na], refs[na], refs[na + 1], refs[na + 2]
        k = pl.program_id(0)

        @pl.when(k == 0)
        def _():
            acc_ref[...] = jnp.zeros_like(acc_ref)

        bv = b_ref[...].astype(BF16)
        off = 0
        for a_ref, mi in zip(a_refs, ms):
            acc_ref[off:off + mi, :] += _dot(a_ref[...].astype(BF16), bv)
            off += mi

        @pl.when(k == nk - 1)
        def _():
            o_ref[...] = acc_ref[...].astype(BF16)

    return pl.pallas_call(
        body, name=name, grid=(nk,),
        in_specs=[pl.BlockSpec((mi, bk), lambda k: (0, k)) for mi in ms] + [pl.BlockSpec((bk, n), lambda k: (k, 0))],
        out_specs=pl.BlockSpec((m, n), lambda k: (0, 0)),
        out_shape=jax.ShapeDtypeStruct((m, n), BF16),
        scratch_shapes=[pltpu.VMEM((m, n), F32)],
        compiler_params=_params(("arbitrary",)),
    )(*a_list, b)


def mix_in_fwd(x, gain, win_t, b_in_col, name):
    t, d = x.shape
    w = win_t.shape[0]
    tm = _token_tile(t)

    def body(x_ref, g_ref, w_ref, bias_ref, qkv_ref, zg_ref):
        xv = x_ref[...]
        r = lax.rsqrt(jnp.mean(xv * xv, axis=-1, keepdims=True) + EPS)
        h = (xv * r * g_ref[...]).astype(BF16)
        qkv_ref[...] = (_dg(w_ref[:QKV_WIDTH, :], h, NT) + bias_ref[:QKV_WIDTH, :]).astype(BF16)
        zg_ref[...] = (_dg(w_ref[QKV_WIDTH:, :], h, NT) + bias_ref[QKV_WIDTH:, :]).astype(BF16)

    return pl.pallas_call(
        body, name=name, grid=(t // tm,),
        in_specs=[pl.BlockSpec((tm, d), lambda i: (i, 0)), _resident((1, d)), _resident((w, d)), _resident((w, 1))],
        out_specs=[pl.BlockSpec((QKV_WIDTH, tm), lambda i: (0, i)), pl.BlockSpec((w - QKV_WIDTH, tm), lambda i: (0, i))],
        out_shape=[jax.ShapeDtypeStruct((QKV_WIDTH, t), BF16), jax.ShapeDtypeStruct((w - QKV_WIDTH, t), BF16)],
        compiler_params=_params(("arbitrary",)),
    )(x, gain, win_t, b_in_col)


def _attn_specs(nb):
    last = nb - 1
    cur = lambda n: jnp.minimum(n, last)
    prev = lambda n: jnp.maximum(jnp.minimum(n, last) - 1, 0)
    k_row = ATTN_WIDTH // KV_WIDTH
    return [
        pl.BlockSpec((ATTN_WIDTH, BLOCK), lambda n: (0, cur(n))),
        pl.BlockSpec((KV_WIDTH, BLOCK), lambda n: (k_row, prev(n))),
        pl.BlockSpec((KV_WIDTH, BLOCK), lambda n: (k_row, cur(n))),
        pl.BlockSpec((KV_WIDTH, BLOCK), lambda n: (k_row + 1, prev(n))),
        pl.BlockSpec((KV_WIDTH, BLOCK), lambda n: (k_row + 1, cur(n))),
    ]


ATTN_REP = N_Q_HEADS // N_KV_HEADS


def _group_rows(ref, g):
    first = g * ATTN_REP
    return jnp.concatenate([ref[(first + r) * HEAD_DIM:(first + r + 1) * HEAD_DIM, :] for r in range(ATTN_REP)],
                           axis=1)


def _ungroup_rows(ref, g, val):
    first = g * ATTN_REP
    for r in range(ATTN_REP):
        ref[(first + r) * HEAD_DIM:(first + r + 1) * HEAD_DIM, :] = val[:, r * BLOCK:(r + 1) * BLOCK]


def _attn_upper():
    key = lax.broadcasted_iota(jnp.int32, (BLOCK, ATTN_REP * BLOCK), 0)
    qry = lax.broadcasted_iota(jnp.int32, (BLOCK, ATTN_REP * BLOCK), 1) & (BLOCK - 1)
    return key > qry


def _attn_probs(q, kp, kc, sink_ref, g, upper, n):
    s = jnp.where(upper, _dg(kp, q, TN), _dg(kc, q, TN)) * ATTN_SCALE
    s = jnp.where(upper & (n == 0), MASK_VALUE, s)
    sink = jnp.concatenate([jnp.full((1, BLOCK), sink_ref[0, g * ATTN_REP + r], F32) for r in range(ATTN_REP)], axis=1)
    m = jnp.maximum(jnp.max(s, axis=0, keepdims=True), sink)
    e = jnp.exp(s - m)
    es = jnp.exp(sink - m)
    inv = 1.0 / (jnp.sum(e, axis=0, keepdims=True) + es)
    return e * inv, es * inv


def _split_band(upper, val):
    return jnp.where(upper, val, 0.0).astype(BF16), jnp.where(upper, 0.0, val).astype(BF16)


def attn_fwd(qkv_t, sinks, name):
    t = qkv_t.shape[1]
    nb = t // BLOCK

    def body(sink_ref, q_ref, kp_ref, kc_ref, vp_ref, vc_ref, y_ref):
        n = pl.program_id(0)
        upper = _attn_upper()
        for g in range(N_KV_HEADS):
            gs = slice(g * HEAD_DIM, (g + 1) * HEAD_DIM)
            p, _ = _attn_probs(_group_rows(q_ref, g), kp_ref[gs, :], kc_ref[gs, :], sink_ref, g, upper, n)
            pp, pc = _split_band(upper, p)
            _ungroup_rows(y_ref, g, (_dot(vp_ref[gs, :], pp) + _dot(vc_ref[gs, :], pc)).astype(BF16))

    return pl.pallas_call(
        body, name=name, grid=(nb,),
        in_specs=[pl.BlockSpec(memory_space=pltpu.SMEM)] + _attn_specs(nb),
        out_specs=pl.BlockSpec((ATTN_WIDTH, BLOCK), lambda n: (0, n)),
        out_shape=jax.ShapeDtypeStruct((ATTN_WIDTH, t), BF16),
        compiler_params=_params(("arbitrary",)),
    )(sinks, qkv_t, qkv_t, qkv_t, qkv_t, qkv_t)


def attn_bwd(qkv_t, dy_t, sinks, name):
    t = qkv_t.shape[1]
    nb = t // BLOCK

    def body(sink_ref, q_ref, kp_ref, kc_ref, vp_ref, vc_ref, do_ref, dq_ref, dk_ref, dv_ref, dsink_ref,
             ck_ref, cv_ref, sacc_ref):
        n = pl.program_id(0)

        @pl.when(n == 0)
        def _():
            sacc_ref[...] = jnp.zeros_like(sacc_ref)

        @pl.when(n < nb)
        def _():
            upper = _attn_upper()
            for g in range(N_KV_HEADS):
                gs = slice(g * HEAD_DIM, (g + 1) * HEAD_DIM)
                kp, kc, vp, vc = kp_ref[gs, :], kc_ref[gs, :], vp_ref[gs, :], vc_ref[gs, :]
                q = _group_rows(q_ref, g)
                do = _group_rows(do_ref, g)
                p, ps = _attn_probs(q, kp, kc, sink_ref, g, upper, n)
                dp = jnp.where(upper, _dg(vp, do, TN), _dg(vc, do, TN))
                delta = jnp.sum(p * dp, axis=0, keepdims=True)
                dsink = -(ps * delta)
                for r in range(ATTN_REP):
                    hd = g * ATTN_REP + r
                    sacc_ref[hd:hd + 1, :] += dsink[:, r * BLOCK:(r + 1) * BLOCK]
                dsp, dsc = _split_band(upper, p * (dp - delta))
                pp, pc = _split_band(upper, p)
                _ungroup_rows(dq_ref, g, (ATTN_SCALE * (_dot(kp, dsp) + _dot(kc, dsc))).astype(BF16))

                @pl.when(n > 0)
                def _():
                    dk_ref[gs, :] = (ck_ref[gs, :] + ATTN_SCALE * _dg(q, dsp, NT)).astype(BF16)
                    dv_ref[gs, :] = (cv_ref[gs, :] + _dg(do, pp, NT)).astype(BF16)

                ck_ref[gs, :] = ATTN_SCALE * _dg(q, dsc, NT)
                cv_ref[gs, :] = _dg(do, pc, NT)

        @pl.when(n == nb)
        def _():
            dk_ref[...] = ck_ref[...].astype(BF16)
            dv_ref[...] = cv_ref[...].astype(BF16)
            dsink_ref[...] = jnp.sum(sacc_ref[...], axis=1, keepdims=True)

    last = nb - 1
    done = lambda n: jnp.maximum(n - 1, 0)
    outs = pl.pallas_call(
        body, name=name, grid=(nb + 1,),
        in_specs=[pl.BlockSpec(memory_space=pltpu.SMEM)] + _attn_specs(nb)
        + [pl.BlockSpec((ATTN_WIDTH, BLOCK), lambda n: (0, jnp.minimum(n, last)))],
        out_specs=[pl.BlockSpec((ATTN_WIDTH, BLOCK), lambda n: (0, jnp.minimum(n, last))),
                   pl.BlockSpec((KV_WIDTH, BLOCK), lambda n: (0, done(n))),
                   pl.BlockSpec((KV_WIDTH, BLOCK), lambda n: (0, done(n))),
                   pl.BlockSpec((N_Q_HEADS, 1), lambda n: (0, 0))],
        out_shape=[jax.ShapeDtypeStruct((ATTN_WIDTH, t), BF16), jax.ShapeDtypeStruct((KV_WIDTH, t), BF16),
                   jax.ShapeDtypeStruct((KV_WIDTH, t), BF16), jax.ShapeDtypeStruct((N_Q_HEADS, 1), F32)],
        scratch_shapes=[pltpu.VMEM((KV_WIDTH, BLOCK), F32), pltpu.VMEM((KV_WIDTH, BLOCK), F32),
                        pltpu.VMEM((N_Q_HEADS, BLOCK), F32)],
        compiler_params=_params(("arbitrary",)),
    )(sinks, qkv_t, qkv_t, qkv_t, qkv_t, qkv_t, dy_t)
    return outs


GELU_C = 0.7978845608028654
GELU_A = 0.044715


def _gelu(x):
    return 0.5 * x * (1.0 + jnp.tanh(GELU_C * (x + GELU_A * x * x * x)))


def _gelu_grad(x):
    th = jnp.tanh(GELU_C * (x + GELU_A * x * x * x))
    return 0.5 * (1.0 + th) + 0.5 * x * (1.0 - th * th) * GELU_C * (1.0 + 3.0 * GELU_A * x * x)


def _gmlp_parts(zg, lng, lnb):
    z = _gelu(zg)
    u = z[:GMLP_WIDTH, :]
    vv = z[GMLP_WIDTH:, :]
    mu = jnp.mean(vv, axis=0, keepdims=True)
    xc = vv - mu
    rstd = lax.rsqrt(jnp.mean(xc * xc, axis=0, keepdims=True) + EPS)
    xhat = xc * rstd
    return u, xhat, rstd, xhat * lng + lnb


def _causal(w):
    row = lax.broadcasted_iota(jnp.int32, (BLOCK, BLOCK), 0)
    col = lax.broadcasted_iota(jnp.int32, (BLOCK, BLOCK), 1)
    return jnp.where(col <= row, w, 0.0)


GMLP_CHUNKS_PER_STEP = 4


def _stack_chunks(v, nc):
    return jnp.concatenate([v[:, c * BLOCK:(c + 1) * BLOCK] for c in range(nc)], axis=0)


def _unstack_chunks(v, nc):
    rows = v.shape[0] // nc
    return jnp.concatenate([v[c * rows:(c + 1) * rows, :] for c in range(nc)], axis=1)


def gmlp_fwd(zg_t, lng, lnb, w_s, b_s, name):
    t = zg_t.shape[1]
    nc = min(GMLP_CHUNKS_PER_STEP, t // BLOCK)
    tw = nc * BLOCK

    def body(zg_ref, lng_ref, lnb_ref, w_ref, bs_ref, y_ref):
        u, _, _, vn = _gmlp_parts(zg_ref[...].astype(F32), lng_ref[...], lnb_ref[...])
        for g in range(GMLP_GROUPS):
            gs = slice(g * GMLP_GROUP_DIM, (g + 1) * GMLP_GROUP_DIM)
            wc = _causal(w_ref[g]).astype(BF16)
            mixed = _dg(_stack_chunks(vn[gs, :].astype(BF16), nc), wc, NT) + bs_ref[g:g + 1, :]
            y_ref[gs, :] = (u[gs, :] * _unstack_chunks(mixed, nc)).astype(BF16)

    return pl.pallas_call(
        body, name=name, grid=(t // tw,),
        in_specs=[pl.BlockSpec((2 * GMLP_WIDTH, tw), lambda n: (0, n)), _resident((GMLP_WIDTH, 1)),
                  _resident((GMLP_WIDTH, 1)), _resident((GMLP_GROUPS, BLOCK, BLOCK)), _resident((GMLP_GROUPS, BLOCK))],
        out_specs=pl.BlockSpec((GMLP_WIDTH, tw), lambda n: (0, n)),
        out_shape=jax.ShapeDtypeStruct((GMLP_WIDTH, t), BF16),
        compiler_params=_params(("arbitrary",)),
    )(zg_t, lng, lnb, w_s, b_s)


def gmlp_bwd(zg_t, dy_t, lng, lnb, w_s, b_s, name):
    t = zg_t.shape[1]
    nc = min(GMLP_CHUNKS_PER_STEP, t // BLOCK)
    tw = nc * BLOCK
    nt = t // tw

    def body(zg_ref, dy_ref, lng_ref, lnb_ref, w_ref, bs_ref, dzg_ref, dw_ref, dbs_ref, dlng_ref, dlnb_ref,
             gacc_ref, bacc_ref):
        n = pl.program_id(0)

        @pl.when(n == 0)
        def _():
            dw_ref[...] = jnp.zeros_like(dw_ref)
            dbs_ref[...] = jnp.zeros_like(dbs_ref)
            gacc_ref[...] = jnp.zeros_like(gacc_ref)
            bacc_ref[...] = jnp.zeros_like(bacc_ref)

        zg = zg_ref[...].astype(F32)
        lng_v = lng_ref[...]
        u, xhat, rstd, vn = _gmlp_parts(zg, lng_v, lnb_ref[...])
        dy = dy_ref[...].astype(F32)
        dvn_parts = []
        for g in range(GMLP_GROUPS):
            gs = slice(g * GMLP_GROUP_DIM, (g + 1) * GMLP_GROUP_DIM)
            wc = _causal(w_ref[g]).astype(BF16)
            vn_s = _stack_chunks(vn[gs, :].astype(BF16), nc)
            mixed = _unstack_chunks(_dg(vn_s, wc, NT) + bs_ref[g:g + 1, :], nc)
            dy_g = dy[gs, :]
            dmixed = dy_g * u[gs, :]
            dm_s = _stack_chunks(dmixed.astype(BF16), nc)
            dzg_ref[gs, :] = (dy_g * mixed * _gelu_grad(zg[gs, :])).astype(BF16)
            dw_ref[g] += _causal(_dg(dm_s, vn_s, TN))
            dbs_ref[g:g + 1, :] += _lane_fold(jnp.sum(dmixed, axis=0, keepdims=True))
            dvn_parts.append(_unstack_chunks(_dot(dm_s, wc), nc))
        dvn = jnp.concatenate(dvn_parts, axis=0)
        gacc_ref[...] += _lane_fold(dvn * xhat)
        bacc_ref[...] += _lane_fold(dvn)
        dxhat = dvn * lng_v
        m1 = jnp.mean(dxhat, axis=0, keepdims=True)
        m2 = jnp.mean(dxhat * xhat, axis=0, keepdims=True)
        dvv = rstd * (dxhat - m1 - xhat * m2)
        dzg_ref[GMLP_WIDTH:, :] = (dvv * _gelu_grad(zg[GMLP_WIDTH:, :])).astype(BF16)

        @pl.when(n == nt - 1)
        def _():
            dlng_ref[...] = jnp.sum(gacc_ref[...], axis=1, keepdims=True)
            dlnb_ref[...] = jnp.sum(bacc_ref[...], axis=1, keepdims=True)

    const2 = lambda n: (0, 0)
    return pl.pallas_call(
        body, name=name, grid=(nt,),
        in_specs=[pl.BlockSpec((2 * GMLP_WIDTH, tw), lambda n: (0, n)), pl.BlockSpec((GMLP_WIDTH, tw), lambda n: (0, n)),
                  _resident((GMLP_WIDTH, 1)), _resident((GMLP_WIDTH, 1)),
                  _resident((GMLP_GROUPS, BLOCK, BLOCK)), _resident((GMLP_GROUPS, BLOCK))],
        out_specs=[pl.BlockSpec((2 * GMLP_WIDTH, tw), lambda n: (0, n)),
                   pl.BlockSpec((GMLP_GROUPS, BLOCK, BLOCK), lambda n: (0, 0, 0)),
                   pl.BlockSpec((GMLP_GROUPS, BLOCK), const2),
                   pl.BlockSpec((GMLP_WIDTH, 1), const2), pl.BlockSpec((GMLP_WIDTH, 1), const2)],
        out_shape=[jax.ShapeDtypeStruct((2 * GMLP_WIDTH, t), BF16),
                   jax.ShapeDtypeStruct((GMLP_GROUPS, BLOCK, BLOCK), F32),
                   jax.ShapeDtypeStruct((GMLP_GROUPS, BLOCK), F32),
                   jax.ShapeDtypeStruct((GMLP_WIDTH, 1), F32), jax.ShapeDtypeStruct((GMLP_WIDTH, 1), F32)],
        scratch_shapes=[pltpu.VMEM((GMLP_WIDTH, BLOCK), F32), pltpu.VMEM((GMLP_WIDTH, BLOCK), F32)],
        compiler_params=_params(("arbitrary",)),
    )(zg_t, dy_t, lng, lnb, w_s, b_s)


def _rmsnorm_cols(y, gain_col):
    r = lax.rsqrt(jnp.mean(y * y, axis=0, keepdims=True) + EPS)
    n = y * r
    return n, r, n * gain_col


def mix_out_fwd(x, ya_t, yg_t, gao, ggo, wout, b_out, name):
    t, d = x.shape
    tm = _token_tile(t)

    def body(x_ref, ya_ref, yg_ref, gao_ref, ggo_ref, w_ref, b_ref, o_ref):
        _, _, ya = _rmsnorm_cols(ya_ref[...].astype(F32), gao_ref[...])
        _, _, yg = _rmsnorm_cols(yg_ref[...].astype(F32), ggo_ref[...])
        o_ref[...] = (x_ref[...] + _dg(ya.astype(BF16), w_ref[:ATTN_WIDTH, :], TN)
                      + _dg(yg.astype(BF16), w_ref[ATTN_WIDTH:, :], TN) + b_ref[...])

    return pl.pallas_call(
        body, name=name, grid=(t // tm,),
        in_specs=[pl.BlockSpec((tm, d), lambda i: (i, 0)), pl.BlockSpec((ATTN_WIDTH, tm), lambda i: (0, i)),
                  pl.BlockSpec((GMLP_WIDTH, tm), lambda i: (0, i)), _resident((ATTN_WIDTH, 1)), _resident((GMLP_WIDTH, 1)),
                  _resident((ATTN_WIDTH + GMLP_WIDTH, d)), _resident((1, d))],
        out_specs=pl.BlockSpec((tm, d), lambda i: (i, 0)),
        out_shape=jax.ShapeDtypeStruct((t, d), F32),
        compiler_params=_params(("arbitrary",)),
    )(x, ya_t, yg_t, gao, ggo, wout, b_out)


def _lane_fold(v):
    out = v[:, :LANES]
    for j in range(1, v.shape[1] // LANES):
        out = out + v[:, j * LANES:(j + 1) * LANES]
    return out


def mix_out_bwd(dx, ya_t, yg_t, gao, ggo, wout, name, deps=()):
    t, d = dx.shape
    tm = _token_tile(t)
    nt = t // tm

    def body(dx_ref, ya_ref, yg_ref, gao_ref, ggo_ref, w_ref, dya_ref, dyg_ref, y_ref, dgao_ref, dggo_ref, db_ref,
             acc_ref):
        i = pl.program_id(0)

        @pl.when(i == 0)
        def _():
            acc_ref[...] = jnp.zeros_like(acc_ref)
            db_ref[...] = jnp.zeros_like(db_ref)

        dxv = dx_ref[...]
        db_ref[...] += jnp.sum(dxv, axis=0, keepdims=True)
        dxb = dxv.astype(BF16)
        for part, (src_ref, gain_ref, dst_ref) in enumerate(((ya_ref, gao_ref, dya_ref), (yg_ref, ggo_ref, dyg_ref))):
            rows = slice(part * ATTN_WIDTH, (part + 1) * ATTN_WIDTH)
            gain = gain_ref[...]
            nrm, r, yn = _rmsnorm_cols(src_ref[...].astype(F32), gain)
            y_ref[rows, :] = yn.astype(BF16)
            dy = _dg(w_ref[rows, :], dxb, NT)
            acc_ref[rows, :] += _lane_fold(dy * nrm)
            dn = dy * gain
            dst_ref[...] = (r * (dn - nrm * jnp.mean(dn * nrm, axis=0, keepdims=True))).astype(BF16)

        @pl.when(i == nt - 1)
        def _():
            dgao_ref[...] = jnp.sum(acc_ref[:ATTN_WIDTH, :], axis=1, keepdims=True)
            dggo_ref[...] = jnp.sum(acc_ref[ATTN_WIDTH:, :], axis=1, keepdims=True)

    const2 = lambda i: (0, 0)
    feat = lambda w: pl.BlockSpec((w, tm), lambda i: (0, i))
    return pl.pallas_call(
        _drop_deps(body, 6, len(deps)), name=name, grid=(nt,),
        in_specs=[pl.BlockSpec((tm, d), lambda i: (i, 0)), feat(ATTN_WIDTH), feat(GMLP_WIDTH),
                  _resident((ATTN_WIDTH, 1)), _resident((GMLP_WIDTH, 1)), _resident((ATTN_WIDTH + GMLP_WIDTH, d))]
        + [ANY_SPEC] * len(deps),
        out_specs=[feat(ATTN_WIDTH), feat(GMLP_WIDTH), feat(ATTN_WIDTH + GMLP_WIDTH),
                   pl.BlockSpec((ATTN_WIDTH, 1), const2), pl.BlockSpec((GMLP_WIDTH, 1), const2),
                   pl.BlockSpec((1, d), const2)],
        out_shape=[jax.ShapeDtypeStruct((ATTN_WIDTH, t), BF16), jax.ShapeDtypeStruct((GMLP_WIDTH, t), BF16),
                   jax.ShapeDtypeStruct((ATTN_WIDTH + GMLP_WIDTH, t), BF16),
                   jax.ShapeDtypeStruct((ATTN_WIDTH, 1), F32), jax.ShapeDtypeStruct((GMLP_WIDTH, 1), F32),
                   jax.ShapeDtypeStruct((1, d), F32)],
        scratch_shapes=[pltpu.VMEM((ATTN_WIDTH + GMLP_WIDTH, LANES), F32)],
        compiler_params=_params(("arbitrary",)),
    )(dx, ya_t, yg_t, gao, ggo, wout, *deps)


def mix_in_bwd(x, dxo, gain, dq_t, dk_t, dv_t, dzg_t, win_t, name):
    t, d = x.shape
    w = win_t.shape[0]
    tm = _token_tile(t)
    nt = t // tm
    parts = ((0, ATTN_WIDTH), (ATTN_WIDTH, KV_WIDTH), (ATTN_WIDTH + KV_WIDTH, KV_WIDTH), (QKV_WIDTH, w - QKV_WIDTH))

    def body(x_ref, dxo_ref, g_ref, dq_ref, dk_ref, dv_ref, dzg_ref, w_ref, dx_ref, h_ref, dg_ref, db_ref, acc_ref):
        i = pl.program_id(0)

        @pl.when(i == 0)
        def _():
            acc_ref[...] = jnp.zeros_like(acc_ref)
            dg_ref[...] = jnp.zeros_like(dg_ref)

        xv = x_ref[...]
        gain_v = g_ref[...]
        r = lax.rsqrt(jnp.mean(xv * xv, axis=-1, keepdims=True) + EPS)
        h_ref[...] = (xv * r * gain_v).astype(BF16)
        dh = jnp.zeros((tm, d), F32)
        for (off, size), src_ref in zip(parts, (dq_ref, dk_ref, dv_ref, dzg_ref)):
            dp = src_ref[...]
            acc_ref[off:off + size, :] += _lane_fold(dp.astype(F32))
            dh = dh + _dg(dp, w_ref[off:off + size, :], TN)
        dx, dgain = _rmsnorm_bwd_rows(xv, r, gain_v, dh)
        dx_ref[...] = dxo_ref[...] + dx
        dg_ref[...] += dgain

        @pl.when(i == nt - 1)
        def _():
            db_ref[...] = jnp.sum(acc_ref[...], axis=1, keepdims=True)

    const2 = lambda i: (0, 0)
    tok = lambda: pl.BlockSpec((tm, d), lambda i: (i, 0))
    feat = lambda rows: pl.BlockSpec((rows, tm), lambda i: (0, i))
    return pl.pallas_call(
        body, name=name, grid=(nt,),
        in_specs=[tok(), tok(), _resident((1, d)), feat(ATTN_WIDTH), feat(KV_WIDTH), feat(KV_WIDTH),
                  feat(w - QKV_WIDTH), _resident((w, d))],
        out_specs=[tok(), tok(), pl.BlockSpec((1, d), const2), pl.BlockSpec((w, 1), const2)],
        out_shape=[jax.ShapeDtypeStruct((t, d), F32), jax.ShapeDtypeStruct((t, d), BF16),
                   jax.ShapeDtypeStruct((1, d), F32), jax.ShapeDtypeStruct((w, 1), F32)],
        scratch_shapes=[pltpu.VMEM((w, LANES), F32)],
        compiler_params=_params(("arbitrary",)),
    )(x, dxo, gain, dq_t, dk_t, dv_t, dzg_t, win_t)


def final_bwd(x, target, gain, name):
    t, d = x.shape
    tm = _token_tile(t)

    def body(x_ref, t_ref, g_ref, dx_ref, dg_ref, loss_ref):
        i = pl.program_id(0)

        @pl.when(i == 0)
        def _():
            dg_ref[...] = jnp.zeros_like(dg_ref)
            loss_ref[...] = jnp.zeros_like(loss_ref)

        xv = x_ref[...]
        gain_v = g_ref[...]
        r = lax.rsqrt(jnp.mean(xv * xv, axis=-1, keepdims=True) + EPS)
        err = xv * r * gain_v - t_ref[...]
        row_loss = jnp.mean(err * err, axis=-1, keepdims=True)
        loss_ref[...] += 0.5 * jnp.sum(row_loss, axis=0, keepdims=True)
        dx, dgain = _rmsnorm_bwd_rows(xv, r, gain_v, err * (1.0 / d))
        dx_ref[...] = dx
        dg_ref[...] += dgain

    const2 = lambda i: (0, 0)
    return pl.pallas_call(
        body, name=name, grid=(t // tm,),
        in_specs=[pl.BlockSpec((tm, d), lambda i: (i, 0)), pl.BlockSpec((tm, d), lambda i: (i, 0)), _resident((1, d))],
        out_specs=[pl.BlockSpec((tm, d), lambda i: (i, 0)), pl.BlockSpec((1, d), const2), pl.BlockSpec((1, LANES), const2)],
        out_shape=[jax.ShapeDtypeStruct((t, d), F32), jax.ShapeDtypeStruct((1, d), F32),
                   jax.ShapeDtypeStruct((1, LANES), F32)],
        compiler_params=_params(("arbitrary",)),
    )(x, target, gain)


def _adamw_math(w, g, m, v):
    m = ADAM_B1 * m + (1.0 - ADAM_B1) * g
    v = ADAM_B2 * v + (1.0 - ADAM_B2) * (g * g)
    m_hat = m / (1.0 - ADAM_B1 ** ADAM_STEP)
    v_hat = v / (1.0 - ADAM_B2 ** ADAM_STEP)
    delta = -ADAM_LR * (m_hat / (jnp.sqrt(v_hat) + ADAM_EPS) + ADAM_WD * w)
    return delta, m, v


def adamw(w, g, m, v, name):
    rows, cols = w.shape
    tr = rows
    if rows * cols > 256 * 1024:
        for cand in (256, 128, 64, 32, 16, 8):
            if rows % cand == 0:
                tr = cand
                break

    def body(w_ref, g_ref, m_ref, v_ref, d_ref, mo_ref, vo_ref):
        delta, mn, vn = _adamw_math(w_ref[...], g_ref[...], m_ref[...], v_ref[...])
        d_ref[...] = delta
        mo_ref[...] = mn
        vo_ref[...] = vn

    spec = pl.BlockSpec((tr, cols), lambda i: (i, 0))
    shape = jax.ShapeDtypeStruct((rows, cols), F32)
    return pl.pallas_call(
        body, name=name, grid=(rows // tr,),
        in_specs=[spec] * 4, out_specs=[spec] * 3, out_shape=[shape] * 3,
        compiler_params=_params(("arbitrary",)),
    )(w, g, m, v)


def sum_partials(landing, grad3d, me, name, deps=()):
    nd, rows, cols = landing.shape
    tr = rows
    for cand in (128, 112, 88, 64, 32, 16):
        if rows % cand == 0:
            tr = cand
            break

    def body(me_ref, l_ref, own_ref, o_ref):
        acc = own_ref[...].astype(F32)
        for j in range(nd):
            acc = acc + l_ref[j].astype(F32)
        o_ref[...] = acc

    return pl.pallas_call(
        _drop_deps(body, 3, len(deps)), name=name,
        grid_spec=pltpu.PrefetchScalarGridSpec(
            num_scalar_prefetch=1, grid=(rows // tr,),
            in_specs=[pl.BlockSpec((nd, tr, cols), lambda i, me_ref: (0, i, 0)),
                      pl.BlockSpec((None, tr, cols), lambda i, me_ref: (me_ref[0], i, 0))] + [ANY_SPEC] * len(deps),
            out_specs=pl.BlockSpec((tr, cols), lambda i, me_ref: (i, 0))),
        out_shape=jax.ShapeDtypeStruct((rows, cols), F32),
        compiler_params=_params(("arbitrary",)),
    )(me, landing, grad3d, *deps)


def _mesh_place():
    x, y, c = lax.axis_index("x"), lax.axis_index("y"), lax.axis_index("c")
    return x, y, c, 4 * x + 2 * y + c


def _peer(x, y, c, k):
    return (x ^ ((k >> 2) & 1), y ^ ((k >> 1) & 1), c ^ (k & 1))


def allgather_rows(shards, n_gather, name):
    na = len(shards)

    def body(*refs):
        in_refs, out_refs = refs[:na], refs[na:2 * na]
        send_sems, recv_sems, local_sems = refs[2 * na:]
        x, y, c, me = _mesh_place()
        local = [pltpu.make_async_copy(in_refs[a], out_refs[a].at[me], local_sems.at[a]) for a in range(na)]
        for cp in local:
            cp.start()
        def copy(sem, a, origin, src, k):
            return pltpu.make_async_remote_copy(
                src_ref=src, dst_ref=out_refs[a].at[origin], send_sem=send_sems.at[sem, a],
                recv_sem=recv_sems.at[sem, a], device_id=_peer(x, y, c, k), device_id_type=pl.DeviceIdType.MESH)

        kx, ky = 4, 2

        def recv(sem, a, origin):
            copy(sem, a, origin, in_refs[a], 1).wait_recv()

        def landed(a, origin):
            return out_refs[a].at[origin]

        @pl.when(c == 1)
        def _():
            sends = []

            def send(sem, a, origin, src, k):
                cp = copy(sem, a, origin, src, k)
                cp.start()
                sends.append(cp)

            for a in range(n_gather):
                send(1, a, me, in_refs[a], kx)
                send(2, a, me, in_refs[a], ky)
                send(0, a, me, in_refs[a], 1)
            for a in range(n_gather):
                recv(0, a, me ^ 1)
                send(3, a, me ^ 1, landed(a, me ^ 1), kx)
                send(4, a, me ^ 1, landed(a, me ^ 1), ky)
            for a in range(n_gather):
                recv(1, a, me ^ kx)
                send(5, a, me ^ kx, landed(a, me ^ kx), ky)
                send(7, a, me ^ kx, landed(a, me ^ kx), 1)
                recv(2, a, me ^ ky)
                send(8, a, me ^ ky, landed(a, me ^ ky), 1)
            for a in range(n_gather):
                recv(3, a, me ^ kx ^ 1)
                send(9, a, me ^ kx ^ 1, landed(a, me ^ kx ^ 1), 1)
                recv(4, a, me ^ ky ^ 1)
                send(6, a, me ^ ky ^ 1, landed(a, me ^ ky ^ 1), kx)
                send(10, a, me ^ ky ^ 1, landed(a, me ^ ky ^ 1), 1)
            for a in range(n_gather):
                recv(5, a, me ^ 6)
                send(11, a, me ^ 6, landed(a, me ^ 6), 1)
                recv(6, a, me ^ 7)
                send(12, a, me ^ 7, landed(a, me ^ 7), 1)
            for cp in sends:
                cp.wait_send()

        @pl.when(c == 0)
        def _():
            north = me ^ 1
            own = [copy(0, a, me, in_refs[a], 1) for a in range(n_gather)]
            for cp in own:
                cp.start()
            for a in range(n_gather):
                recv(0, a, north)
                for sem, origin in ((7, north ^ kx), (8, north ^ ky), (9, north ^ kx ^ 1), (10, north ^ ky ^ 1),
                                    (11, north ^ 6), (12, north ^ 7)):
                    recv(sem, a, origin)
            for cp in own:
                cp.wait_send()

        for cp in local:
            cp.wait()

    return pl.pallas_call(
        body, name=name,
        in_specs=[ANY_SPEC] * na, out_specs=[ANY_SPEC] * na,
        out_shape=[jax.ShapeDtypeStruct((N_DEV,) + s.shape, s.dtype) for s in shards],
        scratch_shapes=[pltpu.SemaphoreType.DMA((13, n_gather)), pltpu.SemaphoreType.DMA((13, n_gather)),
                        pltpu.SemaphoreType.DMA((na,))],
    )(*shards)


HBM_SPEC = pl.BlockSpec(memory_space=pltpu.HBM)
SEM_SPEC = pl.BlockSpec(memory_space=pltpu.SEMAPHORE)
SIDE_EFFECT = pltpu.SideEffectType.DATAFLOW_SIDE_EFFECTING


def _hbm(v):
    return pltpu.with_memory_space_constraint(v, pltpu.HBM)


def gather_slices(src, land, me, k):
    return src, land.at[me], land.at[me ^ k]


def exchange_slices(src, land, me, k):
    return src.at[me ^ k], land.at[k - 1], land.at[k - 1]


def split_start(groups, slices, deps, name):
    sizes = [len(srcs) for srcs, _ in groups]
    flat_src = [s for srcs, _ in groups for s in srcs]
    flat_land = [l for _, lands in groups for l in lands]
    na, ng, nd = len(flat_src), len(groups), len(deps)

    def body(*refs):
        src_refs, land_refs = refs[:na], refs[na:2 * na]
        sem_refs = refs[2 * na + nd:2 * na + nd + 2 * ng]
        token_ref = refs[-1]
        x, y, c, me = _mesh_place()
        a0 = 0
        for gi, size in enumerate(sizes):
            send_sems, recv_sems = sem_refs[2 * gi], sem_refs[2 * gi + 1]
            for k in range(1, N_DEV):
                for j in range(size):
                    src, dst, _ = slices(src_refs[a0 + j], land_refs[a0 + j], me, k)
                    pltpu.make_async_remote_copy(
                        src_ref=src, dst_ref=dst, send_sem=send_sems.at[(k - 1) * size + j],
                        recv_sem=recv_sems.at[(k - 1) * size + j],
                        device_id=_peer(x, y, c, k), device_id_type=pl.DeviceIdType.MESH).start()
            a0 += size
        token_ref[...] = jnp.zeros_like(token_ref)

    sems = [pltpu.SemaphoreType.DMA(((N_DEV - 1) * size,)) for size in sizes for _ in range(2)]
    thru = [pltpu.HBM(v.shape, v.dtype) for v in flat_src + flat_land]
    outs = pl.pallas_call(
        body, name=name,
        in_specs=[HBM_SPEC] * (2 * na) + [ANY_SPEC] * nd,
        out_specs=[SEM_SPEC] * (2 * ng) + [HBM_SPEC] * (2 * na) + [pl.BlockSpec(memory_space=pltpu.VMEM)],
        out_shape=sems + thru + [jax.ShapeDtypeStruct((8, LANES), F32)],
        input_output_aliases={i: 2 * ng + i for i in range(2 * na)},
        compiler_params=pltpu.CompilerParams(has_side_effects=SIDE_EFFECT),
    )(*[_hbm(v) for v in flat_src + flat_land], *deps)
    started, a0 = [], 0
    for gi, size in enumerate(sizes):
        srcs = outs[2 * ng + a0:2 * ng + a0 + size]
        lands = outs[2 * ng + na + a0:2 * ng + na + a0 + size]
        started.append((outs[2 * gi], outs[2 * gi + 1], list(srcs), list(lands)))
        a0 += size
    return started, outs[-1]


def split_wait(started, slices, after, name):
    send_sems, recv_sems, srcs, lands = started
    na = len(srcs)

    def body(*refs):
        src_refs, land_refs = refs[:na], refs[na:2 * na]
        send_ref, recv_ref = refs[2 * na], refs[2 * na + 1]
        x, y, c, me = _mesh_place()
        for k in range(1, N_DEV):
            for j in range(na):
                src, _, got = slices(src_refs[j], land_refs[j], me, k)
                cp = pltpu.make_async_remote_copy(
                    src_ref=src, dst_ref=got, send_sem=send_ref.at[(k - 1) * na + j], recv_sem=recv_ref.at[(k - 1) * na + j],
                    device_id=_peer(x, y, c, k), device_id_type=pl.DeviceIdType.MESH)
                cp.wait_send()
                cp.wait_recv()

    outs = pl.pallas_call(
        body, name=name,
        in_specs=[HBM_SPEC] * (2 * na) + [SEM_SPEC, SEM_SPEC] + [ANY_SPEC] * len(after),
        out_specs=[HBM_SPEC] * (2 * na),
        out_shape=[pltpu.HBM(v.shape, v.dtype) for v in srcs + lands],
        input_output_aliases={i: i for i in range(2 * na)},
        compiler_params=pltpu.CompilerParams(has_side_effects=SIDE_EFFECT),
    )(*srcs, *lands, send_sems, recv_sems, *after)
    return list(outs[:na]), list(outs[na:])


def place_own(pack, me, name):
    rows, cols = pack.shape

    def body(me_ref, p_ref, o_ref):
        o_ref[...] = p_ref[...]

    return pl.pallas_call(
        body, name=name,
        grid_spec=pltpu.PrefetchScalarGridSpec(
            num_scalar_prefetch=1, grid=(1,),
            in_specs=[pl.BlockSpec((rows, cols), lambda i, me_ref: (0, 0))],
            out_specs=pl.BlockSpec((None, rows, cols), lambda i, me_ref: (me_ref[0], 0, 0))),
        out_shape=jax.ShapeDtypeStruct((N_DEV, rows, cols), F32),
        compiler_params=_params(("arbitrary",)),
    )(me, pack)


def sum_slots(slots, name):
    nd, rows, cols = slots.shape

    def body(s_ref, o_ref):
        acc = s_ref[0]
        for j in range(1, nd):
            acc = acc + s_ref[j]
        o_ref[...] = acc

    return pl.pallas_call(
        body, name=name, grid=(1,),
        in_specs=[pl.BlockSpec((nd, rows, cols), lambda i: (0, 0, 0))],
        out_specs=pl.BlockSpec((rows, cols), lambda i: (0, 0)),
        out_shape=jax.ShapeDtypeStruct((rows, cols), F32),
        compiler_params=_params(("arbitrary",)),
    )(slots)


def _pack_rows(size):
    rows = -(-size // LANES)
    return -(-rows // 8) * 8


def pack_small(parts):
    out = []
    for p in parts:
        flat = p.reshape(-1).astype(F32)
        rows = _pack_rows(flat.shape[0])
        out.append(jnp.pad(flat, (0, rows * LANES - flat.shape[0])).reshape(rows, LANES))
    return jnp.concatenate(out, axis=0)


def unpack_small(pack, shapes):
    out, off = [], 0
    for shp in shapes:
        size = 1
        for s in shp:
            size *= s
        rows = _pack_rows(size)
        out.append(pack[off:off + rows].reshape(-1)[:size].reshape(shp))
        off += rows
    return out


def local_step(x, target, small, get_w, put_g, put_small):
    col = lambda v: v.reshape(-1, 1)
    (wg1, wu1, wd1), deps = get_w(("wg1", "wu1", "wd1"), ())
    x1, sil1, gp1, s1 = ffn_fwd(x, small["ffn1_norm_g"], wg1, wu1, wd1, "ffn1_fwd", deps)
    (win, wout), _ = get_w(("win", "wout"), (x1,))
    qkv_t, zg_t = mix_in_fwd(x1, small["mix_norm_g"], win, col(small["b_in"]), "mix_in_fwd")
    ya_t = attn_fwd(qkv_t, small["attn_sinks"], "attn_fwd")
    lng, lnb = col(small["gmlp_ln_g"]), col(small["gmlp_ln_b"])
    w_s, b_s = small["gmlp_w_s"][0], small["gmlp_b_s"][0]
    yg_t = gmlp_fwd(zg_t, lng, lnb, w_s, b_s, "gmlp_fwd")
    gao, ggo = col(small["attn_out_norm_g"]), col(small["gmlp_out_norm_g"])
    x2 = mix_out_fwd(x1, ya_t, yg_t, gao, ggo, wout, small["b_out"], "mix_out_fwd")
    (wg2, wu2, wd2), _ = get_w(("wg2", "wu2", "wd2"), (x2,))
    x3, sil2, gp2, s2 = ffn_fwd(x2, small["ffn2_norm_g"], wg2, wu2, wd2, "ffn2_fwd")

    gs = {}
    dx3, gs["final_norm_g"], loss = final_bwd(x3, target, small["final_norm_g"].reshape(1, -1), "final_bwd")

    dx2, da, db, h, gs["ffn2_norm_g"] = ffn_bwd(x2, dx3, small["ffn2_norm_g"], sil2, gp2, wg2, wu2, wd2, "ffn2_bwd")
    deps = put_g({"wg2": grad_tn(da, h, 1.0, "ffn2_dwg"), "wu2": grad_tn(db, h, 1.0, "ffn2_dwu"),
                  "wd2": grad_tn(s2, dx3, FFN_RES, "ffn2_dwd")})

    dya_t, dyg_t, y_t, dgao, dggo, gs["b_out"] = mix_out_bwd(dx2, ya_t, yg_t, gao, ggo, wout, "mix_out_bwd", deps)
    gs["attn_out_norm_g"], gs["gmlp_out_norm_g"] = dgao.reshape(1, -1), dggo.reshape(1, -1)
    g_wout = grad_nn([y_t], dx2, "dwout")
    dzg_t, dws, dbs, dlng, dlnb = gmlp_bwd(zg_t, dyg_t, lng, lnb, w_s, b_s, "gmlp_bwd")
    gs["gmlp_w_s"], gs["gmlp_b_s"] = dws[None], dbs[None]
    gs["gmlp_ln_g"], gs["gmlp_ln_b"] = dlng.reshape(1, -1), dlnb.reshape(1, -1)
    dq_t, dk_t, dv_t, dsinks = attn_bwd(qkv_t, dya_t, small["attn_sinks"], "attn_bwd")
    gs["attn_sinks"] = dsinks.reshape(1, -1)
    dx1, h2, gs["mix_norm_g"], dbin = mix_in_bwd(x1, dx2, small["mix_norm_g"], dq_t, dk_t, dv_t, dzg_t, win,
                                                 "mix_in_bwd")
    gs["b_in"] = dbin.reshape(1, -1)
    deps = put_g({"wout": g_wout, "win": grad_nn([dq_t, dk_t, dv_t, dzg_t], h2, "dwin")})

    dx0, da, db, h, gs["ffn1_norm_g"] = ffn_bwd(x, dx1, small["ffn1_norm_g"], sil1, gp1, wg1, wu1, wd1, "ffn1_bwd",
                                                deps)
    deps = put_small(gs, loss)
    deps = put_g({"wg1": grad_tn(da, h, 1.0, "ffn1_dwg", deps)})
    deps = put_g({"wu1": grad_tn(db, h, 1.0, "ffn1_dwu", deps)})
    put_g({"wd1": grad_tn(s1, dx1, FFN_RES, "ffn1_dwd", deps)})
    return dx0


SMALL_NAMES = ["ffn1_norm_g", "mix_norm_g", "b_in", "attn_sinks", "gmlp_ln_g", "gmlp_ln_b", "gmlp_w_s", "gmlp_b_s",
               "attn_out_norm_g", "gmlp_out_norm_g", "b_out", "ffn2_norm_g", "final_norm_g"]
BIG_NAMES = {"wg1": ("ffn1_w_gate", True), "wu1": ("ffn1_w_up", True), "wd1": ("ffn1_w_down", False),
             "win": ("w_in", True), "wout": ("w_out", False),
             "wg2": ("ffn2_w_gate", True), "wu2": ("ffn2_w_up", True), "wd2": ("ffn2_w_down", False)}
WEIGHT_ORDER = ["ffn1_norm_g", "ffn1_w_gate", "ffn1_w_up", "ffn1_w_down", "mix_norm_g", "w_in", "b_in", "attn_sinks",
                "gmlp_ln_g", "gmlp_ln_b", "gmlp_w_s", "gmlp_b_s", "attn_out_norm_g", "gmlp_out_norm_g", "w_out",
                "b_out", "ffn2_norm_g", "ffn2_w_gate", "ffn2_w_up", "ffn2_w_down", "final_norm_g"]


def kernel(x, ffn1_norm_g, ffn1_w_gate, ffn1_w_up, ffn1_w_down, mix_norm_g, w_in, b_in, attn_sinks, gmlp_ln_g, gmlp_ln_b, gmlp_w_s, gmlp_b_s, attn_out_norm_g, gmlp_out_norm_g, w_out, b_out, ffn2_norm_g, ffn2_w_gate, ffn2_w_up, ffn2_w_down, final_norm_g, loss_target, m_ffn1_norm_g, m_ffn1_w_gate, m_ffn1_w_up, m_ffn1_w_down, m_mix_norm_g, m_w_in, m_b_in, m_attn_sinks, m_gmlp_ln_g, m_gmlp_ln_b, m_gmlp_w_s, m_gmlp_b_s, m_attn_out_norm_g, m_gmlp_out_norm_g, m_w_out, m_b_out, m_ffn2_norm_g, m_ffn2_w_gate, m_ffn2_w_up, m_ffn2_w_down, m_final_norm_g, v_ffn1_norm_g, v_ffn1_w_gate, v_ffn1_w_up, v_ffn1_w_down, v_mix_norm_g, v_w_in, v_b_in, v_attn_sinks, v_gmlp_ln_g, v_gmlp_ln_b, v_gmlp_w_s, v_gmlp_b_s, v_attn_out_norm_g, v_gmlp_out_norm_g, v_w_out, v_b_out, v_ffn2_norm_g, v_ffn2_w_gate, v_ffn2_w_up, v_ffn2_w_down, v_final_norm_g):
    args = dict(locals())
    w = {n: args[n] for n in WEIGHT_ORDER}
    m = {n: args["m_" + n] for n in WEIGHT_ORDER}
    v = {n: args["v_" + n] for n in WEIGHT_ORDER}

    shards = {}
    for key, (pname, transposed) in BIG_NAMES.items():
        mat = w[pname][0]
        shards[key] = (mat.T if transposed else mat).astype(BF16)
    d_model = x.shape[-1]
    flat = lambda g: g.reshape(-1, d_model)
    me = _mesh_place()[3].astype(jnp.int32).reshape(1)
    first = ("wg1", "wu1", "wd1")
    later = (("win", "wout"), ("wg2", "wu2", "wd2"))
    gathers, exchanges, last_token = {}, [], []

    def get_w(keys, after):
        if keys == first:
            order = list(first) + [k for grp in later for k in grp]
            full = dict(zip(order, allgather_rows([shards[k] for k in order], len(first), "allgather_ffn1")))
            started, token = split_start([([shards[k] for k in grp], [full[k] for k in grp]) for grp in later],
                                         gather_slices, (), "gather_start")
            gathers.update(zip(later, started))
            return [flat(full[k]) for k in keys], (token,)
        _, lands = split_wait(gathers[keys], gather_slices, after, "gather_wait_" + keys[0])
        return [flat(land) for land in lands], ()

    def put_g(gb):
        keys = tuple(gb)
        g3 = [gb[k].reshape(N_DEV, -1, d_model) for k in keys]
        lands = [lax.empty((N_DEV - 1,) + g.shape[1:], BF16) for g in g3]
        started, token = split_start([(g3, lands)], exchange_slices, (), "exchange_start_" + keys[0])
        exchanges.append((keys, started[0]))
        last_token[:] = [token]
        return (token,)

    small_started = []

    def put_small(gs, loss):
        pack = pack_small([gs[n] for n in SMALL_NAMES] + [loss[:, :1]])
        started, token = split_start([([pack], [place_own(pack, me, "place_small")])], gather_slices, (), "small_start")
        small_started[:] = started
        return (token,)

    small = {n: w[n] for n in SMALL_NAMES}
    grad_x = local_step(x[0], loss_target[0], small, get_w, put_g, put_small)

    grads, deltas, new_m, new_v = {}, {}, {}, {}
    after = tuple(last_token)
    for keys, started in exchanges:
        g3, lands = split_wait(started, exchange_slices, after, "exchange_wait_" + keys[0])
        for key, own, land in zip(keys, g3, lands):
            pname, transposed = BIG_NAMES[key]
            g = sum_partials(land, own, me, "sum_" + key)
            after = (g,)
            to_rows = (lambda p: p[0].T) if transposed else (lambda p: p[0])
            from_rows = (lambda r: r.T[None]) if transposed else (lambda r: r[None])
            d_, m_, v_ = adamw(to_rows(w[pname]), g, to_rows(m[pname]), to_rows(v[pname]), "adamw_" + key)
            grads[pname], deltas[pname], new_m[pname], new_v[pname] = (from_rows(g), from_rows(d_), from_rows(m_),
                                                                        from_rows(v_))

    shapes = [w[n].shape for n in SMALL_NAMES]
    _, (slots,) = split_wait(small_started[0], gather_slices, after, "small_wait")
    total = sum_slots(slots, "sum_small")
    small_g = unpack_small(total, shapes + [(1, 1)])
    loss_total = small_g[-1].reshape(())
    gpack = total[:sum(_pack_rows(w[n].size) for n in SMALL_NAMES)]
    d_, m_, v_ = adamw(pack_small([w[n] for n in SMALL_NAMES]), gpack, pack_small([m[n] for n in SMALL_NAMES]),
                       pack_small([v[n] for n in SMALL_NAMES]), "adamw_small")
    for n, g_, dd, mm, vv in zip(SMALL_NAMES, small_g[:-1], unpack_small(d_, shapes), unpack_small(m_, shapes),
                                 unpack_small(v_, shapes)):
        grads[n], deltas[n], new_m[n], new_v[n] = g_, dd, mm, vv

    return (loss_total, grad_x[None], *[grads[n] for n in WEIGHT_ORDER], *[deltas[n] for n in WEIGHT_ORDER],
            *[new_m[n] for n in WEIGHT_ORDER], *[new_v[n] for n in WEIGHT_ORDER])
```

```python
import functools

import jax
import jax.numpy as jnp
from jax import lax
from jax.experimental import pallas as pl
from jax.experimental.pallas import tpu as pltpu

F32 = jnp.float32
BF16 = jnp.bfloat16

N_DEV = 8
N_Q_HEADS = 8
N_KV_HEADS = 2
HEAD_DIM = 64
ATTN_WIDTH = N_Q_HEADS * HEAD_DIM
KV_WIDTH = N_KV_HEADS * HEAD_DIM
QKV_WIDTH = ATTN_WIDTH + 2 * KV_WIDTH
BLOCK = 128
GMLP_GROUPS = 8
GMLP_GROUP_DIM = 64
GMLP_WIDTH = GMLP_GROUPS * GMLP_GROUP_DIM
EPS = 1e-6
FFN_RES = 0.5
ATTN_SCALE = HEAD_DIM ** -0.5
MASK_VALUE = -1e30

ADAM_LR = 0.001
ADAM_B1 = 0.9
ADAM_B2 = 0.999
ADAM_EPS = 1e-08
ADAM_WD = 0.01
ADAM_STEP = 10

V7X_VMEM_BYTES = 64 * 1024 * 1024
VMEM_LIMIT = 62 * 1024 * 1024
LANES = 128
MXU_WIDTH = 256

NT = (((1,), (1,)), ((), ()))
TN = (((0,), (0,)), ((), ()))


def _dot(a, b):
    return jnp.dot(a, b, preferred_element_type=F32)


def _dg(a, b, dims):
    return lax.dot_general(a, b, dims, preferred_element_type=F32)


def _params(semantics=None):
    return pltpu.CompilerParams(dimension_semantics=semantics, vmem_limit_bytes=VMEM_LIMIT)


def _resident(shape):
    zeros = (0,) * len(shape)
    return pl.BlockSpec(shape, lambda *_: zeros, pipeline_mode=pl.Buffered(1))


def _drop_deps(body, n_in, n_deps):
    return lambda *refs: body(*refs[:n_in], *refs[n_in + n_deps:])


ANY_SPEC = pl.BlockSpec(memory_space=pl.ANY)


def _token_tile(t):
    return min(t, 512)


def _f_chunk(f):
    for c in (256, 128):
        if f % c == 0:
            return c
    return f


def ffn_fwd(x, gain, wg_t, wu_t, wd, name, deps=()):
    t, d = x.shape
    f = wd.shape[0]
    tm = _token_tile(t)
    fc = _f_chunk(f)

    def body(x_ref, g_ref, wg_ref, wu_ref, wd_ref, xo_ref, sil_ref, gp_ref, s_ref):
        xv = x_ref[...]
        r = lax.rsqrt(jnp.mean(xv * xv, axis=-1, keepdims=True) + EPS)
        h = (xv * r * g_ref[...]).astype(BF16)
        for c in range(f // fc):
            sl = slice(c * fc, (c + 1) * fc)
            a = _dg(h, wg_ref[sl, :], NT)
            b = _dg(h, wu_ref[sl, :], NT)
            sig = jax.nn.sigmoid(a)
            sil = a * sig
            sil_ref[:, sl] = sil.astype(BF16)
            gp_ref[:, sl] = (b * (sig + sil - sil * sig)).astype(BF16)
            s_ref[:, sl] = (sil * b).astype(BF16)
        xo_ref[...] = xv + FFN_RES * _dot(s_ref[...], wd_ref[...])

    wide = lambda: pl.BlockSpec((tm, f), lambda i: (i, 0))
    return pl.pallas_call(
        _drop_deps(body, 5, len(deps)), name=name, grid=(t // tm,),
        in_specs=[pl.BlockSpec((tm, d), lambda i: (i, 0)), _resident((1, d)),
                  _resident((f, d)), _resident((f, d)), _resident((f, d))] + [ANY_SPEC] * len(deps),
        out_specs=[pl.BlockSpec((tm, d), lambda i: (i, 0)), wide(), wide(), wide()],
        out_shape=[jax.ShapeDtypeStruct((t, d), F32)] + [jax.ShapeDtypeStruct((t, f), BF16)] * 3,
        compiler_params=_params(("arbitrary",)),
    )(x, gain, wg_t, wu_t, wd, *deps)


def _rmsnorm_bwd_rows(xv, r, gain, dh):
    n = xv * r
    dgain = jnp.sum(dh * n, axis=0, keepdims=True)
    dn = dh * gain
    dx = r * (dn - n * jnp.mean(dn * n, axis=-1, keepdims=True))
    return dx, dgain


def ffn_bwd(x, dxo, gain, sil, gp, wg_t, wu_t, wd, name, deps=()):
    t, d = x.shape
    f = wd.shape[0]
    tm = _token_tile(t)
    nh = 2 if f % (2 * LANES) == 0 else 1
    fh = f // nh
    chunks = [(off, min(MXU_WIDTH, fh - off)) for off in range(0, fh, MXU_WIDTH)]

    def body(x_ref, dxo_ref, g_ref, sil_ref, gp_ref, wg_ref, wu_ref, wd_ref,
             dx_ref, da_ref, db_ref, h_ref, dg_ref, dh_ref):
        i, j = pl.program_id(0), pl.program_id(1)
        gain_v = g_ref[...]

        @pl.when(j == 0)
        def _():
            xv = x_ref[...]
            r = lax.rsqrt(jnp.mean(xv * xv, axis=-1, keepdims=True) + EPS)
            h_ref[...] = (xv * r * gain_v).astype(BF16)
            dh_ref[...] = jnp.zeros_like(dh_ref)

        df = (FFN_RES * dxo_ref[...]).astype(BF16)
        for off, size in chunks:
            sl = slice(off, off + size)
            rows = pl.ds(pl.multiple_of(j * fh + off, LANES), size)
            ds = _dg(df, wd_ref[rows, :], NT)
            da_ref[:, sl] = (ds * gp_ref[:, sl].astype(F32)).astype(BF16)
            db_ref[:, sl] = (ds * sil_ref[:, sl].astype(F32)).astype(BF16)
        half_rows = pl.ds(pl.multiple_of(j * fh, LANES), fh)
        dh_ref[...] += _dot(da_ref[...], wg_ref[half_rows, :]) + _dot(db_ref[...], wu_ref[half_rows, :])

        @pl.when(j == nh - 1)
        def _():
            xv = x_ref[...]
            r = lax.rsqrt(jnp.mean(xv * xv, axis=-1, keepdims=True) + EPS)
            dx, dgain = _rmsnorm_bwd_rows(xv, r, gain_v, dh_ref[...])
            dx_ref[...] = dxo_ref[...] + dx

            @pl.when(i == 0)
            def _():
                dg_ref[...] = jnp.zeros_like(dg_ref)

            dg_ref[...] += dgain

    tok = lambda: pl.BlockSpec((tm, d), lambda i, j: (i, 0))
    half = lambda: pl.BlockSpec((tm, fh), lambda i, j: (i, j))
    return pl.pallas_call(
        _drop_deps(body, 8, len(deps)), name=name, grid=(t // tm, nh),
        in_specs=[tok(), tok(), _resident((1, d)), half(), half(),
                  _resident((f, d)), _resident((f, d)), _resident((f, d))] + [ANY_SPEC] * len(deps),
        out_specs=[tok(), half(), half(), tok(), pl.BlockSpec((1, d), lambda i, j: (0, 0))],
        out_shape=[jax.ShapeDtypeStruct((t, d), F32), jax.ShapeDtypeStruct((t, f), BF16),
                   jax.ShapeDtypeStruct((t, f), BF16), jax.ShapeDtypeStruct((t, d), BF16),
                   jax.ShapeDtypeStruct((1, d), F32)],
        scratch_shapes=[pltpu.VMEM((tm, d), F32)],
        compiler_params=_params(("arbitrary", "arbitrary")),
    )(x, dxo, gain, sil, gp, wg_t, wu_t, wd, *deps)


def _row_tile(m, cap):
    if m <= cap:
        return m
    best = None
    for bm in range(LANES, cap + 1, LANES):
        if m % bm == 0:
            best = bm
    return best if best is not None else m


def grad_tn(a, b, scale, name, deps=()):
    t, m = a.shape
    n = b.shape[1]
    bm = _row_tile(m, 1408)
    bk = min(t, 1024)
    nk = t // bk

    def body(a_ref, b_ref, o_ref, acc_ref):
        k = pl.program_id(1)

        @pl.when(k == 0)
        def _():
            acc_ref[...] = jnp.zeros_like(acc_ref)

        acc_ref[...] += _dg(a_ref[...].astype(BF16), b_ref[...].astype(BF16), TN)

        @pl.when(k == nk - 1)
        def _():
            o_ref[...] = (scale * acc_ref[...]).astype(BF16)

    return pl.pallas_call(
        _drop_deps(body, 2, len(deps)), name=name, grid=(m // bm, nk),
        in_specs=[pl.BlockSpec((bk, bm), lambda i, k: (k, i)), pl.BlockSpec((bk, n), lambda i, k: (k, 0))]
        + [ANY_SPEC] * len(deps),
        out_specs=pl.BlockSpec((bm, n), lambda i, k: (i, 0)),
        out_shape=jax.ShapeDtypeStruct((m, n), BF16),
        scratch_shapes=[pltpu.VMEM((bm, n), F32)],
        compiler_params=_params(("arbitrary", "arbitrary")),
    )(a, b, *deps)


def grad_nn(a_list, b, name):
    t, n = b.shape
    ms = [a.shape[0] for a in a_list]
    m = sum(ms)
    bk = min(t, 1024)
    nk = t // bk
    na = len(a_list)

    def body(*refs):
        a_refs, b_ref, o_ref, acc_ref = refs[:na], refs[na], refs[na + 1], refs[na + 2]
        k = pl.program_id(0)

        @pl.when(k == 0)
        def _():
            acc_ref[...] = jnp.zeros_like(acc_ref)

        bv = b_ref[...].astype(BF16)
        off = 0
        for a_ref, mi in zip(a_refs, ms):
            acc_ref[off:off + mi, :] += _dot(a_ref[...].astype(BF16), bv)
            off += mi

        @pl.when(k == nk - 1)
        def _():
            o_ref[...] = acc_ref[...].astype(BF16)

    return pl.pallas_call(
        body, name=name, grid=(nk,),
        in_specs=[pl.BlockSpec((mi, bk), lambda k: (0, k)) for mi in ms] + [pl.BlockSpec((bk, n), lambda k: (k, 0))],
        out_specs=pl.BlockSpec((m, n), lambda k: (0, 0)),
        out_shape=jax.ShapeDtypeStruct((m, n), BF16),
        scratch_shapes=[pltpu.VMEM((m, n), F32)],
        compiler_params=_params(("arbitrary",)),
    )(*a_list, b)


def mix_in_fwd(x, gain, win_t, b_in_col, name):
    t, d = x.shape
    w = win_t.shape[0]
    tm = _token_tile(t)

    def body(x_ref, g_ref, w_ref, bias_ref, qkv_ref, zg_ref):
        xv = x_ref[...]
        r = lax.rsqrt(jnp.mean(xv * xv, axis=-1, keepdims=True) + EPS)
        h = (xv * r * g_ref[...]).astype(BF16)
        qkv_ref[...] = (_dg(w_ref[:QKV_WIDTH, :], h, NT) + bias_ref[:QKV_WIDTH, :]).astype(BF16)
        zg_ref[...] = (_dg(w_ref[QKV_WIDTH:, :], h, NT) + bias_ref[QKV_WIDTH:, :]).astype(BF16)

    return pl.pallas_call(
        body, name=name, grid=(t // tm,),
        in_specs=[pl.BlockSpec((tm, d), lambda i: (i, 0)), _resident((1, d)), _resident((w, d)), _resident((w, 1))],
        out_specs=[pl.BlockSpec((QKV_WIDTH, tm), lambda i: (0, i)), pl.BlockSpec((w - QKV_WIDTH, tm), lambda i: (0, i))],
        out_shape=[jax.ShapeDtypeStruct((QKV_WIDTH, t), BF16), jax.ShapeDtypeStruct((w - QKV_WIDTH, t), BF16)],
        compiler_params=_params(("arbitrary",)),
    )(x, gain, win_t, b_in_col)


def _attn_specs(nb):
    last = nb - 1
    cur = lambda n: jnp.minimum(n, last)
    prev = lambda n: jnp.maximum(jnp.minimum(n, last) - 1, 0)
    k_row = ATTN_WIDTH // KV_WIDTH
    return [
        pl.BlockSpec((ATTN_WIDTH, BLOCK), lambda n: (0, cur(n))),
        pl.BlockSpec((KV_WIDTH, BLOCK), lambda n: (k_row, prev(n))),
        pl.BlockSpec((KV_WIDTH, BLOCK), lambda n: (k_row, cur(n))),
        pl.BlockSpec((KV_WIDTH, BLOCK), lambda n: (k_row + 1, prev(n))),
        pl.BlockSpec((KV_WIDTH, BLOCK), lambda n: (k_row + 1, cur(n))),
    ]


ATTN_REP = N_Q_HEADS // N_KV_HEADS


def _group_rows(ref, g):
    first = g * ATTN_REP
    return jnp.concatenate([ref[(first + r) * HEAD_DIM:(first + r + 1) * HEAD_DIM, :] for r in range(ATTN_REP)],
                           axis=1)


def _ungroup_rows(ref, g, val):
    first = g * ATTN_REP
    for r in range(ATTN_REP):
        ref[(first + r) * HEAD_DIM:(first + r + 1) * HEAD_DIM, :] = val[:, r * BLOCK:(r + 1) * BLOCK]


def _attn_upper():
    key = lax.broadcasted_iota(jnp.int32, (BLOCK, ATTN_REP * BLOCK), 0)
    qry = lax.broadcasted_iota(jnp.int32, (BLOCK, ATTN_REP * BLOCK), 1) & (BLOCK - 1)
    return key > qry


def _attn_probs(q, kp, kc, sink_ref, g, upper, n):
    s = jnp.where(upper, _dg(kp, q, TN), _dg(kc, q, TN)) * ATTN_SCALE
    s = jnp.where(upper & (n == 0), MASK_VALUE, s)
    sink = jnp.concatenate([jnp.full((1, BLOCK), sink_ref[0, g * ATTN_REP + r], F32) for r in range(ATTN_REP)], axis=1)
    m = jnp.maximum(jnp.max(s, axis=0, keepdims=True), sink)
    e = jnp.exp(s - m)
    es = jnp.exp(sink - m)
    inv = 1.0 / (jnp.sum(e, axis=0, keepdims=True) + es)
    return e * inv, es * inv


def _split_band(upper, val):
    return jnp.where(upper, val, 0.0).astype(BF16), jnp.where(upper, 0.0, val).astype(BF16)


def attn_fwd(qkv_t, sinks, name):
    t = qkv_t.shape[1]
    nb = t // BLOCK

    def body(sink_ref, q_ref, kp_ref, kc_ref, vp_ref, vc_ref, y_ref):
        n = pl.program_id(0)
        upper = _attn_upper()
        for g in range(N_KV_HEADS):
            gs = slice(g * HEAD_DIM, (g + 1) * HEAD_DIM)
            p, _ = _attn_probs(_group_rows(q_ref, g), kp_ref[gs, :], kc_ref[gs, :], sink_ref, g, upper, n)
            pp, pc = _split_band(upper, p)
            _ungroup_rows(y_ref, g, (_dot(vp_ref[gs, :], pp) + _dot(vc_ref[gs, :], pc)).astype(BF16))

    return pl.pallas_call(
        body, name=name, grid=(nb,),
        in_specs=[pl.BlockSpec(memory_space=pltpu.SMEM)] + _attn_specs(nb),
        out_specs=pl.BlockSpec((ATTN_WIDTH, BLOCK), lambda n: (0, n)),
        out_shape=jax.ShapeDtypeStruct((ATTN_WIDTH, t), BF16),
        compiler_params=_params(("arbitrary",)),
    )(sinks, qkv_t, qkv_t, qkv_t, qkv_t, qkv_t)


def attn_bwd(qkv_t, dy_t, sinks, name):
    t = qkv_t.shape[1]
    nb = t // BLOCK

    def body(sink_ref, q_ref, kp_ref, kc_ref, vp_ref, vc_ref, do_ref, dq_ref, dk_ref, dv_ref, dsink_ref,
             ck_ref, cv_ref, sacc_ref):
        n = pl.program_id(0)

        @pl.when(n == 0)
        def _():
            sacc_ref[...] = jnp.zeros_like(sacc_ref)

        @pl.when(n < nb)
        def _():
            upper = _attn_upper()
            for g in range(N_KV_HEADS):
                gs = slice(g * HEAD_DIM, (g + 1) * HEAD_DIM)
                kp, kc, vp, vc = kp_ref[gs, :], kc_ref[gs, :], vp_ref[gs, :], vc_ref[gs, :]
                q = _group_rows(q_ref, g)
                do = _group_rows(do_ref, g)
                p, ps = _attn_probs(q, kp, kc, sink_ref, g, upper, n)
                dp = jnp.where(upper, _dg(vp, do, TN), _dg(vc, do, TN))
                delta = jnp.sum(p * dp, axis=0, keepdims=True)
                dsink = -(ps * delta)
                for r in range(ATTN_REP):
                    hd = g * ATTN_REP + r
                    sacc_ref[hd:hd + 1, :] += dsink[:, r * BLOCK:(r + 1) * BLOCK]
                dsp, dsc = _split_band(upper, p * (dp - delta))
                pp, pc = _split_band(upper, p)
                _ungroup_rows(dq_ref, g, (ATTN_SCALE * (_dot(kp, dsp) + _dot(kc, dsc))).astype(BF16))

                @pl.when(n > 0)
                def _():
                    dk_ref[gs, :] = (ck_ref[gs, :] + ATTN_SCALE * _dg(q, dsp, NT)).astype(BF16)
                    dv_ref[gs, :] = (cv_ref[gs, :] + _dg(do, pp, NT)).astype(BF16)

                ck_ref[gs, :] = ATTN_SCALE * _dg(q, dsc, NT)
                cv_ref[gs, :] = _dg(do, pc, NT)

        @pl.when(n == nb)
        def _():
            dk_ref[...] = ck_ref[...].astype(BF16)
            dv_ref[...] = cv_ref[...].astype(BF16)
            dsink_ref[...] = jnp.sum(sacc_ref[...], axis=1, keepdims=True)

    last = nb - 1
    done = lambda n: jnp.maximum(n - 1, 0)
    outs = pl.pallas_call(
        body, name=name, grid=(nb + 1,),
        in_specs=[pl.BlockSpec(memory_space=pltpu.SMEM)] + _attn_specs(nb)
        + [pl.BlockSpec((ATTN_WIDTH, BLOCK), lambda n: (0, jnp.minimum(n, last)))],
        out_specs=[pl.BlockSpec((ATTN_WIDTH, BLOCK), lambda n: (0, jnp.minimum(n, last))),
                   pl.BlockSpec((KV_WIDTH, BLOCK), lambda n: (0, done(n))),
                   pl.BlockSpec((KV_WIDTH, BLOCK), lambda n: (0, done(n))),
                   pl.BlockSpec((N_Q_HEADS, 1), lambda n: (0, 0))],
        out_shape=[jax.ShapeDtypeStruct((ATTN_WIDTH, t), BF16), jax.ShapeDtypeStruct((KV_WIDTH, t), BF16),
                   jax.ShapeDtypeStruct((KV_WIDTH, t), BF16), jax.ShapeDtypeStruct((N_Q_HEADS, 1), F32)],
        scratch_shapes=[pltpu.VMEM((KV_WIDTH, BLOCK), F32), pltpu.VMEM((KV_WIDTH, BLOCK), F32),
                        pltpu.VMEM((N_Q_HEADS, BLOCK), F32)],
        compiler_params=_params(("arbitrary",)),
    )(sinks, qkv_t, qkv_t, qkv_t, qkv_t, qkv_t, dy_t)
    return outs


GELU_C = 0.7978845608028654
GELU_A = 0.044715


def _gelu(x):
    return 0.5 * x * (1.0 + jnp.tanh(GELU_C * (x + GELU_A * x * x * x)))


def _gelu_grad(x):
    th = jnp.tanh(GELU_C * (x + GELU_A * x * x * x))
    return 0.5 * (1.0 + th) + 0.5 * x * (1.0 - th * th) * GELU_C * (1.0 + 3.0 * GELU_A * x * x)


def _gmlp_parts(zg, lng, lnb):
    z = _gelu(zg)
    u = z[:GMLP_WIDTH, :]
    vv = z[GMLP_WIDTH:, :]
    mu = jnp.mean(vv, axis=0, keepdims=True)
    xc = vv - mu
    rstd = lax.rsqrt(jnp.mean(xc * xc, axis=0, keepdims=True) + EPS)
    xhat = xc * rstd
    return u, xhat, rstd, xhat * lng + lnb


def _causal(w):
    row = lax.broadcasted_iota(jnp.int32, (BLOCK, BLOCK), 0)
    col = lax.broadcasted_iota(jnp.int32, (BLOCK, BLOCK), 1)
    return jnp.where(col <= row, w, 0.0)


GMLP_CHUNKS_PER_STEP = 4


def _stack_chunks(v, nc):
    return jnp.concatenate([v[:, c * BLOCK:(c + 1) * BLOCK] for c in range(nc)], axis=0)


def _unstack_chunks(v, nc):
    rows = v.shape[0] // nc
    return jnp.concatenate([v[c * rows:(c + 1) * rows, :] for c in range(nc)], axis=1)


def gmlp_fwd(zg_t, lng, lnb, w_s, b_s, name):
    t = zg_t.shape[1]
    nc = min(GMLP_CHUNKS_PER_STEP, t // BLOCK)
    tw = nc * BLOCK

    def body(zg_ref, lng_ref, lnb_ref, w_ref, bs_ref, y_ref):
        u, _, _, vn = _gmlp_parts(zg_ref[...].astype(F32), lng_ref[...], lnb_ref[...])
        for g in range(GMLP_GROUPS):
            gs = slice(g * GMLP_GROUP_DIM, (g + 1) * GMLP_GROUP_DIM)
            wc = _causal(w_ref[g]).astype(BF16)
            mixed = _dg(_stack_chunks(vn[gs, :].astype(BF16), nc), wc, NT) + bs_ref[g:g + 1, :]
            y_ref[gs, :] = (u[gs, :] * _unstack_chunks(mixed, nc)).astype(BF16)

    return pl.pallas_call(
        body, name=name, grid=(t // tw,),
        in_specs=[pl.BlockSpec((2 * GMLP_WIDTH, tw), lambda n: (0, n)), _resident((GMLP_WIDTH, 1)),
                  _resident((GMLP_WIDTH, 1)), _resident((GMLP_GROUPS, BLOCK, BLOCK)), _resident((GMLP_GROUPS, BLOCK))],
        out_specs=pl.BlockSpec((GMLP_WIDTH, tw), lambda n: (0, n)),
        out_shape=jax.ShapeDtypeStruct((GMLP_WIDTH, t), BF16),
        compiler_params=_params(("arbitrary",)),
    )(zg_t, lng, lnb, w_s, b_s)


def gmlp_bwd(zg_t, dy_t, lng, lnb, w_s, b_s, name):
    t = zg_t.shape[1]
    nc = min(GMLP_CHUNKS_PER_STEP, t // BLOCK)
    tw = nc * BLOCK
    nt = t // tw

    def body(zg_ref, dy_ref, lng_ref, lnb_ref, w_ref, bs_ref, dzg_ref, dw_ref, dbs_ref, dlng_ref, dlnb_ref,
             gacc_ref, bacc_ref):
        n = pl.program_id(0)

        @pl.when(n == 0)
        def _():
            dw_ref[...] = jnp.zeros_like(dw_ref)
            dbs_ref[...] = jnp.zeros_like(dbs_ref)
            gacc_ref[...] = jnp.zeros_like(gacc_ref)
            bacc_ref[...] = jnp.zeros_like(bacc_ref)

        zg = zg_ref[...].astype(F32)
        lng_v = lng_ref[...]
        u, xhat, rstd, vn = _gmlp_parts(zg, lng_v, lnb_ref[...])
        dy = dy_ref[...].astype(F32)
        dvn_parts = []
        for g in range(GMLP_GROUPS):
            gs = slice(g * GMLP_GROUP_DIM, (g + 1) * GMLP_GROUP_DIM)
            wc = _causal(w_ref[g]).astype(BF16)
            vn_s = _stack_chunks(vn[gs, :].astype(BF16), nc)
            mixed = _unstack_chunks(_dg(vn_s, wc, NT) + bs_ref[g:g + 1, :], nc)
            dy_g = dy[gs, :]
            dmixed = dy_g * u[gs, :]
            dm_s = _stack_chunks(dmixed.astype(BF16), nc)
            dzg_ref[gs, :] = (dy_g * mixed * _gelu_grad(zg[gs, :])).astype(BF16)
            dw_ref[g] += _causal(_dg(dm_s, vn_s, TN))
            dbs_ref[g:g + 1, :] += _lane_fold(jnp.sum(dmixed, axis=0, keepdims=True))
            dvn_parts.append(_unstack_chunks(_dot(dm_s, wc), nc))
        dvn = jnp.concatenate(dvn_parts, axis=0)
        gacc_ref[...] += _lane_fold(dvn * xhat)
        bacc_ref[...] += _lane_fold(dvn)
        dxhat = dvn * lng_v
        m1 = jnp.mean(dxhat, axis=0, keepdims=True)
        m2 = jnp.mean(dxhat * xhat, axis=0, keepdims=True)
        dvv = rstd * (dxhat - m1 - xhat * m2)
        dzg_ref[GMLP_WIDTH:, :] = (dvv * _gelu_grad(zg[GMLP_WIDTH:, :])).astype(BF16)

        @pl.when(n == nt - 1)
        def _():
            dlng_ref[...] = jnp.sum(gacc_ref[...], axis=1, keepdims=True)
            dlnb_ref[...] = jnp.sum(bacc_ref[...], axis=1, keepdims=True)

    const2 = lambda n: (0, 0)
    return pl.pallas_call(
        body, name=name, grid=(nt,),
        in_specs=[pl.BlockSpec((2 * GMLP_WIDTH, tw), lambda n: (0, n)), pl.BlockSpec((GMLP_WIDTH, tw), lambda n: (0, n)),
                  _resident((GMLP_WIDTH, 1)), _resident((GMLP_WIDTH, 1)),
                  _resident((GMLP_GROUPS, BLOCK, BLOCK)), _resident((GMLP_GROUPS, BLOCK))],
        out_specs=[pl.BlockSpec((2 * GMLP_WIDTH, tw), lambda n: (0, n)),
                   pl.BlockSpec((GMLP_GROUPS, BLOCK, BLOCK), lambda n: (0, 0, 0)),
                   pl.BlockSpec((GMLP_GROUPS, BLOCK), const2),
                   pl.BlockSpec((GMLP_WIDTH, 1), const2), pl.BlockSpec((GMLP_WIDTH, 1), const2)],
        out_shape=[jax.ShapeDtypeStruct((2 * GMLP_WIDTH, t), BF16),
                   jax.ShapeDtypeStruct((GMLP_GROUPS, BLOCK, BLOCK), F32),
                   jax.ShapeDtypeStruct((GMLP_GROUPS, BLOCK), F32),
                   jax.ShapeDtypeStruct((GMLP_WIDTH, 1), F32), jax.ShapeDtypeStruct((GMLP_WIDTH, 1), F32)],
        scratch_shapes=[pltpu.VMEM((GMLP_WIDTH, BLOCK), F32), pltpu.VMEM((GMLP_WIDTH, BLOCK), F32)],
        compiler_params=_params(("arbitrary",)),
    )(zg_t, dy_t, lng, lnb, w_s, b_s)


def _rmsnorm_cols(y, gain_col):
    r = lax.rsqrt(jnp.mean(y * y, axis=0, keepdims=True) + EPS)
    n = y * r
    return n, r, n * gain_col


def mix_out_fwd(x, ya_t, yg_t, gao, ggo, wout, b_out, name):
    t, d = x.shape
    tm = _token_tile(t)

    def body(x_ref, ya_ref, yg_ref, gao_ref, ggo_ref, w_ref, b_ref, o_ref):
        _, _, ya = _rmsnorm_cols(ya_ref[...].astype(F32), gao_ref[...])
        _, _, yg = _rmsnorm_cols(yg_ref[...].astype(F32), ggo_ref[...])
        o_ref[...] = (x_ref[...] + _dg(ya.astype(BF16), w_ref[:ATTN_WIDTH, :], TN)
                      + _dg(yg.astype(BF16), w_ref[ATTN_WIDTH:, :], TN) + b_ref[...])

    return pl.pallas_call(
        body, name=name, grid=(t // tm,),
        in_specs=[pl.BlockSpec((tm, d), lambda i: (i, 0)), pl.BlockSpec((ATTN_WIDTH, tm), lambda i: (0, i)),
                  pl.BlockSpec((GMLP_WIDTH, tm), lambda i: (0, i)), _resident((ATTN_WIDTH, 1)), _resident((GMLP_WIDTH, 1)),
                  _resident((ATTN_WIDTH + GMLP_WIDTH, d)), _resident((1, d))],
        out_specs=pl.BlockSpec((tm, d), lambda i: (i, 0)),
        out_shape=jax.ShapeDtypeStruct((t, d), F32),
        compiler_params=_params(("arbitrary",)),
    )(x, ya_t, yg_t, gao, ggo, wout, b_out)


def _lane_fold(v):
    out = v[:, :LANES]
    for j in range(1, v.shape[1] // LANES):
        out = out + v[:, j * LANES:(j + 1) * LANES]
    return out


def mix_out_bwd(dx, ya_t, yg_t, gao, ggo, wout, name, deps=()):
    t, d = dx.shape
    tm = _token_tile(t)
    nt = t // tm

    def body(dx_ref, ya_ref, yg_ref, gao_ref, ggo_ref, w_ref, dya_ref, dyg_ref, y_ref, dgao_ref, dggo_ref, db_ref,
             acc_ref):
        i = pl.program_id(0)

        @pl.when(i == 0)
        def _():
            acc_ref[...] = jnp.zeros_like(acc_ref)
            db_ref[...] = jnp.zeros_like(db_ref)

        dxv = dx_ref[...]
        db_ref[...] += jnp.sum(dxv, axis=0, keepdims=True)
        dxb = dxv.astype(BF16)
        for part, (src_ref, gain_ref, dst_ref) in enumerate(((ya_ref, gao_ref, dya_ref), (yg_ref, ggo_ref, dyg_ref))):
            rows = slice(part * ATTN_WIDTH, (part + 1) * ATTN_WIDTH)
            gain = gain_ref[...]
            nrm, r, yn = _rmsnorm_cols(src_ref[...].astype(F32), gain)
            y_ref[rows, :] = yn.astype(BF16)
            dy = _dg(w_ref[rows, :], dxb, NT)
            acc_ref[rows, :] += _lane_fold(dy * nrm)
            dn = dy * gain
            dst_ref[...] = (r * (dn - nrm * jnp.mean(dn * nrm, axis=0, keepdims=True))).astype(BF16)

        @pl.when(i == nt - 1)
        def _():
            dgao_ref[...] = jnp.sum(acc_ref[:ATTN_WIDTH, :], axis=1, keepdims=True)
            dggo_ref[...] = jnp.sum(acc_ref[ATTN_WIDTH:, :], axis=1, keepdims=True)

    const2 = lambda i: (0, 0)
    feat = lambda w: pl.BlockSpec((w, tm), lambda i: (0, i))
    return pl.pallas_call(
        _drop_deps(body, 6, len(deps)), name=name, grid=(nt,),
        in_specs=[pl.BlockSpec((tm, d), lambda i: (i, 0)), feat(ATTN_WIDTH), feat(GMLP_WIDTH),
                  _resident((ATTN_WIDTH, 1)), _resident((GMLP_WIDTH, 1)), _resident((ATTN_WIDTH + GMLP_WIDTH, d))]
        + [ANY_SPEC] * len(deps),
        out_specs=[feat(ATTN_WIDTH), feat(GMLP_WIDTH), feat(ATTN_WIDTH + GMLP_WIDTH),
                   pl.BlockSpec((ATTN_WIDTH, 1), const2), pl.BlockSpec((GMLP_WIDTH, 1), const2),
                   pl.BlockSpec((1, d), const2)],
        out_shape=[jax.ShapeDtypeStruct((ATTN_WIDTH, t), BF16), jax.ShapeDtypeStruct((GMLP_WIDTH, t), BF16),
                   jax.ShapeDtypeStruct((ATTN_WIDTH + GMLP_WIDTH, t), BF16),
                   jax.ShapeDtypeStruct((ATTN_WIDTH, 1), F32), jax.ShapeDtypeStruct((GMLP_WIDTH, 1), F32),
                   jax.ShapeDtypeStruct((1, d), F32)],
        scratch_shapes=[pltpu.VMEM((ATTN_WIDTH + GMLP_WIDTH, LANES), F32)],
        compiler_params=_params(("arbitrary",)),
    )(dx, ya_t, yg_t, gao, ggo, wout, *deps)


def mix_in_bwd(x, dxo, gain, dq_t, dk_t, dv_t, dzg_t, win_t, name):
    t, d = x.shape
    w = win_t.shape[0]
    tm = _token_tile(t)
    nt = t // tm
    parts = ((0, ATTN_WIDTH), (ATTN_WIDTH, KV_WIDTH), (ATTN_WIDTH + KV_WIDTH, KV_WIDTH), (QKV_WIDTH, w - QKV_WIDTH))

    def body(x_ref, dxo_ref, g_ref, dq_ref, dk_ref, dv_ref, dzg_ref, w_ref, dx_ref, h_ref, dg_ref, db_ref, acc_ref):
        i = pl.program_id(0)

        @pl.when(i == 0)
        def _():
            acc_ref[...] = jnp.zeros_like(acc_ref)
            dg_ref[...] = jnp.zeros_like(dg_ref)

        xv = x_ref[...]
        gain_v = g_ref[...]
        r = lax.rsqrt(jnp.mean(xv * xv, axis=-1, keepdims=True) + EPS)
        h_ref[...] = (xv * r * gain_v).astype(BF16)
        dh = jnp.zeros((tm, d), F32)
        for (off, size), src_ref in zip(parts, (dq_ref, dk_ref, dv_ref, dzg_ref)):
            dp = src_ref[...]
            acc_ref[off:off + size, :] += _lane_fold(dp.astype(F32))
            dh = dh + _dg(dp, w_ref[off:off + size, :], TN)
        dx, dgain = _rmsnorm_bwd_rows(xv, r, gain_v, dh)
        dx_ref[...] = dxo_ref[...] + dx
        dg_ref[...] += dgain

        @pl.when(i == nt - 1)
        def _():
            db_ref[...] = jnp.sum(acc_ref[...], axis=1, keepdims=True)

    const2 = lambda i: (0, 0)
    tok = lambda: pl.BlockSpec((tm, d), lambda i: (i, 0))
    feat = lambda rows: pl.BlockSpec((rows, tm), lambda i: (0, i))
    return pl.pallas_call(
        body, name=name, grid=(nt,),
        in_specs=[tok(), tok(), _resident((1, d)), feat(ATTN_WIDTH), feat(KV_WIDTH), feat(KV_WIDTH),
                  feat(w - QKV_WIDTH), _resident((w, d))],
        out_specs=[tok(), tok(), pl.BlockSpec((1, d), const2), pl.BlockSpec((w, 1), const2)],
        out_shape=[jax.ShapeDtypeStruct((t, d), F32), jax.ShapeDtypeStruct((t, d), BF16),
                   jax.ShapeDtypeStruct((1, d), F32), jax.ShapeDtypeStruct((w, 1), F32)],
        scratch_shapes=[pltpu.VMEM((w, LANES), F32)],
        compiler_params=_params(("arbitrary",)),
    )(x, dxo, gain, dq_t, dk_t, dv_t, dzg_t, win_t)


def final_bwd(x, target, gain, name):
    t, d = x.shape
    tm = _token_tile(t)

    def body(x_ref, t_ref, g_ref, dx_ref, dg_ref, loss_ref):
        i = pl.program_id(0)

        @pl.when(i == 0)
        def _():
            dg_ref[...] = jnp.zeros_like(dg_ref)
            loss_ref[...] = jnp.zeros_like(loss_ref)

        xv = x_ref[...]
        gain_v = g_ref[...]
        r = lax.rsqrt(jnp.mean(xv * xv, axis=-1, keepdims=True) + EPS)
        err = xv * r * gain_v - t_ref[...]
        row_loss = jnp.mean(err * err, axis=-1, keepdims=True)
        loss_ref[...] += 0.5 * jnp.sum(row_loss, axis=0, keepdims=True)
        dx, dgain = _rmsnorm_bwd_rows(xv, r, gain_v, err * (1.0 / d))
        dx_ref[...] = dx
        dg_ref[...] += dgain

    const2 = lambda i: (0, 0)
    return pl.pallas_call(
        body, name=name, grid=(t // tm,),
        in_specs=[pl.BlockSpec((tm, d), lambda i: (i, 0)), pl.BlockSpec((tm, d), lambda i: (i, 0)), _resident((1, d))],
        out_specs=[pl.BlockSpec((tm, d), lambda i: (i, 0)), pl.BlockSpec((1, d), const2), pl.BlockSpec((1, LANES), const2)],
        out_shape=[jax.ShapeDtypeStruct((t, d), F32), jax.ShapeDtypeStruct((1, d), F32),
                   jax.ShapeDtypeStruct((1, LANES), F32)],
        compiler_params=_params(("arbitrary",)),
    )(x, target, gain)


def _adamw_math(w, g, m, v):
    m = ADAM_B1 * m + (1.0 - ADAM_B1) * g
    v = ADAM_B2 * v + (1.0 - ADAM_B2) * (g * g)
    m_hat = m / (1.0 - ADAM_B1 ** ADAM_STEP)
    v_hat = v / (1.0 - ADAM_B2 ** ADAM_STEP)
    delta = -ADAM_LR * (m_hat / (jnp.sqrt(v_hat) + ADAM_EPS) + ADAM_WD * w)
    return delta, m, v


def adamw(w, g, m, v, name):
    rows, cols = w.shape
    tr = rows
    if rows * cols > 256 * 1024:
        for cand in (256, 128, 64, 32, 16, 8):
            if rows % cand == 0:
                tr = cand
                break

    def body(w_ref, g_ref, m_ref, v_ref, d_ref, mo_ref, vo_ref):
        delta, mn, vn = _adamw_math(w_ref[...], g_ref[...], m_ref[...], v_ref[...])
        d_ref[...] = delta
        mo_ref[...] = mn
        vo_ref[...] = vn

    spec = pl.BlockSpec((tr, cols), lambda i: (i, 0))
    shape = jax.ShapeDtypeStruct((rows, cols), F32)
    return pl.pallas_call(
        body, name=name, grid=(rows // tr,),
        in_specs=[spec] * 4, out_specs=[spec] * 3, out_shape=[shape] * 3,
        compiler_params=_params(("arbitrary",)),
    )(w, g, m, v)


def sum_partials(landing, grad3d, me, name, deps=()):
    nd, rows, cols = landing.shape
    tr = rows
    for cand in (128, 112, 88, 64, 32, 16):
        if rows % cand == 0:
            tr = cand
            break

    def body(me_ref, l_ref, own_ref, o_ref):
        acc = own_ref[...].astype(F32)
        for j in range(nd):
            acc = acc + l_ref[j].astype(F32)
        o_ref[...] = acc

    return pl.pallas_call(
        _drop_deps(body, 3, len(deps)), name=name,
        grid_spec=pltpu.PrefetchScalarGridSpec(
            num_scalar_prefetch=1, grid=(rows // tr,),
            in_specs=[pl.BlockSpec((nd, tr, cols), lambda i, me_ref: (0, i, 0)),
                      pl.BlockSpec((None, tr, cols), lambda i, me_ref: (me_ref[0], i, 0))] + [ANY_SPEC] * len(deps),
            out_specs=pl.BlockSpec((tr, cols), lambda i, me_ref: (i, 0))),
        out_shape=jax.ShapeDtypeStruct((rows, cols), F32),
        compiler_params=_params(("arbitrary",)),
    )(me, landing, grad3d, *deps)


def _mesh_place():
    x, y, c = lax.axis_index("x"), lax.axis_index("y"), lax.axis_index("c")
    return x, y, c, 4 * x + 2 * y + c


def _peer(x, y, c, k):
    return (x ^ ((k >> 2) & 1), y ^ ((k >> 1) & 1), c ^ (k & 1))


def prep_shards(mats, me, name):
    na = len(mats)

    def body(me_ref, *refs):
        in_refs, shard_refs, land_refs = refs[:na], refs[na:2 * na], refs[2 * na:]
        for a in range(na):
            val = in_refs[a][...].astype(BF16)
            shard_refs[a][...] = val
            land_refs[a][...] = val

    whole = lambda mat: pl.BlockSpec(mat.shape, lambda i, me_ref: (0, 0))
    outs = pl.pallas_call(
        body, name=name,
        grid_spec=pltpu.PrefetchScalarGridSpec(
            num_scalar_prefetch=1, grid=(1,),
            in_specs=[whole(mat) for mat in mats],
            out_specs=[whole(mat) for mat in mats]
            + [pl.BlockSpec((None,) + mat.shape, lambda i, me_ref: (me_ref[0], 0, 0)) for mat in mats]),
        out_shape=[jax.ShapeDtypeStruct(mat.shape, BF16) for mat in mats]
        + [jax.ShapeDtypeStruct((N_DEV,) + mat.shape, BF16) for mat in mats],
        compiler_params=_params(("arbitrary",)),
    )(me, *mats)
    return outs[:na], outs[na:]


def allgather_rows(shards, lands, name):
    na = n_gather = len(shards)

    def body(*refs):
        in_refs, out_refs = refs[:na], refs[2 * na:3 * na]
        send_sems, recv_sems = refs[3 * na:]
        x, y, c, me = _mesh_place()

        def copy(sem, a, origin, src, k):
            return pltpu.make_async_remote_copy(
                src_ref=src, dst_ref=out_refs[a].at[origin], send_sem=send_sems.at[sem, a],
                recv_sem=recv_sems.at[sem, a], device_id=_peer(x, y, c, k), device_id_type=pl.DeviceIdType.MESH)

        kx, ky = 4, 2

        def recv(sem, a, origin):
            copy(sem, a, origin, in_refs[a], 1).wait_recv()

        def landed(a, origin):
            return out_refs[a].at[origin]

        @pl.when(c == 1)
        def _():
            sends = []

            def send(sem, a, origin, src, k):
                cp = copy(sem, a, origin, src, k)
                cp.start()
                sends.append(cp)

            for a in range(n_gather):
                send(1, a, me, in_refs[a], kx)
                send(2, a, me, in_refs[a], ky)
                send(0, a, me, in_refs[a], 1)
            for a in range(n_gather):
                recv(0, a, me ^ 1)
                send(3, a, me ^ 1, landed(a, me ^ 1), kx)
                send(4, a, me ^ 1, landed(a, me ^ 1), ky)
            for a in range(n_gather):
                recv(1, a, me ^ kx)
                send(5, a, me ^ kx, landed(a, me ^ kx), ky)
                send(7, a, me ^ kx, landed(a, me ^ kx), 1)
                recv(2, a, me ^ ky)
                send(8, a, me ^ ky, landed(a, me ^ ky), 1)
            for a in range(n_gather):
                recv(3, a, me ^ kx ^ 1)
                send(9, a, me ^ kx ^ 1, landed(a, me ^ kx ^ 1), 1)
                recv(4, a, me ^ ky ^ 1)
                send(6, a, me ^ ky ^ 1, landed(a, me ^ ky ^ 1), kx)
                send(10, a, me ^ ky ^ 1, landed(a, me ^ ky ^ 1), 1)
            for a in range(n_gather):
                recv(5, a, me ^ 6)
                send(11, a, me ^ 6, landed(a, me ^ 6), 1)
                recv(6, a, me ^ 7)
                send(12, a, me ^ 7, landed(a, me ^ 7), 1)
            for cp in sends:
                cp.wait_send()

        @pl.when(c == 0)
        def _():
            north = me ^ 1
            own = [copy(0, a, me, in_refs[a], 1) for a in range(n_gather)]
            for cp in own:
                cp.start()
            for a in range(n_gather):
                recv(0, a, north)
                for sem, origin in ((7, north ^ kx), (8, north ^ ky), (9, north ^ kx ^ 1), (10, north ^ ky ^ 1),
                                    (11, north ^ 6), (12, north ^ 7)):
                    recv(sem, a, origin)
            for cp in own:
                cp.wait_send()

    return pl.pallas_call(
        body, name=name,
        in_specs=[ANY_SPEC] * (2 * na), out_specs=[ANY_SPEC] * na,
        out_shape=[jax.ShapeDtypeStruct(land.shape, land.dtype) for land in lands],
        input_output_aliases={na + a: a for a in range(na)},
        scratch_shapes=[pltpu.SemaphoreType.DMA((13, n_gather)), pltpu.SemaphoreType.DMA((13, n_gather))],
    )(*shards, *lands)


HBM_SPEC = pl.BlockSpec(memory_space=pltpu.HBM)
SEM_SPEC = pl.BlockSpec(memory_space=pltpu.SEMAPHORE)
SIDE_EFFECT = pltpu.SideEffectType.DATAFLOW_SIDE_EFFECTING


def _hbm(v):
    return pltpu.with_memory_space_constraint(v, pltpu.HBM)


def gather_slices(src, land, me, k):
    return src, land.at[me], land.at[me ^ k]


def exchange_slices(src, land, me, k):
    return src.at[me ^ k], land.at[k - 1], land.at[k - 1]


def split_start(groups, slices, deps, name):
    sizes = [len(srcs) for srcs, _ in groups]
    flat_src = [s for srcs, _ in groups for s in srcs]
    flat_land = [l for _, lands in groups for l in lands]
    na, ng, nd = len(flat_src), len(groups), len(deps)

    def body(*refs):
        src_refs, land_refs = refs[:na], refs[na:2 * na]
        sem_refs = refs[2 * na + nd:2 * na + nd + 2 * ng]
        token_ref = refs[-1]
        x, y, c, me = _mesh_place()
        a0 = 0
        for gi, size in enumerate(sizes):
            send_sems, recv_sems = sem_refs[2 * gi], sem_refs[2 * gi + 1]
            for k in range(1, N_DEV):
                for j in range(size):
                    src, dst, _ = slices(src_refs[a0 + j], land_refs[a0 + j], me, k)
                    pltpu.make_async_remote_copy(
                        src_ref=src, dst_ref=dst, send_sem=send_sems.at[(k - 1) * size + j],
                        recv_sem=recv_sems.at[(k - 1) * size + j],
                        device_id=_peer(x, y, c, k), device_id_type=pl.DeviceIdType.MESH).start()
            a0 += size
        token_ref[...] = jnp.zeros_like(token_ref)

    sems = [pltpu.SemaphoreType.DMA(((N_DEV - 1) * size,)) for size in sizes for _ in range(2)]
    thru = [pltpu.HBM(v.shape, v.dtype) for v in flat_src + flat_land]
    outs = pl.pallas_call(
        body, name=name,
        in_specs=[HBM_SPEC] * (2 * na) + [ANY_SPEC] * nd,
        out_specs=[SEM_SPEC] * (2 * ng) + [HBM_SPEC] * (2 * na) + [pl.BlockSpec(memory_space=pltpu.VMEM)],
        out_shape=sems + thru + [jax.ShapeDtypeStruct((8, LANES), F32)],
        input_output_aliases={i: 2 * ng + i for i in range(2 * na)},
        compiler_params=pltpu.CompilerParams(has_side_effects=SIDE_EFFECT),
    )(*[_hbm(v) for v in flat_src + flat_land], *deps)
    started, a0 = [], 0
    for gi, size in enumerate(sizes):
        srcs = outs[2 * ng + a0:2 * ng + a0 + size]
        lands = outs[2 * ng + na + a0:2 * ng + na + a0 + size]
        started.append((outs[2 * gi], outs[2 * gi + 1], list(srcs), list(lands)))
        a0 += size
    return started, outs[-1]


def split_wait(started, slices, after, name):
    send_sems, recv_sems, srcs, lands = started
    na = len(srcs)

    def body(*refs):
        src_refs, land_refs = refs[:na], refs[na:2 * na]
        send_ref, recv_ref = refs[2 * na], refs[2 * na + 1]
        x, y, c, me = _mesh_place()
        for k in range(1, N_DEV):
            for j in range(na):
                src, _, got = slices(src_refs[j], land_refs[j], me, k)
                cp = pltpu.make_async_remote_copy(
                    src_ref=src, dst_ref=got, send_sem=send_ref.at[(k - 1) * na + j], recv_sem=recv_ref.at[(k - 1) * na + j],
                    device_id=_peer(x, y, c, k), device_id_type=pl.DeviceIdType.MESH)
                cp.wait_send()
                cp.wait_recv()

    outs = pl.pallas_call(
        body, name=name,
        in_specs=[HBM_SPEC] * (2 * na) + [SEM_SPEC, SEM_SPEC] + [ANY_SPEC] * len(after),
        out_specs=[HBM_SPEC] * (2 * na),
        out_shape=[pltpu.HBM(v.shape, v.dtype) for v in srcs + lands],
        input_output_aliases={i: i for i in range(2 * na)},
        compiler_params=pltpu.CompilerParams(has_side_effects=SIDE_EFFECT),
    )(*srcs, *lands, send_sems, recv_sems, *after)
    return list(outs[:na]), list(outs[na:])


def place_own(pack, me, name):
    rows, cols = pack.shape

    def body(me_ref, p_ref, o_ref):
        o_ref[...] = p_ref[...]

    return pl.pallas_call(
        body, name=name,
        grid_spec=pltpu.PrefetchScalarGridSpec(
            num_scalar_prefetch=1, grid=(1,),
            in_specs=[pl.BlockSpec((rows, cols), lambda i, me_ref: (0, 0))],
            out_specs=pl.BlockSpec((None, rows, cols), lambda i, me_ref: (me_ref[0], 0, 0))),
        out_shape=jax.ShapeDtypeStruct((N_DEV, rows, cols), F32),
        compiler_params=_params(("arbitrary",)),
    )(me, pack)


def sum_slots(slots, name):
    nd, rows, cols = slots.shape

    def body(s_ref, o_ref):
        acc = s_ref[0]
        for j in range(1, nd):
            acc = acc + s_ref[j]
        o_ref[...] = acc

    return pl.pallas_call(
        body, name=name, grid=(1,),
        in_specs=[pl.BlockSpec((nd, rows, cols), lambda i: (0, 0, 0))],
        out_specs=pl.BlockSpec((rows, cols), lambda i: (0, 0)),
        out_shape=jax.ShapeDtypeStruct((rows, cols), F32),
        compiler_params=_params(("arbitrary",)),
    )(slots)


def _pack_rows(size):
    rows = -(-size // LANES)
    return -(-rows // 8) * 8


def pack_small(parts):
    out = []
    for p in parts:
        flat = p.reshape(-1).astype(F32)
        rows = _pack_rows(flat.shape[0])
        out.append(jnp.pad(flat, (0, rows * LANES - flat.shape[0])).reshape(rows, LANES))
    return jnp.concatenate(out, axis=0)


def unpack_small(pack, shapes):
    out, off = [], 0
    for shp in shapes:
        size = 1
        for s in shp:
            size *= s
        rows = _pack_rows(size)
        out.append(pack[off:off + rows].reshape(-1)[:size].reshape(shp))
        off += rows
    return out


def local_step(x, target, small, get_w, put_g, put_small):
    col = lambda v: v.reshape(-1, 1)
    (wg1, wu1, wd1), deps = get_w(("wg1", "wu1", "wd1"), ())
    x1, sil1, gp1, s1 = ffn_fwd(x, small["ffn1_norm_g"], wg1, wu1, wd1, "ffn1_fwd", deps)
    (win, wout), _ = get_w(("win", "wout"), (x1,))
    qkv_t, zg_t = mix_in_fwd(x1, small["mix_norm_g"], win, col(small["b_in"]), "mix_in_fwd")
    ya_t = attn_fwd(qkv_t, small["attn_sinks"], "attn_fwd")
    lng, lnb = col(small["gmlp_ln_g"]), col(small["gmlp_ln_b"])
    w_s, b_s = small["gmlp_w_s"][0], small["gmlp_b_s"][0]
    yg_t = gmlp_fwd(zg_t, lng, lnb, w_s, b_s, "gmlp_fwd")
    gao, ggo = col(small["attn_out_norm_g"]), col(small["gmlp_out_norm_g"])
    x2 = mix_out_fwd(x1, ya_t, yg_t, gao, ggo, wout, small["b_out"], "mix_out_fwd")
    (wg2, wu2, wd2), _ = get_w(("wg2", "wu2", "wd2"), (x2,))
    x3, sil2, gp2, s2 = ffn_fwd(x2, small["ffn2_norm_g"], wg2, wu2, wd2, "ffn2_fwd")

    gs = {}
    dx3, gs["final_norm_g"], loss = final_bwd(x3, target, small["final_norm_g"].reshape(1, -1), "final_bwd")

    dx2, da, db, h, gs["ffn2_norm_g"] = ffn_bwd(x2, dx3, small["ffn2_norm_g"], sil2, gp2, wg2, wu2, wd2, "ffn2_bwd")
    deps = put_g({"wg2": grad_tn(da, h, 1.0, "ffn2_dwg"), "wu2": grad_tn(db, h, 1.0, "ffn2_dwu"),
                  "wd2": grad_tn(s2, dx3, FFN_RES, "ffn2_dwd")})

    dya_t, dyg_t, y_t, dgao, dggo, gs["b_out"] = mix_out_bwd(dx2, ya_t, yg_t, gao, ggo, wout, "mix_out_bwd", deps)
    gs["attn_out_norm_g"], gs["gmlp_out_norm_g"] = dgao.reshape(1, -1), dggo.reshape(1, -1)
    g_wout = grad_nn([y_t], dx2, "dwout")
    dzg_t, dws, dbs, dlng, dlnb = gmlp_bwd(zg_t, dyg_t, lng, lnb, w_s, b_s, "gmlp_bwd")
    gs["gmlp_w_s"], gs["gmlp_b_s"] = dws[None], dbs[None]
    gs["gmlp_ln_g"], gs["gmlp_ln_b"] = dlng.reshape(1, -1), dlnb.reshape(1, -1)
    dq_t, dk_t, dv_t, dsinks = attn_bwd(qkv_t, dya_t, small["attn_sinks"], "attn_bwd")
    gs["attn_sinks"] = dsinks.reshape(1, -1)
    dx1, h2, gs["mix_norm_g"], dbin = mix_in_bwd(x1, dx2, small["mix_norm_g"], dq_t, dk_t, dv_t, dzg_t, win,
                                                 "mix_in_bwd")
    gs["b_in"] = dbin.reshape(1, -1)
    deps = put_g({"wout": g_wout, "win": grad_nn([dq_t, dk_t, dv_t, dzg_t], h2, "dwin")})

    dx0, da, db, h, gs["ffn1_norm_g"] = ffn_bwd(x, dx1, small["ffn1_norm_g"], sil1, gp1, wg1, wu1, wd1, "ffn1_bwd",
                                                deps)
    deps = put_small(gs, loss)
    deps = put_g({"wg1": grad_tn(da, h, 1.0, "ffn1_dwg", deps)})
    deps = put_g({"wu1": grad_tn(db, h, 1.0, "ffn1_dwu", deps)})
    put_g({"wd1": grad_tn(s1, dx1, FFN_RES, "ffn1_dwd", deps)})
    return dx0


SMALL_NAMES = ["ffn1_norm_g", "mix_norm_g", "b_in", "attn_sinks", "gmlp_ln_g", "gmlp_ln_b", "gmlp_w_s", "gmlp_b_s",
               "attn_out_norm_g", "gmlp_out_norm_g", "b_out", "ffn2_norm_g", "final_norm_g"]
BIG_NAMES = {"wg1": ("ffn1_w_gate", True), "wu1": ("ffn1_w_up", True), "wd1": ("ffn1_w_down", False),
             "win": ("w_in", True), "wout": ("w_out", False),
             "wg2": ("ffn2_w_gate", True), "wu2": ("ffn2_w_up", True), "wd2": ("ffn2_w_down", False)}
WEIGHT_ORDER = ["ffn1_norm_g", "ffn1_w_gate", "ffn1_w_up", "ffn1_w_down", "mix_norm_g", "w_in", "b_in", "attn_sinks",
                "gmlp_ln_g", "gmlp_ln_b", "gmlp_w_s", "gmlp_b_s", "attn_out_norm_g", "gmlp_out_norm_g", "w_out",
                "b_out", "ffn2_norm_g", "ffn2_w_gate", "ffn2_w_up", "ffn2_w_down", "final_norm_g"]


def kernel(x, ffn1_norm_g, ffn1_w_gate, ffn1_w_up, ffn1_w_down, mix_norm_g, w_in, b_in, attn_sinks, gmlp_ln_g, gmlp_ln_b, gmlp_w_s, gmlp_b_s, attn_out_norm_g, gmlp_out_norm_g, w_out, b_out, ffn2_norm_g, ffn2_w_gate, ffn2_w_up, ffn2_w_down, final_norm_g, loss_target, m_ffn1_norm_g, m_ffn1_w_gate, m_ffn1_w_up, m_ffn1_w_down, m_mix_norm_g, m_w_in, m_b_in, m_attn_sinks, m_gmlp_ln_g, m_gmlp_ln_b, m_gmlp_w_s, m_gmlp_b_s, m_attn_out_norm_g, m_gmlp_out_norm_g, m_w_out, m_b_out, m_ffn2_norm_g, m_ffn2_w_gate, m_ffn2_w_up, m_ffn2_w_down, m_final_norm_g, v_ffn1_norm_g, v_ffn1_w_gate, v_ffn1_w_up, v_ffn1_w_down, v_mix_norm_g, v_w_in, v_b_in, v_attn_sinks, v_gmlp_ln_g, v_gmlp_ln_b, v_gmlp_w_s, v_gmlp_b_s, v_attn_out_norm_g, v_gmlp_out_norm_g, v_w_out, v_b_out, v_ffn2_norm_g, v_ffn2_w_gate, v_ffn2_w_up, v_ffn2_w_down, v_final_norm_g):
    args = dict(locals())
    w = {n: args[n] for n in WEIGHT_ORDER}
    m = {n: args["m_" + n] for n in WEIGHT_ORDER}
    v = {n: args["v_" + n] for n in WEIGHT_ORDER}

    d_model = x.shape[-1]
    flat = lambda g: g.reshape(-1, d_model)
    me = _mesh_place()[3].astype(jnp.int32).reshape(1)
    first = ("wg1", "wu1", "wd1")
    later = (("win", "wout"), ("wg2", "wu2", "wd2"))
    gathers, exchanges, last_token = {}, [], []

    order = list(first) + [k for grp in later for k in grp]
    mats = [w[BIG_NAMES[k][0]][0].T if BIG_NAMES[k][1] else w[BIG_NAMES[k][0]][0] for k in order]
    shard_list, land_list = prep_shards(mats, me, "prep_shards")
    shards, zones = dict(zip(order, shard_list)), dict(zip(order, land_list))

    def get_w(keys, after):
        if keys == first:
            full = allgather_rows([shards[k] for k in first], [zones[k] for k in first], "allgather_ffn1")
            started, token = split_start([([shards[k] for k in grp], [zones[k] for k in grp]) for grp in later],
                                         gather_slices, (full[0],), "gather_start")
            gathers.update(zip(later, started))
            return [flat(g) for g in full], (token,)
        _, lands = split_wait(gathers[keys], gather_slices, after, "gather_wait_" + keys[0])
        return [flat(land) for land in lands], ()

    def put_g(gb):
        keys = tuple(gb)
        g3 = [gb[k].reshape(N_DEV, -1, d_model) for k in keys]
        lands = [lax.empty((N_DEV - 1,) + g.shape[1:], BF16) for g in g3]
        started, token = split_start([(g3, lands)], exchange_slices, (), "exchange_start_" + keys[0])
        exchanges.append((keys, started[0]))
        last_token[:] = [token]
        return (token,)

    small_started = []

    def put_small(gs, loss):
        pack = pack_small([gs[n] for n in SMALL_NAMES] + [loss[:, :1]])
        started, token = split_start([([pack], [place_own(pack, me, "place_small")])], gather_slices, (), "small_start")
        small_started[:] = started
        return (token,)

    small = {n: w[n] for n in SMALL_NAMES}
    grad_x = local_step(x[0], loss_target[0], small, get_w, put_g, put_small)

    grads, deltas, new_m, new_v = {}, {}, {}, {}
    after = tuple(last_token)
    for keys, started in exchanges:
        g3, lands = split_wait(started, exchange_slices, after, "exchange_wait_" + keys[0])
        for key, own, land in zip(keys, g3, lands):
            pname, transposed = BIG_NAMES[key]
            g = sum_partials(land, own, me, "sum_" + key)
            after = (g,)
            to_rows = (lambda p: p[0].T) if transposed else (lambda p: p[0])
            from_rows = (lambda r: r.T[None]) if transposed else (lambda r: r[None])
            d_, m_, v_ = adamw(to_rows(w[pname]), g, to_rows(m[pname]), to_rows(v[pname]), "adamw_" + key)
            grads[pname], deltas[pname], new_m[pname], new_v[pname] = (from_rows(g), from_rows(d_), from_rows(m_),
                                                                        from_rows(v_))

    shapes = [w[n].shape for n in SMALL_NAMES]
    _, (slots,) = split_wait(small_started[0], gather_slices, after, "small_wait")
    total = sum_slots(slots, "sum_small")
    small_g = unpack_small(total, shapes + [(1, 1)])
    loss_total = small_g[-1].reshape(())
    gpack = total[:sum(_pack_rows(w[n].size) for n in SMALL_NAMES)]
    d_, m_, v_ = adamw(pack_small([w[n] for n in SMALL_NAMES]), gpack, pack_small([m[n] for n in SMALL_NAMES]),
                       pack_small([v[n] for n in SMALL_NAMES]), "adamw_small")
    for n, g_, dd, mm, vv in zip(SMALL_NAMES, small_g[:-1], unpack_small(d_, shapes), unpack_small(m_, shapes),
                                 unpack_small(v_, shapes)):
        grads[n], deltas[n], new_m[n], new_v[n] = g_, dd, mm, vv

    return (loss_total, grad_x[None], *[grads[n] for n in WEIGHT_ORDER], *[deltas[n] for n in WEIGHT_ORDER],
            *[new_m[n] for n in WEIGHT_ORDER], *[new_v[n] for n in WEIGHT_ORDER])
```

```python
import functools

import jax
import jax.numpy as jnp
from jax import lax
from jax.experimental import pallas as pl
from jax.experimental.pallas import tpu as pltpu

F32 = jnp.float32
BF16 = jnp.bfloat16

N_DEV = 8
N_Q_HEADS = 8
N_KV_HEADS = 2
HEAD_DIM = 64
ATTN_WIDTH = N_Q_HEADS * HEAD_DIM
KV_WIDTH = N_KV_HEADS * HEAD_DIM
QKV_WIDTH = ATTN_WIDTH + 2 * KV_WIDTH
BLOCK = 128
GMLP_GROUPS = 8
GMLP_GROUP_DIM = 64
GMLP_WIDTH = GMLP_GROUPS * GMLP_GROUP_DIM
EPS = 1e-6
FFN_RES = 0.5
ATTN_SCALE = HEAD_DIM ** -0.5
MASK_VALUE = -1e30

ADAM_LR = 0.001
ADAM_B1 = 0.9
ADAM_B2 = 0.999
ADAM_EPS = 1e-08
ADAM_WD = 0.01
ADAM_STEP = 10

V7X_VMEM_BYTES = 64 * 1024 * 1024
VMEM_LIMIT = 62 * 1024 * 1024
LANES = 128
MXU_WIDTH = 256

NT = (((1,), (1,)), ((), ()))
TN = (((0,), (0,)), ((), ()))


def _dot(a, b):
    return jnp.dot(a, b, preferred_element_type=F32)


def _dg(a, b, dims):
    return lax.dot_general(a, b, dims, preferred_element_type=F32)


def _params(semantics=None):
    return pltpu.CompilerParams(dimension_semantics=semantics, vmem_limit_bytes=VMEM_LIMIT)


def _resident(shape):
    zeros = (0,) * len(shape)
    return pl.BlockSpec(shape, lambda *_: zeros, pipeline_mode=pl.Buffered(1))


def _drop_deps(body, n_in, n_deps):
    return lambda *refs: body(*refs[:n_in], *refs[n_in + n_deps:])


ANY_SPEC = pl.BlockSpec(memory_space=pl.ANY)


def _token_tile(t):
    return min(t, 512)


def _f_chunk(f):
    for c in (256, 128):
        if f % c == 0:
            return c
    return f


def _loss_head(xv, target, gain):
    r = lax.rsqrt(jnp.mean(xv * xv, axis=-1, keepdims=True) + EPS)
    err = xv * r * gain - target
    loss = 0.5 * jnp.sum(jnp.mean(err * err, axis=-1, keepdims=True), axis=0, keepdims=True)
    dx, dgain = _rmsnorm_bwd_rows(xv, r, gain, err * (1.0 / xv.shape[-1]))
    return dx, dgain, loss


def ffn_fwd(x, gain, wg_t, wu_t, wd, name, deps=(), head=None):
    t, d = x.shape
    f = wd.shape[0]
    tm = _token_tile(t)
    fc = _f_chunk(f)
    n_in = 5 if head is None else 7

    def body(x_ref, g_ref, wg_ref, wu_ref, wd_ref, *rest):
        xo_ref, sil_ref, gp_ref, s_ref = rest[n_in - 5:n_in - 1]
        xv = x_ref[...]
        r = lax.rsqrt(jnp.mean(xv * xv, axis=-1, keepdims=True) + EPS)
        h = (xv * r * g_ref[...]).astype(BF16)
        for c in range(f // fc):
            sl = slice(c * fc, (c + 1) * fc)
            a = _dg(h, wg_ref[sl, :], NT)
            b = _dg(h, wu_ref[sl, :], NT)
            sig = jax.nn.sigmoid(a)
            sil = a * sig
            sil_ref[:, sl] = sil.astype(BF16)
            gp_ref[:, sl] = (b * (sig + sil - sil * sig)).astype(BF16)
            s_ref[:, sl] = (sil * b).astype(BF16)
        xo = xv + FFN_RES * _dot(s_ref[...], wd_ref[...])
        if head is None:
            xo_ref[...] = xo
        else:
            t_ref, hg_ref = rest[:2]
            dg_ref, loss_ref = rest[-2:]

            @pl.when(pl.program_id(0) == 0)
            def _():
                dg_ref[...] = jnp.zeros_like(dg_ref)
                loss_ref[...] = jnp.zeros_like(loss_ref)

            dx, dgain, loss = _loss_head(xo, t_ref[...], hg_ref[...])
            xo_ref[...] = dx
            dg_ref[...] += dgain
            loss_ref[...] += loss

    tok = lambda: pl.BlockSpec((tm, d), lambda i: (i, 0))
    wide = lambda: pl.BlockSpec((tm, f), lambda i: (i, 0))
    const2 = lambda i: (0, 0)
    in_specs = [tok(), _resident((1, d)), _resident((f, d)), _resident((f, d)), _resident((f, d))]
    out_specs = [tok(), wide(), wide(), wide()]
    out_shape = [jax.ShapeDtypeStruct((t, d), F32)] + [jax.ShapeDtypeStruct((t, f), BF16)] * 3
    operands = [x, gain, wg_t, wu_t, wd]
    if head is not None:
        in_specs += [tok(), _resident((1, d))]
        out_specs += [pl.BlockSpec((1, d), const2), pl.BlockSpec((1, LANES), const2)]
        out_shape += [jax.ShapeDtypeStruct((1, d), F32), jax.ShapeDtypeStruct((1, LANES), F32)]
        operands += list(head)
    return pl.pallas_call(
        _drop_deps(body, n_in, len(deps)), name=name, grid=(t // tm,),
        in_specs=in_specs + [ANY_SPEC] * len(deps), out_specs=out_specs, out_shape=out_shape,
        compiler_params=_params(("arbitrary",)),
    )(*operands, *deps)


def _rmsnorm_bwd_rows(xv, r, gain, dh):
    n = xv * r
    dgain = jnp.sum(dh * n, axis=0, keepdims=True)
    dn = dh * gain
    dx = r * (dn - n * jnp.mean(dn * n, axis=-1, keepdims=True))
    return dx, dgain


def ffn_bwd(x, dxo, gain, sil, gp, wg_t, wu_t, wd, name, deps=()):
    t, d = x.shape
    f = wd.shape[0]
    tm = _token_tile(t)
    nh = 2 if f % (2 * LANES) == 0 else 1
    fh = f // nh
    chunks = [(off, min(MXU_WIDTH, fh - off)) for off in range(0, fh, MXU_WIDTH)]

    def body(x_ref, dxo_ref, g_ref, sil_ref, gp_ref, wg_ref, wu_ref, wd_ref,
             dx_ref, da_ref, db_ref, h_ref, dg_ref, dh_ref):
        i, j = pl.program_id(0), pl.program_id(1)
        gain_v = g_ref[...]

        @pl.when(j == 0)
        def _():
            xv = x_ref[...]
            r = lax.rsqrt(jnp.mean(xv * xv, axis=-1, keepdims=True) + EPS)
            h_ref[...] = (xv * r * gain_v).astype(BF16)
            dh_ref[...] = jnp.zeros_like(dh_ref)

        df = (FFN_RES * dxo_ref[...]).astype(BF16)
        for off, size in chunks:
            sl = slice(off, off + size)
            rows = pl.ds(pl.multiple_of(j * fh + off, LANES), size)
            ds = _dg(df, wd_ref[rows, :], NT)
            da_ref[:, sl] = (ds * gp_ref[:, sl].astype(F32)).astype(BF16)
            db_ref[:, sl] = (ds * sil_ref[:, sl].astype(F32)).astype(BF16)
        half_rows = pl.ds(pl.multiple_of(j * fh, LANES), fh)
        dh_ref[...] += _dot(da_ref[...], wg_ref[half_rows, :]) + _dot(db_ref[...], wu_ref[half_rows, :])

        @pl.when(j == nh - 1)
        def _():
            xv = x_ref[...]
            r = lax.rsqrt(jnp.mean(xv * xv, axis=-1, keepdims=True) + EPS)
            dx, dgain = _rmsnorm_bwd_rows(xv, r, gain_v, dh_ref[...])
            dx_ref[...] = dxo_ref[...] + dx

            @pl.when(i == 0)
            def _():
                dg_ref[...] = jnp.zeros_like(dg_ref)

            dg_ref[...] += dgain

    tok = lambda: pl.BlockSpec((tm, d), lambda i, j: (i, 0))
    half = lambda: pl.BlockSpec((tm, fh), lambda i, j: (i, j))
    return pl.pallas_call(
        _drop_deps(body, 8, len(deps)), name=name, grid=(t // tm, nh),
        in_specs=[tok(), tok(), _resident((1, d)), half(), half(),
                  _resident((f, d)), _resident((f, d)), _resident((f, d))] + [ANY_SPEC] * len(deps),
        out_specs=[tok(), half(), half(), tok(), pl.BlockSpec((1, d), lambda i, j: (0, 0))],
        out_shape=[jax.ShapeDtypeStruct((t, d), F32), jax.ShapeDtypeStruct((t, f), BF16),
                   jax.ShapeDtypeStruct((t, f), BF16), jax.ShapeDtypeStruct((t, d), BF16),
                   jax.ShapeDtypeStruct((1, d), F32)],
        scratch_shapes=[pltpu.VMEM((tm, d), F32)],
        compiler_params=_params(("arbitrary", "arbitrary")),
    )(x, dxo, gain, sil, gp, wg_t, wu_t, wd, *deps)


def _row_tile(m, cap):
    if m <= cap:
        return m
    best = None
    for bm in range(LANES, cap + 1, LANES):
        if m % bm == 0:
            best = bm
    return best if best is not None else m


def grad_tn(a, b, scale, name, deps=()):
    t, m = a.shape
    n = b.shape[1]
    bm = _row_tile(m, 1408)
    bk = min(t, 1024)
    nk = t // bk

    def body(a_ref, b_ref, o_ref, acc_ref):
        k = pl.program_id(1)

        @pl.when(k == 0)
        def _():
            acc_ref[...] = jnp.zeros_like(acc_ref)

        acc_ref[...] += _dg(a_ref[...].astype(BF16), b_ref[...].astype(BF16), TN)

        @pl.when(k == nk - 1)
        def _():
            o_ref[...] = (scale * acc_ref[...]).astype(BF16)

    return pl.pallas_call(
        _drop_deps(body, 2, len(deps)), name=name, grid=(m // bm, nk),
        in_specs=[pl.BlockSpec((bk, bm), lambda i, k: (k, i)), pl.BlockSpec((bk, n), lambda i, k: (k, 0))]
        + [ANY_SPEC] * len(deps),
        out_specs=pl.BlockSpec((bm, n), lambda i, k: (i, 0)),
        out_shape=jax.ShapeDtypeStruct((m, n), BF16),
        scratch_shapes=[pltpu.VMEM((bm, n), F32)],
        compiler_params=_params(("arbitrary", "arbitrary")),
    )(a, b, *deps)


def grad_nn(a_list, b, name):
    t, n = b.shape
    ms = [a.shape[0] for a in a_list]
    m = sum(ms)
    bk = min(t, 1024)
    nk = t // bk
    na = len(a_list)

    def body(*refs):
        a_refs, b_ref, o_ref, acc_ref = refs[:na], refs[na], refs[na + 1], refs[na + 2]
        k = pl.program_id(0)

        @pl.when(k == 0)
        def _():
            acc_ref[...] = jnp.zeros_like(acc_ref)

        bv = b_ref[...].astype(BF16)
        off = 0
        for a_ref, mi in zip(a_refs, ms):
            acc_ref[off:off + mi, :] += _dot(a_ref[...].astype(BF16), bv)
            off += mi

        @pl.when(k == nk - 1)
        def _():
            o_ref[...] = acc_ref[...].astype(BF16)

    return pl.pallas_call(
        body, name=name, grid=(nk,),
        in_specs=[pl.BlockSpec((mi, bk), lambda k: (0, k)) for mi in ms] + [pl.BlockSpec((bk, n), lambda k: (k, 0))],
        out_specs=pl.BlockSpec((m, n), lambda k: (0, 0)),
        out_shape=jax.ShapeDtypeStruct((m, n), BF16),
        scratch_shapes=[pltpu.VMEM((m, n), F32)],
        compiler_params=_params(("arbitrary",)),
    )(*a_list, b)


def mix_in_fwd(x, gain, win_t, b_in_col, name):
    t, d = x.shape
    w = win_t.shape[0]
    tm = _token_tile(t)

    def body(x_ref, g_ref, w_ref, bias_ref, qkv_ref, zg_ref):
        xv = x_ref[...]
        r = lax.rsqrt(jnp.mean(xv * xv, axis=-1, keepdims=True) + EPS)
        h = (xv * r * g_ref[...]).astype(BF16)
        qkv_ref[...] = (_dg(w_ref[:QKV_WIDTH, :], h, NT) + bias_ref[:QKV_WIDTH, :]).astype(BF16)
        zg_ref[...] = (_dg(w_ref[QKV_WIDTH:, :], h, NT) + bias_ref[QKV_WIDTH:, :]).astype(BF16)

    return pl.pallas_call(
        body, name=name, grid=(t // tm,),
        in_specs=[pl.BlockSpec((tm, d), lambda i: (i, 0)), _resident((1, d)), _resident((w, d)), _resident((w, 1))],
        out_specs=[pl.BlockSpec((QKV_WIDTH, tm), lambda i: (0, i)), pl.BlockSpec((w - QKV_WIDTH, tm), lambda i: (0, i))],
        out_shape=[jax.ShapeDtypeStruct((QKV_WIDTH, t), BF16), jax.ShapeDtypeStruct((w - QKV_WIDTH, t), BF16)],
        compiler_params=_params(("arbitrary",)),
    )(x, gain, win_t, b_in_col)


def _attn_specs(nb):
    last = nb - 1
    cur = lambda n: jnp.minimum(n, last)
    prev = lambda n: jnp.maximum(jnp.minimum(n, last) - 1, 0)
    k_row = ATTN_WIDTH // KV_WIDTH
    return [
        pl.BlockSpec((ATTN_WIDTH, BLOCK), lambda n: (0, cur(n))),
        pl.BlockSpec((KV_WIDTH, BLOCK), lambda n: (k_row, prev(n))),
        pl.BlockSpec((KV_WIDTH, BLOCK), lambda n: (k_row, cur(n))),
        pl.BlockSpec((KV_WIDTH, BLOCK), lambda n: (k_row + 1, prev(n))),
        pl.BlockSpec((KV_WIDTH, BLOCK), lambda n: (k_row + 1, cur(n))),
    ]


ATTN_REP = N_Q_HEADS // N_KV_HEADS


def _group_rows(ref, g):
    first = g * ATTN_REP
    return jnp.concatenate([ref[(first + r) * HEAD_DIM:(first + r + 1) * HEAD_DIM, :] for r in range(ATTN_REP)],
                           axis=1)


def _ungroup_rows(ref, g, val):
    first = g * ATTN_REP
    for r in range(ATTN_REP):
        ref[(first + r) * HEAD_DIM:(first + r + 1) * HEAD_DIM, :] = val[:, r * BLOCK:(r + 1) * BLOCK]


def _attn_upper():
    key = lax.broadcasted_iota(jnp.int32, (BLOCK, ATTN_REP * BLOCK), 0)
    qry = lax.broadcasted_iota(jnp.int32, (BLOCK, ATTN_REP * BLOCK), 1) & (BLOCK - 1)
    return key > qry


def _attn_probs(q, kp, kc, sink_ref, g, upper, n):
    s = jnp.where(upper, _dg(kp, q, TN), _dg(kc, q, TN)) * ATTN_SCALE
    s = jnp.where(upper & (n == 0), MASK_VALUE, s)
    sink = jnp.concatenate([jnp.full((1, BLOCK), sink_ref[0, g * ATTN_REP + r], F32) for r in range(ATTN_REP)], axis=1)
    m = jnp.maximum(jnp.max(s, axis=0, keepdims=True), sink)
    e = jnp.exp(s - m)
    es = jnp.exp(sink - m)
    inv = 1.0 / (jnp.sum(e, axis=0, keepdims=True) + es)
    return e * inv, es * inv


def _split_band(upper, val):
    return jnp.where(upper, val, 0.0).astype(BF16), jnp.where(upper, 0.0, val).astype(BF16)


def attn_fwd(qkv_t, sinks, name):
    t = qkv_t.shape[1]
    nb = t // BLOCK

    def body(sink_ref, q_ref, kp_ref, kc_ref, vp_ref, vc_ref, y_ref):
        n = pl.program_id(0)
        upper = _attn_upper()
        for g in range(N_KV_HEADS):
            gs = slice(g * HEAD_DIM, (g + 1) * HEAD_DIM)
            p, _ = _attn_probs(_group_rows(q_ref, g), kp_ref[gs, :], kc_ref[gs, :], sink_ref, g, upper, n)
            pp, pc = _split_band(upper, p)
            _ungroup_rows(y_ref, g, (_dot(vp_ref[gs, :], pp) + _dot(vc_ref[gs, :], pc)).astype(BF16))

    return pl.pallas_call(
        body, name=name, grid=(nb,),
        in_specs=[pl.BlockSpec(memory_space=pltpu.SMEM)] + _attn_specs(nb),
        out_specs=pl.BlockSpec((ATTN_WIDTH, BLOCK), lambda n: (0, n)),
        out_shape=jax.ShapeDtypeStruct((ATTN_WIDTH, t), BF16),
        compiler_params=_params(("arbitrary",)),
    )(sinks, qkv_t, qkv_t, qkv_t, qkv_t, qkv_t)


def attn_bwd(qkv_t, dy_t, sinks, name):
    t = qkv_t.shape[1]
    nb = t // BLOCK

    def body(sink_ref, q_ref, kp_ref, kc_ref, vp_ref, vc_ref, do_ref, dq_ref, dk_ref, dv_ref, dsink_ref,
             ck_ref, cv_ref, sacc_ref):
        n = pl.program_id(0)

        @pl.when(n == 0)
        def _():
            sacc_ref[...] = jnp.zeros_like(sacc_ref)

        @pl.when(n < nb)
        def _():
            upper = _attn_upper()
            for g in range(N_KV_HEADS):
                gs = slice(g * HEAD_DIM, (g + 1) * HEAD_DIM)
                kp, kc, vp, vc = kp_ref[gs, :], kc_ref[gs, :], vp_ref[gs, :], vc_ref[gs, :]
                q = _group_rows(q_ref, g)
                do = _group_rows(do_ref, g)
                p, ps = _attn_probs(q, kp, kc, sink_ref, g, upper, n)
                dp = jnp.where(upper, _dg(vp, do, TN), _dg(vc, do, TN))
                delta = jnp.sum(p * dp, axis=0, keepdims=True)
                dsink = -(ps * delta)
                for r in range(ATTN_REP):
                    hd = g * ATTN_REP + r
                    sacc_ref[hd:hd + 1, :] += dsink[:, r * BLOCK:(r + 1) * BLOCK]
                dsp, dsc = _split_band(upper, p * (dp - delta))
                pp, pc = _split_band(upper, p)
                _ungroup_rows(dq_ref, g, (ATTN_SCALE * (_dot(kp, dsp) + _dot(kc, dsc))).astype(BF16))

                @pl.when(n > 0)
                def _():
                    dk_ref[gs, :] = (ck_ref[gs, :] + ATTN_SCALE * _dg(q, dsp, NT)).astype(BF16)
                    dv_ref[gs, :] = (cv_ref[gs, :] + _dg(do, pp, NT)).astype(BF16)

                ck_ref[gs, :] = ATTN_SCALE * _dg(q, dsc, NT)
                cv_ref[gs, :] = _dg(do, pc, NT)

        @pl.when(n == nb)
        def _():
            dk_ref[...] = ck_ref[...].astype(BF16)
            dv_ref[...] = cv_ref[...].astype(BF16)
            dsink_ref[...] = jnp.sum(sacc_ref[...], axis=1, keepdims=True)

    last = nb - 1
    done = lambda n: jnp.maximum(n - 1, 0)
    outs = pl.pallas_call(
        body, name=name, grid=(nb + 1,),
        in_specs=[pl.BlockSpec(memory_space=pltpu.SMEM)] + _attn_specs(nb)
        + [pl.BlockSpec((ATTN_WIDTH, BLOCK), lambda n: (0, jnp.minimum(n, last)))],
        out_specs=[pl.BlockSpec((ATTN_WIDTH, BLOCK), lambda n: (0, jnp.minimum(n, last))),
                   pl.BlockSpec((KV_WIDTH, BLOCK), lambda n: (0, done(n))),
                   pl.BlockSpec((KV_WIDTH, BLOCK), lambda n: (0, done(n))),
                   pl.BlockSpec((N_Q_HEADS, 1), lambda n: (0, 0))],
        out_shape=[jax.ShapeDtypeStruct((ATTN_WIDTH, t), BF16), jax.ShapeDtypeStruct((KV_WIDTH, t), BF16),
                   jax.ShapeDtypeStruct((KV_WIDTH, t), BF16), jax.ShapeDtypeStruct((N_Q_HEADS, 1), F32)],
        scratch_shapes=[pltpu.VMEM((KV_WIDTH, BLOCK), F32), pltpu.VMEM((KV_WIDTH, BLOCK), F32),
                        pltpu.VMEM((N_Q_HEADS, BLOCK), F32)],
        compiler_params=_params(("arbitrary",)),
    )(sinks, qkv_t, qkv_t, qkv_t, qkv_t, qkv_t, dy_t)
    return outs


GELU_C = 0.7978845608028654
GELU_A = 0.044715


def _gelu(x):
    return 0.5 * x * (1.0 + jnp.tanh(GELU_C * (x + GELU_A * x * x * x)))


def _gelu_grad(x):
    th = jnp.tanh(GELU_C * (x + GELU_A * x * x * x))
    return 0.5 * (1.0 + th) + 0.5 * x * (1.0 - th * th) * GELU_C * (1.0 + 3.0 * GELU_A * x * x)


def _gmlp_parts(zg, lng, lnb):
    z = _gelu(zg)
    u = z[:GMLP_WIDTH, :]
    vv = z[GMLP_WIDTH:, :]
    mu = jnp.mean(vv, axis=0, keepdims=True)
    xc = vv - mu
    rstd = lax.rsqrt(jnp.mean(xc * xc, axis=0, keepdims=True) + EPS)
    xhat = xc * rstd
    return u, xhat, rstd, xhat * lng + lnb


def _causal(w):
    row = lax.broadcasted_iota(jnp.int32, (BLOCK, BLOCK), 0)
    col = lax.broadcasted_iota(jnp.int32, (BLOCK, BLOCK), 1)
    return jnp.where(col <= row, w, 0.0)


GMLP_CHUNKS_PER_STEP = 4


def _stack_chunks(v, nc):
    return jnp.concatenate([v[:, c * BLOCK:(c + 1) * BLOCK] for c in range(nc)], axis=0)


def _unstack_chunks(v, nc):
    rows = v.shape[0] // nc
    return jnp.concatenate([v[c * rows:(c + 1) * rows, :] for c in range(nc)], axis=1)


def gmlp_fwd(zg_t, lng, lnb, w_s, b_s, name):
    t = zg_t.shape[1]
    nc = min(GMLP_CHUNKS_PER_STEP, t // BLOCK)
    tw = nc * BLOCK

    def body(zg_ref, lng_ref, lnb_ref, w_ref, bs_ref, y_ref):
        u, _, _, vn = _gmlp_parts(zg_ref[...].astype(F32), lng_ref[...], lnb_ref[...])
        for g in range(GMLP_GROUPS):
            gs = slice(g * GMLP_GROUP_DIM, (g + 1) * GMLP_GROUP_DIM)
            wc = _causal(w_ref[g]).astype(BF16)
            mixed = _dg(_stack_chunks(vn[gs, :].astype(BF16), nc), wc, NT) + bs_ref[g:g + 1, :]
            y_ref[gs, :] = (u[gs, :] * _unstack_chunks(mixed, nc)).astype(BF16)

    return pl.pallas_call(
        body, name=name, grid=(t // tw,),
        in_specs=[pl.BlockSpec((2 * GMLP_WIDTH, tw), lambda n: (0, n)), _resident((GMLP_WIDTH, 1)),
                  _resident((GMLP_WIDTH, 1)), _resident((GMLP_GROUPS, BLOCK, BLOCK)), _resident((GMLP_GROUPS, BLOCK))],
        out_specs=pl.BlockSpec((GMLP_WIDTH, tw), lambda n: (0, n)),
        out_shape=jax.ShapeDtypeStruct((GMLP_WIDTH, t), BF16),
        compiler_params=_params(("arbitrary",)),
    )(zg_t, lng, lnb, w_s, b_s)


def gmlp_bwd(zg_t, dy_t, lng, lnb, w_s, b_s, name):
    t = zg_t.shape[1]
    nc = min(GMLP_CHUNKS_PER_STEP, t // BLOCK)
    tw = nc * BLOCK
    nt = t // tw

    def body(zg_ref, dy_ref, lng_ref, lnb_ref, w_ref, bs_ref, dzg_ref, dw_ref, dbs_ref, dlng_ref, dlnb_ref,
             gacc_ref, bacc_ref):
        n = pl.program_id(0)

        @pl.when(n == 0)
        def _():
            dw_ref[...] = jnp.zeros_like(dw_ref)
            dbs_ref[...] = jnp.zeros_like(dbs_ref)
            gacc_ref[...] = jnp.zeros_like(gacc_ref)
            bacc_ref[...] = jnp.zeros_like(bacc_ref)

        zg = zg_ref[...].astype(F32)
        lng_v = lng_ref[...]
        u, xhat, rstd, vn = _gmlp_parts(zg, lng_v, lnb_ref[...])
        dy = dy_ref[...].astype(F32)
        dvn_parts = []
        for g in range(GMLP_GROUPS):
            gs = slice(g * GMLP_GROUP_DIM, (g + 1) * GMLP_GROUP_DIM)
            wc = _causal(w_ref[g]).astype(BF16)
            vn_s = _stack_chunks(vn[gs, :].astype(BF16), nc)
            mixed = _unstack_chunks(_dg(vn_s, wc, NT) + bs_ref[g:g + 1, :], nc)
            dy_g = dy[gs, :]
            dmixed = dy_g * u[gs, :]
            dm_s = _stack_chunks(dmixed.astype(BF16), nc)
            dzg_ref[gs, :] = (dy_g * mixed * _gelu_grad(zg[gs, :])).astype(BF16)
            dw_ref[g] += _causal(_dg(dm_s, vn_s, TN))
            dbs_ref[g:g + 1, :] += _lane_fold(jnp.sum(dmixed, axis=0, keepdims=True))
            dvn_parts.append(_unstack_chunks(_dot(dm_s, wc), nc))
        dvn = jnp.concatenate(dvn_parts, axis=0)
        gacc_ref[...] += _lane_fold(dvn * xhat)
        bacc_ref[...] += _lane_fold(dvn)
        dxhat = dvn * lng_v
        m1 = jnp.mean(dxhat, axis=0, keepdims=True)
        m2 = jnp.mean(dxhat * xhat, axis=0, keepdims=True)
        dvv = rstd * (dxhat - m1 - xhat * m2)
        dzg_ref[GMLP_WIDTH:, :] = (dvv * _gelu_grad(zg[GMLP_WIDTH:, :])).astype(BF16)

        @pl.when(n == nt - 1)
        def _():
            dlng_ref[...] = jnp.sum(gacc_ref[...], axis=1, keepdims=True)
            dlnb_ref[...] = jnp.sum(bacc_ref[...], axis=1, keepdims=True)

    const2 = lambda n: (0, 0)
    return pl.pallas_call(
        body, name=name, grid=(nt,),
        in_specs=[pl.BlockSpec((2 * GMLP_WIDTH, tw), lambda n: (0, n)), pl.BlockSpec((GMLP_WIDTH, tw), lambda n: (0, n)),
                  _resident((GMLP_WIDTH, 1)), _resident((GMLP_WIDTH, 1)),
                  _resident((GMLP_GROUPS, BLOCK, BLOCK)), _resident((GMLP_GROUPS, BLOCK))],
        out_specs=[pl.BlockSpec((2 * GMLP_WIDTH, tw), lambda n: (0, n)),
                   pl.BlockSpec((GMLP_GROUPS, BLOCK, BLOCK), lambda n: (0, 0, 0)),
                   pl.BlockSpec((GMLP_GROUPS, BLOCK), const2),
                   pl.BlockSpec((GMLP_WIDTH, 1), const2), pl.BlockSpec((GMLP_WIDTH, 1), const2)],
        out_shape=[jax.ShapeDtypeStruct((2 * GMLP_WIDTH, t), BF16),
                   jax.ShapeDtypeStruct((GMLP_GROUPS, BLOCK, BLOCK), F32),
                   jax.ShapeDtypeStruct((GMLP_GROUPS, BLOCK), F32),
                   jax.ShapeDtypeStruct((GMLP_WIDTH, 1), F32), jax.ShapeDtypeStruct((GMLP_WIDTH, 1), F32)],
        scratch_shapes=[pltpu.VMEM((GMLP_WIDTH, BLOCK), F32), pltpu.VMEM((GMLP_WIDTH, BLOCK), F32)],
        compiler_params=_params(("arbitrary",)),
    )(zg_t, dy_t, lng, lnb, w_s, b_s)


def _rmsnorm_cols(y, gain_col):
    r = lax.rsqrt(jnp.mean(y * y, axis=0, keepdims=True) + EPS)
    n = y * r
    return n, r, n * gain_col


def mix_out_fwd(x, ya_t, yg_t, gao, ggo, wout, b_out, name):
    t, d = x.shape
    tm = _token_tile(t)

    def body(x_ref, ya_ref, yg_ref, gao_ref, ggo_ref, w_ref, b_ref, o_ref):
        _, _, ya = _rmsnorm_cols(ya_ref[...].astype(F32), gao_ref[...])
        _, _, yg = _rmsnorm_cols(yg_ref[...].astype(F32), ggo_ref[...])
        o_ref[...] = (x_ref[...] + _dg(ya.astype(BF16), w_ref[:ATTN_WIDTH, :], TN)
                      + _dg(yg.astype(BF16), w_ref[ATTN_WIDTH:, :], TN) + b_ref[...])

    return pl.pallas_call(
        body, name=name, grid=(t // tm,),
        in_specs=[pl.BlockSpec((tm, d), lambda i: (i, 0)), pl.BlockSpec((ATTN_WIDTH, tm), lambda i: (0, i)),
                  pl.BlockSpec((GMLP_WIDTH, tm), lambda i: (0, i)), _resident((ATTN_WIDTH, 1)), _resident((GMLP_WIDTH, 1)),
                  _resident((ATTN_WIDTH + GMLP_WIDTH, d)), _resident((1, d))],
        out_specs=pl.BlockSpec((tm, d), lambda i: (i, 0)),
        out_shape=jax.ShapeDtypeStruct((t, d), F32),
        compiler_params=_params(("arbitrary",)),
    )(x, ya_t, yg_t, gao, ggo, wout, b_out)


def _lane_fold(v):
    out = v[:, :LANES]
    for j in range(1, v.shape[1] // LANES):
        out = out + v[:, j * LANES:(j + 1) * LANES]
    return out


def mix_out_bwd(dx, ya_t, yg_t, gao, ggo, wout, name, deps=()):
    t, d = dx.shape
    tm = _token_tile(t)
    nt = t // tm

    def body(dx_ref, ya_ref, yg_ref, gao_ref, ggo_ref, w_ref, dya_ref, dyg_ref, y_ref, dgao_ref, dggo_ref, db_ref,
             acc_ref):
        i = pl.program_id(0)

        @pl.when(i == 0)
        def _():
            acc_ref[...] = jnp.zeros_like(acc_ref)
            db_ref[...] = jnp.zeros_like(db_ref)

        dxv = dx_ref[...]
        db_ref[...] += jnp.sum(dxv, axis=0, keepdims=True)
        dxb = dxv.astype(BF16)
        for part, (src_ref, gain_ref, dst_ref) in enumerate(((ya_ref, gao_ref, dya_ref), (yg_ref, ggo_ref, dyg_ref))):
            rows = slice(part * ATTN_WIDTH, (part + 1) * ATTN_WIDTH)
            gain = gain_ref[...]
            nrm, r, yn = _rmsnorm_cols(src_ref[...].astype(F32), gain)
            y_ref[rows, :] = yn.astype(BF16)
            dy = _dg(w_ref[rows, :], dxb, NT)
            acc_ref[rows, :] += _lane_fold(dy * nrm)
            dn = dy * gain
            dst_ref[...] = (r * (dn - nrm * jnp.mean(dn * nrm, axis=0, keepdims=True))).astype(BF16)

        @pl.when(i == nt - 1)
        def _():
            dgao_ref[...] = jnp.sum(acc_ref[:ATTN_WIDTH, :], axis=1, keepdims=True)
            dggo_ref[...] = jnp.sum(acc_ref[ATTN_WIDTH:, :], axis=1, keepdims=True)

    const2 = lambda i: (0, 0)
    feat = lambda w: pl.BlockSpec((w, tm), lambda i: (0, i))
    return pl.pallas_call(
        _drop_deps(body, 6, len(deps)), name=name, grid=(nt,),
        in_specs=[pl.BlockSpec((tm, d), lambda i: (i, 0)), feat(ATTN_WIDTH), feat(GMLP_WIDTH),
                  _resident((ATTN_WIDTH, 1)), _resident((GMLP_WIDTH, 1)), _resident((ATTN_WIDTH + GMLP_WIDTH, d))]
        + [ANY_SPEC] * len(deps),
        out_specs=[feat(ATTN_WIDTH), feat(GMLP_WIDTH), feat(ATTN_WIDTH + GMLP_WIDTH),
                   pl.BlockSpec((ATTN_WIDTH, 1), const2), pl.BlockSpec((GMLP_WIDTH, 1), const2),
                   pl.BlockSpec((1, d), const2)],
        out_shape=[jax.ShapeDtypeStruct((ATTN_WIDTH, t), BF16), jax.ShapeDtypeStruct((GMLP_WIDTH, t), BF16),
                   jax.ShapeDtypeStruct((ATTN_WIDTH + GMLP_WIDTH, t), BF16),
                   jax.ShapeDtypeStruct((ATTN_WIDTH, 1), F32), jax.ShapeDtypeStruct((GMLP_WIDTH, 1), F32),
                   jax.ShapeDtypeStruct((1, d), F32)],
        scratch_shapes=[pltpu.VMEM((ATTN_WIDTH + GMLP_WIDTH, LANES), F32)],
        compiler_params=_params(("arbitrary",)),
    )(dx, ya_t, yg_t, gao, ggo, wout, *deps)


def mix_in_bwd(x, dxo, gain, dq_t, dk_t, dv_t, dzg_t, win_t, name):
    t, d = x.shape
    w = win_t.shape[0]
    tm = _token_tile(t)
    nt = t // tm
    parts = ((0, ATTN_WIDTH), (ATTN_WIDTH, KV_WIDTH), (ATTN_WIDTH + KV_WIDTH, KV_WIDTH), (QKV_WIDTH, w - QKV_WIDTH))

    def body(x_ref, dxo_ref, g_ref, dq_ref, dk_ref, dv_ref, dzg_ref, w_ref, dx_ref, h_ref, dg_ref, db_ref, acc_ref):
        i = pl.program_id(0)

        @pl.when(i == 0)
        def _():
            acc_ref[...] = jnp.zeros_like(acc_ref)
            dg_ref[...] = jnp.zeros_like(dg_ref)

        xv = x_ref[...]
        gain_v = g_ref[...]
        r = lax.rsqrt(jnp.mean(xv * xv, axis=-1, keepdims=True) + EPS)
        h_ref[...] = (xv * r * gain_v).astype(BF16)
        dh = jnp.zeros((tm, d), F32)
        for (off, size), src_ref in zip(parts, (dq_ref, dk_ref, dv_ref, dzg_ref)):
            dp = src_ref[...]
            acc_ref[off:off + size, :] += _lane_fold(dp.astype(F32))
            dh = dh + _dg(dp, w_ref[off:off + size, :], TN)
        dx, dgain = _rmsnorm_bwd_rows(xv, r, gain_v, dh)
        dx_ref[...] = dxo_ref[...] + dx
        dg_ref[...] += dgain

        @pl.when(i == nt - 1)
        def _():
            db_ref[...] = jnp.sum(acc_ref[...], axis=1, keepdims=True)

    const2 = lambda i: (0, 0)
    tok = lambda: pl.BlockSpec((tm, d), lambda i: (i, 0))
    feat = lambda rows: pl.BlockSpec((rows, tm), lambda i: (0, i))
    return pl.pallas_call(
        body, name=name, grid=(nt,),
        in_specs=[tok(), tok(), _resident((1, d)), feat(ATTN_WIDTH), feat(KV_WIDTH), feat(KV_WIDTH),
                  feat(w - QKV_WIDTH), _resident((w, d))],
        out_specs=[tok(), tok(), pl.BlockSpec((1, d), const2), pl.BlockSpec((w, 1), const2)],
        out_shape=[jax.ShapeDtypeStruct((t, d), F32), jax.ShapeDtypeStruct((t, d), BF16),
                   jax.ShapeDtypeStruct((1, d), F32), jax.ShapeDtypeStruct((w, 1), F32)],
        scratch_shapes=[pltpu.VMEM((w, LANES), F32)],
        compiler_params=_params(("arbitrary",)),
    )(x, dxo, gain, dq_t, dk_t, dv_t, dzg_t, win_t)


def _adamw_math(w, g, m, v):
    m = ADAM_B1 * m + (1.0 - ADAM_B1) * g
    v = ADAM_B2 * v + (1.0 - ADAM_B2) * (g * g)
    m_hat = m / (1.0 - ADAM_B1 ** ADAM_STEP)
    v_hat = v / (1.0 - ADAM_B2 ** ADAM_STEP)
    delta = -ADAM_LR * (m_hat / (jnp.sqrt(v_hat) + ADAM_EPS) + ADAM_WD * w)
    return delta, m, v


def adamw(w, g, m, v, name):
    rows, cols = w.shape
    tr = rows
    if rows * cols > 256 * 1024:
        for cand in (256, 128, 64, 32, 16, 8):
            if rows % cand == 0:
                tr = cand
                break

    def body(w_ref, g_ref, m_ref, v_ref, d_ref, mo_ref, vo_ref):
        delta, mn, vn = _adamw_math(w_ref[...], g_ref[...], m_ref[...], v_ref[...])
        d_ref[...] = delta
        mo_ref[...] = mn
        vo_ref[...] = vn

    spec = pl.BlockSpec((tr, cols), lambda i: (i, 0))
    shape = jax.ShapeDtypeStruct((rows, cols), F32)
    return pl.pallas_call(
        body, name=name, grid=(rows // tr,),
        in_specs=[spec] * 4, out_specs=[spec] * 3, out_shape=[shape] * 3,
        compiler_params=_params(("arbitrary",)),
    )(w, g, m, v)


def sum_adamw(landing, grad3d, me, w, m, v, name):
    nd, rows, cols = landing.shape
    tr = rows // 2

    def body(me_ref, l_ref, own_ref, w_ref, m_ref, v_ref, g_ref, d_ref, mo_ref, vo_ref):
        g = own_ref[...].astype(F32)
        for j in range(nd):
            g = g + l_ref[j].astype(F32)
        delta, mn, vn = _adamw_math(w_ref[...], g, m_ref[...], v_ref[...])
        g_ref[...] = g
        d_ref[...] = delta
        mo_ref[...] = mn
        vo_ref[...] = vn

    spec = lambda: pl.BlockSpec((tr, cols), lambda i, me_ref: (i, 0))
    shape = jax.ShapeDtypeStruct((rows, cols), F32)
    return pl.pallas_call(
        body, name=name,
        grid_spec=pltpu.PrefetchScalarGridSpec(
            num_scalar_prefetch=1, grid=(rows // tr,),
            in_specs=[pl.BlockSpec((nd, tr, cols), lambda i, me_ref: (0, i, 0)),
                      pl.BlockSpec((None, tr, cols), lambda i, me_ref: (me_ref[0], i, 0)), spec(), spec(), spec()],
            out_specs=[spec() for _ in range(4)]),
        out_shape=[shape] * 4,
        compiler_params=_params(("arbitrary",)),
    )(me, landing, grad3d, w, m, v)


def _mesh_place():
    x, y, c = lax.axis_index("x"), lax.axis_index("y"), lax.axis_index("c")
    return x, y, c, 4 * x + 2 * y + c


def _peer(x, y, c, k):
    return (x ^ ((k >> 2) & 1), y ^ ((k >> 1) & 1), c ^ (k & 1))


def prep_shards(mats, me, name):
    na = len(mats)

    def body(me_ref, *refs):
        in_refs, shard_refs, land_refs = refs[:na], refs[na:2 * na], refs[2 * na:]
        for a in range(na):
            val = in_refs[a][...].astype(BF16)
            shard_refs[a][...] = val
            land_refs[a][...] = val

    whole = lambda mat: pl.BlockSpec(mat.shape, lambda i, me_ref: (0, 0))
    outs = pl.pallas_call(
        body, name=name,
        grid_spec=pltpu.PrefetchScalarGridSpec(
            num_scalar_prefetch=1, grid=(1,),
            in_specs=[whole(mat) for mat in mats],
            out_specs=[whole(mat) for mat in mats]
            + [pl.BlockSpec((None,) + mat.shape, lambda i, me_ref: (me_ref[0], 0, 0)) for mat in mats]),
        out_shape=[jax.ShapeDtypeStruct(mat.shape, BF16) for mat in mats]
        + [jax.ShapeDtypeStruct((N_DEV,) + mat.shape, BF16) for mat in mats],
        compiler_params=_params(("arbitrary",)),
    )(me, *mats)
    return outs[:na], outs[na:]


def allgather_rows(shards, lands, name):
    na = n_gather = len(shards)

    def body(*refs):
        in_refs, out_refs = refs[:na], refs[2 * na:3 * na]
        send_sems, recv_sems = refs[3 * na:]
        x, y, c, me = _mesh_place()

        def copy(sem, a, origin, src, k):
            return pltpu.make_async_remote_copy(
                src_ref=src, dst_ref=out_refs[a].at[origin], send_sem=send_sems.at[sem, a],
                recv_sem=recv_sems.at[sem, a], device_id=_peer(x, y, c, k), device_id_type=pl.DeviceIdType.MESH)

        kx, ky = 4, 2

        def recv(sem, a, origin):
            copy(sem, a, origin, in_refs[a], 1).wait_recv()

        def landed(a, origin):
            return out_refs[a].at[origin]

        @pl.when(c == 1)
        def _():
            sends = []

            def send(sem, a, origin, src, k):
                cp = copy(sem, a, origin, src, k)
                cp.start()
                sends.append(cp)

            for a in range(n_gather):
                send(1, a, me, in_refs[a], kx)
                send(2, a, me, in_refs[a], ky)
                send(0, a, me, in_refs[a], 1)
            for a in range(n_gather):
                recv(0, a, me ^ 1)
                send(3, a, me ^ 1, landed(a, me ^ 1), kx)
                send(4, a, me ^ 1, landed(a, me ^ 1), ky)
            for a in range(n_gather):
                recv(1, a, me ^ kx)
                send(5, a, me ^ kx, landed(a, me ^ kx), ky)
                send(7, a, me ^ kx, landed(a, me ^ kx), 1)
                recv(2, a, me ^ ky)
                send(8, a, me ^ ky, landed(a, me ^ ky), 1)
            for a in range(n_gather):
                recv(3, a, me ^ kx ^ 1)
                send(9, a, me ^ kx ^ 1, landed(a, me ^ kx ^ 1), 1)
                recv(4, a, me ^ ky ^ 1)
                send(6, a, me ^ ky ^ 1, landed(a, me ^ ky ^ 1), kx)
                send(10, a, me ^ ky ^ 1, landed(a, me ^ ky ^ 1), 1)
            for a in range(n_gather):
                recv(5, a, me ^ 6)
                send(11, a, me ^ 6, landed(a, me ^ 6), 1)
                recv(6, a, me ^ 7)
                send(12, a, me ^ 7, landed(a, me ^ 7), 1)
            for cp in sends:
                cp.wait_send()

        @pl.when(c == 0)
        def _():
            north = me ^ 1
            own = [copy(0, a, me, in_refs[a], 1) for a in range(n_gather)]
            for cp in own:
                cp.start()
            for a in range(n_gather):
                recv(0, a, north)
                for sem, origin in ((7, north ^ kx), (8, north ^ ky), (9, north ^ kx ^ 1), (10, north ^ ky ^ 1),
                                    (11, north ^ 6), (12, north ^ 7)):
                    recv(sem, a, origin)
            for cp in own:
                cp.wait_send()

    return pl.pallas_call(
        body, name=name,
        in_specs=[ANY_SPEC] * (2 * na), out_specs=[ANY_SPEC] * na,
        out_shape=[jax.ShapeDtypeStruct(land.shape, land.dtype) for land in lands],
        input_output_aliases={na + a: a for a in range(na)},
        scratch_shapes=[pltpu.SemaphoreType.DMA((13, n_gather)), pltpu.SemaphoreType.DMA((13, n_gather))],
    )(*shards, *lands)


HBM_SPEC = pl.BlockSpec(memory_space=pltpu.HBM)
SEM_SPEC = pl.BlockSpec(memory_space=pltpu.SEMAPHORE)
SIDE_EFFECT = pltpu.SideEffectType.DATAFLOW_SIDE_EFFECTING


def _hbm(v):
    return pltpu.with_memory_space_constraint(v, pltpu.HBM)


def gather_slices(src, land, me, k):
    return src, land.at[me], land.at[me ^ k]


def exchange_slices(src, land, me, k):
    return src.at[me ^ k], land.at[k - 1], land.at[k - 1]


def split_start(groups, slices, deps, name):
    sizes = [len(srcs) for srcs, _ in groups]
    flat_src = [s for srcs, _ in groups for s in srcs]
    flat_land = [l for _, lands in groups for l in lands]
    na, ng, nd = len(flat_src), len(groups), len(deps)

    def body(*refs):
        src_refs, land_refs = refs[:na], refs[na:2 * na]
        sem_refs = refs[2 * na + nd:2 * na + nd + 2 * ng]
        token_ref = refs[-1]
        x, y, c, me = _mesh_place()
        a0 = 0
        for gi, size in enumerate(sizes):
            send_sems, recv_sems = sem_refs[2 * gi], sem_refs[2 * gi + 1]
            for k in range(1, N_DEV):
                for j in range(size):
                    src, dst, _ = slices(src_refs[a0 + j], land_refs[a0 + j], me, k)
                    pltpu.make_async_remote_copy(
                        src_ref=src, dst_ref=dst, send_sem=send_sems.at[(k - 1) * size + j],
                        recv_sem=recv_sems.at[(k - 1) * size + j],
                        device_id=_peer(x, y, c, k), device_id_type=pl.DeviceIdType.MESH).start()
            a0 += size
        token_ref[...] = jnp.zeros_like(token_ref)

    sems = [pltpu.SemaphoreType.DMA(((N_DEV - 1) * size,)) for size in sizes for _ in range(2)]
    thru = [pltpu.HBM(v.shape, v.dtype) for v in flat_src + flat_land]
    outs = pl.pallas_call(
        body, name=name,
        in_specs=[HBM_SPEC] * (2 * na) + [ANY_SPEC] * nd,
        out_specs=[SEM_SPEC] * (2 * ng) + [HBM_SPEC] * (2 * na) + [pl.BlockSpec(memory_space=pltpu.VMEM)],
        out_shape=sems + thru + [jax.ShapeDtypeStruct((8, LANES), F32)],
        input_output_aliases={i: 2 * ng + i for i in range(2 * na)},
        compiler_params=pltpu.CompilerParams(has_side_effects=SIDE_EFFECT),
    )(*[_hbm(v) for v in flat_src + flat_land], *deps)
    started, a0 = [], 0
    for gi, size in enumerate(sizes):
        srcs = outs[2 * ng + a0:2 * ng + a0 + size]
        lands = outs[2 * ng + na + a0:2 * ng + na + a0 + size]
        started.append((outs[2 * gi], outs[2 * gi + 1], list(srcs), list(lands)))
        a0 += size
    return started, outs[-1]


def split_wait(started, slices, after, name):
    send_sems, recv_sems, srcs, lands = started
    na = len(srcs)

    def body(*refs):
        src_refs, land_refs = refs[:na], refs[na:2 * na]
        send_ref, recv_ref = refs[2 * na], refs[2 * na + 1]
        x, y, c, me = _mesh_place()
        for k in range(1, N_DEV):
            for j in range(na):
                src, _, got = slices(src_refs[j], land_refs[j], me, k)
                cp = pltpu.make_async_remote_copy(
                    src_ref=src, dst_ref=got, send_sem=send_ref.at[(k - 1) * na + j], recv_sem=recv_ref.at[(k - 1) * na + j],
                    device_id=_peer(x, y, c, k), device_id_type=pl.DeviceIdType.MESH)
                cp.wait_send()
                cp.wait_recv()

    outs = pl.pallas_call(
        body, name=name,
        in_specs=[HBM_SPEC] * (2 * na) + [SEM_SPEC, SEM_SPEC] + [ANY_SPEC] * len(after),
        out_specs=[HBM_SPEC] * (2 * na),
        out_shape=[pltpu.HBM(v.shape, v.dtype) for v in srcs + lands],
        input_output_aliases={i: i for i in range(2 * na)},
        compiler_params=pltpu.CompilerParams(has_side_effects=SIDE_EFFECT),
    )(*srcs, *lands, send_sems, recv_sems, *after)
    return list(outs[:na]), list(outs[na:])


def place_own(pack, me, name):
    rows, cols = pack.shape

    def body(me_ref, p_ref, o_ref):
        o_ref[...] = p_ref[...]

    return pl.pallas_call(
        body, name=name,
        grid_spec=pltpu.PrefetchScalarGridSpec(
            num_scalar_prefetch=1, grid=(1,),
            in_specs=[pl.BlockSpec((rows, cols), lambda i, me_ref: (0, 0))],
            out_specs=pl.BlockSpec((None, rows, cols), lambda i, me_ref: (me_ref[0], 0, 0))),
        out_shape=jax.ShapeDtypeStruct((N_DEV, rows, cols), F32),
        compiler_params=_params(("arbitrary",)),
    )(me, pack)


def sum_slots(slots, name):
    nd, rows, cols = slots.shape

    def body(s_ref, o_ref):
        acc = s_ref[0]
        for j in range(1, nd):
            acc = acc + s_ref[j]
        o_ref[...] = acc

    return pl.pallas_call(
        body, name=name, grid=(1,),
        in_specs=[pl.BlockSpec((nd, rows, cols), lambda i: (0, 0, 0))],
        out_specs=pl.BlockSpec((rows, cols), lambda i: (0, 0)),
        out_shape=jax.ShapeDtypeStruct((rows, cols), F32),
        compiler_params=_params(("arbitrary",)),
    )(slots)


def _pack_rows(size):
    rows = -(-size // LANES)
    return -(-rows // 8) * 8


def pack_small(parts):
    out = []
    for p in parts:
        flat = p.reshape(-1).astype(F32)
        rows = _pack_rows(flat.shape[0])
        out.append(jnp.pad(flat, (0, rows * LANES - flat.shape[0])).reshape(rows, LANES))
    return jnp.concatenate(out, axis=0)


def unpack_small(pack, shapes):
    out, off = [], 0
    for shp in shapes:
        size = 1
        for s in shp:
            size *= s
        rows = _pack_rows(size)
        out.append(pack[off:off + rows].reshape(-1)[:size].reshape(shp))
        off += rows
    return out


def local_step(x, target, small, get_w, put_g, put_small):
    col = lambda v: v.reshape(-1, 1)
    (wg1, wu1, wd1), deps = get_w(("wg1", "wu1", "wd1"), ())
    x1, sil1, gp1, s1 = ffn_fwd(x, small["ffn1_norm_g"], wg1, wu1, wd1, "ffn1_fwd", deps)
    (win, wout), _ = get_w(("win", "wout"), (x1,))
    qkv_t, zg_t = mix_in_fwd(x1, small["mix_norm_g"], win, col(small["b_in"]), "mix_in_fwd")
    ya_t = attn_fwd(qkv_t, small["attn_sinks"], "attn_fwd")
    lng, lnb = col(small["gmlp_ln_g"]), col(small["gmlp_ln_b"])
    w_s, b_s = small["gmlp_w_s"][0], small["gmlp_b_s"][0]
    yg_t = gmlp_fwd(zg_t, lng, lnb, w_s, b_s, "gmlp_fwd")
    gao, ggo = col(small["attn_out_norm_g"]), col(small["gmlp_out_norm_g"])
    x2 = mix_out_fwd(x1, ya_t, yg_t, gao, ggo, wout, small["b_out"], "mix_out_fwd")
    (wg2, wu2, wd2), _ = get_w(("wg2", "wu2", "wd2"), (x2,))
    gs = {}
    dx3, sil2, gp2, s2, gs["final_norm_g"], loss = ffn_fwd(
        x2, small["ffn2_norm_g"], wg2, wu2, wd2, "ffn2_fwd", head=(target, small["final_norm_g"].reshape(1, -1)))

    dx2, da, db, h, gs["ffn2_norm_g"] = ffn_bwd(x2, dx3, small["ffn2_norm_g"], sil2, gp2, wg2, wu2, wd2, "ffn2_bwd")
    deps = put_g({"wg2": grad_tn(da, h, 1.0, "ffn2_dwg"), "wu2": grad_tn(db, h, 1.0, "ffn2_dwu"),
                  "wd2": grad_tn(s2, dx3, FFN_RES, "ffn2_dwd")})

    dya_t, dyg_t, y_t, dgao, dggo, gs["b_out"] = mix_out_bwd(dx2, ya_t, yg_t, gao, ggo, wout, "mix_out_bwd", deps)
    gs["attn_out_norm_g"], gs["gmlp_out_norm_g"] = dgao.reshape(1, -1), dggo.reshape(1, -1)
    g_wout = grad_nn([y_t], dx2, "dwout")
    dzg_t, dws, dbs, dlng, dlnb = gmlp_bwd(zg_t, dyg_t, lng, lnb, w_s, b_s, "gmlp_bwd")
    gs["gmlp_w_s"], gs["gmlp_b_s"] = dws[None], dbs[None]
    gs["gmlp_ln_g"], gs["gmlp_ln_b"] = dlng.reshape(1, -1), dlnb.reshape(1, -1)
    dq_t, dk_t, dv_t, dsinks = attn_bwd(qkv_t, dya_t, small["attn_sinks"], "attn_bwd")
    gs["attn_sinks"] = dsinks.reshape(1, -1)
    dx1, h2, gs["mix_norm_g"], dbin = mix_in_bwd(x1, dx2, small["mix_norm_g"], dq_t, dk_t, dv_t, dzg_t, win,
                                                 "mix_in_bwd")
    gs["b_in"] = dbin.reshape(1, -1)
    deps = put_g({"wout": g_wout, "win": grad_nn([dq_t, dk_t, dv_t, dzg_t], h2, "dwin")})

    dx0, da, db, h, gs["ffn1_norm_g"] = ffn_bwd(x, dx1, small["ffn1_norm_g"], sil1, gp1, wg1, wu1, wd1, "ffn1_bwd",
                                                deps)
    deps = put_small(gs, loss)
    deps = put_g({"wg1": grad_tn(da, h, 1.0, "ffn1_dwg", deps)})
    deps = put_g({"wu1": grad_tn(db, h, 1.0, "ffn1_dwu", deps)})
    put_g({"wd1": grad_tn(s1, dx1, FFN_RES, "ffn1_dwd", deps)})
    return dx0


SMALL_NAMES = ["ffn1_norm_g", "mix_norm_g", "b_in", "attn_sinks", "gmlp_ln_g", "gmlp_ln_b", "gmlp_w_s", "gmlp_b_s",
               "attn_out_norm_g", "gmlp_out_norm_g", "b_out", "ffn2_norm_g", "final_norm_g"]
BIG_NAMES = {"wg1": ("ffn1_w_gate", True), "wu1": ("ffn1_w_up", True), "wd1": ("ffn1_w_down", False),
             "win": ("w_in", True), "wout": ("w_out", False),
             "wg2": ("ffn2_w_gate", True), "wu2": ("ffn2_w_up", True), "wd2": ("ffn2_w_down", False)}
WEIGHT_ORDER = ["ffn1_norm_g", "ffn1_w_gate", "ffn1_w_up", "ffn1_w_down", "mix_norm_g", "w_in", "b_in", "attn_sinks",
                "gmlp_ln_g", "gmlp_ln_b", "gmlp_w_s", "gmlp_b_s", "attn_out_norm_g", "gmlp_out_norm_g", "w_out",
                "b_out", "ffn2_norm_g", "ffn2_w_gate", "ffn2_w_up", "ffn2_w_down", "final_norm_g"]


def kernel(x, ffn1_norm_g, ffn1_w_gate, ffn1_w_up, ffn1_w_down, mix_norm_g, w_in, b_in, attn_sinks, gmlp_ln_g, gmlp_ln_b, gmlp_w_s, gmlp_b_s, attn_out_norm_g, gmlp_out_norm_g, w_out, b_out, ffn2_norm_g, ffn2_w_gate, ffn2_w_up, ffn2_w_down, final_norm_g, loss_target, m_ffn1_norm_g, m_ffn1_w_gate, m_ffn1_w_up, m_ffn1_w_down, m_mix_norm_g, m_w_in, m_b_in, m_attn_sinks, m_gmlp_ln_g, m_gmlp_ln_b, m_gmlp_w_s, m_gmlp_b_s, m_attn_out_norm_g, m_gmlp_out_norm_g, m_w_out, m_b_out, m_ffn2_norm_g, m_ffn2_w_gate, m_ffn2_w_up, m_ffn2_w_down, m_final_norm_g, v_ffn1_norm_g, v_ffn1_w_gate, v_ffn1_w_up, v_ffn1_w_down, v_mix_norm_g, v_w_in, v_b_in, v_attn_sinks, v_gmlp_ln_g, v_gmlp_ln_b, v_gmlp_w_s, v_gmlp_b_s, v_attn_out_norm_g, v_gmlp_out_norm_g, v_w_out, v_b_out, v_ffn2_norm_g, v_ffn2_w_gate, v_ffn2_w_up, v_ffn2_w_down, v_final_norm_g):
    args = dict(locals())
    w = {n: args[n] for n in WEIGHT_ORDER}
    m = {n: args["m_" + n] for n in WEIGHT_ORDER}
    v = {n: args["v_" + n] for n in WEIGHT_ORDER}

    d_model = x.shape[-1]
    flat = lambda g: g.reshape(-1, d_model)
    me = _mesh_place()[3].astype(jnp.int32).reshape(1)
    first = ("wg1", "wu1", "wd1")
    later = (("win", "wout"), ("wg2", "wu2", "wd2"))
    gathers, exchanges, last_token = {}, [], []

    order = list(first) + [k for grp in later for k in grp]
    mats = [w[BIG_NAMES[k][0]][0].T if BIG_NAMES[k][1] else w[BIG_NAMES[k][0]][0] for k in order]
    shard_list, land_list = prep_shards(mats, me, "prep_shards")
    shards, zones = dict(zip(order, shard_list)), dict(zip(order, land_list))

    def get_w(keys, after):
        if keys == first:
            full = allgather_rows([shards[k] for k in first], [zones[k] for k in first], "allgather_ffn1")
            started, token = split_start([([shards[k] for k in grp], [zones[k] for k in grp]) for grp in later],
                                         gather_slices, (full[0],), "gather_start")
            gathers.update(zip(later, started))
            return [flat(g) for g in full], (token,)
        _, lands = split_wait(gathers[keys], gather_slices, after, "gather_wait_" + keys[0])
        return [flat(land) for land in lands], ()

    def put_g(gb):
        keys = tuple(gb)
        g3 = [gb[k].reshape(N_DEV, -1, d_model) for k in keys]
        lands = [lax.empty((N_DEV - 1,) + g.shape[1:], BF16) for g in g3]
        started, token = split_start([(g3, lands)], exchange_slices, (), "exchange_start_" + keys[0])
        exchanges.append((keys, started[0]))
        last_token[:] = [token]
        return (token,)

    small_started = []

    def put_small(gs, loss):
        pack = pack_small([gs[n] for n in SMALL_NAMES] + [loss[:, :1]])
        started, token = split_start([([pack], [place_own(pack, me, "place_small")])], gather_slices, (), "small_start")
        small_started[:] = started
        return (token,)

    small = {n: w[n] for n in SMALL_NAMES}
    grad_x = local_step(x[0], loss_target[0], small, get_w, put_g, put_small)

    grads, deltas, new_m, new_v = {}, {}, {}, {}
    after = tuple(last_token)
    for keys, started in exchanges:
        g3, lands = split_wait(started, exchange_slices, after, "exchange_wait_" + keys[0])
        for key, own, land in zip(keys, g3, lands):
            pname, transposed = BIG_NAMES[key]
            to_rows = (lambda p: p[0].T) if transposed else (lambda p: p[0])
            from_rows = (lambda r: r.T[None]) if transposed else (lambda r: r[None])
            g, d_, m_, v_ = sum_adamw(land, own, me, to_rows(w[pname]), to_rows(m[pname]), to_rows(v[pname]),
                                      "adamw_" + key)
            after = (g,)
            grads[pname], deltas[pname], new_m[pname], new_v[pname] = (from_rows(g), from_rows(d_), from_rows(m_),
                                                                        from_rows(v_))

    shapes = [w[n].shape for n in SMALL_NAMES]
    _, (slots,) = split_wait(small_started[0], gather_slices, after, "small_wait")
    total = sum_slots(slots, "sum_small")
    small_g = unpack_small(total, shapes + [(1, 1)])
    loss_total = small_g[-1].reshape(())
    gpack = total[:sum(_pack_rows(w[n].size) for n in SMALL_NAMES)]
    d_, m_, v_ = adamw(pack_small([w[n] for n in SMALL_NAMES]), gpack, pack_small([m[n] for n in SMALL_NAMES]),
                       pack_small([v[n] for n in SMALL_NAMES]), "adamw_small")
    for n, g_, dd, mm, vv in zip(SMALL_NAMES, small_g[:-1], unpack_small(d_, shapes), unpack_small(m_, shapes),
                                 unpack_small(v_, shapes)):
        grads[n], deltas[n], new_m[n], new_v[n] = g_, dd, mm, vv

    return (loss_total, grad_x[None], *[grads[n] for n in WEIGHT_ORDER], *[deltas[n] for n in WEIGHT_ORDER],
            *[new_m[n] for n in WEIGHT_ORDER], *[new_v[n] for n in WEIGHT_ORDER])
```

```python
import functools

import jax
import jax.numpy as jnp
from jax import lax
from jax.experimental import pallas as pl
from jax.experimental.pallas import tpu as pltpu

F32 = jnp.float32
BF16 = jnp.bfloat16

N_DEV = 8
N_Q_HEADS = 8
N_KV_HEADS = 2
HEAD_DIM = 64
ATTN_WIDTH = N_Q_HEADS * HEAD_DIM
KV_WIDTH = N_KV_HEADS * HEAD_DIM
QKV_WIDTH = ATTN_WIDTH + 2 * KV_WIDTH
BLOCK = 128
GMLP_GROUPS = 8
GMLP_GROUP_DIM = 64
GMLP_WIDTH = GMLP_GROUPS * GMLP_GROUP_DIM
EPS = 1e-6
FFN_RES = 0.5
ATTN_SCALE = HEAD_DIM ** -0.5
MASK_VALUE = -1e30

ADAM_LR = 0.001
ADAM_B1 = 0.9
ADAM_B2 = 0.999
ADAM_EPS = 1e-08
ADAM_WD = 0.01
ADAM_STEP = 10

V7X_VMEM_BYTES = 64 * 1024 * 1024
VMEM_LIMIT = 62 * 1024 * 1024
LANES = 128
MXU_WIDTH = 256

NT = (((1,), (1,)), ((), ()))
TN = (((0,), (0,)), ((), ()))


def _dot(a, b):
    return jnp.dot(a, b, preferred_element_type=F32)


def _dg(a, b, dims):
    return lax.dot_general(a, b, dims, preferred_element_type=F32)


def _params(semantics=None):
    return pltpu.CompilerParams(dimension_semantics=semantics, vmem_limit_bytes=VMEM_LIMIT)


def _resident(shape):
    zeros = (0,) * len(shape)
    return pl.BlockSpec(shape, lambda *_: zeros, pipeline_mode=pl.Buffered(1))


def _drop_deps(body, n_in, n_deps):
    return lambda *refs: body(*refs[:n_in], *refs[n_in + n_deps:])


ANY_SPEC = pl.BlockSpec(memory_space=pl.ANY)


def _token_tile(t):
    return min(t, 512)


def _f_chunk(f):
    for c in (256, 128):
        if f % c == 0:
            return c
    return f


def _loss_head(xv, target, gain):
    r = lax.rsqrt(jnp.mean(xv * xv, axis=-1, keepdims=True) + EPS)
    err = xv * r * gain - target
    loss = 0.5 * jnp.sum(jnp.mean(err * err, axis=-1, keepdims=True), axis=0, keepdims=True)
    dx, dgain = _rmsnorm_bwd_rows(xv, r, gain, err * (1.0 / xv.shape[-1]))
    return dx, dgain, loss


def ffn_fwd(x, gain, wg_t, wu_t, wd, name, deps=(), head=None):
    t, d = x.shape
    f = wd.shape[0]
    tm = _token_tile(t)
    fc = _f_chunk(f)
    n_in = 5 if head is None else 7

    def body(x_ref, g_ref, wg_ref, wu_ref, wd_ref, *rest):
        xo_ref, sil_ref, gp_ref, s_ref = rest[n_in - 5:n_in - 1]
        xv = x_ref[...]
        r = lax.rsqrt(jnp.mean(xv * xv, axis=-1, keepdims=True) + EPS)
        h = (xv * r * g_ref[...]).astype(BF16)
        for c in range(f // fc):
            sl = slice(c * fc, (c + 1) * fc)
            a = _dg(h, wg_ref[sl, :], NT)
            b = _dg(h, wu_ref[sl, :], NT)
            sig = jax.nn.sigmoid(a)
            sil = a * sig
            sil_ref[:, sl] = sil.astype(BF16)
            gp_ref[:, sl] = (b * (sig + sil - sil * sig)).astype(BF16)
            s_ref[:, sl] = (sil * b).astype(BF16)
        xo = xv + FFN_RES * _dot(s_ref[...], wd_ref[...])
        if head is None:
            xo_ref[...] = xo
        else:
            t_ref, hg_ref = rest[:2]
            dg_ref, loss_ref = rest[-2:]

            @pl.when(pl.program_id(0) == 0)
            def _():
                dg_ref[...] = jnp.zeros_like(dg_ref)
                loss_ref[...] = jnp.zeros_like(loss_ref)

            dx, dgain, loss = _loss_head(xo, t_ref[...], hg_ref[...])
            xo_ref[...] = dx
            dg_ref[...] += dgain
            loss_ref[...] += loss

    tok = lambda: pl.BlockSpec((tm, d), lambda i: (i, 0))
    wide = lambda: pl.BlockSpec((tm, f), lambda i: (i, 0))
    const2 = lambda i: (0, 0)
    in_specs = [tok(), _resident((1, d)), _resident((f, d)), _resident((f, d)), _resident((f, d))]
    out_specs = [tok(), wide(), wide(), wide()]
    out_shape = [jax.ShapeDtypeStruct((t, d), F32)] + [jax.ShapeDtypeStruct((t, f), BF16)] * 3
    operands = [x, gain, wg_t, wu_t, wd]
    if head is not None:
        in_specs += [tok(), _resident((1, d))]
        out_specs += [pl.BlockSpec((1, d), const2), pl.BlockSpec((1, LANES), const2)]
        out_shape += [jax.ShapeDtypeStruct((1, d), F32), jax.ShapeDtypeStruct((1, LANES), F32)]
        operands += list(head)
    return pl.pallas_call(
        _drop_deps(body, n_in, len(deps)), name=name, grid=(t // tm,),
        in_specs=in_specs + [ANY_SPEC] * len(deps), out_specs=out_specs, out_shape=out_shape,
        compiler_params=_params(("arbitrary",)),
    )(*operands, *deps)


def _rmsnorm_bwd_rows(xv, r, gain, dh):
    n = xv * r
    dgain = jnp.sum(dh * n, axis=0, keepdims=True)
    dn = dh * gain
    dx = r * (dn - n * jnp.mean(dn * n, axis=-1, keepdims=True))
    return dx, dgain


def ffn_bwd(x, dxo, gain, sil, gp, wg_t, wu_t, wd, name, deps=()):
    t, d = x.shape
    f = wd.shape[0]
    tm = _token_tile(t)
    nh = 2 if f % (2 * LANES) == 0 else 1
    fh = f // nh
    chunks = [(off, min(MXU_WIDTH, fh - off)) for off in range(0, fh, MXU_WIDTH)]

    def body(x_ref, dxo_ref, g_ref, sil_ref, gp_ref, wg_ref, wu_ref, wd_ref,
             dx_ref, da_ref, db_ref, h_ref, df_ref, dg_ref, dh_ref):
        i, j = pl.program_id(0), pl.program_id(1)
        gain_v = g_ref[...]

        @pl.when(j == 0)
        def _():
            xv = x_ref[...]
            r = lax.rsqrt(jnp.mean(xv * xv, axis=-1, keepdims=True) + EPS)
            h_ref[...] = (xv * r * gain_v).astype(BF16)
            df_ref[...] = (FFN_RES * dxo_ref[...]).astype(BF16)
            dh_ref[...] = jnp.zeros_like(dh_ref)

        df = df_ref[...]
        for off, size in chunks:
            sl = slice(off, off + size)
            rows = pl.ds(pl.multiple_of(j * fh + off, LANES), size)
            ds = _dg(df, wd_ref[rows, :], NT)
            da_ref[:, sl] = (ds * gp_ref[:, sl].astype(F32)).astype(BF16)
            db_ref[:, sl] = (ds * sil_ref[:, sl].astype(F32)).astype(BF16)
        half_rows = pl.ds(pl.multiple_of(j * fh, LANES), fh)
        dh_ref[...] += _dot(da_ref[...], wg_ref[half_rows, :]) + _dot(db_ref[...], wu_ref[half_rows, :])

        @pl.when(j == nh - 1)
        def _():
            xv = x_ref[...]
            r = lax.rsqrt(jnp.mean(xv * xv, axis=-1, keepdims=True) + EPS)
            dx, dgain = _rmsnorm_bwd_rows(xv, r, gain_v, dh_ref[...])
            dx_ref[...] = dxo_ref[...] + dx

            @pl.when(i == 0)
            def _():
                dg_ref[...] = jnp.zeros_like(dg_ref)

            dg_ref[...] += dgain

    tok = lambda: pl.BlockSpec((tm, d), lambda i, j: (i, 0))
    half = lambda: pl.BlockSpec((tm, fh), lambda i, j: (i, j))
    return pl.pallas_call(
        _drop_deps(body, 8, len(deps)), name=name, grid=(t // tm, nh),
        in_specs=[tok(), tok(), _resident((1, d)), half(), half(),
                  _resident((f, d)), _resident((f, d)), _resident((f, d))] + [ANY_SPEC] * len(deps),
        out_specs=[tok(), half(), half(), tok(), tok(), pl.BlockSpec((1, d), lambda i, j: (0, 0))],
        out_shape=[jax.ShapeDtypeStruct((t, d), F32), jax.ShapeDtypeStruct((t, f), BF16),
                   jax.ShapeDtypeStruct((t, f), BF16), jax.ShapeDtypeStruct((t, d), BF16),
                   jax.ShapeDtypeStruct((t, d), BF16), jax.ShapeDtypeStruct((1, d), F32)],
        scratch_shapes=[pltpu.VMEM((tm, d), F32)],
        compiler_params=_params(("arbitrary", "arbitrary")),
    )(x, dxo, gain, sil, gp, wg_t, wu_t, wd, *deps)


def _row_tile(m, cap):
    if m <= cap:
        return m
    best = None
    for bm in range(LANES, cap + 1, LANES):
        if m % bm == 0:
            best = bm
    return best if best is not None else m


def grad_tn(a, b, scale, name, deps=()):
    t, m = a.shape
    n = b.shape[1]
    bm = _row_tile(m, 1408)
    bk = min(t, 2048)
    nk = t // bk

    def body(a_ref, b_ref, o_ref, acc_ref):
        k = pl.program_id(1)

        @pl.when(k == 0)
        def _():
            acc_ref[...] = jnp.zeros_like(acc_ref)

        acc_ref[...] += _dg(a_ref[...].astype(BF16), b_ref[...].astype(BF16), TN)

        @pl.when(k == nk - 1)
        def _():
            o_ref[...] = (scale * acc_ref[...]).astype(BF16)

    return pl.pallas_call(
        _drop_deps(body, 2, len(deps)), name=name, grid=(m // bm, nk),
        in_specs=[pl.BlockSpec((bk, bm), lambda i, k: (k, i)), pl.BlockSpec((bk, n), lambda i, k: (k, 0))]
        + [ANY_SPEC] * len(deps),
        out_specs=pl.BlockSpec((bm, n), lambda i, k: (i, 0)),
        out_shape=jax.ShapeDtypeStruct((m, n), BF16),
        scratch_shapes=[pltpu.VMEM((bm, n), F32)],
        compiler_params=_params(("arbitrary", "arbitrary")),
    )(a, b, *deps)


def grad_nn(a_list, b, name):
    t, n = b.shape
    ms = [a.shape[0] for a in a_list]
    m = sum(ms)
    bk = min(t, 1024)
    nk = t // bk
    na = len(a_list)

    def body(*refs):
        a_refs, b_ref, o_ref, acc_ref = refs[:na], refs[na], refs[na + 1], refs[na + 2]
        k = pl.program_id(0)

        @pl.when(k == 0)
        def _():
            acc_ref[...] = jnp.zeros_like(acc_ref)

        bv = b_ref[...].astype(BF16)
        off = 0
        for a_ref, mi in zip(a_refs, ms):
            acc_ref[off:off + mi, :] += _dot(a_ref[...].astype(BF16), bv)
            off += mi

        @pl.when(k == nk - 1)
        def _():
            o_ref[...] = acc_ref[...].astype(BF16)

    return pl.pallas_call(
        body, name=name, grid=(nk,),
        in_specs=[pl.BlockSpec((mi, bk), lambda k: (0, k)) for mi in ms] + [pl.BlockSpec((bk, n), lambda k: (k, 0))],
        out_specs=pl.BlockSpec((m, n), lambda k: (0, 0)),
        out_shape=jax.ShapeDtypeStruct((m, n), BF16),
        scratch_shapes=[pltpu.VMEM((m, n), F32)],
        compiler_params=_params(("arbitrary",)),
    )(*a_list, b)


def mix_in_fwd(x, gain, win_t, b_in_col, name):
    t, d = x.shape
    w = win_t.shape[0]
    tm = _token_tile(t)

    def body(x_ref, g_ref, w_ref, bias_ref, qkv_ref, zg_ref):
        xv = x_ref[...]
        r = lax.rsqrt(jnp.mean(xv * xv, axis=-1, keepdims=True) + EPS)
        h = (xv * r * g_ref[...]).astype(BF16)
        qkv_ref[...] = (_dg(w_ref[:QKV_WIDTH, :], h, NT) + bias_ref[:QKV_WIDTH, :]).astype(BF16)
        zg_ref[...] = (_dg(w_ref[QKV_WIDTH:, :], h, NT) + bias_ref[QKV_WIDTH:, :]).astype(BF16)

    return pl.pallas_call(
        body, name=name, grid=(t // tm,),
        in_specs=[pl.BlockSpec((tm, d), lambda i: (i, 0)), _resident((1, d)), _resident((w, d)), _resident((w, 1))],
        out_specs=[pl.BlockSpec((QKV_WIDTH, tm), lambda i: (0, i)), pl.BlockSpec((w - QKV_WIDTH, tm), lambda i: (0, i))],
        out_shape=[jax.ShapeDtypeStruct((QKV_WIDTH, t), BF16), jax.ShapeDtypeStruct((w - QKV_WIDTH, t), BF16)],
        compiler_params=_params(("arbitrary",)),
    )(x, gain, win_t, b_in_col)


ATTN_REP = N_Q_HEADS // N_KV_HEADS
ATTN_BLOCKS_PER_STEP = 4


def _attn_tiles(t):
    nbs = min(ATTN_BLOCKS_PER_STEP, t // BLOCK)
    return nbs, nbs * BLOCK, t // (nbs * BLOCK)


def _attn_specs(nbs, tiles):
    last = tiles - 1
    cur = lambda n: jnp.minimum(n, last)
    prev = lambda n: jnp.maximum(jnp.minimum(n, last) * nbs - 1, 0)
    k_row = ATTN_WIDTH // KV_WIDTH
    tw = nbs * BLOCK
    return [
        pl.BlockSpec((ATTN_WIDTH, tw), lambda n: (0, cur(n))),
        pl.BlockSpec((KV_WIDTH, BLOCK), lambda n: (k_row, prev(n))),
        pl.BlockSpec((KV_WIDTH, tw), lambda n: (k_row, cur(n))),
        pl.BlockSpec((KV_WIDTH, BLOCK), lambda n: (k_row + 1, prev(n))),
        pl.BlockSpec((KV_WIDTH, tw), lambda n: (k_row + 1, cur(n))),
    ]


def _cols(b):
    return slice(b * BLOCK, (b + 1) * BLOCK)


def _band_rows(prev_ref, tile_ref, gs, b):
    before = prev_ref[gs, :] if b == 0 else tile_ref[gs, _cols(b - 1)]
    return before, tile_ref[gs, _cols(b)]


def _group_rows(ref, g, b):
    first = g * ATTN_REP
    return jnp.concatenate([ref[(first + r) * HEAD_DIM:(first + r + 1) * HEAD_DIM, _cols(b)] for r in range(ATTN_REP)],
                           axis=1)


def _ungroup_rows(ref, g, b, val):
    first = g * ATTN_REP
    for r in range(ATTN_REP):
        ref[(first + r) * HEAD_DIM:(first + r + 1) * HEAD_DIM, _cols(b)] = val[:, r * BLOCK:(r + 1) * BLOCK]


def _attn_upper():
    key = lax.broadcasted_iota(jnp.int32, (BLOCK, ATTN_REP * BLOCK), 0)
    qry = lax.broadcasted_iota(jnp.int32, (BLOCK, ATTN_REP * BLOCK), 1) & (BLOCK - 1)
    return key > qry


def _attn_probs(q, kp, kc, sink_ref, g, upper, no_prev):
    s = jnp.where(upper, _dg(kp, q, TN), _dg(kc, q, TN)) * ATTN_SCALE
    if no_prev is not None:
        s = jnp.where(upper & no_prev, MASK_VALUE, s)
    sink = jnp.concatenate([jnp.full((1, BLOCK), sink_ref[0, g * ATTN_REP + r], F32) for r in range(ATTN_REP)], axis=1)
    m = jnp.maximum(jnp.max(s, axis=0, keepdims=True), sink)
    e = jnp.exp(s - m)
    es = jnp.exp(sink - m)
    inv = 1.0 / (jnp.sum(e, axis=0, keepdims=True) + es)
    return e * inv, es * inv


def _split_band(upper, val):
    return jnp.where(upper, val, 0.0).astype(BF16), jnp.where(upper, 0.0, val).astype(BF16)


def attn_fwd(qkv_t, sinks, name):
    t = qkv_t.shape[1]
    nbs, tw, tiles = _attn_tiles(t)

    def body(sink_ref, q_ref, kp_ref, kc_ref, vp_ref, vc_ref, y_ref):
        n = pl.program_id(0)
        upper = _attn_upper()
        for b in range(nbs):
            for g in range(N_KV_HEADS):
                gs = slice(g * HEAD_DIM, (g + 1) * HEAD_DIM)
                kp, kc = _band_rows(kp_ref, kc_ref, gs, b)
                vp, vc = _band_rows(vp_ref, vc_ref, gs, b)
                p, _ = _attn_probs(_group_rows(q_ref, g, b), kp, kc, sink_ref, g, upper, n == 0 if b == 0 else None)
                pp, pc = _split_band(upper, p)
                _ungroup_rows(y_ref, g, b, (_dot(vp, pp) + _dot(vc, pc)).astype(BF16))

    return pl.pallas_call(
        body, name=name, grid=(tiles,),
        in_specs=[pl.BlockSpec(memory_space=pltpu.SMEM)] + _attn_specs(nbs, tiles),
        out_specs=pl.BlockSpec((ATTN_WIDTH, tw), lambda n: (0, n)),
        out_shape=jax.ShapeDtypeStruct((ATTN_WIDTH, t), BF16),
        compiler_params=_params(("arbitrary",)),
    )(sinks, qkv_t, qkv_t, qkv_t, qkv_t, qkv_t)


def attn_bwd(qkv_t, dy_t, sinks, name):
    t = qkv_t.shape[1]
    nbs, tw, tiles = _attn_tiles(t)
    kept = slice(0, tw - BLOCK)

    def body(sink_ref, q_ref, kp_ref, kc_ref, vp_ref, vc_ref, do_ref, dq_ref, dk_ref, dv_ref, dsink_ref,
             ck_ref, cv_ref, sacc_ref):
        n = pl.program_id(0)

        @pl.when(n == 0)
        def _():
            sacc_ref[...] = jnp.zeros_like(sacc_ref)

        @pl.when((n > 0) & (n < tiles))
        def _():
            if nbs > 1:
                dk_ref[:, kept] = ck_ref[:, kept].astype(BF16)
                dv_ref[:, kept] = cv_ref[:, kept].astype(BF16)

        @pl.when(n < tiles)
        def _():
            upper = _attn_upper()
            for b in range(nbs):
                for g in range(N_KV_HEADS):
                    gs = slice(g * HEAD_DIM, (g + 1) * HEAD_DIM)
                    kp, kc = _band_rows(kp_ref, kc_ref, gs, b)
                    vp, vc = _band_rows(vp_ref, vc_ref, gs, b)
                    q = _group_rows(q_ref, g, b)
                    do = _group_rows(do_ref, g, b)
                    p, ps = _attn_probs(q, kp, kc, sink_ref, g, upper, n == 0 if b == 0 else None)
                    dp = jnp.where(upper, _dg(vp, do, TN), _dg(vc, do, TN))
                    delta = jnp.sum(p * dp, axis=0, keepdims=True)
                    dsink = -(ps * delta)
                    for r in range(ATTN_REP):
                        hd = g * ATTN_REP + r
                        sacc_ref[hd:hd + 1, :] += dsink[:, r * BLOCK:(r + 1) * BLOCK]
                    dsp, dsc = _split_band(upper, p * (dp - delta))
                    pp, pc = _split_band(upper, p)
                    _ungroup_rows(dq_ref, g, b, (ATTN_SCALE * (_dot(kp, dsp) + _dot(kc, dsc))).astype(BF16))
                    dk_before = ATTN_SCALE * _dg(q, dsp, NT)
                    dv_before = _dg(do, pp, NT)
                    if b == 0:
                        @pl.when(n > 0)
                        def _():
                            dk_ref[gs, _cols(nbs - 1)] = (ck_ref[gs, _cols(nbs - 1)] + dk_before).astype(BF16)
                            dv_ref[gs, _cols(nbs - 1)] = (cv_ref[gs, _cols(nbs - 1)] + dv_before).astype(BF16)
                    else:
                        ck_ref[gs, _cols(b - 1)] += dk_before
                        cv_ref[gs, _cols(b - 1)] += dv_before
                    ck_ref[gs, _cols(b)] = ATTN_SCALE * _dg(q, dsc, NT)
                    cv_ref[gs, _cols(b)] = _dg(do, pc, NT)

        @pl.when(n == tiles)
        def _():
            dk_ref[...] = ck_ref[...].astype(BF16)
            dv_ref[...] = cv_ref[...].astype(BF16)
            dsink_ref[...] = jnp.sum(sacc_ref[...], axis=1, keepdims=True)

    last = tiles - 1
    done = lambda n: jnp.maximum(n - 1, 0)
    outs = pl.pallas_call(
        body, name=name, grid=(tiles + 1,),
        in_specs=[pl.BlockSpec(memory_space=pltpu.SMEM)] + _attn_specs(nbs, tiles)
        + [pl.BlockSpec((ATTN_WIDTH, tw), lambda n: (0, jnp.minimum(n, last)))],
        out_specs=[pl.BlockSpec((ATTN_WIDTH, tw), lambda n: (0, jnp.minimum(n, last))),
                   pl.BlockSpec((KV_WIDTH, tw), lambda n: (0, done(n))),
                   pl.BlockSpec((KV_WIDTH, tw), lambda n: (0, done(n))),
                   pl.BlockSpec((N_Q_HEADS, 1), lambda n: (0, 0))],
        out_shape=[jax.ShapeDtypeStruct((ATTN_WIDTH, t), BF16), jax.ShapeDtypeStruct((KV_WIDTH, t), BF16),
                   jax.ShapeDtypeStruct((KV_WIDTH, t), BF16), jax.ShapeDtypeStruct((N_Q_HEADS, 1), F32)],
        scratch_shapes=[pltpu.VMEM((KV_WIDTH, tw), F32), pltpu.VMEM((KV_WIDTH, tw), F32),
                        pltpu.VMEM((N_Q_HEADS, BLOCK), F32)],
        compiler_params=_params(("arbitrary",)),
    )(sinks, qkv_t, qkv_t, qkv_t, qkv_t, qkv_t, dy_t)
    return outs


GELU_C = 0.7978845608028654
GELU_A = 0.044715


def _gelu(x):
    return 0.5 * x * (1.0 + jnp.tanh(GELU_C * (x + GELU_A * x * x * x)))


def _gelu_grad(x):
    th = jnp.tanh(GELU_C * (x + GELU_A * x * x * x))
    return 0.5 * (1.0 + th) + 0.5 * x * (1.0 - th * th) * GELU_C * (1.0 + 3.0 * GELU_A * x * x)


def _gmlp_parts(zg, lng, lnb):
    z = _gelu(zg)
    u = z[:GMLP_WIDTH, :]
    vv = z[GMLP_WIDTH:, :]
    mu = jnp.mean(vv, axis=0, keepdims=True)
    xc = vv - mu
    rstd = lax.rsqrt(jnp.mean(xc * xc, axis=0, keepdims=True) + EPS)
    xhat = xc * rstd
    return u, xhat, rstd, xhat * lng + lnb


def _causal(w):
    row = lax.broadcasted_iota(jnp.int32, (BLOCK, BLOCK), 0)
    col = lax.broadcasted_iota(jnp.int32, (BLOCK, BLOCK), 1)
    return jnp.where(col <= row, w, 0.0)


GMLP_CHUNKS_PER_STEP = 4


def _stack_chunks(v, nc):
    return jnp.concatenate([v[:, c * BLOCK:(c + 1) * BLOCK] for c in range(nc)], axis=0)


def _unstack_chunks(v, nc):
    rows = v.shape[0] // nc
    return jnp.concatenate([v[c * rows:(c + 1) * rows, :] for c in range(nc)], axis=1)


def gmlp_fwd(zg_t, lng, lnb, w_s, b_s, name):
    t = zg_t.shape[1]
    nc = min(GMLP_CHUNKS_PER_STEP, t // BLOCK)
    tw = nc * BLOCK

    def body(zg_ref, lng_ref, lnb_ref, w_ref, bs_ref, y_ref):
        u, _, _, vn = _gmlp_parts(zg_ref[...].astype(F32), lng_ref[...], lnb_ref[...])
        for g in range(GMLP_GROUPS):
            gs = slice(g * GMLP_GROUP_DIM, (g + 1) * GMLP_GROUP_DIM)
            wc = _causal(w_ref[g]).astype(BF16)
            mixed = _dg(_stack_chunks(vn[gs, :].astype(BF16), nc), wc, NT) + bs_ref[g:g + 1, :]
            y_ref[gs, :] = (u[gs, :] * _unstack_chunks(mixed, nc)).astype(BF16)

    return pl.pallas_call(
        body, name=name, grid=(t // tw,),
        in_specs=[pl.BlockSpec((2 * GMLP_WIDTH, tw), lambda n: (0, n)), _resident((GMLP_WIDTH, 1)),
                  _resident((GMLP_WIDTH, 1)), _resident((GMLP_GROUPS, BLOCK, BLOCK)), _resident((GMLP_GROUPS, BLOCK))],
        out_specs=pl.BlockSpec((GMLP_WIDTH, tw), lambda n: (0, n)),
        out_shape=jax.ShapeDtypeStruct((GMLP_WIDTH, t), BF16),
        compiler_params=_params(("arbitrary",)),
    )(zg_t, lng, lnb, w_s, b_s)


def gmlp_bwd(zg_t, dy_t, lng, lnb, w_s, b_s, name):
    t = zg_t.shape[1]
    nc = min(GMLP_CHUNKS_PER_STEP, t // BLOCK)
    tw = nc * BLOCK
    nt = t // tw

    def body(zg_ref, dy_ref, lng_ref, lnb_ref, w_ref, bs_ref, dzg_ref, dw_ref, dbs_ref, dlng_ref, dlnb_ref,
             gacc_ref, bacc_ref):
        n = pl.program_id(0)

        @pl.when(n == 0)
        def _():
            dw_ref[...] = jnp.zeros_like(dw_ref)
            dbs_ref[...] = jnp.zeros_like(dbs_ref)
            gacc_ref[...] = jnp.zeros_like(gacc_ref)
            bacc_ref[...] = jnp.zeros_like(bacc_ref)

        zg = zg_ref[...].astype(F32)
        lng_v = lng_ref[...]
        u, xhat, rstd, vn = _gmlp_parts(zg, lng_v, lnb_ref[...])
        dy = dy_ref[...].astype(F32)
        dvn_parts = []
        for g in range(GMLP_GROUPS):
            gs = slice(g * GMLP_GROUP_DIM, (g + 1) * GMLP_GROUP_DIM)
            wc = _causal(w_ref[g]).astype(BF16)
            vn_s = _stack_chunks(vn[gs, :].astype(BF16), nc)
            mixed = _unstack_chunks(_dg(vn_s, wc, NT) + bs_ref[g:g + 1, :], nc)
            dy_g = dy[gs, :]
            dmixed = dy_g * u[gs, :]
            dm_s = _stack_chunks(dmixed.astype(BF16), nc)
            dzg_ref[gs, :] = (dy_g * mixed * _gelu_grad(zg[gs, :])).astype(BF16)
            dw_ref[g] += _causal(_dg(dm_s, vn_s, TN))
            dbs_ref[g:g + 1, :] += _lane_fold(jnp.sum(dmixed, axis=0, keepdims=True))
            dvn_parts.append(_unstack_chunks(_dot(dm_s, wc), nc))
        dvn = jnp.concatenate(dvn_parts, axis=0)
        gacc_ref[...] += _lane_fold(dvn * xhat)
        bacc_ref[...] += _lane_fold(dvn)
        dxhat = dvn * lng_v
        m1 = jnp.mean(dxhat, axis=0, keepdims=True)
        m2 = jnp.mean(dxhat * xhat, axis=0, keepdims=True)
        dvv = rstd * (dxhat - m1 - xhat * m2)
        dzg_ref[GMLP_WIDTH:, :] = (dvv * _gelu_grad(zg[GMLP_WIDTH:, :])).astype(BF16)

        @pl.when(n == nt - 1)
        def _():
            dlng_ref[...] = jnp.sum(gacc_ref[...], axis=1, keepdims=True)
            dlnb_ref[...] = jnp.sum(bacc_ref[...], axis=1, keepdims=True)

    const2 = lambda n: (0, 0)
    return pl.pallas_call(
        body, name=name, grid=(nt,),
        in_specs=[pl.BlockSpec((2 * GMLP_WIDTH, tw), lambda n: (0, n)), pl.BlockSpec((GMLP_WIDTH, tw), lambda n: (0, n)),
                  _resident((GMLP_WIDTH, 1)), _resident((GMLP_WIDTH, 1)),
                  _resident((GMLP_GROUPS, BLOCK, BLOCK)), _resident((GMLP_GROUPS, BLOCK))],
        out_specs=[pl.BlockSpec((2 * GMLP_WIDTH, tw), lambda n: (0, n)),
                   pl.BlockSpec((GMLP_GROUPS, BLOCK, BLOCK), lambda n: (0, 0, 0)),
                   pl.BlockSpec((GMLP_GROUPS, BLOCK), const2),
                   pl.BlockSpec((GMLP_WIDTH, 1), const2), pl.BlockSpec((GMLP_WIDTH, 1), const2)],
        out_shape=[jax.ShapeDtypeStruct((2 * GMLP_WIDTH, t), BF16),
                   jax.ShapeDtypeStruct((GMLP_GROUPS, BLOCK, BLOCK), F32),
                   jax.ShapeDtypeStruct((GMLP_GROUPS, BLOCK), F32),
                   jax.ShapeDtypeStruct((GMLP_WIDTH, 1), F32), jax.ShapeDtypeStruct((GMLP_WIDTH, 1), F32)],
        scratch_shapes=[pltpu.VMEM((GMLP_WIDTH, BLOCK), F32), pltpu.VMEM((GMLP_WIDTH, BLOCK), F32)],
        compiler_params=_params(("arbitrary",)),
    )(zg_t, dy_t, lng, lnb, w_s, b_s)


def _rmsnorm_cols(y, gain_col):
    r = lax.rsqrt(jnp.mean(y * y, axis=0, keepdims=True) + EPS)
    n = y * r
    return n, r, n * gain_col


def mix_out_fwd(x, ya_t, yg_t, gao, ggo, wout, b_out, name):
    t, d = x.shape
    tm = _token_tile(t)

    def body(x_ref, ya_ref, yg_ref, gao_ref, ggo_ref, w_ref, b_ref, o_ref):
        _, _, ya = _rmsnorm_cols(ya_ref[...].astype(F32), gao_ref[...])
        _, _, yg = _rmsnorm_cols(yg_ref[...].astype(F32), ggo_ref[...])
        o_ref[...] = (x_ref[...] + _dg(ya.astype(BF16), w_ref[:ATTN_WIDTH, :], TN)
                      + _dg(yg.astype(BF16), w_ref[ATTN_WIDTH:, :], TN) + b_ref[...])

    return pl.pallas_call(
        body, name=name, grid=(t // tm,),
        in_specs=[pl.BlockSpec((tm, d), lambda i: (i, 0)), pl.BlockSpec((ATTN_WIDTH, tm), lambda i: (0, i)),
                  pl.BlockSpec((GMLP_WIDTH, tm), lambda i: (0, i)), _resident((ATTN_WIDTH, 1)), _resident((GMLP_WIDTH, 1)),
                  _resident((ATTN_WIDTH + GMLP_WIDTH, d)), _resident((1, d))],
        out_specs=pl.BlockSpec((tm, d), lambda i: (i, 0)),
        out_shape=jax.ShapeDtypeStruct((t, d), F32),
        compiler_params=_params(("arbitrary",)),
    )(x, ya_t, yg_t, gao, ggo, wout, b_out)


def _lane_fold(v):
    out = v[:, :LANES]
    for j in range(1, v.shape[1] // LANES):
        out = out + v[:, j * LANES:(j + 1) * LANES]
    return out


def mix_out_bwd(dx, ya_t, yg_t, gao, ggo, wout, name, deps=()):
    t, d = dx.shape
    tm = _token_tile(t)
    nt = t // tm

    def body(dx_ref, ya_ref, yg_ref, gao_ref, ggo_ref, w_ref, dya_ref, dyg_ref, y_ref, dgao_ref, dggo_ref, db_ref,
             acc_ref):
        i = pl.program_id(0)

        @pl.when(i == 0)
        def _():
            acc_ref[...] = jnp.zeros_like(acc_ref)
            db_ref[...] = jnp.zeros_like(db_ref)

        dxv = dx_ref[...]
        db_ref[...] += jnp.sum(dxv, axis=0, keepdims=True)
        dxb = dxv.astype(BF16)
        for part, (src_ref, gain_ref, dst_ref) in enumerate(((ya_ref, gao_ref, dya_ref), (yg_ref, ggo_ref, dyg_ref))):
            rows = slice(part * ATTN_WIDTH, (part + 1) * ATTN_WIDTH)
            gain = gain_ref[...]
            nrm, r, yn = _rmsnorm_cols(src_ref[...].astype(F32), gain)
            y_ref[rows, :] = yn.astype(BF16)
            dy = _dg(w_ref[rows, :], dxb, NT)
            acc_ref[rows, :] += _lane_fold(dy * nrm)
            dn = dy * gain
            dst_ref[...] = (r * (dn - nrm * jnp.mean(dn * nrm, axis=0, keepdims=True))).astype(BF16)

        @pl.when(i == nt - 1)
        def _():
            dgao_ref[...] = jnp.sum(acc_ref[:ATTN_WIDTH, :], axis=1, keepdims=True)
            dggo_ref[...] = jnp.sum(acc_ref[ATTN_WIDTH:, :], axis=1, keepdims=True)

    const2 = lambda i: (0, 0)
    feat = lambda w: pl.BlockSpec((w, tm), lambda i: (0, i))
    return pl.pallas_call(
        _drop_deps(body, 6, len(deps)), name=name, grid=(nt,),
        in_specs=[pl.BlockSpec((tm, d), lambda i: (i, 0)), feat(ATTN_WIDTH), feat(GMLP_WIDTH),
                  _resident((ATTN_WIDTH, 1)), _resident((GMLP_WIDTH, 1)), _resident((ATTN_WIDTH + GMLP_WIDTH, d))]
        + [ANY_SPEC] * len(deps),
        out_specs=[feat(ATTN_WIDTH), feat(GMLP_WIDTH), feat(ATTN_WIDTH + GMLP_WIDTH),
                   pl.BlockSpec((ATTN_WIDTH, 1), const2), pl.BlockSpec((GMLP_WIDTH, 1), const2),
                   pl.BlockSpec((1, d), const2)],
        out_shape=[jax.ShapeDtypeStruct((ATTN_WIDTH, t), BF16), jax.ShapeDtypeStruct((GMLP_WIDTH, t), BF16),
                   jax.ShapeDtypeStruct((ATTN_WIDTH + GMLP_WIDTH, t), BF16),
                   jax.ShapeDtypeStruct((ATTN_WIDTH, 1), F32), jax.ShapeDtypeStruct((GMLP_WIDTH, 1), F32),
                   jax.ShapeDtypeStruct((1, d), F32)],
        scratch_shapes=[pltpu.VMEM((ATTN_WIDTH + GMLP_WIDTH, LANES), F32)],
        compiler_params=_params(("arbitrary",)),
    )(dx, ya_t, yg_t, gao, ggo, wout, *deps)


def mix_in_bwd(x, dxo, gain, dq_t, dk_t, dv_t, dzg_t, win_t, name):
    t, d = x.shape
    w = win_t.shape[0]
    tm = _token_tile(t)
    nt = t // tm
    parts = ((0, ATTN_WIDTH), (ATTN_WIDTH, KV_WIDTH), (ATTN_WIDTH + KV_WIDTH, KV_WIDTH), (QKV_WIDTH, w - QKV_WIDTH))

    def body(x_ref, dxo_ref, g_ref, dq_ref, dk_ref, dv_ref, dzg_ref, w_ref, dx_ref, h_ref, dg_ref, db_ref, acc_ref):
        i = pl.program_id(0)

        @pl.when(i == 0)
        def _():
            acc_ref[...] = jnp.zeros_like(acc_ref)
            dg_ref[...] = jnp.zeros_like(dg_ref)

        xv = x_ref[...]
        gain_v = g_ref[...]
        r = lax.rsqrt(jnp.mean(xv * xv, axis=-1, keepdims=True) + EPS)
        h_ref[...] = (xv * r * gain_v).astype(BF16)
        dh = jnp.zeros((tm, d), F32)
        for (off, size), src_ref in zip(parts, (dq_ref, dk_ref, dv_ref, dzg_ref)):
            dp = src_ref[...]
            acc_ref[off:off + size, :] += _lane_fold(dp.astype(F32))
            dh = dh + _dg(dp, w_ref[off:off + size, :], TN)
        dx, dgain = _rmsnorm_bwd_rows(xv, r, gain_v, dh)
        dx_ref[...] = dxo_ref[...] + dx
        dg_ref[...] += dgain

        @pl.when(i == nt - 1)
        def _():
            db_ref[...] = jnp.sum(acc_ref[...], axis=1, keepdims=True)

    const2 = lambda i: (0, 0)
    tok = lambda: pl.BlockSpec((tm, d), lambda i: (i, 0))
    feat = lambda rows: pl.BlockSpec((rows, tm), lambda i: (0, i))
    return pl.pallas_call(
        body, name=name, grid=(nt,),
        in_specs=[tok(), tok(), _resident((1, d)), feat(ATTN_WIDTH), feat(KV_WIDTH), feat(KV_WIDTH),
                  feat(w - QKV_WIDTH), _resident((w, d))],
        out_specs=[tok(), tok(), pl.BlockSpec((1, d), const2), pl.BlockSpec((w, 1), const2)],
        out_shape=[jax.ShapeDtypeStruct((t, d), F32), jax.ShapeDtypeStruct((t, d), BF16),
                   jax.ShapeDtypeStruct((1, d), F32), jax.ShapeDtypeStruct((w, 1), F32)],
        scratch_shapes=[pltpu.VMEM((w, LANES), F32)],
        compiler_params=_params(("arbitrary",)),
    )(x, dxo, gain, dq_t, dk_t, dv_t, dzg_t, win_t)


def _adamw_math(w, g, m, v):
    m = ADAM_B1 * m + (1.0 - ADAM_B1) * g
    v = ADAM_B2 * v + (1.0 - ADAM_B2) * (g * g)
    m_hat = m / (1.0 - ADAM_B1 ** ADAM_STEP)
    v_hat = v / (1.0 - ADAM_B2 ** ADAM_STEP)
    delta = -ADAM_LR * (m_hat / (jnp.sqrt(v_hat) + ADAM_EPS) + ADAM_WD * w)
    return delta, m, v


def adamw(w, g, m, v, name):
    rows, cols = w.shape
    tr = rows
    if rows * cols > 256 * 1024:
        for cand in (256, 128, 64, 32, 16, 8):
            if rows % cand == 0:
                tr = cand
                break

    def body(w_ref, g_ref, m_ref, v_ref, d_ref, mo_ref, vo_ref):
        delta, mn, vn = _adamw_math(w_ref[...], g_ref[...], m_ref[...], v_ref[...])
        d_ref[...] = delta
        mo_ref[...] = mn
        vo_ref[...] = vn

    spec = pl.BlockSpec((tr, cols), lambda i: (i, 0))
    shape = jax.ShapeDtypeStruct((rows, cols), F32)
    return pl.pallas_call(
        body, name=name, grid=(rows // tr,),
        in_specs=[spec] * 4, out_specs=[spec] * 3, out_shape=[shape] * 3,
        compiler_params=_params(("arbitrary",)),
    )(w, g, m, v)


def sum_adamw(landing, grad3d, me, w, m, v, name):
    nd, rows, cols = landing.shape
    tr = rows // 2

    def body(me_ref, l_ref, own_ref, w_ref, m_ref, v_ref, g_ref, d_ref, mo_ref, vo_ref):
        g = own_ref[...].astype(F32)
        for j in range(nd):
            g = g + l_ref[j].astype(F32)
        delta, mn, vn = _adamw_math(w_ref[...], g, m_ref[...], v_ref[...])
        g_ref[...] = g
        d_ref[...] = delta
        mo_ref[...] = mn
        vo_ref[...] = vn

    spec = lambda: pl.BlockSpec((tr, cols), lambda i, me_ref: (i, 0))
    shape = jax.ShapeDtypeStruct((rows, cols), F32)
    return pl.pallas_call(
        body, name=name,
        grid_spec=pltpu.PrefetchScalarGridSpec(
            num_scalar_prefetch=1, grid=(rows // tr,),
            in_specs=[pl.BlockSpec((nd, tr, cols), lambda i, me_ref: (0, i, 0)),
                      pl.BlockSpec((None, tr, cols), lambda i, me_ref: (me_ref[0], i, 0)), spec(), spec(), spec()],
            out_specs=[spec() for _ in range(4)]),
        out_shape=[shape] * 4,
        compiler_params=_params(("arbitrary",)),
    )(me, landing, grad3d, w, m, v)


def _mesh_place():
    x, y, c = lax.axis_index("x"), lax.axis_index("y"), lax.axis_index("c")
    return x, y, c, 4 * x + 2 * y + c


def _peer(x, y, c, k):
    return (x ^ ((k >> 2) & 1), y ^ ((k >> 1) & 1), c ^ (k & 1))


def prep_shards(mats, me, name):
    na = len(mats)

    def body(me_ref, *refs):
        in_refs, shard_refs, land_refs = refs[:na], refs[na:2 * na], refs[2 * na:]
        for a in range(na):
            val = in_refs[a][...].astype(BF16)
            shard_refs[a][...] = val
            land_refs[a][...] = val

    whole = lambda mat: pl.BlockSpec(mat.shape, lambda i, me_ref: (0, 0))
    outs = pl.pallas_call(
        body, name=name,
        grid_spec=pltpu.PrefetchScalarGridSpec(
            num_scalar_prefetch=1, grid=(1,),
            in_specs=[whole(mat) for mat in mats],
            out_specs=[whole(mat) for mat in mats]
            + [pl.BlockSpec((None,) + mat.shape, lambda i, me_ref: (me_ref[0], 0, 0)) for mat in mats]),
        out_shape=[jax.ShapeDtypeStruct(mat.shape, BF16) for mat in mats]
        + [jax.ShapeDtypeStruct((N_DEV,) + mat.shape, BF16) for mat in mats],
        compiler_params=_params(("arbitrary",)),
    )(me, *mats)
    return outs[:na], outs[na:]


def allgather_rows(shards, lands, name):
    na = n_gather = len(shards)

    def body(*refs):
        in_refs, out_refs = refs[:na], refs[2 * na:3 * na]
        send_sems, recv_sems = refs[3 * na:]
        x, y, c, me = _mesh_place()

        def copy(sem, a, origin, src, k):
            return pltpu.make_async_remote_copy(
                src_ref=src, dst_ref=out_refs[a].at[origin], send_sem=send_sems.at[sem, a],
                recv_sem=recv_sems.at[sem, a], device_id=_peer(x, y, c, k), device_id_type=pl.DeviceIdType.MESH)

        kx, ky = 4, 2

        def recv(sem, a, origin):
            copy(sem, a, origin, in_refs[a], 1).wait_recv()

        def landed(a, origin):
            return out_refs[a].at[origin]

        @pl.when(c == 1)
        def _():
            sends = []

            def send(sem, a, origin, src, k):
                cp = copy(sem, a, origin, src, k)
                cp.start()
                sends.append(cp)

            for a in range(n_gather):
                send(1, a, me, in_refs[a], kx)
                send(2, a, me, in_refs[a], ky)
                send(0, a, me, in_refs[a], 1)
            for a in range(n_gather):
                recv(0, a, me ^ 1)
                send(3, a, me ^ 1, landed(a, me ^ 1), kx)
                send(4, a, me ^ 1, landed(a, me ^ 1), ky)
            for a in range(n_gather):
                recv(1, a, me ^ kx)
                send(5, a, me ^ kx, landed(a, me ^ kx), ky)
                send(7, a, me ^ kx, landed(a, me ^ kx), 1)
                recv(2, a, me ^ ky)
                send(8, a, me ^ ky, landed(a, me ^ ky), 1)
            for a in range(n_gather):
                recv(3, a, me ^ kx ^ 1)
                send(9, a, me ^ kx ^ 1, landed(a, me ^ kx ^ 1), 1)
                recv(4, a, me ^ ky ^ 1)
                send(6, a, me ^ ky ^ 1, landed(a, me ^ ky ^ 1), kx)
                send(10, a, me ^ ky ^ 1, landed(a, me ^ ky ^ 1), 1)
            for a in range(n_gather):
                recv(5, a, me ^ 6)
                send(11, a, me ^ 6, landed(a, me ^ 6), 1)
                recv(6, a, me ^ 7)
                send(12, a, me ^ 7, landed(a, me ^ 7), 1)
            for cp in sends:
                cp.wait_send()

        @pl.when(c == 0)
        def _():
            north = me ^ 1
            own = [copy(0, a, me, in_refs[a], 1) for a in range(n_gather)]
            for cp in own:
                cp.start()
            for a in range(n_gather):
                recv(0, a, north)
                for sem, origin in ((7, north ^ kx), (8, north ^ ky), (9, north ^ kx ^ 1), (10, north ^ ky ^ 1),
                                    (11, north ^ 6), (12, north ^ 7)):
                    recv(sem, a, origin)
            for cp in own:
                cp.wait_send()

    return pl.pallas_call(
        body, name=name,
        in_specs=[ANY_SPEC] * (2 * na), out_specs=[ANY_SPEC] * na,
        out_shape=[jax.ShapeDtypeStruct(land.shape, land.dtype) for land in lands],
        input_output_aliases={na + a: a for a in range(na)},
        scratch_shapes=[pltpu.SemaphoreType.DMA((13, n_gather)), pltpu.SemaphoreType.DMA((13, n_gather))],
    )(*shards, *lands)


HBM_SPEC = pl.BlockSpec(memory_space=pltpu.HBM)
SEM_SPEC = pl.BlockSpec(memory_space=pltpu.SEMAPHORE)
SIDE_EFFECT = pltpu.SideEffectType.DATAFLOW_SIDE_EFFECTING


def _hbm(v):
    return pltpu.with_memory_space_constraint(v, pltpu.HBM)


def gather_slices(src, land, me, k):
    return src, land.at[me], land.at[me ^ k]


def exchange_slices(src, land, me, k):
    return src.at[me ^ k], land.at[k - 1], land.at[k - 1]


def split_start(groups, slices, deps, name):
    sizes = [len(srcs) for srcs, _ in groups]
    flat_src = [s for srcs, _ in groups for s in srcs]
    flat_land = [l for _, lands in groups for l in lands]
    na, ng, nd = len(flat_src), len(groups), len(deps)

    def body(*refs):
        src_refs, land_refs = refs[:na], refs[na:2 * na]
        sem_refs = refs[2 * na + nd:2 * na + nd + 2 * ng]
        token_ref = refs[-1]
        x, y, c, me = _mesh_place()
        a0 = 0
        for gi, size in enumerate(sizes):
            send_sems, recv_sems = sem_refs[2 * gi], sem_refs[2 * gi + 1]
            for k in range(1, N_DEV):
                for j in range(size):
                    src, dst, _ = slices(src_refs[a0 + j], land_refs[a0 + j], me, k)
                    pltpu.make_async_remote_copy(
                        src_ref=src, dst_ref=dst, send_sem=send_sems.at[(k - 1) * size + j],
                        recv_sem=recv_sems.at[(k - 1) * size + j],
                        device_id=_peer(x, y, c, k), device_id_type=pl.DeviceIdType.MESH).start()
            a0 += size
        token_ref[...] = jnp.zeros_like(token_ref)

    sems = [pltpu.SemaphoreType.DMA(((N_DEV - 1) * size,)) for size in sizes for _ in range(2)]
    thru = [pltpu.HBM(v.shape, v.dtype) for v in flat_src + flat_land]
    outs = pl.pallas_call(
        body, name=name,
        in_specs=[HBM_SPEC] * (2 * na) + [ANY_SPEC] * nd,
        out_specs=[SEM_SPEC] * (2 * ng) + [HBM_SPEC] * (2 * na) + [pl.BlockSpec(memory_space=pltpu.VMEM)],
        out_shape=sems + thru + [jax.ShapeDtypeStruct((8, LANES), F32)],
        input_output_aliases={i: 2 * ng + i for i in range(2 * na)},
        compiler_params=pltpu.CompilerParams(has_side_effects=SIDE_EFFECT),
    )(*[_hbm(v) for v in flat_src + flat_land], *deps)
    started, a0 = [], 0
    for gi, size in enumerate(sizes):
        srcs = outs[2 * ng + a0:2 * ng + a0 + size]
        lands = outs[2 * ng + na + a0:2 * ng + na + a0 + size]
        started.append((outs[2 * gi], outs[2 * gi + 1], list(srcs), list(lands)))
        a0 += size
    return started, outs[-1]


def split_wait(started, slices, after, name):
    send_sems, recv_sems, srcs, lands = started
    na = len(srcs)

    def body(*refs):
        src_refs, land_refs = refs[:na], refs[na:2 * na]
        send_ref, recv_ref = refs[2 * na], refs[2 * na + 1]
        x, y, c, me = _mesh_place()
        for k in range(1, N_DEV):
            for j in range(na):
                src, _, got = slices(src_refs[j], land_refs[j], me, k)
                cp = pltpu.make_async_remote_copy(
                    src_ref=src, dst_ref=got, send_sem=send_ref.at[(k - 1) * na + j], recv_sem=recv_ref.at[(k - 1) * na + j],
                    device_id=_peer(x, y, c, k), device_id_type=pl.DeviceIdType.MESH)
                cp.wait_send()
                cp.wait_recv()

    outs = pl.pallas_call(
        body, name=name,
        in_specs=[HBM_SPEC] * (2 * na) + [SEM_SPEC, SEM_SPEC] + [ANY_SPEC] * len(after),
        out_specs=[HBM_SPEC] * (2 * na),
        out_shape=[pltpu.HBM(v.shape, v.dtype) for v in srcs + lands],
        input_output_aliases={i: i for i in range(2 * na)},
        compiler_params=pltpu.CompilerParams(has_side_effects=SIDE_EFFECT),
    )(*srcs, *lands, send_sems, recv_sems, *after)
    return list(outs[:na]), list(outs[na:])


def place_own(pack, me, name):
    rows, cols = pack.shape

    def body(me_ref, p_ref, o_ref):
        o_ref[...] = p_ref[...]

    return pl.pallas_call(
        body, name=name,
        grid_spec=pltpu.PrefetchScalarGridSpec(
            num_scalar_prefetch=1, grid=(1,),
            in_specs=[pl.BlockSpec((rows, cols), lambda i, me_ref: (0, 0))],
            out_specs=pl.BlockSpec((None, rows, cols), lambda i, me_ref: (me_ref[0], 0, 0))),
        out_shape=jax.ShapeDtypeStruct((N_DEV, rows, cols), F32),
        compiler_params=_params(("arbitrary",)),
    )(me, pack)


def sum_slots(slots, name):
    nd, rows, cols = slots.shape

    def body(s_ref, o_ref):
        acc = s_ref[0]
        for j in range(1, nd):
            acc = acc + s_ref[j]
        o_ref[...] = acc

    return pl.pallas_call(
        body, name=name, grid=(1,),
        in_specs=[pl.BlockSpec((nd, rows, cols), lambda i: (0, 0, 0))],
        out_specs=pl.BlockSpec((rows, cols), lambda i: (0, 0)),
        out_shape=jax.ShapeDtypeStruct((rows, cols), F32),
        compiler_params=_params(("arbitrary",)),
    )(slots)


def _pack_rows(size):
    rows = -(-size // LANES)
    return -(-rows // 8) * 8


def pack_small(parts):
    out = []
    for p in parts:
        flat = p.reshape(-1).astype(F32)
        rows = _pack_rows(flat.shape[0])
        out.append(jnp.pad(flat, (0, rows * LANES - flat.shape[0])).reshape(rows, LANES))
    return jnp.concatenate(out, axis=0)


def unpack_small(pack, shapes):
    out, off = [], 0
    for shp in shapes:
        size = 1
        for s in shp:
            size *= s
        rows = _pack_rows(size)
        out.append(pack[off:off + rows].reshape(-1)[:size].reshape(shp))
        off += rows
    return out


def local_step(x, target, small, get_w, put_g, put_small):
    col = lambda v: v.reshape(-1, 1)
    (wg1, wu1, wd1), deps = get_w(("wg1", "wu1", "wd1"), ())
    x1, sil1, gp1, s1 = ffn_fwd(x, small["ffn1_norm_g"], wg1, wu1, wd1, "ffn1_fwd", deps)
    (win, wout), _ = get_w(("win", "wout"), (x1,))
    qkv_t, zg_t = mix_in_fwd(x1, small["mix_norm_g"], win, col(small["b_in"]), "mix_in_fwd")
    ya_t = attn_fwd(qkv_t, small["attn_sinks"], "attn_fwd")
    lng, lnb = col(small["gmlp_ln_g"]), col(small["gmlp_ln_b"])
    w_s, b_s = small["gmlp_w_s"][0], small["gmlp_b_s"][0]
    yg_t = gmlp_fwd(zg_t, lng, lnb, w_s, b_s, "gmlp_fwd")
    gao, ggo = col(small["attn_out_norm_g"]), col(small["gmlp_out_norm_g"])
    x2 = mix_out_fwd(x1, ya_t, yg_t, gao, ggo, wout, small["b_out"], "mix_out_fwd")
    (wg2, wu2, wd2), _ = get_w(("wg2", "wu2", "wd2"), (x2,))
    gs = {}
    dx3, sil2, gp2, s2, gs["final_norm_g"], loss = ffn_fwd(
        x2, small["ffn2_norm_g"], wg2, wu2, wd2, "ffn2_fwd", head=(target, small["final_norm_g"].reshape(1, -1)))

    dx2, da, db, h, df, gs["ffn2_norm_g"] = ffn_bwd(x2, dx3, small["ffn2_norm_g"], sil2, gp2, wg2, wu2, wd2, "ffn2_bwd")
    deps = put_g({"wg2": grad_tn(da, h, 1.0, "ffn2_dwg"), "wu2": grad_tn(db, h, 1.0, "ffn2_dwu"),
                  "wd2": grad_tn(s2, df, 1.0, "ffn2_dwd")})

    dya_t, dyg_t, y_t, dgao, dggo, gs["b_out"] = mix_out_bwd(dx2, ya_t, yg_t, gao, ggo, wout, "mix_out_bwd", deps)
    gs["attn_out_norm_g"], gs["gmlp_out_norm_g"] = dgao.reshape(1, -1), dggo.reshape(1, -1)
    g_wout = grad_nn([y_t], dx2, "dwout")
    dzg_t, dws, dbs, dlng, dlnb = gmlp_bwd(zg_t, dyg_t, lng, lnb, w_s, b_s, "gmlp_bwd")
    gs["gmlp_w_s"], gs["gmlp_b_s"] = dws[None], dbs[None]
    gs["gmlp_ln_g"], gs["gmlp_ln_b"] = dlng.reshape(1, -1), dlnb.reshape(1, -1)
    dq_t, dk_t, dv_t, dsinks = attn_bwd(qkv_t, dya_t, small["attn_sinks"], "attn_bwd")
    gs["attn_sinks"] = dsinks.reshape(1, -1)
    dx1, h2, gs["mix_norm_g"], dbin = mix_in_bwd(x1, dx2, small["mix_norm_g"], dq_t, dk_t, dv_t, dzg_t, win,
                                                 "mix_in_bwd")
    gs["b_in"] = dbin.reshape(1, -1)
    deps = put_g({"wout": g_wout, "win": grad_nn([dq_t, dk_t, dv_t, dzg_t], h2, "dwin")})

    dx0, da, db, h, df, gs["ffn1_norm_g"] = ffn_bwd(x, dx1, small["ffn1_norm_g"], sil1, gp1, wg1, wu1, wd1, "ffn1_bwd",
                                                deps)
    deps = put_small(gs, loss)
    deps = put_g({"wg1": grad_tn(da, h, 1.0, "ffn1_dwg", deps)})
    deps = put_g({"wu1": grad_tn(db, h, 1.0, "ffn1_dwu", deps)})
    put_g({"wd1": grad_tn(s1, df, 1.0, "ffn1_dwd", deps)})
    return dx0


SMALL_NAMES = ["ffn1_norm_g", "mix_norm_g", "b_in", "attn_sinks", "gmlp_ln_g", "gmlp_ln_b", "gmlp_w_s", "gmlp_b_s",
               "attn_out_norm_g", "gmlp_out_norm_g", "b_out", "ffn2_norm_g", "final_norm_g"]
BIG_NAMES = {"wg1": ("ffn1_w_gate", True), "wu1": ("ffn1_w_up", True), "wd1": ("ffn1_w_down", False),
             "win": ("w_in", True), "wout": ("w_out", False),
             "wg2": ("ffn2_w_gate", True), "wu2": ("ffn2_w_up", True), "wd2": ("ffn2_w_down", False)}
WEIGHT_ORDER = ["ffn1_norm_g", "ffn1_w_gate", "ffn1_w_up", "ffn1_w_down", "mix_norm_g", "w_in", "b_in", "attn_sinks",
                "gmlp_ln_g", "gmlp_ln_b", "gmlp_w_s", "gmlp_b_s", "attn_out_norm_g", "gmlp_out_norm_g", "w_out",
                "b_out", "ffn2_norm_g", "ffn2_w_gate", "ffn2_w_up", "ffn2_w_down", "final_norm_g"]


def kernel(x, ffn1_norm_g, ffn1_w_gate, ffn1_w_up, ffn1_w_down, mix_norm_g, w_in, b_in, attn_sinks, gmlp_ln_g, gmlp_ln_b, gmlp_w_s, gmlp_b_s, attn_out_norm_g, gmlp_out_norm_g, w_out, b_out, ffn2_norm_g, ffn2_w_gate, ffn2_w_up, ffn2_w_down, final_norm_g, loss_target, m_ffn1_norm_g, m_ffn1_w_gate, m_ffn1_w_up, m_ffn1_w_down, m_mix_norm_g, m_w_in, m_b_in, m_attn_sinks, m_gmlp_ln_g, m_gmlp_ln_b, m_gmlp_w_s, m_gmlp_b_s, m_attn_out_norm_g, m_gmlp_out_norm_g, m_w_out, m_b_out, m_ffn2_norm_g, m_ffn2_w_gate, m_ffn2_w_up, m_ffn2_w_down, m_final_norm_g, v_ffn1_norm_g, v_ffn1_w_gate, v_ffn1_w_up, v_ffn1_w_down, v_mix_norm_g, v_w_in, v_b_in, v_attn_sinks, v_gmlp_ln_g, v_gmlp_ln_b, v_gmlp_w_s, v_gmlp_b_s, v_attn_out_norm_g, v_gmlp_out_norm_g, v_w_out, v_b_out, v_ffn2_norm_g, v_ffn2_w_gate, v_ffn2_w_up, v_ffn2_w_down, v_final_norm_g):
    args = dict(locals())
    w = {n: args[n] for n in WEIGHT_ORDER}
    m = {n: args["m_" + n] for n in WEIGHT_ORDER}
    v = {n: args["v_" + n] for n in WEIGHT_ORDER}

    d_model = x.shape[-1]
    flat = lambda g: g.reshape(-1, d_model)
    me = _mesh_place()[3].astype(jnp.int32).reshape(1)
    first = ("wg1", "wu1", "wd1")
    later = (("win", "wout"), ("wg2", "wu2", "wd2"))
    gathers, exchanges, last_token = {}, [], []

    order = list(first) + [k for grp in later for k in grp]
    mats = [w[BIG_NAMES[k][0]][0].T if BIG_NAMES[k][1] else w[BIG_NAMES[k][0]][0] for k in order]
    shard_list, land_list = prep_shards(mats, me, "prep_shards")
    shards, zones = dict(zip(order, shard_list)), dict(zip(order, land_list))

    def get_w(keys, after):
        if keys == first:
            full = allgather_rows([shards[k] for k in first], [zones[k] for k in first], "allgather_ffn1")
            started, token = split_start([([shards[k] for k in grp], [zones[k] for k in grp]) for grp in later],
                                         gather_slices, (full[0],), "gather_start")
            gathers.update(zip(later, started))
            return [flat(g) for g in full], (token,)
        _, lands = split_wait(gathers[keys], gather_slices, after, "gather_wait_" + keys[0])
        return [flat(land) for land in lands], ()

    def put_g(gb):
        keys = tuple(gb)
        g3 = [gb[k].reshape(N_DEV, -1, d_model) for k in keys]
        lands = [lax.empty((N_DEV - 1,) + g.shape[1:], BF16) for g in g3]
        started, token = split_start([(g3, lands)], exchange_slices, (), "exchange_start_" + keys[0])
        exchanges.append((keys, started[0]))
        last_token[:] = [token]
        return (token,)

    small_started = []

    def put_small(gs, loss):
        pack = pack_small([gs[n] for n in SMALL_NAMES] + [loss[:, :1]])
        started, token = split_start([([pack], [place_own(pack, me, "place_small")])], gather_slices, (), "small_start")
        small_started[:] = started
        return (token,)

    small = {n: w[n] for n in SMALL_NAMES}
    grad_x = local_step(x[0], loss_target[0], small, get_w, put_g, put_small)

    grads, deltas, new_m, new_v = {}, {}, {}, {}
    after = tuple(last_token)
    for keys, started in exchanges:
        g3, lands = split_wait(started, exchange_slices, after, "exchange_wait_" + keys[0])
        for key, own, land in zip(keys, g3, lands):
            pname, transposed = BIG_NAMES[key]
            to_rows = (lambda p: p[0].T) if transposed else (lambda p: p[0])
            from_rows = (lambda r: r.T[None]) if transposed else (lambda r: r[None])
            g, d_, m_, v_ = sum_adamw(land, own, me, to_rows(w[pname]), to_rows(m[pname]), to_rows(v[pname]),
                                      "adamw_" + key)
            after = (g,)
            grads[pname], deltas[pname], new_m[pname], new_v[pname] = (from_rows(g), from_rows(d_), from_rows(m_),
                                                                        from_rows(v_))

    shapes = [w[n].shape for n in SMALL_NAMES]
    _, (slots,) = split_wait(small_started[0], gather_slices, after, "small_wait")
    total = sum_slots(slots, "sum_small")
    small_g = unpack_small(total, shapes + [(1, 1)])
    loss_total = small_g[-1].reshape(())
    gpack = total[:sum(_pack_rows(w[n].size) for n in SMALL_NAMES)]
    d_, m_, v_ = adamw(pack_small([w[n] for n in SMALL_NAMES]), gpack, pack_small([m[n] for n in SMALL_NAMES]),
                       pack_small([v[n] for n in SMALL_NAMES]), "adamw_small")
    for n, g_, dd, mm, vv in zip(SMALL_NAMES, small_g[:-1], unpack_small(d_, shapes), unpack_small(m_, shapes),
                                 unpack_small(v_, shapes)):
        grads[n], deltas[n], new_m[n], new_v[n] = g_, dd, mm, vv

    return (loss_total, grad_x[None], *[grads[n] for n in WEIGHT_ORDER], *[deltas[n] for n in WEIGHT_ORDER],
            *[new_m[n] for n in WEIGHT_ORDER], *[new_v[n] for n in WEIGHT_ORDER])
```

```python
import functools

import jax
import jax.numpy as jnp
from jax import lax
from jax.experimental import pallas as pl
from jax.experimental.pallas import tpu as pltpu

F32 = jnp.float32
BF16 = jnp.bfloat16

N_DEV = 8
N_Q_HEADS = 8
N_KV_HEADS = 2
HEAD_DIM = 64
ATTN_WIDTH = N_Q_HEADS * HEAD_DIM
KV_WIDTH = N_KV_HEADS * HEAD_DIM
QKV_WIDTH = ATTN_WIDTH + 2 * KV_WIDTH
BLOCK = 128
GMLP_GROUPS = 8
GMLP_GROUP_DIM = 64
GMLP_WIDTH = GMLP_GROUPS * GMLP_GROUP_DIM
EPS = 1e-6
FFN_RES = 0.5
ATTN_SCALE = HEAD_DIM ** -0.5
MASK_VALUE = -1e30

ADAM_LR = 0.001
ADAM_B1 = 0.9
ADAM_B2 = 0.999
ADAM_EPS = 1e-08
ADAM_WD = 0.01
ADAM_STEP = 10

V7X_VMEM_BYTES = 64 * 1024 * 1024
VMEM_LIMIT = 62 * 1024 * 1024
LANES = 128
MXU_WIDTH = 256

NT = (((1,), (1,)), ((), ()))
TN = (((0,), (0,)), ((), ()))


def _dot(a, b):
    return jnp.dot(a, b, preferred_element_type=F32)


def _dg(a, b, dims):
    return lax.dot_general(a, b, dims, preferred_element_type=F32)


def _params(semantics=None):
    return pltpu.CompilerParams(dimension_semantics=semantics, vmem_limit_bytes=VMEM_LIMIT)


def _resident(shape):
    zeros = (0,) * len(shape)
    return pl.BlockSpec(shape, lambda *_: zeros, pipeline_mode=pl.Buffered(1))


def _drop_deps(body, n_in, n_deps):
    return lambda *refs: body(*refs[:n_in], *refs[n_in + n_deps:])


ANY_SPEC = pl.BlockSpec(memory_space=pl.ANY)


def _token_tile(t):
    return min(t, 512)


def _f_chunk(f):
    for c in (256, 128):
        if f % c == 0:
            return c
    return f


def _loss_head(xv, target, gain):
    r = lax.rsqrt(jnp.mean(xv * xv, axis=-1, keepdims=True) + EPS)
    err = xv * r * gain - target
    loss = 0.5 * jnp.sum(jnp.mean(err * err, axis=-1, keepdims=True), axis=0, keepdims=True)
    dx, dgain = _rmsnorm_bwd_rows(xv, r, gain, err * (1.0 / xv.shape[-1]))
    return dx, dgain, loss


def ffn_fwd(x, gain, wg_t, wu_t, wd, name, deps=(), head=None):
    t, d = x.shape
    f = wd.shape[0]
    tm = _token_tile(t)
    fc = _f_chunk(f)
    n_in = 5 if head is None else 7

    def body(x_ref, g_ref, wg_ref, wu_ref, wd_ref, *rest):
        xo_ref, sil_ref, gp_ref, s_ref, h_ref = rest[n_in - 5:n_in]
        xv = x_ref[...]
        r = lax.rsqrt(jnp.mean(xv * xv, axis=-1, keepdims=True) + EPS)
        h = (xv * r * g_ref[...]).astype(BF16)
        h_ref[...] = h
        for c in range(f // fc):
            sl = slice(c * fc, (c + 1) * fc)
            a = _dg(h, wg_ref[sl, :], NT)
            b = _dg(h, wu_ref[sl, :], NT)
            sig = jax.nn.sigmoid(a)
            sil = a * sig
            sil_ref[:, sl] = sil.astype(BF16)
            gp_ref[:, sl] = (b * (sig + sil - sil * sig)).astype(BF16)
            s_ref[:, sl] = (sil * b).astype(BF16)
        xo = xv + FFN_RES * _dot(s_ref[...], wd_ref[...])
        if head is None:
            xo_ref[...] = xo
        else:
            t_ref, hg_ref = rest[:2]
            dg_ref, loss_ref = rest[-2:]

            @pl.when(pl.program_id(0) == 0)
            def _():
                dg_ref[...] = jnp.zeros_like(dg_ref)
                loss_ref[...] = jnp.zeros_like(loss_ref)

            dx, dgain, loss = _loss_head(xo, t_ref[...], hg_ref[...])
            xo_ref[...] = dx
            dg_ref[...] += dgain
            loss_ref[...] += loss

    tok = lambda: pl.BlockSpec((tm, d), lambda i: (i, 0))
    wide = lambda: pl.BlockSpec((tm, f), lambda i: (i, 0))
    const2 = lambda i: (0, 0)
    in_specs = [tok(), _resident((1, d)), _resident((f, d)), _resident((f, d)), _resident((f, d))]
    out_specs = [tok(), wide(), wide(), wide(), tok()]
    out_shape = ([jax.ShapeDtypeStruct((t, d), F32)] + [jax.ShapeDtypeStruct((t, f), BF16)] * 3
                 + [jax.ShapeDtypeStruct((t, d), BF16)])
    operands = [x, gain, wg_t, wu_t, wd]
    if head is not None:
        in_specs += [tok(), _resident((1, d))]
        out_specs += [pl.BlockSpec((1, d), const2), pl.BlockSpec((1, LANES), const2)]
        out_shape += [jax.ShapeDtypeStruct((1, d), F32), jax.ShapeDtypeStruct((1, LANES), F32)]
        operands += list(head)
    return pl.pallas_call(
        _drop_deps(body, n_in, len(deps)), name=name, grid=(t // tm,),
        in_specs=in_specs + [ANY_SPEC] * len(deps), out_specs=out_specs, out_shape=out_shape,
        compiler_params=_params(("arbitrary",)),
    )(*operands, *deps)


def _rmsnorm_bwd_rows(xv, r, gain, dh):
    n = xv * r
    dgain = jnp.sum(dh * n, axis=0, keepdims=True)
    dn = dh * gain
    dx = r * (dn - n * jnp.mean(dn * n, axis=-1, keepdims=True))
    return dx, dgain


def ffn_bwd(x, dxo, gain, sil, gp, wg_t, wu_t, wd, name, deps=()):
    t, d = x.shape
    f = wd.shape[0]
    tm = _token_tile(t)
    nh = 2 if f % (2 * LANES) == 0 else 1
    fh = f // nh
    chunks = [(off, min(MXU_WIDTH, fh - off)) for off in range(0, fh, MXU_WIDTH)]

    def body(x_ref, dxo_ref, g_ref, sil_ref, gp_ref, wg_ref, wu_ref, wd_ref,
             dx_ref, da_ref, db_ref, df_ref, dg_ref, dh_ref):
        i, j = pl.program_id(0), pl.program_id(1)
        gain_v = g_ref[...]

        @pl.when(j == 0)
        def _():
            df_ref[...] = (FFN_RES * dxo_ref[...]).astype(BF16)

        df = df_ref[...]
        for off, size in chunks:
            sl = slice(off, off + size)
            rows = pl.ds(pl.multiple_of(j * fh + off, LANES), size)
            ds = _dg(df, wd_ref[rows, :], NT)
            da_ref[:, sl] = (ds * gp_ref[:, sl].astype(F32)).astype(BF16)
            db_ref[:, sl] = (ds * sil_ref[:, sl].astype(F32)).astype(BF16)
        half_rows = pl.ds(pl.multiple_of(j * fh, LANES), fh)
        part = _dot(da_ref[...], wg_ref[half_rows, :]) + _dot(db_ref[...], wu_ref[half_rows, :])

        @pl.when(j == 0)
        def _():
            dh_ref[...] = part

        @pl.when(j > 0)
        def _():
            dh_ref[...] += part

        @pl.when(j == nh - 1)
        def _():
            xv = x_ref[...]
            r = lax.rsqrt(jnp.mean(xv * xv, axis=-1, keepdims=True) + EPS)
            dx, dgain = _rmsnorm_bwd_rows(xv, r, gain_v, dh_ref[...])
            dx_ref[...] = dxo_ref[...] + dx

            @pl.when(i == 0)
            def _():
                dg_ref[...] = jnp.zeros_like(dg_ref)

            dg_ref[...] += dgain

    tok = lambda: pl.BlockSpec((tm, d), lambda i, j: (i, 0))
    half = lambda: pl.BlockSpec((tm, fh), lambda i, j: (i, j))
    return pl.pallas_call(
        _drop_deps(body, 8, len(deps)), name=name, grid=(t // tm, nh),
        in_specs=[tok(), tok(), _resident((1, d)), half(), half(),
                  _resident((f, d)), _resident((f, d)), _resident((f, d))] + [ANY_SPEC] * len(deps),
        out_specs=[tok(), half(), half(), tok(), pl.BlockSpec((1, d), lambda i, j: (0, 0))],
        out_shape=[jax.ShapeDtypeStruct((t, d), F32), jax.ShapeDtypeStruct((t, f), BF16),
                   jax.ShapeDtypeStruct((t, f), BF16), jax.ShapeDtypeStruct((t, d), BF16),
                   jax.ShapeDtypeStruct((1, d), F32)],
        scratch_shapes=[pltpu.VMEM((tm, d), F32)],
        compiler_params=_params(("arbitrary", "arbitrary")),
    )(x, dxo, gain, sil, gp, wg_t, wu_t, wd, *deps)


def _row_tile(m, cap):
    if m <= cap:
        return m
    best = None
    for bm in range(LANES, cap + 1, LANES):
        if m % bm == 0:
            best = bm
    return best if best is not None else m


def grad_tn(a, b, scale, name, deps=()):
    t, m = a.shape
    n = b.shape[1]
    bm = _row_tile(m, 1408)
    bk = min(t, 2048)
    nk = t // bk

    def body(a_ref, b_ref, o_ref, acc_ref):
        k = pl.program_id(1)

        @pl.when(k == 0)
        def _():
            acc_ref[...] = jnp.zeros_like(acc_ref)

        acc_ref[...] += _dg(a_ref[...].astype(BF16), b_ref[...].astype(BF16), TN)

        @pl.when(k == nk - 1)
        def _():
            o_ref[...] = (scale * acc_ref[...]).astype(BF16)

    return pl.pallas_call(
        _drop_deps(body, 2, len(deps)), name=name, grid=(m // bm, nk),
        in_specs=[pl.BlockSpec((bk, bm), lambda i, k: (k, i)), pl.BlockSpec((bk, n), lambda i, k: (k, 0))]
        + [ANY_SPEC] * len(deps),
        out_specs=pl.BlockSpec((bm, n), lambda i, k: (i, 0)),
        out_shape=jax.ShapeDtypeStruct((m, n), BF16),
        scratch_shapes=[pltpu.VMEM((bm, n), F32)],
        compiler_params=_params(("arbitrary", "arbitrary")),
    )(a, b, *deps)


def grad_nn(a_list, b, name):
    t, n = b.shape
    ms = [a.shape[0] for a in a_list]
    m = sum(ms)
    bk = min(t, 1024)
    nk = t // bk
    na = len(a_list)

    def body(*refs):
        a_refs, b_ref, o_ref, acc_ref = refs[:na], refs[na], refs[na + 1], refs[na + 2]
        k = pl.program_id(0)

        @pl.when(k == 0)
        def _():
            acc_ref[...] = jnp.zeros_like(acc_ref)

        bv = b_ref[...].astype(BF16)
        off = 0
        for a_ref, mi in zip(a_refs, ms):
            acc_ref[off:off + mi, :] += _dot(a_ref[...].astype(BF16), bv)
            off += mi

        @pl.when(k == nk - 1)
        def _():
            o_ref[...] = acc_ref[...].astype(BF16)

    return pl.pallas_call(
        body, name=name, grid=(nk,),
        in_specs=[pl.BlockSpec((mi, bk), lambda k: (0, k)) for mi in ms] + [pl.BlockSpec((bk, n), lambda k: (k, 0))],
        out_specs=pl.BlockSpec((m, n), lambda k: (0, 0)),
        out_shape=jax.ShapeDtypeStruct((m, n), BF16),
        scratch_shapes=[pltpu.VMEM((m, n), F32)],
        compiler_params=_params(("arbitrary",)),
    )(*a_list, b)


def mix_in_fwd(x, gain, win_t, b_in_col, name):
    t, d = x.shape
    w = win_t.shape[0]
    tm = _token_tile(t)

    def body(x_ref, g_ref, w_ref, bias_ref, qkv_ref, zg_ref):
        xv = x_ref[...]
        r = lax.rsqrt(jnp.mean(xv * xv, axis=-1, keepdims=True) + EPS)
        h = (xv * r * g_ref[...]).astype(BF16)
        qkv_ref[...] = (_dg(w_ref[:QKV_WIDTH, :], h, NT) + bias_ref[:QKV_WIDTH, :]).astype(BF16)
        zg_ref[...] = (_dg(w_ref[QKV_WIDTH:, :], h, NT) + bias_ref[QKV_WIDTH:, :]).astype(BF16)

    return pl.pallas_call(
        body, name=name, grid=(t // tm,),
        in_specs=[pl.BlockSpec((tm, d), lambda i: (i, 0)), _resident((1, d)), _resident((w, d)), _resident((w, 1))],
        out_specs=[pl.BlockSpec((QKV_WIDTH, tm), lambda i: (0, i)), pl.BlockSpec((w - QKV_WIDTH, tm), lambda i: (0, i))],
        out_shape=[jax.ShapeDtypeStruct((QKV_WIDTH, t), BF16), jax.ShapeDtypeStruct((w - QKV_WIDTH, t), BF16)],
        compiler_params=_params(("arbitrary",)),
    )(x, gain, win_t, b_in_col)


ATTN_REP = N_Q_HEADS // N_KV_HEADS
ATTN_BLOCKS_PER_STEP = 4


def _attn_tiles(t):
    nbs = min(ATTN_BLOCKS_PER_STEP, t // BLOCK)
    return nbs, nbs * BLOCK, t // (nbs * BLOCK)


def _attn_specs(nbs, tiles):
    last = tiles - 1
    cur = lambda n: jnp.minimum(n, last)
    prev = lambda n: jnp.maximum(jnp.minimum(n, last) * nbs - 1, 0)
    k_row = ATTN_WIDTH // KV_WIDTH
    tw = nbs * BLOCK
    return [
        pl.BlockSpec((ATTN_WIDTH, tw), lambda n: (0, cur(n))),
        pl.BlockSpec((KV_WIDTH, BLOCK), lambda n: (k_row, prev(n))),
        pl.BlockSpec((KV_WIDTH, tw), lambda n: (k_row, cur(n))),
        pl.BlockSpec((KV_WIDTH, BLOCK), lambda n: (k_row + 1, prev(n))),
        pl.BlockSpec((KV_WIDTH, tw), lambda n: (k_row + 1, cur(n))),
    ]


def _cols(b):
    return slice(b * BLOCK, (b + 1) * BLOCK)


def _band_rows(prev_ref, tile_ref, gs, b):
    before = prev_ref[gs, :] if b == 0 else tile_ref[gs, _cols(b - 1)]
    return before, tile_ref[gs, _cols(b)]


def _group_rows(ref, g, b):
    first = g * ATTN_REP
    return jnp.concatenate([ref[(first + r) * HEAD_DIM:(first + r + 1) * HEAD_DIM, _cols(b)] for r in range(ATTN_REP)],
                           axis=1)


def _ungroup_rows(ref, g, b, val):
    first = g * ATTN_REP
    for r in range(ATTN_REP):
        ref[(first + r) * HEAD_DIM:(first + r + 1) * HEAD_DIM, _cols(b)] = val[:, r * BLOCK:(r + 1) * BLOCK]


def _attn_upper():
    key = lax.broadcasted_iota(jnp.int32, (BLOCK, ATTN_REP * BLOCK), 0)
    qry = lax.broadcasted_iota(jnp.int32, (BLOCK, ATTN_REP * BLOCK), 1) & (BLOCK - 1)
    return key > qry


def _attn_probs(q, kp, kc, sink_ref, g, upper, no_prev):
    s = jnp.where(upper, _dg(kp, q, TN), _dg(kc, q, TN)) * ATTN_SCALE
    if no_prev is not None:
        s = jnp.where(upper & no_prev, MASK_VALUE, s)
    sink = jnp.concatenate([jnp.full((1, BLOCK), sink_ref[0, g * ATTN_REP + r], F32) for r in range(ATTN_REP)], axis=1)
    m = jnp.maximum(jnp.max(s, axis=0, keepdims=True), sink)
    e = jnp.exp(s - m)
    es = jnp.exp(sink - m)
    inv = 1.0 / (jnp.sum(e, axis=0, keepdims=True) + es)
    return e * inv, es * inv


def _split_band(upper, val):
    return jnp.where(upper, val, 0.0).astype(BF16), jnp.where(upper, 0.0, val).astype(BF16)


def attn_fwd(qkv_t, sinks, name):
    t = qkv_t.shape[1]
    nbs, tw, tiles = _attn_tiles(t)

    def body(sink_ref, q_ref, kp_ref, kc_ref, vp_ref, vc_ref, y_ref):
        n = pl.program_id(0)
        upper = _attn_upper()
        for b in range(nbs):
            for g in range(N_KV_HEADS):
                gs = slice(g * HEAD_DIM, (g + 1) * HEAD_DIM)
                kp, kc = _band_rows(kp_ref, kc_ref, gs, b)
                vp, vc = _band_rows(vp_ref, vc_ref, gs, b)
                p, _ = _attn_probs(_group_rows(q_ref, g, b), kp, kc, sink_ref, g, upper, n == 0 if b == 0 else None)
                pp, pc = _split_band(upper, p)
                _ungroup_rows(y_ref, g, b, (_dot(vp, pp) + _dot(vc, pc)).astype(BF16))

    return pl.pallas_call(
        body, name=name, grid=(tiles,),
        in_specs=[pl.BlockSpec(memory_space=pltpu.SMEM)] + _attn_specs(nbs, tiles),
        out_specs=pl.BlockSpec((ATTN_WIDTH, tw), lambda n: (0, n)),
        out_shape=jax.ShapeDtypeStruct((ATTN_WIDTH, t), BF16),
        compiler_params=_params(("arbitrary",)),
    )(sinks, qkv_t, qkv_t, qkv_t, qkv_t, qkv_t)


def attn_bwd(qkv_t, dy_t, sinks, name):
    t = qkv_t.shape[1]
    nbs, tw, tiles = _attn_tiles(t)
    kept = slice(0, tw - BLOCK)

    def body(sink_ref, q_ref, kp_ref, kc_ref, vp_ref, vc_ref, do_ref, dq_ref, dk_ref, dv_ref, dsink_ref,
             ck_ref, cv_ref, sacc_ref):
        n = pl.program_id(0)

        @pl.when(n == 0)
        def _():
            sacc_ref[...] = jnp.zeros_like(sacc_ref)

        @pl.when((n > 0) & (n < tiles))
        def _():
            if nbs > 1:
                dk_ref[:, kept] = ck_ref[:, kept].astype(BF16)
                dv_ref[:, kept] = cv_ref[:, kept].astype(BF16)

        @pl.when(n < tiles)
        def _():
            upper = _attn_upper()
            for b in range(nbs):
                for g in range(N_KV_HEADS):
                    gs = slice(g * HEAD_DIM, (g + 1) * HEAD_DIM)
                    kp, kc = _band_rows(kp_ref, kc_ref, gs, b)
                    vp, vc = _band_rows(vp_ref, vc_ref, gs, b)
                    q = _group_rows(q_ref, g, b)
                    do = _group_rows(do_ref, g, b)
                    p, ps = _attn_probs(q, kp, kc, sink_ref, g, upper, n == 0 if b == 0 else None)
                    dp = jnp.where(upper, _dg(vp, do, TN), _dg(vc, do, TN))
                    delta = jnp.sum(p * dp, axis=0, keepdims=True)
                    dsink = -(ps * delta)
                    for r in range(ATTN_REP):
                        hd = g * ATTN_REP + r
                        sacc_ref[hd:hd + 1, :] += dsink[:, r * BLOCK:(r + 1) * BLOCK]
                    dsp, dsc = _split_band(upper, p * (dp - delta))
                    pp, pc = _split_band(upper, p)
                    _ungroup_rows(dq_ref, g, b, (ATTN_SCALE * (_dot(kp, dsp) + _dot(kc, dsc))).astype(BF16))
                    dk_before = ATTN_SCALE * _dg(q, dsp, NT)
                    dv_before = _dg(do, pp, NT)
                    if b == 0:
                        @pl.when(n > 0)
                        def _():
                            dk_ref[gs, _cols(nbs - 1)] = (ck_ref[gs, _cols(nbs - 1)] + dk_before).astype(BF16)
                            dv_ref[gs, _cols(nbs - 1)] = (cv_ref[gs, _cols(nbs - 1)] + dv_before).astype(BF16)
                    else:
                        ck_ref[gs, _cols(b - 1)] += dk_before
                        cv_ref[gs, _cols(b - 1)] += dv_before
                    ck_ref[gs, _cols(b)] = ATTN_SCALE * _dg(q, dsc, NT)
                    cv_ref[gs, _cols(b)] = _dg(do, pc, NT)

        @pl.when(n == tiles)
        def _():
            dk_ref[...] = ck_ref[...].astype(BF16)
            dv_ref[...] = cv_ref[...].astype(BF16)
            dsink_ref[...] = jnp.sum(sacc_ref[...], axis=1, keepdims=True)

    last = tiles - 1
    done = lambda n: jnp.maximum(n - 1, 0)
    outs = pl.pallas_call(
        body, name=name, grid=(tiles + 1,),
        in_specs=[pl.BlockSpec(memory_space=pltpu.SMEM)] + _attn_specs(nbs, tiles)
        + [pl.BlockSpec((ATTN_WIDTH, tw), lambda n: (0, jnp.minimum(n, last)))],
        out_specs=[pl.BlockSpec((ATTN_WIDTH, tw), lambda n: (0, jnp.minimum(n, last))),
                   pl.BlockSpec((KV_WIDTH, tw), lambda n: (0, done(n))),
                   pl.BlockSpec((KV_WIDTH, tw), lambda n: (0, done(n))),
                   pl.BlockSpec((N_Q_HEADS, 1), lambda n: (0, 0))],
        out_shape=[jax.ShapeDtypeStruct((ATTN_WIDTH, t), BF16), jax.ShapeDtypeStruct((KV_WIDTH, t), BF16),
                   jax.ShapeDtypeStruct((KV_WIDTH, t), BF16), jax.ShapeDtypeStruct((N_Q_HEADS, 1), F32)],
        scratch_shapes=[pltpu.VMEM((KV_WIDTH, tw), F32), pltpu.VMEM((KV_WIDTH, tw), F32),
                        pltpu.VMEM((N_Q_HEADS, BLOCK), F32)],
        compiler_params=_params(("arbitrary",)),
    )(sinks, qkv_t, qkv_t, qkv_t, qkv_t, qkv_t, dy_t)
    return outs


GELU_C = 0.7978845608028654
GELU_A = 0.044715


def _gelu(x):
    return 0.5 * x * (1.0 + jnp.tanh(GELU_C * (x + GELU_A * x * x * x)))


def _gelu_grad(x):
    th = jnp.tanh(GELU_C * (x + GELU_A * x * x * x))
    return 0.5 * (1.0 + th) + 0.5 * x * (1.0 - th * th) * GELU_C * (1.0 + 3.0 * GELU_A * x * x)


def _gmlp_parts(zg, lng, lnb):
    z = _gelu(zg)
    u = z[:GMLP_WIDTH, :]
    vv = z[GMLP_WIDTH:, :]
    mu = jnp.mean(vv, axis=0, keepdims=True)
    xc = vv - mu
    rstd = lax.rsqrt(jnp.mean(xc * xc, axis=0, keepdims=True) + EPS)
    xhat = xc * rstd
    return u, xhat, rstd, xhat * lng + lnb


def _causal(w):
    row = lax.broadcasted_iota(jnp.int32, (BLOCK, BLOCK), 0)
    col = lax.broadcasted_iota(jnp.int32, (BLOCK, BLOCK), 1)
    return jnp.where(col <= row, w, 0.0)


GMLP_CHUNKS_PER_STEP = 4


def _stack_chunks(v, nc):
    return jnp.concatenate([v[:, c * BLOCK:(c + 1) * BLOCK] for c in range(nc)], axis=0)


def _unstack_chunks(v, nc):
    rows = v.shape[0] // nc
    return jnp.concatenate([v[c * rows:(c + 1) * rows, :] for c in range(nc)], axis=1)


def gmlp_fwd(zg_t, lng, lnb, w_s, b_s, name):
    t = zg_t.shape[1]
    nc = min(GMLP_CHUNKS_PER_STEP, t // BLOCK)
    tw = nc * BLOCK

    def body(zg_ref, lng_ref, lnb_ref, w_ref, bs_ref, y_ref):
        u, _, _, vn = _gmlp_parts(zg_ref[...].astype(F32), lng_ref[...], lnb_ref[...])
        for g in range(GMLP_GROUPS):
            gs = slice(g * GMLP_GROUP_DIM, (g + 1) * GMLP_GROUP_DIM)
            wc = _causal(w_ref[g]).astype(BF16)
            mixed = _dg(_stack_chunks(vn[gs, :].astype(BF16), nc), wc, NT) + bs_ref[g:g + 1, :]
            y_ref[gs, :] = (u[gs, :] * _unstack_chunks(mixed, nc)).astype(BF16)

    return pl.pallas_call(
        body, name=name, grid=(t // tw,),
        in_specs=[pl.BlockSpec((2 * GMLP_WIDTH, tw), lambda n: (0, n)), _resident((GMLP_WIDTH, 1)),
                  _resident((GMLP_WIDTH, 1)), _resident((GMLP_GROUPS, BLOCK, BLOCK)), _resident((GMLP_GROUPS, BLOCK))],
        out_specs=pl.BlockSpec((GMLP_WIDTH, tw), lambda n: (0, n)),
        out_shape=jax.ShapeDtypeStruct((GMLP_WIDTH, t), BF16),
        compiler_params=_params(("arbitrary",)),
    )(zg_t, lng, lnb, w_s, b_s)


def gmlp_bwd(zg_t, dy_t, lng, lnb, w_s, b_s, name):
    t = zg_t.shape[1]
    nc = min(GMLP_CHUNKS_PER_STEP, t // BLOCK)
    tw = nc * BLOCK
    nt = t // tw

    def body(zg_ref, dy_ref, lng_ref, lnb_ref, w_ref, bs_ref, dzg_ref, dw_ref, dbs_ref, dlng_ref, dlnb_ref,
             gacc_ref, bacc_ref):
        n = pl.program_id(0)

        @pl.when(n == 0)
        def _():
            dw_ref[...] = jnp.zeros_like(dw_ref)
            dbs_ref[...] = jnp.zeros_like(dbs_ref)
            gacc_ref[...] = jnp.zeros_like(gacc_ref)
            bacc_ref[...] = jnp.zeros_like(bacc_ref)

        zg = zg_ref[...].astype(F32)
        lng_v = lng_ref[...]
        u, xhat, rstd, vn = _gmlp_parts(zg, lng_v, lnb_ref[...])
        dy = dy_ref[...].astype(F32)
        dvn_parts = []
        for g in range(GMLP_GROUPS):
            gs = slice(g * GMLP_GROUP_DIM, (g + 1) * GMLP_GROUP_DIM)
            wc = _causal(w_ref[g]).astype(BF16)
            vn_s = _stack_chunks(vn[gs, :].astype(BF16), nc)
            mixed = _unstack_chunks(_dg(vn_s, wc, NT) + bs_ref[g:g + 1, :], nc)
            dy_g = dy[gs, :]
            dmixed = dy_g * u[gs, :]
            dm_s = _stack_chunks(dmixed.astype(BF16), nc)
            dzg_ref[gs, :] = (dy_g * mixed * _gelu_grad(zg[gs, :])).astype(BF16)
            dw_ref[g] += _causal(_dg(dm_s, vn_s, TN))
            dbs_ref[g:g + 1, :] += _lane_fold(jnp.sum(dmixed, axis=0, keepdims=True))
            dvn_parts.append(_unstack_chunks(_dot(dm_s, wc), nc))
        dvn = jnp.concatenate(dvn_parts, axis=0)
        gacc_ref[...] += _lane_fold(dvn * xhat)
        bacc_ref[...] += _lane_fold(dvn)
        dxhat = dvn * lng_v
        m1 = jnp.mean(dxhat, axis=0, keepdims=True)
        m2 = jnp.mean(dxhat * xhat, axis=0, keepdims=True)
        dvv = rstd * (dxhat - m1 - xhat * m2)
        dzg_ref[GMLP_WIDTH:, :] = (dvv * _gelu_grad(zg[GMLP_WIDTH:, :])).astype(BF16)

        @pl.when(n == nt - 1)
        def _():
            dlng_ref[...] = jnp.sum(gacc_ref[...], axis=1, keepdims=True)
            dlnb_ref[...] = jnp.sum(bacc_ref[...], axis=1, keepdims=True)

    const2 = lambda n: (0, 0)
    return pl.pallas_call(
        body, name=name, grid=(nt,),
        in_specs=[pl.BlockSpec((2 * GMLP_WIDTH, tw), lambda n: (0, n)), pl.BlockSpec((GMLP_WIDTH, tw), lambda n: (0, n)),
                  _resident((GMLP_WIDTH, 1)), _resident((GMLP_WIDTH, 1)),
                  _resident((GMLP_GROUPS, BLOCK, BLOCK)), _resident((GMLP_GROUPS, BLOCK))],
        out_specs=[pl.BlockSpec((2 * GMLP_WIDTH, tw), lambda n: (0, n)),
                   pl.BlockSpec((GMLP_GROUPS, BLOCK, BLOCK), lambda n: (0, 0, 0)),
                   pl.BlockSpec((GMLP_GROUPS, BLOCK), const2),
                   pl.BlockSpec((GMLP_WIDTH, 1), const2), pl.BlockSpec((GMLP_WIDTH, 1), const2)],
        out_shape=[jax.ShapeDtypeStruct((2 * GMLP_WIDTH, t), BF16),
                   jax.ShapeDtypeStruct((GMLP_GROUPS, BLOCK, BLOCK), F32),
                   jax.ShapeDtypeStruct((GMLP_GROUPS, BLOCK), F32),
                   jax.ShapeDtypeStruct((GMLP_WIDTH, 1), F32), jax.ShapeDtypeStruct((GMLP_WIDTH, 1), F32)],
        scratch_shapes=[pltpu.VMEM((GMLP_WIDTH, BLOCK), F32), pltpu.VMEM((GMLP_WIDTH, BLOCK), F32)],
        compiler_params=_params(("arbitrary",)),
    )(zg_t, dy_t, lng, lnb, w_s, b_s)


def _rmsnorm_cols(y, gain_col):
    r = lax.rsqrt(jnp.mean(y * y, axis=0, keepdims=True) + EPS)
    n = y * r
    return n, r, n * gain_col


def mix_out_fwd(x, ya_t, yg_t, gao, ggo, wout, b_out, name):
    t, d = x.shape
    tm = _token_tile(t)

    def body(x_ref, ya_ref, yg_ref, gao_ref, ggo_ref, w_ref, b_ref, o_ref):
        _, _, ya = _rmsnorm_cols(ya_ref[...].astype(F32), gao_ref[...])
        _, _, yg = _rmsnorm_cols(yg_ref[...].astype(F32), ggo_ref[...])
        o_ref[...] = (x_ref[...] + _dg(ya.astype(BF16), w_ref[:ATTN_WIDTH, :], TN)
                      + _dg(yg.astype(BF16), w_ref[ATTN_WIDTH:, :], TN) + b_ref[...])

    return pl.pallas_call(
        body, name=name, grid=(t // tm,),
        in_specs=[pl.BlockSpec((tm, d), lambda i: (i, 0)), pl.BlockSpec((ATTN_WIDTH, tm), lambda i: (0, i)),
                  pl.BlockSpec((GMLP_WIDTH, tm), lambda i: (0, i)), _resident((ATTN_WIDTH, 1)), _resident((GMLP_WIDTH, 1)),
                  _resident((ATTN_WIDTH + GMLP_WIDTH, d)), _resident((1, d))],
        out_specs=pl.BlockSpec((tm, d), lambda i: (i, 0)),
        out_shape=jax.ShapeDtypeStruct((t, d), F32),
        compiler_params=_params(("arbitrary",)),
    )(x, ya_t, yg_t, gao, ggo, wout, b_out)


def _lane_fold(v):
    out = v[:, :LANES]
    for j in range(1, v.shape[1] // LANES):
        out = out + v[:, j * LANES:(j + 1) * LANES]
    return out


def mix_out_bwd(dx, ya_t, yg_t, gao, ggo, wout, name, deps=()):
    t, d = dx.shape
    tm = _token_tile(t)
    nt = t // tm

    def body(dx_ref, ya_ref, yg_ref, gao_ref, ggo_ref, w_ref, dya_ref, dyg_ref, y_ref, dgao_ref, dggo_ref, db_ref,
             acc_ref):
        i = pl.program_id(0)

        @pl.when(i == 0)
        def _():
            acc_ref[...] = jnp.zeros_like(acc_ref)
            db_ref[...] = jnp.zeros_like(db_ref)

        dxv = dx_ref[...]
        db_ref[...] += jnp.sum(dxv, axis=0, keepdims=True)
        dxb = dxv.astype(BF16)
        for part, (src_ref, gain_ref, dst_ref) in enumerate(((ya_ref, gao_ref, dya_ref), (yg_ref, ggo_ref, dyg_ref))):
            rows = slice(part * ATTN_WIDTH, (part + 1) * ATTN_WIDTH)
            gain = gain_ref[...]
            nrm, r, yn = _rmsnorm_cols(src_ref[...].astype(F32), gain)
            y_ref[rows, :] = yn.astype(BF16)
            dy = _dg(w_ref[rows, :], dxb, NT)
            acc_ref[rows, :] += _lane_fold(dy * nrm)
            dn = dy * gain
            dst_ref[...] = (r * (dn - nrm * jnp.mean(dn * nrm, axis=0, keepdims=True))).astype(BF16)

        @pl.when(i == nt - 1)
        def _():
            dgao_ref[...] = jnp.sum(acc_ref[:ATTN_WIDTH, :], axis=1, keepdims=True)
            dggo_ref[...] = jnp.sum(acc_ref[ATTN_WIDTH:, :], axis=1, keepdims=True)

    const2 = lambda i: (0, 0)
    feat = lambda w: pl.BlockSpec((w, tm), lambda i: (0, i))
    return pl.pallas_call(
        _drop_deps(body, 6, len(deps)), name=name, grid=(nt,),
        in_specs=[pl.BlockSpec((tm, d), lambda i: (i, 0)), feat(ATTN_WIDTH), feat(GMLP_WIDTH),
                  _resident((ATTN_WIDTH, 1)), _resident((GMLP_WIDTH, 1)), _resident((ATTN_WIDTH + GMLP_WIDTH, d))]
        + [ANY_SPEC] * len(deps),
        out_specs=[feat(ATTN_WIDTH), feat(GMLP_WIDTH), feat(ATTN_WIDTH + GMLP_WIDTH),
                   pl.BlockSpec((ATTN_WIDTH, 1), const2), pl.BlockSpec((GMLP_WIDTH, 1), const2),
                   pl.BlockSpec((1, d), const2)],
        out_shape=[jax.ShapeDtypeStruct((ATTN_WIDTH, t), BF16), jax.ShapeDtypeStruct((GMLP_WIDTH, t), BF16),
                   jax.ShapeDtypeStruct((ATTN_WIDTH + GMLP_WIDTH, t), BF16),
                   jax.ShapeDtypeStruct((ATTN_WIDTH, 1), F32), jax.ShapeDtypeStruct((GMLP_WIDTH, 1), F32),
                   jax.ShapeDtypeStruct((1, d), F32)],
        scratch_shapes=[pltpu.VMEM((ATTN_WIDTH + GMLP_WIDTH, LANES), F32)],
        compiler_params=_params(("arbitrary",)),
    )(dx, ya_t, yg_t, gao, ggo, wout, *deps)


def mix_in_bwd(x, dxo, gain, dq_t, dk_t, dv_t, dzg_t, win_t, name):
    t, d = x.shape
    w = win_t.shape[0]
    tm = _token_tile(t)
    nt = t // tm
    parts = ((0, ATTN_WIDTH), (ATTN_WIDTH, KV_WIDTH), (ATTN_WIDTH + KV_WIDTH, KV_WIDTH), (QKV_WIDTH, w - QKV_WIDTH))

    def body(x_ref, dxo_ref, g_ref, dq_ref, dk_ref, dv_ref, dzg_ref, w_ref, dx_ref, h_ref, dg_ref, db_ref, acc_ref):
        i = pl.program_id(0)

        @pl.when(i == 0)
        def _():
            acc_ref[...] = jnp.zeros_like(acc_ref)
            dg_ref[...] = jnp.zeros_like(dg_ref)

        xv = x_ref[...]
        gain_v = g_ref[...]
        r = lax.rsqrt(jnp.mean(xv * xv, axis=-1, keepdims=True) + EPS)
        h_ref[...] = (xv * r * gain_v).astype(BF16)
        dh = jnp.zeros((tm, d), F32)
        for (off, size), src_ref in zip(parts, (dq_ref, dk_ref, dv_ref, dzg_ref)):
            dp = src_ref[...]
            acc_ref[off:off + size, :] += _lane_fold(dp.astype(F32))
            dh = dh + _dg(dp, w_ref[off:off + size, :], TN)
        dx, dgain = _rmsnorm_bwd_rows(xv, r, gain_v, dh)
        dx_ref[...] = dxo_ref[...] + dx
        dg_ref[...] += dgain

        @pl.when(i == nt - 1)
        def _():
            db_ref[...] = jnp.sum(acc_ref[...], axis=1, keepdims=True)

    const2 = lambda i: (0, 0)
    tok = lambda: pl.BlockSpec((tm, d), lambda i: (i, 0))
    feat = lambda rows: pl.BlockSpec((rows, tm), lambda i: (0, i))
    return pl.pallas_call(
        body, name=name, grid=(nt,),
        in_specs=[tok(), tok(), _resident((1, d)), feat(ATTN_WIDTH), feat(KV_WIDTH), feat(KV_WIDTH),
                  feat(w - QKV_WIDTH), _resident((w, d))],
        out_specs=[tok(), tok(), pl.BlockSpec((1, d), const2), pl.BlockSpec((w, 1), const2)],
        out_shape=[jax.ShapeDtypeStruct((t, d), F32), jax.ShapeDtypeStruct((t, d), BF16),
                   jax.ShapeDtypeStruct((1, d), F32), jax.ShapeDtypeStruct((w, 1), F32)],
        scratch_shapes=[pltpu.VMEM((w, LANES), F32)],
        compiler_params=_params(("arbitrary",)),
    )(x, dxo, gain, dq_t, dk_t, dv_t, dzg_t, win_t)


def _adamw_math(w, g, m, v):
    m = ADAM_B1 * m + (1.0 - ADAM_B1) * g
    v = ADAM_B2 * v + (1.0 - ADAM_B2) * (g * g)
    m_hat = m / (1.0 - ADAM_B1 ** ADAM_STEP)
    v_hat = v / (1.0 - ADAM_B2 ** ADAM_STEP)
    delta = -ADAM_LR * (m_hat / (jnp.sqrt(v_hat) + ADAM_EPS) + ADAM_WD * w)
    return delta, m, v


def adamw(w, g, m, v, name):
    rows, cols = w.shape
    tr = rows
    if rows * cols > 256 * 1024:
        for cand in (256, 128, 64, 32, 16, 8):
            if rows % cand == 0:
                tr = cand
                break

    def body(w_ref, g_ref, m_ref, v_ref, d_ref, mo_ref, vo_ref):
        delta, mn, vn = _adamw_math(w_ref[...], g_ref[...], m_ref[...], v_ref[...])
        d_ref[...] = delta
        mo_ref[...] = mn
        vo_ref[...] = vn

    spec = pl.BlockSpec((tr, cols), lambda i: (i, 0))
    shape = jax.ShapeDtypeStruct((rows, cols), F32)
    return pl.pallas_call(
        body, name=name, grid=(rows // tr,),
        in_specs=[spec] * 4, out_specs=[spec] * 3, out_shape=[shape] * 3,
        compiler_params=_params(("arbitrary",)),
    )(w, g, m, v)


def sum_adamw(landing, grad3d, me, w, m, v, name):
    nd, rows, cols = landing.shape
    tr = rows // 2

    def body(me_ref, l_ref, own_ref, w_ref, m_ref, v_ref, g_ref, d_ref, mo_ref, vo_ref):
        g = own_ref[...].astype(F32)
        for j in range(nd):
            g = g + l_ref[j].astype(F32)
        delta, mn, vn = _adamw_math(w_ref[...], g, m_ref[...], v_ref[...])
        g_ref[...] = g
        d_ref[...] = delta
        mo_ref[...] = mn
        vo_ref[...] = vn

    spec = lambda: pl.BlockSpec((tr, cols), lambda i, me_ref: (i, 0))
    shape = jax.ShapeDtypeStruct((rows, cols), F32)
    return pl.pallas_call(
        body, name=name,
        grid_spec=pltpu.PrefetchScalarGridSpec(
            num_scalar_prefetch=1, grid=(rows // tr,),
            in_specs=[pl.BlockSpec((nd, tr, cols), lambda i, me_ref: (0, i, 0)),
                      pl.BlockSpec((None, tr, cols), lambda i, me_ref: (me_ref[0], i, 0)), spec(), spec(), spec()],
            out_specs=[spec() for _ in range(4)]),
        out_shape=[shape] * 4,
        compiler_params=_params(("arbitrary",)),
    )(me, landing, grad3d, w, m, v)


def _mesh_place():
    x, y, c = lax.axis_index("x"), lax.axis_index("y"), lax.axis_index("c")
    return x, y, c, 4 * x + 2 * y + c


def _peer(x, y, c, k):
    return (x ^ ((k >> 2) & 1), y ^ ((k >> 1) & 1), c ^ (k & 1))


def prep_shards(mats, me, name):
    na = len(mats)

    def body(me_ref, *refs):
        in_refs, shard_refs, land_refs = refs[:na], refs[na:2 * na], refs[2 * na:]
        for a in range(na):
            val = in_refs[a][...].astype(BF16)
            shard_refs[a][...] = val
            land_refs[a][...] = val

    whole = lambda mat: pl.BlockSpec(mat.shape, lambda i, me_ref: (0, 0))
    outs = pl.pallas_call(
        body, name=name,
        grid_spec=pltpu.PrefetchScalarGridSpec(
            num_scalar_prefetch=1, grid=(1,),
            in_specs=[whole(mat) for mat in mats],
            out_specs=[whole(mat) for mat in mats]
            + [pl.BlockSpec((None,) + mat.shape, lambda i, me_ref: (me_ref[0], 0, 0)) for mat in mats]),
        out_shape=[jax.ShapeDtypeStruct(mat.shape, BF16) for mat in mats]
        + [jax.ShapeDtypeStruct((N_DEV,) + mat.shape, BF16) for mat in mats],
        compiler_params=_params(("arbitrary",)),
    )(me, *mats)
    return outs[:na], outs[na:]


def allgather_rows(shards, lands, name):
    na = n_gather = len(shards)

    def body(*refs):
        in_refs, out_refs = refs[:na], refs[2 * na:3 * na]
        send_sems, recv_sems = refs[3 * na:]
        x, y, c, me = _mesh_place()

        def copy(sem, a, origin, src, k):
            return pltpu.make_async_remote_copy(
                src_ref=src, dst_ref=out_refs[a].at[origin], send_sem=send_sems.at[sem, a],
                recv_sem=recv_sems.at[sem, a], device_id=_peer(x, y, c, k), device_id_type=pl.DeviceIdType.MESH)

        kx, ky = 4, 2

        def recv(sem, a, origin):
            copy(sem, a, origin, in_refs[a], 1).wait_recv()

        def landed(a, origin):
            return out_refs[a].at[origin]

        @pl.when(c == 1)
        def _():
            sends = []

            def send(sem, a, origin, src, k):
                cp = copy(sem, a, origin, src, k)
                cp.start()
                sends.append(cp)

            for a in range(n_gather):
                send(1, a, me, in_refs[a], kx)
                send(2, a, me, in_refs[a], ky)
                send(0, a, me, in_refs[a], 1)
            for a in range(n_gather):
                recv(0, a, me ^ 1)
                send(3, a, me ^ 1, landed(a, me ^ 1), kx)
                send(4, a, me ^ 1, landed(a, me ^ 1), ky)
            for a in range(n_gather):
                recv(1, a, me ^ kx)
                send(5, a, me ^ kx, landed(a, me ^ kx), ky)
                send(7, a, me ^ kx, landed(a, me ^ kx), 1)
                recv(2, a, me ^ ky)
                send(8, a, me ^ ky, landed(a, me ^ ky), 1)
            for a in range(n_gather):
                recv(3, a, me ^ kx ^ 1)
                send(9, a, me ^ kx ^ 1, landed(a, me ^ kx ^ 1), 1)
                recv(4, a, me ^ ky ^ 1)
                send(6, a, me ^ ky ^ 1, landed(a, me ^ ky ^ 1), kx)
                send(10, a, me ^ ky ^ 1, landed(a, me ^ ky ^ 1), 1)
            for a in range(n_gather):
                recv(5, a, me ^ 6)
                send(11, a, me ^ 6, landed(a, me ^ 6), 1)
                recv(6, a, me ^ 7)
                send(12, a, me ^ 7, landed(a, me ^ 7), 1)
            for cp in sends:
                cp.wait_send()

        @pl.when(c == 0)
        def _():
            north = me ^ 1
            own = [copy(0, a, me, in_refs[a], 1) for a in range(n_gather)]
            for cp in own:
                cp.start()
            for a in range(n_gather):
                recv(0, a, north)
                for sem, origin in ((7, north ^ kx), (8, north ^ ky), (9, north ^ kx ^ 1), (10, north ^ ky ^ 1),
                                    (11, north ^ 6), (12, north ^ 7)):
                    recv(sem, a, origin)
            for cp in own:
                cp.wait_send()

    return pl.pallas_call(
        body, name=name,
        in_specs=[ANY_SPEC] * (2 * na), out_specs=[ANY_SPEC] * na,
        out_shape=[jax.ShapeDtypeStruct(land.shape, land.dtype) for land in lands],
        input_output_aliases={na + a: a for a in range(na)},
        scratch_shapes=[pltpu.SemaphoreType.DMA((13, n_gather)), pltpu.SemaphoreType.DMA((13, n_gather))],
    )(*shards, *lands)


HBM_SPEC = pl.BlockSpec(memory_space=pltpu.HBM)
SEM_SPEC = pl.BlockSpec(memory_space=pltpu.SEMAPHORE)
SIDE_EFFECT = pltpu.SideEffectType.DATAFLOW_SIDE_EFFECTING


def _hbm(v):
    return pltpu.with_memory_space_constraint(v, pltpu.HBM)


def gather_slices(src, land, me, k):
    return src, land.at[me], land.at[me ^ k]


def exchange_slices(src, land, me, k):
    return src.at[me ^ k], land.at[k - 1], land.at[k - 1]


def split_start(groups, slices, deps, name):
    sizes = [len(srcs) for srcs, _ in groups]
    flat_src = [s for srcs, _ in groups for s in srcs]
    flat_land = [l for _, lands in groups for l in lands]
    na, ng, nd = len(flat_src), len(groups), len(deps)

    def body(*refs):
        src_refs, land_refs = refs[:na], refs[na:2 * na]
        sem_refs = refs[2 * na + nd:2 * na + nd + 2 * ng]
        token_ref = refs[-1]
        x, y, c, me = _mesh_place()
        a0 = 0
        for gi, size in enumerate(sizes):
            send_sems, recv_sems = sem_refs[2 * gi], sem_refs[2 * gi + 1]
            for k in range(1, N_DEV):
                for j in range(size):
                    src, dst, _ = slices(src_refs[a0 + j], land_refs[a0 + j], me, k)
                    pltpu.make_async_remote_copy(
                        src_ref=src, dst_ref=dst, send_sem=send_sems.at[(k - 1) * size + j],
                        recv_sem=recv_sems.at[(k - 1) * size + j],
                        device_id=_peer(x, y, c, k), device_id_type=pl.DeviceIdType.MESH).start()
            a0 += size
        token_ref[...] = jnp.zeros_like(token_ref)

    sems = [pltpu.SemaphoreType.DMA(((N_DEV - 1) * size,)) for size in sizes for _ in range(2)]
    thru = [pltpu.HBM(v.shape, v.dtype) for v in flat_src + flat_land]
    outs = pl.pallas_call(
        body, name=name,
        in_specs=[HBM_SPEC] * (2 * na) + [ANY_SPEC] * nd,
        out_specs=[SEM_SPEC] * (2 * ng) + [HBM_SPEC] * (2 * na) + [pl.BlockSpec(memory_space=pltpu.VMEM)],
        out_shape=sems + thru + [jax.ShapeDtypeStruct((8, LANES), F32)],
        input_output_aliases={i: 2 * ng + i for i in range(2 * na)},
        compiler_params=pltpu.CompilerParams(has_side_effects=SIDE_EFFECT),
    )(*[_hbm(v) for v in flat_src + flat_land], *deps)
    started, a0 = [], 0
    for gi, size in enumerate(sizes):
        srcs = outs[2 * ng + a0:2 * ng + a0 + size]
        lands = outs[2 * ng + na + a0:2 * ng + na + a0 + size]
        started.append((outs[2 * gi], outs[2 * gi + 1], list(srcs), list(lands)))
        a0 += size
    return started, outs[-1]


def split_wait(started, slices, after, name):
    send_sems, recv_sems, srcs, lands = started
    na = len(srcs)

    def body(*refs):
        src_refs, land_refs = refs[:na], refs[na:2 * na]
        send_ref, recv_ref = refs[2 * na], refs[2 * na + 1]
        x, y, c, me = _mesh_place()
        for k in range(1, N_DEV):
            for j in range(na):
                src, _, got = slices(src_refs[j], land_refs[j], me, k)
                cp = pltpu.make_async_remote_copy(
                    src_ref=src, dst_ref=got, send_sem=send_ref.at[(k - 1) * na + j], recv_sem=recv_ref.at[(k - 1) * na + j],
                    device_id=_peer(x, y, c, k), device_id_type=pl.DeviceIdType.MESH)
                cp.wait_send()
                cp.wait_recv()

    outs = pl.pallas_call(
        body, name=name,
        in_specs=[HBM_SPEC] * (2 * na) + [SEM_SPEC, SEM_SPEC] + [ANY_SPEC] * len(after),
        out_specs=[HBM_SPEC] * (2 * na),
        out_shape=[pltpu.HBM(v.shape, v.dtype) for v in srcs + lands],
        input_output_aliases={i: i for i in range(2 * na)},
        compiler_params=pltpu.CompilerParams(has_side_effects=SIDE_EFFECT),
    )(*srcs, *lands, send_sems, recv_sems, *after)
    return list(outs[:na]), list(outs[na:])


def place_own(pack, me, name):
    rows, cols = pack.shape

    def body(me_ref, p_ref, o_ref):
        o_ref[...] = p_ref[...]

    return pl.pallas_call(
        body, name=name,
        grid_spec=pltpu.PrefetchScalarGridSpec(
            num_scalar_prefetch=1, grid=(1,),
            in_specs=[pl.BlockSpec((rows, cols), lambda i, me_ref: (0, 0))],
            out_specs=pl.BlockSpec((None, rows, cols), lambda i, me_ref: (me_ref[0], 0, 0))),
        out_shape=jax.ShapeDtypeStruct((N_DEV, rows, cols), F32),
        compiler_params=_params(("arbitrary",)),
    )(me, pack)


def sum_slots(slots, name):
    nd, rows, cols = slots.shape

    def body(s_ref, o_ref):
        acc = s_ref[0]
        for j in range(1, nd):
            acc = acc + s_ref[j]
        o_ref[...] = acc

    return pl.pallas_call(
        body, name=name, grid=(1,),
        in_specs=[pl.BlockSpec((nd, rows, cols), lambda i: (0, 0, 0))],
        out_specs=pl.BlockSpec((rows, cols), lambda i: (0, 0)),
        out_shape=jax.ShapeDtypeStruct((rows, cols), F32),
        compiler_params=_params(("arbitrary",)),
    )(slots)


def _pack_rows(size):
    rows = -(-size // LANES)
    return -(-rows // 8) * 8


def pack_small(parts):
    out = []
    for p in parts:
        flat = p.reshape(-1).astype(F32)
        rows = _pack_rows(flat.shape[0])
        out.append(jnp.pad(flat, (0, rows * LANES - flat.shape[0])).reshape(rows, LANES))
    return jnp.concatenate(out, axis=0)


def unpack_small(pack, shapes):
    out, off = [], 0
    for shp in shapes:
        size = 1
        for s in shp:
            size *= s
        rows = _pack_rows(size)
        out.append(pack[off:off + rows].reshape(-1)[:size].reshape(shp))
        off += rows
    return out


def local_step(x, target, small, get_w, put_g, put_small):
    col = lambda v: v.reshape(-1, 1)
    (wg1, wu1, wd1), deps = get_w(("wg1", "wu1", "wd1"), ())
    x1, sil1, gp1, s1, h1 = ffn_fwd(x, small["ffn1_norm_g"], wg1, wu1, wd1, "ffn1_fwd", deps)
    (win, wout), _ = get_w(("win", "wout"), (x1,))
    qkv_t, zg_t = mix_in_fwd(x1, small["mix_norm_g"], win, col(small["b_in"]), "mix_in_fwd")
    ya_t = attn_fwd(qkv_t, small["attn_sinks"], "attn_fwd")
    lng, lnb = col(small["gmlp_ln_g"]), col(small["gmlp_ln_b"])
    w_s, b_s = small["gmlp_w_s"][0], small["gmlp_b_s"][0]
    yg_t = gmlp_fwd(zg_t, lng, lnb, w_s, b_s, "gmlp_fwd")
    gao, ggo = col(small["attn_out_norm_g"]), col(small["gmlp_out_norm_g"])
    x2 = mix_out_fwd(x1, ya_t, yg_t, gao, ggo, wout, small["b_out"], "mix_out_fwd")
    (wg2, wu2, wd2), _ = get_w(("wg2", "wu2", "wd2"), (x2,))
    gs = {}
    dx3, sil2, gp2, s2, h2f, gs["final_norm_g"], loss = ffn_fwd(
        x2, small["ffn2_norm_g"], wg2, wu2, wd2, "ffn2_fwd", head=(target, small["final_norm_g"].reshape(1, -1)))

    dx2, da, db, df, gs["ffn2_norm_g"] = ffn_bwd(x2, dx3, small["ffn2_norm_g"], sil2, gp2, wg2, wu2, wd2, "ffn2_bwd")
    deps = put_g({"wg2": grad_tn(da, h2f, 1.0, "ffn2_dwg"), "wu2": grad_tn(db, h2f, 1.0, "ffn2_dwu"),
                  "wd2": grad_tn(s2, df, 1.0, "ffn2_dwd")})

    dya_t, dyg_t, y_t, dgao, dggo, gs["b_out"] = mix_out_bwd(dx2, ya_t, yg_t, gao, ggo, wout, "mix_out_bwd", deps)
    gs["attn_out_norm_g"], gs["gmlp_out_norm_g"] = dgao.reshape(1, -1), dggo.reshape(1, -1)
    g_wout = grad_nn([y_t], dx2, "dwout")
    dzg_t, dws, dbs, dlng, dlnb = gmlp_bwd(zg_t, dyg_t, lng, lnb, w_s, b_s, "gmlp_bwd")
    gs["gmlp_w_s"], gs["gmlp_b_s"] = dws[None], dbs[None]
    gs["gmlp_ln_g"], gs["gmlp_ln_b"] = dlng.reshape(1, -1), dlnb.reshape(1, -1)
    dq_t, dk_t, dv_t, dsinks = attn_bwd(qkv_t, dya_t, small["attn_sinks"], "attn_bwd")
    gs["attn_sinks"] = dsinks.reshape(1, -1)
    dx1, h2, gs["mix_norm_g"], dbin = mix_in_bwd(x1, dx2, small["mix_norm_g"], dq_t, dk_t, dv_t, dzg_t, win,
                                                 "mix_in_bwd")
    gs["b_in"] = dbin.reshape(1, -1)
    deps = put_g({"wout": g_wout, "win": grad_nn([dq_t, dk_t, dv_t, dzg_t], h2, "dwin")})

    dx0, da, db, df, gs["ffn1_norm_g"] = ffn_bwd(x, dx1, small["ffn1_norm_g"], sil1, gp1, wg1, wu1, wd1, "ffn1_bwd",
                                             deps)
    deps = put_small(gs, loss)
    deps = put_g({"wg1": grad_tn(da, h1, 1.0, "ffn1_dwg", deps)})
    deps = put_g({"wu1": grad_tn(db, h1, 1.0, "ffn1_dwu", deps)})
    put_g({"wd1": grad_tn(s1, df, 1.0, "ffn1_dwd", deps)})
    return dx0


SMALL_NAMES = ["ffn1_norm_g", "mix_norm_g", "b_in", "attn_sinks", "gmlp_ln_g", "gmlp_ln_b", "gmlp_w_s", "gmlp_b_s",
               "attn_out_norm_g", "gmlp_out_norm_g", "b_out", "ffn2_norm_g", "final_norm_g"]
BIG_NAMES = {"wg1": ("ffn1_w_gate", True), "wu1": ("ffn1_w_up", True), "wd1": ("ffn1_w_down", False),
             "win": ("w_in", True), "wout": ("w_out", False),
             "wg2": ("ffn2_w_gate", True), "wu2": ("ffn2_w_up", True), "wd2": ("ffn2_w_down", False)}
WEIGHT_ORDER = ["ffn1_norm_g", "ffn1_w_gate", "ffn1_w_up", "ffn1_w_down", "mix_norm_g", "w_in", "b_in", "attn_sinks",
                "gmlp_ln_g", "gmlp_ln_b", "gmlp_w_s", "gmlp_b_s", "attn_out_norm_g", "gmlp_out_norm_g", "w_out",
                "b_out", "ffn2_norm_g", "ffn2_w_gate", "ffn2_w_up", "ffn2_w_down", "final_norm_g"]


def kernel(x, ffn1_norm_g, ffn1_w_gate, ffn1_w_up, ffn1_w_down, mix_norm_g, w_in, b_in, attn_sinks, gmlp_ln_g, gmlp_ln_b, gmlp_w_s, gmlp_b_s, attn_out_norm_g, gmlp_out_norm_g, w_out, b_out, ffn2_norm_g, ffn2_w_gate, ffn2_w_up, ffn2_w_down, final_norm_g, loss_target, m_ffn1_norm_g, m_ffn1_w_gate, m_ffn1_w_up, m_ffn1_w_down, m_mix_norm_g, m_w_in, m_b_in, m_attn_sinks, m_gmlp_ln_g, m_gmlp_ln_b, m_gmlp_w_s, m_gmlp_b_s, m_attn_out_norm_g, m_gmlp_out_norm_g, m_w_out, m_b_out, m_ffn2_norm_g, m_ffn2_w_gate, m_ffn2_w_up, m_ffn2_w_down, m_final_norm_g, v_ffn1_norm_g, v_ffn1_w_gate, v_ffn1_w_up, v_ffn1_w_down, v_mix_norm_g, v_w_in, v_b_in, v_attn_sinks, v_gmlp_ln_g, v_gmlp_ln_b, v_gmlp_w_s, v_gmlp_b_s, v_attn_out_norm_g, v_gmlp_out_norm_g, v_w_out, v_b_out, v_ffn2_norm_g, v_ffn2_w_gate, v_ffn2_w_up, v_ffn2_w_down, v_final_norm_g):
    args = dict(locals())
    w = {n: args[n] for n in WEIGHT_ORDER}
    m = {n: args["m_" + n] for n in WEIGHT_ORDER}
    v = {n: args["v_" + n] for n in WEIGHT_ORDER}

    d_model = x.shape[-1]
    flat = lambda g: g.reshape(-1, d_model)
    me = _mesh_place()[3].astype(jnp.int32).reshape(1)
    first = ("wg1", "wu1", "wd1")
    later = (("win", "wout"), ("wg2", "wu2", "wd2"))
    gathers, exchanges, last_token = {}, [], []

    order = list(first) + [k for grp in later for k in grp]
    mats = [w[BIG_NAMES[k][0]][0].T if BIG_NAMES[k][1] else w[BIG_NAMES[k][0]][0] for k in order]
    shard_list, land_list = prep_shards(mats, me, "prep_shards")
    shards, zones = dict(zip(order, shard_list)), dict(zip(order, land_list))

    def get_w(keys, after):
        if keys == first:
            full = allgather_rows([shards[k] for k in first], [zones[k] for k in first], "allgather_ffn1")
            started, token = split_start([([shards[k] for k in grp], [zones[k] for k in grp]) for grp in later],
                                         gather_slices, (full[0],), "gather_start")
            gathers.update(zip(later, started))
            return [flat(g) for g in full], (token,)
        _, lands = split_wait(gathers[keys], gather_slices, after, "gather_wait_" + keys[0])
        return [flat(land) for land in lands], ()

    def put_g(gb):
        keys = tuple(gb)
        g3 = [gb[k].reshape(N_DEV, -1, d_model) for k in keys]
        lands = [lax.empty((N_DEV - 1,) + g.shape[1:], BF16) for g in g3]
        started, token = split_start([(g3, lands)], exchange_slices, (), "exchange_start_" + keys[0])
        exchanges.append((keys, started[0]))
        last_token[:] = [token]
        return (token,)

    small_started = []

    def put_small(gs, loss):
        pack = pack_small([gs[n] for n in SMALL_NAMES] + [loss[:, :1]])
        started, token = split_start([([pack], [place_own(pack, me, "place_small")])], gather_slices, (), "small_start")
        small_started[:] = started
        return (token,)

    small = {n: w[n] for n in SMALL_NAMES}
    grad_x = local_step(x[0], loss_target[0], small, get_w, put_g, put_small)

    grads, deltas, new_m, new_v = {}, {}, {}, {}
    after = tuple(last_token)
    for keys, started in exchanges:
        g3, lands = split_wait(started, exchange_slices, after, "exchange_wait_" + keys[0])
        for key, own, land in zip(keys, g3, lands):
            pname, transposed = BIG_NAMES[key]
            to_rows = (lambda p: p[0].T) if transposed else (lambda p: p[0])
            from_rows = (lambda r: r.T[None]) if transposed else (lambda r: r[None])
            g, d_, m_, v_ = sum_adamw(land, own, me, to_rows(w[pname]), to_rows(m[pname]), to_rows(v[pname]),
                                      "adamw_" + key)
            after = (g,)
            grads[pname], deltas[pname], new_m[pname], new_v[pname] = (from_rows(g), from_rows(d_), from_rows(m_),
                                                                        from_rows(v_))

    shapes = [w[n].shape for n in SMALL_NAMES]
    _, (slots,) = split_wait(small_started[0], gather_slices, after, "small_wait")
    total = sum_slots(slots, "sum_small")
    small_g = unpack_small(total, shapes + [(1, 1)])
    loss_total = small_g[-1].reshape(())
    gpack = total[:sum(_pack_rows(w[n].size) for n in SMALL_NAMES)]
    d_, m_, v_ = adamw(pack_small([w[n] for n in SMALL_NAMES]), gpack, pack_small([m[n] for n in SMALL_NAMES]),
                       pack_small([v[n] for n in SMALL_NAMES]), "adamw_small")
    for n, g_, dd, mm, vv in zip(SMALL_NAMES, small_g[:-1], unpack_small(d_, shapes), unpack_small(m_, shapes),
                                 unpack_small(v_, shapes)):
        grads[n], deltas[n], new_m[n], new_v[n] = g_, dd, mm, vv

    return (loss_total, grad_x[None], *[grads[n] for n in WEIGHT_ORDER], *[deltas[n] for n in WEIGHT_ORDER],
            *[new_m[n] for n in WEIGHT_ORDER], *[new_v[n] for n in WEIGHT_ORDER])
```

```python
import functools

import jax
import jax.numpy as jnp
from jax import lax
from jax.experimental import pallas as pl
from jax.experimental.pallas import tpu as pltpu

F32 = jnp.float32
BF16 = jnp.bfloat16

N_DEV = 8
N_Q_HEADS = 8
N_KV_HEADS = 2
HEAD_DIM = 64
ATTN_WIDTH = N_Q_HEADS * HEAD_DIM
KV_WIDTH = N_KV_HEADS * HEAD_DIM
QKV_WIDTH = ATTN_WIDTH + 2 * KV_WIDTH
BLOCK = 128
GMLP_GROUPS = 8
GMLP_GROUP_DIM = 64
GMLP_WIDTH = GMLP_GROUPS * GMLP_GROUP_DIM
EPS = 1e-6
FFN_RES = 0.5
ATTN_SCALE = HEAD_DIM ** -0.5
MASK_VALUE = -1e30

ADAM_LR = 0.001
ADAM_B1 = 0.9
ADAM_B2 = 0.999
ADAM_EPS = 1e-08
ADAM_WD = 0.01
ADAM_STEP = 10

V7X_VMEM_BYTES = 64 * 1024 * 1024
VMEM_LIMIT = 62 * 1024 * 1024
LANES = 128
MXU_WIDTH = 256

NT = (((1,), (1,)), ((), ()))
TN = (((0,), (0,)), ((), ()))


def _dot(a, b):
    return jnp.dot(a, b, preferred_element_type=F32)


def _dg(a, b, dims):
    return lax.dot_general(a, b, dims, preferred_element_type=F32)


def _params(semantics=None):
    return pltpu.CompilerParams(dimension_semantics=semantics, vmem_limit_bytes=VMEM_LIMIT)


def _resident(shape):
    zeros = (0,) * len(shape)
    return pl.BlockSpec(shape, lambda *_: zeros, pipeline_mode=pl.Buffered(1))


def _drop_deps(body, n_in, n_deps):
    return lambda *refs: body(*refs[:n_in], *refs[n_in + n_deps:])


ANY_SPEC = pl.BlockSpec(memory_space=pl.ANY)


def _token_tile(t):
    return min(t, 512)


def _f_chunk(f):
    for c in (256, 128):
        if f % c == 0:
            return c
    return f


def _loss_head(xv, target, gain):
    r = lax.rsqrt(jnp.mean(xv * xv, axis=-1, keepdims=True) + EPS)
    err = xv * r * gain - target
    loss = 0.5 * jnp.sum(jnp.mean(err * err, axis=-1, keepdims=True), axis=0, keepdims=True)
    dx, dgain = _rmsnorm_bwd_rows(xv, r, gain, err * (1.0 / xv.shape[-1]))
    return dx, dgain, loss


def ffn_fwd(x, gain, wg_t, wu_t, wd, name, deps=(), head=None):
    t, d = x.shape
    f = wd.shape[0]
    tm = _token_tile(t)
    fc = _f_chunk(f)
    n_in = 5 if head is None else 7

    def body(x_ref, g_ref, wg_ref, wu_ref, wd_ref, *rest):
        xo_ref, sil_ref, gp_ref, s_ref, h_ref = rest[n_in - 5:n_in]
        xv = x_ref[...]
        r = lax.rsqrt(jnp.mean(xv * xv, axis=-1, keepdims=True) + EPS)
        h = (xv * r * g_ref[...]).astype(BF16)
        h_ref[...] = h
        for c in range(f // fc):
            sl = slice(c * fc, (c + 1) * fc)
            a = _dg(h, wg_ref[sl, :], NT)
            b = _dg(h, wu_ref[sl, :], NT)
            sig = jax.nn.sigmoid(a)
            sil = a * sig
            sil_ref[:, sl] = sil.astype(BF16)
            gp_ref[:, sl] = (b * (sig + sil - sil * sig)).astype(BF16)
            s_ref[:, sl] = (sil * b).astype(BF16)
        xo = xv + FFN_RES * _dot(s_ref[...], wd_ref[...])
        if head is None:
            xo_ref[...] = xo
        else:
            t_ref, hg_ref = rest[:2]
            dg_ref, loss_ref = rest[-2:]

            @pl.when(pl.program_id(0) == 0)
            def _():
                dg_ref[...] = jnp.zeros_like(dg_ref)
                loss_ref[...] = jnp.zeros_like(loss_ref)

            dx, dgain, loss = _loss_head(xo, t_ref[...], hg_ref[...])
            xo_ref[...] = dx
            dg_ref[...] += dgain
            loss_ref[...] += loss

    tok = lambda: pl.BlockSpec((tm, d), lambda i: (i, 0))
    wide = lambda: pl.BlockSpec((tm, f), lambda i: (i, 0))
    const2 = lambda i: (0, 0)
    in_specs = [tok(), _resident((1, d)), _resident((f, d)), _resident((f, d)), _resident((f, d))]
    out_specs = [tok(), wide(), wide(), wide(), tok()]
    out_shape = ([jax.ShapeDtypeStruct((t, d), F32)] + [jax.ShapeDtypeStruct((t, f), BF16)] * 3
                 + [jax.ShapeDtypeStruct((t, d), BF16)])
    operands = [x, gain, wg_t, wu_t, wd]
    if head is not None:
        in_specs += [tok(), _resident((1, d))]
        out_specs += [pl.BlockSpec((1, d), const2), pl.BlockSpec((1, LANES), const2)]
        out_shape += [jax.ShapeDtypeStruct((1, d), F32), jax.ShapeDtypeStruct((1, LANES), F32)]
        operands += list(head)
    return pl.pallas_call(
        _drop_deps(body, n_in, len(deps)), name=name, grid=(t // tm,),
        in_specs=in_specs + [ANY_SPEC] * len(deps), out_specs=out_specs, out_shape=out_shape,
        compiler_params=_params(("arbitrary",)),
    )(*operands, *deps)


def _rmsnorm_bwd_rows(xv, r, gain, dh):
    n = xv * r
    dgain = jnp.sum(dh * n, axis=0, keepdims=True)
    dn = dh * gain
    dx = r * (dn - n * jnp.mean(dn * n, axis=-1, keepdims=True))
    return dx, dgain


def ffn_bwd(x, dxo, gain, sil, gp, wg_t, wu_t, wd, name, deps=()):
    t, d = x.shape
    f = wd.shape[0]
    tm = _token_tile(t)
    nh = 2 if f % (2 * LANES) == 0 else 1
    fh = f // nh
    chunks = [(off, min(MXU_WIDTH, fh - off)) for off in range(0, fh, MXU_WIDTH)]

    def body(x_ref, dxo_ref, g_ref, sil_ref, gp_ref, wg_ref, wu_ref, wd_ref,
             dx_ref, da_ref, db_ref, df_ref, dg_ref, dh_ref):
        i, j = pl.program_id(0), pl.program_id(1)
        gain_v = g_ref[...]

        @pl.when(j == 0)
        def _():
            df_ref[...] = (FFN_RES * dxo_ref[...]).astype(BF16)

        df = df_ref[...]
        for off, size in chunks:
            sl = slice(off, off + size)
            rows = pl.ds(pl.multiple_of(j * fh + off, LANES), size)
            ds = _dg(df, wd_ref[rows, :], NT)
            da_ref[:, sl] = (ds * gp_ref[:, sl].astype(F32)).astype(BF16)
            db_ref[:, sl] = (ds * sil_ref[:, sl].astype(F32)).astype(BF16)
        half_rows = pl.ds(pl.multiple_of(j * fh, LANES), fh)
        part = _dot(da_ref[...], wg_ref[half_rows, :]) + _dot(db_ref[...], wu_ref[half_rows, :])

        @pl.when(j == 0)
        def _():
            dh_ref[...] = part

        @pl.when(j > 0)
        def _():
            dh_ref[...] += part

        @pl.when(j == nh - 1)
        def _():
            xv = x_ref[...]
            r = lax.rsqrt(jnp.mean(xv * xv, axis=-1, keepdims=True) + EPS)
            dx, dgain = _rmsnorm_bwd_rows(xv, r, gain_v, dh_ref[...])
            dx_ref[...] = dxo_ref[...] + dx

            @pl.when(i == 0)
            def _():
                dg_ref[...] = jnp.zeros_like(dg_ref)

            dg_ref[...] += dgain

    tok = lambda: pl.BlockSpec((tm, d), lambda i, j: (i, 0))
    half = lambda: pl.BlockSpec((tm, fh), lambda i, j: (i, j))
    return pl.pallas_call(
        _drop_deps(body, 8, len(deps)), name=name, grid=(t // tm, nh),
        in_specs=[tok(), tok(), _resident((1, d)), half(), half(),
                  _resident((f, d)), _resident((f, d)), _resident((f, d))] + [ANY_SPEC] * len(deps),
        out_specs=[tok(), half(), half(), tok(), pl.BlockSpec((1, d), lambda i, j: (0, 0))],
        out_shape=[jax.ShapeDtypeStruct((t, d), F32), jax.ShapeDtypeStruct((t, f), BF16),
                   jax.ShapeDtypeStruct((t, f), BF16), jax.ShapeDtypeStruct((t, d), BF16),
                   jax.ShapeDtypeStruct((1, d), F32)],
        scratch_shapes=[pltpu.VMEM((tm, d), F32)],
        compiler_params=_params(("arbitrary", "arbitrary")),
    )(x, dxo, gain, sil, gp, wg_t, wu_t, wd, *deps)


def _row_tile(m, cap):
    if m <= cap:
        return m
    best = None
    for bm in range(LANES, cap + 1, LANES):
        if m % bm == 0:
            best = bm
    return best if best is not None else m


def grad_tn(a, b, scale, name, deps=()):
    t, m = a.shape
    n = b.shape[1]
    bm = _row_tile(m, 1408)
    bk = min(t, 2048)
    nk = t // bk

    def body(a_ref, b_ref, o_ref, acc_ref):
        k = pl.program_id(1)

        @pl.when(k == 0)
        def _():
            acc_ref[...] = jnp.zeros_like(acc_ref)

        acc_ref[...] += _dg(a_ref[...].astype(BF16), b_ref[...].astype(BF16), TN)

        @pl.when(k == nk - 1)
        def _():
            o_ref[...] = (scale * acc_ref[...]).astype(BF16)

    return pl.pallas_call(
        _drop_deps(body, 2, len(deps)), name=name, grid=(m // bm, nk),
        in_specs=[pl.BlockSpec((bk, bm), lambda i, k: (k, i)), pl.BlockSpec((bk, n), lambda i, k: (k, 0))]
        + [ANY_SPEC] * len(deps),
        out_specs=pl.BlockSpec((bm, n), lambda i, k: (i, 0)),
        out_shape=jax.ShapeDtypeStruct((m, n), BF16),
        scratch_shapes=[pltpu.VMEM((bm, n), F32)],
        compiler_params=_params(("arbitrary", "arbitrary")),
    )(a, b, *deps)


def grad_nn(a_list, b, name):
    t, n = b.shape
    ms = [a.shape[0] for a in a_list]
    m = sum(ms)
    bk = min(t, 1024)
    nk = t // bk
    na = len(a_list)

    def body(*refs):
        a_refs, b_ref, o_ref, acc_ref = refs[:na], refs[na], refs[na + 1], refs[na + 2]
        k = pl.program_id(0)

        @pl.when(k == 0)
        def _():
            acc_ref[...] = jnp.zeros_like(acc_ref)

        bv = b_ref[...].astype(BF16)
        off = 0
        for a_ref, mi in zip(a_refs, ms):
            acc_ref[off:off + mi, :] += _dot(a_ref[...].astype(BF16), bv)
            off += mi

        @pl.when(k == nk - 1)
        def _():
            o_ref[...] = acc_ref[...].astype(BF16)

    return pl.pallas_call(
        body, name=name, grid=(nk,),
        in_specs=[pl.BlockSpec((mi, bk), lambda k: (0, k)) for mi in ms] + [pl.BlockSpec((bk, n), lambda k: (k, 0))],
        out_specs=pl.BlockSpec((m, n), lambda k: (0, 0)),
        out_shape=jax.ShapeDtypeStruct((m, n), BF16),
        scratch_shapes=[pltpu.VMEM((m, n), F32)],
        compiler_params=_params(("arbitrary",)),
    )(*a_list, b)


def mix_in_fwd(x, gain, win_t, b_in_col, name):
    t, d = x.shape
    w = win_t.shape[0]
    tm = _token_tile(t)

    def body(x_ref, g_ref, w_ref, bias_ref, qkv_ref, zg_ref):
        xv = x_ref[...]
        r = lax.rsqrt(jnp.mean(xv * xv, axis=-1, keepdims=True) + EPS)
        h = (xv * r * g_ref[...]).astype(BF16)
        qkv_ref[...] = (_dg(w_ref[:QKV_WIDTH, :], h, NT) + bias_ref[:QKV_WIDTH, :]).astype(BF16)
        zg_ref[...] = (_dg(w_ref[QKV_WIDTH:, :], h, NT) + bias_ref[QKV_WIDTH:, :]).astype(BF16)

    return pl.pallas_call(
        body, name=name, grid=(t // tm,),
        in_specs=[pl.BlockSpec((tm, d), lambda i: (i, 0)), _resident((1, d)), _resident((w, d)), _resident((w, 1))],
        out_specs=[pl.BlockSpec((QKV_WIDTH, tm), lambda i: (0, i)), pl.BlockSpec((w - QKV_WIDTH, tm), lambda i: (0, i))],
        out_shape=[jax.ShapeDtypeStruct((QKV_WIDTH, t), BF16), jax.ShapeDtypeStruct((w - QKV_WIDTH, t), BF16)],
        compiler_params=_params(("arbitrary",)),
    )(x, gain, win_t, b_in_col)


ATTN_REP = N_Q_HEADS // N_KV_HEADS
ATTN_BLOCKS_PER_STEP = 4


def _attn_tiles(t):
    nbs = min(ATTN_BLOCKS_PER_STEP, t // BLOCK)
    return nbs, nbs * BLOCK, t // (nbs * BLOCK)


def _attn_specs(nbs, tiles):
    last = tiles - 1
    cur = lambda n: jnp.minimum(n, last)
    prev = lambda n: jnp.maximum(jnp.minimum(n, last) * nbs - 1, 0)
    k_row = ATTN_WIDTH // KV_WIDTH
    tw = nbs * BLOCK
    return [
        pl.BlockSpec((ATTN_WIDTH, tw), lambda n: (0, cur(n))),
        pl.BlockSpec((KV_WIDTH, BLOCK), lambda n: (k_row, prev(n))),
        pl.BlockSpec((KV_WIDTH, tw), lambda n: (k_row, cur(n))),
        pl.BlockSpec((KV_WIDTH, BLOCK), lambda n: (k_row + 1, prev(n))),
        pl.BlockSpec((KV_WIDTH, tw), lambda n: (k_row + 1, cur(n))),
    ]


def _cols(b):
    return slice(b * BLOCK, (b + 1) * BLOCK)


def _band_rows(prev_ref, tile_ref, gs, b):
    before = prev_ref[gs, :] if b == 0 else tile_ref[gs, _cols(b - 1)]
    return before, tile_ref[gs, _cols(b)]


def _group_rows(ref, g, b):
    first = g * ATTN_REP
    return jnp.concatenate([ref[(first + r) * HEAD_DIM:(first + r + 1) * HEAD_DIM, _cols(b)] for r in range(ATTN_REP)],
                           axis=1)


def _ungroup_rows(ref, g, b, val):
    first = g * ATTN_REP
    for r in range(ATTN_REP):
        ref[(first + r) * HEAD_DIM:(first + r + 1) * HEAD_DIM, _cols(b)] = val[:, r * BLOCK:(r + 1) * BLOCK]


def _attn_upper():
    key = lax.broadcasted_iota(jnp.int32, (BLOCK, ATTN_REP * BLOCK), 0)
    qry = lax.broadcasted_iota(jnp.int32, (BLOCK, ATTN_REP * BLOCK), 1) & (BLOCK - 1)
    return key > qry


def _attn_probs(q, kp, kc, sink_ref, g, upper, no_prev):
    s = jnp.where(upper, _dg(kp, q, TN), _dg(kc, q, TN)) * ATTN_SCALE
    if no_prev is not None:
        s = jnp.where(upper & no_prev, MASK_VALUE, s)
    sink = jnp.concatenate([jnp.full((1, BLOCK), sink_ref[0, g * ATTN_REP + r], F32) for r in range(ATTN_REP)], axis=1)
    m = jnp.maximum(jnp.max(s, axis=0, keepdims=True), sink)
    e = jnp.exp(s - m)
    es = jnp.exp(sink - m)
    inv = 1.0 / (jnp.sum(e, axis=0, keepdims=True) + es)
    return e * inv, es * inv


def _split_band(upper, val):
    return jnp.where(upper, val, 0.0).astype(BF16), jnp.where(upper, 0.0, val).astype(BF16)


def attn_fwd(qkv_t, sinks, name):
    t = qkv_t.shape[1]
    nbs, tw, tiles = _attn_tiles(t)

    def body(sink_ref, q_ref, kp_ref, kc_ref, vp_ref, vc_ref, y_ref):
        n = pl.program_id(0)
        upper = _attn_upper()
        for b in range(nbs):
            for g in range(N_KV_HEADS):
                gs = slice(g * HEAD_DIM, (g + 1) * HEAD_DIM)
                kp, kc = _band_rows(kp_ref, kc_ref, gs, b)
                vp, vc = _band_rows(vp_ref, vc_ref, gs, b)
                p, _ = _attn_probs(_group_rows(q_ref, g, b), kp, kc, sink_ref, g, upper, n == 0 if b == 0 else None)
                pp, pc = _split_band(upper, p)
                _ungroup_rows(y_ref, g, b, (_dot(vp, pp) + _dot(vc, pc)).astype(BF16))

    return pl.pallas_call(
        body, name=name, grid=(tiles,),
        in_specs=[pl.BlockSpec(memory_space=pltpu.SMEM)] + _attn_specs(nbs, tiles),
        out_specs=pl.BlockSpec((ATTN_WIDTH, tw), lambda n: (0, n)),
        out_shape=jax.ShapeDtypeStruct((ATTN_WIDTH, t), BF16),
        compiler_params=_params(("arbitrary",)),
    )(sinks, qkv_t, qkv_t, qkv_t, qkv_t, qkv_t)


def attn_bwd(qkv_t, dy_t, sinks, name):
    t = qkv_t.shape[1]
    nbs, tw, tiles = _attn_tiles(t)
    kept = slice(0, tw - BLOCK)

    def body(sink_ref, q_ref, kp_ref, kc_ref, vp_ref, vc_ref, do_ref, dq_ref, dk_ref, dv_ref, dsink_ref,
             ck_ref, cv_ref, sacc_ref):
        n = pl.program_id(0)

        @pl.when(n == 0)
        def _():
            sacc_ref[...] = jnp.zeros_like(sacc_ref)

        @pl.when((n > 0) & (n < tiles))
        def _():
            if nbs > 1:
                dk_ref[:, kept] = ck_ref[:, kept].astype(BF16)
                dv_ref[:, kept] = cv_ref[:, kept].astype(BF16)

        @pl.when(n < tiles)
        def _():
            upper = _attn_upper()
            for b in range(nbs):
                for g in range(N_KV_HEADS):
                    gs = slice(g * HEAD_DIM, (g + 1) * HEAD_DIM)
                    kp, kc = _band_rows(kp_ref, kc_ref, gs, b)
                    vp, vc = _band_rows(vp_ref, vc_ref, gs, b)
                    q = _group_rows(q_ref, g, b)
                    do = _group_rows(do_ref, g, b)
                    p, ps = _attn_probs(q, kp, kc, sink_ref, g, upper, n == 0 if b == 0 else None)
                    dp = jnp.where(upper, _dg(vp, do, TN), _dg(vc, do, TN))
                    delta = jnp.sum(p * dp, axis=0, keepdims=True)
                    dsink = -(ps * delta)
                    for r in range(ATTN_REP):
                        hd = g * ATTN_REP + r
                        sacc_ref[hd:hd + 1, :] += dsink[:, r * BLOCK:(r + 1) * BLOCK]
                    dsp, dsc = _split_band(upper, p * (dp - delta))
                    pp, pc = _split_band(upper, p)
                    _ungroup_rows(dq_ref, g, b, (ATTN_SCALE * (_dot(kp, dsp) + _dot(kc, dsc))).astype(BF16))
                    dk_before = ATTN_SCALE * _dg(q, dsp, NT)
                    dv_before = _dg(do, pp, NT)
                    if b == 0:
                        @pl.when(n > 0)
                        def _():
                            dk_ref[gs, _cols(nbs - 1)] = (ck_ref[gs, _cols(nbs - 1)] + dk_before).astype(BF16)
                            dv_ref[gs, _cols(nbs - 1)] = (cv_ref[gs, _cols(nbs - 1)] + dv_before).astype(BF16)
                    else:
                        ck_ref[gs, _cols(b - 1)] += dk_before
                        cv_ref[gs, _cols(b - 1)] += dv_before
                    ck_ref[gs, _cols(b)] = ATTN_SCALE * _dg(q, dsc, NT)
                    cv_ref[gs, _cols(b)] = _dg(do, pc, NT)

        @pl.when(n == tiles)
        def _():
            dk_ref[...] = ck_ref[...].astype(BF16)
            dv_ref[...] = cv_ref[...].astype(BF16)
            dsink_ref[...] = jnp.sum(sacc_ref[...], axis=1, keepdims=True)

    last = tiles - 1
    done = lambda n: jnp.maximum(n - 1, 0)
    outs = pl.pallas_call(
        body, name=name, grid=(tiles + 1,),
        in_specs=[pl.BlockSpec(memory_space=pltpu.SMEM)] + _attn_specs(nbs, tiles)
        + [pl.BlockSpec((ATTN_WIDTH, tw), lambda n: (0, jnp.minimum(n, last)))],
        out_specs=[pl.BlockSpec((ATTN_WIDTH, tw), lambda n: (0, jnp.minimum(n, last))),
                   pl.BlockSpec((KV_WIDTH, tw), lambda n: (0, done(n))),
                   pl.BlockSpec((KV_WIDTH, tw), lambda n: (0, done(n))),
                   pl.BlockSpec((N_Q_HEADS, 1), lambda n: (0, 0))],
        out_shape=[jax.ShapeDtypeStruct((ATTN_WIDTH, t), BF16), jax.ShapeDtypeStruct((KV_WIDTH, t), BF16),
                   jax.ShapeDtypeStruct((KV_WIDTH, t), BF16), jax.ShapeDtypeStruct((N_Q_HEADS, 1), F32)],
        scratch_shapes=[pltpu.VMEM((KV_WIDTH, tw), F32), pltpu.VMEM((KV_WIDTH, tw), F32),
                        pltpu.VMEM((N_Q_HEADS, BLOCK), F32)],
        compiler_params=_params(("arbitrary",)),
    )(sinks, qkv_t, qkv_t, qkv_t, qkv_t, qkv_t, dy_t)
    return outs


GELU_C = 0.7978845608028654
GELU_A = 0.044715


def _gelu(x):
    return 0.5 * x * (1.0 + jnp.tanh(GELU_C * (x + GELU_A * x * x * x)))


def _gelu_grad(x):
    th = jnp.tanh(GELU_C * (x + GELU_A * x * x * x))
    return 0.5 * (1.0 + th) + 0.5 * x * (1.0 - th * th) * GELU_C * (1.0 + 3.0 * GELU_A * x * x)


def _gmlp_parts(zg, lng, lnb):
    z = _gelu(zg)
    u = z[:GMLP_WIDTH, :]
    vv = z[GMLP_WIDTH:, :]
    mu = jnp.mean(vv, axis=0, keepdims=True)
    xc = vv - mu
    rstd = lax.rsqrt(jnp.mean(xc * xc, axis=0, keepdims=True) + EPS)
    xhat = xc * rstd
    return u, xhat, rstd, xhat * lng + lnb


def _causal(w):
    row = lax.broadcasted_iota(jnp.int32, (BLOCK, BLOCK), 0)
    col = lax.broadcasted_iota(jnp.int32, (BLOCK, BLOCK), 1)
    return jnp.where(col <= row, w, 0.0)


GMLP_CHUNKS_PER_STEP = 4


def _stack_chunks(v, nc):
    return jnp.concatenate([v[:, c * BLOCK:(c + 1) * BLOCK] for c in range(nc)], axis=0)


def _unstack_chunks(v, nc):
    rows = v.shape[0] // nc
    return jnp.concatenate([v[c * rows:(c + 1) * rows, :] for c in range(nc)], axis=1)


def gmlp_fwd(zg_t, lng, lnb, w_s, b_s, name):
    t = zg_t.shape[1]
    nc = min(GMLP_CHUNKS_PER_STEP, t // BLOCK)
    tw = nc * BLOCK

    def body(zg_ref, lng_ref, lnb_ref, w_ref, bs_ref, y_ref):
        u, _, _, vn = _gmlp_parts(zg_ref[...].astype(F32), lng_ref[...], lnb_ref[...])
        for g in range(GMLP_GROUPS):
            gs = slice(g * GMLP_GROUP_DIM, (g + 1) * GMLP_GROUP_DIM)
            wc = _causal(w_ref[g]).astype(BF16)
            mixed = _dg(_stack_chunks(vn[gs, :].astype(BF16), nc), wc, NT) + bs_ref[g:g + 1, :]
            y_ref[gs, :] = (u[gs, :] * _unstack_chunks(mixed, nc)).astype(BF16)

    return pl.pallas_call(
        body, name=name, grid=(t // tw,),
        in_specs=[pl.BlockSpec((2 * GMLP_WIDTH, tw), lambda n: (0, n)), _resident((GMLP_WIDTH, 1)),
                  _resident((GMLP_WIDTH, 1)), _resident((GMLP_GROUPS, BLOCK, BLOCK)), _resident((GMLP_GROUPS, BLOCK))],
        out_specs=pl.BlockSpec((GMLP_WIDTH, tw), lambda n: (0, n)),
        out_shape=jax.ShapeDtypeStruct((GMLP_WIDTH, t), BF16),
        compiler_params=_params(("arbitrary",)),
    )(zg_t, lng, lnb, w_s, b_s)


def gmlp_bwd(zg_t, dy_t, lng, lnb, w_s, b_s, name):
    t = zg_t.shape[1]
    nc = min(GMLP_CHUNKS_PER_STEP, t // BLOCK)
    tw = nc * BLOCK
    nt = t // tw

    def body(zg_ref, dy_ref, lng_ref, lnb_ref, w_ref, bs_ref, dzg_ref, dw_ref, dbs_ref, dlng_ref, dlnb_ref,
             gacc_ref, bacc_ref):
        n = pl.program_id(0)

        @pl.when(n == 0)
        def _():
            dw_ref[...] = jnp.zeros_like(dw_ref)
            dbs_ref[...] = jnp.zeros_like(dbs_ref)
            gacc_ref[...] = jnp.zeros_like(gacc_ref)
            bacc_ref[...] = jnp.zeros_like(bacc_ref)

        zg = zg_ref[...].astype(F32)
        lng_v = lng_ref[...]
        u, xhat, rstd, vn = _gmlp_parts(zg, lng_v, lnb_ref[...])
        dy = dy_ref[...].astype(F32)
        dvn_parts = []
        for g in range(GMLP_GROUPS):
            gs = slice(g * GMLP_GROUP_DIM, (g + 1) * GMLP_GROUP_DIM)
            wc = _causal(w_ref[g]).astype(BF16)
            vn_s = _stack_chunks(vn[gs, :].astype(BF16), nc)
            mixed = _unstack_chunks(_dg(vn_s, wc, NT) + bs_ref[g:g + 1, :], nc)
            dy_g = dy[gs, :]
            dmixed = dy_g * u[gs, :]
            dm_s = _stack_chunks(dmixed.astype(BF16), nc)
            dzg_ref[gs, :] = (dy_g * mixed * _gelu_grad(zg[gs, :])).astype(BF16)
            dw_ref[g] += _causal(_dg(dm_s, vn_s, TN))
            dbs_ref[g:g + 1, :] += _lane_fold(jnp.sum(dmixed, axis=0, keepdims=True))
            dvn_parts.append(_unstack_chunks(_dot(dm_s, wc), nc))
        dvn = jnp.concatenate(dvn_parts, axis=0)
        gacc_ref[...] += _lane_fold(dvn * xhat)
        bacc_ref[...] += _lane_fold(dvn)
        dxhat = dvn * lng_v
        m1 = jnp.mean(dxhat, axis=0, keepdims=True)
        m2 = jnp.mean(dxhat * xhat, axis=0, keepdims=True)
        dvv = rstd * (dxhat - m1 - xhat * m2)
        dzg_ref[GMLP_WIDTH:, :] = (dvv * _gelu_grad(zg[GMLP_WIDTH:, :])).astype(BF16)

        @pl.when(n == nt - 1)
        def _():
            dlng_ref[...] = jnp.sum(gacc_ref[...], axis=1, keepdims=True)
            dlnb_ref[...] = jnp.sum(bacc_ref[...], axis=1, keepdims=True)

    const2 = lambda n: (0, 0)
    return pl.pallas_call(
        body, name=name, grid=(nt,),
        in_specs=[pl.BlockSpec((2 * GMLP_WIDTH, tw), lambda n: (0, n)), pl.BlockSpec((GMLP_WIDTH, tw), lambda n: (0, n)),
                  _resident((GMLP_WIDTH, 1)), _resident((GMLP_WIDTH, 1)),
                  _resident((GMLP_GROUPS, BLOCK, BLOCK)), _resident((GMLP_GROUPS, BLOCK))],
        out_specs=[pl.BlockSpec((2 * GMLP_WIDTH, tw), lambda n: (0, n)),
                   pl.BlockSpec((GMLP_GROUPS, BLOCK, BLOCK), lambda n: (0, 0, 0)),
                   pl.BlockSpec((GMLP_GROUPS, BLOCK), const2),
                   pl.BlockSpec((GMLP_WIDTH, 1), const2), pl.BlockSpec((GMLP_WIDTH, 1), const2)],
        out_shape=[jax.ShapeDtypeStruct((2 * GMLP_WIDTH, t), BF16),
                   jax.ShapeDtypeStruct((GMLP_GROUPS, BLOCK, BLOCK), F32),
                   jax.ShapeDtypeStruct((GMLP_GROUPS, BLOCK), F32),
                   jax.ShapeDtypeStruct((GMLP_WIDTH, 1), F32), jax.ShapeDtypeStruct((GMLP_WIDTH, 1), F32)],
        scratch_shapes=[pltpu.VMEM((GMLP_WIDTH, BLOCK), F32), pltpu.VMEM((GMLP_WIDTH, BLOCK), F32)],
        compiler_params=_params(("arbitrary",)),
    )(zg_t, dy_t, lng, lnb, w_s, b_s)


def _rmsnorm_cols(y, gain_col):
    r = lax.rsqrt(jnp.mean(y * y, axis=0, keepdims=True) + EPS)
    n = y * r
    return n, r, n * gain_col


def mix_out_fwd(x, ya_t, yg_t, gao, ggo, wout, b_out, name):
    t, d = x.shape
    tm = _token_tile(t)

    def body(x_ref, ya_ref, yg_ref, gao_ref, ggo_ref, w_ref, b_ref, o_ref):
        _, _, ya = _rmsnorm_cols(ya_ref[...].astype(F32), gao_ref[...])
        _, _, yg = _rmsnorm_cols(yg_ref[...].astype(F32), ggo_ref[...])
        o_ref[...] = (x_ref[...] + _dg(ya.astype(BF16), w_ref[:ATTN_WIDTH, :], TN)
                      + _dg(yg.astype(BF16), w_ref[ATTN_WIDTH:, :], TN) + b_ref[...])

    return pl.pallas_call(
        body, name=name, grid=(t // tm,),
        in_specs=[pl.BlockSpec((tm, d), lambda i: (i, 0)), pl.BlockSpec((ATTN_WIDTH, tm), lambda i: (0, i)),
                  pl.BlockSpec((GMLP_WIDTH, tm), lambda i: (0, i)), _resident((ATTN_WIDTH, 1)), _resident((GMLP_WIDTH, 1)),
                  _resident((ATTN_WIDTH + GMLP_WIDTH, d)), _resident((1, d))],
        out_specs=pl.BlockSpec((tm, d), lambda i: (i, 0)),
        out_shape=jax.ShapeDtypeStruct((t, d), F32),
        compiler_params=_params(("arbitrary",)),
    )(x, ya_t, yg_t, gao, ggo, wout, b_out)


def _lane_fold(v):
    out = v[:, :LANES]
    for j in range(1, v.shape[1] // LANES):
        out = out + v[:, j * LANES:(j + 1) * LANES]
    return out


def mix_out_bwd(dx, ya_t, yg_t, gao, ggo, wout, name, deps=()):
    t, d = dx.shape
    tm = _token_tile(t)
    nt = t // tm

    def body(dx_ref, ya_ref, yg_ref, gao_ref, ggo_ref, w_ref, dya_ref, dyg_ref, y_ref, dgao_ref, dggo_ref, db_ref,
             acc_ref):
        i = pl.program_id(0)

        @pl.when(i == 0)
        def _():
            acc_ref[...] = jnp.zeros_like(acc_ref)
            db_ref[...] = jnp.zeros_like(db_ref)

        dxv = dx_ref[...]
        db_ref[...] += jnp.sum(dxv, axis=0, keepdims=True)
        dxb = dxv.astype(BF16)
        for part, (src_ref, gain_ref, dst_ref) in enumerate(((ya_ref, gao_ref, dya_ref), (yg_ref, ggo_ref, dyg_ref))):
            rows = slice(part * ATTN_WIDTH, (part + 1) * ATTN_WIDTH)
            gain = gain_ref[...]
            nrm, r, yn = _rmsnorm_cols(src_ref[...].astype(F32), gain)
            y_ref[rows, :] = yn.astype(BF16)
            dy = _dg(w_ref[rows, :], dxb, NT)
            acc_ref[rows, :] += _lane_fold(dy * nrm)
            dn = dy * gain
            dst_ref[...] = (r * (dn - nrm * jnp.mean(dn * nrm, axis=0, keepdims=True))).astype(BF16)

        @pl.when(i == nt - 1)
        def _():
            dgao_ref[...] = jnp.sum(acc_ref[:ATTN_WIDTH, :], axis=1, keepdims=True)
            dggo_ref[...] = jnp.sum(acc_ref[ATTN_WIDTH:, :], axis=1, keepdims=True)

    const2 = lambda i: (0, 0)
    feat = lambda w: pl.BlockSpec((w, tm), lambda i: (0, i))
    return pl.pallas_call(
        _drop_deps(body, 6, len(deps)), name=name, grid=(nt,),
        in_specs=[pl.BlockSpec((tm, d), lambda i: (i, 0)), feat(ATTN_WIDTH), feat(GMLP_WIDTH),
                  _resident((ATTN_WIDTH, 1)), _resident((GMLP_WIDTH, 1)), _resident((ATTN_WIDTH + GMLP_WIDTH, d))]
        + [ANY_SPEC] * len(deps),
        out_specs=[feat(ATTN_WIDTH), feat(GMLP_WIDTH), feat(ATTN_WIDTH + GMLP_WIDTH),
                   pl.BlockSpec((ATTN_WIDTH, 1), const2), pl.BlockSpec((GMLP_WIDTH, 1), const2),
                   pl.BlockSpec((1, d), const2)],
        out_shape=[jax.ShapeDtypeStruct((ATTN_WIDTH, t), BF16), jax.ShapeDtypeStruct((GMLP_WIDTH, t), BF16),
                   jax.ShapeDtypeStruct((ATTN_WIDTH + GMLP_WIDTH, t), BF16),
                   jax.ShapeDtypeStruct((ATTN_WIDTH, 1), F32), jax.ShapeDtypeStruct((GMLP_WIDTH, 1), F32),
                   jax.ShapeDtypeStruct((1, d), F32)],
        scratch_shapes=[pltpu.VMEM((ATTN_WIDTH + GMLP_WIDTH, LANES), F32)],
        compiler_params=_params(("arbitrary",)),
    )(dx, ya_t, yg_t, gao, ggo, wout, *deps)


def mix_in_bwd(x, dxo, gain, dq_t, dk_t, dv_t, dzg_t, win_t, name):
    t, d = x.shape
    w = win_t.shape[0]
    tm = _token_tile(t)
    nt = t // tm
    parts = ((0, ATTN_WIDTH), (ATTN_WIDTH, KV_WIDTH), (ATTN_WIDTH + KV_WIDTH, KV_WIDTH), (QKV_WIDTH, w - QKV_WIDTH))

    def body(x_ref, dxo_ref, g_ref, dq_ref, dk_ref, dv_ref, dzg_ref, w_ref, dx_ref, h_ref, dg_ref, db_ref, acc_ref):
        i = pl.program_id(0)

        @pl.when(i == 0)
        def _():
            acc_ref[...] = jnp.zeros_like(acc_ref)
            dg_ref[...] = jnp.zeros_like(dg_ref)

        xv = x_ref[...]
        gain_v = g_ref[...]
        r = lax.rsqrt(jnp.mean(xv * xv, axis=-1, keepdims=True) + EPS)
        h_ref[...] = (xv * r * gain_v).astype(BF16)
        dh = jnp.zeros((tm, d), F32)
        for (off, size), src_ref in zip(parts, (dq_ref, dk_ref, dv_ref, dzg_ref)):
            dp = src_ref[...]
            acc_ref[off:off + size, :] += _lane_fold(dp.astype(F32))
            dh = dh + _dg(dp, w_ref[off:off + size, :], TN)
        dx, dgain = _rmsnorm_bwd_rows(xv, r, gain_v, dh)
        dx_ref[...] = dxo_ref[...] + dx
        dg_ref[...] += dgain

        @pl.when(i == nt - 1)
        def _():
            db_ref[...] = jnp.sum(acc_ref[...], axis=1, keepdims=True)

    const2 = lambda i: (0, 0)
    tok = lambda: pl.BlockSpec((tm, d), lambda i: (i, 0))
    feat = lambda rows: pl.BlockSpec((rows, tm), lambda i: (0, i))
    return pl.pallas_call(
        body, name=name, grid=(nt,),
        in_specs=[tok(), tok(), _resident((1, d)), feat(ATTN_WIDTH), feat(KV_WIDTH), feat(KV_WIDTH),
                  feat(w - QKV_WIDTH), _resident((w, d))],
        out_specs=[tok(), tok(), pl.BlockSpec((1, d), const2), pl.BlockSpec((w, 1), const2)],
        out_shape=[jax.ShapeDtypeStruct((t, d), F32), jax.ShapeDtypeStruct((t, d), BF16),
                   jax.ShapeDtypeStruct((1, d), F32), jax.ShapeDtypeStruct((w, 1), F32)],
        scratch_shapes=[pltpu.VMEM((w, LANES), F32)],
        compiler_params=_params(("arbitrary",)),
    )(x, dxo, gain, dq_t, dk_t, dv_t, dzg_t, win_t)


def _adamw_math(w, g, m, v):
    m = ADAM_B1 * m + (1.0 - ADAM_B1) * g
    v = ADAM_B2 * v + (1.0 - ADAM_B2) * (g * g)
    m_hat = m / (1.0 - ADAM_B1 ** ADAM_STEP)
    v_hat = v / (1.0 - ADAM_B2 ** ADAM_STEP)
    delta = -ADAM_LR * (m_hat / (jnp.sqrt(v_hat) + ADAM_EPS) + ADAM_WD * w)
    return delta, m, v


def adamw(w, g, m, v, name):
    rows, cols = w.shape
    tr = rows
    if rows * cols > 256 * 1024:
        for cand in (256, 128, 64, 32, 16, 8):
            if rows % cand == 0:
                tr = cand
                break

    def body(w_ref, g_ref, m_ref, v_ref, d_ref, mo_ref, vo_ref):
        delta, mn, vn = _adamw_math(w_ref[...], g_ref[...], m_ref[...], v_ref[...])
        d_ref[...] = delta
        mo_ref[...] = mn
        vo_ref[...] = vn

    spec = pl.BlockSpec((tr, cols), lambda i: (i, 0))
    shape = jax.ShapeDtypeStruct((rows, cols), F32)
    return pl.pallas_call(
        body, name=name, grid=(rows // tr,),
        in_specs=[spec] * 4, out_specs=[spec] * 3, out_shape=[shape] * 3,
        compiler_params=_params(("arbitrary",)),
    )(w, g, m, v)


def sum_adamw(landing, grad3d, me, w, m, v, name):
    nd, rows, cols = landing.shape
    tr = rows // 2

    def body(me_ref, l_ref, own_ref, w_ref, m_ref, v_ref, g_ref, d_ref, mo_ref, vo_ref):
        g = own_ref[...].astype(F32)
        for j in range(nd):
            g = g + l_ref[j].astype(F32)
        delta, mn, vn = _adamw_math(w_ref[...], g, m_ref[...], v_ref[...])
        g_ref[...] = g
        d_ref[...] = delta
        mo_ref[...] = mn
        vo_ref[...] = vn

    spec = lambda: pl.BlockSpec((tr, cols), lambda i, me_ref: (i, 0))
    shape = jax.ShapeDtypeStruct((rows, cols), F32)
    return pl.pallas_call(
        body, name=name,
        grid_spec=pltpu.PrefetchScalarGridSpec(
            num_scalar_prefetch=1, grid=(rows // tr,),
            in_specs=[pl.BlockSpec((nd, tr, cols), lambda i, me_ref: (0, i, 0)),
                      pl.BlockSpec((None, tr, cols), lambda i, me_ref: (me_ref[0], i, 0)), spec(), spec(), spec()],
            out_specs=[spec() for _ in range(4)]),
        out_shape=[shape] * 4,
        compiler_params=_params(("arbitrary",)),
    )(me, landing, grad3d, w, m, v)


def _mesh_place():
    x, y, c = lax.axis_index("x"), lax.axis_index("y"), lax.axis_index("c")
    return x, y, c, 4 * x + 2 * y + c


def _peer(x, y, c, k):
    return (x ^ ((k >> 2) & 1), y ^ ((k >> 1) & 1), c ^ (k & 1))


def prep_shards(mats, me, name):
    na = len(mats)

    def body(me_ref, *refs):
        in_refs, shard_refs, land_refs = refs[:na], refs[na:2 * na], refs[2 * na:]
        for a in range(na):
            val = in_refs[a][...].astype(BF16)
            shard_refs[a][...] = val
            land_refs[a][...] = val

    whole = lambda mat: pl.BlockSpec(mat.shape, lambda i, me_ref: (0, 0))
    outs = pl.pallas_call(
        body, name=name,
        grid_spec=pltpu.PrefetchScalarGridSpec(
            num_scalar_prefetch=1, grid=(1,),
            in_specs=[whole(mat) for mat in mats],
            out_specs=[whole(mat) for mat in mats]
            + [pl.BlockSpec((None,) + mat.shape, lambda i, me_ref: (me_ref[0], 0, 0)) for mat in mats]),
        out_shape=[jax.ShapeDtypeStruct(mat.shape, BF16) for mat in mats]
        + [jax.ShapeDtypeStruct((N_DEV,) + mat.shape, BF16) for mat in mats],
        compiler_params=_params(("arbitrary",)),
    )(me, *mats)
    return outs[:na], outs[na:]


def allgather_rows(shards, lands, name):
    na = n_gather = len(shards)

    def body(*refs):
        in_refs, out_refs = refs[:na], refs[2 * na:3 * na]
        send_sems, recv_sems = refs[3 * na:]
        x, y, c, me = _mesh_place()

        def copy(sem, a, origin, src, k):
            return pltpu.make_async_remote_copy(
                src_ref=src, dst_ref=out_refs[a].at[origin], send_sem=send_sems.at[sem, a],
                recv_sem=recv_sems.at[sem, a], device_id=_peer(x, y, c, k), device_id_type=pl.DeviceIdType.MESH)

        kx, ky = 4, 2

        def recv(sem, a, origin):
            copy(sem, a, origin, in_refs[a], 1).wait_recv()

        def landed(a, origin):
            return out_refs[a].at[origin]

        @pl.when(c == 1)
        def _():
            sends = []

            def send(sem, a, origin, src, k):
                cp = copy(sem, a, origin, src, k)
                cp.start()
                sends.append(cp)

            for a in range(n_gather):
                send(1, a, me, in_refs[a], kx)
                send(2, a, me, in_refs[a], ky)
                send(0, a, me, in_refs[a], 1)
            for a in range(n_gather):
                recv(0, a, me ^ 1)
                send(3, a, me ^ 1, landed(a, me ^ 1), kx)
                send(4, a, me ^ 1, landed(a, me ^ 1), ky)
            for a in range(n_gather):
                recv(1, a, me ^ kx)
                send(5, a, me ^ kx, landed(a, me ^ kx), ky)
                send(7, a, me ^ kx, landed(a, me ^ kx), 1)
                recv(2, a, me ^ ky)
                send(8, a, me ^ ky, landed(a, me ^ ky), 1)
            for a in range(n_gather):
                recv(3, a, me ^ kx ^ 1)
                send(9, a, me ^ kx ^ 1, landed(a, me ^ kx ^ 1), 1)
                recv(4, a, me ^ ky ^ 1)
                send(6, a, me ^ ky ^ 1, landed(a, me ^ ky ^ 1), kx)
                send(10, a, me ^ ky ^ 1, landed(a, me ^ ky ^ 1), 1)
            for a in range(n_gather):
                recv(5, a, me ^ 6)
                send(11, a, me ^ 6, landed(a, me ^ 6), 1)
                recv(6, a, me ^ 7)
                send(12, a, me ^ 7, landed(a, me ^ 7), 1)
            for cp in sends:
                cp.wait_send()

        @pl.when(c == 0)
        def _():
            north = me ^ 1
            own = [copy(0, a, me, in_refs[a], 1) for a in range(n_gather)]
            for cp in own:
                cp.start()
            for a in range(n_gather):
                recv(0, a, north)
                for sem, origin in ((7, north ^ kx), (8, north ^ ky), (9, north ^ kx ^ 1), (10, north ^ ky ^ 1),
                                    (11, north ^ 6), (12, north ^ 7)):
                    recv(sem, a, origin)
            for cp in own:
                cp.wait_send()

    return pl.pallas_call(
        body, name=name,
        in_specs=[ANY_SPEC] * (2 * na), out_specs=[ANY_SPEC] * na,
        out_shape=[jax.ShapeDtypeStruct(land.shape, land.dtype) for land in lands],
        input_output_aliases={na + a: a for a in range(na)},
        scratch_shapes=[pltpu.SemaphoreType.DMA((13, n_gather)), pltpu.SemaphoreType.DMA((13, n_gather))],
    )(*shards, *lands)


HBM_SPEC = pl.BlockSpec(memory_space=pltpu.HBM)
SEM_SPEC = pl.BlockSpec(memory_space=pltpu.SEMAPHORE)
SIDE_EFFECT = pltpu.SideEffectType.DATAFLOW_SIDE_EFFECTING


def _hbm(v):
    return pltpu.with_memory_space_constraint(v, pltpu.HBM)


def gather_slices(src, land, me, k):
    return src, land.at[me], land.at[me ^ k]


def exchange_slices(src, land, me, k):
    return src.at[me ^ k], land.at[k - 1], land.at[k - 1]


def split_start(groups, slices, deps, name):
    sizes = [len(srcs) for srcs, _ in groups]
    flat_src = [s for srcs, _ in groups for s in srcs]
    flat_land = [l for _, lands in groups for l in lands]
    na, ng, nd = len(flat_src), len(groups), len(deps)

    def body(*refs):
        src_refs, land_refs = refs[:na], refs[na:2 * na]
        sem_refs = refs[2 * na + nd:2 * na + nd + 2 * ng]
        token_ref = refs[-1]
        x, y, c, me = _mesh_place()
        a0 = 0
        for gi, size in enumerate(sizes):
            send_sems, recv_sems = sem_refs[2 * gi], sem_refs[2 * gi + 1]
            for k in range(1, N_DEV):
                for j in range(size):
                    src, dst, _ = slices(src_refs[a0 + j], land_refs[a0 + j], me, k)
                    pltpu.make_async_remote_copy(
                        src_ref=src, dst_ref=dst, send_sem=send_sems.at[(k - 1) * size + j],
                        recv_sem=recv_sems.at[(k - 1) * size + j],
                        device_id=_peer(x, y, c, k), device_id_type=pl.DeviceIdType.MESH).start()
            a0 += size
        token_ref[...] = jnp.zeros_like(token_ref)

    sems = [pltpu.SemaphoreType.DMA(((N_DEV - 1) * size,)) for size in sizes for _ in range(2)]
    thru = [pltpu.HBM(v.shape, v.dtype) for v in flat_src + flat_land]
    outs = pl.pallas_call(
        body, name=name,
        in_specs=[HBM_SPEC] * (2 * na) + [ANY_SPEC] * nd,
        out_specs=[SEM_SPEC] * (2 * ng) + [HBM_SPEC] * (2 * na) + [pl.BlockSpec(memory_space=pltpu.VMEM)],
        out_shape=sems + thru + [jax.ShapeDtypeStruct((8, LANES), F32)],
        input_output_aliases={i: 2 * ng + i for i in range(2 * na)},
        compiler_params=pltpu.CompilerParams(has_side_effects=SIDE_EFFECT),
    )(*[_hbm(v) for v in flat_src + flat_land], *deps)
    started, a0 = [], 0
    for gi, size in enumerate(sizes):
        srcs = outs[2 * ng + a0:2 * ng + a0 + size]
        lands = outs[2 * ng + na + a0:2 * ng + na + a0 + size]
        started.append((outs[2 * gi], outs[2 * gi + 1], list(srcs), list(lands)))
        a0 += size
    return started, outs[-1]


def split_wait(started, slices, after, name):
    send_sems, recv_sems, srcs, lands = started
    na = len(srcs)

    def body(*refs):
        src_refs, land_refs = refs[:na], refs[na:2 * na]
        send_ref, recv_ref = refs[2 * na], refs[2 * na + 1]
        x, y, c, me = _mesh_place()
        for k in range(1, N_DEV):
            for j in range(na):
                src, _, got = slices(src_refs[j], land_refs[j], me, k)
                cp = pltpu.make_async_remote_copy(
                    src_ref=src, dst_ref=got, send_sem=send_ref.at[(k - 1) * na + j], recv_sem=recv_ref.at[(k - 1) * na + j],
                    device_id=_peer(x, y, c, k), device_id_type=pl.DeviceIdType.MESH)
                cp.wait_send()
                cp.wait_recv()

    outs = pl.pallas_call(
        body, name=name,
        in_specs=[HBM_SPEC] * (2 * na) + [SEM_SPEC, SEM_SPEC] + [ANY_SPEC] * len(after),
        out_specs=[HBM_SPEC] * (2 * na),
        out_shape=[pltpu.HBM(v.shape, v.dtype) for v in srcs + lands],
        input_output_aliases={i: i for i in range(2 * na)},
        compiler_params=pltpu.CompilerParams(has_side_effects=SIDE_EFFECT),
    )(*srcs, *lands, send_sems, recv_sems, *after)
    return list(outs[:na]), list(outs[na:])


def place_own(pack, me, name):
    rows, cols = pack.shape

    def body(me_ref, p_ref, o_ref):
        o_ref[...] = p_ref[...]

    return pl.pallas_call(
        body, name=name,
        grid_spec=pltpu.PrefetchScalarGridSpec(
            num_scalar_prefetch=1, grid=(1,),
            in_specs=[pl.BlockSpec((rows, cols), lambda i, me_ref: (0, 0))],
            out_specs=pl.BlockSpec((None, rows, cols), lambda i, me_ref: (me_ref[0], 0, 0))),
        out_shape=jax.ShapeDtypeStruct((N_DEV, rows, cols), F32),
        compiler_params=_params(("arbitrary",)),
    )(me, pack)


def sum_slots(slots, name):
    nd, rows, cols = slots.shape

    def body(s_ref, o_ref):
        acc = s_ref[0]
        for j in range(1, nd):
            acc = acc + s_ref[j]
        o_ref[...] = acc

    return pl.pallas_call(
        body, name=name, grid=(1,),
        in_specs=[pl.BlockSpec((nd, rows, cols), lambda i: (0, 0, 0))],
        out_specs=pl.BlockSpec((rows, cols), lambda i: (0, 0)),
        out_shape=jax.ShapeDtypeStruct((rows, cols), F32),
        compiler_params=_params(("arbitrary",)),
    )(slots)


def _pack_rows(size):
    rows = -(-size // LANES)
    return -(-rows // 8) * 8


def pack_small(parts):
    out = []
    for p in parts:
        flat = p.reshape(-1).astype(F32)
        rows = _pack_rows(flat.shape[0])
        out.append(jnp.pad(flat, (0, rows * LANES - flat.shape[0])).reshape(rows, LANES))
    return jnp.concatenate(out, axis=0)


def unpack_small(pack, shapes):
    out, off = [], 0
    for shp in shapes:
        size = 1
        for s in shp:
            size *= s
        rows = _pack_rows(size)
        out.append(pack[off:off + rows].reshape(-1)[:size].reshape(shp))
        off += rows
    return out


def local_step(x, target, small, get_w, put_g, put_small):
    col = lambda v: v.reshape(-1, 1)
    (wg1, wu1, wd1), deps = get_w(("wg1", "wu1", "wd1"), ())
    x1, sil1, gp1, s1, h1 = ffn_fwd(x, small["ffn1_norm_g"], wg1, wu1, wd1, "ffn1_fwd", deps)
    (win, wout), _ = get_w(("win", "wout"), (x1,))
    qkv_t, zg_t = mix_in_fwd(x1, small["mix_norm_g"], win, col(small["b_in"]), "mix_in_fwd")
    ya_t = attn_fwd(qkv_t, small["attn_sinks"], "attn_fwd")
    lng, lnb = col(small["gmlp_ln_g"]), col(small["gmlp_ln_b"])
    w_s, b_s = small["gmlp_w_s"][0], small["gmlp_b_s"][0]
    yg_t = gmlp_fwd(zg_t, lng, lnb, w_s, b_s, "gmlp_fwd")
    gao, ggo = col(small["attn_out_norm_g"]), col(small["gmlp_out_norm_g"])
    x2 = mix_out_fwd(x1, ya_t, yg_t, gao, ggo, wout, small["b_out"], "mix_out_fwd")
    (wg2, wu2, wd2), _ = get_w(("wg2", "wu2", "wd2"), (x2,))
    gs = {}
    dx3, sil2, gp2, s2, h2f, gs["final_norm_g"], loss = ffn_fwd(
        x2, small["ffn2_norm_g"], wg2, wu2, wd2, "ffn2_fwd", head=(target, small["final_norm_g"].reshape(1, -1)))

    dx2, da, db, df, gs["ffn2_norm_g"] = ffn_bwd(x2, dx3, small["ffn2_norm_g"], sil2, gp2, wg2, wu2, wd2, "ffn2_bwd")
    deps = put_g({"wg2": grad_tn(da, h2f, 1.0, "ffn2_dwg"), "wu2": grad_tn(db, h2f, 1.0, "ffn2_dwu"),
                  "wd2": grad_tn(s2, df, 1.0, "ffn2_dwd")})

    dya_t, dyg_t, y_t, dgao, dggo, gs["b_out"] = mix_out_bwd(dx2, ya_t, yg_t, gao, ggo, wout, "mix_out_bwd", deps)
    gs["attn_out_norm_g"], gs["gmlp_out_norm_g"] = dgao.reshape(1, -1), dggo.reshape(1, -1)
    g_wout = grad_nn([y_t], dx2, "dwout")
    dzg_t, dws, dbs, dlng, dlnb = gmlp_bwd(zg_t, dyg_t, lng, lnb, w_s, b_s, "gmlp_bwd")
    gs["gmlp_w_s"], gs["gmlp_b_s"] = dws[None], dbs[None]
    gs["gmlp_ln_g"], gs["gmlp_ln_b"] = dlng.reshape(1, -1), dlnb.reshape(1, -1)
    dq_t, dk_t, dv_t, dsinks = attn_bwd(qkv_t, dya_t, small["attn_sinks"], "attn_bwd")
    gs["attn_sinks"] = dsinks.reshape(1, -1)
    dx1, h2, gs["mix_norm_g"], dbin = mix_in_bwd(x1, dx2, small["mix_norm_g"], dq_t, dk_t, dv_t, dzg_t, win,
                                                 "mix_in_bwd")
    gs["b_in"] = dbin.reshape(1, -1)
    deps = put_g({"wout": g_wout, "win": grad_nn([dq_t, dk_t, dv_t, dzg_t], h2, "dwin")})

    dx0, da, db, df, gs["ffn1_norm_g"] = ffn_bwd(x, dx1, small["ffn1_norm_g"], sil1, gp1, wg1, wu1, wd1, "ffn1_bwd",
                                             deps)
    deps = put_small(gs, loss)
    deps = put_g({"wg1": grad_tn(da, h1, 1.0, "ffn1_dwg", deps)})
    deps = put_g({"wu1": grad_tn(db, h1, 1.0, "ffn1_dwu", deps)})
    put_g({"wd1": grad_tn(s1, df, 1.0, "ffn1_dwd", deps)})
    return dx0


SMALL_NAMES = ["ffn1_norm_g", "mix_norm_g", "b_in", "attn_sinks", "gmlp_ln_g", "gmlp_ln_b", "gmlp_w_s", "gmlp_b_s",
               "attn_out_norm_g", "gmlp_out_norm_g", "b_out", "ffn2_norm_g", "final_norm_g"]
BIG_NAMES = {"wg1": ("ffn1_w_gate", True), "wu1": ("ffn1_w_up", True), "wd1": ("ffn1_w_down", False),
             "win": ("w_in", True), "wout": ("w_out", False),
             "wg2": ("ffn2_w_gate", True), "wu2": ("ffn2_w_up", True), "wd2": ("ffn2_w_down", False)}
WEIGHT_ORDER = ["ffn1_norm_g", "ffn1_w_gate", "ffn1_w_up", "ffn1_w_down", "mix_norm_g", "w_in", "b_in", "attn_sinks",
                "gmlp_ln_g", "gmlp_ln_b", "gmlp_w_s", "gmlp_b_s", "attn_out_norm_g", "gmlp_out_norm_g", "w_out",
                "b_out", "ffn2_norm_g", "ffn2_w_gate", "ffn2_w_up", "ffn2_w_down", "final_norm_g"]


def kernel(x, ffn1_norm_g, ffn1_w_gate, ffn1_w_up, ffn1_w_down, mix_norm_g, w_in, b_in, attn_sinks, gmlp_ln_g, gmlp_ln_b, gmlp_w_s, gmlp_b_s, attn_out_norm_g, gmlp_out_norm_g, w_out, b_out, ffn2_norm_g, ffn2_w_gate, ffn2_w_up, ffn2_w_down, final_norm_g, loss_target, m_ffn1_norm_g, m_ffn1_w_gate, m_ffn1_w_up, m_ffn1_w_down, m_mix_norm_g, m_w_in, m_b_in, m_attn_sinks, m_gmlp_ln_g, m_gmlp_ln_b, m_gmlp_w_s, m_gmlp_b_s, m_attn_out_norm_g, m_gmlp_out_norm_g, m_w_out, m_b_out, m_ffn2_norm_g, m_ffn2_w_gate, m_ffn2_w_up, m_ffn2_w_down, m_final_norm_g, v_ffn1_norm_g, v_ffn1_w_gate, v_ffn1_w_up, v_ffn1_w_down, v_mix_norm_g, v_w_in, v_b_in, v_attn_sinks, v_gmlp_ln_g, v_gmlp_ln_b, v_gmlp_w_s, v_gmlp_b_s, v_attn_out_norm_g, v_gmlp_out_norm_g, v_w_out, v_b_out, v_ffn2_norm_g, v_ffn2_w_gate, v_ffn2_w_up, v_ffn2_w_down, v_final_norm_g):
    args = dict(locals())
    w = {n: args[n] for n in WEIGHT_ORDER}
    m = {n: args["m_" + n] for n in WEIGHT_ORDER}
    v = {n: args["v_" + n] for n in WEIGHT_ORDER}

    d_model = x.shape[-1]
    flat = lambda g: g.reshape(-1, d_model)
    me = _mesh_place()[3].astype(jnp.int32).reshape(1)
    first = ("wg1", "wu1", "wd1")
    later = (("win", "wout"), ("wg2", "wu2", "wd2"))
    gathers, exchanges, last_token = {}, [], []

    order = list(first) + [k for grp in later for k in grp]
    mats = [w[BIG_NAMES[k][0]][0].T if BIG_NAMES[k][1] else w[BIG_NAMES[k][0]][0] for k in order]
    shard_list, land_list = prep_shards(mats, me, "prep_shards")
    shards, zones = dict(zip(order, shard_list)), dict(zip(order, land_list))

    def get_w(keys, after):
        if keys == first:
            full = allgather_rows([shards[k] for k in first], [zones[k] for k in first], "allgather_ffn1")
            started, token = split_start([([shards[k] for k in grp], [zones[k] for k in grp]) for grp in later],
                                         gather_slices, (full[0],), "gather_start")
            gathers.update(zip(later, started))
            return [flat(g) for g in full], (token,)
        _, lands = split_wait(gathers[keys], gather_slices, after, "gather_wait_" + keys[0])
        return [flat(land) for land in lands], ()

    def put_g(gb):
        keys = tuple(gb)
        g3 = [gb[k].reshape(N_DEV, -1, d_model) for k in keys]
        lands = [lax.empty((N_DEV - 1,) + g.shape[1:], BF16) for g in g3]
        started, token = split_start([(g3, lands)], exchange_slices, (), "exchange_start_" + keys[0])
        exchanges.append((keys, started[0]))
        last_token[:] = [token]
        return (token,)

    small_started = []

    def put_small(gs, loss):
        pack = pack_small([gs[n] for n in SMALL_NAMES] + [loss[:, :1]])
        started, token = split_start([([pack], [place_own(pack, me, "place_small")])], gather_slices, (), "small_start")
        small_started[:] = started
        return (token,)

    small = {n: w[n] for n in SMALL_NAMES}
    grad_x = local_step(x[0], loss_target[0], small, get_w, put_g, put_small)

    grads, deltas, new_m, new_v = {}, {}, {}, {}
    shapes = [w[n].shape for n in SMALL_NAMES]
    _, (slots,) = split_wait(small_started[0], gather_slices, tuple(last_token), "small_wait")
    total = sum_slots(slots, "sum_small")
    small_g = unpack_small(total, shapes + [(1, 1)])
    loss_total = small_g[-1].reshape(())
    gpack = total[:sum(_pack_rows(w[n].size) for n in SMALL_NAMES)]
    d_, m_, v_ = adamw(pack_small([w[n] for n in SMALL_NAMES]), gpack, pack_small([m[n] for n in SMALL_NAMES]),
                       pack_small([v[n] for n in SMALL_NAMES]), "adamw_small")
    for n, g_, dd, mm, vv in zip(SMALL_NAMES, small_g[:-1], unpack_small(d_, shapes), unpack_small(m_, shapes),
                                 unpack_small(v_, shapes)):
        grads[n], deltas[n], new_m[n], new_v[n] = g_, dd, mm, vv

    after = tuple(arr for n in SMALL_NAMES for arr in (grads[n], deltas[n], new_m[n], new_v[n]))
    for keys, started in exchanges:
        g3, lands = split_wait(started, exchange_slices, after, "exchange_wait_" + keys[0])
        for key, own, land in zip(keys, g3, lands):
            pname, transposed = BIG_NAMES[key]
            to_rows = (lambda p: p[0].T) if transposed else (lambda p: p[0])
            from_rows = (lambda r: r.T[None]) if transposed else (lambda r: r[None])
            g, d_, m_, v_ = sum_adamw(land, own, me, to_rows(w[pname]), to_rows(m[pname]), to_rows(v[pname]),
                                      "adamw_" + key)
            after = after + (g,)
            grads[pname], deltas[pname], new_m[pname], new_v[pname] = (from_rows(g), from_rows(d_), from_rows(m_),
                                                                        from_rows(v_))

    return (loss_total, grad_x[None], *[grads[n] for n in WEIGHT_ORDER], *[deltas[n] for n in WEIGHT_ORDER],
            *[new_m[n] for n in WEIGHT_ORDER], *[new_v[n] for n in WEIGHT_ORDER])
```

```python
import functools

import jax
import jax.numpy as jnp
from jax import lax
from jax.experimental import pallas as pl
from jax.experimental.pallas import tpu as pltpu

F32 = jnp.float32
BF16 = jnp.bfloat16

N_DEV = 8
N_Q_HEADS = 8
N_KV_HEADS = 2
HEAD_DIM = 64
ATTN_WIDTH = N_Q_HEADS * HEAD_DIM
KV_WIDTH = N_KV_HEADS * HEAD_DIM
QKV_WIDTH = ATTN_WIDTH + 2 * KV_WIDTH
BLOCK = 128
GMLP_GROUPS = 8
GMLP_GROUP_DIM = 64
GMLP_WIDTH = GMLP_GROUPS * GMLP_GROUP_DIM
EPS = 1e-6
FFN_RES = 0.5
ATTN_SCALE = HEAD_DIM ** -0.5
MASK_VALUE = -1e30

ADAM_LR = 0.001
ADAM_B1 = 0.9
ADAM_B2 = 0.999
ADAM_EPS = 1e-08
ADAM_WD = 0.01
ADAM_STEP = 10

V7X_VMEM_BYTES = 64 * 1024 * 1024
VMEM_LIMIT = 62 * 1024 * 1024
LANES = 128
MXU_WIDTH = 256

NT = (((1,), (1,)), ((), ()))
TN = (((0,), (0,)), ((), ()))


def _dot(a, b):
    return jnp.dot(a, b, preferred_element_type=F32)


def _dg(a, b, dims):
    return lax.dot_general(a, b, dims, preferred_element_type=F32)


def _params(semantics=None):
    return pltpu.CompilerParams(dimension_semantics=semantics, vmem_limit_bytes=VMEM_LIMIT)


def _resident(shape):
    zeros = (0,) * len(shape)
    return pl.BlockSpec(shape, lambda *_: zeros, pipeline_mode=pl.Buffered(1))


def _drop_deps(body, n_in, n_deps):
    return lambda *refs: body(*refs[:n_in], *refs[n_in + n_deps:])


ANY_SPEC = pl.BlockSpec(memory_space=pl.ANY)


def _token_tile(t):
    return min(t, 512)


def _f_chunk(f):
    for c in (256, 128):
        if f % c == 0:
            return c
    return f


def _loss_head(xv, target, gain):
    r = lax.rsqrt(jnp.mean(xv * xv, axis=-1, keepdims=True) + EPS)
    err = xv * r * gain - target
    loss = 0.5 * jnp.sum(jnp.mean(err * err, axis=-1, keepdims=True), axis=0, keepdims=True)
    dx, dgain = _rmsnorm_bwd_rows(xv, r, gain, err * (1.0 / xv.shape[-1]))
    return dx, dgain, loss


def ffn_fwd(x, gain, wg_t, wu_t, wd, name, deps=(), head=None):
    t, d = x.shape
    f = wg_t.shape[0]
    tm = _token_tile(t)
    fc = _f_chunk(f)
    weights = [wg_t, wu_t] + ([] if wd is None else [wd])
    n_in = 2 + len(weights) + (0 if head is None else 2)
    n_x = 0 if wd is None else 1

    def body(x_ref, g_ref, wg_ref, wu_ref, *rest):
        sil_ref, gp_ref, s_ref, h_ref = rest[n_in - 4 + n_x:n_in + n_x]
        xv = x_ref[...]
        r = lax.rsqrt(jnp.mean(xv * xv, axis=-1, keepdims=True) + EPS)
        h = (xv * r * g_ref[...]).astype(BF16)
        h_ref[...] = h
        for c in range(f // fc):
            sl = slice(c * fc, (c + 1) * fc)
            a = _dg(h, wg_ref[sl, :], NT)
            b = _dg(h, wu_ref[sl, :], NT)
            sig = jax.nn.sigmoid(a)
            sil = a * sig
            sil_ref[:, sl] = sil.astype(BF16)
            gp_ref[:, sl] = (b * (sig + sil - sil * sig)).astype(BF16)
            s_ref[:, sl] = (sil * b).astype(BF16)
        if wd is None:
            return
        wd_ref, xo_ref = rest[0], rest[n_in - 4]
        xo = xv + FFN_RES * _dot(s_ref[...], wd_ref[...])
        if head is None:
            xo_ref[...] = xo
        else:
            t_ref, hg_ref = rest[1:3]
            dg_ref, loss_ref = rest[-2:]

            @pl.when(pl.program_id(0) == 0)
            def _():
                dg_ref[...] = jnp.zeros_like(dg_ref)
                loss_ref[...] = jnp.zeros_like(loss_ref)

            dx, dgain, loss = _loss_head(xo, t_ref[...], hg_ref[...])
            xo_ref[...] = dx
            dg_ref[...] += dgain
            loss_ref[...] += loss

    tok = lambda: pl.BlockSpec((tm, d), lambda i: (i, 0))
    wide = lambda: pl.BlockSpec((tm, f), lambda i: (i, 0))
    const2 = lambda i: (0, 0)
    in_specs = [tok(), _resident((1, d))] + [_resident((f, d)) for _ in weights]
    out_specs = [tok()] * n_x + [wide(), wide(), wide(), tok()]
    out_shape = ([jax.ShapeDtypeStruct((t, d), F32)] * n_x + [jax.ShapeDtypeStruct((t, f), BF16)] * 3
                 + [jax.ShapeDtypeStruct((t, d), BF16)])
    operands = [x, gain] + weights
    if head is not None:
        in_specs += [tok(), _resident((1, d))]
        out_specs += [pl.BlockSpec((1, d), const2), pl.BlockSpec((1, LANES), const2)]
        out_shape += [jax.ShapeDtypeStruct((1, d), F32), jax.ShapeDtypeStruct((1, LANES), F32)]
        operands += list(head)
    return pl.pallas_call(
        _drop_deps(body, n_in, len(deps)), name=name, grid=(t // tm,),
        in_specs=in_specs + [ANY_SPEC] * len(deps), out_specs=out_specs, out_shape=out_shape,
        compiler_params=_params(("arbitrary",)),
    )(*operands, *deps)


def _rmsnorm_bwd_rows(xv, r, gain, dh):
    n = xv * r
    dgain = jnp.sum(dh * n, axis=0, keepdims=True)
    dn = dh * gain
    dx = r * (dn - n * jnp.mean(dn * n, axis=-1, keepdims=True))
    return dx, dgain


def ffn_bwd(x, dxo, gain, sil, gp, wg_t, wu_t, wd, name, deps=()):
    t, d = x.shape
    f = wd.shape[0]
    tm = _token_tile(t)
    nh = 2 if f % (2 * LANES) == 0 else 1
    fh = f // nh
    chunks = [(off, min(MXU_WIDTH, fh - off)) for off in range(0, fh, MXU_WIDTH)]

    def body(x_ref, dxo_ref, g_ref, sil_ref, gp_ref, wg_ref, wu_ref, wd_ref,
             dx_ref, da_ref, db_ref, df_ref, dg_ref, dh_ref):
        i, j = pl.program_id(0), pl.program_id(1)
        gain_v = g_ref[...]

        @pl.when(j == 0)
        def _():
            df_ref[...] = (FFN_RES * dxo_ref[...]).astype(BF16)

        df = df_ref[...]
        for off, size in chunks:
            sl = slice(off, off + size)
            rows = pl.ds(pl.multiple_of(j * fh + off, LANES), size)
            ds = _dg(df, wd_ref[rows, :], NT)
            da_ref[:, sl] = (ds * gp_ref[:, sl].astype(F32)).astype(BF16)
            db_ref[:, sl] = (ds * sil_ref[:, sl].astype(F32)).astype(BF16)
        half_rows = pl.ds(pl.multiple_of(j * fh, LANES), fh)
        part = _dot(da_ref[...], wg_ref[half_rows, :]) + _dot(db_ref[...], wu_ref[half_rows, :])

        @pl.when(j == 0)
        def _():
            dh_ref[...] = part

        @pl.when(j > 0)
        def _():
            dh_ref[...] += part

        @pl.when(j == nh - 1)
        def _():
            xv = x_ref[...]
            r = lax.rsqrt(jnp.mean(xv * xv, axis=-1, keepdims=True) + EPS)
            dx, dgain = _rmsnorm_bwd_rows(xv, r, gain_v, dh_ref[...])
            dx_ref[...] = dxo_ref[...] + dx

            @pl.when(i == 0)
            def _():
                dg_ref[...] = jnp.zeros_like(dg_ref)

            dg_ref[...] += dgain

    tok = lambda: pl.BlockSpec((tm, d), lambda i, j: (i, 0))
    half = lambda: pl.BlockSpec((tm, fh), lambda i, j: (i, j))
    return pl.pallas_call(
        _drop_deps(body, 8, len(deps)), name=name, grid=(t // tm, nh),
        in_specs=[tok(), tok(), _resident((1, d)), half(), half(),
                  _resident((f, d)), _resident((f, d)), _resident((f, d))] + [ANY_SPEC] * len(deps),
        out_specs=[tok(), half(), half(), tok(), pl.BlockSpec((1, d), lambda i, j: (0, 0))],
        out_shape=[jax.ShapeDtypeStruct((t, d), F32), jax.ShapeDtypeStruct((t, f), BF16),
                   jax.ShapeDtypeStruct((t, f), BF16), jax.ShapeDtypeStruct((t, d), BF16),
                   jax.ShapeDtypeStruct((1, d), F32)],
        scratch_shapes=[pltpu.VMEM((tm, d), F32)],
        compiler_params=_params(("arbitrary", "arbitrary")),
    )(x, dxo, gain, sil, gp, wg_t, wu_t, wd, *deps)


def _row_tile(m, cap):
    if m <= cap:
        return m
    best = None
    for bm in range(LANES, cap + 1, LANES):
        if m % bm == 0:
            best = bm
    return best if best is not None else m


def grad_tn(a, b, scale, name, deps=()):
    t, m = a.shape
    n = b.shape[1]
    bm = _row_tile(m, 1408)
    bk = min(t, 2048)
    nk = t // bk

    def body(a_ref, b_ref, o_ref, acc_ref):
        k = pl.program_id(1)

        @pl.when(k == 0)
        def _():
            acc_ref[...] = jnp.zeros_like(acc_ref)

        acc_ref[...] += _dg(a_ref[...].astype(BF16), b_ref[...].astype(BF16), TN)

        @pl.when(k == nk - 1)
        def _():
            o_ref[...] = (scale * acc_ref[...]).astype(BF16)

    return pl.pallas_call(
        _drop_deps(body, 2, len(deps)), name=name, grid=(m // bm, nk),
        in_specs=[pl.BlockSpec((bk, bm), lambda i, k: (k, i)), pl.BlockSpec((bk, n), lambda i, k: (k, 0))]
        + [ANY_SPEC] * len(deps),
        out_specs=pl.BlockSpec((bm, n), lambda i, k: (i, 0)),
        out_shape=jax.ShapeDtypeStruct((m, n), BF16),
        scratch_shapes=[pltpu.VMEM((bm, n), F32)],
        compiler_params=_params(("arbitrary", "arbitrary")),
    )(a, b, *deps)


def grad_nn(a_list, b, name):
    t, n = b.shape
    ms = [a.shape[0] for a in a_list]
    m = sum(ms)
    bk = min(t, 1024)
    nk = t // bk
    na = len(a_list)

    def body(*refs):
        a_refs, b_ref, o_ref, acc_ref = refs[:na], refs[na], refs[na + 1], refs[na + 2]
        k = pl.program_id(0)

        @pl.when(k == 0)
        def _():
            acc_ref[...] = jnp.zeros_like(acc_ref)

        bv = b_ref[...].astype(BF16)
        off = 0
        for a_ref, mi in zip(a_refs, ms):
            acc_ref[off:off + mi, :] += _dot(a_ref[...].astype(BF16), bv)
            off += mi

        @pl.when(k == nk - 1)
        def _():
            o_ref[...] = acc_ref[...].astype(BF16)

    return pl.pallas_call(
        body, name=name, grid=(nk,),
        in_specs=[pl.BlockSpec((mi, bk), lambda k: (0, k)) for mi in ms] + [pl.BlockSpec((bk, n), lambda k: (k, 0))],
        out_specs=pl.BlockSpec((m, n), lambda k: (0, 0)),
        out_shape=jax.ShapeDtypeStruct((m, n), BF16),
        scratch_shapes=[pltpu.VMEM((m, n), F32)],
        compiler_params=_params(("arbitrary",)),
    )(*a_list, b)


def mix_in_fwd(x, gain, win_t, b_in_col, name, ffn_tail=None):
    t, d = x.shape
    w = win_t.shape[0]
    tm = _token_tile(t)

    def body(x_ref, g_ref, w_ref, bias_ref, *rest):
        xv = x_ref[...]
        if ffn_tail is not None:
            s_ref, wd_ref, xo_ref = rest[0], rest[1], rest[4]
            xv = xv + FFN_RES * _dot(s_ref[...], wd_ref[...])
            xo_ref[...] = xv
        qkv_ref, zg_ref = rest[-3:-1] if ffn_tail is not None else rest
        r = lax.rsqrt(jnp.mean(xv * xv, axis=-1, keepdims=True) + EPS)
        h = (xv * r * g_ref[...]).astype(BF16)
        qkv_ref[...] = (_dg(w_ref[:QKV_WIDTH, :], h, NT) + bias_ref[:QKV_WIDTH, :]).astype(BF16)
        zg_ref[...] = (_dg(w_ref[QKV_WIDTH:, :], h, NT) + bias_ref[QKV_WIDTH:, :]).astype(BF16)

    tok = lambda: pl.BlockSpec((tm, d), lambda i: (i, 0))
    in_specs = [tok(), _resident((1, d)), _resident((w, d)), _resident((w, 1))]
    out_specs = [pl.BlockSpec((QKV_WIDTH, tm), lambda i: (0, i)), pl.BlockSpec((w - QKV_WIDTH, tm), lambda i: (0, i))]
    out_shape = [jax.ShapeDtypeStruct((QKV_WIDTH, t), BF16), jax.ShapeDtypeStruct((w - QKV_WIDTH, t), BF16)]
    operands = [x, gain, win_t, b_in_col]
    if ffn_tail is not None:
        f = ffn_tail[1].shape[0]
        in_specs += [pl.BlockSpec((tm, f), lambda i: (i, 0)), _resident((f, d))]
        out_specs.append(tok())
        out_shape.append(jax.ShapeDtypeStruct((t, d), F32))
        operands += list(ffn_tail)
    return pl.pallas_call(
        body, name=name, grid=(t // tm,), in_specs=in_specs, out_specs=out_specs, out_shape=out_shape,
        compiler_params=_params(("arbitrary",)),
    )(*operands)


ATTN_REP = N_Q_HEADS // N_KV_HEADS
ATTN_BLOCKS_PER_STEP = 8


def _attn_tiles(t):
    nbs = min(ATTN_BLOCKS_PER_STEP, t // BLOCK)
    return nbs, nbs * BLOCK, t // (nbs * BLOCK)


def _attn_specs(nbs, tiles):
    last = tiles - 1
    cur = lambda n: jnp.minimum(n, last)
    prev = lambda n: jnp.maximum(jnp.minimum(n, last) * nbs - 1, 0)
    k_row = ATTN_WIDTH // KV_WIDTH
    tw = nbs * BLOCK
    return [
        pl.BlockSpec((ATTN_WIDTH, tw), lambda n: (0, cur(n))),
        pl.BlockSpec((KV_WIDTH, BLOCK), lambda n: (k_row, prev(n))),
        pl.BlockSpec((KV_WIDTH, tw), lambda n: (k_row, cur(n))),
        pl.BlockSpec((KV_WIDTH, BLOCK), lambda n: (k_row + 1, prev(n))),
        pl.BlockSpec((KV_WIDTH, tw), lambda n: (k_row + 1, cur(n))),
    ]


def _cols(b):
    return slice(b * BLOCK, (b + 1) * BLOCK)


def _band_rows(prev_ref, tile_ref, gs, b):
    before = prev_ref[gs, :] if b == 0 else tile_ref[gs, _cols(b - 1)]
    return before, tile_ref[gs, _cols(b)]


def _group_rows(ref, g, b):
    first = g * ATTN_REP
    return jnp.concatenate([ref[(first + r) * HEAD_DIM:(first + r + 1) * HEAD_DIM, _cols(b)] for r in range(ATTN_REP)],
                           axis=1)


def _ungroup_rows(ref, g, b, val):
    first = g * ATTN_REP
    for r in range(ATTN_REP):
        ref[(first + r) * HEAD_DIM:(first + r + 1) * HEAD_DIM, _cols(b)] = val[:, r * BLOCK:(r + 1) * BLOCK]


def _attn_upper():
    key = lax.broadcasted_iota(jnp.int32, (BLOCK, ATTN_REP * BLOCK), 0)
    qry = lax.broadcasted_iota(jnp.int32, (BLOCK, ATTN_REP * BLOCK), 1) & (BLOCK - 1)
    return key > qry


def _attn_probs(q, kp, kc, sink_ref, g, upper, no_prev):
    s = jnp.where(upper, _dg(kp, q, TN), _dg(kc, q, TN)) * ATTN_SCALE
    if no_prev is not None:
        s = jnp.where(upper & no_prev, MASK_VALUE, s)
    sink = jnp.concatenate([jnp.full((1, BLOCK), sink_ref[0, g * ATTN_REP + r], F32) for r in range(ATTN_REP)], axis=1)
    m = jnp.maximum(jnp.max(s, axis=0, keepdims=True), sink)
    e = jnp.exp(s - m)
    es = jnp.exp(sink - m)
    inv = 1.0 / (jnp.sum(e, axis=0, keepdims=True) + es)
    return e * inv, es * inv


def _split_band(upper, val):
    return jnp.where(upper, val, 0.0).astype(BF16), jnp.where(upper, 0.0, val).astype(BF16)


def attn_fwd(qkv_t, sinks, name):
    t = qkv_t.shape[1]
    nbs, tw, tiles = _attn_tiles(t)

    def body(sink_ref, q_ref, kp_ref, kc_ref, vp_ref, vc_ref, y_ref):
        n = pl.program_id(0)
        upper = _attn_upper()
        for b in range(nbs):
            for g in range(N_KV_HEADS):
                gs = slice(g * HEAD_DIM, (g + 1) * HEAD_DIM)
                kp, kc = _band_rows(kp_ref, kc_ref, gs, b)
                vp, vc = _band_rows(vp_ref, vc_ref, gs, b)
                p, _ = _attn_probs(_group_rows(q_ref, g, b), kp, kc, sink_ref, g, upper, n == 0 if b == 0 else None)
                pp, pc = _split_band(upper, p)
                _ungroup_rows(y_ref, g, b, (_dot(vp, pp) + _dot(vc, pc)).astype(BF16))

    return pl.pallas_call(
        body, name=name, grid=(tiles,),
        in_specs=[pl.BlockSpec(memory_space=pltpu.SMEM)] + _attn_specs(nbs, tiles),
        out_specs=pl.BlockSpec((ATTN_WIDTH, tw), lambda n: (0, n)),
        out_shape=jax.ShapeDtypeStruct((ATTN_WIDTH, t), BF16),
        compiler_params=_params(("arbitrary",)),
    )(sinks, qkv_t, qkv_t, qkv_t, qkv_t, qkv_t)


def attn_bwd(qkv_t, dy_t, sinks, name):
    t = qkv_t.shape[1]
    nbs, tw, tiles = _attn_tiles(t)
    kept = slice(0, tw - BLOCK)

    def body(sink_ref, q_ref, kp_ref, kc_ref, vp_ref, vc_ref, do_ref, dq_ref, dk_ref, dv_ref, dsink_ref,
             ck_ref, cv_ref, sacc_ref):
        n = pl.program_id(0)

        @pl.when(n == 0)
        def _():
            sacc_ref[...] = jnp.zeros_like(sacc_ref)

        @pl.when((n > 0) & (n < tiles))
        def _():
            if nbs > 1:
                dk_ref[:, kept] = ck_ref[:, kept].astype(BF16)
                dv_ref[:, kept] = cv_ref[:, kept].astype(BF16)

        @pl.when(n < tiles)
        def _():
            upper = _attn_upper()
            for b in range(nbs):
                for g in range(N_KV_HEADS):
                    gs = slice(g * HEAD_DIM, (g + 1) * HEAD_DIM)
                    kp, kc = _band_rows(kp_ref, kc_ref, gs, b)
                    vp, vc = _band_rows(vp_ref, vc_ref, gs, b)
                    q = _group_rows(q_ref, g, b)
                    do = _group_rows(do_ref, g, b)
                    p, ps = _attn_probs(q, kp, kc, sink_ref, g, upper, n == 0 if b == 0 else None)
                    dp = jnp.where(upper, _dg(vp, do, TN), _dg(vc, do, TN))
                    delta = jnp.sum(p * dp, axis=0, keepdims=True)
                    dsink = -(ps * delta)
                    for r in range(ATTN_REP):
                        hd = g * ATTN_REP + r
                        sacc_ref[hd:hd + 1, :] += dsink[:, r * BLOCK:(r + 1) * BLOCK]
                    dsp, dsc = _split_band(upper, p * (dp - delta))
                    pp, pc = _split_band(upper, p)
                    _ungroup_rows(dq_ref, g, b, (ATTN_SCALE * (_dot(kp, dsp) + _dot(kc, dsc))).astype(BF16))
                    dk_before = ATTN_SCALE * _dg(q, dsp, NT)
                    dv_before = _dg(do, pp, NT)
                    if b == 0:
                        @pl.when(n > 0)
                        def _():
                            dk_ref[gs, _cols(nbs - 1)] = (ck_ref[gs, _cols(nbs - 1)] + dk_before).astype(BF16)
                            dv_ref[gs, _cols(nbs - 1)] = (cv_ref[gs, _cols(nbs - 1)] + dv_before).astype(BF16)
                    else:
                        ck_ref[gs, _cols(b - 1)] += dk_before
                        cv_ref[gs, _cols(b - 1)] += dv_before
                    ck_ref[gs, _cols(b)] = ATTN_SCALE * _dg(q, dsc, NT)
                    cv_ref[gs, _cols(b)] = _dg(do, pc, NT)

        @pl.when(n == tiles)
        def _():
            dk_ref[...] = ck_ref[...].astype(BF16)
            dv_ref[...] = cv_ref[...].astype(BF16)
            dsink_ref[...] = jnp.sum(sacc_ref[...], axis=1, keepdims=True)

    last = tiles - 1
    done = lambda n: jnp.maximum(n - 1, 0)
    outs = pl.pallas_call(
        body, name=name, grid=(tiles + 1,),
        in_specs=[pl.BlockSpec(memory_space=pltpu.SMEM)] + _attn_specs(nbs, tiles)
        + [pl.BlockSpec((ATTN_WIDTH, tw), lambda n: (0, jnp.minimum(n, last)))],
        out_specs=[pl.BlockSpec((ATTN_WIDTH, tw), lambda n: (0, jnp.minimum(n, last))),
                   pl.BlockSpec((KV_WIDTH, tw), lambda n: (0, done(n))),
                   pl.BlockSpec((KV_WIDTH, tw), lambda n: (0, done(n))),
                   pl.BlockSpec((N_Q_HEADS, 1), lambda n: (0, 0))],
        out_shape=[jax.ShapeDtypeStruct((ATTN_WIDTH, t), BF16), jax.ShapeDtypeStruct((KV_WIDTH, t), BF16),
                   jax.ShapeDtypeStruct((KV_WIDTH, t), BF16), jax.ShapeDtypeStruct((N_Q_HEADS, 1), F32)],
        scratch_shapes=[pltpu.VMEM((KV_WIDTH, tw), F32), pltpu.VMEM((KV_WIDTH, tw), F32),
                        pltpu.VMEM((N_Q_HEADS, BLOCK), F32)],
        compiler_params=_params(("arbitrary",)),
    )(sinks, qkv_t, qkv_t, qkv_t, qkv_t, qkv_t, dy_t)
    return outs


GELU_C = 0.7978845608028654
GELU_A = 0.044715


def _gelu(x):
    return 0.5 * x * (1.0 + jnp.tanh(GELU_C * (x + GELU_A * x * x * x)))


def _gelu_grad(x):
    th = jnp.tanh(GELU_C * (x + GELU_A * x * x * x))
    return 0.5 * (1.0 + th) + 0.5 * x * (1.0 - th * th) * GELU_C * (1.0 + 3.0 * GELU_A * x * x)


def _gmlp_parts(zg, lng, lnb):
    z = _gelu(zg)
    u = z[:GMLP_WIDTH, :]
    vv = z[GMLP_WIDTH:, :]
    mu = jnp.mean(vv, axis=0, keepdims=True)
    xc = vv - mu
    rstd = lax.rsqrt(jnp.mean(xc * xc, axis=0, keepdims=True) + EPS)
    xhat = xc * rstd
    return u, xhat, rstd, xhat * lng + lnb


def _causal(w):
    row = lax.broadcasted_iota(jnp.int32, (BLOCK, BLOCK), 0)
    col = lax.broadcasted_iota(jnp.int32, (BLOCK, BLOCK), 1)
    return jnp.where(col <= row, w, 0.0)


GMLP_CHUNKS_PER_STEP = 4


def _stack_chunks(v, nc):
    return jnp.concatenate([v[:, c * BLOCK:(c + 1) * BLOCK] for c in range(nc)], axis=0)


def _unstack_chunks(v, nc):
    rows = v.shape[0] // nc
    return jnp.concatenate([v[c * rows:(c + 1) * rows, :] for c in range(nc)], axis=1)


def gmlp_fwd(zg_t, lng, lnb, w_s, b_s, name):
    t = zg_t.shape[1]
    nc = min(GMLP_CHUNKS_PER_STEP, t // BLOCK)
    tw = nc * BLOCK

    def body(zg_ref, lng_ref, lnb_ref, w_ref, bs_ref, y_ref):
        u, _, _, vn = _gmlp_parts(zg_ref[...].astype(F32), lng_ref[...], lnb_ref[...])
        for g in range(GMLP_GROUPS):
            gs = slice(g * GMLP_GROUP_DIM, (g + 1) * GMLP_GROUP_DIM)
            wc = _causal(w_ref[g]).astype(BF16)
            mixed = _dg(_stack_chunks(vn[gs, :].astype(BF16), nc), wc, NT) + bs_ref[g:g + 1, :]
            y_ref[gs, :] = (u[gs, :] * _unstack_chunks(mixed, nc)).astype(BF16)

    return pl.pallas_call(
        body, name=name, grid=(t // tw,),
        in_specs=[pl.BlockSpec((2 * GMLP_WIDTH, tw), lambda n: (0, n)), _resident((GMLP_WIDTH, 1)),
                  _resident((GMLP_WIDTH, 1)), _resident((GMLP_GROUPS, BLOCK, BLOCK)), _resident((GMLP_GROUPS, BLOCK))],
        out_specs=pl.BlockSpec((GMLP_WIDTH, tw), lambda n: (0, n)),
        out_shape=jax.ShapeDtypeStruct((GMLP_WIDTH, t), BF16),
        compiler_params=_params(("arbitrary",)),
    )(zg_t, lng, lnb, w_s, b_s)


def gmlp_bwd(zg_t, dy_t, lng, lnb, w_s, b_s, name):
    t = zg_t.shape[1]
    nc = min(GMLP_CHUNKS_PER_STEP, t // BLOCK)
    tw = nc * BLOCK
    nt = t // tw

    def body(zg_ref, dy_ref, lng_ref, lnb_ref, w_ref, bs_ref, dzg_ref, dw_ref, dbs_ref, dlng_ref, dlnb_ref,
             gacc_ref, bacc_ref):
        n = pl.program_id(0)

        @pl.when(n == 0)
        def _():
            dw_ref[...] = jnp.zeros_like(dw_ref)
            dbs_ref[...] = jnp.zeros_like(dbs_ref)
            gacc_ref[...] = jnp.zeros_like(gacc_ref)
            bacc_ref[...] = jnp.zeros_like(bacc_ref)

        zg = zg_ref[...].astype(F32)
        lng_v = lng_ref[...]
        u, xhat, rstd, vn = _gmlp_parts(zg, lng_v, lnb_ref[...])
        dy = dy_ref[...].astype(F32)
        dvn_parts = []
        for g in range(GMLP_GROUPS):
            gs = slice(g * GMLP_GROUP_DIM, (g + 1) * GMLP_GROUP_DIM)
            wc = _causal(w_ref[g]).astype(BF16)
            vn_s = _stack_chunks(vn[gs, :].astype(BF16), nc)
            mixed = _unstack_chunks(_dg(vn_s, wc, NT) + bs_ref[g:g + 1, :], nc)
            dy_g = dy[gs, :]
            dmixed = dy_g * u[gs, :]
            dm_s = _stack_chunks(dmixed.astype(BF16), nc)
            dzg_ref[gs, :] = (dy_g * mixed * _gelu_grad(zg[gs, :])).astype(BF16)
            dw_ref[g] += _causal(_dg(dm_s, vn_s, TN))
            dbs_ref[g:g + 1, :] += _lane_fold(jnp.sum(dmixed, axis=0, keepdims=True))
            dvn_parts.append(_unstack_chunks(_dot(dm_s, wc), nc))
        dvn = jnp.concatenate(dvn_parts, axis=0)
        gacc_ref[...] += _lane_fold(dvn * xhat)
        bacc_ref[...] += _lane_fold(dvn)
        dxhat = dvn * lng_v
        m1 = jnp.mean(dxhat, axis=0, keepdims=True)
        m2 = jnp.mean(dxhat * xhat, axis=0, keepdims=True)
        dvv = rstd * (dxhat - m1 - xhat * m2)
        dzg_ref[GMLP_WIDTH:, :] = (dvv * _gelu_grad(zg[GMLP_WIDTH:, :])).astype(BF16)

        @pl.when(n == nt - 1)
        def _():
            dlng_ref[...] = jnp.sum(gacc_ref[...], axis=1, keepdims=True)
            dlnb_ref[...] = jnp.sum(bacc_ref[...], axis=1, keepdims=True)

    const2 = lambda n: (0, 0)
    return pl.pallas_call(
        body, name=name, grid=(nt,),
        in_specs=[pl.BlockSpec((2 * GMLP_WIDTH, tw), lambda n: (0, n)), pl.BlockSpec((GMLP_WIDTH, tw), lambda n: (0, n)),
                  _resident((GMLP_WIDTH, 1)), _resident((GMLP_WIDTH, 1)),
                  _resident((GMLP_GROUPS, BLOCK, BLOCK)), _resident((GMLP_GROUPS, BLOCK))],
        out_specs=[pl.BlockSpec((2 * GMLP_WIDTH, tw), lambda n: (0, n)),
                   pl.BlockSpec((GMLP_GROUPS, BLOCK, BLOCK), lambda n: (0, 0, 0)),
                   pl.BlockSpec((GMLP_GROUPS, BLOCK), const2),
                   pl.BlockSpec((GMLP_WIDTH, 1), const2), pl.BlockSpec((GMLP_WIDTH, 1), const2)],
        out_shape=[jax.ShapeDtypeStruct((2 * GMLP_WIDTH, t), BF16),
                   jax.ShapeDtypeStruct((GMLP_GROUPS, BLOCK, BLOCK), F32),
                   jax.ShapeDtypeStruct((GMLP_GROUPS, BLOCK), F32),
                   jax.ShapeDtypeStruct((GMLP_WIDTH, 1), F32), jax.ShapeDtypeStruct((GMLP_WIDTH, 1), F32)],
        scratch_shapes=[pltpu.VMEM((GMLP_WIDTH, BLOCK), F32), pltpu.VMEM((GMLP_WIDTH, BLOCK), F32)],
        compiler_params=_params(("arbitrary",)),
    )(zg_t, dy_t, lng, lnb, w_s, b_s)


def _rmsnorm_cols(y, gain_col):
    r = lax.rsqrt(jnp.mean(y * y, axis=0, keepdims=True) + EPS)
    n = y * r
    return n, r, n * gain_col


def mix_out_fwd(x, ya_t, yg_t, gao, ggo, wout, b_out, name):
    t, d = x.shape
    tm = _token_tile(t)

    def body(x_ref, ya_ref, yg_ref, gao_ref, ggo_ref, w_ref, b_ref, o_ref):
        _, _, ya = _rmsnorm_cols(ya_ref[...].astype(F32), gao_ref[...])
        _, _, yg = _rmsnorm_cols(yg_ref[...].astype(F32), ggo_ref[...])
        o_ref[...] = (x_ref[...] + _dg(ya.astype(BF16), w_ref[:ATTN_WIDTH, :], TN)
                      + _dg(yg.astype(BF16), w_ref[ATTN_WIDTH:, :], TN) + b_ref[...])

    return pl.pallas_call(
        body, name=name, grid=(t // tm,),
        in_specs=[pl.BlockSpec((tm, d), lambda i: (i, 0)), pl.BlockSpec((ATTN_WIDTH, tm), lambda i: (0, i)),
                  pl.BlockSpec((GMLP_WIDTH, tm), lambda i: (0, i)), _resident((ATTN_WIDTH, 1)), _resident((GMLP_WIDTH, 1)),
                  _resident((ATTN_WIDTH + GMLP_WIDTH, d)), _resident((1, d))],
        out_specs=pl.BlockSpec((tm, d), lambda i: (i, 0)),
        out_shape=jax.ShapeDtypeStruct((t, d), F32),
        compiler_params=_params(("arbitrary",)),
    )(x, ya_t, yg_t, gao, ggo, wout, b_out)


def _lane_fold(v):
    out = v[:, :LANES]
    for j in range(1, v.shape[1] // LANES):
        out = out + v[:, j * LANES:(j + 1) * LANES]
    return out


def mix_out_bwd(dx, ya_t, yg_t, gao, ggo, wout, name, deps=()):
    t, d = dx.shape
    tm = _token_tile(t)
    nt = t // tm

    def body(dx_ref, ya_ref, yg_ref, gao_ref, ggo_ref, w_ref, dya_ref, dyg_ref, y_ref, dgao_ref, dggo_ref, db_ref,
             acc_ref):
        i = pl.program_id(0)

        @pl.when(i == 0)
        def _():
            acc_ref[...] = jnp.zeros_like(acc_ref)
            db_ref[...] = jnp.zeros_like(db_ref)

        dxv = dx_ref[...]
        db_ref[...] += jnp.sum(dxv, axis=0, keepdims=True)
        dxb = dxv.astype(BF16)
        for part, (src_ref, gain_ref, dst_ref) in enumerate(((ya_ref, gao_ref, dya_ref), (yg_ref, ggo_ref, dyg_ref))):
            rows = slice(part * ATTN_WIDTH, (part + 1) * ATTN_WIDTH)
            gain = gain_ref[...]
            nrm, r, yn = _rmsnorm_cols(src_ref[...].astype(F32), gain)
            y_ref[rows, :] = yn.astype(BF16)
            dy = _dg(w_ref[rows, :], dxb, NT)
            acc_ref[rows, :] += _lane_fold(dy * nrm)
            dn = dy * gain
            dst_ref[...] = (r * (dn - nrm * jnp.mean(dn * nrm, axis=0, keepdims=True))).astype(BF16)

        @pl.when(i == nt - 1)
        def _():
            dgao_ref[...] = jnp.sum(acc_ref[:ATTN_WIDTH, :], axis=1, keepdims=True)
            dggo_ref[...] = jnp.sum(acc_ref[ATTN_WIDTH:, :], axis=1, keepdims=True)

    const2 = lambda i: (0, 0)
    feat = lambda w: pl.BlockSpec((w, tm), lambda i: (0, i))
    return pl.pallas_call(
        _drop_deps(body, 6, len(deps)), name=name, grid=(nt,),
        in_specs=[pl.BlockSpec((tm, d), lambda i: (i, 0)), feat(ATTN_WIDTH), feat(GMLP_WIDTH),
                  _resident((ATTN_WIDTH, 1)), _resident((GMLP_WIDTH, 1)), _resident((ATTN_WIDTH + GMLP_WIDTH, d))]
        + [ANY_SPEC] * len(deps),
        out_specs=[feat(ATTN_WIDTH), feat(GMLP_WIDTH), feat(ATTN_WIDTH + GMLP_WIDTH),
                   pl.BlockSpec((ATTN_WIDTH, 1), const2), pl.BlockSpec((GMLP_WIDTH, 1), const2),
                   pl.BlockSpec((1, d), const2)],
        out_shape=[jax.ShapeDtypeStruct((ATTN_WIDTH, t), BF16), jax.ShapeDtypeStruct((GMLP_WIDTH, t), BF16),
                   jax.ShapeDtypeStruct((ATTN_WIDTH + GMLP_WIDTH, t), BF16),
                   jax.ShapeDtypeStruct((ATTN_WIDTH, 1), F32), jax.ShapeDtypeStruct((GMLP_WIDTH, 1), F32),
                   jax.ShapeDtypeStruct((1, d), F32)],
        scratch_shapes=[pltpu.VMEM((ATTN_WIDTH + GMLP_WIDTH, LANES), F32)],
        compiler_params=_params(("arbitrary",)),
    )(dx, ya_t, yg_t, gao, ggo, wout, *deps)


def mix_in_bwd(x, dxo, gain, dq_t, dk_t, dv_t, dzg_t, win_t, name):
    t, d = x.shape
    w = win_t.shape[0]
    tm = _token_tile(t)
    nt = t // tm
    parts = ((0, ATTN_WIDTH), (ATTN_WIDTH, KV_WIDTH), (ATTN_WIDTH + KV_WIDTH, KV_WIDTH), (QKV_WIDTH, w - QKV_WIDTH))

    def body(x_ref, dxo_ref, g_ref, dq_ref, dk_ref, dv_ref, dzg_ref, w_ref, dx_ref, h_ref, dg_ref, db_ref, acc_ref):
        i = pl.program_id(0)

        @pl.when(i == 0)
        def _():
            acc_ref[...] = jnp.zeros_like(acc_ref)
            dg_ref[...] = jnp.zeros_like(dg_ref)

        xv = x_ref[...]
        gain_v = g_ref[...]
        r = lax.rsqrt(jnp.mean(xv * xv, axis=-1, keepdims=True) + EPS)
        h_ref[...] = (xv * r * gain_v).astype(BF16)
        dh = jnp.zeros((tm, d), F32)
        for (off, size), src_ref in zip(parts, (dq_ref, dk_ref, dv_ref, dzg_ref)):
            dp = src_ref[...]
            acc_ref[off:off + size, :] += _lane_fold(dp.astype(F32))
            dh = dh + _dg(dp, w_ref[off:off + size, :], TN)
        dx, dgain = _rmsnorm_bwd_rows(xv, r, gain_v, dh)
        dx_ref[...] = dxo_ref[...] + dx
        dg_ref[...] += dgain

        @pl.when(i == nt - 1)
        def _():
            db_ref[...] = jnp.sum(acc_ref[...], axis=1, keepdims=True)

    const2 = lambda i: (0, 0)
    tok = lambda: pl.BlockSpec((tm, d), lambda i: (i, 0))
    feat = lambda rows: pl.BlockSpec((rows, tm), lambda i: (0, i))
    return pl.pallas_call(
        body, name=name, grid=(nt,),
        in_specs=[tok(), tok(), _resident((1, d)), feat(ATTN_WIDTH), feat(KV_WIDTH), feat(KV_WIDTH),
                  feat(w - QKV_WIDTH), _resident((w, d))],
        out_specs=[tok(), tok(), pl.BlockSpec((1, d), const2), pl.BlockSpec((w, 1), const2)],
        out_shape=[jax.ShapeDtypeStruct((t, d), F32), jax.ShapeDtypeStruct((t, d), BF16),
                   jax.ShapeDtypeStruct((1, d), F32), jax.ShapeDtypeStruct((w, 1), F32)],
        scratch_shapes=[pltpu.VMEM((w, LANES), F32)],
        compiler_params=_params(("arbitrary",)),
    )(x, dxo, gain, dq_t, dk_t, dv_t, dzg_t, win_t)


def _adamw_math(w, g, m, v):
    m = ADAM_B1 * m + (1.0 - ADAM_B1) * g
    v = ADAM_B2 * v + (1.0 - ADAM_B2) * (g * g)
    m_hat = m / (1.0 - ADAM_B1 ** ADAM_STEP)
    v_hat = v / (1.0 - ADAM_B2 ** ADAM_STEP)
    delta = -ADAM_LR * (m_hat / (jnp.sqrt(v_hat) + ADAM_EPS) + ADAM_WD * w)
    return delta, m, v


def adamw(w, g, m, v, name):
    rows, cols = w.shape
    tr = rows
    if rows * cols > 256 * 1024:
        for cand in (256, 128, 64, 32, 16, 8):
            if rows % cand == 0:
                tr = cand
                break

    def body(w_ref, g_ref, m_ref, v_ref, d_ref, mo_ref, vo_ref):
        delta, mn, vn = _adamw_math(w_ref[...], g_ref[...], m_ref[...], v_ref[...])
        d_ref[...] = delta
        mo_ref[...] = mn
        vo_ref[...] = vn

    spec = pl.BlockSpec((tr, cols), lambda i: (i, 0))
    shape = jax.ShapeDtypeStruct((rows, cols), F32)
    return pl.pallas_call(
        body, name=name, grid=(rows // tr,),
        in_specs=[spec] * 4, out_specs=[spec] * 3, out_shape=[shape] * 3,
        compiler_params=_params(("arbitrary",)),
    )(w, g, m, v)


def sum_adamw(landing, grad3d, me, w, m, v, name):
    nd, rows, cols = landing.shape
    tr = rows // 2

    def body(me_ref, l_ref, own_ref, w_ref, m_ref, v_ref, g_ref, d_ref, mo_ref, vo_ref):
        g = own_ref[...].astype(F32)
        for j in range(nd):
            g = g + l_ref[j].astype(F32)
        delta, mn, vn = _adamw_math(w_ref[...], g, m_ref[...], v_ref[...])
        g_ref[...] = g
        d_ref[...] = delta
        mo_ref[...] = mn
        vo_ref[...] = vn

    spec = lambda: pl.BlockSpec((tr, cols), lambda i, me_ref: (i, 0))
    shape = jax.ShapeDtypeStruct((rows, cols), F32)
    return pl.pallas_call(
        body, name=name,
        grid_spec=pltpu.PrefetchScalarGridSpec(
            num_scalar_prefetch=1, grid=(rows // tr,),
            in_specs=[pl.BlockSpec((nd, tr, cols), lambda i, me_ref: (0, i, 0)),
                      pl.BlockSpec((None, tr, cols), lambda i, me_ref: (me_ref[0], i, 0)), spec(), spec(), spec()],
            out_specs=[spec() for _ in range(4)]),
        out_shape=[shape] * 4,
        compiler_params=_params(("arbitrary",)),
    )(me, landing, grad3d, w, m, v)


def _mesh_place():
    x, y, c = lax.axis_index("x"), lax.axis_index("y"), lax.axis_index("c")
    return x, y, c, 4 * x + 2 * y + c


def _peer(x, y, c, k):
    return (x ^ ((k >> 2) & 1), y ^ ((k >> 1) & 1), c ^ (k & 1))


def prep_shards(mats, me, name):
    na = len(mats)

    def body(me_ref, *refs):
        in_refs, shard_refs, land_refs = refs[:na], refs[na:2 * na], refs[2 * na:]
        for a in range(na):
            val = in_refs[a][...].astype(BF16)
            shard_refs[a][...] = val
            land_refs[a][...] = val

    whole = lambda mat: pl.BlockSpec(mat.shape, lambda i, me_ref: (0, 0))
    outs = pl.pallas_call(
        body, name=name,
        grid_spec=pltpu.PrefetchScalarGridSpec(
            num_scalar_prefetch=1, grid=(1,),
            in_specs=[whole(mat) for mat in mats],
            out_specs=[whole(mat) for mat in mats]
            + [pl.BlockSpec((None,) + mat.shape, lambda i, me_ref: (me_ref[0], 0, 0)) for mat in mats]),
        out_shape=[jax.ShapeDtypeStruct(mat.shape, BF16) for mat in mats]
        + [jax.ShapeDtypeStruct((N_DEV,) + mat.shape, BF16) for mat in mats],
        compiler_params=_params(("arbitrary",)),
    )(me, *mats)
    return outs[:na], outs[na:]


def allgather_rows(shards, lands, name):
    na = n_gather = len(shards)

    def body(*refs):
        in_refs, out_refs = refs[:na], refs[2 * na:3 * na]
        send_sems, recv_sems = refs[3 * na:]
        x, y, c, me = _mesh_place()

        def copy(sem, a, origin, src, k):
            return pltpu.make_async_remote_copy(
                src_ref=src, dst_ref=out_refs[a].at[origin], send_sem=send_sems.at[sem, a],
                recv_sem=recv_sems.at[sem, a], device_id=_peer(x, y, c, k), device_id_type=pl.DeviceIdType.MESH)

        kx, ky = 4, 2

        def recv(sem, a, origin):
            copy(sem, a, origin, in_refs[a], 1).wait_recv()

        def landed(a, origin):
            return out_refs[a].at[origin]

        @pl.when(c == 1)
        def _():
            sends = []

            def send(sem, a, origin, src, k):
                cp = copy(sem, a, origin, src, k)
                cp.start()
                sends.append(cp)

            for a in range(n_gather):
                send(1, a, me, in_refs[a], kx)
                send(2, a, me, in_refs[a], ky)
                send(0, a, me, in_refs[a], 1)
            for a in range(n_gather):
                recv(0, a, me ^ 1)
                send(3, a, me ^ 1, landed(a, me ^ 1), kx)
                send(4, a, me ^ 1, landed(a, me ^ 1), ky)
            for a in range(n_gather):
                recv(1, a, me ^ kx)
                send(5, a, me ^ kx, landed(a, me ^ kx), ky)
                send(7, a, me ^ kx, landed(a, me ^ kx), 1)
                recv(2, a, me ^ ky)
                send(8, a, me ^ ky, landed(a, me ^ ky), 1)
            for a in range(n_gather):
                recv(3, a, me ^ kx ^ 1)
                send(9, a, me ^ kx ^ 1, landed(a, me ^ kx ^ 1), 1)
                recv(4, a, me ^ ky ^ 1)
                send(6, a, me ^ ky ^ 1, landed(a, me ^ ky ^ 1), kx)
                send(10, a, me ^ ky ^ 1, landed(a, me ^ ky ^ 1), 1)
            for a in range(n_gather):
                recv(5, a, me ^ 6)
                send(11, a, me ^ 6, landed(a, me ^ 6), 1)
                recv(6, a, me ^ 7)
                send(12, a, me ^ 7, landed(a, me ^ 7), 1)
            for cp in sends:
                cp.wait_send()

        @pl.when(c == 0)
        def _():
            north = me ^ 1
            own = [copy(0, a, me, in_refs[a], 1) for a in range(n_gather)]
            for cp in own:
                cp.start()
            for a in range(n_gather):
                recv(0, a, north)
                for sem, origin in ((7, north ^ kx), (8, north ^ ky), (9, north ^ kx ^ 1), (10, north ^ ky ^ 1),
                                    (11, north ^ 6), (12, north ^ 7)):
                    recv(sem, a, origin)
            for cp in own:
                cp.wait_send()

    return pl.pallas_call(
        body, name=name,
        in_specs=[ANY_SPEC] * (2 * na), out_specs=[ANY_SPEC] * na,
        out_shape=[jax.ShapeDtypeStruct(land.shape, land.dtype) for land in lands],
        input_output_aliases={na + a: a for a in range(na)},
        scratch_shapes=[pltpu.SemaphoreType.DMA((13, n_gather)), pltpu.SemaphoreType.DMA((13, n_gather))],
    )(*shards, *lands)


HBM_SPEC = pl.BlockSpec(memory_space=pltpu.HBM)
SEM_SPEC = pl.BlockSpec(memory_space=pltpu.SEMAPHORE)
SIDE_EFFECT = pltpu.SideEffectType.DATAFLOW_SIDE_EFFECTING


def _hbm(v):
    return pltpu.with_memory_space_constraint(v, pltpu.HBM)


def gather_slices(src, land, me, k):
    return src, land.at[me], land.at[me ^ k]


def exchange_slices(src, land, me, k):
    return src.at[me ^ k], land.at[k - 1], land.at[k - 1]


def split_start(groups, slices, deps, name):
    sizes = [len(srcs) for srcs, _ in groups]
    flat_src = [s for srcs, _ in groups for s in srcs]
    flat_land = [l for _, lands in groups for l in lands]
    na, ng, nd = len(flat_src), len(groups), len(deps)

    def body(*refs):
        src_refs, land_refs = refs[:na], refs[na:2 * na]
        sem_refs = refs[2 * na + nd:2 * na + nd + 2 * ng]
        token_ref = refs[-1]
        x, y, c, me = _mesh_place()
        a0 = 0
        for gi, size in enumerate(sizes):
            send_sems, recv_sems = sem_refs[2 * gi], sem_refs[2 * gi + 1]
            for k in range(1, N_DEV):
                for j in range(size):
                    src, dst, _ = slices(src_refs[a0 + j], land_refs[a0 + j], me, k)
                    pltpu.make_async_remote_copy(
                        src_ref=src, dst_ref=dst, send_sem=send_sems.at[(k - 1) * size + j],
                        recv_sem=recv_sems.at[(k - 1) * size + j],
                        device_id=_peer(x, y, c, k), device_id_type=pl.DeviceIdType.MESH).start()
            a0 += size
        token_ref[...] = jnp.zeros_like(token_ref)

    sems = [pltpu.SemaphoreType.DMA(((N_DEV - 1) * size,)) for size in sizes for _ in range(2)]
    thru = [pltpu.HBM(v.shape, v.dtype) for v in flat_src + flat_land]
    outs = pl.pallas_call(
        body, name=name,
        in_specs=[HBM_SPEC] * (2 * na) + [ANY_SPEC] * nd,
        out_specs=[SEM_SPEC] * (2 * ng) + [HBM_SPEC] * (2 * na) + [pl.BlockSpec(memory_space=pltpu.VMEM)],
        out_shape=sems + thru + [jax.ShapeDtypeStruct((8, LANES), F32)],
        input_output_aliases={i: 2 * ng + i for i in range(2 * na)},
        compiler_params=pltpu.CompilerParams(has_side_effects=SIDE_EFFECT),
    )(*[_hbm(v) for v in flat_src + flat_land], *deps)
    started, a0 = [], 0
    for gi, size in enumerate(sizes):
        srcs = outs[2 * ng + a0:2 * ng + a0 + size]
        lands = outs[2 * ng + na + a0:2 * ng + na + a0 + size]
        started.append((outs[2 * gi], outs[2 * gi + 1], list(srcs), list(lands)))
        a0 += size
    return started, outs[-1]


def split_wait(started, slices, after, name):
    send_sems, recv_sems, srcs, lands = started
    na = len(srcs)

    def body(*refs):
        src_refs, land_refs = refs[:na], refs[na:2 * na]
        send_ref, recv_ref = refs[2 * na], refs[2 * na + 1]
        x, y, c, me = _mesh_place()
        for k in range(1, N_DEV):
            for j in range(na):
                src, _, got = slices(src_refs[j], land_refs[j], me, k)
                cp = pltpu.make_async_remote_copy(
                    src_ref=src, dst_ref=got, send_sem=send_ref.at[(k - 1) * na + j], recv_sem=recv_ref.at[(k - 1) * na + j],
                    device_id=_peer(x, y, c, k), device_id_type=pl.DeviceIdType.MESH)
                cp.wait_send()
                cp.wait_recv()

    outs = pl.pallas_call(
        body, name=name,
        in_specs=[HBM_SPEC] * (2 * na) + [SEM_SPEC, SEM_SPEC] + [ANY_SPEC] * len(after),
        out_specs=[HBM_SPEC] * (2 * na),
        out_shape=[pltpu.HBM(v.shape, v.dtype) for v in srcs + lands],
        input_output_aliases={i: i for i in range(2 * na)},
        compiler_params=pltpu.CompilerParams(has_side_effects=SIDE_EFFECT),
    )(*srcs, *lands, send_sems, recv_sems, *after)
    return list(outs[:na]), list(outs[na:])


def place_own(pack, me, name):
    rows, cols = pack.shape

    def body(me_ref, p_ref, o_ref):
        o_ref[...] = p_ref[...]

    return pl.pallas_call(
        body, name=name,
        grid_spec=pltpu.PrefetchScalarGridSpec(
            num_scalar_prefetch=1, grid=(1,),
            in_specs=[pl.BlockSpec((rows, cols), lambda i, me_ref: (0, 0))],
            out_specs=pl.BlockSpec((None, rows, cols), lambda i, me_ref: (me_ref[0], 0, 0))),
        out_shape=jax.ShapeDtypeStruct((N_DEV, rows, cols), F32),
        compiler_params=_params(("arbitrary",)),
    )(me, pack)


def sum_slots(slots, name):
    nd, rows, cols = slots.shape

    def body(s_ref, o_ref):
        acc = s_ref[0]
        for j in range(1, nd):
            acc = acc + s_ref[j]
        o_ref[...] = acc

    return pl.pallas_call(
        body, name=name, grid=(1,),
        in_specs=[pl.BlockSpec((nd, rows, cols), lambda i: (0, 0, 0))],
        out_specs=pl.BlockSpec((rows, cols), lambda i: (0, 0)),
        out_shape=jax.ShapeDtypeStruct((rows, cols), F32),
        compiler_params=_params(("arbitrary",)),
    )(slots)


def _pack_rows(size):
    rows = -(-size // LANES)
    return -(-rows // 8) * 8


def pack_small(parts):
    out = []
    for p in parts:
        flat = p.reshape(-1).astype(F32)
        rows = _pack_rows(flat.shape[0])
        out.append(jnp.pad(flat, (0, rows * LANES - flat.shape[0])).reshape(rows, LANES))
    return jnp.concatenate(out, axis=0)


def unpack_small(pack, shapes):
    out, off = [], 0
    for shp in shapes:
        size = 1
        for s in shp:
            size *= s
        rows = _pack_rows(size)
        out.append(pack[off:off + rows].reshape(-1)[:size].reshape(shp))
        off += rows
    return out


def local_step(x, target, small, get_w, put_g, put_small):
    col = lambda v: v.reshape(-1, 1)
    (wg1, wu1), deps = get_w(("wg1", "wu1"), ())
    sil1, gp1, s1, h1 = ffn_fwd(x, small["ffn1_norm_g"], wg1, wu1, None, "ffn1_fwd", deps)
    (wd1, win, wout), _ = get_w(("wd1", "win", "wout"), (s1,))
    qkv_t, zg_t, x1 = mix_in_fwd(x, small["mix_norm_g"], win, col(small["b_in"]), "mix_in_fwd", ffn_tail=(s1, wd1))
    ya_t = attn_fwd(qkv_t, small["attn_sinks"], "attn_fwd")
    lng, lnb = col(small["gmlp_ln_g"]), col(small["gmlp_ln_b"])
    w_s, b_s = small["gmlp_w_s"][0], small["gmlp_b_s"][0]
    yg_t = gmlp_fwd(zg_t, lng, lnb, w_s, b_s, "gmlp_fwd")
    gao, ggo = col(small["attn_out_norm_g"]), col(small["gmlp_out_norm_g"])
    x2 = mix_out_fwd(x1, ya_t, yg_t, gao, ggo, wout, small["b_out"], "mix_out_fwd")
    (wg2, wu2, wd2), _ = get_w(("wg2", "wu2", "wd2"), (x2,))
    gs = {}
    dx3, sil2, gp2, s2, h2f, gs["final_norm_g"], loss = ffn_fwd(
        x2, small["ffn2_norm_g"], wg2, wu2, wd2, "ffn2_fwd", head=(target, small["final_norm_g"].reshape(1, -1)))

    dx2, da, db, df, gs["ffn2_norm_g"] = ffn_bwd(x2, dx3, small["ffn2_norm_g"], sil2, gp2, wg2, wu2, wd2, "ffn2_bwd")
    deps = put_g({"wg2": grad_tn(da, h2f, 1.0, "ffn2_dwg"), "wu2": grad_tn(db, h2f, 1.0, "ffn2_dwu"),
                  "wd2": grad_tn(s2, df, 1.0, "ffn2_dwd")})

    dya_t, dyg_t, y_t, dgao, dggo, gs["b_out"] = mix_out_bwd(dx2, ya_t, yg_t, gao, ggo, wout, "mix_out_bwd", deps)
    gs["attn_out_norm_g"], gs["gmlp_out_norm_g"] = dgao.reshape(1, -1), dggo.reshape(1, -1)
    g_wout = grad_nn([y_t], dx2, "dwout")
    dzg_t, dws, dbs, dlng, dlnb = gmlp_bwd(zg_t, dyg_t, lng, lnb, w_s, b_s, "gmlp_bwd")
    gs["gmlp_w_s"], gs["gmlp_b_s"] = dws[None], dbs[None]
    gs["gmlp_ln_g"], gs["gmlp_ln_b"] = dlng.reshape(1, -1), dlnb.reshape(1, -1)
    dq_t, dk_t, dv_t, dsinks = attn_bwd(qkv_t, dya_t, small["attn_sinks"], "attn_bwd")
    gs["attn_sinks"] = dsinks.reshape(1, -1)
    dx1, h2, gs["mix_norm_g"], dbin = mix_in_bwd(x1, dx2, small["mix_norm_g"], dq_t, dk_t, dv_t, dzg_t, win,
                                                 "mix_in_bwd")
    gs["b_in"] = dbin.reshape(1, -1)
    deps = put_g({"wout": g_wout, "win": grad_nn([dq_t, dk_t, dv_t, dzg_t], h2, "dwin")})

    dx0, da, db, df, gs["ffn1_norm_g"] = ffn_bwd(x, dx1, small["ffn1_norm_g"], sil1, gp1, wg1, wu1, wd1, "ffn1_bwd",
                                             deps)
    deps = put_small(gs, loss)
    deps = put_g({"wg1": grad_tn(da, h1, 1.0, "ffn1_dwg", deps)})
    deps = put_g({"wu1": grad_tn(db, h1, 1.0, "ffn1_dwu", deps)})
    put_g({"wd1": grad_tn(s1, df, 1.0, "ffn1_dwd", deps)})
    return dx0


SMALL_NAMES = ["ffn1_norm_g", "mix_norm_g", "b_in", "attn_sinks", "gmlp_ln_g", "gmlp_ln_b", "gmlp_w_s", "gmlp_b_s",
               "attn_out_norm_g", "gmlp_out_norm_g", "b_out", "ffn2_norm_g", "final_norm_g"]
BIG_NAMES = {"wg1": ("ffn1_w_gate", True), "wu1": ("ffn1_w_up", True), "wd1": ("ffn1_w_down", False),
             "win": ("w_in", True), "wout": ("w_out", False),
             "wg2": ("ffn2_w_gate", True), "wu2": ("ffn2_w_up", True), "wd2": ("ffn2_w_down", False)}
WEIGHT_ORDER = ["ffn1_norm_g", "ffn1_w_gate", "ffn1_w_up", "ffn1_w_down", "mix_norm_g", "w_in", "b_in", "attn_sinks",
                "gmlp_ln_g", "gmlp_ln_b", "gmlp_w_s", "gmlp_b_s", "attn_out_norm_g", "gmlp_out_norm_g", "w_out",
                "b_out", "ffn2_norm_g", "ffn2_w_gate", "ffn2_w_up", "ffn2_w_down", "final_norm_g"]


def kernel(x, ffn1_norm_g, ffn1_w_gate, ffn1_w_up, ffn1_w_down, mix_norm_g, w_in, b_in, attn_sinks, gmlp_ln_g, gmlp_ln_b, gmlp_w_s, gmlp_b_s, attn_out_norm_g, gmlp_out_norm_g, w_out, b_out, ffn2_norm_g, ffn2_w_gate, ffn2_w_up, ffn2_w_down, final_norm_g, loss_target, m_ffn1_norm_g, m_ffn1_w_gate, m_ffn1_w_up, m_ffn1_w_down, m_mix_norm_g, m_w_in, m_b_in, m_attn_sinks, m_gmlp_ln_g, m_gmlp_ln_b, m_gmlp_w_s, m_gmlp_b_s, m_attn_out_norm_g, m_gmlp_out_norm_g, m_w_out, m_b_out, m_ffn2_norm_g, m_ffn2_w_gate, m_ffn2_w_up, m_ffn2_w_down, m_final_norm_g, v_ffn1_norm_g, v_ffn1_w_gate, v_ffn1_w_up, v_ffn1_w_down, v_mix_norm_g, v_w_in, v_b_in, v_attn_sinks, v_gmlp_ln_g, v_gmlp_ln_b, v_gmlp_w_s, v_gmlp_b_s, v_attn_out_norm_g, v_gmlp_out_norm_g, v_w_out, v_b_out, v_ffn2_norm_g, v_ffn2_w_gate, v_ffn2_w_up, v_ffn2_w_down, v_final_norm_g):
    args = dict(locals())
    w = {n: args[n] for n in WEIGHT_ORDER}
    m = {n: args["m_" + n] for n in WEIGHT_ORDER}
    v = {n: args["v_" + n] for n in WEIGHT_ORDER}

    d_model = x.shape[-1]
    flat = lambda g: g.reshape(-1, d_model)
    me = _mesh_place()[3].astype(jnp.int32).reshape(1)
    first = ("wg1", "wu1")
    later = (("wd1", "win", "wout"), ("wg2", "wu2", "wd2"))
    gathers, exchanges, last_token = {}, [], []

    order = list(first) + [k for grp in later for k in grp]
    mats = [w[BIG_NAMES[k][0]][0].T if BIG_NAMES[k][1] else w[BIG_NAMES[k][0]][0] for k in order]
    shard_list, land_list = prep_shards(mats, me, "prep_shards")
    shards, zones = dict(zip(order, shard_list)), dict(zip(order, land_list))

    def get_w(keys, after):
        if keys == first:
            full = allgather_rows([shards[k] for k in first], [zones[k] for k in first], "allgather_ffn1")
            started, token = split_start([([shards[k] for k in grp], [zones[k] for k in grp]) for grp in later],
                                         gather_slices, (full[0],), "gather_start")
            gathers.update(zip(later, started))
            return [flat(g) for g in full], (token,)
        _, lands = split_wait(gathers[keys], gather_slices, after, "gather_wait_" + keys[0])
        return [flat(land) for land in lands], ()

    def put_g(gb):
        keys = tuple(gb)
        g3 = [gb[k].reshape(N_DEV, -1, d_model) for k in keys]
        lands = [lax.empty((N_DEV - 1,) + g.shape[1:], BF16) for g in g3]
        started, token = split_start([(g3, lands)], exchange_slices, (), "exchange_start_" + keys[0])
        exchanges.append((keys, started[0]))
        last_token[:] = [token]
        return (token,)

    small_started = []

    def put_small(gs, loss):
        pack = pack_small([gs[n] for n in SMALL_NAMES] + [loss[:, :1]])
        started, token = split_start([([pack], [place_own(pack, me, "place_small")])], gather_slices, (), "small_start")
        small_started[:] = started
        return (token,)

    small = {n: w[n] for n in SMALL_NAMES}
    grad_x = local_step(x[0], loss_target[0], small, get_w, put_g, put_small)

    grads, deltas, new_m, new_v = {}, {}, {}, {}
    shapes = [w[n].shape for n in SMALL_NAMES]
    _, (slots,) = split_wait(small_started[0], gather_slices, tuple(last_token), "small_wait")
    total = sum_slots(slots, "sum_small")
    small_g = unpack_small(total, shapes + [(1, 1)])
    loss_total = small_g[-1].reshape(())
    gpack = total[:sum(_pack_rows(w[n].size) for n in SMALL_NAMES)]
    d_, m_, v_ = adamw(pack_small([w[n] for n in SMALL_NAMES]), gpack, pack_small([m[n] for n in SMALL_NAMES]),
                       pack_small([v[n] for n in SMALL_NAMES]), "adamw_small")
    for n, g_, dd, mm, vv in zip(SMALL_NAMES, small_g[:-1], unpack_small(d_, shapes), unpack_small(m_, shapes),
                                 unpack_small(v_, shapes)):
        grads[n], deltas[n], new_m[n], new_v[n] = g_, dd, mm, vv

    after = tuple(arr for n in SMALL_NAMES for arr in (grads[n], deltas[n], new_m[n], new_v[n]))
    for keys, started in exchanges:
        g3, lands = split_wait(started, exchange_slices, after, "exchange_wait_" + keys[0])
        for key, own, land in zip(keys, g3, lands):
            pname, transposed = BIG_NAMES[key]
            to_rows = (lambda p: p[0].T) if transposed else (lambda p: p[0])
            from_rows = (lambda r: r.T[None]) if transposed else (lambda r: r[None])
            g, d_, m_, v_ = sum_adamw(land, own, me, to_rows(w[pname]), to_rows(m[pname]), to_rows(v[pname]),
                                      "adamw_" + key)
            after = after + (g,)
            grads[pname], deltas[pname], new_m[pname], new_v[pname] = (from_rows(g), from_rows(d_), from_rows(m_),
                                                                        from_rows(v_))

    return (loss_total, grad_x[None], *[grads[n] for n in WEIGHT_ORDER], *[deltas[n] for n in WEIGHT_ORDER],
            *[new_m[n] for n in WEIGHT_ORDER], *[new_v[n] for n in WEIGHT_ORDER])
```

```python
import functools

import jax
import jax.numpy as jnp
from jax import lax
from jax.experimental import pallas as pl
from jax.experimental.pallas import tpu as pltpu

F32 = jnp.float32
BF16 = jnp.bfloat16

N_DEV = 8
N_Q_HEADS = 8
N_KV_HEADS = 2
HEAD_DIM = 64
ATTN_WIDTH = N_Q_HEADS * HEAD_DIM
KV_WIDTH = N_KV_HEADS * HEAD_DIM
QKV_WIDTH = ATTN_WIDTH + 2 * KV_WIDTH
BLOCK = 128
GMLP_GROUPS = 8
GMLP_GROUP_DIM = 64
GMLP_WIDTH = GMLP_GROUPS * GMLP_GROUP_DIM
EPS = 1e-6
FFN_RES = 0.5
ATTN_SCALE = HEAD_DIM ** -0.5
MASK_VALUE = -1e30

ADAM_LR = 0.001
ADAM_B1 = 0.9
ADAM_B2 = 0.999
ADAM_EPS = 1e-08
ADAM_WD = 0.01
ADAM_STEP = 10

V7X_VMEM_BYTES = 64 * 1024 * 1024
VMEM_LIMIT = 62 * 1024 * 1024
LANES = 128
MXU_WIDTH = 256
FFN_BWD_TOKENS = 256

NT = (((1,), (1,)), ((), ()))
TN = (((0,), (0,)), ((), ()))


def _dot(a, b):
    return jnp.dot(a, b, preferred_element_type=F32)


def _dg(a, b, dims):
    return lax.dot_general(a, b, dims, preferred_element_type=F32)


def _params(semantics=None):
    return pltpu.CompilerParams(dimension_semantics=semantics, vmem_limit_bytes=VMEM_LIMIT)


def _resident(shape):
    zeros = (0,) * len(shape)
    return pl.BlockSpec(shape, lambda *_: zeros, pipeline_mode=pl.Buffered(1))


def _drop_deps(body, n_in, n_deps):
    return lambda *refs: body(*refs[:n_in], *refs[n_in + n_deps:])


ANY_SPEC = pl.BlockSpec(memory_space=pl.ANY)


def _token_tile(t):
    return min(t, 512)


def _f_chunk(f):
    for c in (256, 128):
        if f % c == 0:
            return c
    return f


def _loss_head(xv, target, gain):
    r = lax.rsqrt(jnp.mean(xv * xv, axis=-1, keepdims=True) + EPS)
    err = xv * r * gain - target
    loss = 0.5 * jnp.sum(jnp.mean(err * err, axis=-1, keepdims=True), axis=0, keepdims=True)
    dx, dgain = _rmsnorm_bwd_rows(xv, r, gain, err * (1.0 / xv.shape[-1]))
    return dx, dgain, loss


def ffn_fwd(x, gain, wg_t, wu_t, wd, name, deps=(), head=None):
    t, d = x.shape
    f = wg_t.shape[0]
    tm = _token_tile(t)
    fc = _f_chunk(f)
    weights = [wg_t, wu_t] + ([] if wd is None else [wd])
    n_in = 2 + len(weights) + (0 if head is None else 2)
    n_x = 0 if wd is None else 1

    def body(x_ref, g_ref, wg_ref, wu_ref, *rest):
        sil_ref, gp_ref, s_ref, h_ref = rest[n_in - 4 + n_x:n_in + n_x]
        xv = x_ref[...]
        r = lax.rsqrt(jnp.mean(xv * xv, axis=-1, keepdims=True) + EPS)
        h = (xv * r * g_ref[...]).astype(BF16)
        h_ref[...] = h
        for c in range(f // fc):
            sl = slice(c * fc, (c + 1) * fc)
            a = _dg(h, wg_ref[sl, :], NT)
            b = _dg(h, wu_ref[sl, :], NT)
            sig = jax.nn.sigmoid(a)
            sil = a * sig
            sil_ref[:, sl] = sil.astype(BF16)
            gp_ref[:, sl] = (b * (sig + sil - sil * sig)).astype(BF16)
            s_ref[:, sl] = (sil * b).astype(BF16)
        if wd is None:
            return
        wd_ref, xo_ref = rest[0], rest[n_in - 4]
        xo = xv + FFN_RES * _dot(s_ref[...], wd_ref[...])
        if head is None:
            xo_ref[...] = xo
        else:
            t_ref, hg_ref = rest[1:3]
            dg_ref, loss_ref = rest[-2:]

            @pl.when(pl.program_id(0) == 0)
            def _():
                dg_ref[...] = jnp.zeros_like(dg_ref)
                loss_ref[...] = jnp.zeros_like(loss_ref)

            dx, dgain, loss = _loss_head(xo, t_ref[...], hg_ref[...])
            xo_ref[...] = dx
            dg_ref[...] += dgain
            loss_ref[...] += loss

    tok = lambda: pl.BlockSpec((tm, d), lambda i: (i, 0))
    wide = lambda: pl.BlockSpec((tm, f), lambda i: (i, 0))
    const2 = lambda i: (0, 0)
    in_specs = [tok(), _resident((1, d))] + [_resident((f, d)) for _ in weights]
    out_specs = [tok()] * n_x + [wide(), wide(), wide(), tok()]
    out_shape = ([jax.ShapeDtypeStruct((t, d), F32)] * n_x + [jax.ShapeDtypeStruct((t, f), BF16)] * 3
                 + [jax.ShapeDtypeStruct((t, d), BF16)])
    operands = [x, gain] + weights
    if head is not None:
        in_specs += [tok(), _resident((1, d))]
        out_specs += [pl.BlockSpec((1, d), const2), pl.BlockSpec((1, LANES), const2)]
        out_shape += [jax.ShapeDtypeStruct((1, d), F32), jax.ShapeDtypeStruct((1, LANES), F32)]
        operands += list(head)
    return pl.pallas_call(
        _drop_deps(body, n_in, len(deps)), name=name, grid=(t // tm,),
        in_specs=in_specs + [ANY_SPEC] * len(deps), out_specs=out_specs, out_shape=out_shape,
        compiler_params=_params(("arbitrary",)),
    )(*operands, *deps)


def _rmsnorm_bwd_rows(xv, r, gain, dh):
    n = xv * r
    dgain = jnp.sum(dh * n, axis=0, keepdims=True)
    dn = dh * gain
    dx = r * (dn - n * jnp.mean(dn * n, axis=-1, keepdims=True))
    return dx, dgain


def ffn_bwd(x, dxo, gain, sil, gp, wg_t, wu_t, wd, name, deps=()):
    t, d = x.shape
    f = wd.shape[0]
    tm = min(t, FFN_BWD_TOKENS)
    nt = t // tm
    chunks = [(off, min(MXU_WIDTH, f - off)) for off in range(0, f, MXU_WIDTH)]

    def body(xp_ref, dxp_ref, dxo_ref, g_ref, sil_ref, gp_ref, wg_ref, wu_ref, wd_ref,
             dx_ref, da_ref, db_ref, df_ref, dg_ref, dh_ref):
        i = pl.program_id(0)
        gain_v = g_ref[...]

        @pl.when(i == 0)
        def _():
            dh_ref[...] = jnp.zeros_like(dh_ref)
            dg_ref[...] = jnp.zeros_like(dg_ref)

        def finish_previous(slot):
            xv = xp_ref[...]
            r = lax.rsqrt(jnp.mean(xv * xv, axis=-1, keepdims=True) + EPS)
            dx, dgain = _rmsnorm_bwd_rows(xv, r, gain_v, dh_ref[slot])
            dx_ref[...] = dxp_ref[...] + dx
            dg_ref[...] += dgain

        @pl.when(i < nt)
        def _():
            df = (FFN_RES * dxo_ref[...]).astype(BF16)
            df_ref[...] = df
            for off, size in chunks:
                sl = slice(off, off + size)
                ds = _dg(df, wd_ref[sl, :], NT)
                da_ref[:, sl] = (ds * gp_ref[:, sl].astype(F32)).astype(BF16)
                db_ref[:, sl] = (ds * sil_ref[:, sl].astype(F32)).astype(BF16)
            dh = _dot(da_ref[...], wg_ref[...]) + _dot(db_ref[...], wu_ref[...])
            finish_previous((i + 1) % 2)
            dh_ref[i % 2] = dh

        @pl.when(i == nt)
        def _():
            finish_previous((nt - 1) % 2)

    prev = lambda i: (jnp.maximum(i - 1, 0), 0)
    cur = lambda i: (jnp.minimum(i, nt - 1), 0)
    return pl.pallas_call(
        _drop_deps(body, 9, len(deps)), name=name, grid=(nt + 1,),
        in_specs=[pl.BlockSpec((tm, d), prev), pl.BlockSpec((tm, d), prev), pl.BlockSpec((tm, d), cur),
                  _resident((1, d)), pl.BlockSpec((tm, f), cur), pl.BlockSpec((tm, f), cur),
                  _resident((f, d)), _resident((f, d)), _resident((f, d))] + [ANY_SPEC] * len(deps),
        out_specs=[pl.BlockSpec((tm, d), prev), pl.BlockSpec((tm, f), cur), pl.BlockSpec((tm, f), cur),
                   pl.BlockSpec((tm, d), cur), pl.BlockSpec((1, d), lambda i: (0, 0))],
        out_shape=[jax.ShapeDtypeStruct((t, d), F32), jax.ShapeDtypeStruct((t, f), BF16),
                   jax.ShapeDtypeStruct((t, f), BF16), jax.ShapeDtypeStruct((t, d), BF16),
                   jax.ShapeDtypeStruct((1, d), F32)],
        scratch_shapes=[pltpu.VMEM((2, tm, d), F32)],
        compiler_params=_params(("arbitrary",)),
    )(x, dxo, dxo, gain, sil, gp, wg_t, wu_t, wd, *deps)


def _row_tile(m, cap):
    if m <= cap:
        return m
    best = None
    for bm in range(LANES, cap + 1, LANES):
        if m % bm == 0:
            best = bm
    return best if best is not None else m


def grad_tn(a, b, scale, name, deps=()):
    t, m = a.shape
    n = b.shape[1]
    bm = _row_tile(m, 1408)
    bk = min(t, 2048)
    nk = t // bk

    def body(a_ref, b_ref, o_ref, acc_ref):
        k = pl.program_id(1)

        @pl.when(k == 0)
        def _():
            acc_ref[...] = jnp.zeros_like(acc_ref)

        acc_ref[...] += _dg(a_ref[...].astype(BF16), b_ref[...].astype(BF16), TN)

        @pl.when(k == nk - 1)
        def _():
            o_ref[...] = (scale * acc_ref[...]).astype(BF16)

    return pl.pallas_call(
        _drop_deps(body, 2, len(deps)), name=name, grid=(m // bm, nk),
        in_specs=[pl.BlockSpec((bk, bm), lambda i, k: (k, i)), pl.BlockSpec((bk, n), lambda i, k: (k, 0))]
        + [ANY_SPEC] * len(deps),
        out_specs=pl.BlockSpec((bm, n), lambda i, k: (i, 0)),
        out_shape=jax.ShapeDtypeStruct((m, n), BF16),
        scratch_shapes=[pltpu.VMEM((bm, n), F32)],
        compiler_params=_params(("arbitrary", "arbitrary")),
    )(a, b, *deps)


def grad_nn(a_list, b, name):
    t, n = b.shape
    ms = [a.shape[0] for a in a_list]
    m = sum(ms)
    bk = min(t, 1024)
    nk = t // bk
    na = len(a_list)

    def body(*refs):
        a_refs, b_ref, o_ref, acc_ref = refs[:na], refs[na], refs[na + 1], refs[na + 2]
        k = pl.program_id(0)

        @pl.when(k == 0)
        def _():
            acc_ref[...] = jnp.zeros_like(acc_ref)

        bv = b_ref[...].astype(BF16)
        off = 0
        for a_ref, mi in zip(a_refs, ms):
            acc_ref[off:off + mi, :] += _dot(a_ref[...].astype(BF16), bv)
            off += mi

        @pl.when(k == nk - 1)
        def _():
            o_ref[...] = acc_ref[...].astype(BF16)

    return pl.pallas_call(
        body, name=name, grid=(nk,),
        in_specs=[pl.BlockSpec((mi, bk), lambda k: (0, k)) for mi in ms] + [pl.BlockSpec((bk, n), lambda k: (k, 0))],
        out_specs=pl.BlockSpec((m, n), lambda k: (0, 0)),
        out_shape=jax.ShapeDtypeStruct((m, n), BF16),
        scratch_shapes=[pltpu.VMEM((m, n), F32)],
        compiler_params=_params(("arbitrary",)),
    )(*a_list, b)


def mix_in_fwd(x, gain, win_t, b_in_col, name, ffn_tail=None):
    t, d = x.shape
    w = win_t.shape[0]
    tm = _token_tile(t)

    def body(x_ref, g_ref, w_ref, bias_ref, *rest):
        xv = x_ref[...]
        if ffn_tail is not None:
            s_ref, wd_ref, xo_ref = rest[0], rest[1], rest[4]
            xv = xv + FFN_RES * _dot(s_ref[...], wd_ref[...])
            xo_ref[...] = xv
        qkv_ref, zg_ref = rest[-3:-1] if ffn_tail is not None else rest
        r = lax.rsqrt(jnp.mean(xv * xv, axis=-1, keepdims=True) + EPS)
        h = (xv * r * g_ref[...]).astype(BF16)
        qkv_ref[...] = (_dg(w_ref[:QKV_WIDTH, :], h, NT) + bias_ref[:QKV_WIDTH, :]).astype(BF16)
        zg_ref[...] = (_dg(w_ref[QKV_WIDTH:, :], h, NT) + bias_ref[QKV_WIDTH:, :]).astype(BF16)

    tok = lambda: pl.BlockSpec((tm, d), lambda i: (i, 0))
    in_specs = [tok(), _resident((1, d)), _resident((w, d)), _resident((w, 1))]
    out_specs = [pl.BlockSpec((QKV_WIDTH, tm), lambda i: (0, i)), pl.BlockSpec((w - QKV_WIDTH, tm), lambda i: (0, i))]
    out_shape = [jax.ShapeDtypeStruct((QKV_WIDTH, t), BF16), jax.ShapeDtypeStruct((w - QKV_WIDTH, t), BF16)]
    operands = [x, gain, win_t, b_in_col]
    if ffn_tail is not None:
        f = ffn_tail[1].shape[0]
        in_specs += [pl.BlockSpec((tm, f), lambda i: (i, 0)), _resident((f, d))]
        out_specs.append(tok())
        out_shape.append(jax.ShapeDtypeStruct((t, d), F32))
        operands += list(ffn_tail)
    return pl.pallas_call(
        body, name=name, grid=(t // tm,), in_specs=in_specs, out_specs=out_specs, out_shape=out_shape,
        compiler_params=_params(("arbitrary",)),
    )(*operands)


ATTN_REP = N_Q_HEADS // N_KV_HEADS
ATTN_BLOCKS_PER_STEP = 8


def _attn_tiles(t):
    nbs = min(ATTN_BLOCKS_PER_STEP, t // BLOCK)
    return nbs, nbs * BLOCK, t // (nbs * BLOCK)


def _attn_specs(nbs, tiles):
    last = tiles - 1
    cur = lambda n: jnp.minimum(n, last)
    prev = lambda n: jnp.maximum(jnp.minimum(n, last) * nbs - 1, 0)
    k_row = ATTN_WIDTH // KV_WIDTH
    tw = nbs * BLOCK
    return [
        pl.BlockSpec((ATTN_WIDTH, tw), lambda n: (0, cur(n))),
        pl.BlockSpec((KV_WIDTH, BLOCK), lambda n: (k_row, prev(n))),
        pl.BlockSpec((KV_WIDTH, tw), lambda n: (k_row, cur(n))),
        pl.BlockSpec((KV_WIDTH, BLOCK), lambda n: (k_row + 1, prev(n))),
        pl.BlockSpec((KV_WIDTH, tw), lambda n: (k_row + 1, cur(n))),
    ]


def _cols(b):
    return slice(b * BLOCK, (b + 1) * BLOCK)


def _band_rows(prev_ref, tile_ref, gs, b):
    before = prev_ref[gs, :] if b == 0 else tile_ref[gs, _cols(b - 1)]
    return before, tile_ref[gs, _cols(b)]


def _group_rows(ref, g, b):
    first = g * ATTN_REP
    return jnp.concatenate([ref[(first + r) * HEAD_DIM:(first + r + 1) * HEAD_DIM, _cols(b)] for r in range(ATTN_REP)],
                           axis=1)


def _ungroup_rows(ref, g, b, val):
    first = g * ATTN_REP
    for r in range(ATTN_REP):
        ref[(first + r) * HEAD_DIM:(first + r + 1) * HEAD_DIM, _cols(b)] = val[:, r * BLOCK:(r + 1) * BLOCK]


def _attn_upper():
    key = lax.broadcasted_iota(jnp.int32, (BLOCK, ATTN_REP * BLOCK), 0)
    qry = lax.broadcasted_iota(jnp.int32, (BLOCK, ATTN_REP * BLOCK), 1) & (BLOCK - 1)
    return key > qry


def _attn_probs(q, kp, kc, sink_ref, g, upper, no_prev):
    s = jnp.where(upper, _dg(kp, q, TN), _dg(kc, q, TN)) * ATTN_SCALE
    if no_prev is not None:
        s = jnp.where(upper & no_prev, MASK_VALUE, s)
    sink = jnp.concatenate([jnp.full((1, BLOCK), sink_ref[0, g * ATTN_REP + r], F32) for r in range(ATTN_REP)], axis=1)
    m = jnp.maximum(jnp.max(s, axis=0, keepdims=True), sink)
    e = jnp.exp(s - m)
    es = jnp.exp(sink - m)
    inv = 1.0 / (jnp.sum(e, axis=0, keepdims=True) + es)
    return e * inv, es * inv


def _split_band(upper, val):
    return jnp.where(upper, val, 0.0).astype(BF16), jnp.where(upper, 0.0, val).astype(BF16)


def attn_fwd(qkv_t, sinks, name):
    t = qkv_t.shape[1]
    nbs, tw, tiles = _attn_tiles(t)

    def body(sink_ref, q_ref, kp_ref, kc_ref, vp_ref, vc_ref, y_ref):
        n = pl.program_id(0)
        upper = _attn_upper()
        for b in range(nbs):
            for g in range(N_KV_HEADS):
                gs = slice(g * HEAD_DIM, (g + 1) * HEAD_DIM)
                kp, kc = _band_rows(kp_ref, kc_ref, gs, b)
                vp, vc = _band_rows(vp_ref, vc_ref, gs, b)
                p, _ = _attn_probs(_group_rows(q_ref, g, b), kp, kc, sink_ref, g, upper, n == 0 if b == 0 else None)
                pp, pc = _split_band(upper, p)
                _ungroup_rows(y_ref, g, b, (_dot(vp, pp) + _dot(vc, pc)).astype(BF16))

    return pl.pallas_call(
        body, name=name, grid=(tiles,),
        in_specs=[pl.BlockSpec(memory_space=pltpu.SMEM)] + _attn_specs(nbs, tiles),
        out_specs=pl.BlockSpec((ATTN_WIDTH, tw), lambda n: (0, n)),
        out_shape=jax.ShapeDtypeStruct((ATTN_WIDTH, t), BF16),
        compiler_params=_params(("arbitrary",)),
    )(sinks, qkv_t, qkv_t, qkv_t, qkv_t, qkv_t)


def attn_bwd(qkv_t, dy_t, sinks, name):
    t = qkv_t.shape[1]
    nbs, tw, tiles = _attn_tiles(t)
    kept = slice(0, tw - BLOCK)

    def body(sink_ref, q_ref, kp_ref, kc_ref, vp_ref, vc_ref, do_ref, dq_ref, dk_ref, dv_ref, dsink_ref,
             ck_ref, cv_ref, sacc_ref):
        n = pl.program_id(0)

        @pl.when(n == 0)
        def _():
            sacc_ref[...] = jnp.zeros_like(sacc_ref)

        @pl.when((n > 0) & (n < tiles))
        def _():
            if nbs > 1:
                dk_ref[:, kept] = ck_ref[:, kept].astype(BF16)
                dv_ref[:, kept] = cv_ref[:, kept].astype(BF16)

        @pl.when(n < tiles)
        def _():
            upper = _attn_upper()
            for b in range(nbs):
                for g in range(N_KV_HEADS):
                    gs = slice(g * HEAD_DIM, (g + 1) * HEAD_DIM)
                    kp, kc = _band_rows(kp_ref, kc_ref, gs, b)
                    vp, vc = _band_rows(vp_ref, vc_ref, gs, b)
                    q = _group_rows(q_ref, g, b)
                    do = _group_rows(do_ref, g, b)
                    p, ps = _attn_probs(q, kp, kc, sink_ref, g, upper, n == 0 if b == 0 else None)
                    dp = jnp.where(upper, _dg(vp, do, TN), _dg(vc, do, TN))
                    delta = jnp.sum(p * dp, axis=0, keepdims=True)
                    dsink = -(ps * delta)
                    for r in range(ATTN_REP):
                        hd = g * ATTN_REP + r
                        sacc_ref[hd:hd + 1, :] += dsink[:, r * BLOCK:(r + 1) * BLOCK]
                    dsp, dsc = _split_band(upper, p * (dp - delta))
                    pp, pc = _split_band(upper, p)
                    _ungroup_rows(dq_ref, g, b, (ATTN_SCALE * (_dot(kp, dsp) + _dot(kc, dsc))).astype(BF16))
                    dk_before = ATTN_SCALE * _dg(q, dsp, NT)
                    dv_before = _dg(do, pp, NT)
                    if b == 0:
                        @pl.when(n > 0)
                        def _():
                            dk_ref[gs, _cols(nbs - 1)] = (ck_ref[gs, _cols(nbs - 1)] + dk_before).astype(BF16)
                            dv_ref[gs, _cols(nbs - 1)] = (cv_ref[gs, _cols(nbs - 1)] + dv_before).astype(BF16)
                    else:
                        ck_ref[gs, _cols(b - 1)] += dk_before
                        cv_ref[gs, _cols(b - 1)] += dv_before
                    ck_ref[gs, _cols(b)] = ATTN_SCALE * _dg(q, dsc, NT)
                    cv_ref[gs, _cols(b)] = _dg(do, pc, NT)

        @pl.when(n == tiles)
        def _():
            dk_ref[...] = ck_ref[...].astype(BF16)
            dv_ref[...] = cv_ref[...].astype(BF16)
            dsink_ref[...] = jnp.sum(sacc_ref[...], axis=1, keepdims=True)

    last = tiles - 1
    done = lambda n: jnp.maximum(n - 1, 0)
    outs = pl.pallas_call(
        body, name=name, grid=(tiles + 1,),
        in_specs=[pl.BlockSpec(memory_space=pltpu.SMEM)] + _attn_specs(nbs, tiles)
        + [pl.BlockSpec((ATTN_WIDTH, tw), lambda n: (0, jnp.minimum(n, last)))],
        out_specs=[pl.BlockSpec((ATTN_WIDTH, tw), lambda n: (0, jnp.minimum(n, last))),
                   pl.BlockSpec((KV_WIDTH, tw), lambda n: (0, done(n))),
                   pl.BlockSpec((KV_WIDTH, tw), lambda n: (0, done(n))),
                   pl.BlockSpec((N_Q_HEADS, 1), lambda n: (0, 0))],
        out_shape=[jax.ShapeDtypeStruct((ATTN_WIDTH, t), BF16), jax.ShapeDtypeStruct((KV_WIDTH, t), BF16),
                   jax.ShapeDtypeStruct((KV_WIDTH, t), BF16), jax.ShapeDtypeStruct((N_Q_HEADS, 1), F32)],
        scratch_shapes=[pltpu.VMEM((KV_WIDTH, tw), F32), pltpu.VMEM((KV_WIDTH, tw), F32),
                        pltpu.VMEM((N_Q_HEADS, BLOCK), F32)],
        compiler_params=_params(("arbitrary",)),
    )(sinks, qkv_t, qkv_t, qkv_t, qkv_t, qkv_t, dy_t)
    return outs


GELU_C = 0.7978845608028654
GELU_A = 0.044715


def _gelu(x):
    return 0.5 * x * (1.0 + jnp.tanh(GELU_C * (x + GELU_A * x * x * x)))


def _gelu_grad(x):
    th = jnp.tanh(GELU_C * (x + GELU_A * x * x * x))
    return 0.5 * (1.0 + th) + 0.5 * x * (1.0 - th * th) * GELU_C * (1.0 + 3.0 * GELU_A * x * x)


def _gmlp_parts(zg, lng, lnb):
    z = _gelu(zg)
    u = z[:GMLP_WIDTH, :]
    vv = z[GMLP_WIDTH:, :]
    mu = jnp.mean(vv, axis=0, keepdims=True)
    xc = vv - mu
    rstd = lax.rsqrt(jnp.mean(xc * xc, axis=0, keepdims=True) + EPS)
    xhat = xc * rstd
    return u, xhat, rstd, xhat * lng + lnb


def _causal(w):
    row = lax.broadcasted_iota(jnp.int32, (BLOCK, BLOCK), 0)
    col = lax.broadcasted_iota(jnp.int32, (BLOCK, BLOCK), 1)
    return jnp.where(col <= row, w, 0.0)


GMLP_CHUNKS_PER_STEP = 4


def _stack_chunks(v, nc):
    return jnp.concatenate([v[:, c * BLOCK:(c + 1) * BLOCK] for c in range(nc)], axis=0)


def _unstack_chunks(v, nc):
    rows = v.shape[0] // nc
    return jnp.concatenate([v[c * rows:(c + 1) * rows, :] for c in range(nc)], axis=1)


def gmlp_fwd(zg_t, lng, lnb, w_s, b_s, name):
    t = zg_t.shape[1]
    nc = min(GMLP_CHUNKS_PER_STEP, t // BLOCK)
    tw = nc * BLOCK

    def body(zg_ref, lng_ref, lnb_ref, w_ref, bs_ref, y_ref):
        u, _, _, vn = _gmlp_parts(zg_ref[...].astype(F32), lng_ref[...], lnb_ref[...])
        for g in range(GMLP_GROUPS):
            gs = slice(g * GMLP_GROUP_DIM, (g + 1) * GMLP_GROUP_DIM)
            wc = _causal(w_ref[g]).astype(BF16)
            mixed = _dg(_stack_chunks(vn[gs, :].astype(BF16), nc), wc, NT) + bs_ref[g:g + 1, :]
            y_ref[gs, :] = (u[gs, :] * _unstack_chunks(mixed, nc)).astype(BF16)

    return pl.pallas_call(
        body, name=name, grid=(t // tw,),
        in_specs=[pl.BlockSpec((2 * GMLP_WIDTH, tw), lambda n: (0, n)), _resident((GMLP_WIDTH, 1)),
                  _resident((GMLP_WIDTH, 1)), _resident((GMLP_GROUPS, BLOCK, BLOCK)), _resident((GMLP_GROUPS, BLOCK))],
        out_specs=pl.BlockSpec((GMLP_WIDTH, tw), lambda n: (0, n)),
        out_shape=jax.ShapeDtypeStruct((GMLP_WIDTH, t), BF16),
        compiler_params=_params(("arbitrary",)),
    )(zg_t, lng, lnb, w_s, b_s)


def gmlp_bwd(zg_t, dy_t, lng, lnb, w_s, b_s, name):
    t = zg_t.shape[1]
    nc = min(GMLP_CHUNKS_PER_STEP, t // BLOCK)
    tw = nc * BLOCK
    nt = t // tw

    def body(zg_ref, dy_ref, lng_ref, lnb_ref, w_ref, bs_ref, dzg_ref, dw_ref, dbs_ref, dlng_ref, dlnb_ref,
             gacc_ref, bacc_ref):
        n = pl.program_id(0)

        @pl.when(n == 0)
        def _():
            dw_ref[...] = jnp.zeros_like(dw_ref)
            dbs_ref[...] = jnp.zeros_like(dbs_ref)
            gacc_ref[...] = jnp.zeros_like(gacc_ref)
            bacc_ref[...] = jnp.zeros_like(bacc_ref)

        zg = zg_ref[...].astype(F32)
        lng_v = lng_ref[...]
        u, xhat, rstd, vn = _gmlp_parts(zg, lng_v, lnb_ref[...])
        dy = dy_ref[...].astype(F32)
        dvn_parts = []
        for g in range(GMLP_GROUPS):
            gs = slice(g * GMLP_GROUP_DIM, (g + 1) * GMLP_GROUP_DIM)
            wc = _causal(w_ref[g]).astype(BF16)
            vn_s = _stack_chunks(vn[gs, :].astype(BF16), nc)
            mixed = _unstack_chunks(_dg(vn_s, wc, NT) + bs_ref[g:g + 1, :], nc)
            dy_g = dy[gs, :]
            dmixed = dy_g * u[gs, :]
            dm_s = _stack_chunks(dmixed.astype(BF16), nc)
            dzg_ref[gs, :] = (dy_g * mixed * _gelu_grad(zg[gs, :])).astype(BF16)
            dw_ref[g] += _causal(_dg(dm_s, vn_s, TN))
            dbs_ref[g:g + 1, :] += _lane_fold(jnp.sum(dmixed, axis=0, keepdims=True))
            dvn_parts.append(_unstack_chunks(_dot(dm_s, wc), nc))
        dvn = jnp.concatenate(dvn_parts, axis=0)
        gacc_ref[...] += _lane_fold(dvn * xhat)
        bacc_ref[...] += _lane_fold(dvn)
        dxhat = dvn * lng_v
        m1 = jnp.mean(dxhat, axis=0, keepdims=True)
        m2 = jnp.mean(dxhat * xhat, axis=0, keepdims=True)
        dvv = rstd * (dxhat - m1 - xhat * m2)
        dzg_ref[GMLP_WIDTH:, :] = (dvv * _gelu_grad(zg[GMLP_WIDTH:, :])).astype(BF16)

        @pl.when(n == nt - 1)
        def _():
            dlng_ref[...] = jnp.sum(gacc_ref[...], axis=1, keepdims=True)
            dlnb_ref[...] = jnp.sum(bacc_ref[...], axis=1, keepdims=True)

    const2 = lambda n: (0, 0)
    return pl.pallas_call(
        body, name=name, grid=(nt,),
        in_specs=[pl.BlockSpec((2 * GMLP_WIDTH, tw), lambda n: (0, n)), pl.BlockSpec((GMLP_WIDTH, tw), lambda n: (0, n)),
                  _resident((GMLP_WIDTH, 1)), _resident((GMLP_WIDTH, 1)),
                  _resident((GMLP_GROUPS, BLOCK, BLOCK)), _resident((GMLP_GROUPS, BLOCK))],
        out_specs=[pl.BlockSpec((2 * GMLP_WIDTH, tw), lambda n: (0, n)),
                   pl.BlockSpec((GMLP_GROUPS, BLOCK, BLOCK), lambda n: (0, 0, 0)),
                   pl.BlockSpec((GMLP_GROUPS, BLOCK), const2),
                   pl.BlockSpec((GMLP_WIDTH, 1), const2), pl.BlockSpec((GMLP_WIDTH, 1), const2)],
        out_shape=[jax.ShapeDtypeStruct((2 * GMLP_WIDTH, t), BF16),
                   jax.ShapeDtypeStruct((GMLP_GROUPS, BLOCK, BLOCK), F32),
                   jax.ShapeDtypeStruct((GMLP_GROUPS, BLOCK), F32),
                   jax.ShapeDtypeStruct((GMLP_WIDTH, 1), F32), jax.ShapeDtypeStruct((GMLP_WIDTH, 1), F32)],
        scratch_shapes=[pltpu.VMEM((GMLP_WIDTH, BLOCK), F32), pltpu.VMEM((GMLP_WIDTH, BLOCK), F32)],
        compiler_params=_params(("arbitrary",)),
    )(zg_t, dy_t, lng, lnb, w_s, b_s)


def _rmsnorm_cols(y, gain_col):
    r = lax.rsqrt(jnp.mean(y * y, axis=0, keepdims=True) + EPS)
    n = y * r
    return n, r, n * gain_col


def mix_out_fwd(x, ya_t, yg_t, gao, ggo, wout, b_out, name):
    t, d = x.shape
    tm = _token_tile(t)

    def body(x_ref, ya_ref, yg_ref, gao_ref, ggo_ref, w_ref, b_ref, o_ref):
        _, _, ya = _rmsnorm_cols(ya_ref[...].astype(F32), gao_ref[...])
        _, _, yg = _rmsnorm_cols(yg_ref[...].astype(F32), ggo_ref[...])
        o_ref[...] = (x_ref[...] + _dg(ya.astype(BF16), w_ref[:ATTN_WIDTH, :], TN)
                      + _dg(yg.astype(BF16), w_ref[ATTN_WIDTH:, :], TN) + b_ref[...])

    return pl.pallas_call(
        body, name=name, grid=(t // tm,),
        in_specs=[pl.BlockSpec((tm, d), lambda i: (i, 0)), pl.BlockSpec((ATTN_WIDTH, tm), lambda i: (0, i)),
                  pl.BlockSpec((GMLP_WIDTH, tm), lambda i: (0, i)), _resident((ATTN_WIDTH, 1)), _resident((GMLP_WIDTH, 1)),
                  _resident((ATTN_WIDTH + GMLP_WIDTH, d)), _resident((1, d))],
        out_specs=pl.BlockSpec((tm, d), lambda i: (i, 0)),
        out_shape=jax.ShapeDtypeStruct((t, d), F32),
        compiler_params=_params(("arbitrary",)),
    )(x, ya_t, yg_t, gao, ggo, wout, b_out)


def _lane_fold(v):
    out = v[:, :LANES]
    for j in range(1, v.shape[1] // LANES):
        out = out + v[:, j * LANES:(j + 1) * LANES]
    return out


def mix_out_bwd(dx, ya_t, yg_t, gao, ggo, wout, name, deps=()):
    t, d = dx.shape
    tm = _token_tile(t)
    nt = t // tm

    def body(dx_ref, ya_ref, yg_ref, gao_ref, ggo_ref, w_ref, dya_ref, dyg_ref, y_ref, dgao_ref, dggo_ref, db_ref,
             acc_ref):
        i = pl.program_id(0)

        @pl.when(i == 0)
        def _():
            acc_ref[...] = jnp.zeros_like(acc_ref)
            db_ref[...] = jnp.zeros_like(db_ref)

        dxv = dx_ref[...]
        db_ref[...] += jnp.sum(dxv, axis=0, keepdims=True)
        dxb = dxv.astype(BF16)
        for part, (src_ref, gain_ref, dst_ref) in enumerate(((ya_ref, gao_ref, dya_ref), (yg_ref, ggo_ref, dyg_ref))):
            rows = slice(part * ATTN_WIDTH, (part + 1) * ATTN_WIDTH)
            gain = gain_ref[...]
            nrm, r, yn = _rmsnorm_cols(src_ref[...].astype(F32), gain)
            y_ref[rows, :] = yn.astype(BF16)
            dy = _dg(w_ref[rows, :], dxb, NT)
            acc_ref[rows, :] += _lane_fold(dy * nrm)
            dn = dy * gain
            dst_ref[...] = (r * (dn - nrm * jnp.mean(dn * nrm, axis=0, keepdims=True))).astype(BF16)

        @pl.when(i == nt - 1)
        def _():
            dgao_ref[...] = jnp.sum(acc_ref[:ATTN_WIDTH, :], axis=1, keepdims=True)
            dggo_ref[...] = jnp.sum(acc_ref[ATTN_WIDTH:, :], axis=1, keepdims=True)

    const2 = lambda i: (0, 0)
    feat = lambda w: pl.BlockSpec((w, tm), lambda i: (0, i))
    return pl.pallas_call(
        _drop_deps(body, 6, len(deps)), name=name, grid=(nt,),
        in_specs=[pl.BlockSpec((tm, d), lambda i: (i, 0)), feat(ATTN_WIDTH), feat(GMLP_WIDTH),
                  _resident((ATTN_WIDTH, 1)), _resident((GMLP_WIDTH, 1)), _resident((ATTN_WIDTH + GMLP_WIDTH, d))]
        + [ANY_SPEC] * len(deps),
        out_specs=[feat(ATTN_WIDTH), feat(GMLP_WIDTH), feat(ATTN_WIDTH + GMLP_WIDTH),
                   pl.BlockSpec((ATTN_WIDTH, 1), const2), pl.BlockSpec((GMLP_WIDTH, 1), const2),
                   pl.BlockSpec((1, d), const2)],
        out_shape=[jax.ShapeDtypeStruct((ATTN_WIDTH, t), BF16), jax.ShapeDtypeStruct((GMLP_WIDTH, t), BF16),
                   jax.ShapeDtypeStruct((ATTN_WIDTH + GMLP_WIDTH, t), BF16),
                   jax.ShapeDtypeStruct((ATTN_WIDTH, 1), F32), jax.ShapeDtypeStruct((GMLP_WIDTH, 1), F32),
                   jax.ShapeDtypeStruct((1, d), F32)],
        scratch_shapes=[pltpu.VMEM((ATTN_WIDTH + GMLP_WIDTH, LANES), F32)],
        compiler_params=_params(("arbitrary",)),
    )(dx, ya_t, yg_t, gao, ggo, wout, *deps)


def mix_in_bwd(x, dxo, gain, dq_t, dk_t, dv_t, dzg_t, win_t, name):
    t, d = x.shape
    w = win_t.shape[0]
    tm = _token_tile(t)
    nt = t // tm
    parts = ((0, ATTN_WIDTH), (ATTN_WIDTH, KV_WIDTH), (ATTN_WIDTH + KV_WIDTH, KV_WIDTH), (QKV_WIDTH, w - QKV_WIDTH))

    def body(x_ref, dxo_ref, g_ref, dq_ref, dk_ref, dv_ref, dzg_ref, w_ref, dx_ref, h_ref, dg_ref, db_ref, acc_ref):
        i = pl.program_id(0)

        @pl.when(i == 0)
        def _():
            acc_ref[...] = jnp.zeros_like(acc_ref)
            dg_ref[...] = jnp.zeros_like(dg_ref)

        xv = x_ref[...]
        gain_v = g_ref[...]
        r = lax.rsqrt(jnp.mean(xv * xv, axis=-1, keepdims=True) + EPS)
        h_ref[...] = (xv * r * gain_v).astype(BF16)
        dh = jnp.zeros((tm, d), F32)
        for (off, size), src_ref in zip(parts, (dq_ref, dk_ref, dv_ref, dzg_ref)):
            dp = src_ref[...]
            acc_ref[off:off + size, :] += _lane_fold(dp.astype(F32))
            dh = dh + _dg(dp, w_ref[off:off + size, :], TN)
        dx, dgain = _rmsnorm_bwd_rows(xv, r, gain_v, dh)
        dx_ref[...] = dxo_ref[...] + dx
        dg_ref[...] += dgain

        @pl.when(i == nt - 1)
        def _():
            db_ref[...] = jnp.sum(acc_ref[...], axis=1, keepdims=True)

    const2 = lambda i: (0, 0)
    tok = lambda: pl.BlockSpec((tm, d), lambda i: (i, 0))
    feat = lambda rows: pl.BlockSpec((rows, tm), lambda i: (0, i))
    return pl.pallas_call(
        body, name=name, grid=(nt,),
        in_specs=[tok(), tok(), _resident((1, d)), feat(ATTN_WIDTH), feat(KV_WIDTH), feat(KV_WIDTH),
                  feat(w - QKV_WIDTH), _resident((w, d))],
        out_specs=[tok(), tok(), pl.BlockSpec((1, d), const2), pl.BlockSpec((w, 1), const2)],
        out_shape=[jax.ShapeDtypeStruct((t, d), F32), jax.ShapeDtypeStruct((t, d), BF16),
                   jax.ShapeDtypeStruct((1, d), F32), jax.ShapeDtypeStruct((w, 1), F32)],
        scratch_shapes=[pltpu.VMEM((w, LANES), F32)],
        compiler_params=_params(("arbitrary",)),
    )(x, dxo, gain, dq_t, dk_t, dv_t, dzg_t, win_t)


def _adamw_math(w, g, m, v):
    m = ADAM_B1 * m + (1.0 - ADAM_B1) * g
    v = ADAM_B2 * v + (1.0 - ADAM_B2) * (g * g)
    m_hat = m / (1.0 - ADAM_B1 ** ADAM_STEP)
    v_hat = v / (1.0 - ADAM_B2 ** ADAM_STEP)
    delta = -ADAM_LR * (m_hat / (jnp.sqrt(v_hat) + ADAM_EPS) + ADAM_WD * w)
    return delta, m, v


def adamw(w, g, m, v, name):
    rows, cols = w.shape
    tr = rows
    if rows * cols > 256 * 1024:
        for cand in (256, 128, 64, 32, 16, 8):
            if rows % cand == 0:
                tr = cand
                break

    def body(w_ref, g_ref, m_ref, v_ref, d_ref, mo_ref, vo_ref):
        delta, mn, vn = _adamw_math(w_ref[...], g_ref[...], m_ref[...], v_ref[...])
        d_ref[...] = delta
        mo_ref[...] = mn
        vo_ref[...] = vn

    spec = pl.BlockSpec((tr, cols), lambda i: (i, 0))
    shape = jax.ShapeDtypeStruct((rows, cols), F32)
    return pl.pallas_call(
        body, name=name, grid=(rows // tr,),
        in_specs=[spec] * 4, out_specs=[spec] * 3, out_shape=[shape] * 3,
        compiler_params=_params(("arbitrary",)),
    )(w, g, m, v)


def sum_adamw(landing, grad3d, me, w, m, v, name):
    nd, rows, cols = landing.shape
    tr = rows // 2

    def body(me_ref, l_ref, own_ref, w_ref, m_ref, v_ref, g_ref, d_ref, mo_ref, vo_ref):
        g = own_ref[...].astype(F32)
        for j in range(nd):
            g = g + l_ref[j].astype(F32)
        delta, mn, vn = _adamw_math(w_ref[...], g, m_ref[...], v_ref[...])
        g_ref[...] = g
        d_ref[...] = delta
        mo_ref[...] = mn
        vo_ref[...] = vn

    spec = lambda: pl.BlockSpec((tr, cols), lambda i, me_ref: (i, 0))
    shape = jax.ShapeDtypeStruct((rows, cols), F32)
    return pl.pallas_call(
        body, name=name,
        grid_spec=pltpu.PrefetchScalarGridSpec(
            num_scalar_prefetch=1, grid=(rows // tr,),
            in_specs=[pl.BlockSpec((nd, tr, cols), lambda i, me_ref: (0, i, 0)),
                      pl.BlockSpec((None, tr, cols), lambda i, me_ref: (me_ref[0], i, 0)), spec(), spec(), spec()],
            out_specs=[spec() for _ in range(4)]),
        out_shape=[shape] * 4,
        compiler_params=_params(("arbitrary",)),
    )(me, landing, grad3d, w, m, v)


def _mesh_place():
    x, y, c = lax.axis_index("x"), lax.axis_index("y"), lax.axis_index("c")
    return x, y, c, 4 * x + 2 * y + c


def _peer(x, y, c, k):
    return (x ^ ((k >> 2) & 1), y ^ ((k >> 1) & 1), c ^ (k & 1))


def prep_shards(mats, me, name):
    na = len(mats)

    def body(me_ref, *refs):
        in_refs, shard_refs, land_refs = refs[:na], refs[na:2 * na], refs[2 * na:]
        for a in range(na):
            val = in_refs[a][...].astype(BF16)
            shard_refs[a][...] = val
            land_refs[a][...] = val

    whole = lambda mat: pl.BlockSpec(mat.shape, lambda i, me_ref: (0, 0))
    outs = pl.pallas_call(
        body, name=name,
        grid_spec=pltpu.PrefetchScalarGridSpec(
            num_scalar_prefetch=1, grid=(1,),
            in_specs=[whole(mat) for mat in mats],
            out_specs=[whole(mat) for mat in mats]
            + [pl.BlockSpec((None,) + mat.shape, lambda i, me_ref: (me_ref[0], 0, 0)) for mat in mats]),
        out_shape=[jax.ShapeDtypeStruct(mat.shape, BF16) for mat in mats]
        + [jax.ShapeDtypeStruct((N_DEV,) + mat.shape, BF16) for mat in mats],
        compiler_params=_params(("arbitrary",)),
    )(me, *mats)
    return outs[:na], outs[na:]


def allgather_rows(shards, lands, name):
    na = n_gather = len(shards)

    def body(*refs):
        in_refs, out_refs = refs[:na], refs[2 * na:3 * na]
        send_sems, recv_sems = refs[3 * na:]
        x, y, c, me = _mesh_place()

        def copy(sem, a, origin, src, k):
            return pltpu.make_async_remote_copy(
                src_ref=src, dst_ref=out_refs[a].at[origin], send_sem=send_sems.at[sem, a],
                recv_sem=recv_sems.at[sem, a], device_id=_peer(x, y, c, k), device_id_type=pl.DeviceIdType.MESH)

        kx, ky = 4, 2

        def recv(sem, a, origin):
            copy(sem, a, origin, in_refs[a], 1).wait_recv()

        def landed(a, origin):
            return out_refs[a].at[origin]

        @pl.when(c == 1)
        def _():
            sends = []

            def send(sem, a, origin, src, k):
                cp = copy(sem, a, origin, src, k)
                cp.start()
                sends.append(cp)

            for a in range(n_gather):
                send(1, a, me, in_refs[a], kx)
                send(2, a, me, in_refs[a], ky)
                send(0, a, me, in_refs[a], 1)
            for a in range(n_gather):
                recv(0, a, me ^ 1)
                send(3, a, me ^ 1, landed(a, me ^ 1), kx)
                send(4, a, me ^ 1, landed(a, me ^ 1), ky)
            for a in range(n_gather):
                recv(1, a, me ^ kx)
                send(5, a, me ^ kx, landed(a, me ^ kx), ky)
                send(7, a, me ^ kx, landed(a, me ^ kx), 1)
                recv(2, a, me ^ ky)
                send(8, a, me ^ ky, landed(a, me ^ ky), 1)
            for a in range(n_gather):
                recv(3, a, me ^ kx ^ 1)
                send(9, a, me ^ kx ^ 1, landed(a, me ^ kx ^ 1), 1)
                recv(4, a, me ^ ky ^ 1)
                send(6, a, me ^ ky ^ 1, landed(a, me ^ ky ^ 1), kx)
                send(10, a, me ^ ky ^ 1, landed(a, me ^ ky ^ 1), 1)
            for a in range(n_gather):
                recv(5, a, me ^ 6)
                send(11, a, me ^ 6, landed(a, me ^ 6), 1)
                recv(6, a, me ^ 7)
                send(12, a, me ^ 7, landed(a, me ^ 7), 1)
            for cp in sends:
                cp.wait_send()

        @pl.when(c == 0)
        def _():
            north = me ^ 1
            own = [copy(0, a, me, in_refs[a], 1) for a in range(n_gather)]
            for cp in own:
                cp.start()
            for a in range(n_gather):
                recv(0, a, north)
                for sem, origin in ((7, north ^ kx), (8, north ^ ky), (9, north ^ kx ^ 1), (10, north ^ ky ^ 1),
                                    (11, north ^ 6), (12, north ^ 7)):
                    recv(sem, a, origin)
            for cp in own:
                cp.wait_send()

    return pl.pallas_call(
        body, name=name,
        in_specs=[ANY_SPEC] * (2 * na), out_specs=[ANY_SPEC] * na,
        out_shape=[jax.ShapeDtypeStruct(land.shape, land.dtype) for land in lands],
        input_output_aliases={na + a: a for a in range(na)},
        scratch_shapes=[pltpu.SemaphoreType.DMA((13, n_gather)), pltpu.SemaphoreType.DMA((13, n_gather))],
    )(*shards, *lands)


HBM_SPEC = pl.BlockSpec(memory_space=pltpu.HBM)
SEM_SPEC = pl.BlockSpec(memory_space=pltpu.SEMAPHORE)
SIDE_EFFECT = pltpu.SideEffectType.DATAFLOW_SIDE_EFFECTING


def _hbm(v):
    return pltpu.with_memory_space_constraint(v, pltpu.HBM)


def gather_slices(src, land, me, k):
    return src, land.at[me], land.at[me ^ k]


def exchange_slices(src, land, me, k):
    return src.at[me ^ k], land.at[k - 1], land.at[k - 1]


def split_start(groups, slices, deps, name):
    sizes = [len(srcs) for srcs, _ in groups]
    flat_src = [s for srcs, _ in groups for s in srcs]
    flat_land = [l for _, lands in groups for l in lands]
    na, ng, nd = len(flat_src), len(groups), len(deps)

    def body(*refs):
        src_refs, land_refs = refs[:na], refs[na:2 * na]
        sem_refs = refs[2 * na + nd:2 * na + nd + 2 * ng]
        token_ref = refs[-1]
        x, y, c, me = _mesh_place()
        a0 = 0
        for gi, size in enumerate(sizes):
            send_sems, recv_sems = sem_refs[2 * gi], sem_refs[2 * gi + 1]
            for k in range(1, N_DEV):
                for j in range(size):
                    src, dst, _ = slices(src_refs[a0 + j], land_refs[a0 + j], me, k)
                    pltpu.make_async_remote_copy(
                        src_ref=src, dst_ref=dst, send_sem=send_sems.at[(k - 1) * size + j],
                        recv_sem=recv_sems.at[(k - 1) * size + j],
                        device_id=_peer(x, y, c, k), device_id_type=pl.DeviceIdType.MESH).start()
            a0 += size
        token_ref[...] = jnp.zeros_like(token_ref)

    sems = [pltpu.SemaphoreType.DMA(((N_DEV - 1) * size,)) for size in sizes for _ in range(2)]
    thru = [pltpu.HBM(v.shape, v.dtype) for v in flat_src + flat_land]
    outs = pl.pallas_call(
        body, name=name,
        in_specs=[HBM_SPEC] * (2 * na) + [ANY_SPEC] * nd,
        out_specs=[SEM_SPEC] * (2 * ng) + [HBM_SPEC] * (2 * na) + [pl.BlockSpec(memory_space=pltpu.VMEM)],
        out_shape=sems + thru + [jax.ShapeDtypeStruct((8, LANES), F32)],
        input_output_aliases={i: 2 * ng + i for i in range(2 * na)},
        compiler_params=pltpu.CompilerParams(has_side_effects=SIDE_EFFECT),
    )(*[_hbm(v) for v in flat_src + flat_land], *deps)
    started, a0 = [], 0
    for gi, size in enumerate(sizes):
        srcs = outs[2 * ng + a0:2 * ng + a0 + size]
        lands = outs[2 * ng + na + a0:2 * ng + na + a0 + size]
        started.append((outs[2 * gi], outs[2 * gi + 1], list(srcs), list(lands)))
        a0 += size
    return started, outs[-1]


def split_wait(started, slices, after, name):
    send_sems, recv_sems, srcs, lands = started
    na = len(srcs)

    def body(*refs):
        src_refs, land_refs = refs[:na], refs[na:2 * na]
        send_ref, recv_ref = refs[2 * na], refs[2 * na + 1]
        x, y, c, me = _mesh_place()
        for k in range(1, N_DEV):
            for j in range(na):
                src, _, got = slices(src_refs[j], land_refs[j], me, k)
                cp = pltpu.make_async_remote_copy(
                    src_ref=src, dst_ref=got, send_sem=send_ref.at[(k - 1) * na + j], recv_sem=recv_ref.at[(k - 1) * na + j],
                    device_id=_peer(x, y, c, k), device_id_type=pl.DeviceIdType.MESH)
                cp.wait_send()
                cp.wait_recv()

    outs = pl.pallas_call(
        body, name=name,
        in_specs=[HBM_SPEC] * (2 * na) + [SEM_SPEC, SEM_SPEC] + [ANY_SPEC] * len(after),
        out_specs=[HBM_SPEC] * (2 * na),
        out_shape=[pltpu.HBM(v.shape, v.dtype) for v in srcs + lands],
        input_output_aliases={i: i for i in range(2 * na)},
        compiler_params=pltpu.CompilerParams(has_side_effects=SIDE_EFFECT),
    )(*srcs, *lands, send_sems, recv_sems, *after)
    return list(outs[:na]), list(outs[na:])


def place_own(pack, me, name):
    rows, cols = pack.shape

    def body(me_ref, p_ref, o_ref):
        o_ref[...] = p_ref[...]

    return pl.pallas_call(
        body, name=name,
        grid_spec=pltpu.PrefetchScalarGridSpec(
            num_scalar_prefetch=1, grid=(1,),
            in_specs=[pl.BlockSpec((rows, cols), lambda i, me_ref: (0, 0))],
            out_specs=pl.BlockSpec((None, rows, cols), lambda i, me_ref: (me_ref[0], 0, 0))),
        out_shape=jax.ShapeDtypeStruct((N_DEV, rows, cols), F32),
        compiler_params=_params(("arbitrary",)),
    )(me, pack)


def sum_slots(slots, name):
    nd, rows, cols = slots.shape

    def body(s_ref, o_ref):
        acc = s_ref[0]
        for j in range(1, nd):
            acc = acc + s_ref[j]
        o_ref[...] = acc

    return pl.pallas_call(
        body, name=name, grid=(1,),
        in_specs=[pl.BlockSpec((nd, rows, cols), lambda i: (0, 0, 0))],
        out_specs=pl.BlockSpec((rows, cols), lambda i: (0, 0)),
        out_shape=jax.ShapeDtypeStruct((rows, cols), F32),
        compiler_params=_params(("arbitrary",)),
    )(slots)


def _pack_rows(size):
    rows = -(-size // LANES)
    return -(-rows // 8) * 8


def pack_small(parts):
    out = []
    for p in parts:
        flat = p.reshape(-1).astype(F32)
        rows = _pack_rows(flat.shape[0])
        out.append(jnp.pad(flat, (0, rows * LANES - flat.shape[0])).reshape(rows, LANES))
    return jnp.concatenate(out, axis=0)


def unpack_small(pack, shapes):
    out, off = [], 0
    for shp in shapes:
        size = 1
        for s in shp:
            size *= s
        rows = _pack_rows(size)
        out.append(pack[off:off + rows].reshape(-1)[:size].reshape(shp))
        off += rows
    return out


def local_step(x, target, small, get_w, put_g, put_small):
    col = lambda v: v.reshape(-1, 1)
    (wg1, wu1), deps = get_w(("wg1", "wu1"), ())
    sil1, gp1, s1, h1 = ffn_fwd(x, small["ffn1_norm_g"], wg1, wu1, None, "ffn1_fwd", deps)
    (wd1, win, wout), _ = get_w(("wd1", "win", "wout"), (s1,))
    qkv_t, zg_t, x1 = mix_in_fwd(x, small["mix_norm_g"], win, col(small["b_in"]), "mix_in_fwd", ffn_tail=(s1, wd1))
    ya_t = attn_fwd(qkv_t, small["attn_sinks"], "attn_fwd")
    lng, lnb = col(small["gmlp_ln_g"]), col(small["gmlp_ln_b"])
    w_s, b_s = small["gmlp_w_s"][0], small["gmlp_b_s"][0]
    yg_t = gmlp_fwd(zg_t, lng, lnb, w_s, b_s, "gmlp_fwd")
    gao, ggo = col(small["attn_out_norm_g"]), col(small["gmlp_out_norm_g"])
    x2 = mix_out_fwd(x1, ya_t, yg_t, gao, ggo, wout, small["b_out"], "mix_out_fwd")
    (wg2, wu2, wd2), _ = get_w(("wg2", "wu2", "wd2"), (x2,))
    gs = {}
    dx3, sil2, gp2, s2, h2f, gs["final_norm_g"], loss = ffn_fwd(
        x2, small["ffn2_norm_g"], wg2, wu2, wd2, "ffn2_fwd", head=(target, small["final_norm_g"].reshape(1, -1)))

    dx2, da, db, df, gs["ffn2_norm_g"] = ffn_bwd(x2, dx3, small["ffn2_norm_g"], sil2, gp2, wg2, wu2, wd2, "ffn2_bwd")
    deps = put_g({"wg2": grad_tn(da, h2f, 1.0, "ffn2_dwg"), "wu2": grad_tn(db, h2f, 1.0, "ffn2_dwu"),
                  "wd2": grad_tn(s2, df, 1.0, "ffn2_dwd")})

    dya_t, dyg_t, y_t, dgao, dggo, gs["b_out"] = mix_out_bwd(dx2, ya_t, yg_t, gao, ggo, wout, "mix_out_bwd", deps)
    gs["attn_out_norm_g"], gs["gmlp_out_norm_g"] = dgao.reshape(1, -1), dggo.reshape(1, -1)
    g_wout = grad_nn([y_t], dx2, "dwout")
    dzg_t, dws, dbs, dlng, dlnb = gmlp_bwd(zg_t, dyg_t, lng, lnb, w_s, b_s, "gmlp_bwd")
    gs["gmlp_w_s"], gs["gmlp_b_s"] = dws[None], dbs[None]
    gs["gmlp_ln_g"], gs["gmlp_ln_b"] = dlng.reshape(1, -1), dlnb.reshape(1, -1)
    dq_t, dk_t, dv_t, dsinks = attn_bwd(qkv_t, dya_t, small["attn_sinks"], "attn_bwd")
    gs["attn_sinks"] = dsinks.reshape(1, -1)
    dx1, h2, gs["mix_norm_g"], dbin = mix_in_bwd(x1, dx2, small["mix_norm_g"], dq_t, dk_t, dv_t, dzg_t, win,
                                                 "mix_in_bwd")
    gs["b_in"] = dbin.reshape(1, -1)
    deps = put_g({"wout": g_wout, "win": grad_nn([dq_t, dk_t, dv_t, dzg_t], h2, "dwin")})

    dx0, da, db, df, gs["ffn1_norm_g"] = ffn_bwd(x, dx1, small["ffn1_norm_g"], sil1, gp1, wg1, wu1, wd1, "ffn1_bwd",
                                             deps)
    deps = put_small(gs, loss)
    deps = put_g({"wg1": grad_tn(da, h1, 1.0, "ffn1_dwg", deps)})
    deps = put_g({"wu1": grad_tn(db, h1, 1.0, "ffn1_dwu", deps)})
    put_g({"wd1": grad_tn(s1, df, 1.0, "ffn1_dwd", deps)})
    return dx0


SMALL_NAMES = ["ffn1_norm_g", "mix_norm_g", "b_in", "attn_sinks", "gmlp_ln_g", "gmlp_ln_b", "gmlp_w_s", "gmlp_b_s",
               "attn_out_norm_g", "gmlp_out_norm_g", "b_out", "ffn2_norm_g", "final_norm_g"]
BIG_NAMES = {"wg1": ("ffn1_w_gate", True), "wu1": ("ffn1_w_up", True), "wd1": ("ffn1_w_down", False),
             "win": ("w_in", True), "wout": ("w_out", False),
             "wg2": ("ffn2_w_gate", True), "wu2": ("ffn2_w_up", True), "wd2": ("ffn2_w_down", False)}
WEIGHT_ORDER = ["ffn1_norm_g", "ffn1_w_gate", "ffn1_w_up", "ffn1_w_down", "mix_norm_g", "w_in", "b_in", "attn_sinks",
                "gmlp_ln_g", "gmlp_ln_b", "gmlp_w_s", "gmlp_b_s", "attn_out_norm_g", "gmlp_out_norm_g", "w_out",
                "b_out", "ffn2_norm_g", "ffn2_w_gate", "ffn2_w_up", "ffn2_w_down", "final_norm_g"]


def kernel(x, ffn1_norm_g, ffn1_w_gate, ffn1_w_up, ffn1_w_down, mix_norm_g, w_in, b_in, attn_sinks, gmlp_ln_g, gmlp_ln_b, gmlp_w_s, gmlp_b_s, attn_out_norm_g, gmlp_out_norm_g, w_out, b_out, ffn2_norm_g, ffn2_w_gate, ffn2_w_up, ffn2_w_down, final_norm_g, loss_target, m_ffn1_norm_g, m_ffn1_w_gate, m_ffn1_w_up, m_ffn1_w_down, m_mix_norm_g, m_w_in, m_b_in, m_attn_sinks, m_gmlp_ln_g, m_gmlp_ln_b, m_gmlp_w_s, m_gmlp_b_s, m_attn_out_norm_g, m_gmlp_out_norm_g, m_w_out, m_b_out, m_ffn2_norm_g, m_ffn2_w_gate, m_ffn2_w_up, m_ffn2_w_down, m_final_norm_g, v_ffn1_norm_g, v_ffn1_w_gate, v_ffn1_w_up, v_ffn1_w_down, v_mix_norm_g, v_w_in, v_b_in, v_attn_sinks, v_gmlp_ln_g, v_gmlp_ln_b, v_gmlp_w_s, v_gmlp_b_s, v_attn_out_norm_g, v_gmlp_out_norm_g, v_w_out, v_b_out, v_ffn2_norm_g, v_ffn2_w_gate, v_ffn2_w_up, v_ffn2_w_down, v_final_norm_g):
    args = dict(locals())
    w = {n: args[n] for n in WEIGHT_ORDER}
    m = {n: args["m_" + n] for n in WEIGHT_ORDER}
    v = {n: args["v_" + n] for n in WEIGHT_ORDER}

    d_model = x.shape[-1]
    flat = lambda g: g.reshape(-1, d_model)
    me = _mesh_place()[3].astype(jnp.int32).reshape(1)
    first = ("wg1", "wu1")
    later = (("wd1", "win", "wout"), ("wg2", "wu2", "wd2"))
    gathers, exchanges, last_token = {}, [], []

    order = list(first) + [k for grp in later for k in grp]
    mats = [w[BIG_NAMES[k][0]][0].T if BIG_NAMES[k][1] else w[BIG_NAMES[k][0]][0] for k in order]
    shard_list, land_list = prep_shards(mats, me, "prep_shards")
    shards, zones = dict(zip(order, shard_list)), dict(zip(order, land_list))

    def get_w(keys, after):
        if keys == first:
            full = allgather_rows([shards[k] for k in first], [zones[k] for k in first], "allgather_ffn1")
            started, token = split_start([([shards[k] for k in grp], [zones[k] for k in grp]) for grp in later],
                                         gather_slices, (full[0],), "gather_start")
            gathers.update(zip(later, started))
            return [flat(g) for g in full], (token,)
        _, lands = split_wait(gathers[keys], gather_slices, after, "gather_wait_" + keys[0])
        return [flat(land) for land in lands], ()

    def put_g(gb):
        keys = tuple(gb)
        g3 = [gb[k].reshape(N_DEV, -1, d_model) for k in keys]
        lands = [lax.empty((N_DEV - 1,) + g.shape[1:], BF16) for g in g3]
        started, token = split_start([(g3, lands)], exchange_slices, (), "exchange_start_" + keys[0])
        exchanges.append((keys, started[0]))
        last_token[:] = [token]
        return (token,)

    small_started = []

    def put_small(gs, loss):
        pack = pack_small([gs[n] for n in SMALL_NAMES] + [loss[:, :1]])
        started, token = split_start([([pack], [place_own(pack, me, "place_small")])], gather_slices, (), "small_start")
        small_started[:] = started
        return (token,)

    small = {n: w[n] for n in SMALL_NAMES}
    grad_x = local_step(x[0], loss_target[0], small, get_w, put_g, put_small)

    grads, deltas, new_m, new_v = {}, {}, {}, {}
    shapes = [w[n].shape for n in SMALL_NAMES]
    _, (slots,) = split_wait(small_started[0], gather_slices, tuple(last_token), "small_wait")
    total = sum_slots(slots, "sum_small")
    small_g = unpack_small(total, shapes + [(1, 1)])
    loss_total = small_g[-1].reshape(())
    gpack = total[:sum(_pack_rows(w[n].size) for n in SMALL_NAMES)]
    d_, m_, v_ = adamw(pack_small([w[n] for n in SMALL_NAMES]), gpack, pack_small([m[n] for n in SMALL_NAMES]),
                       pack_small([v[n] for n in SMALL_NAMES]), "adamw_small")
    for n, g_, dd, mm, vv in zip(SMALL_NAMES, small_g[:-1], unpack_small(d_, shapes), unpack_small(m_, shapes),
                                 unpack_small(v_, shapes)):
        grads[n], deltas[n], new_m[n], new_v[n] = g_, dd, mm, vv

    after = tuple(arr for n in SMALL_NAMES for arr in (grads[n], deltas[n], new_m[n], new_v[n]))
    for keys, started in exchanges:
        g3, lands = split_wait(started, exchange_slices, after, "exchange_wait_" + keys[0])
        for key, own, land in zip(keys, g3, lands):
            pname, transposed = BIG_NAMES[key]
            to_rows = (lambda p: p[0].T) if transposed else (lambda p: p[0])
            from_rows = (lambda r: r.T[None]) if transposed else (lambda r: r[None])
            g, d_, m_, v_ = sum_adamw(land, own, me, to_rows(w[pname]), to_rows(m[pname]), to_rows(v[pname]),
                                      "adamw_" + key)
            after = after + (g,)
            grads[pname], deltas[pname], new_m[pname], new_v[pname] = (from_rows(g), from_rows(d_), from_rows(m_),
                                                                        from_rows(v_))

    return (loss_total, grad_x[None], *[grads[n] for n in WEIGHT_ORDER], *[deltas[n] for n in WEIGHT_ORDER],
            *[new_m[n] for n in WEIGHT_ORDER], *[new_v[n] for n in WEIGHT_ORDER])
```

```python
import functools

import jax
import jax.numpy as jnp
from jax import lax
from jax.experimental import pallas as pl
from jax.experimental.pallas import tpu as pltpu

F32 = jnp.float32
BF16 = jnp.bfloat16

N_DEV = 8
N_Q_HEADS = 8
N_KV_HEADS = 2
HEAD_DIM = 64
ATTN_WIDTH = N_Q_HEADS * HEAD_DIM
KV_WIDTH = N_KV_HEADS * HEAD_DIM
QKV_WIDTH = ATTN_WIDTH + 2 * KV_WIDTH
BLOCK = 128
GMLP_GROUPS = 8
GMLP_GROUP_DIM = 64
GMLP_WIDTH = GMLP_GROUPS * GMLP_GROUP_DIM
EPS = 1e-6
FFN_RES = 0.5
ATTN_SCALE = HEAD_DIM ** -0.5
MASK_VALUE = -1e30

ADAM_LR = 0.001
ADAM_B1 = 0.9
ADAM_B2 = 0.999
ADAM_EPS = 1e-08
ADAM_WD = 0.01
ADAM_STEP = 10

V7X_VMEM_BYTES = 64 * 1024 * 1024
VMEM_LIMIT = 62 * 1024 * 1024
LANES = 128
MXU_WIDTH = 256
FFN_BWD_TOKENS = 256

NT = (((1,), (1,)), ((), ()))
TN = (((0,), (0,)), ((), ()))


def _dot(a, b):
    return jnp.dot(a, b, preferred_element_type=F32)


def _dg(a, b, dims):
    return lax.dot_general(a, b, dims, preferred_element_type=F32)


def _params(semantics=None):
    return pltpu.CompilerParams(dimension_semantics=semantics, vmem_limit_bytes=VMEM_LIMIT)


def _resident(shape):
    zeros = (0,) * len(shape)
    return pl.BlockSpec(shape, lambda *_: zeros, pipeline_mode=pl.Buffered(1))


def _drop_deps(body, n_in, n_deps):
    return lambda *refs: body(*refs[:n_in], *refs[n_in + n_deps:])


ANY_SPEC = pl.BlockSpec(memory_space=pl.ANY)


TOKEN_TILE = 512
GRAD_TOKENS_PER_STEP = 2048
GRAD_ROW_TILE_CAP = 1408
N_GATHER_ROLES = 13


def _token_tile(t):
    return min(t, TOKEN_TILE)


def _f_chunk(f):
    for c in (256, 128):
        if f % c == 0:
            return c
    return f


def _loss_head(xv, target, gain):
    r = lax.rsqrt(jnp.mean(xv * xv, axis=-1, keepdims=True) + EPS)
    err = xv * r * gain - target
    loss = 0.5 * jnp.sum(jnp.mean(err * err, axis=-1, keepdims=True), axis=0, keepdims=True)
    dx, dgain = _rmsnorm_bwd_rows(xv, r, gain, err * (1.0 / xv.shape[-1]))
    return dx, dgain, loss


def ffn_fwd(x, gain, wg_t, wu_t, wd, name, deps=(), head=None):
    t, d = x.shape
    f = wg_t.shape[0]
    tm = _token_tile(t)
    fc = _f_chunk(f)
    weights = [wg_t, wu_t] + ([] if wd is None else [wd])
    n_in = 2 + len(weights) + (0 if head is None else 2)
    n_x = 0 if wd is None else 1

    def body(x_ref, g_ref, wg_ref, wu_ref, *rest):
        sil_ref, gp_ref, s_ref, h_ref = rest[n_in - 4 + n_x:n_in + n_x]
        xv = x_ref[...]
        r = lax.rsqrt(jnp.mean(xv * xv, axis=-1, keepdims=True) + EPS)
        h = (xv * r * g_ref[...]).astype(BF16)
        h_ref[...] = h
        for c in range(f // fc):
            sl = slice(c * fc, (c + 1) * fc)
            a = _dg(h, wg_ref[sl, :], NT)
            b = _dg(h, wu_ref[sl, :], NT)
            sig = jax.nn.sigmoid(a)
            sil = a * sig
            sil_ref[:, sl] = sil.astype(BF16)
            gp_ref[:, sl] = (b * (sig + sil - sil * sig)).astype(BF16)
            s_ref[:, sl] = (sil * b).astype(BF16)
        if wd is None:
            return
        wd_ref, xo_ref = rest[0], rest[n_in - 4]
        xo = xv + FFN_RES * _dot(s_ref[...], wd_ref[...])
        if head is None:
            xo_ref[...] = xo
        else:
            t_ref, hg_ref = rest[1:3]
            dg_ref, loss_ref = rest[-2:]

            @pl.when(pl.program_id(0) == 0)
            def _():
                dg_ref[...] = jnp.zeros_like(dg_ref)
                loss_ref[...] = jnp.zeros_like(loss_ref)

            dx, dgain, loss = _loss_head(xo, t_ref[...], hg_ref[...])
            xo_ref[...] = dx
            dg_ref[...] += dgain
            loss_ref[...] += loss

    tok = lambda: pl.BlockSpec((tm, d), lambda i: (i, 0))
    wide = lambda: pl.BlockSpec((tm, f), lambda i: (i, 0))
    const2 = lambda i: (0, 0)
    in_specs = [tok(), _resident((1, d))] + [_resident((f, d)) for _ in weights]
    out_specs = [tok()] * n_x + [wide(), wide(), wide(), tok()]
    out_shape = ([jax.ShapeDtypeStruct((t, d), F32)] * n_x + [jax.ShapeDtypeStruct((t, f), BF16)] * 3
                 + [jax.ShapeDtypeStruct((t, d), BF16)])
    operands = [x, gain] + weights
    if head is not None:
        in_specs += [tok(), _resident((1, d))]
        out_specs += [pl.BlockSpec((1, d), const2), pl.BlockSpec((1, LANES), const2)]
        out_shape += [jax.ShapeDtypeStruct((1, d), F32), jax.ShapeDtypeStruct((1, LANES), F32)]
        operands += list(head)
    return pl.pallas_call(
        _drop_deps(body, n_in, len(deps)), name=name, grid=(t // tm,),
        in_specs=in_specs + [ANY_SPEC] * len(deps), out_specs=out_specs, out_shape=out_shape,
        compiler_params=_params(("arbitrary",)),
    )(*operands, *deps)


def _rmsnorm_bwd_rows(xv, r, gain, dh):
    n = xv * r
    dgain = jnp.sum(dh * n, axis=0, keepdims=True)
    dn = dh * gain
    dx = r * (dn - n * jnp.mean(dn * n, axis=-1, keepdims=True))
    return dx, dgain


def ffn_bwd(x, dxo, gain, sil, gp, wg_t, wu_t, wd, name, deps=()):
    t, d = x.shape
    f = wd.shape[0]
    tm = min(t, FFN_BWD_TOKENS)
    nt = t // tm
    chunks = [(off, min(MXU_WIDTH, f - off)) for off in range(0, f, MXU_WIDTH)]

    def body(xp_ref, dxp_ref, dxo_ref, g_ref, sil_ref, gp_ref, wg_ref, wu_ref, wd_ref,
             dx_ref, da_ref, db_ref, df_ref, dg_ref, dh_ref):
        i = pl.program_id(0)
        gain_v = g_ref[...]

        @pl.when(i == 0)
        def _():
            dh_ref[...] = jnp.zeros_like(dh_ref)
            dg_ref[...] = jnp.zeros_like(dg_ref)

        def finish_previous(slot):
            xv = xp_ref[...]
            r = lax.rsqrt(jnp.mean(xv * xv, axis=-1, keepdims=True) + EPS)
            dx, dgain = _rmsnorm_bwd_rows(xv, r, gain_v, dh_ref[slot])
            dx_ref[...] = dxp_ref[...] + dx
            dg_ref[...] += dgain

        @pl.when(i < nt)
        def _():
            df = (FFN_RES * dxo_ref[...]).astype(BF16)
            df_ref[...] = df
            for off, size in chunks:
                sl = slice(off, off + size)
                ds = _dg(df, wd_ref[sl, :], NT)
                da_ref[:, sl] = (ds * gp_ref[:, sl].astype(F32)).astype(BF16)
                db_ref[:, sl] = (ds * sil_ref[:, sl].astype(F32)).astype(BF16)
            dh = _dot(da_ref[...], wg_ref[...]) + _dot(db_ref[...], wu_ref[...])
            finish_previous((i + 1) % 2)
            dh_ref[i % 2] = dh

        @pl.when(i == nt)
        def _():
            finish_previous((nt - 1) % 2)

    prev = lambda i: (jnp.maximum(i - 1, 0), 0)
    cur = lambda i: (jnp.minimum(i, nt - 1), 0)
    return pl.pallas_call(
        _drop_deps(body, 9, len(deps)), name=name, grid=(nt + 1,),
        in_specs=[pl.BlockSpec((tm, d), prev), pl.BlockSpec((tm, d), prev), pl.BlockSpec((tm, d), cur),
                  _resident((1, d)), pl.BlockSpec((tm, f), cur), pl.BlockSpec((tm, f), cur),
                  _resident((f, d)), _resident((f, d)), _resident((f, d))] + [ANY_SPEC] * len(deps),
        out_specs=[pl.BlockSpec((tm, d), prev), pl.BlockSpec((tm, f), cur), pl.BlockSpec((tm, f), cur),
                   pl.BlockSpec((tm, d), cur), pl.BlockSpec((1, d), lambda i: (0, 0))],
        out_shape=[jax.ShapeDtypeStruct((t, d), F32), jax.ShapeDtypeStruct((t, f), BF16),
                   jax.ShapeDtypeStruct((t, f), BF16), jax.ShapeDtypeStruct((t, d), BF16),
                   jax.ShapeDtypeStruct((1, d), F32)],
        scratch_shapes=[pltpu.VMEM((2, tm, d), F32)],
        compiler_params=_params(("arbitrary",)),
    )(x, dxo, dxo, gain, sil, gp, wg_t, wu_t, wd, *deps)


def _row_tile(m, cap):
    if m <= cap:
        return m
    best = None
    for bm in range(LANES, cap + 1, LANES):
        if m % bm == 0:
            best = bm
    return best if best is not None else m


def grad_tn(a, b, scale, name, deps=()):
    t, m = a.shape
    n = b.shape[1]
    bm = _row_tile(m, GRAD_ROW_TILE_CAP)
    bk = min(t, GRAD_TOKENS_PER_STEP)
    nk = t // bk

    def body(a_ref, b_ref, o_ref, acc_ref):
        k = pl.program_id(1)

        @pl.when(k == 0)
        def _():
            acc_ref[...] = jnp.zeros_like(acc_ref)

        acc_ref[...] += _dg(a_ref[...].astype(BF16), b_ref[...].astype(BF16), TN)

        @pl.when(k == nk - 1)
        def _():
            o_ref[...] = (scale * acc_ref[...]).astype(BF16)

    return pl.pallas_call(
        _drop_deps(body, 2, len(deps)), name=name, grid=(m // bm, nk),
        in_specs=[pl.BlockSpec((bk, bm), lambda i, k: (k, i)), pl.BlockSpec((bk, n), lambda i, k: (k, 0))]
        + [ANY_SPEC] * len(deps),
        out_specs=pl.BlockSpec((bm, n), lambda i, k: (i, 0)),
        out_shape=jax.ShapeDtypeStruct((m, n), BF16),
        scratch_shapes=[pltpu.VMEM((bm, n), F32)],
        compiler_params=_params(("arbitrary", "arbitrary")),
    )(a, b, *deps)


def grad_nn(a_list, b, name):
    t, n = b.shape
    ms = [a.shape[0] for a in a_list]
    m = sum(ms)
    bk = min(t, GRAD_TOKENS_PER_STEP)
    nk = t // bk
    na = len(a_list)

    def body(*refs):
        a_refs, b_ref, o_ref, acc_ref = refs[:na], refs[na], refs[na + 1], refs[na + 2]
        k = pl.program_id(0)

        @pl.when(k == 0)
        def _():
            acc_ref[...] = jnp.zeros_like(acc_ref)

        bv = b_ref[...].astype(BF16)
        off = 0
        for a_ref, mi in zip(a_refs, ms):
            acc_ref[off:off + mi, :] += _dot(a_ref[...].astype(BF16), bv)
            off += mi

        @pl.when(k == nk - 1)
        def _():
            o_ref[...] = acc_ref[...].astype(BF16)

    return pl.pallas_call(
        body, name=name, grid=(nk,),
        in_specs=[pl.BlockSpec((mi, bk), lambda k: (0, k)) for mi in ms] + [pl.BlockSpec((bk, n), lambda k: (k, 0))],
        out_specs=pl.BlockSpec((m, n), lambda k: (0, 0)),
        out_shape=jax.ShapeDtypeStruct((m, n), BF16),
        scratch_shapes=[pltpu.VMEM((m, n), F32)],
        compiler_params=_params(("arbitrary",)),
    )(*a_list, b)


def mix_in_fwd(x, gain, win_t, b_in_col, name, ffn_tail=None):
    t, d = x.shape
    w = win_t.shape[0]
    tm = _token_tile(t)

    def body(x_ref, g_ref, w_ref, bias_ref, *rest):
        xv = x_ref[...]
        if ffn_tail is not None:
            s_ref, wd_ref, xo_ref = rest[0], rest[1], rest[4]
            xv = xv + FFN_RES * _dot(s_ref[...], wd_ref[...])
            xo_ref[...] = xv
        qkv_ref, zg_ref = rest[-3:-1] if ffn_tail is not None else rest
        r = lax.rsqrt(jnp.mean(xv * xv, axis=-1, keepdims=True) + EPS)
        h = (xv * r * g_ref[...]).astype(BF16)
        qkv_ref[...] = (_dg(w_ref[:QKV_WIDTH, :], h, NT) + bias_ref[:QKV_WIDTH, :]).astype(BF16)
        zg_ref[...] = (_dg(w_ref[QKV_WIDTH:, :], h, NT) + bias_ref[QKV_WIDTH:, :]).astype(BF16)

    tok = lambda: pl.BlockSpec((tm, d), lambda i: (i, 0))
    in_specs = [tok(), _resident((1, d)), _resident((w, d)), _resident((w, 1))]
    out_specs = [pl.BlockSpec((QKV_WIDTH, tm), lambda i: (0, i)), pl.BlockSpec((w - QKV_WIDTH, tm), lambda i: (0, i))]
    out_shape = [jax.ShapeDtypeStruct((QKV_WIDTH, t), BF16), jax.ShapeDtypeStruct((w - QKV_WIDTH, t), BF16)]
    operands = [x, gain, win_t, b_in_col]
    if ffn_tail is not None:
        f = ffn_tail[1].shape[0]
        in_specs += [pl.BlockSpec((tm, f), lambda i: (i, 0)), _resident((f, d))]
        out_specs.append(tok())
        out_shape.append(jax.ShapeDtypeStruct((t, d), F32))
        operands += list(ffn_tail)
    return pl.pallas_call(
        body, name=name, grid=(t // tm,), in_specs=in_specs, out_specs=out_specs, out_shape=out_shape,
        compiler_params=_params(("arbitrary",)),
    )(*operands)


ATTN_REP = N_Q_HEADS // N_KV_HEADS
ATTN_BLOCKS_PER_STEP = 8


def _attn_tiles(t):
    nbs = min(ATTN_BLOCKS_PER_STEP, t // BLOCK)
    return nbs, nbs * BLOCK, t // (nbs * BLOCK)


def _attn_specs(nbs, tiles):
    last = tiles - 1
    cur = lambda n: jnp.minimum(n, last)
    prev = lambda n: jnp.maximum(jnp.minimum(n, last) * nbs - 1, 0)
    k_row = ATTN_WIDTH // KV_WIDTH
    tw = nbs * BLOCK
    return [
        pl.BlockSpec((ATTN_WIDTH, tw), lambda n: (0, cur(n))),
        pl.BlockSpec((KV_WIDTH, BLOCK), lambda n: (k_row, prev(n))),
        pl.BlockSpec((KV_WIDTH, tw), lambda n: (k_row, cur(n))),
        pl.BlockSpec((KV_WIDTH, BLOCK), lambda n: (k_row + 1, prev(n))),
        pl.BlockSpec((KV_WIDTH, tw), lambda n: (k_row + 1, cur(n))),
    ]


def _cols(b):
    return slice(b * BLOCK, (b + 1) * BLOCK)


def _band_rows(prev_ref, tile_ref, gs, b):
    before = prev_ref[gs, :] if b == 0 else tile_ref[gs, _cols(b - 1)]
    return before, tile_ref[gs, _cols(b)]


def _group_rows(ref, g, b):
    first = g * ATTN_REP
    return jnp.concatenate([ref[(first + r) * HEAD_DIM:(first + r + 1) * HEAD_DIM, _cols(b)] for r in range(ATTN_REP)],
                           axis=1)


def _ungroup_rows(ref, g, b, val):
    first = g * ATTN_REP
    for r in range(ATTN_REP):
        ref[(first + r) * HEAD_DIM:(first + r + 1) * HEAD_DIM, _cols(b)] = val[:, r * BLOCK:(r + 1) * BLOCK]


def _attn_upper():
    key = lax.broadcasted_iota(jnp.int32, (BLOCK, ATTN_REP * BLOCK), 0)
    qry = lax.broadcasted_iota(jnp.int32, (BLOCK, ATTN_REP * BLOCK), 1) & (BLOCK - 1)
    return key > qry


def _attn_probs(q, kp, kc, sink_ref, g, upper, no_prev):
    s = jnp.where(upper, _dg(kp, q, TN), _dg(kc, q, TN)) * ATTN_SCALE
    if no_prev is not None:
        s = jnp.where(upper & no_prev, MASK_VALUE, s)
    sink = jnp.concatenate([jnp.full((1, BLOCK), sink_ref[0, g * ATTN_REP + r], F32) for r in range(ATTN_REP)], axis=1)
    m = jnp.maximum(jnp.max(s, axis=0, keepdims=True), sink)
    e = jnp.exp(s - m)
    es = jnp.exp(sink - m)
    inv = 1.0 / (jnp.sum(e, axis=0, keepdims=True) + es)
    return e * inv, es * inv


def _split_band(upper, val):
    return jnp.where(upper, val, 0.0).astype(BF16), jnp.where(upper, 0.0, val).astype(BF16)


def attn_fwd(qkv_t, sinks, name):
    t = qkv_t.shape[1]
    nbs, tw, tiles = _attn_tiles(t)

    def body(sink_ref, q_ref, kp_ref, kc_ref, vp_ref, vc_ref, y_ref):
        n = pl.program_id(0)
        upper = _attn_upper()
        for b in range(nbs):
            for g in range(N_KV_HEADS):
                gs = slice(g * HEAD_DIM, (g + 1) * HEAD_DIM)
                kp, kc = _band_rows(kp_ref, kc_ref, gs, b)
                vp, vc = _band_rows(vp_ref, vc_ref, gs, b)
                p, _ = _attn_probs(_group_rows(q_ref, g, b), kp, kc, sink_ref, g, upper, n == 0 if b == 0 else None)
                pp, pc = _split_band(upper, p)
                _ungroup_rows(y_ref, g, b, (_dot(vp, pp) + _dot(vc, pc)).astype(BF16))

    return pl.pallas_call(
        body, name=name, grid=(tiles,),
        in_specs=[pl.BlockSpec(memory_space=pltpu.SMEM)] + _attn_specs(nbs, tiles),
        out_specs=pl.BlockSpec((ATTN_WIDTH, tw), lambda n: (0, n)),
        out_shape=jax.ShapeDtypeStruct((ATTN_WIDTH, t), BF16),
        compiler_params=_params(("arbitrary",)),
    )(sinks, qkv_t, qkv_t, qkv_t, qkv_t, qkv_t)


def attn_bwd(qkv_t, dy_t, sinks, name):
    t = qkv_t.shape[1]
    nbs, tw, tiles = _attn_tiles(t)
    kept = slice(0, tw - BLOCK)

    def body(sink_ref, q_ref, kp_ref, kc_ref, vp_ref, vc_ref, do_ref, dq_ref, dk_ref, dv_ref, dsink_ref,
             ck_ref, cv_ref, sacc_ref):
        n = pl.program_id(0)

        @pl.when(n == 0)
        def _():
            sacc_ref[...] = jnp.zeros_like(sacc_ref)

        @pl.when((n > 0) & (n < tiles))
        def _():
            if nbs > 1:
                dk_ref[:, kept] = ck_ref[:, kept].astype(BF16)
                dv_ref[:, kept] = cv_ref[:, kept].astype(BF16)

        @pl.when(n < tiles)
        def _():
            upper = _attn_upper()
            for b in range(nbs):
                for g in range(N_KV_HEADS):
                    gs = slice(g * HEAD_DIM, (g + 1) * HEAD_DIM)
                    kp, kc = _band_rows(kp_ref, kc_ref, gs, b)
                    vp, vc = _band_rows(vp_ref, vc_ref, gs, b)
                    q = _group_rows(q_ref, g, b)
                    do = _group_rows(do_ref, g, b)
                    p, ps = _attn_probs(q, kp, kc, sink_ref, g, upper, n == 0 if b == 0 else None)
                    dp = jnp.where(upper, _dg(vp, do, TN), _dg(vc, do, TN))
                    delta = jnp.sum(p * dp, axis=0, keepdims=True)
                    dsink = -(ps * delta)
                    for r in range(ATTN_REP):
                        hd = g * ATTN_REP + r
                        sacc_ref[hd:hd + 1, :] += dsink[:, r * BLOCK:(r + 1) * BLOCK]
                    dsp, dsc = _split_band(upper, p * (dp - delta))
                    pp, pc = _split_band(upper, p)
                    _ungroup_rows(dq_ref, g, b, (ATTN_SCALE * (_dot(kp, dsp) + _dot(kc, dsc))).astype(BF16))
                    dk_before = ATTN_SCALE * _dg(q, dsp, NT)
                    dv_before = _dg(do, pp, NT)
                    if b == 0:
                        @pl.when(n > 0)
                        def _():
                            dk_ref[gs, _cols(nbs - 1)] = (ck_ref[gs, _cols(nbs - 1)] + dk_before).astype(BF16)
                            dv_ref[gs, _cols(nbs - 1)] = (cv_ref[gs, _cols(nbs - 1)] + dv_before).astype(BF16)
                    else:
                        ck_ref[gs, _cols(b - 1)] += dk_before
                        cv_ref[gs, _cols(b - 1)] += dv_before
                    ck_ref[gs, _cols(b)] = ATTN_SCALE * _dg(q, dsc, NT)
                    cv_ref[gs, _cols(b)] = _dg(do, pc, NT)

        @pl.when(n == tiles)
        def _():
            dk_ref[...] = ck_ref[...].astype(BF16)
            dv_ref[...] = cv_ref[...].astype(BF16)
            dsink_ref[...] = jnp.sum(sacc_ref[...], axis=1, keepdims=True)

    last = tiles - 1
    done = lambda n: jnp.maximum(n - 1, 0)
    outs = pl.pallas_call(
        body, name=name, grid=(tiles + 1,),
        in_specs=[pl.BlockSpec(memory_space=pltpu.SMEM)] + _attn_specs(nbs, tiles)
        + [pl.BlockSpec((ATTN_WIDTH, tw), lambda n: (0, jnp.minimum(n, last)))],
        out_specs=[pl.BlockSpec((ATTN_WIDTH, tw), lambda n: (0, jnp.minimum(n, last))),
                   pl.BlockSpec((KV_WIDTH, tw), lambda n: (0, done(n))),
                   pl.BlockSpec((KV_WIDTH, tw), lambda n: (0, done(n))),
                   pl.BlockSpec((N_Q_HEADS, 1), lambda n: (0, 0))],
        out_shape=[jax.ShapeDtypeStruct((ATTN_WIDTH, t), BF16), jax.ShapeDtypeStruct((KV_WIDTH, t), BF16),
                   jax.ShapeDtypeStruct((KV_WIDTH, t), BF16), jax.ShapeDtypeStruct((N_Q_HEADS, 1), F32)],
        scratch_shapes=[pltpu.VMEM((KV_WIDTH, tw), F32), pltpu.VMEM((KV_WIDTH, tw), F32),
                        pltpu.VMEM((N_Q_HEADS, BLOCK), F32)],
        compiler_params=_params(("arbitrary",)),
    )(sinks, qkv_t, qkv_t, qkv_t, qkv_t, qkv_t, dy_t)
    return outs


GELU_C = 0.7978845608028654
GELU_A = 0.044715


def _gelu(x):
    return 0.5 * x * (1.0 + jnp.tanh(GELU_C * (x + GELU_A * x * x * x)))


def _gelu_grad(x):
    th = jnp.tanh(GELU_C * (x + GELU_A * x * x * x))
    return 0.5 * (1.0 + th) + 0.5 * x * (1.0 - th * th) * GELU_C * (1.0 + 3.0 * GELU_A * x * x)


def _gmlp_parts(zg, lng, lnb):
    z = _gelu(zg)
    u = z[:GMLP_WIDTH, :]
    vv = z[GMLP_WIDTH:, :]
    mu = jnp.mean(vv, axis=0, keepdims=True)
    xc = vv - mu
    rstd = lax.rsqrt(jnp.mean(xc * xc, axis=0, keepdims=True) + EPS)
    xhat = xc * rstd
    return u, xhat, rstd, xhat * lng + lnb


def _causal(w):
    row = lax.broadcasted_iota(jnp.int32, (BLOCK, BLOCK), 0)
    col = lax.broadcasted_iota(jnp.int32, (BLOCK, BLOCK), 1)
    return jnp.where(col <= row, w, 0.0)


GMLP_CHUNKS_PER_STEP = 4


def _stack_chunks(v, nc):
    return jnp.concatenate([v[:, c * BLOCK:(c + 1) * BLOCK] for c in range(nc)], axis=0)


def _unstack_chunks(v, nc):
    rows = v.shape[0] // nc
    return jnp.concatenate([v[c * rows:(c + 1) * rows, :] for c in range(nc)], axis=1)


def gmlp_fwd(zg_t, lng, lnb, w_s, b_s, name):
    t = zg_t.shape[1]
    nc = min(GMLP_CHUNKS_PER_STEP, t // BLOCK)
    tw = nc * BLOCK

    def body(zg_ref, lng_ref, lnb_ref, w_ref, bs_ref, y_ref):
        u, _, _, vn = _gmlp_parts(zg_ref[...].astype(F32), lng_ref[...], lnb_ref[...])
        for g in range(GMLP_GROUPS):
            gs = slice(g * GMLP_GROUP_DIM, (g + 1) * GMLP_GROUP_DIM)
            wc = _causal(w_ref[g]).astype(BF16)
            mixed = _dg(_stack_chunks(vn[gs, :].astype(BF16), nc), wc, NT) + bs_ref[g:g + 1, :]
            y_ref[gs, :] = (u[gs, :] * _unstack_chunks(mixed, nc)).astype(BF16)

    return pl.pallas_call(
        body, name=name, grid=(t // tw,),
        in_specs=[pl.BlockSpec((2 * GMLP_WIDTH, tw), lambda n: (0, n)), _resident((GMLP_WIDTH, 1)),
                  _resident((GMLP_WIDTH, 1)), _resident((GMLP_GROUPS, BLOCK, BLOCK)), _resident((GMLP_GROUPS, BLOCK))],
        out_specs=pl.BlockSpec((GMLP_WIDTH, tw), lambda n: (0, n)),
        out_shape=jax.ShapeDtypeStruct((GMLP_WIDTH, t), BF16),
        compiler_params=_params(("arbitrary",)),
    )(zg_t, lng, lnb, w_s, b_s)


def gmlp_bwd(zg_t, dy_t, lng, lnb, w_s, b_s, name):
    t = zg_t.shape[1]
    nc = min(GMLP_CHUNKS_PER_STEP, t // BLOCK)
    tw = nc * BLOCK
    nt = t // tw

    def body(zg_ref, dy_ref, lng_ref, lnb_ref, w_ref, bs_ref, dzg_ref, dw_ref, dbs_ref, dlng_ref, dlnb_ref,
             gacc_ref, bacc_ref):
        n = pl.program_id(0)

        @pl.when(n == 0)
        def _():
            dw_ref[...] = jnp.zeros_like(dw_ref)
            dbs_ref[...] = jnp.zeros_like(dbs_ref)
            gacc_ref[...] = jnp.zeros_like(gacc_ref)
            bacc_ref[...] = jnp.zeros_like(bacc_ref)

        zg = zg_ref[...].astype(F32)
        lng_v = lng_ref[...]
        u, xhat, rstd, vn = _gmlp_parts(zg, lng_v, lnb_ref[...])
        dy = dy_ref[...].astype(F32)
        dvn_parts = []
        for g in range(GMLP_GROUPS):
            gs = slice(g * GMLP_GROUP_DIM, (g + 1) * GMLP_GROUP_DIM)
            wc = _causal(w_ref[g]).astype(BF16)
            vn_s = _stack_chunks(vn[gs, :].astype(BF16), nc)
            mixed = _unstack_chunks(_dg(vn_s, wc, NT) + bs_ref[g:g + 1, :], nc)
            dy_g = dy[gs, :]
            dmixed = dy_g * u[gs, :]
            dm_s = _stack_chunks(dmixed.astype(BF16), nc)
            dzg_ref[gs, :] = (dy_g * mixed * _gelu_grad(zg[gs, :])).astype(BF16)
            dw_ref[g] += _causal(_dg(dm_s, vn_s, TN))
            dbs_ref[g:g + 1, :] += _lane_fold(jnp.sum(dmixed, axis=0, keepdims=True))
            dvn_parts.append(_unstack_chunks(_dot(dm_s, wc), nc))
        dvn = jnp.concatenate(dvn_parts, axis=0)
        gacc_ref[...] += _lane_fold(dvn * xhat)
        bacc_ref[...] += _lane_fold(dvn)
        dxhat = dvn * lng_v
        m1 = jnp.mean(dxhat, axis=0, keepdims=True)
        m2 = jnp.mean(dxhat * xhat, axis=0, keepdims=True)
        dvv = rstd * (dxhat - m1 - xhat * m2)
        dzg_ref[GMLP_WIDTH:, :] = (dvv * _gelu_grad(zg[GMLP_WIDTH:, :])).astype(BF16)

        @pl.when(n == nt - 1)
        def _():
            dlng_ref[...] = jnp.sum(gacc_ref[...], axis=1, keepdims=True)
            dlnb_ref[...] = jnp.sum(bacc_ref[...], axis=1, keepdims=True)

    const2 = lambda n: (0, 0)
    return pl.pallas_call(
        body, name=name, grid=(nt,),
        in_specs=[pl.BlockSpec((2 * GMLP_WIDTH, tw), lambda n: (0, n)), pl.BlockSpec((GMLP_WIDTH, tw), lambda n: (0, n)),
                  _resident((GMLP_WIDTH, 1)), _resident((GMLP_WIDTH, 1)),
                  _resident((GMLP_GROUPS, BLOCK, BLOCK)), _resident((GMLP_GROUPS, BLOCK))],
        out_specs=[pl.BlockSpec((2 * GMLP_WIDTH, tw), lambda n: (0, n)),
                   pl.BlockSpec((GMLP_GROUPS, BLOCK, BLOCK), lambda n: (0, 0, 0)),
                   pl.BlockSpec((GMLP_GROUPS, BLOCK), const2),
                   pl.BlockSpec((GMLP_WIDTH, 1), const2), pl.BlockSpec((GMLP_WIDTH, 1), const2)],
        out_shape=[jax.ShapeDtypeStruct((2 * GMLP_WIDTH, t), BF16),
                   jax.ShapeDtypeStruct((GMLP_GROUPS, BLOCK, BLOCK), F32),
                   jax.ShapeDtypeStruct((GMLP_GROUPS, BLOCK), F32),
                   jax.ShapeDtypeStruct((GMLP_WIDTH, 1), F32), jax.ShapeDtypeStruct((GMLP_WIDTH, 1), F32)],
        scratch_shapes=[pltpu.VMEM((GMLP_WIDTH, BLOCK), F32), pltpu.VMEM((GMLP_WIDTH, BLOCK), F32)],
        compiler_params=_params(("arbitrary",)),
    )(zg_t, dy_t, lng, lnb, w_s, b_s)


def _rmsnorm_cols(y, gain_col):
    r = lax.rsqrt(jnp.mean(y * y, axis=0, keepdims=True) + EPS)
    n = y * r
    return n, r, n * gain_col


def mix_out_fwd(x, ya_t, yg_t, gao, ggo, wout, b_out, name):
    t, d = x.shape
    tm = _token_tile(t)

    def body(x_ref, ya_ref, yg_ref, gao_ref, ggo_ref, w_ref, b_ref, o_ref):
        _, _, ya = _rmsnorm_cols(ya_ref[...].astype(F32), gao_ref[...])
        _, _, yg = _rmsnorm_cols(yg_ref[...].astype(F32), ggo_ref[...])
        o_ref[...] = (x_ref[...] + _dg(ya.astype(BF16), w_ref[:ATTN_WIDTH, :], TN)
                      + _dg(yg.astype(BF16), w_ref[ATTN_WIDTH:, :], TN) + b_ref[...])

    return pl.pallas_call(
        body, name=name, grid=(t // tm,),
        in_specs=[pl.BlockSpec((tm, d), lambda i: (i, 0)), pl.BlockSpec((ATTN_WIDTH, tm), lambda i: (0, i)),
                  pl.BlockSpec((GMLP_WIDTH, tm), lambda i: (0, i)), _resident((ATTN_WIDTH, 1)), _resident((GMLP_WIDTH, 1)),
                  _resident((ATTN_WIDTH + GMLP_WIDTH, d)), _resident((1, d))],
        out_specs=pl.BlockSpec((tm, d), lambda i: (i, 0)),
        out_shape=jax.ShapeDtypeStruct((t, d), F32),
        compiler_params=_params(("arbitrary",)),
    )(x, ya_t, yg_t, gao, ggo, wout, b_out)


def _lane_fold(v):
    out = v[:, :LANES]
    for j in range(1, v.shape[1] // LANES):
        out = out + v[:, j * LANES:(j + 1) * LANES]
    return out


def mix_out_bwd(dx, ya_t, yg_t, gao, ggo, wout, name, deps=()):
    t, d = dx.shape
    tm = _token_tile(t)
    nt = t // tm

    def body(dx_ref, ya_ref, yg_ref, gao_ref, ggo_ref, w_ref, dya_ref, dyg_ref, y_ref, dgao_ref, dggo_ref, db_ref,
             acc_ref):
        i = pl.program_id(0)

        @pl.when(i == 0)
        def _():
            acc_ref[...] = jnp.zeros_like(acc_ref)
            db_ref[...] = jnp.zeros_like(db_ref)

        dxv = dx_ref[...]
        db_ref[...] += jnp.sum(dxv, axis=0, keepdims=True)
        dxb = dxv.astype(BF16)
        for part, (src_ref, gain_ref, dst_ref) in enumerate(((ya_ref, gao_ref, dya_ref), (yg_ref, ggo_ref, dyg_ref))):
            rows = slice(part * ATTN_WIDTH, (part + 1) * ATTN_WIDTH)
            gain = gain_ref[...]
            nrm, r, yn = _rmsnorm_cols(src_ref[...].astype(F32), gain)
            y_ref[rows, :] = yn.astype(BF16)
            dy = _dg(w_ref[rows, :], dxb, NT)
            acc_ref[rows, :] += _lane_fold(dy * nrm)
            dn = dy * gain
            dst_ref[...] = (r * (dn - nrm * jnp.mean(dn * nrm, axis=0, keepdims=True))).astype(BF16)

        @pl.when(i == nt - 1)
        def _():
            dgao_ref[...] = jnp.sum(acc_ref[:ATTN_WIDTH, :], axis=1, keepdims=True)
            dggo_ref[...] = jnp.sum(acc_ref[ATTN_WIDTH:, :], axis=1, keepdims=True)

    const2 = lambda i: (0, 0)
    feat = lambda w: pl.BlockSpec((w, tm), lambda i: (0, i))
    return pl.pallas_call(
        _drop_deps(body, 6, len(deps)), name=name, grid=(nt,),
        in_specs=[pl.BlockSpec((tm, d), lambda i: (i, 0)), feat(ATTN_WIDTH), feat(GMLP_WIDTH),
                  _resident((ATTN_WIDTH, 1)), _resident((GMLP_WIDTH, 1)), _resident((ATTN_WIDTH + GMLP_WIDTH, d))]
        + [ANY_SPEC] * len(deps),
        out_specs=[feat(ATTN_WIDTH), feat(GMLP_WIDTH), feat(ATTN_WIDTH + GMLP_WIDTH),
                   pl.BlockSpec((ATTN_WIDTH, 1), const2), pl.BlockSpec((GMLP_WIDTH, 1), const2),
                   pl.BlockSpec((1, d), const2)],
        out_shape=[jax.ShapeDtypeStruct((ATTN_WIDTH, t), BF16), jax.ShapeDtypeStruct((GMLP_WIDTH, t), BF16),
                   jax.ShapeDtypeStruct((ATTN_WIDTH + GMLP_WIDTH, t), BF16),
                   jax.ShapeDtypeStruct((ATTN_WIDTH, 1), F32), jax.ShapeDtypeStruct((GMLP_WIDTH, 1), F32),
                   jax.ShapeDtypeStruct((1, d), F32)],
        scratch_shapes=[pltpu.VMEM((ATTN_WIDTH + GMLP_WIDTH, LANES), F32)],
        compiler_params=_params(("arbitrary",)),
    )(dx, ya_t, yg_t, gao, ggo, wout, *deps)


def mix_in_bwd(x, dxo, gain, dq_t, dk_t, dv_t, dzg_t, win_t, name):
    t, d = x.shape
    w = win_t.shape[0]
    tm = _token_tile(t)
    nt = t // tm
    parts = ((0, ATTN_WIDTH), (ATTN_WIDTH, KV_WIDTH), (ATTN_WIDTH + KV_WIDTH, KV_WIDTH), (QKV_WIDTH, w - QKV_WIDTH))

    def body(x_ref, dxo_ref, g_ref, dq_ref, dk_ref, dv_ref, dzg_ref, w_ref, dx_ref, h_ref, dg_ref, db_ref, acc_ref):
        i = pl.program_id(0)

        @pl.when(i == 0)
        def _():
            acc_ref[...] = jnp.zeros_like(acc_ref)
            dg_ref[...] = jnp.zeros_like(dg_ref)

        xv = x_ref[...]
        gain_v = g_ref[...]
        r = lax.rsqrt(jnp.mean(xv * xv, axis=-1, keepdims=True) + EPS)
        h_ref[...] = (xv * r * gain_v).astype(BF16)
        dh = jnp.zeros((tm, d), F32)
        for (off, size), src_ref in zip(parts, (dq_ref, dk_ref, dv_ref, dzg_ref)):
            dp = src_ref[...]
            acc_ref[off:off + size, :] += _lane_fold(dp.astype(F32))
            dh = dh + _dg(dp, w_ref[off:off + size, :], TN)
        dx, dgain = _rmsnorm_bwd_rows(xv, r, gain_v, dh)
        dx_ref[...] = dxo_ref[...] + dx
        dg_ref[...] += dgain

        @pl.when(i == nt - 1)
        def _():
            db_ref[...] = jnp.sum(acc_ref[...], axis=1, keepdims=True)

    const2 = lambda i: (0, 0)
    tok = lambda: pl.BlockSpec((tm, d), lambda i: (i, 0))
    feat = lambda rows: pl.BlockSpec((rows, tm), lambda i: (0, i))
    return pl.pallas_call(
        body, name=name, grid=(nt,),
        in_specs=[tok(), tok(), _resident((1, d)), feat(ATTN_WIDTH), feat(KV_WIDTH), feat(KV_WIDTH),
                  feat(w - QKV_WIDTH), _resident((w, d))],
        out_specs=[tok(), tok(), pl.BlockSpec((1, d), const2), pl.BlockSpec((w, 1), const2)],
        out_shape=[jax.ShapeDtypeStruct((t, d), F32), jax.ShapeDtypeStruct((t, d), BF16),
                   jax.ShapeDtypeStruct((1, d), F32), jax.ShapeDtypeStruct((w, 1), F32)],
        scratch_shapes=[pltpu.VMEM((w, LANES), F32)],
        compiler_params=_params(("arbitrary",)),
    )(x, dxo, gain, dq_t, dk_t, dv_t, dzg_t, win_t)


def _adamw_math(w, g, m, v):
    m = ADAM_B1 * m + (1.0 - ADAM_B1) * g
    v = ADAM_B2 * v + (1.0 - ADAM_B2) * (g * g)
    m_hat = m / (1.0 - ADAM_B1 ** ADAM_STEP)
    v_hat = v / (1.0 - ADAM_B2 ** ADAM_STEP)
    delta = -ADAM_LR * (m_hat / (jnp.sqrt(v_hat) + ADAM_EPS) + ADAM_WD * w)
    return delta, m, v


def adamw(w, g, m, v, name):
    rows, cols = w.shape
    tr = rows
    if rows * cols > 256 * 1024:
        for cand in (256, 128, 64, 32, 16, 8):
            if rows % cand == 0:
                tr = cand
                break

    def body(w_ref, g_ref, m_ref, v_ref, d_ref, mo_ref, vo_ref):
        delta, mn, vn = _adamw_math(w_ref[...], g_ref[...], m_ref[...], v_ref[...])
        d_ref[...] = delta
        mo_ref[...] = mn
        vo_ref[...] = vn

    spec = pl.BlockSpec((tr, cols), lambda i: (i, 0))
    shape = jax.ShapeDtypeStruct((rows, cols), F32)
    return pl.pallas_call(
        body, name=name, grid=(rows // tr,),
        in_specs=[spec] * 4, out_specs=[spec] * 3, out_shape=[shape] * 3,
        compiler_params=_params(("arbitrary",)),
    )(w, g, m, v)


def sum_adamw(landing, grad3d, me, w, m, v, name):
    nd, rows, cols = landing.shape
    tr = rows // 2

    def body(me_ref, l_ref, own_ref, w_ref, m_ref, v_ref, g_ref, d_ref, mo_ref, vo_ref):
        g = own_ref[...].astype(F32)
        for j in range(nd):
            g = g + l_ref[j].astype(F32)
        delta, mn, vn = _adamw_math(w_ref[...], g, m_ref[...], v_ref[...])
        g_ref[...] = g
        d_ref[...] = delta
        mo_ref[...] = mn
        vo_ref[...] = vn

    spec = lambda: pl.BlockSpec((tr, cols), lambda i, me_ref: (i, 0))
    shape = jax.ShapeDtypeStruct((rows, cols), F32)
    return pl.pallas_call(
        body, name=name,
        grid_spec=pltpu.PrefetchScalarGridSpec(
            num_scalar_prefetch=1, grid=(rows // tr,),
            in_specs=[pl.BlockSpec((nd, tr, cols), lambda i, me_ref: (0, i, 0)),
                      pl.BlockSpec((None, tr, cols), lambda i, me_ref: (me_ref[0], i, 0)), spec(), spec(), spec()],
            out_specs=[spec() for _ in range(4)]),
        out_shape=[shape] * 4,
        compiler_params=_params(("arbitrary",)),
    )(me, landing, grad3d, w, m, v)


def _mesh_place():
    x, y, c = lax.axis_index("x"), lax.axis_index("y"), lax.axis_index("c")
    return x, y, c, 4 * x + 2 * y + c


def _peer(x, y, c, k):
    return (x ^ ((k >> 2) & 1), y ^ ((k >> 1) & 1), c ^ (k & 1))


def prep_shards(mats, me, name):
    na = len(mats)

    def body(me_ref, *refs):
        in_refs, shard_refs, land_refs = refs[:na], refs[na:2 * na], refs[2 * na:]
        for a in range(na):
            val = in_refs[a][...].astype(BF16)
            shard_refs[a][...] = val
            land_refs[a][...] = val

    whole = lambda mat: pl.BlockSpec(mat.shape, lambda i, me_ref: (0, 0))
    outs = pl.pallas_call(
        body, name=name,
        grid_spec=pltpu.PrefetchScalarGridSpec(
            num_scalar_prefetch=1, grid=(1,),
            in_specs=[whole(mat) for mat in mats],
            out_specs=[whole(mat) for mat in mats]
            + [pl.BlockSpec((None,) + mat.shape, lambda i, me_ref: (me_ref[0], 0, 0)) for mat in mats]),
        out_shape=[jax.ShapeDtypeStruct(mat.shape, BF16) for mat in mats]
        + [jax.ShapeDtypeStruct((N_DEV,) + mat.shape, BF16) for mat in mats],
        compiler_params=_params(("arbitrary",)),
    )(me, *mats)
    return outs[:na], outs[na:]


def allgather_rows(shards, lands, name):
    na = n_gather = len(shards)

    def body(*refs):
        in_refs, out_refs = refs[:na], refs[2 * na:3 * na]
        send_sems, recv_sems = refs[3 * na:]
        x, y, c, me = _mesh_place()

        def copy(sem, a, origin, src, k):
            return pltpu.make_async_remote_copy(
                src_ref=src, dst_ref=out_refs[a].at[origin], send_sem=send_sems.at[sem, a],
                recv_sem=recv_sems.at[sem, a], device_id=_peer(x, y, c, k), device_id_type=pl.DeviceIdType.MESH)

        kx, ky = 4, 2

        def recv(sem, a, origin):
            copy(sem, a, origin, in_refs[a], 1).wait_recv()

        def landed(a, origin):
            return out_refs[a].at[origin]

        @pl.when(c == 1)
        def _():
            sends = []

            def send(sem, a, origin, src, k):
                cp = copy(sem, a, origin, src, k)
                cp.start()
                sends.append(cp)

            for a in range(n_gather):
                send(1, a, me, in_refs[a], kx)
                send(2, a, me, in_refs[a], ky)
                send(0, a, me, in_refs[a], 1)
            for a in range(n_gather):
                recv(0, a, me ^ 1)
                send(3, a, me ^ 1, landed(a, me ^ 1), kx)
                send(4, a, me ^ 1, landed(a, me ^ 1), ky)
            for a in range(n_gather):
                recv(1, a, me ^ kx)
                send(5, a, me ^ kx, landed(a, me ^ kx), ky)
                send(7, a, me ^ kx, landed(a, me ^ kx), 1)
                recv(2, a, me ^ ky)
                send(8, a, me ^ ky, landed(a, me ^ ky), 1)
            for a in range(n_gather):
                recv(3, a, me ^ kx ^ 1)
                send(9, a, me ^ kx ^ 1, landed(a, me ^ kx ^ 1), 1)
                recv(4, a, me ^ ky ^ 1)
                send(6, a, me ^ ky ^ 1, landed(a, me ^ ky ^ 1), kx)
                send(10, a, me ^ ky ^ 1, landed(a, me ^ ky ^ 1), 1)
            for a in range(n_gather):
                recv(5, a, me ^ 6)
                send(11, a, me ^ 6, landed(a, me ^ 6), 1)
                recv(6, a, me ^ 7)
                send(12, a, me ^ 7, landed(a, me ^ 7), 1)
            for cp in sends:
                cp.wait_send()

        @pl.when(c == 0)
        def _():
            north = me ^ 1
            own = [copy(0, a, me, in_refs[a], 1) for a in range(n_gather)]
            for cp in own:
                cp.start()
            for a in range(n_gather):
                recv(0, a, north)
                for sem, origin in ((7, north ^ kx), (8, north ^ ky), (9, north ^ kx ^ 1), (10, north ^ ky ^ 1),
                                    (11, north ^ 6), (12, north ^ 7)):
                    recv(sem, a, origin)
            for cp in own:
                cp.wait_send()

    return pl.pallas_call(
        body, name=name,
        in_specs=[ANY_SPEC] * (2 * na), out_specs=[ANY_SPEC] * na,
        out_shape=[jax.ShapeDtypeStruct(land.shape, land.dtype) for land in lands],
        input_output_aliases={na + a: a for a in range(na)},
        scratch_shapes=[pltpu.SemaphoreType.DMA((N_GATHER_ROLES, n_gather)),
                        pltpu.SemaphoreType.DMA((N_GATHER_ROLES, n_gather))],
    )(*shards, *lands)


HBM_SPEC = pl.BlockSpec(memory_space=pltpu.HBM)
SEM_SPEC = pl.BlockSpec(memory_space=pltpu.SEMAPHORE)
SIDE_EFFECT = pltpu.SideEffectType.DATAFLOW_SIDE_EFFECTING


def _hbm(v):
    return pltpu.with_memory_space_constraint(v, pltpu.HBM)


def gather_slices(src, land, me, k):
    return src, land.at[me], land.at[me ^ k]


def exchange_slices(src, land, me, k):
    return src.at[me ^ k], land.at[k - 1], land.at[k - 1]


def split_start(groups, slices, deps, name):
    sizes = [len(srcs) for srcs, _ in groups]
    flat_src = [s for srcs, _ in groups for s in srcs]
    flat_land = [l for _, lands in groups for l in lands]
    na, ng, nd = len(flat_src), len(groups), len(deps)

    def body(*refs):
        src_refs, land_refs = refs[:na], refs[na:2 * na]
        sem_refs = refs[2 * na + nd:2 * na + nd + 2 * ng]
        token_ref = refs[-1]
        x, y, c, me = _mesh_place()
        a0 = 0
        for gi, size in enumerate(sizes):
            send_sems, recv_sems = sem_refs[2 * gi], sem_refs[2 * gi + 1]
            for k in range(1, N_DEV):
                for j in range(size):
                    src, dst, _ = slices(src_refs[a0 + j], land_refs[a0 + j], me, k)
                    pltpu.make_async_remote_copy(
                        src_ref=src, dst_ref=dst, send_sem=send_sems.at[(k - 1) * size + j],
                        recv_sem=recv_sems.at[(k - 1) * size + j],
                        device_id=_peer(x, y, c, k), device_id_type=pl.DeviceIdType.MESH).start()
            a0 += size
        token_ref[...] = jnp.zeros_like(token_ref)

    sems = [pltpu.SemaphoreType.DMA(((N_DEV - 1) * size,)) for size in sizes for _ in range(2)]
    thru = [pltpu.HBM(v.shape, v.dtype) for v in flat_src + flat_land]
    outs = pl.pallas_call(
        body, name=name,
        in_specs=[HBM_SPEC] * (2 * na) + [ANY_SPEC] * nd,
        out_specs=[SEM_SPEC] * (2 * ng) + [HBM_SPEC] * (2 * na) + [pl.BlockSpec(memory_space=pltpu.VMEM)],
        out_shape=sems + thru + [jax.ShapeDtypeStruct((8, LANES), F32)],
        input_output_aliases={i: 2 * ng + i for i in range(2 * na)},
        compiler_params=pltpu.CompilerParams(has_side_effects=SIDE_EFFECT),
    )(*[_hbm(v) for v in flat_src + flat_land], *deps)
    started, a0 = [], 0
    for gi, size in enumerate(sizes):
        srcs = outs[2 * ng + a0:2 * ng + a0 + size]
        lands = outs[2 * ng + na + a0:2 * ng + na + a0 + size]
        started.append((outs[2 * gi], outs[2 * gi + 1], list(srcs), list(lands)))
        a0 += size
    return started, outs[-1]


def split_wait(started, slices, after, name):
    send_sems, recv_sems, srcs, lands = started
    na = len(srcs)

    def body(*refs):
        src_refs, land_refs = refs[:na], refs[na:2 * na]
        send_ref, recv_ref = refs[2 * na], refs[2 * na + 1]
        x, y, c, me = _mesh_place()
        for k in range(1, N_DEV):
            for j in range(na):
                src, _, got = slices(src_refs[j], land_refs[j], me, k)
                cp = pltpu.make_async_remote_copy(
                    src_ref=src, dst_ref=got, send_sem=send_ref.at[(k - 1) * na + j], recv_sem=recv_ref.at[(k - 1) * na + j],
                    device_id=_peer(x, y, c, k), device_id_type=pl.DeviceIdType.MESH)
                cp.wait_send()
                cp.wait_recv()

    outs = pl.pallas_call(
        body, name=name,
        in_specs=[HBM_SPEC] * (2 * na) + [SEM_SPEC, SEM_SPEC] + [ANY_SPEC] * len(after),
        out_specs=[HBM_SPEC] * (2 * na),
        out_shape=[pltpu.HBM(v.shape, v.dtype) for v in srcs + lands],
        input_output_aliases={i: i for i in range(2 * na)},
        compiler_params=pltpu.CompilerParams(has_side_effects=SIDE_EFFECT),
    )(*srcs, *lands, send_sems, recv_sems, *after)
    return list(outs[:na]), list(outs[na:])


def place_own(pack, me, name):
    rows, cols = pack.shape

    def body(me_ref, p_ref, o_ref):
        o_ref[...] = p_ref[...]

    return pl.pallas_call(
        body, name=name,
        grid_spec=pltpu.PrefetchScalarGridSpec(
            num_scalar_prefetch=1, grid=(1,),
            in_specs=[pl.BlockSpec((rows, cols), lambda i, me_ref: (0, 0))],
            out_specs=pl.BlockSpec((None, rows, cols), lambda i, me_ref: (me_ref[0], 0, 0))),
        out_shape=jax.ShapeDtypeStruct((N_DEV, rows, cols), F32),
        compiler_params=_params(("arbitrary",)),
    )(me, pack)


def sum_slots(slots, name):
    nd, rows, cols = slots.shape

    def body(s_ref, o_ref):
        acc = s_ref[0]
        for j in range(1, nd):
            acc = acc + s_ref[j]
        o_ref[...] = acc

    return pl.pallas_call(
        body, name=name, grid=(1,),
        in_specs=[pl.BlockSpec((nd, rows, cols), lambda i: (0, 0, 0))],
        out_specs=pl.BlockSpec((rows, cols), lambda i: (0, 0)),
        out_shape=jax.ShapeDtypeStruct((rows, cols), F32),
        compiler_params=_params(("arbitrary",)),
    )(slots)


def _pack_rows(size):
    rows = -(-size // LANES)
    return -(-rows // 8) * 8


def pack_small(parts):
    out = []
    for p in parts:
        flat = p.reshape(-1).astype(F32)
        rows = _pack_rows(flat.shape[0])
        out.append(jnp.pad(flat, (0, rows * LANES - flat.shape[0])).reshape(rows, LANES))
    return jnp.concatenate(out, axis=0)


def unpack_small(pack, shapes):
    out, off = [], 0
    for shp in shapes:
        size = 1
        for s in shp:
            size *= s
        rows = _pack_rows(size)
        out.append(pack[off:off + rows].reshape(-1)[:size].reshape(shp))
        off += rows
    return out


def local_step(x, target, small, get_w, put_g, put_small):
    col = lambda v: v.reshape(-1, 1)
    (wg1, wu1), deps = get_w(("wg1", "wu1"), ())
    sil1, gp1, s1, h1 = ffn_fwd(x, small["ffn1_norm_g"], wg1, wu1, None, "ffn1_fwd", deps)
    (wd1, win, wout), _ = get_w(("wd1", "win", "wout"), (s1,))
    qkv_t, zg_t, x1 = mix_in_fwd(x, small["mix_norm_g"], win, col(small["b_in"]), "mix_in_fwd", ffn_tail=(s1, wd1))
    ya_t = attn_fwd(qkv_t, small["attn_sinks"], "attn_fwd")
    lng, lnb = col(small["gmlp_ln_g"]), col(small["gmlp_ln_b"])
    w_s, b_s = small["gmlp_w_s"][0], small["gmlp_b_s"][0]
    yg_t = gmlp_fwd(zg_t, lng, lnb, w_s, b_s, "gmlp_fwd")
    gao, ggo = col(small["attn_out_norm_g"]), col(small["gmlp_out_norm_g"])
    x2 = mix_out_fwd(x1, ya_t, yg_t, gao, ggo, wout, small["b_out"], "mix_out_fwd")
    (wg2, wu2, wd2), _ = get_w(("wg2", "wu2", "wd2"), (x2,))
    gs = {}
    dx3, sil2, gp2, s2, h2f, gs["final_norm_g"], loss = ffn_fwd(
        x2, small["ffn2_norm_g"], wg2, wu2, wd2, "ffn2_fwd", head=(target, small["final_norm_g"].reshape(1, -1)))

    dx2, da, db, df, gs["ffn2_norm_g"] = ffn_bwd(x2, dx3, small["ffn2_norm_g"], sil2, gp2, wg2, wu2, wd2, "ffn2_bwd")
    deps = put_g({"wg2": grad_tn(da, h2f, 1.0, "ffn2_dwg"), "wu2": grad_tn(db, h2f, 1.0, "ffn2_dwu"),
                  "wd2": grad_tn(s2, df, 1.0, "ffn2_dwd")})

    dya_t, dyg_t, y_t, dgao, dggo, gs["b_out"] = mix_out_bwd(dx2, ya_t, yg_t, gao, ggo, wout, "mix_out_bwd", deps)
    gs["attn_out_norm_g"], gs["gmlp_out_norm_g"] = dgao.reshape(1, -1), dggo.reshape(1, -1)
    g_wout = grad_nn([y_t], dx2, "dwout")
    dzg_t, dws, dbs, dlng, dlnb = gmlp_bwd(zg_t, dyg_t, lng, lnb, w_s, b_s, "gmlp_bwd")
    gs["gmlp_w_s"], gs["gmlp_b_s"] = dws[None], dbs[None]
    gs["gmlp_ln_g"], gs["gmlp_ln_b"] = dlng.reshape(1, -1), dlnb.reshape(1, -1)
    dq_t, dk_t, dv_t, dsinks = attn_bwd(qkv_t, dya_t, small["attn_sinks"], "attn_bwd")
    gs["attn_sinks"] = dsinks.reshape(1, -1)
    dx1, h2, gs["mix_norm_g"], dbin = mix_in_bwd(x1, dx2, small["mix_norm_g"], dq_t, dk_t, dv_t, dzg_t, win,
                                                 "mix_in_bwd")
    gs["b_in"] = dbin.reshape(1, -1)
    deps = put_g({"wout": g_wout, "win": grad_nn([dq_t, dk_t, dv_t, dzg_t], h2, "dwin")})

    dx0, da, db, df, gs["ffn1_norm_g"] = ffn_bwd(x, dx1, small["ffn1_norm_g"], sil1, gp1, wg1, wu1, wd1, "ffn1_bwd",
                                             deps)
    deps = put_small(gs, loss)
    deps = put_g({"wg1": grad_tn(da, h1, 1.0, "ffn1_dwg", deps)})
    deps = put_g({"wu1": grad_tn(db, h1, 1.0, "ffn1_dwu", deps)})
    put_g({"wd1": grad_tn(s1, df, 1.0, "ffn1_dwd", deps)})
    return dx0


SMALL_NAMES = ["ffn1_norm_g", "mix_norm_g", "b_in", "attn_sinks", "gmlp_ln_g", "gmlp_ln_b", "gmlp_w_s", "gmlp_b_s",
               "attn_out_norm_g", "gmlp_out_norm_g", "b_out", "ffn2_norm_g", "final_norm_g"]
BIG_NAMES = {"wg1": ("ffn1_w_gate", True), "wu1": ("ffn1_w_up", True), "wd1": ("ffn1_w_down", False),
             "win": ("w_in", True), "wout": ("w_out", False),
             "wg2": ("ffn2_w_gate", True), "wu2": ("ffn2_w_up", True), "wd2": ("ffn2_w_down", False)}
WEIGHT_ORDER = ["ffn1_norm_g", "ffn1_w_gate", "ffn1_w_up", "ffn1_w_down", "mix_norm_g", "w_in", "b_in", "attn_sinks",
                "gmlp_ln_g", "gmlp_ln_b", "gmlp_w_s", "gmlp_b_s", "attn_out_norm_g", "gmlp_out_norm_g", "w_out",
                "b_out", "ffn2_norm_g", "ffn2_w_gate", "ffn2_w_up", "ffn2_w_down", "final_norm_g"]


def kernel(x, ffn1_norm_g, ffn1_w_gate, ffn1_w_up, ffn1_w_down, mix_norm_g, w_in, b_in, attn_sinks, gmlp_ln_g, gmlp_ln_b, gmlp_w_s, gmlp_b_s, attn_out_norm_g, gmlp_out_norm_g, w_out, b_out, ffn2_norm_g, ffn2_w_gate, ffn2_w_up, ffn2_w_down, final_norm_g, loss_target, m_ffn1_norm_g, m_ffn1_w_gate, m_ffn1_w_up, m_ffn1_w_down, m_mix_norm_g, m_w_in, m_b_in, m_attn_sinks, m_gmlp_ln_g, m_gmlp_ln_b, m_gmlp_w_s, m_gmlp_b_s, m_attn_out_norm_g, m_gmlp_out_norm_g, m_w_out, m_b_out, m_ffn2_norm_g, m_ffn2_w_gate, m_ffn2_w_up, m_ffn2_w_down, m_final_norm_g, v_ffn1_norm_g, v_ffn1_w_gate, v_ffn1_w_up, v_ffn1_w_down, v_mix_norm_g, v_w_in, v_b_in, v_attn_sinks, v_gmlp_ln_g, v_gmlp_ln_b, v_gmlp_w_s, v_gmlp_b_s, v_attn_out_norm_g, v_gmlp_out_norm_g, v_w_out, v_b_out, v_ffn2_norm_g, v_ffn2_w_gate, v_ffn2_w_up, v_ffn2_w_down, v_final_norm_g):
    args = dict(locals())
    w = {n: args[n] for n in WEIGHT_ORDER}
    m = {n: args["m_" + n] for n in WEIGHT_ORDER}
    v = {n: args["v_" + n] for n in WEIGHT_ORDER}

    d_model = x.shape[-1]
    flat = lambda g: g.reshape(-1, d_model)
    me = _mesh_place()[3].astype(jnp.int32).reshape(1)
    first = ("wg1", "wu1")
    later = (("wd1", "win", "wout"), ("wg2", "wu2", "wd2"))
    gathers, exchanges, last_token = {}, [], []

    order = list(first) + [k for grp in later for k in grp]
    mats = [w[BIG_NAMES[k][0]][0].T if BIG_NAMES[k][1] else w[BIG_NAMES[k][0]][0] for k in order]
    shard_list, land_list = prep_shards(mats, me, "prep_shards")
    shards, zones = dict(zip(order, shard_list)), dict(zip(order, land_list))

    def get_w(keys, after):
        if keys == first:
            full = allgather_rows([shards[k] for k in first], [zones[k] for k in first], "allgather_ffn1")
            started, token = split_start([([shards[k] for k in grp], [zones[k] for k in grp]) for grp in later],
                                         gather_slices, (full[0],), "gather_start")
            gathers.update(zip(later, started))
            return [flat(g) for g in full], (token,)
        _, lands = split_wait(gathers[keys], gather_slices, after, "gather_wait_" + keys[0])
        return [flat(land) for land in lands], ()

    def put_g(gb):
        keys = tuple(gb)
        g3 = [gb[k].reshape(N_DEV, -1, d_model) for k in keys]
        lands = [lax.empty((N_DEV - 1,) + g.shape[1:], BF16) for g in g3]
        started, token = split_start([(g3, lands)], exchange_slices, (), "exchange_start_" + keys[0])
        exchanges.append((keys, started[0]))
        last_token[:] = [token]
        return (token,)

    small_started = []

    def put_small(gs, loss):
        pack = pack_small([gs[n] for n in SMALL_NAMES] + [loss[:, :1]])
        started, token = split_start([([pack], [place_own(pack, me, "place_small")])], gather_slices, (), "small_start")
        small_started[:] = started
        return (token,)

    small = {n: w[n] for n in SMALL_NAMES}
    grad_x = local_step(x[0], loss_target[0], small, get_w, put_g, put_small)

    grads, deltas, new_m, new_v = {}, {}, {}, {}
    shapes = [w[n].shape for n in SMALL_NAMES]
    _, (slots,) = split_wait(small_started[0], gather_slices, tuple(last_token), "small_wait")
    total = sum_slots(slots, "sum_small")
    small_g = unpack_small(total, shapes + [(1, 1)])
    loss_total = small_g[-1].reshape(())
    gpack = total[:sum(_pack_rows(w[n].size) for n in SMALL_NAMES)]
    d_, m_, v_ = adamw(pack_small([w[n] for n in SMALL_NAMES]), gpack, pack_small([m[n] for n in SMALL_NAMES]),
                       pack_small([v[n] for n in SMALL_NAMES]), "adamw_small")
    for n, g_, dd, mm, vv in zip(SMALL_NAMES, small_g[:-1], unpack_small(d_, shapes), unpack_small(m_, shapes),
                                 unpack_small(v_, shapes)):
        grads[n], deltas[n], new_m[n], new_v[n] = g_, dd, mm, vv

    after = tuple(arr for n in SMALL_NAMES for arr in (grads[n], deltas[n], new_m[n], new_v[n]))
    for keys, started in exchanges:
        g3, lands = split_wait(started, exchange_slices, after, "exchange_wait_" + keys[0])
        for key, own, land in zip(keys, g3, lands):
            pname, transposed = BIG_NAMES[key]
            to_rows = (lambda p: p[0].T) if transposed else (lambda p: p[0])
            from_rows = (lambda r: r.T[None]) if transposed else (lambda r: r[None])
            g, d_, m_, v_ = sum_adamw(land, own, me, to_rows(w[pname]), to_rows(m[pname]), to_rows(v[pname]),
                                      "adamw_" + key)
            after = after + (g,)
            grads[pname], deltas[pname], new_m[pname], new_v[pname] = (from_rows(g), from_rows(d_), from_rows(m_),
                                                                        from_rows(v_))

    return (loss_total, grad_x[None], *[grads[n] for n in WEIGHT_ORDER], *[deltas[n] for n in WEIGHT_ORDER],
            *[new_m[n] for n in WEIGHT_ORDER], *[new_v[n] for n in WEIGHT_ORDER])
```

```python
import functools

import jax
import jax.numpy as jnp
from jax import lax
from jax.experimental import pallas as pl
from jax.experimental.pallas import tpu as pltpu

F32 = jnp.float32
BF16 = jnp.bfloat16

N_DEV = 8
N_Q_HEADS = 8
N_KV_HEADS = 2
HEAD_DIM = 64
ATTN_WIDTH = N_Q_HEADS * HEAD_DIM
KV_WIDTH = N_KV_HEADS * HEAD_DIM
QKV_WIDTH = ATTN_WIDTH + 2 * KV_WIDTH
BLOCK = 128
GMLP_GROUPS = 8
GMLP_GROUP_DIM = 64
GMLP_WIDTH = GMLP_GROUPS * GMLP_GROUP_DIM
EPS = 1e-6
FFN_RES = 0.5
ATTN_SCALE = HEAD_DIM ** -0.5
MASK_VALUE = -1e30

ADAM_LR = 0.001
ADAM_B1 = 0.9
ADAM_B2 = 0.999
ADAM_EPS = 1e-08
ADAM_WD = 0.01
ADAM_STEP = 10

V7X_VMEM_BYTES = 64 * 1024 * 1024
VMEM_LIMIT = 62 * 1024 * 1024
LANES = 128
MXU_WIDTH = 256
FFN_BWD_TOKENS = 256

NT = (((1,), (1,)), ((), ()))
TN = (((0,), (0,)), ((), ()))


def _dot(a, b):
    return jnp.dot(a, b, preferred_element_type=F32)


def _dg(a, b, dims):
    return lax.dot_general(a, b, dims, preferred_element_type=F32)


def _params(semantics=None):
    return pltpu.CompilerParams(dimension_semantics=semantics, vmem_limit_bytes=VMEM_LIMIT)


def _resident(shape):
    zeros = (0,) * len(shape)
    return pl.BlockSpec(shape, lambda *_: zeros, pipeline_mode=pl.Buffered(1))


def _drop_deps(body, n_in, n_deps):
    return lambda *refs: body(*refs[:n_in], *refs[n_in + n_deps:])


ANY_SPEC = pl.BlockSpec(memory_space=pl.ANY)


TOKEN_TILE = 512
GRAD_TOKENS_PER_STEP = 2048
GRAD_ROW_TILE_CAP = 1408
N_GATHER_ROLES = 13


def _token_tile(t):
    return min(t, TOKEN_TILE)


def _f_chunk(f):
    for c in (256, 128):
        if f % c == 0:
            return c
    return f


def _loss_head(xv, target, gain):
    r = lax.rsqrt(jnp.mean(xv * xv, axis=-1, keepdims=True) + EPS)
    err = xv * r * gain - target
    loss = 0.5 * jnp.sum(jnp.mean(err * err, axis=-1, keepdims=True), axis=0, keepdims=True)
    dx, dgain = _rmsnorm_bwd_rows(xv, r, gain, err * (1.0 / xv.shape[-1]))
    return dx, dgain, loss


def ffn_fwd(x, gain, wg_t, wu_t, wd, name, deps=(), head=None):
    t, d = x.shape
    f = wg_t.shape[0]
    tm = _token_tile(t)
    fc = _f_chunk(f)
    weights = [wg_t, wu_t] + ([] if wd is None else [wd])
    n_in = 2 + len(weights) + (0 if head is None else 2)
    n_x = 0 if wd is None else 1

    def body(x_ref, g_ref, wg_ref, wu_ref, *rest):
        sil_ref, gp_ref, s_ref, h_ref = rest[n_in - 4 + n_x:n_in + n_x]
        xv = x_ref[...]
        r = lax.rsqrt(jnp.mean(xv * xv, axis=-1, keepdims=True) + EPS)
        h = (xv * r * g_ref[...]).astype(BF16)
        h_ref[...] = h
        for c in range(f // fc):
            sl = slice(c * fc, (c + 1) * fc)
            a = _dg(h, wg_ref[sl, :], NT)
            b = _dg(h, wu_ref[sl, :], NT)
            sig = jax.nn.sigmoid(a)
            sil = a * sig
            sil_ref[:, sl] = sil.astype(BF16)
            gp_ref[:, sl] = (b * (sig + sil - sil * sig)).astype(BF16)
            s_ref[:, sl] = (sil * b).astype(BF16)
        if wd is None:
            return
        wd_ref, xo_ref = rest[0], rest[n_in - 4]
        xo = xv + FFN_RES * _dot(s_ref[...], wd_ref[...])
        if head is None:
            xo_ref[...] = xo
        else:
            t_ref, hg_ref = rest[1:3]
            dg_ref, loss_ref = rest[-2:]

            @pl.when(pl.program_id(0) == 0)
            def _():
                dg_ref[...] = jnp.zeros_like(dg_ref)
                loss_ref[...] = jnp.zeros_like(loss_ref)

            dx, dgain, loss = _loss_head(xo, t_ref[...], hg_ref[...])
            xo_ref[...] = dx
            dg_ref[...] += dgain
            loss_ref[...] += loss

    tok = lambda: pl.BlockSpec((tm, d), lambda i: (i, 0))
    wide = lambda: pl.BlockSpec((tm, f), lambda i: (i, 0))
    const2 = lambda i: (0, 0)
    in_specs = [tok(), _resident((1, d))] + [_resident((f, d)) for _ in weights]
    out_specs = [tok()] * n_x + [wide(), wide(), wide(), tok()]
    out_shape = ([jax.ShapeDtypeStruct((t, d), F32)] * n_x + [jax.ShapeDtypeStruct((t, f), BF16)] * 3
                 + [jax.ShapeDtypeStruct((t, d), BF16)])
    operands = [x, gain] + weights
    if head is not None:
        in_specs += [tok(), _resident((1, d))]
        out_specs += [pl.BlockSpec((1, d), const2), pl.BlockSpec((1, LANES), const2)]
        out_shape += [jax.ShapeDtypeStruct((1, d), F32), jax.ShapeDtypeStruct((1, LANES), F32)]
        operands += list(head)
    return pl.pallas_call(
        _drop_deps(body, n_in, len(deps)), name=name, grid=(t // tm,),
        in_specs=in_specs + [ANY_SPEC] * len(deps), out_specs=out_specs, out_shape=out_shape,
        compiler_params=_params(("arbitrary",)),
    )(*operands, *deps)


def _rmsnorm_bwd_rows(xv, r, gain, dh):
    n = xv * r
    dgain = jnp.sum(dh * n, axis=0, keepdims=True)
    dn = dh * gain
    dx = r * (dn - n * jnp.mean(dn * n, axis=-1, keepdims=True))
    return dx, dgain


def ffn_bwd(x, dxo, gain, sil, gp, wg_t, wu_t, wd, name, deps=()):
    t, d = x.shape
    f = wd.shape[0]
    tm = min(t, FFN_BWD_TOKENS)
    nt = t // tm
    chunks = [(off, min(MXU_WIDTH, f - off)) for off in range(0, f, MXU_WIDTH)]

    def body(xp_ref, dxp_ref, dxo_ref, g_ref, sil_ref, gp_ref, wg_ref, wu_ref, wd_ref,
             dx_ref, da_ref, db_ref, df_ref, dg_ref, dh_ref):
        i = pl.program_id(0)
        gain_v = g_ref[...]

        @pl.when(i == 0)
        def _():
            dh_ref[...] = jnp.zeros_like(dh_ref)
            dg_ref[...] = jnp.zeros_like(dg_ref)

        def finish_previous(slot):
            xv = xp_ref[...]
            r = lax.rsqrt(jnp.mean(xv * xv, axis=-1, keepdims=True) + EPS)
            dx, dgain = _rmsnorm_bwd_rows(xv, r, gain_v, dh_ref[slot])
            dx_ref[...] = dxp_ref[...] + dx
            dg_ref[...] += dgain

        @pl.when(i < nt)
        def _():
            df = (FFN_RES * dxo_ref[...]).astype(BF16)
            df_ref[...] = df
            for off, size in chunks:
                sl = slice(off, off + size)
                ds = _dg(df, wd_ref[sl, :], NT)
                da_ref[:, sl] = (ds * gp_ref[:, sl].astype(F32)).astype(BF16)
                db_ref[:, sl] = (ds * sil_ref[:, sl].astype(F32)).astype(BF16)
            dh = _dot(da_ref[...], wg_ref[...]) + _dot(db_ref[...], wu_ref[...])
            finish_previous((i + 1) % 2)
            dh_ref[i % 2] = dh

        @pl.when(i == nt)
        def _():
            finish_previous((nt - 1) % 2)

    prev = lambda i: (jnp.maximum(i - 1, 0), 0)
    cur = lambda i: (jnp.minimum(i, nt - 1), 0)
    return pl.pallas_call(
        _drop_deps(body, 9, len(deps)), name=name, grid=(nt + 1,),
        in_specs=[pl.BlockSpec((tm, d), prev), pl.BlockSpec((tm, d), prev), pl.BlockSpec((tm, d), cur),
                  _resident((1, d)), pl.BlockSpec((tm, f), cur), pl.BlockSpec((tm, f), cur),
                  _resident((f, d)), _resident((f, d)), _resident((f, d))] + [ANY_SPEC] * len(deps),
        out_specs=[pl.BlockSpec((tm, d), prev), pl.BlockSpec((tm, f), cur), pl.BlockSpec((tm, f), cur),
                   pl.BlockSpec((tm, d), cur), pl.BlockSpec((1, d), lambda i: (0, 0))],
        out_shape=[jax.ShapeDtypeStruct((t, d), F32), jax.ShapeDtypeStruct((t, f), BF16),
                   jax.ShapeDtypeStruct((t, f), BF16), jax.ShapeDtypeStruct((t, d), BF16),
                   jax.ShapeDtypeStruct((1, d), F32)],
        scratch_shapes=[pltpu.VMEM((2, tm, d), F32)],
        compiler_params=_params(("arbitrary",)),
    )(x, dxo, dxo, gain, sil, gp, wg_t, wu_t, wd, *deps)


def _row_tile(m, cap):
    if m <= cap:
        return m
    best = None
    for bm in range(LANES, cap + 1, LANES):
        if m % bm == 0:
            best = bm
    return best if best is not None else m


def grad_tn(a, b, scale, name, deps=()):
    t, m = a.shape
    n = b.shape[1]
    bm = _row_tile(m, GRAD_ROW_TILE_CAP)
    bk = min(t, GRAD_TOKENS_PER_STEP)
    nk = t // bk

    def body(a_ref, b_ref, o_ref, acc_ref):
        k = pl.program_id(1)

        @pl.when(k == 0)
        def _():
            acc_ref[...] = jnp.zeros_like(acc_ref)

        acc_ref[...] += _dg(a_ref[...].astype(BF16), b_ref[...].astype(BF16), TN)

        @pl.when(k == nk - 1)
        def _():
            o_ref[...] = (scale * acc_ref[...]).astype(BF16)

    return pl.pallas_call(
        _drop_deps(body, 2, len(deps)), name=name, grid=(m // bm, nk),
        in_specs=[pl.BlockSpec((bk, bm), lambda i, k: (k, i)), pl.BlockSpec((bk, n), lambda i, k: (k, 0))]
        + [ANY_SPEC] * len(deps),
        out_specs=pl.BlockSpec((bm, n), lambda i, k: (i, 0)),
        out_shape=jax.ShapeDtypeStruct((m, n), BF16),
        scratch_shapes=[pltpu.VMEM((bm, n), F32)],
        compiler_params=_params(("arbitrary", "arbitrary")),
    )(a, b, *deps)


def grad_nn(a_list, b, name):
    t, n = b.shape
    ms = [a.shape[0] for a in a_list]
    m = sum(ms)
    bk = min(t, GRAD_TOKENS_PER_STEP)
    nk = t // bk
    na = len(a_list)

    def body(*refs):
        a_refs, b_ref, o_ref, acc_ref = refs[:na], refs[na], refs[na + 1], refs[na + 2]
        k = pl.program_id(0)

        @pl.when(k == 0)
        def _():
            acc_ref[...] = jnp.zeros_like(acc_ref)

        bv = b_ref[...].astype(BF16)
        off = 0
        for a_ref, mi in zip(a_refs, ms):
            acc_ref[off:off + mi, :] += _dot(a_ref[...].astype(BF16), bv)
            off += mi

        @pl.when(k == nk - 1)
        def _():
            o_ref[...] = acc_ref[...].astype(BF16)

    return pl.pallas_call(
        body, name=name, grid=(nk,),
        in_specs=[pl.BlockSpec((mi, bk), lambda k: (0, k)) for mi in ms] + [pl.BlockSpec((bk, n), lambda k: (k, 0))],
        out_specs=pl.BlockSpec((m, n), lambda k: (0, 0)),
        out_shape=jax.ShapeDtypeStruct((m, n), BF16),
        scratch_shapes=[pltpu.VMEM((m, n), F32)],
        compiler_params=_params(("arbitrary",)),
    )(*a_list, b)


def mix_in_fwd(x, gain, win_t, b_in_col, name, ffn_tail=None):
    t, d = x.shape
    w = win_t.shape[0]
    tm = _token_tile(t)

    def body(x_ref, g_ref, w_ref, bias_ref, *rest):
        xv = x_ref[...]
        if ffn_tail is not None:
            s_ref, wd_ref, xo_ref = rest[0], rest[1], rest[4]
            xv = xv + FFN_RES * _dot(s_ref[...], wd_ref[...])
            xo_ref[...] = xv
        qkv_ref, zg_ref, h_ref = rest[2:4] + rest[-1:] if ffn_tail is not None else rest
        r = lax.rsqrt(jnp.mean(xv * xv, axis=-1, keepdims=True) + EPS)
        h = (xv * r * g_ref[...]).astype(BF16)
        h_ref[...] = h
        qkv_ref[...] = (_dg(w_ref[:QKV_WIDTH, :], h, NT) + bias_ref[:QKV_WIDTH, :]).astype(BF16)
        zg_ref[...] = (_dg(w_ref[QKV_WIDTH:, :], h, NT) + bias_ref[QKV_WIDTH:, :]).astype(BF16)

    tok = lambda: pl.BlockSpec((tm, d), lambda i: (i, 0))
    in_specs = [tok(), _resident((1, d)), _resident((w, d)), _resident((w, 1))]
    out_specs = [pl.BlockSpec((QKV_WIDTH, tm), lambda i: (0, i)), pl.BlockSpec((w - QKV_WIDTH, tm), lambda i: (0, i))]
    out_shape = [jax.ShapeDtypeStruct((QKV_WIDTH, t), BF16), jax.ShapeDtypeStruct((w - QKV_WIDTH, t), BF16)]
    operands = [x, gain, win_t, b_in_col]
    if ffn_tail is not None:
        f = ffn_tail[1].shape[0]
        in_specs += [pl.BlockSpec((tm, f), lambda i: (i, 0)), _resident((f, d))]
        out_specs.append(tok())
        out_shape.append(jax.ShapeDtypeStruct((t, d), F32))
        operands += list(ffn_tail)
    out_specs.append(tok())
    out_shape.append(jax.ShapeDtypeStruct((t, d), BF16))
    return pl.pallas_call(
        body, name=name, grid=(t // tm,), in_specs=in_specs, out_specs=out_specs, out_shape=out_shape,
        compiler_params=_params(("arbitrary",)),
    )(*operands)


ATTN_REP = N_Q_HEADS // N_KV_HEADS
ATTN_BLOCKS_PER_STEP = 8


def _attn_tiles(t):
    nbs = min(ATTN_BLOCKS_PER_STEP, t // BLOCK)
    return nbs, nbs * BLOCK, t // (nbs * BLOCK)


def _attn_specs(nbs, tiles):
    last = tiles - 1
    cur = lambda n: jnp.minimum(n, last)
    prev = lambda n: jnp.maximum(jnp.minimum(n, last) * nbs - 1, 0)
    k_row = ATTN_WIDTH // KV_WIDTH
    tw = nbs * BLOCK
    return [
        pl.BlockSpec((ATTN_WIDTH, tw), lambda n: (0, cur(n))),
        pl.BlockSpec((KV_WIDTH, BLOCK), lambda n: (k_row, prev(n))),
        pl.BlockSpec((KV_WIDTH, tw), lambda n: (k_row, cur(n))),
        pl.BlockSpec((KV_WIDTH, BLOCK), lambda n: (k_row + 1, prev(n))),
        pl.BlockSpec((KV_WIDTH, tw), lambda n: (k_row + 1, cur(n))),
    ]


def _cols(b):
    return slice(b * BLOCK, (b + 1) * BLOCK)


def _band_rows(prev_ref, tile_ref, gs, b):
    before = prev_ref[gs, :] if b == 0 else tile_ref[gs, _cols(b - 1)]
    return before, tile_ref[gs, _cols(b)]


def _group_rows(ref, g, b):
    first = g * ATTN_REP
    return jnp.concatenate([ref[(first + r) * HEAD_DIM:(first + r + 1) * HEAD_DIM, _cols(b)] for r in range(ATTN_REP)],
                           axis=1)


def _ungroup_rows(ref, g, b, val):
    first = g * ATTN_REP
    for r in range(ATTN_REP):
        ref[(first + r) * HEAD_DIM:(first + r + 1) * HEAD_DIM, _cols(b)] = val[:, r * BLOCK:(r + 1) * BLOCK]


def _attn_upper():
    key = lax.broadcasted_iota(jnp.int32, (BLOCK, ATTN_REP * BLOCK), 0)
    qry = lax.broadcasted_iota(jnp.int32, (BLOCK, ATTN_REP * BLOCK), 1) & (BLOCK - 1)
    return key > qry


def _attn_probs(q, kp, kc, sink_ref, g, upper, no_prev):
    s = jnp.where(upper, _dg(kp, q, TN), _dg(kc, q, TN)) * ATTN_SCALE
    if no_prev is not None:
        s = jnp.where(upper & no_prev, MASK_VALUE, s)
    sink = jnp.concatenate([jnp.full((1, BLOCK), sink_ref[0, g * ATTN_REP + r], F32) for r in range(ATTN_REP)], axis=1)
    m = jnp.maximum(jnp.max(s, axis=0, keepdims=True), sink)
    e = jnp.exp(s - m)
    es = jnp.exp(sink - m)
    inv = 1.0 / (jnp.sum(e, axis=0, keepdims=True) + es)
    return e * inv, es * inv


def _split_band(upper, val):
    return jnp.where(upper, val, 0.0).astype(BF16), jnp.where(upper, 0.0, val).astype(BF16)


def attn_fwd(qkv_t, sinks, name):
    t = qkv_t.shape[1]
    nbs, tw, tiles = _attn_tiles(t)

    def body(sink_ref, q_ref, kp_ref, kc_ref, vp_ref, vc_ref, y_ref):
        n = pl.program_id(0)
        upper = _attn_upper()
        for b in range(nbs):
            for g in range(N_KV_HEADS):
                gs = slice(g * HEAD_DIM, (g + 1) * HEAD_DIM)
                kp, kc = _band_rows(kp_ref, kc_ref, gs, b)
                vp, vc = _band_rows(vp_ref, vc_ref, gs, b)
                p, _ = _attn_probs(_group_rows(q_ref, g, b), kp, kc, sink_ref, g, upper, n == 0 if b == 0 else None)
                pp, pc = _split_band(upper, p)
                _ungroup_rows(y_ref, g, b, (_dot(vp, pp) + _dot(vc, pc)).astype(BF16))

    return pl.pallas_call(
        body, name=name, grid=(tiles,),
        in_specs=[pl.BlockSpec(memory_space=pltpu.SMEM)] + _attn_specs(nbs, tiles),
        out_specs=pl.BlockSpec((ATTN_WIDTH, tw), lambda n: (0, n)),
        out_shape=jax.ShapeDtypeStruct((ATTN_WIDTH, t), BF16),
        compiler_params=_params(("arbitrary",)),
    )(sinks, qkv_t, qkv_t, qkv_t, qkv_t, qkv_t)


def attn_bwd(qkv_t, dy_t, sinks, name):
    t = qkv_t.shape[1]
    nbs, tw, tiles = _attn_tiles(t)
    kept = slice(0, tw - BLOCK)

    def body(sink_ref, q_ref, kp_ref, kc_ref, vp_ref, vc_ref, do_ref, dq_ref, dk_ref, dv_ref, dsink_ref,
             ck_ref, cv_ref, sacc_ref):
        n = pl.program_id(0)

        @pl.when(n == 0)
        def _():
            sacc_ref[...] = jnp.zeros_like(sacc_ref)

        @pl.when((n > 0) & (n < tiles))
        def _():
            if nbs > 1:
                dk_ref[:, kept] = ck_ref[:, kept].astype(BF16)
                dv_ref[:, kept] = cv_ref[:, kept].astype(BF16)

        @pl.when(n < tiles)
        def _():
            upper = _attn_upper()
            for b in range(nbs):
                for g in range(N_KV_HEADS):
                    gs = slice(g * HEAD_DIM, (g + 1) * HEAD_DIM)
                    kp, kc = _band_rows(kp_ref, kc_ref, gs, b)
                    vp, vc = _band_rows(vp_ref, vc_ref, gs, b)
                    q = _group_rows(q_ref, g, b)
                    do = _group_rows(do_ref, g, b)
                    p, ps = _attn_probs(q, kp, kc, sink_ref, g, upper, n == 0 if b == 0 else None)
                    dp = jnp.where(upper, _dg(vp, do, TN), _dg(vc, do, TN))
                    delta = jnp.sum(p * dp, axis=0, keepdims=True)
                    dsink = -(ps * delta)
                    for r in range(ATTN_REP):
                        hd = g * ATTN_REP + r
                        sacc_ref[hd:hd + 1, :] += dsink[:, r * BLOCK:(r + 1) * BLOCK]
                    dsp, dsc = _split_band(upper, p * (dp - delta))
                    pp, pc = _split_band(upper, p)
                    _ungroup_rows(dq_ref, g, b, (ATTN_SCALE * (_dot(kp, dsp) + _dot(kc, dsc))).astype(BF16))
                    dk_before = ATTN_SCALE * _dg(q, dsp, NT)
                    dv_before = _dg(do, pp, NT)
                    if b == 0:
                        @pl.when(n > 0)
                        def _():
                            dk_ref[gs, _cols(nbs - 1)] = (ck_ref[gs, _cols(nbs - 1)] + dk_before).astype(BF16)
                            dv_ref[gs, _cols(nbs - 1)] = (cv_ref[gs, _cols(nbs - 1)] + dv_before).astype(BF16)
                    else:
                        ck_ref[gs, _cols(b - 1)] += dk_before
                        cv_ref[gs, _cols(b - 1)] += dv_before
                    ck_ref[gs, _cols(b)] = ATTN_SCALE * _dg(q, dsc, NT)
                    cv_ref[gs, _cols(b)] = _dg(do, pc, NT)

        @pl.when(n == tiles)
        def _():
            dk_ref[...] = ck_ref[...].astype(BF16)
            dv_ref[...] = cv_ref[...].astype(BF16)
            dsink_ref[...] = jnp.sum(sacc_ref[...], axis=1, keepdims=True)

    last = tiles - 1
    done = lambda n: jnp.maximum(n - 1, 0)
    outs = pl.pallas_call(
        body, name=name, grid=(tiles + 1,),
        in_specs=[pl.BlockSpec(memory_space=pltpu.SMEM)] + _attn_specs(nbs, tiles)
        + [pl.BlockSpec((ATTN_WIDTH, tw), lambda n: (0, jnp.minimum(n, last)))],
        out_specs=[pl.BlockSpec((ATTN_WIDTH, tw), lambda n: (0, jnp.minimum(n, last))),
                   pl.BlockSpec((KV_WIDTH, tw), lambda n: (0, done(n))),
                   pl.BlockSpec((KV_WIDTH, tw), lambda n: (0, done(n))),
                   pl.BlockSpec((N_Q_HEADS, 1), lambda n: (0, 0))],
        out_shape=[jax.ShapeDtypeStruct((ATTN_WIDTH, t), BF16), jax.ShapeDtypeStruct((KV_WIDTH, t), BF16),
                   jax.ShapeDtypeStruct((KV_WIDTH, t), BF16), jax.ShapeDtypeStruct((N_Q_HEADS, 1), F32)],
        scratch_shapes=[pltpu.VMEM((KV_WIDTH, tw), F32), pltpu.VMEM((KV_WIDTH, tw), F32),
                        pltpu.VMEM((N_Q_HEADS, BLOCK), F32)],
        compiler_params=_params(("arbitrary",)),
    )(sinks, qkv_t, qkv_t, qkv_t, qkv_t, qkv_t, dy_t)
    return outs


GELU_C = 0.7978845608028654
GELU_A = 0.044715


def _gelu(x):
    return 0.5 * x * (1.0 + jnp.tanh(GELU_C * (x + GELU_A * x * x * x)))


def _gelu_grad(x):
    th = jnp.tanh(GELU_C * (x + GELU_A * x * x * x))
    return 0.5 * (1.0 + th) + 0.5 * x * (1.0 - th * th) * GELU_C * (1.0 + 3.0 * GELU_A * x * x)


def _gmlp_parts(zg, lng, lnb):
    z = _gelu(zg)
    u = z[:GMLP_WIDTH, :]
    vv = z[GMLP_WIDTH:, :]
    mu = jnp.mean(vv, axis=0, keepdims=True)
    xc = vv - mu
    rstd = lax.rsqrt(jnp.mean(xc * xc, axis=0, keepdims=True) + EPS)
    xhat = xc * rstd
    return u, xhat, rstd, xhat * lng + lnb


def _causal(w):
    row = lax.broadcasted_iota(jnp.int32, (BLOCK, BLOCK), 0)
    col = lax.broadcasted_iota(jnp.int32, (BLOCK, BLOCK), 1)
    return jnp.where(col <= row, w, 0.0)


GMLP_CHUNKS_PER_STEP = 4


def _stack_chunks(v, nc):
    return jnp.concatenate([v[:, c * BLOCK:(c + 1) * BLOCK] for c in range(nc)], axis=0)


def _unstack_chunks(v, nc):
    rows = v.shape[0] // nc
    return jnp.concatenate([v[c * rows:(c + 1) * rows, :] for c in range(nc)], axis=1)


def gmlp_fwd(zg_t, lng, lnb, w_s, b_s, name):
    t = zg_t.shape[1]
    nc = min(GMLP_CHUNKS_PER_STEP, t // BLOCK)
    tw = nc * BLOCK

    def body(zg_ref, lng_ref, lnb_ref, w_ref, bs_ref, y_ref):
        u, _, _, vn = _gmlp_parts(zg_ref[...].astype(F32), lng_ref[...], lnb_ref[...])
        for g in range(GMLP_GROUPS):
            gs = slice(g * GMLP_GROUP_DIM, (g + 1) * GMLP_GROUP_DIM)
            wc = _causal(w_ref[g]).astype(BF16)
            mixed = _dg(_stack_chunks(vn[gs, :].astype(BF16), nc), wc, NT) + bs_ref[g:g + 1, :]
            y_ref[gs, :] = (u[gs, :] * _unstack_chunks(mixed, nc)).astype(BF16)

    return pl.pallas_call(
        body, name=name, grid=(t // tw,),
        in_specs=[pl.BlockSpec((2 * GMLP_WIDTH, tw), lambda n: (0, n)), _resident((GMLP_WIDTH, 1)),
                  _resident((GMLP_WIDTH, 1)), _resident((GMLP_GROUPS, BLOCK, BLOCK)), _resident((GMLP_GROUPS, BLOCK))],
        out_specs=pl.BlockSpec((GMLP_WIDTH, tw), lambda n: (0, n)),
        out_shape=jax.ShapeDtypeStruct((GMLP_WIDTH, t), BF16),
        compiler_params=_params(("arbitrary",)),
    )(zg_t, lng, lnb, w_s, b_s)


def gmlp_bwd(zg_t, dy_t, lng, lnb, w_s, b_s, name):
    t = zg_t.shape[1]
    nc = min(GMLP_CHUNKS_PER_STEP, t // BLOCK)
    tw = nc * BLOCK
    nt = t // tw

    def body(zg_ref, dy_ref, lng_ref, lnb_ref, w_ref, bs_ref, dzg_ref, dw_ref, dbs_ref, dlng_ref, dlnb_ref,
             gacc_ref, bacc_ref):
        n = pl.program_id(0)

        @pl.when(n == 0)
        def _():
            dw_ref[...] = jnp.zeros_like(dw_ref)
            dbs_ref[...] = jnp.zeros_like(dbs_ref)
            gacc_ref[...] = jnp.zeros_like(gacc_ref)
            bacc_ref[...] = jnp.zeros_like(bacc_ref)

        zg = zg_ref[...].astype(F32)
        lng_v = lng_ref[...]
        u, xhat, rstd, vn = _gmlp_parts(zg, lng_v, lnb_ref[...])
        dy = dy_ref[...].astype(F32)
        dvn_parts = []
        for g in range(GMLP_GROUPS):
            gs = slice(g * GMLP_GROUP_DIM, (g + 1) * GMLP_GROUP_DIM)
            wc = _causal(w_ref[g]).astype(BF16)
            vn_s = _stack_chunks(vn[gs, :].astype(BF16), nc)
            mixed = _unstack_chunks(_dg(vn_s, wc, NT) + bs_ref[g:g + 1, :], nc)
            dy_g = dy[gs, :]
            dmixed = dy_g * u[gs, :]
            dm_s = _stack_chunks(dmixed.astype(BF16), nc)
            dzg_ref[gs, :] = (dy_g * mixed * _gelu_grad(zg[gs, :])).astype(BF16)
            dw_ref[g] += _causal(_dg(dm_s, vn_s, TN))
            dbs_ref[g:g + 1, :] += _lane_fold(jnp.sum(dmixed, axis=0, keepdims=True))
            dvn_parts.append(_unstack_chunks(_dot(dm_s, wc), nc))
        dvn = jnp.concatenate(dvn_parts, axis=0)
        gacc_ref[...] += _lane_fold(dvn * xhat)
        bacc_ref[...] += _lane_fold(dvn)
        dxhat = dvn * lng_v
        m1 = jnp.mean(dxhat, axis=0, keepdims=True)
        m2 = jnp.mean(dxhat * xhat, axis=0, keepdims=True)
        dvv = rstd * (dxhat - m1 - xhat * m2)
        dzg_ref[GMLP_WIDTH:, :] = (dvv * _gelu_grad(zg[GMLP_WIDTH:, :])).astype(BF16)

        @pl.when(n == nt - 1)
        def _():
            dlng_ref[...] = jnp.sum(gacc_ref[...], axis=1, keepdims=True)
            dlnb_ref[...] = jnp.sum(bacc_ref[...], axis=1, keepdims=True)

    const2 = lambda n: (0, 0)
    return pl.pallas_call(
        body, name=name, grid=(nt,),
        in_specs=[pl.BlockSpec((2 * GMLP_WIDTH, tw), lambda n: (0, n)), pl.BlockSpec((GMLP_WIDTH, tw), lambda n: (0, n)),
                  _resident((GMLP_WIDTH, 1)), _resident((GMLP_WIDTH, 1)),
                  _resident((GMLP_GROUPS, BLOCK, BLOCK)), _resident((GMLP_GROUPS, BLOCK))],
        out_specs=[pl.BlockSpec((2 * GMLP_WIDTH, tw), lambda n: (0, n)),
                   pl.BlockSpec((GMLP_GROUPS, BLOCK, BLOCK), lambda n: (0, 0, 0)),
                   pl.BlockSpec((GMLP_GROUPS, BLOCK), const2),
                   pl.BlockSpec((GMLP_WIDTH, 1), const2), pl.BlockSpec((GMLP_WIDTH, 1), const2)],
        out_shape=[jax.ShapeDtypeStruct((2 * GMLP_WIDTH, t), BF16),
                   jax.ShapeDtypeStruct((GMLP_GROUPS, BLOCK, BLOCK), F32),
                   jax.ShapeDtypeStruct((GMLP_GROUPS, BLOCK), F32),
                   jax.ShapeDtypeStruct((GMLP_WIDTH, 1), F32), jax.ShapeDtypeStruct((GMLP_WIDTH, 1), F32)],
        scratch_shapes=[pltpu.VMEM((GMLP_WIDTH, BLOCK), F32), pltpu.VMEM((GMLP_WIDTH, BLOCK), F32)],
        compiler_params=_params(("arbitrary",)),
    )(zg_t, dy_t, lng, lnb, w_s, b_s)


def _rmsnorm_cols(y, gain_col):
    r = lax.rsqrt(jnp.mean(y * y, axis=0, keepdims=True) + EPS)
    n = y * r
    return n, r, n * gain_col


def mix_out_fwd(x, ya_t, yg_t, gao, ggo, wout, b_out, name):
    t, d = x.shape
    tm = _token_tile(t)

    def body(x_ref, ya_ref, yg_ref, gao_ref, ggo_ref, w_ref, b_ref, o_ref):
        _, _, ya = _rmsnorm_cols(ya_ref[...].astype(F32), gao_ref[...])
        _, _, yg = _rmsnorm_cols(yg_ref[...].astype(F32), ggo_ref[...])
        o_ref[...] = (x_ref[...] + _dg(ya.astype(BF16), w_ref[:ATTN_WIDTH, :], TN)
                      + _dg(yg.astype(BF16), w_ref[ATTN_WIDTH:, :], TN) + b_ref[...])

    return pl.pallas_call(
        body, name=name, grid=(t // tm,),
        in_specs=[pl.BlockSpec((tm, d), lambda i: (i, 0)), pl.BlockSpec((ATTN_WIDTH, tm), lambda i: (0, i)),
                  pl.BlockSpec((GMLP_WIDTH, tm), lambda i: (0, i)), _resident((ATTN_WIDTH, 1)), _resident((GMLP_WIDTH, 1)),
                  _resident((ATTN_WIDTH + GMLP_WIDTH, d)), _resident((1, d))],
        out_specs=pl.BlockSpec((tm, d), lambda i: (i, 0)),
        out_shape=jax.ShapeDtypeStruct((t, d), F32),
        compiler_params=_params(("arbitrary",)),
    )(x, ya_t, yg_t, gao, ggo, wout, b_out)


def _lane_fold(v):
    out = v[:, :LANES]
    for j in range(1, v.shape[1] // LANES):
        out = out + v[:, j * LANES:(j + 1) * LANES]
    return out


def mix_out_bwd(dx, ya_t, yg_t, gao, ggo, wout, name, deps=()):
    t, d = dx.shape
    tm = _token_tile(t)
    nt = t // tm

    def body(dx_ref, ya_ref, yg_ref, gao_ref, ggo_ref, w_ref, dya_ref, dyg_ref, y_ref, dgao_ref, dggo_ref, db_ref,
             acc_ref):
        i = pl.program_id(0)

        @pl.when(i == 0)
        def _():
            acc_ref[...] = jnp.zeros_like(acc_ref)
            db_ref[...] = jnp.zeros_like(db_ref)

        dxv = dx_ref[...]
        db_ref[...] += jnp.sum(dxv, axis=0, keepdims=True)
        dxb = dxv.astype(BF16)
        for part, (src_ref, gain_ref, dst_ref) in enumerate(((ya_ref, gao_ref, dya_ref), (yg_ref, ggo_ref, dyg_ref))):
            rows = slice(part * ATTN_WIDTH, (part + 1) * ATTN_WIDTH)
            gain = gain_ref[...]
            nrm, r, yn = _rmsnorm_cols(src_ref[...].astype(F32), gain)
            y_ref[rows, :] = yn.astype(BF16)
            dy = _dg(w_ref[rows, :], dxb, NT)
            acc_ref[rows, :] += _lane_fold(dy * nrm)
            dn = dy * gain
            dst_ref[...] = (r * (dn - nrm * jnp.mean(dn * nrm, axis=0, keepdims=True))).astype(BF16)

        @pl.when(i == nt - 1)
        def _():
            dgao_ref[...] = jnp.sum(acc_ref[:ATTN_WIDTH, :], axis=1, keepdims=True)
            dggo_ref[...] = jnp.sum(acc_ref[ATTN_WIDTH:, :], axis=1, keepdims=True)

    const2 = lambda i: (0, 0)
    feat = lambda w: pl.BlockSpec((w, tm), lambda i: (0, i))
    return pl.pallas_call(
        _drop_deps(body, 6, len(deps)), name=name, grid=(nt,),
        in_specs=[pl.BlockSpec((tm, d), lambda i: (i, 0)), feat(ATTN_WIDTH), feat(GMLP_WIDTH),
                  _resident((ATTN_WIDTH, 1)), _resident((GMLP_WIDTH, 1)), _resident((ATTN_WIDTH + GMLP_WIDTH, d))]
        + [ANY_SPEC] * len(deps),
        out_specs=[feat(ATTN_WIDTH), feat(GMLP_WIDTH), feat(ATTN_WIDTH + GMLP_WIDTH),
                   pl.BlockSpec((ATTN_WIDTH, 1), const2), pl.BlockSpec((GMLP_WIDTH, 1), const2),
                   pl.BlockSpec((1, d), const2)],
        out_shape=[jax.ShapeDtypeStruct((ATTN_WIDTH, t), BF16), jax.ShapeDtypeStruct((GMLP_WIDTH, t), BF16),
                   jax.ShapeDtypeStruct((ATTN_WIDTH + GMLP_WIDTH, t), BF16),
                   jax.ShapeDtypeStruct((ATTN_WIDTH, 1), F32), jax.ShapeDtypeStruct((GMLP_WIDTH, 1), F32),
                   jax.ShapeDtypeStruct((1, d), F32)],
        scratch_shapes=[pltpu.VMEM((ATTN_WIDTH + GMLP_WIDTH, LANES), F32)],
        compiler_params=_params(("arbitrary",)),
    )(dx, ya_t, yg_t, gao, ggo, wout, *deps)


def mix_in_bwd(x, dxo, gain, dq_t, dk_t, dv_t, dzg_t, win_t, name):
    t, d = x.shape
    w = win_t.shape[0]
    tm = _token_tile(t)
    nt = t // tm
    parts = ((0, ATTN_WIDTH), (ATTN_WIDTH, KV_WIDTH), (ATTN_WIDTH + KV_WIDTH, KV_WIDTH), (QKV_WIDTH, w - QKV_WIDTH))

    def body(xp_ref, dxp_ref, g_ref, dq_ref, dk_ref, dv_ref, dzg_ref, w_ref, dx_ref, dg_ref, db_ref, acc_ref, dh_ref):
        i = pl.program_id(0)
        gain_v = g_ref[...]

        @pl.when(i == 0)
        def _():
            acc_ref[...] = jnp.zeros_like(acc_ref)
            dg_ref[...] = jnp.zeros_like(dg_ref)
            dh_ref[...] = jnp.zeros_like(dh_ref)

        def finish_previous(slot):
            xv = xp_ref[...]
            r = lax.rsqrt(jnp.mean(xv * xv, axis=-1, keepdims=True) + EPS)
            dx, dgain = _rmsnorm_bwd_rows(xv, r, gain_v, dh_ref[slot])
            dx_ref[...] = dxp_ref[...] + dx
            dg_ref[...] += dgain

        @pl.when(i < nt)
        def _():
            dh = jnp.zeros((tm, d), F32)
            for (off, size), src_ref in zip(parts, (dq_ref, dk_ref, dv_ref, dzg_ref)):
                dp = src_ref[...]
                acc_ref[off:off + size, :] += _lane_fold(dp.astype(F32))
                dh = dh + _dg(dp, w_ref[off:off + size, :], TN)
            finish_previous((i + 1) % 2)
            dh_ref[i % 2] = dh

        @pl.when(i == nt)
        def _():
            finish_previous((nt - 1) % 2)
            db_ref[...] = jnp.sum(acc_ref[...], axis=1, keepdims=True)

    const2 = lambda i: (0, 0)
    prev = lambda: pl.BlockSpec((tm, d), lambda i: (jnp.maximum(i - 1, 0), 0))
    feat = lambda rows: pl.BlockSpec((rows, tm), lambda i: (0, jnp.minimum(i, nt - 1)))
    return pl.pallas_call(
        body, name=name, grid=(nt + 1,),
        in_specs=[prev(), prev(), _resident((1, d)), feat(ATTN_WIDTH), feat(KV_WIDTH), feat(KV_WIDTH),
                  feat(w - QKV_WIDTH), _resident((w, d))],
        out_specs=[prev(), pl.BlockSpec((1, d), const2), pl.BlockSpec((w, 1), const2)],
        out_shape=[jax.ShapeDtypeStruct((t, d), F32), jax.ShapeDtypeStruct((1, d), F32),
                   jax.ShapeDtypeStruct((w, 1), F32)],
        scratch_shapes=[pltpu.VMEM((w, LANES), F32), pltpu.VMEM((2, tm, d), F32)],
        compiler_params=_params(("arbitrary",)),
    )(x, dxo, gain, dq_t, dk_t, dv_t, dzg_t, win_t)


def _adamw_math(w, g, m, v):
    m = ADAM_B1 * m + (1.0 - ADAM_B1) * g
    v = ADAM_B2 * v + (1.0 - ADAM_B2) * (g * g)
    m_hat = m / (1.0 - ADAM_B1 ** ADAM_STEP)
    v_hat = v / (1.0 - ADAM_B2 ** ADAM_STEP)
    delta = -ADAM_LR * (m_hat / (jnp.sqrt(v_hat) + ADAM_EPS) + ADAM_WD * w)
    return delta, m, v


def adamw(w, g, m, v, name):
    rows, cols = w.shape
    tr = rows
    if rows * cols > 256 * 1024:
        for cand in (256, 128, 64, 32, 16, 8):
            if rows % cand == 0:
                tr = cand
                break

    def body(w_ref, g_ref, m_ref, v_ref, d_ref, mo_ref, vo_ref):
        delta, mn, vn = _adamw_math(w_ref[...], g_ref[...], m_ref[...], v_ref[...])
        d_ref[...] = delta
        mo_ref[...] = mn
        vo_ref[...] = vn

    spec = pl.BlockSpec((tr, cols), lambda i: (i, 0))
    shape = jax.ShapeDtypeStruct((rows, cols), F32)
    return pl.pallas_call(
        body, name=name, grid=(rows // tr,),
        in_specs=[spec] * 4, out_specs=[spec] * 3, out_shape=[shape] * 3,
        compiler_params=_params(("arbitrary",)),
    )(w, g, m, v)


def sum_adamw(landing, grad3d, me, w, m, v, name):
    nd, rows, cols = landing.shape
    tr = rows // 2

    def body(me_ref, l_ref, own_ref, w_ref, m_ref, v_ref, g_ref, d_ref, mo_ref, vo_ref):
        g = own_ref[...].astype(F32)
        for j in range(nd):
            g = g + l_ref[j].astype(F32)
        delta, mn, vn = _adamw_math(w_ref[...], g, m_ref[...], v_ref[...])
        g_ref[...] = g
        d_ref[...] = delta
        mo_ref[...] = mn
        vo_ref[...] = vn

    spec = lambda: pl.BlockSpec((tr, cols), lambda i, me_ref: (i, 0))
    shape = jax.ShapeDtypeStruct((rows, cols), F32)
    return pl.pallas_call(
        body, name=name,
        grid_spec=pltpu.PrefetchScalarGridSpec(
            num_scalar_prefetch=1, grid=(rows // tr,),
            in_specs=[pl.BlockSpec((nd, tr, cols), lambda i, me_ref: (0, i, 0)),
                      pl.BlockSpec((None, tr, cols), lambda i, me_ref: (me_ref[0], i, 0)), spec(), spec(), spec()],
            out_specs=[spec() for _ in range(4)]),
        out_shape=[shape] * 4,
        compiler_params=_params(("arbitrary",)),
    )(me, landing, grad3d, w, m, v)


def _mesh_place():
    x, y, c = lax.axis_index("x"), lax.axis_index("y"), lax.axis_index("c")
    return x, y, c, 4 * x + 2 * y + c


def _peer(x, y, c, k):
    return (x ^ ((k >> 2) & 1), y ^ ((k >> 1) & 1), c ^ (k & 1))


def prep_shards(mats, me, name):
    na = len(mats)

    def body(me_ref, *refs):
        in_refs, shard_refs, land_refs = refs[:na], refs[na:2 * na], refs[2 * na:]
        for a in range(na):
            val = in_refs[a][...].astype(BF16)
            shard_refs[a][...] = val
            land_refs[a][...] = val

    whole = lambda mat: pl.BlockSpec(mat.shape, lambda i, me_ref: (0, 0))
    outs = pl.pallas_call(
        body, name=name,
        grid_spec=pltpu.PrefetchScalarGridSpec(
            num_scalar_prefetch=1, grid=(1,),
            in_specs=[whole(mat) for mat in mats],
            out_specs=[whole(mat) for mat in mats]
            + [pl.BlockSpec((None,) + mat.shape, lambda i, me_ref: (me_ref[0], 0, 0)) for mat in mats]),
        out_shape=[jax.ShapeDtypeStruct(mat.shape, BF16) for mat in mats]
        + [jax.ShapeDtypeStruct((N_DEV,) + mat.shape, BF16) for mat in mats],
        compiler_params=_params(("arbitrary",)),
    )(me, *mats)
    return outs[:na], outs[na:]


def allgather_rows(shards, lands, name):
    na = n_gather = len(shards)

    def body(*refs):
        in_refs, out_refs = refs[:na], refs[2 * na:3 * na]
        send_sems, recv_sems = refs[3 * na:]
        x, y, c, me = _mesh_place()

        def copy(sem, a, origin, src, k):
            return pltpu.make_async_remote_copy(
                src_ref=src, dst_ref=out_refs[a].at[origin], send_sem=send_sems.at[sem, a],
                recv_sem=recv_sems.at[sem, a], device_id=_peer(x, y, c, k), device_id_type=pl.DeviceIdType.MESH)

        kx, ky = 4, 2

        def recv(sem, a, origin):
            copy(sem, a, origin, in_refs[a], 1).wait_recv()

        def landed(a, origin):
            return out_refs[a].at[origin]

        @pl.when(c == 1)
        def _():
            sends = []

            def send(sem, a, origin, src, k):
                cp = copy(sem, a, origin, src, k)
                cp.start()
                sends.append(cp)

            for a in range(n_gather):
                send(1, a, me, in_refs[a], kx)
                send(2, a, me, in_refs[a], ky)
                send(0, a, me, in_refs[a], 1)
            for a in range(n_gather):
                recv(0, a, me ^ 1)
                send(3, a, me ^ 1, landed(a, me ^ 1), kx)
                send(4, a, me ^ 1, landed(a, me ^ 1), ky)
            for a in range(n_gather):
                recv(1, a, me ^ kx)
                send(5, a, me ^ kx, landed(a, me ^ kx), ky)
                send(7, a, me ^ kx, landed(a, me ^ kx), 1)
                recv(2, a, me ^ ky)
                send(8, a, me ^ ky, landed(a, me ^ ky), 1)
            for a in range(n_gather):
                recv(3, a, me ^ kx ^ 1)
                send(9, a, me ^ kx ^ 1, landed(a, me ^ kx ^ 1), 1)
                recv(4, a, me ^ ky ^ 1)
                send(6, a, me ^ ky ^ 1, landed(a, me ^ ky ^ 1), kx)
                send(10, a, me ^ ky ^ 1, landed(a, me ^ ky ^ 1), 1)
            for a in range(n_gather):
                recv(5, a, me ^ 6)
                send(11, a, me ^ 6, landed(a, me ^ 6), 1)
                recv(6, a, me ^ 7)
                send(12, a, me ^ 7, landed(a, me ^ 7), 1)
            for cp in sends:
                cp.wait_send()

        @pl.when(c == 0)
        def _():
            north = me ^ 1
            own = [copy(0, a, me, in_refs[a], 1) for a in range(n_gather)]
            for cp in own:
                cp.start()
            for a in range(n_gather):
                recv(0, a, north)
                for sem, origin in ((7, north ^ kx), (8, north ^ ky), (9, north ^ kx ^ 1), (10, north ^ ky ^ 1),
                                    (11, north ^ 6), (12, north ^ 7)):
                    recv(sem, a, origin)
            for cp in own:
                cp.wait_send()

    return pl.pallas_call(
        body, name=name,
        in_specs=[ANY_SPEC] * (2 * na), out_specs=[ANY_SPEC] * na,
        out_shape=[jax.ShapeDtypeStruct(land.shape, land.dtype) for land in lands],
        input_output_aliases={na + a: a for a in range(na)},
        scratch_shapes=[pltpu.SemaphoreType.DMA((N_GATHER_ROLES, n_gather)),
                        pltpu.SemaphoreType.DMA((N_GATHER_ROLES, n_gather))],
    )(*shards, *lands)


HBM_SPEC = pl.BlockSpec(memory_space=pltpu.HBM)
SEM_SPEC = pl.BlockSpec(memory_space=pltpu.SEMAPHORE)
SIDE_EFFECT = pltpu.SideEffectType.DATAFLOW_SIDE_EFFECTING


def _hbm(v):
    return pltpu.with_memory_space_constraint(v, pltpu.HBM)


def gather_slices(src, land, me, k):
    return src, land.at[me], land.at[me ^ k]


def exchange_slices(src, land, me, k):
    return src.at[me ^ k], land.at[k - 1], land.at[k - 1]


def split_start(groups, slices, deps, name):
    sizes = [len(srcs) for srcs, _ in groups]
    flat_src = [s for srcs, _ in groups for s in srcs]
    flat_land = [l for _, lands in groups for l in lands]
    na, ng, nd = len(flat_src), len(groups), len(deps)

    def body(*refs):
        src_refs, land_refs = refs[:na], refs[na:2 * na]
        sem_refs = refs[2 * na + nd:2 * na + nd + 2 * ng]
        token_ref = refs[-1]
        x, y, c, me = _mesh_place()
        a0 = 0
        for gi, size in enumerate(sizes):
            send_sems, recv_sems = sem_refs[2 * gi], sem_refs[2 * gi + 1]
            for k in range(1, N_DEV):
                for j in range(size):
                    src, dst, _ = slices(src_refs[a0 + j], land_refs[a0 + j], me, k)
                    pltpu.make_async_remote_copy(
                        src_ref=src, dst_ref=dst, send_sem=send_sems.at[(k - 1) * size + j],
                        recv_sem=recv_sems.at[(k - 1) * size + j],
                        device_id=_peer(x, y, c, k), device_id_type=pl.DeviceIdType.MESH).start()
            a0 += size
        token_ref[...] = jnp.zeros_like(token_ref)

    sems = [pltpu.SemaphoreType.DMA(((N_DEV - 1) * size,)) for size in sizes for _ in range(2)]
    thru = [pltpu.HBM(v.shape, v.dtype) for v in flat_src + flat_land]
    outs = pl.pallas_call(
        body, name=name,
        in_specs=[HBM_SPEC] * (2 * na) + [ANY_SPEC] * nd,
        out_specs=[SEM_SPEC] * (2 * ng) + [HBM_SPEC] * (2 * na) + [pl.BlockSpec(memory_space=pltpu.VMEM)],
        out_shape=sems + thru + [jax.ShapeDtypeStruct((8, LANES), F32)],
        input_output_aliases={i: 2 * ng + i for i in range(2 * na)},
        compiler_params=pltpu.CompilerParams(has_side_effects=SIDE_EFFECT),
    )(*[_hbm(v) for v in flat_src + flat_land], *deps)
    started, a0 = [], 0
    for gi, size in enumerate(sizes):
        srcs = outs[2 * ng + a0:2 * ng + a0 + size]
        lands = outs[2 * ng + na + a0:2 * ng + na + a0 + size]
        started.append((outs[2 * gi], outs[2 * gi + 1], list(srcs), list(lands)))
        a0 += size
    return started, outs[-1]


def split_wait(started, slices, after, name):
    send_sems, recv_sems, srcs, lands = started
    na = len(srcs)

    def body(*refs):
        src_refs, land_refs = refs[:na], refs[na:2 * na]
        send_ref, recv_ref = refs[2 * na], refs[2 * na + 1]
        x, y, c, me = _mesh_place()
        for k in range(1, N_DEV):
            for j in range(na):
                src, _, got = slices(src_refs[j], land_refs[j], me, k)
                cp = pltpu.make_async_remote_copy(
                    src_ref=src, dst_ref=got, send_sem=send_ref.at[(k - 1) * na + j], recv_sem=recv_ref.at[(k - 1) * na + j],
                    device_id=_peer(x, y, c, k), device_id_type=pl.DeviceIdType.MESH)
                cp.wait_send()
                cp.wait_recv()

    outs = pl.pallas_call(
        body, name=name,
        in_specs=[HBM_SPEC] * (2 * na) + [SEM_SPEC, SEM_SPEC] + [ANY_SPEC] * len(after),
        out_specs=[HBM_SPEC] * (2 * na),
        out_shape=[pltpu.HBM(v.shape, v.dtype) for v in srcs + lands],
        input_output_aliases={i: i for i in range(2 * na)},
        compiler_params=pltpu.CompilerParams(has_side_effects=SIDE_EFFECT),
    )(*srcs, *lands, send_sems, recv_sems, *after)
    return list(outs[:na]), list(outs[na:])


def place_own(pack, me, name):
    rows, cols = pack.shape

    def body(me_ref, p_ref, o_ref):
        o_ref[...] = p_ref[...]

    return pl.pallas_call(
        body, name=name,
        grid_spec=pltpu.PrefetchScalarGridSpec(
            num_scalar_prefetch=1, grid=(1,),
            in_specs=[pl.BlockSpec((rows, cols), lambda i, me_ref: (0, 0))],
            out_specs=pl.BlockSpec((None, rows, cols), lambda i, me_ref: (me_ref[0], 0, 0))),
        out_shape=jax.ShapeDtypeStruct((N_DEV, rows, cols), F32),
        compiler_params=_params(("arbitrary",)),
    )(me, pack)


def sum_slots(slots, name):
    nd, rows, cols = slots.shape

    def body(s_ref, o_ref):
        acc = s_ref[0]
        for j in range(1, nd):
            acc = acc + s_ref[j]
        o_ref[...] = acc

    return pl.pallas_call(
        body, name=name, grid=(1,),
        in_specs=[pl.BlockSpec((nd, rows, cols), lambda i: (0, 0, 0))],
        out_specs=pl.BlockSpec((rows, cols), lambda i: (0, 0)),
        out_shape=jax.ShapeDtypeStruct((rows, cols), F32),
        compiler_params=_params(("arbitrary",)),
    )(slots)


def _pack_rows(size):
    rows = -(-size // LANES)
    return -(-rows // 8) * 8


def pack_small(parts):
    out = []
    for p in parts:
        flat = p.reshape(-1).astype(F32)
        rows = _pack_rows(flat.shape[0])
        out.append(jnp.pad(flat, (0, rows * LANES - flat.shape[0])).reshape(rows, LANES))
    return jnp.concatenate(out, axis=0)


def unpack_small(pack, shapes):
    out, off = [], 0
    for shp in shapes:
        size = 1
        for s in shp:
            size *= s
        rows = _pack_rows(size)
        out.append(pack[off:off + rows].reshape(-1)[:size].reshape(shp))
        off += rows
    return out


def local_step(x, target, small, get_w, put_g, put_small):
    col = lambda v: v.reshape(-1, 1)
    (wg1, wu1), deps = get_w(("wg1", "wu1"), ())
    sil1, gp1, s1, h1 = ffn_fwd(x, small["ffn1_norm_g"], wg1, wu1, None, "ffn1_fwd", deps)
    (wd1, win, wout), _ = get_w(("wd1", "win", "wout"), (s1,))
    qkv_t, zg_t, x1, h2 = mix_in_fwd(x, small["mix_norm_g"], win, col(small["b_in"]), "mix_in_fwd", ffn_tail=(s1, wd1))
    ya_t = attn_fwd(qkv_t, small["attn_sinks"], "attn_fwd")
    lng, lnb = col(small["gmlp_ln_g"]), col(small["gmlp_ln_b"])
    w_s, b_s = small["gmlp_w_s"][0], small["gmlp_b_s"][0]
    yg_t = gmlp_fwd(zg_t, lng, lnb, w_s, b_s, "gmlp_fwd")
    gao, ggo = col(small["attn_out_norm_g"]), col(small["gmlp_out_norm_g"])
    x2 = mix_out_fwd(x1, ya_t, yg_t, gao, ggo, wout, small["b_out"], "mix_out_fwd")
    (wg2, wu2, wd2), _ = get_w(("wg2", "wu2", "wd2"), (x2,))
    gs = {}
    dx3, sil2, gp2, s2, h2f, gs["final_norm_g"], loss = ffn_fwd(
        x2, small["ffn2_norm_g"], wg2, wu2, wd2, "ffn2_fwd", head=(target, small["final_norm_g"].reshape(1, -1)))

    dx2, da, db, df, gs["ffn2_norm_g"] = ffn_bwd(x2, dx3, small["ffn2_norm_g"], sil2, gp2, wg2, wu2, wd2, "ffn2_bwd")
    deps = put_g({"wg2": grad_tn(da, h2f, 1.0, "ffn2_dwg"), "wu2": grad_tn(db, h2f, 1.0, "ffn2_dwu"),
                  "wd2": grad_tn(s2, df, 1.0, "ffn2_dwd")})

    dya_t, dyg_t, y_t, dgao, dggo, gs["b_out"] = mix_out_bwd(dx2, ya_t, yg_t, gao, ggo, wout, "mix_out_bwd", deps)
    gs["attn_out_norm_g"], gs["gmlp_out_norm_g"] = dgao.reshape(1, -1), dggo.reshape(1, -1)
    g_wout = grad_nn([y_t], dx2, "dwout")
    dzg_t, dws, dbs, dlng, dlnb = gmlp_bwd(zg_t, dyg_t, lng, lnb, w_s, b_s, "gmlp_bwd")
    gs["gmlp_w_s"], gs["gmlp_b_s"] = dws[None], dbs[None]
    gs["gmlp_ln_g"], gs["gmlp_ln_b"] = dlng.reshape(1, -1), dlnb.reshape(1, -1)
    dq_t, dk_t, dv_t, dsinks = attn_bwd(qkv_t, dya_t, small["attn_sinks"], "attn_bwd")
    gs["attn_sinks"] = dsinks.reshape(1, -1)
    dx1, gs["mix_norm_g"], dbin = mix_in_bwd(x1, dx2, small["mix_norm_g"], dq_t, dk_t, dv_t, dzg_t, win,
                                                 "mix_in_bwd")
    gs["b_in"] = dbin.reshape(1, -1)
    deps = put_g({"wout": g_wout, "win": grad_nn([dq_t, dk_t, dv_t, dzg_t], h2, "dwin")})

    dx0, da, db, df, gs["ffn1_norm_g"] = ffn_bwd(x, dx1, small["ffn1_norm_g"], sil1, gp1, wg1, wu1, wd1, "ffn1_bwd",
                                             deps)
    deps = put_small(gs, loss)
    deps = put_g({"wg1": grad_tn(da, h1, 1.0, "ffn1_dwg", deps)})
    deps = put_g({"wu1": grad_tn(db, h1, 1.0, "ffn1_dwu", deps)})
    put_g({"wd1": grad_tn(s1, df, 1.0, "ffn1_dwd", deps)})
    return dx0


SMALL_NAMES = ["ffn1_norm_g", "mix_norm_g", "b_in", "attn_sinks", "gmlp_ln_g", "gmlp_ln_b", "gmlp_w_s", "gmlp_b_s",
               "attn_out_norm_g", "gmlp_out_norm_g", "b_out", "ffn2_norm_g", "final_norm_g"]
BIG_NAMES = {"wg1": ("ffn1_w_gate", True), "wu1": ("ffn1_w_up", True), "wd1": ("ffn1_w_down", False),
             "win": ("w_in", True), "wout": ("w_out", False),
             "wg2": ("ffn2_w_gate", True), "wu2": ("ffn2_w_up", True), "wd2": ("ffn2_w_down", False)}
WEIGHT_ORDER = ["ffn1_norm_g", "ffn1_w_gate", "ffn1_w_up", "ffn1_w_down", "mix_norm_g", "w_in", "b_in", "attn_sinks",
                "gmlp_ln_g", "gmlp_ln_b", "gmlp_w_s", "gmlp_b_s", "attn_out_norm_g", "gmlp_out_norm_g", "w_out",
                "b_out", "ffn2_norm_g", "ffn2_w_gate", "ffn2_w_up", "ffn2_w_down", "final_norm_g"]


def kernel(x, ffn1_norm_g, ffn1_w_gate, ffn1_w_up, ffn1_w_down, mix_norm_g, w_in, b_in, attn_sinks, gmlp_ln_g, gmlp_ln_b, gmlp_w_s, gmlp_b_s, attn_out_norm_g, gmlp_out_norm_g, w_out, b_out, ffn2_norm_g, ffn2_w_gate, ffn2_w_up, ffn2_w_down, final_norm_g, loss_target, m_ffn1_norm_g, m_ffn1_w_gate, m_ffn1_w_up, m_ffn1_w_down, m_mix_norm_g, m_w_in, m_b_in, m_attn_sinks, m_gmlp_ln_g, m_gmlp_ln_b, m_gmlp_w_s, m_gmlp_b_s, m_attn_out_norm_g, m_gmlp_out_norm_g, m_w_out, m_b_out, m_ffn2_norm_g, m_ffn2_w_gate, m_ffn2_w_up, m_ffn2_w_down, m_final_norm_g, v_ffn1_norm_g, v_ffn1_w_gate, v_ffn1_w_up, v_ffn1_w_down, v_mix_norm_g, v_w_in, v_b_in, v_attn_sinks, v_gmlp_ln_g, v_gmlp_ln_b, v_gmlp_w_s, v_gmlp_b_s, v_attn_out_norm_g, v_gmlp_out_norm_g, v_w_out, v_b_out, v_ffn2_norm_g, v_ffn2_w_gate, v_ffn2_w_up, v_ffn2_w_down, v_final_norm_g):
    args = dict(locals())
    w = {n: args[n] for n in WEIGHT_ORDER}
    m = {n: args["m_" + n] for n in WEIGHT_ORDER}
    v = {n: args["v_" + n] for n in WEIGHT_ORDER}

    d_model = x.shape[-1]
    flat = lambda g: g.reshape(-1, d_model)
    me = _mesh_place()[3].astype(jnp.int32).reshape(1)
    first = ("wg1", "wu1")
    later = (("wd1", "win", "wout"), ("wg2", "wu2", "wd2"))
    gathers, exchanges, last_token = {}, [], []

    order = list(first) + [k for grp in later for k in grp]
    mats = [w[BIG_NAMES[k][0]][0].T if BIG_NAMES[k][1] else w[BIG_NAMES[k][0]][0] for k in order]
    shard_list, land_list = prep_shards(mats, me, "prep_shards")
    shards, zones = dict(zip(order, shard_list)), dict(zip(order, land_list))

    def get_w(keys, after):
        if keys == first:
            full = allgather_rows([shards[k] for k in first], [zones[k] for k in first], "allgather_ffn1")
            started, token = split_start([([shards[k] for k in grp], [zones[k] for k in grp]) for grp in later],
                                         gather_slices, (full[0],), "gather_start")
            gathers.update(zip(later, started))
            return [flat(g) for g in full], (token,)
        _, lands = split_wait(gathers[keys], gather_slices, after, "gather_wait_" + keys[0])
        return [flat(land) for land in lands], ()

    def put_g(gb):
        keys = tuple(gb)
        g3 = [gb[k].reshape(N_DEV, -1, d_model) for k in keys]
        lands = [lax.empty((N_DEV - 1,) + g.shape[1:], BF16) for g in g3]
        started, token = split_start([(g3, lands)], exchange_slices, (), "exchange_start_" + keys[0])
        exchanges.append((keys, started[0]))
        last_token[:] = [token]
        return (token,)

    small_started = []

    def put_small(gs, loss):
        pack = pack_small([gs[n] for n in SMALL_NAMES] + [loss[:, :1]])
        started, token = split_start([([pack], [place_own(pack, me, "place_small")])], gather_slices, (), "small_start")
        small_started[:] = started
        return (token,)

    small = {n: w[n] for n in SMALL_NAMES}
    grad_x = local_step(x[0], loss_target[0], small, get_w, put_g, put_small)

    grads, deltas, new_m, new_v = {}, {}, {}, {}
    shapes = [w[n].shape for n in SMALL_NAMES]
    _, (slots,) = split_wait(small_started[0], gather_slices, tuple(last_token), "small_wait")
    total = sum_slots(slots, "sum_small")
    small_g = unpack_small(total, shapes + [(1, 1)])
    loss_total = small_g[-1].reshape(())
    gpack = total[:sum(_pack_rows(w[n].size) for n in SMALL_NAMES)]
    d_, m_, v_ = adamw(pack_small([w[n] for n in SMALL_NAMES]), gpack, pack_small([m[n] for n in SMALL_NAMES]),
                       pack_small([v[n] for n in SMALL_NAMES]), "adamw_small")
    for n, g_, dd, mm, vv in zip(SMALL_NAMES, small_g[:-1], unpack_small(d_, shapes), unpack_small(m_, shapes),
                                 unpack_small(v_, shapes)):
        grads[n], deltas[n], new_m[n], new_v[n] = g_, dd, mm, vv

    after = tuple(arr for n in SMALL_NAMES for arr in (grads[n], deltas[n], new_m[n], new_v[n]))
    for keys, started in exchanges:
        g3, lands = split_wait(started, exchange_slices, after, "exchange_wait_" + keys[0])
        for key, own, land in zip(keys, g3, lands):
            pname, transposed = BIG_NAMES[key]
            to_rows = (lambda p: p[0].T) if transposed else (lambda p: p[0])
            from_rows = (lambda r: r.T[None]) if transposed else (lambda r: r[None])
            g, d_, m_, v_ = sum_adamw(land, own, me, to_rows(w[pname]), to_rows(m[pname]), to_rows(v[pname]),
                                      "adamw_" + key)
            after = after + (g,)
            grads[pname], deltas[pname], new_m[pname], new_v[pname] = (from_rows(g), from_rows(d_), from_rows(m_),
                                                                        from_rows(v_))

    return (loss_total, grad_x[None], *[grads[n] for n in WEIGHT_ORDER], *[deltas[n] for n in WEIGHT_ORDER],
            *[new_m[n] for n in WEIGHT_ORDER], *[new_v[n] for n in WEIGHT_ORDER])
```

```python
import functools

import jax
import jax.numpy as jnp
from jax import lax
from jax.experimental import pallas as pl
from jax.experimental.pallas import tpu as pltpu

F32 = jnp.float32
BF16 = jnp.bfloat16

N_DEV = 8
N_Q_HEADS = 8
N_KV_HEADS = 2
HEAD_DIM = 64
ATTN_WIDTH = N_Q_HEADS * HEAD_DIM
KV_WIDTH = N_KV_HEADS * HEAD_DIM
QKV_WIDTH = ATTN_WIDTH + 2 * KV_WIDTH
BLOCK = 128
GMLP_GROUPS = 8
GMLP_GROUP_DIM = 64
GMLP_WIDTH = GMLP_GROUPS * GMLP_GROUP_DIM
EPS = 1e-6
FFN_RES = 0.5
ATTN_SCALE = HEAD_DIM ** -0.5
MASK_VALUE = -1e30

ADAM_LR = 0.001
ADAM_B1 = 0.9
ADAM_B2 = 0.999
ADAM_EPS = 1e-08
ADAM_WD = 0.01
ADAM_STEP = 10

V7X_VMEM_BYTES = 64 * 1024 * 1024
VMEM_LIMIT = 62 * 1024 * 1024
LANES = 128
MXU_WIDTH = 256
FFN_BWD_TOKENS = 256

NT = (((1,), (1,)), ((), ()))
TN = (((0,), (0,)), ((), ()))


def _dot(a, b):
    return jnp.dot(a, b, preferred_element_type=F32)


def _dg(a, b, dims):
    return lax.dot_general(a, b, dims, preferred_element_type=F32)


def _params(semantics=None):
    return pltpu.CompilerParams(dimension_semantics=semantics, vmem_limit_bytes=VMEM_LIMIT)


def _resident(shape):
    zeros = (0,) * len(shape)
    return pl.BlockSpec(shape, lambda *_: zeros, pipeline_mode=pl.Buffered(1))


def _drop_deps(body, n_in, n_deps):
    return lambda *refs: body(*refs[:n_in], *refs[n_in + n_deps:])


ANY_SPEC = pl.BlockSpec(memory_space=pl.ANY)


TOKEN_TILE = 512
GRAD_TOKENS_PER_STEP = 2048
GRAD_ROW_TILE_CAP = 1408
N_GATHER_ROLES = 13


def _token_tile(t):
    return min(t, TOKEN_TILE)


def _f_chunk(f):
    for c in (256, 128):
        if f % c == 0:
            return c
    return f


def _loss_head(xv, target, gain):
    r = lax.rsqrt(jnp.mean(xv * xv, axis=-1, keepdims=True) + EPS)
    err = xv * r * gain - target
    loss = 0.5 * jnp.sum(jnp.mean(err * err, axis=-1, keepdims=True), axis=0, keepdims=True)
    dx, dgain = _rmsnorm_bwd_rows(xv, r, gain, err * (1.0 / xv.shape[-1]))
    return dx, dgain, loss


def ffn_fwd(x, gain, wg_t, wu_t, wd, name, deps=(), head=None):
    t, d = x.shape
    f = wg_t.shape[0]
    tm = _token_tile(t)
    fc = _f_chunk(f)
    weights = [wg_t, wu_t] + ([] if wd is None else [wd])
    n_in = 2 + len(weights) + (0 if head is None else 2)
    n_x = 0 if wd is None else 1

    def body(x_ref, g_ref, wg_ref, wu_ref, *rest):
        sil_ref, gp_ref, s_ref, h_ref = rest[n_in - 4 + n_x:n_in + n_x]
        xv = x_ref[...]
        r = lax.rsqrt(jnp.mean(xv * xv, axis=-1, keepdims=True) + EPS)
        h = (xv * r * g_ref[...]).astype(BF16)
        h_ref[...] = h
        for c in range(f // fc):
            sl = slice(c * fc, (c + 1) * fc)
            a = _dg(h, wg_ref[sl, :], NT)
            b = _dg(h, wu_ref[sl, :], NT)
            sig = jax.nn.sigmoid(a)
            sil = a * sig
            sil_ref[:, sl] = sil.astype(BF16)
            gp_ref[:, sl] = (b * (sig + sil - sil * sig)).astype(BF16)
            s_ref[:, sl] = (sil * b).astype(BF16)
        if wd is None:
            return
        wd_ref, xo_ref = rest[0], rest[n_in - 4]
        xo = xv + FFN_RES * _dot(s_ref[...], wd_ref[...])
        if head is None:
            xo_ref[...] = xo
        else:
            t_ref, hg_ref = rest[1:3]
            dg_ref, loss_ref = rest[-2:]

            @pl.when(pl.program_id(0) == 0)
            def _():
                dg_ref[...] = jnp.zeros_like(dg_ref)
                loss_ref[...] = jnp.zeros_like(loss_ref)

            dx, dgain, loss = _loss_head(xo, t_ref[...], hg_ref[...])
            xo_ref[...] = dx
            dg_ref[...] += dgain
            loss_ref[...] += loss

    tok = lambda: pl.BlockSpec((tm, d), lambda i: (i, 0))
    wide = lambda: pl.BlockSpec((tm, f), lambda i: (i, 0))
    const2 = lambda i: (0, 0)
    in_specs = [tok(), _resident((1, d))] + [_resident((f, d)) for _ in weights]
    out_specs = [tok()] * n_x + [wide(), wide(), wide(), tok()]
    out_shape = ([jax.ShapeDtypeStruct((t, d), F32)] * n_x + [jax.ShapeDtypeStruct((t, f), BF16)] * 3
                 + [jax.ShapeDtypeStruct((t, d), BF16)])
    operands = [x, gain] + weights
    if head is not None:
        in_specs += [tok(), _resident((1, d))]
        out_specs += [pl.BlockSpec((1, d), const2), pl.BlockSpec((1, LANES), const2)]
        out_shape += [jax.ShapeDtypeStruct((1, d), F32), jax.ShapeDtypeStruct((1, LANES), F32)]
        operands += list(head)
    return pl.pallas_call(
        _drop_deps(body, n_in, len(deps)), name=name, grid=(t // tm,),
        in_specs=in_specs + [ANY_SPEC] * len(deps), out_specs=out_specs, out_shape=out_shape,
        compiler_params=_params(("arbitrary",)),
    )(*operands, *deps)


def _rmsnorm_bwd_rows(xv, r, gain, dh):
    n = xv * r
    dgain = jnp.sum(dh * n, axis=0, keepdims=True)
    dn = dh * gain
    dx = r * (dn - n * jnp.mean(dn * n, axis=-1, keepdims=True))
    return dx, dgain


def ffn_bwd(x, dxo, gain, sil, gp, wg_t, wu_t, wd, name, deps=()):
    t, d = x.shape
    f = wd.shape[0]
    tm = min(t, FFN_BWD_TOKENS)
    nt = t // tm
    chunks = [(off, min(MXU_WIDTH, f - off)) for off in range(0, f, MXU_WIDTH)]

    def body(xp_ref, dxp_ref, dxo_ref, g_ref, sil_ref, gp_ref, wg_ref, wu_ref, wd_ref,
             dx_ref, da_ref, db_ref, df_ref, dg_ref, dh_ref):
        i = pl.program_id(0)
        gain_v = g_ref[...]

        @pl.when(i == 0)
        def _():
            dh_ref[...] = jnp.zeros_like(dh_ref)
            dg_ref[...] = jnp.zeros_like(dg_ref)

        def finish_previous(slot):
            xv = xp_ref[...]
            r = lax.rsqrt(jnp.mean(xv * xv, axis=-1, keepdims=True) + EPS)
            dx, dgain = _rmsnorm_bwd_rows(xv, r, gain_v, dh_ref[slot])
            dx_ref[...] = dxp_ref[...] + dx
            dg_ref[...] += dgain

        @pl.when(i < nt)
        def _():
            df = (FFN_RES * dxo_ref[...]).astype(BF16)
            df_ref[...] = df
            for off, size in chunks:
                sl = slice(off, off + size)
                ds = _dg(df, wd_ref[sl, :], NT)
                da_ref[:, sl] = (ds * gp_ref[:, sl].astype(F32)).astype(BF16)
                db_ref[:, sl] = (ds * sil_ref[:, sl].astype(F32)).astype(BF16)
            dh = _dot(da_ref[...], wg_ref[...]) + _dot(db_ref[...], wu_ref[...])
            finish_previous((i + 1) % 2)
            dh_ref[i % 2] = dh

        @pl.when(i == nt)
        def _():
            finish_previous((nt - 1) % 2)

    prev = lambda i: (jnp.maximum(i - 1, 0), 0)
    cur = lambda i: (jnp.minimum(i, nt - 1), 0)
    return pl.pallas_call(
        _drop_deps(body, 9, len(deps)), name=name, grid=(nt + 1,),
        in_specs=[pl.BlockSpec((tm, d), prev), pl.BlockSpec((tm, d), prev), pl.BlockSpec((tm, d), cur),
                  _resident((1, d)), pl.BlockSpec((tm, f), cur), pl.BlockSpec((tm, f), cur),
                  _resident((f, d)), _resident((f, d)), _resident((f, d))] + [ANY_SPEC] * len(deps),
        out_specs=[pl.BlockSpec((tm, d), prev), pl.BlockSpec((tm, f), cur), pl.BlockSpec((tm, f), cur),
                   pl.BlockSpec((tm, d), cur), pl.BlockSpec((1, d), lambda i: (0, 0))],
        out_shape=[jax.ShapeDtypeStruct((t, d), F32), jax.ShapeDtypeStruct((t, f), BF16),
                   jax.ShapeDtypeStruct((t, f), BF16), jax.ShapeDtypeStruct((t, d), BF16),
                   jax.ShapeDtypeStruct((1, d), F32)],
        scratch_shapes=[pltpu.VMEM((2, tm, d), F32)],
        compiler_params=_params(("arbitrary",)),
    )(x, dxo, dxo, gain, sil, gp, wg_t, wu_t, wd, *deps)


def _row_tile(m, cap):
    if m <= cap:
        return m
    best = None
    for bm in range(LANES, cap + 1, LANES):
        if m % bm == 0:
            best = bm
    return best if best is not None else m


def grad_tn(a, b, scale, name, deps=()):
    t, m = a.shape
    n = b.shape[1]
    bm = _row_tile(m, GRAD_ROW_TILE_CAP)
    bk = min(t, GRAD_TOKENS_PER_STEP)
    nk = t // bk

    def body(a_ref, b_ref, o_ref, acc_ref):
        k = pl.program_id(1)

        @pl.when(k == 0)
        def _():
            acc_ref[...] = jnp.zeros_like(acc_ref)

        acc_ref[...] += _dg(a_ref[...].astype(BF16), b_ref[...].astype(BF16), TN)

        @pl.when(k == nk - 1)
        def _():
            o_ref[...] = (scale * acc_ref[...]).astype(BF16)

    return pl.pallas_call(
        _drop_deps(body, 2, len(deps)), name=name, grid=(m // bm, nk),
        in_specs=[pl.BlockSpec((bk, bm), lambda i, k: (k, i)), pl.BlockSpec((bk, n), lambda i, k: (k, 0))]
        + [ANY_SPEC] * len(deps),
        out_specs=pl.BlockSpec((bm, n), lambda i, k: (i, 0)),
        out_shape=jax.ShapeDtypeStruct((m, n), BF16),
        scratch_shapes=[pltpu.VMEM((bm, n), F32)],
        compiler_params=_params(("arbitrary", "arbitrary")),
    )(a, b, *deps)


def grad_nn(a_list, b, name):
    t, n = b.shape
    ms = [a.shape[0] for a in a_list]
    m = sum(ms)
    bk = min(t, GRAD_TOKENS_PER_STEP)
    nk = t // bk
    na = len(a_list)

    def body(*refs):
        a_refs, b_ref, o_ref, acc_ref = refs[:na], refs[na], refs[na + 1], refs[na + 2]
        k = pl.program_id(0)

        @pl.when(k == 0)
        def _():
            acc_ref[...] = jnp.zeros_like(acc_ref)

        bv = b_ref[...].astype(BF16)
        off = 0
        for a_ref, mi in zip(a_refs, ms):
            acc_ref[off:off + mi, :] += _dot(a_ref[...].astype(BF16), bv)
            off += mi

        @pl.when(k == nk - 1)
        def _():
            o_ref[...] = acc_ref[...].astype(BF16)

    return pl.pallas_call(
        body, name=name, grid=(nk,),
        in_specs=[pl.BlockSpec((mi, bk), lambda k: (0, k)) for mi in ms] + [pl.BlockSpec((bk, n), lambda k: (k, 0))],
        out_specs=pl.BlockSpec((m, n), lambda k: (0, 0)),
        out_shape=jax.ShapeDtypeStruct((m, n), BF16),
        scratch_shapes=[pltpu.VMEM((m, n), F32)],
        compiler_params=_params(("arbitrary",)),
    )(*a_list, b)


def mix_in_fwd(x, gain, win_t, b_in_col, name, ffn_tail=None):
    t, d = x.shape
    w = win_t.shape[0]
    tm = _token_tile(t)

    def body(x_ref, g_ref, w_ref, bias_ref, *rest):
        xv = x_ref[...]
        if ffn_tail is not None:
            s_ref, wd_ref, xo_ref = rest[0], rest[1], rest[4]
            xv = xv + FFN_RES * _dot(s_ref[...], wd_ref[...])
            xo_ref[...] = xv
        qkv_ref, zg_ref = rest[-3:-1] if ffn_tail is not None else rest
        r = lax.rsqrt(jnp.mean(xv * xv, axis=-1, keepdims=True) + EPS)
        h = (xv * r * g_ref[...]).astype(BF16)
        qkv_ref[...] = (_dg(w_ref[:QKV_WIDTH, :], h, NT) + bias_ref[:QKV_WIDTH, :]).astype(BF16)
        zg_ref[...] = (_dg(w_ref[QKV_WIDTH:, :], h, NT) + bias_ref[QKV_WIDTH:, :]).astype(BF16)

    tok = lambda: pl.BlockSpec((tm, d), lambda i: (i, 0))
    in_specs = [tok(), _resident((1, d)), _resident((w, d)), _resident((w, 1))]
    out_specs = [pl.BlockSpec((QKV_WIDTH, tm), lambda i: (0, i)), pl.BlockSpec((w - QKV_WIDTH, tm), lambda i: (0, i))]
    out_shape = [jax.ShapeDtypeStruct((QKV_WIDTH, t), BF16), jax.ShapeDtypeStruct((w - QKV_WIDTH, t), BF16)]
    operands = [x, gain, win_t, b_in_col]
    if ffn_tail is not None:
        f = ffn_tail[1].shape[0]
        in_specs += [pl.BlockSpec((tm, f), lambda i: (i, 0)), _resident((f, d))]
        out_specs.append(tok())
        out_shape.append(jax.ShapeDtypeStruct((t, d), F32))
        operands += list(ffn_tail)
    return pl.pallas_call(
        body, name=name, grid=(t // tm,), in_specs=in_specs, out_specs=out_specs, out_shape=out_shape,
        compiler_params=_params(("arbitrary",)),
    )(*operands)


ATTN_REP = N_Q_HEADS // N_KV_HEADS
ATTN_BLOCKS_PER_STEP = 8


def _attn_tiles(t):
    nbs = min(ATTN_BLOCKS_PER_STEP, t // BLOCK)
    return nbs, nbs * BLOCK, t // (nbs * BLOCK)


def _attn_specs(nbs, tiles):
    last = tiles - 1
    cur = lambda n: jnp.minimum(n, last)
    prev = lambda n: jnp.maximum(jnp.minimum(n, last) * nbs - 1, 0)
    k_row = ATTN_WIDTH // KV_WIDTH
    tw = nbs * BLOCK
    return [
        pl.BlockSpec((ATTN_WIDTH, tw), lambda n: (0, cur(n))),
        pl.BlockSpec((KV_WIDTH, BLOCK), lambda n: (k_row, prev(n))),
        pl.BlockSpec((KV_WIDTH, tw), lambda n: (k_row, cur(n))),
        pl.BlockSpec((KV_WIDTH, BLOCK), lambda n: (k_row + 1, prev(n))),
        pl.BlockSpec((KV_WIDTH, tw), lambda n: (k_row + 1, cur(n))),
    ]


def _cols(b):
    return slice(b * BLOCK, (b + 1) * BLOCK)


def _band_rows(prev_ref, tile_ref, gs, b):
    before = prev_ref[gs, :] if b == 0 else tile_ref[gs, _cols(b - 1)]
    return before, tile_ref[gs, _cols(b)]


def _group_rows(ref, g, b):
    first = g * ATTN_REP
    return jnp.concatenate([ref[(first + r) * HEAD_DIM:(first + r + 1) * HEAD_DIM, _cols(b)] for r in range(ATTN_REP)],
                           axis=1)


def _ungroup_rows(ref, g, b, val):
    first = g * ATTN_REP
    for r in range(ATTN_REP):
        ref[(first + r) * HEAD_DIM:(first + r + 1) * HEAD_DIM, _cols(b)] = val[:, r * BLOCK:(r + 1) * BLOCK]


def _attn_upper():
    key = lax.broadcasted_iota(jnp.int32, (BLOCK, ATTN_REP * BLOCK), 0)
    qry = lax.broadcasted_iota(jnp.int32, (BLOCK, ATTN_REP * BLOCK), 1) & (BLOCK - 1)
    return key > qry


def _attn_probs(q, kp, kc, sink_ref, g, upper, no_prev):
    s = jnp.where(upper, _dg(kp, q, TN), _dg(kc, q, TN)) * ATTN_SCALE
    if no_prev is not None:
        s = jnp.where(upper & no_prev, MASK_VALUE, s)
    sink = jnp.concatenate([jnp.full((1, BLOCK), sink_ref[0, g * ATTN_REP + r], F32) for r in range(ATTN_REP)], axis=1)
    m = jnp.maximum(jnp.max(s, axis=0, keepdims=True), sink)
    e = jnp.exp(s - m)
    es = jnp.exp(sink - m)
    inv = 1.0 / (jnp.sum(e, axis=0, keepdims=True) + es)
    return e * inv, es * inv


def _split_band(upper, val):
    return jnp.where(upper, val, 0.0).astype(BF16), jnp.where(upper, 0.0, val).astype(BF16)


def attn_fwd(qkv_t, sinks, name):
    t = qkv_t.shape[1]
    nbs, tw, tiles = _attn_tiles(t)

    def body(sink_ref, q_ref, kp_ref, kc_ref, vp_ref, vc_ref, y_ref):
        n = pl.program_id(0)
        upper = _attn_upper()
        for b in range(nbs):
            for g in range(N_KV_HEADS):
                gs = slice(g * HEAD_DIM, (g + 1) * HEAD_DIM)
                kp, kc = _band_rows(kp_ref, kc_ref, gs, b)
                vp, vc = _band_rows(vp_ref, vc_ref, gs, b)
                p, _ = _attn_probs(_group_rows(q_ref, g, b), kp, kc, sink_ref, g, upper, n == 0 if b == 0 else None)
                pp, pc = _split_band(upper, p)
                _ungroup_rows(y_ref, g, b, (_dot(vp, pp) + _dot(vc, pc)).astype(BF16))

    return pl.pallas_call(
        body, name=name, grid=(tiles,),
        in_specs=[pl.BlockSpec(memory_space=pltpu.SMEM)] + _attn_specs(nbs, tiles),
        out_specs=pl.BlockSpec((ATTN_WIDTH, tw), lambda n: (0, n)),
        out_shape=jax.ShapeDtypeStruct((ATTN_WIDTH, t), BF16),
        compiler_params=_params(("arbitrary",)),
    )(sinks, qkv_t, qkv_t, qkv_t, qkv_t, qkv_t)


def attn_bwd(qkv_t, dy_t, sinks, name):
    t = qkv_t.shape[1]
    nbs, tw, tiles = _attn_tiles(t)
    kept = slice(0, tw - BLOCK)

    def body(sink_ref, q_ref, kp_ref, kc_ref, vp_ref, vc_ref, do_ref, dq_ref, dk_ref, dv_ref, dsink_ref,
             ck_ref, cv_ref, sacc_ref):
        n = pl.program_id(0)

        @pl.when(n == 0)
        def _():
            sacc_ref[...] = jnp.zeros_like(sacc_ref)

        @pl.when((n > 0) & (n < tiles))
        def _():
            if nbs > 1:
                dk_ref[:, kept] = ck_ref[:, kept].astype(BF16)
                dv_ref[:, kept] = cv_ref[:, kept].astype(BF16)

        @pl.when(n < tiles)
        def _():
            upper = _attn_upper()
            for b in range(nbs):
                for g in range(N_KV_HEADS):
                    gs = slice(g * HEAD_DIM, (g + 1) * HEAD_DIM)
                    kp, kc = _band_rows(kp_ref, kc_ref, gs, b)
                    vp, vc = _band_rows(vp_ref, vc_ref, gs, b)
                    q = _group_rows(q_ref, g, b)
                    do = _group_rows(do_ref, g, b)
                    p, ps = _attn_probs(q, kp, kc, sink_ref, g, upper, n == 0 if b == 0 else None)
                    dp = jnp.where(upper, _dg(vp, do, TN), _dg(vc, do, TN))
                    delta = jnp.sum(p * dp, axis=0, keepdims=True)
                    dsink = -(ps * delta)
                    for r in range(ATTN_REP):
                        hd = g * ATTN_REP + r
                        sacc_ref[hd:hd + 1, :] += dsink[:, r * BLOCK:(r + 1) * BLOCK]
                    dsp, dsc = _split_band(upper, p * (dp - delta))
                    pp, pc = _split_band(upper, p)
                    _ungroup_rows(dq_ref, g, b, (ATTN_SCALE * (_dot(kp, dsp) + _dot(kc, dsc))).astype(BF16))
                    dk_before = ATTN_SCALE * _dg(q, dsp, NT)
                    dv_before = _dg(do, pp, NT)
                    if b == 0:
                        @pl.when(n > 0)
                        def _():
                            dk_ref[gs, _cols(nbs - 1)] = (ck_ref[gs, _cols(nbs - 1)] + dk_before).astype(BF16)
                            dv_ref[gs, _cols(nbs - 1)] = (cv_ref[gs, _cols(nbs - 1)] + dv_before).astype(BF16)
                    else:
                        ck_ref[gs, _cols(b - 1)] += dk_before
                        cv_ref[gs, _cols(b - 1)] += dv_before
                    ck_ref[gs, _cols(b)] = ATTN_SCALE * _dg(q, dsc, NT)
                    cv_ref[gs, _cols(b)] = _dg(do, pc, NT)

        @pl.when(n == tiles)
        def _():
            dk_ref[...] = ck_ref[...].astype(BF16)
            dv_ref[...] = cv_ref[...].astype(BF16)
            dsink_ref[...] = jnp.sum(sacc_ref[...], axis=1, keepdims=True)

    last = tiles - 1
    done = lambda n: jnp.maximum(n - 1, 0)
    outs = pl.pallas_call(
        body, name=name, grid=(tiles + 1,),
        in_specs=[pl.BlockSpec(memory_space=pltpu.SMEM)] + _attn_specs(nbs, tiles)
        + [pl.BlockSpec((ATTN_WIDTH, tw), lambda n: (0, jnp.minimum(n, last)))],
        out_specs=[pl.BlockSpec((ATTN_WIDTH, tw), lambda n: (0, jnp.minimum(n, last))),
                   pl.BlockSpec((KV_WIDTH, tw), lambda n: (0, done(n))),
                   pl.BlockSpec((KV_WIDTH, tw), lambda n: (0, done(n))),
                   pl.BlockSpec((N_Q_HEADS, 1), lambda n: (0, 0))],
        out_shape=[jax.ShapeDtypeStruct((ATTN_WIDTH, t), BF16), jax.ShapeDtypeStruct((KV_WIDTH, t), BF16),
                   jax.ShapeDtypeStruct((KV_WIDTH, t), BF16), jax.ShapeDtypeStruct((N_Q_HEADS, 1), F32)],
        scratch_shapes=[pltpu.VMEM((KV_WIDTH, tw), F32), pltpu.VMEM((KV_WIDTH, tw), F32),
                        pltpu.VMEM((N_Q_HEADS, BLOCK), F32)],
        compiler_params=_params(("arbitrary",)),
    )(sinks, qkv_t, qkv_t, qkv_t, qkv_t, qkv_t, dy_t)
    return outs


GELU_C = 0.7978845608028654
GELU_A = 0.044715


def _gelu(x):
    return 0.5 * x * (1.0 + jnp.tanh(GELU_C * (x + GELU_A * x * x * x)))


def _gelu_grad(x):
    th = jnp.tanh(GELU_C * (x + GELU_A * x * x * x))
    return 0.5 * (1.0 + th) + 0.5 * x * (1.0 - th * th) * GELU_C * (1.0 + 3.0 * GELU_A * x * x)


def _gmlp_parts(zg, lng, lnb):
    z = _gelu(zg)
    u = z[:GMLP_WIDTH, :]
    vv = z[GMLP_WIDTH:, :]
    mu = jnp.mean(vv, axis=0, keepdims=True)
    xc = vv - mu
    rstd = lax.rsqrt(jnp.mean(xc * xc, axis=0, keepdims=True) + EPS)
    xhat = xc * rstd
    return u, xhat, rstd, xhat * lng + lnb


def _causal(w):
    row = lax.broadcasted_iota(jnp.int32, (BLOCK, BLOCK), 0)
    col = lax.broadcasted_iota(jnp.int32, (BLOCK, BLOCK), 1)
    return jnp.where(col <= row, w, 0.0)


GMLP_CHUNKS_PER_STEP = 4


def _stack_chunks(v, nc):
    return jnp.concatenate([v[:, c * BLOCK:(c + 1) * BLOCK] for c in range(nc)], axis=0)


def _unstack_chunks(v, nc):
    rows = v.shape[0] // nc
    return jnp.concatenate([v[c * rows:(c + 1) * rows, :] for c in range(nc)], axis=1)


def gmlp_fwd(zg_t, lng, lnb, w_s, b_s, name):
    t = zg_t.shape[1]
    nc = min(GMLP_CHUNKS_PER_STEP, t // BLOCK)
    tw = nc * BLOCK

    def body(zg_ref, lng_ref, lnb_ref, w_ref, bs_ref, y_ref):
        u, _, _, vn = _gmlp_parts(zg_ref[...].astype(F32), lng_ref[...], lnb_ref[...])
        for g in range(GMLP_GROUPS):
            gs = slice(g * GMLP_GROUP_DIM, (g + 1) * GMLP_GROUP_DIM)
            wc = _causal(w_ref[g]).astype(BF16)
            mixed = _dg(_stack_chunks(vn[gs, :].astype(BF16), nc), wc, NT) + bs_ref[g:g + 1, :]
            y_ref[gs, :] = (u[gs, :] * _unstack_chunks(mixed, nc)).astype(BF16)

    return pl.pallas_call(
        body, name=name, grid=(t // tw,),
        in_specs=[pl.BlockSpec((2 * GMLP_WIDTH, tw), lambda n: (0, n)), _resident((GMLP_WIDTH, 1)),
                  _resident((GMLP_WIDTH, 1)), _resident((GMLP_GROUPS, BLOCK, BLOCK)), _resident((GMLP_GROUPS, BLOCK))],
        out_specs=pl.BlockSpec((GMLP_WIDTH, tw), lambda n: (0, n)),
        out_shape=jax.ShapeDtypeStruct((GMLP_WIDTH, t), BF16),
        compiler_params=_params(("arbitrary",)),
    )(zg_t, lng, lnb, w_s, b_s)


def gmlp_bwd(zg_t, dy_t, lng, lnb, w_s, b_s, name):
    t = zg_t.shape[1]
    nc = min(GMLP_CHUNKS_PER_STEP, t // BLOCK)
    tw = nc * BLOCK
    nt = t // tw

    def body(zg_ref, dy_ref, lng_ref, lnb_ref, w_ref, bs_ref, dzg_ref, dw_ref, dbs_ref, dlng_ref, dlnb_ref,
             gacc_ref, bacc_ref):
        n = pl.program_id(0)

        @pl.when(n == 0)
        def _():
            dw_ref[...] = jnp.zeros_like(dw_ref)
            dbs_ref[...] = jnp.zeros_like(dbs_ref)
            gacc_ref[...] = jnp.zeros_like(gacc_ref)
            bacc_ref[...] = jnp.zeros_like(bacc_ref)

        zg = zg_ref[...].astype(F32)
        lng_v = lng_ref[...]
        u, xhat, rstd, vn = _gmlp_parts(zg, lng_v, lnb_ref[...])
        dy = dy_ref[...].astype(F32)
        dvn_parts = []
        for g in range(GMLP_GROUPS):
            gs = slice(g * GMLP_GROUP_DIM, (g + 1) * GMLP_GROUP_DIM)
            wc = _causal(w_ref[g]).astype(BF16)
            vn_s = _stack_chunks(vn[gs, :].astype(BF16), nc)
            mixed = _unstack_chunks(_dg(vn_s, wc, NT) + bs_ref[g:g + 1, :], nc)
            dy_g = dy[gs, :]
            dmixed = dy_g * u[gs, :]
            dm_s = _stack_chunks(dmixed.astype(BF16), nc)
            dzg_ref[gs, :] = (dy_g * mixed * _gelu_grad(zg[gs, :])).astype(BF16)
            dw_ref[g] += _causal(_dg(dm_s, vn_s, TN))
            dbs_ref[g:g + 1, :] += _lane_fold(jnp.sum(dmixed, axis=0, keepdims=True))
            dvn_parts.append(_unstack_chunks(_dot(dm_s, wc), nc))
        dvn = jnp.concatenate(dvn_parts, axis=0)
        gacc_ref[...] += _lane_fold(dvn * xhat)
        bacc_ref[...] += _lane_fold(dvn)
        dxhat = dvn * lng_v
        m1 = jnp.mean(dxhat, axis=0, keepdims=True)
        m2 = jnp.mean(dxhat * xhat, axis=0, keepdims=True)
        dvv = rstd * (dxhat - m1 - xhat * m2)
        dzg_ref[GMLP_WIDTH:, :] = (dvv * _gelu_grad(zg[GMLP_WIDTH:, :])).astype(BF16)

        @pl.when(n == nt - 1)
        def _():
            dlng_ref[...] = jnp.sum(gacc_ref[...], axis=1, keepdims=True)
            dlnb_ref[...] = jnp.sum(bacc_ref[...], axis=1, keepdims=True)

    const2 = lambda n: (0, 0)
    return pl.pallas_call(
        body, name=name, grid=(nt,),
        in_specs=[pl.BlockSpec((2 * GMLP_WIDTH, tw), lambda n: (0, n)), pl.BlockSpec((GMLP_WIDTH, tw), lambda n: (0, n)),
                  _resident((GMLP_WIDTH, 1)), _resident((GMLP_WIDTH, 1)),
                  _resident((GMLP_GROUPS, BLOCK, BLOCK)), _resident((GMLP_GROUPS, BLOCK))],
        out_specs=[pl.BlockSpec((2 * GMLP_WIDTH, tw), lambda n: (0, n)),
                   pl.BlockSpec((GMLP_GROUPS, BLOCK, BLOCK), lambda n: (0, 0, 0)),
                   pl.BlockSpec((GMLP_GROUPS, BLOCK), const2),
                   pl.BlockSpec((GMLP_WIDTH, 1), const2), pl.BlockSpec((GMLP_WIDTH, 1), const2)],
        out_shape=[jax.ShapeDtypeStruct((2 * GMLP_WIDTH, t), BF16),
                   jax.ShapeDtypeStruct((GMLP_GROUPS, BLOCK, BLOCK), F32),
                   jax.ShapeDtypeStruct((GMLP_GROUPS, BLOCK), F32),
                   jax.ShapeDtypeStruct((GMLP_WIDTH, 1), F32), jax.ShapeDtypeStruct((GMLP_WIDTH, 1), F32)],
        scratch_shapes=[pltpu.VMEM((GMLP_WIDTH, BLOCK), F32), pltpu.VMEM((GMLP_WIDTH, BLOCK), F32)],
        compiler_params=_params(("arbitrary",)),
    )(zg_t, dy_t, lng, lnb, w_s, b_s)


def _rmsnorm_cols(y, gain_col):
    r = lax.rsqrt(jnp.mean(y * y, axis=0, keepdims=True) + EPS)
    n = y * r
    return n, r, n * gain_col


def mix_out_fwd(x, ya_t, yg_t, gao, ggo, wout, b_out, name):
    t, d = x.shape
    tm = _token_tile(t)

    def body(x_ref, ya_ref, yg_ref, gao_ref, ggo_ref, w_ref, b_ref, o_ref):
        _, _, ya = _rmsnorm_cols(ya_ref[...].astype(F32), gao_ref[...])
        _, _, yg = _rmsnorm_cols(yg_ref[...].astype(F32), ggo_ref[...])
        o_ref[...] = (x_ref[...] + _dg(ya.astype(BF16), w_ref[:ATTN_WIDTH, :], TN)
                      + _dg(yg.astype(BF16), w_ref[ATTN_WIDTH:, :], TN) + b_ref[...])

    return pl.pallas_call(
        body, name=name, grid=(t // tm,),
        in_specs=[pl.BlockSpec((tm, d), lambda i: (i, 0)), pl.BlockSpec((ATTN_WIDTH, tm), lambda i: (0, i)),
                  pl.BlockSpec((GMLP_WIDTH, tm), lambda i: (0, i)), _resident((ATTN_WIDTH, 1)), _resident((GMLP_WIDTH, 1)),
                  _resident((ATTN_WIDTH + GMLP_WIDTH, d)), _resident((1, d))],
        out_specs=pl.BlockSpec((tm, d), lambda i: (i, 0)),
        out_shape=jax.ShapeDtypeStruct((t, d), F32),
        compiler_params=_params(("arbitrary",)),
    )(x, ya_t, yg_t, gao, ggo, wout, b_out)


def _lane_fold(v):
    out = v[:, :LANES]
    for j in range(1, v.shape[1] // LANES):
        out = out + v[:, j * LANES:(j + 1) * LANES]
    return out


def mix_out_bwd(dx, ya_t, yg_t, gao, ggo, wout, name, deps=()):
    t, d = dx.shape
    tm = _token_tile(t)
    nt = t // tm

    def body(dx_ref, ya_ref, yg_ref, gao_ref, ggo_ref, w_ref, dya_ref, dyg_ref, y_ref, dgao_ref, dggo_ref, db_ref,
             acc_ref):
        i = pl.program_id(0)

        @pl.when(i == 0)
        def _():
            acc_ref[...] = jnp.zeros_like(acc_ref)
            db_ref[...] = jnp.zeros_like(db_ref)

        dxv = dx_ref[...]
        db_ref[...] += jnp.sum(dxv, axis=0, keepdims=True)
        dxb = dxv.astype(BF16)
        for part, (src_ref, gain_ref, dst_ref) in enumerate(((ya_ref, gao_ref, dya_ref), (yg_ref, ggo_ref, dyg_ref))):
            rows = slice(part * ATTN_WIDTH, (part + 1) * ATTN_WIDTH)
            gain = gain_ref[...]
            nrm, r, yn = _rmsnorm_cols(src_ref[...].astype(F32), gain)
            y_ref[rows, :] = yn.astype(BF16)
            dy = _dg(w_ref[rows, :], dxb, NT)
            acc_ref[rows, :] += _lane_fold(dy * nrm)
            dn = dy * gain
            dst_ref[...] = (r * (dn - nrm * jnp.mean(dn * nrm, axis=0, keepdims=True))).astype(BF16)

        @pl.when(i == nt - 1)
        def _():
            dgao_ref[...] = jnp.sum(acc_ref[:ATTN_WIDTH, :], axis=1, keepdims=True)
            dggo_ref[...] = jnp.sum(acc_ref[ATTN_WIDTH:, :], axis=1, keepdims=True)

    const2 = lambda i: (0, 0)
    feat = lambda w: pl.BlockSpec((w, tm), lambda i: (0, i))
    return pl.pallas_call(
        _drop_deps(body, 6, len(deps)), name=name, grid=(nt,),
        in_specs=[pl.BlockSpec((tm, d), lambda i: (i, 0)), feat(ATTN_WIDTH), feat(GMLP_WIDTH),
                  _resident((ATTN_WIDTH, 1)), _resident((GMLP_WIDTH, 1)), _resident((ATTN_WIDTH + GMLP_WIDTH, d))]
        + [ANY_SPEC] * len(deps),
        out_specs=[feat(ATTN_WIDTH), feat(GMLP_WIDTH), feat(ATTN_WIDTH + GMLP_WIDTH),
                   pl.BlockSpec((ATTN_WIDTH, 1), const2), pl.BlockSpec((GMLP_WIDTH, 1), const2),
                   pl.BlockSpec((1, d), const2)],
        out_shape=[jax.ShapeDtypeStruct((ATTN_WIDTH, t), BF16), jax.ShapeDtypeStruct((GMLP_WIDTH, t), BF16),
                   jax.ShapeDtypeStruct((ATTN_WIDTH + GMLP_WIDTH, t), BF16),
                   jax.ShapeDtypeStruct((ATTN_WIDTH, 1), F32), jax.ShapeDtypeStruct((GMLP_WIDTH, 1), F32),
                   jax.ShapeDtypeStruct((1, d), F32)],
        scratch_shapes=[pltpu.VMEM((ATTN_WIDTH + GMLP_WIDTH, LANES), F32)],
        compiler_params=_params(("arbitrary",)),
    )(dx, ya_t, yg_t, gao, ggo, wout, *deps)


def mix_in_bwd(x, dxo, gain, dq_t, dk_t, dv_t, dzg_t, win_t, name):
    t, d = x.shape
    w = win_t.shape[0]
    tm = _token_tile(t)
    nt = t // tm
    parts = ((0, ATTN_WIDTH), (ATTN_WIDTH, KV_WIDTH), (ATTN_WIDTH + KV_WIDTH, KV_WIDTH), (QKV_WIDTH, w - QKV_WIDTH))

    def body(x_ref, dxo_ref, g_ref, dq_ref, dk_ref, dv_ref, dzg_ref, w_ref, dx_ref, h_ref, dg_ref, db_ref, acc_ref):
        i = pl.program_id(0)

        @pl.when(i == 0)
        def _():
            acc_ref[...] = jnp.zeros_like(acc_ref)
            dg_ref[...] = jnp.zeros_like(dg_ref)

        xv = x_ref[...]
        gain_v = g_ref[...]
        r = lax.rsqrt(jnp.mean(xv * xv, axis=-1, keepdims=True) + EPS)
        h_ref[...] = (xv * r * gain_v).astype(BF16)
        dh = jnp.zeros((tm, d), F32)
        for (off, size), src_ref in zip(parts, (dq_ref, dk_ref, dv_ref, dzg_ref)):
            dp = src_ref[...]
            acc_ref[off:off + size, :] += _lane_fold(dp.astype(F32))
            dh = dh + _dg(dp, w_ref[off:off + size, :], TN)
        dx, dgain = _rmsnorm_bwd_rows(xv, r, gain_v, dh)
        dx_ref[...] = dxo_ref[...] + dx
        dg_ref[...] += dgain

        @pl.when(i == nt - 1)
        def _():
            db_ref[...] = jnp.sum(acc_ref[...], axis=1, keepdims=True)

    const2 = lambda i: (0, 0)
    tok = lambda: pl.BlockSpec((tm, d), lambda i: (i, 0))
    feat = lambda rows: pl.BlockSpec((rows, tm), lambda i: (0, i))
    return pl.pallas_call(
        body, name=name, grid=(nt,),
        in_specs=[tok(), tok(), _resident((1, d)), feat(ATTN_WIDTH), feat(KV_WIDTH), feat(KV_WIDTH),
                  feat(w - QKV_WIDTH), _resident((w, d))],
        out_specs=[tok(), tok(), pl.BlockSpec((1, d), const2), pl.BlockSpec((w, 1), const2)],
        out_shape=[jax.ShapeDtypeStruct((t, d), F32), jax.ShapeDtypeStruct((t, d), BF16),
                   jax.ShapeDtypeStruct((1, d), F32), jax.ShapeDtypeStruct((w, 1), F32)],
        scratch_shapes=[pltpu.VMEM((w, LANES), F32)],
        compiler_params=_params(("arbitrary",)),
    )(x, dxo, gain, dq_t, dk_t, dv_t, dzg_t, win_t)


def _adamw_math(w, g, m, v):
    m = ADAM_B1 * m + (1.0 - ADAM_B1) * g
    v = ADAM_B2 * v + (1.0 - ADAM_B2) * (g * g)
    m_hat = m / (1.0 - ADAM_B1 ** ADAM_STEP)
    v_hat = v / (1.0 - ADAM_B2 ** ADAM_STEP)
    delta = -ADAM_LR * (m_hat / (jnp.sqrt(v_hat) + ADAM_EPS) + ADAM_WD * w)
    return delta, m, v


def adamw_many(ws, gs, ms, vs, name):
    n = len(ws)

    def body(*refs):
        w_refs, g_refs, m_refs, v_refs = refs[:n], refs[n:2 * n], refs[2 * n:3 * n], refs[3 * n:4 * n]
        d_refs, mo_refs, vo_refs = refs[4 * n:5 * n], refs[5 * n:6 * n], refs[6 * n:]
        for k in range(n):
            delta, mn, vn = _adamw_math(w_refs[k][...], g_refs[k][...], m_refs[k][...], v_refs[k][...])
            d_refs[k][...] = delta
            mo_refs[k][...] = mn
            vo_refs[k][...] = vn

    def whole(a):
        zeros = (0,) * a.ndim
        return pl.BlockSpec(a.shape, lambda i: zeros)

    outs = pl.pallas_call(
        body, name=name, grid=(1,),
        in_specs=[whole(a) for a in ws] * 4, out_specs=[whole(a) for a in ws] * 3,
        out_shape=[jax.ShapeDtypeStruct(a.shape, F32) for a in ws] * 3,
        compiler_params=_params(("arbitrary",)),
    )(*ws, *gs, *ms, *vs)
    return outs[:n], outs[n:2 * n], outs[2 * n:]


def sum_adamw(landing, grad3d, me, w, m, v, name):
    nd, rows, cols = landing.shape
    tr = rows // 2

    def body(me_ref, l_ref, own_ref, w_ref, m_ref, v_ref, g_ref, d_ref, mo_ref, vo_ref):
        g = own_ref[...].astype(F32)
        for j in range(nd):
            g = g + l_ref[j].astype(F32)
        delta, mn, vn = _adamw_math(w_ref[...], g, m_ref[...], v_ref[...])
        g_ref[...] = g
        d_ref[...] = delta
        mo_ref[...] = mn
        vo_ref[...] = vn

    spec = lambda: pl.BlockSpec((tr, cols), lambda i, me_ref: (i, 0))
    shape = jax.ShapeDtypeStruct((rows, cols), F32)
    return pl.pallas_call(
        body, name=name,
        grid_spec=pltpu.PrefetchScalarGridSpec(
            num_scalar_prefetch=1, grid=(rows // tr,),
            in_specs=[pl.BlockSpec((nd, tr, cols), lambda i, me_ref: (0, i, 0)),
                      pl.BlockSpec((None, tr, cols), lambda i, me_ref: (me_ref[0], i, 0)), spec(), spec(), spec()],
            out_specs=[spec() for _ in range(4)]),
        out_shape=[shape] * 4,
        compiler_params=_params(("arbitrary",)),
    )(me, landing, grad3d, w, m, v)


def _mesh_place():
    x, y, c = lax.axis_index("x"), lax.axis_index("y"), lax.axis_index("c")
    return x, y, c, 4 * x + 2 * y + c


def _peer(x, y, c, k):
    return (x ^ ((k >> 2) & 1), y ^ ((k >> 1) & 1), c ^ (k & 1))


def prep_shards(mats, me, name):
    na = len(mats)

    def body(me_ref, *refs):
        in_refs, shard_refs, land_refs = refs[:na], refs[na:2 * na], refs[2 * na:]
        for a in range(na):
            val = in_refs[a][...].astype(BF16)
            shard_refs[a][...] = val
            land_refs[a][...] = val

    whole = lambda mat: pl.BlockSpec(mat.shape, lambda i, me_ref: (0, 0))
    outs = pl.pallas_call(
        body, name=name,
        grid_spec=pltpu.PrefetchScalarGridSpec(
            num_scalar_prefetch=1, grid=(1,),
            in_specs=[whole(mat) for mat in mats],
            out_specs=[whole(mat) for mat in mats]
            + [pl.BlockSpec((None,) + mat.shape, lambda i, me_ref: (me_ref[0], 0, 0)) for mat in mats]),
        out_shape=[jax.ShapeDtypeStruct(mat.shape, BF16) for mat in mats]
        + [jax.ShapeDtypeStruct((N_DEV,) + mat.shape, BF16) for mat in mats],
        compiler_params=_params(("arbitrary",)),
    )(me, *mats)
    return outs[:na], outs[na:]


def allgather_rows(shards, lands, name):
    na = n_gather = len(shards)

    def body(*refs):
        in_refs, out_refs = refs[:na], refs[2 * na:3 * na]
        send_sems, recv_sems = refs[3 * na:]
        x, y, c, me = _mesh_place()

        def copy(sem, a, origin, src, k):
            return pltpu.make_async_remote_copy(
                src_ref=src, dst_ref=out_refs[a].at[origin], send_sem=send_sems.at[sem, a],
                recv_sem=recv_sems.at[sem, a], device_id=_peer(x, y, c, k), device_id_type=pl.DeviceIdType.MESH)

        kx, ky = 4, 2

        def recv(sem, a, origin):
            copy(sem, a, origin, in_refs[a], 1).wait_recv()

        def landed(a, origin):
            return out_refs[a].at[origin]

        @pl.when(c == 1)
        def _():
            sends = []

            def send(sem, a, origin, src, k):
                cp = copy(sem, a, origin, src, k)
                cp.start()
                sends.append(cp)

            for a in range(n_gather):
                send(1, a, me, in_refs[a], kx)
                send(2, a, me, in_refs[a], ky)
                send(0, a, me, in_refs[a], 1)
            for a in range(n_gather):
                recv(0, a, me ^ 1)
                send(3, a, me ^ 1, landed(a, me ^ 1), kx)
                send(4, a, me ^ 1, landed(a, me ^ 1), ky)
            for a in range(n_gather):
                recv(1, a, me ^ kx)
                send(5, a, me ^ kx, landed(a, me ^ kx), ky)
                send(7, a, me ^ kx, landed(a, me ^ kx), 1)
                recv(2, a, me ^ ky)
                send(8, a, me ^ ky, landed(a, me ^ ky), 1)
            for a in range(n_gather):
                recv(3, a, me ^ kx ^ 1)
                send(9, a, me ^ kx ^ 1, landed(a, me ^ kx ^ 1), 1)
                recv(4, a, me ^ ky ^ 1)
                send(6, a, me ^ ky ^ 1, landed(a, me ^ ky ^ 1), kx)
                send(10, a, me ^ ky ^ 1, landed(a, me ^ ky ^ 1), 1)
            for a in range(n_gather):
                recv(5, a, me ^ 6)
                send(11, a, me ^ 6, landed(a, me ^ 6), 1)
                recv(6, a, me ^ 7)
                send(12, a, me ^ 7, landed(a, me ^ 7), 1)
            for cp in sends:
                cp.wait_send()

        @pl.when(c == 0)
        def _():
            north = me ^ 1
            own = [copy(0, a, me, in_refs[a], 1) for a in range(n_gather)]
            for cp in own:
                cp.start()
            for a in range(n_gather):
                recv(0, a, north)
                for sem, origin in ((7, north ^ kx), (8, north ^ ky), (9, north ^ kx ^ 1), (10, north ^ ky ^ 1),
                                    (11, north ^ 6), (12, north ^ 7)):
                    recv(sem, a, origin)
            for cp in own:
                cp.wait_send()

    return pl.pallas_call(
        body, name=name,
        in_specs=[ANY_SPEC] * (2 * na), out_specs=[ANY_SPEC] * na,
        out_shape=[jax.ShapeDtypeStruct(land.shape, land.dtype) for land in lands],
        input_output_aliases={na + a: a for a in range(na)},
        scratch_shapes=[pltpu.SemaphoreType.DMA((N_GATHER_ROLES, n_gather)),
                        pltpu.SemaphoreType.DMA((N_GATHER_ROLES, n_gather))],
    )(*shards, *lands)


HBM_SPEC = pl.BlockSpec(memory_space=pltpu.HBM)
SEM_SPEC = pl.BlockSpec(memory_space=pltpu.SEMAPHORE)
SIDE_EFFECT = pltpu.SideEffectType.DATAFLOW_SIDE_EFFECTING


def _hbm(v):
    return pltpu.with_memory_space_constraint(v, pltpu.HBM)


def gather_slices(src, land, me, k):
    return src, land.at[me], land.at[me ^ k]


def exchange_slices(src, land, me, k):
    return src.at[me ^ k], land.at[k - 1], land.at[k - 1]


def split_start(groups, slices, deps, name):
    sizes = [len(srcs) for srcs, _ in groups]
    flat_src = [s for srcs, _ in groups for s in srcs]
    flat_land = [l for _, lands in groups for l in lands]
    na, ng, nd = len(flat_src), len(groups), len(deps)

    def body(*refs):
        src_refs, land_refs = refs[:na], refs[na:2 * na]
        sem_refs = refs[2 * na + nd:2 * na + nd + 2 * ng]
        token_ref = refs[-1]
        x, y, c, me = _mesh_place()
        a0 = 0
        for gi, size in enumerate(sizes):
            send_sems, recv_sems = sem_refs[2 * gi], sem_refs[2 * gi + 1]
            for k in range(1, N_DEV):
                for j in range(size):
                    src, dst, _ = slices(src_refs[a0 + j], land_refs[a0 + j], me, k)
                    pltpu.make_async_remote_copy(
                        src_ref=src, dst_ref=dst, send_sem=send_sems.at[(k - 1) * size + j],
                        recv_sem=recv_sems.at[(k - 1) * size + j],
                        device_id=_peer(x, y, c, k), device_id_type=pl.DeviceIdType.MESH).start()
            a0 += size
        token_ref[...] = jnp.zeros_like(token_ref)

    sems = [pltpu.SemaphoreType.DMA(((N_DEV - 1) * size,)) for size in sizes for _ in range(2)]
    thru = [pltpu.HBM(v.shape, v.dtype) for v in flat_src + flat_land]
    outs = pl.pallas_call(
        body, name=name,
        in_specs=[HBM_SPEC] * (2 * na) + [ANY_SPEC] * nd,
        out_specs=[SEM_SPEC] * (2 * ng) + [HBM_SPEC] * (2 * na) + [pl.BlockSpec(memory_space=pltpu.VMEM)],
        out_shape=sems + thru + [jax.ShapeDtypeStruct((8, LANES), F32)],
        input_output_aliases={i: 2 * ng + i for i in range(2 * na)},
        compiler_params=pltpu.CompilerParams(has_side_effects=SIDE_EFFECT),
    )(*[_hbm(v) for v in flat_src + flat_land], *deps)
    started, a0 = [], 0
    for gi, size in enumerate(sizes):
        srcs = outs[2 * ng + a0:2 * ng + a0 + size]
        lands = outs[2 * ng + na + a0:2 * ng + na + a0 + size]
        started.append((outs[2 * gi], outs[2 * gi + 1], list(srcs), list(lands)))
        a0 += size
    return started, outs[-1]


def split_wait(started, slices, after, name):
    send_sems, recv_sems, srcs, lands = started
    na = len(srcs)

    def body(*refs):
        src_refs, land_refs = refs[:na], refs[na:2 * na]
        send_ref, recv_ref = refs[2 * na], refs[2 * na + 1]
        x, y, c, me = _mesh_place()
        for k in range(1, N_DEV):
            for j in range(na):
                src, _, got = slices(src_refs[j], land_refs[j], me, k)
                cp = pltpu.make_async_remote_copy(
                    src_ref=src, dst_ref=got, send_sem=send_ref.at[(k - 1) * na + j], recv_sem=recv_ref.at[(k - 1) * na + j],
                    device_id=_peer(x, y, c, k), device_id_type=pl.DeviceIdType.MESH)
                cp.wait_send()
                cp.wait_recv()

    outs = pl.pallas_call(
        body, name=name,
        in_specs=[HBM_SPEC] * (2 * na) + [SEM_SPEC, SEM_SPEC] + [ANY_SPEC] * len(after),
        out_specs=[HBM_SPEC] * (2 * na),
        out_shape=[pltpu.HBM(v.shape, v.dtype) for v in srcs + lands],
        input_output_aliases={i: i for i in range(2 * na)},
        compiler_params=pltpu.CompilerParams(has_side_effects=SIDE_EFFECT),
    )(*srcs, *lands, send_sems, recv_sems, *after)
    return list(outs[:na]), list(outs[na:])


def place_own(pack, me, name):
    rows, cols = pack.shape

    def body(me_ref, p_ref, o_ref):
        o_ref[...] = p_ref[...]

    return pl.pallas_call(
        body, name=name,
        grid_spec=pltpu.PrefetchScalarGridSpec(
            num_scalar_prefetch=1, grid=(1,),
            in_specs=[pl.BlockSpec((rows, cols), lambda i, me_ref: (0, 0))],
            out_specs=pl.BlockSpec((None, rows, cols), lambda i, me_ref: (me_ref[0], 0, 0))),
        out_shape=jax.ShapeDtypeStruct((N_DEV, rows, cols), F32),
        compiler_params=_params(("arbitrary",)),
    )(me, pack)


def sum_slots(slots, name):
    nd, rows, cols = slots.shape

    def body(s_ref, o_ref):
        acc = s_ref[0]
        for j in range(1, nd):
            acc = acc + s_ref[j]
        o_ref[...] = acc

    return pl.pallas_call(
        body, name=name, grid=(1,),
        in_specs=[pl.BlockSpec((nd, rows, cols), lambda i: (0, 0, 0))],
        out_specs=pl.BlockSpec((rows, cols), lambda i: (0, 0)),
        out_shape=jax.ShapeDtypeStruct((rows, cols), F32),
        compiler_params=_params(("arbitrary",)),
    )(slots)


def _pack_rows(size):
    rows = -(-size // LANES)
    return -(-rows // 8) * 8


def pack_small(parts):
    out = []
    for p in parts:
        flat = p.reshape(-1).astype(F32)
        rows = _pack_rows(flat.shape[0])
        out.append(jnp.pad(flat, (0, rows * LANES - flat.shape[0])).reshape(rows, LANES))
    return jnp.concatenate(out, axis=0)


def unpack_small(pack, shapes):
    out, off = [], 0
    for shp in shapes:
        size = 1
        for s in shp:
            size *= s
        rows = _pack_rows(size)
        out.append(pack[off:off + rows].reshape(-1)[:size].reshape(shp))
        off += rows
    return out


def local_step(x, target, small, get_w, put_g, put_small):
    col = lambda v: v.reshape(-1, 1)
    (wg1, wu1), deps = get_w(("wg1", "wu1"), ())
    sil1, gp1, s1, h1 = ffn_fwd(x, small["ffn1_norm_g"], wg1, wu1, None, "ffn1_fwd", deps)
    (wd1, win, wout), _ = get_w(("wd1", "win", "wout"), (s1,))
    qkv_t, zg_t, x1 = mix_in_fwd(x, small["mix_norm_g"], win, col(small["b_in"]), "mix_in_fwd", ffn_tail=(s1, wd1))
    ya_t = attn_fwd(qkv_t, small["attn_sinks"], "attn_fwd")
    lng, lnb = col(small["gmlp_ln_g"]), col(small["gmlp_ln_b"])
    w_s, b_s = small["gmlp_w_s"][0], small["gmlp_b_s"][0]
    yg_t = gmlp_fwd(zg_t, lng, lnb, w_s, b_s, "gmlp_fwd")
    gao, ggo = col(small["attn_out_norm_g"]), col(small["gmlp_out_norm_g"])
    x2 = mix_out_fwd(x1, ya_t, yg_t, gao, ggo, wout, small["b_out"], "mix_out_fwd")
    (wg2, wu2, wd2), _ = get_w(("wg2", "wu2", "wd2"), (x2,))
    gs = {}
    dx3, sil2, gp2, s2, h2f, gs["final_norm_g"], loss = ffn_fwd(
        x2, small["ffn2_norm_g"], wg2, wu2, wd2, "ffn2_fwd", head=(target, small["final_norm_g"].reshape(1, -1)))

    dx2, da, db, df, gs["ffn2_norm_g"] = ffn_bwd(x2, dx3, small["ffn2_norm_g"], sil2, gp2, wg2, wu2, wd2, "ffn2_bwd")
    deps = put_g({"wg2": grad_tn(da, h2f, 1.0, "ffn2_dwg"), "wu2": grad_tn(db, h2f, 1.0, "ffn2_dwu"),
                  "wd2": grad_tn(s2, df, 1.0, "ffn2_dwd")})

    dya_t, dyg_t, y_t, dgao, dggo, gs["b_out"] = mix_out_bwd(dx2, ya_t, yg_t, gao, ggo, wout, "mix_out_bwd", deps)
    gs["attn_out_norm_g"], gs["gmlp_out_norm_g"] = dgao.reshape(1, -1), dggo.reshape(1, -1)
    g_wout = grad_nn([y_t], dx2, "dwout")
    dzg_t, dws, dbs, dlng, dlnb = gmlp_bwd(zg_t, dyg_t, lng, lnb, w_s, b_s, "gmlp_bwd")
    gs["gmlp_w_s"], gs["gmlp_b_s"] = dws[None], dbs[None]
    gs["gmlp_ln_g"], gs["gmlp_ln_b"] = dlng.reshape(1, -1), dlnb.reshape(1, -1)
    dq_t, dk_t, dv_t, dsinks = attn_bwd(qkv_t, dya_t, small["attn_sinks"], "attn_bwd")
    gs["attn_sinks"] = dsinks.reshape(1, -1)
    dx1, h2, gs["mix_norm_g"], dbin = mix_in_bwd(x1, dx2, small["mix_norm_g"], dq_t, dk_t, dv_t, dzg_t, win,
                                                 "mix_in_bwd")
    gs["b_in"] = dbin.reshape(1, -1)
    deps = put_g({"wout": g_wout, "win": grad_nn([dq_t, dk_t, dv_t, dzg_t], h2, "dwin")})

    dx0, da, db, df, gs["ffn1_norm_g"] = ffn_bwd(x, dx1, small["ffn1_norm_g"], sil1, gp1, wg1, wu1, wd1, "ffn1_bwd",
                                             deps)
    deps = put_small(gs, loss)
    deps = put_g({"wg1": grad_tn(da, h1, 1.0, "ffn1_dwg", deps)})
    deps = put_g({"wu1": grad_tn(db, h1, 1.0, "ffn1_dwu", deps)})
    put_g({"wd1": grad_tn(s1, df, 1.0, "ffn1_dwd", deps)})
    return dx0


SMALL_NAMES = ["ffn1_norm_g", "mix_norm_g", "b_in", "attn_sinks", "gmlp_ln_g", "gmlp_ln_b", "gmlp_w_s", "gmlp_b_s",
               "attn_out_norm_g", "gmlp_out_norm_g", "b_out", "ffn2_norm_g", "final_norm_g"]
BIG_NAMES = {"wg1": ("ffn1_w_gate", True), "wu1": ("ffn1_w_up", True), "wd1": ("ffn1_w_down", False),
             "win": ("w_in", True), "wout": ("w_out", False),
             "wg2": ("ffn2_w_gate", True), "wu2": ("ffn2_w_up", True), "wd2": ("ffn2_w_down", False)}
WEIGHT_ORDER = ["ffn1_norm_g", "ffn1_w_gate", "ffn1_w_up", "ffn1_w_down", "mix_norm_g", "w_in", "b_in", "attn_sinks",
                "gmlp_ln_g", "gmlp_ln_b", "gmlp_w_s", "gmlp_b_s", "attn_out_norm_g", "gmlp_out_norm_g", "w_out",
                "b_out", "ffn2_norm_g", "ffn2_w_gate", "ffn2_w_up", "ffn2_w_down", "final_norm_g"]


def kernel(x, ffn1_norm_g, ffn1_w_gate, ffn1_w_up, ffn1_w_down, mix_norm_g, w_in, b_in, attn_sinks, gmlp_ln_g, gmlp_ln_b, gmlp_w_s, gmlp_b_s, attn_out_norm_g, gmlp_out_norm_g, w_out, b_out, ffn2_norm_g, ffn2_w_gate, ffn2_w_up, ffn2_w_down, final_norm_g, loss_target, m_ffn1_norm_g, m_ffn1_w_gate, m_ffn1_w_up, m_ffn1_w_down, m_mix_norm_g, m_w_in, m_b_in, m_attn_sinks, m_gmlp_ln_g, m_gmlp_ln_b, m_gmlp_w_s, m_gmlp_b_s, m_attn_out_norm_g, m_gmlp_out_norm_g, m_w_out, m_b_out, m_ffn2_norm_g, m_ffn2_w_gate, m_ffn2_w_up, m_ffn2_w_down, m_final_norm_g, v_ffn1_norm_g, v_ffn1_w_gate, v_ffn1_w_up, v_ffn1_w_down, v_mix_norm_g, v_w_in, v_b_in, v_attn_sinks, v_gmlp_ln_g, v_gmlp_ln_b, v_gmlp_w_s, v_gmlp_b_s, v_attn_out_norm_g, v_gmlp_out_norm_g, v_w_out, v_b_out, v_ffn2_norm_g, v_ffn2_w_gate, v_ffn2_w_up, v_ffn2_w_down, v_final_norm_g):
    args = dict(locals())
    w = {n: args[n] for n in WEIGHT_ORDER}
    m = {n: args["m_" + n] for n in WEIGHT_ORDER}
    v = {n: args["v_" + n] for n in WEIGHT_ORDER}

    d_model = x.shape[-1]
    flat = lambda g: g.reshape(-1, d_model)
    me = _mesh_place()[3].astype(jnp.int32).reshape(1)
    first = ("wg1", "wu1")
    later = (("wd1", "win", "wout"), ("wg2", "wu2", "wd2"))
    gathers, exchanges, last_token = {}, [], []

    order = list(first) + [k for grp in later for k in grp]
    mats = [w[BIG_NAMES[k][0]][0].T if BIG_NAMES[k][1] else w[BIG_NAMES[k][0]][0] for k in order]
    shard_list, land_list = prep_shards(mats, me, "prep_shards")
    shards, zones = dict(zip(order, shard_list)), dict(zip(order, land_list))

    def get_w(keys, after):
        if keys == first:
            full = allgather_rows([shards[k] for k in first], [zones[k] for k in first], "allgather_ffn1")
            started, token = split_start([([shards[k] for k in grp], [zones[k] for k in grp]) for grp in later],
                                         gather_slices, (full[0],), "gather_start")
            gathers.update(zip(later, started))
            return [flat(g) for g in full], (token,)
        _, lands = split_wait(gathers[keys], gather_slices, after, "gather_wait_" + keys[0])
        return [flat(land) for land in lands], ()

    def put_g(gb):
        keys = tuple(gb)
        g3 = [gb[k].reshape(N_DEV, -1, d_model) for k in keys]
        lands = [lax.empty((N_DEV - 1,) + g.shape[1:], BF16) for g in g3]
        started, token = split_start([(g3, lands)], exchange_slices, (), "exchange_start_" + keys[0])
        exchanges.append((keys, started[0]))
        last_token[:] = [token]
        return (token,)

    small_started = []

    def put_small(gs, loss):
        pack = pack_small([gs[n] for n in SMALL_NAMES] + [loss[:, :1]])
        started, token = split_start([([pack], [place_own(pack, me, "place_small")])], gather_slices, (), "small_start")
        small_started[:] = started
        return (token,)

    small = {n: w[n] for n in SMALL_NAMES}
    grad_x = local_step(x[0], loss_target[0], small, get_w, put_g, put_small)

    grads, deltas, new_m, new_v = {}, {}, {}, {}
    shapes = [w[n].shape for n in SMALL_NAMES]
    _, (slots,) = split_wait(small_started[0], gather_slices, tuple(last_token), "small_wait")
    total = sum_slots(slots, "sum_small")
    small_g = unpack_small(total, shapes + [(1, 1)])
    loss_total = small_g[-1].reshape(())
    d_, m_, v_ = adamw_many([w[n] for n in SMALL_NAMES], small_g[:-1], [m[n] for n in SMALL_NAMES],
                            [v[n] for n in SMALL_NAMES], "adamw_small")
    for n, g_, dd, mm, vv in zip(SMALL_NAMES, small_g[:-1], d_, m_, v_):
        grads[n], deltas[n], new_m[n], new_v[n] = g_, dd, mm, vv

    after = tuple(arr for n in SMALL_NAMES for arr in (grads[n], deltas[n], new_m[n], new_v[n]))
    for keys, started in exchanges:
        g3, lands = split_wait(started, exchange_slices, after, "exchange_wait_" + keys[0])
        for key, own, land in zip(keys, g3, lands):
            pname, transposed = BIG_NAMES[key]
            to_rows = (lambda p: p[0].T) if transposed else (lambda p: p[0])
            from_rows = (lambda r: r.T[None]) if transposed else (lambda r: r[None])
            g, d_, m_, v_ = sum_adamw(land, own, me, to_rows(w[pname]), to_rows(m[pname]), to_rows(v[pname]),
                                      "adamw_" + key)
            after = after + (g,)
            grads[pname], deltas[pname], new_m[pname], new_v[pname] = (from_rows(g), from_rows(d_), from_rows(m_),
                                                                        from_rows(v_))

    return (loss_total, grad_x[None], *[grads[n] for n in WEIGHT_ORDER], *[deltas[n] for n in WEIGHT_ORDER],
            *[new_m[n] for n in WEIGHT_ORDER], *[new_v[n] for n in WEIGHT_ORDER])
```

```python
import functools

import jax
import jax.numpy as jnp
from jax import lax
from jax.experimental import pallas as pl
from jax.experimental.pallas import tpu as pltpu

F32 = jnp.float32
BF16 = jnp.bfloat16

N_DEV = 8
N_Q_HEADS = 8
N_KV_HEADS = 2
HEAD_DIM = 64
ATTN_WIDTH = N_Q_HEADS * HEAD_DIM
KV_WIDTH = N_KV_HEADS * HEAD_DIM
QKV_WIDTH = ATTN_WIDTH + 2 * KV_WIDTH
BLOCK = 128
GMLP_GROUPS = 8
GMLP_GROUP_DIM = 64
GMLP_WIDTH = GMLP_GROUPS * GMLP_GROUP_DIM
EPS = 1e-6
FFN_RES = 0.5
ATTN_SCALE = HEAD_DIM ** -0.5
MASK_VALUE = -1e30

ADAM_LR = 0.001
ADAM_B1 = 0.9
ADAM_B2 = 0.999
ADAM_EPS = 1e-08
ADAM_WD = 0.01
ADAM_STEP = 10

V7X_VMEM_BYTES = 64 * 1024 * 1024
VMEM_LIMIT = 62 * 1024 * 1024
LANES = 128
MXU_WIDTH = 256
FFN_BWD_TOKENS = 256

NT = (((1,), (1,)), ((), ()))
TN = (((0,), (0,)), ((), ()))


def _dot(a, b):
    return jnp.dot(a, b, preferred_element_type=F32)


def _dg(a, b, dims):
    return lax.dot_general(a, b, dims, preferred_element_type=F32)


def _params(semantics=None):
    return pltpu.CompilerParams(dimension_semantics=semantics, vmem_limit_bytes=VMEM_LIMIT)


def _resident(shape):
    zeros = (0,) * len(shape)
    return pl.BlockSpec(shape, lambda *_: zeros, pipeline_mode=pl.Buffered(1))


def _drop_deps(body, n_in, n_deps):
    return lambda *refs: body(*refs[:n_in], *refs[n_in + n_deps:])


ANY_SPEC = pl.BlockSpec(memory_space=pl.ANY)


TOKEN_TILE = 512
GRAD_TOKENS_PER_STEP = 2048
GRAD_ROW_TILE_CAP = 1408
N_GATHER_ROLES = 13


def _token_tile(t):
    return min(t, TOKEN_TILE)


def _f_chunk(f):
    for c in (256, 128):
        if f % c == 0:
            return c
    return f


def _loss_head(xv, target, gain):
    r = lax.rsqrt(jnp.mean(xv * xv, axis=-1, keepdims=True) + EPS)
    err = xv * r * gain - target
    loss = 0.5 * jnp.sum(jnp.mean(err * err, axis=-1, keepdims=True), axis=0, keepdims=True)
    dx, dgain = _rmsnorm_bwd_rows(xv, r, gain, err * (1.0 / xv.shape[-1]))
    return dx, dgain, loss


def ffn_fwd(x, gain, wg_t, wu_t, wd, name, deps=(), head=None):
    t, d = x.shape
    f = wg_t.shape[0]
    tm = _token_tile(t)
    fc = _f_chunk(f)
    weights = [wg_t, wu_t] + ([] if wd is None else [wd])
    n_in = 2 + len(weights) + (0 if head is None else 2)
    n_x = 0 if wd is None else 1

    def body(x_ref, g_ref, wg_ref, wu_ref, *rest):
        sil_ref, gp_ref, s_ref, h_ref = rest[n_in - 4 + n_x:n_in + n_x]
        xv = x_ref[...]
        r = lax.rsqrt(jnp.mean(xv * xv, axis=-1, keepdims=True) + EPS)
        h = (xv * r * g_ref[...]).astype(BF16)
        h_ref[...] = h
        for c in range(f // fc):
            sl = slice(c * fc, (c + 1) * fc)
            a = _dg(h, wg_ref[sl, :], NT)
            b = _dg(h, wu_ref[sl, :], NT)
            sig = jax.nn.sigmoid(a)
            sil = a * sig
            sil_ref[:, sl] = sil.astype(BF16)
            gp_ref[:, sl] = (b * (sig + sil - sil * sig)).astype(BF16)
            s_ref[:, sl] = (sil * b).astype(BF16)
        if wd is None:
            return
        wd_ref, xo_ref = rest[0], rest[n_in - 4]
        xo = xv + FFN_RES * _dot(s_ref[...], wd_ref[...])
        if head is None:
            xo_ref[...] = xo
        else:
            t_ref, hg_ref = rest[1:3]
            dg_ref, loss_ref = rest[-2:]

            @pl.when(pl.program_id(0) == 0)
            def _():
                dg_ref[...] = jnp.zeros_like(dg_ref)
                loss_ref[...] = jnp.zeros_like(loss_ref)

            dx, dgain, loss = _loss_head(xo, t_ref[...], hg_ref[...])
            xo_ref[...] = dx
            dg_ref[...] += dgain
            loss_ref[...] += loss

    tok = lambda: pl.BlockSpec((tm, d), lambda i: (i, 0))
    wide = lambda: pl.BlockSpec((tm, f), lambda i: (i, 0))
    const2 = lambda i: (0, 0)
    in_specs = [tok(), _resident((1, d))] + [_resident((f, d)) for _ in weights]
    out_specs = [tok()] * n_x + [wide(), wide(), wide(), tok()]
    out_shape = ([jax.ShapeDtypeStruct((t, d), F32)] * n_x + [jax.ShapeDtypeStruct((t, f), BF16)] * 3
                 + [jax.ShapeDtypeStruct((t, d), BF16)])
    operands = [x, gain] + weights
    if head is not None:
        in_specs += [tok(), _resident((1, d))]
        out_specs += [pl.BlockSpec((1, d), const2), pl.BlockSpec((1, LANES), const2)]
        out_shape += [jax.ShapeDtypeStruct((1, d), F32), jax.ShapeDtypeStruct((1, LANES), F32)]
        operands += list(head)
    return pl.pallas_call(
        _drop_deps(body, n_in, len(deps)), name=name, grid=(t // tm,),
        in_specs=in_specs + [ANY_SPEC] * len(deps), out_specs=out_specs, out_shape=out_shape,
        compiler_params=_params(("arbitrary",)),
    )(*operands, *deps)


def ffn_gate(x, gain, wg_t, name, deps=()):
    t, d = x.shape
    f = wg_t.shape[0]
    tm = _token_tile(t)
    fc = _f_chunk(f)

    def body(x_ref, g_ref, wg_ref, a_ref, h_ref):
        xv = x_ref[...]
        r = lax.rsqrt(jnp.mean(xv * xv, axis=-1, keepdims=True) + EPS)
        h = (xv * r * g_ref[...]).astype(BF16)
        h_ref[...] = h
        for c in range(f // fc):
            sl = slice(c * fc, (c + 1) * fc)
            a_ref[:, sl] = _dg(h, wg_ref[sl, :], NT).astype(BF16)

    tok = lambda: pl.BlockSpec((tm, d), lambda i: (i, 0))
    return pl.pallas_call(
        _drop_deps(body, 3, len(deps)), name=name, grid=(t // tm,),
        in_specs=[tok(), _resident((1, d)), _resident((f, d))] + [ANY_SPEC] * len(deps),
        out_specs=[pl.BlockSpec((tm, f), lambda i: (i, 0)), tok()],
        out_shape=[jax.ShapeDtypeStruct((t, f), BF16), jax.ShapeDtypeStruct((t, d), BF16)],
        compiler_params=_params(("arbitrary",)),
    )(x, gain, wg_t, *deps)


def ffn_up(h, a, wu_t, name):
    t, d = h.shape
    f = wu_t.shape[0]
    tm = _token_tile(t)
    fc = _f_chunk(f)

    def body(h_ref, a_ref, wu_ref, sil_ref, gp_ref, s_ref):
        hv = h_ref[...]
        for c in range(f // fc):
            sl = slice(c * fc, (c + 1) * fc)
            av = a_ref[:, sl].astype(F32)
            b = _dg(hv, wu_ref[sl, :], NT)
            sig = jax.nn.sigmoid(av)
            sil = av * sig
            sil_ref[:, sl] = sil.astype(BF16)
            gp_ref[:, sl] = (b * (sig + sil - sil * sig)).astype(BF16)
            s_ref[:, sl] = (sil * b).astype(BF16)

    wide = lambda: pl.BlockSpec((tm, f), lambda i: (i, 0))
    return pl.pallas_call(
        body, name=name, grid=(t // tm,),
        in_specs=[pl.BlockSpec((tm, d), lambda i: (i, 0)), wide(), _resident((f, d))],
        out_specs=[wide(), wide(), wide()],
        out_shape=[jax.ShapeDtypeStruct((t, f), BF16)] * 3,
        compiler_params=_params(("arbitrary",)),
    )(h, a, wu_t)


def _rmsnorm_bwd_rows(xv, r, gain, dh):
    n = xv * r
    dgain = jnp.sum(dh * n, axis=0, keepdims=True)
    dn = dh * gain
    dx = r * (dn - n * jnp.mean(dn * n, axis=-1, keepdims=True))
    return dx, dgain


def ffn_bwd(x, dxo, gain, sil, gp, wg_t, wu_t, wd, name, deps=()):
    t, d = x.shape
    f = wd.shape[0]
    tm = min(t, FFN_BWD_TOKENS)
    nt = t // tm
    chunks = [(off, min(MXU_WIDTH, f - off)) for off in range(0, f, MXU_WIDTH)]

    def body(xp_ref, dxp_ref, dxo_ref, g_ref, sil_ref, gp_ref, wg_ref, wu_ref, wd_ref,
             dx_ref, da_ref, db_ref, df_ref, dg_ref, dh_ref):
        i = pl.program_id(0)
        gain_v = g_ref[...]

        @pl.when(i == 0)
        def _():
            dh_ref[...] = jnp.zeros_like(dh_ref)
            dg_ref[...] = jnp.zeros_like(dg_ref)

        def finish_previous(slot):
            xv = xp_ref[...]
            r = lax.rsqrt(jnp.mean(xv * xv, axis=-1, keepdims=True) + EPS)
            dx, dgain = _rmsnorm_bwd_rows(xv, r, gain_v, dh_ref[slot])
            dx_ref[...] = dxp_ref[...] + dx
            dg_ref[...] += dgain

        @pl.when(i < nt)
        def _():
            df = (FFN_RES * dxo_ref[...]).astype(BF16)
            df_ref[...] = df
            for off, size in chunks:
                sl = slice(off, off + size)
                ds = _dg(df, wd_ref[sl, :], NT)
                da_ref[:, sl] = (ds * gp_ref[:, sl].astype(F32)).astype(BF16)
                db_ref[:, sl] = (ds * sil_ref[:, sl].astype(F32)).astype(BF16)
            dh = _dot(da_ref[...], wg_ref[...]) + _dot(db_ref[...], wu_ref[...])
            finish_previous((i + 1) % 2)
            dh_ref[i % 2] = dh

        @pl.when(i == nt)
        def _():
            finish_previous((nt - 1) % 2)

    prev = lambda i: (jnp.maximum(i - 1, 0), 0)
    cur = lambda i: (jnp.minimum(i, nt - 1), 0)
    return pl.pallas_call(
        _drop_deps(body, 9, len(deps)), name=name, grid=(nt + 1,),
        in_specs=[pl.BlockSpec((tm, d), prev), pl.BlockSpec((tm, d), prev), pl.BlockSpec((tm, d), cur),
                  _resident((1, d)), pl.BlockSpec((tm, f), cur), pl.BlockSpec((tm, f), cur),
                  _resident((f, d)), _resident((f, d)), _resident((f, d))] + [ANY_SPEC] * len(deps),
        out_specs=[pl.BlockSpec((tm, d), prev), pl.BlockSpec((tm, f), cur), pl.BlockSpec((tm, f), cur),
                   pl.BlockSpec((tm, d), cur), pl.BlockSpec((1, d), lambda i: (0, 0))],
        out_shape=[jax.ShapeDtypeStruct((t, d), F32), jax.ShapeDtypeStruct((t, f), BF16),
                   jax.ShapeDtypeStruct((t, f), BF16), jax.ShapeDtypeStruct((t, d), BF16),
                   jax.ShapeDtypeStruct((1, d), F32)],
        scratch_shapes=[pltpu.VMEM((2, tm, d), F32)],
        compiler_params=_params(("arbitrary",)),
    )(x, dxo, dxo, gain, sil, gp, wg_t, wu_t, wd, *deps)


def _row_tile(m, cap):
    if m <= cap:
        return m
    best = None
    for bm in range(LANES, cap + 1, LANES):
        if m % bm == 0:
            best = bm
    return best if best is not None else m


def grad_tn(a, b, scale, name, deps=()):
    t, m = a.shape
    n = b.shape[1]
    bm = _row_tile(m, GRAD_ROW_TILE_CAP)
    bk = min(t, GRAD_TOKENS_PER_STEP)
    nk = t // bk

    def body(a_ref, b_ref, o_ref, acc_ref):
        k = pl.program_id(1)

        @pl.when(k == 0)
        def _():
            acc_ref[...] = jnp.zeros_like(acc_ref)

        acc_ref[...] += _dg(a_ref[...].astype(BF16), b_ref[...].astype(BF16), TN)

        @pl.when(k == nk - 1)
        def _():
            o_ref[...] = (scale * acc_ref[...]).astype(BF16)

    return pl.pallas_call(
        _drop_deps(body, 2, len(deps)), name=name, grid=(m // bm, nk),
        in_specs=[pl.BlockSpec((bk, bm), lambda i, k: (k, i)), pl.BlockSpec((bk, n), lambda i, k: (k, 0))]
        + [ANY_SPEC] * len(deps),
        out_specs=pl.BlockSpec((bm, n), lambda i, k: (i, 0)),
        out_shape=jax.ShapeDtypeStruct((m, n), BF16),
        scratch_shapes=[pltpu.VMEM((bm, n), F32)],
        compiler_params=_params(("arbitrary", "arbitrary")),
    )(a, b, *deps)


def grad_nn(a_list, b, name):
    t, n = b.shape
    ms = [a.shape[0] for a in a_list]
    m = sum(ms)
    bk = min(t, GRAD_TOKENS_PER_STEP)
    nk = t // bk
    na = len(a_list)

    def body(*refs):
        a_refs, b_ref, o_ref, acc_ref = refs[:na], refs[na], refs[na + 1], refs[na + 2]
        k = pl.program_id(0)

        @pl.when(k == 0)
        def _():
            acc_ref[...] = jnp.zeros_like(acc_ref)

        bv = b_ref[...].astype(BF16)
        off = 0
        for a_ref, mi in zip(a_refs, ms):
            acc_ref[off:off + mi, :] += _dot(a_ref[...].astype(BF16), bv)
            off += mi

        @pl.when(k == nk - 1)
        def _():
            o_ref[...] = acc_ref[...].astype(BF16)

    return pl.pallas_call(
        body, name=name, grid=(nk,),
        in_specs=[pl.BlockSpec((mi, bk), lambda k: (0, k)) for mi in ms] + [pl.BlockSpec((bk, n), lambda k: (k, 0))],
        out_specs=pl.BlockSpec((m, n), lambda k: (0, 0)),
        out_shape=jax.ShapeDtypeStruct((m, n), BF16),
        scratch_shapes=[pltpu.VMEM((m, n), F32)],
        compiler_params=_params(("arbitrary",)),
    )(*a_list, b)


def mix_in_fwd(x, gain, win_t, b_in_col, name, ffn_tail=None):
    t, d = x.shape
    w = win_t.shape[0]
    tm = _token_tile(t)

    def body(x_ref, g_ref, w_ref, bias_ref, *rest):
        xv = x_ref[...]
        if ffn_tail is not None:
            s_ref, wd_ref, xo_ref = rest[0], rest[1], rest[4]
            xv = xv + FFN_RES * _dot(s_ref[...], wd_ref[...])
            xo_ref[...] = xv
        qkv_ref, zg_ref = rest[-3:-1] if ffn_tail is not None else rest
        r = lax.rsqrt(jnp.mean(xv * xv, axis=-1, keepdims=True) + EPS)
        h = (xv * r * g_ref[...]).astype(BF16)
        qkv_ref[...] = (_dg(w_ref[:QKV_WIDTH, :], h, NT) + bias_ref[:QKV_WIDTH, :]).astype(BF16)
        zg_ref[...] = (_dg(w_ref[QKV_WIDTH:, :], h, NT) + bias_ref[QKV_WIDTH:, :]).astype(BF16)

    tok = lambda: pl.BlockSpec((tm, d), lambda i: (i, 0))
    in_specs = [tok(), _resident((1, d)), _resident((w, d)), _resident((w, 1))]
    out_specs = [pl.BlockSpec((QKV_WIDTH, tm), lambda i: (0, i)), pl.BlockSpec((w - QKV_WIDTH, tm), lambda i: (0, i))]
    out_shape = [jax.ShapeDtypeStruct((QKV_WIDTH, t), BF16), jax.ShapeDtypeStruct((w - QKV_WIDTH, t), BF16)]
    operands = [x, gain, win_t, b_in_col]
    if ffn_tail is not None:
        f = ffn_tail[1].shape[0]
        in_specs += [pl.BlockSpec((tm, f), lambda i: (i, 0)), _resident((f, d))]
        out_specs.append(tok())
        out_shape.append(jax.ShapeDtypeStruct((t, d), F32))
        operands += list(ffn_tail)
    return pl.pallas_call(
        body, name=name, grid=(t // tm,), in_specs=in_specs, out_specs=out_specs, out_shape=out_shape,
        compiler_params=_params(("arbitrary",)),
    )(*operands)


ATTN_REP = N_Q_HEADS // N_KV_HEADS
ATTN_BLOCKS_PER_STEP = 8


def _attn_tiles(t):
    nbs = min(ATTN_BLOCKS_PER_STEP, t // BLOCK)
    return nbs, nbs * BLOCK, t // (nbs * BLOCK)


def _attn_specs(nbs, tiles):
    last = tiles - 1
    cur = lambda n: jnp.minimum(n, last)
    prev = lambda n: jnp.maximum(jnp.minimum(n, last) * nbs - 1, 0)
    k_row = ATTN_WIDTH // KV_WIDTH
    tw = nbs * BLOCK
    return [
        pl.BlockSpec((ATTN_WIDTH, tw), lambda n: (0, cur(n))),
        pl.BlockSpec((KV_WIDTH, BLOCK), lambda n: (k_row, prev(n))),
        pl.BlockSpec((KV_WIDTH, tw), lambda n: (k_row, cur(n))),
        pl.BlockSpec((KV_WIDTH, BLOCK), lambda n: (k_row + 1, prev(n))),
        pl.BlockSpec((KV_WIDTH, tw), lambda n: (k_row + 1, cur(n))),
    ]


def _cols(b):
    return slice(b * BLOCK, (b + 1) * BLOCK)


def _band_rows(prev_ref, tile_ref, gs, b):
    before = prev_ref[gs, :] if b == 0 else tile_ref[gs, _cols(b - 1)]
    return before, tile_ref[gs, _cols(b)]


def _group_rows(ref, g, b):
    first = g * ATTN_REP
    return jnp.concatenate([ref[(first + r) * HEAD_DIM:(first + r + 1) * HEAD_DIM, _cols(b)] for r in range(ATTN_REP)],
                           axis=1)


def _ungroup_rows(ref, g, b, val):
    first = g * ATTN_REP
    for r in range(ATTN_REP):
        ref[(first + r) * HEAD_DIM:(first + r + 1) * HEAD_DIM, _cols(b)] = val[:, r * BLOCK:(r + 1) * BLOCK]


def _attn_upper():
    key = lax.broadcasted_iota(jnp.int32, (BLOCK, ATTN_REP * BLOCK), 0)
    qry = lax.broadcasted_iota(jnp.int32, (BLOCK, ATTN_REP * BLOCK), 1) & (BLOCK - 1)
    return key > qry


def _attn_probs(q, kp, kc, sink_ref, g, upper, no_prev):
    s = jnp.where(upper, _dg(kp, q, TN), _dg(kc, q, TN)) * ATTN_SCALE
    if no_prev is not None:
        s = jnp.where(upper & no_prev, MASK_VALUE, s)
    sink = jnp.concatenate([jnp.full((1, BLOCK), sink_ref[0, g * ATTN_REP + r], F32) for r in range(ATTN_REP)], axis=1)
    m = jnp.maximum(jnp.max(s, axis=0, keepdims=True), sink)
    e = jnp.exp(s - m)
    es = jnp.exp(sink - m)
    inv = 1.0 / (jnp.sum(e, axis=0, keepdims=True) + es)
    return e * inv, es * inv


def _split_band(upper, val):
    return jnp.where(upper, val, 0.0).astype(BF16), jnp.where(upper, 0.0, val).astype(BF16)


def attn_fwd(qkv_t, sinks, name):
    t = qkv_t.shape[1]
    nbs, tw, tiles = _attn_tiles(t)

    def body(sink_ref, q_ref, kp_ref, kc_ref, vp_ref, vc_ref, y_ref):
        n = pl.program_id(0)
        upper = _attn_upper()
        for b in range(nbs):
            for g in range(N_KV_HEADS):
                gs = slice(g * HEAD_DIM, (g + 1) * HEAD_DIM)
                kp, kc = _band_rows(kp_ref, kc_ref, gs, b)
                vp, vc = _band_rows(vp_ref, vc_ref, gs, b)
                p, _ = _attn_probs(_group_rows(q_ref, g, b), kp, kc, sink_ref, g, upper, n == 0 if b == 0 else None)
                pp, pc = _split_band(upper, p)
                _ungroup_rows(y_ref, g, b, (_dot(vp, pp) + _dot(vc, pc)).astype(BF16))

    return pl.pallas_call(
        body, name=name, grid=(tiles,),
        in_specs=[pl.BlockSpec(memory_space=pltpu.SMEM)] + _attn_specs(nbs, tiles),
        out_specs=pl.BlockSpec((ATTN_WIDTH, tw), lambda n: (0, n)),
        out_shape=jax.ShapeDtypeStruct((ATTN_WIDTH, t), BF16),
        compiler_params=_params(("arbitrary",)),
    )(sinks, qkv_t, qkv_t, qkv_t, qkv_t, qkv_t)


def attn_bwd(qkv_t, dy_t, sinks, name):
    t = qkv_t.shape[1]
    nbs, tw, tiles = _attn_tiles(t)
    kept = slice(0, tw - BLOCK)

    def body(sink_ref, q_ref, kp_ref, kc_ref, vp_ref, vc_ref, do_ref, dq_ref, dk_ref, dv_ref, dsink_ref,
             ck_ref, cv_ref, sacc_ref):
        n = pl.program_id(0)

        @pl.when(n == 0)
        def _():
            sacc_ref[...] = jnp.zeros_like(sacc_ref)

        @pl.when((n > 0) & (n < tiles))
        def _():
            if nbs > 1:
                dk_ref[:, kept] = ck_ref[:, kept].astype(BF16)
                dv_ref[:, kept] = cv_ref[:, kept].astype(BF16)

        @pl.when(n < tiles)
        def _():
            upper = _attn_upper()
            for b in range(nbs):
                for g in range(N_KV_HEADS):
                    gs = slice(g * HEAD_DIM, (g + 1) * HEAD_DIM)
                    kp, kc = _band_rows(kp_ref, kc_ref, gs, b)
                    vp, vc = _band_rows(vp_ref, vc_ref, gs, b)
                    q = _group_rows(q_ref, g, b)
                    do = _group_rows(do_ref, g, b)
                    p, ps = _attn_probs(q, kp, kc, sink_ref, g, upper, n == 0 if b == 0 else None)
                    dp = jnp.where(upper, _dg(vp, do, TN), _dg(vc, do, TN))
                    delta = jnp.sum(p * dp, axis=0, keepdims=True)
                    dsink = -(ps * delta)
                    for r in range(ATTN_REP):
                        hd = g * ATTN_REP + r
                        sacc_ref[hd:hd + 1, :] += dsink[:, r * BLOCK:(r + 1) * BLOCK]
                    dsp, dsc = _split_band(upper, p * (dp - delta))
                    pp, pc = _split_band(upper, p)
                    _ungroup_rows(dq_ref, g, b, (ATTN_SCALE * (_dot(kp, dsp) + _dot(kc, dsc))).astype(BF16))
                    dk_before = ATTN_SCALE * _dg(q, dsp, NT)
                    dv_before = _dg(do, pp, NT)
                    if b == 0:
                        @pl.when(n > 0)
                        def _():
                            dk_ref[gs, _cols(nbs - 1)] = (ck_ref[gs, _cols(nbs - 1)] + dk_before).astype(BF16)
                            dv_ref[gs, _cols(nbs - 1)] = (cv_ref[gs, _cols(nbs - 1)] + dv_before).astype(BF16)
                    else:
                        ck_ref[gs, _cols(b - 1)] += dk_before
                        cv_ref[gs, _cols(b - 1)] += dv_before
                    ck_ref[gs, _cols(b)] = ATTN_SCALE * _dg(q, dsc, NT)
                    cv_ref[gs, _cols(b)] = _dg(do, pc, NT)

        @pl.when(n == tiles)
        def _():
            dk_ref[...] = ck_ref[...].astype(BF16)
            dv_ref[...] = cv_ref[...].astype(BF16)
            dsink_ref[...] = jnp.sum(sacc_ref[...], axis=1, keepdims=True)

    last = tiles - 1
    done = lambda n: jnp.maximum(n - 1, 0)
    outs = pl.pallas_call(
        body, name=name, grid=(tiles + 1,),
        in_specs=[pl.BlockSpec(memory_space=pltpu.SMEM)] + _attn_specs(nbs, tiles)
        + [pl.BlockSpec((ATTN_WIDTH, tw), lambda n: (0, jnp.minimum(n, last)))],
        out_specs=[pl.BlockSpec((ATTN_WIDTH, tw), lambda n: (0, jnp.minimum(n, last))),
                   pl.BlockSpec((KV_WIDTH, tw), lambda n: (0, done(n))),
                   pl.BlockSpec((KV_WIDTH, tw), lambda n: (0, done(n))),
                   pl.BlockSpec((N_Q_HEADS, 1), lambda n: (0, 0))],
        out_shape=[jax.ShapeDtypeStruct((ATTN_WIDTH, t), BF16), jax.ShapeDtypeStruct((KV_WIDTH, t), BF16),
                   jax.ShapeDtypeStruct((KV_WIDTH, t), BF16), jax.ShapeDtypeStruct((N_Q_HEADS, 1), F32)],
        scratch_shapes=[pltpu.VMEM((KV_WIDTH, tw), F32), pltpu.VMEM((KV_WIDTH, tw), F32),
                        pltpu.VMEM((N_Q_HEADS, BLOCK), F32)],
        compiler_params=_params(("arbitrary",)),
    )(sinks, qkv_t, qkv_t, qkv_t, qkv_t, qkv_t, dy_t)
    return outs


GELU_C = 0.7978845608028654
GELU_A = 0.044715


def _gelu(x):
    return 0.5 * x * (1.0 + jnp.tanh(GELU_C * (x + GELU_A * x * x * x)))


def _gelu_grad(x):
    th = jnp.tanh(GELU_C * (x + GELU_A * x * x * x))
    return 0.5 * (1.0 + th) + 0.5 * x * (1.0 - th * th) * GELU_C * (1.0 + 3.0 * GELU_A * x * x)


def _gmlp_parts(zg, lng, lnb):
    z = _gelu(zg)
    u = z[:GMLP_WIDTH, :]
    vv = z[GMLP_WIDTH:, :]
    mu = jnp.mean(vv, axis=0, keepdims=True)
    xc = vv - mu
    rstd = lax.rsqrt(jnp.mean(xc * xc, axis=0, keepdims=True) + EPS)
    xhat = xc * rstd
    return u, xhat, rstd, xhat * lng + lnb


def _causal(w):
    row = lax.broadcasted_iota(jnp.int32, (BLOCK, BLOCK), 0)
    col = lax.broadcasted_iota(jnp.int32, (BLOCK, BLOCK), 1)
    return jnp.where(col <= row, w, 0.0)


GMLP_CHUNKS_PER_STEP = 4


def _stack_chunks(v, nc):
    return jnp.concatenate([v[:, c * BLOCK:(c + 1) * BLOCK] for c in range(nc)], axis=0)


def _unstack_chunks(v, nc):
    rows = v.shape[0] // nc
    return jnp.concatenate([v[c * rows:(c + 1) * rows, :] for c in range(nc)], axis=1)


def gmlp_fwd(zg_t, lng, lnb, w_s, b_s, name):
    t = zg_t.shape[1]
    nc = min(GMLP_CHUNKS_PER_STEP, t // BLOCK)
    tw = nc * BLOCK

    def body(zg_ref, lng_ref, lnb_ref, w_ref, bs_ref, y_ref):
        u, _, _, vn = _gmlp_parts(zg_ref[...].astype(F32), lng_ref[...], lnb_ref[...])
        for g in range(GMLP_GROUPS):
            gs = slice(g * GMLP_GROUP_DIM, (g + 1) * GMLP_GROUP_DIM)
            wc = _causal(w_ref[g]).astype(BF16)
            mixed = _dg(_stack_chunks(vn[gs, :].astype(BF16), nc), wc, NT) + bs_ref[g:g + 1, :]
            y_ref[gs, :] = (u[gs, :] * _unstack_chunks(mixed, nc)).astype(BF16)

    return pl.pallas_call(
        body, name=name, grid=(t // tw,),
        in_specs=[pl.BlockSpec((2 * GMLP_WIDTH, tw), lambda n: (0, n)), _resident((GMLP_WIDTH, 1)),
                  _resident((GMLP_WIDTH, 1)), _resident((GMLP_GROUPS, BLOCK, BLOCK)), _resident((GMLP_GROUPS, BLOCK))],
        out_specs=pl.BlockSpec((GMLP_WIDTH, tw), lambda n: (0, n)),
        out_shape=jax.ShapeDtypeStruct((GMLP_WIDTH, t), BF16),
        compiler_params=_params(("arbitrary",)),
    )(zg_t, lng, lnb, w_s, b_s)


def gmlp_bwd(zg_t, dy_t, lng, lnb, w_s, b_s, name):
    t = zg_t.shape[1]
    nc = min(GMLP_CHUNKS_PER_STEP, t // BLOCK)
    tw = nc * BLOCK
    nt = t // tw

    def body(zg_ref, dy_ref, lng_ref, lnb_ref, w_ref, bs_ref, dzg_ref, dw_ref, dbs_ref, dlng_ref, dlnb_ref,
             gacc_ref, bacc_ref):
        n = pl.program_id(0)

        @pl.when(n == 0)
        def _():
            dw_ref[...] = jnp.zeros_like(dw_ref)
            dbs_ref[...] = jnp.zeros_like(dbs_ref)
            gacc_ref[...] = jnp.zeros_like(gacc_ref)
            bacc_ref[...] = jnp.zeros_like(bacc_ref)

        zg = zg_ref[...].astype(F32)
        lng_v = lng_ref[...]
        u, xhat, rstd, vn = _gmlp_parts(zg, lng_v, lnb_ref[...])
        dy = dy_ref[...].astype(F32)
        dvn_parts = []
        for g in range(GMLP_GROUPS):
            gs = slice(g * GMLP_GROUP_DIM, (g + 1) * GMLP_GROUP_DIM)
            wc = _causal(w_ref[g]).astype(BF16)
            vn_s = _stack_chunks(vn[gs, :].astype(BF16), nc)
            mixed = _unstack_chunks(_dg(vn_s, wc, NT) + bs_ref[g:g + 1, :], nc)
            dy_g = dy[gs, :]
            dmixed = dy_g * u[gs, :]
            dm_s = _stack_chunks(dmixed.astype(BF16), nc)
            dzg_ref[gs, :] = (dy_g * mixed * _gelu_grad(zg[gs, :])).astype(BF16)
            dw_ref[g] += _causal(_dg(dm_s, vn_s, TN))
            dbs_ref[g:g + 1, :] += _lane_fold(jnp.sum(dmixed, axis=0, keepdims=True))
            dvn_parts.append(_unstack_chunks(_dot(dm_s, wc), nc))
        dvn = jnp.concatenate(dvn_parts, axis=0)
        gacc_ref[...] += _lane_fold(dvn * xhat)
        bacc_ref[...] += _lane_fold(dvn)
        dxhat = dvn * lng_v
        m1 = jnp.mean(dxhat, axis=0, keepdims=True)
        m2 = jnp.mean(dxhat * xhat, axis=0, keepdims=True)
        dvv = rstd * (dxhat - m1 - xhat * m2)
        dzg_ref[GMLP_WIDTH:, :] = (dvv * _gelu_grad(zg[GMLP_WIDTH:, :])).astype(BF16)

        @pl.when(n == nt - 1)
        def _():
            dlng_ref[...] = jnp.sum(gacc_ref[...], axis=1, keepdims=True)
            dlnb_ref[...] = jnp.sum(bacc_ref[...], axis=1, keepdims=True)

    const2 = lambda n: (0, 0)
    return pl.pallas_call(
        body, name=name, grid=(nt,),
        in_specs=[pl.BlockSpec((2 * GMLP_WIDTH, tw), lambda n: (0, n)), pl.BlockSpec((GMLP_WIDTH, tw), lambda n: (0, n)),
                  _resident((GMLP_WIDTH, 1)), _resident((GMLP_WIDTH, 1)),
                  _resident((GMLP_GROUPS, BLOCK, BLOCK)), _resident((GMLP_GROUPS, BLOCK))],
        out_specs=[pl.BlockSpec((2 * GMLP_WIDTH, tw), lambda n: (0, n)),
                   pl.BlockSpec((GMLP_GROUPS, BLOCK, BLOCK), lambda n: (0, 0, 0)),
                   pl.BlockSpec((GMLP_GROUPS, BLOCK), const2),
                   pl.BlockSpec((GMLP_WIDTH, 1), const2), pl.BlockSpec((GMLP_WIDTH, 1), const2)],
        out_shape=[jax.ShapeDtypeStruct((2 * GMLP_WIDTH, t), BF16),
                   jax.ShapeDtypeStruct((GMLP_GROUPS, BLOCK, BLOCK), F32),
                   jax.ShapeDtypeStruct((GMLP_GROUPS, BLOCK), F32),
                   jax.ShapeDtypeStruct((GMLP_WIDTH, 1), F32), jax.ShapeDtypeStruct((GMLP_WIDTH, 1), F32)],
        scratch_shapes=[pltpu.VMEM((GMLP_WIDTH, BLOCK), F32), pltpu.VMEM((GMLP_WIDTH, BLOCK), F32)],
        compiler_params=_params(("arbitrary",)),
    )(zg_t, dy_t, lng, lnb, w_s, b_s)


def _rmsnorm_cols(y, gain_col):
    r = lax.rsqrt(jnp.mean(y * y, axis=0, keepdims=True) + EPS)
    n = y * r
    return n, r, n * gain_col


def mix_out_fwd(x, ya_t, yg_t, gao, ggo, wout, b_out, name):
    t, d = x.shape
    tm = _token_tile(t)

    def body(x_ref, ya_ref, yg_ref, gao_ref, ggo_ref, w_ref, b_ref, o_ref):
        _, _, ya = _rmsnorm_cols(ya_ref[...].astype(F32), gao_ref[...])
        _, _, yg = _rmsnorm_cols(yg_ref[...].astype(F32), ggo_ref[...])
        o_ref[...] = (x_ref[...] + _dg(ya.astype(BF16), w_ref[:ATTN_WIDTH, :], TN)
                      + _dg(yg.astype(BF16), w_ref[ATTN_WIDTH:, :], TN) + b_ref[...])

    return pl.pallas_call(
        body, name=name, grid=(t // tm,),
        in_specs=[pl.BlockSpec((tm, d), lambda i: (i, 0)), pl.BlockSpec((ATTN_WIDTH, tm), lambda i: (0, i)),
                  pl.BlockSpec((GMLP_WIDTH, tm), lambda i: (0, i)), _resident((ATTN_WIDTH, 1)), _resident((GMLP_WIDTH, 1)),
                  _resident((ATTN_WIDTH + GMLP_WIDTH, d)), _resident((1, d))],
        out_specs=pl.BlockSpec((tm, d), lambda i: (i, 0)),
        out_shape=jax.ShapeDtypeStruct((t, d), F32),
        compiler_params=_params(("arbitrary",)),
    )(x, ya_t, yg_t, gao, ggo, wout, b_out)


def _lane_fold(v):
    out = v[:, :LANES]
    for j in range(1, v.shape[1] // LANES):
        out = out + v[:, j * LANES:(j + 1) * LANES]
    return out


def mix_out_bwd(dx, ya_t, yg_t, gao, ggo, wout, name, deps=()):
    t, d = dx.shape
    tm = _token_tile(t)
    nt = t // tm

    def body(dx_ref, ya_ref, yg_ref, gao_ref, ggo_ref, w_ref, dya_ref, dyg_ref, y_ref, dgao_ref, dggo_ref, db_ref,
             acc_ref):
        i = pl.program_id(0)

        @pl.when(i == 0)
        def _():
            acc_ref[...] = jnp.zeros_like(acc_ref)
            db_ref[...] = jnp.zeros_like(db_ref)

        dxv = dx_ref[...]
        db_ref[...] += jnp.sum(dxv, axis=0, keepdims=True)
        dxb = dxv.astype(BF16)
        for part, (src_ref, gain_ref, dst_ref) in enumerate(((ya_ref, gao_ref, dya_ref), (yg_ref, ggo_ref, dyg_ref))):
            rows = slice(part * ATTN_WIDTH, (part + 1) * ATTN_WIDTH)
            gain = gain_ref[...]
            nrm, r, yn = _rmsnorm_cols(src_ref[...].astype(F32), gain)
            y_ref[rows, :] = yn.astype(BF16)
            dy = _dg(w_ref[rows, :], dxb, NT)
            acc_ref[rows, :] += _lane_fold(dy * nrm)
            dn = dy * gain
            dst_ref[...] = (r * (dn - nrm * jnp.mean(dn * nrm, axis=0, keepdims=True))).astype(BF16)

        @pl.when(i == nt - 1)
        def _():
            dgao_ref[...] = jnp.sum(acc_ref[:ATTN_WIDTH, :], axis=1, keepdims=True)
            dggo_ref[...] = jnp.sum(acc_ref[ATTN_WIDTH:, :], axis=1, keepdims=True)

    const2 = lambda i: (0, 0)
    feat = lambda w: pl.BlockSpec((w, tm), lambda i: (0, i))
    return pl.pallas_call(
        _drop_deps(body, 6, len(deps)), name=name, grid=(nt,),
        in_specs=[pl.BlockSpec((tm, d), lambda i: (i, 0)), feat(ATTN_WIDTH), feat(GMLP_WIDTH),
                  _resident((ATTN_WIDTH, 1)), _resident((GMLP_WIDTH, 1)), _resident((ATTN_WIDTH + GMLP_WIDTH, d))]
        + [ANY_SPEC] * len(deps),
        out_specs=[feat(ATTN_WIDTH), feat(GMLP_WIDTH), feat(ATTN_WIDTH + GMLP_WIDTH),
                   pl.BlockSpec((ATTN_WIDTH, 1), const2), pl.BlockSpec((GMLP_WIDTH, 1), const2),
                   pl.BlockSpec((1, d), const2)],
        out_shape=[jax.ShapeDtypeStruct((ATTN_WIDTH, t), BF16), jax.ShapeDtypeStruct((GMLP_WIDTH, t), BF16),
                   jax.ShapeDtypeStruct((ATTN_WIDTH + GMLP_WIDTH, t), BF16),
                   jax.ShapeDtypeStruct((ATTN_WIDTH, 1), F32), jax.ShapeDtypeStruct((GMLP_WIDTH, 1), F32),
                   jax.ShapeDtypeStruct((1, d), F32)],
        scratch_shapes=[pltpu.VMEM((ATTN_WIDTH + GMLP_WIDTH, LANES), F32)],
        compiler_params=_params(("arbitrary",)),
    )(dx, ya_t, yg_t, gao, ggo, wout, *deps)


def mix_in_bwd(x, dxo, gain, dq_t, dk_t, dv_t, dzg_t, win_t, name):
    t, d = x.shape
    w = win_t.shape[0]
    tm = _token_tile(t)
    nt = t // tm
    parts = ((0, ATTN_WIDTH), (ATTN_WIDTH, KV_WIDTH), (ATTN_WIDTH + KV_WIDTH, KV_WIDTH), (QKV_WIDTH, w - QKV_WIDTH))

    def body(x_ref, dxo_ref, g_ref, dq_ref, dk_ref, dv_ref, dzg_ref, w_ref, dx_ref, h_ref, dg_ref, db_ref, acc_ref):
        i = pl.program_id(0)

        @pl.when(i == 0)
        def _():
            acc_ref[...] = jnp.zeros_like(acc_ref)
            dg_ref[...] = jnp.zeros_like(dg_ref)

        xv = x_ref[...]
        gain_v = g_ref[...]
        r = lax.rsqrt(jnp.mean(xv * xv, axis=-1, keepdims=True) + EPS)
        h_ref[...] = (xv * r * gain_v).astype(BF16)
        dh = jnp.zeros((tm, d), F32)
        for (off, size), src_ref in zip(parts, (dq_ref, dk_ref, dv_ref, dzg_ref)):
            dp = src_ref[...]
            acc_ref[off:off + size, :] += _lane_fold(dp.astype(F32))
            dh = dh + _dg(dp, w_ref[off:off + size, :], TN)
        dx, dgain = _rmsnorm_bwd_rows(xv, r, gain_v, dh)
        dx_ref[...] = dxo_ref[...] + dx
        dg_ref[...] += dgain

        @pl.when(i == nt - 1)
        def _():
            db_ref[...] = jnp.sum(acc_ref[...], axis=1, keepdims=True)

    const2 = lambda i: (0, 0)
    tok = lambda: pl.BlockSpec((tm, d), lambda i: (i, 0))
    feat = lambda rows: pl.BlockSpec((rows, tm), lambda i: (0, i))
    return pl.pallas_call(
        body, name=name, grid=(nt,),
        in_specs=[tok(), tok(), _resident((1, d)), feat(ATTN_WIDTH), feat(KV_WIDTH), feat(KV_WIDTH),
                  feat(w - QKV_WIDTH), _resident((w, d))],
        out_specs=[tok(), tok(), pl.BlockSpec((1, d), const2), pl.BlockSpec((w, 1), const2)],
        out_shape=[jax.ShapeDtypeStruct((t, d), F32), jax.ShapeDtypeStruct((t, d), BF16),
                   jax.ShapeDtypeStruct((1, d), F32), jax.ShapeDtypeStruct((w, 1), F32)],
        scratch_shapes=[pltpu.VMEM((w, LANES), F32)],
        compiler_params=_params(("arbitrary",)),
    )(x, dxo, gain, dq_t, dk_t, dv_t, dzg_t, win_t)


def _adamw_math(w, g, m, v):
    m = ADAM_B1 * m + (1.0 - ADAM_B1) * g
    v = ADAM_B2 * v + (1.0 - ADAM_B2) * (g * g)
    m_hat = m / (1.0 - ADAM_B1 ** ADAM_STEP)
    v_hat = v / (1.0 - ADAM_B2 ** ADAM_STEP)
    delta = -ADAM_LR * (m_hat / (jnp.sqrt(v_hat) + ADAM_EPS) + ADAM_WD * w)
    return delta, m, v


def adamw_many(ws, gs, ms, vs, name):
    n = len(ws)

    def body(*refs):
        w_refs, g_refs, m_refs, v_refs = refs[:n], refs[n:2 * n], refs[2 * n:3 * n], refs[3 * n:4 * n]
        d_refs, mo_refs, vo_refs = refs[4 * n:5 * n], refs[5 * n:6 * n], refs[6 * n:]
        for k in range(n):
            delta, mn, vn = _adamw_math(w_refs[k][...], g_refs[k][...], m_refs[k][...], v_refs[k][...])
            d_refs[k][...] = delta
            mo_refs[k][...] = mn
            vo_refs[k][...] = vn

    def whole(a):
        zeros = (0,) * a.ndim
        return pl.BlockSpec(a.shape, lambda i: zeros)

    outs = pl.pallas_call(
        body, name=name, grid=(1,),
        in_specs=[whole(a) for a in ws] * 4, out_specs=[whole(a) for a in ws] * 3,
        out_shape=[jax.ShapeDtypeStruct(a.shape, F32) for a in ws] * 3,
        compiler_params=_params(("arbitrary",)),
    )(*ws, *gs, *ms, *vs)
    return outs[:n], outs[n:2 * n], outs[2 * n:]


def sum_adamw(landing, grad3d, me, w, m, v, name):
    nd, rows, cols = landing.shape
    tr = rows // 2

    def body(me_ref, l_ref, own_ref, w_ref, m_ref, v_ref, g_ref, d_ref, mo_ref, vo_ref):
        g = own_ref[...].astype(F32)
        for j in range(nd):
            g = g + l_ref[j].astype(F32)
        delta, mn, vn = _adamw_math(w_ref[...], g, m_ref[...], v_ref[...])
        g_ref[...] = g
        d_ref[...] = delta
        mo_ref[...] = mn
        vo_ref[...] = vn

    spec = lambda: pl.BlockSpec((tr, cols), lambda i, me_ref: (i, 0))
    shape = jax.ShapeDtypeStruct((rows, cols), F32)
    return pl.pallas_call(
        body, name=name,
        grid_spec=pltpu.PrefetchScalarGridSpec(
            num_scalar_prefetch=1, grid=(rows // tr,),
            in_specs=[pl.BlockSpec((nd, tr, cols), lambda i, me_ref: (0, i, 0)),
                      pl.BlockSpec((None, tr, cols), lambda i, me_ref: (me_ref[0], i, 0)), spec(), spec(), spec()],
            out_specs=[spec() for _ in range(4)]),
        out_shape=[shape] * 4,
        compiler_params=_params(("arbitrary",)),
    )(me, landing, grad3d, w, m, v)


def _mesh_place():
    x, y, c = lax.axis_index("x"), lax.axis_index("y"), lax.axis_index("c")
    return x, y, c, 4 * x + 2 * y + c


def _peer(x, y, c, k):
    return (x ^ ((k >> 2) & 1), y ^ ((k >> 1) & 1), c ^ (k & 1))


def prep_shards(mats, me, name):
    na = len(mats)

    def body(me_ref, *refs):
        in_refs, shard_refs, land_refs = refs[:na], refs[na:2 * na], refs[2 * na:]
        for a in range(na):
            val = in_refs[a][...].astype(BF16)
            shard_refs[a][...] = val
            land_refs[a][...] = val

    whole = lambda mat: pl.BlockSpec(mat.shape, lambda i, me_ref: (0, 0))
    outs = pl.pallas_call(
        body, name=name,
        grid_spec=pltpu.PrefetchScalarGridSpec(
            num_scalar_prefetch=1, grid=(1,),
            in_specs=[whole(mat) for mat in mats],
            out_specs=[whole(mat) for mat in mats]
            + [pl.BlockSpec((None,) + mat.shape, lambda i, me_ref: (me_ref[0], 0, 0)) for mat in mats]),
        out_shape=[jax.ShapeDtypeStruct(mat.shape, BF16) for mat in mats]
        + [jax.ShapeDtypeStruct((N_DEV,) + mat.shape, BF16) for mat in mats],
        compiler_params=_params(("arbitrary",)),
    )(me, *mats)
    return outs[:na], outs[na:]


def allgather_rows(shards, lands, name):
    na = n_gather = len(shards)

    def body(*refs):
        in_refs, out_refs = refs[:na], refs[2 * na:3 * na]
        send_sems, recv_sems = refs[3 * na:]
        x, y, c, me = _mesh_place()

        def copy(sem, a, origin, src, k):
            return pltpu.make_async_remote_copy(
                src_ref=src, dst_ref=out_refs[a].at[origin], send_sem=send_sems.at[sem, a],
                recv_sem=recv_sems.at[sem, a], device_id=_peer(x, y, c, k), device_id_type=pl.DeviceIdType.MESH)

        kx, ky = 4, 2

        def recv(sem, a, origin):
            copy(sem, a, origin, in_refs[a], 1).wait_recv()

        def landed(a, origin):
            return out_refs[a].at[origin]

        @pl.when(c == 1)
        def _():
            sends = []

            def send(sem, a, origin, src, k):
                cp = copy(sem, a, origin, src, k)
                cp.start()
                sends.append(cp)

            for a in range(n_gather):
                send(1, a, me, in_refs[a], kx)
                send(2, a, me, in_refs[a], ky)
                send(0, a, me, in_refs[a], 1)
            for a in range(n_gather):
                recv(0, a, me ^ 1)
                send(3, a, me ^ 1, landed(a, me ^ 1), kx)
                send(4, a, me ^ 1, landed(a, me ^ 1), ky)
            for a in range(n_gather):
                recv(1, a, me ^ kx)
                send(5, a, me ^ kx, landed(a, me ^ kx), ky)
                send(7, a, me ^ kx, landed(a, me ^ kx), 1)
                recv(2, a, me ^ ky)
                send(8, a, me ^ ky, landed(a, me ^ ky), 1)
            for a in range(n_gather):
                recv(3, a, me ^ kx ^ 1)
                send(9, a, me ^ kx ^ 1, landed(a, me ^ kx ^ 1), 1)
                recv(4, a, me ^ ky ^ 1)
                send(6, a, me ^ ky ^ 1, landed(a, me ^ ky ^ 1), kx)
                send(10, a, me ^ ky ^ 1, landed(a, me ^ ky ^ 1), 1)
            for a in range(n_gather):
                recv(5, a, me ^ 6)
                send(11, a, me ^ 6, landed(a, me ^ 6), 1)
                recv(6, a, me ^ 7)
                send(12, a, me ^ 7, landed(a, me ^ 7), 1)
            for cp in sends:
                cp.wait_send()

        @pl.when(c == 0)
        def _():
            north = me ^ 1
            own = [copy(0, a, me, in_refs[a], 1) for a in range(n_gather)]
            for cp in own:
                cp.start()
            for a in range(n_gather):
                recv(0, a, north)
                for sem, origin in ((7, north ^ kx), (8, north ^ ky), (9, north ^ kx ^ 1), (10, north ^ ky ^ 1),
                                    (11, north ^ 6), (12, north ^ 7)):
                    recv(sem, a, origin)
            for cp in own:
                cp.wait_send()

    return pl.pallas_call(
        body, name=name,
        in_specs=[ANY_SPEC] * (2 * na), out_specs=[ANY_SPEC] * na,
        out_shape=[jax.ShapeDtypeStruct(land.shape, land.dtype) for land in lands],
        input_output_aliases={na + a: a for a in range(na)},
        scratch_shapes=[pltpu.SemaphoreType.DMA((N_GATHER_ROLES, n_gather)),
                        pltpu.SemaphoreType.DMA((N_GATHER_ROLES, n_gather))],
    )(*shards, *lands)


HBM_SPEC = pl.BlockSpec(memory_space=pltpu.HBM)
SEM_SPEC = pl.BlockSpec(memory_space=pltpu.SEMAPHORE)
SIDE_EFFECT = pltpu.SideEffectType.DATAFLOW_SIDE_EFFECTING


def _hbm(v):
    return pltpu.with_memory_space_constraint(v, pltpu.HBM)


def gather_slices(src, land, me, k):
    return src, land.at[me], land.at[me ^ k]


def exchange_slices(src, land, me, k):
    return src.at[me ^ k], land.at[k - 1], land.at[k - 1]


def split_start(groups, slices, deps, name):
    sizes = [len(srcs) for srcs, _ in groups]
    flat_src = [s for srcs, _ in groups for s in srcs]
    flat_land = [l for _, lands in groups for l in lands]
    na, ng, nd = len(flat_src), len(groups), len(deps)

    def body(*refs):
        src_refs, land_refs = refs[:na], refs[na:2 * na]
        sem_refs = refs[2 * na + nd:2 * na + nd + 2 * ng]
        token_ref = refs[-1]
        x, y, c, me = _mesh_place()
        a0 = 0
        for gi, size in enumerate(sizes):
            send_sems, recv_sems = sem_refs[2 * gi], sem_refs[2 * gi + 1]
            for k in range(1, N_DEV):
                for j in range(size):
                    src, dst, _ = slices(src_refs[a0 + j], land_refs[a0 + j], me, k)
                    pltpu.make_async_remote_copy(
                        src_ref=src, dst_ref=dst, send_sem=send_sems.at[(k - 1) * size + j],
                        recv_sem=recv_sems.at[(k - 1) * size + j],
                        device_id=_peer(x, y, c, k), device_id_type=pl.DeviceIdType.MESH).start()
            a0 += size
        token_ref[...] = jnp.zeros_like(token_ref)

    sems = [pltpu.SemaphoreType.DMA(((N_DEV - 1) * size,)) for size in sizes for _ in range(2)]
    thru = [pltpu.HBM(v.shape, v.dtype) for v in flat_src + flat_land]
    outs = pl.pallas_call(
        body, name=name,
        in_specs=[HBM_SPEC] * (2 * na) + [ANY_SPEC] * nd,
        out_specs=[SEM_SPEC] * (2 * ng) + [HBM_SPEC] * (2 * na) + [pl.BlockSpec(memory_space=pltpu.VMEM)],
        out_shape=sems + thru + [jax.ShapeDtypeStruct((8, LANES), F32)],
        input_output_aliases={i: 2 * ng + i for i in range(2 * na)},
        compiler_params=pltpu.CompilerParams(has_side_effects=SIDE_EFFECT),
    )(*[_hbm(v) for v in flat_src + flat_land], *deps)
    started, a0 = [], 0
    for gi, size in enumerate(sizes):
        srcs = outs[2 * ng + a0:2 * ng + a0 + size]
        lands = outs[2 * ng + na + a0:2 * ng + na + a0 + size]
        started.append((outs[2 * gi], outs[2 * gi + 1], list(srcs), list(lands)))
        a0 += size
    return started, outs[-1]


def split_wait(started, slices, after, name):
    send_sems, recv_sems, srcs, lands = started
    na = len(srcs)

    def body(*refs):
        src_refs, land_refs = refs[:na], refs[na:2 * na]
        send_ref, recv_ref = refs[2 * na], refs[2 * na + 1]
        x, y, c, me = _mesh_place()
        for k in range(1, N_DEV):
            for j in range(na):
                src, _, got = slices(src_refs[j], land_refs[j], me, k)
                cp = pltpu.make_async_remote_copy(
                    src_ref=src, dst_ref=got, send_sem=send_ref.at[(k - 1) * na + j], recv_sem=recv_ref.at[(k - 1) * na + j],
                    device_id=_peer(x, y, c, k), device_id_type=pl.DeviceIdType.MESH)
                cp.wait_send()
                cp.wait_recv()

    outs = pl.pallas_call(
        body, name=name,
        in_specs=[HBM_SPEC] * (2 * na) + [SEM_SPEC, SEM_SPEC] + [ANY_SPEC] * len(after),
        out_specs=[HBM_SPEC] * (2 * na),
        out_shape=[pltpu.HBM(v.shape, v.dtype) for v in srcs + lands],
        input_output_aliases={i: i for i in range(2 * na)},
        compiler_params=pltpu.CompilerParams(has_side_effects=SIDE_EFFECT),
    )(*srcs, *lands, send_sems, recv_sems, *after)
    return list(outs[:na]), list(outs[na:])


def place_own(pack, me, name):
    rows, cols = pack.shape

    def body(me_ref, p_ref, o_ref):
        o_ref[...] = p_ref[...]

    return pl.pallas_call(
        body, name=name,
        grid_spec=pltpu.PrefetchScalarGridSpec(
            num_scalar_prefetch=1, grid=(1,),
            in_specs=[pl.BlockSpec((rows, cols), lambda i, me_ref: (0, 0))],
            out_specs=pl.BlockSpec((None, rows, cols), lambda i, me_ref: (me_ref[0], 0, 0))),
        out_shape=jax.ShapeDtypeStruct((N_DEV, rows, cols), F32),
        compiler_params=_params(("arbitrary",)),
    )(me, pack)


def sum_slots(slots, name):
    nd, rows, cols = slots.shape

    def body(s_ref, o_ref):
        acc = s_ref[0]
        for j in range(1, nd):
            acc = acc + s_ref[j]
        o_ref[...] = acc

    return pl.pallas_call(
        body, name=name, grid=(1,),
        in_specs=[pl.BlockSpec((nd, rows, cols), lambda i: (0, 0, 0))],
        out_specs=pl.BlockSpec((rows, cols), lambda i: (0, 0)),
        out_shape=jax.ShapeDtypeStruct((rows, cols), F32),
        compiler_params=_params(("arbitrary",)),
    )(slots)


def _pack_rows(size):
    rows = -(-size // LANES)
    return -(-rows // 8) * 8


def pack_small(parts):
    out = []
    for p in parts:
        flat = p.reshape(-1).astype(F32)
        rows = _pack_rows(flat.shape[0])
        out.append(jnp.pad(flat, (0, rows * LANES - flat.shape[0])).reshape(rows, LANES))
    return jnp.concatenate(out, axis=0)


def unpack_small(pack, shapes):
    out, off = [], 0
    for shp in shapes:
        size = 1
        for s in shp:
            size *= s
        rows = _pack_rows(size)
        out.append(pack[off:off + rows].reshape(-1)[:size].reshape(shp))
        off += rows
    return out


def local_step(x, target, small, get_w, put_g, put_small):
    col = lambda v: v.reshape(-1, 1)
    (wg1,), deps = get_w(("wg1",), ())
    a1, h1 = ffn_gate(x, small["ffn1_norm_g"], wg1, "ffn1_gate", deps)
    (wu1,), _ = get_w(("wu1",), (a1,))
    sil1, gp1, s1 = ffn_up(h1, a1, wu1, "ffn1_up")
    (wd1, win, wout), _ = get_w(("wd1", "win", "wout"), (s1,))
    qkv_t, zg_t, x1 = mix_in_fwd(x, small["mix_norm_g"], win, col(small["b_in"]), "mix_in_fwd", ffn_tail=(s1, wd1))
    ya_t = attn_fwd(qkv_t, small["attn_sinks"], "attn_fwd")
    lng, lnb = col(small["gmlp_ln_g"]), col(small["gmlp_ln_b"])
    w_s, b_s = small["gmlp_w_s"][0], small["gmlp_b_s"][0]
    yg_t = gmlp_fwd(zg_t, lng, lnb, w_s, b_s, "gmlp_fwd")
    gao, ggo = col(small["attn_out_norm_g"]), col(small["gmlp_out_norm_g"])
    x2 = mix_out_fwd(x1, ya_t, yg_t, gao, ggo, wout, small["b_out"], "mix_out_fwd")
    (wg2, wu2, wd2), _ = get_w(("wg2", "wu2", "wd2"), (x2,))
    gs = {}
    dx3, sil2, gp2, s2, h2f, gs["final_norm_g"], loss = ffn_fwd(
        x2, small["ffn2_norm_g"], wg2, wu2, wd2, "ffn2_fwd", head=(target, small["final_norm_g"].reshape(1, -1)))

    dx2, da, db, df, gs["ffn2_norm_g"] = ffn_bwd(x2, dx3, small["ffn2_norm_g"], sil2, gp2, wg2, wu2, wd2, "ffn2_bwd")
    deps = put_g({"wg2": grad_tn(da, h2f, 1.0, "ffn2_dwg"), "wu2": grad_tn(db, h2f, 1.0, "ffn2_dwu"),
                  "wd2": grad_tn(s2, df, 1.0, "ffn2_dwd")})

    dya_t, dyg_t, y_t, dgao, dggo, gs["b_out"] = mix_out_bwd(dx2, ya_t, yg_t, gao, ggo, wout, "mix_out_bwd", deps)
    gs["attn_out_norm_g"], gs["gmlp_out_norm_g"] = dgao.reshape(1, -1), dggo.reshape(1, -1)
    g_wout = grad_nn([y_t], dx2, "dwout")
    dzg_t, dws, dbs, dlng, dlnb = gmlp_bwd(zg_t, dyg_t, lng, lnb, w_s, b_s, "gmlp_bwd")
    gs["gmlp_w_s"], gs["gmlp_b_s"] = dws[None], dbs[None]
    gs["gmlp_ln_g"], gs["gmlp_ln_b"] = dlng.reshape(1, -1), dlnb.reshape(1, -1)
    dq_t, dk_t, dv_t, dsinks = attn_bwd(qkv_t, dya_t, small["attn_sinks"], "attn_bwd")
    gs["attn_sinks"] = dsinks.reshape(1, -1)
    dx1, h2, gs["mix_norm_g"], dbin = mix_in_bwd(x1, dx2, small["mix_norm_g"], dq_t, dk_t, dv_t, dzg_t, win,
                                                 "mix_in_bwd")
    gs["b_in"] = dbin.reshape(1, -1)
    deps = put_g({"wout": g_wout, "win": grad_nn([dq_t, dk_t, dv_t, dzg_t], h2, "dwin")})

    dx0, da, db, df, gs["ffn1_norm_g"] = ffn_bwd(x, dx1, small["ffn1_norm_g"], sil1, gp1, wg1, wu1, wd1, "ffn1_bwd",
                                             deps)
    deps = put_small(gs, loss)
    deps = put_g({"wg1": grad_tn(da, h1, 1.0, "ffn1_dwg", deps)})
    deps = put_g({"wu1": grad_tn(db, h1, 1.0, "ffn1_dwu", deps)})
    put_g({"wd1": grad_tn(s1, df, 1.0, "ffn1_dwd", deps)})
    return dx0


SMALL_NAMES = ["ffn1_norm_g", "mix_norm_g", "b_in", "attn_sinks", "gmlp_ln_g", "gmlp_ln_b", "gmlp_w_s", "gmlp_b_s",
               "attn_out_norm_g", "gmlp_out_norm_g", "b_out", "ffn2_norm_g", "final_norm_g"]
BIG_NAMES = {"wg1": ("ffn1_w_gate", True), "wu1": ("ffn1_w_up", True), "wd1": ("ffn1_w_down", False),
             "win": ("w_in", True), "wout": ("w_out", False),
             "wg2": ("ffn2_w_gate", True), "wu2": ("ffn2_w_up", True), "wd2": ("ffn2_w_down", False)}
WEIGHT_ORDER = ["ffn1_norm_g", "ffn1_w_gate", "ffn1_w_up", "ffn1_w_down", "mix_norm_g", "w_in", "b_in", "attn_sinks",
                "gmlp_ln_g", "gmlp_ln_b", "gmlp_w_s", "gmlp_b_s", "attn_out_norm_g", "gmlp_out_norm_g", "w_out",
                "b_out", "ffn2_norm_g", "ffn2_w_gate", "ffn2_w_up", "ffn2_w_down", "final_norm_g"]


def kernel(x, ffn1_norm_g, ffn1_w_gate, ffn1_w_up, ffn1_w_down, mix_norm_g, w_in, b_in, attn_sinks, gmlp_ln_g, gmlp_ln_b, gmlp_w_s, gmlp_b_s, attn_out_norm_g, gmlp_out_norm_g, w_out, b_out, ffn2_norm_g, ffn2_w_gate, ffn2_w_up, ffn2_w_down, final_norm_g, loss_target, m_ffn1_norm_g, m_ffn1_w_gate, m_ffn1_w_up, m_ffn1_w_down, m_mix_norm_g, m_w_in, m_b_in, m_attn_sinks, m_gmlp_ln_g, m_gmlp_ln_b, m_gmlp_w_s, m_gmlp_b_s, m_attn_out_norm_g, m_gmlp_out_norm_g, m_w_out, m_b_out, m_ffn2_norm_g, m_ffn2_w_gate, m_ffn2_w_up, m_ffn2_w_down, m_final_norm_g, v_ffn1_norm_g, v_ffn1_w_gate, v_ffn1_w_up, v_ffn1_w_down, v_mix_norm_g, v_w_in, v_b_in, v_attn_sinks, v_gmlp_ln_g, v_gmlp_ln_b, v_gmlp_w_s, v_gmlp_b_s, v_attn_out_norm_g, v_gmlp_out_norm_g, v_w_out, v_b_out, v_ffn2_norm_g, v_ffn2_w_gate, v_ffn2_w_up, v_ffn2_w_down, v_final_norm_g):
    args = dict(locals())
    w = {n: args[n] for n in WEIGHT_ORDER}
    m = {n: args["m_" + n] for n in WEIGHT_ORDER}
    v = {n: args["v_" + n] for n in WEIGHT_ORDER}

    d_model = x.shape[-1]
    flat = lambda g: g.reshape(-1, d_model)
    me = _mesh_place()[3].astype(jnp.int32).reshape(1)
    first = ("wg1",)
    later = (("wu1",), ("wd1", "win", "wout"), ("wg2", "wu2", "wd2"))
    gathers, exchanges, last_token = {}, [], []

    order = list(first) + [k for grp in later for k in grp]
    mats = [w[BIG_NAMES[k][0]][0].T if BIG_NAMES[k][1] else w[BIG_NAMES[k][0]][0] for k in order]
    shard_list, land_list = prep_shards(mats, me, "prep_shards")
    shards, zones = dict(zip(order, shard_list)), dict(zip(order, land_list))

    def get_w(keys, after):
        if keys == first:
            full = allgather_rows([shards[k] for k in first], [zones[k] for k in first], "allgather_ffn1")
            started, token = split_start([([shards[k] for k in grp], [zones[k] for k in grp]) for grp in later],
                                         gather_slices, (full[0],), "gather_start")
            gathers.update(zip(later, started))
            return [flat(g) for g in full], (token,)
        _, lands = split_wait(gathers[keys], gather_slices, after, "gather_wait_" + keys[0])
        return [flat(land) for land in lands], ()

    def put_g(gb):
        keys = tuple(gb)
        g3 = [gb[k].reshape(N_DEV, -1, d_model) for k in keys]
        lands = [lax.empty((N_DEV - 1,) + g.shape[1:], BF16) for g in g3]
        started, token = split_start([(g3, lands)], exchange_slices, (), "exchange_start_" + keys[0])
        exchanges.append((keys, started[0]))
        last_token[:] = [token]
        return (token,)

    small_started = []

    def put_small(gs, loss):
        pack = pack_small([gs[n] for n in SMALL_NAMES] + [loss[:, :1]])
        started, token = split_start([([pack], [place_own(pack, me, "place_small")])], gather_slices, (), "small_start")
        small_started[:] = started
        return (token,)

    small = {n: w[n] for n in SMALL_NAMES}
    grad_x = local_step(x[0], loss_target[0], small, get_w, put_g, put_small)

    grads, deltas, new_m, new_v = {}, {}, {}, {}
    shapes = [w[n].shape for n in SMALL_NAMES]
    _, (slots,) = split_wait(small_started[0], gather_slices, tuple(last_token), "small_wait")
    total = sum_slots(slots, "sum_small")
    small_g = unpack_small(total, shapes + [(1, 1)])
    loss_total = small_g[-1].reshape(())
    d_, m_, v_ = adamw_many([w[n] for n in SMALL_NAMES], small_g[:-1], [m[n] for n in SMALL_NAMES],
                            [v[n] for n in SMALL_NAMES], "adamw_small")
    for n, g_, dd, mm, vv in zip(SMALL_NAMES, small_g[:-1], d_, m_, v_):
        grads[n], deltas[n], new_m[n], new_v[n] = g_, dd, mm, vv

    after = tuple(arr for n in SMALL_NAMES for arr in (grads[n], deltas[n], new_m[n], new_v[n]))
    for keys, started in exchanges:
        g3, lands = split_wait(started, exchange_slices, after, "exchange_wait_" + keys[0])
        for key, own, land in zip(keys, g3, lands):
            pname, transposed = BIG_NAMES[key]
            to_rows = (lambda p: p[0].T) if transposed else (lambda p: p[0])
            from_rows = (lambda r: r.T[None]) if transposed else (lambda r: r[None])
            g, d_, m_, v_ = sum_adamw(land, own, me, to_rows(w[pname]), to_rows(m[pname]), to_rows(v[pname]),
                                      "adamw_" + key)
            after = after + (g,)
            grads[pname], deltas[pname], new_m[pname], new_v[pname] = (from_rows(g), from_rows(d_), from_rows(m_),
                                                                        from_rows(v_))

    return (loss_total, grad_x[None], *[grads[n] for n in WEIGHT_ORDER], *[deltas[n] for n in WEIGHT_ORDER],
            *[new_m[n] for n in WEIGHT_ORDER], *[new_v[n] for n in WEIGHT_ORDER])
```

```python
import functools

import jax
import jax.numpy as jnp
from jax import lax
from jax.experimental import pallas as pl
from jax.experimental.pallas import tpu as pltpu

F32 = jnp.float32
BF16 = jnp.bfloat16

N_DEV = 8
N_Q_HEADS = 8
N_KV_HEADS = 2
HEAD_DIM = 64
ATTN_WIDTH = N_Q_HEADS * HEAD_DIM
KV_WIDTH = N_KV_HEADS * HEAD_DIM
QKV_WIDTH = ATTN_WIDTH + 2 * KV_WIDTH
BLOCK = 128
GMLP_GROUPS = 8
GMLP_GROUP_DIM = 64
GMLP_WIDTH = GMLP_GROUPS * GMLP_GROUP_DIM
EPS = 1e-6
FFN_RES = 0.5
ATTN_SCALE = HEAD_DIM ** -0.5
MASK_VALUE = -1e30

ADAM_LR = 0.001
ADAM_B1 = 0.9
ADAM_B2 = 0.999
ADAM_EPS = 1e-08
ADAM_WD = 0.01
ADAM_STEP = 10

V7X_VMEM_BYTES = 64 * 1024 * 1024
VMEM_LIMIT = 62 * 1024 * 1024
LANES = 128
MXU_WIDTH = 256
FFN_BWD_TOKENS = 256

NT = (((1,), (1,)), ((), ()))
TN = (((0,), (0,)), ((), ()))


def _dot(a, b):
    return jnp.dot(a, b, preferred_element_type=F32)


def _dg(a, b, dims):
    return lax.dot_general(a, b, dims, preferred_element_type=F32)


def _params(semantics=None):
    return pltpu.CompilerParams(dimension_semantics=semantics, vmem_limit_bytes=VMEM_LIMIT)


def _resident(shape):
    zeros = (0,) * len(shape)
    return pl.BlockSpec(shape, lambda *_: zeros, pipeline_mode=pl.Buffered(1))


def _drop_deps(body, n_in, n_deps):
    return lambda *refs: body(*refs[:n_in], *refs[n_in + n_deps:])


ANY_SPEC = pl.BlockSpec(memory_space=pl.ANY)


TOKEN_TILE = 512
GRAD_TOKENS_PER_STEP = 2048
GRAD_ROW_TILE_CAP = 1408
N_GATHER_ROLES = 13


def _token_tile(t):
    return min(t, TOKEN_TILE)


def _f_chunk(f):
    for c in (256, 128):
        if f % c == 0:
            return c
    return f


def _loss_head(xv, target, gain):
    r = lax.rsqrt(jnp.mean(xv * xv, axis=-1, keepdims=True) + EPS)
    err = xv * r * gain - target
    loss = 0.5 * jnp.sum(jnp.mean(err * err, axis=-1, keepdims=True), axis=0, keepdims=True)
    dx, dgain = _rmsnorm_bwd_rows(xv, r, gain, err * (1.0 / xv.shape[-1]))
    return dx, dgain, loss


def ffn_fwd(x, gain, wg_t, wu_t, wd, name, deps=(), head=None):
    t, d = x.shape
    f = wg_t.shape[0]
    tm = _token_tile(t)
    fc = _f_chunk(f)
    weights = [wg_t, wu_t] + ([] if wd is None else [wd])
    n_in = 2 + len(weights) + (0 if head is None else 2)
    n_x = 0 if wd is None else 1

    def body(x_ref, g_ref, wg_ref, wu_ref, *rest):
        sil_ref, gp_ref, s_ref, h_ref = rest[n_in - 4 + n_x:n_in + n_x]
        xv = x_ref[...]
        r = lax.rsqrt(jnp.mean(xv * xv, axis=-1, keepdims=True) + EPS)
        h = (xv * r * g_ref[...]).astype(BF16)
        h_ref[...] = h
        for c in range(f // fc):
            sl = slice(c * fc, (c + 1) * fc)
            a = _dg(h, wg_ref[sl, :], NT)
            b = _dg(h, wu_ref[sl, :], NT)
            sig = jax.nn.sigmoid(a)
            sil = a * sig
            sil_ref[:, sl] = sil.astype(BF16)
            gp_ref[:, sl] = (b * (sig + sil - sil * sig)).astype(BF16)
            s_ref[:, sl] = (sil * b).astype(BF16)
        if wd is None:
            return
        wd_ref, xo_ref = rest[0], rest[n_in - 4]
        xo = xv + FFN_RES * _dot(s_ref[...], wd_ref[...])
        if head is None:
            xo_ref[...] = xo
        else:
            t_ref, hg_ref = rest[1:3]
            dg_ref, loss_ref = rest[-2:]

            @pl.when(pl.program_id(0) == 0)
            def _():
                dg_ref[...] = jnp.zeros_like(dg_ref)
                loss_ref[...] = jnp.zeros_like(loss_ref)

            dx, dgain, loss = _loss_head(xo, t_ref[...], hg_ref[...])
            xo_ref[...] = dx
            dg_ref[...] += dgain
            loss_ref[...] += loss

    tok = lambda: pl.BlockSpec((tm, d), lambda i: (i, 0))
    wide = lambda: pl.BlockSpec((tm, f), lambda i: (i, 0))
    const2 = lambda i: (0, 0)
    in_specs = [tok(), _resident((1, d))] + [_resident((f, d)) for _ in weights]
    out_specs = [tok()] * n_x + [wide(), wide(), wide(), tok()]
    out_shape = ([jax.ShapeDtypeStruct((t, d), F32)] * n_x + [jax.ShapeDtypeStruct((t, f), BF16)] * 3
                 + [jax.ShapeDtypeStruct((t, d), BF16)])
    operands = [x, gain] + weights
    if head is not None:
        in_specs += [tok(), _resident((1, d))]
        out_specs += [pl.BlockSpec((1, d), const2), pl.BlockSpec((1, LANES), const2)]
        out_shape += [jax.ShapeDtypeStruct((1, d), F32), jax.ShapeDtypeStruct((1, LANES), F32)]
        operands += list(head)
    return pl.pallas_call(
        _drop_deps(body, n_in, len(deps)), name=name, grid=(t // tm,),
        in_specs=in_specs + [ANY_SPEC] * len(deps), out_specs=out_specs, out_shape=out_shape,
        compiler_params=_params(("arbitrary",)),
    )(*operands, *deps)


def _rmsnorm_bwd_rows(xv, r, gain, dh):
    n = xv * r
    dgain = jnp.sum(dh * n, axis=0, keepdims=True)
    dn = dh * gain
    dx = r * (dn - n * jnp.mean(dn * n, axis=-1, keepdims=True))
    return dx, dgain


def ffn_bwd(x, dxo, gain, sil, gp, wg_t, wu_t, wd, name, deps=()):
    t, d = x.shape
    f = wd.shape[0]
    tm = min(t, FFN_BWD_TOKENS)
    nt = t // tm
    chunks = [(off, min(MXU_WIDTH, f - off)) for off in range(0, f, MXU_WIDTH)]

    def body(xp_ref, dxp_ref, dxo_ref, g_ref, sil_ref, gp_ref, wg_ref, wu_ref, wd_ref,
             dx_ref, da_ref, db_ref, df_ref, dg_ref, dh_ref):
        i = pl.program_id(0)
        gain_v = g_ref[...]

        @pl.when(i == 0)
        def _():
            dh_ref[...] = jnp.zeros_like(dh_ref)
            dg_ref[...] = jnp.zeros_like(dg_ref)

        def finish_previous(slot):
            xv = xp_ref[...]
            r = lax.rsqrt(jnp.mean(xv * xv, axis=-1, keepdims=True) + EPS)
            dx, dgain = _rmsnorm_bwd_rows(xv, r, gain_v, dh_ref[slot])
            dx_ref[...] = dxp_ref[...] + dx
            dg_ref[...] += dgain

        @pl.when(i < nt)
        def _():
            df = (FFN_RES * dxo_ref[...]).astype(BF16)
            df_ref[...] = df
            for off, size in chunks:
                sl = slice(off, off + size)
                ds = _dg(df, wd_ref[sl, :], NT)
                da_ref[:, sl] = (ds * gp_ref[:, sl].astype(F32)).astype(BF16)
                db_ref[:, sl] = (ds * sil_ref[:, sl].astype(F32)).astype(BF16)
            dh = _dot(da_ref[...], wg_ref[...]) + _dot(db_ref[...], wu_ref[...])
            finish_previous((i + 1) % 2)
            dh_ref[i % 2] = dh

        @pl.when(i == nt)
        def _():
            finish_previous((nt - 1) % 2)

    prev = lambda i: (jnp.maximum(i - 1, 0), 0)
    cur = lambda i: (jnp.minimum(i, nt - 1), 0)
    return pl.pallas_call(
        _drop_deps(body, 9, len(deps)), name=name, grid=(nt + 1,),
        in_specs=[pl.BlockSpec((tm, d), prev), pl.BlockSpec((tm, d), prev), pl.BlockSpec((tm, d), cur),
                  _resident((1, d)), pl.BlockSpec((tm, f), cur), pl.BlockSpec((tm, f), cur),
                  _resident((f, d)), _resident((f, d)), _resident((f, d))] + [ANY_SPEC] * len(deps),
        out_specs=[pl.BlockSpec((tm, d), prev), pl.BlockSpec((tm, f), cur), pl.BlockSpec((tm, f), cur),
                   pl.BlockSpec((tm, d), cur), pl.BlockSpec((1, d), lambda i: (0, 0))],
        out_shape=[jax.ShapeDtypeStruct((t, d), F32), jax.ShapeDtypeStruct((t, f), BF16),
                   jax.ShapeDtypeStruct((t, f), BF16), jax.ShapeDtypeStruct((t, d), BF16),
                   jax.ShapeDtypeStruct((1, d), F32)],
        scratch_shapes=[pltpu.VMEM((2, tm, d), F32)],
        compiler_params=_params(("arbitrary",)),
    )(x, dxo, dxo, gain, sil, gp, wg_t, wu_t, wd, *deps)


def _row_tile(m, cap):
    if m <= cap:
        return m
    best = None
    for bm in range(LANES, cap + 1, LANES):
        if m % bm == 0:
            best = bm
    return best if best is not None else m


def grad_tn(a, b, scale, name, deps=()):
    t, m = a.shape
    n = b.shape[1]
    bm = _row_tile(m, GRAD_ROW_TILE_CAP)
    bk = min(t, GRAD_TOKENS_PER_STEP)
    nk = t // bk

    def body(a_ref, b_ref, o_ref, acc_ref):
        k = pl.program_id(1)

        @pl.when(k == 0)
        def _():
            acc_ref[...] = jnp.zeros_like(acc_ref)

        acc_ref[...] += _dg(a_ref[...].astype(BF16), b_ref[...].astype(BF16), TN)

        @pl.when(k == nk - 1)
        def _():
            o_ref[...] = (scale * acc_ref[...]).astype(BF16)

    return pl.pallas_call(
        _drop_deps(body, 2, len(deps)), name=name, grid=(m // bm, nk),
        in_specs=[pl.BlockSpec((bk, bm), lambda i, k: (k, i)), pl.BlockSpec((bk, n), lambda i, k: (k, 0))]
        + [ANY_SPEC] * len(deps),
        out_specs=pl.BlockSpec((bm, n), lambda i, k: (i, 0)),
        out_shape=jax.ShapeDtypeStruct((m, n), BF16),
        scratch_shapes=[pltpu.VMEM((bm, n), F32)],
        compiler_params=_params(("arbitrary", "arbitrary")),
    )(a, b, *deps)


def grad_nn(a_list, b, name):
    t, n = b.shape
    ms = [a.shape[0] for a in a_list]
    m = sum(ms)
    bk = min(t, GRAD_TOKENS_PER_STEP)
    nk = t // bk
    na = len(a_list)

    def body(*refs):
        a_refs, b_ref, o_ref, acc_ref = refs[:na], refs[na], refs[na + 1], refs[na + 2]
        k = pl.program_id(0)

        @pl.when(k == 0)
        def _():
            acc_ref[...] = jnp.zeros_like(acc_ref)

        bv = b_ref[...].astype(BF16)
        off = 0
        for a_ref, mi in zip(a_refs, ms):
            acc_ref[off:off + mi, :] += _dot(a_ref[...].astype(BF16), bv)
            off += mi

        @pl.when(k == nk - 1)
        def _():
            o_ref[...] = acc_ref[...].astype(BF16)

    return pl.pallas_call(
        body, name=name, grid=(nk,),
        in_specs=[pl.BlockSpec((mi, bk), lambda k: (0, k)) for mi in ms] + [pl.BlockSpec((bk, n), lambda k: (k, 0))],
        out_specs=pl.BlockSpec((m, n), lambda k: (0, 0)),
        out_shape=jax.ShapeDtypeStruct((m, n), BF16),
        scratch_shapes=[pltpu.VMEM((m, n), F32)],
        compiler_params=_params(("arbitrary",)),
    )(*a_list, b)


def mix_in_fwd(x, gain, win_t, b_in_col, name, ffn_tail=None):
    t, d = x.shape
    w = win_t.shape[0]
    tm = _token_tile(t)

    def body(x_ref, g_ref, w_ref, bias_ref, *rest):
        xv = x_ref[...]
        if ffn_tail is not None:
            s_ref, wd_ref, xo_ref = rest[0], rest[1], rest[4]
            xv = xv + FFN_RES * _dot(s_ref[...], wd_ref[...])
            xo_ref[...] = xv
        qkv_ref, zg_ref = rest[-3:-1] if ffn_tail is not None else rest
        r = lax.rsqrt(jnp.mean(xv * xv, axis=-1, keepdims=True) + EPS)
        h = (xv * r * g_ref[...]).astype(BF16)
        qkv_ref[...] = (_dg(w_ref[:QKV_WIDTH, :], h, NT) + bias_ref[:QKV_WIDTH, :]).astype(BF16)
        zg_ref[...] = (_dg(w_ref[QKV_WIDTH:, :], h, NT) + bias_ref[QKV_WIDTH:, :]).astype(BF16)

    tok = lambda: pl.BlockSpec((tm, d), lambda i: (i, 0))
    in_specs = [tok(), _resident((1, d)), _resident((w, d)), _resident((w, 1))]
    out_specs = [pl.BlockSpec((QKV_WIDTH, tm), lambda i: (0, i)), pl.BlockSpec((w - QKV_WIDTH, tm), lambda i: (0, i))]
    out_shape = [jax.ShapeDtypeStruct((QKV_WIDTH, t), BF16), jax.ShapeDtypeStruct((w - QKV_WIDTH, t), BF16)]
    operands = [x, gain, win_t, b_in_col]
    if ffn_tail is not None:
        f = ffn_tail[1].shape[0]
        in_specs += [pl.BlockSpec((tm, f), lambda i: (i, 0)), _resident((f, d))]
        out_specs.append(tok())
        out_shape.append(jax.ShapeDtypeStruct((t, d), F32))
        operands += list(ffn_tail)
    return pl.pallas_call(
        body, name=name, grid=(t // tm,), in_specs=in_specs, out_specs=out_specs, out_shape=out_shape,
        compiler_params=_params(("arbitrary",)),
    )(*operands)


ATTN_REP = N_Q_HEADS // N_KV_HEADS
ATTN_BLOCKS_PER_STEP = 8


def _attn_tiles(t):
    nbs = min(ATTN_BLOCKS_PER_STEP, t // BLOCK)
    return nbs, nbs * BLOCK, t // (nbs * BLOCK)


def _attn_specs(nbs, tiles):
    last = tiles - 1
    cur = lambda n: jnp.minimum(n, last)
    prev = lambda n: jnp.maximum(jnp.minimum(n, last) * nbs - 1, 0)
    k_row = ATTN_WIDTH // KV_WIDTH
    tw = nbs * BLOCK
    return [
        pl.BlockSpec((ATTN_WIDTH, tw), lambda n: (0, cur(n))),
        pl.BlockSpec((KV_WIDTH, BLOCK), lambda n: (k_row, prev(n))),
        pl.BlockSpec((KV_WIDTH, tw), lambda n: (k_row, cur(n))),
        pl.BlockSpec((KV_WIDTH, BLOCK), lambda n: (k_row + 1, prev(n))),
        pl.BlockSpec((KV_WIDTH, tw), lambda n: (k_row + 1, cur(n))),
    ]


def _cols(b):
    return slice(b * BLOCK, (b + 1) * BLOCK)


def _band_rows(prev_ref, tile_ref, gs, b):
    before = prev_ref[gs, :] if b == 0 else tile_ref[gs, _cols(b - 1)]
    return before, tile_ref[gs, _cols(b)]


def _group_rows(ref, g, b):
    first = g * ATTN_REP
    return jnp.concatenate([ref[(first + r) * HEAD_DIM:(first + r + 1) * HEAD_DIM, _cols(b)] for r in range(ATTN_REP)],
                           axis=1)


def _ungroup_rows(ref, g, b, val):
    first = g * ATTN_REP
    for r in range(ATTN_REP):
        ref[(first + r) * HEAD_DIM:(first + r + 1) * HEAD_DIM, _cols(b)] = val[:, r * BLOCK:(r + 1) * BLOCK]


def _attn_upper():
    key = lax.broadcasted_iota(jnp.int32, (BLOCK, ATTN_REP * BLOCK), 0)
    qry = lax.broadcasted_iota(jnp.int32, (BLOCK, ATTN_REP * BLOCK), 1) & (BLOCK - 1)
    return key > qry


def _attn_probs(q, kp, kc, sink_ref, g, upper, no_prev):
    s = jnp.where(upper, _dg(kp, q, TN), _dg(kc, q, TN)) * ATTN_SCALE
    if no_prev is not None:
        s = jnp.where(upper & no_prev, MASK_VALUE, s)
    sink = jnp.concatenate([jnp.full((1, BLOCK), sink_ref[0, g * ATTN_REP + r], F32) for r in range(ATTN_REP)], axis=1)
    m = jnp.maximum(jnp.max(s, axis=0, keepdims=True), sink)
    e = jnp.exp(s - m)
    es = jnp.exp(sink - m)
    inv = 1.0 / (jnp.sum(e, axis=0, keepdims=True) + es)
    return e * inv, es * inv


def _split_band(upper, val):
    return jnp.where(upper, val, 0.0).astype(BF16), jnp.where(upper, 0.0, val).astype(BF16)


def attn_fwd(qkv_t, sinks, name):
    t = qkv_t.shape[1]
    nbs, tw, tiles = _attn_tiles(t)

    def body(sink_ref, q_ref, kp_ref, kc_ref, vp_ref, vc_ref, y_ref):
        n = pl.program_id(0)
        upper = _attn_upper()
        for b in range(nbs):
            for g in range(N_KV_HEADS):
                gs = slice(g * HEAD_DIM, (g + 1) * HEAD_DIM)
                kp, kc = _band_rows(kp_ref, kc_ref, gs, b)
                vp, vc = _band_rows(vp_ref, vc_ref, gs, b)
                p, _ = _attn_probs(_group_rows(q_ref, g, b), kp, kc, sink_ref, g, upper, n == 0 if b == 0 else None)
                pp, pc = _split_band(upper, p)
                _ungroup_rows(y_ref, g, b, (_dot(vp, pp) + _dot(vc, pc)).astype(BF16))

    return pl.pallas_call(
        body, name=name, grid=(tiles,),
        in_specs=[pl.BlockSpec(memory_space=pltpu.SMEM)] + _attn_specs(nbs, tiles),
        out_specs=pl.BlockSpec((ATTN_WIDTH, tw), lambda n: (0, n)),
        out_shape=jax.ShapeDtypeStruct((ATTN_WIDTH, t), BF16),
        compiler_params=_params(("arbitrary",)),
    )(sinks, qkv_t, qkv_t, qkv_t, qkv_t, qkv_t)


def attn_bwd(qkv_t, dy_t, sinks, name):
    t = qkv_t.shape[1]
    nbs, tw, tiles = _attn_tiles(t)
    kept = slice(0, tw - BLOCK)

    def body(sink_ref, q_ref, kp_ref, kc_ref, vp_ref, vc_ref, do_ref, dq_ref, dk_ref, dv_ref, dsink_ref,
             ck_ref, cv_ref, sacc_ref):
        n = pl.program_id(0)

        @pl.when(n == 0)
        def _():
            sacc_ref[...] = jnp.zeros_like(sacc_ref)

        @pl.when((n > 0) & (n < tiles))
        def _():
            if nbs > 1:
                dk_ref[:, kept] = ck_ref[:, kept].astype(BF16)
                dv_ref[:, kept] = cv_ref[:, kept].astype(BF16)

        @pl.when(n < tiles)
        def _():
            upper = _attn_upper()
            for b in range(nbs):
                for g in range(N_KV_HEADS):
                    gs = slice(g * HEAD_DIM, (g + 1) * HEAD_DIM)
                    kp, kc = _band_rows(kp_ref, kc_ref, gs, b)
                    vp, vc = _band_rows(vp_ref, vc_ref, gs, b)
                    q = _group_rows(q_ref, g, b)
                    do = _group_rows(do_ref, g, b)
                    p, ps = _attn_probs(q, kp, kc, sink_ref, g, upper, n == 0 if b == 0 else None)
                    dp = jnp.where(upper, _dg(vp, do, TN), _dg(vc, do, TN))
                    delta = jnp.sum(p * dp, axis=0, keepdims=True)
                    dsink = -(ps * delta)
                    for r in range(ATTN_REP):
                        hd = g * ATTN_REP + r
                        sacc_ref[hd:hd + 1, :] += dsink[:, r * BLOCK:(r + 1) * BLOCK]
                    dsp, dsc = _split_band(upper, p * (dp - delta))
                    pp, pc = _split_band(upper, p)
                    _ungroup_rows(dq_ref, g, b, (ATTN_SCALE * (_dot(kp, dsp) + _dot(kc, dsc))).astype(BF16))
                    dk_before = ATTN_SCALE * _dg(q, dsp, NT)
                    dv_before = _dg(do, pp, NT)
                    if b == 0:
                        @pl.when(n > 0)
                        def _():
                            dk_ref[gs, _cols(nbs - 1)] = (ck_ref[gs, _cols(nbs - 1)] + dk_before).astype(BF16)
                            dv_ref[gs, _cols(nbs - 1)] = (cv_ref[gs, _cols(nbs - 1)] + dv_before).astype(BF16)
                    else:
                        ck_ref[gs, _cols(b - 1)] += dk_before
                        cv_ref[gs, _cols(b - 1)] += dv_before
                    ck_ref[gs, _cols(b)] = ATTN_SCALE * _dg(q, dsc, NT)
                    cv_ref[gs, _cols(b)] = _dg(do, pc, NT)

        @pl.when(n == tiles)
        def _():
            dk_ref[...] = ck_ref[...].astype(BF16)
            dv_ref[...] = cv_ref[...].astype(BF16)
            dsink_ref[...] = jnp.sum(sacc_ref[...], axis=1, keepdims=True)

    last = tiles - 1
    done = lambda n: jnp.maximum(n - 1, 0)
    outs = pl.pallas_call(
        body, name=name, grid=(tiles + 1,),
        in_specs=[pl.BlockSpec(memory_space=pltpu.SMEM)] + _attn_specs(nbs, tiles)
        + [pl.BlockSpec((ATTN_WIDTH, tw), lambda n: (0, jnp.minimum(n, last)))],
        out_specs=[pl.BlockSpec((ATTN_WIDTH, tw), lambda n: (0, jnp.minimum(n, last))),
                   pl.BlockSpec((KV_WIDTH, tw), lambda n: (0, done(n))),
                   pl.BlockSpec((KV_WIDTH, tw), lambda n: (0, done(n))),
                   pl.BlockSpec((N_Q_HEADS, 1), lambda n: (0, 0))],
        out_shape=[jax.ShapeDtypeStruct((ATTN_WIDTH, t), BF16), jax.ShapeDtypeStruct((KV_WIDTH, t), BF16),
                   jax.ShapeDtypeStruct((KV_WIDTH, t), BF16), jax.ShapeDtypeStruct((N_Q_HEADS, 1), F32)],
        scratch_shapes=[pltpu.VMEM((KV_WIDTH, tw), F32), pltpu.VMEM((KV_WIDTH, tw), F32),
                        pltpu.VMEM((N_Q_HEADS, BLOCK), F32)],
        compiler_params=_params(("arbitrary",)),
    )(sinks, qkv_t, qkv_t, qkv_t, qkv_t, qkv_t, dy_t)
    return outs


GELU_C = 0.7978845608028654
GELU_A = 0.044715


def _gelu(x):
    return 0.5 * x * (1.0 + jnp.tanh(GELU_C * (x + GELU_A * x * x * x)))


def _gelu_grad(x):
    th = jnp.tanh(GELU_C * (x + GELU_A * x * x * x))
    return 0.5 * (1.0 + th) + 0.5 * x * (1.0 - th * th) * GELU_C * (1.0 + 3.0 * GELU_A * x * x)


def _gmlp_parts(zg, lng, lnb):
    z = _gelu(zg)
    u = z[:GMLP_WIDTH, :]
    vv = z[GMLP_WIDTH:, :]
    mu = jnp.mean(vv, axis=0, keepdims=True)
    xc = vv - mu
    rstd = lax.rsqrt(jnp.mean(xc * xc, axis=0, keepdims=True) + EPS)
    xhat = xc * rstd
    return u, xhat, rstd, xhat * lng + lnb


def _causal(w):
    row = lax.broadcasted_iota(jnp.int32, (BLOCK, BLOCK), 0)
    col = lax.broadcasted_iota(jnp.int32, (BLOCK, BLOCK), 1)
    return jnp.where(col <= row, w, 0.0)


GMLP_CHUNKS_PER_STEP = 8


def _stack_chunks(v, nc):
    return jnp.concatenate([v[:, c * BLOCK:(c + 1) * BLOCK] for c in range(nc)], axis=0)


def _unstack_chunks(v, nc):
    rows = v.shape[0] // nc
    return jnp.concatenate([v[c * rows:(c + 1) * rows, :] for c in range(nc)], axis=1)


def gmlp_fwd(zg_t, lng, lnb, w_s, b_s, name):
    t = zg_t.shape[1]
    nc = min(GMLP_CHUNKS_PER_STEP, t // BLOCK)
    tw = nc * BLOCK

    def body(zg_ref, lng_ref, lnb_ref, w_ref, bs_ref, y_ref):
        u, _, _, vn = _gmlp_parts(zg_ref[...].astype(F32), lng_ref[...], lnb_ref[...])
        for g in range(GMLP_GROUPS):
            gs = slice(g * GMLP_GROUP_DIM, (g + 1) * GMLP_GROUP_DIM)
            wc = _causal(w_ref[g]).astype(BF16)
            mixed = _dg(_stack_chunks(vn[gs, :].astype(BF16), nc), wc, NT) + bs_ref[g:g + 1, :]
            y_ref[gs, :] = (u[gs, :] * _unstack_chunks(mixed, nc)).astype(BF16)

    return pl.pallas_call(
        body, name=name, grid=(t // tw,),
        in_specs=[pl.BlockSpec((2 * GMLP_WIDTH, tw), lambda n: (0, n)), _resident((GMLP_WIDTH, 1)),
                  _resident((GMLP_WIDTH, 1)), _resident((GMLP_GROUPS, BLOCK, BLOCK)), _resident((GMLP_GROUPS, BLOCK))],
        out_specs=pl.BlockSpec((GMLP_WIDTH, tw), lambda n: (0, n)),
        out_shape=jax.ShapeDtypeStruct((GMLP_WIDTH, t), BF16),
        compiler_params=_params(("arbitrary",)),
    )(zg_t, lng, lnb, w_s, b_s)


def gmlp_bwd(zg_t, dy_t, lng, lnb, w_s, b_s, name):
    t = zg_t.shape[1]
    nc = min(GMLP_CHUNKS_PER_STEP, t // BLOCK)
    tw = nc * BLOCK
    nt = t // tw

    def body(zg_ref, dy_ref, lng_ref, lnb_ref, w_ref, bs_ref, dzg_ref, dw_ref, dbs_ref, dlng_ref, dlnb_ref,
             gacc_ref, bacc_ref):
        n = pl.program_id(0)

        @pl.when(n == 0)
        def _():
            dw_ref[...] = jnp.zeros_like(dw_ref)
            dbs_ref[...] = jnp.zeros_like(dbs_ref)
            gacc_ref[...] = jnp.zeros_like(gacc_ref)
            bacc_ref[...] = jnp.zeros_like(bacc_ref)

        zg = zg_ref[...].astype(F32)
        lng_v = lng_ref[...]
        u, xhat, rstd, vn = _gmlp_parts(zg, lng_v, lnb_ref[...])
        dy = dy_ref[...].astype(F32)
        dvn_parts = []
        for g in range(GMLP_GROUPS):
            gs = slice(g * GMLP_GROUP_DIM, (g + 1) * GMLP_GROUP_DIM)
            wc = _causal(w_ref[g]).astype(BF16)
            vn_s = _stack_chunks(vn[gs, :].astype(BF16), nc)
            mixed = _unstack_chunks(_dg(vn_s, wc, NT) + bs_ref[g:g + 1, :], nc)
            dy_g = dy[gs, :]
            dmixed = dy_g * u[gs, :]
            dm_s = _stack_chunks(dmixed.astype(BF16), nc)
            dzg_ref[gs, :] = (dy_g * mixed * _gelu_grad(zg[gs, :])).astype(BF16)
            dw_ref[g] += _causal(_dg(dm_s, vn_s, TN))
            dbs_ref[g:g + 1, :] += _lane_fold(jnp.sum(dmixed, axis=0, keepdims=True))
            dvn_parts.append(_unstack_chunks(_dot(dm_s, wc), nc))
        dvn = jnp.concatenate(dvn_parts, axis=0)
        gacc_ref[...] += _lane_fold(dvn * xhat)
        bacc_ref[...] += _lane_fold(dvn)
        dxhat = dvn * lng_v
        m1 = jnp.mean(dxhat, axis=0, keepdims=True)
        m2 = jnp.mean(dxhat * xhat, axis=0, keepdims=True)
        dvv = rstd * (dxhat - m1 - xhat * m2)
        dzg_ref[GMLP_WIDTH:, :] = (dvv * _gelu_grad(zg[GMLP_WIDTH:, :])).astype(BF16)

        @pl.when(n == nt - 1)
        def _():
            dlng_ref[...] = jnp.sum(gacc_ref[...], axis=1, keepdims=True)
            dlnb_ref[...] = jnp.sum(bacc_ref[...], axis=1, keepdims=True)

    const2 = lambda n: (0, 0)
    return pl.pallas_call(
        body, name=name, grid=(nt,),
        in_specs=[pl.BlockSpec((2 * GMLP_WIDTH, tw), lambda n: (0, n)), pl.BlockSpec((GMLP_WIDTH, tw), lambda n: (0, n)),
                  _resident((GMLP_WIDTH, 1)), _resident((GMLP_WIDTH, 1)),
                  _resident((GMLP_GROUPS, BLOCK, BLOCK)), _resident((GMLP_GROUPS, BLOCK))],
        out_specs=[pl.BlockSpec((2 * GMLP_WIDTH, tw), lambda n: (0, n)),
                   pl.BlockSpec((GMLP_GROUPS, BLOCK, BLOCK), lambda n: (0, 0, 0)),
                   pl.BlockSpec((GMLP_GROUPS, BLOCK), const2),
                   pl.BlockSpec((GMLP_WIDTH, 1), const2), pl.BlockSpec((GMLP_WIDTH, 1), const2)],
        out_shape=[jax.ShapeDtypeStruct((2 * GMLP_WIDTH, t), BF16),
                   jax.ShapeDtypeStruct((GMLP_GROUPS, BLOCK, BLOCK), F32),
                   jax.ShapeDtypeStruct((GMLP_GROUPS, BLOCK), F32),
                   jax.ShapeDtypeStruct((GMLP_WIDTH, 1), F32), jax.ShapeDtypeStruct((GMLP_WIDTH, 1), F32)],
        scratch_shapes=[pltpu.VMEM((GMLP_WIDTH, BLOCK), F32), pltpu.VMEM((GMLP_WIDTH, BLOCK), F32)],
        compiler_params=_params(("arbitrary",)),
    )(zg_t, dy_t, lng, lnb, w_s, b_s)


def _rmsnorm_cols(y, gain_col):
    r = lax.rsqrt(jnp.mean(y * y, axis=0, keepdims=True) + EPS)
    n = y * r
    return n, r, n * gain_col


def mix_out_fwd(x, ya_t, yg_t, gao, ggo, wout, b_out, name):
    t, d = x.shape
    tm = _token_tile(t)

    def body(x_ref, ya_ref, yg_ref, gao_ref, ggo_ref, w_ref, b_ref, o_ref):
        _, _, ya = _rmsnorm_cols(ya_ref[...].astype(F32), gao_ref[...])
        _, _, yg = _rmsnorm_cols(yg_ref[...].astype(F32), ggo_ref[...])
        o_ref[...] = (x_ref[...] + _dg(ya.astype(BF16), w_ref[:ATTN_WIDTH, :], TN)
                      + _dg(yg.astype(BF16), w_ref[ATTN_WIDTH:, :], TN) + b_ref[...])

    return pl.pallas_call(
        body, name=name, grid=(t // tm,),
        in_specs=[pl.BlockSpec((tm, d), lambda i: (i, 0)), pl.BlockSpec((ATTN_WIDTH, tm), lambda i: (0, i)),
                  pl.BlockSpec((GMLP_WIDTH, tm), lambda i: (0, i)), _resident((ATTN_WIDTH, 1)), _resident((GMLP_WIDTH, 1)),
                  _resident((ATTN_WIDTH + GMLP_WIDTH, d)), _resident((1, d))],
        out_specs=pl.BlockSpec((tm, d), lambda i: (i, 0)),
        out_shape=jax.ShapeDtypeStruct((t, d), F32),
        compiler_params=_params(("arbitrary",)),
    )(x, ya_t, yg_t, gao, ggo, wout, b_out)


def _lane_fold(v):
    out = v[:, :LANES]
    for j in range(1, v.shape[1] // LANES):
        out = out + v[:, j * LANES:(j + 1) * LANES]
    return out


def mix_out_bwd(dx, ya_t, yg_t, gao, ggo, wout, name, deps=()):
    t, d = dx.shape
    tm = _token_tile(t)
    nt = t // tm

    def body(dx_ref, ya_ref, yg_ref, gao_ref, ggo_ref, w_ref, dya_ref, dyg_ref, y_ref, dgao_ref, dggo_ref, db_ref,
             acc_ref):
        i = pl.program_id(0)

        @pl.when(i == 0)
        def _():
            acc_ref[...] = jnp.zeros_like(acc_ref)
            db_ref[...] = jnp.zeros_like(db_ref)

        dxv = dx_ref[...]
        db_ref[...] += jnp.sum(dxv, axis=0, keepdims=True)
        dxb = dxv.astype(BF16)
        for part, (src_ref, gain_ref, dst_ref) in enumerate(((ya_ref, gao_ref, dya_ref), (yg_ref, ggo_ref, dyg_ref))):
            rows = slice(part * ATTN_WIDTH, (part + 1) * ATTN_WIDTH)
            gain = gain_ref[...]
            nrm, r, yn = _rmsnorm_cols(src_ref[...].astype(F32), gain)
            y_ref[rows, :] = yn.astype(BF16)
            dy = _dg(w_ref[rows, :], dxb, NT)
            acc_ref[rows, :] += _lane_fold(dy * nrm)
            dn = dy * gain
            dst_ref[...] = (r * (dn - nrm * jnp.mean(dn * nrm, axis=0, keepdims=True))).astype(BF16)

        @pl.when(i == nt - 1)
        def _():
            dgao_ref[...] = jnp.sum(acc_ref[:ATTN_WIDTH, :], axis=1, keepdims=True)
            dggo_ref[...] = jnp.sum(acc_ref[ATTN_WIDTH:, :], axis=1, keepdims=True)

    const2 = lambda i: (0, 0)
    feat = lambda w: pl.BlockSpec((w, tm), lambda i: (0, i))
    return pl.pallas_call(
        _drop_deps(body, 6, len(deps)), name=name, grid=(nt,),
        in_specs=[pl.BlockSpec((tm, d), lambda i: (i, 0)), feat(ATTN_WIDTH), feat(GMLP_WIDTH),
                  _resident((ATTN_WIDTH, 1)), _resident((GMLP_WIDTH, 1)), _resident((ATTN_WIDTH + GMLP_WIDTH, d))]
        + [ANY_SPEC] * len(deps),
        out_specs=[feat(ATTN_WIDTH), feat(GMLP_WIDTH), feat(ATTN_WIDTH + GMLP_WIDTH),
                   pl.BlockSpec((ATTN_WIDTH, 1), const2), pl.BlockSpec((GMLP_WIDTH, 1), const2),
                   pl.BlockSpec((1, d), const2)],
        out_shape=[jax.ShapeDtypeStruct((ATTN_WIDTH, t), BF16), jax.ShapeDtypeStruct((GMLP_WIDTH, t), BF16),
                   jax.ShapeDtypeStruct((ATTN_WIDTH + GMLP_WIDTH, t), BF16),
                   jax.ShapeDtypeStruct((ATTN_WIDTH, 1), F32), jax.ShapeDtypeStruct((GMLP_WIDTH, 1), F32),
                   jax.ShapeDtypeStruct((1, d), F32)],
        scratch_shapes=[pltpu.VMEM((ATTN_WIDTH + GMLP_WIDTH, LANES), F32)],
        compiler_params=_params(("arbitrary",)),
    )(dx, ya_t, yg_t, gao, ggo, wout, *deps)


def mix_in_bwd(x, dxo, gain, dq_t, dk_t, dv_t, dzg_t, win_t, name):
    t, d = x.shape
    w = win_t.shape[0]
    tm = _token_tile(t)
    nt = t // tm
    parts = ((0, ATTN_WIDTH), (ATTN_WIDTH, KV_WIDTH), (ATTN_WIDTH + KV_WIDTH, KV_WIDTH), (QKV_WIDTH, w - QKV_WIDTH))

    def body(x_ref, dxo_ref, g_ref, dq_ref, dk_ref, dv_ref, dzg_ref, w_ref, dx_ref, h_ref, dg_ref, db_ref, acc_ref):
        i = pl.program_id(0)

        @pl.when(i == 0)
        def _():
            acc_ref[...] = jnp.zeros_like(acc_ref)
            dg_ref[...] = jnp.zeros_like(dg_ref)

        xv = x_ref[...]
        gain_v = g_ref[...]
        r = lax.rsqrt(jnp.mean(xv * xv, axis=-1, keepdims=True) + EPS)
        h_ref[...] = (xv * r * gain_v).astype(BF16)
        dh = jnp.zeros((tm, d), F32)
        for (off, size), src_ref in zip(parts, (dq_ref, dk_ref, dv_ref, dzg_ref)):
            dp = src_ref[...]
            acc_ref[off:off + size, :] += _lane_fold(dp.astype(F32))
            dh = dh + _dg(dp, w_ref[off:off + size, :], TN)
        dx, dgain = _rmsnorm_bwd_rows(xv, r, gain_v, dh)
        dx_ref[...] = dxo_ref[...] + dx
        dg_ref[...] += dgain

        @pl.when(i == nt - 1)
        def _():
            db_ref[...] = jnp.sum(acc_ref[...], axis=1, keepdims=True)

    const2 = lambda i: (0, 0)
    tok = lambda: pl.BlockSpec((tm, d), lambda i: (i, 0))
    feat = lambda rows: pl.BlockSpec((rows, tm), lambda i: (0, i))
    return pl.pallas_call(
        body, name=name, grid=(nt,),
        in_specs=[tok(), tok(), _resident((1, d)), feat(ATTN_WIDTH), feat(KV_WIDTH), feat(KV_WIDTH),
                  feat(w - QKV_WIDTH), _resident((w, d))],
        out_specs=[tok(), tok(), pl.BlockSpec((1, d), const2), pl.BlockSpec((w, 1), const2)],
        out_shape=[jax.ShapeDtypeStruct((t, d), F32), jax.ShapeDtypeStruct((t, d), BF16),
                   jax.ShapeDtypeStruct((1, d), F32), jax.ShapeDtypeStruct((w, 1), F32)],
        scratch_shapes=[pltpu.VMEM((w, LANES), F32)],
        compiler_params=_params(("arbitrary",)),
    )(x, dxo, gain, dq_t, dk_t, dv_t, dzg_t, win_t)


def _adamw_math(w, g, m, v):
    m = ADAM_B1 * m + (1.0 - ADAM_B1) * g
    v = ADAM_B2 * v + (1.0 - ADAM_B2) * (g * g)
    m_hat = m / (1.0 - ADAM_B1 ** ADAM_STEP)
    v_hat = v / (1.0 - ADAM_B2 ** ADAM_STEP)
    delta = -ADAM_LR * (m_hat / (jnp.sqrt(v_hat) + ADAM_EPS) + ADAM_WD * w)
    return delta, m, v


def adamw_many(ws, gs, ms, vs, name):
    n = len(ws)

    def body(*refs):
        w_refs, g_refs, m_refs, v_refs = refs[:n], refs[n:2 * n], refs[2 * n:3 * n], refs[3 * n:4 * n]
        d_refs, mo_refs, vo_refs = refs[4 * n:5 * n], refs[5 * n:6 * n], refs[6 * n:]
        for k in range(n):
            delta, mn, vn = _adamw_math(w_refs[k][...], g_refs[k][...], m_refs[k][...], v_refs[k][...])
            d_refs[k][...] = delta
            mo_refs[k][...] = mn
            vo_refs[k][...] = vn

    def whole(a):
        zeros = (0,) * a.ndim
        return pl.BlockSpec(a.shape, lambda i: zeros)

    outs = pl.pallas_call(
        body, name=name, grid=(1,),
        in_specs=[whole(a) for a in ws] * 4, out_specs=[whole(a) for a in ws] * 3,
        out_shape=[jax.ShapeDtypeStruct(a.shape, F32) for a in ws] * 3,
        compiler_params=_params(("arbitrary",)),
    )(*ws, *gs, *ms, *vs)
    return outs[:n], outs[n:2 * n], outs[2 * n:]


def sum_adamw(landing, grad3d, me, w, m, v, name):
    nd, rows, cols = landing.shape
    tr = rows // 2

    def body(me_ref, l_ref, own_ref, w_ref, m_ref, v_ref, g_ref, d_ref, mo_ref, vo_ref):
        g = own_ref[...].astype(F32)
        for j in range(nd):
            g = g + l_ref[j].astype(F32)
        delta, mn, vn = _adamw_math(w_ref[...], g, m_ref[...], v_ref[...])
        g_ref[...] = g
        d_ref[...] = delta
        mo_ref[...] = mn
        vo_ref[...] = vn

    spec = lambda: pl.BlockSpec((tr, cols), lambda i, me_ref: (i, 0))
    shape = jax.ShapeDtypeStruct((rows, cols), F32)
    return pl.pallas_call(
        body, name=name,
        grid_spec=pltpu.PrefetchScalarGridSpec(
            num_scalar_prefetch=1, grid=(rows // tr,),
            in_specs=[pl.BlockSpec((nd, tr, cols), lambda i, me_ref: (0, i, 0)),
                      pl.BlockSpec((None, tr, cols), lambda i, me_ref: (me_ref[0], i, 0)), spec(), spec(), spec()],
            out_specs=[spec() for _ in range(4)]),
        out_shape=[shape] * 4,
        compiler_params=_params(("arbitrary",)),
    )(me, landing, grad3d, w, m, v)


def _mesh_place():
    x, y, c = lax.axis_index("x"), lax.axis_index("y"), lax.axis_index("c")
    return x, y, c, 4 * x + 2 * y + c


def _peer(x, y, c, k):
    return (x ^ ((k >> 2) & 1), y ^ ((k >> 1) & 1), c ^ (k & 1))


def prep_shards(mats, me, name):
    na = len(mats)

    def body(me_ref, *refs):
        in_refs, shard_refs, land_refs = refs[:na], refs[na:2 * na], refs[2 * na:]
        for a in range(na):
            val = in_refs[a][...].astype(BF16)
            shard_refs[a][...] = val
            land_refs[a][...] = val

    whole = lambda mat: pl.BlockSpec(mat.shape, lambda i, me_ref: (0, 0))
    outs = pl.pallas_call(
        body, name=name,
        grid_spec=pltpu.PrefetchScalarGridSpec(
            num_scalar_prefetch=1, grid=(1,),
            in_specs=[whole(mat) for mat in mats],
            out_specs=[whole(mat) for mat in mats]
            + [pl.BlockSpec((None,) + mat.shape, lambda i, me_ref: (me_ref[0], 0, 0)) for mat in mats]),
        out_shape=[jax.ShapeDtypeStruct(mat.shape, BF16) for mat in mats]
        + [jax.ShapeDtypeStruct((N_DEV,) + mat.shape, BF16) for mat in mats],
        compiler_params=_params(("arbitrary",)),
    )(me, *mats)
    return outs[:na], outs[na:]


def allgather_rows(shards, lands, name):
    na = n_gather = len(shards)

    def body(*refs):
        in_refs, out_refs = refs[:na], refs[2 * na:3 * na]
        send_sems, recv_sems = refs[3 * na:]
        x, y, c, me = _mesh_place()

        def copy(sem, a, origin, src, k):
            return pltpu.make_async_remote_copy(
                src_ref=src, dst_ref=out_refs[a].at[origin], send_sem=send_sems.at[sem, a],
                recv_sem=recv_sems.at[sem, a], device_id=_peer(x, y, c, k), device_id_type=pl.DeviceIdType.MESH)

        kx, ky = 4, 2

        def recv(sem, a, origin):
            copy(sem, a, origin, in_refs[a], 1).wait_recv()

        def landed(a, origin):
            return out_refs[a].at[origin]

        @pl.when(c == 1)
        def _():
            sends = []

            def send(sem, a, origin, src, k):
                cp = copy(sem, a, origin, src, k)
                cp.start()
                sends.append(cp)

            for a in range(n_gather):
                send(1, a, me, in_refs[a], kx)
                send(2, a, me, in_refs[a], ky)
                send(0, a, me, in_refs[a], 1)
            for a in range(n_gather):
                recv(0, a, me ^ 1)
                send(3, a, me ^ 1, landed(a, me ^ 1), kx)
                send(4, a, me ^ 1, landed(a, me ^ 1), ky)
            for a in range(n_gather):
                recv(1, a, me ^ kx)
                send(5, a, me ^ kx, landed(a, me ^ kx), ky)
                send(7, a, me ^ kx, landed(a, me ^ kx), 1)
                recv(2, a, me ^ ky)
                send(8, a, me ^ ky, landed(a, me ^ ky), 1)
            for a in range(n_gather):
                recv(3, a, me ^ kx ^ 1)
                send(9, a, me ^ kx ^ 1, landed(a, me ^ kx ^ 1), 1)
                recv(4, a, me ^ ky ^ 1)
                send(6, a, me ^ ky ^ 1, landed(a, me ^ ky ^ 1), kx)
                send(10, a, me ^ ky ^ 1, landed(a, me ^ ky ^ 1), 1)
            for a in range(n_gather):
                recv(5, a, me ^ 6)
                send(11, a, me ^ 6, landed(a, me ^ 6), 1)
                recv(6, a, me ^ 7)
                send(12, a, me ^ 7, landed(a, me ^ 7), 1)
            for cp in sends:
                cp.wait_send()

        @pl.when(c == 0)
        def _():
            north = me ^ 1
            own = [copy(0, a, me, in_refs[a], 1) for a in range(n_gather)]
            for cp in own:
                cp.start()
            for a in range(n_gather):
                recv(0, a, north)
                for sem, origin in ((7, north ^ kx), (8, north ^ ky), (9, north ^ kx ^ 1), (10, north ^ ky ^ 1),
                                    (11, north ^ 6), (12, north ^ 7)):
                    recv(sem, a, origin)
            for cp in own:
                cp.wait_send()

    return pl.pallas_call(
        body, name=name,
        in_specs=[ANY_SPEC] * (2 * na), out_specs=[ANY_SPEC] * na,
        out_shape=[jax.ShapeDtypeStruct(land.shape, land.dtype) for land in lands],
        input_output_aliases={na + a: a for a in range(na)},
        scratch_shapes=[pltpu.SemaphoreType.DMA((N_GATHER_ROLES, n_gather)),
                        pltpu.SemaphoreType.DMA((N_GATHER_ROLES, n_gather))],
    )(*shards, *lands)


HBM_SPEC = pl.BlockSpec(memory_space=pltpu.HBM)
SEM_SPEC = pl.BlockSpec(memory_space=pltpu.SEMAPHORE)
SIDE_EFFECT = pltpu.SideEffectType.DATAFLOW_SIDE_EFFECTING


def _hbm(v):
    return pltpu.with_memory_space_constraint(v, pltpu.HBM)


def gather_slices(src, land, me, k):
    return src, land.at[me], land.at[me ^ k]


def exchange_slices(src, land, me, k):
    return src.at[me ^ k], land.at[k - 1], land.at[k - 1]


def split_start(groups, slices, deps, name):
    sizes = [len(srcs) for srcs, _ in groups]
    flat_src = [s for srcs, _ in groups for s in srcs]
    flat_land = [l for _, lands in groups for l in lands]
    na, ng, nd = len(flat_src), len(groups), len(deps)

    def body(*refs):
        src_refs, land_refs = refs[:na], refs[na:2 * na]
        sem_refs = refs[2 * na + nd:2 * na + nd + 2 * ng]
        token_ref = refs[-1]
        x, y, c, me = _mesh_place()
        a0 = 0
        for gi, size in enumerate(sizes):
            send_sems, recv_sems = sem_refs[2 * gi], sem_refs[2 * gi + 1]
            for k in range(1, N_DEV):
                for j in range(size):
                    src, dst, _ = slices(src_refs[a0 + j], land_refs[a0 + j], me, k)
                    pltpu.make_async_remote_copy(
                        src_ref=src, dst_ref=dst, send_sem=send_sems.at[(k - 1) * size + j],
                        recv_sem=recv_sems.at[(k - 1) * size + j],
                        device_id=_peer(x, y, c, k), device_id_type=pl.DeviceIdType.MESH).start()
            a0 += size
        token_ref[...] = jnp.zeros_like(token_ref)

    sems = [pltpu.SemaphoreType.DMA(((N_DEV - 1) * size,)) for size in sizes for _ in range(2)]
    thru = [pltpu.HBM(v.shape, v.dtype) for v in flat_src + flat_land]
    outs = pl.pallas_call(
        body, name=name,
        in_specs=[HBM_SPEC] * (2 * na) + [ANY_SPEC] * nd,
        out_specs=[SEM_SPEC] * (2 * ng) + [HBM_SPEC] * (2 * na) + [pl.BlockSpec(memory_space=pltpu.VMEM)],
        out_shape=sems + thru + [jax.ShapeDtypeStruct((8, LANES), F32)],
        input_output_aliases={i: 2 * ng + i for i in range(2 * na)},
        compiler_params=pltpu.CompilerParams(has_side_effects=SIDE_EFFECT),
    )(*[_hbm(v) for v in flat_src + flat_land], *deps)
    started, a0 = [], 0
    for gi, size in enumerate(sizes):
        srcs = outs[2 * ng + a0:2 * ng + a0 + size]
        lands = outs[2 * ng + na + a0:2 * ng + na + a0 + size]
        started.append((outs[2 * gi], outs[2 * gi + 1], list(srcs), list(lands)))
        a0 += size
    return started, outs[-1]


def split_wait(started, slices, after, name):
    send_sems, recv_sems, srcs, lands = started
    na = len(srcs)

    def body(*refs):
        src_refs, land_refs = refs[:na], refs[na:2 * na]
        send_ref, recv_ref = refs[2 * na], refs[2 * na + 1]
        x, y, c, me = _mesh_place()
        for k in range(1, N_DEV):
            for j in range(na):
                src, _, got = slices(src_refs[j], land_refs[j], me, k)
                cp = pltpu.make_async_remote_copy(
                    src_ref=src, dst_ref=got, send_sem=send_ref.at[(k - 1) * na + j], recv_sem=recv_ref.at[(k - 1) * na + j],
                    device_id=_peer(x, y, c, k), device_id_type=pl.DeviceIdType.MESH)
                cp.wait_send()
                cp.wait_recv()

    outs = pl.pallas_call(
        body, name=name,
        in_specs=[HBM_SPEC] * (2 * na) + [SEM_SPEC, SEM_SPEC] + [ANY_SPEC] * len(after),
        out_specs=[HBM_SPEC] * (2 * na),
        out_shape=[pltpu.HBM(v.shape, v.dtype) for v in srcs + lands],
        input_output_aliases={i: i for i in range(2 * na)},
        compiler_params=pltpu.CompilerParams(has_side_effects=SIDE_EFFECT),
    )(*srcs, *lands, send_sems, recv_sems, *after)
    return list(outs[:na]), list(outs[na:])


def place_own(pack, me, name):
    rows, cols = pack.shape

    def body(me_ref, p_ref, o_ref):
        o_ref[...] = p_ref[...]

    return pl.pallas_call(
        body, name=name,
        grid_spec=pltpu.PrefetchScalarGridSpec(
            num_scalar_prefetch=1, grid=(1,),
            in_specs=[pl.BlockSpec((rows, cols), lambda i, me_ref: (0, 0))],
            out_specs=pl.BlockSpec((None, rows, cols), lambda i, me_ref: (me_ref[0], 0, 0))),
        out_shape=jax.ShapeDtypeStruct((N_DEV, rows, cols), F32),
        compiler_params=_params(("arbitrary",)),
    )(me, pack)


def sum_slots(slots, name):
    nd, rows, cols = slots.shape

    def body(s_ref, o_ref):
        acc = s_ref[0]
        for j in range(1, nd):
            acc = acc + s_ref[j]
        o_ref[...] = acc

    return pl.pallas_call(
        body, name=name, grid=(1,),
        in_specs=[pl.BlockSpec((nd, rows, cols), lambda i: (0, 0, 0))],
        out_specs=pl.BlockSpec((rows, cols), lambda i: (0, 0)),
        out_shape=jax.ShapeDtypeStruct((rows, cols), F32),
        compiler_params=_params(("arbitrary",)),
    )(slots)


def _pack_rows(size):
    rows = -(-size // LANES)
    return -(-rows // 8) * 8


def pack_small(parts):
    out = []
    for p in parts:
        flat = p.reshape(-1).astype(F32)
        rows = _pack_rows(flat.shape[0])
        out.append(jnp.pad(flat, (0, rows * LANES - flat.shape[0])).reshape(rows, LANES))
    return jnp.concatenate(out, axis=0)


def unpack_small(pack, shapes):
    out, off = [], 0
    for shp in shapes:
        size = 1
        for s in shp:
            size *= s
        rows = _pack_rows(size)
        out.append(pack[off:off + rows].reshape(-1)[:size].reshape(shp))
        off += rows
    return out


def local_step(x, target, small, get_w, put_g, put_small):
    col = lambda v: v.reshape(-1, 1)
    (wg1, wu1), deps = get_w(("wg1", "wu1"), ())
    sil1, gp1, s1, h1 = ffn_fwd(x, small["ffn1_norm_g"], wg1, wu1, None, "ffn1_fwd", deps)
    (wd1, win, wout), _ = get_w(("wd1", "win", "wout"), (s1,))
    qkv_t, zg_t, x1 = mix_in_fwd(x, small["mix_norm_g"], win, col(small["b_in"]), "mix_in_fwd", ffn_tail=(s1, wd1))
    ya_t = attn_fwd(qkv_t, small["attn_sinks"], "attn_fwd")
    lng, lnb = col(small["gmlp_ln_g"]), col(small["gmlp_ln_b"])
    w_s, b_s = small["gmlp_w_s"][0], small["gmlp_b_s"][0]
    yg_t = gmlp_fwd(zg_t, lng, lnb, w_s, b_s, "gmlp_fwd")
    gao, ggo = col(small["attn_out_norm_g"]), col(small["gmlp_out_norm_g"])
    x2 = mix_out_fwd(x1, ya_t, yg_t, gao, ggo, wout, small["b_out"], "mix_out_fwd")
    (wg2, wu2, wd2), _ = get_w(("wg2", "wu2", "wd2"), (x2,))
    gs = {}
    dx3, sil2, gp2, s2, h2f, gs["final_norm_g"], loss = ffn_fwd(
        x2, small["ffn2_norm_g"], wg2, wu2, wd2, "ffn2_fwd", head=(target, small["final_norm_g"].reshape(1, -1)))

    dx2, da, db, df, gs["ffn2_norm_g"] = ffn_bwd(x2, dx3, small["ffn2_norm_g"], sil2, gp2, wg2, wu2, wd2, "ffn2_bwd")
    deps = put_g({"wg2": grad_tn(da, h2f, 1.0, "ffn2_dwg"), "wu2": grad_tn(db, h2f, 1.0, "ffn2_dwu"),
                  "wd2": grad_tn(s2, df, 1.0, "ffn2_dwd")})

    dya_t, dyg_t, y_t, dgao, dggo, gs["b_out"] = mix_out_bwd(dx2, ya_t, yg_t, gao, ggo, wout, "mix_out_bwd", deps)
    gs["attn_out_norm_g"], gs["gmlp_out_norm_g"] = dgao.reshape(1, -1), dggo.reshape(1, -1)
    g_wout = grad_nn([y_t], dx2, "dwout")
    dzg_t, dws, dbs, dlng, dlnb = gmlp_bwd(zg_t, dyg_t, lng, lnb, w_s, b_s, "gmlp_bwd")
    gs["gmlp_w_s"], gs["gmlp_b_s"] = dws[None], dbs[None]
    gs["gmlp_ln_g"], gs["gmlp_ln_b"] = dlng.reshape(1, -1), dlnb.reshape(1, -1)
    dq_t, dk_t, dv_t, dsinks = attn_bwd(qkv_t, dya_t, small["attn_sinks"], "attn_bwd")
    gs["attn_sinks"] = dsinks.reshape(1, -1)
    dx1, h2, gs["mix_norm_g"], dbin = mix_in_bwd(x1, dx2, small["mix_norm_g"], dq_t, dk_t, dv_t, dzg_t, win,
                                                 "mix_in_bwd")
    gs["b_in"] = dbin.reshape(1, -1)
    deps = put_g({"wout": g_wout, "win": grad_nn([dq_t, dk_t, dv_t, dzg_t], h2, "dwin")})

    dx0, da, db, df, gs["ffn1_norm_g"] = ffn_bwd(x, dx1, small["ffn1_norm_g"], sil1, gp1, wg1, wu1, wd1, "ffn1_bwd",
                                             deps)
    deps = put_small(gs, loss)
    deps = put_g({"wg1": grad_tn(da, h1, 1.0, "ffn1_dwg", deps)})
    deps = put_g({"wu1": grad_tn(db, h1, 1.0, "ffn1_dwu", deps)})
    put_g({"wd1": grad_tn(s1, df, 1.0, "ffn1_dwd", deps)})
    return dx0


SMALL_NAMES = ["ffn1_norm_g", "mix_norm_g", "b_in", "attn_sinks", "gmlp_ln_g", "gmlp_ln_b", "gmlp_w_s", "gmlp_b_s",
               "attn_out_norm_g", "gmlp_out_norm_g", "b_out", "ffn2_norm_g", "final_norm_g"]
BIG_NAMES = {"wg1": ("ffn1_w_gate", True), "wu1": ("ffn1_w_up", True), "wd1": ("ffn1_w_down", False),
             "win": ("w_in", True), "wout": ("w_out", False),
             "wg2": ("ffn2_w_gate", True), "wu2": ("ffn2_w_up", True), "wd2": ("ffn2_w_down", False)}
WEIGHT_ORDER = ["ffn1_norm_g", "ffn1_w_gate", "ffn1_w_up", "ffn1_w_down", "mix_norm_g", "w_in", "b_in", "attn_sinks",
                "gmlp_ln_g", "gmlp_ln_b", "gmlp_w_s", "gmlp_b_s", "attn_out_norm_g", "gmlp_out_norm_g", "w_out",
                "b_out", "ffn2_norm_g", "ffn2_w_gate", "ffn2_w_up", "ffn2_w_down", "final_norm_g"]


def kernel(x, ffn1_norm_g, ffn1_w_gate, ffn1_w_up, ffn1_w_down, mix_norm_g, w_in, b_in, attn_sinks, gmlp_ln_g, gmlp_ln_b, gmlp_w_s, gmlp_b_s, attn_out_norm_g, gmlp_out_norm_g, w_out, b_out, ffn2_norm_g, ffn2_w_gate, ffn2_w_up, ffn2_w_down, final_norm_g, loss_target, m_ffn1_norm_g, m_ffn1_w_gate, m_ffn1_w_up, m_ffn1_w_down, m_mix_norm_g, m_w_in, m_b_in, m_attn_sinks, m_gmlp_ln_g, m_gmlp_ln_b, m_gmlp_w_s, m_gmlp_b_s, m_attn_out_norm_g, m_gmlp_out_norm_g, m_w_out, m_b_out, m_ffn2_norm_g, m_ffn2_w_gate, m_ffn2_w_up, m_ffn2_w_down, m_final_norm_g, v_ffn1_norm_g, v_ffn1_w_gate, v_ffn1_w_up, v_ffn1_w_down, v_mix_norm_g, v_w_in, v_b_in, v_attn_sinks, v_gmlp_ln_g, v_gmlp_ln_b, v_gmlp_w_s, v_gmlp_b_s, v_attn_out_norm_g, v_gmlp_out_norm_g, v_w_out, v_b_out, v_ffn2_norm_g, v_ffn2_w_gate, v_ffn2_w_up, v_ffn2_w_down, v_final_norm_g):
    args = dict(locals())
    w = {n: args[n] for n in WEIGHT_ORDER}
    m = {n: args["m_" + n] for n in WEIGHT_ORDER}
    v = {n: args["v_" + n] for n in WEIGHT_ORDER}

    d_model = x.shape[-1]
    flat = lambda g: g.reshape(-1, d_model)
    me = _mesh_place()[3].astype(jnp.int32).reshape(1)
    first = ("wg1", "wu1")
    later = (("wd1", "win", "wout"), ("wg2", "wu2", "wd2"))
    gathers, exchanges, last_token = {}, [], []

    order = list(first) + [k for grp in later for k in grp]
    mats = [w[BIG_NAMES[k][0]][0].T if BIG_NAMES[k][1] else w[BIG_NAMES[k][0]][0] for k in order]
    shard_list, land_list = prep_shards(mats, me, "prep_shards")
    shards, zones = dict(zip(order, shard_list)), dict(zip(order, land_list))

    def get_w(keys, after):
        if keys == first:
            full = allgather_rows([shards[k] for k in first], [zones[k] for k in first], "allgather_ffn1")
            started, token = split_start([([shards[k] for k in grp], [zones[k] for k in grp]) for grp in later],
                                         gather_slices, (full[0],), "gather_start")
            gathers.update(zip(later, started))
            return [flat(g) for g in full], (token,)
        _, lands = split_wait(gathers[keys], gather_slices, after, "gather_wait_" + keys[0])
        return [flat(land) for land in lands], ()

    def put_g(gb):
        keys = tuple(gb)
        g3 = [gb[k].reshape(N_DEV, -1, d_model) for k in keys]
        lands = [lax.empty((N_DEV - 1,) + g.shape[1:], BF16) for g in g3]
        started, token = split_start([(g3, lands)], exchange_slices, (), "exchange_start_" + keys[0])
        exchanges.append((keys, started[0]))
        last_token[:] = [token]
        return (token,)

    small_started = []

    def put_small(gs, loss):
        pack = pack_small([gs[n] for n in SMALL_NAMES] + [loss[:, :1]])
        started, token = split_start([([pack], [place_own(pack, me, "place_small")])], gather_slices, (), "small_start")
        small_started[:] = started
        return (token,)

    small = {n: w[n] for n in SMALL_NAMES}
    grad_x = local_step(x[0], loss_target[0], small, get_w, put_g, put_small)

    grads, deltas, new_m, new_v = {}, {}, {}, {}
    shapes = [w[n].shape for n in SMALL_NAMES]
    _, (slots,) = split_wait(small_started[0], gather_slices, tuple(last_token), "small_wait")
    total = sum_slots(slots, "sum_small")
    small_g = unpack_small(total, shapes + [(1, 1)])
    loss_total = small_g[-1].reshape(())
    d_, m_, v_ = adamw_many([w[n] for n in SMALL_NAMES], small_g[:-1], [m[n] for n in SMALL_NAMES],
                            [v[n] for n in SMALL_NAMES], "adamw_small")
    for n, g_, dd, mm, vv in zip(SMALL_NAMES, small_g[:-1], d_, m_, v_):
        grads[n], deltas[n], new_m[n], new_v[n] = g_, dd, mm, vv

    after = tuple(arr for n in SMALL_NAMES for arr in (grads[n], deltas[n], new_m[n], new_v[n]))
    for keys, started in exchanges:
        g3, lands = split_wait(started, exchange_slices, after, "exchange_wait_" + keys[0])
        for key, own, land in zip(keys, g3, lands):
            pname, transposed = BIG_NAMES[key]
            to_rows = (lambda p: p[0].T) if transposed else (lambda p: p[0])
            from_rows = (lambda r: r.T[None]) if transposed else (lambda r: r[None])
            g, d_, m_, v_ = sum_adamw(land, own, me, to_rows(w[pname]), to_rows(m[pname]), to_rows(v[pname]),
                                      "adamw_" + key)
            after = after + (g,)
            grads[pname], deltas[pname], new_m[pname], new_v[pname] = (from_rows(g), from_rows(d_), from_rows(m_),
                                                                        from_rows(v_))

    return (loss_total, grad_x[None], *[grads[n] for n in WEIGHT_ORDER], *[deltas[n] for n in WEIGHT_ORDER],
            *[new_m[n] for n in WEIGHT_ORDER], *[new_v[n] for n in WEIGHT_ORDER])
```

```python
import functools

import jax
import jax.numpy as jnp
from jax import lax
from jax.experimental import pallas as pl
from jax.experimental.pallas import tpu as pltpu

F32 = jnp.float32
BF16 = jnp.bfloat16

N_DEV = 8
N_Q_HEADS = 8
N_KV_HEADS = 2
HEAD_DIM = 64
ATTN_WIDTH = N_Q_HEADS * HEAD_DIM
KV_WIDTH = N_KV_HEADS * HEAD_DIM
QKV_WIDTH = ATTN_WIDTH + 2 * KV_WIDTH
BLOCK = 128
GMLP_GROUPS = 8
GMLP_GROUP_DIM = 64
GMLP_WIDTH = GMLP_GROUPS * GMLP_GROUP_DIM
EPS = 1e-6
FFN_RES = 0.5
ATTN_SCALE = HEAD_DIM ** -0.5
MASK_VALUE = -1e30

ADAM_LR = 0.001
ADAM_B1 = 0.9
ADAM_B2 = 0.999
ADAM_EPS = 1e-08
ADAM_WD = 0.01
ADAM_STEP = 10

V7X_VMEM_BYTES = 64 * 1024 * 1024
VMEM_LIMIT = 62 * 1024 * 1024
LANES = 128
MXU_WIDTH = 256
FFN_BWD_TOKENS = 256

NT = (((1,), (1,)), ((), ()))
TN = (((0,), (0,)), ((), ()))


def _dot(a, b):
    return jnp.dot(a, b, preferred_element_type=F32)


def _dg(a, b, dims):
    return lax.dot_general(a, b, dims, preferred_element_type=F32)


def _params(semantics=None):
    return pltpu.CompilerParams(dimension_semantics=semantics, vmem_limit_bytes=VMEM_LIMIT)


def _resident(shape):
    zeros = (0,) * len(shape)
    return pl.BlockSpec(shape, lambda *_: zeros, pipeline_mode=pl.Buffered(1))


def _drop_deps(body, n_in, n_deps):
    return lambda *refs: body(*refs[:n_in], *refs[n_in + n_deps:])


ANY_SPEC = pl.BlockSpec(memory_space=pl.ANY)


TOKEN_TILE = 512
MIX_OUT_BWD_TOKENS = 1024
GRAD_TOKENS_PER_STEP = 2048
GRAD_ROW_TILE_CAP = 1408
N_GATHER_ROLES = 13


def _token_tile(t):
    return min(t, TOKEN_TILE)


def _f_chunk(f):
    for c in (256, 128):
        if f % c == 0:
            return c
    return f


def _loss_head(xv, target, gain):
    r = lax.rsqrt(jnp.mean(xv * xv, axis=-1, keepdims=True) + EPS)
    err = xv * r * gain - target
    loss = 0.5 * jnp.sum(jnp.mean(err * err, axis=-1, keepdims=True), axis=0, keepdims=True)
    dx, dgain = _rmsnorm_bwd_rows(xv, r, gain, err * (1.0 / xv.shape[-1]))
    return dx, dgain, loss


def ffn_fwd(x, gain, wg_t, wu_t, wd, name, deps=(), head=None):
    t, d = x.shape
    f = wg_t.shape[0]
    tm = _token_tile(t)
    fc = _f_chunk(f)
    weights = [wg_t, wu_t] + ([] if wd is None else [wd])
    n_in = 2 + len(weights) + (0 if head is None else 2)
    n_x = 0 if wd is None else 1

    def body(x_ref, g_ref, wg_ref, wu_ref, *rest):
        sil_ref, gp_ref, s_ref, h_ref = rest[n_in - 4 + n_x:n_in + n_x]
        xv = x_ref[...]
        r = lax.rsqrt(jnp.mean(xv * xv, axis=-1, keepdims=True) + EPS)
        h = (xv * r * g_ref[...]).astype(BF16)
        h_ref[...] = h
        for c in range(f // fc):
            sl = slice(c * fc, (c + 1) * fc)
            a = _dg(h, wg_ref[sl, :], NT)
            b = _dg(h, wu_ref[sl, :], NT)
            sig = jax.nn.sigmoid(a)
            sil = a * sig
            sil_ref[:, sl] = sil.astype(BF16)
            gp_ref[:, sl] = (b * (sig + sil - sil * sig)).astype(BF16)
            s_ref[:, sl] = (sil * b).astype(BF16)
        if wd is None:
            return
        wd_ref, xo_ref = rest[0], rest[n_in - 4]
        xo = xv + FFN_RES * _dot(s_ref[...], wd_ref[...])
        if head is None:
            xo_ref[...] = xo
        else:
            t_ref, hg_ref = rest[1:3]
            dg_ref, loss_ref = rest[-2:]

            @pl.when(pl.program_id(0) == 0)
            def _():
                dg_ref[...] = jnp.zeros_like(dg_ref)
                loss_ref[...] = jnp.zeros_like(loss_ref)

            dx, dgain, loss = _loss_head(xo, t_ref[...], hg_ref[...])
            xo_ref[...] = dx
            dg_ref[...] += dgain
            loss_ref[...] += loss

    tok = lambda: pl.BlockSpec((tm, d), lambda i: (i, 0))
    wide = lambda: pl.BlockSpec((tm, f), lambda i: (i, 0))
    const2 = lambda i: (0, 0)
    in_specs = [tok(), _resident((1, d))] + [_resident((f, d)) for _ in weights]
    out_specs = [tok()] * n_x + [wide(), wide(), wide(), tok()]
    out_shape = ([jax.ShapeDtypeStruct((t, d), F32)] * n_x + [jax.ShapeDtypeStruct((t, f), BF16)] * 3
                 + [jax.ShapeDtypeStruct((t, d), BF16)])
    operands = [x, gain] + weights
    if head is not None:
        in_specs += [tok(), _resident((1, d))]
        out_specs += [pl.BlockSpec((1, d), const2), pl.BlockSpec((1, LANES), const2)]
        out_shape += [jax.ShapeDtypeStruct((1, d), F32), jax.ShapeDtypeStruct((1, LANES), F32)]
        operands += list(head)
    return pl.pallas_call(
        _drop_deps(body, n_in, len(deps)), name=name, grid=(t // tm,),
        in_specs=in_specs + [ANY_SPEC] * len(deps), out_specs=out_specs, out_shape=out_shape,
        compiler_params=_params(("arbitrary",)),
    )(*operands, *deps)


def _rmsnorm_bwd_rows(xv, r, gain, dh):
    n = xv * r
    dgain = jnp.sum(dh * n, axis=0, keepdims=True)
    dn = dh * gain
    dx = r * (dn - n * jnp.mean(dn * n, axis=-1, keepdims=True))
    return dx, dgain


def ffn_bwd(x, dxo, gain, sil, gp, wg_t, wu_t, wd, name, deps=()):
    t, d = x.shape
    f = wd.shape[0]
    tm = min(t, FFN_BWD_TOKENS)
    nt = t // tm
    chunks = [(off, min(MXU_WIDTH, f - off)) for off in range(0, f, MXU_WIDTH)]

    def body(xp_ref, dxp_ref, dxo_ref, g_ref, sil_ref, gp_ref, wg_ref, wu_ref, wd_ref,
             dx_ref, da_ref, db_ref, df_ref, dg_ref, dh_ref):
        i = pl.program_id(0)
        gain_v = g_ref[...]

        @pl.when(i == 0)
        def _():
            dh_ref[...] = jnp.zeros_like(dh_ref)
            dg_ref[...] = jnp.zeros_like(dg_ref)

        def finish_previous(slot):
            xv = xp_ref[...]
            r = lax.rsqrt(jnp.mean(xv * xv, axis=-1, keepdims=True) + EPS)
            dx, dgain = _rmsnorm_bwd_rows(xv, r, gain_v, dh_ref[slot])
            dx_ref[...] = dxp_ref[...] + dx
            dg_ref[...] += dgain

        @pl.when(i < nt)
        def _():
            df = (FFN_RES * dxo_ref[...]).astype(BF16)
            df_ref[...] = df
            for off, size in chunks:
                sl = slice(off, off + size)
                ds = _dg(df, wd_ref[sl, :], NT)
                da_ref[:, sl] = (ds * gp_ref[:, sl].astype(F32)).astype(BF16)
                db_ref[:, sl] = (ds * sil_ref[:, sl].astype(F32)).astype(BF16)
            dh = _dot(da_ref[...], wg_ref[...]) + _dot(db_ref[...], wu_ref[...])
            finish_previous((i + 1) % 2)
            dh_ref[i % 2] = dh

        @pl.when(i == nt)
        def _():
            finish_previous((nt - 1) % 2)

    prev = lambda i: (jnp.maximum(i - 1, 0), 0)
    cur = lambda i: (jnp.minimum(i, nt - 1), 0)
    return pl.pallas_call(
        _drop_deps(body, 9, len(deps)), name=name, grid=(nt + 1,),
        in_specs=[pl.BlockSpec((tm, d), prev), pl.BlockSpec((tm, d), prev), pl.BlockSpec((tm, d), cur),
                  _resident((1, d)), pl.BlockSpec((tm, f), cur), pl.BlockSpec((tm, f), cur),
                  _resident((f, d)), _resident((f, d)), _resident((f, d))] + [ANY_SPEC] * len(deps),
        out_specs=[pl.BlockSpec((tm, d), prev), pl.BlockSpec((tm, f), cur), pl.BlockSpec((tm, f), cur),
                   pl.BlockSpec((tm, d), cur), pl.BlockSpec((1, d), lambda i: (0, 0))],
        out_shape=[jax.ShapeDtypeStruct((t, d), F32), jax.ShapeDtypeStruct((t, f), BF16),
                   jax.ShapeDtypeStruct((t, f), BF16), jax.ShapeDtypeStruct((t, d), BF16),
                   jax.ShapeDtypeStruct((1, d), F32)],
        scratch_shapes=[pltpu.VMEM((2, tm, d), F32)],
        compiler_params=_params(("arbitrary",)),
    )(x, dxo, dxo, gain, sil, gp, wg_t, wu_t, wd, *deps)


def _row_tile(m, cap):
    if m <= cap:
        return m
    best = None
    for bm in range(LANES, cap + 1, LANES):
        if m % bm == 0:
            best = bm
    return best if best is not None else m


def grad_tn(a, b, scale, name, deps=()):
    t, m = a.shape
    n = b.shape[1]
    bm = _row_tile(m, GRAD_ROW_TILE_CAP)
    bk = min(t, GRAD_TOKENS_PER_STEP)
    nk = t // bk

    def body(a_ref, b_ref, o_ref, acc_ref):
        k = pl.program_id(1)

        @pl.when(k == 0)
        def _():
            acc_ref[...] = jnp.zeros_like(acc_ref)

        acc_ref[...] += _dg(a_ref[...].astype(BF16), b_ref[...].astype(BF16), TN)

        @pl.when(k == nk - 1)
        def _():
            o_ref[...] = (scale * acc_ref[...]).astype(BF16)

    return pl.pallas_call(
        _drop_deps(body, 2, len(deps)), name=name, grid=(m // bm, nk),
        in_specs=[pl.BlockSpec((bk, bm), lambda i, k: (k, i)), pl.BlockSpec((bk, n), lambda i, k: (k, 0))]
        + [ANY_SPEC] * len(deps),
        out_specs=pl.BlockSpec((bm, n), lambda i, k: (i, 0)),
        out_shape=jax.ShapeDtypeStruct((m, n), BF16),
        scratch_shapes=[pltpu.VMEM((bm, n), F32)],
        compiler_params=_params(("arbitrary", "arbitrary")),
    )(a, b, *deps)


def grad_nn(a_list, b, name):
    t, n = b.shape
    ms = [a.shape[0] for a in a_list]
    m = sum(ms)
    bk = min(t, GRAD_TOKENS_PER_STEP)
    nk = t // bk
    na = len(a_list)

    def body(*refs):
        a_refs, b_ref, o_ref, acc_ref = refs[:na], refs[na], refs[na + 1], refs[na + 2]
        k = pl.program_id(0)

        @pl.when(k == 0)
        def _():
            acc_ref[...] = jnp.zeros_like(acc_ref)

        bv = b_ref[...].astype(BF16)
        off = 0
        for a_ref, mi in zip(a_refs, ms):
            acc_ref[off:off + mi, :] += _dot(a_ref[...].astype(BF16), bv)
            off += mi

        @pl.when(k == nk - 1)
        def _():
            o_ref[...] = acc_ref[...].astype(BF16)

    return pl.pallas_call(
        body, name=name, grid=(nk,),
        in_specs=[pl.BlockSpec((mi, bk), lambda k: (0, k)) for mi in ms] + [pl.BlockSpec((bk, n), lambda k: (k, 0))],
        out_specs=pl.BlockSpec((m, n), lambda k: (0, 0)),
        out_shape=jax.ShapeDtypeStruct((m, n), BF16),
        scratch_shapes=[pltpu.VMEM((m, n), F32)],
        compiler_params=_params(("arbitrary",)),
    )(*a_list, b)


def mix_in_fwd(x, gain, win_t, b_in_col, name, ffn_tail=None):
    t, d = x.shape
    w = win_t.shape[0]
    tm = _token_tile(t)

    def body(x_ref, g_ref, w_ref, bias_ref, *rest):
        xv = x_ref[...]
        if ffn_tail is not None:
            s_ref, wd_ref, xo_ref = rest[0], rest[1], rest[4]
            xv = xv + FFN_RES * _dot(s_ref[...], wd_ref[...])
            xo_ref[...] = xv
        qkv_ref, zg_ref = rest[-3:-1] if ffn_tail is not None else rest
        r = lax.rsqrt(jnp.mean(xv * xv, axis=-1, keepdims=True) + EPS)
        h = (xv * r * g_ref[...]).astype(BF16)
        qkv_ref[...] = (_dg(w_ref[:QKV_WIDTH, :], h, NT) + bias_ref[:QKV_WIDTH, :]).astype(BF16)
        zg_ref[...] = (_dg(w_ref[QKV_WIDTH:, :], h, NT) + bias_ref[QKV_WIDTH:, :]).astype(BF16)

    tok = lambda: pl.BlockSpec((tm, d), lambda i: (i, 0))
    in_specs = [tok(), _resident((1, d)), _resident((w, d)), _resident((w, 1))]
    out_specs = [pl.BlockSpec((QKV_WIDTH, tm), lambda i: (0, i)), pl.BlockSpec((w - QKV_WIDTH, tm), lambda i: (0, i))]
    out_shape = [jax.ShapeDtypeStruct((QKV_WIDTH, t), BF16), jax.ShapeDtypeStruct((w - QKV_WIDTH, t), BF16)]
    operands = [x, gain, win_t, b_in_col]
    if ffn_tail is not None:
        f = ffn_tail[1].shape[0]
        in_specs += [pl.BlockSpec((tm, f), lambda i: (i, 0)), _resident((f, d))]
        out_specs.append(tok())
        out_shape.append(jax.ShapeDtypeStruct((t, d), F32))
        operands += list(ffn_tail)
    return pl.pallas_call(
        body, name=name, grid=(t // tm,), in_specs=in_specs, out_specs=out_specs, out_shape=out_shape,
        compiler_params=_params(("arbitrary",)),
    )(*operands)


ATTN_REP = N_Q_HEADS // N_KV_HEADS
ATTN_BLOCKS_PER_STEP = 8


def _attn_tiles(t):
    nbs = min(ATTN_BLOCKS_PER_STEP, t // BLOCK)
    return nbs, nbs * BLOCK, t // (nbs * BLOCK)


def _attn_specs(nbs, tiles):
    last = tiles - 1
    cur = lambda n: jnp.minimum(n, last)
    prev = lambda n: jnp.maximum(jnp.minimum(n, last) * nbs - 1, 0)
    k_row = ATTN_WIDTH // KV_WIDTH
    tw = nbs * BLOCK
    return [
        pl.BlockSpec((ATTN_WIDTH, tw), lambda n: (0, cur(n))),
        pl.BlockSpec((KV_WIDTH, BLOCK), lambda n: (k_row, prev(n))),
        pl.BlockSpec((KV_WIDTH, tw), lambda n: (k_row, cur(n))),
        pl.BlockSpec((KV_WIDTH, BLOCK), lambda n: (k_row + 1, prev(n))),
        pl.BlockSpec((KV_WIDTH, tw), lambda n: (k_row + 1, cur(n))),
    ]


def _cols(b):
    return slice(b * BLOCK, (b + 1) * BLOCK)


def _band_rows(prev_ref, tile_ref, gs, b):
    before = prev_ref[gs, :] if b == 0 else tile_ref[gs, _cols(b - 1)]
    return before, tile_ref[gs, _cols(b)]


def _group_rows(ref, g, b):
    first = g * ATTN_REP
    return jnp.concatenate([ref[(first + r) * HEAD_DIM:(first + r + 1) * HEAD_DIM, _cols(b)] for r in range(ATTN_REP)],
                           axis=1)


def _ungroup_rows(ref, g, b, val):
    first = g * ATTN_REP
    for r in range(ATTN_REP):
        ref[(first + r) * HEAD_DIM:(first + r + 1) * HEAD_DIM, _cols(b)] = val[:, r * BLOCK:(r + 1) * BLOCK]


def _attn_upper():
    key = lax.broadcasted_iota(jnp.int32, (BLOCK, ATTN_REP * BLOCK), 0)
    qry = lax.broadcasted_iota(jnp.int32, (BLOCK, ATTN_REP * BLOCK), 1) & (BLOCK - 1)
    return key > qry


def _attn_probs(q, kp, kc, sink_ref, g, upper, no_prev):
    s = jnp.where(upper, _dg(kp, q, TN), _dg(kc, q, TN)) * ATTN_SCALE
    if no_prev is not None:
        s = jnp.where(upper & no_prev, MASK_VALUE, s)
    sink = jnp.concatenate([jnp.full((1, BLOCK), sink_ref[0, g * ATTN_REP + r], F32) for r in range(ATTN_REP)], axis=1)
    m = jnp.maximum(jnp.max(s, axis=0, keepdims=True), sink)
    e = jnp.exp(s - m)
    es = jnp.exp(sink - m)
    inv = 1.0 / (jnp.sum(e, axis=0, keepdims=True) + es)
    return e * inv, es * inv


def _split_band(upper, val):
    return jnp.where(upper, val, 0.0).astype(BF16), jnp.where(upper, 0.0, val).astype(BF16)


def attn_fwd(qkv_t, sinks, name):
    t = qkv_t.shape[1]
    nbs, tw, tiles = _attn_tiles(t)

    def body(sink_ref, q_ref, kp_ref, kc_ref, vp_ref, vc_ref, y_ref):
        n = pl.program_id(0)
        upper = _attn_upper()
        for b in range(nbs):
            for g in range(N_KV_HEADS):
                gs = slice(g * HEAD_DIM, (g + 1) * HEAD_DIM)
                kp, kc = _band_rows(kp_ref, kc_ref, gs, b)
                vp, vc = _band_rows(vp_ref, vc_ref, gs, b)
                p, _ = _attn_probs(_group_rows(q_ref, g, b), kp, kc, sink_ref, g, upper, n == 0 if b == 0 else None)
                pp, pc = _split_band(upper, p)
                _ungroup_rows(y_ref, g, b, (_dot(vp, pp) + _dot(vc, pc)).astype(BF16))

    return pl.pallas_call(
        body, name=name, grid=(tiles,),
        in_specs=[pl.BlockSpec(memory_space=pltpu.SMEM)] + _attn_specs(nbs, tiles),
        out_specs=pl.BlockSpec((ATTN_WIDTH, tw), lambda n: (0, n)),
        out_shape=jax.ShapeDtypeStruct((ATTN_WIDTH, t), BF16),
        compiler_params=_params(("arbitrary",)),
    )(sinks, qkv_t, qkv_t, qkv_t, qkv_t, qkv_t)


def attn_bwd(qkv_t, dy_t, sinks, name):
    t = qkv_t.shape[1]
    nbs, tw, tiles = _attn_tiles(t)
    kept = slice(0, tw - BLOCK)

    def body(sink_ref, q_ref, kp_ref, kc_ref, vp_ref, vc_ref, do_ref, dq_ref, dk_ref, dv_ref, dsink_ref,
             ck_ref, cv_ref, sacc_ref):
        n = pl.program_id(0)

        @pl.when(n == 0)
        def _():
            sacc_ref[...] = jnp.zeros_like(sacc_ref)

        @pl.when((n > 0) & (n < tiles))
        def _():
            if nbs > 1:
                dk_ref[:, kept] = ck_ref[:, kept].astype(BF16)
                dv_ref[:, kept] = cv_ref[:, kept].astype(BF16)

        @pl.when(n < tiles)
        def _():
            upper = _attn_upper()
            for b in range(nbs):
                for g in range(N_KV_HEADS):
                    gs = slice(g * HEAD_DIM, (g + 1) * HEAD_DIM)
                    kp, kc = _band_rows(kp_ref, kc_ref, gs, b)
                    vp, vc = _band_rows(vp_ref, vc_ref, gs, b)
                    q = _group_rows(q_ref, g, b)
                    do = _group_rows(do_ref, g, b)
                    p, ps = _attn_probs(q, kp, kc, sink_ref, g, upper, n == 0 if b == 0 else None)
                    dp = jnp.where(upper, _dg(vp, do, TN), _dg(vc, do, TN))
                    delta = jnp.sum(p * dp, axis=0, keepdims=True)
                    dsink = -(ps * delta)
                    for r in range(ATTN_REP):
                        hd = g * ATTN_REP + r
                        sacc_ref[hd:hd + 1, :] += dsink[:, r * BLOCK:(r + 1) * BLOCK]
                    dsp, dsc = _split_band(upper, p * (dp - delta))
                    pp, pc = _split_band(upper, p)
                    _ungroup_rows(dq_ref, g, b, (ATTN_SCALE * (_dot(kp, dsp) + _dot(kc, dsc))).astype(BF16))
                    dk_before = ATTN_SCALE * _dg(q, dsp, NT)
                    dv_before = _dg(do, pp, NT)
                    if b == 0:
                        @pl.when(n > 0)
                        def _():
                            dk_ref[gs, _cols(nbs - 1)] = (ck_ref[gs, _cols(nbs - 1)] + dk_before).astype(BF16)
                            dv_ref[gs, _cols(nbs - 1)] = (cv_ref[gs, _cols(nbs - 1)] + dv_before).astype(BF16)
                    else:
                        ck_ref[gs, _cols(b - 1)] += dk_before
                        cv_ref[gs, _cols(b - 1)] += dv_before
                    ck_ref[gs, _cols(b)] = ATTN_SCALE * _dg(q, dsc, NT)
                    cv_ref[gs, _cols(b)] = _dg(do, pc, NT)

        @pl.when(n == tiles)
        def _():
            dk_ref[...] = ck_ref[...].astype(BF16)
            dv_ref[...] = cv_ref[...].astype(BF16)
            dsink_ref[...] = jnp.sum(sacc_ref[...], axis=1, keepdims=True)

    last = tiles - 1
    done = lambda n: jnp.maximum(n - 1, 0)
    outs = pl.pallas_call(
        body, name=name, grid=(tiles + 1,),
        in_specs=[pl.BlockSpec(memory_space=pltpu.SMEM)] + _attn_specs(nbs, tiles)
        + [pl.BlockSpec((ATTN_WIDTH, tw), lambda n: (0, jnp.minimum(n, last)))],
        out_specs=[pl.BlockSpec((ATTN_WIDTH, tw), lambda n: (0, jnp.minimum(n, last))),
                   pl.BlockSpec((KV_WIDTH, tw), lambda n: (0, done(n))),
                   pl.BlockSpec((KV_WIDTH, tw), lambda n: (0, done(n))),
                   pl.BlockSpec((N_Q_HEADS, 1), lambda n: (0, 0))],
        out_shape=[jax.ShapeDtypeStruct((ATTN_WIDTH, t), BF16), jax.ShapeDtypeStruct((KV_WIDTH, t), BF16),
                   jax.ShapeDtypeStruct((KV_WIDTH, t), BF16), jax.ShapeDtypeStruct((N_Q_HEADS, 1), F32)],
        scratch_shapes=[pltpu.VMEM((KV_WIDTH, tw), F32), pltpu.VMEM((KV_WIDTH, tw), F32),
                        pltpu.VMEM((N_Q_HEADS, BLOCK), F32)],
        compiler_params=_params(("arbitrary",)),
    )(sinks, qkv_t, qkv_t, qkv_t, qkv_t, qkv_t, dy_t)
    return outs


GELU_C = 0.7978845608028654
GELU_A = 0.044715


def _gelu(x):
    return 0.5 * x * (1.0 + jnp.tanh(GELU_C * (x + GELU_A * x * x * x)))


def _gelu_grad(x):
    th = jnp.tanh(GELU_C * (x + GELU_A * x * x * x))
    return 0.5 * (1.0 + th) + 0.5 * x * (1.0 - th * th) * GELU_C * (1.0 + 3.0 * GELU_A * x * x)


def _gmlp_parts(zg, lng, lnb):
    z = _gelu(zg)
    u = z[:GMLP_WIDTH, :]
    vv = z[GMLP_WIDTH:, :]
    mu = jnp.mean(vv, axis=0, keepdims=True)
    xc = vv - mu
    rstd = lax.rsqrt(jnp.mean(xc * xc, axis=0, keepdims=True) + EPS)
    xhat = xc * rstd
    return u, xhat, rstd, xhat * lng + lnb


def _causal(w):
    row = lax.broadcasted_iota(jnp.int32, (BLOCK, BLOCK), 0)
    col = lax.broadcasted_iota(jnp.int32, (BLOCK, BLOCK), 1)
    return jnp.where(col <= row, w, 0.0)


GMLP_CHUNKS_PER_STEP = 8


def _stack_chunks(v, nc):
    return jnp.concatenate([v[:, c * BLOCK:(c + 1) * BLOCK] for c in range(nc)], axis=0)


def _unstack_chunks(v, nc):
    rows = v.shape[0] // nc
    return jnp.concatenate([v[c * rows:(c + 1) * rows, :] for c in range(nc)], axis=1)


def gmlp_fwd(zg_t, lng, lnb, w_s, b_s, name):
    t = zg_t.shape[1]
    nc = min(GMLP_CHUNKS_PER_STEP, t // BLOCK)
    tw = nc * BLOCK

    def body(zg_ref, lng_ref, lnb_ref, w_ref, bs_ref, y_ref):
        u, _, _, vn = _gmlp_parts(zg_ref[...].astype(F32), lng_ref[...], lnb_ref[...])
        for g in range(GMLP_GROUPS):
            gs = slice(g * GMLP_GROUP_DIM, (g + 1) * GMLP_GROUP_DIM)
            wc = _causal(w_ref[g]).astype(BF16)
            mixed = _dg(_stack_chunks(vn[gs, :].astype(BF16), nc), wc, NT) + bs_ref[g:g + 1, :]
            y_ref[gs, :] = (u[gs, :] * _unstack_chunks(mixed, nc)).astype(BF16)

    return pl.pallas_call(
        body, name=name, grid=(t // tw,),
        in_specs=[pl.BlockSpec((2 * GMLP_WIDTH, tw), lambda n: (0, n)), _resident((GMLP_WIDTH, 1)),
                  _resident((GMLP_WIDTH, 1)), _resident((GMLP_GROUPS, BLOCK, BLOCK)), _resident((GMLP_GROUPS, BLOCK))],
        out_specs=pl.BlockSpec((GMLP_WIDTH, tw), lambda n: (0, n)),
        out_shape=jax.ShapeDtypeStruct((GMLP_WIDTH, t), BF16),
        compiler_params=_params(("arbitrary",)),
    )(zg_t, lng, lnb, w_s, b_s)


def gmlp_bwd(zg_t, dy_t, lng, lnb, w_s, b_s, name):
    t = zg_t.shape[1]
    nc = min(GMLP_CHUNKS_PER_STEP, t // BLOCK)
    tw = nc * BLOCK
    nt = t // tw

    def body(zg_ref, dy_ref, lng_ref, lnb_ref, w_ref, bs_ref, dzg_ref, dw_ref, dbs_ref, dlng_ref, dlnb_ref,
             gacc_ref, bacc_ref):
        n = pl.program_id(0)

        @pl.when(n == 0)
        def _():
            dw_ref[...] = jnp.zeros_like(dw_ref)
            dbs_ref[...] = jnp.zeros_like(dbs_ref)
            gacc_ref[...] = jnp.zeros_like(gacc_ref)
            bacc_ref[...] = jnp.zeros_like(bacc_ref)

        zg = zg_ref[...].astype(F32)
        lng_v = lng_ref[...]
        u, xhat, rstd, vn = _gmlp_parts(zg, lng_v, lnb_ref[...])
        dy = dy_ref[...].astype(F32)
        dvn_parts = []
        for g in range(GMLP_GROUPS):
            gs = slice(g * GMLP_GROUP_DIM, (g + 1) * GMLP_GROUP_DIM)
            wc = _causal(w_ref[g]).astype(BF16)
            vn_s = _stack_chunks(vn[gs, :].astype(BF16), nc)
            mixed = _unstack_chunks(_dg(vn_s, wc, NT) + bs_ref[g:g + 1, :], nc)
            dy_g = dy[gs, :]
            dmixed = dy_g * u[gs, :]
            dm_s = _stack_chunks(dmixed.astype(BF16), nc)
            dzg_ref[gs, :] = (dy_g * mixed * _gelu_grad(zg[gs, :])).astype(BF16)
            dw_ref[g] += _causal(_dg(dm_s, vn_s, TN))
            dbs_ref[g:g + 1, :] += _lane_fold(jnp.sum(dmixed, axis=0, keepdims=True))
            dvn_parts.append(_unstack_chunks(_dot(dm_s, wc), nc))
        dvn = jnp.concatenate(dvn_parts, axis=0)
        gacc_ref[...] += _lane_fold(dvn * xhat)
        bacc_ref[...] += _lane_fold(dvn)
        dxhat = dvn * lng_v
        m1 = jnp.mean(dxhat, axis=0, keepdims=True)
        m2 = jnp.mean(dxhat * xhat, axis=0, keepdims=True)
        dvv = rstd * (dxhat - m1 - xhat * m2)
        dzg_ref[GMLP_WIDTH:, :] = (dvv * _gelu_grad(zg[GMLP_WIDTH:, :])).astype(BF16)

        @pl.when(n == nt - 1)
        def _():
            dlng_ref[...] = jnp.sum(gacc_ref[...], axis=1, keepdims=True)
            dlnb_ref[...] = jnp.sum(bacc_ref[...], axis=1, keepdims=True)

    const2 = lambda n: (0, 0)
    return pl.pallas_call(
        body, name=name, grid=(nt,),
        in_specs=[pl.BlockSpec((2 * GMLP_WIDTH, tw), lambda n: (0, n)), pl.BlockSpec((GMLP_WIDTH, tw), lambda n: (0, n)),
                  _resident((GMLP_WIDTH, 1)), _resident((GMLP_WIDTH, 1)),
                  _resident((GMLP_GROUPS, BLOCK, BLOCK)), _resident((GMLP_GROUPS, BLOCK))],
        out_specs=[pl.BlockSpec((2 * GMLP_WIDTH, tw), lambda n: (0, n)),
                   pl.BlockSpec((GMLP_GROUPS, BLOCK, BLOCK), lambda n: (0, 0, 0)),
                   pl.BlockSpec((GMLP_GROUPS, BLOCK), const2),
                   pl.BlockSpec((GMLP_WIDTH, 1), const2), pl.BlockSpec((GMLP_WIDTH, 1), const2)],
        out_shape=[jax.ShapeDtypeStruct((2 * GMLP_WIDTH, t), BF16),
                   jax.ShapeDtypeStruct((GMLP_GROUPS, BLOCK, BLOCK), F32),
                   jax.ShapeDtypeStruct((GMLP_GROUPS, BLOCK), F32),
                   jax.ShapeDtypeStruct((GMLP_WIDTH, 1), F32), jax.ShapeDtypeStruct((GMLP_WIDTH, 1), F32)],
        scratch_shapes=[pltpu.VMEM((GMLP_WIDTH, BLOCK), F32), pltpu.VMEM((GMLP_WIDTH, BLOCK), F32)],
        compiler_params=_params(("arbitrary",)),
    )(zg_t, dy_t, lng, lnb, w_s, b_s)


def _rmsnorm_cols(y, gain_col):
    r = lax.rsqrt(jnp.mean(y * y, axis=0, keepdims=True) + EPS)
    n = y * r
    return n, r, n * gain_col


def mix_out_fwd(x, ya_t, yg_t, gao, ggo, wout, b_out, name):
    t, d = x.shape
    tm = _token_tile(t)

    def body(x_ref, ya_ref, yg_ref, gao_ref, ggo_ref, w_ref, b_ref, o_ref):
        _, _, ya = _rmsnorm_cols(ya_ref[...].astype(F32), gao_ref[...])
        _, _, yg = _rmsnorm_cols(yg_ref[...].astype(F32), ggo_ref[...])
        o_ref[...] = (x_ref[...] + _dg(ya.astype(BF16), w_ref[:ATTN_WIDTH, :], TN)
                      + _dg(yg.astype(BF16), w_ref[ATTN_WIDTH:, :], TN) + b_ref[...])

    return pl.pallas_call(
        body, name=name, grid=(t // tm,),
        in_specs=[pl.BlockSpec((tm, d), lambda i: (i, 0)), pl.BlockSpec((ATTN_WIDTH, tm), lambda i: (0, i)),
                  pl.BlockSpec((GMLP_WIDTH, tm), lambda i: (0, i)), _resident((ATTN_WIDTH, 1)), _resident((GMLP_WIDTH, 1)),
                  _resident((ATTN_WIDTH + GMLP_WIDTH, d)), _resident((1, d))],
        out_specs=pl.BlockSpec((tm, d), lambda i: (i, 0)),
        out_shape=jax.ShapeDtypeStruct((t, d), F32),
        compiler_params=_params(("arbitrary",)),
    )(x, ya_t, yg_t, gao, ggo, wout, b_out)


def _lane_fold(v):
    out = v[:, :LANES]
    for j in range(1, v.shape[1] // LANES):
        out = out + v[:, j * LANES:(j + 1) * LANES]
    return out


def mix_out_bwd(dx, ya_t, yg_t, gao, ggo, wout, name, deps=()):
    t, d = dx.shape
    tm = min(t, MIX_OUT_BWD_TOKENS)
    nt = t // tm

    def body(dx_ref, ya_ref, yg_ref, gao_ref, ggo_ref, w_ref, dya_ref, dyg_ref, y_ref, dgao_ref, dggo_ref, db_ref,
             acc_ref):
        i = pl.program_id(0)

        @pl.when(i == 0)
        def _():
            acc_ref[...] = jnp.zeros_like(acc_ref)
            db_ref[...] = jnp.zeros_like(db_ref)

        dxv = dx_ref[...]
        db_ref[...] += jnp.sum(dxv, axis=0, keepdims=True)
        dxb = dxv.astype(BF16)
        for part, (src_ref, gain_ref, dst_ref) in enumerate(((ya_ref, gao_ref, dya_ref), (yg_ref, ggo_ref, dyg_ref))):
            rows = slice(part * ATTN_WIDTH, (part + 1) * ATTN_WIDTH)
            gain = gain_ref[...]
            nrm, r, yn = _rmsnorm_cols(src_ref[...].astype(F32), gain)
            y_ref[rows, :] = yn.astype(BF16)
            dy = _dg(w_ref[rows, :], dxb, NT)
            acc_ref[rows, :] += _lane_fold(dy * nrm)
            dn = dy * gain
            dst_ref[...] = (r * (dn - nrm * jnp.mean(dn * nrm, axis=0, keepdims=True))).astype(BF16)

        @pl.when(i == nt - 1)
        def _():
            dgao_ref[...] = jnp.sum(acc_ref[:ATTN_WIDTH, :], axis=1, keepdims=True)
            dggo_ref[...] = jnp.sum(acc_ref[ATTN_WIDTH:, :], axis=1, keepdims=True)

    const2 = lambda i: (0, 0)
    feat = lambda w: pl.BlockSpec((w, tm), lambda i: (0, i))
    return pl.pallas_call(
        _drop_deps(body, 6, len(deps)), name=name, grid=(nt,),
        in_specs=[pl.BlockSpec((tm, d), lambda i: (i, 0)), feat(ATTN_WIDTH), feat(GMLP_WIDTH),
                  _resident((ATTN_WIDTH, 1)), _resident((GMLP_WIDTH, 1)), _resident((ATTN_WIDTH + GMLP_WIDTH, d))]
        + [ANY_SPEC] * len(deps),
        out_specs=[feat(ATTN_WIDTH), feat(GMLP_WIDTH), feat(ATTN_WIDTH + GMLP_WIDTH),
                   pl.BlockSpec((ATTN_WIDTH, 1), const2), pl.BlockSpec((GMLP_WIDTH, 1), const2),
                   pl.BlockSpec((1, d), const2)],
        out_shape=[jax.ShapeDtypeStruct((ATTN_WIDTH, t), BF16), jax.ShapeDtypeStruct((GMLP_WIDTH, t), BF16),
                   jax.ShapeDtypeStruct((ATTN_WIDTH + GMLP_WIDTH, t), BF16),
                   jax.ShapeDtypeStruct((ATTN_WIDTH, 1), F32), jax.ShapeDtypeStruct((GMLP_WIDTH, 1), F32),
                   jax.ShapeDtypeStruct((1, d), F32)],
        scratch_shapes=[pltpu.VMEM((ATTN_WIDTH + GMLP_WIDTH, LANES), F32)],
        compiler_params=_params(("arbitrary",)),
    )(dx, ya_t, yg_t, gao, ggo, wout, *deps)


def mix_in_bwd(x, dxo, gain, dq_t, dk_t, dv_t, dzg_t, win_t, name):
    t, d = x.shape
    w = win_t.shape[0]
    tm = _token_tile(t)
    nt = t // tm
    parts = ((0, ATTN_WIDTH), (ATTN_WIDTH, KV_WIDTH), (ATTN_WIDTH + KV_WIDTH, KV_WIDTH), (QKV_WIDTH, w - QKV_WIDTH))

    def body(x_ref, dxo_ref, g_ref, dq_ref, dk_ref, dv_ref, dzg_ref, w_ref, dx_ref, h_ref, dg_ref, db_ref, acc_ref):
        i = pl.program_id(0)

        @pl.when(i == 0)
        def _():
            acc_ref[...] = jnp.zeros_like(acc_ref)
            dg_ref[...] = jnp.zeros_like(dg_ref)

        xv = x_ref[...]
        gain_v = g_ref[...]
        r = lax.rsqrt(jnp.mean(xv * xv, axis=-1, keepdims=True) + EPS)
        h_ref[...] = (xv * r * gain_v).astype(BF16)
        dh = jnp.zeros((tm, d), F32)
        for (off, size), src_ref in zip(parts, (dq_ref, dk_ref, dv_ref, dzg_ref)):
            dp = src_ref[...]
            acc_ref[off:off + size, :] += _lane_fold(dp.astype(F32))
            dh = dh + _dg(dp, w_ref[off:off + size, :], TN)
        dx, dgain = _rmsnorm_bwd_rows(xv, r, gain_v, dh)
        dx_ref[...] = dxo_ref[...] + dx
        dg_ref[...] += dgain

        @pl.when(i == nt - 1)
        def _():
            db_ref[...] = jnp.sum(acc_ref[...], axis=1, keepdims=True)

    const2 = lambda i: (0, 0)
    tok = lambda: pl.BlockSpec((tm, d), lambda i: (i, 0))
    feat = lambda rows: pl.BlockSpec((rows, tm), lambda i: (0, i))
    return pl.pallas_call(
        body, name=name, grid=(nt,),
        in_specs=[tok(), tok(), _resident((1, d)), feat(ATTN_WIDTH), feat(KV_WIDTH), feat(KV_WIDTH),
                  feat(w - QKV_WIDTH), _resident((w, d))],
        out_specs=[tok(), tok(), pl.BlockSpec((1, d), const2), pl.BlockSpec((w, 1), const2)],
        out_shape=[jax.ShapeDtypeStruct((t, d), F32), jax.ShapeDtypeStruct((t, d), BF16),
                   jax.ShapeDtypeStruct((1, d), F32), jax.ShapeDtypeStruct((w, 1), F32)],
        scratch_shapes=[pltpu.VMEM((w, LANES), F32)],
        compiler_params=_params(("arbitrary",)),
    )(x, dxo, gain, dq_t, dk_t, dv_t, dzg_t, win_t)


def _adamw_math(w, g, m, v):
    m = ADAM_B1 * m + (1.0 - ADAM_B1) * g
    v = ADAM_B2 * v + (1.0 - ADAM_B2) * (g * g)
    m_hat = m / (1.0 - ADAM_B1 ** ADAM_STEP)
    v_hat = v / (1.0 - ADAM_B2 ** ADAM_STEP)
    delta = -ADAM_LR * (m_hat / (jnp.sqrt(v_hat) + ADAM_EPS) + ADAM_WD * w)
    return delta, m, v


def adamw_many(ws, gs, ms, vs, name):
    n = len(ws)

    def body(*refs):
        w_refs, g_refs, m_refs, v_refs = refs[:n], refs[n:2 * n], refs[2 * n:3 * n], refs[3 * n:4 * n]
        d_refs, mo_refs, vo_refs = refs[4 * n:5 * n], refs[5 * n:6 * n], refs[6 * n:]
        for k in range(n):
            delta, mn, vn = _adamw_math(w_refs[k][...], g_refs[k][...], m_refs[k][...], v_refs[k][...])
            d_refs[k][...] = delta
            mo_refs[k][...] = mn
            vo_refs[k][...] = vn

    def whole(a):
        zeros = (0,) * a.ndim
        return pl.BlockSpec(a.shape, lambda i: zeros)

    outs = pl.pallas_call(
        body, name=name, grid=(1,),
        in_specs=[whole(a) for a in ws] * 4, out_specs=[whole(a) for a in ws] * 3,
        out_shape=[jax.ShapeDtypeStruct(a.shape, F32) for a in ws] * 3,
        compiler_params=_params(("arbitrary",)),
    )(*ws, *gs, *ms, *vs)
    return outs[:n], outs[n:2 * n], outs[2 * n:]


def sum_adamw(landing, grad3d, me, w, m, v, name):
    nd, rows, cols = landing.shape
    tr = rows // 2

    def body(me_ref, l_ref, own_ref, w_ref, m_ref, v_ref, g_ref, d_ref, mo_ref, vo_ref):
        g = own_ref[...].astype(F32)
        for j in range(nd):
            g = g + l_ref[j].astype(F32)
        delta, mn, vn = _adamw_math(w_ref[...], g, m_ref[...], v_ref[...])
        g_ref[...] = g
        d_ref[...] = delta
        mo_ref[...] = mn
        vo_ref[...] = vn

    spec = lambda: pl.BlockSpec((tr, cols), lambda i, me_ref: (i, 0))
    shape = jax.ShapeDtypeStruct((rows, cols), F32)
    return pl.pallas_call(
        body, name=name,
        grid_spec=pltpu.PrefetchScalarGridSpec(
            num_scalar_prefetch=1, grid=(rows // tr,),
            in_specs=[pl.BlockSpec((nd, tr, cols), lambda i, me_ref: (0, i, 0)),
                      pl.BlockSpec((None, tr, cols), lambda i, me_ref: (me_ref[0], i, 0)), spec(), spec(), spec()],
            out_specs=[spec() for _ in range(4)]),
        out_shape=[shape] * 4,
        compiler_params=_params(("arbitrary",)),
    )(me, landing, grad3d, w, m, v)


def _mesh_place():
    x, y, c = lax.axis_index("x"), lax.axis_index("y"), lax.axis_index("c")
    return x, y, c, 4 * x + 2 * y + c


def _peer(x, y, c, k):
    return (x ^ ((k >> 2) & 1), y ^ ((k >> 1) & 1), c ^ (k & 1))


def prep_shards(mats, me, name):
    na = len(mats)

    def body(me_ref, *refs):
        in_refs, shard_refs, land_refs = refs[:na], refs[na:2 * na], refs[2 * na:]
        for a in range(na):
            val = in_refs[a][...].astype(BF16)
            shard_refs[a][...] = val
            land_refs[a][...] = val

    whole = lambda mat: pl.BlockSpec(mat.shape, lambda i, me_ref: (0, 0))
    outs = pl.pallas_call(
        body, name=name,
        grid_spec=pltpu.PrefetchScalarGridSpec(
            num_scalar_prefetch=1, grid=(1,),
            in_specs=[whole(mat) for mat in mats],
            out_specs=[whole(mat) for mat in mats]
            + [pl.BlockSpec((None,) + mat.shape, lambda i, me_ref: (me_ref[0], 0, 0)) for mat in mats]),
        out_shape=[jax.ShapeDtypeStruct(mat.shape, BF16) for mat in mats]
        + [jax.ShapeDtypeStruct((N_DEV,) + mat.shape, BF16) for mat in mats],
        compiler_params=_params(("arbitrary",)),
    )(me, *mats)
    return outs[:na], outs[na:]


def allgather_rows(shards, lands, name):
    na = n_gather = len(shards)

    def body(*refs):
        in_refs, out_refs = refs[:na], refs[2 * na:3 * na]
        send_sems, recv_sems = refs[3 * na:]
        x, y, c, me = _mesh_place()

        def copy(sem, a, origin, src, k):
            return pltpu.make_async_remote_copy(
                src_ref=src, dst_ref=out_refs[a].at[origin], send_sem=send_sems.at[sem, a],
                recv_sem=recv_sems.at[sem, a], device_id=_peer(x, y, c, k), device_id_type=pl.DeviceIdType.MESH)

        kx, ky = 4, 2

        def recv(sem, a, origin):
            copy(sem, a, origin, in_refs[a], 1).wait_recv()

        def landed(a, origin):
            return out_refs[a].at[origin]

        @pl.when(c == 1)
        def _():
            sends = []

            def send(sem, a, origin, src, k):
                cp = copy(sem, a, origin, src, k)
                cp.start()
                sends.append(cp)

            for a in range(n_gather):
                send(1, a, me, in_refs[a], kx)
                send(2, a, me, in_refs[a], ky)
                send(0, a, me, in_refs[a], 1)
            for a in range(n_gather):
                recv(0, a, me ^ 1)
                send(3, a, me ^ 1, landed(a, me ^ 1), kx)
                send(4, a, me ^ 1, landed(a, me ^ 1), ky)
            for a in range(n_gather):
                recv(1, a, me ^ kx)
                send(5, a, me ^ kx, landed(a, me ^ kx), ky)
                send(7, a, me ^ kx, landed(a, me ^ kx), 1)
                recv(2, a, me ^ ky)
                send(8, a, me ^ ky, landed(a, me ^ ky), 1)
            for a in range(n_gather):
                recv(3, a, me ^ kx ^ 1)
                send(9, a, me ^ kx ^ 1, landed(a, me ^ kx ^ 1), 1)
                recv(4, a, me ^ ky ^ 1)
                send(6, a, me ^ ky ^ 1, landed(a, me ^ ky ^ 1), kx)
                send(10, a, me ^ ky ^ 1, landed(a, me ^ ky ^ 1), 1)
            for a in range(n_gather):
                recv(5, a, me ^ 6)
                send(11, a, me ^ 6, landed(a, me ^ 6), 1)
                recv(6, a, me ^ 7)
                send(12, a, me ^ 7, landed(a, me ^ 7), 1)
            for cp in sends:
                cp.wait_send()

        @pl.when(c == 0)
        def _():
            north = me ^ 1
            own = [copy(0, a, me, in_refs[a], 1) for a in range(n_gather)]
            for cp in own:
                cp.start()
            for a in range(n_gather):
                recv(0, a, north)
                for sem, origin in ((7, north ^ kx), (8, north ^ ky), (9, north ^ kx ^ 1), (10, north ^ ky ^ 1),
                                    (11, north ^ 6), (12, north ^ 7)):
                    recv(sem, a, origin)
            for cp in own:
                cp.wait_send()

    return pl.pallas_call(
        body, name=name,
        in_specs=[ANY_SPEC] * (2 * na), out_specs=[ANY_SPEC] * na,
        out_shape=[jax.ShapeDtypeStruct(land.shape, land.dtype) for land in lands],
        input_output_aliases={na + a: a for a in range(na)},
        scratch_shapes=[pltpu.SemaphoreType.DMA((N_GATHER_ROLES, n_gather)),
                        pltpu.SemaphoreType.DMA((N_GATHER_ROLES, n_gather))],
    )(*shards, *lands)


HBM_SPEC = pl.BlockSpec(memory_space=pltpu.HBM)
SEM_SPEC = pl.BlockSpec(memory_space=pltpu.SEMAPHORE)
SIDE_EFFECT = pltpu.SideEffectType.DATAFLOW_SIDE_EFFECTING


def _hbm(v):
    return pltpu.with_memory_space_constraint(v, pltpu.HBM)


def gather_slices(src, land, me, k):
    return src, land.at[me], land.at[me ^ k]


def exchange_slices(src, land, me, k):
    return src.at[me ^ k], land.at[k - 1], land.at[k - 1]


def split_start(groups, slices, deps, name):
    sizes = [len(srcs) for srcs, _ in groups]
    flat_src = [s for srcs, _ in groups for s in srcs]
    flat_land = [l for _, lands in groups for l in lands]
    na, ng, nd = len(flat_src), len(groups), len(deps)

    def body(*refs):
        src_refs, land_refs = refs[:na], refs[na:2 * na]
        sem_refs = refs[2 * na + nd:2 * na + nd + 2 * ng]
        token_ref = refs[-1]
        x, y, c, me = _mesh_place()
        a0 = 0
        for gi, size in enumerate(sizes):
            send_sems, recv_sems = sem_refs[2 * gi], sem_refs[2 * gi + 1]
            for k in range(1, N_DEV):
                for j in range(size):
                    src, dst, _ = slices(src_refs[a0 + j], land_refs[a0 + j], me, k)
                    pltpu.make_async_remote_copy(
                        src_ref=src, dst_ref=dst, send_sem=send_sems.at[(k - 1) * size + j],
                        recv_sem=recv_sems.at[(k - 1) * size + j],
                        device_id=_peer(x, y, c, k), device_id_type=pl.DeviceIdType.MESH).start()
            a0 += size
        token_ref[...] = jnp.zeros_like(token_ref)

    sems = [pltpu.SemaphoreType.DMA(((N_DEV - 1) * size,)) for size in sizes for _ in range(2)]
    thru = [pltpu.HBM(v.shape, v.dtype) for v in flat_src + flat_land]
    outs = pl.pallas_call(
        body, name=name,
        in_specs=[HBM_SPEC] * (2 * na) + [ANY_SPEC] * nd,
        out_specs=[SEM_SPEC] * (2 * ng) + [HBM_SPEC] * (2 * na) + [pl.BlockSpec(memory_space=pltpu.VMEM)],
        out_shape=sems + thru + [jax.ShapeDtypeStruct((8, LANES), F32)],
        input_output_aliases={i: 2 * ng + i for i in range(2 * na)},
        compiler_params=pltpu.CompilerParams(has_side_effects=SIDE_EFFECT),
    )(*[_hbm(v) for v in flat_src + flat_land], *deps)
    started, a0 = [], 0
    for gi, size in enumerate(sizes):
        srcs = outs[2 * ng + a0:2 * ng + a0 + size]
        lands = outs[2 * ng + na + a0:2 * ng + na + a0 + size]
        started.append((outs[2 * gi], outs[2 * gi + 1], list(srcs), list(lands)))
        a0 += size
    return started, outs[-1]


def split_wait(started, slices, after, name):
    send_sems, recv_sems, srcs, lands = started
    na = len(srcs)

    def body(*refs):
        src_refs, land_refs = refs[:na], refs[na:2 * na]
        send_ref, recv_ref = refs[2 * na], refs[2 * na + 1]
        x, y, c, me = _mesh_place()
        for k in range(1, N_DEV):
            for j in range(na):
                src, _, got = slices(src_refs[j], land_refs[j], me, k)
                cp = pltpu.make_async_remote_copy(
                    src_ref=src, dst_ref=got, send_sem=send_ref.at[(k - 1) * na + j], recv_sem=recv_ref.at[(k - 1) * na + j],
                    device_id=_peer(x, y, c, k), device_id_type=pl.DeviceIdType.MESH)
                cp.wait_send()
                cp.wait_recv()

    outs = pl.pallas_call(
        body, name=name,
        in_specs=[HBM_SPEC] * (2 * na) + [SEM_SPEC, SEM_SPEC] + [ANY_SPEC] * len(after),
        out_specs=[HBM_SPEC] * (2 * na),
        out_shape=[pltpu.HBM(v.shape, v.dtype) for v in srcs + lands],
        input_output_aliases={i: i for i in range(2 * na)},
        compiler_params=pltpu.CompilerParams(has_side_effects=SIDE_EFFECT),
    )(*srcs, *lands, send_sems, recv_sems, *after)
    return list(outs[:na]), list(outs[na:])


def place_own(pack, me, name):
    rows, cols = pack.shape

    def body(me_ref, p_ref, o_ref):
        o_ref[...] = p_ref[...]

    return pl.pallas_call(
        body, name=name,
        grid_spec=pltpu.PrefetchScalarGridSpec(
            num_scalar_prefetch=1, grid=(1,),
            in_specs=[pl.BlockSpec((rows, cols), lambda i, me_ref: (0, 0))],
            out_specs=pl.BlockSpec((None, rows, cols), lambda i, me_ref: (me_ref[0], 0, 0))),
        out_shape=jax.ShapeDtypeStruct((N_DEV, rows, cols), F32),
        compiler_params=_params(("arbitrary",)),
    )(me, pack)


def sum_slots(slots, name):
    nd, rows, cols = slots.shape

    def body(s_ref, o_ref):
        acc = s_ref[0]
        for j in range(1, nd):
            acc = acc + s_ref[j]
        o_ref[...] = acc

    return pl.pallas_call(
        body, name=name, grid=(1,),
        in_specs=[pl.BlockSpec((nd, rows, cols), lambda i: (0, 0, 0))],
        out_specs=pl.BlockSpec((rows, cols), lambda i: (0, 0)),
        out_shape=jax.ShapeDtypeStruct((rows, cols), F32),
        compiler_params=_params(("arbitrary",)),
    )(slots)


def _pack_rows(size):
    rows = -(-size // LANES)
    return -(-rows // 8) * 8


def pack_small(parts):
    out = []
    for p in parts:
        flat = p.reshape(-1).astype(F32)
        rows = _pack_rows(flat.shape[0])
        out.append(jnp.pad(flat, (0, rows * LANES - flat.shape[0])).reshape(rows, LANES))
    return jnp.concatenate(out, axis=0)


def unpack_small(pack, shapes):
    out, off = [], 0
    for shp in shapes:
        size = 1
        for s in shp:
            size *= s
        rows = _pack_rows(size)
        out.append(pack[off:off + rows].reshape(-1)[:size].reshape(shp))
        off += rows
    return out


def local_step(x, target, small, get_w, put_g, put_small):
    col = lambda v: v.reshape(-1, 1)
    (wg1, wu1), deps = get_w(("wg1", "wu1"), ())
    sil1, gp1, s1, h1 = ffn_fwd(x, small["ffn1_norm_g"], wg1, wu1, None, "ffn1_fwd", deps)
    (wd1, win, wout), _ = get_w(("wd1", "win", "wout"), (s1,))
    qkv_t, zg_t, x1 = mix_in_fwd(x, small["mix_norm_g"], win, col(small["b_in"]), "mix_in_fwd", ffn_tail=(s1, wd1))
    ya_t = attn_fwd(qkv_t, small["attn_sinks"], "attn_fwd")
    lng, lnb = col(small["gmlp_ln_g"]), col(small["gmlp_ln_b"])
    w_s, b_s = small["gmlp_w_s"][0], small["gmlp_b_s"][0]
    yg_t = gmlp_fwd(zg_t, lng, lnb, w_s, b_s, "gmlp_fwd")
    gao, ggo = col(small["attn_out_norm_g"]), col(small["gmlp_out_norm_g"])
    x2 = mix_out_fwd(x1, ya_t, yg_t, gao, ggo, wout, small["b_out"], "mix_out_fwd")
    (wg2, wu2, wd2), _ = get_w(("wg2", "wu2", "wd2"), (x2,))
    gs = {}
    dx3, sil2, gp2, s2, h2f, gs["final_norm_g"], loss = ffn_fwd(
        x2, small["ffn2_norm_g"], wg2, wu2, wd2, "ffn2_fwd", head=(target, small["final_norm_g"].reshape(1, -1)))

    dx2, da, db, df, gs["ffn2_norm_g"] = ffn_bwd(x2, dx3, small["ffn2_norm_g"], sil2, gp2, wg2, wu2, wd2, "ffn2_bwd")
    deps = put_g({"wg2": grad_tn(da, h2f, 1.0, "ffn2_dwg"), "wu2": grad_tn(db, h2f, 1.0, "ffn2_dwu"),
                  "wd2": grad_tn(s2, df, 1.0, "ffn2_dwd")})

    dya_t, dyg_t, y_t, dgao, dggo, gs["b_out"] = mix_out_bwd(dx2, ya_t, yg_t, gao, ggo, wout, "mix_out_bwd", deps)
    gs["attn_out_norm_g"], gs["gmlp_out_norm_g"] = dgao.reshape(1, -1), dggo.reshape(1, -1)
    g_wout = grad_nn([y_t], dx2, "dwout")
    dzg_t, dws, dbs, dlng, dlnb = gmlp_bwd(zg_t, dyg_t, lng, lnb, w_s, b_s, "gmlp_bwd")
    gs["gmlp_w_s"], gs["gmlp_b_s"] = dws[None], dbs[None]
    gs["gmlp_ln_g"], gs["gmlp_ln_b"] = dlng.reshape(1, -1), dlnb.reshape(1, -1)
    dq_t, dk_t, dv_t, dsinks = attn_bwd(qkv_t, dya_t, small["attn_sinks"], "attn_bwd")
    gs["attn_sinks"] = dsinks.reshape(1, -1)
    dx1, h2, gs["mix_norm_g"], dbin = mix_in_bwd(x1, dx2, small["mix_norm_g"], dq_t, dk_t, dv_t, dzg_t, win,
                                                 "mix_in_bwd")
    gs["b_in"] = dbin.reshape(1, -1)
    deps = put_g({"wout": g_wout, "win": grad_nn([dq_t, dk_t, dv_t, dzg_t], h2, "dwin")})

    dx0, da, db, df, gs["ffn1_norm_g"] = ffn_bwd(x, dx1, small["ffn1_norm_g"], sil1, gp1, wg1, wu1, wd1, "ffn1_bwd",
                                             deps)
    deps = put_small(gs, loss)
    deps = put_g({"wg1": grad_tn(da, h1, 1.0, "ffn1_dwg", deps)})
    deps = put_g({"wu1": grad_tn(db, h1, 1.0, "ffn1_dwu", deps)})
    put_g({"wd1": grad_tn(s1, df, 1.0, "ffn1_dwd", deps)})
    return dx0


SMALL_NAMES = ["ffn1_norm_g", "mix_norm_g", "b_in", "attn_sinks", "gmlp_ln_g", "gmlp_ln_b", "gmlp_w_s", "gmlp_b_s",
               "attn_out_norm_g", "gmlp_out_norm_g", "b_out", "ffn2_norm_g", "final_norm_g"]
BIG_NAMES = {"wg1": ("ffn1_w_gate", True), "wu1": ("ffn1_w_up", True), "wd1": ("ffn1_w_down", False),
             "win": ("w_in", True), "wout": ("w_out", False),
             "wg2": ("ffn2_w_gate", True), "wu2": ("ffn2_w_up", True), "wd2": ("ffn2_w_down", False)}
WEIGHT_ORDER = ["ffn1_norm_g", "ffn1_w_gate", "ffn1_w_up", "ffn1_w_down", "mix_norm_g", "w_in", "b_in", "attn_sinks",
                "gmlp_ln_g", "gmlp_ln_b", "gmlp_w_s", "gmlp_b_s", "attn_out_norm_g", "gmlp_out_norm_g", "w_out",
                "b_out", "ffn2_norm_g", "ffn2_w_gate", "ffn2_w_up", "ffn2_w_down", "final_norm_g"]


def kernel(x, ffn1_norm_g, ffn1_w_gate, ffn1_w_up, ffn1_w_down, mix_norm_g, w_in, b_in, attn_sinks, gmlp_ln_g, gmlp_ln_b, gmlp_w_s, gmlp_b_s, attn_out_norm_g, gmlp_out_norm_g, w_out, b_out, ffn2_norm_g, ffn2_w_gate, ffn2_w_up, ffn2_w_down, final_norm_g, loss_target, m_ffn1_norm_g, m_ffn1_w_gate, m_ffn1_w_up, m_ffn1_w_down, m_mix_norm_g, m_w_in, m_b_in, m_attn_sinks, m_gmlp_ln_g, m_gmlp_ln_b, m_gmlp_w_s, m_gmlp_b_s, m_attn_out_norm_g, m_gmlp_out_norm_g, m_w_out, m_b_out, m_ffn2_norm_g, m_ffn2_w_gate, m_ffn2_w_up, m_ffn2_w_down, m_final_norm_g, v_ffn1_norm_g, v_ffn1_w_gate, v_ffn1_w_up, v_ffn1_w_down, v_mix_norm_g, v_w_in, v_b_in, v_attn_sinks, v_gmlp_ln_g, v_gmlp_ln_b, v_gmlp_w_s, v_gmlp_b_s, v_attn_out_norm_g, v_gmlp_out_norm_g, v_w_out, v_b_out, v_ffn2_norm_g, v_ffn2_w_gate, v_ffn2_w_up, v_ffn2_w_down, v_final_norm_g):
    args = dict(locals())
    w = {n: args[n] for n in WEIGHT_ORDER}
    m = {n: args["m_" + n] for n in WEIGHT_ORDER}
    v = {n: args["v_" + n] for n in WEIGHT_ORDER}

    d_model = x.shape[-1]
    flat = lambda g: g.reshape(-1, d_model)
    me = _mesh_place()[3].astype(jnp.int32).reshape(1)
    first = ("wg1", "wu1")
    later = (("wd1", "win", "wout"), ("wg2", "wu2", "wd2"))
    gathers, exchanges, last_token = {}, [], []

    order = list(first) + [k for grp in later for k in grp]
    mats = [w[BIG_NAMES[k][0]][0].T if BIG_NAMES[k][1] else w[BIG_NAMES[k][0]][0] for k in order]
    shard_list, land_list = prep_shards(mats, me, "prep_shards")
    shards, zones = dict(zip(order, shard_list)), dict(zip(order, land_list))

    def get_w(keys, after):
        if keys == first:
            full = allgather_rows([shards[k] for k in first], [zones[k] for k in first], "allgather_ffn1")
            started, token = split_start([([shards[k] for k in grp], [zones[k] for k in grp]) for grp in later],
                                         gather_slices, (full[0],), "gather_start")
            gathers.update(zip(later, started))
            return [flat(g) for g in full], (token,)
        _, lands = split_wait(gathers[keys], gather_slices, after, "gather_wait_" + keys[0])
        return [flat(land) for land in lands], ()

    def put_g(gb):
        keys = tuple(gb)
        g3 = [gb[k].reshape(N_DEV, -1, d_model) for k in keys]
        lands = [lax.empty((N_DEV - 1,) + g.shape[1:], BF16) for g in g3]
        started, token = split_start([(g3, lands)], exchange_slices, (), "exchange_start_" + keys[0])
        exchanges.append((keys, started[0]))
        last_token[:] = [token]
        return (token,)

    small_started = []

    def put_small(gs, loss):
        pack = pack_small([gs[n] for n in SMALL_NAMES] + [loss[:, :1]])
        started, token = split_start([([pack], [place_own(pack, me, "place_small")])], gather_slices, (), "small_start")
        small_started[:] = started
        return (token,)

    small = {n: w[n] for n in SMALL_NAMES}
    grad_x = local_step(x[0], loss_target[0], small, get_w, put_g, put_small)

    grads, deltas, new_m, new_v = {}, {}, {}, {}
    shapes = [w[n].shape for n in SMALL_NAMES]
    _, (slots,) = split_wait(small_started[0], gather_slices, tuple(last_token), "small_wait")
    total = sum_slots(slots, "sum_small")
    small_g = unpack_small(total, shapes + [(1, 1)])
    loss_total = small_g[-1].reshape(())
    d_, m_, v_ = adamw_many([w[n] for n in SMALL_NAMES], small_g[:-1], [m[n] for n in SMALL_NAMES],
                            [v[n] for n in SMALL_NAMES], "adamw_small")
    for n, g_, dd, mm, vv in zip(SMALL_NAMES, small_g[:-1], d_, m_, v_):
        grads[n], deltas[n], new_m[n], new_v[n] = g_, dd, mm, vv

    after = tuple(arr for n in SMALL_NAMES for arr in (grads[n], deltas[n], new_m[n], new_v[n]))
    for keys, started in exchanges:
        g3, lands = split_wait(started, exchange_slices, after, "exchange_wait_" + keys[0])
        for key, own, land in zip(keys, g3, lands):
            pname, transposed = BIG_NAMES[key]
            to_rows = (lambda p: p[0].T) if transposed else (lambda p: p[0])
            from_rows = (lambda r: r.T[None]) if transposed else (lambda r: r[None])
            g, d_, m_, v_ = sum_adamw(land, own, me, to_rows(w[pname]), to_rows(m[pname]), to_rows(v[pname]),
                                      "adamw_" + key)
            after = after + (g,)
            grads[pname], deltas[pname], new_m[pname], new_v[pname] = (from_rows(g), from_rows(d_), from_rows(m_),
                                                                        from_rows(v_))

    return (loss_total, grad_x[None], *[grads[n] for n in WEIGHT_ORDER], *[deltas[n] for n in WEIGHT_ORDER],
            *[new_m[n] for n in WEIGHT_ORDER], *[new_v[n] for n in WEIGHT_ORDER])
```

```python
import functools

import jax
import jax.numpy as jnp
from jax import lax
from jax.experimental import pallas as pl
from jax.experimental.pallas import tpu as pltpu

F32 = jnp.float32
BF16 = jnp.bfloat16

N_DEV = 8
N_Q_HEADS = 8
N_KV_HEADS = 2
HEAD_DIM = 64
ATTN_WIDTH = N_Q_HEADS * HEAD_DIM
KV_WIDTH = N_KV_HEADS * HEAD_DIM
QKV_WIDTH = ATTN_WIDTH + 2 * KV_WIDTH
BLOCK = 128
GMLP_GROUPS = 8
GMLP_GROUP_DIM = 64
GMLP_WIDTH = GMLP_GROUPS * GMLP_GROUP_DIM
EPS = 1e-6
FFN_RES = 0.5
ATTN_SCALE = HEAD_DIM ** -0.5
MASK_VALUE = -1e30

ADAM_LR = 0.001
ADAM_B1 = 0.9
ADAM_B2 = 0.999
ADAM_EPS = 1e-08
ADAM_WD = 0.01
ADAM_STEP = 10

V7X_VMEM_BYTES = 64 * 1024 * 1024
VMEM_LIMIT = 62 * 1024 * 1024
LANES = 128
MXU_WIDTH = 256
FFN_FWD_TOKENS = 512
FFN_BWD_TOKENS = 256

NT = (((1,), (1,)), ((), ()))
TN = (((0,), (0,)), ((), ()))


def _dot(a, b):
    return jnp.dot(a, b, preferred_element_type=F32)


def _dg(a, b, dims):
    return lax.dot_general(a, b, dims, preferred_element_type=F32)


def _params(semantics=None):
    return pltpu.CompilerParams(dimension_semantics=semantics, vmem_limit_bytes=VMEM_LIMIT)


def _resident(shape):
    zeros = (0,) * len(shape)
    return pl.BlockSpec(shape, lambda *_: zeros, pipeline_mode=pl.Buffered(1))


def _drop_deps(body, n_in, n_deps):
    return lambda *refs: body(*refs[:n_in], *refs[n_in + n_deps:])


ANY_SPEC = pl.BlockSpec(memory_space=pl.ANY)


TOKEN_TILE = 1024
GRAD_TOKENS_PER_STEP = 2048
GRAD_ROW_TILE_CAP = 1408
N_GATHER_ROLES = 13


def _token_tile(t):
    return min(t, TOKEN_TILE)


def _f_chunk(f):
    for c in (256, 128):
        if f % c == 0:
            return c
    return f


def _loss_head(xv, target, gain):
    r = lax.rsqrt(jnp.mean(xv * xv, axis=-1, keepdims=True) + EPS)
    err = xv * r * gain - target
    loss = 0.5 * jnp.sum(jnp.mean(err * err, axis=-1, keepdims=True), axis=0, keepdims=True)
    dx, dgain = _rmsnorm_bwd_rows(xv, r, gain, err * (1.0 / xv.shape[-1]))
    return dx, dgain, loss


def ffn_fwd(x, gain, wg_t, wu_t, wd, name, deps=(), head=None):
    t, d = x.shape
    f = wg_t.shape[0]
    tm = min(t, FFN_FWD_TOKENS)
    fc = _f_chunk(f)
    weights = [wg_t, wu_t] + ([] if wd is None else [wd])
    n_in = 2 + len(weights) + (0 if head is None else 2)
    n_x = 0 if wd is None else 1

    def body(x_ref, g_ref, wg_ref, wu_ref, *rest):
        sil_ref, gp_ref, s_ref, h_ref = rest[n_in - 4 + n_x:n_in + n_x]
        xv = x_ref[...]
        r = lax.rsqrt(jnp.mean(xv * xv, axis=-1, keepdims=True) + EPS)
        h = (xv * r * g_ref[...]).astype(BF16)
        h_ref[...] = h
        for c in range(f // fc):
            sl = slice(c * fc, (c + 1) * fc)
            a = _dg(h, wg_ref[sl, :], NT)
            b = _dg(h, wu_ref[sl, :], NT)
            sig = jax.nn.sigmoid(a)
            sil = a * sig
            sil_ref[:, sl] = sil.astype(BF16)
            gp_ref[:, sl] = (b * (sig + sil - sil * sig)).astype(BF16)
            s_ref[:, sl] = (sil * b).astype(BF16)
        if wd is None:
            return
        wd_ref, xo_ref = rest[0], rest[n_in - 4]
        xo = xv + FFN_RES * _dot(s_ref[...], wd_ref[...])
        if head is None:
            xo_ref[...] = xo
        else:
            t_ref, hg_ref = rest[1:3]
            dg_ref, loss_ref = rest[-2:]

            @pl.when(pl.program_id(0) == 0)
            def _():
                dg_ref[...] = jnp.zeros_like(dg_ref)
                loss_ref[...] = jnp.zeros_like(loss_ref)

            dx, dgain, loss = _loss_head(xo, t_ref[...], hg_ref[...])
            xo_ref[...] = dx
            dg_ref[...] += dgain
            loss_ref[...] += loss

    tok = lambda: pl.BlockSpec((tm, d), lambda i: (i, 0))
    wide = lambda: pl.BlockSpec((tm, f), lambda i: (i, 0))
    const2 = lambda i: (0, 0)
    in_specs = [tok(), _resident((1, d))] + [_resident((f, d)) for _ in weights]
    out_specs = [tok()] * n_x + [wide(), wide(), wide(), tok()]
    out_shape = ([jax.ShapeDtypeStruct((t, d), F32)] * n_x + [jax.ShapeDtypeStruct((t, f), BF16)] * 3
                 + [jax.ShapeDtypeStruct((t, d), BF16)])
    operands = [x, gain] + weights
    if head is not None:
        in_specs += [tok(), _resident((1, d))]
        out_specs += [pl.BlockSpec((1, d), const2), pl.BlockSpec((1, LANES), const2)]
        out_shape += [jax.ShapeDtypeStruct((1, d), F32), jax.ShapeDtypeStruct((1, LANES), F32)]
        operands += list(head)
    return pl.pallas_call(
        _drop_deps(body, n_in, len(deps)), name=name, grid=(t // tm,),
        in_specs=in_specs + [ANY_SPEC] * len(deps), out_specs=out_specs, out_shape=out_shape,
        compiler_params=_params(("arbitrary",)),
    )(*operands, *deps)


def _rmsnorm_bwd_rows(xv, r, gain, dh):
    n = xv * r
    dgain = jnp.sum(dh * n, axis=0, keepdims=True)
    dn = dh * gain
    dx = r * (dn - n * jnp.mean(dn * n, axis=-1, keepdims=True))
    return dx, dgain


def ffn_bwd(x, dxo, gain, sil, gp, wg_t, wu_t, wd, name, deps=()):
    t, d = x.shape
    f = wd.shape[0]
    tm = min(t, FFN_BWD_TOKENS)
    nt = t // tm
    chunks = [(off, min(MXU_WIDTH, f - off)) for off in range(0, f, MXU_WIDTH)]

    def body(xp_ref, dxp_ref, dxo_ref, g_ref, sil_ref, gp_ref, wg_ref, wu_ref, wd_ref,
             dx_ref, da_ref, db_ref, df_ref, dg_ref, dh_ref):
        i = pl.program_id(0)
        gain_v = g_ref[...]

        @pl.when(i == 0)
        def _():
            dh_ref[...] = jnp.zeros_like(dh_ref)
            dg_ref[...] = jnp.zeros_like(dg_ref)

        def finish_previous(slot):
            xv = xp_ref[...]
            r = lax.rsqrt(jnp.mean(xv * xv, axis=-1, keepdims=True) + EPS)
            dx, dgain = _rmsnorm_bwd_rows(xv, r, gain_v, dh_ref[slot])
            dx_ref[...] = dxp_ref[...] + dx
            dg_ref[...] += dgain

        @pl.when(i < nt)
        def _():
            df = (FFN_RES * dxo_ref[...]).astype(BF16)
            df_ref[...] = df
            for off, size in chunks:
                sl = slice(off, off + size)
                ds = _dg(df, wd_ref[sl, :], NT)
                da_ref[:, sl] = (ds * gp_ref[:, sl].astype(F32)).astype(BF16)
                db_ref[:, sl] = (ds * sil_ref[:, sl].astype(F32)).astype(BF16)
            dh = _dot(da_ref[...], wg_ref[...]) + _dot(db_ref[...], wu_ref[...])
            finish_previous((i + 1) % 2)
            dh_ref[i % 2] = dh

        @pl.when(i == nt)
        def _():
            finish_previous((nt - 1) % 2)

    prev = lambda i: (jnp.maximum(i - 1, 0), 0)
    cur = lambda i: (jnp.minimum(i, nt - 1), 0)
    return pl.pallas_call(
        _drop_deps(body, 9, len(deps)), name=name, grid=(nt + 1,),
        in_specs=[pl.BlockSpec((tm, d), prev), pl.BlockSpec((tm, d), prev), pl.BlockSpec((tm, d), cur),
                  _resident((1, d)), pl.BlockSpec((tm, f), cur), pl.BlockSpec((tm, f), cur),
                  _resident((f, d)), _resident((f, d)), _resident((f, d))] + [ANY_SPEC] * len(deps),
        out_specs=[pl.BlockSpec((tm, d), prev), pl.BlockSpec((tm, f), cur), pl.BlockSpec((tm, f), cur),
                   pl.BlockSpec((tm, d), cur), pl.BlockSpec((1, d), lambda i: (0, 0))],
        out_shape=[jax.ShapeDtypeStruct((t, d), F32), jax.ShapeDtypeStruct((t, f), BF16),
                   jax.ShapeDtypeStruct((t, f), BF16), jax.ShapeDtypeStruct((t, d), BF16),
                   jax.ShapeDtypeStruct((1, d), F32)],
        scratch_shapes=[pltpu.VMEM((2, tm, d), F32)],
        compiler_params=_params(("arbitrary",)),
    )(x, dxo, dxo, gain, sil, gp, wg_t, wu_t, wd, *deps)


def _row_tile(m, cap):
    if m <= cap:
        return m
    best = None
    for bm in range(LANES, cap + 1, LANES):
        if m % bm == 0:
            best = bm
    return best if best is not None else m


def grad_tn(a, b, scale, name, deps=()):
    t, m = a.shape
    n = b.shape[1]
    bm = _row_tile(m, GRAD_ROW_TILE_CAP)
    bk = min(t, GRAD_TOKENS_PER_STEP)
    nk = t // bk

    def body(a_ref, b_ref, o_ref, acc_ref):
        k = pl.program_id(1)

        @pl.when(k == 0)
        def _():
            acc_ref[...] = jnp.zeros_like(acc_ref)

        acc_ref[...] += _dg(a_ref[...].astype(BF16), b_ref[...].astype(BF16), TN)

        @pl.when(k == nk - 1)
        def _():
            o_ref[...] = (scale * acc_ref[...]).astype(BF16)

    return pl.pallas_call(
        _drop_deps(body, 2, len(deps)), name=name, grid=(m // bm, nk),
        in_specs=[pl.BlockSpec((bk, bm), lambda i, k: (k, i)), pl.BlockSpec((bk, n), lambda i, k: (k, 0))]
        + [ANY_SPEC] * len(deps),
        out_specs=pl.BlockSpec((bm, n), lambda i, k: (i, 0)),
        out_shape=jax.ShapeDtypeStruct((m, n), BF16),
        scratch_shapes=[pltpu.VMEM((bm, n), F32)],
        compiler_params=_params(("arbitrary", "arbitrary")),
    )(a, b, *deps)


def grad_nn(a_list, b, name):
    t, n = b.shape
    ms = [a.shape[0] for a in a_list]
    m = sum(ms)
    bk = min(t, GRAD_TOKENS_PER_STEP)
    nk = t // bk
    na = len(a_list)

    def body(*refs):
        a_refs, b_ref, o_ref, acc_ref = refs[:na], refs[na], refs[na + 1], refs[na + 2]
        k = pl.program_id(0)

        @pl.when(k == 0)
        def _():
            acc_ref[...] = jnp.zeros_like(acc_ref)

        bv = b_ref[...].astype(BF16)
        off = 0
        for a_ref, mi in zip(a_refs, ms):
            acc_ref[off:off + mi, :] += _dot(a_ref[...].astype(BF16), bv)
            off += mi

        @pl.when(k == nk - 1)
        def _():
            o_ref[...] = acc_ref[...].astype(BF16)

    return pl.pallas_call(
        body, name=name, grid=(nk,),
        in_specs=[pl.BlockSpec((mi, bk), lambda k: (0, k)) for mi in ms] + [pl.BlockSpec((bk, n), lambda k: (k, 0))],
        out_specs=pl.BlockSpec((m, n), lambda k: (0, 0)),
        out_shape=jax.ShapeDtypeStruct((m, n), BF16),
        scratch_shapes=[pltpu.VMEM((m, n), F32)],
        compiler_params=_params(("arbitrary",)),
    )(*a_list, b)


def mix_in_fwd(x, gain, win_t, b_in_col, name, ffn_tail=None):
    t, d = x.shape
    w = win_t.shape[0]
    tm = _token_tile(t)

    def body(x_ref, g_ref, w_ref, bias_ref, *rest):
        xv = x_ref[...]
        if ffn_tail is not None:
            s_ref, wd_ref, xo_ref = rest[0], rest[1], rest[4]
            xv = xv + FFN_RES * _dot(s_ref[...], wd_ref[...])
            xo_ref[...] = xv
        qkv_ref, zg_ref = rest[-3:-1] if ffn_tail is not None else rest
        r = lax.rsqrt(jnp.mean(xv * xv, axis=-1, keepdims=True) + EPS)
        h = (xv * r * g_ref[...]).astype(BF16)
        qkv_ref[...] = (_dg(w_ref[:QKV_WIDTH, :], h, NT) + bias_ref[:QKV_WIDTH, :]).astype(BF16)
        zg_ref[...] = (_dg(w_ref[QKV_WIDTH:, :], h, NT) + bias_ref[QKV_WIDTH:, :]).astype(BF16)

    tok = lambda: pl.BlockSpec((tm, d), lambda i: (i, 0))
    in_specs = [tok(), _resident((1, d)), _resident((w, d)), _resident((w, 1))]
    out_specs = [pl.BlockSpec((QKV_WIDTH, tm), lambda i: (0, i)), pl.BlockSpec((w - QKV_WIDTH, tm), lambda i: (0, i))]
    out_shape = [jax.ShapeDtypeStruct((QKV_WIDTH, t), BF16), jax.ShapeDtypeStruct((w - QKV_WIDTH, t), BF16)]
    operands = [x, gain, win_t, b_in_col]
    if ffn_tail is not None:
        f = ffn_tail[1].shape[0]
        in_specs += [pl.BlockSpec((tm, f), lambda i: (i, 0)), _resident((f, d))]
        out_specs.append(tok())
        out_shape.append(jax.ShapeDtypeStruct((t, d), F32))
        operands += list(ffn_tail)
    return pl.pallas_call(
        body, name=name, grid=(t // tm,), in_specs=in_specs, out_specs=out_specs, out_shape=out_shape,
        compiler_params=_params(("arbitrary",)),
    )(*operands)


ATTN_REP = N_Q_HEADS // N_KV_HEADS
ATTN_BLOCKS_PER_STEP = 8


def _attn_tiles(t):
    nbs = min(ATTN_BLOCKS_PER_STEP, t // BLOCK)
    return nbs, nbs * BLOCK, t // (nbs * BLOCK)


def _attn_specs(nbs, tiles):
    last = tiles - 1
    cur = lambda n: jnp.minimum(n, last)
    prev = lambda n: jnp.maximum(jnp.minimum(n, last) * nbs - 1, 0)
    k_row = ATTN_WIDTH // KV_WIDTH
    tw = nbs * BLOCK
    return [
        pl.BlockSpec((ATTN_WIDTH, tw), lambda n: (0, cur(n))),
        pl.BlockSpec((KV_WIDTH, BLOCK), lambda n: (k_row, prev(n))),
        pl.BlockSpec((KV_WIDTH, tw), lambda n: (k_row, cur(n))),
        pl.BlockSpec((KV_WIDTH, BLOCK), lambda n: (k_row + 1, prev(n))),
        pl.BlockSpec((KV_WIDTH, tw), lambda n: (k_row + 1, cur(n))),
    ]


def _cols(b):
    return slice(b * BLOCK, (b + 1) * BLOCK)


def _band_rows(prev_ref, tile_ref, gs, b):
    before = prev_ref[gs, :] if b == 0 else tile_ref[gs, _cols(b - 1)]
    return before, tile_ref[gs, _cols(b)]


def _group_rows(ref, g, b):
    first = g * ATTN_REP
    return jnp.concatenate([ref[(first + r) * HEAD_DIM:(first + r + 1) * HEAD_DIM, _cols(b)] for r in range(ATTN_REP)],
                           axis=1)


def _ungroup_rows(ref, g, b, val):
    first = g * ATTN_REP
    for r in range(ATTN_REP):
        ref[(first + r) * HEAD_DIM:(first + r + 1) * HEAD_DIM, _cols(b)] = val[:, r * BLOCK:(r + 1) * BLOCK]


def _attn_upper():
    key = lax.broadcasted_iota(jnp.int32, (BLOCK, ATTN_REP * BLOCK), 0)
    qry = lax.broadcasted_iota(jnp.int32, (BLOCK, ATTN_REP * BLOCK), 1) & (BLOCK - 1)
    return key > qry


def _attn_probs(q, kp, kc, sink_ref, g, upper, no_prev):
    s = jnp.where(upper, _dg(kp, q, TN), _dg(kc, q, TN)) * ATTN_SCALE
    if no_prev is not None:
        s = jnp.where(upper & no_prev, MASK_VALUE, s)
    sink = jnp.concatenate([jnp.full((1, BLOCK), sink_ref[0, g * ATTN_REP + r], F32) for r in range(ATTN_REP)], axis=1)
    m = jnp.maximum(jnp.max(s, axis=0, keepdims=True), sink)
    e = jnp.exp(s - m)
    es = jnp.exp(sink - m)
    inv = 1.0 / (jnp.sum(e, axis=0, keepdims=True) + es)
    return e * inv, es * inv


def _split_band(upper, val):
    return jnp.where(upper, val, 0.0).astype(BF16), jnp.where(upper, 0.0, val).astype(BF16)


def attn_fwd(qkv_t, sinks, name):
    t = qkv_t.shape[1]
    nbs, tw, tiles = _attn_tiles(t)

    def body(sink_ref, q_ref, kp_ref, kc_ref, vp_ref, vc_ref, y_ref):
        n = pl.program_id(0)
        upper = _attn_upper()
        for b in range(nbs):
            for g in range(N_KV_HEADS):
                gs = slice(g * HEAD_DIM, (g + 1) * HEAD_DIM)
                kp, kc = _band_rows(kp_ref, kc_ref, gs, b)
                vp, vc = _band_rows(vp_ref, vc_ref, gs, b)
                p, _ = _attn_probs(_group_rows(q_ref, g, b), kp, kc, sink_ref, g, upper, n == 0 if b == 0 else None)
                pp, pc = _split_band(upper, p)
                _ungroup_rows(y_ref, g, b, (_dot(vp, pp) + _dot(vc, pc)).astype(BF16))

    return pl.pallas_call(
        body, name=name, grid=(tiles,),
        in_specs=[pl.BlockSpec(memory_space=pltpu.SMEM)] + _attn_specs(nbs, tiles),
        out_specs=pl.BlockSpec((ATTN_WIDTH, tw), lambda n: (0, n)),
        out_shape=jax.ShapeDtypeStruct((ATTN_WIDTH, t), BF16),
        compiler_params=_params(("arbitrary",)),
    )(sinks, qkv_t, qkv_t, qkv_t, qkv_t, qkv_t)


def attn_bwd(qkv_t, dy_t, sinks, name):
    t = qkv_t.shape[1]
    nbs, tw, tiles = _attn_tiles(t)
    kept = slice(0, tw - BLOCK)

    def body(sink_ref, q_ref, kp_ref, kc_ref, vp_ref, vc_ref, do_ref, dq_ref, dk_ref, dv_ref, dsink_ref,
             ck_ref, cv_ref, sacc_ref):
        n = pl.program_id(0)

        @pl.when(n == 0)
        def _():
            sacc_ref[...] = jnp.zeros_like(sacc_ref)

        @pl.when((n > 0) & (n < tiles))
        def _():
            if nbs > 1:
                dk_ref[:, kept] = ck_ref[:, kept].astype(BF16)
                dv_ref[:, kept] = cv_ref[:, kept].astype(BF16)

        @pl.when(n < tiles)
        def _():
            upper = _attn_upper()
            for b in range(nbs):
                for g in range(N_KV_HEADS):
                    gs = slice(g * HEAD_DIM, (g + 1) * HEAD_DIM)
                    kp, kc = _band_rows(kp_ref, kc_ref, gs, b)
                    vp, vc = _band_rows(vp_ref, vc_ref, gs, b)
                    q = _group_rows(q_ref, g, b)
                    do = _group_rows(do_ref, g, b)
                    p, ps = _attn_probs(q, kp, kc, sink_ref, g, upper, n == 0 if b == 0 else None)
                    dp = jnp.where(upper, _dg(vp, do, TN), _dg(vc, do, TN))
                    delta = jnp.sum(p * dp, axis=0, keepdims=True)
                    dsink = -(ps * delta)
                    for r in range(ATTN_REP):
                        hd = g * ATTN_REP + r
                        sacc_ref[hd:hd + 1, :] += dsink[:, r * BLOCK:(r + 1) * BLOCK]
                    dsp, dsc = _split_band(upper, p * (dp - delta))
                    pp, pc = _split_band(upper, p)
                    _ungroup_rows(dq_ref, g, b, (ATTN_SCALE * (_dot(kp, dsp) + _dot(kc, dsc))).astype(BF16))
                    dk_before = ATTN_SCALE * _dg(q, dsp, NT)
                    dv_before = _dg(do, pp, NT)
                    if b == 0:
                        @pl.when(n > 0)
                        def _():
                            dk_ref[gs, _cols(nbs - 1)] = (ck_ref[gs, _cols(nbs - 1)] + dk_before).astype(BF16)
                            dv_ref[gs, _cols(nbs - 1)] = (cv_ref[gs, _cols(nbs - 1)] + dv_before).astype(BF16)
                    else:
                        ck_ref[gs, _cols(b - 1)] += dk_before
                        cv_ref[gs, _cols(b - 1)] += dv_before
                    ck_ref[gs, _cols(b)] = ATTN_SCALE * _dg(q, dsc, NT)
                    cv_ref[gs, _cols(b)] = _dg(do, pc, NT)

        @pl.when(n == tiles)
        def _():
            dk_ref[...] = ck_ref[...].astype(BF16)
            dv_ref[...] = cv_ref[...].astype(BF16)
            dsink_ref[...] = jnp.sum(sacc_ref[...], axis=1, keepdims=True)

    last = tiles - 1
    done = lambda n: jnp.maximum(n - 1, 0)
    outs = pl.pallas_call(
        body, name=name, grid=(tiles + 1,),
        in_specs=[pl.BlockSpec(memory_space=pltpu.SMEM)] + _attn_specs(nbs, tiles)
        + [pl.BlockSpec((ATTN_WIDTH, tw), lambda n: (0, jnp.minimum(n, last)))],
        out_specs=[pl.BlockSpec((ATTN_WIDTH, tw), lambda n: (0, jnp.minimum(n, last))),
                   pl.BlockSpec((KV_WIDTH, tw), lambda n: (0, done(n))),
                   pl.BlockSpec((KV_WIDTH, tw), lambda n: (0, done(n))),
                   pl.BlockSpec((N_Q_HEADS, 1), lambda n: (0, 0))],
        out_shape=[jax.ShapeDtypeStruct((ATTN_WIDTH, t), BF16), jax.ShapeDtypeStruct((KV_WIDTH, t), BF16),
                   jax.ShapeDtypeStruct((KV_WIDTH, t), BF16), jax.ShapeDtypeStruct((N_Q_HEADS, 1), F32)],
        scratch_shapes=[pltpu.VMEM((KV_WIDTH, tw), F32), pltpu.VMEM((KV_WIDTH, tw), F32),
                        pltpu.VMEM((N_Q_HEADS, BLOCK), F32)],
        compiler_params=_params(("arbitrary",)),
    )(sinks, qkv_t, qkv_t, qkv_t, qkv_t, qkv_t, dy_t)
    return outs


GELU_C = 0.7978845608028654
GELU_A = 0.044715


def _gelu(x):
    return 0.5 * x * (1.0 + jnp.tanh(GELU_C * (x + GELU_A * x * x * x)))


def _gelu_grad(x):
    th = jnp.tanh(GELU_C * (x + GELU_A * x * x * x))
    return 0.5 * (1.0 + th) + 0.5 * x * (1.0 - th * th) * GELU_C * (1.0 + 3.0 * GELU_A * x * x)


def _gmlp_parts(zg, lng, lnb):
    z = _gelu(zg)
    u = z[:GMLP_WIDTH, :]
    vv = z[GMLP_WIDTH:, :]
    mu = jnp.mean(vv, axis=0, keepdims=True)
    xc = vv - mu
    rstd = lax.rsqrt(jnp.mean(xc * xc, axis=0, keepdims=True) + EPS)
    xhat = xc * rstd
    return u, xhat, rstd, xhat * lng + lnb


def _causal(w):
    row = lax.broadcasted_iota(jnp.int32, (BLOCK, BLOCK), 0)
    col = lax.broadcasted_iota(jnp.int32, (BLOCK, BLOCK), 1)
    return jnp.where(col <= row, w, 0.0)


GMLP_CHUNKS_PER_STEP = 8


def _stack_chunks(v, nc):
    return jnp.concatenate([v[:, c * BLOCK:(c + 1) * BLOCK] for c in range(nc)], axis=0)


def _unstack_chunks(v, nc):
    rows = v.shape[0] // nc
    return jnp.concatenate([v[c * rows:(c + 1) * rows, :] for c in range(nc)], axis=1)


def gmlp_fwd(zg_t, lng, lnb, w_s, b_s, name):
    t = zg_t.shape[1]
    nc = min(GMLP_CHUNKS_PER_STEP, t // BLOCK)
    tw = nc * BLOCK

    def body(zg_ref, lng_ref, lnb_ref, w_ref, bs_ref, y_ref):
        u, _, _, vn = _gmlp_parts(zg_ref[...].astype(F32), lng_ref[...], lnb_ref[...])
        for g in range(GMLP_GROUPS):
            gs = slice(g * GMLP_GROUP_DIM, (g + 1) * GMLP_GROUP_DIM)
            wc = _causal(w_ref[g]).astype(BF16)
            mixed = _dg(_stack_chunks(vn[gs, :].astype(BF16), nc), wc, NT) + bs_ref[g:g + 1, :]
            y_ref[gs, :] = (u[gs, :] * _unstack_chunks(mixed, nc)).astype(BF16)

    return pl.pallas_call(
        body, name=name, grid=(t // tw,),
        in_specs=[pl.BlockSpec((2 * GMLP_WIDTH, tw), lambda n: (0, n)), _resident((GMLP_WIDTH, 1)),
                  _resident((GMLP_WIDTH, 1)), _resident((GMLP_GROUPS, BLOCK, BLOCK)), _resident((GMLP_GROUPS, BLOCK))],
        out_specs=pl.BlockSpec((GMLP_WIDTH, tw), lambda n: (0, n)),
        out_shape=jax.ShapeDtypeStruct((GMLP_WIDTH, t), BF16),
        compiler_params=_params(("arbitrary",)),
    )(zg_t, lng, lnb, w_s, b_s)


def gmlp_bwd(zg_t, dy_t, lng, lnb, w_s, b_s, name):
    t = zg_t.shape[1]
    nc = min(GMLP_CHUNKS_PER_STEP, t // BLOCK)
    tw = nc * BLOCK
    nt = t // tw

    def body(zg_ref, dy_ref, lng_ref, lnb_ref, w_ref, bs_ref, dzg_ref, dw_ref, dbs_ref, dlng_ref, dlnb_ref,
             gacc_ref, bacc_ref):
        n = pl.program_id(0)

        @pl.when(n == 0)
        def _():
            dw_ref[...] = jnp.zeros_like(dw_ref)
            dbs_ref[...] = jnp.zeros_like(dbs_ref)
            gacc_ref[...] = jnp.zeros_like(gacc_ref)
            bacc_ref[...] = jnp.zeros_like(bacc_ref)

        zg = zg_ref[...].astype(F32)
        lng_v = lng_ref[...]
        u, xhat, rstd, vn = _gmlp_parts(zg, lng_v, lnb_ref[...])
        dy = dy_ref[...].astype(F32)
        dvn_parts = []
        for g in range(GMLP_GROUPS):
            gs = slice(g * GMLP_GROUP_DIM, (g + 1) * GMLP_GROUP_DIM)
            wc = _causal(w_ref[g]).astype(BF16)
            vn_s = _stack_chunks(vn[gs, :].astype(BF16), nc)
            mixed = _unstack_chunks(_dg(vn_s, wc, NT) + bs_ref[g:g + 1, :], nc)
            dy_g = dy[gs, :]
            dmixed = dy_g * u[gs, :]
            dm_s = _stack_chunks(dmixed.astype(BF16), nc)
            dzg_ref[gs, :] = (dy_g * mixed * _gelu_grad(zg[gs, :])).astype(BF16)
            dw_ref[g] += _causal(_dg(dm_s, vn_s, TN))
            dbs_ref[g:g + 1, :] += _lane_fold(jnp.sum(dmixed, axis=0, keepdims=True))
            dvn_parts.append(_unstack_chunks(_dot(dm_s, wc), nc))
        dvn = jnp.concatenate(dvn_parts, axis=0)
        gacc_ref[...] += _lane_fold(dvn * xhat)
        bacc_ref[...] += _lane_fold(dvn)
        dxhat = dvn * lng_v
        m1 = jnp.mean(dxhat, axis=0, keepdims=True)
        m2 = jnp.mean(dxhat * xhat, axis=0, keepdims=True)
        dvv = rstd * (dxhat - m1 - xhat * m2)
        dzg_ref[GMLP_WIDTH:, :] = (dvv * _gelu_grad(zg[GMLP_WIDTH:, :])).astype(BF16)

        @pl.when(n == nt - 1)
        def _():
            dlng_ref[...] = jnp.sum(gacc_ref[...], axis=1, keepdims=True)
            dlnb_ref[...] = jnp.sum(bacc_ref[...], axis=1, keepdims=True)

    const2 = lambda n: (0, 0)
    return pl.pallas_call(
        body, name=name, grid=(nt,),
        in_specs=[pl.BlockSpec((2 * GMLP_WIDTH, tw), lambda n: (0, n)), pl.BlockSpec((GMLP_WIDTH, tw), lambda n: (0, n)),
                  _resident((GMLP_WIDTH, 1)), _resident((GMLP_WIDTH, 1)),
                  _resident((GMLP_GROUPS, BLOCK, BLOCK)), _resident((GMLP_GROUPS, BLOCK))],
        out_specs=[pl.BlockSpec((2 * GMLP_WIDTH, tw), lambda n: (0, n)),
                   pl.BlockSpec((GMLP_GROUPS, BLOCK, BLOCK), lambda n: (0, 0, 0)),
                   pl.BlockSpec((GMLP_GROUPS, BLOCK), const2),
                   pl.BlockSpec((GMLP_WIDTH, 1), const2), pl.BlockSpec((GMLP_WIDTH, 1), const2)],
        out_shape=[jax.ShapeDtypeStruct((2 * GMLP_WIDTH, t), BF16),
                   jax.ShapeDtypeStruct((GMLP_GROUPS, BLOCK, BLOCK), F32),
                   jax.ShapeDtypeStruct((GMLP_GROUPS, BLOCK), F32),
                   jax.ShapeDtypeStruct((GMLP_WIDTH, 1), F32), jax.ShapeDtypeStruct((GMLP_WIDTH, 1), F32)],
        scratch_shapes=[pltpu.VMEM((GMLP_WIDTH, BLOCK), F32), pltpu.VMEM((GMLP_WIDTH, BLOCK), F32)],
        compiler_params=_params(("arbitrary",)),
    )(zg_t, dy_t, lng, lnb, w_s, b_s)


def _rmsnorm_cols(y, gain_col):
    r = lax.rsqrt(jnp.mean(y * y, axis=0, keepdims=True) + EPS)
    n = y * r
    return n, r, n * gain_col


def mix_out_fwd(x, ya_t, yg_t, gao, ggo, wout, b_out, name):
    t, d = x.shape
    tm = _token_tile(t)

    def body(x_ref, ya_ref, yg_ref, gao_ref, ggo_ref, w_ref, b_ref, o_ref):
        _, _, ya = _rmsnorm_cols(ya_ref[...].astype(F32), gao_ref[...])
        _, _, yg = _rmsnorm_cols(yg_ref[...].astype(F32), ggo_ref[...])
        o_ref[...] = (x_ref[...] + _dg(ya.astype(BF16), w_ref[:ATTN_WIDTH, :], TN)
                      + _dg(yg.astype(BF16), w_ref[ATTN_WIDTH:, :], TN) + b_ref[...])

    return pl.pallas_call(
        body, name=name, grid=(t // tm,),
        in_specs=[pl.BlockSpec((tm, d), lambda i: (i, 0)), pl.BlockSpec((ATTN_WIDTH, tm), lambda i: (0, i)),
                  pl.BlockSpec((GMLP_WIDTH, tm), lambda i: (0, i)), _resident((ATTN_WIDTH, 1)), _resident((GMLP_WIDTH, 1)),
                  _resident((ATTN_WIDTH + GMLP_WIDTH, d)), _resident((1, d))],
        out_specs=pl.BlockSpec((tm, d), lambda i: (i, 0)),
        out_shape=jax.ShapeDtypeStruct((t, d), F32),
        compiler_params=_params(("arbitrary",)),
    )(x, ya_t, yg_t, gao, ggo, wout, b_out)


def _lane_fold(v):
    out = v[:, :LANES]
    for j in range(1, v.shape[1] // LANES):
        out = out + v[:, j * LANES:(j + 1) * LANES]
    return out


def mix_out_bwd(dx, ya_t, yg_t, gao, ggo, wout, name, deps=()):
    t, d = dx.shape
    tm = _token_tile(t)
    nt = t // tm

    def body(dx_ref, ya_ref, yg_ref, gao_ref, ggo_ref, w_ref, dya_ref, dyg_ref, y_ref, dgao_ref, dggo_ref, db_ref,
             acc_ref):
        i = pl.program_id(0)

        @pl.when(i == 0)
        def _():
            acc_ref[...] = jnp.zeros_like(acc_ref)
            db_ref[...] = jnp.zeros_like(db_ref)

        dxv = dx_ref[...]
        db_ref[...] += jnp.sum(dxv, axis=0, keepdims=True)
        dxb = dxv.astype(BF16)
        for part, (src_ref, gain_ref, dst_ref) in enumerate(((ya_ref, gao_ref, dya_ref), (yg_ref, ggo_ref, dyg_ref))):
            rows = slice(part * ATTN_WIDTH, (part + 1) * ATTN_WIDTH)
            gain = gain_ref[...]
            nrm, r, yn = _rmsnorm_cols(src_ref[...].astype(F32), gain)
            y_ref[rows, :] = yn.astype(BF16)
            dy = _dg(w_ref[rows, :], dxb, NT)
            acc_ref[rows, :] += _lane_fold(dy * nrm)
            dn = dy * gain
            dst_ref[...] = (r * (dn - nrm * jnp.mean(dn * nrm, axis=0, keepdims=True))).astype(BF16)

        @pl.when(i == nt - 1)
        def _():
            dgao_ref[...] = jnp.sum(acc_ref[:ATTN_WIDTH, :], axis=1, keepdims=True)
            dggo_ref[...] = jnp.sum(acc_ref[ATTN_WIDTH:, :], axis=1, keepdims=True)

    const2 = lambda i: (0, 0)
    feat = lambda w: pl.BlockSpec((w, tm), lambda i: (0, i))
    return pl.pallas_call(
        _drop_deps(body, 6, len(deps)), name=name, grid=(nt,),
        in_specs=[pl.BlockSpec((tm, d), lambda i: (i, 0)), feat(ATTN_WIDTH), feat(GMLP_WIDTH),
                  _resident((ATTN_WIDTH, 1)), _resident((GMLP_WIDTH, 1)), _resident((ATTN_WIDTH + GMLP_WIDTH, d))]
        + [ANY_SPEC] * len(deps),
        out_specs=[feat(ATTN_WIDTH), feat(GMLP_WIDTH), feat(ATTN_WIDTH + GMLP_WIDTH),
                   pl.BlockSpec((ATTN_WIDTH, 1), const2), pl.BlockSpec((GMLP_WIDTH, 1), const2),
                   pl.BlockSpec((1, d), const2)],
        out_shape=[jax.ShapeDtypeStruct((ATTN_WIDTH, t), BF16), jax.ShapeDtypeStruct((GMLP_WIDTH, t), BF16),
                   jax.ShapeDtypeStruct((ATTN_WIDTH + GMLP_WIDTH, t), BF16),
                   jax.ShapeDtypeStruct((ATTN_WIDTH, 1), F32), jax.ShapeDtypeStruct((GMLP_WIDTH, 1), F32),
                   jax.ShapeDtypeStruct((1, d), F32)],
        scratch_shapes=[pltpu.VMEM((ATTN_WIDTH + GMLP_WIDTH, LANES), F32)],
        compiler_params=_params(("arbitrary",)),
    )(dx, ya_t, yg_t, gao, ggo, wout, *deps)


def mix_in_bwd(x, dxo, gain, dq_t, dk_t, dv_t, dzg_t, win_t, name):
    t, d = x.shape
    w = win_t.shape[0]
    tm = _token_tile(t)
    nt = t // tm
    parts = ((0, ATTN_WIDTH), (ATTN_WIDTH, KV_WIDTH), (ATTN_WIDTH + KV_WIDTH, KV_WIDTH), (QKV_WIDTH, w - QKV_WIDTH))

    def body(x_ref, dxo_ref, g_ref, dq_ref, dk_ref, dv_ref, dzg_ref, w_ref, dx_ref, h_ref, dg_ref, db_ref, acc_ref):
        i = pl.program_id(0)

        @pl.when(i == 0)
        def _():
            acc_ref[...] = jnp.zeros_like(acc_ref)
            dg_ref[...] = jnp.zeros_like(dg_ref)

        xv = x_ref[...]
        gain_v = g_ref[...]
        r = lax.rsqrt(jnp.mean(xv * xv, axis=-1, keepdims=True) + EPS)
        h_ref[...] = (xv * r * gain_v).astype(BF16)
        dh = jnp.zeros((tm, d), F32)
        for (off, size), src_ref in zip(parts, (dq_ref, dk_ref, dv_ref, dzg_ref)):
            dp = src_ref[...]
            acc_ref[off:off + size, :] += _lane_fold(dp.astype(F32))
            dh = dh + _dg(dp, w_ref[off:off + size, :], TN)
        dx, dgain = _rmsnorm_bwd_rows(xv, r, gain_v, dh)
        dx_ref[...] = dxo_ref[...] + dx
        dg_ref[...] += dgain

        @pl.when(i == nt - 1)
        def _():
            db_ref[...] = jnp.sum(acc_ref[...], axis=1, keepdims=True)

    const2 = lambda i: (0, 0)
    tok = lambda: pl.BlockSpec((tm, d), lambda i: (i, 0))
    feat = lambda rows: pl.BlockSpec((rows, tm), lambda i: (0, i))
    return pl.pallas_call(
        body, name=name, grid=(nt,),
        in_specs=[tok(), tok(), _resident((1, d)), feat(ATTN_WIDTH), feat(KV_WIDTH), feat(KV_WIDTH),
                  feat(w - QKV_WIDTH), _resident((w, d))],
        out_specs=[tok(), tok(), pl.BlockSpec((1, d), const2), pl.BlockSpec((w, 1), const2)],
        out_shape=[jax.ShapeDtypeStruct((t, d), F32), jax.ShapeDtypeStruct((t, d), BF16),
                   jax.ShapeDtypeStruct((1, d), F32), jax.ShapeDtypeStruct((w, 1), F32)],
        scratch_shapes=[pltpu.VMEM((w, LANES), F32)],
        compiler_params=_params(("arbitrary",)),
    )(x, dxo, gain, dq_t, dk_t, dv_t, dzg_t, win_t)


def _adamw_math(w, g, m, v):
    m = ADAM_B1 * m + (1.0 - ADAM_B1) * g
    v = ADAM_B2 * v + (1.0 - ADAM_B2) * (g * g)
    m_hat = m / (1.0 - ADAM_B1 ** ADAM_STEP)
    v_hat = v / (1.0 - ADAM_B2 ** ADAM_STEP)
    delta = -ADAM_LR * (m_hat / (jnp.sqrt(v_hat) + ADAM_EPS) + ADAM_WD * w)
    return delta, m, v


def adamw_many(ws, gs, ms, vs, name):
    n = len(ws)

    def body(*refs):
        w_refs, g_refs, m_refs, v_refs = refs[:n], refs[n:2 * n], refs[2 * n:3 * n], refs[3 * n:4 * n]
        d_refs, mo_refs, vo_refs = refs[4 * n:5 * n], refs[5 * n:6 * n], refs[6 * n:]
        for k in range(n):
            delta, mn, vn = _adamw_math(w_refs[k][...], g_refs[k][...], m_refs[k][...], v_refs[k][...])
            d_refs[k][...] = delta
            mo_refs[k][...] = mn
            vo_refs[k][...] = vn

    def whole(a):
        zeros = (0,) * a.ndim
        return pl.BlockSpec(a.shape, lambda i: zeros)

    outs = pl.pallas_call(
        body, name=name, grid=(1,),
        in_specs=[whole(a) for a in ws] * 4, out_specs=[whole(a) for a in ws] * 3,
        out_shape=[jax.ShapeDtypeStruct(a.shape, F32) for a in ws] * 3,
        compiler_params=_params(("arbitrary",)),
    )(*ws, *gs, *ms, *vs)
    return outs[:n], outs[n:2 * n], outs[2 * n:]


def sum_adamw(landing, grad3d, me, w, m, v, name):
    nd, rows, cols = landing.shape
    tr = rows // 2

    def body(me_ref, l_ref, own_ref, w_ref, m_ref, v_ref, g_ref, d_ref, mo_ref, vo_ref):
        g = own_ref[...].astype(F32)
        for j in range(nd):
            g = g + l_ref[j].astype(F32)
        delta, mn, vn = _adamw_math(w_ref[...], g, m_ref[...], v_ref[...])
        g_ref[...] = g
        d_ref[...] = delta
        mo_ref[...] = mn
        vo_ref[...] = vn

    spec = lambda: pl.BlockSpec((tr, cols), lambda i, me_ref: (i, 0))
    shape = jax.ShapeDtypeStruct((rows, cols), F32)
    return pl.pallas_call(
        body, name=name,
        grid_spec=pltpu.PrefetchScalarGridSpec(
            num_scalar_prefetch=1, grid=(rows // tr,),
            in_specs=[pl.BlockSpec((nd, tr, cols), lambda i, me_ref: (0, i, 0)),
                      pl.BlockSpec((None, tr, cols), lambda i, me_ref: (me_ref[0], i, 0)), spec(), spec(), spec()],
            out_specs=[spec() for _ in range(4)]),
        out_shape=[shape] * 4,
        compiler_params=_params(("arbitrary",)),
    )(me, landing, grad3d, w, m, v)


def _mesh_place():
    x, y, c = lax.axis_index("x"), lax.axis_index("y"), lax.axis_index("c")
    return x, y, c, 4 * x + 2 * y + c


def _peer(x, y, c, k):
    return (x ^ ((k >> 2) & 1), y ^ ((k >> 1) & 1), c ^ (k & 1))


def prep_shards(mats, me, name):
    na = len(mats)

    def body(me_ref, *refs):
        in_refs, shard_refs, land_refs = refs[:na], refs[na:2 * na], refs[2 * na:]
        for a in range(na):
            val = in_refs[a][...].astype(BF16)
            shard_refs[a][...] = val
            land_refs[a][...] = val

    whole = lambda mat: pl.BlockSpec(mat.shape, lambda i, me_ref: (0, 0))
    outs = pl.pallas_call(
        body, name=name,
        grid_spec=pltpu.PrefetchScalarGridSpec(
            num_scalar_prefetch=1, grid=(1,),
            in_specs=[whole(mat) for mat in mats],
            out_specs=[whole(mat) for mat in mats]
            + [pl.BlockSpec((None,) + mat.shape, lambda i, me_ref: (me_ref[0], 0, 0)) for mat in mats]),
        out_shape=[jax.ShapeDtypeStruct(mat.shape, BF16) for mat in mats]
        + [jax.ShapeDtypeStruct((N_DEV,) + mat.shape, BF16) for mat in mats],
        compiler_params=_params(("arbitrary",)),
    )(me, *mats)
    return outs[:na], outs[na:]


def allgather_rows(shards, lands, name):
    na = n_gather = len(shards)

    def body(*refs):
        in_refs, out_refs = refs[:na], refs[2 * na:3 * na]
        send_sems, recv_sems = refs[3 * na:]
        x, y, c, me = _mesh_place()

        def copy(sem, a, origin, src, k):
            return pltpu.make_async_remote_copy(
                src_ref=src, dst_ref=out_refs[a].at[origin], send_sem=send_sems.at[sem, a],
                recv_sem=recv_sems.at[sem, a], device_id=_peer(x, y, c, k), device_id_type=pl.DeviceIdType.MESH)

        kx, ky = 4, 2

        def recv(sem, a, origin):
            copy(sem, a, origin, in_refs[a], 1).wait_recv()

        def landed(a, origin):
            return out_refs[a].at[origin]

        @pl.when(c == 1)
        def _():
            sends = []

            def send(sem, a, origin, src, k):
                cp = copy(sem, a, origin, src, k)
                cp.start()
                sends.append(cp)

            for a in range(n_gather):
                send(1, a, me, in_refs[a], kx)
                send(2, a, me, in_refs[a], ky)
                send(0, a, me, in_refs[a], 1)
            for a in range(n_gather):
                recv(0, a, me ^ 1)
                send(3, a, me ^ 1, landed(a, me ^ 1), kx)
                send(4, a, me ^ 1, landed(a, me ^ 1), ky)
            for a in range(n_gather):
                recv(1, a, me ^ kx)
                send(5, a, me ^ kx, landed(a, me ^ kx), ky)
                send(7, a, me ^ kx, landed(a, me ^ kx), 1)
                recv(2, a, me ^ ky)
                send(8, a, me ^ ky, landed(a, me ^ ky), 1)
            for a in range(n_gather):
                recv(3, a, me ^ kx ^ 1)
                send(9, a, me ^ kx ^ 1, landed(a, me ^ kx ^ 1), 1)
                recv(4, a, me ^ ky ^ 1)
                send(6, a, me ^ ky ^ 1, landed(a, me ^ ky ^ 1), kx)
                send(10, a, me ^ ky ^ 1, landed(a, me ^ ky ^ 1), 1)
            for a in range(n_gather):
                recv(5, a, me ^ 6)
                send(11, a, me ^ 6, landed(a, me ^ 6), 1)
                recv(6, a, me ^ 7)
                send(12, a, me ^ 7, landed(a, me ^ 7), 1)
            for cp in sends:
                cp.wait_send()

        @pl.when(c == 0)
        def _():
            north = me ^ 1
            own = [copy(0, a, me, in_refs[a], 1) for a in range(n_gather)]
            for cp in own:
                cp.start()
            for a in range(n_gather):
                recv(0, a, north)
                for sem, origin in ((7, north ^ kx), (8, north ^ ky), (9, north ^ kx ^ 1), (10, north ^ ky ^ 1),
                                    (11, north ^ 6), (12, north ^ 7)):
                    recv(sem, a, origin)
            for cp in own:
                cp.wait_send()

    return pl.pallas_call(
        body, name=name,
        in_specs=[ANY_SPEC] * (2 * na), out_specs=[ANY_SPEC] * na,
        out_shape=[jax.ShapeDtypeStruct(land.shape, land.dtype) for land in lands],
        input_output_aliases={na + a: a for a in range(na)},
        scratch_shapes=[pltpu.SemaphoreType.DMA((N_GATHER_ROLES, n_gather)),
                        pltpu.SemaphoreType.DMA((N_GATHER_ROLES, n_gather))],
    )(*shards, *lands)


HBM_SPEC = pl.BlockSpec(memory_space=pltpu.HBM)
SEM_SPEC = pl.BlockSpec(memory_space=pltpu.SEMAPHORE)
SIDE_EFFECT = pltpu.SideEffectType.DATAFLOW_SIDE_EFFECTING


def _hbm(v):
    return pltpu.with_memory_space_constraint(v, pltpu.HBM)


def gather_slices(src, land, me, k):
    return src, land.at[me], land.at[me ^ k]


def exchange_slices(src, land, me, k):
    return src.at[me ^ k], land.at[k - 1], land.at[k - 1]


def split_start(groups, slices, deps, name):
    sizes = [len(srcs) for srcs, _ in groups]
    flat_src = [s for srcs, _ in groups for s in srcs]
    flat_land = [l for _, lands in groups for l in lands]
    na, ng, nd = len(flat_src), len(groups), len(deps)

    def body(*refs):
        src_refs, land_refs = refs[:na], refs[na:2 * na]
        sem_refs = refs[2 * na + nd:2 * na + nd + 2 * ng]
        token_ref = refs[-1]
        x, y, c, me = _mesh_place()
        a0 = 0
        for gi, size in enumerate(sizes):
            send_sems, recv_sems = sem_refs[2 * gi], sem_refs[2 * gi + 1]
            for k in range(1, N_DEV):
                for j in range(size):
                    src, dst, _ = slices(src_refs[a0 + j], land_refs[a0 + j], me, k)
                    pltpu.make_async_remote_copy(
                        src_ref=src, dst_ref=dst, send_sem=send_sems.at[(k - 1) * size + j],
                        recv_sem=recv_sems.at[(k - 1) * size + j],
                        device_id=_peer(x, y, c, k), device_id_type=pl.DeviceIdType.MESH).start()
            a0 += size
        token_ref[...] = jnp.zeros_like(token_ref)

    sems = [pltpu.SemaphoreType.DMA(((N_DEV - 1) * size,)) for size in sizes for _ in range(2)]
    thru = [pltpu.HBM(v.shape, v.dtype) for v in flat_src + flat_land]
    outs = pl.pallas_call(
        body, name=name,
        in_specs=[HBM_SPEC] * (2 * na) + [ANY_SPEC] * nd,
        out_specs=[SEM_SPEC] * (2 * ng) + [HBM_SPEC] * (2 * na) + [pl.BlockSpec(memory_space=pltpu.VMEM)],
        out_shape=sems + thru + [jax.ShapeDtypeStruct((8, LANES), F32)],
        input_output_aliases={i: 2 * ng + i for i in range(2 * na)},
        compiler_params=pltpu.CompilerParams(has_side_effects=SIDE_EFFECT),
    )(*[_hbm(v) for v in flat_src + flat_land], *deps)
    started, a0 = [], 0
    for gi, size in enumerate(sizes):
        srcs = outs[2 * ng + a0:2 * ng + a0 + size]
        lands = outs[2 * ng + na + a0:2 * ng + na + a0 + size]
        started.append((outs[2 * gi], outs[2 * gi + 1], list(srcs), list(lands)))
        a0 += size
    return started, outs[-1]


def split_wait(started, slices, after, name):
    send_sems, recv_sems, srcs, lands = started
    na = len(srcs)

    def body(*refs):
        src_refs, land_refs = refs[:na], refs[na:2 * na]
        send_ref, recv_ref = refs[2 * na], refs[2 * na + 1]
        x, y, c, me = _mesh_place()
        for k in range(1, N_DEV):
            for j in range(na):
                src, _, got = slices(src_refs[j], land_refs[j], me, k)
                cp = pltpu.make_async_remote_copy(
                    src_ref=src, dst_ref=got, send_sem=send_ref.at[(k - 1) * na + j], recv_sem=recv_ref.at[(k - 1) * na + j],
                    device_id=_peer(x, y, c, k), device_id_type=pl.DeviceIdType.MESH)
                cp.wait_send()
                cp.wait_recv()

    outs = pl.pallas_call(
        body, name=name,
        in_specs=[HBM_SPEC] * (2 * na) + [SEM_SPEC, SEM_SPEC] + [ANY_SPEC] * len(after),
        out_specs=[HBM_SPEC] * (2 * na),
        out_shape=[pltpu.HBM(v.shape, v.dtype) for v in srcs + lands],
        input_output_aliases={i: i for i in range(2 * na)},
        compiler_params=pltpu.CompilerParams(has_side_effects=SIDE_EFFECT),
    )(*srcs, *lands, send_sems, recv_sems, *after)
    return list(outs[:na]), list(outs[na:])


def place_own(pack, me, name):
    rows, cols = pack.shape

    def body(me_ref, p_ref, o_ref):
        o_ref[...] = p_ref[...]

    return pl.pallas_call(
        body, name=name,
        grid_spec=pltpu.PrefetchScalarGridSpec(
            num_scalar_prefetch=1, grid=(1,),
            in_specs=[pl.BlockSpec((rows, cols), lambda i, me_ref: (0, 0))],
            out_specs=pl.BlockSpec((None, rows, cols), lambda i, me_ref: (me_ref[0], 0, 0))),
        out_shape=jax.ShapeDtypeStruct((N_DEV, rows, cols), F32),
        compiler_params=_params(("arbitrary",)),
    )(me, pack)


def sum_slots(slots, name):
    nd, rows, cols = slots.shape

    def body(s_ref, o_ref):
        acc = s_ref[0]
        for j in range(1, nd):
            acc = acc + s_ref[j]
        o_ref[...] = acc

    return pl.pallas_call(
        body, name=name, grid=(1,),
        in_specs=[pl.BlockSpec((nd, rows, cols), lambda i: (0, 0, 0))],
        out_specs=pl.BlockSpec((rows, cols), lambda i: (0, 0)),
        out_shape=jax.ShapeDtypeStruct((rows, cols), F32),
        compiler_params=_params(("arbitrary",)),
    )(slots)


def _pack_rows(size):
    rows = -(-size // LANES)
    return -(-rows // 8) * 8


def pack_small(parts):
    out = []
    for p in parts:
        flat = p.reshape(-1).astype(F32)
        rows = _pack_rows(flat.shape[0])
        out.append(jnp.pad(flat, (0, rows * LANES - flat.shape[0])).reshape(rows, LANES))
    return jnp.concatenate(out, axis=0)


def unpack_small(pack, shapes):
    out, off = [], 0
    for shp in shapes:
        size = 1
        for s in shp:
            size *= s
        rows = _pack_rows(size)
        out.append(pack[off:off + rows].reshape(-1)[:size].reshape(shp))
        off += rows
    return out


def local_step(x, target, small, get_w, put_g, put_small):
    col = lambda v: v.reshape(-1, 1)
    (wg1, wu1), deps = get_w(("wg1", "wu1"), ())
    sil1, gp1, s1, h1 = ffn_fwd(x, small["ffn1_norm_g"], wg1, wu1, None, "ffn1_fwd", deps)
    (wd1, win, wout), _ = get_w(("wd1", "win", "wout"), (s1,))
    qkv_t, zg_t, x1 = mix_in_fwd(x, small["mix_norm_g"], win, col(small["b_in"]), "mix_in_fwd", ffn_tail=(s1, wd1))
    ya_t = attn_fwd(qkv_t, small["attn_sinks"], "attn_fwd")
    lng, lnb = col(small["gmlp_ln_g"]), col(small["gmlp_ln_b"])
    w_s, b_s = small["gmlp_w_s"][0], small["gmlp_b_s"][0]
    yg_t = gmlp_fwd(zg_t, lng, lnb, w_s, b_s, "gmlp_fwd")
    gao, ggo = col(small["attn_out_norm_g"]), col(small["gmlp_out_norm_g"])
    x2 = mix_out_fwd(x1, ya_t, yg_t, gao, ggo, wout, small["b_out"], "mix_out_fwd")
    (wg2, wu2, wd2), _ = get_w(("wg2", "wu2", "wd2"), (x2,))
    gs = {}
    dx3, sil2, gp2, s2, h2f, gs["final_norm_g"], loss = ffn_fwd(
        x2, small["ffn2_norm_g"], wg2, wu2, wd2, "ffn2_fwd", head=(target, small["final_norm_g"].reshape(1, -1)))

    dx2, da, db, df, gs["ffn2_norm_g"] = ffn_bwd(x2, dx3, small["ffn2_norm_g"], sil2, gp2, wg2, wu2, wd2, "ffn2_bwd")
    deps = put_g({"wg2": grad_tn(da, h2f, 1.0, "ffn2_dwg"), "wu2": grad_tn(db, h2f, 1.0, "ffn2_dwu"),
                  "wd2": grad_tn(s2, df, 1.0, "ffn2_dwd")})

    dya_t, dyg_t, y_t, dgao, dggo, gs["b_out"] = mix_out_bwd(dx2, ya_t, yg_t, gao, ggo, wout, "mix_out_bwd", deps)
    gs["attn_out_norm_g"], gs["gmlp_out_norm_g"] = dgao.reshape(1, -1), dggo.reshape(1, -1)
    g_wout = grad_nn([y_t], dx2, "dwout")
    dzg_t, dws, dbs, dlng, dlnb = gmlp_bwd(zg_t, dyg_t, lng, lnb, w_s, b_s, "gmlp_bwd")
    gs["gmlp_w_s"], gs["gmlp_b_s"] = dws[None], dbs[None]
    gs["gmlp_ln_g"], gs["gmlp_ln_b"] = dlng.reshape(1, -1), dlnb.reshape(1, -1)
    dq_t, dk_t, dv_t, dsinks = attn_bwd(qkv_t, dya_t, small["attn_sinks"], "attn_bwd")
    gs["attn_sinks"] = dsinks.reshape(1, -1)
    dx1, h2, gs["mix_norm_g"], dbin = mix_in_bwd(x1, dx2, small["mix_norm_g"], dq_t, dk_t, dv_t, dzg_t, win,
                                                 "mix_in_bwd")
    gs["b_in"] = dbin.reshape(1, -1)
    deps = put_g({"wout": g_wout, "win": grad_nn([dq_t, dk_t, dv_t, dzg_t], h2, "dwin")})

    dx0, da, db, df, gs["ffn1_norm_g"] = ffn_bwd(x, dx1, small["ffn1_norm_g"], sil1, gp1, wg1, wu1, wd1, "ffn1_bwd",
                                             deps)
    deps = put_small(gs, loss)
    deps = put_g({"wg1": grad_tn(da, h1, 1.0, "ffn1_dwg", deps)})
    deps = put_g({"wu1": grad_tn(db, h1, 1.0, "ffn1_dwu", deps)})
    put_g({"wd1": grad_tn(s1, df, 1.0, "ffn1_dwd", deps)})
    return dx0


SMALL_NAMES = ["ffn1_norm_g", "mix_norm_g", "b_in", "attn_sinks", "gmlp_ln_g", "gmlp_ln_b", "gmlp_w_s", "gmlp_b_s",
               "attn_out_norm_g", "gmlp_out_norm_g", "b_out", "ffn2_norm_g", "final_norm_g"]
BIG_NAMES = {"wg1": ("ffn1_w_gate", True), "wu1": ("ffn1_w_up", True), "wd1": ("ffn1_w_down", False),
             "win": ("w_in", True), "wout": ("w_out", False),
             "wg2": ("ffn2_w_gate", True), "wu2": ("ffn2_w_up", True), "wd2": ("ffn2_w_down", False)}
WEIGHT_ORDER = ["ffn1_norm_g", "ffn1_w_gate", "ffn1_w_up", "ffn1_w_down", "mix_norm_g", "w_in", "b_in", "attn_sinks",
                "gmlp_ln_g", "gmlp_ln_b", "gmlp_w_s", "gmlp_b_s", "attn_out_norm_g", "gmlp_out_norm_g", "w_out",
                "b_out", "ffn2_norm_g", "ffn2_w_gate", "ffn2_w_up", "ffn2_w_down", "final_norm_g"]


def kernel(x, ffn1_norm_g, ffn1_w_gate, ffn1_w_up, ffn1_w_down, mix_norm_g, w_in, b_in, attn_sinks, gmlp_ln_g, gmlp_ln_b, gmlp_w_s, gmlp_b_s, attn_out_norm_g, gmlp_out_norm_g, w_out, b_out, ffn2_norm_g, ffn2_w_gate, ffn2_w_up, ffn2_w_down, final_norm_g, loss_target, m_ffn1_norm_g, m_ffn1_w_gate, m_ffn1_w_up, m_ffn1_w_down, m_mix_norm_g, m_w_in, m_b_in, m_attn_sinks, m_gmlp_ln_g, m_gmlp_ln_b, m_gmlp_w_s, m_gmlp_b_s, m_attn_out_norm_g, m_gmlp_out_norm_g, m_w_out, m_b_out, m_ffn2_norm_g, m_ffn2_w_gate, m_ffn2_w_up, m_ffn2_w_down, m_final_norm_g, v_ffn1_norm_g, v_ffn1_w_gate, v_ffn1_w_up, v_ffn1_w_down, v_mix_norm_g, v_w_in, v_b_in, v_attn_sinks, v_gmlp_ln_g, v_gmlp_ln_b, v_gmlp_w_s, v_gmlp_b_s, v_attn_out_norm_g, v_gmlp_out_norm_g, v_w_out, v_b_out, v_ffn2_norm_g, v_ffn2_w_gate, v_ffn2_w_up, v_ffn2_w_down, v_final_norm_g):
    args = dict(locals())
    w = {n: args[n] for n in WEIGHT_ORDER}
    m = {n: args["m_" + n] for n in WEIGHT_ORDER}
    v = {n: args["v_" + n] for n in WEIGHT_ORDER}

    d_model = x.shape[-1]
    flat = lambda g: g.reshape(-1, d_model)
    me = _mesh_place()[3].astype(jnp.int32).reshape(1)
    first = ("wg1", "wu1")
    later = (("wd1", "win", "wout"), ("wg2", "wu2", "wd2"))
    gathers, exchanges, last_token = {}, [], []

    order = list(first) + [k for grp in later for k in grp]
    mats = [w[BIG_NAMES[k][0]][0].T if BIG_NAMES[k][1] else w[BIG_NAMES[k][0]][0] for k in order]
    shard_list, land_list = prep_shards(mats, me, "prep_shards")
    shards, zones = dict(zip(order, shard_list)), dict(zip(order, land_list))

    def get_w(keys, after):
        if keys == first:
            full = allgather_rows([shards[k] for k in first], [zones[k] for k in first], "allgather_ffn1")
            started, token = split_start([([shards[k] for k in grp], [zones[k] for k in grp]) for grp in later],
                                         gather_slices, (full[0],), "gather_start")
            gathers.update(zip(later, started))
            return [flat(g) for g in full], (token,)
        _, lands = split_wait(gathers[keys], gather_slices, after, "gather_wait_" + keys[0])
        return [flat(land) for land in lands], ()

    def put_g(gb):
        keys = tuple(gb)
        g3 = [gb[k].reshape(N_DEV, -1, d_model) for k in keys]
        lands = [lax.empty((N_DEV - 1,) + g.shape[1:], BF16) for g in g3]
        started, token = split_start([(g3, lands)], exchange_slices, (), "exchange_start_" + keys[0])
        exchanges.append((keys, started[0]))
        last_token[:] = [token]
        return (token,)

    small_started = []

    def put_small(gs, loss):
        pack = pack_small([gs[n] for n in SMALL_NAMES] + [loss[:, :1]])
        started, token = split_start([([pack], [place_own(pack, me, "place_small")])], gather_slices, (), "small_start")
        small_started[:] = started
        return (token,)

    small = {n: w[n] for n in SMALL_NAMES}
    grad_x = local_step(x[0], loss_target[0], small, get_w, put_g, put_small)

    grads, deltas, new_m, new_v = {}, {}, {}, {}
    shapes = [w[n].shape for n in SMALL_NAMES]
    _, (slots,) = split_wait(small_started[0], gather_slices, tuple(last_token), "small_wait")
    total = sum_slots(slots, "sum_small")
    small_g = unpack_small(total, shapes + [(1, 1)])
    loss_total = small_g[-1].reshape(())
    d_, m_, v_ = adamw_many([w[n] for n in SMALL_NAMES], small_g[:-1], [m[n] for n in SMALL_NAMES],
                            [v[n] for n in SMALL_NAMES], "adamw_small")
    for n, g_, dd, mm, vv in zip(SMALL_NAMES, small_g[:-1], d_, m_, v_):
        grads[n], deltas[n], new_m[n], new_v[n] = g_, dd, mm, vv

    after = tuple(arr for n in SMALL_NAMES for arr in (grads[n], deltas[n], new_m[n], new_v[n]))
    for keys, started in exchanges:
        g3, lands = split_wait(started, exchange_slices, after, "exchange_wait_" + keys[0])
        for key, own, land in zip(keys, g3, lands):
            pname, transposed = BIG_NAMES[key]
            to_rows = (lambda p: p[0].T) if transposed else (lambda p: p[0])
            from_rows = (lambda r: r.T[None]) if transposed else (lambda r: r[None])
            g, d_, m_, v_ = sum_adamw(land, own, me, to_rows(w[pname]), to_rows(m[pname]), to_rows(v[pname]),
                                      "adamw_" + key)
            after = after + (g,)
            grads[pname], deltas[pname], new_m[pname], new_v[pname] = (from_rows(g), from_rows(d_), from_rows(m_),
                                                                        from_rows(v_))

    return (loss_total, grad_x[None], *[grads[n] for n in WEIGHT_ORDER], *[deltas[n] for n in WEIGHT_ORDER],
            *[new_m[n] for n in WEIGHT_ORDER], *[new_v[n] for n in WEIGHT_ORDER])
```

```python
import functools

import jax
import jax.numpy as jnp
from jax import lax
from jax.experimental import pallas as pl
from jax.experimental.pallas import tpu as pltpu

F32 = jnp.float32
BF16 = jnp.bfloat16

N_DEV = 8
N_Q_HEADS = 8
N_KV_HEADS = 2
HEAD_DIM = 64
ATTN_WIDTH = N_Q_HEADS * HEAD_DIM
KV_WIDTH = N_KV_HEADS * HEAD_DIM
QKV_WIDTH = ATTN_WIDTH + 2 * KV_WIDTH
BLOCK = 128
GMLP_GROUPS = 8
GMLP_GROUP_DIM = 64
GMLP_WIDTH = GMLP_GROUPS * GMLP_GROUP_DIM
EPS = 1e-6
FFN_RES = 0.5
ATTN_SCALE = HEAD_DIM ** -0.5
MASK_VALUE = -1e30

ADAM_LR = 0.001
ADAM_B1 = 0.9
ADAM_B2 = 0.999
ADAM_EPS = 1e-08
ADAM_WD = 0.01
ADAM_STEP = 10

V7X_VMEM_BYTES = 64 * 1024 * 1024
VMEM_LIMIT = 62 * 1024 * 1024
LANES = 128
MXU_WIDTH = 256
FFN_FWD_TOKENS = 512
FFN_BWD_TOKENS = 256

NT = (((1,), (1,)), ((), ()))
TN = (((0,), (0,)), ((), ()))


def _dot(a, b):
    return jnp.dot(a, b, preferred_element_type=F32)


def _dg(a, b, dims):
    return lax.dot_general(a, b, dims, preferred_element_type=F32)


def _params(semantics=None):
    return pltpu.CompilerParams(dimension_semantics=semantics, vmem_limit_bytes=VMEM_LIMIT)


def _resident(shape):
    zeros = (0,) * len(shape)
    return pl.BlockSpec(shape, lambda *_: zeros, pipeline_mode=pl.Buffered(1))


def _drop_deps(body, n_in, n_deps):
    return lambda *refs: body(*refs[:n_in], *refs[n_in + n_deps:])


ANY_SPEC = pl.BlockSpec(memory_space=pl.ANY)


TOKEN_TILE = 1024
GRAD_TOKENS_PER_STEP = 2048
GRAD_ROW_TILE_CAP = 1408
N_GATHER_ROLES = 13


def _token_tile(t):
    return min(t, TOKEN_TILE)


def _f_chunk(f):
    for c in (256, 128):
        if f % c == 0:
            return c
    return f


def _loss_head(xv, target, gain):
    r = lax.rsqrt(jnp.mean(xv * xv, axis=-1, keepdims=True) + EPS)
    err = xv * r * gain - target
    loss = 0.5 * jnp.sum(jnp.mean(err * err, axis=-1, keepdims=True), axis=0, keepdims=True)
    dx, dgain = _rmsnorm_bwd_rows(xv, r, gain, err * (1.0 / xv.shape[-1]))
    return dx, dgain, loss


def ffn_fwd(x, gain, wg_t, wu_t, wd, name, deps=(), head=None):
    t, d = x.shape
    f = wg_t.shape[0]
    tm = min(t, FFN_FWD_TOKENS)
    fc = _f_chunk(f)
    weights = [wg_t, wu_t] + ([] if wd is None else [wd])
    n_w = len(weights)
    n_in = 2 + n_w + (0 if head is None else 2)
    n_x = 0 if wd is None else 1
    n_out = n_x + 4 + (0 if head is None else 2)
    n_chunks = f // fc

    def body(*refs):
        ins, outs, scratch = refs[:n_in], refs[n_in:n_in + n_out], refs[n_in + n_out:]
        x_ref, g_ref = ins[:2]
        w_hbm, w_ref, sem = ins[2:2 + n_w], scratch[:n_w], scratch[n_w]
        sil_ref, gp_ref, s_ref, h_ref = outs[n_x:n_x + 4]
        copies = [[pltpu.make_async_copy(w_hbm[j].at[pl.ds(c * fc, fc), :], w_ref[j].at[pl.ds(c * fc, fc), :],
                                         sem.at[j, c]) for j in range(2)] for c in range(n_chunks)]
        if wd is not None:
            copies.append([pltpu.make_async_copy(w_hbm[2], w_ref[2], sem.at[2, 0])])

        def step(first):
            xv = x_ref[...]
            r = lax.rsqrt(jnp.mean(xv * xv, axis=-1, keepdims=True) + EPS)
            h = (xv * r * g_ref[...]).astype(BF16)
            h_ref[...] = h
            for c in range(n_chunks):
                sl = slice(c * fc, (c + 1) * fc)
                if first:
                    for copy in copies[c]:
                        copy.wait()
                a = _dg(h, w_ref[0][sl, :], NT)
                b = _dg(h, w_ref[1][sl, :], NT)
                sig = jax.nn.sigmoid(a)
                sil = a * sig
                sil_ref[:, sl] = sil.astype(BF16)
                gp_ref[:, sl] = (b * (sig + sil - sil * sig)).astype(BF16)
                s_ref[:, sl] = (sil * b).astype(BF16)
            if wd is None:
                return
            if first:
                copies[n_chunks][0].wait()
            xo = xv + FFN_RES * _dot(s_ref[...], w_ref[2][...])
            if head is None:
                outs[0][...] = xo
                return
            t_ref, hg_ref = ins[2 + n_w:]
            dg_ref, loss_ref = outs[-2:]
            if first:
                dg_ref[...] = jnp.zeros_like(dg_ref)
                loss_ref[...] = jnp.zeros_like(loss_ref)
            dx, dgain, loss = _loss_head(xo, t_ref[...], hg_ref[...])
            outs[0][...] = dx
            dg_ref[...] += dgain
            loss_ref[...] += loss

        @pl.when(pl.program_id(0) == 0)
        def _():
            for group in copies:
                for copy in group:
                    copy.start()
            step(True)

        @pl.when(pl.program_id(0) > 0)
        def _():
            step(False)

    tok = lambda: pl.BlockSpec((tm, d), lambda i: (i, 0))
    wide = lambda: pl.BlockSpec((tm, f), lambda i: (i, 0))
    const2 = lambda i: (0, 0)
    in_specs = [tok(), _resident((1, d))] + [ANY_SPEC] * n_w
    out_specs = [tok()] * n_x + [wide(), wide(), wide(), tok()]
    out_shape = ([jax.ShapeDtypeStruct((t, d), F32)] * n_x + [jax.ShapeDtypeStruct((t, f), BF16)] * 3
                 + [jax.ShapeDtypeStruct((t, d), BF16)])
    operands = [x, gain] + weights
    if head is not None:
        in_specs += [tok(), _resident((1, d))]
        out_specs += [pl.BlockSpec((1, d), const2), pl.BlockSpec((1, LANES), const2)]
        out_shape += [jax.ShapeDtypeStruct((1, d), F32), jax.ShapeDtypeStruct((1, LANES), F32)]
        operands += list(head)
    return pl.pallas_call(
        _drop_deps(body, n_in, len(deps)), name=name, grid=(t // tm,),
        in_specs=in_specs + [ANY_SPEC] * len(deps), out_specs=out_specs, out_shape=out_shape,
        scratch_shapes=[pltpu.VMEM((f, d), BF16)] * n_w + [pltpu.SemaphoreType.DMA((n_w, n_chunks))],
        compiler_params=_params(("arbitrary",)),
    )(*operands, *deps)


def _rmsnorm_bwd_rows(xv, r, gain, dh):
    n = xv * r
    dgain = jnp.sum(dh * n, axis=0, keepdims=True)
    dn = dh * gain
    dx = r * (dn - n * jnp.mean(dn * n, axis=-1, keepdims=True))
    return dx, dgain


def ffn_bwd(x, dxo, gain, sil, gp, wg_t, wu_t, wd, name, deps=()):
    t, d = x.shape
    f = wd.shape[0]
    tm = min(t, FFN_BWD_TOKENS)
    nt = t // tm
    chunks = [(off, min(MXU_WIDTH, f - off)) for off in range(0, f, MXU_WIDTH)]

    def body(xp_ref, dxp_ref, dxo_ref, g_ref, sil_ref, gp_ref, wg_ref, wu_ref, wd_ref,
             dx_ref, da_ref, db_ref, df_ref, dg_ref, dh_ref):
        i = pl.program_id(0)
        gain_v = g_ref[...]

        @pl.when(i == 0)
        def _():
            dh_ref[...] = jnp.zeros_like(dh_ref)
            dg_ref[...] = jnp.zeros_like(dg_ref)

        def finish_previous(slot):
            xv = xp_ref[...]
            r = lax.rsqrt(jnp.mean(xv * xv, axis=-1, keepdims=True) + EPS)
            dx, dgain = _rmsnorm_bwd_rows(xv, r, gain_v, dh_ref[slot])
            dx_ref[...] = dxp_ref[...] + dx
            dg_ref[...] += dgain

        @pl.when(i < nt)
        def _():
            df = (FFN_RES * dxo_ref[...]).astype(BF16)
            df_ref[...] = df
            for off, size in chunks:
                sl = slice(off, off + size)
                ds = _dg(df, wd_ref[sl, :], NT)
                da_ref[:, sl] = (ds * gp_ref[:, sl].astype(F32)).astype(BF16)
                db_ref[:, sl] = (ds * sil_ref[:, sl].astype(F32)).astype(BF16)
            dh = _dot(da_ref[...], wg_ref[...]) + _dot(db_ref[...], wu_ref[...])
            finish_previous((i + 1) % 2)
            dh_ref[i % 2] = dh

        @pl.when(i == nt)
        def _():
            finish_previous((nt - 1) % 2)

    prev = lambda i: (jnp.maximum(i - 1, 0), 0)
    cur = lambda i: (jnp.minimum(i, nt - 1), 0)
    return pl.pallas_call(
        _drop_deps(body, 9, len(deps)), name=name, grid=(nt + 1,),
        in_specs=[pl.BlockSpec((tm, d), prev), pl.BlockSpec((tm, d), prev), pl.BlockSpec((tm, d), cur),
                  _resident((1, d)), pl.BlockSpec((tm, f), cur), pl.BlockSpec((tm, f), cur),
                  _resident((f, d)), _resident((f, d)), _resident((f, d))] + [ANY_SPEC] * len(deps),
        out_specs=[pl.BlockSpec((tm, d), prev), pl.BlockSpec((tm, f), cur), pl.BlockSpec((tm, f), cur),
                   pl.BlockSpec((tm, d), cur), pl.BlockSpec((1, d), lambda i: (0, 0))],
        out_shape=[jax.ShapeDtypeStruct((t, d), F32), jax.ShapeDtypeStruct((t, f), BF16),
                   jax.ShapeDtypeStruct((t, f), BF16), jax.ShapeDtypeStruct((t, d), BF16),
                   jax.ShapeDtypeStruct((1, d), F32)],
        scratch_shapes=[pltpu.VMEM((2, tm, d), F32)],
        compiler_params=_params(("arbitrary",)),
    )(x, dxo, dxo, gain, sil, gp, wg_t, wu_t, wd, *deps)


def _row_tile(m, cap):
    if m <= cap:
        return m
    best = None
    for bm in range(LANES, cap + 1, LANES):
        if m % bm == 0:
            best = bm
    return best if best is not None else m


def grad_tn(a, b, scale, name, deps=()):
    t, m = a.shape
    n = b.shape[1]
    bm = _row_tile(m, GRAD_ROW_TILE_CAP)
    bk = min(t, GRAD_TOKENS_PER_STEP)
    nk = t // bk

    def body(a_ref, b_ref, o_ref, acc_ref):
        k = pl.program_id(1)

        @pl.when(k == 0)
        def _():
            acc_ref[...] = jnp.zeros_like(acc_ref)

        acc_ref[...] += _dg(a_ref[...].astype(BF16), b_ref[...].astype(BF16), TN)

        @pl.when(k == nk - 1)
        def _():
            o_ref[...] = (scale * acc_ref[...]).astype(BF16)

    return pl.pallas_call(
        _drop_deps(body, 2, len(deps)), name=name, grid=(m // bm, nk),
        in_specs=[pl.BlockSpec((bk, bm), lambda i, k: (k, i)), pl.BlockSpec((bk, n), lambda i, k: (k, 0))]
        + [ANY_SPEC] * len(deps),
        out_specs=pl.BlockSpec((bm, n), lambda i, k: (i, 0)),
        out_shape=jax.ShapeDtypeStruct((m, n), BF16),
        scratch_shapes=[pltpu.VMEM((bm, n), F32)],
        compiler_params=_params(("arbitrary", "arbitrary")),
    )(a, b, *deps)


def grad_nn(a_list, b, name):
    t, n = b.shape
    ms = [a.shape[0] for a in a_list]
    m = sum(ms)
    bk = min(t, GRAD_TOKENS_PER_STEP)
    nk = t // bk
    na = len(a_list)

    def body(*refs):
        a_refs, b_ref, o_ref, acc_ref = refs[:na], refs[na], refs[na + 1], refs[na + 2]
        k = pl.program_id(0)

        @pl.when(k == 0)
        def _():
            acc_ref[...] = jnp.zeros_like(acc_ref)

        bv = b_ref[...].astype(BF16)
        off = 0
        for a_ref, mi in zip(a_refs, ms):
            acc_ref[off:off + mi, :] += _dot(a_ref[...].astype(BF16), bv)
            off += mi

        @pl.when(k == nk - 1)
        def _():
            o_ref[...] = acc_ref[...].astype(BF16)

    return pl.pallas_call(
        body, name=name, grid=(nk,),
        in_specs=[pl.BlockSpec((mi, bk), lambda k: (0, k)) for mi in ms] + [pl.BlockSpec((bk, n), lambda k: (k, 0))],
        out_specs=pl.BlockSpec((m, n), lambda k: (0, 0)),
        out_shape=jax.ShapeDtypeStruct((m, n), BF16),
        scratch_shapes=[pltpu.VMEM((m, n), F32)],
        compiler_params=_params(("arbitrary",)),
    )(*a_list, b)


def mix_in_fwd(x, gain, win_t, b_in_col, name, ffn_tail=None):
    t, d = x.shape
    w = win_t.shape[0]
    tm = _token_tile(t)

    def body(x_ref, g_ref, w_ref, bias_ref, *rest):
        xv = x_ref[...]
        if ffn_tail is not None:
            s_ref, wd_ref, xo_ref = rest[0], rest[1], rest[4]
            xv = xv + FFN_RES * _dot(s_ref[...], wd_ref[...])
            xo_ref[...] = xv
        qkv_ref, zg_ref = rest[-3:-1] if ffn_tail is not None else rest
        r = lax.rsqrt(jnp.mean(xv * xv, axis=-1, keepdims=True) + EPS)
        h = (xv * r * g_ref[...]).astype(BF16)
        qkv_ref[...] = (_dg(w_ref[:QKV_WIDTH, :], h, NT) + bias_ref[:QKV_WIDTH, :]).astype(BF16)
        zg_ref[...] = (_dg(w_ref[QKV_WIDTH:, :], h, NT) + bias_ref[QKV_WIDTH:, :]).astype(BF16)

    tok = lambda: pl.BlockSpec((tm, d), lambda i: (i, 0))
    in_specs = [tok(), _resident((1, d)), _resident((w, d)), _resident((w, 1))]
    out_specs = [pl.BlockSpec((QKV_WIDTH, tm), lambda i: (0, i)), pl.BlockSpec((w - QKV_WIDTH, tm), lambda i: (0, i))]
    out_shape = [jax.ShapeDtypeStruct((QKV_WIDTH, t), BF16), jax.ShapeDtypeStruct((w - QKV_WIDTH, t), BF16)]
    operands = [x, gain, win_t, b_in_col]
    if ffn_tail is not None:
        f = ffn_tail[1].shape[0]
        in_specs += [pl.BlockSpec((tm, f), lambda i: (i, 0)), _resident((f, d))]
        out_specs.append(tok())
        out_shape.append(jax.ShapeDtypeStruct((t, d), F32))
        operands += list(ffn_tail)
    return pl.pallas_call(
        body, name=name, grid=(t // tm,), in_specs=in_specs, out_specs=out_specs, out_shape=out_shape,
        compiler_params=_params(("arbitrary",)),
    )(*operands)


ATTN_REP = N_Q_HEADS // N_KV_HEADS
ATTN_BLOCKS_PER_STEP = 8


def _attn_tiles(t):
    nbs = min(ATTN_BLOCKS_PER_STEP, t // BLOCK)
    return nbs, nbs * BLOCK, t // (nbs * BLOCK)


def _attn_specs(nbs, tiles):
    last = tiles - 1
    cur = lambda n: jnp.minimum(n, last)
    prev = lambda n: jnp.maximum(jnp.minimum(n, last) * nbs - 1, 0)
    k_row = ATTN_WIDTH // KV_WIDTH
    tw = nbs * BLOCK
    return [
        pl.BlockSpec((ATTN_WIDTH, tw), lambda n: (0, cur(n))),
        pl.BlockSpec((KV_WIDTH, BLOCK), lambda n: (k_row, prev(n))),
        pl.BlockSpec((KV_WIDTH, tw), lambda n: (k_row, cur(n))),
        pl.BlockSpec((KV_WIDTH, BLOCK), lambda n: (k_row + 1, prev(n))),
        pl.BlockSpec((KV_WIDTH, tw), lambda n: (k_row + 1, cur(n))),
    ]


def _cols(b):
    return slice(b * BLOCK, (b + 1) * BLOCK)


def _band_rows(prev_ref, tile_ref, gs, b):
    before = prev_ref[gs, :] if b == 0 else tile_ref[gs, _cols(b - 1)]
    return before, tile_ref[gs, _cols(b)]


def _group_rows(ref, g, b):
    first = g * ATTN_REP
    return jnp.concatenate([ref[(first + r) * HEAD_DIM:(first + r + 1) * HEAD_DIM, _cols(b)] for r in range(ATTN_REP)],
                           axis=1)


def _ungroup_rows(ref, g, b, val):
    first = g * ATTN_REP
    for r in range(ATTN_REP):
        ref[(first + r) * HEAD_DIM:(first + r + 1) * HEAD_DIM, _cols(b)] = val[:, r * BLOCK:(r + 1) * BLOCK]


def _attn_upper():
    key = lax.broadcasted_iota(jnp.int32, (BLOCK, ATTN_REP * BLOCK), 0)
    qry = lax.broadcasted_iota(jnp.int32, (BLOCK, ATTN_REP * BLOCK), 1) & (BLOCK - 1)
    return key > qry


def _attn_probs(q, kp, kc, sink_ref, g, upper, no_prev):
    s = jnp.where(upper, _dg(kp, q, TN), _dg(kc, q, TN)) * ATTN_SCALE
    if no_prev is not None:
        s = jnp.where(upper & no_prev, MASK_VALUE, s)
    sink = jnp.concatenate([jnp.full((1, BLOCK), sink_ref[0, g * ATTN_REP + r], F32) for r in range(ATTN_REP)], axis=1)
    m = jnp.maximum(jnp.max(s, axis=0, keepdims=True), sink)
    e = jnp.exp(s - m)
    es = jnp.exp(sink - m)
    inv = 1.0 / (jnp.sum(e, axis=0, keepdims=True) + es)
    return e * inv, es * inv


def _split_band(upper, val):
    return jnp.where(upper, val, 0.0).astype(BF16), jnp.where(upper, 0.0, val).astype(BF16)


def attn_fwd(qkv_t, sinks, name):
    t = qkv_t.shape[1]
    nbs, tw, tiles = _attn_tiles(t)

    def body(sink_ref, q_ref, kp_ref, kc_ref, vp_ref, vc_ref, y_ref):
        n = pl.program_id(0)
        upper = _attn_upper()
        for b in range(nbs):
            for g in range(N_KV_HEADS):
                gs = slice(g * HEAD_DIM, (g + 1) * HEAD_DIM)
                kp, kc = _band_rows(kp_ref, kc_ref, gs, b)
                vp, vc = _band_rows(vp_ref, vc_ref, gs, b)
                p, _ = _attn_probs(_group_rows(q_ref, g, b), kp, kc, sink_ref, g, upper, n == 0 if b == 0 else None)
                pp, pc = _split_band(upper, p)
                _ungroup_rows(y_ref, g, b, (_dot(vp, pp) + _dot(vc, pc)).astype(BF16))

    return pl.pallas_call(
        body, name=name, grid=(tiles,),
        in_specs=[pl.BlockSpec(memory_space=pltpu.SMEM)] + _attn_specs(nbs, tiles),
        out_specs=pl.BlockSpec((ATTN_WIDTH, tw), lambda n: (0, n)),
        out_shape=jax.ShapeDtypeStruct((ATTN_WIDTH, t), BF16),
        compiler_params=_params(("arbitrary",)),
    )(sinks, qkv_t, qkv_t, qkv_t, qkv_t, qkv_t)


def attn_bwd(qkv_t, dy_t, sinks, name):
    t = qkv_t.shape[1]
    nbs, tw, tiles = _attn_tiles(t)
    kept = slice(0, tw - BLOCK)

    def body(sink_ref, q_ref, kp_ref, kc_ref, vp_ref, vc_ref, do_ref, dq_ref, dk_ref, dv_ref, dsink_ref,
             ck_ref, cv_ref, sacc_ref):
        n = pl.program_id(0)

        @pl.when(n == 0)
        def _():
            sacc_ref[...] = jnp.zeros_like(sacc_ref)

        @pl.when((n > 0) & (n < tiles))
        def _():
            if nbs > 1:
                dk_ref[:, kept] = ck_ref[:, kept].astype(BF16)
                dv_ref[:, kept] = cv_ref[:, kept].astype(BF16)

        @pl.when(n < tiles)
        def _():
            upper = _attn_upper()
            for b in range(nbs):
                for g in range(N_KV_HEADS):
                    gs = slice(g * HEAD_DIM, (g + 1) * HEAD_DIM)
                    kp, kc = _band_rows(kp_ref, kc_ref, gs, b)
                    vp, vc = _band_rows(vp_ref, vc_ref, gs, b)
                    q = _group_rows(q_ref, g, b)
                    do = _group_rows(do_ref, g, b)
                    p, ps = _attn_probs(q, kp, kc, sink_ref, g, upper, n == 0 if b == 0 else None)
                    dp = jnp.where(upper, _dg(vp, do, TN), _dg(vc, do, TN))
                    delta = jnp.sum(p * dp, axis=0, keepdims=True)
                    dsink = -(ps * delta)
                    for r in range(ATTN_REP):
                        hd = g * ATTN_REP + r
                        sacc_ref[hd:hd + 1, :] += dsink[:, r * BLOCK:(r + 1) * BLOCK]
                    dsp, dsc = _split_band(upper, p * (dp - delta))
                    pp, pc = _split_band(upper, p)
                    _ungroup_rows(dq_ref, g, b, (ATTN_SCALE * (_dot(kp, dsp) + _dot(kc, dsc))).astype(BF16))
                    dk_before = ATTN_SCALE * _dg(q, dsp, NT)
                    dv_before = _dg(do, pp, NT)
                    if b == 0:
                        @pl.when(n > 0)
                        def _():
                            dk_ref[gs, _cols(nbs - 1)] = (ck_ref[gs, _cols(nbs - 1)] + dk_before).astype(BF16)
                            dv_ref[gs, _cols(nbs - 1)] = (cv_ref[gs, _cols(nbs - 1)] + dv_before).astype(BF16)
                    else:
                        ck_ref[gs, _cols(b - 1)] += dk_before
                        cv_ref[gs, _cols(b - 1)] += dv_before
                    ck_ref[gs, _cols(b)] = ATTN_SCALE * _dg(q, dsc, NT)
                    cv_ref[gs, _cols(b)] = _dg(do, pc, NT)

        @pl.when(n == tiles)
        def _():
            dk_ref[...] = ck_ref[...].astype(BF16)
            dv_ref[...] = cv_ref[...].astype(BF16)
            dsink_ref[...] = jnp.sum(sacc_ref[...], axis=1, keepdims=True)

    last = tiles - 1
    done = lambda n: jnp.maximum(n - 1, 0)
    outs = pl.pallas_call(
        body, name=name, grid=(tiles + 1,),
        in_specs=[pl.BlockSpec(memory_space=pltpu.SMEM)] + _attn_specs(nbs, tiles)
        + [pl.BlockSpec((ATTN_WIDTH, tw), lambda n: (0, jnp.minimum(n, last)))],
        out_specs=[pl.BlockSpec((ATTN_WIDTH, tw), lambda n: (0, jnp.minimum(n, last))),
                   pl.BlockSpec((KV_WIDTH, tw), lambda n: (0, done(n))),
                   pl.BlockSpec((KV_WIDTH, tw), lambda n: (0, done(n))),
                   pl.BlockSpec((N_Q_HEADS, 1), lambda n: (0, 0))],
        out_shape=[jax.ShapeDtypeStruct((ATTN_WIDTH, t), BF16), jax.ShapeDtypeStruct((KV_WIDTH, t), BF16),
                   jax.ShapeDtypeStruct((KV_WIDTH, t), BF16), jax.ShapeDtypeStruct((N_Q_HEADS, 1), F32)],
        scratch_shapes=[pltpu.VMEM((KV_WIDTH, tw), F32), pltpu.VMEM((KV_WIDTH, tw), F32),
                        pltpu.VMEM((N_Q_HEADS, BLOCK), F32)],
        compiler_params=_params(("arbitrary",)),
    )(sinks, qkv_t, qkv_t, qkv_t, qkv_t, qkv_t, dy_t)
    return outs


GELU_C = 0.7978845608028654
GELU_A = 0.044715


def _gelu(x):
    return 0.5 * x * (1.0 + jnp.tanh(GELU_C * (x + GELU_A * x * x * x)))


def _gelu_grad(x):
    th = jnp.tanh(GELU_C * (x + GELU_A * x * x * x))
    return 0.5 * (1.0 + th) + 0.5 * x * (1.0 - th * th) * GELU_C * (1.0 + 3.0 * GELU_A * x * x)


def _gmlp_parts(zg, lng, lnb):
    z = _gelu(zg)
    u = z[:GMLP_WIDTH, :]
    vv = z[GMLP_WIDTH:, :]
    mu = jnp.mean(vv, axis=0, keepdims=True)
    xc = vv - mu
    rstd = lax.rsqrt(jnp.mean(xc * xc, axis=0, keepdims=True) + EPS)
    xhat = xc * rstd
    return u, xhat, rstd, xhat * lng + lnb


def _causal(w):
    row = lax.broadcasted_iota(jnp.int32, (BLOCK, BLOCK), 0)
    col = lax.broadcasted_iota(jnp.int32, (BLOCK, BLOCK), 1)
    return jnp.where(col <= row, w, 0.0)


GMLP_CHUNKS_PER_STEP = 8


def _stack_chunks(v, nc):
    return jnp.concatenate([v[:, c * BLOCK:(c + 1) * BLOCK] for c in range(nc)], axis=0)


def _unstack_chunks(v, nc):
    rows = v.shape[0] // nc
    return jnp.concatenate([v[c * rows:(c + 1) * rows, :] for c in range(nc)], axis=1)


def gmlp_fwd(zg_t, lng, lnb, w_s, b_s, name):
    t = zg_t.shape[1]
    nc = min(GMLP_CHUNKS_PER_STEP, t // BLOCK)
    tw = nc * BLOCK

    def body(zg_ref, lng_ref, lnb_ref, w_ref, bs_ref, y_ref):
        u, _, _, vn = _gmlp_parts(zg_ref[...].astype(F32), lng_ref[...], lnb_ref[...])
        for g in range(GMLP_GROUPS):
            gs = slice(g * GMLP_GROUP_DIM, (g + 1) * GMLP_GROUP_DIM)
            wc = _causal(w_ref[g]).astype(BF16)
            mixed = _dg(_stack_chunks(vn[gs, :].astype(BF16), nc), wc, NT) + bs_ref[g:g + 1, :]
            y_ref[gs, :] = (u[gs, :] * _unstack_chunks(mixed, nc)).astype(BF16)

    return pl.pallas_call(
        body, name=name, grid=(t // tw,),
        in_specs=[pl.BlockSpec((2 * GMLP_WIDTH, tw), lambda n: (0, n)), _resident((GMLP_WIDTH, 1)),
                  _resident((GMLP_WIDTH, 1)), _resident((GMLP_GROUPS, BLOCK, BLOCK)), _resident((GMLP_GROUPS, BLOCK))],
        out_specs=pl.BlockSpec((GMLP_WIDTH, tw), lambda n: (0, n)),
        out_shape=jax.ShapeDtypeStruct((GMLP_WIDTH, t), BF16),
        compiler_params=_params(("arbitrary",)),
    )(zg_t, lng, lnb, w_s, b_s)


def gmlp_bwd(zg_t, dy_t, lng, lnb, w_s, b_s, name):
    t = zg_t.shape[1]
    nc = min(GMLP_CHUNKS_PER_STEP, t // BLOCK)
    tw = nc * BLOCK
    nt = t // tw

    def body(zg_ref, dy_ref, lng_ref, lnb_ref, w_ref, bs_ref, dzg_ref, dw_ref, dbs_ref, dlng_ref, dlnb_ref,
             gacc_ref, bacc_ref):
        n = pl.program_id(0)

        @pl.when(n == 0)
        def _():
            dw_ref[...] = jnp.zeros_like(dw_ref)
            dbs_ref[...] = jnp.zeros_like(dbs_ref)
            gacc_ref[...] = jnp.zeros_like(gacc_ref)
            bacc_ref[...] = jnp.zeros_like(bacc_ref)

        zg = zg_ref[...].astype(F32)
        lng_v = lng_ref[...]
        u, xhat, rstd, vn = _gmlp_parts(zg, lng_v, lnb_ref[...])
        dy = dy_ref[...].astype(F32)
        dvn_parts = []
        for g in range(GMLP_GROUPS):
            gs = slice(g * GMLP_GROUP_DIM, (g + 1) * GMLP_GROUP_DIM)
            wc = _causal(w_ref[g]).astype(BF16)
            vn_s = _stack_chunks(vn[gs, :].astype(BF16), nc)
            mixed = _unstack_chunks(_dg(vn_s, wc, NT) + bs_ref[g:g + 1, :], nc)
            dy_g = dy[gs, :]
            dmixed = dy_g * u[gs, :]
            dm_s = _stack_chunks(dmixed.astype(BF16), nc)
            dzg_ref[gs, :] = (dy_g * mixed * _gelu_grad(zg[gs, :])).astype(BF16)
            dw_ref[g] += _causal(_dg(dm_s, vn_s, TN))
            dbs_ref[g:g + 1, :] += _lane_fold(jnp.sum(dmixed, axis=0, keepdims=True))
            dvn_parts.append(_unstack_chunks(_dot(dm_s, wc), nc))
        dvn = jnp.concatenate(dvn_parts, axis=0)
        gacc_ref[...] += _lane_fold(dvn * xhat)
        bacc_ref[...] += _lane_fold(dvn)
        dxhat = dvn * lng_v
        m1 = jnp.mean(dxhat, axis=0, keepdims=True)
        m2 = jnp.mean(dxhat * xhat, axis=0, keepdims=True)
        dvv = rstd * (dxhat - m1 - xhat * m2)
        dzg_ref[GMLP_WIDTH:, :] = (dvv * _gelu_grad(zg[GMLP_WIDTH:, :])).astype(BF16)

        @pl.when(n == nt - 1)
        def _():
            dlng_ref[...] = jnp.sum(gacc_ref[...], axis=1, keepdims=True)
            dlnb_ref[...] = jnp.sum(bacc_ref[...], axis=1, keepdims=True)

    const2 = lambda n: (0, 0)
    return pl.pallas_call(
        body, name=name, grid=(nt,),
        in_specs=[pl.BlockSpec((2 * GMLP_WIDTH, tw), lambda n: (0, n)), pl.BlockSpec((GMLP_WIDTH, tw), lambda n: (0, n)),
                  _resident((GMLP_WIDTH, 1)), _resident((GMLP_WIDTH, 1)),
                  _resident((GMLP_GROUPS, BLOCK, BLOCK)), _resident((GMLP_GROUPS, BLOCK))],
        out_specs=[pl.BlockSpec((2 * GMLP_WIDTH, tw), lambda n: (0, n)),
                   pl.BlockSpec((GMLP_GROUPS, BLOCK, BLOCK), lambda n: (0, 0, 0)),
                   pl.BlockSpec((GMLP_GROUPS, BLOCK), const2),
                   pl.BlockSpec((GMLP_WIDTH, 1), const2), pl.BlockSpec((GMLP_WIDTH, 1), const2)],
        out_shape=[jax.ShapeDtypeStruct((2 * GMLP_WIDTH, t), BF16),
                   jax.ShapeDtypeStruct((GMLP_GROUPS, BLOCK, BLOCK), F32),
                   jax.ShapeDtypeStruct((GMLP_GROUPS, BLOCK), F32),
                   jax.ShapeDtypeStruct((GMLP_WIDTH, 1), F32), jax.ShapeDtypeStruct((GMLP_WIDTH, 1), F32)],
        scratch_shapes=[pltpu.VMEM((GMLP_WIDTH, BLOCK), F32), pltpu.VMEM((GMLP_WIDTH, BLOCK), F32)],
        compiler_params=_params(("arbitrary",)),
    )(zg_t, dy_t, lng, lnb, w_s, b_s)


def _rmsnorm_cols(y, gain_col):
    r = lax.rsqrt(jnp.mean(y * y, axis=0, keepdims=True) + EPS)
    n = y * r
    return n, r, n * gain_col


def mix_out_fwd(x, ya_t, yg_t, gao, ggo, wout, b_out, name):
    t, d = x.shape
    tm = _token_tile(t)

    def body(x_ref, ya_ref, yg_ref, gao_ref, ggo_ref, w_ref, b_ref, o_ref):
        _, _, ya = _rmsnorm_cols(ya_ref[...].astype(F32), gao_ref[...])
        _, _, yg = _rmsnorm_cols(yg_ref[...].astype(F32), ggo_ref[...])
        o_ref[...] = (x_ref[...] + _dg(ya.astype(BF16), w_ref[:ATTN_WIDTH, :], TN)
                      + _dg(yg.astype(BF16), w_ref[ATTN_WIDTH:, :], TN) + b_ref[...])

    return pl.pallas_call(
        body, name=name, grid=(t // tm,),
        in_specs=[pl.BlockSpec((tm, d), lambda i: (i, 0)), pl.BlockSpec((ATTN_WIDTH, tm), lambda i: (0, i)),
                  pl.BlockSpec((GMLP_WIDTH, tm), lambda i: (0, i)), _resident((ATTN_WIDTH, 1)), _resident((GMLP_WIDTH, 1)),
                  _resident((ATTN_WIDTH + GMLP_WIDTH, d)), _resident((1, d))],
        out_specs=pl.BlockSpec((tm, d), lambda i: (i, 0)),
        out_shape=jax.ShapeDtypeStruct((t, d), F32),
        compiler_params=_params(("arbitrary",)),
    )(x, ya_t, yg_t, gao, ggo, wout, b_out)


def _lane_fold(v):
    out = v[:, :LANES]
    for j in range(1, v.shape[1] // LANES):
        out = out + v[:, j * LANES:(j + 1) * LANES]
    return out


def mix_out_bwd(dx, ya_t, yg_t, gao, ggo, wout, name, deps=()):
    t, d = dx.shape
    tm = _token_tile(t)
    nt = t // tm

    def body(dx_ref, ya_ref, yg_ref, gao_ref, ggo_ref, w_ref, dya_ref, dyg_ref, y_ref, dgao_ref, dggo_ref, db_ref,
             acc_ref):
        i = pl.program_id(0)

        @pl.when(i == 0)
        def _():
            acc_ref[...] = jnp.zeros_like(acc_ref)
            db_ref[...] = jnp.zeros_like(db_ref)

        dxv = dx_ref[...]
        db_ref[...] += jnp.sum(dxv, axis=0, keepdims=True)
        dxb = dxv.astype(BF16)
        for part, (src_ref, gain_ref, dst_ref) in enumerate(((ya_ref, gao_ref, dya_ref), (yg_ref, ggo_ref, dyg_ref))):
            rows = slice(part * ATTN_WIDTH, (part + 1) * ATTN_WIDTH)
            gain = gain_ref[...]
            nrm, r, yn = _rmsnorm_cols(src_ref[...].astype(F32), gain)
            y_ref[rows, :] = yn.astype(BF16)
            dy = _dg(w_ref[rows, :], dxb, NT)
            acc_ref[rows, :] += _lane_fold(dy * nrm)
            dn = dy * gain
            dst_ref[...] = (r * (dn - nrm * jnp.mean(dn * nrm, axis=0, keepdims=True))).astype(BF16)

        @pl.when(i == nt - 1)
        def _():
            dgao_ref[...] = jnp.sum(acc_ref[:ATTN_WIDTH, :], axis=1, keepdims=True)
            dggo_ref[...] = jnp.sum(acc_ref[ATTN_WIDTH:, :], axis=1, keepdims=True)

    const2 = lambda i: (0, 0)
    feat = lambda w: pl.BlockSpec((w, tm), lambda i: (0, i))
    return pl.pallas_call(
        _drop_deps(body, 6, len(deps)), name=name, grid=(nt,),
        in_specs=[pl.BlockSpec((tm, d), lambda i: (i, 0)), feat(ATTN_WIDTH), feat(GMLP_WIDTH),
                  _resident((ATTN_WIDTH, 1)), _resident((GMLP_WIDTH, 1)), _resident((ATTN_WIDTH + GMLP_WIDTH, d))]
        + [ANY_SPEC] * len(deps),
        out_specs=[feat(ATTN_WIDTH), feat(GMLP_WIDTH), feat(ATTN_WIDTH + GMLP_WIDTH),
                   pl.BlockSpec((ATTN_WIDTH, 1), const2), pl.BlockSpec((GMLP_WIDTH, 1), const2),
                   pl.BlockSpec((1, d), const2)],
        out_shape=[jax.ShapeDtypeStruct((ATTN_WIDTH, t), BF16), jax.ShapeDtypeStruct((GMLP_WIDTH, t), BF16),
                   jax.ShapeDtypeStruct((ATTN_WIDTH + GMLP_WIDTH, t), BF16),
                   jax.ShapeDtypeStruct((ATTN_WIDTH, 1), F32), jax.ShapeDtypeStruct((GMLP_WIDTH, 1), F32),
                   jax.ShapeDtypeStruct((1, d), F32)],
        scratch_shapes=[pltpu.VMEM((ATTN_WIDTH + GMLP_WIDTH, LANES), F32)],
        compiler_params=_params(("arbitrary",)),
    )(dx, ya_t, yg_t, gao, ggo, wout, *deps)


def mix_in_bwd(x, dxo, gain, dq_t, dk_t, dv_t, dzg_t, win_t, name):
    t, d = x.shape
    w = win_t.shape[0]
    tm = _token_tile(t)
    nt = t // tm
    parts = ((0, ATTN_WIDTH), (ATTN_WIDTH, KV_WIDTH), (ATTN_WIDTH + KV_WIDTH, KV_WIDTH), (QKV_WIDTH, w - QKV_WIDTH))

    def body(x_ref, dxo_ref, g_ref, dq_ref, dk_ref, dv_ref, dzg_ref, w_ref, dx_ref, h_ref, dg_ref, db_ref, acc_ref):
        i = pl.program_id(0)

        @pl.when(i == 0)
        def _():
            acc_ref[...] = jnp.zeros_like(acc_ref)
            dg_ref[...] = jnp.zeros_like(dg_ref)

        xv = x_ref[...]
        gain_v = g_ref[...]
        r = lax.rsqrt(jnp.mean(xv * xv, axis=-1, keepdims=True) + EPS)
        h_ref[...] = (xv * r * gain_v).astype(BF16)
        dh = jnp.zeros((tm, d), F32)
        for (off, size), src_ref in zip(parts, (dq_ref, dk_ref, dv_ref, dzg_ref)):
            dp = src_ref[...]
            acc_ref[off:off + size, :] += _lane_fold(dp.astype(F32))
            dh = dh + _dg(dp, w_ref[off:off + size, :], TN)
        dx, dgain = _rmsnorm_bwd_rows(xv, r, gain_v, dh)
        dx_ref[...] = dxo_ref[...] + dx
        dg_ref[...] += dgain

        @pl.when(i == nt - 1)
        def _():
            db_ref[...] = jnp.sum(acc_ref[...], axis=1, keepdims=True)

    const2 = lambda i: (0, 0)
    tok = lambda: pl.BlockSpec((tm, d), lambda i: (i, 0))
    feat = lambda rows: pl.BlockSpec((rows, tm), lambda i: (0, i))
    return pl.pallas_call(
        body, name=name, grid=(nt,),
        in_specs=[tok(), tok(), _resident((1, d)), feat(ATTN_WIDTH), feat(KV_WIDTH), feat(KV_WIDTH),
                  feat(w - QKV_WIDTH), _resident((w, d))],
        out_specs=[tok(), tok(), pl.BlockSpec((1, d), const2), pl.BlockSpec((w, 1), const2)],
        out_shape=[jax.ShapeDtypeStruct((t, d), F32), jax.ShapeDtypeStruct((t, d), BF16),
                   jax.ShapeDtypeStruct((1, d), F32), jax.ShapeDtypeStruct((w, 1), F32)],
        scratch_shapes=[pltpu.VMEM((w, LANES), F32)],
        compiler_params=_params(("arbitrary",)),
    )(x, dxo, gain, dq_t, dk_t, dv_t, dzg_t, win_t)


def _adamw_math(w, g, m, v):
    m = ADAM_B1 * m + (1.0 - ADAM_B1) * g
    v = ADAM_B2 * v + (1.0 - ADAM_B2) * (g * g)
    m_hat = m / (1.0 - ADAM_B1 ** ADAM_STEP)
    v_hat = v / (1.0 - ADAM_B2 ** ADAM_STEP)
    delta = -ADAM_LR * (m_hat / (jnp.sqrt(v_hat) + ADAM_EPS) + ADAM_WD * w)
    return delta, m, v


def adamw_many(ws, gs, ms, vs, name):
    n = len(ws)

    def body(*refs):
        w_refs, g_refs, m_refs, v_refs = refs[:n], refs[n:2 * n], refs[2 * n:3 * n], refs[3 * n:4 * n]
        d_refs, mo_refs, vo_refs = refs[4 * n:5 * n], refs[5 * n:6 * n], refs[6 * n:]
        for k in range(n):
            delta, mn, vn = _adamw_math(w_refs[k][...], g_refs[k][...], m_refs[k][...], v_refs[k][...])
            d_refs[k][...] = delta
            mo_refs[k][...] = mn
            vo_refs[k][...] = vn

    def whole(a):
        zeros = (0,) * a.ndim
        return pl.BlockSpec(a.shape, lambda i: zeros)

    outs = pl.pallas_call(
        body, name=name, grid=(1,),
        in_specs=[whole(a) for a in ws] * 4, out_specs=[whole(a) for a in ws] * 3,
        out_shape=[jax.ShapeDtypeStruct(a.shape, F32) for a in ws] * 3,
        compiler_params=_params(("arbitrary",)),
    )(*ws, *gs, *ms, *vs)
    return outs[:n], outs[n:2 * n], outs[2 * n:]


def sum_adamw(landing, grad3d, me, w, m, v, name):
    nd, rows, cols = landing.shape
    tr = rows // 2

    def body(me_ref, l_ref, own_ref, w_ref, m_ref, v_ref, g_ref, d_ref, mo_ref, vo_ref):
        g = own_ref[...].astype(F32)
        for j in range(nd):
            g = g + l_ref[j].astype(F32)
        delta, mn, vn = _adamw_math(w_ref[...], g, m_ref[...], v_ref[...])
        g_ref[...] = g
        d_ref[...] = delta
        mo_ref[...] = mn
        vo_ref[...] = vn

    spec = lambda: pl.BlockSpec((tr, cols), lambda i, me_ref: (i, 0))
    shape = jax.ShapeDtypeStruct((rows, cols), F32)
    return pl.pallas_call(
        body, name=name,
        grid_spec=pltpu.PrefetchScalarGridSpec(
            num_scalar_prefetch=1, grid=(rows // tr,),
            in_specs=[pl.BlockSpec((nd, tr, cols), lambda i, me_ref: (0, i, 0)),
                      pl.BlockSpec((None, tr, cols), lambda i, me_ref: (me_ref[0], i, 0)), spec(), spec(), spec()],
            out_specs=[spec() for _ in range(4)]),
        out_shape=[shape] * 4,
        compiler_params=_params(("arbitrary",)),
    )(me, landing, grad3d, w, m, v)


def _mesh_place():
    x, y, c = lax.axis_index("x"), lax.axis_index("y"), lax.axis_index("c")
    return x, y, c, 4 * x + 2 * y + c


def _peer(x, y, c, k):
    return (x ^ ((k >> 2) & 1), y ^ ((k >> 1) & 1), c ^ (k & 1))


def prep_shards(mats, me, name):
    na = len(mats)

    def body(me_ref, *refs):
        in_refs, shard_refs, land_refs = refs[:na], refs[na:2 * na], refs[2 * na:]
        for a in range(na):
            val = in_refs[a][...].astype(BF16)
            shard_refs[a][...] = val
            land_refs[a][...] = val

    whole = lambda mat: pl.BlockSpec(mat.shape, lambda i, me_ref: (0, 0))
    outs = pl.pallas_call(
        body, name=name,
        grid_spec=pltpu.PrefetchScalarGridSpec(
            num_scalar_prefetch=1, grid=(1,),
            in_specs=[whole(mat) for mat in mats],
            out_specs=[whole(mat) for mat in mats]
            + [pl.BlockSpec((None,) + mat.shape, lambda i, me_ref: (me_ref[0], 0, 0)) for mat in mats]),
        out_shape=[jax.ShapeDtypeStruct(mat.shape, BF16) for mat in mats]
        + [jax.ShapeDtypeStruct((N_DEV,) + mat.shape, BF16) for mat in mats],
        compiler_params=_params(("arbitrary",)),
    )(me, *mats)
    return outs[:na], outs[na:]


def allgather_rows(shards, lands, name):
    na = n_gather = len(shards)

    def body(*refs):
        in_refs, out_refs = refs[:na], refs[2 * na:3 * na]
        send_sems, recv_sems = refs[3 * na:]
        x, y, c, me = _mesh_place()

        def copy(sem, a, origin, src, k):
            return pltpu.make_async_remote_copy(
                src_ref=src, dst_ref=out_refs[a].at[origin], send_sem=send_sems.at[sem, a],
                recv_sem=recv_sems.at[sem, a], device_id=_peer(x, y, c, k), device_id_type=pl.DeviceIdType.MESH)

        kx, ky = 4, 2

        def recv(sem, a, origin):
            copy(sem, a, origin, in_refs[a], 1).wait_recv()

        def landed(a, origin):
            return out_refs[a].at[origin]

        @pl.when(c == 1)
        def _():
            sends = []

            def send(sem, a, origin, src, k):
                cp = copy(sem, a, origin, src, k)
                cp.start()
                sends.append(cp)

            for a in range(n_gather):
                send(1, a, me, in_refs[a], kx)
                send(2, a, me, in_refs[a], ky)
                send(0, a, me, in_refs[a], 1)
            for a in range(n_gather):
                recv(0, a, me ^ 1)
                send(3, a, me ^ 1, landed(a, me ^ 1), kx)
                send(4, a, me ^ 1, landed(a, me ^ 1), ky)
            for a in range(n_gather):
                recv(1, a, me ^ kx)
                send(5, a, me ^ kx, landed(a, me ^ kx), ky)
                send(7, a, me ^ kx, landed(a, me ^ kx), 1)
                recv(2, a, me ^ ky)
                send(8, a, me ^ ky, landed(a, me ^ ky), 1)
            for a in range(n_gather):
                recv(3, a, me ^ kx ^ 1)
                send(9, a, me ^ kx ^ 1, landed(a, me ^ kx ^ 1), 1)
                recv(4, a, me ^ ky ^ 1)
                send(6, a, me ^ ky ^ 1, landed(a, me ^ ky ^ 1), kx)
                send(10, a, me ^ ky ^ 1, landed(a, me ^ ky ^ 1), 1)
            for a in range(n_gather):
                recv(5, a, me ^ 6)
                send(11, a, me ^ 6, landed(a, me ^ 6), 1)
                recv(6, a, me ^ 7)
                send(12, a, me ^ 7, landed(a, me ^ 7), 1)
            for cp in sends:
                cp.wait_send()

        @pl.when(c == 0)
        def _():
            north = me ^ 1
            own = [copy(0, a, me, in_refs[a], 1) for a in range(n_gather)]
            for cp in own:
                cp.start()
            for a in range(n_gather):
                recv(0, a, north)
                for sem, origin in ((7, north ^ kx), (8, north ^ ky), (9, north ^ kx ^ 1), (10, north ^ ky ^ 1),
                                    (11, north ^ 6), (12, north ^ 7)):
                    recv(sem, a, origin)
            for cp in own:
                cp.wait_send()

    return pl.pallas_call(
        body, name=name,
        in_specs=[ANY_SPEC] * (2 * na), out_specs=[ANY_SPEC] * na,
        out_shape=[jax.ShapeDtypeStruct(land.shape, land.dtype) for land in lands],
        input_output_aliases={na + a: a for a in range(na)},
        scratch_shapes=[pltpu.SemaphoreType.DMA((N_GATHER_ROLES, n_gather)),
                        pltpu.SemaphoreType.DMA((N_GATHER_ROLES, n_gather))],
    )(*shards, *lands)


HBM_SPEC = pl.BlockSpec(memory_space=pltpu.HBM)
SEM_SPEC = pl.BlockSpec(memory_space=pltpu.SEMAPHORE)
SIDE_EFFECT = pltpu.SideEffectType.DATAFLOW_SIDE_EFFECTING


def _hbm(v):
    return pltpu.with_memory_space_constraint(v, pltpu.HBM)


def gather_slices(src, land, me, k):
    return src, land.at[me], land.at[me ^ k]


def exchange_slices(src, land, me, k):
    return src.at[me ^ k], land.at[k - 1], land.at[k - 1]


def split_start(groups, slices, deps, name):
    sizes = [len(srcs) for srcs, _ in groups]
    flat_src = [s for srcs, _ in groups for s in srcs]
    flat_land = [l for _, lands in groups for l in lands]
    na, ng, nd = len(flat_src), len(groups), len(deps)

    def body(*refs):
        src_refs, land_refs = refs[:na], refs[na:2 * na]
        sem_refs = refs[2 * na + nd:2 * na + nd + 2 * ng]
        token_ref = refs[-1]
        x, y, c, me = _mesh_place()
        a0 = 0
        for gi, size in enumerate(sizes):
            send_sems, recv_sems = sem_refs[2 * gi], sem_refs[2 * gi + 1]
            for k in range(1, N_DEV):
                for j in range(size):
                    src, dst, _ = slices(src_refs[a0 + j], land_refs[a0 + j], me, k)
                    pltpu.make_async_remote_copy(
                        src_ref=src, dst_ref=dst, send_sem=send_sems.at[(k - 1) * size + j],
                        recv_sem=recv_sems.at[(k - 1) * size + j],
                        device_id=_peer(x, y, c, k), device_id_type=pl.DeviceIdType.MESH).start()
            a0 += size
        token_ref[...] = jnp.zeros_like(token_ref)

    sems = [pltpu.SemaphoreType.DMA(((N_DEV - 1) * size,)) for size in sizes for _ in range(2)]
    thru = [pltpu.HBM(v.shape, v.dtype) for v in flat_src + flat_land]
    outs = pl.pallas_call(
        body, name=name,
        in_specs=[HBM_SPEC] * (2 * na) + [ANY_SPEC] * nd,
        out_specs=[SEM_SPEC] * (2 * ng) + [HBM_SPEC] * (2 * na) + [pl.BlockSpec(memory_space=pltpu.VMEM)],
        out_shape=sems + thru + [jax.ShapeDtypeStruct((8, LANES), F32)],
        input_output_aliases={i: 2 * ng + i for i in range(2 * na)},
        compiler_params=pltpu.CompilerParams(has_side_effects=SIDE_EFFECT),
    )(*[_hbm(v) for v in flat_src + flat_land], *deps)
    started, a0 = [], 0
    for gi, size in enumerate(sizes):
        srcs = outs[2 * ng + a0:2 * ng + a0 + size]
        lands = outs[2 * ng + na + a0:2 * ng + na + a0 + size]
        started.append((outs[2 * gi], outs[2 * gi + 1], list(srcs), list(lands)))
        a0 += size
    return started, outs[-1]


def split_wait(started, slices, after, name):
    send_sems, recv_sems, srcs, lands = started
    na = len(srcs)

    def body(*refs):
        src_refs, land_refs = refs[:na], refs[na:2 * na]
        send_ref, recv_ref = refs[2 * na], refs[2 * na + 1]
        x, y, c, me = _mesh_place()
        for k in range(1, N_DEV):
            for j in range(na):
                src, _, got = slices(src_refs[j], land_refs[j], me, k)
                cp = pltpu.make_async_remote_copy(
                    src_ref=src, dst_ref=got, send_sem=send_ref.at[(k - 1) * na + j], recv_sem=recv_ref.at[(k - 1) * na + j],
                    device_id=_peer(x, y, c, k), device_id_type=pl.DeviceIdType.MESH)
                cp.wait_send()
                cp.wait_recv()

    outs = pl.pallas_call(
        body, name=name,
        in_specs=[HBM_SPEC] * (2 * na) + [SEM_SPEC, SEM_SPEC] + [ANY_SPEC] * len(after),
        out_specs=[HBM_SPEC] * (2 * na),
        out_shape=[pltpu.HBM(v.shape, v.dtype) for v in srcs + lands],
        input_output_aliases={i: i for i in range(2 * na)},
        compiler_params=pltpu.CompilerParams(has_side_effects=SIDE_EFFECT),
    )(*srcs, *lands, send_sems, recv_sems, *after)
    return list(outs[:na]), list(outs[na:])


def place_own(pack, me, name):
    rows, cols = pack.shape

    def body(me_ref, p_ref, o_ref):
        o_ref[...] = p_ref[...]

    return pl.pallas_call(
        body, name=name,
        grid_spec=pltpu.PrefetchScalarGridSpec(
            num_scalar_prefetch=1, grid=(1,),
            in_specs=[pl.BlockSpec((rows, cols), lambda i, me_ref: (0, 0))],
            out_specs=pl.BlockSpec((None, rows, cols), lambda i, me_ref: (me_ref[0], 0, 0))),
        out_shape=jax.ShapeDtypeStruct((N_DEV, rows, cols), F32),
        compiler_params=_params(("arbitrary",)),
    )(me, pack)


def sum_slots(slots, name):
    nd, rows, cols = slots.shape

    def body(s_ref, o_ref):
        acc = s_ref[0]
        for j in range(1, nd):
            acc = acc + s_ref[j]
        o_ref[...] = acc

    return pl.pallas_call(
        body, name=name, grid=(1,),
        in_specs=[pl.BlockSpec((nd, rows, cols), lambda i: (0, 0, 0))],
        out_specs=pl.BlockSpec((rows, cols), lambda i: (0, 0)),
        out_shape=jax.ShapeDtypeStruct((rows, cols), F32),
        compiler_params=_params(("arbitrary",)),
    )(slots)


def _pack_rows(size):
    rows = -(-size // LANES)
    return -(-rows // 8) * 8


def pack_small(parts):
    out = []
    for p in parts:
        flat = p.reshape(-1).astype(F32)
        rows = _pack_rows(flat.shape[0])
        out.append(jnp.pad(flat, (0, rows * LANES - flat.shape[0])).reshape(rows, LANES))
    return jnp.concatenate(out, axis=0)


def unpack_small(pack, shapes):
    out, off = [], 0
    for shp in shapes:
        size = 1
        for s in shp:
            size *= s
        rows = _pack_rows(size)
        out.append(pack[off:off + rows].reshape(-1)[:size].reshape(shp))
        off += rows
    return out


def local_step(x, target, small, get_w, put_g, put_small):
    col = lambda v: v.reshape(-1, 1)
    (wg1, wu1), deps = get_w(("wg1", "wu1"), ())
    sil1, gp1, s1, h1 = ffn_fwd(x, small["ffn1_norm_g"], wg1, wu1, None, "ffn1_fwd", deps)
    (wd1, win, wout), _ = get_w(("wd1", "win", "wout"), (s1,))
    qkv_t, zg_t, x1 = mix_in_fwd(x, small["mix_norm_g"], win, col(small["b_in"]), "mix_in_fwd", ffn_tail=(s1, wd1))
    ya_t = attn_fwd(qkv_t, small["attn_sinks"], "attn_fwd")
    lng, lnb = col(small["gmlp_ln_g"]), col(small["gmlp_ln_b"])
    w_s, b_s = small["gmlp_w_s"][0], small["gmlp_b_s"][0]
    yg_t = gmlp_fwd(zg_t, lng, lnb, w_s, b_s, "gmlp_fwd")
    gao, ggo = col(small["attn_out_norm_g"]), col(small["gmlp_out_norm_g"])
    x2 = mix_out_fwd(x1, ya_t, yg_t, gao, ggo, wout, small["b_out"], "mix_out_fwd")
    (wg2, wu2, wd2), _ = get_w(("wg2", "wu2", "wd2"), (x2,))
    gs = {}
    dx3, sil2, gp2, s2, h2f, gs["final_norm_g"], loss = ffn_fwd(
        x2, small["ffn2_norm_g"], wg2, wu2, wd2, "ffn2_fwd", head=(target, small["final_norm_g"].reshape(1, -1)))

    dx2, da, db, df, gs["ffn2_norm_g"] = ffn_bwd(x2, dx3, small["ffn2_norm_g"], sil2, gp2, wg2, wu2, wd2, "ffn2_bwd")
    deps = put_g({"wg2": grad_tn(da, h2f, 1.0, "ffn2_dwg"), "wu2": grad_tn(db, h2f, 1.0, "ffn2_dwu"),
                  "wd2": grad_tn(s2, df, 1.0, "ffn2_dwd")})

    dya_t, dyg_t, y_t, dgao, dggo, gs["b_out"] = mix_out_bwd(dx2, ya_t, yg_t, gao, ggo, wout, "mix_out_bwd", deps)
    gs["attn_out_norm_g"], gs["gmlp_out_norm_g"] = dgao.reshape(1, -1), dggo.reshape(1, -1)
    g_wout = grad_nn([y_t], dx2, "dwout")
    dzg_t, dws, dbs, dlng, dlnb = gmlp_bwd(zg_t, dyg_t, lng, lnb, w_s, b_s, "gmlp_bwd")
    gs["gmlp_w_s"], gs["gmlp_b_s"] = dws[None], dbs[None]
    gs["gmlp_ln_g"], gs["gmlp_ln_b"] = dlng.reshape(1, -1), dlnb.reshape(1, -1)
    dq_t, dk_t, dv_t, dsinks = attn_bwd(qkv_t, dya_t, small["attn_sinks"], "attn_bwd")
    gs["attn_sinks"] = dsinks.reshape(1, -1)
    dx1, h2, gs["mix_norm_g"], dbin = mix_in_bwd(x1, dx2, small["mix_norm_g"], dq_t, dk_t, dv_t, dzg_t, win,
                                                 "mix_in_bwd")
    gs["b_in"] = dbin.reshape(1, -1)
    deps = put_g({"wout": g_wout, "win": grad_nn([dq_t, dk_t, dv_t, dzg_t], h2, "dwin")})

    dx0, da, db, df, gs["ffn1_norm_g"] = ffn_bwd(x, dx1, small["ffn1_norm_g"], sil1, gp1, wg1, wu1, wd1, "ffn1_bwd",
                                             deps)
    deps = put_small(gs, loss)
    deps = put_g({"wg1": grad_tn(da, h1, 1.0, "ffn1_dwg", deps)})
    deps = put_g({"wu1": grad_tn(db, h1, 1.0, "ffn1_dwu", deps)})
    put_g({"wd1": grad_tn(s1, df, 1.0, "ffn1_dwd", deps)})
    return dx0


SMALL_NAMES = ["ffn1_norm_g", "mix_norm_g", "b_in", "attn_sinks", "gmlp_ln_g", "gmlp_ln_b", "gmlp_w_s", "gmlp_b_s",
               "attn_out_norm_g", "gmlp_out_norm_g", "b_out", "ffn2_norm_g", "final_norm_g"]
BIG_NAMES = {"wg1": ("ffn1_w_gate", True), "wu1": ("ffn1_w_up", True), "wd1": ("ffn1_w_down", False),
             "win": ("w_in", True), "wout": ("w_out", False),
             "wg2": ("ffn2_w_gate", True), "wu2": ("ffn2_w_up", True), "wd2": ("ffn2_w_down", False)}
WEIGHT_ORDER = ["ffn1_norm_g", "ffn1_w_gate", "ffn1_w_up", "ffn1_w_down", "mix_norm_g", "w_in", "b_in", "attn_sinks",
                "gmlp_ln_g", "gmlp_ln_b", "gmlp_w_s", "gmlp_b_s", "attn_out_norm_g", "gmlp_out_norm_g", "w_out",
                "b_out", "ffn2_norm_g", "ffn2_w_gate", "ffn2_w_up", "ffn2_w_down", "final_norm_g"]


def kernel(x, ffn1_norm_g, ffn1_w_gate, ffn1_w_up, ffn1_w_down, mix_norm_g, w_in, b_in, attn_sinks, gmlp_ln_g, gmlp_ln_b, gmlp_w_s, gmlp_b_s, attn_out_norm_g, gmlp_out_norm_g, w_out, b_out, ffn2_norm_g, ffn2_w_gate, ffn2_w_up, ffn2_w_down, final_norm_g, loss_target, m_ffn1_norm_g, m_ffn1_w_gate, m_ffn1_w_up, m_ffn1_w_down, m_mix_norm_g, m_w_in, m_b_in, m_attn_sinks, m_gmlp_ln_g, m_gmlp_ln_b, m_gmlp_w_s, m_gmlp_b_s, m_attn_out_norm_g, m_gmlp_out_norm_g, m_w_out, m_b_out, m_ffn2_norm_g, m_ffn2_w_gate, m_ffn2_w_up, m_ffn2_w_down, m_final_norm_g, v_ffn1_norm_g, v_ffn1_w_gate, v_ffn1_w_up, v_ffn1_w_down, v_mix_norm_g, v_w_in, v_b_in, v_attn_sinks, v_gmlp_ln_g, v_gmlp_ln_b, v_gmlp_w_s, v_gmlp_b_s, v_attn_out_norm_g, v_gmlp_out_norm_g, v_w_out, v_b_out, v_ffn2_norm_g, v_ffn2_w_gate, v_ffn2_w_up, v_ffn2_w_down, v_final_norm_g):
    args = dict(locals())
    w = {n: args[n] for n in WEIGHT_ORDER}
    m = {n: args["m_" + n] for n in WEIGHT_ORDER}
    v = {n: args["v_" + n] for n in WEIGHT_ORDER}

    d_model = x.shape[-1]
    flat = lambda g: g.reshape(-1, d_model)
    me = _mesh_place()[3].astype(jnp.int32).reshape(1)
    first = ("wg1", "wu1")
    later = (("wd1", "win", "wout"), ("wg2", "wu2", "wd2"))
    gathers, exchanges, last_token = {}, [], []

    order = list(first) + [k for grp in later for k in grp]
    mats = [w[BIG_NAMES[k][0]][0].T if BIG_NAMES[k][1] else w[BIG_NAMES[k][0]][0] for k in order]
    shard_list, land_list = prep_shards(mats, me, "prep_shards")
    shards, zones = dict(zip(order, shard_list)), dict(zip(order, land_list))

    def get_w(keys, after):
        if keys == first:
            full = allgather_rows([shards[k] for k in first], [zones[k] for k in first], "allgather_ffn1")
            started, token = split_start([([shards[k] for k in grp], [zones[k] for k in grp]) for grp in later],
                                         gather_slices, (full[0],), "gather_start")
            gathers.update(zip(later, started))
            return [flat(g) for g in full], (token,)
        _, lands = split_wait(gathers[keys], gather_slices, after, "gather_wait_" + keys[0])
        return [flat(land) for land in lands], ()

    def put_g(gb):
        keys = tuple(gb)
        g3 = [gb[k].reshape(N_DEV, -1, d_model) for k in keys]
        lands = [lax.empty((N_DEV - 1,) + g.shape[1:], BF16) for g in g3]
        started, token = split_start([(g3, lands)], exchange_slices, (), "exchange_start_" + keys[0])
        exchanges.append((keys, started[0]))
        last_token[:] = [token]
        return (token,)

    small_started = []

    def put_small(gs, loss):
        pack = pack_small([gs[n] for n in SMALL_NAMES] + [loss[:, :1]])
        started, token = split_start([([pack], [place_own(pack, me, "place_small")])], gather_slices, (), "small_start")
        small_started[:] = started
        return (token,)

    small = {n: w[n] for n in SMALL_NAMES}
    grad_x = local_step(x[0], loss_target[0], small, get_w, put_g, put_small)

    grads, deltas, new_m, new_v = {}, {}, {}, {}
    shapes = [w[n].shape for n in SMALL_NAMES]
    _, (slots,) = split_wait(small_started[0], gather_slices, tuple(last_token), "small_wait")
    total = sum_slots(slots, "sum_small")
    small_g = unpack_small(total, shapes + [(1, 1)])
    loss_total = small_g[-1].reshape(())
    d_, m_, v_ = adamw_many([w[n] for n in SMALL_NAMES], small_g[:-1], [m[n] for n in SMALL_NAMES],
                            [v[n] for n in SMALL_NAMES], "adamw_small")
    for n, g_, dd, mm, vv in zip(SMALL_NAMES, small_g[:-1], d_, m_, v_):
        grads[n], deltas[n], new_m[n], new_v[n] = g_, dd, mm, vv

    after = tuple(arr for n in SMALL_NAMES for arr in (grads[n], deltas[n], new_m[n], new_v[n]))
    for keys, started in exchanges:
        g3, lands = split_wait(started, exchange_slices, after, "exchange_wait_" + keys[0])
        for key, own, land in zip(keys, g3, lands):
            pname, transposed = BIG_NAMES[key]
            to_rows = (lambda p: p[0].T) if transposed else (lambda p: p[0])
            from_rows = (lambda r: r.T[None]) if transposed else (lambda r: r[None])
            g, d_, m_, v_ = sum_adamw(land, own, me, to_rows(w[pname]), to_rows(m[pname]), to_rows(v[pname]),
                                      "adamw_" + key)
            after = after + (g,)
            grads[pname], deltas[pname], new_m[pname], new_v[pname] = (from_rows(g), from_rows(d_), from_rows(m_),
                                                                        from_rows(v_))

    return (loss_total, grad_x[None], *[grads[n] for n in WEIGHT_ORDER], *[deltas[n] for n in WEIGHT_ORDER],
            *[new_m[n] for n in WEIGHT_ORDER], *[new_v[n] for n in WEIGHT_ORDER])
```

```python
import functools

import jax
import jax.numpy as jnp
from jax import lax
from jax.experimental import pallas as pl
from jax.experimental.pallas import tpu as pltpu

F32 = jnp.float32
BF16 = jnp.bfloat16

N_DEV = 8
N_Q_HEADS = 8
N_KV_HEADS = 2
HEAD_DIM = 64
ATTN_WIDTH = N_Q_HEADS * HEAD_DIM
KV_WIDTH = N_KV_HEADS * HEAD_DIM
QKV_WIDTH = ATTN_WIDTH + 2 * KV_WIDTH
BLOCK = 128
GMLP_GROUPS = 8
GMLP_GROUP_DIM = 64
GMLP_WIDTH = GMLP_GROUPS * GMLP_GROUP_DIM
EPS = 1e-6
FFN_RES = 0.5
ATTN_SCALE = HEAD_DIM ** -0.5
MASK_VALUE = -1e30

ADAM_LR = 0.001
ADAM_B1 = 0.9
ADAM_B2 = 0.999
ADAM_EPS = 1e-08
ADAM_WD = 0.01
ADAM_STEP = 10

V7X_VMEM_BYTES = 64 * 1024 * 1024
VMEM_LIMIT = 62 * 1024 * 1024
LANES = 128
MXU_WIDTH = 256
FFN_FWD_TOKENS = 512
FFN_BWD_TOKENS = 256

NT = (((1,), (1,)), ((), ()))
TN = (((0,), (0,)), ((), ()))


def _dot(a, b):
    return jnp.dot(a, b, preferred_element_type=F32)


def _dg(a, b, dims):
    return lax.dot_general(a, b, dims, preferred_element_type=F32)


def _params(semantics=None):
    return pltpu.CompilerParams(dimension_semantics=semantics, vmem_limit_bytes=VMEM_LIMIT)


def _resident(shape):
    zeros = (0,) * len(shape)
    return pl.BlockSpec(shape, lambda *_: zeros, pipeline_mode=pl.Buffered(1))


def _drop_deps(body, n_in, n_deps):
    return lambda *refs: body(*refs[:n_in], *refs[n_in + n_deps:])


ANY_SPEC = pl.BlockSpec(memory_space=pl.ANY)


TOKEN_TILE = 1024
GRAD_TOKENS_PER_STEP = 2048
GRAD_ROW_TILE_CAP = 1408
N_GATHER_ROLES = 13


def _token_tile(t):
    return min(t, TOKEN_TILE)


def _f_chunk(f):
    for c in (256, 128):
        if f % c == 0:
            return c
    return f


def _loss_head(xv, target, gain):
    r = lax.rsqrt(jnp.mean(xv * xv, axis=-1, keepdims=True) + EPS)
    err = xv * r * gain - target
    loss = 0.5 * jnp.sum(jnp.mean(err * err, axis=-1, keepdims=True), axis=0, keepdims=True)
    dx, dgain = _rmsnorm_bwd_rows(xv, r, gain, err * (1.0 / xv.shape[-1]))
    return dx, dgain, loss


def ffn_fwd(x, gain, wg_t, wu_t, wd, name, deps=(), head=None):
    t, d = x.shape
    f = wg_t.shape[0]
    tm = min(t, FFN_FWD_TOKENS)
    fc = _f_chunk(f)
    weights = [wg_t, wu_t] + ([] if wd is None else [wd])
    n_in = 2 + len(weights) + (0 if head is None else 2)
    n_x = 0 if wd is None else 1

    def body(x_ref, g_ref, wg_ref, wu_ref, *rest):
        sil_ref, gp_ref, s_ref, h_ref = rest[n_in - 4 + n_x:n_in + n_x]
        xv = x_ref[...]
        r = lax.rsqrt(jnp.mean(xv * xv, axis=-1, keepdims=True) + EPS)
        h = (xv * r * g_ref[...]).astype(BF16)
        h_ref[...] = h
        for c in range(f // fc):
            sl = slice(c * fc, (c + 1) * fc)
            a = _dg(h, wg_ref[sl, :], NT)
            b = _dg(h, wu_ref[sl, :], NT)
            sig = jax.nn.sigmoid(a)
            sil = a * sig
            sil_ref[:, sl] = sil.astype(BF16)
            gp_ref[:, sl] = (b * (sig + sil - sil * sig)).astype(BF16)
            s_ref[:, sl] = (sil * b).astype(BF16)
        if wd is None:
            return
        wd_ref, xo_ref = rest[0], rest[n_in - 4]
        xo = xv + FFN_RES * _dot(s_ref[...], wd_ref[...])
        if head is None:
            xo_ref[...] = xo
        else:
            t_ref, hg_ref = rest[1:3]
            dg_ref, loss_ref = rest[-2:]

            @pl.when(pl.program_id(0) == 0)
            def _():
                dg_ref[...] = jnp.zeros_like(dg_ref)
                loss_ref[...] = jnp.zeros_like(loss_ref)

            dx, dgain, loss = _loss_head(xo, t_ref[...], hg_ref[...])
            xo_ref[...] = dx
            dg_ref[...] += dgain
            loss_ref[...] += loss

    tok = lambda: pl.BlockSpec((tm, d), lambda i: (i, 0))
    wide = lambda: pl.BlockSpec((tm, f), lambda i: (i, 0))
    const2 = lambda i: (0, 0)
    in_specs = [tok(), _resident((1, d))] + [_resident((f, d)) for _ in weights]
    out_specs = [tok()] * n_x + [wide(), wide(), wide(), tok()]
    out_shape = ([jax.ShapeDtypeStruct((t, d), F32)] * n_x + [jax.ShapeDtypeStruct((t, f), BF16)] * 3
                 + [jax.ShapeDtypeStruct((t, d), BF16)])
    operands = [x, gain] + weights
    if head is not None:
        in_specs += [tok(), _resident((1, d))]
        out_specs += [pl.BlockSpec((1, d), const2), pl.BlockSpec((1, LANES), const2)]
        out_shape += [jax.ShapeDtypeStruct((1, d), F32), jax.ShapeDtypeStruct((1, LANES), F32)]
        operands += list(head)
    return pl.pallas_call(
        _drop_deps(body, n_in, len(deps)), name=name, grid=(t // tm,),
        in_specs=in_specs + [ANY_SPEC] * len(deps), out_specs=out_specs, out_shape=out_shape,
        compiler_params=_params(("arbitrary",)),
    )(*operands, *deps)


def _rmsnorm_bwd_rows(xv, r, gain, dh):
    n = xv * r
    dgain = jnp.sum(dh * n, axis=0, keepdims=True)
    dn = dh * gain
    dx = r * (dn - n * jnp.mean(dn * n, axis=-1, keepdims=True))
    return dx, dgain


def ffn_bwd(x, dxo, gain, sil, gp, wg_t, wu_t, wd, name, deps=()):
    t, d = x.shape
    f = wd.shape[0]
    tm = min(t, FFN_BWD_TOKENS)
    nt = t // tm
    chunks = [(off, min(MXU_WIDTH, f - off)) for off in range(0, f, MXU_WIDTH)]

    def body(xp_ref, dxp_ref, dxo_ref, g_ref, sil_ref, gp_ref, wg_ref, wu_ref, wd_ref,
             dx_ref, da_ref, db_ref, df_ref, dg_ref, dh_ref):
        i = pl.program_id(0)
        gain_v = g_ref[...]

        @pl.when(i == 0)
        def _():
            dh_ref[...] = jnp.zeros_like(dh_ref)
            dg_ref[...] = jnp.zeros_like(dg_ref)

        def finish_previous(slot):
            xv = xp_ref[...]
            r = lax.rsqrt(jnp.mean(xv * xv, axis=-1, keepdims=True) + EPS)
            dx, dgain = _rmsnorm_bwd_rows(xv, r, gain_v, dh_ref[slot])
            dx_ref[...] = dxp_ref[...] + dx
            dg_ref[...] += dgain

        @pl.when(i < nt)
        def _():
            df = (FFN_RES * dxo_ref[...]).astype(BF16)
            df_ref[...] = df
            for off, size in chunks:
                sl = slice(off, off + size)
                ds = _dg(df, wd_ref[sl, :], NT)
                da_ref[:, sl] = (ds * gp_ref[:, sl].astype(F32)).astype(BF16)
                db_ref[:, sl] = (ds * sil_ref[:, sl].astype(F32)).astype(BF16)
            dh = _dot(da_ref[...], wg_ref[...]) + _dot(db_ref[...], wu_ref[...])
            finish_previous((i + 1) % 2)
            dh_ref[i % 2] = dh

        @pl.when(i == nt)
        def _():
            finish_previous((nt - 1) % 2)

    prev = lambda i: (jnp.maximum(i - 1, 0), 0)
    cur = lambda i: (jnp.minimum(i, nt - 1), 0)
    return pl.pallas_call(
        _drop_deps(body, 9, len(deps)), name=name, grid=(nt + 1,),
        in_specs=[pl.BlockSpec((tm, d), prev), pl.BlockSpec((tm, d), prev), pl.BlockSpec((tm, d), cur),
                  _resident((1, d)), pl.BlockSpec((tm, f), cur), pl.BlockSpec((tm, f), cur),
                  _resident((f, d)), _resident((f, d)), _resident((f, d))] + [ANY_SPEC] * len(deps),
        out_specs=[pl.BlockSpec((tm, d), prev), pl.BlockSpec((tm, f), cur), pl.BlockSpec((tm, f), cur),
                   pl.BlockSpec((tm, d), cur), pl.BlockSpec((1, d), lambda i: (0, 0))],
        out_shape=[jax.ShapeDtypeStruct((t, d), F32), jax.ShapeDtypeStruct((t, f), BF16),
                   jax.ShapeDtypeStruct((t, f), BF16), jax.ShapeDtypeStruct((t, d), BF16),
                   jax.ShapeDtypeStruct((1, d), F32)],
        scratch_shapes=[pltpu.VMEM((2, tm, d), F32)],
        compiler_params=_params(("arbitrary",)),
    )(x, dxo, dxo, gain, sil, gp, wg_t, wu_t, wd, *deps)


def _row_tile(m, cap):
    if m <= cap:
        return m
    best = None
    for bm in range(LANES, cap + 1, LANES):
        if m % bm == 0:
            best = bm
    return best if best is not None else m


def grad_tn(a, b, scale, name, deps=()):
    t, m = a.shape
    n = b.shape[1]
    bm = _row_tile(m, GRAD_ROW_TILE_CAP)
    bk = min(t, GRAD_TOKENS_PER_STEP)
    nk = t // bk

    def body(a_ref, b_ref, o_ref, acc_ref):
        k = pl.program_id(1)

        @pl.when(k == 0)
        def _():
            acc_ref[...] = jnp.zeros_like(acc_ref)

        acc_ref[...] += _dg(a_ref[...].astype(BF16), b_ref[...].astype(BF16), TN)

        @pl.when(k == nk - 1)
        def _():
            o_ref[...] = (scale * acc_ref[...]).astype(BF16)

    return pl.pallas_call(
        _drop_deps(body, 2, len(deps)), name=name, grid=(m // bm, nk),
        in_specs=[pl.BlockSpec((bk, bm), lambda i, k: (k, i)), pl.BlockSpec((bk, n), lambda i, k: (k, 0))]
        + [ANY_SPEC] * len(deps),
        out_specs=pl.BlockSpec((bm, n), lambda i, k: (i, 0)),
        out_shape=jax.ShapeDtypeStruct((m, n), BF16),
        scratch_shapes=[pltpu.VMEM((bm, n), F32)],
        compiler_params=_params(("arbitrary", "arbitrary")),
    )(a, b, *deps)


def grad_nn(a_list, b, name):
    t, n = b.shape
    ms = [a.shape[0] for a in a_list]
    m = sum(ms)
    bk = min(t, GRAD_TOKENS_PER_STEP)
    nk = t // bk
    na = len(a_list)

    def body(*refs):
        a_refs, b_ref, o_ref, acc_ref = refs[:na], refs[na], refs[na + 1], refs[na + 2]
        k = pl.program_id(0)

        @pl.when(k == 0)
        def _():
            acc_ref[...] = jnp.zeros_like(acc_ref)

        bv = b_ref[...].astype(BF16)
        off = 0
        for a_ref, mi in zip(a_refs, ms):
            acc_ref[off:off + mi, :] += _dot(a_ref[...].astype(BF16), bv)
            off += mi

        @pl.when(k == nk - 1)
        def _():
            o_ref[...] = acc_ref[...].astype(BF16)

    return pl.pallas_call(
        body, name=name, grid=(nk,),
        in_specs=[pl.BlockSpec((mi, bk), lambda k: (0, k)) for mi in ms] + [pl.BlockSpec((bk, n), lambda k: (k, 0))],
        out_specs=pl.BlockSpec((m, n), lambda k: (0, 0)),
        out_shape=jax.ShapeDtypeStruct((m, n), BF16),
        scratch_shapes=[pltpu.VMEM((m, n), F32)],
        compiler_params=_params(("arbitrary",)),
    )(*a_list, b)


def mix_in_fwd(x, gain, win_t, b_in_col, name, ffn_tail=None):
    t, d = x.shape
    w = win_t.shape[0]
    tm = _token_tile(t)

    def body(x_ref, g_ref, w_ref, bias_ref, *rest):
        xv = x_ref[...]
        if ffn_tail is not None:
            s_ref, wd_ref, xo_ref = rest[0], rest[1], rest[4]
            xv = xv + FFN_RES * _dot(s_ref[...], wd_ref[...])
            xo_ref[...] = xv
        qkv_ref, zg_ref = rest[-3:-1] if ffn_tail is not None else rest
        r = lax.rsqrt(jnp.mean(xv * xv, axis=-1, keepdims=True) + EPS)
        h = (xv * r * g_ref[...]).astype(BF16)
        qkv_ref[...] = (_dg(w_ref[:QKV_WIDTH, :], h, NT) + bias_ref[:QKV_WIDTH, :]).astype(BF16)
        zg_ref[...] = (_dg(w_ref[QKV_WIDTH:, :], h, NT) + bias_ref[QKV_WIDTH:, :]).astype(BF16)

    tok = lambda: pl.BlockSpec((tm, d), lambda i: (i, 0))
    in_specs = [tok(), _resident((1, d)), _resident((w, d)), _resident((w, 1))]
    out_specs = [pl.BlockSpec((QKV_WIDTH, tm), lambda i: (0, i)), pl.BlockSpec((w - QKV_WIDTH, tm), lambda i: (0, i))]
    out_shape = [jax.ShapeDtypeStruct((QKV_WIDTH, t), BF16), jax.ShapeDtypeStruct((w - QKV_WIDTH, t), BF16)]
    operands = [x, gain, win_t, b_in_col]
    if ffn_tail is not None:
        f = ffn_tail[1].shape[0]
        in_specs += [pl.BlockSpec((tm, f), lambda i: (i, 0)), _resident((f, d))]
        out_specs.append(tok())
        out_shape.append(jax.ShapeDtypeStruct((t, d), F32))
        operands += list(ffn_tail)
    return pl.pallas_call(
        body, name=name, grid=(t // tm,), in_specs=in_specs, out_specs=out_specs, out_shape=out_shape,
        compiler_params=_params(("arbitrary",)),
    )(*operands)


ATTN_REP = N_Q_HEADS // N_KV_HEADS
ATTN_BLOCKS_PER_STEP = 16


def _attn_tiles(t):
    nbs = min(ATTN_BLOCKS_PER_STEP, t // BLOCK)
    return nbs, nbs * BLOCK, t // (nbs * BLOCK)


def _attn_specs(nbs, tiles):
    last = tiles - 1
    cur = lambda n: jnp.minimum(n, last)
    prev = lambda n: jnp.maximum(jnp.minimum(n, last) * nbs - 1, 0)
    k_row = ATTN_WIDTH // KV_WIDTH
    tw = nbs * BLOCK
    return [
        pl.BlockSpec((ATTN_WIDTH, tw), lambda n: (0, cur(n))),
        pl.BlockSpec((KV_WIDTH, BLOCK), lambda n: (k_row, prev(n))),
        pl.BlockSpec((KV_WIDTH, tw), lambda n: (k_row, cur(n))),
        pl.BlockSpec((KV_WIDTH, BLOCK), lambda n: (k_row + 1, prev(n))),
        pl.BlockSpec((KV_WIDTH, tw), lambda n: (k_row + 1, cur(n))),
    ]


def _cols(b):
    return slice(b * BLOCK, (b + 1) * BLOCK)


def _band_rows(prev_ref, tile_ref, gs, b):
    before = prev_ref[gs, :] if b == 0 else tile_ref[gs, _cols(b - 1)]
    return before, tile_ref[gs, _cols(b)]


def _group_rows(ref, g, b):
    first = g * ATTN_REP
    return jnp.concatenate([ref[(first + r) * HEAD_DIM:(first + r + 1) * HEAD_DIM, _cols(b)] for r in range(ATTN_REP)],
                           axis=1)


def _ungroup_rows(ref, g, b, val):
    first = g * ATTN_REP
    for r in range(ATTN_REP):
        ref[(first + r) * HEAD_DIM:(first + r + 1) * HEAD_DIM, _cols(b)] = val[:, r * BLOCK:(r + 1) * BLOCK]


def _attn_upper():
    key = lax.broadcasted_iota(jnp.int32, (BLOCK, ATTN_REP * BLOCK), 0)
    qry = lax.broadcasted_iota(jnp.int32, (BLOCK, ATTN_REP * BLOCK), 1) & (BLOCK - 1)
    return key > qry


def _attn_probs(q, kp, kc, sink_ref, g, upper, no_prev):
    s = jnp.where(upper, _dg(kp, q, TN), _dg(kc, q, TN)) * ATTN_SCALE
    if no_prev is not None:
        s = jnp.where(upper & no_prev, MASK_VALUE, s)
    sink = jnp.concatenate([jnp.full((1, BLOCK), sink_ref[0, g * ATTN_REP + r], F32) for r in range(ATTN_REP)], axis=1)
    m = jnp.maximum(jnp.max(s, axis=0, keepdims=True), sink)
    e = jnp.exp(s - m)
    es = jnp.exp(sink - m)
    inv = 1.0 / (jnp.sum(e, axis=0, keepdims=True) + es)
    return e * inv, es * inv


def _split_band(upper, val):
    return jnp.where(upper, val, 0.0).astype(BF16), jnp.where(upper, 0.0, val).astype(BF16)


def attn_fwd(qkv_t, sinks, name):
    t = qkv_t.shape[1]
    nbs, tw, tiles = _attn_tiles(t)

    def body(sink_ref, q_ref, kp_ref, kc_ref, vp_ref, vc_ref, y_ref):
        n = pl.program_id(0)
        upper = _attn_upper()
        for b in range(nbs):
            for g in range(N_KV_HEADS):
                gs = slice(g * HEAD_DIM, (g + 1) * HEAD_DIM)
                kp, kc = _band_rows(kp_ref, kc_ref, gs, b)
                vp, vc = _band_rows(vp_ref, vc_ref, gs, b)
                p, _ = _attn_probs(_group_rows(q_ref, g, b), kp, kc, sink_ref, g, upper, n == 0 if b == 0 else None)
                pp, pc = _split_band(upper, p)
                _ungroup_rows(y_ref, g, b, (_dot(vp, pp) + _dot(vc, pc)).astype(BF16))

    return pl.pallas_call(
        body, name=name, grid=(tiles,),
        in_specs=[pl.BlockSpec(memory_space=pltpu.SMEM)] + _attn_specs(nbs, tiles),
        out_specs=pl.BlockSpec((ATTN_WIDTH, tw), lambda n: (0, n)),
        out_shape=jax.ShapeDtypeStruct((ATTN_WIDTH, t), BF16),
        compiler_params=_params(("arbitrary",)),
    )(sinks, qkv_t, qkv_t, qkv_t, qkv_t, qkv_t)


def attn_bwd(qkv_t, dy_t, sinks, name):
    t = qkv_t.shape[1]
    nbs, tw, tiles = _attn_tiles(t)
    kept = slice(0, tw - BLOCK)

    def body(sink_ref, q_ref, kp_ref, kc_ref, vp_ref, vc_ref, do_ref, dq_ref, dk_ref, dv_ref, dsink_ref,
             ck_ref, cv_ref, sacc_ref):
        n = pl.program_id(0)

        @pl.when(n == 0)
        def _():
            sacc_ref[...] = jnp.zeros_like(sacc_ref)

        @pl.when((n > 0) & (n < tiles))
        def _():
            if nbs > 1:
                dk_ref[:, kept] = ck_ref[:, kept].astype(BF16)
                dv_ref[:, kept] = cv_ref[:, kept].astype(BF16)

        @pl.when(n < tiles)
        def _():
            upper = _attn_upper()
            for b in range(nbs):
                for g in range(N_KV_HEADS):
                    gs = slice(g * HEAD_DIM, (g + 1) * HEAD_DIM)
                    kp, kc = _band_rows(kp_ref, kc_ref, gs, b)
                    vp, vc = _band_rows(vp_ref, vc_ref, gs, b)
                    q = _group_rows(q_ref, g, b)
                    do = _group_rows(do_ref, g, b)
                    p, ps = _attn_probs(q, kp, kc, sink_ref, g, upper, n == 0 if b == 0 else None)
                    dp = jnp.where(upper, _dg(vp, do, TN), _dg(vc, do, TN))
                    delta = jnp.sum(p * dp, axis=0, keepdims=True)
                    dsink = -(ps * delta)
                    for r in range(ATTN_REP):
                        hd = g * ATTN_REP + r
                        sacc_ref[hd:hd + 1, :] += dsink[:, r * BLOCK:(r + 1) * BLOCK]
                    dsp, dsc = _split_band(upper, p * (dp - delta))
                    pp, pc = _split_band(upper, p)
                    _ungroup_rows(dq_ref, g, b, (ATTN_SCALE * (_dot(kp, dsp) + _dot(kc, dsc))).astype(BF16))
                    dk_before = ATTN_SCALE * _dg(q, dsp, NT)
                    dv_before = _dg(do, pp, NT)
                    if b == 0:
                        @pl.when(n > 0)
                        def _():
                            dk_ref[gs, _cols(nbs - 1)] = (ck_ref[gs, _cols(nbs - 1)] + dk_before).astype(BF16)
                            dv_ref[gs, _cols(nbs - 1)] = (cv_ref[gs, _cols(nbs - 1)] + dv_before).astype(BF16)
                    else:
                        ck_ref[gs, _cols(b - 1)] += dk_before
                        cv_ref[gs, _cols(b - 1)] += dv_before
                    ck_ref[gs, _cols(b)] = ATTN_SCALE * _dg(q, dsc, NT)
                    cv_ref[gs, _cols(b)] = _dg(do, pc, NT)

        @pl.when(n == tiles)
        def _():
            dk_ref[...] = ck_ref[...].astype(BF16)
            dv_ref[...] = cv_ref[...].astype(BF16)
            dsink_ref[...] = jnp.sum(sacc_ref[...], axis=1, keepdims=True)

    last = tiles - 1
    done = lambda n: jnp.maximum(n - 1, 0)
    outs = pl.pallas_call(
        body, name=name, grid=(tiles + 1,),
        in_specs=[pl.BlockSpec(memory_space=pltpu.SMEM)] + _attn_specs(nbs, tiles)
        + [pl.BlockSpec((ATTN_WIDTH, tw), lambda n: (0, jnp.minimum(n, last)))],
        out_specs=[pl.BlockSpec((ATTN_WIDTH, tw), lambda n: (0, jnp.minimum(n, last))),
                   pl.BlockSpec((KV_WIDTH, tw), lambda n: (0, done(n))),
                   pl.BlockSpec((KV_WIDTH, tw), lambda n: (0, done(n))),
                   pl.BlockSpec((N_Q_HEADS, 1), lambda n: (0, 0))],
        out_shape=[jax.ShapeDtypeStruct((ATTN_WIDTH, t), BF16), jax.ShapeDtypeStruct((KV_WIDTH, t), BF16),
                   jax.ShapeDtypeStruct((KV_WIDTH, t), BF16), jax.ShapeDtypeStruct((N_Q_HEADS, 1), F32)],
        scratch_shapes=[pltpu.VMEM((KV_WIDTH, tw), F32), pltpu.VMEM((KV_WIDTH, tw), F32),
                        pltpu.VMEM((N_Q_HEADS, BLOCK), F32)],
        compiler_params=_params(("arbitrary",)),
    )(sinks, qkv_t, qkv_t, qkv_t, qkv_t, qkv_t, dy_t)
    return outs


GELU_C = 0.7978845608028654
GELU_A = 0.044715


def _gelu(x):
    return 0.5 * x * (1.0 + jnp.tanh(GELU_C * (x + GELU_A * x * x * x)))


def _gelu_grad(x):
    th = jnp.tanh(GELU_C * (x + GELU_A * x * x * x))
    return 0.5 * (1.0 + th) + 0.5 * x * (1.0 - th * th) * GELU_C * (1.0 + 3.0 * GELU_A * x * x)


def _gmlp_parts(zg, lng, lnb):
    z = _gelu(zg)
    u = z[:GMLP_WIDTH, :]
    vv = z[GMLP_WIDTH:, :]
    mu = jnp.mean(vv, axis=0, keepdims=True)
    xc = vv - mu
    rstd = lax.rsqrt(jnp.mean(xc * xc, axis=0, keepdims=True) + EPS)
    xhat = xc * rstd
    return u, xhat, rstd, xhat * lng + lnb


def _causal(w):
    row = lax.broadcasted_iota(jnp.int32, (BLOCK, BLOCK), 0)
    col = lax.broadcasted_iota(jnp.int32, (BLOCK, BLOCK), 1)
    return jnp.where(col <= row, w, 0.0)


GMLP_CHUNKS_PER_STEP = 16


def _stack_chunks(v, nc):
    return jnp.concatenate([v[:, c * BLOCK:(c + 1) * BLOCK] for c in range(nc)], axis=0)


def _unstack_chunks(v, nc):
    rows = v.shape[0] // nc
    return jnp.concatenate([v[c * rows:(c + 1) * rows, :] for c in range(nc)], axis=1)


def gmlp_fwd(zg_t, lng, lnb, w_s, b_s, name):
    t = zg_t.shape[1]
    nc = min(GMLP_CHUNKS_PER_STEP, t // BLOCK)
    tw = nc * BLOCK

    def body(zg_ref, lng_ref, lnb_ref, w_ref, bs_ref, y_ref):
        u, _, _, vn = _gmlp_parts(zg_ref[...].astype(F32), lng_ref[...], lnb_ref[...])
        for g in range(GMLP_GROUPS):
            gs = slice(g * GMLP_GROUP_DIM, (g + 1) * GMLP_GROUP_DIM)
            wc = _causal(w_ref[g]).astype(BF16)
            mixed = _dg(_stack_chunks(vn[gs, :].astype(BF16), nc), wc, NT) + bs_ref[g:g + 1, :]
            y_ref[gs, :] = (u[gs, :] * _unstack_chunks(mixed, nc)).astype(BF16)

    return pl.pallas_call(
        body, name=name, grid=(t // tw,),
        in_specs=[pl.BlockSpec((2 * GMLP_WIDTH, tw), lambda n: (0, n)), _resident((GMLP_WIDTH, 1)),
                  _resident((GMLP_WIDTH, 1)), _resident((GMLP_GROUPS, BLOCK, BLOCK)), _resident((GMLP_GROUPS, BLOCK))],
        out_specs=pl.BlockSpec((GMLP_WIDTH, tw), lambda n: (0, n)),
        out_shape=jax.ShapeDtypeStruct((GMLP_WIDTH, t), BF16),
        compiler_params=_params(("arbitrary",)),
    )(zg_t, lng, lnb, w_s, b_s)


def gmlp_bwd(zg_t, dy_t, lng, lnb, w_s, b_s, name):
    t = zg_t.shape[1]
    nc = min(GMLP_CHUNKS_PER_STEP, t // BLOCK)
    tw = nc * BLOCK
    nt = t // tw

    def body(zg_ref, dy_ref, lng_ref, lnb_ref, w_ref, bs_ref, dzg_ref, dw_ref, dbs_ref, dlng_ref, dlnb_ref,
             gacc_ref, bacc_ref):
        n = pl.program_id(0)

        @pl.when(n == 0)
        def _():
            dw_ref[...] = jnp.zeros_like(dw_ref)
            dbs_ref[...] = jnp.zeros_like(dbs_ref)
            gacc_ref[...] = jnp.zeros_like(gacc_ref)
            bacc_ref[...] = jnp.zeros_like(bacc_ref)

        zg = zg_ref[...].astype(F32)
        lng_v = lng_ref[...]
        u, xhat, rstd, vn = _gmlp_parts(zg, lng_v, lnb_ref[...])
        dy = dy_ref[...].astype(F32)
        dvn_parts = []
        for g in range(GMLP_GROUPS):
            gs = slice(g * GMLP_GROUP_DIM, (g + 1) * GMLP_GROUP_DIM)
            wc = _causal(w_ref[g]).astype(BF16)
            vn_s = _stack_chunks(vn[gs, :].astype(BF16), nc)
            mixed = _unstack_chunks(_dg(vn_s, wc, NT) + bs_ref[g:g + 1, :], nc)
            dy_g = dy[gs, :]
            dmixed = dy_g * u[gs, :]
            dm_s = _stack_chunks(dmixed.astype(BF16), nc)
            dzg_ref[gs, :] = (dy_g * mixed * _gelu_grad(zg[gs, :])).astype(BF16)
            dw_ref[g] += _causal(_dg(dm_s, vn_s, TN))
            dbs_ref[g:g + 1, :] += _lane_fold(jnp.sum(dmixed, axis=0, keepdims=True))
            dvn_parts.append(_unstack_chunks(_dot(dm_s, wc), nc))
        dvn = jnp.concatenate(dvn_parts, axis=0)
        gacc_ref[...] += _lane_fold(dvn * xhat)
        bacc_ref[...] += _lane_fold(dvn)
        dxhat = dvn * lng_v
        m1 = jnp.mean(dxhat, axis=0, keepdims=True)
        m2 = jnp.mean(dxhat * xhat, axis=0, keepdims=True)
        dvv = rstd * (dxhat - m1 - xhat * m2)
        dzg_ref[GMLP_WIDTH:, :] = (dvv * _gelu_grad(zg[GMLP_WIDTH:, :])).astype(BF16)

        @pl.when(n == nt - 1)
        def _():
            dlng_ref[...] = jnp.sum(gacc_ref[...], axis=1, keepdims=True)
            dlnb_ref[...] = jnp.sum(bacc_ref[...], axis=1, keepdims=True)

    const2 = lambda n: (0, 0)
    return pl.pallas_call(
        body, name=name, grid=(nt,),
        in_specs=[pl.BlockSpec((2 * GMLP_WIDTH, tw), lambda n: (0, n)), pl.BlockSpec((GMLP_WIDTH, tw), lambda n: (0, n)),
                  _resident((GMLP_WIDTH, 1)), _resident((GMLP_WIDTH, 1)),
                  _resident((GMLP_GROUPS, BLOCK, BLOCK)), _resident((GMLP_GROUPS, BLOCK))],
        out_specs=[pl.BlockSpec((2 * GMLP_WIDTH, tw), lambda n: (0, n)),
                   pl.BlockSpec((GMLP_GROUPS, BLOCK, BLOCK), lambda n: (0, 0, 0)),
                   pl.BlockSpec((GMLP_GROUPS, BLOCK), const2),
                   pl.BlockSpec((GMLP_WIDTH, 1), const2), pl.BlockSpec((GMLP_WIDTH, 1), const2)],
        out_shape=[jax.ShapeDtypeStruct((2 * GMLP_WIDTH, t), BF16),
                   jax.ShapeDtypeStruct((GMLP_GROUPS, BLOCK, BLOCK), F32),
                   jax.ShapeDtypeStruct((GMLP_GROUPS, BLOCK), F32),
                   jax.ShapeDtypeStruct((GMLP_WIDTH, 1), F32), jax.ShapeDtypeStruct((GMLP_WIDTH, 1), F32)],
        scratch_shapes=[pltpu.VMEM((GMLP_WIDTH, BLOCK), F32), pltpu.VMEM((GMLP_WIDTH, BLOCK), F32)],
        compiler_params=_params(("arbitrary",)),
    )(zg_t, dy_t, lng, lnb, w_s, b_s)


def _rmsnorm_cols(y, gain_col):
    r = lax.rsqrt(jnp.mean(y * y, axis=0, keepdims=True) + EPS)
    n = y * r
    return n, r, n * gain_col


def mix_out_fwd(x, ya_t, yg_t, gao, ggo, wout, b_out, name):
    t, d = x.shape
    tm = _token_tile(t)

    def body(x_ref, ya_ref, yg_ref, gao_ref, ggo_ref, w_ref, b_ref, o_ref):
        _, _, ya = _rmsnorm_cols(ya_ref[...].astype(F32), gao_ref[...])
        _, _, yg = _rmsnorm_cols(yg_ref[...].astype(F32), ggo_ref[...])
        o_ref[...] = (x_ref[...] + _dg(ya.astype(BF16), w_ref[:ATTN_WIDTH, :], TN)
                      + _dg(yg.astype(BF16), w_ref[ATTN_WIDTH:, :], TN) + b_ref[...])

    return pl.pallas_call(
        body, name=name, grid=(t // tm,),
        in_specs=[pl.BlockSpec((tm, d), lambda i: (i, 0)), pl.BlockSpec((ATTN_WIDTH, tm), lambda i: (0, i)),
                  pl.BlockSpec((GMLP_WIDTH, tm), lambda i: (0, i)), _resident((ATTN_WIDTH, 1)), _resident((GMLP_WIDTH, 1)),
                  _resident((ATTN_WIDTH + GMLP_WIDTH, d)), _resident((1, d))],
        out_specs=pl.BlockSpec((tm, d), lambda i: (i, 0)),
        out_shape=jax.ShapeDtypeStruct((t, d), F32),
        compiler_params=_params(("arbitrary",)),
    )(x, ya_t, yg_t, gao, ggo, wout, b_out)


def _lane_fold(v):
    out = v[:, :LANES]
    for j in range(1, v.shape[1] // LANES):
        out = out + v[:, j * LANES:(j + 1) * LANES]
    return out


def mix_out_bwd(dx, ya_t, yg_t, gao, ggo, wout, name, deps=()):
    t, d = dx.shape
    tm = _token_tile(t)
    nt = t // tm

    def body(dx_ref, ya_ref, yg_ref, gao_ref, ggo_ref, w_ref, dya_ref, dyg_ref, y_ref, dgao_ref, dggo_ref, db_ref,
             acc_ref):
        i = pl.program_id(0)

        @pl.when(i == 0)
        def _():
            acc_ref[...] = jnp.zeros_like(acc_ref)
            db_ref[...] = jnp.zeros_like(db_ref)

        dxv = dx_ref[...]
        db_ref[...] += jnp.sum(dxv, axis=0, keepdims=True)
        dxb = dxv.astype(BF16)
        for part, (src_ref, gain_ref, dst_ref) in enumerate(((ya_ref, gao_ref, dya_ref), (yg_ref, ggo_ref, dyg_ref))):
            rows = slice(part * ATTN_WIDTH, (part + 1) * ATTN_WIDTH)
            gain = gain_ref[...]
            nrm, r, yn = _rmsnorm_cols(src_ref[...].astype(F32), gain)
            y_ref[rows, :] = yn.astype(BF16)
            dy = _dg(w_ref[rows, :], dxb, NT)
            acc_ref[rows, :] += _lane_fold(dy * nrm)
            dn = dy * gain
            dst_ref[...] = (r * (dn - nrm * jnp.mean(dn * nrm, axis=0, keepdims=True))).astype(BF16)

        @pl.when(i == nt - 1)
        def _():
            dgao_ref[...] = jnp.sum(acc_ref[:ATTN_WIDTH, :], axis=1, keepdims=True)
            dggo_ref[...] = jnp.sum(acc_ref[ATTN_WIDTH:, :], axis=1, keepdims=True)

    const2 = lambda i: (0, 0)
    feat = lambda w: pl.BlockSpec((w, tm), lambda i: (0, i))
    return pl.pallas_call(
        _drop_deps(body, 6, len(deps)), name=name, grid=(nt,),
        in_specs=[pl.BlockSpec((tm, d), lambda i: (i, 0)), feat(ATTN_WIDTH), feat(GMLP_WIDTH),
                  _resident((ATTN_WIDTH, 1)), _resident((GMLP_WIDTH, 1)), _resident((ATTN_WIDTH + GMLP_WIDTH, d))]
        + [ANY_SPEC] * len(deps),
        out_specs=[feat(ATTN_WIDTH), feat(GMLP_WIDTH), feat(ATTN_WIDTH + GMLP_WIDTH),
                   pl.BlockSpec((ATTN_WIDTH, 1), const2), pl.BlockSpec((GMLP_WIDTH, 1), const2),
                   pl.BlockSpec((1, d), const2)],
        out_shape=[jax.ShapeDtypeStruct((ATTN_WIDTH, t), BF16), jax.ShapeDtypeStruct((GMLP_WIDTH, t), BF16),
                   jax.ShapeDtypeStruct((ATTN_WIDTH + GMLP_WIDTH, t), BF16),
                   jax.ShapeDtypeStruct((ATTN_WIDTH, 1), F32), jax.ShapeDtypeStruct((GMLP_WIDTH, 1), F32),
                   jax.ShapeDtypeStruct((1, d), F32)],
        scratch_shapes=[pltpu.VMEM((ATTN_WIDTH + GMLP_WIDTH, LANES), F32)],
        compiler_params=_params(("arbitrary",)),
    )(dx, ya_t, yg_t, gao, ggo, wout, *deps)


def mix_in_bwd(x, dxo, gain, dq_t, dk_t, dv_t, dzg_t, win_t, name):
    t, d = x.shape
    w = win_t.shape[0]
    tm = _token_tile(t)
    nt = t // tm
    parts = ((0, ATTN_WIDTH), (ATTN_WIDTH, KV_WIDTH), (ATTN_WIDTH + KV_WIDTH, KV_WIDTH), (QKV_WIDTH, w - QKV_WIDTH))

    def body(x_ref, dxo_ref, g_ref, dq_ref, dk_ref, dv_ref, dzg_ref, w_ref, dx_ref, h_ref, dg_ref, db_ref, acc_ref):
        i = pl.program_id(0)

        @pl.when(i == 0)
        def _():
            acc_ref[...] = jnp.zeros_like(acc_ref)
            dg_ref[...] = jnp.zeros_like(dg_ref)

        xv = x_ref[...]
        gain_v = g_ref[...]
        r = lax.rsqrt(jnp.mean(xv * xv, axis=-1, keepdims=True) + EPS)
        h_ref[...] = (xv * r * gain_v).astype(BF16)
        dh = jnp.zeros((tm, d), F32)
        for (off, size), src_ref in zip(parts, (dq_ref, dk_ref, dv_ref, dzg_ref)):
            dp = src_ref[...]
            acc_ref[off:off + size, :] += _lane_fold(dp.astype(F32))
            dh = dh + _dg(dp, w_ref[off:off + size, :], TN)
        dx, dgain = _rmsnorm_bwd_rows(xv, r, gain_v, dh)
        dx_ref[...] = dxo_ref[...] + dx
        dg_ref[...] += dgain

        @pl.when(i == nt - 1)
        def _():
            db_ref[...] = jnp.sum(acc_ref[...], axis=1, keepdims=True)

    const2 = lambda i: (0, 0)
    tok = lambda: pl.BlockSpec((tm, d), lambda i: (i, 0))
    feat = lambda rows: pl.BlockSpec((rows, tm), lambda i: (0, i))
    return pl.pallas_call(
        body, name=name, grid=(nt,),
        in_specs=[tok(), tok(), _resident((1, d)), feat(ATTN_WIDTH), feat(KV_WIDTH), feat(KV_WIDTH),
                  feat(w - QKV_WIDTH), _resident((w, d))],
        out_specs=[tok(), tok(), pl.BlockSpec((1, d), const2), pl.BlockSpec((w, 1), const2)],
        out_shape=[jax.ShapeDtypeStruct((t, d), F32), jax.ShapeDtypeStruct((t, d), BF16),
                   jax.ShapeDtypeStruct((1, d), F32), jax.ShapeDtypeStruct((w, 1), F32)],
        scratch_shapes=[pltpu.VMEM((w, LANES), F32)],
        compiler_params=_params(("arbitrary",)),
    )(x, dxo, gain, dq_t, dk_t, dv_t, dzg_t, win_t)


def _adamw_math(w, g, m, v):
    m = ADAM_B1 * m + (1.0 - ADAM_B1) * g
    v = ADAM_B2 * v + (1.0 - ADAM_B2) * (g * g)
    m_hat = m / (1.0 - ADAM_B1 ** ADAM_STEP)
    v_hat = v / (1.0 - ADAM_B2 ** ADAM_STEP)
    delta = -ADAM_LR * (m_hat / (jnp.sqrt(v_hat) + ADAM_EPS) + ADAM_WD * w)
    return delta, m, v


def adamw_many(ws, gs, ms, vs, name):
    n = len(ws)

    def body(*refs):
        w_refs, g_refs, m_refs, v_refs = refs[:n], refs[n:2 * n], refs[2 * n:3 * n], refs[3 * n:4 * n]
        d_refs, mo_refs, vo_refs = refs[4 * n:5 * n], refs[5 * n:6 * n], refs[6 * n:]
        for k in range(n):
            delta, mn, vn = _adamw_math(w_refs[k][...], g_refs[k][...], m_refs[k][...], v_refs[k][...])
            d_refs[k][...] = delta
            mo_refs[k][...] = mn
            vo_refs[k][...] = vn

    def whole(a):
        zeros = (0,) * a.ndim
        return pl.BlockSpec(a.shape, lambda i: zeros)

    outs = pl.pallas_call(
        body, name=name, grid=(1,),
        in_specs=[whole(a) for a in ws] * 4, out_specs=[whole(a) for a in ws] * 3,
        out_shape=[jax.ShapeDtypeStruct(a.shape, F32) for a in ws] * 3,
        compiler_params=_params(("arbitrary",)),
    )(*ws, *gs, *ms, *vs)
    return outs[:n], outs[n:2 * n], outs[2 * n:]


def sum_adamw(landing, grad3d, me, w, m, v, name):
    nd, rows, cols = landing.shape
    tr = rows // 2

    def body(me_ref, l_ref, own_ref, w_ref, m_ref, v_ref, g_ref, d_ref, mo_ref, vo_ref):
        g = own_ref[...].astype(F32)
        for j in range(nd):
            g = g + l_ref[j].astype(F32)
        delta, mn, vn = _adamw_math(w_ref[...], g, m_ref[...], v_ref[...])
        g_ref[...] = g
        d_ref[...] = delta
        mo_ref[...] = mn
        vo_ref[...] = vn

    spec = lambda: pl.BlockSpec((tr, cols), lambda i, me_ref: (i, 0))
    shape = jax.ShapeDtypeStruct((rows, cols), F32)
    return pl.pallas_call(
        body, name=name,
        grid_spec=pltpu.PrefetchScalarGridSpec(
            num_scalar_prefetch=1, grid=(rows // tr,),
            in_specs=[pl.BlockSpec((nd, tr, cols), lambda i, me_ref: (0, i, 0)),
                      pl.BlockSpec((None, tr, cols), lambda i, me_ref: (me_ref[0], i, 0)), spec(), spec(), spec()],
            out_specs=[spec() for _ in range(4)]),
        out_shape=[shape] * 4,
        compiler_params=_params(("arbitrary",)),
    )(me, landing, grad3d, w, m, v)


def _mesh_place():
    x, y, c = lax.axis_index("x"), lax.axis_index("y"), lax.axis_index("c")
    return x, y, c, 4 * x + 2 * y + c


def _peer(x, y, c, k):
    return (x ^ ((k >> 2) & 1), y ^ ((k >> 1) & 1), c ^ (k & 1))


def prep_shards(mats, me, name):
    na = len(mats)

    def body(me_ref, *refs):
        in_refs, shard_refs, land_refs = refs[:na], refs[na:2 * na], refs[2 * na:]
        for a in range(na):
            val = in_refs[a][...].astype(BF16)
            shard_refs[a][...] = val
            land_refs[a][...] = val

    whole = lambda mat: pl.BlockSpec(mat.shape, lambda i, me_ref: (0, 0))
    outs = pl.pallas_call(
        body, name=name,
        grid_spec=pltpu.PrefetchScalarGridSpec(
            num_scalar_prefetch=1, grid=(1,),
            in_specs=[whole(mat) for mat in mats],
            out_specs=[whole(mat) for mat in mats]
            + [pl.BlockSpec((None,) + mat.shape, lambda i, me_ref: (me_ref[0], 0, 0)) for mat in mats]),
        out_shape=[jax.ShapeDtypeStruct(mat.shape, BF16) for mat in mats]
        + [jax.ShapeDtypeStruct((N_DEV,) + mat.shape, BF16) for mat in mats],
        compiler_params=_params(("arbitrary",)),
    )(me, *mats)
    return outs[:na], outs[na:]


def allgather_rows(shards, lands, name):
    na = n_gather = len(shards)

    def body(*refs):
        in_refs, out_refs = refs[:na], refs[2 * na:3 * na]
        send_sems, recv_sems = refs[3 * na:]
        x, y, c, me = _mesh_place()

        def copy(sem, a, origin, src, k):
            return pltpu.make_async_remote_copy(
                src_ref=src, dst_ref=out_refs[a].at[origin], send_sem=send_sems.at[sem, a],
                recv_sem=recv_sems.at[sem, a], device_id=_peer(x, y, c, k), device_id_type=pl.DeviceIdType.MESH)

        kx, ky = 4, 2

        def recv(sem, a, origin):
            copy(sem, a, origin, in_refs[a], 1).wait_recv()

        def landed(a, origin):
            return out_refs[a].at[origin]

        @pl.when(c == 1)
        def _():
            sends = []

            def send(sem, a, origin, src, k):
                cp = copy(sem, a, origin, src, k)
                cp.start()
                sends.append(cp)

            for a in range(n_gather):
                send(1, a, me, in_refs[a], kx)
                send(2, a, me, in_refs[a], ky)
                send(0, a, me, in_refs[a], 1)
            for a in range(n_gather):
                recv(0, a, me ^ 1)
                send(3, a, me ^ 1, landed(a, me ^ 1), kx)
                send(4, a, me ^ 1, landed(a, me ^ 1), ky)
            for a in range(n_gather):
                recv(1, a, me ^ kx)
                send(5, a, me ^ kx, landed(a, me ^ kx), ky)
                send(7, a, me ^ kx, landed(a, me ^ kx), 1)
                recv(2, a, me ^ ky)
                send(8, a, me ^ ky, landed(a, me ^ ky), 1)
            for a in range(n_gather):
                recv(3, a, me ^ kx ^ 1)
                send(9, a, me ^ kx ^ 1, landed(a, me ^ kx ^ 1), 1)
                recv(4, a, me ^ ky ^ 1)
                send(6, a, me ^ ky ^ 1, landed(a, me ^ ky ^ 1), kx)
                send(10, a, me ^ ky ^ 1, landed(a, me ^ ky ^ 1), 1)
            for a in range(n_gather):
                recv(5, a, me ^ 6)
                send(11, a, me ^ 6, landed(a, me ^ 6), 1)
                recv(6, a, me ^ 7)
                send(12, a, me ^ 7, landed(a, me ^ 7), 1)
            for cp in sends:
                cp.wait_send()

        @pl.when(c == 0)
        def _():
            north = me ^ 1
            own = [copy(0, a, me, in_refs[a], 1) for a in range(n_gather)]
            for cp in own:
                cp.start()
            for a in range(n_gather):
                recv(0, a, north)
                for sem, origin in ((7, north ^ kx), (8, north ^ ky), (9, north ^ kx ^ 1), (10, north ^ ky ^ 1),
                                    (11, north ^ 6), (12, north ^ 7)):
                    recv(sem, a, origin)
            for cp in own:
                cp.wait_send()

    return pl.pallas_call(
        body, name=name,
        in_specs=[ANY_SPEC] * (2 * na), out_specs=[ANY_SPEC] * na,
        out_shape=[jax.ShapeDtypeStruct(land.shape, land.dtype) for land in lands],
        input_output_aliases={na + a: a for a in range(na)},
        scratch_shapes=[pltpu.SemaphoreType.DMA((N_GATHER_ROLES, n_gather)),
                        pltpu.SemaphoreType.DMA((N_GATHER_ROLES, n_gather))],
    )(*shards, *lands)


HBM_SPEC = pl.BlockSpec(memory_space=pltpu.HBM)
SEM_SPEC = pl.BlockSpec(memory_space=pltpu.SEMAPHORE)
SIDE_EFFECT = pltpu.SideEffectType.DATAFLOW_SIDE_EFFECTING


def _hbm(v):
    return pltpu.with_memory_space_constraint(v, pltpu.HBM)


def gather_slices(src, land, me, k):
    return src, land.at[me], land.at[me ^ k]


def exchange_slices(src, land, me, k):
    return src.at[me ^ k], land.at[k - 1], land.at[k - 1]


def split_start(groups, slices, deps, name):
    sizes = [len(srcs) for srcs, _ in groups]
    flat_src = [s for srcs, _ in groups for s in srcs]
    flat_land = [l for _, lands in groups for l in lands]
    na, ng, nd = len(flat_src), len(groups), len(deps)

    def body(*refs):
        src_refs, land_refs = refs[:na], refs[na:2 * na]
        sem_refs = refs[2 * na + nd:2 * na + nd + 2 * ng]
        token_ref = refs[-1]
        x, y, c, me = _mesh_place()
        a0 = 0
        for gi, size in enumerate(sizes):
            send_sems, recv_sems = sem_refs[2 * gi], sem_refs[2 * gi + 1]
            for k in range(1, N_DEV):
                for j in range(size):
                    src, dst, _ = slices(src_refs[a0 + j], land_refs[a0 + j], me, k)
                    pltpu.make_async_remote_copy(
                        src_ref=src, dst_ref=dst, send_sem=send_sems.at[(k - 1) * size + j],
                        recv_sem=recv_sems.at[(k - 1) * size + j],
                        device_id=_peer(x, y, c, k), device_id_type=pl.DeviceIdType.MESH).start()
            a0 += size
        token_ref[...] = jnp.zeros_like(token_ref)

    sems = [pltpu.SemaphoreType.DMA(((N_DEV - 1) * size,)) for size in sizes for _ in range(2)]
    thru = [pltpu.HBM(v.shape, v.dtype) for v in flat_src + flat_land]
    outs = pl.pallas_call(
        body, name=name,
        in_specs=[HBM_SPEC] * (2 * na) + [ANY_SPEC] * nd,
        out_specs=[SEM_SPEC] * (2 * ng) + [HBM_SPEC] * (2 * na) + [pl.BlockSpec(memory_space=pltpu.VMEM)],
        out_shape=sems + thru + [jax.ShapeDtypeStruct((8, LANES), F32)],
        input_output_aliases={i: 2 * ng + i for i in range(2 * na)},
        compiler_params=pltpu.CompilerParams(has_side_effects=SIDE_EFFECT),
    )(*[_hbm(v) for v in flat_src + flat_land], *deps)
    started, a0 = [], 0
    for gi, size in enumerate(sizes):
        srcs = outs[2 * ng + a0:2 * ng + a0 + size]
        lands = outs[2 * ng + na + a0:2 * ng + na + a0 + size]
        started.append((outs[2 * gi], outs[2 * gi + 1], list(srcs), list(lands)))
        a0 += size
    return started, outs[-1]


def split_wait(started, slices, after, name):
    send_sems, recv_sems, srcs, lands = started
    na = len(srcs)

    def body(*refs):
        src_refs, land_refs = refs[:na], refs[na:2 * na]
        send_ref, recv_ref = refs[2 * na], refs[2 * na + 1]
        x, y, c, me = _mesh_place()
        for k in range(1, N_DEV):
            for j in range(na):
                src, _, got = slices(src_refs[j], land_refs[j], me, k)
                cp = pltpu.make_async_remote_copy(
                    src_ref=src, dst_ref=got, send_sem=send_ref.at[(k - 1) * na + j], recv_sem=recv_ref.at[(k - 1) * na + j],
                    device_id=_peer(x, y, c, k), device_id_type=pl.DeviceIdType.MESH)
                cp.wait_send()
                cp.wait_recv()

    outs = pl.pallas_call(
        body, name=name,
        in_specs=[HBM_SPEC] * (2 * na) + [SEM_SPEC, SEM_SPEC] + [ANY_SPEC] * len(after),
        out_specs=[HBM_SPEC] * (2 * na),
        out_shape=[pltpu.HBM(v.shape, v.dtype) for v in srcs + lands],
        input_output_aliases={i: i for i in range(2 * na)},
        compiler_params=pltpu.CompilerParams(has_side_effects=SIDE_EFFECT),
    )(*srcs, *lands, send_sems, recv_sems, *after)
    return list(outs[:na]), list(outs[na:])


def place_own(pack, me, name):
    rows, cols = pack.shape

    def body(me_ref, p_ref, o_ref):
        o_ref[...] = p_ref[...]

    return pl.pallas_call(
        body, name=name,
        grid_spec=pltpu.PrefetchScalarGridSpec(
            num_scalar_prefetch=1, grid=(1,),
            in_specs=[pl.BlockSpec((rows, cols), lambda i, me_ref: (0, 0))],
            out_specs=pl.BlockSpec((None, rows, cols), lambda i, me_ref: (me_ref[0], 0, 0))),
        out_shape=jax.ShapeDtypeStruct((N_DEV, rows, cols), F32),
        compiler_params=_params(("arbitrary",)),
    )(me, pack)


def sum_slots(slots, name):
    nd, rows, cols = slots.shape

    def body(s_ref, o_ref):
        acc = s_ref[0]
        for j in range(1, nd):
            acc = acc + s_ref[j]
        o_ref[...] = acc

    return pl.pallas_call(
        body, name=name, grid=(1,),
        in_specs=[pl.BlockSpec((nd, rows, cols), lambda i: (0, 0, 0))],
        out_specs=pl.BlockSpec((rows, cols), lambda i: (0, 0)),
        out_shape=jax.ShapeDtypeStruct((rows, cols), F32),
        compiler_params=_params(("arbitrary",)),
    )(slots)


def _pack_rows(size):
    rows = -(-size // LANES)
    return -(-rows // 8) * 8


def pack_small(parts):
    out = []
    for p in parts:
        flat = p.reshape(-1).astype(F32)
        rows = _pack_rows(flat.shape[0])
        out.append(jnp.pad(flat, (0, rows * LANES - flat.shape[0])).reshape(rows, LANES))
    return jnp.concatenate(out, axis=0)


def unpack_small(pack, shapes):
    out, off = [], 0
    for shp in shapes:
        size = 1
        for s in shp:
            size *= s
        rows = _pack_rows(size)
        out.append(pack[off:off + rows].reshape(-1)[:size].reshape(shp))
        off += rows
    return out


def local_step(x, target, small, get_w, put_g, put_small):
    col = lambda v: v.reshape(-1, 1)
    (wg1, wu1), deps = get_w(("wg1", "wu1"), ())
    sil1, gp1, s1, h1 = ffn_fwd(x, small["ffn1_norm_g"], wg1, wu1, None, "ffn1_fwd", deps)
    (wd1, win, wout), _ = get_w(("wd1", "win", "wout"), (s1,))
    qkv_t, zg_t, x1 = mix_in_fwd(x, small["mix_norm_g"], win, col(small["b_in"]), "mix_in_fwd", ffn_tail=(s1, wd1))
    ya_t = attn_fwd(qkv_t, small["attn_sinks"], "attn_fwd")
    lng, lnb = col(small["gmlp_ln_g"]), col(small["gmlp_ln_b"])
    w_s, b_s = small["gmlp_w_s"][0], small["gmlp_b_s"][0]
    yg_t = gmlp_fwd(zg_t, lng, lnb, w_s, b_s, "gmlp_fwd")
    gao, ggo = col(small["attn_out_norm_g"]), col(small["gmlp_out_norm_g"])
    x2 = mix_out_fwd(x1, ya_t, yg_t, gao, ggo, wout, small["b_out"], "mix_out_fwd")
    (wg2, wu2, wd2), _ = get_w(("wg2", "wu2", "wd2"), (x2,))
    gs = {}
    dx3, sil2, gp2, s2, h2f, gs["final_norm_g"], loss = ffn_fwd(
        x2, small["ffn2_norm_g"], wg2, wu2, wd2, "ffn2_fwd", head=(target, small["final_norm_g"].reshape(1, -1)))

    dx2, da, db, df, gs["ffn2_norm_g"] = ffn_bwd(x2, dx3, small["ffn2_norm_g"], sil2, gp2, wg2, wu2, wd2, "ffn2_bwd")
    deps = put_g({"wg2": grad_tn(da, h2f, 1.0, "ffn2_dwg"), "wu2": grad_tn(db, h2f, 1.0, "ffn2_dwu"),
                  "wd2": grad_tn(s2, df, 1.0, "ffn2_dwd")})

    dya_t, dyg_t, y_t, dgao, dggo, gs["b_out"] = mix_out_bwd(dx2, ya_t, yg_t, gao, ggo, wout, "mix_out_bwd", deps)
    gs["attn_out_norm_g"], gs["gmlp_out_norm_g"] = dgao.reshape(1, -1), dggo.reshape(1, -1)
    g_wout = grad_nn([y_t], dx2, "dwout")
    dzg_t, dws, dbs, dlng, dlnb = gmlp_bwd(zg_t, dyg_t, lng, lnb, w_s, b_s, "gmlp_bwd")
    gs["gmlp_w_s"], gs["gmlp_b_s"] = dws[None], dbs[None]
    gs["gmlp_ln_g"], gs["gmlp_ln_b"] = dlng.reshape(1, -1), dlnb.reshape(1, -1)
    dq_t, dk_t, dv_t, dsinks = attn_bwd(qkv_t, dya_t, small["attn_sinks"], "attn_bwd")
    gs["attn_sinks"] = dsinks.reshape(1, -1)
    dx1, h2, gs["mix_norm_g"], dbin = mix_in_bwd(x1, dx2, small["mix_norm_g"], dq_t, dk_t, dv_t, dzg_t, win,
                                                 "mix_in_bwd")
    gs["b_in"] = dbin.reshape(1, -1)
    deps = put_g({"wout": g_wout, "win": grad_nn([dq_t, dk_t, dv_t, dzg_t], h2, "dwin")})

    dx0, da, db, df, gs["ffn1_norm_g"] = ffn_bwd(x, dx1, small["ffn1_norm_g"], sil1, gp1, wg1, wu1, wd1, "ffn1_bwd",
                                             deps)
    deps = put_small(gs, loss)
    deps = put_g({"wg1": grad_tn(da, h1, 1.0, "ffn1_dwg", deps)})
    deps = put_g({"wu1": grad_tn(db, h1, 1.0, "ffn1_dwu", deps)})
    put_g({"wd1": grad_tn(s1, df, 1.0, "ffn1_dwd", deps)})
    return dx0


SMALL_NAMES = ["ffn1_norm_g", "mix_norm_g", "b_in", "attn_sinks", "gmlp_ln_g", "gmlp_ln_b", "gmlp_w_s", "gmlp_b_s",
               "attn_out_norm_g", "gmlp_out_norm_g", "b_out", "ffn2_norm_g", "final_norm_g"]
BIG_NAMES = {"wg1": ("ffn1_w_gate", True), "wu1": ("ffn1_w_up", True), "wd1": ("ffn1_w_down", False),
             "win": ("w_in", True), "wout": ("w_out", False),
             "wg2": ("ffn2_w_gate", True), "wu2": ("ffn2_w_up", True), "wd2": ("ffn2_w_down", False)}
WEIGHT_ORDER = ["ffn1_norm_g", "ffn1_w_gate", "ffn1_w_up", "ffn1_w_down", "mix_norm_g", "w_in", "b_in", "attn_sinks",
                "gmlp_ln_g", "gmlp_ln_b", "gmlp_w_s", "gmlp_b_s", "attn_out_norm_g", "gmlp_out_norm_g", "w_out",
                "b_out", "ffn2_norm_g", "ffn2_w_gate", "ffn2_w_up", "ffn2_w_down", "final_norm_g"]


def kernel(x, ffn1_norm_g, ffn1_w_gate, ffn1_w_up, ffn1_w_down, mix_norm_g, w_in, b_in, attn_sinks, gmlp_ln_g, gmlp_ln_b, gmlp_w_s, gmlp_b_s, attn_out_norm_g, gmlp_out_norm_g, w_out, b_out, ffn2_norm_g, ffn2_w_gate, ffn2_w_up, ffn2_w_down, final_norm_g, loss_target, m_ffn1_norm_g, m_ffn1_w_gate, m_ffn1_w_up, m_ffn1_w_down, m_mix_norm_g, m_w_in, m_b_in, m_attn_sinks, m_gmlp_ln_g, m_gmlp_ln_b, m_gmlp_w_s, m_gmlp_b_s, m_attn_out_norm_g, m_gmlp_out_norm_g, m_w_out, m_b_out, m_ffn2_norm_g, m_ffn2_w_gate, m_ffn2_w_up, m_ffn2_w_down, m_final_norm_g, v_ffn1_norm_g, v_ffn1_w_gate, v_ffn1_w_up, v_ffn1_w_down, v_mix_norm_g, v_w_in, v_b_in, v_attn_sinks, v_gmlp_ln_g, v_gmlp_ln_b, v_gmlp_w_s, v_gmlp_b_s, v_attn_out_norm_g, v_gmlp_out_norm_g, v_w_out, v_b_out, v_ffn2_norm_g, v_ffn2_w_gate, v_ffn2_w_up, v_ffn2_w_down, v_final_norm_g):
    args = dict(locals())
    w = {n: args[n] for n in WEIGHT_ORDER}
    m = {n: args["m_" + n] for n in WEIGHT_ORDER}
    v = {n: args["v_" + n] for n in WEIGHT_ORDER}

    d_model = x.shape[-1]
    flat = lambda g: g.reshape(-1, d_model)
    me = _mesh_place()[3].astype(jnp.int32).reshape(1)
    first = ("wg1", "wu1")
    later = (("wd1", "win", "wout"), ("wg2", "wu2", "wd2"))
    gathers, exchanges, last_token = {}, [], []

    order = list(first) + [k for grp in later for k in grp]
    mats = [w[BIG_NAMES[k][0]][0].T if BIG_NAMES[k][1] else w[BIG_NAMES[k][0]][0] for k in order]
    shard_list, land_list = prep_shards(mats, me, "prep_shards")
    shards, zones = dict(zip(order, shard_list)), dict(zip(order, land_list))

    def get_w(keys, after):
        if keys == first:
            full = allgather_rows([shards[k] for k in first], [zones[k] for k in first], "allgather_ffn1")
            started, token = split_start([([shards[k] for k in grp], [zones[k] for k in grp]) for grp in later],
                                         gather_slices, (full[0],), "gather_start")
            gathers.update(zip(later, started))
            return [flat(g) for g in full], (token,)
        _, lands = split_wait(gathers[keys], gather_slices, after, "gather_wait_" + keys[0])
        return [flat(land) for land in lands], ()

    def put_g(gb):
        keys = tuple(gb)
        g3 = [gb[k].reshape(N_DEV, -1, d_model) for k in keys]
        lands = [lax.empty((N_DEV - 1,) + g.shape[1:], BF16) for g in g3]
        started, token = split_start([(g3, lands)], exchange_slices, (), "exchange_start_" + keys[0])
        exchanges.append((keys, started[0]))
        last_token[:] = [token]
        return (token,)

    small_started = []

    def put_small(gs, loss):
        pack = pack_small([gs[n] for n in SMALL_NAMES] + [loss[:, :1]])
        started, token = split_start([([pack], [place_own(pack, me, "place_small")])], gather_slices, (), "small_start")
        small_started[:] = started
        return (token,)

    small = {n: w[n] for n in SMALL_NAMES}
    grad_x = local_step(x[0], loss_target[0], small, get_w, put_g, put_small)

    grads, deltas, new_m, new_v = {}, {}, {}, {}
    shapes = [w[n].shape for n in SMALL_NAMES]
    _, (slots,) = split_wait(small_started[0], gather_slices, tuple(last_token), "small_wait")
    total = sum_slots(slots, "sum_small")
    small_g = unpack_small(total, shapes + [(1, 1)])
    loss_total = small_g[-1].reshape(())
    d_, m_, v_ = adamw_many([w[n] for n in SMALL_NAMES], small_g[:-1], [m[n] for n in SMALL_NAMES],
                            [v[n] for n in SMALL_NAMES], "adamw_small")
    for n, g_, dd, mm, vv in zip(SMALL_NAMES, small_g[:-1], d_, m_, v_):
        grads[n], deltas[n], new_m[n], new_v[n] = g_, dd, mm, vv

    after = tuple(arr for n in SMALL_NAMES for arr in (grads[n], deltas[n], new_m[n], new_v[n]))
    for keys, started in exchanges:
        g3, lands = split_wait(started, exchange_slices, after, "exchange_wait_" + keys[0])
        for key, own, land in zip(keys, g3, lands):
            pname, transposed = BIG_NAMES[key]
            to_rows = (lambda p: p[0].T) if transposed else (lambda p: p[0])
            from_rows = (lambda r: r.T[None]) if transposed else (lambda r: r[None])
            g, d_, m_, v_ = sum_adamw(land, own, me, to_rows(w[pname]), to_rows(m[pname]), to_rows(v[pname]),
                                      "adamw_" + key)
            after = after + (g,)
            grads[pname], deltas[pname], new_m[pname], new_v[pname] = (from_rows(g), from_rows(d_), from_rows(m_),
                                                                        from_rows(v_))

    return (loss_total, grad_x[None], *[grads[n] for n in WEIGHT_ORDER], *[deltas[n] for n in WEIGHT_ORDER],
            *[new_m[n] for n in WEIGHT_ORDER], *[new_v[n] for n in WEIGHT_ORDER])
```

```python
import functools

import jax
import jax.numpy as jnp
from jax import lax
from jax.experimental import pallas as pl
from jax.experimental.pallas import tpu as pltpu

F32 = jnp.float32
BF16 = jnp.bfloat16

N_DEV = 8
N_Q_HEADS = 8
N_KV_HEADS = 2
HEAD_DIM = 64
ATTN_WIDTH = N_Q_HEADS * HEAD_DIM
KV_WIDTH = N_KV_HEADS * HEAD_DIM
QKV_WIDTH = ATTN_WIDTH + 2 * KV_WIDTH
BLOCK = 128
GMLP_GROUPS = 8
GMLP_GROUP_DIM = 64
GMLP_WIDTH = GMLP_GROUPS * GMLP_GROUP_DIM
EPS = 1e-6
FFN_RES = 0.5
ATTN_SCALE = HEAD_DIM ** -0.5
MASK_VALUE = -1e30

ADAM_LR = 0.001
ADAM_B1 = 0.9
ADAM_B2 = 0.999
ADAM_EPS = 1e-08
ADAM_WD = 0.01
ADAM_STEP = 10

V7X_VMEM_BYTES = 64 * 1024 * 1024
VMEM_LIMIT = 62 * 1024 * 1024
LANES = 128
MXU_WIDTH = 256
FFN_FWD_TOKENS = 512
FFN_BWD_TOKENS = 256

NT = (((1,), (1,)), ((), ()))
TN = (((0,), (0,)), ((), ()))


def _dot(a, b):
    return jnp.dot(a, b, preferred_element_type=F32)


def _dg(a, b, dims):
    return lax.dot_general(a, b, dims, preferred_element_type=F32)


def _params(semantics=None):
    return pltpu.CompilerParams(dimension_semantics=semantics, vmem_limit_bytes=VMEM_LIMIT)


def _resident(shape):
    zeros = (0,) * len(shape)
    return pl.BlockSpec(shape, lambda *_: zeros, pipeline_mode=pl.Buffered(1))


def _drop_deps(body, n_in, n_deps):
    return lambda *refs: body(*refs[:n_in], *refs[n_in + n_deps:])


ANY_SPEC = pl.BlockSpec(memory_space=pl.ANY)


TOKEN_TILE = 1024
GRAD_TOKENS_PER_STEP = 2048
GRAD_ROW_TILE_CAP = 1408
N_GATHER_ROLES = 13


def _token_tile(t):
    return min(t, TOKEN_TILE)


def _f_chunk(f):
    for c in (256, 128):
        if f % c == 0:
            return c
    return f


def _loss_head(xv, target, gain):
    r = lax.rsqrt(jnp.mean(xv * xv, axis=-1, keepdims=True) + EPS)
    err = xv * r * gain - target
    loss = 0.5 * jnp.sum(jnp.mean(err * err, axis=-1, keepdims=True), axis=0, keepdims=True)
    dx, dgain = _rmsnorm_bwd_rows(xv, r, gain, err * (1.0 / xv.shape[-1]))
    return dx, dgain, loss


def ffn_fwd(x, gain, wg_t, wu_t, wd, name, deps=(), head=None):
    t, d = x.shape
    f = wg_t.shape[0]
    tm = min(t, FFN_FWD_TOKENS)
    fc = _f_chunk(f)
    weights = [wg_t, wu_t] + ([] if wd is None else [wd])
    n_in = 2 + len(weights) + (0 if head is None else 2)
    n_x = 0 if wd is None else 1

    def body(x_ref, g_ref, wg_ref, wu_ref, *rest):
        sil_ref, gp_ref, s_ref, h_ref = rest[n_in - 4 + n_x:n_in + n_x]
        xv = x_ref[...]
        r = lax.rsqrt(jnp.mean(xv * xv, axis=-1, keepdims=True) + EPS)
        h = (xv * r * g_ref[...]).astype(BF16)
        h_ref[...] = h
        for c in range(f // fc):
            sl = slice(c * fc, (c + 1) * fc)
            a = _dg(h, wg_ref[sl, :], NT)
            b = _dg(h, wu_ref[sl, :], NT)
            sig = jax.nn.sigmoid(a)
            sil = a * sig
            sil_ref[:, sl] = sil.astype(BF16)
            gp_ref[:, sl] = (b * (sig + sil - sil * sig)).astype(BF16)
            s_ref[:, sl] = (sil * b).astype(BF16)
        if wd is None:
            return
        wd_ref, xo_ref = rest[0], rest[n_in - 4]
        xo = xv + FFN_RES * _dot(s_ref[...], wd_ref[...])
        if head is None:
            xo_ref[...] = xo
        else:
            t_ref, hg_ref = rest[1:3]
            dg_ref, loss_ref = rest[-2:]

            @pl.when(pl.program_id(0) == 0)
            def _():
                dg_ref[...] = jnp.zeros_like(dg_ref)
                loss_ref[...] = jnp.zeros_like(loss_ref)

            dx, dgain, loss = _loss_head(xo, t_ref[...], hg_ref[...])
            xo_ref[...] = dx
            dg_ref[...] += dgain
            loss_ref[...] += loss

    tok = lambda: pl.BlockSpec((tm, d), lambda i: (i, 0))
    wide = lambda: pl.BlockSpec((tm, f), lambda i: (i, 0))
    const2 = lambda i: (0, 0)
    in_specs = [tok(), _resident((1, d))] + [_resident((f, d)) for _ in weights]
    out_specs = [tok()] * n_x + [wide(), wide(), wide(), tok()]
    out_shape = ([jax.ShapeDtypeStruct((t, d), F32)] * n_x + [jax.ShapeDtypeStruct((t, f), BF16)] * 3
                 + [jax.ShapeDtypeStruct((t, d), BF16)])
    operands = [x, gain] + weights
    if head is not None:
        in_specs += [tok(), _resident((1, d))]
        out_specs += [pl.BlockSpec((1, d), const2), pl.BlockSpec((1, LANES), const2)]
        out_shape += [jax.ShapeDtypeStruct((1, d), F32), jax.ShapeDtypeStruct((1, LANES), F32)]
        operands += list(head)
    return pl.pallas_call(
        _drop_deps(body, n_in, len(deps)), name=name, grid=(t // tm,),
        in_specs=in_specs + [ANY_SPEC] * len(deps), out_specs=out_specs, out_shape=out_shape,
        compiler_params=_params(("arbitrary",)),
    )(*operands, *deps)


def _rmsnorm_bwd_rows(xv, r, gain, dh):
    n = xv * r
    dgain = jnp.sum(dh * n, axis=0, keepdims=True)
    dn = dh * gain
    dx = r * (dn - n * jnp.mean(dn * n, axis=-1, keepdims=True))
    return dx, dgain


def ffn_bwd(x, dxo, gain, sil, gp, wg_t, wu_t, wd, name, deps=()):
    t, d = x.shape
    f = wd.shape[0]
    tm = min(t, FFN_BWD_TOKENS)
    nt = t // tm
    chunks = [(off, min(MXU_WIDTH, f - off)) for off in range(0, f, MXU_WIDTH)]

    def body(xp_ref, dxp_ref, dxo_ref, g_ref, sil_ref, gp_ref, wg_ref, wu_ref, wd_ref,
             dx_ref, da_ref, db_ref, df_ref, dg_ref, dh_ref):
        i = pl.program_id(0)
        gain_v = g_ref[...]

        @pl.when(i == 0)
        def _():
            dh_ref[...] = jnp.zeros_like(dh_ref)
            dg_ref[...] = jnp.zeros_like(dg_ref)

        def finish_previous(slot):
            xv = xp_ref[...]
            r = lax.rsqrt(jnp.mean(xv * xv, axis=-1, keepdims=True) + EPS)
            dx, dgain = _rmsnorm_bwd_rows(xv, r, gain_v, dh_ref[slot])
            dx_ref[...] = dxp_ref[...] + dx
            dg_ref[...] += dgain

        @pl.when(i < nt)
        def _():
            df = (FFN_RES * dxo_ref[...]).astype(BF16)
            df_ref[...] = df
            for off, size in chunks:
                sl = slice(off, off + size)
                ds = _dg(df, wd_ref[sl, :], NT)
                da_ref[:, sl] = (ds * gp_ref[:, sl].astype(F32)).astype(BF16)
                db_ref[:, sl] = (ds * sil_ref[:, sl].astype(F32)).astype(BF16)
            dh = _dot(da_ref[...], wg_ref[...]) + _dot(db_ref[...], wu_ref[...])
            finish_previous((i + 1) % 2)
            dh_ref[i % 2] = dh

        @pl.when(i == nt)
        def _():
            finish_previous((nt - 1) % 2)

    prev = lambda i: (jnp.maximum(i - 1, 0), 0)
    cur = lambda i: (jnp.minimum(i, nt - 1), 0)
    return pl.pallas_call(
        _drop_deps(body, 9, len(deps)), name=name, grid=(nt + 1,),
        in_specs=[pl.BlockSpec((tm, d), prev), pl.BlockSpec((tm, d), prev), pl.BlockSpec((tm, d), cur),
                  _resident((1, d)), pl.BlockSpec((tm, f), cur), pl.BlockSpec((tm, f), cur),
                  _resident((f, d)), _resident((f, d)), _resident((f, d))] + [ANY_SPEC] * len(deps),
        out_specs=[pl.BlockSpec((tm, d), prev), pl.BlockSpec((tm, f), cur), pl.BlockSpec((tm, f), cur),
                   pl.BlockSpec((tm, d), cur), pl.BlockSpec((1, d), lambda i: (0, 0))],
        out_shape=[jax.ShapeDtypeStruct((t, d), F32), jax.ShapeDtypeStruct((t, f), BF16),
                   jax.ShapeDtypeStruct((t, f), BF16), jax.ShapeDtypeStruct((t, d), BF16),
                   jax.ShapeDtypeStruct((1, d), F32)],
        scratch_shapes=[pltpu.VMEM((2, tm, d), F32)],
        compiler_params=_params(("arbitrary",)),
    )(x, dxo, dxo, gain, sil, gp, wg_t, wu_t, wd, *deps)


def _row_tile(m, cap):
    if m <= cap:
        return m
    best = None
    for bm in range(LANES, cap + 1, LANES):
        if m % bm == 0:
            best = bm
    return best if best is not None else m


def grad_tn(a, b, scale, name, deps=()):
    t, m = a.shape
    n = b.shape[1]
    bm = _row_tile(m, GRAD_ROW_TILE_CAP)
    bk = min(t, GRAD_TOKENS_PER_STEP)
    nk = t // bk

    def body(a_ref, b_ref, o_ref, acc_ref):
        k = pl.program_id(1)

        @pl.when(k == 0)
        def _():
            acc_ref[...] = jnp.zeros_like(acc_ref)

        acc_ref[...] += _dg(a_ref[...].astype(BF16), b_ref[...].astype(BF16), TN)

        @pl.when(k == nk - 1)
        def _():
            o_ref[...] = (scale * acc_ref[...]).astype(BF16)

    return pl.pallas_call(
        _drop_deps(body, 2, len(deps)), name=name, grid=(m // bm, nk),
        in_specs=[pl.BlockSpec((bk, bm), lambda i, k: (k, i)), pl.BlockSpec((bk, n), lambda i, k: (k, 0))]
        + [ANY_SPEC] * len(deps),
        out_specs=pl.BlockSpec((bm, n), lambda i, k: (i, 0)),
        out_shape=jax.ShapeDtypeStruct((m, n), BF16),
        scratch_shapes=[pltpu.VMEM((bm, n), F32)],
        compiler_params=_params(("arbitrary", "arbitrary")),
    )(a, b, *deps)


def grad_nn(a_list, b, name):
    t, n = b.shape
    ms = [a.shape[0] for a in a_list]
    m = sum(ms)
    bk = min(t, GRAD_TOKENS_PER_STEP)
    nk = t // bk
    na = len(a_list)

    def body(*refs):
        a_refs, b_ref, o_ref, acc_ref = refs[:na], refs[na], refs[na + 1], refs[na + 2]
        k = pl.program_id(0)

        @pl.when(k == 0)
        def _():
            acc_ref[...] = jnp.zeros_like(acc_ref)

        bv = b_ref[...].astype(BF16)
        off = 0
        for a_ref, mi in zip(a_refs, ms):
            acc_ref[off:off + mi, :] += _dot(a_ref[...].astype(BF16), bv)
            off += mi

        @pl.when(k == nk - 1)
        def _():
            o_ref[...] = acc_ref[...].astype(BF16)

    return pl.pallas_call(
        body, name=name, grid=(nk,),
        in_specs=[pl.BlockSpec((mi, bk), lambda k: (0, k)) for mi in ms] + [pl.BlockSpec((bk, n), lambda k: (k, 0))],
        out_specs=pl.BlockSpec((m, n), lambda k: (0, 0)),
        out_shape=jax.ShapeDtypeStruct((m, n), BF16),
        scratch_shapes=[pltpu.VMEM((m, n), F32)],
        compiler_params=_params(("arbitrary",)),
    )(*a_list, b)


def mix_in_fwd(x, gain, win_t, b_in_col, name, ffn_tail=None):
    t, d = x.shape
    w = win_t.shape[0]
    tm = _token_tile(t)

    def body(x_ref, g_ref, w_ref, bias_ref, *rest):
        xv = x_ref[...]
        if ffn_tail is not None:
            s_ref, wd_ref, xo_ref = rest[0], rest[1], rest[4]
            xv = xv + FFN_RES * _dot(s_ref[...], wd_ref[...])
            xo_ref[...] = xv
        qkv_ref, zg_ref = rest[-3:-1] if ffn_tail is not None else rest
        r = lax.rsqrt(jnp.mean(xv * xv, axis=-1, keepdims=True) + EPS)
        h = (xv * r * g_ref[...]).astype(BF16)
        qkv_ref[...] = (_dg(w_ref[:QKV_WIDTH, :], h, NT) + bias_ref[:QKV_WIDTH, :]).astype(BF16)
        zg_ref[...] = (_dg(w_ref[QKV_WIDTH:, :], h, NT) + bias_ref[QKV_WIDTH:, :]).astype(BF16)

    tok = lambda: pl.BlockSpec((tm, d), lambda i: (i, 0))
    in_specs = [tok(), _resident((1, d)), _resident((w, d)), _resident((w, 1))]
    out_specs = [pl.BlockSpec((QKV_WIDTH, tm), lambda i: (0, i)), pl.BlockSpec((w - QKV_WIDTH, tm), lambda i: (0, i))]
    out_shape = [jax.ShapeDtypeStruct((QKV_WIDTH, t), BF16), jax.ShapeDtypeStruct((w - QKV_WIDTH, t), BF16)]
    operands = [x, gain, win_t, b_in_col]
    if ffn_tail is not None:
        f = ffn_tail[1].shape[0]
        in_specs += [pl.BlockSpec((tm, f), lambda i: (i, 0)), _resident((f, d))]
        out_specs.append(tok())
        out_shape.append(jax.ShapeDtypeStruct((t, d), F32))
        operands += list(ffn_tail)
    return pl.pallas_call(
        body, name=name, grid=(t // tm,), in_specs=in_specs, out_specs=out_specs, out_shape=out_shape,
        compiler_params=_params(("arbitrary",)),
    )(*operands)


ATTN_REP = N_Q_HEADS // N_KV_HEADS
ATTN_BLOCKS_PER_STEP = 8


def _attn_tiles(t):
    nbs = min(ATTN_BLOCKS_PER_STEP, t // BLOCK)
    return nbs, nbs * BLOCK, t // (nbs * BLOCK)


def _attn_specs(nbs, tiles):
    last = tiles - 1
    cur = lambda n: jnp.minimum(n, last)
    prev = lambda n: jnp.maximum(jnp.minimum(n, last) * nbs - 1, 0)
    k_row = ATTN_WIDTH // KV_WIDTH
    tw = nbs * BLOCK
    return [
        pl.BlockSpec((ATTN_WIDTH, tw), lambda n: (0, cur(n))),
        pl.BlockSpec((KV_WIDTH, BLOCK), lambda n: (k_row, prev(n))),
        pl.BlockSpec((KV_WIDTH, tw), lambda n: (k_row, cur(n))),
        pl.BlockSpec((KV_WIDTH, BLOCK), lambda n: (k_row + 1, prev(n))),
        pl.BlockSpec((KV_WIDTH, tw), lambda n: (k_row + 1, cur(n))),
    ]


def _cols(b):
    return slice(b * BLOCK, (b + 1) * BLOCK)


def _band_rows(prev_ref, tile_ref, gs, b):
    before = prev_ref[gs, :] if b == 0 else tile_ref[gs, _cols(b - 1)]
    return before, tile_ref[gs, _cols(b)]


def _group_rows(ref, g, b):
    first = g * ATTN_REP
    return jnp.concatenate([ref[(first + r) * HEAD_DIM:(first + r + 1) * HEAD_DIM, _cols(b)] for r in range(ATTN_REP)],
                           axis=1)


def _ungroup_rows(ref, g, b, val):
    first = g * ATTN_REP
    for r in range(ATTN_REP):
        ref[(first + r) * HEAD_DIM:(first + r + 1) * HEAD_DIM, _cols(b)] = val[:, r * BLOCK:(r + 1) * BLOCK]


def _attn_upper():
    key = lax.broadcasted_iota(jnp.int32, (BLOCK, ATTN_REP * BLOCK), 0)
    qry = lax.broadcasted_iota(jnp.int32, (BLOCK, ATTN_REP * BLOCK), 1) & (BLOCK - 1)
    return key > qry


def _attn_probs(q, kp, kc, sink_ref, g, upper, no_prev):
    s = jnp.where(upper, _dg(kp, q, TN), _dg(kc, q, TN)) * ATTN_SCALE
    if no_prev is not None:
        s = jnp.where(upper & no_prev, MASK_VALUE, s)
    sink = jnp.concatenate([jnp.full((1, BLOCK), sink_ref[0, g * ATTN_REP + r], F32) for r in range(ATTN_REP)], axis=1)
    m = jnp.maximum(jnp.max(s, axis=0, keepdims=True), sink)
    e = jnp.exp(s - m)
    es = jnp.exp(sink - m)
    inv = 1.0 / (jnp.sum(e, axis=0, keepdims=True) + es)
    return e * inv, es * inv


def _split_band(upper, val):
    return jnp.where(upper, val, 0.0).astype(BF16), jnp.where(upper, 0.0, val).astype(BF16)


def attn_fwd(qkv_t, sinks, name):
    t = qkv_t.shape[1]
    nbs, tw, tiles = _attn_tiles(t)

    def body(sink_ref, q_ref, kp_ref, kc_ref, vp_ref, vc_ref, y_ref):
        n = pl.program_id(0)
        upper = _attn_upper()
        for b in range(nbs):
            for g in range(N_KV_HEADS):
                gs = slice(g * HEAD_DIM, (g + 1) * HEAD_DIM)
                kp, kc = _band_rows(kp_ref, kc_ref, gs, b)
                vp, vc = _band_rows(vp_ref, vc_ref, gs, b)
                p, _ = _attn_probs(_group_rows(q_ref, g, b), kp, kc, sink_ref, g, upper, n == 0 if b == 0 else None)
                pp, pc = _split_band(upper, p)
                _ungroup_rows(y_ref, g, b, (_dot(vp, pp) + _dot(vc, pc)).astype(BF16))

    return pl.pallas_call(
        body, name=name, grid=(tiles,),
        in_specs=[pl.BlockSpec(memory_space=pltpu.SMEM)] + _attn_specs(nbs, tiles),
        out_specs=pl.BlockSpec((ATTN_WIDTH, tw), lambda n: (0, n)),
        out_shape=jax.ShapeDtypeStruct((ATTN_WIDTH, t), BF16),
        compiler_params=_params(("arbitrary",)),
    )(sinks, qkv_t, qkv_t, qkv_t, qkv_t, qkv_t)


def attn_bwd(qkv_t, dy_t, sinks, name):
    t = qkv_t.shape[1]
    nbs, tw, tiles = _attn_tiles(t)
    kept = slice(0, tw - BLOCK)

    def body(sink_ref, q_ref, kp_ref, kc_ref, vp_ref, vc_ref, do_ref, dq_ref, dk_ref, dv_ref, dsink_ref,
             ck_ref, cv_ref, sacc_ref):
        n = pl.program_id(0)

        @pl.when(n == 0)
        def _():
            sacc_ref[...] = jnp.zeros_like(sacc_ref)

        @pl.when((n > 0) & (n < tiles))
        def _():
            if nbs > 1:
                dk_ref[:, kept] = ck_ref[:, kept].astype(BF16)
                dv_ref[:, kept] = cv_ref[:, kept].astype(BF16)

        @pl.when(n < tiles)
        def _():
            upper = _attn_upper()
            for b in range(nbs):
                for g in range(N_KV_HEADS):
                    gs = slice(g * HEAD_DIM, (g + 1) * HEAD_DIM)
                    kp, kc = _band_rows(kp_ref, kc_ref, gs, b)
                    vp, vc = _band_rows(vp_ref, vc_ref, gs, b)
                    q = _group_rows(q_ref, g, b)
                    do = _group_rows(do_ref, g, b)
                    p, ps = _attn_probs(q, kp, kc, sink_ref, g, upper, n == 0 if b == 0 else None)
                    dp = jnp.where(upper, _dg(vp, do, TN), _dg(vc, do, TN))
                    delta = jnp.sum(p * dp, axis=0, keepdims=True)
                    dsink = -(ps * delta)
                    for r in range(ATTN_REP):
                        hd = g * ATTN_REP + r
                        sacc_ref[hd:hd + 1, :] += dsink[:, r * BLOCK:(r + 1) * BLOCK]
                    dsp, dsc = _split_band(upper, p * (dp - delta))
                    pp, pc = _split_band(upper, p)
                    _ungroup_rows(dq_ref, g, b, (ATTN_SCALE * (_dot(kp, dsp) + _dot(kc, dsc))).astype(BF16))
                    dk_before = ATTN_SCALE * _dg(q, dsp, NT)
                    dv_before = _dg(do, pp, NT)
                    if b == 0:
                        @pl.when(n > 0)
                        def _():
                            dk_ref[gs, _cols(nbs - 1)] = (ck_ref[gs, _cols(nbs - 1)] + dk_before).astype(BF16)
                            dv_ref[gs, _cols(nbs - 1)] = (cv_ref[gs, _cols(nbs - 1)] + dv_before).astype(BF16)
                    else:
                        ck_ref[gs, _cols(b - 1)] += dk_before
                        cv_ref[gs, _cols(b - 1)] += dv_before
                    ck_ref[gs, _cols(b)] = ATTN_SCALE * _dg(q, dsc, NT)
                    cv_ref[gs, _cols(b)] = _dg(do, pc, NT)

        @pl.when(n == tiles)
        def _():
            dk_ref[...] = ck_ref[...].astype(BF16)
            dv_ref[...] = cv_ref[...].astype(BF16)
            dsink_ref[...] = jnp.sum(sacc_ref[...], axis=1, keepdims=True)

    last = tiles - 1
    done = lambda n: jnp.maximum(n - 1, 0)
    outs = pl.pallas_call(
        body, name=name, grid=(tiles + 1,),
        in_specs=[pl.BlockSpec(memory_space=pltpu.SMEM)] + _attn_specs(nbs, tiles)
        + [pl.BlockSpec((ATTN_WIDTH, tw), lambda n: (0, jnp.minimum(n, last)))],
        out_specs=[pl.BlockSpec((ATTN_WIDTH, tw), lambda n: (0, jnp.minimum(n, last))),
                   pl.BlockSpec((KV_WIDTH, tw), lambda n: (0, done(n))),
                   pl.BlockSpec((KV_WIDTH, tw), lambda n: (0, done(n))),
                   pl.BlockSpec((N_Q_HEADS, 1), lambda n: (0, 0))],
        out_shape=[jax.ShapeDtypeStruct((ATTN_WIDTH, t), BF16), jax.ShapeDtypeStruct((KV_WIDTH, t), BF16),
                   jax.ShapeDtypeStruct((KV_WIDTH, t), BF16), jax.ShapeDtypeStruct((N_Q_HEADS, 1), F32)],
        scratch_shapes=[pltpu.VMEM((KV_WIDTH, tw), F32), pltpu.VMEM((KV_WIDTH, tw), F32),
                        pltpu.VMEM((N_Q_HEADS, BLOCK), F32)],
        compiler_params=_params(("arbitrary",)),
    )(sinks, qkv_t, qkv_t, qkv_t, qkv_t, qkv_t, dy_t)
    return outs


GELU_C = 0.7978845608028654
GELU_A = 0.044715


def _gelu(x):
    return 0.5 * x * (1.0 + jnp.tanh(GELU_C * (x + GELU_A * x * x * x)))


def _gelu_grad(x):
    th = jnp.tanh(GELU_C * (x + GELU_A * x * x * x))
    return 0.5 * (1.0 + th) + 0.5 * x * (1.0 - th * th) * GELU_C * (1.0 + 3.0 * GELU_A * x * x)


def _gmlp_parts(zg, lng, lnb):
    z = _gelu(zg)
    u = z[:GMLP_WIDTH, :]
    vv = z[GMLP_WIDTH:, :]
    mu = jnp.mean(vv, axis=0, keepdims=True)
    xc = vv - mu
    rstd = lax.rsqrt(jnp.mean(xc * xc, axis=0, keepdims=True) + EPS)
    xhat = xc * rstd
    return u, xhat, rstd, xhat * lng + lnb


def _row_sums(acc):
    return jnp.sum(acc.T, axis=0, keepdims=True)


def _causal(w):
    row = lax.broadcasted_iota(jnp.int32, (BLOCK, BLOCK), 0)
    col = lax.broadcasted_iota(jnp.int32, (BLOCK, BLOCK), 1)
    return jnp.where(col <= row, w, 0.0)


GMLP_CHUNKS_PER_STEP = 8


def _stack_chunks(v, nc):
    return jnp.concatenate([v[:, c * BLOCK:(c + 1) * BLOCK] for c in range(nc)], axis=0)


def _unstack_chunks(v, nc):
    rows = v.shape[0] // nc
    return jnp.concatenate([v[c * rows:(c + 1) * rows, :] for c in range(nc)], axis=1)


def gmlp_fwd(zg_t, lng, lnb, w_s, b_s, name):
    t = zg_t.shape[1]
    nc = min(GMLP_CHUNKS_PER_STEP, t // BLOCK)
    tw = nc * BLOCK

    def body(zg_ref, lng_ref, lnb_ref, w_ref, bs_ref, y_ref):
        u, _, _, vn = _gmlp_parts(zg_ref[...].astype(F32), lng_ref[...], lnb_ref[...])
        for g in range(GMLP_GROUPS):
            gs = slice(g * GMLP_GROUP_DIM, (g + 1) * GMLP_GROUP_DIM)
            wc = _causal(w_ref[g]).astype(BF16)
            mixed = _dg(_stack_chunks(vn[gs, :].astype(BF16), nc), wc, NT) + bs_ref[g:g + 1, :]
            y_ref[gs, :] = (u[gs, :] * _unstack_chunks(mixed, nc)).astype(BF16)

    return pl.pallas_call(
        body, name=name, grid=(t // tw,),
        in_specs=[pl.BlockSpec((2 * GMLP_WIDTH, tw), lambda n: (0, n)), _resident((GMLP_WIDTH, 1)),
                  _resident((GMLP_WIDTH, 1)), _resident((GMLP_GROUPS, BLOCK, BLOCK)), _resident((GMLP_GROUPS, BLOCK))],
        out_specs=pl.BlockSpec((GMLP_WIDTH, tw), lambda n: (0, n)),
        out_shape=jax.ShapeDtypeStruct((GMLP_WIDTH, t), BF16),
        compiler_params=_params(("arbitrary",)),
    )(zg_t, lng, lnb, w_s, b_s)


def gmlp_bwd(zg_t, dy_t, lng, lnb, w_s, b_s, name):
    t = zg_t.shape[1]
    nc = min(GMLP_CHUNKS_PER_STEP, t // BLOCK)
    tw = nc * BLOCK
    nt = t // tw

    def body(zg_ref, dy_ref, lng_ref, lnb_ref, w_ref, bs_ref, dzg_ref, dw_ref, dbs_ref, dlng_ref, dlnb_ref,
             gacc_ref, bacc_ref):
        n = pl.program_id(0)

        @pl.when(n == 0)
        def _():
            dw_ref[...] = jnp.zeros_like(dw_ref)
            dbs_ref[...] = jnp.zeros_like(dbs_ref)
            gacc_ref[...] = jnp.zeros_like(gacc_ref)
            bacc_ref[...] = jnp.zeros_like(bacc_ref)

        zg = zg_ref[...].astype(F32)
        lng_v = lng_ref[...]
        u, xhat, rstd, vn = _gmlp_parts(zg, lng_v, lnb_ref[...])
        dy = dy_ref[...].astype(F32)
        dvn_parts = []
        for g in range(GMLP_GROUPS):
            gs = slice(g * GMLP_GROUP_DIM, (g + 1) * GMLP_GROUP_DIM)
            wc = _causal(w_ref[g]).astype(BF16)
            vn_s = _stack_chunks(vn[gs, :].astype(BF16), nc)
            mixed = _unstack_chunks(_dg(vn_s, wc, NT) + bs_ref[g:g + 1, :], nc)
            dy_g = dy[gs, :]
            dmixed = dy_g * u[gs, :]
            dm_s = _stack_chunks(dmixed.astype(BF16), nc)
            dzg_ref[gs, :] = (dy_g * mixed * _gelu_grad(zg[gs, :])).astype(BF16)
            dw_ref[g] += _causal(_dg(dm_s, vn_s, TN))
            dbs_ref[g:g + 1, :] += _lane_fold(jnp.sum(dmixed, axis=0, keepdims=True))
            dvn_parts.append(_unstack_chunks(_dot(dm_s, wc), nc))
        dvn = jnp.concatenate(dvn_parts, axis=0)
        gacc_ref[...] += _lane_fold(dvn * xhat)
        bacc_ref[...] += _lane_fold(dvn)
        dxhat = dvn * lng_v
        m1 = jnp.mean(dxhat, axis=0, keepdims=True)
        m2 = jnp.mean(dxhat * xhat, axis=0, keepdims=True)
        dvv = rstd * (dxhat - m1 - xhat * m2)
        dzg_ref[GMLP_WIDTH:, :] = (dvv * _gelu_grad(zg[GMLP_WIDTH:, :])).astype(BF16)

        @pl.when(n == nt - 1)
        def _():
            dlng_ref[...] = _row_sums(gacc_ref[...])
            dlnb_ref[...] = _row_sums(bacc_ref[...])

    const2 = lambda n: (0, 0)
    return pl.pallas_call(
        body, name=name, grid=(nt,),
        in_specs=[pl.BlockSpec((2 * GMLP_WIDTH, tw), lambda n: (0, n)), pl.BlockSpec((GMLP_WIDTH, tw), lambda n: (0, n)),
                  _resident((GMLP_WIDTH, 1)), _resident((GMLP_WIDTH, 1)),
                  _resident((GMLP_GROUPS, BLOCK, BLOCK)), _resident((GMLP_GROUPS, BLOCK))],
        out_specs=[pl.BlockSpec((2 * GMLP_WIDTH, tw), lambda n: (0, n)),
                   pl.BlockSpec((GMLP_GROUPS, BLOCK, BLOCK), lambda n: (0, 0, 0)),
                   pl.BlockSpec((GMLP_GROUPS, BLOCK), const2),
                   pl.BlockSpec((1, GMLP_WIDTH), const2), pl.BlockSpec((1, GMLP_WIDTH), const2)],
        out_shape=[jax.ShapeDtypeStruct((2 * GMLP_WIDTH, t), BF16),
                   jax.ShapeDtypeStruct((GMLP_GROUPS, BLOCK, BLOCK), F32),
                   jax.ShapeDtypeStruct((GMLP_GROUPS, BLOCK), F32),
                   jax.ShapeDtypeStruct((1, GMLP_WIDTH), F32), jax.ShapeDtypeStruct((1, GMLP_WIDTH), F32)],
        scratch_shapes=[pltpu.VMEM((GMLP_WIDTH, BLOCK), F32), pltpu.VMEM((GMLP_WIDTH, BLOCK), F32)],
        compiler_params=_params(("arbitrary",)),
    )(zg_t, dy_t, lng, lnb, w_s, b_s)


def _rmsnorm_cols(y, gain_col):
    r = lax.rsqrt(jnp.mean(y * y, axis=0, keepdims=True) + EPS)
    n = y * r
    return n, r, n * gain_col


def mix_out_fwd(x, ya_t, yg_t, gao, ggo, wout, b_out, name):
    t, d = x.shape
    tm = _token_tile(t)

    def body(x_ref, ya_ref, yg_ref, gao_ref, ggo_ref, w_ref, b_ref, o_ref):
        _, _, ya = _rmsnorm_cols(ya_ref[...].astype(F32), gao_ref[...])
        _, _, yg = _rmsnorm_cols(yg_ref[...].astype(F32), ggo_ref[...])
        o_ref[...] = (x_ref[...] + _dg(ya.astype(BF16), w_ref[:ATTN_WIDTH, :], TN)
                      + _dg(yg.astype(BF16), w_ref[ATTN_WIDTH:, :], TN) + b_ref[...])

    return pl.pallas_call(
        body, name=name, grid=(t // tm,),
        in_specs=[pl.BlockSpec((tm, d), lambda i: (i, 0)), pl.BlockSpec((ATTN_WIDTH, tm), lambda i: (0, i)),
                  pl.BlockSpec((GMLP_WIDTH, tm), lambda i: (0, i)), _resident((ATTN_WIDTH, 1)), _resident((GMLP_WIDTH, 1)),
                  _resident((ATTN_WIDTH + GMLP_WIDTH, d)), _resident((1, d))],
        out_specs=pl.BlockSpec((tm, d), lambda i: (i, 0)),
        out_shape=jax.ShapeDtypeStruct((t, d), F32),
        compiler_params=_params(("arbitrary",)),
    )(x, ya_t, yg_t, gao, ggo, wout, b_out)


def _lane_fold(v):
    out = v[:, :LANES]
    for j in range(1, v.shape[1] // LANES):
        out = out + v[:, j * LANES:(j + 1) * LANES]
    return out


def mix_out_bwd(dx, ya_t, yg_t, gao, ggo, wout, name, deps=()):
    t, d = dx.shape
    tm = _token_tile(t)
    nt = t // tm

    def body(dx_ref, ya_ref, yg_ref, gao_ref, ggo_ref, w_ref, dya_ref, dyg_ref, y_ref, dgao_ref, dggo_ref, db_ref,
             acc_ref):
        i = pl.program_id(0)

        @pl.when(i == 0)
        def _():
            acc_ref[...] = jnp.zeros_like(acc_ref)
            db_ref[...] = jnp.zeros_like(db_ref)

        dxv = dx_ref[...]
        db_ref[...] += jnp.sum(dxv, axis=0, keepdims=True)
        dxb = dxv.astype(BF16)
        for part, (src_ref, gain_ref, dst_ref) in enumerate(((ya_ref, gao_ref, dya_ref), (yg_ref, ggo_ref, dyg_ref))):
            rows = slice(part * ATTN_WIDTH, (part + 1) * ATTN_WIDTH)
            gain = gain_ref[...]
            nrm, r, yn = _rmsnorm_cols(src_ref[...].astype(F32), gain)
            y_ref[rows, :] = yn.astype(BF16)
            dy = _dg(w_ref[rows, :], dxb, NT)
            acc_ref[rows, :] += _lane_fold(dy * nrm)
            dn = dy * gain
            dst_ref[...] = (r * (dn - nrm * jnp.mean(dn * nrm, axis=0, keepdims=True))).astype(BF16)

        @pl.when(i == nt - 1)
        def _():
            dgao_ref[...] = _row_sums(acc_ref[:ATTN_WIDTH, :])
            dggo_ref[...] = _row_sums(acc_ref[ATTN_WIDTH:, :])

    const2 = lambda i: (0, 0)
    feat = lambda w: pl.BlockSpec((w, tm), lambda i: (0, i))
    return pl.pallas_call(
        _drop_deps(body, 6, len(deps)), name=name, grid=(nt,),
        in_specs=[pl.BlockSpec((tm, d), lambda i: (i, 0)), feat(ATTN_WIDTH), feat(GMLP_WIDTH),
                  _resident((ATTN_WIDTH, 1)), _resident((GMLP_WIDTH, 1)), _resident((ATTN_WIDTH + GMLP_WIDTH, d))]
        + [ANY_SPEC] * len(deps),
        out_specs=[feat(ATTN_WIDTH), feat(GMLP_WIDTH), feat(ATTN_WIDTH + GMLP_WIDTH),
                   pl.BlockSpec((1, ATTN_WIDTH), const2), pl.BlockSpec((1, GMLP_WIDTH), const2),
                   pl.BlockSpec((1, d), const2)],
        out_shape=[jax.ShapeDtypeStruct((ATTN_WIDTH, t), BF16), jax.ShapeDtypeStruct((GMLP_WIDTH, t), BF16),
                   jax.ShapeDtypeStruct((ATTN_WIDTH + GMLP_WIDTH, t), BF16),
                   jax.ShapeDtypeStruct((1, ATTN_WIDTH), F32), jax.ShapeDtypeStruct((1, GMLP_WIDTH), F32),
                   jax.ShapeDtypeStruct((1, d), F32)],
        scratch_shapes=[pltpu.VMEM((ATTN_WIDTH + GMLP_WIDTH, LANES), F32)],
        compiler_params=_params(("arbitrary",)),
    )(dx, ya_t, yg_t, gao, ggo, wout, *deps)


def mix_in_bwd(x, dxo, gain, dq_t, dk_t, dv_t, dzg_t, win_t, name):
    t, d = x.shape
    w = win_t.shape[0]
    tm = _token_tile(t)
    nt = t // tm
    parts = ((0, ATTN_WIDTH), (ATTN_WIDTH, KV_WIDTH), (ATTN_WIDTH + KV_WIDTH, KV_WIDTH), (QKV_WIDTH, w - QKV_WIDTH))

    def body(x_ref, dxo_ref, g_ref, dq_ref, dk_ref, dv_ref, dzg_ref, w_ref, dx_ref, h_ref, dg_ref, db_ref, acc_ref):
        i = pl.program_id(0)

        @pl.when(i == 0)
        def _():
            acc_ref[...] = jnp.zeros_like(acc_ref)
            dg_ref[...] = jnp.zeros_like(dg_ref)

        xv = x_ref[...]
        gain_v = g_ref[...]
        r = lax.rsqrt(jnp.mean(xv * xv, axis=-1, keepdims=True) + EPS)
        h_ref[...] = (xv * r * gain_v).astype(BF16)
        dh = jnp.zeros((tm, d), F32)
        for (off, size), src_ref in zip(parts, (dq_ref, dk_ref, dv_ref, dzg_ref)):
            dp = src_ref[...]
            acc_ref[off:off + size, :] += _lane_fold(dp.astype(F32))
            dh = dh + _dg(dp, w_ref[off:off + size, :], TN)
        dx, dgain = _rmsnorm_bwd_rows(xv, r, gain_v, dh)
        dx_ref[...] = dxo_ref[...] + dx
        dg_ref[...] += dgain

        @pl.when(i == nt - 1)
        def _():
            db_ref[...] = _row_sums(acc_ref[...])

    const2 = lambda i: (0, 0)
    tok = lambda: pl.BlockSpec((tm, d), lambda i: (i, 0))
    feat = lambda rows: pl.BlockSpec((rows, tm), lambda i: (0, i))
    return pl.pallas_call(
        body, name=name, grid=(nt,),
        in_specs=[tok(), tok(), _resident((1, d)), feat(ATTN_WIDTH), feat(KV_WIDTH), feat(KV_WIDTH),
                  feat(w - QKV_WIDTH), _resident((w, d))],
        out_specs=[tok(), tok(), pl.BlockSpec((1, d), const2), pl.BlockSpec((1, w), const2)],
        out_shape=[jax.ShapeDtypeStruct((t, d), F32), jax.ShapeDtypeStruct((t, d), BF16),
                   jax.ShapeDtypeStruct((1, d), F32), jax.ShapeDtypeStruct((1, w), F32)],
        scratch_shapes=[pltpu.VMEM((w, LANES), F32)],
        compiler_params=_params(("arbitrary",)),
    )(x, dxo, gain, dq_t, dk_t, dv_t, dzg_t, win_t)


def _adamw_math(w, g, m, v):
    m = ADAM_B1 * m + (1.0 - ADAM_B1) * g
    v = ADAM_B2 * v + (1.0 - ADAM_B2) * (g * g)
    m_hat = m / (1.0 - ADAM_B1 ** ADAM_STEP)
    v_hat = v / (1.0 - ADAM_B2 ** ADAM_STEP)
    delta = -ADAM_LR * (m_hat / (jnp.sqrt(v_hat) + ADAM_EPS) + ADAM_WD * w)
    return delta, m, v


def adamw_many(ws, gs, ms, vs, name):
    n = len(ws)

    def body(*refs):
        w_refs, g_refs, m_refs, v_refs = refs[:n], refs[n:2 * n], refs[2 * n:3 * n], refs[3 * n:4 * n]
        d_refs, mo_refs, vo_refs = refs[4 * n:5 * n], refs[5 * n:6 * n], refs[6 * n:]
        for k in range(n):
            delta, mn, vn = _adamw_math(w_refs[k][...], g_refs[k][...], m_refs[k][...], v_refs[k][...])
            d_refs[k][...] = delta
            mo_refs[k][...] = mn
            vo_refs[k][...] = vn

    def whole(a):
        zeros = (0,) * a.ndim
        return pl.BlockSpec(a.shape, lambda i: zeros)

    outs = pl.pallas_call(
        body, name=name, grid=(1,),
        in_specs=[whole(a) for a in ws] * 4, out_specs=[whole(a) for a in ws] * 3,
        out_shape=[jax.ShapeDtypeStruct(a.shape, F32) for a in ws] * 3,
        compiler_params=_params(("arbitrary",)),
    )(*ws, *gs, *ms, *vs)
    return outs[:n], outs[n:2 * n], outs[2 * n:]


def sum_adamw(landing, grad3d, me, w, m, v, name):
    nd, rows, cols = landing.shape
    tr = rows // 2

    def body(me_ref, l_ref, own_ref, w_ref, m_ref, v_ref, g_ref, d_ref, mo_ref, vo_ref):
        g = own_ref[...].astype(F32)
        for j in range(nd):
            g = g + l_ref[j].astype(F32)
        delta, mn, vn = _adamw_math(w_ref[...], g, m_ref[...], v_ref[...])
        g_ref[...] = g
        d_ref[...] = delta
        mo_ref[...] = mn
        vo_ref[...] = vn

    spec = lambda: pl.BlockSpec((tr, cols), lambda i, me_ref: (i, 0))
    shape = jax.ShapeDtypeStruct((rows, cols), F32)
    return pl.pallas_call(
        body, name=name,
        grid_spec=pltpu.PrefetchScalarGridSpec(
            num_scalar_prefetch=1, grid=(rows // tr,),
            in_specs=[pl.BlockSpec((nd, tr, cols), lambda i, me_ref: (0, i, 0)),
                      pl.BlockSpec((None, tr, cols), lambda i, me_ref: (me_ref[0], i, 0)), spec(), spec(), spec()],
            out_specs=[spec() for _ in range(4)]),
        out_shape=[shape] * 4,
        compiler_params=_params(("arbitrary",)),
    )(me, landing, grad3d, w, m, v)


def _mesh_place():
    x, y, c = lax.axis_index("x"), lax.axis_index("y"), lax.axis_index("c")
    return x, y, c, 4 * x + 2 * y + c


def _peer(x, y, c, k):
    return (x ^ ((k >> 2) & 1), y ^ ((k >> 1) & 1), c ^ (k & 1))


def prep_shards(mats, me, name):
    na = len(mats)

    def body(me_ref, *refs):
        in_refs, shard_refs, land_refs = refs[:na], refs[na:2 * na], refs[2 * na:]
        for a in range(na):
            val = in_refs[a][...].astype(BF16)
            shard_refs[a][...] = val
            land_refs[a][...] = val

    whole = lambda mat: pl.BlockSpec(mat.shape, lambda i, me_ref: (0, 0))
    outs = pl.pallas_call(
        body, name=name,
        grid_spec=pltpu.PrefetchScalarGridSpec(
            num_scalar_prefetch=1, grid=(1,),
            in_specs=[whole(mat) for mat in mats],
            out_specs=[whole(mat) for mat in mats]
            + [pl.BlockSpec((None,) + mat.shape, lambda i, me_ref: (me_ref[0], 0, 0)) for mat in mats]),
        out_shape=[jax.ShapeDtypeStruct(mat.shape, BF16) for mat in mats]
        + [jax.ShapeDtypeStruct((N_DEV,) + mat.shape, BF16) for mat in mats],
        compiler_params=_params(("arbitrary",)),
    )(me, *mats)
    return outs[:na], outs[na:]


def allgather_rows(shards, lands, name):
    na = n_gather = len(shards)

    def body(*refs):
        in_refs, out_refs = refs[:na], refs[2 * na:3 * na]
        send_sems, recv_sems = refs[3 * na:]
        x, y, c, me = _mesh_place()

        def copy(sem, a, origin, src, k):
            return pltpu.make_async_remote_copy(
                src_ref=src, dst_ref=out_refs[a].at[origin], send_sem=send_sems.at[sem, a],
                recv_sem=recv_sems.at[sem, a], device_id=_peer(x, y, c, k), device_id_type=pl.DeviceIdType.MESH)

        kx, ky = 4, 2

        def recv(sem, a, origin):
            copy(sem, a, origin, in_refs[a], 1).wait_recv()

        def landed(a, origin):
            return out_refs[a].at[origin]

        @pl.when(c == 1)
        def _():
            sends = []

            def send(sem, a, origin, src, k):
                cp = copy(sem, a, origin, src, k)
                cp.start()
                sends.append(cp)

            for a in range(n_gather):
                send(1, a, me, in_refs[a], kx)
                send(2, a, me, in_refs[a], ky)
                send(0, a, me, in_refs[a], 1)
            for a in range(n_gather):
                recv(0, a, me ^ 1)
                send(3, a, me ^ 1, landed(a, me ^ 1), kx)
                send(4, a, me ^ 1, landed(a, me ^ 1), ky)
            for a in range(n_gather):
                recv(1, a, me ^ kx)
                send(5, a, me ^ kx, landed(a, me ^ kx), ky)
                send(7, a, me ^ kx, landed(a, me ^ kx), 1)
                recv(2, a, me ^ ky)
                send(8, a, me ^ ky, landed(a, me ^ ky), 1)
            for a in range(n_gather):
                recv(3, a, me ^ kx ^ 1)
                send(9, a, me ^ kx ^ 1, landed(a, me ^ kx ^ 1), 1)
                recv(4, a, me ^ ky ^ 1)
                send(6, a, me ^ ky ^ 1, landed(a, me ^ ky ^ 1), kx)
                send(10, a, me ^ ky ^ 1, landed(a, me ^ ky ^ 1), 1)
            for a in range(n_gather):
                recv(5, a, me ^ 6)
                send(11, a, me ^ 6, landed(a, me ^ 6), 1)
                recv(6, a, me ^ 7)
                send(12, a, me ^ 7, landed(a, me ^ 7), 1)
            for cp in sends:
                cp.wait_send()

        @pl.when(c == 0)
        def _():
            north = me ^ 1
            own = [copy(0, a, me, in_refs[a], 1) for a in range(n_gather)]
            for cp in own:
                cp.start()
            for a in range(n_gather):
                recv(0, a, north)
                for sem, origin in ((7, north ^ kx), (8, north ^ ky), (9, north ^ kx ^ 1), (10, north ^ ky ^ 1),
                                    (11, north ^ 6), (12, north ^ 7)):
                    recv(sem, a, origin)
            for cp in own:
                cp.wait_send()

    return pl.pallas_call(
        body, name=name,
        in_specs=[ANY_SPEC] * (2 * na), out_specs=[ANY_SPEC] * na,
        out_shape=[jax.ShapeDtypeStruct(land.shape, land.dtype) for land in lands],
        input_output_aliases={na + a: a for a in range(na)},
        scratch_shapes=[pltpu.SemaphoreType.DMA((N_GATHER_ROLES, n_gather)),
                        pltpu.SemaphoreType.DMA((N_GATHER_ROLES, n_gather))],
    )(*shards, *lands)


HBM_SPEC = pl.BlockSpec(memory_space=pltpu.HBM)
SEM_SPEC = pl.BlockSpec(memory_space=pltpu.SEMAPHORE)
SIDE_EFFECT = pltpu.SideEffectType.DATAFLOW_SIDE_EFFECTING


def _hbm(v):
    return pltpu.with_memory_space_constraint(v, pltpu.HBM)


def gather_slices(src, land, me, k):
    return src, land.at[me], land.at[me ^ k]


def exchange_slices(src, land, me, k):
    return src.at[me ^ k], land.at[k - 1], land.at[k - 1]


def split_start(groups, slices, deps, name):
    sizes = [len(srcs) for srcs, _ in groups]
    flat_src = [s for srcs, _ in groups for s in srcs]
    flat_land = [l for _, lands in groups for l in lands]
    na, ng, nd = len(flat_src), len(groups), len(deps)

    def body(*refs):
        src_refs, land_refs = refs[:na], refs[na:2 * na]
        sem_refs = refs[2 * na + nd:2 * na + nd + 2 * ng]
        token_ref = refs[-1]
        x, y, c, me = _mesh_place()
        a0 = 0
        for gi, size in enumerate(sizes):
            send_sems, recv_sems = sem_refs[2 * gi], sem_refs[2 * gi + 1]
            for k in range(1, N_DEV):
                for j in range(size):
                    src, dst, _ = slices(src_refs[a0 + j], land_refs[a0 + j], me, k)
                    pltpu.make_async_remote_copy(
                        src_ref=src, dst_ref=dst, send_sem=send_sems.at[(k - 1) * size + j],
                        recv_sem=recv_sems.at[(k - 1) * size + j],
                        device_id=_peer(x, y, c, k), device_id_type=pl.DeviceIdType.MESH).start()
            a0 += size
        token_ref[...] = jnp.zeros_like(token_ref)

    sems = [pltpu.SemaphoreType.DMA(((N_DEV - 1) * size,)) for size in sizes for _ in range(2)]
    thru = [pltpu.HBM(v.shape, v.dtype) for v in flat_src + flat_land]
    outs = pl.pallas_call(
        body, name=name,
        in_specs=[HBM_SPEC] * (2 * na) + [ANY_SPEC] * nd,
        out_specs=[SEM_SPEC] * (2 * ng) + [HBM_SPEC] * (2 * na) + [pl.BlockSpec(memory_space=pltpu.VMEM)],
        out_shape=sems + thru + [jax.ShapeDtypeStruct((8, LANES), F32)],
        input_output_aliases={i: 2 * ng + i for i in range(2 * na)},
        compiler_params=pltpu.CompilerParams(has_side_effects=SIDE_EFFECT),
    )(*[_hbm(v) for v in flat_src + flat_land], *deps)
    started, a0 = [], 0
    for gi, size in enumerate(sizes):
        srcs = outs[2 * ng + a0:2 * ng + a0 + size]
        lands = outs[2 * ng + na + a0:2 * ng + na + a0 + size]
        started.append((outs[2 * gi], outs[2 * gi + 1], list(srcs), list(lands)))
        a0 += size
    return started, outs[-1]


def split_wait(started, slices, after, name):
    send_sems, recv_sems, srcs, lands = started
    na = len(srcs)

    def body(*refs):
        src_refs, land_refs = refs[:na], refs[na:2 * na]
        send_ref, recv_ref = refs[2 * na], refs[2 * na + 1]
        x, y, c, me = _mesh_place()
        for k in range(1, N_DEV):
            for j in range(na):
                src, _, got = slices(src_refs[j], land_refs[j], me, k)
                cp = pltpu.make_async_remote_copy(
                    src_ref=src, dst_ref=got, send_sem=send_ref.at[(k - 1) * na + j], recv_sem=recv_ref.at[(k - 1) * na + j],
                    device_id=_peer(x, y, c, k), device_id_type=pl.DeviceIdType.MESH)
                cp.wait_send()
                cp.wait_recv()

    outs = pl.pallas_call(
        body, name=name,
        in_specs=[HBM_SPEC] * (2 * na) + [SEM_SPEC, SEM_SPEC] + [ANY_SPEC] * len(after),
        out_specs=[HBM_SPEC] * (2 * na),
        out_shape=[pltpu.HBM(v.shape, v.dtype) for v in srcs + lands],
        input_output_aliases={i: i for i in range(2 * na)},
        compiler_params=pltpu.CompilerParams(has_side_effects=SIDE_EFFECT),
    )(*srcs, *lands, send_sems, recv_sems, *after)
    return list(outs[:na]), list(outs[na:])


def place_own(pack, me, name):
    rows, cols = pack.shape

    def body(me_ref, p_ref, o_ref):
        o_ref[...] = p_ref[...]

    return pl.pallas_call(
        body, name=name,
        grid_spec=pltpu.PrefetchScalarGridSpec(
            num_scalar_prefetch=1, grid=(1,),
            in_specs=[pl.BlockSpec((rows, cols), lambda i, me_ref: (0, 0))],
            out_specs=pl.BlockSpec((None, rows, cols), lambda i, me_ref: (me_ref[0], 0, 0))),
        out_shape=jax.ShapeDtypeStruct((N_DEV, rows, cols), F32),
        compiler_params=_params(("arbitrary",)),
    )(me, pack)


def sum_slots(slots, name):
    nd, rows, cols = slots.shape

    def body(s_ref, o_ref):
        acc = s_ref[0]
        for j in range(1, nd):
            acc = acc + s_ref[j]
        o_ref[...] = acc

    return pl.pallas_call(
        body, name=name, grid=(1,),
        in_specs=[pl.BlockSpec((nd, rows, cols), lambda i: (0, 0, 0))],
        out_specs=pl.BlockSpec((rows, cols), lambda i: (0, 0)),
        out_shape=jax.ShapeDtypeStruct((rows, cols), F32),
        compiler_params=_params(("arbitrary",)),
    )(slots)


def _pack_rows(size):
    rows = -(-size // LANES)
    return -(-rows // 8) * 8


def pack_small(parts):
    out = []
    for p in parts:
        flat = p.reshape(-1).astype(F32)
        rows = _pack_rows(flat.shape[0])
        out.append(jnp.pad(flat, (0, rows * LANES - flat.shape[0])).reshape(rows, LANES))
    return jnp.concatenate(out, axis=0)


def unpack_small(pack, shapes):
    out, off = [], 0
    for shp in shapes:
        size = 1
        for s in shp:
            size *= s
        rows = _pack_rows(size)
        out.append(pack[off:off + rows].reshape(-1)[:size].reshape(shp))
        off += rows
    return out


def local_step(x, target, small, get_w, put_g, put_small):
    col = lambda v: v.reshape(-1, 1)
    (wg1, wu1), deps = get_w(("wg1", "wu1"), ())
    sil1, gp1, s1, h1 = ffn_fwd(x, small["ffn1_norm_g"], wg1, wu1, None, "ffn1_fwd", deps)
    (wd1, win, wout), _ = get_w(("wd1", "win", "wout"), (s1,))
    qkv_t, zg_t, x1 = mix_in_fwd(x, small["mix_norm_g"], win, col(small["b_in"]), "mix_in_fwd", ffn_tail=(s1, wd1))
    ya_t = attn_fwd(qkv_t, small["attn_sinks"], "attn_fwd")
    lng, lnb = col(small["gmlp_ln_g"]), col(small["gmlp_ln_b"])
    w_s, b_s = small["gmlp_w_s"][0], small["gmlp_b_s"][0]
    yg_t = gmlp_fwd(zg_t, lng, lnb, w_s, b_s, "gmlp_fwd")
    gao, ggo = col(small["attn_out_norm_g"]), col(small["gmlp_out_norm_g"])
    x2 = mix_out_fwd(x1, ya_t, yg_t, gao, ggo, wout, small["b_out"], "mix_out_fwd")
    (wg2, wu2, wd2), _ = get_w(("wg2", "wu2", "wd2"), (x2,))
    gs = {}
    dx3, sil2, gp2, s2, h2f, gs["final_norm_g"], loss = ffn_fwd(
        x2, small["ffn2_norm_g"], wg2, wu2, wd2, "ffn2_fwd", head=(target, small["final_norm_g"].reshape(1, -1)))

    dx2, da, db, df, gs["ffn2_norm_g"] = ffn_bwd(x2, dx3, small["ffn2_norm_g"], sil2, gp2, wg2, wu2, wd2, "ffn2_bwd")
    deps = put_g({"wg2": grad_tn(da, h2f, 1.0, "ffn2_dwg"), "wu2": grad_tn(db, h2f, 1.0, "ffn2_dwu"),
                  "wd2": grad_tn(s2, df, 1.0, "ffn2_dwd")})

    dya_t, dyg_t, y_t, dgao, dggo, gs["b_out"] = mix_out_bwd(dx2, ya_t, yg_t, gao, ggo, wout, "mix_out_bwd", deps)
    gs["attn_out_norm_g"], gs["gmlp_out_norm_g"] = dgao.reshape(1, -1), dggo.reshape(1, -1)
    g_wout = grad_nn([y_t], dx2, "dwout")
    dzg_t, dws, dbs, dlng, dlnb = gmlp_bwd(zg_t, dyg_t, lng, lnb, w_s, b_s, "gmlp_bwd")
    gs["gmlp_w_s"], gs["gmlp_b_s"] = dws[None], dbs[None]
    gs["gmlp_ln_g"], gs["gmlp_ln_b"] = dlng.reshape(1, -1), dlnb.reshape(1, -1)
    dq_t, dk_t, dv_t, dsinks = attn_bwd(qkv_t, dya_t, small["attn_sinks"], "attn_bwd")
    gs["attn_sinks"] = dsinks.reshape(1, -1)
    dx1, h2, gs["mix_norm_g"], dbin = mix_in_bwd(x1, dx2, small["mix_norm_g"], dq_t, dk_t, dv_t, dzg_t, win,
                                                 "mix_in_bwd")
    gs["b_in"] = dbin.reshape(1, -1)
    deps = put_g({"wout": g_wout, "win": grad_nn([dq_t, dk_t, dv_t, dzg_t], h2, "dwin")})

    dx0, da, db, df, gs["ffn1_norm_g"] = ffn_bwd(x, dx1, small["ffn1_norm_g"], sil1, gp1, wg1, wu1, wd1, "ffn1_bwd",
                                             deps)
    deps = put_small(gs, loss)
    deps = put_g({"wg1": grad_tn(da, h1, 1.0, "ffn1_dwg", deps)})
    deps = put_g({"wu1": grad_tn(db, h1, 1.0, "ffn1_dwu", deps)})
    put_g({"wd1": grad_tn(s1, df, 1.0, "ffn1_dwd", deps)})
    return dx0


SMALL_NAMES = ["ffn1_norm_g", "mix_norm_g", "b_in", "attn_sinks", "gmlp_ln_g", "gmlp_ln_b", "gmlp_w_s", "gmlp_b_s",
               "attn_out_norm_g", "gmlp_out_norm_g", "b_out", "ffn2_norm_g", "final_norm_g"]
BIG_NAMES = {"wg1": ("ffn1_w_gate", True), "wu1": ("ffn1_w_up", True), "wd1": ("ffn1_w_down", False),
             "win": ("w_in", True), "wout": ("w_out", False),
             "wg2": ("ffn2_w_gate", True), "wu2": ("ffn2_w_up", True), "wd2": ("ffn2_w_down", False)}
WEIGHT_ORDER = ["ffn1_norm_g", "ffn1_w_gate", "ffn1_w_up", "ffn1_w_down", "mix_norm_g", "w_in", "b_in", "attn_sinks",
                "gmlp_ln_g", "gmlp_ln_b", "gmlp_w_s", "gmlp_b_s", "attn_out_norm_g", "gmlp_out_norm_g", "w_out",
                "b_out", "ffn2_norm_g", "ffn2_w_gate", "ffn2_w_up", "ffn2_w_down", "final_norm_g"]


def kernel(x, ffn1_norm_g, ffn1_w_gate, ffn1_w_up, ffn1_w_down, mix_norm_g, w_in, b_in, attn_sinks, gmlp_ln_g, gmlp_ln_b, gmlp_w_s, gmlp_b_s, attn_out_norm_g, gmlp_out_norm_g, w_out, b_out, ffn2_norm_g, ffn2_w_gate, ffn2_w_up, ffn2_w_down, final_norm_g, loss_target, m_ffn1_norm_g, m_ffn1_w_gate, m_ffn1_w_up, m_ffn1_w_down, m_mix_norm_g, m_w_in, m_b_in, m_attn_sinks, m_gmlp_ln_g, m_gmlp_ln_b, m_gmlp_w_s, m_gmlp_b_s, m_attn_out_norm_g, m_gmlp_out_norm_g, m_w_out, m_b_out, m_ffn2_norm_g, m_ffn2_w_gate, m_ffn2_w_up, m_ffn2_w_down, m_final_norm_g, v_ffn1_norm_g, v_ffn1_w_gate, v_ffn1_w_up, v_ffn1_w_down, v_mix_norm_g, v_w_in, v_b_in, v_attn_sinks, v_gmlp_ln_g, v_gmlp_ln_b, v_gmlp_w_s, v_gmlp_b_s, v_attn_out_norm_g, v_gmlp_out_norm_g, v_w_out, v_b_out, v_ffn2_norm_g, v_ffn2_w_gate, v_ffn2_w_up, v_ffn2_w_down, v_final_norm_g):
    args = dict(locals())
    w = {n: args[n] for n in WEIGHT_ORDER}
    m = {n: args["m_" + n] for n in WEIGHT_ORDER}
    v = {n: args["v_" + n] for n in WEIGHT_ORDER}

    d_model = x.shape[-1]
    flat = lambda g: g.reshape(-1, d_model)
    me = _mesh_place()[3].astype(jnp.int32).reshape(1)
    first = ("wg1", "wu1")
    later = (("wd1", "win", "wout"), ("wg2", "wu2", "wd2"))
    gathers, exchanges, last_token = {}, [], []

    order = list(first) + [k for grp in later for k in grp]
    mats = [w[BIG_NAMES[k][0]][0].T if BIG_NAMES[k][1] else w[BIG_NAMES[k][0]][0] for k in order]
    shard_list, land_list = prep_shards(mats, me, "prep_shards")
    shards, zones = dict(zip(order, shard_list)), dict(zip(order, land_list))

    def get_w(keys, after):
        if keys == first:
            full = allgather_rows([shards[k] for k in first], [zones[k] for k in first], "allgather_ffn1")
            started, token = split_start([([shards[k] for k in grp], [zones[k] for k in grp]) for grp in later],
                                         gather_slices, (full[0],), "gather_start")
            gathers.update(zip(later, started))
            return [flat(g) for g in full], (token,)
        _, lands = split_wait(gathers[keys], gather_slices, after, "gather_wait_" + keys[0])
        return [flat(land) for land in lands], ()

    def put_g(gb):
        keys = tuple(gb)
        g3 = [gb[k].reshape(N_DEV, -1, d_model) for k in keys]
        lands = [lax.empty((N_DEV - 1,) + g.shape[1:], BF16) for g in g3]
        started, token = split_start([(g3, lands)], exchange_slices, (), "exchange_start_" + keys[0])
        exchanges.append((keys, started[0]))
        last_token[:] = [token]
        return (token,)

    small_started = []

    def put_small(gs, loss):
        pack = pack_small([gs[n] for n in SMALL_NAMES] + [loss[:, :1]])
        started, token = split_start([([pack], [place_own(pack, me, "place_small")])], gather_slices, (), "small_start")
        small_started[:] = started
        return (token,)

    small = {n: w[n] for n in SMALL_NAMES}
    grad_x = local_step(x[0], loss_target[0], small, get_w, put_g, put_small)

    grads, deltas, new_m, new_v = {}, {}, {}, {}
    shapes = [w[n].shape for n in SMALL_NAMES]
    _, (slots,) = split_wait(small_started[0], gather_slices, tuple(last_token), "small_wait")
    total = sum_slots(slots, "sum_small")
    small_g = unpack_small(total, shapes + [(1, 1)])
    loss_total = small_g[-1].reshape(())
    d_, m_, v_ = adamw_many([w[n] for n in SMALL_NAMES], small_g[:-1], [m[n] for n in SMALL_NAMES],
                            [v[n] for n in SMALL_NAMES], "adamw_small")
    for n, g_, dd, mm, vv in zip(SMALL_NAMES, small_g[:-1], d_, m_, v_):
        grads[n], deltas[n], new_m[n], new_v[n] = g_, dd, mm, vv

    after = tuple(arr for n in SMALL_NAMES for arr in (grads[n], deltas[n], new_m[n], new_v[n]))
    for keys, started in exchanges:
        g3, lands = split_wait(started, exchange_slices, after, "exchange_wait_" + keys[0])
        for key, own, land in zip(keys, g3, lands):
            pname, transposed = BIG_NAMES[key]
            to_rows = (lambda p: p[0].T) if transposed else (lambda p: p[0])
            from_rows = (lambda r: r.T[None]) if transposed else (lambda r: r[None])
            g, d_, m_, v_ = sum_adamw(land, own, me, to_rows(w[pname]), to_rows(m[pname]), to_rows(v[pname]),
                                      "adamw_" + key)
            after = after + (g,)
            grads[pname], deltas[pname], new_m[pname], new_v[pname] = (from_rows(g), from_rows(d_), from_rows(m_),
                                                                        from_rows(v_))

    return (loss_total, grad_x[None], *[grads[n] for n in WEIGHT_ORDER], *[deltas[n] for n in WEIGHT_ORDER],
            *[new_m[n] for n in WEIGHT_ORDER], *[new_v[n] for n in WEIGHT_ORDER])
```

```python
import functools

import jax
import jax.numpy as jnp
from jax import lax
from jax.experimental import pallas as pl
from jax.experimental.pallas import tpu as pltpu

F32 = jnp.float32
BF16 = jnp.bfloat16

N_DEV = 8
N_Q_HEADS = 8
N_KV_HEADS = 2
HEAD_DIM = 64
ATTN_WIDTH = N_Q_HEADS * HEAD_DIM
KV_WIDTH = N_KV_HEADS * HEAD_DIM
QKV_WIDTH = ATTN_WIDTH + 2 * KV_WIDTH
BLOCK = 128
GMLP_GROUPS = 8
GMLP_GROUP_DIM = 64
GMLP_WIDTH = GMLP_GROUPS * GMLP_GROUP_DIM
EPS = 1e-6
FFN_RES = 0.5
ATTN_SCALE = HEAD_DIM ** -0.5
MASK_VALUE = -1e30

ADAM_LR = 0.001
ADAM_B1 = 0.9
ADAM_B2 = 0.999
ADAM_EPS = 1e-08
ADAM_WD = 0.01
ADAM_STEP = 10

V7X_VMEM_BYTES = 64 * 1024 * 1024
VMEM_LIMIT = 62 * 1024 * 1024
LANES = 128
MXU_WIDTH = 256
FFN_FWD_TOKENS = 512
FFN_BWD_TOKENS = 256

NT = (((1,), (1,)), ((), ()))
TN = (((0,), (0,)), ((), ()))


def _dot(a, b):
    return jnp.dot(a, b, preferred_element_type=F32)


def _dg(a, b, dims):
    return lax.dot_general(a, b, dims, preferred_element_type=F32)


def _params(semantics=None):
    return pltpu.CompilerParams(dimension_semantics=semantics, vmem_limit_bytes=VMEM_LIMIT)


def _resident(shape):
    zeros = (0,) * len(shape)
    return pl.BlockSpec(shape, lambda *_: zeros, pipeline_mode=pl.Buffered(1))


def _drop_deps(body, n_in, n_deps):
    return lambda *refs: body(*refs[:n_in], *refs[n_in + n_deps:])


ANY_SPEC = pl.BlockSpec(memory_space=pl.ANY)


TOKEN_TILE = 1024
GRAD_TOKENS_PER_STEP = 2048
GRAD_ROW_TILE_CAP = 1408
N_GATHER_ROLES = 13


def _token_tile(t):
    return min(t, TOKEN_TILE)


def _f_chunk(f):
    for c in (256, 128):
        if f % c == 0:
            return c
    return f


def _loss_head(xv, target, gain):
    r = lax.rsqrt(jnp.mean(xv * xv, axis=-1, keepdims=True) + EPS)
    err = xv * r * gain - target
    loss = 0.5 * jnp.sum(jnp.mean(err * err, axis=-1, keepdims=True), axis=0, keepdims=True)
    dx, dgain = _rmsnorm_bwd_rows(xv, r, gain, err * (1.0 / xv.shape[-1]))
    return dx, dgain, loss


def ffn_fwd(x, gain, wg_t, wu_t, wd, name, deps=(), head=None):
    t, d = x.shape
    f = wg_t.shape[0]
    tm = min(t, FFN_FWD_TOKENS)
    fc = _f_chunk(f)
    weights = [wg_t, wu_t] + ([] if wd is None else [wd])
    n_in = 2 + len(weights) + (0 if head is None else 2)
    n_x = 0 if wd is None else 1

    def body(x_ref, g_ref, wg_ref, wu_ref, *rest):
        sil_ref, gp_ref, s_ref, h_ref = rest[n_in - 4 + n_x:n_in + n_x]
        xv = x_ref[...]
        r = lax.rsqrt(jnp.mean(xv * xv, axis=-1, keepdims=True) + EPS)
        h = (xv * r * g_ref[...]).astype(BF16)
        h_ref[...] = h
        for c in range(f // fc):
            sl = slice(c * fc, (c + 1) * fc)
            a = _dg(h, wg_ref[sl, :], NT)
            b = _dg(h, wu_ref[sl, :], NT)
            sig = jax.nn.sigmoid(a)
            sil = a * sig
            sil_ref[:, sl] = sil.astype(BF16)
            gp_ref[:, sl] = (b * (sig + sil - sil * sig)).astype(BF16)
            s_ref[:, sl] = (sil * b).astype(BF16)
        if wd is None:
            return
        wd_ref, xo_ref = rest[0], rest[n_in - 4]
        xo = xv + FFN_RES * _dot(s_ref[...], wd_ref[...])
        if head is None:
            xo_ref[...] = xo
        else:
            t_ref, hg_ref = rest[1:3]
            dg_ref, loss_ref = rest[-2:]

            @pl.when(pl.program_id(0) == 0)
            def _():
                dg_ref[...] = jnp.zeros_like(dg_ref)
                loss_ref[...] = jnp.zeros_like(loss_ref)

            dx, dgain, loss = _loss_head(xo, t_ref[...], hg_ref[...])
            xo_ref[...] = dx
            dg_ref[...] += dgain
            loss_ref[...] += loss

    tok = lambda: pl.BlockSpec((tm, d), lambda i: (i, 0))
    wide = lambda: pl.BlockSpec((tm, f), lambda i: (i, 0))
    const2 = lambda i: (0, 0)
    in_specs = [tok(), _resident((1, d))] + [_resident((f, d)) for _ in weights]
    out_specs = [tok()] * n_x + [wide(), wide(), wide(), tok()]
    out_shape = ([jax.ShapeDtypeStruct((t, d), F32)] * n_x + [jax.ShapeDtypeStruct((t, f), BF16)] * 3
                 + [jax.ShapeDtypeStruct((t, d), BF16)])
    operands = [x, gain] + weights
    if head is not None:
        in_specs += [tok(), _resident((1, d))]
        out_specs += [pl.BlockSpec((1, d), const2), pl.BlockSpec((1, LANES), const2)]
        out_shape += [jax.ShapeDtypeStruct((1, d), F32), jax.ShapeDtypeStruct((1, LANES), F32)]
        operands += list(head)
    return pl.pallas_call(
        _drop_deps(body, n_in, len(deps)), name=name, grid=(t // tm,),
        in_specs=in_specs + [ANY_SPEC] * len(deps), out_specs=out_specs, out_shape=out_shape,
        compiler_params=_params(("arbitrary",)),
    )(*operands, *deps)


def _rmsnorm_bwd_rows(xv, r, gain, dh):
    n = xv * r
    dgain = jnp.sum(dh * n, axis=0, keepdims=True)
    dn = dh * gain
    dx = r * (dn - n * jnp.mean(dn * n, axis=-1, keepdims=True))
    return dx, dgain


def ffn_bwd(x, dxo, gain, sil, gp, wg_t, wu_t, wd, name, deps=()):
    t, d = x.shape
    f = wd.shape[0]
    tm = min(t, FFN_BWD_TOKENS)
    nt = t // tm
    chunks = [(off, min(MXU_WIDTH, f - off)) for off in range(0, f, MXU_WIDTH)]

    def body(xp_ref, dxp_ref, dxo_ref, g_ref, sil_ref, gp_ref, wg_ref, wu_ref, wd_ref,
             dx_ref, da_ref, db_ref, df_ref, dg_ref, dh_ref):
        i = pl.program_id(0)
        gain_v = g_ref[...]

        @pl.when(i == 0)
        def _():
            dh_ref[...] = jnp.zeros_like(dh_ref)
            dg_ref[...] = jnp.zeros_like(dg_ref)

        def finish_previous(slot):
            xv = xp_ref[...]
            r = lax.rsqrt(jnp.mean(xv * xv, axis=-1, keepdims=True) + EPS)
            dx, dgain = _rmsnorm_bwd_rows(xv, r, gain_v, dh_ref[slot])
            dx_ref[...] = dxp_ref[...] + dx
            dg_ref[...] += dgain

        @pl.when(i < nt)
        def _():
            df = (FFN_RES * dxo_ref[...]).astype(BF16)
            df_ref[...] = df
            for off, size in chunks:
                sl = slice(off, off + size)
                ds = _dg(df, wd_ref[sl, :], NT)
                da_ref[:, sl] = (ds * gp_ref[:, sl].astype(F32)).astype(BF16)
                db_ref[:, sl] = (ds * sil_ref[:, sl].astype(F32)).astype(BF16)
            dh = _dot(da_ref[...], wg_ref[...]) + _dot(db_ref[...], wu_ref[...])
            finish_previous((i + 1) % 2)
            dh_ref[i % 2] = dh

        @pl.when(i == nt)
        def _():
            finish_previous((nt - 1) % 2)

    prev = lambda i: (jnp.maximum(i - 1, 0), 0)
    cur = lambda i: (jnp.minimum(i, nt - 1), 0)
    return pl.pallas_call(
        _drop_deps(body, 9, len(deps)), name=name, grid=(nt + 1,),
        in_specs=[pl.BlockSpec((tm, d), prev), pl.BlockSpec((tm, d), prev), pl.BlockSpec((tm, d), cur),
                  _resident((1, d)), pl.BlockSpec((tm, f), cur), pl.BlockSpec((tm, f), cur),
                  _resident((f, d)), _resident((f, d)), _resident((f, d))] + [ANY_SPEC] * len(deps),
        out_specs=[pl.BlockSpec((tm, d), prev), pl.BlockSpec((tm, f), cur), pl.BlockSpec((tm, f), cur),
                   pl.BlockSpec((tm, d), cur), pl.BlockSpec((1, d), lambda i: (0, 0))],
        out_shape=[jax.ShapeDtypeStruct((t, d), F32), jax.ShapeDtypeStruct((t, f), BF16),
                   jax.ShapeDtypeStruct((t, f), BF16), jax.ShapeDtypeStruct((t, d), BF16),
                   jax.ShapeDtypeStruct((1, d), F32)],
        scratch_shapes=[pltpu.VMEM((2, tm, d), F32)],
        compiler_params=_params(("arbitrary",)),
    )(x, dxo, dxo, gain, sil, gp, wg_t, wu_t, wd, *deps)


def _row_tile(m, cap):
    if m <= cap:
        return m
    best = None
    for bm in range(LANES, cap + 1, LANES):
        if m % bm == 0:
            best = bm
    return best if best is not None else m


def grad_tn(a, b, scale, name, deps=()):
    t, m = a.shape
    n = b.shape[1]
    bm = _row_tile(m, GRAD_ROW_TILE_CAP)
    bk = min(t, GRAD_TOKENS_PER_STEP)
    nk = t // bk

    def body(a_ref, b_ref, o_ref, acc_ref):
        k = pl.program_id(1)

        @pl.when(k == 0)
        def _():
            acc_ref[...] = jnp.zeros_like(acc_ref)

        acc_ref[...] += _dg(a_ref[...].astype(BF16), b_ref[...].astype(BF16), TN)

        @pl.when(k == nk - 1)
        def _():
            o_ref[...] = (scale * acc_ref[...]).astype(BF16)

    return pl.pallas_call(
        _drop_deps(body, 2, len(deps)), name=name, grid=(m // bm, nk),
        in_specs=[pl.BlockSpec((bk, bm), lambda i, k: (k, i)), pl.BlockSpec((bk, n), lambda i, k: (k, 0))]
        + [ANY_SPEC] * len(deps),
        out_specs=pl.BlockSpec((bm, n), lambda i, k: (i, 0)),
        out_shape=jax.ShapeDtypeStruct((m, n), BF16),
        scratch_shapes=[pltpu.VMEM((bm, n), F32)],
        compiler_params=_params(("arbitrary", "arbitrary")),
    )(a, b, *deps)


def grad_nn(a_list, b, name):
    t, n = b.shape
    ms = [a.shape[0] for a in a_list]
    m = sum(ms)
    bk = min(t, GRAD_TOKENS_PER_STEP)
    nk = t // bk
    na = len(a_list)

    def body(*refs):
        a_refs, b_ref, o_ref, acc_ref = refs[:na], refs[na], refs[na + 1], refs[na + 2]
        k = pl.program_id(0)

        @pl.when(k == 0)
        def _():
            acc_ref[...] = jnp.zeros_like(acc_ref)

        bv = b_ref[...].astype(BF16)
        off = 0
        for a_ref, mi in zip(a_refs, ms):
            acc_ref[off:off + mi, :] += _dot(a_ref[...].astype(BF16), bv)
            off += mi

        @pl.when(k == nk - 1)
        def _():
            o_ref[...] = acc_ref[...].astype(BF16)

    return pl.pallas_call(
        body, name=name, grid=(nk,),
        in_specs=[pl.BlockSpec((mi, bk), lambda k: (0, k)) for mi in ms] + [pl.BlockSpec((bk, n), lambda k: (k, 0))],
        out_specs=pl.BlockSpec((m, n), lambda k: (0, 0)),
        out_shape=jax.ShapeDtypeStruct((m, n), BF16),
        scratch_shapes=[pltpu.VMEM((m, n), F32)],
        compiler_params=_params(("arbitrary",)),
    )(*a_list, b)


def mix_in_fwd(x, gain, win_t, b_in_row, name, ffn_tail=None):
    t, d = x.shape
    w = win_t.shape[0]
    tm = _token_tile(t)

    def body(x_ref, g_ref, w_ref, bias_ref, *rest):
        xv = x_ref[...]
        if ffn_tail is not None:
            s_ref, wd_ref, xo_ref = rest[0], rest[1], rest[4]
            xv = xv + FFN_RES * _dot(s_ref[...], wd_ref[...])
            xo_ref[...] = xv
        qkv_ref, zg_ref = rest[-3:-1] if ffn_tail is not None else rest
        r = lax.rsqrt(jnp.mean(xv * xv, axis=-1, keepdims=True) + EPS)
        h = (xv * r * g_ref[...]).astype(BF16)
        bias = _as_column(bias_ref[...])
        qkv_ref[...] = (_dg(w_ref[:QKV_WIDTH, :], h, NT) + bias[:QKV_WIDTH, :]).astype(BF16)
        zg_ref[...] = (_dg(w_ref[QKV_WIDTH:, :], h, NT) + bias[QKV_WIDTH:, :]).astype(BF16)

    tok = lambda: pl.BlockSpec((tm, d), lambda i: (i, 0))
    in_specs = [tok(), _resident((1, d)), _resident((w, d)), _resident((1, w))]
    out_specs = [pl.BlockSpec((QKV_WIDTH, tm), lambda i: (0, i)), pl.BlockSpec((w - QKV_WIDTH, tm), lambda i: (0, i))]
    out_shape = [jax.ShapeDtypeStruct((QKV_WIDTH, t), BF16), jax.ShapeDtypeStruct((w - QKV_WIDTH, t), BF16)]
    operands = [x, gain, win_t, b_in_row]
    if ffn_tail is not None:
        f = ffn_tail[1].shape[0]
        in_specs += [pl.BlockSpec((tm, f), lambda i: (i, 0)), _resident((f, d))]
        out_specs.append(tok())
        out_shape.append(jax.ShapeDtypeStruct((t, d), F32))
        operands += list(ffn_tail)
    return pl.pallas_call(
        body, name=name, grid=(t // tm,), in_specs=in_specs, out_specs=out_specs, out_shape=out_shape,
        compiler_params=_params(("arbitrary",)),
    )(*operands)


ATTN_REP = N_Q_HEADS // N_KV_HEADS
ATTN_BLOCKS_PER_STEP = 8


def _attn_tiles(t):
    nbs = min(ATTN_BLOCKS_PER_STEP, t // BLOCK)
    return nbs, nbs * BLOCK, t // (nbs * BLOCK)


def _attn_specs(nbs, tiles):
    last = tiles - 1
    cur = lambda n: jnp.minimum(n, last)
    prev = lambda n: jnp.maximum(jnp.minimum(n, last) * nbs - 1, 0)
    k_row = ATTN_WIDTH // KV_WIDTH
    tw = nbs * BLOCK
    return [
        pl.BlockSpec((ATTN_WIDTH, tw), lambda n: (0, cur(n))),
        pl.BlockSpec((KV_WIDTH, BLOCK), lambda n: (k_row, prev(n))),
        pl.BlockSpec((KV_WIDTH, tw), lambda n: (k_row, cur(n))),
        pl.BlockSpec((KV_WIDTH, BLOCK), lambda n: (k_row + 1, prev(n))),
        pl.BlockSpec((KV_WIDTH, tw), lambda n: (k_row + 1, cur(n))),
    ]


def _cols(b):
    return slice(b * BLOCK, (b + 1) * BLOCK)


def _band_rows(prev_ref, tile_ref, gs, b):
    before = prev_ref[gs, :] if b == 0 else tile_ref[gs, _cols(b - 1)]
    return before, tile_ref[gs, _cols(b)]


def _group_rows(ref, g, b):
    first = g * ATTN_REP
    return jnp.concatenate([ref[(first + r) * HEAD_DIM:(first + r + 1) * HEAD_DIM, _cols(b)] for r in range(ATTN_REP)],
                           axis=1)


def _ungroup_rows(ref, g, b, val):
    first = g * ATTN_REP
    for r in range(ATTN_REP):
        ref[(first + r) * HEAD_DIM:(first + r + 1) * HEAD_DIM, _cols(b)] = val[:, r * BLOCK:(r + 1) * BLOCK]


def _attn_upper():
    key = lax.broadcasted_iota(jnp.int32, (BLOCK, ATTN_REP * BLOCK), 0)
    qry = lax.broadcasted_iota(jnp.int32, (BLOCK, ATTN_REP * BLOCK), 1) & (BLOCK - 1)
    return key > qry


def _attn_probs(q, kp, kc, sink_ref, g, upper, no_prev):
    s = jnp.where(upper, _dg(kp, q, TN), _dg(kc, q, TN)) * ATTN_SCALE
    if no_prev is not None:
        s = jnp.where(upper & no_prev, MASK_VALUE, s)
    sink = jnp.concatenate([jnp.full((1, BLOCK), sink_ref[0, g * ATTN_REP + r], F32) for r in range(ATTN_REP)], axis=1)
    m = jnp.maximum(jnp.max(s, axis=0, keepdims=True), sink)
    e = jnp.exp(s - m)
    es = jnp.exp(sink - m)
    inv = 1.0 / (jnp.sum(e, axis=0, keepdims=True) + es)
    return e * inv, es * inv


def _split_band(upper, val):
    return jnp.where(upper, val, 0.0).astype(BF16), jnp.where(upper, 0.0, val).astype(BF16)


def attn_fwd(qkv_t, sinks, name):
    t = qkv_t.shape[1]
    nbs, tw, tiles = _attn_tiles(t)

    def body(sink_ref, q_ref, kp_ref, kc_ref, vp_ref, vc_ref, y_ref):
        n = pl.program_id(0)
        upper = _attn_upper()
        for b in range(nbs):
            for g in range(N_KV_HEADS):
                gs = slice(g * HEAD_DIM, (g + 1) * HEAD_DIM)
                kp, kc = _band_rows(kp_ref, kc_ref, gs, b)
                vp, vc = _band_rows(vp_ref, vc_ref, gs, b)
                p, _ = _attn_probs(_group_rows(q_ref, g, b), kp, kc, sink_ref, g, upper, n == 0 if b == 0 else None)
                pp, pc = _split_band(upper, p)
                _ungroup_rows(y_ref, g, b, (_dot(vp, pp) + _dot(vc, pc)).astype(BF16))

    return pl.pallas_call(
        body, name=name, grid=(tiles,),
        in_specs=[pl.BlockSpec(memory_space=pltpu.SMEM)] + _attn_specs(nbs, tiles),
        out_specs=pl.BlockSpec((ATTN_WIDTH, tw), lambda n: (0, n)),
        out_shape=jax.ShapeDtypeStruct((ATTN_WIDTH, t), BF16),
        compiler_params=_params(("arbitrary",)),
    )(sinks, qkv_t, qkv_t, qkv_t, qkv_t, qkv_t)


def attn_bwd(qkv_t, dy_t, sinks, name):
    t = qkv_t.shape[1]
    nbs, tw, tiles = _attn_tiles(t)
    kept = slice(0, tw - BLOCK)

    def body(sink_ref, q_ref, kp_ref, kc_ref, vp_ref, vc_ref, do_ref, dq_ref, dk_ref, dv_ref, dsink_ref,
             ck_ref, cv_ref, sacc_ref):
        n = pl.program_id(0)

        @pl.when(n == 0)
        def _():
            sacc_ref[...] = jnp.zeros_like(sacc_ref)

        @pl.when((n > 0) & (n < tiles))
        def _():
            if nbs > 1:
                dk_ref[:, kept] = ck_ref[:, kept].astype(BF16)
                dv_ref[:, kept] = cv_ref[:, kept].astype(BF16)

        @pl.when(n < tiles)
        def _():
            upper = _attn_upper()
            for b in range(nbs):
                for g in range(N_KV_HEADS):
                    gs = slice(g * HEAD_DIM, (g + 1) * HEAD_DIM)
                    kp, kc = _band_rows(kp_ref, kc_ref, gs, b)
                    vp, vc = _band_rows(vp_ref, vc_ref, gs, b)
                    q = _group_rows(q_ref, g, b)
                    do = _group_rows(do_ref, g, b)
                    p, ps = _attn_probs(q, kp, kc, sink_ref, g, upper, n == 0 if b == 0 else None)
                    dp = jnp.where(upper, _dg(vp, do, TN), _dg(vc, do, TN))
                    delta = jnp.sum(p * dp, axis=0, keepdims=True)
                    dsink = -(ps * delta)
                    for r in range(ATTN_REP):
                        hd = g * ATTN_REP + r
                        sacc_ref[hd:hd + 1, :] += dsink[:, r * BLOCK:(r + 1) * BLOCK]
                    dsp, dsc = _split_band(upper, p * (dp - delta))
                    pp, pc = _split_band(upper, p)
                    _ungroup_rows(dq_ref, g, b, (ATTN_SCALE * (_dot(kp, dsp) + _dot(kc, dsc))).astype(BF16))
                    dk_before = ATTN_SCALE * _dg(q, dsp, NT)
                    dv_before = _dg(do, pp, NT)
                    if b == 0:
                        @pl.when(n > 0)
                        def _():
                            dk_ref[gs, _cols(nbs - 1)] = (ck_ref[gs, _cols(nbs - 1)] + dk_before).astype(BF16)
                            dv_ref[gs, _cols(nbs - 1)] = (cv_ref[gs, _cols(nbs - 1)] + dv_before).astype(BF16)
                    else:
                        ck_ref[gs, _cols(b - 1)] += dk_before
                        cv_ref[gs, _cols(b - 1)] += dv_before
                    ck_ref[gs, _cols(b)] = ATTN_SCALE * _dg(q, dsc, NT)
                    cv_ref[gs, _cols(b)] = _dg(do, pc, NT)

        @pl.when(n == tiles)
        def _():
            dk_ref[...] = ck_ref[...].astype(BF16)
            dv_ref[...] = cv_ref[...].astype(BF16)
            dsink_ref[...] = jnp.sum(sacc_ref[...], axis=1, keepdims=True)

    last = tiles - 1
    done = lambda n: jnp.maximum(n - 1, 0)
    outs = pl.pallas_call(
        body, name=name, grid=(tiles + 1,),
        in_specs=[pl.BlockSpec(memory_space=pltpu.SMEM)] + _attn_specs(nbs, tiles)
        + [pl.BlockSpec((ATTN_WIDTH, tw), lambda n: (0, jnp.minimum(n, last)))],
        out_specs=[pl.BlockSpec((ATTN_WIDTH, tw), lambda n: (0, jnp.minimum(n, last))),
                   pl.BlockSpec((KV_WIDTH, tw), lambda n: (0, done(n))),
                   pl.BlockSpec((KV_WIDTH, tw), lambda n: (0, done(n))),
                   pl.BlockSpec((N_Q_HEADS, 1), lambda n: (0, 0))],
        out_shape=[jax.ShapeDtypeStruct((ATTN_WIDTH, t), BF16), jax.ShapeDtypeStruct((KV_WIDTH, t), BF16),
                   jax.ShapeDtypeStruct((KV_WIDTH, t), BF16), jax.ShapeDtypeStruct((N_Q_HEADS, 1), F32)],
        scratch_shapes=[pltpu.VMEM((KV_WIDTH, tw), F32), pltpu.VMEM((KV_WIDTH, tw), F32),
                        pltpu.VMEM((N_Q_HEADS, BLOCK), F32)],
        compiler_params=_params(("arbitrary",)),
    )(sinks, qkv_t, qkv_t, qkv_t, qkv_t, qkv_t, dy_t)
    return outs


GELU_C = 0.7978845608028654
GELU_A = 0.044715


def _gelu(x):
    return 0.5 * x * (1.0 + jnp.tanh(GELU_C * (x + GELU_A * x * x * x)))


def _gelu_grad(x):
    th = jnp.tanh(GELU_C * (x + GELU_A * x * x * x))
    return 0.5 * (1.0 + th) + 0.5 * x * (1.0 - th * th) * GELU_C * (1.0 + 3.0 * GELU_A * x * x)


def _gmlp_parts(zg, lng, lnb):
    z = _gelu(zg)
    u = z[:GMLP_WIDTH, :]
    vv = z[GMLP_WIDTH:, :]
    mu = jnp.mean(vv, axis=0, keepdims=True)
    xc = vv - mu
    rstd = lax.rsqrt(jnp.mean(xc * xc, axis=0, keepdims=True) + EPS)
    xhat = xc * rstd
    return u, xhat, rstd, xhat * lng + lnb


def _as_column(row):
    return jnp.broadcast_to(row, (LANES, row.shape[1])).T[:, :1]


def _row_sums(acc):
    return jnp.sum(acc.T, axis=0, keepdims=True)


def _causal(w):
    row = lax.broadcasted_iota(jnp.int32, (BLOCK, BLOCK), 0)
    col = lax.broadcasted_iota(jnp.int32, (BLOCK, BLOCK), 1)
    return jnp.where(col <= row, w, 0.0)


GMLP_CHUNKS_PER_STEP = 8


def _stack_chunks(v, nc):
    return jnp.concatenate([v[:, c * BLOCK:(c + 1) * BLOCK] for c in range(nc)], axis=0)


def _unstack_chunks(v, nc):
    rows = v.shape[0] // nc
    return jnp.concatenate([v[c * rows:(c + 1) * rows, :] for c in range(nc)], axis=1)


def gmlp_fwd(zg_t, lng, lnb, w_s, b_s, name):
    t = zg_t.shape[1]
    nc = min(GMLP_CHUNKS_PER_STEP, t // BLOCK)
    tw = nc * BLOCK

    def body(zg_ref, lng_ref, lnb_ref, w_ref, bs_ref, y_ref):
        u, _, _, vn = _gmlp_parts(zg_ref[...].astype(F32), _as_column(lng_ref[...]), _as_column(lnb_ref[...]))
        for g in range(GMLP_GROUPS):
            gs = slice(g * GMLP_GROUP_DIM, (g + 1) * GMLP_GROUP_DIM)
            wc = _causal(w_ref[g]).astype(BF16)
            mixed = _dg(_stack_chunks(vn[gs, :].astype(BF16), nc), wc, NT) + bs_ref[g:g + 1, :]
            y_ref[gs, :] = (u[gs, :] * _unstack_chunks(mixed, nc)).astype(BF16)

    return pl.pallas_call(
        body, name=name, grid=(t // tw,),
        in_specs=[pl.BlockSpec((2 * GMLP_WIDTH, tw), lambda n: (0, n)), _resident((1, GMLP_WIDTH)),
                  _resident((1, GMLP_WIDTH)), _resident((GMLP_GROUPS, BLOCK, BLOCK)), _resident((GMLP_GROUPS, BLOCK))],
        out_specs=pl.BlockSpec((GMLP_WIDTH, tw), lambda n: (0, n)),
        out_shape=jax.ShapeDtypeStruct((GMLP_WIDTH, t), BF16),
        compiler_params=_params(("arbitrary",)),
    )(zg_t, lng, lnb, w_s, b_s)


def gmlp_bwd(zg_t, dy_t, lng, lnb, w_s, b_s, name):
    t = zg_t.shape[1]
    nc = min(GMLP_CHUNKS_PER_STEP, t // BLOCK)
    tw = nc * BLOCK
    nt = t // tw

    def body(zg_ref, dy_ref, lng_ref, lnb_ref, w_ref, bs_ref, dzg_ref, dw_ref, dbs_ref, dlng_ref, dlnb_ref,
             gacc_ref, bacc_ref):
        n = pl.program_id(0)

        @pl.when(n == 0)
        def _():
            dw_ref[...] = jnp.zeros_like(dw_ref)
            dbs_ref[...] = jnp.zeros_like(dbs_ref)
            gacc_ref[...] = jnp.zeros_like(gacc_ref)
            bacc_ref[...] = jnp.zeros_like(bacc_ref)

        zg = zg_ref[...].astype(F32)
        lng_v = _as_column(lng_ref[...])
        u, xhat, rstd, vn = _gmlp_parts(zg, lng_v, _as_column(lnb_ref[...]))
        dy = dy_ref[...].astype(F32)
        dvn_parts = []
        for g in range(GMLP_GROUPS):
            gs = slice(g * GMLP_GROUP_DIM, (g + 1) * GMLP_GROUP_DIM)
            wc = _causal(w_ref[g]).astype(BF16)
            vn_s = _stack_chunks(vn[gs, :].astype(BF16), nc)
            mixed = _unstack_chunks(_dg(vn_s, wc, NT) + bs_ref[g:g + 1, :], nc)
            dy_g = dy[gs, :]
            dmixed = dy_g * u[gs, :]
            dm_s = _stack_chunks(dmixed.astype(BF16), nc)
            dzg_ref[gs, :] = (dy_g * mixed * _gelu_grad(zg[gs, :])).astype(BF16)
            dw_ref[g] += _causal(_dg(dm_s, vn_s, TN))
            dbs_ref[g:g + 1, :] += _lane_fold(jnp.sum(dmixed, axis=0, keepdims=True))
            dvn_parts.append(_unstack_chunks(_dot(dm_s, wc), nc))
        dvn = jnp.concatenate(dvn_parts, axis=0)
        gacc_ref[...] += _lane_fold(dvn * xhat)
        bacc_ref[...] += _lane_fold(dvn)
        dxhat = dvn * lng_v
        m1 = jnp.mean(dxhat, axis=0, keepdims=True)
        m2 = jnp.mean(dxhat * xhat, axis=0, keepdims=True)
        dvv = rstd * (dxhat - m1 - xhat * m2)
        dzg_ref[GMLP_WIDTH:, :] = (dvv * _gelu_grad(zg[GMLP_WIDTH:, :])).astype(BF16)

        @pl.when(n == nt - 1)
        def _():
            dlng_ref[...] = _row_sums(gacc_ref[...])
            dlnb_ref[...] = _row_sums(bacc_ref[...])

    const2 = lambda n: (0, 0)
    return pl.pallas_call(
        body, name=name, grid=(nt,),
        in_specs=[pl.BlockSpec((2 * GMLP_WIDTH, tw), lambda n: (0, n)), pl.BlockSpec((GMLP_WIDTH, tw), lambda n: (0, n)),
                  _resident((1, GMLP_WIDTH)), _resident((1, GMLP_WIDTH)),
                  _resident((GMLP_GROUPS, BLOCK, BLOCK)), _resident((GMLP_GROUPS, BLOCK))],
        out_specs=[pl.BlockSpec((2 * GMLP_WIDTH, tw), lambda n: (0, n)),
                   pl.BlockSpec((GMLP_GROUPS, BLOCK, BLOCK), lambda n: (0, 0, 0)),
                   pl.BlockSpec((GMLP_GROUPS, BLOCK), const2),
                   pl.BlockSpec((1, GMLP_WIDTH), const2), pl.BlockSpec((1, GMLP_WIDTH), const2)],
        out_shape=[jax.ShapeDtypeStruct((2 * GMLP_WIDTH, t), BF16),
                   jax.ShapeDtypeStruct((GMLP_GROUPS, BLOCK, BLOCK), F32),
                   jax.ShapeDtypeStruct((GMLP_GROUPS, BLOCK), F32),
                   jax.ShapeDtypeStruct((1, GMLP_WIDTH), F32), jax.ShapeDtypeStruct((1, GMLP_WIDTH), F32)],
        scratch_shapes=[pltpu.VMEM((GMLP_WIDTH, BLOCK), F32), pltpu.VMEM((GMLP_WIDTH, BLOCK), F32)],
        compiler_params=_params(("arbitrary",)),
    )(zg_t, dy_t, lng, lnb, w_s, b_s)


def _rmsnorm_cols(y, gain_col):
    r = lax.rsqrt(jnp.mean(y * y, axis=0, keepdims=True) + EPS)
    n = y * r
    return n, r, n * gain_col


def mix_out_fwd(x, ya_t, yg_t, gao, ggo, wout, b_out, name):
    t, d = x.shape
    tm = _token_tile(t)

    def body(x_ref, ya_ref, yg_ref, gao_ref, ggo_ref, w_ref, b_ref, o_ref):
        _, _, ya = _rmsnorm_cols(ya_ref[...].astype(F32), _as_column(gao_ref[...]))
        _, _, yg = _rmsnorm_cols(yg_ref[...].astype(F32), _as_column(ggo_ref[...]))
        o_ref[...] = (x_ref[...] + _dg(ya.astype(BF16), w_ref[:ATTN_WIDTH, :], TN)
                      + _dg(yg.astype(BF16), w_ref[ATTN_WIDTH:, :], TN) + b_ref[...])

    return pl.pallas_call(
        body, name=name, grid=(t // tm,),
        in_specs=[pl.BlockSpec((tm, d), lambda i: (i, 0)), pl.BlockSpec((ATTN_WIDTH, tm), lambda i: (0, i)),
                  pl.BlockSpec((GMLP_WIDTH, tm), lambda i: (0, i)), _resident((1, ATTN_WIDTH)), _resident((1, GMLP_WIDTH)),
                  _resident((ATTN_WIDTH + GMLP_WIDTH, d)), _resident((1, d))],
        out_specs=pl.BlockSpec((tm, d), lambda i: (i, 0)),
        out_shape=jax.ShapeDtypeStruct((t, d), F32),
        compiler_params=_params(("arbitrary",)),
    )(x, ya_t, yg_t, gao, ggo, wout, b_out)


def _lane_fold(v):
    out = v[:, :LANES]
    for j in range(1, v.shape[1] // LANES):
        out = out + v[:, j * LANES:(j + 1) * LANES]
    return out


def mix_out_bwd(dx, ya_t, yg_t, gao, ggo, wout, name, deps=()):
    t, d = dx.shape
    tm = _token_tile(t)
    nt = t // tm

    def body(dx_ref, ya_ref, yg_ref, gao_ref, ggo_ref, w_ref, dya_ref, dyg_ref, y_ref, dgao_ref, dggo_ref, db_ref,
             acc_ref):
        i = pl.program_id(0)

        @pl.when(i == 0)
        def _():
            acc_ref[...] = jnp.zeros_like(acc_ref)
            db_ref[...] = jnp.zeros_like(db_ref)

        dxv = dx_ref[...]
        db_ref[...] += jnp.sum(dxv, axis=0, keepdims=True)
        dxb = dxv.astype(BF16)
        for part, (src_ref, gain_ref, dst_ref) in enumerate(((ya_ref, gao_ref, dya_ref), (yg_ref, ggo_ref, dyg_ref))):
            rows = slice(part * ATTN_WIDTH, (part + 1) * ATTN_WIDTH)
            gain = _as_column(gain_ref[...])
            nrm, r, yn = _rmsnorm_cols(src_ref[...].astype(F32), gain)
            y_ref[rows, :] = yn.astype(BF16)
            dy = _dg(w_ref[rows, :], dxb, NT)
            acc_ref[rows, :] += _lane_fold(dy * nrm)
            dn = dy * gain
            dst_ref[...] = (r * (dn - nrm * jnp.mean(dn * nrm, axis=0, keepdims=True))).astype(BF16)

        @pl.when(i == nt - 1)
        def _():
            dgao_ref[...] = _row_sums(acc_ref[:ATTN_WIDTH, :])
            dggo_ref[...] = _row_sums(acc_ref[ATTN_WIDTH:, :])

    const2 = lambda i: (0, 0)
    feat = lambda w: pl.BlockSpec((w, tm), lambda i: (0, i))
    return pl.pallas_call(
        _drop_deps(body, 6, len(deps)), name=name, grid=(nt,),
        in_specs=[pl.BlockSpec((tm, d), lambda i: (i, 0)), feat(ATTN_WIDTH), feat(GMLP_WIDTH),
                  _resident((1, ATTN_WIDTH)), _resident((1, GMLP_WIDTH)), _resident((ATTN_WIDTH + GMLP_WIDTH, d))]
        + [ANY_SPEC] * len(deps),
        out_specs=[feat(ATTN_WIDTH), feat(GMLP_WIDTH), feat(ATTN_WIDTH + GMLP_WIDTH),
                   pl.BlockSpec((1, ATTN_WIDTH), const2), pl.BlockSpec((1, GMLP_WIDTH), const2),
                   pl.BlockSpec((1, d), const2)],
        out_shape=[jax.ShapeDtypeStruct((ATTN_WIDTH, t), BF16), jax.ShapeDtypeStruct((GMLP_WIDTH, t), BF16),
                   jax.ShapeDtypeStruct((ATTN_WIDTH + GMLP_WIDTH, t), BF16),
                   jax.ShapeDtypeStruct((1, ATTN_WIDTH), F32), jax.ShapeDtypeStruct((1, GMLP_WIDTH), F32),
                   jax.ShapeDtypeStruct((1, d), F32)],
        scratch_shapes=[pltpu.VMEM((ATTN_WIDTH + GMLP_WIDTH, LANES), F32)],
        compiler_params=_params(("arbitrary",)),
    )(dx, ya_t, yg_t, gao, ggo, wout, *deps)


def mix_in_bwd(x, dxo, gain, dq_t, dk_t, dv_t, dzg_t, win_t, name):
    t, d = x.shape
    w = win_t.shape[0]
    tm = _token_tile(t)
    nt = t // tm
    parts = ((0, ATTN_WIDTH), (ATTN_WIDTH, KV_WIDTH), (ATTN_WIDTH + KV_WIDTH, KV_WIDTH), (QKV_WIDTH, w - QKV_WIDTH))

    def body(x_ref, dxo_ref, g_ref, dq_ref, dk_ref, dv_ref, dzg_ref, w_ref, dx_ref, h_ref, dg_ref, db_ref, acc_ref):
        i = pl.program_id(0)

        @pl.when(i == 0)
        def _():
            acc_ref[...] = jnp.zeros_like(acc_ref)
            dg_ref[...] = jnp.zeros_like(dg_ref)

        xv = x_ref[...]
        gain_v = g_ref[...]
        r = lax.rsqrt(jnp.mean(xv * xv, axis=-1, keepdims=True) + EPS)
        h_ref[...] = (xv * r * gain_v).astype(BF16)
        dh = jnp.zeros((tm, d), F32)
        for (off, size), src_ref in zip(parts, (dq_ref, dk_ref, dv_ref, dzg_ref)):
            dp = src_ref[...]
            acc_ref[off:off + size, :] += _lane_fold(dp.astype(F32))
            dh = dh + _dg(dp, w_ref[off:off + size, :], TN)
        dx, dgain = _rmsnorm_bwd_rows(xv, r, gain_v, dh)
        dx_ref[...] = dxo_ref[...] + dx
        dg_ref[...] += dgain

        @pl.when(i == nt - 1)
        def _():
            db_ref[...] = _row_sums(acc_ref[...])

    const2 = lambda i: (0, 0)
    tok = lambda: pl.BlockSpec((tm, d), lambda i: (i, 0))
    feat = lambda rows: pl.BlockSpec((rows, tm), lambda i: (0, i))
    return pl.pallas_call(
        body, name=name, grid=(nt,),
        in_specs=[tok(), tok(), _resident((1, d)), feat(ATTN_WIDTH), feat(KV_WIDTH), feat(KV_WIDTH),
                  feat(w - QKV_WIDTH), _resident((w, d))],
        out_specs=[tok(), tok(), pl.BlockSpec((1, d), const2), pl.BlockSpec((1, w), const2)],
        out_shape=[jax.ShapeDtypeStruct((t, d), F32), jax.ShapeDtypeStruct((t, d), BF16),
                   jax.ShapeDtypeStruct((1, d), F32), jax.ShapeDtypeStruct((1, w), F32)],
        scratch_shapes=[pltpu.VMEM((w, LANES), F32)],
        compiler_params=_params(("arbitrary",)),
    )(x, dxo, gain, dq_t, dk_t, dv_t, dzg_t, win_t)


def _adamw_math(w, g, m, v):
    m = ADAM_B1 * m + (1.0 - ADAM_B1) * g
    v = ADAM_B2 * v + (1.0 - ADAM_B2) * (g * g)
    m_hat = m / (1.0 - ADAM_B1 ** ADAM_STEP)
    v_hat = v / (1.0 - ADAM_B2 ** ADAM_STEP)
    delta = -ADAM_LR * (m_hat / (jnp.sqrt(v_hat) + ADAM_EPS) + ADAM_WD * w)
    return delta, m, v


def adamw_many(ws, gs, ms, vs, name):
    n = len(ws)

    def body(*refs):
        w_refs, g_refs, m_refs, v_refs = refs[:n], refs[n:2 * n], refs[2 * n:3 * n], refs[3 * n:4 * n]
        d_refs, mo_refs, vo_refs = refs[4 * n:5 * n], refs[5 * n:6 * n], refs[6 * n:]
        for k in range(n):
            delta, mn, vn = _adamw_math(w_refs[k][...], g_refs[k][...], m_refs[k][...], v_refs[k][...])
            d_refs[k][...] = delta
            mo_refs[k][...] = mn
            vo_refs[k][...] = vn

    def whole(a):
        zeros = (0,) * a.ndim
        return pl.BlockSpec(a.shape, lambda i: zeros)

    outs = pl.pallas_call(
        body, name=name, grid=(1,),
        in_specs=[whole(a) for a in ws] * 4, out_specs=[whole(a) for a in ws] * 3,
        out_shape=[jax.ShapeDtypeStruct(a.shape, F32) for a in ws] * 3,
        compiler_params=_params(("arbitrary",)),
    )(*ws, *gs, *ms, *vs)
    return outs[:n], outs[n:2 * n], outs[2 * n:]


def sum_adamw(landing, grad3d, me, w, m, v, name):
    nd, rows, cols = landing.shape
    tr = rows // 2

    def body(me_ref, l_ref, own_ref, w_ref, m_ref, v_ref, g_ref, d_ref, mo_ref, vo_ref):
        g = own_ref[...].astype(F32)
        for j in range(nd):
            g = g + l_ref[j].astype(F32)
        delta, mn, vn = _adamw_math(w_ref[...], g, m_ref[...], v_ref[...])
        g_ref[...] = g
        d_ref[...] = delta
        mo_ref[...] = mn
        vo_ref[...] = vn

    spec = lambda: pl.BlockSpec((tr, cols), lambda i, me_ref: (i, 0))
    shape = jax.ShapeDtypeStruct((rows, cols), F32)
    return pl.pallas_call(
        body, name=name,
        grid_spec=pltpu.PrefetchScalarGridSpec(
            num_scalar_prefetch=1, grid=(rows // tr,),
            in_specs=[pl.BlockSpec((nd, tr, cols), lambda i, me_ref: (0, i, 0)),
                      pl.BlockSpec((None, tr, cols), lambda i, me_ref: (me_ref[0], i, 0)), spec(), spec(), spec()],
            out_specs=[spec() for _ in range(4)]),
        out_shape=[shape] * 4,
        compiler_params=_params(("arbitrary",)),
    )(me, landing, grad3d, w, m, v)


def _mesh_place():
    x, y, c = lax.axis_index("x"), lax.axis_index("y"), lax.axis_index("c")
    return x, y, c, 4 * x + 2 * y + c


def _peer(x, y, c, k):
    return (x ^ ((k >> 2) & 1), y ^ ((k >> 1) & 1), c ^ (k & 1))


def prep_shards(mats, me, name):
    na = len(mats)

    def body(me_ref, *refs):
        in_refs, shard_refs, land_refs = refs[:na], refs[na:2 * na], refs[2 * na:]
        for a in range(na):
            val = in_refs[a][...].astype(BF16)
            shard_refs[a][...] = val
            land_refs[a][...] = val

    whole = lambda mat: pl.BlockSpec(mat.shape, lambda i, me_ref: (0, 0))
    outs = pl.pallas_call(
        body, name=name,
        grid_spec=pltpu.PrefetchScalarGridSpec(
            num_scalar_prefetch=1, grid=(1,),
            in_specs=[whole(mat) for mat in mats],
            out_specs=[whole(mat) for mat in mats]
            + [pl.BlockSpec((None,) + mat.shape, lambda i, me_ref: (me_ref[0], 0, 0)) for mat in mats]),
        out_shape=[jax.ShapeDtypeStruct(mat.shape, BF16) for mat in mats]
        + [jax.ShapeDtypeStruct((N_DEV,) + mat.shape, BF16) for mat in mats],
        compiler_params=_params(("arbitrary",)),
    )(me, *mats)
    return outs[:na], outs[na:]


def allgather_rows(shards, lands, name):
    na = n_gather = len(shards)

    def body(*refs):
        in_refs, out_refs = refs[:na], refs[2 * na:3 * na]
        send_sems, recv_sems = refs[3 * na:]
        x, y, c, me = _mesh_place()

        def copy(sem, a, origin, src, k):
            return pltpu.make_async_remote_copy(
                src_ref=src, dst_ref=out_refs[a].at[origin], send_sem=send_sems.at[sem, a],
                recv_sem=recv_sems.at[sem, a], device_id=_peer(x, y, c, k), device_id_type=pl.DeviceIdType.MESH)

        kx, ky = 4, 2

        def recv(sem, a, origin):
            copy(sem, a, origin, in_refs[a], 1).wait_recv()

        def landed(a, origin):
            return out_refs[a].at[origin]

        @pl.when(c == 1)
        def _():
            sends = []

            def send(sem, a, origin, src, k):
                cp = copy(sem, a, origin, src, k)
                cp.start()
                sends.append(cp)

            for a in range(n_gather):
                send(1, a, me, in_refs[a], kx)
                send(2, a, me, in_refs[a], ky)
                send(0, a, me, in_refs[a], 1)
            for a in range(n_gather):
                recv(0, a, me ^ 1)
                send(3, a, me ^ 1, landed(a, me ^ 1), kx)
                send(4, a, me ^ 1, landed(a, me ^ 1), ky)
            for a in range(n_gather):
                recv(1, a, me ^ kx)
                send(5, a, me ^ kx, landed(a, me ^ kx), ky)
                send(7, a, me ^ kx, landed(a, me ^ kx), 1)
                recv(2, a, me ^ ky)
                send(8, a, me ^ ky, landed(a, me ^ ky), 1)
            for a in range(n_gather):
                recv(3, a, me ^ kx ^ 1)
                send(9, a, me ^ kx ^ 1, landed(a, me ^ kx ^ 1), 1)
                recv(4, a, me ^ ky ^ 1)
                send(6, a, me ^ ky ^ 1, landed(a, me ^ ky ^ 1), kx)
                send(10, a, me ^ ky ^ 1, landed(a, me ^ ky ^ 1), 1)
            for a in range(n_gather):
                recv(5, a, me ^ 6)
                send(11, a, me ^ 6, landed(a, me ^ 6), 1)
                recv(6, a, me ^ 7)
                send(12, a, me ^ 7, landed(a, me ^ 7), 1)
            for cp in sends:
                cp.wait_send()

        @pl.when(c == 0)
        def _():
            north = me ^ 1
            own = [copy(0, a, me, in_refs[a], 1) for a in range(n_gather)]
            for cp in own:
                cp.start()
            for a in range(n_gather):
                recv(0, a, north)
                for sem, origin in ((7, north ^ kx), (8, north ^ ky), (9, north ^ kx ^ 1), (10, north ^ ky ^ 1),
                                    (11, north ^ 6), (12, north ^ 7)):
                    recv(sem, a, origin)
            for cp in own:
                cp.wait_send()

    return pl.pallas_call(
        body, name=name,
        in_specs=[ANY_SPEC] * (2 * na), out_specs=[ANY_SPEC] * na,
        out_shape=[jax.ShapeDtypeStruct(land.shape, land.dtype) for land in lands],
        input_output_aliases={na + a: a for a in range(na)},
        scratch_shapes=[pltpu.SemaphoreType.DMA((N_GATHER_ROLES, n_gather)),
                        pltpu.SemaphoreType.DMA((N_GATHER_ROLES, n_gather))],
    )(*shards, *lands)


HBM_SPEC = pl.BlockSpec(memory_space=pltpu.HBM)
SEM_SPEC = pl.BlockSpec(memory_space=pltpu.SEMAPHORE)
SIDE_EFFECT = pltpu.SideEffectType.DATAFLOW_SIDE_EFFECTING


def _hbm(v):
    return pltpu.with_memory_space_constraint(v, pltpu.HBM)


def gather_slices(src, land, me, k):
    return src, land.at[me], land.at[me ^ k]


def exchange_slices(src, land, me, k):
    return src.at[me ^ k], land.at[k - 1], land.at[k - 1]


def split_start(groups, slices, deps, name):
    sizes = [len(srcs) for srcs, _ in groups]
    flat_src = [s for srcs, _ in groups for s in srcs]
    flat_land = [l for _, lands in groups for l in lands]
    na, ng, nd = len(flat_src), len(groups), len(deps)

    def body(*refs):
        src_refs, land_refs = refs[:na], refs[na:2 * na]
        sem_refs = refs[2 * na + nd:2 * na + nd + 2 * ng]
        token_ref = refs[-1]
        x, y, c, me = _mesh_place()
        a0 = 0
        for gi, size in enumerate(sizes):
            send_sems, recv_sems = sem_refs[2 * gi], sem_refs[2 * gi + 1]
            for k in range(1, N_DEV):
                for j in range(size):
                    src, dst, _ = slices(src_refs[a0 + j], land_refs[a0 + j], me, k)
                    pltpu.make_async_remote_copy(
                        src_ref=src, dst_ref=dst, send_sem=send_sems.at[(k - 1) * size + j],
                        recv_sem=recv_sems.at[(k - 1) * size + j],
                        device_id=_peer(x, y, c, k), device_id_type=pl.DeviceIdType.MESH).start()
            a0 += size
        token_ref[...] = jnp.zeros_like(token_ref)

    sems = [pltpu.SemaphoreType.DMA(((N_DEV - 1) * size,)) for size in sizes for _ in range(2)]
    thru = [pltpu.HBM(v.shape, v.dtype) for v in flat_src + flat_land]
    outs = pl.pallas_call(
        body, name=name,
        in_specs=[HBM_SPEC] * (2 * na) + [ANY_SPEC] * nd,
        out_specs=[SEM_SPEC] * (2 * ng) + [HBM_SPEC] * (2 * na) + [pl.BlockSpec(memory_space=pltpu.VMEM)],
        out_shape=sems + thru + [jax.ShapeDtypeStruct((8, LANES), F32)],
        input_output_aliases={i: 2 * ng + i for i in range(2 * na)},
        compiler_params=pltpu.CompilerParams(has_side_effects=SIDE_EFFECT),
    )(*[_hbm(v) for v in flat_src + flat_land], *deps)
    started, a0 = [], 0
    for gi, size in enumerate(sizes):
        srcs = outs[2 * ng + a0:2 * ng + a0 + size]
        lands = outs[2 * ng + na + a0:2 * ng + na + a0 + size]
        started.append((outs[2 * gi], outs[2 * gi + 1], list(srcs), list(lands)))
        a0 += size
    return started, outs[-1]


def split_wait(started, slices, after, name):
    send_sems, recv_sems, srcs, lands = started
    na = len(srcs)

    def body(*refs):
        src_refs, land_refs = refs[:na], refs[na:2 * na]
        send_ref, recv_ref = refs[2 * na], refs[2 * na + 1]
        x, y, c, me = _mesh_place()
        for k in range(1, N_DEV):
            for j in range(na):
                src, _, got = slices(src_refs[j], land_refs[j], me, k)
                cp = pltpu.make_async_remote_copy(
                    src_ref=src, dst_ref=got, send_sem=send_ref.at[(k - 1) * na + j], recv_sem=recv_ref.at[(k - 1) * na + j],
                    device_id=_peer(x, y, c, k), device_id_type=pl.DeviceIdType.MESH)
                cp.wait_send()
                cp.wait_recv()

    outs = pl.pallas_call(
        body, name=name,
        in_specs=[HBM_SPEC] * (2 * na) + [SEM_SPEC, SEM_SPEC] + [ANY_SPEC] * len(after),
        out_specs=[HBM_SPEC] * (2 * na),
        out_shape=[pltpu.HBM(v.shape, v.dtype) for v in srcs + lands],
        input_output_aliases={i: i for i in range(2 * na)},
        compiler_params=pltpu.CompilerParams(has_side_effects=SIDE_EFFECT),
    )(*srcs, *lands, send_sems, recv_sems, *after)
    return list(outs[:na]), list(outs[na:])


def place_own(pack, me, name):
    rows, cols = pack.shape

    def body(me_ref, p_ref, o_ref):
        o_ref[...] = p_ref[...]

    return pl.pallas_call(
        body, name=name,
        grid_spec=pltpu.PrefetchScalarGridSpec(
            num_scalar_prefetch=1, grid=(1,),
            in_specs=[pl.BlockSpec((rows, cols), lambda i, me_ref: (0, 0))],
            out_specs=pl.BlockSpec((None, rows, cols), lambda i, me_ref: (me_ref[0], 0, 0))),
        out_shape=jax.ShapeDtypeStruct((N_DEV, rows, cols), F32),
        compiler_params=_params(("arbitrary",)),
    )(me, pack)


def sum_slots(slots, name):
    nd, rows, cols = slots.shape

    def body(s_ref, o_ref):
        acc = s_ref[0]
        for j in range(1, nd):
            acc = acc + s_ref[j]
        o_ref[...] = acc

    return pl.pallas_call(
        body, name=name, grid=(1,),
        in_specs=[pl.BlockSpec((nd, rows, cols), lambda i: (0, 0, 0))],
        out_specs=pl.BlockSpec((rows, cols), lambda i: (0, 0)),
        out_shape=jax.ShapeDtypeStruct((rows, cols), F32),
        compiler_params=_params(("arbitrary",)),
    )(slots)


def _pack_rows(size):
    rows = -(-size // LANES)
    return -(-rows // 8) * 8


def pack_small(parts):
    out = []
    for p in parts:
        flat = p.reshape(-1).astype(F32)
        rows = _pack_rows(flat.shape[0])
        out.append(jnp.pad(flat, (0, rows * LANES - flat.shape[0])).reshape(rows, LANES))
    return jnp.concatenate(out, axis=0)


def unpack_small(pack, shapes):
    out, off = [], 0
    for shp in shapes:
        size = 1
        for s in shp:
            size *= s
        rows = _pack_rows(size)
        out.append(pack[off:off + rows].reshape(-1)[:size].reshape(shp))
        off += rows
    return out


def local_step(x, target, small, get_w, put_g, put_small):
    row = lambda v: v.reshape(1, -1)
    (wg1, wu1), deps = get_w(("wg1", "wu1"), ())
    sil1, gp1, s1, h1 = ffn_fwd(x, small["ffn1_norm_g"], wg1, wu1, None, "ffn1_fwd", deps)
    (wd1, win, wout), _ = get_w(("wd1", "win", "wout"), (s1,))
    qkv_t, zg_t, x1 = mix_in_fwd(x, small["mix_norm_g"], win, row(small["b_in"]), "mix_in_fwd", ffn_tail=(s1, wd1))
    ya_t = attn_fwd(qkv_t, small["attn_sinks"], "attn_fwd")
    lng, lnb = row(small["gmlp_ln_g"]), row(small["gmlp_ln_b"])
    w_s, b_s = small["gmlp_w_s"][0], small["gmlp_b_s"][0]
    yg_t = gmlp_fwd(zg_t, lng, lnb, w_s, b_s, "gmlp_fwd")
    gao, ggo = row(small["attn_out_norm_g"]), row(small["gmlp_out_norm_g"])
    x2 = mix_out_fwd(x1, ya_t, yg_t, gao, ggo, wout, small["b_out"], "mix_out_fwd")
    (wg2, wu2, wd2), _ = get_w(("wg2", "wu2", "wd2"), (x2,))
    gs = {}
    dx3, sil2, gp2, s2, h2f, gs["final_norm_g"], loss = ffn_fwd(
        x2, small["ffn2_norm_g"], wg2, wu2, wd2, "ffn2_fwd", head=(target, small["final_norm_g"].reshape(1, -1)))

    dx2, da, db, df, gs["ffn2_norm_g"] = ffn_bwd(x2, dx3, small["ffn2_norm_g"], sil2, gp2, wg2, wu2, wd2, "ffn2_bwd")
    deps = put_g({"wg2": grad_tn(da, h2f, 1.0, "ffn2_dwg"), "wu2": grad_tn(db, h2f, 1.0, "ffn2_dwu"),
                  "wd2": grad_tn(s2, df, 1.0, "ffn2_dwd")})

    dya_t, dyg_t, y_t, dgao, dggo, gs["b_out"] = mix_out_bwd(dx2, ya_t, yg_t, gao, ggo, wout, "mix_out_bwd", deps)
    gs["attn_out_norm_g"], gs["gmlp_out_norm_g"] = dgao.reshape(1, -1), dggo.reshape(1, -1)
    g_wout = grad_nn([y_t], dx2, "dwout")
    dzg_t, dws, dbs, dlng, dlnb = gmlp_bwd(zg_t, dyg_t, lng, lnb, w_s, b_s, "gmlp_bwd")
    gs["gmlp_w_s"], gs["gmlp_b_s"] = dws[None], dbs[None]
    gs["gmlp_ln_g"], gs["gmlp_ln_b"] = dlng.reshape(1, -1), dlnb.reshape(1, -1)
    dq_t, dk_t, dv_t, dsinks = attn_bwd(qkv_t, dya_t, small["attn_sinks"], "attn_bwd")
    gs["attn_sinks"] = dsinks.reshape(1, -1)
    dx1, h2, gs["mix_norm_g"], dbin = mix_in_bwd(x1, dx2, small["mix_norm_g"], dq_t, dk_t, dv_t, dzg_t, win,
                                                 "mix_in_bwd")
    gs["b_in"] = dbin.reshape(1, -1)
    deps = put_g({"wout": g_wout, "win": grad_nn([dq_t, dk_t, dv_t, dzg_t], h2, "dwin")})

    dx0, da, db, df, gs["ffn1_norm_g"] = ffn_bwd(x, dx1, small["ffn1_norm_g"], sil1, gp1, wg1, wu1, wd1, "ffn1_bwd",
                                             deps)
    deps = put_small(gs, loss)
    deps = put_g({"wg1": grad_tn(da, h1, 1.0, "ffn1_dwg", deps)})
    deps = put_g({"wu1": grad_tn(db, h1, 1.0, "ffn1_dwu", deps)})
    put_g({"wd1": grad_tn(s1, df, 1.0, "ffn1_dwd", deps)})
    return dx0


SMALL_NAMES = ["ffn1_norm_g", "mix_norm_g", "b_in", "attn_sinks", "gmlp_ln_g", "gmlp_ln_b", "gmlp_w_s", "gmlp_b_s",
               "attn_out_norm_g", "gmlp_out_norm_g", "b_out", "ffn2_norm_g", "final_norm_g"]
BIG_NAMES = {"wg1": ("ffn1_w_gate", True), "wu1": ("ffn1_w_up", True), "wd1": ("ffn1_w_down", False),
             "win": ("w_in", True), "wout": ("w_out", False),
             "wg2": ("ffn2_w_gate", True), "wu2": ("ffn2_w_up", True), "wd2": ("ffn2_w_down", False)}
WEIGHT_ORDER = ["ffn1_norm_g", "ffn1_w_gate", "ffn1_w_up", "ffn1_w_down", "mix_norm_g", "w_in", "b_in", "attn_sinks",
                "gmlp_ln_g", "gmlp_ln_b", "gmlp_w_s", "gmlp_b_s", "attn_out_norm_g", "gmlp_out_norm_g", "w_out",
                "b_out", "ffn2_norm_g", "ffn2_w_gate", "ffn2_w_up", "ffn2_w_down", "final_norm_g"]


def kernel(x, ffn1_norm_g, ffn1_w_gate, ffn1_w_up, ffn1_w_down, mix_norm_g, w_in, b_in, attn_sinks, gmlp_ln_g, gmlp_ln_b, gmlp_w_s, gmlp_b_s, attn_out_norm_g, gmlp_out_norm_g, w_out, b_out, ffn2_norm_g, ffn2_w_gate, ffn2_w_up, ffn2_w_down, final_norm_g, loss_target, m_ffn1_norm_g, m_ffn1_w_gate, m_ffn1_w_up, m_ffn1_w_down, m_mix_norm_g, m_w_in, m_b_in, m_attn_sinks, m_gmlp_ln_g, m_gmlp_ln_b, m_gmlp_w_s, m_gmlp_b_s, m_attn_out_norm_g, m_gmlp_out_norm_g, m_w_out, m_b_out, m_ffn2_norm_g, m_ffn2_w_gate, m_ffn2_w_up, m_ffn2_w_down, m_final_norm_g, v_ffn1_norm_g, v_ffn1_w_gate, v_ffn1_w_up, v_ffn1_w_down, v_mix_norm_g, v_w_in, v_b_in, v_attn_sinks, v_gmlp_ln_g, v_gmlp_ln_b, v_gmlp_w_s, v_gmlp_b_s, v_attn_out_norm_g, v_gmlp_out_norm_g, v_w_out, v_b_out, v_ffn2_norm_g, v_ffn2_w_gate, v_ffn2_w_up, v_ffn2_w_down, v_final_norm_g):
    args = dict(locals())
    w = {n: args[n] for n in WEIGHT_ORDER}
    m = {n: args["m_" + n] for n in WEIGHT_ORDER}
    v = {n: args["v_" + n] for n in WEIGHT_ORDER}

    d_model = x.shape[-1]
    flat = lambda g: g.reshape(-1, d_model)
    me = _mesh_place()[3].astype(jnp.int32).reshape(1)
    first = ("wg1", "wu1")
    later = (("wd1", "win", "wout"), ("wg2", "wu2", "wd2"))
    gathers, exchanges, last_token = {}, [], []

    order = list(first) + [k for grp in later for k in grp]
    mats = [w[BIG_NAMES[k][0]][0].T if BIG_NAMES[k][1] else w[BIG_NAMES[k][0]][0] for k in order]
    shard_list, land_list = prep_shards(mats, me, "prep_shards")
    shards, zones = dict(zip(order, shard_list)), dict(zip(order, land_list))

    def get_w(keys, after):
        if keys == first:
            full = allgather_rows([shards[k] for k in first], [zones[k] for k in first], "allgather_ffn1")
            started, token = split_start([([shards[k] for k in grp], [zones[k] for k in grp]) for grp in later],
                                         gather_slices, (full[0],), "gather_start")
            gathers.update(zip(later, started))
            return [flat(g) for g in full], (token,)
        _, lands = split_wait(gathers[keys], gather_slices, after, "gather_wait_" + keys[0])
        return [flat(land) for land in lands], ()

    def put_g(gb):
        keys = tuple(gb)
        g3 = [gb[k].reshape(N_DEV, -1, d_model) for k in keys]
        lands = [lax.empty((N_DEV - 1,) + g.shape[1:], BF16) for g in g3]
        started, token = split_start([(g3, lands)], exchange_slices, (), "exchange_start_" + keys[0])
        exchanges.append((keys, started[0]))
        last_token[:] = [token]
        return (token,)

    small_started = []

    def put_small(gs, loss):
        pack = pack_small([gs[n] for n in SMALL_NAMES] + [loss[:, :1]])
        started, token = split_start([([pack], [place_own(pack, me, "place_small")])], gather_slices, (), "small_start")
        small_started[:] = started
        return (token,)

    small = {n: w[n] for n in SMALL_NAMES}
    grad_x = local_step(x[0], loss_target[0], small, get_w, put_g, put_small)

    grads, deltas, new_m, new_v = {}, {}, {}, {}
    shapes = [w[n].shape for n in SMALL_NAMES]
    _, (slots,) = split_wait(small_started[0], gather_slices, tuple(last_token), "small_wait")
    total = sum_slots(slots, "sum_small")
    small_g = unpack_small(total, shapes + [(1, 1)])
    loss_total = small_g[-1].reshape(())
    d_, m_, v_ = adamw_many([w[n] for n in SMALL_NAMES], small_g[:-1], [m[n] for n in SMALL_NAMES],
                            [v[n] for n in SMALL_NAMES], "adamw_small")
    for n, g_, dd, mm, vv in zip(SMALL_NAMES, small_g[:-1], d_, m_, v_):
        grads[n], deltas[n], new_m[n], new_v[n] = g_, dd, mm, vv

    after = tuple(arr for n in SMALL_NAMES for arr in (grads[n], deltas[n], new_m[n], new_v[n]))
    for keys, started in exchanges:
        g3, lands = split_wait(started, exchange_slices, after, "exchange_wait_" + keys[0])
        for key, own, land in zip(keys, g3, lands):
            pname, transposed = BIG_NAMES[key]
            to_rows = (lambda p: p[0].T) if transposed else (lambda p: p[0])
            from_rows = (lambda r: r.T[None]) if transposed else (lambda r: r[None])
            g, d_, m_, v_ = sum_adamw(land, own, me, to_rows(w[pname]), to_rows(m[pname]), to_rows(v[pname]),
                                      "adamw_" + key)
            after = after + (g,)
            grads[pname], deltas[pname], new_m[pname], new_v[pname] = (from_rows(g), from_rows(d_), from_rows(m_),
                                                                        from_rows(v_))

    return (loss_total, grad_x[None], *[grads[n] for n in WEIGHT_ORDER], *[deltas[n] for n in WEIGHT_ORDER],
            *[new_m[n] for n in WEIGHT_ORDER], *[new_v[n] for n in WEIGHT_ORDER])
```

```python
import functools

import jax
import jax.numpy as jnp
from jax import lax
from jax.experimental import pallas as pl
from jax.experimental.pallas import tpu as pltpu

F32 = jnp.float32
BF16 = jnp.bfloat16

N_DEV = 8
N_Q_HEADS = 8
N_KV_HEADS = 2
HEAD_DIM = 64
ATTN_WIDTH = N_Q_HEADS * HEAD_DIM
KV_WIDTH = N_KV_HEADS * HEAD_DIM
QKV_WIDTH = ATTN_WIDTH + 2 * KV_WIDTH
BLOCK = 128
GMLP_GROUPS = 8
GMLP_GROUP_DIM = 64
GMLP_WIDTH = GMLP_GROUPS * GMLP_GROUP_DIM
EPS = 1e-6
FFN_RES = 0.5
ATTN_SCALE = HEAD_DIM ** -0.5
MASK_VALUE = -1e30

ADAM_LR = 0.001
ADAM_B1 = 0.9
ADAM_B2 = 0.999
ADAM_EPS = 1e-08
ADAM_WD = 0.01
ADAM_STEP = 10

V7X_VMEM_BYTES = 64 * 1024 * 1024
VMEM_LIMIT = 62 * 1024 * 1024
LANES = 128
MXU_WIDTH = 256
FFN_FWD_TOKENS = 512
FFN_BWD_TOKENS = 256

NT = (((1,), (1,)), ((), ()))
TN = (((0,), (0,)), ((), ()))


def _dot(a, b):
    return jnp.dot(a, b, preferred_element_type=F32)


def _dg(a, b, dims):
    return lax.dot_general(a, b, dims, preferred_element_type=F32)


def _params(semantics=None):
    return pltpu.CompilerParams(dimension_semantics=semantics, vmem_limit_bytes=VMEM_LIMIT)


def _resident(shape):
    zeros = (0,) * len(shape)
    return pl.BlockSpec(shape, lambda *_: zeros, pipeline_mode=pl.Buffered(1))


def _drop_deps(body, n_in, n_deps):
    return lambda *refs: body(*refs[:n_in], *refs[n_in + n_deps:])


ANY_SPEC = pl.BlockSpec(memory_space=pl.ANY)


TOKEN_TILE = 1024
X_RING_SLOTS = 3
GRAD_TOKENS_PER_STEP = 2048
GRAD_ROW_TILE_CAP = 1408
N_GATHER_ROLES = 13


def _token_tile(t):
    return min(t, TOKEN_TILE)


def _f_chunk(f):
    for c in (256, 128):
        if f % c == 0:
            return c
    return f


def _loss_head(xv, target, gain):
    r = lax.rsqrt(jnp.mean(xv * xv, axis=-1, keepdims=True) + EPS)
    err = xv * r * gain - target
    loss = 0.5 * jnp.sum(jnp.mean(err * err, axis=-1, keepdims=True), axis=0, keepdims=True)
    dx, dgain = _rmsnorm_bwd_rows(xv, r, gain, err * (1.0 / xv.shape[-1]))
    return dx, dgain, loss


def ffn_fwd(x, gain, wg_t, wu_t, wd, name, deps=(), head=None):
    t, d = x.shape
    f = wg_t.shape[0]
    tm = min(t, FFN_FWD_TOKENS)
    fc = _f_chunk(f)
    weights = [wg_t, wu_t] + ([] if wd is None else [wd])
    n_in = 2 + len(weights) + (0 if head is None else 2)
    n_x = 0 if wd is None else 1

    def body(x_ref, g_ref, wg_ref, wu_ref, *rest):
        sil_ref, gp_ref, s_ref, h_ref = rest[n_in - 4 + n_x:n_in + n_x]
        xv = x_ref[...]
        r = lax.rsqrt(jnp.mean(xv * xv, axis=-1, keepdims=True) + EPS)
        h = (xv * r * g_ref[...]).astype(BF16)
        h_ref[...] = h
        for c in range(f // fc):
            sl = slice(c * fc, (c + 1) * fc)
            a = _dg(h, wg_ref[sl, :], NT)
            b = _dg(h, wu_ref[sl, :], NT)
            sig = jax.nn.sigmoid(a)
            sil = a * sig
            sil_ref[:, sl] = sil.astype(BF16)
            gp_ref[:, sl] = (b * (sig + sil - sil * sig)).astype(BF16)
            s_ref[:, sl] = (sil * b).astype(BF16)
        if wd is None:
            return
        wd_ref, xo_ref = rest[0], rest[n_in - 4]
        xo = xv + FFN_RES * _dot(s_ref[...], wd_ref[...])
        if head is None:
            xo_ref[...] = xo
        else:
            t_ref, hg_ref = rest[1:3]
            dg_ref, loss_ref = rest[-2:]

            @pl.when(pl.program_id(0) == 0)
            def _():
                dg_ref[...] = jnp.zeros_like(dg_ref)
                loss_ref[...] = jnp.zeros_like(loss_ref)

            dx, dgain, loss = _loss_head(xo, t_ref[...], hg_ref[...])
            xo_ref[...] = dx
            dg_ref[...] += dgain
            loss_ref[...] += loss

    tok = lambda: pl.BlockSpec((tm, d), lambda i: (i, 0))
    wide = lambda: pl.BlockSpec((tm, f), lambda i: (i, 0))
    const2 = lambda i: (0, 0)
    in_specs = [tok(), _resident((1, d))] + [_resident((f, d)) for _ in weights]
    out_specs = [tok()] * n_x + [wide(), wide(), wide(), tok()]
    out_shape = ([jax.ShapeDtypeStruct((t, d), F32)] * n_x + [jax.ShapeDtypeStruct((t, f), BF16)] * 3
                 + [jax.ShapeDtypeStruct((t, d), BF16)])
    operands = [x, gain] + weights
    if head is not None:
        in_specs += [tok(), _resident((1, d))]
        out_specs += [pl.BlockSpec((1, d), const2), pl.BlockSpec((1, LANES), const2)]
        out_shape += [jax.ShapeDtypeStruct((1, d), F32), jax.ShapeDtypeStruct((1, LANES), F32)]
        operands += list(head)
    return pl.pallas_call(
        _drop_deps(body, n_in, len(deps)), name=name, grid=(t // tm,),
        in_specs=in_specs + [ANY_SPEC] * len(deps), out_specs=out_specs, out_shape=out_shape,
        compiler_params=_params(("arbitrary",)),
    )(*operands, *deps)


def _rmsnorm_bwd_rows(xv, r, gain, dh):
    n = xv * r
    dgain = jnp.sum(dh * n, axis=0, keepdims=True)
    dn = dh * gain
    dx = r * (dn - n * jnp.mean(dn * n, axis=-1, keepdims=True))
    return dx, dgain


def ffn_bwd(x, dxo, gain, sil, gp, wg_t, wu_t, wd, name, deps=()):
    t, d = x.shape
    f = wd.shape[0]
    tm = min(t, FFN_BWD_TOKENS)
    nt = t // tm
    chunks = [(off, min(MXU_WIDTH, f - off)) for off in range(0, f, MXU_WIDTH)]

    def body(xp_ref, dxp_ref, dxo_ref, g_ref, sil_ref, gp_ref, wg_ref, wu_ref, wd_ref,
             dx_ref, da_ref, db_ref, df_ref, dg_ref, dh_ref):
        i = pl.program_id(0)
        gain_v = g_ref[...]

        @pl.when(i == 0)
        def _():
            dh_ref[...] = jnp.zeros_like(dh_ref)
            dg_ref[...] = jnp.zeros_like(dg_ref)

        def finish_previous(slot):
            xv = xp_ref[...]
            r = lax.rsqrt(jnp.mean(xv * xv, axis=-1, keepdims=True) + EPS)
            dx, dgain = _rmsnorm_bwd_rows(xv, r, gain_v, dh_ref[slot])
            dx_ref[...] = dxp_ref[...] + dx
            dg_ref[...] += dgain

        @pl.when(i < nt)
        def _():
            df = (FFN_RES * dxo_ref[...]).astype(BF16)
            df_ref[...] = df
            for off, size in chunks:
                sl = slice(off, off + size)
                ds = _dg(df, wd_ref[sl, :], NT)
                da_ref[:, sl] = (ds * gp_ref[:, sl].astype(F32)).astype(BF16)
                db_ref[:, sl] = (ds * sil_ref[:, sl].astype(F32)).astype(BF16)
            dh = _dot(da_ref[...], wg_ref[...]) + _dot(db_ref[...], wu_ref[...])
            finish_previous((i + 1) % 2)
            dh_ref[i % 2] = dh

        @pl.when(i == nt)
        def _():
            finish_previous((nt - 1) % 2)

    prev = lambda i: (jnp.maximum(i - 1, 0), 0)
    cur = lambda i: (jnp.minimum(i, nt - 1), 0)
    return pl.pallas_call(
        _drop_deps(body, 9, len(deps)), name=name, grid=(nt + 1,),
        in_specs=[pl.BlockSpec((tm, d), prev), pl.BlockSpec((tm, d), prev), pl.BlockSpec((tm, d), cur),
                  _resident((1, d)), pl.BlockSpec((tm, f), cur), pl.BlockSpec((tm, f), cur),
                  _resident((f, d)), _resident((f, d)), _resident((f, d))] + [ANY_SPEC] * len(deps),
        out_specs=[pl.BlockSpec((tm, d), prev), pl.BlockSpec((tm, f), cur), pl.BlockSpec((tm, f), cur),
                   pl.BlockSpec((tm, d), cur), pl.BlockSpec((1, d), lambda i: (0, 0))],
        out_shape=[jax.ShapeDtypeStruct((t, d), F32), jax.ShapeDtypeStruct((t, f), BF16),
                   jax.ShapeDtypeStruct((t, f), BF16), jax.ShapeDtypeStruct((t, d), BF16),
                   jax.ShapeDtypeStruct((1, d), F32)],
        scratch_shapes=[pltpu.VMEM((2, tm, d), F32)],
        compiler_params=_params(("arbitrary",)),
    )(x, dxo, dxo, gain, sil, gp, wg_t, wu_t, wd, *deps)


def _row_tile(m, cap):
    if m <= cap:
        return m
    best = None
    for bm in range(LANES, cap + 1, LANES):
        if m % bm == 0:
            best = bm
    return best if best is not None else m


def grad_tn(a, b, scale, name, deps=()):
    t, m = a.shape
    n = b.shape[1]
    bm = _row_tile(m, GRAD_ROW_TILE_CAP)
    bk = min(t, GRAD_TOKENS_PER_STEP)
    nk = t // bk

    def body(a_ref, b_ref, o_ref, acc_ref):
        k = pl.program_id(1)

        @pl.when(k == 0)
        def _():
            acc_ref[...] = jnp.zeros_like(acc_ref)

        acc_ref[...] += _dg(a_ref[...].astype(BF16), b_ref[...].astype(BF16), TN)

        @pl.when(k == nk - 1)
        def _():
            o_ref[...] = (scale * acc_ref[...]).astype(BF16)

    return pl.pallas_call(
        _drop_deps(body, 2, len(deps)), name=name, grid=(m // bm, nk),
        in_specs=[pl.BlockSpec((bk, bm), lambda i, k: (k, i)), pl.BlockSpec((bk, n), lambda i, k: (k, 0))]
        + [ANY_SPEC] * len(deps),
        out_specs=pl.BlockSpec((bm, n), lambda i, k: (i, 0)),
        out_shape=jax.ShapeDtypeStruct((m, n), BF16),
        scratch_shapes=[pltpu.VMEM((bm, n), F32)],
        compiler_params=_params(("arbitrary", "arbitrary")),
    )(a, b, *deps)


def grad_nn(a_list, b, name):
    t, n = b.shape
    ms = [a.shape[0] for a in a_list]
    m = sum(ms)
    bk = min(t, GRAD_TOKENS_PER_STEP)
    nk = t // bk
    na = len(a_list)

    def body(*refs):
        a_refs, b_ref, o_ref, acc_ref = refs[:na], refs[na], refs[na + 1], refs[na + 2]
        k = pl.program_id(0)

        @pl.when(k == 0)
        def _():
            acc_ref[...] = jnp.zeros_like(acc_ref)

        bv = b_ref[...].astype(BF16)
        off = 0
        for a_ref, mi in zip(a_refs, ms):
            acc_ref[off:off + mi, :] += _dot(a_ref[...].astype(BF16), bv)
            off += mi

        @pl.when(k == nk - 1)
        def _():
            o_ref[...] = acc_ref[...].astype(BF16)

    return pl.pallas_call(
        body, name=name, grid=(nk,),
        in_specs=[pl.BlockSpec((mi, bk), lambda k: (0, k)) for mi in ms] + [pl.BlockSpec((bk, n), lambda k: (k, 0))],
        out_specs=pl.BlockSpec((m, n), lambda k: (0, 0)),
        out_shape=jax.ShapeDtypeStruct((m, n), BF16),
        scratch_shapes=[pltpu.VMEM((m, n), F32)],
        compiler_params=_params(("arbitrary",)),
    )(*a_list, b)


def mix_in_fwd(x, gain, win_t, b_in_row, name, ffn_tail=None):
    t, d = x.shape
    w = win_t.shape[0]
    tm = _token_tile(t)

    def body(x_ref, g_ref, w_ref, bias_ref, *rest):
        xv = x_ref[...]
        if ffn_tail is not None:
            s_ref, wd_ref, xo_ref = rest[0], rest[1], rest[4]
            xv = xv + FFN_RES * _dot(s_ref[...], wd_ref[...])
            xo_ref[...] = xv
        qkv_ref, zg_ref = rest[-3:-1] if ffn_tail is not None else rest
        r = lax.rsqrt(jnp.mean(xv * xv, axis=-1, keepdims=True) + EPS)
        h = (xv * r * g_ref[...]).astype(BF16)
        bias = _as_column(bias_ref[...])
        qkv_ref[...] = (_dg(w_ref[:QKV_WIDTH, :], h, NT) + bias[:QKV_WIDTH, :]).astype(BF16)
        zg_ref[...] = (_dg(w_ref[QKV_WIDTH:, :], h, NT) + bias[QKV_WIDTH:, :]).astype(BF16)

    tok = lambda: pl.BlockSpec((tm, d), lambda i: (i, 0))
    in_specs = [tok(), _resident((1, d)), _resident((w, d)), _resident((1, w))]
    out_specs = [pl.BlockSpec((QKV_WIDTH, tm), lambda i: (0, i)), pl.BlockSpec((w - QKV_WIDTH, tm), lambda i: (0, i))]
    out_shape = [jax.ShapeDtypeStruct((QKV_WIDTH, t), BF16), jax.ShapeDtypeStruct((w - QKV_WIDTH, t), BF16)]
    operands = [x, gain, win_t, b_in_row]
    if ffn_tail is not None:
        f = ffn_tail[1].shape[0]
        in_specs += [pl.BlockSpec((tm, f), lambda i: (i, 0)), _resident((f, d))]
        out_specs.append(tok())
        out_shape.append(jax.ShapeDtypeStruct((t, d), F32))
        operands += list(ffn_tail)
    return pl.pallas_call(
        body, name=name, grid=(t // tm,), in_specs=in_specs, out_specs=out_specs, out_shape=out_shape,
        compiler_params=_params(("arbitrary",)),
    )(*operands)


ATTN_REP = N_Q_HEADS // N_KV_HEADS
ATTN_BLOCKS_PER_STEP = 8


def _attn_tiles(t):
    nbs = min(ATTN_BLOCKS_PER_STEP, t // BLOCK)
    return nbs, nbs * BLOCK, t // (nbs * BLOCK)


def _attn_specs(nbs, tiles):
    last = tiles - 1
    cur = lambda n: jnp.minimum(n, last)
    prev = lambda n: jnp.maximum(jnp.minimum(n, last) * nbs - 1, 0)
    k_row = ATTN_WIDTH // KV_WIDTH
    tw = nbs * BLOCK
    return [
        pl.BlockSpec((ATTN_WIDTH, tw), lambda n: (0, cur(n))),
        pl.BlockSpec((KV_WIDTH, BLOCK), lambda n: (k_row, prev(n))),
        pl.BlockSpec((KV_WIDTH, tw), lambda n: (k_row, cur(n))),
        pl.BlockSpec((KV_WIDTH, BLOCK), lambda n: (k_row + 1, prev(n))),
        pl.BlockSpec((KV_WIDTH, tw), lambda n: (k_row + 1, cur(n))),
    ]


def _cols(b):
    return slice(b * BLOCK, (b + 1) * BLOCK)


def _band_rows(prev_ref, tile_ref, gs, b):
    before = prev_ref[gs, :] if b == 0 else tile_ref[gs, _cols(b - 1)]
    return before, tile_ref[gs, _cols(b)]


def _group_rows(ref, g, b):
    first = g * ATTN_REP
    return jnp.concatenate([ref[(first + r) * HEAD_DIM:(first + r + 1) * HEAD_DIM, _cols(b)] for r in range(ATTN_REP)],
                           axis=1)


def _ungroup_rows(ref, g, b, val):
    first = g * ATTN_REP
    for r in range(ATTN_REP):
        ref[(first + r) * HEAD_DIM:(first + r + 1) * HEAD_DIM, _cols(b)] = val[:, r * BLOCK:(r + 1) * BLOCK]


def _attn_upper():
    key = lax.broadcasted_iota(jnp.int32, (BLOCK, ATTN_REP * BLOCK), 0)
    qry = lax.broadcasted_iota(jnp.int32, (BLOCK, ATTN_REP * BLOCK), 1) & (BLOCK - 1)
    return key > qry


def _attn_probs(q, kp, kc, sink_ref, g, upper, no_prev):
    s = jnp.where(upper, _dg(kp, q, TN), _dg(kc, q, TN)) * ATTN_SCALE
    if no_prev is not None:
        s = jnp.where(upper & no_prev, MASK_VALUE, s)
    sink = jnp.concatenate([jnp.full((1, BLOCK), sink_ref[0, g * ATTN_REP + r], F32) for r in range(ATTN_REP)], axis=1)
    m = jnp.maximum(jnp.max(s, axis=0, keepdims=True), sink)
    e = jnp.exp(s - m)
    es = jnp.exp(sink - m)
    inv = 1.0 / (jnp.sum(e, axis=0, keepdims=True) + es)
    return e * inv, es * inv


def _split_band(upper, val):
    return jnp.where(upper, val, 0.0).astype(BF16), jnp.where(upper, 0.0, val).astype(BF16)


def attn_fwd(qkv_t, sinks, name):
    t = qkv_t.shape[1]
    nbs, tw, tiles = _attn_tiles(t)

    def body(sink_ref, q_ref, kp_ref, kc_ref, vp_ref, vc_ref, y_ref):
        n = pl.program_id(0)
        upper = _attn_upper()
        for b in range(nbs):
            for g in range(N_KV_HEADS):
                gs = slice(g * HEAD_DIM, (g + 1) * HEAD_DIM)
                kp, kc = _band_rows(kp_ref, kc_ref, gs, b)
                vp, vc = _band_rows(vp_ref, vc_ref, gs, b)
                p, _ = _attn_probs(_group_rows(q_ref, g, b), kp, kc, sink_ref, g, upper, n == 0 if b == 0 else None)
                pp, pc = _split_band(upper, p)
                _ungroup_rows(y_ref, g, b, (_dot(vp, pp) + _dot(vc, pc)).astype(BF16))

    return pl.pallas_call(
        body, name=name, grid=(tiles,),
        in_specs=[pl.BlockSpec(memory_space=pltpu.SMEM)] + _attn_specs(nbs, tiles),
        out_specs=pl.BlockSpec((ATTN_WIDTH, tw), lambda n: (0, n)),
        out_shape=jax.ShapeDtypeStruct((ATTN_WIDTH, t), BF16),
        compiler_params=_params(("arbitrary",)),
    )(sinks, qkv_t, qkv_t, qkv_t, qkv_t, qkv_t)


def attn_bwd(qkv_t, dy_t, sinks, name):
    t = qkv_t.shape[1]
    nbs, tw, tiles = _attn_tiles(t)
    kept = slice(0, tw - BLOCK)

    def body(sink_ref, q_ref, kp_ref, kc_ref, vp_ref, vc_ref, do_ref, dq_ref, dk_ref, dv_ref, dsink_ref,
             ck_ref, cv_ref, sacc_ref):
        n = pl.program_id(0)

        @pl.when(n == 0)
        def _():
            sacc_ref[...] = jnp.zeros_like(sacc_ref)

        @pl.when((n > 0) & (n < tiles))
        def _():
            if nbs > 1:
                dk_ref[:, kept] = ck_ref[:, kept].astype(BF16)
                dv_ref[:, kept] = cv_ref[:, kept].astype(BF16)

        @pl.when(n < tiles)
        def _():
            upper = _attn_upper()
            for b in range(nbs):
                for g in range(N_KV_HEADS):
                    gs = slice(g * HEAD_DIM, (g + 1) * HEAD_DIM)
                    kp, kc = _band_rows(kp_ref, kc_ref, gs, b)
                    vp, vc = _band_rows(vp_ref, vc_ref, gs, b)
                    q = _group_rows(q_ref, g, b)
                    do = _group_rows(do_ref, g, b)
                    p, ps = _attn_probs(q, kp, kc, sink_ref, g, upper, n == 0 if b == 0 else None)
                    dp = jnp.where(upper, _dg(vp, do, TN), _dg(vc, do, TN))
                    delta = jnp.sum(p * dp, axis=0, keepdims=True)
                    dsink = -(ps * delta)
                    for r in range(ATTN_REP):
                        hd = g * ATTN_REP + r
                        sacc_ref[hd:hd + 1, :] += dsink[:, r * BLOCK:(r + 1) * BLOCK]
                    dsp, dsc = _split_band(upper, p * (dp - delta))
                    pp, pc = _split_band(upper, p)
                    _ungroup_rows(dq_ref, g, b, (ATTN_SCALE * (_dot(kp, dsp) + _dot(kc, dsc))).astype(BF16))
                    dk_before = ATTN_SCALE * _dg(q, dsp, NT)
                    dv_before = _dg(do, pp, NT)
                    if b == 0:
                        @pl.when(n > 0)
                        def _():
                            dk_ref[gs, _cols(nbs - 1)] = (ck_ref[gs, _cols(nbs - 1)] + dk_before).astype(BF16)
                            dv_ref[gs, _cols(nbs - 1)] = (cv_ref[gs, _cols(nbs - 1)] + dv_before).astype(BF16)
                    else:
                        ck_ref[gs, _cols(b - 1)] += dk_before
                        cv_ref[gs, _cols(b - 1)] += dv_before
                    ck_ref[gs, _cols(b)] = ATTN_SCALE * _dg(q, dsc, NT)
                    cv_ref[gs, _cols(b)] = _dg(do, pc, NT)

        @pl.when(n == tiles)
        def _():
            dk_ref[...] = ck_ref[...].astype(BF16)
            dv_ref[...] = cv_ref[...].astype(BF16)
            dsink_ref[...] = jnp.sum(sacc_ref[...], axis=1, keepdims=True)

    last = tiles - 1
    done = lambda n: jnp.maximum(n - 1, 0)
    outs = pl.pallas_call(
        body, name=name, grid=(tiles + 1,),
        in_specs=[pl.BlockSpec(memory_space=pltpu.SMEM)] + _attn_specs(nbs, tiles)
        + [pl.BlockSpec((ATTN_WIDTH, tw), lambda n: (0, jnp.minimum(n, last)))],
        out_specs=[pl.BlockSpec((ATTN_WIDTH, tw), lambda n: (0, jnp.minimum(n, last))),
                   pl.BlockSpec((KV_WIDTH, tw), lambda n: (0, done(n))),
                   pl.BlockSpec((KV_WIDTH, tw), lambda n: (0, done(n))),
                   pl.BlockSpec((N_Q_HEADS, 1), lambda n: (0, 0))],
        out_shape=[jax.ShapeDtypeStruct((ATTN_WIDTH, t), BF16), jax.ShapeDtypeStruct((KV_WIDTH, t), BF16),
                   jax.ShapeDtypeStruct((KV_WIDTH, t), BF16), jax.ShapeDtypeStruct((N_Q_HEADS, 1), F32)],
        scratch_shapes=[pltpu.VMEM((KV_WIDTH, tw), F32), pltpu.VMEM((KV_WIDTH, tw), F32),
                        pltpu.VMEM((N_Q_HEADS, BLOCK), F32)],
        compiler_params=_params(("arbitrary",)),
    )(sinks, qkv_t, qkv_t, qkv_t, qkv_t, qkv_t, dy_t)
    return outs


GELU_C = 0.7978845608028654
GELU_A = 0.044715


def _gelu(x):
    return 0.5 * x * (1.0 + jnp.tanh(GELU_C * (x + GELU_A * x * x * x)))


def _gelu_grad(x):
    th = jnp.tanh(GELU_C * (x + GELU_A * x * x * x))
    return 0.5 * (1.0 + th) + 0.5 * x * (1.0 - th * th) * GELU_C * (1.0 + 3.0 * GELU_A * x * x)


def _gmlp_parts(zg, lng, lnb):
    z = _gelu(zg)
    u = z[:GMLP_WIDTH, :]
    vv = z[GMLP_WIDTH:, :]
    mu = jnp.mean(vv, axis=0, keepdims=True)
    xc = vv - mu
    rstd = lax.rsqrt(jnp.mean(xc * xc, axis=0, keepdims=True) + EPS)
    xhat = xc * rstd
    return u, xhat, rstd, xhat * lng + lnb


def _as_column(row):
    return jnp.broadcast_to(row, (LANES, row.shape[1])).T[:, :1]


def _row_sums(acc):
    return jnp.sum(acc.T, axis=0, keepdims=True)


def _causal(w):
    row = lax.broadcasted_iota(jnp.int32, (BLOCK, BLOCK), 0)
    col = lax.broadcasted_iota(jnp.int32, (BLOCK, BLOCK), 1)
    return jnp.where(col <= row, w, 0.0)


GMLP_CHUNKS_PER_STEP = 8


def _stack_chunks(v, nc):
    return jnp.concatenate([v[:, c * BLOCK:(c + 1) * BLOCK] for c in range(nc)], axis=0)


def _unstack_chunks(v, nc):
    rows = v.shape[0] // nc
    return jnp.concatenate([v[c * rows:(c + 1) * rows, :] for c in range(nc)], axis=1)


def gmlp_fwd(zg_t, lng, lnb, w_s, b_s, name):
    t = zg_t.shape[1]
    nc = min(GMLP_CHUNKS_PER_STEP, t // BLOCK)
    tw = nc * BLOCK

    def body(zg_ref, lng_ref, lnb_ref, w_ref, bs_ref, y_ref):
        u, _, _, vn = _gmlp_parts(zg_ref[...].astype(F32), _as_column(lng_ref[...]), _as_column(lnb_ref[...]))
        for g in range(GMLP_GROUPS):
            gs = slice(g * GMLP_GROUP_DIM, (g + 1) * GMLP_GROUP_DIM)
            wc = _causal(w_ref[g]).astype(BF16)
            mixed = _dg(_stack_chunks(vn[gs, :].astype(BF16), nc), wc, NT) + bs_ref[g:g + 1, :]
            y_ref[gs, :] = (u[gs, :] * _unstack_chunks(mixed, nc)).astype(BF16)

    return pl.pallas_call(
        body, name=name, grid=(t // tw,),
        in_specs=[pl.BlockSpec((2 * GMLP_WIDTH, tw), lambda n: (0, n)), _resident((1, GMLP_WIDTH)),
                  _resident((1, GMLP_WIDTH)), _resident((GMLP_GROUPS, BLOCK, BLOCK)), _resident((GMLP_GROUPS, BLOCK))],
        out_specs=pl.BlockSpec((GMLP_WIDTH, tw), lambda n: (0, n)),
        out_shape=jax.ShapeDtypeStruct((GMLP_WIDTH, t), BF16),
        compiler_params=_params(("arbitrary",)),
    )(zg_t, lng, lnb, w_s, b_s)


def gmlp_bwd(zg_t, dy_t, lng, lnb, w_s, b_s, name):
    t = zg_t.shape[1]
    nc = min(GMLP_CHUNKS_PER_STEP, t // BLOCK)
    tw = nc * BLOCK
    nt = t // tw

    def body(zg_ref, dy_ref, lng_ref, lnb_ref, w_ref, bs_ref, dzg_ref, dw_ref, dbs_ref, dlng_ref, dlnb_ref,
             gacc_ref, bacc_ref):
        n = pl.program_id(0)

        @pl.when(n == 0)
        def _():
            dw_ref[...] = jnp.zeros_like(dw_ref)
            dbs_ref[...] = jnp.zeros_like(dbs_ref)
            gacc_ref[...] = jnp.zeros_like(gacc_ref)
            bacc_ref[...] = jnp.zeros_like(bacc_ref)

        zg = zg_ref[...].astype(F32)
        lng_v = _as_column(lng_ref[...])
        u, xhat, rstd, vn = _gmlp_parts(zg, lng_v, _as_column(lnb_ref[...]))
        dy = dy_ref[...].astype(F32)
        dvn_parts = []
        for g in range(GMLP_GROUPS):
            gs = slice(g * GMLP_GROUP_DIM, (g + 1) * GMLP_GROUP_DIM)
            wc = _causal(w_ref[g]).astype(BF16)
            vn_s = _stack_chunks(vn[gs, :].astype(BF16), nc)
            mixed = _unstack_chunks(_dg(vn_s, wc, NT) + bs_ref[g:g + 1, :], nc)
            dy_g = dy[gs, :]
            dmixed = dy_g * u[gs, :]
            dm_s = _stack_chunks(dmixed.astype(BF16), nc)
            dzg_ref[gs, :] = (dy_g * mixed * _gelu_grad(zg[gs, :])).astype(BF16)
            dw_ref[g] += _causal(_dg(dm_s, vn_s, TN))
            dbs_ref[g:g + 1, :] += _lane_fold(jnp.sum(dmixed, axis=0, keepdims=True))
            dvn_parts.append(_unstack_chunks(_dot(dm_s, wc), nc))
        dvn = jnp.concatenate(dvn_parts, axis=0)
        gacc_ref[...] += _lane_fold(dvn * xhat)
        bacc_ref[...] += _lane_fold(dvn)
        dxhat = dvn * lng_v
        m1 = jnp.mean(dxhat, axis=0, keepdims=True)
        m2 = jnp.mean(dxhat * xhat, axis=0, keepdims=True)
        dvv = rstd * (dxhat - m1 - xhat * m2)
        dzg_ref[GMLP_WIDTH:, :] = (dvv * _gelu_grad(zg[GMLP_WIDTH:, :])).astype(BF16)

        @pl.when(n == nt - 1)
        def _():
            dlng_ref[...] = _row_sums(gacc_ref[...])
            dlnb_ref[...] = _row_sums(bacc_ref[...])

    const2 = lambda n: (0, 0)
    return pl.pallas_call(
        body, name=name, grid=(nt,),
        in_specs=[pl.BlockSpec((2 * GMLP_WIDTH, tw), lambda n: (0, n)), pl.BlockSpec((GMLP_WIDTH, tw), lambda n: (0, n)),
                  _resident((1, GMLP_WIDTH)), _resident((1, GMLP_WIDTH)),
                  _resident((GMLP_GROUPS, BLOCK, BLOCK)), _resident((GMLP_GROUPS, BLOCK))],
        out_specs=[pl.BlockSpec((2 * GMLP_WIDTH, tw), lambda n: (0, n)),
                   pl.BlockSpec((GMLP_GROUPS, BLOCK, BLOCK), lambda n: (0, 0, 0)),
                   pl.BlockSpec((GMLP_GROUPS, BLOCK), const2),
                   pl.BlockSpec((1, GMLP_WIDTH), const2), pl.BlockSpec((1, GMLP_WIDTH), const2)],
        out_shape=[jax.ShapeDtypeStruct((2 * GMLP_WIDTH, t), BF16),
                   jax.ShapeDtypeStruct((GMLP_GROUPS, BLOCK, BLOCK), F32),
                   jax.ShapeDtypeStruct((GMLP_GROUPS, BLOCK), F32),
                   jax.ShapeDtypeStruct((1, GMLP_WIDTH), F32), jax.ShapeDtypeStruct((1, GMLP_WIDTH), F32)],
        scratch_shapes=[pltpu.VMEM((GMLP_WIDTH, BLOCK), F32), pltpu.VMEM((GMLP_WIDTH, BLOCK), F32)],
        compiler_params=_params(("arbitrary",)),
    )(zg_t, dy_t, lng, lnb, w_s, b_s)


def _rmsnorm_cols(y, gain_col):
    r = lax.rsqrt(jnp.mean(y * y, axis=0, keepdims=True) + EPS)
    n = y * r
    return n, r, n * gain_col


def mix_out_fwd(x, ya_t, yg_t, gao, ggo, wout, b_out, name):
    t, d = x.shape
    tm = _token_tile(t)
    nt = t // tm

    def body(x_hbm, ya_ref, yg_ref, gao_ref, ggo_ref, w_ref, b_ref, o_ref, x_ring, sem):
        i = pl.program_id(0)

        def x_copy(step):
            slot = step % X_RING_SLOTS
            return pltpu.make_async_copy(x_hbm.at[pl.ds(pl.multiple_of(step * tm, tm), tm), :], x_ring.at[slot],
                                         sem.at[slot])

        @pl.when(i == 0)
        def _():
            for step in range(min(X_RING_SLOTS - 1, nt)):
                x_copy(step).start()

        @pl.when(i + X_RING_SLOTS - 1 < nt)
        def _():
            x_copy(i + X_RING_SLOTS - 1).start()

        _, _, ya = _rmsnorm_cols(ya_ref[...].astype(F32), _as_column(gao_ref[...]))
        _, _, yg = _rmsnorm_cols(yg_ref[...].astype(F32), _as_column(ggo_ref[...]))
        mixed = _dg(ya.astype(BF16), w_ref[:ATTN_WIDTH, :], TN) + _dg(yg.astype(BF16), w_ref[ATTN_WIDTH:, :], TN)
        x_copy(i).wait()
        o_ref[...] = x_ring[i % X_RING_SLOTS] + mixed + b_ref[...]

    return pl.pallas_call(
        body, name=name, grid=(nt,),
        in_specs=[ANY_SPEC, pl.BlockSpec((ATTN_WIDTH, tm), lambda i: (0, i)),
                  pl.BlockSpec((GMLP_WIDTH, tm), lambda i: (0, i)), _resident((1, ATTN_WIDTH)), _resident((1, GMLP_WIDTH)),
                  _resident((ATTN_WIDTH + GMLP_WIDTH, d)), _resident((1, d))],
        out_specs=pl.BlockSpec((tm, d), lambda i: (i, 0)),
        out_shape=jax.ShapeDtypeStruct((t, d), F32),
        scratch_shapes=[pltpu.VMEM((X_RING_SLOTS, tm, d), F32), pltpu.SemaphoreType.DMA((X_RING_SLOTS,))],
        compiler_params=_params(("arbitrary",)),
    )(x, ya_t, yg_t, gao, ggo, wout, b_out)


def _lane_fold(v):
    out = v[:, :LANES]
    for j in range(1, v.shape[1] // LANES):
        out = out + v[:, j * LANES:(j + 1) * LANES]
    return out


def mix_out_bwd(dx, ya_t, yg_t, gao, ggo, wout, name, deps=()):
    t, d = dx.shape
    tm = _token_tile(t)
    nt = t // tm

    def body(dx_ref, ya_ref, yg_ref, gao_ref, ggo_ref, w_ref, dya_ref, dyg_ref, y_ref, dgao_ref, dggo_ref, db_ref,
             acc_ref):
        i = pl.program_id(0)

        @pl.when(i == 0)
        def _():
            acc_ref[...] = jnp.zeros_like(acc_ref)
            db_ref[...] = jnp.zeros_like(db_ref)

        dxv = dx_ref[...]
        db_ref[...] += jnp.sum(dxv, axis=0, keepdims=True)
        dxb = dxv.astype(BF16)
        for part, (src_ref, gain_ref, dst_ref) in enumerate(((ya_ref, gao_ref, dya_ref), (yg_ref, ggo_ref, dyg_ref))):
            rows = slice(part * ATTN_WIDTH, (part + 1) * ATTN_WIDTH)
            gain = _as_column(gain_ref[...])
            nrm, r, yn = _rmsnorm_cols(src_ref[...].astype(F32), gain)
            y_ref[rows, :] = yn.astype(BF16)
            dy = _dg(w_ref[rows, :], dxb, NT)
            acc_ref[rows, :] += _lane_fold(dy * nrm)
            dn = dy * gain
            dst_ref[...] = (r * (dn - nrm * jnp.mean(dn * nrm, axis=0, keepdims=True))).astype(BF16)

        @pl.when(i == nt - 1)
        def _():
            dgao_ref[...] = _row_sums(acc_ref[:ATTN_WIDTH, :])
            dggo_ref[...] = _row_sums(acc_ref[ATTN_WIDTH:, :])

    const2 = lambda i: (0, 0)
    feat = lambda w: pl.BlockSpec((w, tm), lambda i: (0, i))
    return pl.pallas_call(
        _drop_deps(body, 6, len(deps)), name=name, grid=(nt,),
        in_specs=[pl.BlockSpec((tm, d), lambda i: (i, 0)), feat(ATTN_WIDTH), feat(GMLP_WIDTH),
                  _resident((1, ATTN_WIDTH)), _resident((1, GMLP_WIDTH)), _resident((ATTN_WIDTH + GMLP_WIDTH, d))]
        + [ANY_SPEC] * len(deps),
        out_specs=[feat(ATTN_WIDTH), feat(GMLP_WIDTH), feat(ATTN_WIDTH + GMLP_WIDTH),
                   pl.BlockSpec((1, ATTN_WIDTH), const2), pl.BlockSpec((1, GMLP_WIDTH), const2),
                   pl.BlockSpec((1, d), const2)],
        out_shape=[jax.ShapeDtypeStruct((ATTN_WIDTH, t), BF16), jax.ShapeDtypeStruct((GMLP_WIDTH, t), BF16),
                   jax.ShapeDtypeStruct((ATTN_WIDTH + GMLP_WIDTH, t), BF16),
                   jax.ShapeDtypeStruct((1, ATTN_WIDTH), F32), jax.ShapeDtypeStruct((1, GMLP_WIDTH), F32),
                   jax.ShapeDtypeStruct((1, d), F32)],
        scratch_shapes=[pltpu.VMEM((ATTN_WIDTH + GMLP_WIDTH, LANES), F32)],
        compiler_params=_params(("arbitrary",)),
    )(dx, ya_t, yg_t, gao, ggo, wout, *deps)


def mix_in_bwd(x, dxo, gain, dq_t, dk_t, dv_t, dzg_t, win_t, name):
    t, d = x.shape
    w = win_t.shape[0]
    tm = _token_tile(t)
    nt = t // tm
    parts = ((0, ATTN_WIDTH), (ATTN_WIDTH, KV_WIDTH), (ATTN_WIDTH + KV_WIDTH, KV_WIDTH), (QKV_WIDTH, w - QKV_WIDTH))

    def body(x_ref, dxo_ref, g_ref, dq_ref, dk_ref, dv_ref, dzg_ref, w_ref, dx_ref, h_ref, dg_ref, db_ref, acc_ref):
        i = pl.program_id(0)

        @pl.when(i == 0)
        def _():
            acc_ref[...] = jnp.zeros_like(acc_ref)
            dg_ref[...] = jnp.zeros_like(dg_ref)

        xv = x_ref[...]
        gain_v = g_ref[...]
        r = lax.rsqrt(jnp.mean(xv * xv, axis=-1, keepdims=True) + EPS)
        h_ref[...] = (xv * r * gain_v).astype(BF16)
        dh = jnp.zeros((tm, d), F32)
        for (off, size), src_ref in zip(parts, (dq_ref, dk_ref, dv_ref, dzg_ref)):
            dp = src_ref[...]
            acc_ref[off:off + size, :] += _lane_fold(dp.astype(F32))
            dh = dh + _dg(dp, w_ref[off:off + size, :], TN)
        dx, dgain = _rmsnorm_bwd_rows(xv, r, gain_v, dh)
        dx_ref[...] = dxo_ref[...] + dx
        dg_ref[...] += dgain

        @pl.when(i == nt - 1)
        def _():
            db_ref[...] = _row_sums(acc_ref[...])

    const2 = lambda i: (0, 0)
    tok = lambda: pl.BlockSpec((tm, d), lambda i: (i, 0))
    feat = lambda rows: pl.BlockSpec((rows, tm), lambda i: (0, i))
    return pl.pallas_call(
        body, name=name, grid=(nt,),
        in_specs=[tok(), tok(), _resident((1, d)), feat(ATTN_WIDTH), feat(KV_WIDTH), feat(KV_WIDTH),
                  feat(w - QKV_WIDTH), _resident((w, d))],
        out_specs=[tok(), tok(), pl.BlockSpec((1, d), const2), pl.BlockSpec((1, w), const2)],
        out_shape=[jax.ShapeDtypeStruct((t, d), F32), jax.ShapeDtypeStruct((t, d), BF16),
                   jax.ShapeDtypeStruct((1, d), F32), jax.ShapeDtypeStruct((1, w), F32)],
        scratch_shapes=[pltpu.VMEM((w, LANES), F32)],
        compiler_params=_params(("arbitrary",)),
    )(x, dxo, gain, dq_t, dk_t, dv_t, dzg_t, win_t)


def _adamw_math(w, g, m, v):
    m = ADAM_B1 * m + (1.0 - ADAM_B1) * g
    v = ADAM_B2 * v + (1.0 - ADAM_B2) * (g * g)
    m_hat = m / (1.0 - ADAM_B1 ** ADAM_STEP)
    v_hat = v / (1.0 - ADAM_B2 ** ADAM_STEP)
    delta = -ADAM_LR * (m_hat / (jnp.sqrt(v_hat) + ADAM_EPS) + ADAM_WD * w)
    return delta, m, v


def adamw_many(ws, gs, ms, vs, name):
    n = len(ws)

    def body(*refs):
        w_refs, g_refs, m_refs, v_refs = refs[:n], refs[n:2 * n], refs[2 * n:3 * n], refs[3 * n:4 * n]
        d_refs, mo_refs, vo_refs = refs[4 * n:5 * n], refs[5 * n:6 * n], refs[6 * n:]
        for k in range(n):
            delta, mn, vn = _adamw_math(w_refs[k][...], g_refs[k][...], m_refs[k][...], v_refs[k][...])
            d_refs[k][...] = delta
            mo_refs[k][...] = mn
            vo_refs[k][...] = vn

    def whole(a):
        zeros = (0,) * a.ndim
        return pl.BlockSpec(a.shape, lambda i: zeros)

    outs = pl.pallas_call(
        body, name=name, grid=(1,),
        in_specs=[whole(a) for a in ws] * 4, out_specs=[whole(a) for a in ws] * 3,
        out_shape=[jax.ShapeDtypeStruct(a.shape, F32) for a in ws] * 3,
        compiler_params=_params(("arbitrary",)),
    )(*ws, *gs, *ms, *vs)
    return outs[:n], outs[n:2 * n], outs[2 * n:]


def sum_adamw(landing, grad3d, me, w, m, v, name):
    nd, rows, cols = landing.shape
    tr = rows // 2

    def body(me_ref, l_ref, own_ref, w_ref, m_ref, v_ref, g_ref, d_ref, mo_ref, vo_ref):
        g = own_ref[...].astype(F32)
        for j in range(nd):
            g = g + l_ref[j].astype(F32)
        delta, mn, vn = _adamw_math(w_ref[...], g, m_ref[...], v_ref[...])
        g_ref[...] = g
        d_ref[...] = delta
        mo_ref[...] = mn
        vo_ref[...] = vn

    spec = lambda: pl.BlockSpec((tr, cols), lambda i, me_ref: (i, 0))
    shape = jax.ShapeDtypeStruct((rows, cols), F32)
    return pl.pallas_call(
        body, name=name,
        grid_spec=pltpu.PrefetchScalarGridSpec(
            num_scalar_prefetch=1, grid=(rows // tr,),
            in_specs=[pl.BlockSpec((nd, tr, cols), lambda i, me_ref: (0, i, 0)),
                      pl.BlockSpec((None, tr, cols), lambda i, me_ref: (me_ref[0], i, 0)), spec(), spec(), spec()],
            out_specs=[spec() for _ in range(4)]),
        out_shape=[shape] * 4,
        compiler_params=_params(("arbitrary",)),
    )(me, landing, grad3d, w, m, v)


def _mesh_place():
    x, y, c = lax.axis_index("x"), lax.axis_index("y"), lax.axis_index("c")
    return x, y, c, 4 * x + 2 * y + c


def _peer(x, y, c, k):
    return (x ^ ((k >> 2) & 1), y ^ ((k >> 1) & 1), c ^ (k & 1))


def prep_shards(mats, me, name):
    na = len(mats)

    def body(me_ref, *refs):
        in_refs, shard_refs, land_refs = refs[:na], refs[na:2 * na], refs[2 * na:]
        for a in range(na):
            val = in_refs[a][...].astype(BF16)
            shard_refs[a][...] = val
            land_refs[a][...] = val

    whole = lambda mat: pl.BlockSpec(mat.shape, lambda i, me_ref: (0, 0))
    outs = pl.pallas_call(
        body, name=name,
        grid_spec=pltpu.PrefetchScalarGridSpec(
            num_scalar_prefetch=1, grid=(1,),
            in_specs=[whole(mat) for mat in mats],
            out_specs=[whole(mat) for mat in mats]
            + [pl.BlockSpec((None,) + mat.shape, lambda i, me_ref: (me_ref[0], 0, 0)) for mat in mats]),
        out_shape=[jax.ShapeDtypeStruct(mat.shape, BF16) for mat in mats]
        + [jax.ShapeDtypeStruct((N_DEV,) + mat.shape, BF16) for mat in mats],
        compiler_params=_params(("arbitrary",)),
    )(me, *mats)
    return outs[:na], outs[na:]


def allgather_rows(shards, lands, name):
    na = n_gather = len(shards)

    def body(*refs):
        in_refs, out_refs = refs[:na], refs[2 * na:3 * na]
        send_sems, recv_sems = refs[3 * na:]
        x, y, c, me = _mesh_place()

        def copy(sem, a, origin, src, k):
            return pltpu.make_async_remote_copy(
                src_ref=src, dst_ref=out_refs[a].at[origin], send_sem=send_sems.at[sem, a],
                recv_sem=recv_sems.at[sem, a], device_id=_peer(x, y, c, k), device_id_type=pl.DeviceIdType.MESH)

        kx, ky = 4, 2

        def recv(sem, a, origin):
            copy(sem, a, origin, in_refs[a], 1).wait_recv()

        def landed(a, origin):
            return out_refs[a].at[origin]

        @pl.when(c == 1)
        def _():
            sends = []

            def send(sem, a, origin, src, k):
                cp = copy(sem, a, origin, src, k)
                cp.start()
                sends.append(cp)

            for a in range(n_gather):
                send(1, a, me, in_refs[a], kx)
                send(2, a, me, in_refs[a], ky)
                send(0, a, me, in_refs[a], 1)
            for a in range(n_gather):
                recv(0, a, me ^ 1)
                send(3, a, me ^ 1, landed(a, me ^ 1), kx)
                send(4, a, me ^ 1, landed(a, me ^ 1), ky)
            for a in range(n_gather):
                recv(1, a, me ^ kx)
                send(5, a, me ^ kx, landed(a, me ^ kx), ky)
                send(7, a, me ^ kx, landed(a, me ^ kx), 1)
                recv(2, a, me ^ ky)
                send(8, a, me ^ ky, landed(a, me ^ ky), 1)
            for a in range(n_gather):
                recv(3, a, me ^ kx ^ 1)
                send(9, a, me ^ kx ^ 1, landed(a, me ^ kx ^ 1), 1)
                recv(4, a, me ^ ky ^ 1)
                send(6, a, me ^ ky ^ 1, landed(a, me ^ ky ^ 1), kx)
                send(10, a, me ^ ky ^ 1, landed(a, me ^ ky ^ 1), 1)
            for a in range(n_gather):
                recv(5, a, me ^ 6)
                send(11, a, me ^ 6, landed(a, me ^ 6), 1)
                recv(6, a, me ^ 7)
                send(12, a, me ^ 7, landed(a, me ^ 7), 1)
            for cp in sends:
                cp.wait_send()

        @pl.when(c == 0)
        def _():
            north = me ^ 1
            own = [copy(0, a, me, in_refs[a], 1) for a in range(n_gather)]
            for cp in own:
                cp.start()
            for a in range(n_gather):
                recv(0, a, north)
                for sem, origin in ((7, north ^ kx), (8, north ^ ky), (9, north ^ kx ^ 1), (10, north ^ ky ^ 1),
                                    (11, north ^ 6), (12, north ^ 7)):
                    recv(sem, a, origin)
            for cp in own:
                cp.wait_send()

    return pl.pallas_call(
        body, name=name,
        in_specs=[ANY_SPEC] * (2 * na), out_specs=[ANY_SPEC] * na,
        out_shape=[jax.ShapeDtypeStruct(land.shape, land.dtype) for land in lands],
        input_output_aliases={na + a: a for a in range(na)},
        scratch_shapes=[pltpu.SemaphoreType.DMA((N_GATHER_ROLES, n_gather)),
                        pltpu.SemaphoreType.DMA((N_GATHER_ROLES, n_gather))],
    )(*shards, *lands)


HBM_SPEC = pl.BlockSpec(memory_space=pltpu.HBM)
SEM_SPEC = pl.BlockSpec(memory_space=pltpu.SEMAPHORE)
SIDE_EFFECT = pltpu.SideEffectType.DATAFLOW_SIDE_EFFECTING


def _hbm(v):
    return pltpu.with_memory_space_constraint(v, pltpu.HBM)


def gather_slices(src, land, me, k):
    return src, land.at[me], land.at[me ^ k]


def exchange_slices(src, land, me, k):
    return src.at[me ^ k], land.at[k - 1], land.at[k - 1]


def split_start(groups, slices, deps, name):
    sizes = [len(srcs) for srcs, _ in groups]
    flat_src = [s for srcs, _ in groups for s in srcs]
    flat_land = [l for _, lands in groups for l in lands]
    na, ng, nd = len(flat_src), len(groups), len(deps)

    def body(*refs):
        src_refs, land_refs = refs[:na], refs[na:2 * na]
        sem_refs = refs[2 * na + nd:2 * na + nd + 2 * ng]
        token_ref = refs[-1]
        x, y, c, me = _mesh_place()
        a0 = 0
        for gi, size in enumerate(sizes):
            send_sems, recv_sems = sem_refs[2 * gi], sem_refs[2 * gi + 1]
            for k in range(1, N_DEV):
                for j in range(size):
                    src, dst, _ = slices(src_refs[a0 + j], land_refs[a0 + j], me, k)
                    pltpu.make_async_remote_copy(
                        src_ref=src, dst_ref=dst, send_sem=send_sems.at[(k - 1) * size + j],
                        recv_sem=recv_sems.at[(k - 1) * size + j],
                        device_id=_peer(x, y, c, k), device_id_type=pl.DeviceIdType.MESH).start()
            a0 += size
        token_ref[...] = jnp.zeros_like(token_ref)

    sems = [pltpu.SemaphoreType.DMA(((N_DEV - 1) * size,)) for size in sizes for _ in range(2)]
    thru = [pltpu.HBM(v.shape, v.dtype) for v in flat_src + flat_land]
    outs = pl.pallas_call(
        body, name=name,
        in_specs=[HBM_SPEC] * (2 * na) + [ANY_SPEC] * nd,
        out_specs=[SEM_SPEC] * (2 * ng) + [HBM_SPEC] * (2 * na) + [pl.BlockSpec(memory_space=pltpu.VMEM)],
        out_shape=sems + thru + [jax.ShapeDtypeStruct((8, LANES), F32)],
        input_output_aliases={i: 2 * ng + i for i in range(2 * na)},
        compiler_params=pltpu.CompilerParams(has_side_effects=SIDE_EFFECT),
    )(*[_hbm(v) for v in flat_src + flat_land], *deps)
    started, a0 = [], 0
    for gi, size in enumerate(sizes):
        srcs = outs[2 * ng + a0:2 * ng + a0 + size]
        lands = outs[2 * ng + na + a0:2 * ng + na + a0 + size]
        started.append((outs[2 * gi], outs[2 * gi + 1], list(srcs), list(lands)))
        a0 += size
    return started, outs[-1]


def split_wait(started, slices, after, name):
    send_sems, recv_sems, srcs, lands = started
    na = len(srcs)

    def body(*refs):
        src_refs, land_refs = refs[:na], refs[na:2 * na]
        send_ref, recv_ref = refs[2 * na], refs[2 * na + 1]
        x, y, c, me = _mesh_place()
        for k in range(1, N_DEV):
            for j in range(na):
                src, _, got = slices(src_refs[j], land_refs[j], me, k)
                cp = pltpu.make_async_remote_copy(
                    src_ref=src, dst_ref=got, send_sem=send_ref.at[(k - 1) * na + j], recv_sem=recv_ref.at[(k - 1) * na + j],
                    device_id=_peer(x, y, c, k), device_id_type=pl.DeviceIdType.MESH)
                cp.wait_send()
                cp.wait_recv()

    outs = pl.pallas_call(
        body, name=name,
        in_specs=[HBM_SPEC] * (2 * na) + [SEM_SPEC, SEM_SPEC] + [ANY_SPEC] * len(after),
        out_specs=[HBM_SPEC] * (2 * na),
        out_shape=[pltpu.HBM(v.shape, v.dtype) for v in srcs + lands],
        input_output_aliases={i: i for i in range(2 * na)},
        compiler_params=pltpu.CompilerParams(has_side_effects=SIDE_EFFECT),
    )(*srcs, *lands, send_sems, recv_sems, *after)
    return list(outs[:na]), list(outs[na:])


def place_own(pack, me, name):
    rows, cols = pack.shape

    def body(me_ref, p_ref, o_ref):
        o_ref[...] = p_ref[...]

    return pl.pallas_call(
        body, name=name,
        grid_spec=pltpu.PrefetchScalarGridSpec(
            num_scalar_prefetch=1, grid=(1,),
            in_specs=[pl.BlockSpec((rows, cols), lambda i, me_ref: (0, 0))],
            out_specs=pl.BlockSpec((None, rows, cols), lambda i, me_ref: (me_ref[0], 0, 0))),
        out_shape=jax.ShapeDtypeStruct((N_DEV, rows, cols), F32),
        compiler_params=_params(("arbitrary",)),
    )(me, pack)


def sum_slots(slots, name):
    nd, rows, cols = slots.shape

    def body(s_ref, o_ref):
        acc = s_ref[0]
        for j in range(1, nd):
            acc = acc + s_ref[j]
        o_ref[...] = acc

    return pl.pallas_call(
        body, name=name, grid=(1,),
        in_specs=[pl.BlockSpec((nd, rows, cols), lambda i: (0, 0, 0))],
        out_specs=pl.BlockSpec((rows, cols), lambda i: (0, 0)),
        out_shape=jax.ShapeDtypeStruct((rows, cols), F32),
        compiler_params=_params(("arbitrary",)),
    )(slots)


def _pack_rows(size):
    rows = -(-size // LANES)
    return -(-rows // 8) * 8


def pack_small(parts):
    out = []
    for p in parts:
        flat = p.reshape(-1).astype(F32)
        rows = _pack_rows(flat.shape[0])
        out.append(jnp.pad(flat, (0, rows * LANES - flat.shape[0])).reshape(rows, LANES))
    return jnp.concatenate(out, axis=0)


def unpack_small(pack, shapes):
    out, off = [], 0
    for shp in shapes:
        size = 1
        for s in shp:
            size *= s
        rows = _pack_rows(size)
        out.append(pack[off:off + rows].reshape(-1)[:size].reshape(shp))
        off += rows
    return out


def local_step(x, target, small, get_w, put_g, put_small):
    row = lambda v: v.reshape(1, -1)
    (wg1, wu1), deps = get_w(("wg1", "wu1"), ())
    sil1, gp1, s1, h1 = ffn_fwd(x, small["ffn1_norm_g"], wg1, wu1, None, "ffn1_fwd", deps)
    (wd1, win, wout), _ = get_w(("wd1", "win", "wout"), (s1,))
    qkv_t, zg_t, x1 = mix_in_fwd(x, small["mix_norm_g"], win, row(small["b_in"]), "mix_in_fwd", ffn_tail=(s1, wd1))
    ya_t = attn_fwd(qkv_t, small["attn_sinks"], "attn_fwd")
    lng, lnb = row(small["gmlp_ln_g"]), row(small["gmlp_ln_b"])
    w_s, b_s = small["gmlp_w_s"][0], small["gmlp_b_s"][0]
    yg_t = gmlp_fwd(zg_t, lng, lnb, w_s, b_s, "gmlp_fwd")
    gao, ggo = row(small["attn_out_norm_g"]), row(small["gmlp_out_norm_g"])
    x2 = mix_out_fwd(x1, ya_t, yg_t, gao, ggo, wout, small["b_out"], "mix_out_fwd")
    (wg2, wu2, wd2), _ = get_w(("wg2", "wu2", "wd2"), (x2,))
    gs = {}
    dx3, sil2, gp2, s2, h2f, gs["final_norm_g"], loss = ffn_fwd(
        x2, small["ffn2_norm_g"], wg2, wu2, wd2, "ffn2_fwd", head=(target, small["final_norm_g"].reshape(1, -1)))

    dx2, da, db, df, gs["ffn2_norm_g"] = ffn_bwd(x2, dx3, small["ffn2_norm_g"], sil2, gp2, wg2, wu2, wd2, "ffn2_bwd")
    deps = put_g({"wg2": grad_tn(da, h2f, 1.0, "ffn2_dwg"), "wu2": grad_tn(db, h2f, 1.0, "ffn2_dwu"),
                  "wd2": grad_tn(s2, df, 1.0, "ffn2_dwd")})

    dya_t, dyg_t, y_t, dgao, dggo, gs["b_out"] = mix_out_bwd(dx2, ya_t, yg_t, gao, ggo, wout, "mix_out_bwd", deps)
    gs["attn_out_norm_g"], gs["gmlp_out_norm_g"] = dgao.reshape(1, -1), dggo.reshape(1, -1)
    g_wout = grad_nn([y_t], dx2, "dwout")
    dzg_t, dws, dbs, dlng, dlnb = gmlp_bwd(zg_t, dyg_t, lng, lnb, w_s, b_s, "gmlp_bwd")
    gs["gmlp_w_s"], gs["gmlp_b_s"] = dws[None], dbs[None]
    gs["gmlp_ln_g"], gs["gmlp_ln_b"] = dlng.reshape(1, -1), dlnb.reshape(1, -1)
    dq_t, dk_t, dv_t, dsinks = attn_bwd(qkv_t, dya_t, small["attn_sinks"], "attn_bwd")
    gs["attn_sinks"] = dsinks.reshape(1, -1)
    dx1, h2, gs["mix_norm_g"], dbin = mix_in_bwd(x1, dx2, small["mix_norm_g"], dq_t, dk_t, dv_t, dzg_t, win,
                                                 "mix_in_bwd")
    gs["b_in"] = dbin.reshape(1, -1)
    deps = put_g({"wout": g_wout, "win": grad_nn([dq_t, dk_t, dv_t, dzg_t], h2, "dwin")})

    dx0, da, db, df, gs["ffn1_norm_g"] = ffn_bwd(x, dx1, small["ffn1_norm_g"], sil1, gp1, wg1, wu1, wd1, "ffn1_bwd",
                                             deps)
    deps = put_small(gs, loss)
    deps = put_g({"wg1": grad_tn(da, h1, 1.0, "ffn1_dwg", deps)})
    deps = put_g({"wu1": grad_tn(db, h1, 1.0, "ffn1_dwu", deps)})
    put_g({"wd1": grad_tn(s1, df, 1.0, "ffn1_dwd", deps)})
    return dx0


SMALL_NAMES = ["ffn1_norm_g", "mix_norm_g", "b_in", "attn_sinks", "gmlp_ln_g", "gmlp_ln_b", "gmlp_w_s", "gmlp_b_s",
               "attn_out_norm_g", "gmlp_out_norm_g", "b_out", "ffn2_norm_g", "final_norm_g"]
BIG_NAMES = {"wg1": ("ffn1_w_gate", True), "wu1": ("ffn1_w_up", True), "wd1": ("ffn1_w_down", False),
             "win": ("w_in", True), "wout": ("w_out", False),
             "wg2": ("ffn2_w_gate", True), "wu2": ("ffn2_w_up", True), "wd2": ("ffn2_w_down", False)}
WEIGHT_ORDER = ["ffn1_norm_g", "ffn1_w_gate", "ffn1_w_up", "ffn1_w_down", "mix_norm_g", "w_in", "b_in", "attn_sinks",
                "gmlp_ln_g", "gmlp_ln_b", "gmlp_w_s", "gmlp_b_s", "attn_out_norm_g", "gmlp_out_norm_g", "w_out",
                "b_out", "ffn2_norm_g", "ffn2_w_gate", "ffn2_w_up", "ffn2_w_down", "final_norm_g"]


def kernel(x, ffn1_norm_g, ffn1_w_gate, ffn1_w_up, ffn1_w_down, mix_norm_g, w_in, b_in, attn_sinks, gmlp_ln_g, gmlp_ln_b, gmlp_w_s, gmlp_b_s, attn_out_norm_g, gmlp_out_norm_g, w_out, b_out, ffn2_norm_g, ffn2_w_gate, ffn2_w_up, ffn2_w_down, final_norm_g, loss_target, m_ffn1_norm_g, m_ffn1_w_gate, m_ffn1_w_up, m_ffn1_w_down, m_mix_norm_g, m_w_in, m_b_in, m_attn_sinks, m_gmlp_ln_g, m_gmlp_ln_b, m_gmlp_w_s, m_gmlp_b_s, m_attn_out_norm_g, m_gmlp_out_norm_g, m_w_out, m_b_out, m_ffn2_norm_g, m_ffn2_w_gate, m_ffn2_w_up, m_ffn2_w_down, m_final_norm_g, v_ffn1_norm_g, v_ffn1_w_gate, v_ffn1_w_up, v_ffn1_w_down, v_mix_norm_g, v_w_in, v_b_in, v_attn_sinks, v_gmlp_ln_g, v_gmlp_ln_b, v_gmlp_w_s, v_gmlp_b_s, v_attn_out_norm_g, v_gmlp_out_norm_g, v_w_out, v_b_out, v_ffn2_norm_g, v_ffn2_w_gate, v_ffn2_w_up, v_ffn2_w_down, v_final_norm_g):
    args = dict(locals())
    w = {n: args[n] for n in WEIGHT_ORDER}
    m = {n: args["m_" + n] for n in WEIGHT_ORDER}
    v = {n: args["v_" + n] for n in WEIGHT_ORDER}

    d_model = x.shape[-1]
    flat = lambda g: g.reshape(-1, d_model)
    me = _mesh_place()[3].astype(jnp.int32).reshape(1)
    first = ("wg1", "wu1")
    later = (("wd1", "win", "wout"), ("wg2", "wu2", "wd2"))
    gathers, exchanges, last_token = {}, [], []

    order = list(first) + [k for grp in later for k in grp]
    mats = [w[BIG_NAMES[k][0]][0].T if BIG_NAMES[k][1] else w[BIG_NAMES[k][0]][0] for k in order]
    shard_list, land_list = prep_shards(mats, me, "prep_shards")
    shards, zones = dict(zip(order, shard_list)), dict(zip(order, land_list))

    def get_w(keys, after):
        if keys == first:
            full = allgather_rows([shards[k] for k in first], [zones[k] for k in first], "allgather_ffn1")
            started, token = split_start([([shards[k] for k in grp], [zones[k] for k in grp]) for grp in later],
                                         gather_slices, (full[0],), "gather_start")
            gathers.update(zip(later, started))
            return [flat(g) for g in full], (token,)
        _, lands = split_wait(gathers[keys], gather_slices, after, "gather_wait_" + keys[0])
        return [flat(land) for land in lands], ()

    def put_g(gb):
        keys = tuple(gb)
        g3 = [gb[k].reshape(N_DEV, -1, d_model) for k in keys]
        lands = [lax.empty((N_DEV - 1,) + g.shape[1:], BF16) for g in g3]
        started, token = split_start([(g3, lands)], exchange_slices, (), "exchange_start_" + keys[0])
        exchanges.append((keys, started[0]))
        last_token[:] = [token]
        return (token,)

    small_started = []

    def put_small(gs, loss):
        pack = pack_small([gs[n] for n in SMALL_NAMES] + [loss[:, :1]])
        started, token = split_start([([pack], [place_own(pack, me, "place_small")])], gather_slices, (), "small_start")
        small_started[:] = started
        return (token,)

    small = {n: w[n] for n in SMALL_NAMES}
    grad_x = local_step(x[0], loss_target[0], small, get_w, put_g, put_small)

    grads, deltas, new_m, new_v = {}, {}, {}, {}
    shapes = [w[n].shape for n in SMALL_NAMES]
    _, (slots,) = split_wait(small_started[0], gather_slices, tuple(last_token), "small_wait")
    total = sum_slots(slots, "sum_small")
    small_g = unpack_small(total, shapes + [(1, 1)])
    loss_total = small_g[-1].reshape(())
    d_, m_, v_ = adamw_many([w[n] for n in SMALL_NAMES], small_g[:-1], [m[n] for n in SMALL_NAMES],
                            [v[n] for n in SMALL_NAMES], "adamw_small")
    for n, g_, dd, mm, vv in zip(SMALL_NAMES, small_g[:-1], d_, m_, v_):
        grads[n], deltas[n], new_m[n], new_v[n] = g_, dd, mm, vv

    after = tuple(arr for n in SMALL_NAMES for arr in (grads[n], deltas[n], new_m[n], new_v[n]))
    for keys, started in exchanges:
        g3, lands = split_wait(started, exchange_slices, after, "exchange_wait_" + keys[0])
        for key, own, land in zip(keys, g3, lands):
            pname, transposed = BIG_NAMES[key]
            to_rows = (lambda p: p[0].T) if transposed else (lambda p: p[0])
            from_rows = (lambda r: r.T[None]) if transposed else (lambda r: r[None])
            g, d_, m_, v_ = sum_adamw(land, own, me, to_rows(w[pname]), to_rows(m[pname]), to_rows(v[pname]),
                                      "adamw_" + key)
            after = after + (g,)
            grads[pname], deltas[pname], new_m[pname], new_v[pname] = (from_rows(g), from_rows(d_), from_rows(m_),
                                                                        from_rows(v_))

    return (loss_total, grad_x[None], *[grads[n] for n in WEIGHT_ORDER], *[deltas[n] for n in WEIGHT_ORDER],
            *[new_m[n] for n in WEIGHT_ORDER], *[new_v[n] for n in WEIGHT_ORDER])
```

```python
import functools

import jax
import jax.numpy as jnp
from jax import lax
from jax.experimental import pallas as pl
from jax.experimental.pallas import tpu as pltpu

F32 = jnp.float32
BF16 = jnp.bfloat16

N_DEV = 8
N_Q_HEADS = 8
N_KV_HEADS = 2
HEAD_DIM = 64
ATTN_WIDTH = N_Q_HEADS * HEAD_DIM
KV_WIDTH = N_KV_HEADS * HEAD_DIM
QKV_WIDTH = ATTN_WIDTH + 2 * KV_WIDTH
BLOCK = 128
GMLP_GROUPS = 8
GMLP_GROUP_DIM = 64
GMLP_WIDTH = GMLP_GROUPS * GMLP_GROUP_DIM
EPS = 1e-6
FFN_RES = 0.5
ATTN_SCALE = HEAD_DIM ** -0.5
MASK_VALUE = -1e30

ADAM_LR = 0.001
ADAM_B1 = 0.9
ADAM_B2 = 0.999
ADAM_EPS = 1e-08
ADAM_WD = 0.01
ADAM_STEP = 10

V7X_VMEM_BYTES = 64 * 1024 * 1024
VMEM_LIMIT = 62 * 1024 * 1024
LANES = 128
MXU_WIDTH = 256
FFN_FWD_TOKENS = 512
FFN_BWD_TOKENS = 256

NT = (((1,), (1,)), ((), ()))
TN = (((0,), (0,)), ((), ()))


def _dot(a, b):
    return jnp.dot(a, b, preferred_element_type=F32)


def _dg(a, b, dims):
    return lax.dot_general(a, b, dims, preferred_element_type=F32)


def _params(semantics=None):
    return pltpu.CompilerParams(dimension_semantics=semantics, vmem_limit_bytes=VMEM_LIMIT)


def _resident(shape):
    zeros = (0,) * len(shape)
    return pl.BlockSpec(shape, lambda *_: zeros, pipeline_mode=pl.Buffered(1))


def _drop_deps(body, n_in, n_deps):
    return lambda *refs: body(*refs[:n_in], *refs[n_in + n_deps:])


ANY_SPEC = pl.BlockSpec(memory_space=pl.ANY)


TOKEN_TILE = 1024
RING_SLOTS = 3
GRAD_TOKENS_PER_STEP = 2048
GRAD_ROW_TILE_CAP = 1408
N_GATHER_ROLES = 13


def _token_tile(t):
    return min(t, TOKEN_TILE)


def _f_chunk(f):
    for c in (256, 128):
        if f % c == 0:
            return c
    return f


def _loss_head(xv, target, gain):
    r = lax.rsqrt(jnp.mean(xv * xv, axis=-1, keepdims=True) + EPS)
    err = xv * r * gain - target
    loss = 0.5 * jnp.sum(jnp.mean(err * err, axis=-1, keepdims=True), axis=0, keepdims=True)
    dx, dgain = _rmsnorm_bwd_rows(xv, r, gain, err * (1.0 / xv.shape[-1]))
    return dx, dgain, loss


def ffn_fwd(x, gain, wg_t, wu_t, wd, name, deps=(), head=None):
    t, d = x.shape
    f = wg_t.shape[0]
    tm = min(t, FFN_FWD_TOKENS)
    fc = _f_chunk(f)
    weights = [wg_t, wu_t] + ([] if wd is None else [wd])
    n_in = 2 + len(weights) + (0 if head is None else 2)
    n_x = 0 if wd is None else 1

    def body(x_ref, g_ref, wg_ref, wu_ref, *rest):
        sil_ref, gp_ref, s_ref, h_ref = rest[n_in - 4 + n_x:n_in + n_x]
        xv = x_ref[...]
        r = lax.rsqrt(jnp.mean(xv * xv, axis=-1, keepdims=True) + EPS)
        h = (xv * r * g_ref[...]).astype(BF16)
        h_ref[...] = h
        for c in range(f // fc):
            sl = slice(c * fc, (c + 1) * fc)
            a = _dg(h, wg_ref[sl, :], NT)
            b = _dg(h, wu_ref[sl, :], NT)
            sig = jax.nn.sigmoid(a)
            sil = a * sig
            sil_ref[:, sl] = sil.astype(BF16)
            gp_ref[:, sl] = (b * (sig + sil - sil * sig)).astype(BF16)
            s_ref[:, sl] = (sil * b).astype(BF16)
        if wd is None:
            return
        wd_ref, xo_ref = rest[0], rest[n_in - 4]
        xo = xv + FFN_RES * _dot(s_ref[...], wd_ref[...])
        if head is None:
            xo_ref[...] = xo
        else:
            t_ref, hg_ref = rest[1:3]
            dg_ref, loss_ref = rest[-2:]

            @pl.when(pl.program_id(0) == 0)
            def _():
                dg_ref[...] = jnp.zeros_like(dg_ref)
                loss_ref[...] = jnp.zeros_like(loss_ref)

            dx, dgain, loss = _loss_head(xo, t_ref[...], hg_ref[...])
            xo_ref[...] = dx
            dg_ref[...] += dgain
            loss_ref[...] += loss

    tok = lambda: pl.BlockSpec((tm, d), lambda i: (i, 0))
    wide = lambda: pl.BlockSpec((tm, f), lambda i: (i, 0))
    const2 = lambda i: (0, 0)
    in_specs = [tok(), _resident((1, d))] + [_resident((f, d)) for _ in weights]
    out_specs = [tok()] * n_x + [wide(), wide(), wide(), tok()]
    out_shape = ([jax.ShapeDtypeStruct((t, d), F32)] * n_x + [jax.ShapeDtypeStruct((t, f), BF16)] * 3
                 + [jax.ShapeDtypeStruct((t, d), BF16)])
    operands = [x, gain] + weights
    if head is not None:
        in_specs += [tok(), _resident((1, d))]
        out_specs += [pl.BlockSpec((1, d), const2), pl.BlockSpec((1, LANES), const2)]
        out_shape += [jax.ShapeDtypeStruct((1, d), F32), jax.ShapeDtypeStruct((1, LANES), F32)]
        operands += list(head)
    return pl.pallas_call(
        _drop_deps(body, n_in, len(deps)), name=name, grid=(t // tm,),
        in_specs=in_specs + [ANY_SPEC] * len(deps), out_specs=out_specs, out_shape=out_shape,
        compiler_params=_params(("arbitrary",)),
    )(*operands, *deps)


def _rmsnorm_bwd_rows(xv, r, gain, dh):
    n = xv * r
    dgain = jnp.sum(dh * n, axis=0, keepdims=True)
    dn = dh * gain
    dx = r * (dn - n * jnp.mean(dn * n, axis=-1, keepdims=True))
    return dx, dgain


def ffn_bwd(x, dxo, gain, sil, gp, wg_t, wu_t, wd, name, deps=()):
    t, d = x.shape
    f = wd.shape[0]
    tm = min(t, FFN_BWD_TOKENS)
    nt = t // tm
    chunks = [(off, min(MXU_WIDTH, f - off)) for off in range(0, f, MXU_WIDTH)]

    def body(xp_ref, dxp_ref, dxo_ref, g_ref, sil_ref, gp_ref, wg_ref, wu_ref, wd_ref,
             dx_ref, da_ref, db_ref, df_ref, dg_ref, dh_ref):
        i = pl.program_id(0)
        gain_v = g_ref[...]

        @pl.when(i == 0)
        def _():
            dh_ref[...] = jnp.zeros_like(dh_ref)
            dg_ref[...] = jnp.zeros_like(dg_ref)

        def finish_previous(slot):
            xv = xp_ref[...]
            r = lax.rsqrt(jnp.mean(xv * xv, axis=-1, keepdims=True) + EPS)
            dx, dgain = _rmsnorm_bwd_rows(xv, r, gain_v, dh_ref[slot])
            dx_ref[...] = dxp_ref[...] + dx
            dg_ref[...] += dgain

        @pl.when(i < nt)
        def _():
            df = (FFN_RES * dxo_ref[...]).astype(BF16)
            df_ref[...] = df
            for off, size in chunks:
                sl = slice(off, off + size)
                ds = _dg(df, wd_ref[sl, :], NT)
                da_ref[:, sl] = (ds * gp_ref[:, sl].astype(F32)).astype(BF16)
                db_ref[:, sl] = (ds * sil_ref[:, sl].astype(F32)).astype(BF16)
            dh = _dot(da_ref[...], wg_ref[...]) + _dot(db_ref[...], wu_ref[...])
            finish_previous((i + 1) % 2)
            dh_ref[i % 2] = dh

        @pl.when(i == nt)
        def _():
            finish_previous((nt - 1) % 2)

    prev = lambda i: (jnp.maximum(i - 1, 0), 0)
    cur = lambda i: (jnp.minimum(i, nt - 1), 0)
    return pl.pallas_call(
        _drop_deps(body, 9, len(deps)), name=name, grid=(nt + 1,),
        in_specs=[pl.BlockSpec((tm, d), prev), pl.BlockSpec((tm, d), prev), pl.BlockSpec((tm, d), cur),
                  _resident((1, d)), pl.BlockSpec((tm, f), cur), pl.BlockSpec((tm, f), cur),
                  _resident((f, d)), _resident((f, d)), _resident((f, d))] + [ANY_SPEC] * len(deps),
        out_specs=[pl.BlockSpec((tm, d), prev), pl.BlockSpec((tm, f), cur), pl.BlockSpec((tm, f), cur),
                   pl.BlockSpec((tm, d), cur), pl.BlockSpec((1, d), lambda i: (0, 0))],
        out_shape=[jax.ShapeDtypeStruct((t, d), F32), jax.ShapeDtypeStruct((t, f), BF16),
                   jax.ShapeDtypeStruct((t, f), BF16), jax.ShapeDtypeStruct((t, d), BF16),
                   jax.ShapeDtypeStruct((1, d), F32)],
        scratch_shapes=[pltpu.VMEM((2, tm, d), F32)],
        compiler_params=_params(("arbitrary",)),
    )(x, dxo, dxo, gain, sil, gp, wg_t, wu_t, wd, *deps)


def _row_tile(m, cap):
    if m <= cap:
        return m
    best = None
    for bm in range(LANES, cap + 1, LANES):
        if m % bm == 0:
            best = bm
    return best if best is not None else m


def grad_tn(a, b, scale, name, deps=()):
    t, m = a.shape
    n = b.shape[1]
    bm = _row_tile(m, GRAD_ROW_TILE_CAP)
    bk = min(t, GRAD_TOKENS_PER_STEP)
    nk = t // bk

    def body(a_ref, b_ref, o_ref, acc_ref):
        k = pl.program_id(1)

        @pl.when(k == 0)
        def _():
            acc_ref[...] = jnp.zeros_like(acc_ref)

        acc_ref[...] += _dg(a_ref[...].astype(BF16), b_ref[...].astype(BF16), TN)

        @pl.when(k == nk - 1)
        def _():
            o_ref[...] = (scale * acc_ref[...]).astype(BF16)

    return pl.pallas_call(
        _drop_deps(body, 2, len(deps)), name=name, grid=(m // bm, nk),
        in_specs=[pl.BlockSpec((bk, bm), lambda i, k: (k, i)), pl.BlockSpec((bk, n), lambda i, k: (k, 0))]
        + [ANY_SPEC] * len(deps),
        out_specs=pl.BlockSpec((bm, n), lambda i, k: (i, 0)),
        out_shape=jax.ShapeDtypeStruct((m, n), BF16),
        scratch_shapes=[pltpu.VMEM((bm, n), F32)],
        compiler_params=_params(("arbitrary", "arbitrary")),
    )(a, b, *deps)


def grad_nn(a_list, b, name):
    t, n = b.shape
    ms = [a.shape[0] for a in a_list]
    m = sum(ms)
    bk = min(t, GRAD_TOKENS_PER_STEP)
    nk = t // bk
    na = len(a_list)

    def body(*refs):
        a_refs, b_ref, o_ref, acc_ref = refs[:na], refs[na], refs[na + 1], refs[na + 2]
        k = pl.program_id(0)

        @pl.when(k == 0)
        def _():
            acc_ref[...] = jnp.zeros_like(acc_ref)

        bv = b_ref[...].astype(BF16)
        off = 0
        for a_ref, mi in zip(a_refs, ms):
            acc_ref[off:off + mi, :] += _dot(a_ref[...].astype(BF16), bv)
            off += mi

        @pl.when(k == nk - 1)
        def _():
            o_ref[...] = acc_ref[...].astype(BF16)

    return pl.pallas_call(
        body, name=name, grid=(nk,),
        in_specs=[pl.BlockSpec((mi, bk), lambda k: (0, k)) for mi in ms] + [pl.BlockSpec((bk, n), lambda k: (k, 0))],
        out_specs=pl.BlockSpec((m, n), lambda k: (0, 0)),
        out_shape=jax.ShapeDtypeStruct((m, n), BF16),
        scratch_shapes=[pltpu.VMEM((m, n), F32)],
        compiler_params=_params(("arbitrary",)),
    )(*a_list, b)


def mix_in_fwd(x, gain, win_t, b_in_row, name, ffn_tail=None):
    t, d = x.shape
    w = win_t.shape[0]
    tm = _token_tile(t)

    def body(x_ref, g_ref, w_ref, bias_ref, *rest):
        xv = x_ref[...]
        if ffn_tail is not None:
            s_ref, wd_ref, xo_ref = rest[0], rest[1], rest[4]
            xv = xv + FFN_RES * _dot(s_ref[...], wd_ref[...])
            xo_ref[...] = xv
        qkv_ref, zg_ref = rest[-3:-1] if ffn_tail is not None else rest
        r = lax.rsqrt(jnp.mean(xv * xv, axis=-1, keepdims=True) + EPS)
        h = (xv * r * g_ref[...]).astype(BF16)
        bias = _as_column(bias_ref[...])
        qkv_ref[...] = (_dg(w_ref[:QKV_WIDTH, :], h, NT) + bias[:QKV_WIDTH, :]).astype(BF16)
        zg_ref[...] = (_dg(w_ref[QKV_WIDTH:, :], h, NT) + bias[QKV_WIDTH:, :]).astype(BF16)

    tok = lambda: pl.BlockSpec((tm, d), lambda i: (i, 0))
    in_specs = [tok(), _resident((1, d)), _resident((w, d)), _resident((1, w))]
    out_specs = [pl.BlockSpec((QKV_WIDTH, tm), lambda i: (0, i)), pl.BlockSpec((w - QKV_WIDTH, tm), lambda i: (0, i))]
    out_shape = [jax.ShapeDtypeStruct((QKV_WIDTH, t), BF16), jax.ShapeDtypeStruct((w - QKV_WIDTH, t), BF16)]
    operands = [x, gain, win_t, b_in_row]
    if ffn_tail is not None:
        f = ffn_tail[1].shape[0]
        in_specs += [pl.BlockSpec((tm, f), lambda i: (i, 0)), _resident((f, d))]
        out_specs.append(tok())
        out_shape.append(jax.ShapeDtypeStruct((t, d), F32))
        operands += list(ffn_tail)
    return pl.pallas_call(
        body, name=name, grid=(t // tm,), in_specs=in_specs, out_specs=out_specs, out_shape=out_shape,
        compiler_params=_params(("arbitrary",)),
    )(*operands)


ATTN_REP = N_Q_HEADS // N_KV_HEADS
ATTN_BLOCKS_PER_STEP = 8


def _attn_tiles(t):
    nbs = min(ATTN_BLOCKS_PER_STEP, t // BLOCK)
    return nbs, nbs * BLOCK, t // (nbs * BLOCK)


def _attn_specs(nbs, tiles):
    last = tiles - 1
    cur = lambda n: jnp.minimum(n, last)
    prev = lambda n: jnp.maximum(jnp.minimum(n, last) * nbs - 1, 0)
    k_row = ATTN_WIDTH // KV_WIDTH
    tw = nbs * BLOCK
    return [
        pl.BlockSpec((ATTN_WIDTH, tw), lambda n: (0, cur(n))),
        pl.BlockSpec((KV_WIDTH, BLOCK), lambda n: (k_row, prev(n))),
        pl.BlockSpec((KV_WIDTH, tw), lambda n: (k_row, cur(n))),
        pl.BlockSpec((KV_WIDTH, BLOCK), lambda n: (k_row + 1, prev(n))),
        pl.BlockSpec((KV_WIDTH, tw), lambda n: (k_row + 1, cur(n))),
    ]


def _cols(b):
    return slice(b * BLOCK, (b + 1) * BLOCK)


def _band_rows(prev_ref, tile_ref, gs, b):
    before = prev_ref[gs, :] if b == 0 else tile_ref[gs, _cols(b - 1)]
    return before, tile_ref[gs, _cols(b)]


def _group_rows(ref, g, b):
    first = g * ATTN_REP
    return jnp.concatenate([ref[(first + r) * HEAD_DIM:(first + r + 1) * HEAD_DIM, _cols(b)] for r in range(ATTN_REP)],
                           axis=1)


def _ungroup_rows(ref, g, b, val):
    first = g * ATTN_REP
    for r in range(ATTN_REP):
        ref[(first + r) * HEAD_DIM:(first + r + 1) * HEAD_DIM, _cols(b)] = val[:, r * BLOCK:(r + 1) * BLOCK]


def _attn_upper():
    key = lax.broadcasted_iota(jnp.int32, (BLOCK, ATTN_REP * BLOCK), 0)
    qry = lax.broadcasted_iota(jnp.int32, (BLOCK, ATTN_REP * BLOCK), 1) & (BLOCK - 1)
    return key > qry


def _attn_probs(q, kp, kc, sink_ref, g, upper, no_prev):
    s = jnp.where(upper, _dg(kp, q, TN), _dg(kc, q, TN)) * ATTN_SCALE
    if no_prev is not None:
        s = jnp.where(upper & no_prev, MASK_VALUE, s)
    sink = jnp.concatenate([jnp.full((1, BLOCK), sink_ref[0, g * ATTN_REP + r], F32) for r in range(ATTN_REP)], axis=1)
    m = jnp.maximum(jnp.max(s, axis=0, keepdims=True), sink)
    e = jnp.exp(s - m)
    es = jnp.exp(sink - m)
    inv = 1.0 / (jnp.sum(e, axis=0, keepdims=True) + es)
    return e * inv, es * inv


def _split_band(upper, val):
    return jnp.where(upper, val, 0.0).astype(BF16), jnp.where(upper, 0.0, val).astype(BF16)


def attn_fwd(qkv_t, sinks, name):
    t = qkv_t.shape[1]
    nbs, tw, tiles = _attn_tiles(t)

    def body(sink_ref, q_ref, kp_ref, kc_ref, vp_ref, vc_ref, y_ref):
        n = pl.program_id(0)
        upper = _attn_upper()
        for b in range(nbs):
            for g in range(N_KV_HEADS):
                gs = slice(g * HEAD_DIM, (g + 1) * HEAD_DIM)
                kp, kc = _band_rows(kp_ref, kc_ref, gs, b)
                vp, vc = _band_rows(vp_ref, vc_ref, gs, b)
                p, _ = _attn_probs(_group_rows(q_ref, g, b), kp, kc, sink_ref, g, upper, n == 0 if b == 0 else None)
                pp, pc = _split_band(upper, p)
                _ungroup_rows(y_ref, g, b, (_dot(vp, pp) + _dot(vc, pc)).astype(BF16))

    return pl.pallas_call(
        body, name=name, grid=(tiles,),
        in_specs=[pl.BlockSpec(memory_space=pltpu.SMEM)] + _attn_specs(nbs, tiles),
        out_specs=pl.BlockSpec((ATTN_WIDTH, tw), lambda n: (0, n)),
        out_shape=jax.ShapeDtypeStruct((ATTN_WIDTH, t), BF16),
        compiler_params=_params(("arbitrary",)),
    )(sinks, qkv_t, qkv_t, qkv_t, qkv_t, qkv_t)


def attn_bwd(qkv_t, dy_t, sinks, name):
    t = qkv_t.shape[1]
    nbs, tw, tiles = _attn_tiles(t)
    kept = slice(0, tw - BLOCK)

    def body(sink_ref, q_ref, kp_ref, kc_ref, vp_ref, vc_ref, do_ref, dq_ref, dk_ref, dv_ref, dsink_ref,
             ck_ref, cv_ref, sacc_ref):
        n = pl.program_id(0)

        @pl.when(n == 0)
        def _():
            sacc_ref[...] = jnp.zeros_like(sacc_ref)

        @pl.when((n > 0) & (n < tiles))
        def _():
            if nbs > 1:
                dk_ref[:, kept] = ck_ref[:, kept].astype(BF16)
                dv_ref[:, kept] = cv_ref[:, kept].astype(BF16)

        @pl.when(n < tiles)
        def _():
            upper = _attn_upper()
            for b in range(nbs):
                for g in range(N_KV_HEADS):
                    gs = slice(g * HEAD_DIM, (g + 1) * HEAD_DIM)
                    kp, kc = _band_rows(kp_ref, kc_ref, gs, b)
                    vp, vc = _band_rows(vp_ref, vc_ref, gs, b)
                    q = _group_rows(q_ref, g, b)
                    do = _group_rows(do_ref, g, b)
                    p, ps = _attn_probs(q, kp, kc, sink_ref, g, upper, n == 0 if b == 0 else None)
                    dp = jnp.where(upper, _dg(vp, do, TN), _dg(vc, do, TN))
                    delta = jnp.sum(p * dp, axis=0, keepdims=True)
                    dsink = -(ps * delta)
                    for r in range(ATTN_REP):
                        hd = g * ATTN_REP + r
                        sacc_ref[hd:hd + 1, :] += dsink[:, r * BLOCK:(r + 1) * BLOCK]
                    dsp, dsc = _split_band(upper, p * (dp - delta))
                    pp, pc = _split_band(upper, p)
                    _ungroup_rows(dq_ref, g, b, (ATTN_SCALE * (_dot(kp, dsp) + _dot(kc, dsc))).astype(BF16))
                    dk_before = ATTN_SCALE * _dg(q, dsp, NT)
                    dv_before = _dg(do, pp, NT)
                    if b == 0:
                        @pl.when(n > 0)
                        def _():
                            dk_ref[gs, _cols(nbs - 1)] = (ck_ref[gs, _cols(nbs - 1)] + dk_before).astype(BF16)
                            dv_ref[gs, _cols(nbs - 1)] = (cv_ref[gs, _cols(nbs - 1)] + dv_before).astype(BF16)
                    else:
                        ck_ref[gs, _cols(b - 1)] += dk_before
                        cv_ref[gs, _cols(b - 1)] += dv_before
                    ck_ref[gs, _cols(b)] = ATTN_SCALE * _dg(q, dsc, NT)
                    cv_ref[gs, _cols(b)] = _dg(do, pc, NT)

        @pl.when(n == tiles)
        def _():
            dk_ref[...] = ck_ref[...].astype(BF16)
            dv_ref[...] = cv_ref[...].astype(BF16)
            dsink_ref[...] = jnp.sum(sacc_ref[...], axis=1, keepdims=True)

    last = tiles - 1
    done = lambda n: jnp.maximum(n - 1, 0)
    outs = pl.pallas_call(
        body, name=name, grid=(tiles + 1,),
        in_specs=[pl.BlockSpec(memory_space=pltpu.SMEM)] + _attn_specs(nbs, tiles)
        + [pl.BlockSpec((ATTN_WIDTH, tw), lambda n: (0, jnp.minimum(n, last)))],
        out_specs=[pl.BlockSpec((ATTN_WIDTH, tw), lambda n: (0, jnp.minimum(n, last))),
                   pl.BlockSpec((KV_WIDTH, tw), lambda n: (0, done(n))),
                   pl.BlockSpec((KV_WIDTH, tw), lambda n: (0, done(n))),
                   pl.BlockSpec((N_Q_HEADS, 1), lambda n: (0, 0))],
        out_shape=[jax.ShapeDtypeStruct((ATTN_WIDTH, t), BF16), jax.ShapeDtypeStruct((KV_WIDTH, t), BF16),
                   jax.ShapeDtypeStruct((KV_WIDTH, t), BF16), jax.ShapeDtypeStruct((N_Q_HEADS, 1), F32)],
        scratch_shapes=[pltpu.VMEM((KV_WIDTH, tw), F32), pltpu.VMEM((KV_WIDTH, tw), F32),
                        pltpu.VMEM((N_Q_HEADS, BLOCK), F32)],
        compiler_params=_params(("arbitrary",)),
    )(sinks, qkv_t, qkv_t, qkv_t, qkv_t, qkv_t, dy_t)
    return outs


GELU_C = 0.7978845608028654
GELU_A = 0.044715


def _gelu(x):
    return 0.5 * x * (1.0 + jnp.tanh(GELU_C * (x + GELU_A * x * x * x)))


def _gelu_grad(x):
    th = jnp.tanh(GELU_C * (x + GELU_A * x * x * x))
    return 0.5 * (1.0 + th) + 0.5 * x * (1.0 - th * th) * GELU_C * (1.0 + 3.0 * GELU_A * x * x)


def _gmlp_parts(zg, lng, lnb):
    z = _gelu(zg)
    u = z[:GMLP_WIDTH, :]
    vv = z[GMLP_WIDTH:, :]
    mu = jnp.mean(vv, axis=0, keepdims=True)
    xc = vv - mu
    rstd = lax.rsqrt(jnp.mean(xc * xc, axis=0, keepdims=True) + EPS)
    xhat = xc * rstd
    return u, xhat, rstd, xhat * lng + lnb


def _as_column(row):
    return jnp.broadcast_to(row, (LANES, row.shape[1])).T[:, :1]


def _row_sums(acc):
    return jnp.sum(acc.T, axis=0, keepdims=True)


def _causal(w):
    row = lax.broadcasted_iota(jnp.int32, (BLOCK, BLOCK), 0)
    col = lax.broadcasted_iota(jnp.int32, (BLOCK, BLOCK), 1)
    return jnp.where(col <= row, w, 0.0)


GMLP_CHUNKS_PER_STEP = 8


def _stack_chunks(v, nc):
    return jnp.concatenate([v[:, c * BLOCK:(c + 1) * BLOCK] for c in range(nc)], axis=0)


def _unstack_chunks(v, nc):
    rows = v.shape[0] // nc
    return jnp.concatenate([v[c * rows:(c + 1) * rows, :] for c in range(nc)], axis=1)


def gmlp_fwd(zg_t, lng, lnb, w_s, b_s, name):
    t = zg_t.shape[1]
    nc = min(GMLP_CHUNKS_PER_STEP, t // BLOCK)
    tw = nc * BLOCK

    def body(zg_ref, lng_ref, lnb_ref, w_ref, bs_ref, y_ref):
        u, _, _, vn = _gmlp_parts(zg_ref[...].astype(F32), _as_column(lng_ref[...]), _as_column(lnb_ref[...]))
        for g in range(GMLP_GROUPS):
            gs = slice(g * GMLP_GROUP_DIM, (g + 1) * GMLP_GROUP_DIM)
            wc = _causal(w_ref[g]).astype(BF16)
            mixed = _dg(_stack_chunks(vn[gs, :].astype(BF16), nc), wc, NT) + bs_ref[g:g + 1, :]
            y_ref[gs, :] = (u[gs, :] * _unstack_chunks(mixed, nc)).astype(BF16)

    return pl.pallas_call(
        body, name=name, grid=(t // tw,),
        in_specs=[pl.BlockSpec((2 * GMLP_WIDTH, tw), lambda n: (0, n)), _resident((1, GMLP_WIDTH)),
                  _resident((1, GMLP_WIDTH)), _resident((GMLP_GROUPS, BLOCK, BLOCK)), _resident((GMLP_GROUPS, BLOCK))],
        out_specs=pl.BlockSpec((GMLP_WIDTH, tw), lambda n: (0, n)),
        out_shape=jax.ShapeDtypeStruct((GMLP_WIDTH, t), BF16),
        compiler_params=_params(("arbitrary",)),
    )(zg_t, lng, lnb, w_s, b_s)


def gmlp_bwd(zg_t, dy_t, lng, lnb, w_s, b_s, name):
    t = zg_t.shape[1]
    nc = min(GMLP_CHUNKS_PER_STEP, t // BLOCK)
    tw = nc * BLOCK
    nt = t // tw

    def body(zg_ref, dy_ref, lng_ref, lnb_ref, w_ref, bs_ref, dzg_ref, dw_ref, dbs_ref, dlng_ref, dlnb_ref,
             gacc_ref, bacc_ref):
        n = pl.program_id(0)

        @pl.when(n == 0)
        def _():
            dw_ref[...] = jnp.zeros_like(dw_ref)
            dbs_ref[...] = jnp.zeros_like(dbs_ref)
            gacc_ref[...] = jnp.zeros_like(gacc_ref)
            bacc_ref[...] = jnp.zeros_like(bacc_ref)

        zg = zg_ref[...].astype(F32)
        lng_v = _as_column(lng_ref[...])
        u, xhat, rstd, vn = _gmlp_parts(zg, lng_v, _as_column(lnb_ref[...]))
        dy = dy_ref[...].astype(F32)
        dvn_parts = []
        for g in range(GMLP_GROUPS):
            gs = slice(g * GMLP_GROUP_DIM, (g + 1) * GMLP_GROUP_DIM)
            wc = _causal(w_ref[g]).astype(BF16)
            vn_s = _stack_chunks(vn[gs, :].astype(BF16), nc)
            mixed = _unstack_chunks(_dg(vn_s, wc, NT) + bs_ref[g:g + 1, :], nc)
            dy_g = dy[gs, :]
            dmixed = dy_g * u[gs, :]
            dm_s = _stack_chunks(dmixed.astype(BF16), nc)
            dzg_ref[gs, :] = (dy_g * mixed * _gelu_grad(zg[gs, :])).astype(BF16)
            dw_ref[g] += _causal(_dg(dm_s, vn_s, TN))
            dbs_ref[g:g + 1, :] += _lane_fold(jnp.sum(dmixed, axis=0, keepdims=True))
            dvn_parts.append(_unstack_chunks(_dot(dm_s, wc), nc))
        dvn = jnp.concatenate(dvn_parts, axis=0)
        gacc_ref[...] += _lane_fold(dvn * xhat)
        bacc_ref[...] += _lane_fold(dvn)
        dxhat = dvn * lng_v
        m1 = jnp.mean(dxhat, axis=0, keepdims=True)
        m2 = jnp.mean(dxhat * xhat, axis=0, keepdims=True)
        dvv = rstd * (dxhat - m1 - xhat * m2)
        dzg_ref[GMLP_WIDTH:, :] = (dvv * _gelu_grad(zg[GMLP_WIDTH:, :])).astype(BF16)

        @pl.when(n == nt - 1)
        def _():
            dlng_ref[...] = _row_sums(gacc_ref[...])
            dlnb_ref[...] = _row_sums(bacc_ref[...])

    const2 = lambda n: (0, 0)
    return pl.pallas_call(
        body, name=name, grid=(nt,),
        in_specs=[pl.BlockSpec((2 * GMLP_WIDTH, tw), lambda n: (0, n)), pl.BlockSpec((GMLP_WIDTH, tw), lambda n: (0, n)),
                  _resident((1, GMLP_WIDTH)), _resident((1, GMLP_WIDTH)),
                  _resident((GMLP_GROUPS, BLOCK, BLOCK)), _resident((GMLP_GROUPS, BLOCK))],
        out_specs=[pl.BlockSpec((2 * GMLP_WIDTH, tw), lambda n: (0, n)),
                   pl.BlockSpec((GMLP_GROUPS, BLOCK, BLOCK), lambda n: (0, 0, 0)),
                   pl.BlockSpec((GMLP_GROUPS, BLOCK), const2),
                   pl.BlockSpec((1, GMLP_WIDTH), const2), pl.BlockSpec((1, GMLP_WIDTH), const2)],
        out_shape=[jax.ShapeDtypeStruct((2 * GMLP_WIDTH, t), BF16),
                   jax.ShapeDtypeStruct((GMLP_GROUPS, BLOCK, BLOCK), F32),
                   jax.ShapeDtypeStruct((GMLP_GROUPS, BLOCK), F32),
                   jax.ShapeDtypeStruct((1, GMLP_WIDTH), F32), jax.ShapeDtypeStruct((1, GMLP_WIDTH), F32)],
        scratch_shapes=[pltpu.VMEM((GMLP_WIDTH, BLOCK), F32), pltpu.VMEM((GMLP_WIDTH, BLOCK), F32)],
        compiler_params=_params(("arbitrary",)),
    )(zg_t, dy_t, lng, lnb, w_s, b_s)


def _rmsnorm_cols(y, gain_col):
    r = lax.rsqrt(jnp.mean(y * y, axis=0, keepdims=True) + EPS)
    n = y * r
    return n, r, n * gain_col


def _ring_copy(hbm, ring, sem, step):
    tm = ring.shape[1]
    slot = step % RING_SLOTS
    return pltpu.make_async_copy(hbm.at[pl.ds(pl.multiple_of(step * tm, tm), tm), :], ring.at[slot], sem.at[slot])


def _ring_start(hbm, ring, sem, i, nt):
    @pl.when(i == 0)
    def _():
        for step in range(min(RING_SLOTS - 1, nt)):
            _ring_copy(hbm, ring, sem, step).start()

    @pl.when(i + RING_SLOTS - 1 < nt)
    def _():
        _ring_copy(hbm, ring, sem, i + RING_SLOTS - 1).start()


def _ring_wait(hbm, ring, sem, i):
    _ring_copy(hbm, ring, sem, i).wait()
    return ring[i % RING_SLOTS]


def mix_out_fwd(x, ya_t, yg_t, gao, ggo, wout, b_out, name):
    t, d = x.shape
    tm = _token_tile(t)
    nt = t // tm

    def body(x_hbm, ya_ref, yg_ref, gao_ref, ggo_ref, w_ref, b_ref, o_ref, x_ring, sem):
        i = pl.program_id(0)
        _ring_start(x_hbm, x_ring, sem, i, nt)
        _, _, ya = _rmsnorm_cols(ya_ref[...].astype(F32), _as_column(gao_ref[...]))
        _, _, yg = _rmsnorm_cols(yg_ref[...].astype(F32), _as_column(ggo_ref[...]))
        mixed = _dg(ya.astype(BF16), w_ref[:ATTN_WIDTH, :], TN) + _dg(yg.astype(BF16), w_ref[ATTN_WIDTH:, :], TN)
        o_ref[...] = _ring_wait(x_hbm, x_ring, sem, i) + mixed + b_ref[...]

    return pl.pallas_call(
        body, name=name, grid=(nt,),
        in_specs=[ANY_SPEC, pl.BlockSpec((ATTN_WIDTH, tm), lambda i: (0, i)),
                  pl.BlockSpec((GMLP_WIDTH, tm), lambda i: (0, i)), _resident((1, ATTN_WIDTH)), _resident((1, GMLP_WIDTH)),
                  _resident((ATTN_WIDTH + GMLP_WIDTH, d)), _resident((1, d))],
        out_specs=pl.BlockSpec((tm, d), lambda i: (i, 0)),
        out_shape=jax.ShapeDtypeStruct((t, d), F32),
        scratch_shapes=[pltpu.VMEM((RING_SLOTS, tm, d), F32), pltpu.SemaphoreType.DMA((RING_SLOTS,))],
        compiler_params=_params(("arbitrary",)),
    )(x, ya_t, yg_t, gao, ggo, wout, b_out)


def _lane_fold(v):
    out = v[:, :LANES]
    for j in range(1, v.shape[1] // LANES):
        out = out + v[:, j * LANES:(j + 1) * LANES]
    return out


def mix_out_bwd(dx, ya_t, yg_t, gao, ggo, wout, name, deps=()):
    t, d = dx.shape
    tm = _token_tile(t)
    nt = t // tm

    def body(dx_hbm, ya_ref, yg_ref, gao_ref, ggo_ref, w_ref, dya_ref, dyg_ref, y_ref, dgao_ref, dggo_ref, db_ref,
             acc_ref, dx_ring, sem):
        i = pl.program_id(0)
        _ring_start(dx_hbm, dx_ring, sem, i, nt)

        @pl.when(i == 0)
        def _():
            acc_ref[...] = jnp.zeros_like(acc_ref)
            db_ref[...] = jnp.zeros_like(db_ref)

        dxv = _ring_wait(dx_hbm, dx_ring, sem, i)
        db_ref[...] += jnp.sum(dxv, axis=0, keepdims=True)
        dxb = dxv.astype(BF16)
        for part, (src_ref, gain_ref, dst_ref) in enumerate(((ya_ref, gao_ref, dya_ref), (yg_ref, ggo_ref, dyg_ref))):
            rows = slice(part * ATTN_WIDTH, (part + 1) * ATTN_WIDTH)
            gain = _as_column(gain_ref[...])
            nrm, r, yn = _rmsnorm_cols(src_ref[...].astype(F32), gain)
            y_ref[rows, :] = yn.astype(BF16)
            dy = _dg(w_ref[rows, :], dxb, NT)
            acc_ref[rows, :] += _lane_fold(dy * nrm)
            dn = dy * gain
            dst_ref[...] = (r * (dn - nrm * jnp.mean(dn * nrm, axis=0, keepdims=True))).astype(BF16)

        @pl.when(i == nt - 1)
        def _():
            dgao_ref[...] = _row_sums(acc_ref[:ATTN_WIDTH, :])
            dggo_ref[...] = _row_sums(acc_ref[ATTN_WIDTH:, :])

    const2 = lambda i: (0, 0)
    feat = lambda w: pl.BlockSpec((w, tm), lambda i: (0, i))
    return pl.pallas_call(
        _drop_deps(body, 6, len(deps)), name=name, grid=(nt,),
        in_specs=[ANY_SPEC, feat(ATTN_WIDTH), feat(GMLP_WIDTH),
                  _resident((1, ATTN_WIDTH)), _resident((1, GMLP_WIDTH)), _resident((ATTN_WIDTH + GMLP_WIDTH, d))]
        + [ANY_SPEC] * len(deps),
        out_specs=[feat(ATTN_WIDTH), feat(GMLP_WIDTH), feat(ATTN_WIDTH + GMLP_WIDTH),
                   pl.BlockSpec((1, ATTN_WIDTH), const2), pl.BlockSpec((1, GMLP_WIDTH), const2),
                   pl.BlockSpec((1, d), const2)],
        out_shape=[jax.ShapeDtypeStruct((ATTN_WIDTH, t), BF16), jax.ShapeDtypeStruct((GMLP_WIDTH, t), BF16),
                   jax.ShapeDtypeStruct((ATTN_WIDTH + GMLP_WIDTH, t), BF16),
                   jax.ShapeDtypeStruct((1, ATTN_WIDTH), F32), jax.ShapeDtypeStruct((1, GMLP_WIDTH), F32),
                   jax.ShapeDtypeStruct((1, d), F32)],
        scratch_shapes=[pltpu.VMEM((ATTN_WIDTH + GMLP_WIDTH, LANES), F32), pltpu.VMEM((RING_SLOTS, tm, d), F32),
                        pltpu.SemaphoreType.DMA((RING_SLOTS,))],
        compiler_params=_params(("arbitrary",)),
    )(dx, ya_t, yg_t, gao, ggo, wout, *deps)


def mix_in_bwd(x, dxo, gain, dq_t, dk_t, dv_t, dzg_t, win_t, name):
    t, d = x.shape
    w = win_t.shape[0]
    tm = _token_tile(t)
    nt = t // tm
    parts = ((0, ATTN_WIDTH), (ATTN_WIDTH, KV_WIDTH), (ATTN_WIDTH + KV_WIDTH, KV_WIDTH), (QKV_WIDTH, w - QKV_WIDTH))

    def body(x_hbm, dxo_hbm, g_ref, dq_ref, dk_ref, dv_ref, dzg_ref, w_ref, dx_ref, h_ref, dg_ref, db_ref, acc_ref,
             x_ring, dxo_ring, x_sem, dxo_sem):
        i = pl.program_id(0)
        _ring_start(x_hbm, x_ring, x_sem, i, nt)
        _ring_start(dxo_hbm, dxo_ring, dxo_sem, i, nt)

        @pl.when(i == 0)
        def _():
            acc_ref[...] = jnp.zeros_like(acc_ref)
            dg_ref[...] = jnp.zeros_like(dg_ref)

        xv = _ring_wait(x_hbm, x_ring, x_sem, i)
        gain_v = g_ref[...]
        r = lax.rsqrt(jnp.mean(xv * xv, axis=-1, keepdims=True) + EPS)
        h_ref[...] = (xv * r * gain_v).astype(BF16)
        dh = jnp.zeros((tm, d), F32)
        for (off, size), src_ref in zip(parts, (dq_ref, dk_ref, dv_ref, dzg_ref)):
            dp = src_ref[...]
            acc_ref[off:off + size, :] += _lane_fold(dp.astype(F32))
            dh = dh + _dg(dp, w_ref[off:off + size, :], TN)
        dx, dgain = _rmsnorm_bwd_rows(xv, r, gain_v, dh)
        dx_ref[...] = _ring_wait(dxo_hbm, dxo_ring, dxo_sem, i) + dx
        dg_ref[...] += dgain

        @pl.when(i == nt - 1)
        def _():
            db_ref[...] = _row_sums(acc_ref[...])

    const2 = lambda i: (0, 0)
    tok = lambda: pl.BlockSpec((tm, d), lambda i: (i, 0))
    feat = lambda rows: pl.BlockSpec((rows, tm), lambda i: (0, i))
    return pl.pallas_call(
        body, name=name, grid=(nt,),
        in_specs=[ANY_SPEC, ANY_SPEC, _resident((1, d)), feat(ATTN_WIDTH), feat(KV_WIDTH), feat(KV_WIDTH),
                  feat(w - QKV_WIDTH), _resident((w, d))],
        out_specs=[tok(), tok(), pl.BlockSpec((1, d), const2), pl.BlockSpec((1, w), const2)],
        out_shape=[jax.ShapeDtypeStruct((t, d), F32), jax.ShapeDtypeStruct((t, d), BF16),
                   jax.ShapeDtypeStruct((1, d), F32), jax.ShapeDtypeStruct((1, w), F32)],
        scratch_shapes=[pltpu.VMEM((w, LANES), F32), pltpu.VMEM((RING_SLOTS, tm, d), F32),
                        pltpu.VMEM((RING_SLOTS, tm, d), F32), pltpu.SemaphoreType.DMA((RING_SLOTS,)),
                        pltpu.SemaphoreType.DMA((RING_SLOTS,))],
        compiler_params=_params(("arbitrary",)),
    )(x, dxo, gain, dq_t, dk_t, dv_t, dzg_t, win_t)


def _adamw_math(w, g, m, v):
    m = ADAM_B1 * m + (1.0 - ADAM_B1) * g
    v = ADAM_B2 * v + (1.0 - ADAM_B2) * (g * g)
    m_hat = m / (1.0 - ADAM_B1 ** ADAM_STEP)
    v_hat = v / (1.0 - ADAM_B2 ** ADAM_STEP)
    delta = -ADAM_LR * (m_hat / (jnp.sqrt(v_hat) + ADAM_EPS) + ADAM_WD * w)
    return delta, m, v


def adamw_many(ws, gs, ms, vs, name):
    n = len(ws)

    def body(*refs):
        w_refs, g_refs, m_refs, v_refs = refs[:n], refs[n:2 * n], refs[2 * n:3 * n], refs[3 * n:4 * n]
        d_refs, mo_refs, vo_refs = refs[4 * n:5 * n], refs[5 * n:6 * n], refs[6 * n:]
        for k in range(n):
            delta, mn, vn = _adamw_math(w_refs[k][...], g_refs[k][...], m_refs[k][...], v_refs[k][...])
            d_refs[k][...] = delta
            mo_refs[k][...] = mn
            vo_refs[k][...] = vn

    def whole(a):
        zeros = (0,) * a.ndim
        return pl.BlockSpec(a.shape, lambda i: zeros)

    outs = pl.pallas_call(
        body, name=name, grid=(1,),
        in_specs=[whole(a) for a in ws] * 4, out_specs=[whole(a) for a in ws] * 3,
        out_shape=[jax.ShapeDtypeStruct(a.shape, F32) for a in ws] * 3,
        compiler_params=_params(("arbitrary",)),
    )(*ws, *gs, *ms, *vs)
    return outs[:n], outs[n:2 * n], outs[2 * n:]


def sum_adamw(landing, grad3d, me, w, m, v, name):
    nd, rows, cols = landing.shape
    tr = rows // 2

    def body(me_ref, l_ref, own_ref, w_ref, m_ref, v_ref, g_ref, d_ref, mo_ref, vo_ref):
        g = own_ref[...].astype(F32)
        for j in range(nd):
            g = g + l_ref[j].astype(F32)
        delta, mn, vn = _adamw_math(w_ref[...], g, m_ref[...], v_ref[...])
        g_ref[...] = g
        d_ref[...] = delta
        mo_ref[...] = mn
        vo_ref[...] = vn

    spec = lambda: pl.BlockSpec((tr, cols), lambda i, me_ref: (i, 0))
    shape = jax.ShapeDtypeStruct((rows, cols), F32)
    return pl.pallas_call(
        body, name=name,
        grid_spec=pltpu.PrefetchScalarGridSpec(
            num_scalar_prefetch=1, grid=(rows // tr,),
            in_specs=[pl.BlockSpec((nd, tr, cols), lambda i, me_ref: (0, i, 0)),
                      pl.BlockSpec((None, tr, cols), lambda i, me_ref: (me_ref[0], i, 0)), spec(), spec(), spec()],
            out_specs=[spec() for _ in range(4)]),
        out_shape=[shape] * 4,
        compiler_params=_params(("arbitrary",)),
    )(me, landing, grad3d, w, m, v)


def _mesh_place():
    x, y, c = lax.axis_index("x"), lax.axis_index("y"), lax.axis_index("c")
    return x, y, c, 4 * x + 2 * y + c


def _peer(x, y, c, k):
    return (x ^ ((k >> 2) & 1), y ^ ((k >> 1) & 1), c ^ (k & 1))


def prep_shards(mats, me, name):
    na = len(mats)

    def body(me_ref, *refs):
        in_refs, shard_refs, land_refs = refs[:na], refs[na:2 * na], refs[2 * na:]
        for a in range(na):
            val = in_refs[a][...].astype(BF16)
            shard_refs[a][...] = val
            land_refs[a][...] = val

    whole = lambda mat: pl.BlockSpec(mat.shape, lambda i, me_ref: (0, 0))
    outs = pl.pallas_call(
        body, name=name,
        grid_spec=pltpu.PrefetchScalarGridSpec(
            num_scalar_prefetch=1, grid=(1,),
            in_specs=[whole(mat) for mat in mats],
            out_specs=[whole(mat) for mat in mats]
            + [pl.BlockSpec((None,) + mat.shape, lambda i, me_ref: (me_ref[0], 0, 0)) for mat in mats]),
        out_shape=[jax.ShapeDtypeStruct(mat.shape, BF16) for mat in mats]
        + [jax.ShapeDtypeStruct((N_DEV,) + mat.shape, BF16) for mat in mats],
        compiler_params=_params(("arbitrary",)),
    )(me, *mats)
    return outs[:na], outs[na:]


def allgather_rows(shards, lands, name):
    na = n_gather = len(shards)

    def body(*refs):
        in_refs, out_refs = refs[:na], refs[2 * na:3 * na]
        send_sems, recv_sems = refs[3 * na:]
        x, y, c, me = _mesh_place()

        def copy(sem, a, origin, src, k):
            return pltpu.make_async_remote_copy(
                src_ref=src, dst_ref=out_refs[a].at[origin], send_sem=send_sems.at[sem, a],
                recv_sem=recv_sems.at[sem, a], device_id=_peer(x, y, c, k), device_id_type=pl.DeviceIdType.MESH)

        kx, ky = 4, 2

        def recv(sem, a, origin):
            copy(sem, a, origin, in_refs[a], 1).wait_recv()

        def landed(a, origin):
            return out_refs[a].at[origin]

        @pl.when(c == 1)
        def _():
            sends = []

            def send(sem, a, origin, src, k):
                cp = copy(sem, a, origin, src, k)
                cp.start()
                sends.append(cp)

            for a in range(n_gather):
                send(1, a, me, in_refs[a], kx)
                send(2, a, me, in_refs[a], ky)
                send(0, a, me, in_refs[a], 1)
            for a in range(n_gather):
                recv(0, a, me ^ 1)
                send(3, a, me ^ 1, landed(a, me ^ 1), kx)
                send(4, a, me ^ 1, landed(a, me ^ 1), ky)
            for a in range(n_gather):
                recv(1, a, me ^ kx)
                send(5, a, me ^ kx, landed(a, me ^ kx), ky)
                send(7, a, me ^ kx, landed(a, me ^ kx), 1)
                recv(2, a, me ^ ky)
                send(8, a, me ^ ky, landed(a, me ^ ky), 1)
            for a in range(n_gather):
                recv(3, a, me ^ kx ^ 1)
                send(9, a, me ^ kx ^ 1, landed(a, me ^ kx ^ 1), 1)
                recv(4, a, me ^ ky ^ 1)
                send(6, a, me ^ ky ^ 1, landed(a, me ^ ky ^ 1), kx)
                send(10, a, me ^ ky ^ 1, landed(a, me ^ ky ^ 1), 1)
            for a in range(n_gather):
                recv(5, a, me ^ 6)
                send(11, a, me ^ 6, landed(a, me ^ 6), 1)
                recv(6, a, me ^ 7)
                send(12, a, me ^ 7, landed(a, me ^ 7), 1)
            for cp in sends:
                cp.wait_send()

        @pl.when(c == 0)
        def _():
            north = me ^ 1
            own = [copy(0, a, me, in_refs[a], 1) for a in range(n_gather)]
            for cp in own:
                cp.start()
            for a in range(n_gather):
                recv(0, a, north)
                for sem, origin in ((7, north ^ kx), (8, north ^ ky), (9, north ^ kx ^ 1), (10, north ^ ky ^ 1),
                                    (11, north ^ 6), (12, north ^ 7)):
                    recv(sem, a, origin)
            for cp in own:
                cp.wait_send()

    return pl.pallas_call(
        body, name=name,
        in_specs=[ANY_SPEC] * (2 * na), out_specs=[ANY_SPEC] * na,
        out_shape=[jax.ShapeDtypeStruct(land.shape, land.dtype) for land in lands],
        input_output_aliases={na + a: a for a in range(na)},
        scratch_shapes=[pltpu.SemaphoreType.DMA((N_GATHER_ROLES, n_gather)),
                        pltpu.SemaphoreType.DMA((N_GATHER_ROLES, n_gather))],
    )(*shards, *lands)


HBM_SPEC = pl.BlockSpec(memory_space=pltpu.HBM)
SEM_SPEC = pl.BlockSpec(memory_space=pltpu.SEMAPHORE)
SIDE_EFFECT = pltpu.SideEffectType.DATAFLOW_SIDE_EFFECTING


def _hbm(v):
    return pltpu.with_memory_space_constraint(v, pltpu.HBM)


def gather_slices(src, land, me, k):
    return src, land.at[me], land.at[me ^ k]


def exchange_slices(src, land, me, k):
    return src.at[me ^ k], land.at[k - 1], land.at[k - 1]


def split_start(groups, slices, deps, name):
    sizes = [len(srcs) for srcs, _ in groups]
    flat_src = [s for srcs, _ in groups for s in srcs]
    flat_land = [l for _, lands in groups for l in lands]
    na, ng, nd = len(flat_src), len(groups), len(deps)

    def body(*refs):
        src_refs, land_refs = refs[:na], refs[na:2 * na]
        sem_refs = refs[2 * na + nd:2 * na + nd + 2 * ng]
        token_ref = refs[-1]
        x, y, c, me = _mesh_place()
        a0 = 0
        for gi, size in enumerate(sizes):
            send_sems, recv_sems = sem_refs[2 * gi], sem_refs[2 * gi + 1]
            for k in range(1, N_DEV):
                for j in range(size):
                    src, dst, _ = slices(src_refs[a0 + j], land_refs[a0 + j], me, k)
                    pltpu.make_async_remote_copy(
                        src_ref=src, dst_ref=dst, send_sem=send_sems.at[(k - 1) * size + j],
                        recv_sem=recv_sems.at[(k - 1) * size + j],
                        device_id=_peer(x, y, c, k), device_id_type=pl.DeviceIdType.MESH).start()
            a0 += size
        token_ref[...] = jnp.zeros_like(token_ref)

    sems = [pltpu.SemaphoreType.DMA(((N_DEV - 1) * size,)) for size in sizes for _ in range(2)]
    thru = [pltpu.HBM(v.shape, v.dtype) for v in flat_src + flat_land]
    outs = pl.pallas_call(
        body, name=name,
        in_specs=[HBM_SPEC] * (2 * na) + [ANY_SPEC] * nd,
        out_specs=[SEM_SPEC] * (2 * ng) + [HBM_SPEC] * (2 * na) + [pl.BlockSpec(memory_space=pltpu.VMEM)],
        out_shape=sems + thru + [jax.ShapeDtypeStruct((8, LANES), F32)],
        input_output_aliases={i: 2 * ng + i for i in range(2 * na)},
        compiler_params=pltpu.CompilerParams(has_side_effects=SIDE_EFFECT),
    )(*[_hbm(v) for v in flat_src + flat_land], *deps)
    started, a0 = [], 0
    for gi, size in enumerate(sizes):
        srcs = outs[2 * ng + a0:2 * ng + a0 + size]
        lands = outs[2 * ng + na + a0:2 * ng + na + a0 + size]
        started.append((outs[2 * gi], outs[2 * gi + 1], list(srcs), list(lands)))
        a0 += size
    return started, outs[-1]


def split_wait(started, slices, after, name):
    send_sems, recv_sems, srcs, lands = started
    na = len(srcs)

    def body(*refs):
        src_refs, land_refs = refs[:na], refs[na:2 * na]
        send_ref, recv_ref = refs[2 * na], refs[2 * na + 1]
        x, y, c, me = _mesh_place()
        for k in range(1, N_DEV):
            for j in range(na):
                src, _, got = slices(src_refs[j], land_refs[j], me, k)
                cp = pltpu.make_async_remote_copy(
                    src_ref=src, dst_ref=got, send_sem=send_ref.at[(k - 1) * na + j], recv_sem=recv_ref.at[(k - 1) * na + j],
                    device_id=_peer(x, y, c, k), device_id_type=pl.DeviceIdType.MESH)
                cp.wait_send()
                cp.wait_recv()

    outs = pl.pallas_call(
        body, name=name,
        in_specs=[HBM_SPEC] * (2 * na) + [SEM_SPEC, SEM_SPEC] + [ANY_SPEC] * len(after),
        out_specs=[HBM_SPEC] * (2 * na),
        out_shape=[pltpu.HBM(v.shape, v.dtype) for v in srcs + lands],
        input_output_aliases={i: i for i in range(2 * na)},
        compiler_params=pltpu.CompilerParams(has_side_effects=SIDE_EFFECT),
    )(*srcs, *lands, send_sems, recv_sems, *after)
    return list(outs[:na]), list(outs[na:])


def place_own(pack, me, name):
    rows, cols = pack.shape

    def body(me_ref, p_ref, o_ref):
        o_ref[...] = p_ref[...]

    return pl.pallas_call(
        body, name=name,
        grid_spec=pltpu.PrefetchScalarGridSpec(
            num_scalar_prefetch=1, grid=(1,),
            in_specs=[pl.BlockSpec((rows, cols), lambda i, me_ref: (0, 0))],
            out_specs=pl.BlockSpec((None, rows, cols), lambda i, me_ref: (me_ref[0], 0, 0))),
        out_shape=jax.ShapeDtypeStruct((N_DEV, rows, cols), F32),
        compiler_params=_params(("arbitrary",)),
    )(me, pack)


def sum_slots(slots, name):
    nd, rows, cols = slots.shape

    def body(s_ref, o_ref):
        acc = s_ref[0]
        for j in range(1, nd):
            acc = acc + s_ref[j]
        o_ref[...] = acc

    return pl.pallas_call(
        body, name=name, grid=(1,),
        in_specs=[pl.BlockSpec((nd, rows, cols), lambda i: (0, 0, 0))],
        out_specs=pl.BlockSpec((rows, cols), lambda i: (0, 0)),
        out_shape=jax.ShapeDtypeStruct((rows, cols), F32),
        compiler_params=_params(("arbitrary",)),
    )(slots)


def _pack_rows(size):
    rows = -(-size // LANES)
    return -(-rows // 8) * 8


def pack_small(parts):
    out = []
    for p in parts:
        flat = p.reshape(-1).astype(F32)
        rows = _pack_rows(flat.shape[0])
        out.append(jnp.pad(flat, (0, rows * LANES - flat.shape[0])).reshape(rows, LANES))
    return jnp.concatenate(out, axis=0)


def unpack_small(pack, shapes):
    out, off = [], 0
    for shp in shapes:
        size = 1
        for s in shp:
            size *= s
        rows = _pack_rows(size)
        out.append(pack[off:off + rows].reshape(-1)[:size].reshape(shp))
        off += rows
    return out


def local_step(x, target, small, get_w, put_g, put_small):
    row = lambda v: v.reshape(1, -1)
    (wg1, wu1), deps = get_w(("wg1", "wu1"), ())
    sil1, gp1, s1, h1 = ffn_fwd(x, small["ffn1_norm_g"], wg1, wu1, None, "ffn1_fwd", deps)
    (wd1, win, wout), _ = get_w(("wd1", "win", "wout"), (s1,))
    qkv_t, zg_t, x1 = mix_in_fwd(x, small["mix_norm_g"], win, row(small["b_in"]), "mix_in_fwd", ffn_tail=(s1, wd1))
    ya_t = attn_fwd(qkv_t, small["attn_sinks"], "attn_fwd")
    lng, lnb = row(small["gmlp_ln_g"]), row(small["gmlp_ln_b"])
    w_s, b_s = small["gmlp_w_s"][0], small["gmlp_b_s"][0]
    yg_t = gmlp_fwd(zg_t, lng, lnb, w_s, b_s, "gmlp_fwd")
    gao, ggo = row(small["attn_out_norm_g"]), row(small["gmlp_out_norm_g"])
    x2 = mix_out_fwd(x1, ya_t, yg_t, gao, ggo, wout, small["b_out"], "mix_out_fwd")
    (wg2, wu2, wd2), _ = get_w(("wg2", "wu2", "wd2"), (x2,))
    gs = {}
    dx3, sil2, gp2, s2, h2f, gs["final_norm_g"], loss = ffn_fwd(
        x2, small["ffn2_norm_g"], wg2, wu2, wd2, "ffn2_fwd", head=(target, small["final_norm_g"].reshape(1, -1)))

    dx2, da, db, df, gs["ffn2_norm_g"] = ffn_bwd(x2, dx3, small["ffn2_norm_g"], sil2, gp2, wg2, wu2, wd2, "ffn2_bwd")
    deps = put_g({"wg2": grad_tn(da, h2f, 1.0, "ffn2_dwg"), "wu2": grad_tn(db, h2f, 1.0, "ffn2_dwu"),
                  "wd2": grad_tn(s2, df, 1.0, "ffn2_dwd")})

    dya_t, dyg_t, y_t, dgao, dggo, gs["b_out"] = mix_out_bwd(dx2, ya_t, yg_t, gao, ggo, wout, "mix_out_bwd", deps)
    gs["attn_out_norm_g"], gs["gmlp_out_norm_g"] = dgao.reshape(1, -1), dggo.reshape(1, -1)
    g_wout = grad_nn([y_t], dx2, "dwout")
    dzg_t, dws, dbs, dlng, dlnb = gmlp_bwd(zg_t, dyg_t, lng, lnb, w_s, b_s, "gmlp_bwd")
    gs["gmlp_w_s"], gs["gmlp_b_s"] = dws[None], dbs[None]
    gs["gmlp_ln_g"], gs["gmlp_ln_b"] = dlng.reshape(1, -1), dlnb.reshape(1, -1)
    dq_t, dk_t, dv_t, dsinks = attn_bwd(qkv_t, dya_t, small["attn_sinks"], "attn_bwd")
    gs["attn_sinks"] = dsinks.reshape(1, -1)
    dx1, h2, gs["mix_norm_g"], dbin = mix_in_bwd(x1, dx2, small["mix_norm_g"], dq_t, dk_t, dv_t, dzg_t, win,
                                                 "mix_in_bwd")
    gs["b_in"] = dbin.reshape(1, -1)
    deps = put_g({"wout": g_wout, "win": grad_nn([dq_t, dk_t, dv_t, dzg_t], h2, "dwin")})

    dx0, da, db, df, gs["ffn1_norm_g"] = ffn_bwd(x, dx1, small["ffn1_norm_g"], sil1, gp1, wg1, wu1, wd1, "ffn1_bwd",
                                             deps)
    deps = put_small(gs, loss)
    deps = put_g({"wg1": grad_tn(da, h1, 1.0, "ffn1_dwg", deps)})
    deps = put_g({"wu1": grad_tn(db, h1, 1.0, "ffn1_dwu", deps)})
    put_g({"wd1": grad_tn(s1, df, 1.0, "ffn1_dwd", deps)})
    return dx0


SMALL_NAMES = ["ffn1_norm_g", "mix_norm_g", "b_in", "attn_sinks", "gmlp_ln_g", "gmlp_ln_b", "gmlp_w_s", "gmlp_b_s",
               "attn_out_norm_g", "gmlp_out_norm_g", "b_out", "ffn2_norm_g", "final_norm_g"]
BIG_NAMES = {"wg1": ("ffn1_w_gate", True), "wu1": ("ffn1_w_up", True), "wd1": ("ffn1_w_down", False),
             "win": ("w_in", True), "wout": ("w_out", False),
             "wg2": ("ffn2_w_gate", True), "wu2": ("ffn2_w_up", True), "wd2": ("ffn2_w_down", False)}
WEIGHT_ORDER = ["ffn1_norm_g", "ffn1_w_gate", "ffn1_w_up", "ffn1_w_down", "mix_norm_g", "w_in", "b_in", "attn_sinks",
                "gmlp_ln_g", "gmlp_ln_b", "gmlp_w_s", "gmlp_b_s", "attn_out_norm_g", "gmlp_out_norm_g", "w_out",
                "b_out", "ffn2_norm_g", "ffn2_w_gate", "ffn2_w_up", "ffn2_w_down", "final_norm_g"]


def kernel(x, ffn1_norm_g, ffn1_w_gate, ffn1_w_up, ffn1_w_down, mix_norm_g, w_in, b_in, attn_sinks, gmlp_ln_g, gmlp_ln_b, gmlp_w_s, gmlp_b_s, attn_out_norm_g, gmlp_out_norm_g, w_out, b_out, ffn2_norm_g, ffn2_w_gate, ffn2_w_up, ffn2_w_down, final_norm_g, loss_target, m_ffn1_norm_g, m_ffn1_w_gate, m_ffn1_w_up, m_ffn1_w_down, m_mix_norm_g, m_w_in, m_b_in, m_attn_sinks, m_gmlp_ln_g, m_gmlp_ln_b, m_gmlp_w_s, m_gmlp_b_s, m_attn_out_norm_g, m_gmlp_out_norm_g, m_w_out, m_b_out, m_ffn2_norm_g, m_ffn2_w_gate, m_ffn2_w_up, m_ffn2_w_down, m_final_norm_g, v_ffn1_norm_g, v_ffn1_w_gate, v_ffn1_w_up, v_ffn1_w_down, v_mix_norm_g, v_w_in, v_b_in, v_attn_sinks, v_gmlp_ln_g, v_gmlp_ln_b, v_gmlp_w_s, v_gmlp_b_s, v_attn_out_norm_g, v_gmlp_out_norm_g, v_w_out, v_b_out, v_ffn2_norm_g, v_ffn2_w_gate, v_ffn2_w_up, v_ffn2_w_down, v_final_norm_g):
    args = dict(locals())
    w = {n: args[n] for n in WEIGHT_ORDER}
    m = {n: args["m_" + n] for n in WEIGHT_ORDER}
    v = {n: args["v_" + n] for n in WEIGHT_ORDER}

    d_model = x.shape[-1]
    flat = lambda g: g.reshape(-1, d_model)
    me = _mesh_place()[3].astype(jnp.int32).reshape(1)
    first = ("wg1", "wu1")
    later = (("wd1", "win", "wout"), ("wg2", "wu2", "wd2"))
    gathers, exchanges, last_token = {}, [], []

    order = list(first) + [k for grp in later for k in grp]
    mats = [w[BIG_NAMES[k][0]][0].T if BIG_NAMES[k][1] else w[BIG_NAMES[k][0]][0] for k in order]
    shard_list, land_list = prep_shards(mats, me, "prep_shards")
    shards, zones = dict(zip(order, shard_list)), dict(zip(order, land_list))

    def get_w(keys, after):
        if keys == first:
            full = allgather_rows([shards[k] for k in first], [zones[k] for k in first], "allgather_ffn1")
            started, token = split_start([([shards[k] for k in grp], [zones[k] for k in grp]) for grp in later],
                                         gather_slices, (full[0],), "gather_start")
            gathers.update(zip(later, started))
            return [flat(g) for g in full], (token,)
        _, lands = split_wait(gathers[keys], gather_slices, after, "gather_wait_" + keys[0])
        return [flat(land) for land in lands], ()

    def put_g(gb):
        keys = tuple(gb)
        g3 = [gb[k].reshape(N_DEV, -1, d_model) for k in keys]
        lands = [lax.empty((N_DEV - 1,) + g.shape[1:], BF16) for g in g3]
        started, token = split_start([(g3, lands)], exchange_slices, (), "exchange_start_" + keys[0])
        exchanges.append((keys, started[0]))
        last_token[:] = [token]
        return (token,)

    small_started = []

    def put_small(gs, loss):
        pack = pack_small([gs[n] for n in SMALL_NAMES] + [loss[:, :1]])
        started, token = split_start([([pack], [place_own(pack, me, "place_small")])], gather_slices, (), "small_start")
        small_started[:] = started
        return (token,)

    small = {n: w[n] for n in SMALL_NAMES}
    grad_x = local_step(x[0], loss_target[0], small, get_w, put_g, put_small)

    grads, deltas, new_m, new_v = {}, {}, {}, {}
    shapes = [w[n].shape for n in SMALL_NAMES]
    _, (slots,) = split_wait(small_started[0], gather_slices, tuple(last_token), "small_wait")
    total = sum_slots(slots, "sum_small")
    small_g = unpack_small(total, shapes + [(1, 1)])
    loss_total = small_g[-1].reshape(())
    d_, m_, v_ = adamw_many([w[n] for n in SMALL_NAMES], small_g[:-1], [m[n] for n in SMALL_NAMES],
                            [v[n] for n in SMALL_NAMES], "adamw_small")
    for n, g_, dd, mm, vv in zip(SMALL_NAMES, small_g[:-1], d_, m_, v_):
        grads[n], deltas[n], new_m[n], new_v[n] = g_, dd, mm, vv

    after = tuple(arr for n in SMALL_NAMES for arr in (grads[n], deltas[n], new_m[n], new_v[n]))
    for keys, started in exchanges:
        g3, lands = split_wait(started, exchange_slices, after, "exchange_wait_" + keys[0])
        for key, own, land in zip(keys, g3, lands):
            pname, transposed = BIG_NAMES[key]
            to_rows = (lambda p: p[0].T) if transposed else (lambda p: p[0])
            from_rows = (lambda r: r.T[None]) if transposed else (lambda r: r[None])
            g, d_, m_, v_ = sum_adamw(land, own, me, to_rows(w[pname]), to_rows(m[pname]), to_rows(v[pname]),
                                      "adamw_" + key)
            after = after + (g,)
            grads[pname], deltas[pname], new_m[pname], new_v[pname] = (from_rows(g), from_rows(d_), from_rows(m_),
                                                                        from_rows(v_))

    return (loss_total, grad_x[None], *[grads[n] for n in WEIGHT_ORDER], *[deltas[n] for n in WEIGHT_ORDER],
            *[new_m[n] for n in WEIGHT_ORDER], *[new_v[n] for n in WEIGHT_ORDER])
```
